```python
import math
import jax, jax.numpy as jnp
from jax import lax
import numpy as np

D_MODEL = 1024
BATCH = 8
SEQ = 2048
DEPTH = 1

N_META = 16
MLA_HEADS = 4
QK_NOPE_DIM = 128
QK_ROPE_DIM = 64
QK_HEAD_DIM = QK_NOPE_DIM + QK_ROPE_DIM
V_HEAD_DIM = 128
Q_LORA_RANK = 256
KV_LORA_RANK = 256
ROPE_THETA = 10000.0
Q_BLOCK = 128
DN_HEADS = 4
DN_HEAD_DIM = 128
DN_WIDTH = DN_HEADS * DN_HEAD_DIM
DN_CONV_WIDTH = 4
DN_CHUNK = 64
MIX_WIDTH = MLA_HEADS * V_HEAD_DIM + DN_WIDTH
IN_COLS = Q_LORA_RANK + KV_LORA_RANK + QK_ROPE_DIM + 3 * DN_WIDTH + DN_WIDTH + 2 * DN_HEADS
D_FF = 2816
FFN_CONV_WIDTH = 3
NORM_EPS = 1e-6

kernel_name = "hymba_mla_gdn_convglu_layer"


def _rmsnorm(x, w):
    xf = x.astype(jnp.float32)
    y = xf * lax.rsqrt(jnp.mean(xf * xf, axis=-1, keepdims=True) + NORM_EPS)
    return (y * w.astype(jnp.float32)).astype(x.dtype)


def _l2norm(x):
    return x * lax.rsqrt(jnp.sum(x * x, axis=-1, keepdims=True) + NORM_EPS)


def _causal_dwconv(x, w):
    width, channels = w.shape
    return lax.conv_general_dilated(
        x, w[:, None, :].astype(x.dtype), window_strides=(1,),
        padding=[(width - 1, 0)], dimension_numbers=("NWC", "WIO", "NWC"),
        feature_group_count=channels)


def _rope(x, pos):
    half = x.shape[-1] // 2
    inv_freq = ROPE_THETA ** (-jnp.arange(half, dtype=jnp.float32) / half)
    ang = pos.astype(jnp.float32)[:, None] * inv_freq[None, :]
    cos = jnp.cos(ang)[None, :, None, :]
    sin = jnp.sin(ang)[None, :, None, :]
    xf = x.astype(jnp.float32)
    x1, x2 = xf[..., :half], xf[..., half:]
    return jnp.concatenate([x1 * cos - x2 * sin, x1 * sin + x2 * cos], axis=-1).astype(x.dtype)


def _causal_attention(q, k, v):
    B, H, L, dq = q.shape
    dv = v.shape[-1]
    nb = -(-L // Q_BLOCK)
    Lp = nb * Q_BLOCK
    pad = ((0, 0), (0, 0), (0, Lp - L), (0, 0))
    q, k, v = jnp.pad(q, pad), jnp.pad(k, pad), jnp.pad(v, pad)
    scale = 1.0 / math.sqrt(dq)
    q_blocks = q.reshape(B, H, nb, Q_BLOCK, dq).transpose(2, 0, 1, 3, 4)
    key_pos = jnp.arange(Lp)

    def one_block(args):
        qb, start = args
        s = jnp.einsum("bhqd,bhkd->bhqk", qb, k).astype(jnp.float32) * scale
        q_pos = start + jnp.arange(Q_BLOCK)
        mask = key_pos[None, :] <= q_pos[:, None]
        p = jax.nn.softmax(jnp.where(mask, s, -jnp.inf), axis=-1)
        return jnp.einsum("bhqk,bhkd->bhqd", p.astype(v.dtype), v)

    out = lax.map(one_block, (q_blocks, jnp.arange(nb) * Q_BLOCK))
    return out.transpose(1, 2, 0, 3, 4).reshape(B, H, Lp, dv)[:, :, :L]


def _mla(q_lat, kv_lat, k_pe, pos, q_a_norm_w, w_q_b, kv_a_norm_w, w_kv_b,
         q_norm_w, k_norm_w, mla_out_norm_w):
    B, L, _ = q_lat.shape
    q = (_rmsnorm(q_lat, q_a_norm_w) @ w_q_b).reshape(B, L, MLA_HEADS, QK_HEAD_DIM)
    kv = (_rmsnorm(kv_lat, kv_a_norm_w) @ w_kv_b).reshape(B, L, MLA_HEADS, QK_NOPE_DIM + V_HEAD_DIM)
    k_nope, v = kv[..., :QK_NOPE_DIM], kv[..., QK_NOPE_DIM:]
    k = jnp.concatenate(
        [k_nope, jnp.broadcast_to(k_pe[:, :, None, :], (B, L, MLA_HEADS, QK_ROPE_DIM))], axis=-1)
    q = _rmsnorm(q, q_norm_w)
    k = _rmsnorm(k, k_norm_w)
    q = jnp.concatenate([q[..., :QK_NOPE_DIM], _rope(q[..., QK_NOPE_DIM:], pos)], axis=-1)
    k = jnp.concatenate([k[..., :QK_NOPE_DIM], _rope(k[..., QK_NOPE_DIM:], pos)], axis=-1)
    o = _causal_attention(q.transpose(0, 2, 1, 3), k.transpose(0, 2, 1, 3), v.transpose(0, 2, 1, 3))
    o = _rmsnorm(o.transpose(0, 2, 1, 3), mla_out_norm_w)
    return o.reshape(B, L, MLA_HEADS * V_HEAD_DIM)


def _chunk_gated_delta_rule(q, k, v, g, beta):
    B, H, T, dk = q.shape
    dv = v.shape[-1]
    C = DN_CHUNK
    N = T // C
    q = q * (1.0 / math.sqrt(dk))
    q = q.reshape(B, H, N, C, dk)
    k = k.reshape(B, H, N, C, dk)
    v = v.reshape(B, H, N, C, dv)
    g = jnp.cumsum(g.reshape(B, H, N, C), axis=-1)
    beta = beta.reshape(B, H, N, C)
    tri = jnp.tril(jnp.ones((C, C), dtype=bool))
    strict = jnp.tril(jnp.ones((C, C), dtype=jnp.float32), -1)
    decay = jnp.exp(jnp.where(tri, g[..., :, None] - g[..., None, :], -jnp.inf))
    k_beta = k * beta[..., None]
    v_beta = v * beta[..., None]
    a_strict = jnp.einsum("bhnid,bhnjd->bhnij", k_beta, k) * decay * strict
    eye = jnp.eye(C, dtype=jnp.float32)
    t_inv = lax.linalg.triangular_solve(
        eye + a_strict, jnp.broadcast_to(eye, a_strict.shape),
        left_side=True, lower=True, unit_diagonal=True)
    u = jnp.einsum("bhnij,bhnjd->bhnid", t_inv, v_beta)
    w = jnp.einsum("bhnij,bhnjd->bhnid", t_inv, k_beta * jnp.exp(g)[..., None])
    qk = jnp.einsum("bhnid,bhnjd->bhnij", q, k) * decay

    def step(S, inp):
        q_c, k_c, u_c, w_c, g_c, qk_c = inp
        v_new = u_c - jnp.einsum("bhcd,bhde->bhce", w_c, S)
        o = (jnp.einsum("bhcd,bhde->bhce", q_c * jnp.exp(g_c)[..., None], S)
             + jnp.einsum("bhij,bhje->bhie", qk_c, v_new))
        g_last = g_c[..., -1]
        S = (S * jnp.exp(g_last)[..., None, None]
             + jnp.einsum("bhcd,bhce->bhde", k_c * jnp.exp(g_last[..., None] - g_c)[..., None], v_new))
        return S, o

    to_scan = lambda t: jnp.moveaxis(t, 2, 0)
    S0 = jnp.zeros((B, H, dk, dv), jnp.float32)
    _, o = lax.scan(step, S0, (to_scan(q), to_scan(k), to_scan(u), to_scan(w), to_scan(g), to_scan(qk)))
    return jnp.moveaxis(o, 0, 2).reshape(B, H, T, dv)


def _gated_deltanet(qkv, z, a, b, dn_conv_w, dn_A_log, dn_dt_bias, dn_out_norm_w):
    B, L, _ = qkv.shape
    qkv = jax.nn.silu(_causal_dwconv(qkv, dn_conv_w)).astype(jnp.float32)
    heads = lambda t: t.reshape(B, L, DN_HEADS, DN_HEAD_DIM).transpose(0, 2, 1, 3)
    q = _l2norm(heads(qkv[..., :DN_WIDTH]))
    k = _l2norm(heads(qkv[..., DN_WIDTH:2 * DN_WIDTH]))
    v = heads(qkv[..., 2 * DN_WIDTH:])
    beta = jax.nn.sigmoid(b.astype(jnp.float32)).transpose(0, 2, 1)
    g = (-jnp.exp(dn_A_log.astype(jnp.float32))
         * jax.nn.softplus(a.astype(jnp.float32) + dn_dt_bias.astype(jnp.float32))).transpose(0, 2, 1)
    pad = (-N_META) % DN_CHUNK
    p4 = ((0, 0), (0, 0), (pad, 0), (0, 0))
    p3 = ((0, 0), (0, 0), (pad, 0))
    o = _chunk_gated_delta_rule(jnp.pad(q, p4), jnp.pad(k, p4), jnp.pad(v, p4),
                                jnp.pad(g, p3), jnp.pad(beta, p3))[:, :, pad:]
    o = _rmsnorm(o.transpose(0, 2, 1, 3), dn_out_norm_w)
    o = o * jax.nn.silu(z.reshape(B, L, DN_HEADS, DN_HEAD_DIM).astype(jnp.float32))
    return o.reshape(B, L, DN_WIDTH).astype(qkv.dtype if False else z.dtype)


def _conv_glu(h, w_gate, w_up, ffn_conv_w, ffn_conv_b, w_down):
    gate = _causal_dwconv(h @ w_gate, ffn_conv_w) + ffn_conv_b
    return (jax.nn.silu(gate) * (h @ w_up)) @ w_down


def _fwd_setup_inputs(seed: int = 0) -> dict:
    key = jax.random.key(seed)
    ks = jax.random.split(key, 26)
    f32 = jnp.float32
    nrm = lambda k, shape, scale: jax.random.normal(k, shape, f32) * scale
    gain = lambda k, shape: 1.0 + 0.02 * jax.random.normal(k, shape, f32)
    Ld = DEPTH
    dt = jnp.exp(jax.random.uniform(ks[14], (Ld, DN_HEADS), f32, math.log(1e-3), math.log(1e-1)))
    return {
        "x": nrm(ks[0], (BATCH, SEQ, D_MODEL), 1.0),
        "meta_tokens": nrm(ks[1], (N_META, D_MODEL), 1.0),
        "attn_norm_w": gain(ks[2], (Ld, D_MODEL)),
        "w_in": nrm(ks[3], (Ld, D_MODEL, IN_COLS), D_MODEL ** -0.5),
        "q_a_norm_w": gain(ks[4], (Ld, Q_LORA_RANK)),
        "w_q_b": nrm(ks[5], (Ld, Q_LORA_RANK, MLA_HEADS * QK_HEAD_DIM), Q_LORA_RANK ** -0.5),
        "kv_a_norm_w": gain(ks[6], (Ld, KV_LORA_RANK)),
        "w_kv_b": nrm(ks[7], (Ld, KV_LORA_RANK, MLA_HEADS * (QK_NOPE_DIM + V_HEAD_DIM)), KV_LORA_RANK ** -0.5),
        "q_norm_w": gain(ks[8], (Ld, QK_HEAD_DIM)),
        "k_norm_w": gain(ks[9], (Ld, QK_HEAD_DIM)),
        "mla_out_norm_w": gain(ks[10], (Ld, V_HEAD_DIM)),
        "dn_conv_w": nrm(ks[11], (Ld, DN_CONV_WIDTH, 3 * DN_WIDTH), DN_CONV_WIDTH ** -0.5),
        "dn_A_log": jnp.log(jax.random.uniform(ks[12], (Ld, DN_HEADS), f32, 1.0, 16.0)),
        "dn_dt_bias": dt + jnp.log(-jnp.expm1(-dt)),
        "dn_out_norm_w": gain(ks[13], (Ld, DN_HEAD_DIM)),
        "w_out": nrm(ks[15], (Ld, MIX_WIDTH, D_MODEL), MIX_WIDTH ** -0.5),
        "ffn_norm_w": gain(ks[16], (Ld, D_MODEL)),
        "w_gate": nrm(ks[17], (Ld, D_MODEL, D_FF), D_MODEL ** -0.5),
        "w_up": nrm(ks[18], (Ld, D_MODEL, D_FF), D_MODEL ** -0.5),
        "ffn_conv_w": nrm(ks[19], (Ld, FFN_CONV_WIDTH, D_FF), FFN_CONV_WIDTH ** -0.5),
        "ffn_conv_b": nrm(ks[20], (Ld, D_FF), 0.01),
        "w_down": nrm(ks[21], (Ld, D_FF, D_MODEL), D_FF ** -0.5),
    }


def _fwd_reference(x, meta_tokens, attn_norm_w, w_in, q_a_norm_w, w_q_b, kv_a_norm_w, w_kv_b,
              q_norm_w, k_norm_w, mla_out_norm_w, dn_conv_w, dn_A_log, dn_dt_bias,
              dn_out_norm_w, w_out, ffn_norm_w, w_gate, w_up, ffn_conv_w, ffn_conv_b, w_down):
    B = x.shape[0]
    meta = jnp.broadcast_to(meta_tokens[None].astype(x.dtype), (B, N_META, D_MODEL))
    h = jnp.concatenate([meta, x], axis=1)
    L = h.shape[1]
    pos = jnp.arange(L, dtype=jnp.int32)
    c1 = Q_LORA_RANK
    c2 = c1 + KV_LORA_RANK
    c3 = c2 + QK_ROPE_DIM
    c4 = c3 + 3 * DN_WIDTH
    c5 = c4 + DN_WIDTH
    c6 = c5 + DN_HEADS
    for l in range(DEPTH):
        u = _rmsnorm(h, attn_norm_w[l])
        proj = u @ w_in[l]
        q_lat, kv_lat, k_pe = proj[..., :c1], proj[..., c1:c2], proj[..., c2:c3]
        dn_qkv, dn_z = proj[..., c3:c4], proj[..., c4:c5]
        dn_a, dn_b = proj[..., c5:c6], proj[..., c6:]
        y_mla = _mla(q_lat, kv_lat, k_pe, pos, q_a_norm_w[l], w_q_b[l], kv_a_norm_w[l], w_kv_b[l],
                     q_norm_w[l], k_norm_w[l], mla_out_norm_w[l])
        y_dn = _gated_deltanet(dn_qkv, dn_z, dn_a, dn_b, dn_conv_w[l], dn_A_log[l], dn_dt_bias[l],
                               dn_out_norm_w[l])
        mixed = jnp.concatenate([y_mla, y_dn], axis=-1)
        h = h + mixed @ w_out[l]
        h = h + _conv_glu(_rmsnorm(h, ffn_norm_w[l]), w_gate[l], w_up[l], ffn_conv_w[l],
                          ffn_conv_b[l], w_down[l])
    return h[:, N_META:]


import jax as _jax
import jax.numpy as _jnp

TWIN_FORMAT = 'train_step'
FWD_PARAMS = ['x', 'meta_tokens', 'attn_norm_w', 'w_in', 'q_a_norm_w', 'w_q_b', 'kv_a_norm_w', 'w_kv_b', 'q_norm_w', 'k_norm_w', 'mla_out_norm_w', 'dn_conv_w', 'dn_A_log', 'dn_dt_bias', 'dn_out_norm_w', 'w_out', 'ffn_norm_w', 'w_gate', 'w_up', 'ffn_conv_w', 'ffn_conv_b', 'w_down']
TWIN_WEIGHTS = ['meta_tokens', 'attn_norm_w', 'w_in', 'q_a_norm_w', 'w_q_b', 'kv_a_norm_w', 'w_kv_b', 'q_norm_w', 'k_norm_w', 'mla_out_norm_w', 'dn_conv_w', 'dn_A_log', 'dn_dt_bias', 'dn_out_norm_w', 'w_out', 'ffn_norm_w', 'w_gate', 'w_up', 'ffn_conv_w', 'ffn_conv_b', 'w_down']
TWIN_DIFF_INPUT = 'x'
TWIN_INPUTS = ['x', 'meta_tokens', 'attn_norm_w', 'w_in', 'q_a_norm_w', 'w_q_b', 'kv_a_norm_w', 'w_kv_b', 'q_norm_w', 'k_norm_w', 'mla_out_norm_w', 'dn_conv_w', 'dn_A_log', 'dn_dt_bias', 'dn_out_norm_w', 'w_out', 'ffn_norm_w', 'w_gate', 'w_up', 'ffn_conv_w', 'ffn_conv_b', 'w_down', 'loss_target', 'm_meta_tokens', 'm_attn_norm_w', 'm_w_in', 'm_q_a_norm_w', 'm_w_q_b', 'm_kv_a_norm_w', 'm_w_kv_b', 'm_q_norm_w', 'm_k_norm_w', 'm_mla_out_norm_w', 'm_dn_conv_w', 'm_dn_A_log', 'm_dn_dt_bias', 'm_dn_out_norm_w', 'm_w_out', 'm_ffn_norm_w', 'm_w_gate', 'm_w_up', 'm_ffn_conv_w', 'm_ffn_conv_b', 'm_w_down', 'v_meta_tokens', 'v_attn_norm_w', 'v_w_in', 'v_q_a_norm_w', 'v_w_q_b', 'v_kv_a_norm_w', 'v_w_kv_b', 'v_q_norm_w', 'v_k_norm_w', 'v_mla_out_norm_w', 'v_dn_conv_w', 'v_dn_A_log', 'v_dn_dt_bias', 'v_dn_out_norm_w', 'v_w_out', 'v_ffn_norm_w', 'v_w_gate', 'v_w_up', 'v_ffn_conv_w', 'v_ffn_conv_b', 'v_w_down']
TWIN_OUTPUTS = ['loss', 'grad_x', 'grad_meta_tokens', 'grad_attn_norm_w', 'grad_w_in', 'grad_q_a_norm_w', 'grad_w_q_b', 'grad_kv_a_norm_w', 'grad_w_kv_b', 'grad_q_norm_w', 'grad_k_norm_w', 'grad_mla_out_norm_w', 'grad_dn_conv_w', 'grad_dn_A_log', 'grad_dn_dt_bias', 'grad_dn_out_norm_w', 'grad_w_out', 'grad_ffn_norm_w', 'grad_w_gate', 'grad_w_up', 'grad_ffn_conv_w', 'grad_ffn_conv_b', 'grad_w_down', 'delta_meta_tokens', 'delta_attn_norm_w', 'delta_w_in', 'delta_q_a_norm_w', 'delta_w_q_b', 'delta_kv_a_norm_w', 'delta_w_kv_b', 'delta_q_norm_w', 'delta_k_norm_w', 'delta_mla_out_norm_w', 'delta_dn_conv_w', 'delta_dn_A_log', 'delta_dn_dt_bias', 'delta_dn_out_norm_w', 'delta_w_out', 'delta_ffn_norm_w', 'delta_w_gate', 'delta_w_up', 'delta_ffn_conv_w', 'delta_ffn_conv_b', 'delta_w_down', 'new_m_meta_tokens', 'new_m_attn_norm_w', 'new_m_w_in', 'new_m_q_a_norm_w', 'new_m_w_q_b', 'new_m_kv_a_norm_w', 'new_m_w_kv_b', 'new_m_q_norm_w', 'new_m_k_norm_w', 'new_m_mla_out_norm_w', 'new_m_dn_conv_w', 'new_m_dn_A_log', 'new_m_dn_dt_bias', 'new_m_dn_out_norm_w', 'new_m_w_out', 'new_m_ffn_norm_w', 'new_m_w_gate', 'new_m_w_up', 'new_m_ffn_conv_w', 'new_m_ffn_conv_b', 'new_m_w_down', 'new_v_meta_tokens', 'new_v_attn_norm_w', 'new_v_w_in', 'new_v_q_a_norm_w', 'new_v_w_q_b', 'new_v_kv_a_norm_w', 'new_v_w_kv_b', 'new_v_q_norm_w', 'new_v_k_norm_w', 'new_v_mla_out_norm_w', 'new_v_dn_conv_w', 'new_v_dn_A_log', 'new_v_dn_dt_bias', 'new_v_dn_out_norm_w', 'new_v_w_out', 'new_v_ffn_norm_w', 'new_v_w_gate', 'new_v_w_up', 'new_v_ffn_conv_w', 'new_v_ffn_conv_b', 'new_v_w_down']
TWIN_LEAF_KINDS = {'loss': 'loss', 'grad_x': 'grad_x', 'grad_meta_tokens': 'grad_w', 'grad_attn_norm_w': 'grad_w', 'grad_w_in': 'grad_w', 'grad_q_a_norm_w': 'grad_w', 'grad_w_q_b': 'grad_w', 'grad_kv_a_norm_w': 'grad_w', 'grad_w_kv_b': 'grad_w', 'grad_q_norm_w': 'grad_w', 'grad_k_norm_w': 'grad_w', 'grad_mla_out_norm_w': 'grad_w', 'grad_dn_conv_w': 'grad_w', 'grad_dn_A_log': 'grad_w', 'grad_dn_dt_bias': 'grad_w', 'grad_dn_out_norm_w': 'grad_w', 'grad_w_out': 'grad_w', 'grad_ffn_norm_w': 'grad_w', 'grad_w_gate': 'grad_w', 'grad_w_up': 'grad_w', 'grad_ffn_conv_w': 'grad_w', 'grad_ffn_conv_b': 'grad_w', 'grad_w_down': 'grad_w', 'delta_meta_tokens': 'delta_w', 'delta_attn_norm_w': 'delta_w', 'delta_w_in': 'delta_w', 'delta_q_a_norm_w': 'delta_w', 'delta_w_q_b': 'delta_w', 'delta_kv_a_norm_w': 'delta_w', 'delta_w_kv_b': 'delta_w', 'delta_q_norm_w': 'delta_w', 'delta_k_norm_w': 'delta_w', 'delta_mla_out_norm_w': 'delta_w', 'delta_dn_conv_w': 'delta_w', 'delta_dn_A_log': 'delta_w', 'delta_dn_dt_bias': 'delta_w', 'delta_dn_out_norm_w': 'delta_w', 'delta_w_out': 'delta_w', 'delta_ffn_norm_w': 'delta_w', 'delta_w_gate': 'delta_w', 'delta_w_up': 'delta_w', 'delta_ffn_conv_w': 'delta_w', 'delta_ffn_conv_b': 'delta_w', 'delta_w_down': 'delta_w', 'new_m_meta_tokens': 'new_m', 'new_m_attn_norm_w': 'new_m', 'new_m_w_in': 'new_m', 'new_m_q_a_norm_w': 'new_m', 'new_m_w_q_b': 'new_m', 'new_m_kv_a_norm_w': 'new_m', 'new_m_w_kv_b': 'new_m', 'new_m_q_norm_w': 'new_m', 'new_m_k_norm_w': 'new_m', 'new_m_mla_out_norm_w': 'new_m', 'new_m_dn_conv_w': 'new_m', 'new_m_dn_A_log': 'new_m', 'new_m_dn_dt_bias': 'new_m', 'new_m_dn_out_norm_w': 'new_m', 'new_m_w_out': 'new_m', 'new_m_ffn_norm_w': 'new_m', 'new_m_w_gate': 'new_m', 'new_m_w_up': 'new_m', 'new_m_ffn_conv_w': 'new_m', 'new_m_ffn_conv_b': 'new_m', 'new_m_w_down': 'new_m', 'new_v_meta_tokens': 'new_v', 'new_v_attn_norm_w': 'new_v', 'new_v_w_in': 'new_v', 'new_v_q_a_norm_w': 'new_v', 'new_v_w_q_b': 'new_v', 'new_v_kv_a_norm_w': 'new_v', 'new_v_w_kv_b': 'new_v', 'new_v_q_norm_w': 'new_v', 'new_v_k_norm_w': 'new_v', 'new_v_mla_out_norm_w': 'new_v', 'new_v_dn_conv_w': 'new_v', 'new_v_dn_A_log': 'new_v', 'new_v_dn_dt_bias': 'new_v', 'new_v_dn_out_norm_w': 'new_v', 'new_v_w_out': 'new_v', 'new_v_ffn_norm_w': 'new_v', 'new_v_w_gate': 'new_v', 'new_v_w_up': 'new_v', 'new_v_ffn_conv_w': 'new_v', 'new_v_ffn_conv_b': 'new_v', 'new_v_w_down': 'new_v'}


def _forward(args):
    return _fwd_reference(*[args[k] for k in FWD_PARAMS])


def _output_shape():
    out = _jax.eval_shape(lambda: _forward(_fwd_setup_inputs(0)))
    return out.shape, out.dtype

N_MICROBATCH = 1
ADAM_LR = 0.001
ADAM_B1 = 0.9
ADAM_B2 = 0.999
ADAM_EPS = 1e-08
ADAM_WD = 0.01
ADAM_STEP = 10
PER_EXAMPLE_BATCH_AXIS = {'x': 0, 'loss_target': 0}
SHARED_INPUTS = []
_WEIGHT_DTYPES = {'meta_tokens': _jnp.float32, 'attn_norm_w': _jnp.float32, 'w_in': _jnp.float32, 'q_a_norm_w': _jnp.float32, 'w_q_b': _jnp.float32, 'kv_a_norm_w': _jnp.float32, 'w_kv_b': _jnp.float32, 'q_norm_w': _jnp.float32, 'k_norm_w': _jnp.float32, 'mla_out_norm_w': _jnp.float32, 'dn_conv_w': _jnp.float32, 'dn_A_log': _jnp.float32, 'dn_dt_bias': _jnp.float32, 'dn_out_norm_w': _jnp.float32, 'w_out': _jnp.float32, 'ffn_norm_w': _jnp.float32, 'w_gate': _jnp.float32, 'w_up': _jnp.float32, 'ffn_conv_w': _jnp.float32, 'ffn_conv_b': _jnp.float32, 'w_down': _jnp.float32}
MOMENT_SCALE = {'meta_tokens': 7.208776e-02, 'attn_norm_w': 3.309279e+00, 'w_in': 5.051568e-01, 'q_a_norm_w': 8.225475e-01, 'w_q_b': 3.908036e-01, 'kv_a_norm_w': 2.289150e+00, 'w_kv_b': 8.929043e-01, 'q_norm_w': 8.493015e-01, 'k_norm_w': 8.554674e-01, 'mla_out_norm_w': 6.436406e+01, 'dn_conv_w': 3.102166e-01, 'dn_A_log': 6.145251e+00, 'dn_dt_bias': 6.056656e+00, 'dn_out_norm_w': 2.541427e+01, 'w_out': 9.711401e-01, 'ffn_norm_w': 1.240207e+01, 'w_gate': 3.137420e-01, 'w_up': 1.883971e-01, 'ffn_conv_w': 1.405373e+00, 'ffn_conv_b': 1.710097e+00, 'w_down': 2.573712e-01}


def _to_microbatches(a, axis):
    t = _jnp.moveaxis(a, axis, 0)
    t = t.reshape((N_MICROBATCH, t.shape[0] // N_MICROBATCH) + t.shape[1:])
    return _jnp.moveaxis(t, 1, axis + 1)


def setup_inputs(seed: int = 0) -> dict:
    inp = _fwd_setup_inputs(seed)
    key = _jax.random.fold_in(_jax.random.key(seed), 7919)
    shape, _ = _output_shape()
    out = dict(inp)
    out["loss_target"] = _jax.random.normal(_jax.random.fold_in(key, 0), shape, _jnp.float32)
    for i, name in enumerate(TWIN_WEIGHTS):
        w = inp[name].astype(_jnp.float32)
        if MOMENT_SCALE is None:
            s = _jnp.sqrt(_jnp.mean(_jnp.square(w)) + 1e-30)
        else:
            s = MOMENT_SCALE[name]
        km, kv = _jax.random.split(_jax.random.fold_in(key, i + 1))
        out[name] = w
        out["m_" + name] = s * _jax.random.normal(km, w.shape, _jnp.float32)
        out["v_" + name] = (s * s) * _jax.random.uniform(kv, w.shape, _jnp.float32, 0.5, 1.5)
    if N_MICROBATCH > 1:
        for name, axis in PER_EXAMPLE_BATCH_AXIS.items():
            out[name] = _to_microbatches(out[name], axis)
    return {'x': out['x'], 'meta_tokens': out['meta_tokens'], 'attn_norm_w': out['attn_norm_w'], 'w_in': out['w_in'], 'q_a_norm_w': out['q_a_norm_w'], 'w_q_b': out['w_q_b'], 'kv_a_norm_w': out['kv_a_norm_w'], 'w_kv_b': out['w_kv_b'], 'q_norm_w': out['q_norm_w'], 'k_norm_w': out['k_norm_w'], 'mla_out_norm_w': out['mla_out_norm_w'], 'dn_conv_w': out['dn_conv_w'], 'dn_A_log': out['dn_A_log'], 'dn_dt_bias': out['dn_dt_bias'], 'dn_out_norm_w': out['dn_out_norm_w'], 'w_out': out['w_out'], 'ffn_norm_w': out['ffn_norm_w'], 'w_gate': out['w_gate'], 'w_up': out['w_up'], 'ffn_conv_w': out['ffn_conv_w'], 'ffn_conv_b': out['ffn_conv_b'], 'w_down': out['w_down'], 'loss_target': out['loss_target'], 'm_meta_tokens': out['m_meta_tokens'], 'm_attn_norm_w': out['m_attn_norm_w'], 'm_w_in': out['m_w_in'], 'm_q_a_norm_w': out['m_q_a_norm_w'], 'm_w_q_b': out['m_w_q_b'], 'm_kv_a_norm_w': out['m_kv_a_norm_w'], 'm_w_kv_b': out['m_w_kv_b'], 'm_q_norm_w': out['m_q_norm_w'], 'm_k_norm_w': out['m_k_norm_w'], 'm_mla_out_norm_w': out['m_mla_out_norm_w'], 'm_dn_conv_w': out['m_dn_conv_w'], 'm_dn_A_log': out['m_dn_A_log'], 'm_dn_dt_bias': out['m_dn_dt_bias'], 'm_dn_out_norm_w': out['m_dn_out_norm_w'], 'm_w_out': out['m_w_out'], 'm_ffn_norm_w': out['m_ffn_norm_w'], 'm_w_gate': out['m_w_gate'], 'm_w_up': out['m_w_up'], 'm_ffn_conv_w': out['m_ffn_conv_w'], 'm_ffn_conv_b': out['m_ffn_conv_b'], 'm_w_down': out['m_w_down'], 'v_meta_tokens': out['v_meta_tokens'], 'v_attn_norm_w': out['v_attn_norm_w'], 'v_w_in': out['v_w_in'], 'v_q_a_norm_w': out['v_q_a_norm_w'], 'v_w_q_b': out['v_w_q_b'], 'v_kv_a_norm_w': out['v_kv_a_norm_w'], 'v_w_kv_b': out['v_w_kv_b'], 'v_q_norm_w': out['v_q_norm_w'], 'v_k_norm_w': out['v_k_norm_w'], 'v_mla_out_norm_w': out['v_mla_out_norm_w'], 'v_dn_conv_w': out['v_dn_conv_w'], 'v_dn_A_log': out['v_dn_A_log'], 'v_dn_dt_bias': out['v_dn_dt_bias'], 'v_dn_out_norm_w': out['v_dn_out_norm_w'], 'v_w_out': out['v_w_out'], 'v_ffn_norm_w': out['v_ffn_norm_w'], 'v_w_gate': out['v_w_gate'], 'v_w_up': out['v_w_up'], 'v_ffn_conv_w': out['v_ffn_conv_w'], 'v_ffn_conv_b': out['v_ffn_conv_b'], 'v_w_down': out['v_w_down']}


def _loss(weights, diff, rest, loss_target):
    with _jax.named_scope("forward"):
        args = {**rest, TWIN_DIFF_INPUT: diff, **{k: w.astype(_WEIGHT_DTYPES[k]) for k, w in weights.items()}}
        y = _forward(args)
    with _jax.named_scope("loss_head"):
        err = _jnp.square(y.astype(_jnp.float32) - loss_target)
        return 0.5 * _jnp.sum(_jnp.mean(err, axis=-1)) if err.ndim else 0.5 * err


def _adamw(w, g, m, v):
    m = ADAM_B1 * m + (1.0 - ADAM_B1) * g
    v = ADAM_B2 * v + (1.0 - ADAM_B2) * _jnp.square(g)
    m_hat = m / (1.0 - ADAM_B1 ** ADAM_STEP)
    v_hat = v / (1.0 - ADAM_B2 ** ADAM_STEP)
    delta = -ADAM_LR * (m_hat / (_jnp.sqrt(v_hat) + ADAM_EPS) + ADAM_WD * w)
    return delta, m, v


def reference(x, meta_tokens, attn_norm_w, w_in, q_a_norm_w, w_q_b, kv_a_norm_w, w_kv_b, q_norm_w, k_norm_w, mla_out_norm_w, dn_conv_w, dn_A_log, dn_dt_bias, dn_out_norm_w, w_out, ffn_norm_w, w_gate, w_up, ffn_conv_w, ffn_conv_b, w_down, loss_target, m_meta_tokens, m_attn_norm_w, m_w_in, m_q_a_norm_w, m_w_q_b, m_kv_a_norm_w, m_w_kv_b, m_q_norm_w, m_k_norm_w, m_mla_out_norm_w, m_dn_conv_w, m_dn_A_log, m_dn_dt_bias, m_dn_out_norm_w, m_w_out, m_ffn_norm_w, m_w_gate, m_w_up, m_ffn_conv_w, m_ffn_conv_b, m_w_down, v_meta_tokens, v_attn_norm_w, v_w_in, v_q_a_norm_w, v_w_q_b, v_kv_a_norm_w, v_w_kv_b, v_q_norm_w, v_k_norm_w, v_mla_out_norm_w, v_dn_conv_w, v_dn_A_log, v_dn_dt_bias, v_dn_out_norm_w, v_w_out, v_ffn_norm_w, v_w_gate, v_w_up, v_ffn_conv_w, v_ffn_conv_b, v_w_down):
    given = dict(x=x, meta_tokens=meta_tokens, attn_norm_w=attn_norm_w, w_in=w_in, q_a_norm_w=q_a_norm_w, w_q_b=w_q_b, kv_a_norm_w=kv_a_norm_w, w_kv_b=w_kv_b, q_norm_w=q_norm_w, k_norm_w=k_norm_w, mla_out_norm_w=mla_out_norm_w, dn_conv_w=dn_conv_w, dn_A_log=dn_A_log, dn_dt_bias=dn_dt_bias, dn_out_norm_w=dn_out_norm_w, w_out=w_out, ffn_norm_w=ffn_norm_w, w_gate=w_gate, w_up=w_up, ffn_conv_w=ffn_conv_w, ffn_conv_b=ffn_conv_b, w_down=w_down, loss_target=loss_target, m_meta_tokens=m_meta_tokens, m_attn_norm_w=m_attn_norm_w, m_w_in=m_w_in, m_q_a_norm_w=m_q_a_norm_w, m_w_q_b=m_w_q_b, m_kv_a_norm_w=m_kv_a_norm_w, m_w_kv_b=m_w_kv_b, m_q_norm_w=m_q_norm_w, m_k_norm_w=m_k_norm_w, m_mla_out_norm_w=m_mla_out_norm_w, m_dn_conv_w=m_dn_conv_w, m_dn_A_log=m_dn_A_log, m_dn_dt_bias=m_dn_dt_bias, m_dn_out_norm_w=m_dn_out_norm_w, m_w_out=m_w_out, m_ffn_norm_w=m_ffn_norm_w, m_w_gate=m_w_gate, m_w_up=m_w_up, m_ffn_conv_w=m_ffn_conv_w, m_ffn_conv_b=m_ffn_conv_b, m_w_down=m_w_down, v_meta_tokens=v_meta_tokens, v_attn_norm_w=v_attn_norm_w, v_w_in=v_w_in, v_q_a_norm_w=v_q_a_norm_w, v_w_q_b=v_w_q_b, v_kv_a_norm_w=v_kv_a_norm_w, v_w_kv_b=v_w_kv_b, v_q_norm_w=v_q_norm_w, v_k_norm_w=v_k_norm_w, v_mla_out_norm_w=v_mla_out_norm_w, v_dn_conv_w=v_dn_conv_w, v_dn_A_log=v_dn_A_log, v_dn_dt_bias=v_dn_dt_bias, v_dn_out_norm_w=v_dn_out_norm_w, v_w_out=v_w_out, v_ffn_norm_w=v_ffn_norm_w, v_w_gate=v_w_gate, v_w_up=v_w_up, v_ffn_conv_w=v_ffn_conv_w, v_ffn_conv_b=v_ffn_conv_b, v_w_down=v_w_down)
    weights = {n: given[n] for n in TWIN_WEIGHTS}
    shared = {n: given[n] for n in SHARED_INPUTS}
    per_example = {n: given[n] for n in ['x']}
    grad_fn = _jax.value_and_grad(_loss, argnums=(0, 1))

    def one_microbatch(ex, loss_target):
        ex = dict(ex)
        diff = ex.pop(TWIN_DIFF_INPUT)
        return grad_fn(weights, diff, {**shared, **ex}, loss_target)

    if N_MICROBATCH == 1:
        loss, (grad_w, grad_x) = one_microbatch(per_example, given["loss_target"])
    else:
        def body(carry, xs):
            loss_sum, grad_sum = carry
            l_k, (gw_k, gx_k) = one_microbatch(xs[0], xs[1])
            with _jax.named_scope("update"):
                return (loss_sum + l_k, _jax.tree.map(_jnp.add, grad_sum, gw_k)), gx_k

        init = (_jnp.zeros((), _jnp.float32), _jax.tree.map(_jnp.zeros_like, weights))
        (loss, grad_w), grad_x = _jax.lax.scan(body, init, (per_example, given["loss_target"]))
    with _jax.named_scope("update"):
        delta_w, new_m, new_v = {}, {}, {}
        for n in TWIN_WEIGHTS:
            delta_w[n], new_m[n], new_v[n] = _adamw(weights[n], grad_w[n], given["m_" + n], given["v_" + n])
    return (loss, grad_x, *[grad_w[n] for n in TWIN_WEIGHTS], *[delta_w[n] for n in TWIN_WEIGHTS],
            *[new_m[n] for n in TWIN_WEIGHTS], *[new_v[n] for n in TWIN_WEIGHTS])
```

```python
import functools
import math

import jax
import jax.numpy as jnp
from jax import lax
from jax.experimental import pallas as pl
from jax.experimental.pallas import tpu as pltpu

F32 = jnp.float32
BF16 = jnp.bfloat16
HI = lax.Precision.HIGHEST
MESH = pl.DeviceIdType.MESH

N_META = 16
D_MODEL = 1024
HEADS = 4
HEAD = 128
ROPE = 64
QK_DIM = HEAD + ROPE
QK_PAD = 2 * HEAD
LORA = 256
DN_WIDTH = HEADS * HEAD
CHUNK = 64
D_FF = 2816
IN_COLS = 2632
IN_PAD = 2816
NORM_EPS = 1e-6
ROPE_THETA = 10000.0
N_CHIPS = 4
LANES = 512

ADAM_LR, ADAM_B1, ADAM_B2, ADAM_EPS, ADAM_WD, ADAM_STEP = 0.001, 0.9, 0.999, 1e-08, 0.01, 10

VMEM_LIMIT = 56 * 1024 * 1024

BIG = (("w_in", (1024, 658), 1), ("w_q_b", (256, 192), 1), ("w_kv_b", (256, 256), 1), ("w_out", (256, 1024), 0),
       ("w_gate", (1024, 704), 1), ("w_up", (1024, 704), 1), ("w_down", (704, 1024), 0))
SMALL_SHARDED = (("meta_tokens", (16, 256), 1), ("dn_conv_w", (4, 384), 1), ("ffn_conv_w", (3, 704), 1))
REPLICATED = (("attn_norm_w", 1024), ("q_a_norm_w", 256), ("kv_a_norm_w", 256), ("q_norm_w", 192), ("k_norm_w", 192),
              ("mla_out_norm_w", 128), ("dn_A_log", 4), ("dn_dt_bias", 4), ("dn_out_norm_w", 128), ("ffn_norm_w", 1024),
              ("ffn_conv_b", 2816))
WEIGHTS = ("meta_tokens", "attn_norm_w", "w_in", "q_a_norm_w", "w_q_b", "kv_a_norm_w", "w_kv_b", "q_norm_w", "k_norm_w",
           "mla_out_norm_w", "dn_conv_w", "dn_A_log", "dn_dt_bias", "dn_out_norm_w", "w_out", "ffn_norm_w", "w_gate",
           "w_up", "ffn_conv_w", "ffn_conv_b", "w_down")

BIG_ROWS = sum(s[0] * s[1] for _, s, _ in BIG) // LANES
SMALL_ROWS = 16
REP_ROWS = 12
AG_ROWS = 6304
RS_ROWS = 6400


def _cparams(sem):
    return pltpu.CompilerParams(dimension_semantics=sem, vmem_limit_bytes=VMEM_LIMIT)


NN, NT, TN = ((1,), (0,)), ((1,), (1,)), ((0,), (0,))


def _shift_dims(dims, batch):
    if not batch:
        return (dims, ((), ()))
    return (((dims[0][0] + 1,), (dims[1][0] + 1,)), ((0,), (0,)))


def _make_mm(dims, exact, batch=False):
    def raw(a, b, d):
        if exact:
            return lax.dot_general(a.astype(F32), b.astype(F32), _shift_dims(d, batch), precision=HI,
                                   preferred_element_type=F32)
        return lax.dot_general(a.astype(BF16), b.astype(BF16), _shift_dims(d, batch), preferred_element_type=F32)

    @jax.custom_vjp
    def mm(a, b):
        return raw(a, b, dims)

    def fwd(a, b):
        return raw(a, b, dims), (a, b)

    def bwd(res, g):
        a, b = res
        if dims == NN:
            da, db = raw(g, b, NT), raw(a, g, TN)
        elif dims == NT:
            da, db = raw(g, b, NN), raw(g, a, TN)
        else:
            da, db = raw(b, g, NT), raw(a, g, NN)
        return da.astype(a.dtype), db.astype(b.dtype)

    mm.defvjp(fwd, bwd)
    return mm


_mm = _make_mm(NN, False)
_mm_nt = _make_mm(NT, False)
_mm_tn = _make_mm(TN, False)
_mmx = _make_mm(NN, True)
_bmm = _make_mm(NN, False, batch=True)
_bmm_nt = _make_mm(NT, False, batch=True)
_bmmx = _make_mm(NN, True, batch=True)


def _rms(x, w, n):
    ms = jnp.sum(x * x, axis=-1, keepdims=True) * (1.0 / n)
    return x * lax.rsqrt(ms + NORM_EPS) * w


def _silu(x):
    return x * jax.nn.sigmoid(x)


def _softplus(x):
    return jnp.maximum(x, 0.0) + jnp.log(1.0 + jnp.exp(-jnp.abs(x)))


def _rope(x, cos, sin, perm):
    return x * cos + _mmx(x, perm) * sin


def _mla_prep_fn(rows, consts):
    q_lat, kv_lat, k_pe, cos, sin = rows
    qn = _rms(q_lat, consts["qa_w"], LORA)
    kvn = _rms(kv_lat, consts["kva_w"], LORA)
    outs = []
    for h in range(HEADS):
        q_n = _mm(qn, consts["wq_n"][h])
        q_r = _mm(qn, consts["wq_r"][h])
        rs = lax.rsqrt((jnp.sum(q_n * q_n, -1, keepdims=True) + jnp.sum(q_r * q_r, -1, keepdims=True)) * (1.0 / QK_DIM)
                       + NORM_EPS)
        q_n = q_n * rs * consts["qn_n"]
        q_r = _rope(q_r * rs * consts["qn_r"], cos, sin, consts["perm"])
        k_n = _mm(kvn, consts["wk_n"][h])
        v = _mm(kvn, consts["wv"][h])
        rk = lax.rsqrt((jnp.sum(k_n * k_n, -1, keepdims=True) + jnp.sum(k_pe * k_pe, -1, keepdims=True)) * (1.0 / QK_DIM)
                       + NORM_EPS)
        k_n = k_n * rk * consts["kn_n"]
        k_r = _rope(k_pe * rk * consts["kn_r"], cos, sin, consts["perm"])
        outs += [q_n, q_r, k_n, k_r, v]
    return tuple(outs)


def _attn_fn(q, k, v, row0):
    s = _mm_nt(q, k) * (1.0 / math.sqrt(QK_DIM))
    qpos = row0 + lax.broadcasted_iota(jnp.int32, s.shape, 0)
    kpos = lax.broadcasted_iota(jnp.int32, s.shape, 1)
    s = jnp.where(kpos <= qpos, s, -1e30)
    m = lax.stop_gradient(jnp.max(s, axis=-1, keepdims=True))
    p = jnp.exp(s - m)
    p = p / jnp.sum(p, axis=-1, keepdims=True)
    return _mm(p, v)


def _dn_prep_fn(rows, consts):
    qc, kc, ab = rows
    a_b = _mmx(ab, consts["sel_a"])
    b_b = _mmx(ab, consts["sel_b"])
    beta = jax.nn.sigmoid(b_b)
    g = -jnp.exp(consts["alog"]) * _softplus(a_b + consts["dtb"])
    qs, ks = [], []
    for h in range(HEADS):
        q, k = qc[h], kc[h]
        qs.append(q * lax.rsqrt(jnp.sum(q * q, -1, keepdims=True) + NORM_EPS))
        ks.append(k * lax.rsqrt(jnp.sum(k * k, -1, keepdims=True) + NORM_EPS))
    return tuple(qs), tuple(ks), g, beta


def _dn_chunk_fn(q, k, v, gb, g64, bb):
    nb = q.shape[0]
    ri = lax.broadcasted_iota(jnp.int32, (nb, CHUNK, CHUNK), 1)
    ci = lax.broadcasted_iota(jnp.int32, (nb, CHUNK, CHUNK), 2)
    tri = ri >= ci
    strict = ri > ci
    tril = tri.astype(F32)
    eye = (ri == ci).astype(F32)
    ones = jnp.ones((nb, CHUNK, CHUNK), F32)
    gc = _bmmx(tril, gb)
    gc64 = _bmmx(tril, g64)
    grow = _bmmx(ones, eye * gc64)
    diff = gc64 - grow
    decay = jnp.where(tri, jnp.exp(jnp.where(tri, diff, 0.0)), 0.0)
    kb = k * bb
    vb = v * bb
    a = jnp.where(strict, _bmm_nt(kb, k) * decay, 0.0)
    x = -a
    tinv = eye + x
    for _ in range(5):
        x = _bmmx(x, x)
        tinv = tinv + _bmmx(tinv, x)
    u = _bmm(tinv, vb)
    w = _bmm(tinv, kb * jnp.exp(gc))
    qs = q * (1.0 / math.sqrt(HEAD))
    qk = _bmm_nt(qs, k) * decay
    qg = qs * jnp.exp(gc)
    glast = jnp.sum(gb, axis=1, keepdims=True)
    kdec = k * jnp.exp(glast - gc)
    eg = jnp.broadcast_to(jnp.exp(glast), gb.shape)
    return u, w, qg, kdec, eg, qk


def _dn_rec_fn(s, u, w, qg, qk, kdec, eg):
    v_new = u - _mm(w, s)
    o = _mm(qg, s) + _mm(qk, v_new)
    s_new = s * eg + _mm_tn(kdec, v_new)
    return s_new, o


def _dn_out_fn(o, z, w):
    return _rms(o, w, HEAD) * _silu(z)


def _row_tile(t):
    return t // 8 if (t // 8) % 16 == 0 else t


def _tile(n, pref, unit):
    best = n
    for cand in range(unit, min(n, pref) + 1, unit):
        if n % cand == 0:
            best = cand
    return best if best <= pref else n


def _rows_call(name, body, rows, consts, outs, accs, r):
    t = rows[0].shape[0]
    zero = lambda nd: (lambda i: (0,) * nd)
    in_specs = [pl.BlockSpec((r, a.shape[1]), lambda i: (i, 0)) for a in rows]
    in_specs += [pl.BlockSpec(a.shape, zero(a.ndim)) for a in consts]
    out_shape = [jax.ShapeDtypeStruct((t, w), dt) for w, dt in outs] + [jax.ShapeDtypeStruct(s, F32) for s in accs]
    out_specs = [pl.BlockSpec((r, w), lambda i: (i, 0)) for w, _ in outs] + [pl.BlockSpec(s, zero(len(s))) for s in accs]
    return pl.pallas_call(body, name=name, grid=(t // r,), in_specs=in_specs, out_specs=out_specs, out_shape=out_shape,
                          compiler_params=_cparams(("arbitrary",)))(*rows, *consts)


def _accumulate(ref, val):
    @pl.when(pl.program_id(0) == 0)
    def _():
        ref[...] = jnp.zeros(ref.shape, ref.dtype)

    ref[...] += val


def _matmul(name, a, b, dims, out_dtype, res=None):
    if dims == "nn":
        (m, k), n = a.shape, b.shape[1]
    elif dims == "nt":
        (m, k), n = a.shape, b.shape[0]
    else:
        (k, m), n = a.shape, b.shape[1]
    tm = _tile(m, 640, 16 if dims != "tn" else 128)
    tn = _tile(n, 1408, 128)
    if dims == "nn":
        a_spec, b_spec, dn = pl.BlockSpec((tm, k), lambda i, j: (i, 0)), pl.BlockSpec((k, tn), lambda i, j: (0, j)), NN
    elif dims == "nt":
        a_spec, b_spec, dn = pl.BlockSpec((tm, k), lambda i, j: (i, 0)), pl.BlockSpec((tn, k), lambda i, j: (j, 0)), NT
    else:
        a_spec, b_spec, dn = pl.BlockSpec((k, tm), lambda i, j: (0, i)), pl.BlockSpec((k, tn), lambda i, j: (0, j)), TN
    o_spec = pl.BlockSpec((tm, tn), lambda i, j: (i, j))

    def body(*refs):
        a_ref, b_ref, o_ref = refs[0], refs[1], refs[-1]
        acc = lax.dot_general(a_ref[...].astype(BF16), b_ref[...].astype(BF16), (dn, ((), ())),
                              preferred_element_type=F32)
        if res is not None:
            acc = acc + refs[2][...]
        o_ref[...] = acc.astype(out_dtype)

    ins = [a, b] + ([res] if res is not None else [])
    specs = [a_spec, b_spec] + ([o_spec] if res is not None else [])
    return pl.pallas_call(body, name=name, grid=(m // tm, n // tn), in_specs=specs, out_specs=o_spec,
                          out_shape=jax.ShapeDtypeStruct((m, n), out_dtype),
                          compiler_params=_cparams(("arbitrary", "arbitrary")))(*ins)


def _rms_fwd(name, h, w):
    n = h.shape[1]

    def body(h_ref, w_ref, o_ref):
        o_ref[...] = _rms(h_ref[...], w_ref[...], n).astype(BF16)

    return _rows_call(name, body, [h], [w], [(n, BF16)], [], _row_tile(h.shape[0]))[0]


def _rms_bwd(name, h, w, cts, resid):
    n = h.shape[1]
    nct = len(cts)

    def body(*refs):
        h_ref, ct_refs, r_ref, w_ref, dh_ref, dw_ref = refs[0], refs[1:1 + nct], refs[1 + nct], refs[2 + nct], refs[-2], refs[-1]
        ct = ct_refs[0][...].astype(F32)
        for c in ct_refs[1:]:
            ct = ct + c[...].astype(F32)
        _, vjp = jax.vjp(lambda x, ww: _rms(x, ww, n), h_ref[...], w_ref[...])
        dh, dw = vjp(ct)
        dh_ref[...] = dh + r_ref[...]
        _accumulate(dw_ref, dw)

    return _rows_call(name, body, [h, *cts, resid], [w], [(n, F32)], [(1, n)], _row_tile(h.shape[0]))


def _mla_consts_from_refs(qa, wq, kva, wkv, qn, kn, perm):
    f = lambda r: r[...].astype(F32)
    return dict(
        qa_w=f(qa), kva_w=f(kva), perm=f(perm),
        wq_n=[wq[:, h * QK_PAD:h * QK_PAD + HEAD].astype(F32) for h in range(HEADS)],
        wq_r=[wq[:, h * QK_PAD + HEAD:(h + 1) * QK_PAD].astype(F32) for h in range(HEADS)],
        wk_n=[wkv[:, h * QK_PAD:h * QK_PAD + HEAD].astype(F32) for h in range(HEADS)],
        wv=[wkv[:, h * QK_PAD + HEAD:(h + 1) * QK_PAD].astype(F32) for h in range(HEADS)],
        qn_n=qn[:, 0:HEAD], qn_r=qn[:, HEAD:QK_PAD], kn_n=kn[:, 0:HEAD], kn_r=kn[:, HEAD:QK_PAD])


def _mla_prep_fwd(q_lat, kv_lat, k_pe, cos, sin, qa, wq, kva, wkv, qn, kn, perm):
    def body(ql, kvl, kp, c, s, qa_r, wq_r, kva_r, wkv_r, qn_r, kn_r, p_r, q_out, k_out, v_out):
        consts = _mla_consts_from_refs(qa_r, wq_r, kva_r, wkv_r, qn_r, kn_r, p_r)
        outs = _mla_prep_fn((ql[...], kvl[...], kp[...], c[...], s[...]), consts)
        for h in range(HEADS):
            q_n, q_r, k_n, k_r, v = outs[5 * h:5 * h + 5]
            q_out[:, h * QK_PAD:h * QK_PAD + HEAD] = q_n.astype(BF16)
            q_out[:, h * QK_PAD + HEAD:(h + 1) * QK_PAD] = q_r.astype(BF16)
            k_out[:, h * QK_PAD:h * QK_PAD + HEAD] = k_n.astype(BF16)
            k_out[:, h * QK_PAD + HEAD:(h + 1) * QK_PAD] = k_r.astype(BF16)
            v_out[:, h * HEAD:(h + 1) * HEAD] = v.astype(BF16)

    return _rows_call("mla_prep_fwd", body, [q_lat, kv_lat, k_pe, cos, sin], [qa, wq, kva, wkv, qn, kn, perm],
                      [(HEADS * QK_PAD, BF16), (HEADS * QK_PAD, BF16), (DN_WIDTH, BF16)], [], _row_tile(q_lat.shape[0]))


def _mla_prep_bwd(q_lat, kv_lat, k_pe, cos, sin, dq, dk, dv, qa, wq, kva, wkv, qn, kn, perm):
    def body(ql, kvl, kp, c, s, dq_r, dk_r, dv_r, qa_r, wq_r, kva_r, wkv_r, qn_r, kn_r, p_r,
             dql, dkvl, dkp, dqa, dwq, dkva, dwkv, dqn, dkn):
        consts = _mla_consts_from_refs(qa_r, wq_r, kva_r, wkv_r, qn_r, kn_r, p_r)
        cc, ss, pm = c[...], s[...], consts.pop("perm")
        _, vjp = jax.vjp(lambda rows, cs: _mla_prep_fn((*rows, cc, ss), dict(cs, perm=pm)), (ql[...], kvl[...], kp[...]),
                         consts)
        cts = []
        for h in range(HEADS):
            cts += [dq_r[:, h * QK_PAD:h * QK_PAD + HEAD], dq_r[:, h * QK_PAD + HEAD:(h + 1) * QK_PAD],
                    dk_r[:, h * QK_PAD:h * QK_PAD + HEAD], dk_r[:, h * QK_PAD + HEAD:(h + 1) * QK_PAD],
                    dv_r[:, h * HEAD:(h + 1) * HEAD]]
        (d_ql, d_kvl, d_kp), dc = vjp(tuple(cts))
        dql[...] = d_ql.astype(BF16)
        dkvl[...] = d_kvl.astype(BF16)
        dkp[...] = d_kp.astype(BF16)
        first = pl.program_id(0) == 0

        def acc(ref, sl, val):
            @pl.when(first)
            def _():
                ref[sl] = val

            @pl.when(jnp.logical_not(first))
            def _():
                ref[sl] += val

        full = (slice(None), slice(None))
        acc(dqa, full, dc["qa_w"])
        acc(dkva, full, dc["kva_w"])
        for h in range(HEADS):
            acc(dwq, (slice(None), slice(h * QK_PAD, h * QK_PAD + HEAD)), dc["wq_n"][h])
            acc(dwq, (slice(None), slice(h * QK_PAD + HEAD, (h + 1) * QK_PAD)), dc["wq_r"][h])
            acc(dwkv, (slice(None), slice(h * QK_PAD, h * QK_PAD + HEAD)), dc["wk_n"][h])
            acc(dwkv, (slice(None), slice(h * QK_PAD + HEAD, (h + 1) * QK_PAD)), dc["wv"][h])
        acc(dqn, (slice(None), slice(0, HEAD)), dc["qn_n"])
        acc(dqn, (slice(None), slice(HEAD, QK_PAD)), dc["qn_r"])
        acc(dkn, (slice(None), slice(0, HEAD)), dc["kn_n"])
        acc(dkn, (slice(None), slice(HEAD, QK_PAD)), dc["kn_r"])

    return _rows_call("mla_prep_bwd", body, [q_lat, kv_lat, k_pe, cos, sin, dq, dk, dv],
                      [qa, wq, kva, wkv, qn, kn, perm],
                      [(LORA, BF16), (LORA, BF16), (HEAD, BF16)],
                      [(1, LORA), wq.shape, (1, LORA), wkv.shape, (1, QK_PAD), (1, QK_PAD)], _row_tile(q_lat.shape[0]))


def _attn_fwd(q, k, v):
    t = q.shape[0]
    rq = HEAD

    def body(q_ref, k_ref, v_ref, o_ref):
        o_ref[...] = _attn_fn(q_ref[...], k_ref[...], v_ref[...], pl.program_id(1) * rq)

    return pl.pallas_call(
        body, name="attn_fwd", grid=(HEADS, t // rq),
        in_specs=[pl.BlockSpec((rq, QK_PAD), lambda h, i: (i, h)), pl.BlockSpec((t, QK_PAD), lambda h, i: (0, h)),
                  pl.BlockSpec((t, HEAD), lambda h, i: (0, h))],
        out_specs=pl.BlockSpec((rq, HEAD), lambda h, i: (i, h)),
        out_shape=jax.ShapeDtypeStruct((t, HEADS * HEAD), F32),
        compiler_params=_cparams(("arbitrary", "arbitrary")))(q, k, v)


def _attn_bwd(q, k, v, do):
    t = q.shape[0]
    rq = HEAD

    def body(q_ref, k_ref, v_ref, do_ref, dq_ref, dk_ref, dv_ref):
        i = pl.program_id(1)
        _, vjp = jax.vjp(lambda a, b, c: _attn_fn(a, b, c, i * rq), q_ref[...].astype(F32), k_ref[...].astype(F32),
                         v_ref[...].astype(F32))
        dq, dk, dv = vjp(do_ref[...])
        dq_ref[...] = dq

        @pl.when(i == 0)
        def _():
            dk_ref[...] = dk
            dv_ref[...] = dv

        @pl.when(i != 0)
        def _():
            dk_ref[...] += dk
            dv_ref[...] += dv

    return pl.pallas_call(
        body, name="attn_bwd", grid=(HEADS, t // rq),
        in_specs=[pl.BlockSpec((rq, QK_PAD), lambda h, i: (i, h)), pl.BlockSpec((t, QK_PAD), lambda h, i: (0, h)),
                  pl.BlockSpec((t, HEAD), lambda h, i: (0, h)), pl.BlockSpec((rq, HEAD), lambda h, i: (i, h))],
        out_specs=[pl.BlockSpec((rq, QK_PAD), lambda h, i: (i, h)), pl.BlockSpec((t, QK_PAD), lambda h, i: (0, h)),
                   pl.BlockSpec((t, HEAD), lambda h, i: (0, h))],
        out_shape=[jax.ShapeDtypeStruct((t, HEADS * QK_PAD), F32), jax.ShapeDtypeStruct((t, HEADS * QK_PAD), F32),
                   jax.ShapeDtypeStruct((t, HEADS * HEAD), F32)],
        compiler_params=_cparams(("arbitrary", "arbitrary")))(q, k, v, do)


def _headwise_fwd(name, fn, rows, w, out_dtype):
    def body(*refs):
        row_refs, w_ref, o_ref = refs[:len(rows)], refs[len(rows)], refs[-1]
        for h in range(HEADS):
            sl = slice(h * HEAD, (h + 1) * HEAD)
            o_ref[:, sl] = fn(*[r[:, sl].astype(F32) for r in row_refs], w_ref[...]).astype(out_dtype)

    return _rows_call(name, body, rows, [w], [(DN_WIDTH, out_dtype)], [], _row_tile(rows[0].shape[0]))[0]


def _headwise_bwd(name, fn, rows, w, ct):
    def body(*refs):
        n = len(rows)
        row_refs, ct_ref, w_ref, d_refs, dw_ref = refs[:n], refs[n], refs[n + 1], refs[n + 2:2 * n + 2], refs[-1]
        dw_tot = None
        for h in range(HEADS):
            sl = slice(h * HEAD, (h + 1) * HEAD)
            _, vjp = jax.vjp(fn, *[r[:, sl].astype(F32) for r in row_refs], w_ref[...])
            grads = vjp(ct_ref[:, sl].astype(F32))
            for d_ref, gval in zip(d_refs, grads[:n]):
                d_ref[:, sl] = gval.astype(d_ref.dtype)
            dw_tot = grads[n] if dw_tot is None else dw_tot + grads[n]
        _accumulate(dw_ref, dw_tot)

    return _rows_call(name, body, [*rows, ct], [w], [(DN_WIDTH, F32)] * len(rows), [(1, HEAD)],
                      _row_tile(rows[0].shape[0]))


def _shift_down(x, s):
    if s == 0:
        return x
    rows = lax.broadcasted_iota(jnp.int32, x.shape, 0)
    return jnp.where(rows >= s, pltpu.roll(x, s, 0), 0.0)


def _shift_up(x, s):
    if s == 0:
        return x
    t = x.shape[0]
    rows = lax.broadcasted_iota(jnp.int32, x.shape, 0)
    return jnp.where(rows < t - s, pltpu.roll(x, t - s, 0), 0.0)


def _col_call(name, body, cols, taps, outs, tap_outs, cw):
    t, c = cols[0].shape
    in_specs = [pl.BlockSpec((t, cw), lambda j: (0, j)) for _ in cols]
    in_specs += [pl.BlockSpec((a.shape[0], cw), lambda j: (0, j)) for a in taps]
    out_shape = [jax.ShapeDtypeStruct((t, c), dt) for dt in outs] + [jax.ShapeDtypeStruct((n, c), F32) for n in tap_outs]
    out_specs = [pl.BlockSpec((t, cw), lambda j: (0, j)) for _ in outs]
    out_specs += [pl.BlockSpec((n, cw), lambda j: (0, j)) for n in tap_outs]
    return pl.pallas_call(body, name=name, grid=(c // cw,), in_specs=in_specs, out_specs=out_specs, out_shape=out_shape,
                          compiler_params=_cparams(("arbitrary",)))(*cols, *taps)


def _causal_conv(x, w_ref, width):
    acc = w_ref[width - 1:width, :] * x
    for j in range(width - 1):
        acc = acc + w_ref[j:j + 1, :] * _shift_down(x, width - 1 - j)
    return acc


def _causal_conv_bwd(x, dpre, w_ref, dx_ref, dw_ref, width):
    dx = w_ref[width - 1:width, :] * dpre
    dw_ref[width - 1:width, :] = jnp.sum(dpre * x, axis=0, keepdims=True)
    for j in range(width - 1):
        s = width - 1 - j
        dx = dx + w_ref[j:j + 1, :] * _shift_up(dpre, s)
        dw_ref[j:j + 1, :] = jnp.sum(dpre * _shift_down(x, s), axis=0, keepdims=True)
    dx_ref[...] = dx.astype(dx_ref.dtype)


def _dsilu(x):
    sg = jax.nn.sigmoid(x)
    return sg * (1.0 + x * (1.0 - sg))


def _dn_conv_fwd(x, w):
    def body(x_ref, w_ref, y_ref):
        y_ref[...] = _silu(_causal_conv(x_ref[...], w_ref, 4))

    return _col_call("dn_conv_fwd", body, [x], [w], [F32], [], 256)[0]


def _dn_conv_bwd(x, w, dy):
    def body(x_ref, dy_ref, w_ref, dx_ref, dw_ref):
        xv = x_ref[...]
        dpre = dy_ref[...] * _dsilu(_causal_conv(xv, w_ref, 4))
        _causal_conv_bwd(xv, dpre, w_ref, dx_ref, dw_ref, 4)

    return _col_call("dn_conv_bwd", body, [x, dy], [w], [BF16], [4], 256)


def _glu_fwd(gpre, up, w, b):
    def body(g_ref, u_ref, w_ref, b_ref, a_ref):
        gate = _causal_conv(g_ref[...], w_ref, 3) + b_ref[...]
        a_ref[...] = (_silu(gate) * u_ref[...]).astype(BF16)

    return _col_call("glu_fwd", body, [gpre, up], [w, b], [BF16], [], 256)[0]


def _glu_bwd(gpre, up, w, b, dact):
    def body(g_ref, u_ref, da_ref, w_ref, b_ref, dg_ref, du_ref, dw_ref, db_ref):
        gv = g_ref[...]
        gate = _causal_conv(gv, w_ref, 3) + b_ref[...]
        da = da_ref[...]
        du_ref[...] = (da * _silu(gate)).astype(BF16)
        dgate = da * u_ref[...] * _dsilu(gate)
        db_ref[...] = jnp.sum(dgate, axis=0, keepdims=True)
        _causal_conv_bwd(gv, dgate, w_ref, dg_ref, dw_ref, 3)

    return _col_call("glu_bwd", body, [gpre, up, dact], [w, b], [BF16, BF16], [3, 1], 256)


def _dn_prep_consts(sa, sb, al, dt):
    return dict(sel_a=sa[...], sel_b=sb[...], alog=al[...], dtb=dt[...])


def _dn_prep_fwd(conv, ab, sel_a, sel_b, alog, dtb):
    def body(c_ref, ab_ref, sa, sb, al, dt, q_out, k_out, g_out, b_out):
        qc = tuple(c_ref[:, h * HEAD:(h + 1) * HEAD] for h in range(HEADS))
        kc = tuple(c_ref[:, DN_WIDTH + h * HEAD:DN_WIDTH + (h + 1) * HEAD] for h in range(HEADS))
        qs, ks, g, beta = _dn_prep_fn((qc, kc, ab_ref[...]), _dn_prep_consts(sa, sb, al, dt))
        for h in range(HEADS):
            q_out[:, h * HEAD:(h + 1) * HEAD] = qs[h]
            k_out[:, h * HEAD:(h + 1) * HEAD] = ks[h]
        g_out[...] = g
        b_out[...] = beta

    return _rows_call("dn_prep_fwd", body, [conv, ab], [sel_a, sel_b, alog, dtb], [(DN_WIDTH, F32)] * 4, [],
                      _row_tile(conv.shape[0]))


def _dn_prep_bwd(conv, ab, dq, dk, dv, dg, db, sel_a, sel_b, alog, dtb):
    def body(c_ref, ab_ref, dq_r, dk_r, dv_r, dg_r, db_r, sa, sb, al, dt, dc_out, dab_out, dal_out, ddt_out):
        qc = tuple(c_ref[:, h * HEAD:(h + 1) * HEAD] for h in range(HEADS))
        kc = tuple(c_ref[:, DN_WIDTH + h * HEAD:DN_WIDTH + (h + 1) * HEAD] for h in range(HEADS))
        consts = _dn_prep_consts(sa, sb, al, dt)
        sel = dict(sel_a=consts["sel_a"], sel_b=consts["sel_b"])
        _, vjp = jax.vjp(lambda rows, ad: _dn_prep_fn(rows, {**sel, **ad}), (qc, kc, ab_ref[...]),
                         dict(alog=consts["alog"], dtb=consts["dtb"]))
        cq = tuple(dq_r[:, h * HEAD:(h + 1) * HEAD] for h in range(HEADS))
        ck = tuple(dk_r[:, h * HEAD:(h + 1) * HEAD] for h in range(HEADS))
        (dqc, dkc, dab), dad = vjp((cq, ck, dg_r[...], db_r[...]))
        for h in range(HEADS):
            dc_out[:, h * HEAD:(h + 1) * HEAD] = dqc[h]
            dc_out[:, DN_WIDTH + h * HEAD:DN_WIDTH + (h + 1) * HEAD] = dkc[h]
        dc_out[:, 2 * DN_WIDTH:3 * DN_WIDTH] = dv_r[...]
        dab_out[...] = dab.astype(BF16)
        _accumulate(dal_out, dad["alog"])
        _accumulate(ddt_out, dad["dtb"])

    return _rows_call("dn_prep_bwd", body, [conv, ab, dq, dk, dv, dg, db], [sel_a, sel_b, alog, dtb],
                      [(3 * DN_WIDTH, F32), (HEAD, BF16)], [(1, DN_WIDTH), (1, DN_WIDTH)], _row_tile(conv.shape[0]))


def _chunk_batch(t):
    nc = t // CHUNK
    return nc // 2 if nc % 2 == 0 else nc


def _dn_chunk_specs(t, nb):
    rows = nb * CHUNK
    hb = lambda h, b: (b, h)
    vb = lambda h, b: (b, 2 * HEADS + h)
    qk_spec = pl.BlockSpec((None, rows, CHUNK), lambda h, b: (h, b, 0))
    blk = pl.BlockSpec((rows, HEAD), hb)
    return rows, blk, pl.BlockSpec((rows, HEAD), vb), qk_spec


def _dn_chunk_fwd(qn, kn, conv, g, beta):
    t = qn.shape[0]
    nb = _chunk_batch(t)
    rows, blk, vblk, qk_spec = _dn_chunk_specs(t, nb)

    def body(q_ref, k_ref, v_ref, g_ref, b_ref, u_o, w_o, qg_o, kd_o, eg_o, qk_o):
        r3 = lambda x: x.reshape(nb, CHUNK, x.shape[-1])
        outs = _dn_chunk_fn(r3(q_ref[...]), r3(k_ref[...]), r3(v_ref[...]), r3(g_ref[...]), r3(g_ref[:, 0:CHUNK]),
                            r3(b_ref[...]))
        for o_ref, val in zip((u_o, w_o, qg_o, kd_o, eg_o, qk_o), outs):
            o_ref[...] = val.reshape(rows, val.shape[-1])

    return pl.pallas_call(
        body, name="dn_chunk_fwd", grid=(HEADS, t // rows), in_specs=[blk, blk, vblk, blk, blk],
        out_specs=[blk] * 5 + [qk_spec],
        out_shape=[jax.ShapeDtypeStruct((t, DN_WIDTH), F32)] * 5 + [jax.ShapeDtypeStruct((HEADS, t, CHUNK), F32)],
        compiler_params=_cparams(("arbitrary", "arbitrary")))(qn, kn, conv, g, beta)


def _dn_chunk_bwd(qn, kn, conv, g, beta, cts):
    t = qn.shape[0]
    nb = _chunk_batch(t)
    rows, blk, vblk, qk_spec = _dn_chunk_specs(t, nb)

    def body(q_ref, k_ref, v_ref, g_ref, b_ref, du, dw, dqg, dkd, deg, dqk, dq_o, dk_o, dv_o, dg_o, db_o):
        r3 = lambda x: x.reshape(nb, CHUNK, x.shape[-1])
        _, vjp = jax.vjp(_dn_chunk_fn, r3(q_ref[...]), r3(k_ref[...]), r3(v_ref[...]), r3(g_ref[...]),
                         r3(g_ref[:, 0:CHUNK]), r3(b_ref[...]))
        dq, dk, dv, dg, dg64, db = vjp(tuple(r3(c[...]) for c in (du, dw, dqg, dkd, deg, dqk)))
        for o_ref, val in zip((dq_o, dk_o, dv_o, dg_o, db_o), (dq, dk, dv, dg, db)):
            o_ref[...] = val.reshape(rows, HEAD)
        dg_o[:, 0:CHUNK] += dg64.reshape(rows, CHUNK)

    return pl.pallas_call(
        body, name="dn_chunk_bwd", grid=(HEADS, t // rows), in_specs=[blk, blk, vblk, blk, blk] + [blk] * 5 + [qk_spec],
        out_specs=[blk] * 5, out_shape=[jax.ShapeDtypeStruct((t, DN_WIDTH), F32)] * 5,
        compiler_params=_cparams(("arbitrary", "arbitrary")))(qn, kn, conv, g, beta, *cts)


def _dn_rec_fwd(u, w, qg, kd, eg, qk):
    t = u.shape[0]
    nc = t // CHUNK
    blk = pl.BlockSpec((CHUNK, DN_WIDTH), lambda c: (c, 0))
    qk_spec = pl.BlockSpec((HEADS, CHUNK, CHUNK), lambda c: (0, c, 0))
    s_spec = pl.BlockSpec((None, DN_WIDTH, HEAD), lambda c: (c, 0, 0))

    def body(u_ref, w_ref, qg_ref, kd_ref, eg_ref, qk_ref, o_ref, sall_ref, s_scr):
        @pl.when(pl.program_id(0) == 0)
        def _():
            s_scr[...] = jnp.zeros(s_scr.shape, F32)

        sall_ref[...] = s_scr[...]
        for h in range(HEADS):
            sl = slice(h * HEAD, (h + 1) * HEAD)
            s_new, o = _dn_rec_fn(s_scr[sl, :], u_ref[:, sl], w_ref[:, sl], qg_ref[:, sl], qk_ref[h], kd_ref[:, sl],
                                  eg_ref[0:1, sl])
            o_ref[:, sl] = o
            s_scr[sl, :] = s_new

    return pl.pallas_call(
        body, name="dn_rec_fwd", grid=(nc,), in_specs=[blk] * 5 + [qk_spec], out_specs=[blk, s_spec],
        out_shape=[jax.ShapeDtypeStruct((t, DN_WIDTH), F32), jax.ShapeDtypeStruct((nc, DN_WIDTH, HEAD), F32)],
        scratch_shapes=[pltpu.VMEM((DN_WIDTH, HEAD), F32)],
        compiler_params=_cparams(("arbitrary",)))(u, w, qg, kd, eg, qk)


def _dn_rec_bwd(u, w, qg, kd, eg, qk, sall, do):
    t = u.shape[0]
    nc = t // CHUNK
    blk = pl.BlockSpec((CHUNK, DN_WIDTH), lambda c: (nc - 1 - c, 0))
    qk_spec = pl.BlockSpec((HEADS, CHUNK, CHUNK), lambda c: (0, nc - 1 - c, 0))
    s_spec = pl.BlockSpec((None, DN_WIDTH, HEAD), lambda c: (nc - 1 - c, 0, 0))

    def body(u_ref, w_ref, qg_ref, kd_ref, eg_ref, qk_ref, s_ref, do_ref, du_o, dw_o, dqg_o, dkd_o, deg_o, dqk_o, ds_scr):
        @pl.when(pl.program_id(0) == 0)
        def _():
            ds_scr[...] = jnp.zeros(ds_scr.shape, F32)

        deg_o[...] = jnp.zeros(deg_o.shape, F32)
        for h in range(HEADS):
            sl = slice(h * HEAD, (h + 1) * HEAD)
            _, vjp = jax.vjp(_dn_rec_fn, s_ref[sl, :], u_ref[:, sl], w_ref[:, sl], qg_ref[:, sl], qk_ref[h],
                             kd_ref[:, sl], eg_ref[0:1, sl])
            ds, du, dw, dqg, dqk, dkd, deg = vjp((ds_scr[sl, :], do_ref[:, sl]))
            du_o[:, sl] = du
            dw_o[:, sl] = dw
            dqg_o[:, sl] = dqg
            dkd_o[:, sl] = dkd
            deg_o[0:1, sl] = deg
            dqk_o[h] = dqk
            ds_scr[sl, :] = ds

    return pl.pallas_call(
        body, name="dn_rec_bwd", grid=(nc,), in_specs=[blk] * 5 + [qk_spec, s_spec, blk],
        out_specs=[blk] * 5 + [qk_spec],
        out_shape=[jax.ShapeDtypeStruct((t, DN_WIDTH), F32)] * 5 + [jax.ShapeDtypeStruct((HEADS, t, CHUNK), F32)],
        scratch_shapes=[pltpu.VMEM((DN_WIDTH, HEAD), F32)],
        compiler_params=_cparams(("arbitrary",)))(u, w, qg, kd, eg, qk, sall, do)


def _loss_call(h2, tgt, n_valid):
    t, n = h2.shape
    r = _row_tile(t)

    def body(h_ref, t_ref, dy_ref, acc_ref):
        rows = pl.program_id(0) * r + lax.broadcasted_iota(jnp.int32, (r, n), 0)
        valid = jnp.logical_and(rows >= N_META, rows < n_valid)
        e = jnp.where(valid, h_ref[...] - t_ref[...], 0.0)
        dy_ref[...] = e * (1.0 / n)
        _accumulate(acc_ref, jnp.sum(e * e, axis=0, keepdims=True))

    return _rows_call("loss", body, [h2, tgt], [], [(n, F32)], [(1, n)], r)


def _adamw_call(name, w, g, m, v):
    rows, cols = w.shape
    tr = _tile(rows, 256, 8)

    def body(w_ref, g_ref, m_ref, v_ref, d_ref, m_out, v_out):
        gv = g_ref[...]
        m2 = ADAM_B1 * m_ref[...] + (1.0 - ADAM_B1) * gv
        v2 = ADAM_B2 * v_ref[...] + (1.0 - ADAM_B2) * (gv * gv)
        m_hat = m2 / (1.0 - ADAM_B1 ** ADAM_STEP)
        v_hat = v2 / (1.0 - ADAM_B2 ** ADAM_STEP)
        d_ref[...] = -ADAM_LR * (m_hat / (jnp.sqrt(v_hat) + ADAM_EPS) + ADAM_WD * w_ref[...])
        m_out[...] = m2
        v_out[...] = v2

    spec = pl.BlockSpec((tr, cols), lambda i: (i, 0))
    return pl.pallas_call(body, name=name, grid=(rows // tr,), in_specs=[spec] * 4, out_specs=[spec] * 3,
                          out_shape=[jax.ShapeDtypeStruct((rows, cols), F32)] * 3,
                          compiler_params=_cparams(("arbitrary",)))(w, g, m, v)


def _rope_tables(t):
    half = ROPE // 2
    inv_freq = ROPE_THETA ** (-jnp.arange(half, dtype=F32) / half)
    ang = jnp.arange(t, dtype=F32)[:, None] * inv_freq[None, :]
    z = jnp.zeros((t, HEAD - ROPE), F32)
    cos = jnp.concatenate([jnp.cos(ang), jnp.cos(ang), z], axis=1)
    sin = jnp.concatenate([jnp.sin(ang), jnp.sin(ang), z], axis=1)
    k = jnp.arange(HEAD)[:, None]
    l = jnp.arange(HEAD)[None, :]
    perm = jnp.where((l < half) & (k == l + half), -1.0, 0.0) + jnp.where((l >= half) & (l < ROPE) & (k == l - half), 1.0, 0.0)
    return cos, sin, perm.astype(F32)


def _win_to_pad(w):
    z = lambda n: jnp.zeros((w.shape[0], n), w.dtype)
    return jnp.concatenate([w[:, 576:2112], w[:, 2112:2624], w[:, 0:256], w[:, 256:512], w[:, 512:576], z(64),
                            w[:, 2624:2632], z(120)], axis=1)


def _win_from_pad(g):
    return jnp.concatenate([g[:, 2048:2304], g[:, 2304:2560], g[:, 2560:2624], g[:, 0:1536], g[:, 1536:2048],
                            g[:, 2688:2696]], axis=1)


def _qk_to_pad(w):
    lead = w.shape[:-1]
    w4 = w.reshape(*lead, HEADS, QK_DIM)
    return jnp.concatenate([w4, jnp.zeros((*lead, HEADS, QK_PAD - QK_DIM), w.dtype)], axis=-1).reshape(*lead, HEADS * QK_PAD)


def _qk_from_pad(g):
    lead = g.shape[:-1]
    return g.reshape(*lead, HEADS, QK_PAD)[..., :QK_DIM].reshape(*lead, HEADS * QK_DIM)


def _local_step(x, tgt, meta, wt):
    s = x.shape[0]
    n_valid = N_META + s
    t = -(-n_valid // HEAD) * HEAD
    zpad = jnp.zeros((t - n_valid, D_MODEL), F32)
    h0 = jnp.concatenate([meta, x, zpad], axis=0)
    tgt_p = jnp.concatenate([jnp.zeros((N_META, D_MODEL), F32), tgt, zpad], axis=0)
    cos, sin, perm = _rope_tables(t)
    win = _win_to_pad(wt["w_in"])
    wq = _qk_to_pad(wt["w_q_b"])
    wkv = wt["w_kv_b"]
    qn_w = jnp.concatenate([wt["q_norm_w"], jnp.zeros((1, QK_PAD - QK_DIM), F32)], axis=1)
    kn_w = jnp.concatenate([wt["k_norm_w"], jnp.zeros((1, QK_PAD - QK_DIM), F32)], axis=1)
    head_id = jnp.arange(DN_WIDTH)[None, :] // HEAD
    lane = jnp.arange(HEAD)[:, None]
    sel_a = (lane == head_id).astype(F32)
    sel_b = (lane == head_id + HEADS).astype(F32)
    alog = jnp.repeat(wt["dn_A_log"], HEAD, axis=1)
    dtb = jnp.repeat(wt["dn_dt_bias"], HEAD, axis=1)
    w_out, w_gate, w_up, w_down = wt["w_out"], wt["w_gate"], wt["w_up"], wt["w_down"]

    u = _rms_fwd("attn_norm_fwd", h0, wt["attn_norm_w"])
    proj = _matmul("in_proj", u, win, "nn", F32)
    dn_pre, z = proj[:, 0:1536], proj[:, 1536:2048]
    q_lat, kv_lat, k_pe, ab = proj[:, 2048:2304], proj[:, 2304:2560], proj[:, 2560:2688], proj[:, 2688:2816]
    mla_consts = (wt["q_a_norm_w"], wq, wt["kv_a_norm_w"], wkv, qn_w, kn_w, perm)
    q, k, v = _mla_prep_fwd(q_lat, kv_lat, k_pe, cos, sin, *mla_consts)
    o_mla = _attn_fwd(q, k, v)
    y_mla = _headwise_fwd("mla_out_fwd", lambda o, w: _rms(o, w, HEAD), [o_mla], wt["mla_out_norm_w"], BF16)
    conv = _dn_conv_fwd(dn_pre, wt["dn_conv_w"])
    dn_consts = (sel_a, sel_b, alog, dtb)
    qn, kn, g, beta = _dn_prep_fwd(conv, ab, *dn_consts)
    cu, cw, cqg, ckd, ceg, cqk = _dn_chunk_fwd(qn, kn, conv, g, beta)
    o_dn, sall = _dn_rec_fwd(cu, cw, cqg, ckd, ceg, cqk)
    y_dn = _headwise_fwd("dn_out_fwd", _dn_out_fn, [o_dn, z], wt["dn_out_norm_w"], BF16)
    mixed = jnp.concatenate([y_mla, y_dn], axis=1)
    h1 = _matmul("out_proj", mixed, w_out, "nn", F32, res=h0)
    n2 = _rms_fwd("ffn_norm_fwd", h1, wt["ffn_norm_w"])
    gpre = _matmul("gate_proj", n2, w_gate, "nn", F32)
    up = _matmul("up_proj", n2, w_up, "nn", F32)
    act = _glu_fwd(gpre, up, wt["ffn_conv_w"], wt["ffn_conv_b"])
    h2 = _matmul("down_proj", act, w_down, "nn", F32, res=h1)
    dy, sq = _loss_call(h2, tgt_p, n_valid)

    grads = {}
    dy16 = dy.astype(BF16)
    dact = _matmul("down_dx", dy16, w_down, "nt", F32)
    grads["w_down"] = _matmul("down_dw", act, dy16, "tn", F32)
    dgpre, dup, grads["ffn_conv_w"], grads["ffn_conv_b"] = _glu_bwd(gpre, up, wt["ffn_conv_w"], wt["ffn_conv_b"], dact)
    grads["w_gate"] = _matmul("gate_dw", n2, dgpre, "tn", F32)
    grads["w_up"] = _matmul("up_dw", n2, dup, "tn", F32)
    dn2a = _matmul("gate_dx", dgpre, w_gate, "nt", F32)
    dn2b = _matmul("up_dx", dup, w_up, "nt", F32)
    dh1, grads["ffn_norm_w"] = _rms_bwd("ffn_norm_bwd", h1, wt["ffn_norm_w"], [dn2a, dn2b], dy)
    dh1_16 = dh1.astype(BF16)
    dmixed = _matmul("out_dx", dh1_16, w_out, "nt", F32)
    grads["w_out"] = _matmul("out_dw", mixed, dh1_16, "tn", F32)
    dy_mla, dy_dn = dmixed[:, 0:DN_WIDTH], dmixed[:, DN_WIDTH:]
    do_dn, dz, grads["dn_out_norm_w"] = _headwise_bwd("dn_out_bwd", _dn_out_fn, [o_dn, z], wt["dn_out_norm_w"], dy_dn)
    rec_cts = _dn_rec_bwd(cu, cw, cqg, ckd, ceg, cqk, sall, do_dn)
    dqn, dkn, dv_dn, dg, dbeta = _dn_chunk_bwd(qn, kn, conv, g, beta, rec_cts)
    dconv, dab, dalog, ddtb = _dn_prep_bwd(conv, ab, dqn, dkn, dv_dn, dg, dbeta, *dn_consts)
    grads["dn_A_log"] = jnp.sum(dalog.reshape(HEADS, HEAD), axis=1)[None, :]
    grads["dn_dt_bias"] = jnp.sum(ddtb.reshape(HEADS, HEAD), axis=1)[None, :]
    ddn_pre, grads["dn_conv_w"] = _dn_conv_bwd(dn_pre, wt["dn_conv_w"], dconv)
    do_mla, grads["mla_out_norm_w"] = _headwise_bwd("mla_out_bwd", lambda o, w: _rms(o, w, HEAD), [o_mla],
                                                    wt["mla_out_norm_w"], dy_mla)
    dq, dk, dv = _attn_bwd(q, k, v, do_mla)
    dq_lat, dkv_lat, dk_pe, dqa, dwq, dkva, dwkv, dqnw, dknw = _mla_prep_bwd(q_lat, kv_lat, k_pe, cos, sin, dq, dk, dv,
                                                                               *mla_consts)
    grads["q_a_norm_w"], grads["kv_a_norm_w"] = dqa, dkva
    grads["w_q_b"], grads["w_kv_b"] = _qk_from_pad(dwq), dwkv
    grads["q_norm_w"], grads["k_norm_w"] = dqnw[:, :QK_DIM], dknw[:, :QK_DIM]
    dproj = jnp.concatenate([ddn_pre, dz.astype(BF16), dq_lat, dkv_lat, dk_pe, dab], axis=1)
    grads["w_in"] = _win_from_pad(_matmul("in_dw", u, dproj, "tn", F32))
    du = _matmul("in_dx", dproj, win, "nt", F32)
    dh0, grads["attn_norm_w"] = _rms_bwd("attn_norm_bwd", h0, wt["attn_norm_w"], [du], dh1)
    grads["meta_tokens"] = dh0[0:N_META]
    return sq, dh0[N_META:n_valid], grads


def _mesh_pos():
    return lax.axis_index("x"), lax.axis_index("y"), lax.axis_index("c")


def _other_chips(x, y):
    return [(1 - x, y), (x, 1 - y), (1 - x, 1 - y)]


def _all_gather(wb, wf):
    any_spec = pl.BlockSpec(memory_space=pl.ANY)

    def body(wb_ref, wf_ref, gb_ref, gf_ref, send_sems, recv_sems, local_sems):
        x, y, c = _mesh_pos()
        p = 2 * x + y
        sibling = (x, y, 1 - c)
        chips = _other_chips(x, y)
        bufs = ((wb_ref, gb_ref, wb_ref.shape[0] // 2), (wf_ref, gf_ref, wf_ref.shape[0] // 2))

        def half(ref, rows, which):
            return ref.at[pl.ds(which * rows, rows), :]

        local = [pltpu.make_async_copy(src, dst.at[p], local_sems.at[i]) for i, (src, dst, _) in enumerate(bufs)]
        for cp in local:
            cp.start()

        def copy(i, k, src, dst, to):
            return pltpu.make_async_remote_copy(src_ref=src, dst_ref=dst, send_sem=send_sems.at[i, k],
                                                recv_sem=recv_sems.at[i, k], device_id=to, device_id_type=MESH)

        sends = []
        for i, (src, dst, rows) in enumerate(bufs):
            for j, chip in enumerate(chips):
                sends.append(copy(i, j, half(src, rows, c), half(dst.at[p], rows, c), (*chip, c)))
        for cp in sends:
            cp.start()
        passed = []
        for i, (src, dst, rows) in enumerate(bufs):
            for j, (qx, qy) in enumerate(chips):
                block = half(dst.at[2 * qx + qy], rows, c)
                copy(i, j, block, block, (x, y, c)).wait_recv()
                fwd = copy(i, 3 + j, block, block, sibling)
                fwd.start()
                passed.append(fwd)
        for i, (src, dst, rows) in enumerate(bufs):
            for j, (qx, qy) in enumerate(chips):
                block = half(dst.at[2 * qx + qy], rows, 1 - c)
                copy(i, 3 + j, block, block, (x, y, c)).wait_recv()
        for cp in sends + passed:
            cp.wait_send()
        for cp in local:
            cp.wait()

    return pl.pallas_call(
        body, name="all_gather_weights", in_specs=[any_spec, any_spec], out_specs=[any_spec, any_spec],
        out_shape=[jax.ShapeDtypeStruct((N_CHIPS, *wb.shape), wb.dtype), jax.ShapeDtypeStruct((N_CHIPS, *wf.shape), wf.dtype)],
        scratch_shapes=[pltpu.SemaphoreType.DMA((2, 6)), pltpu.SemaphoreType.DMA((2, 6)), pltpu.SemaphoreType.DMA((2,))],
        compiler_params=pltpu.CompilerParams(has_side_effects=True))(wb, wf)


def _rs_to_sibling(buf):
    half = buf.shape[1] // 2
    any_spec = pl.BlockSpec(memory_space=pl.ANY)

    def body(buf_ref, got_ref, send_sem, recv_sem):
        x, y, c = _mesh_pos()
        src = buf_ref.at[:, pl.ds((1 - c) * half, half), :]
        cp = pltpu.make_async_remote_copy(src_ref=src, dst_ref=got_ref, send_sem=send_sem, recv_sem=recv_sem,
                                          device_id=(x, y, 1 - c), device_id_type=MESH)
        cp.start()
        cp.wait()

    return pl.pallas_call(
        body, name="rs_sibling_exchange", in_specs=[any_spec], out_specs=any_spec,
        out_shape=jax.ShapeDtypeStruct((N_CHIPS, half, LANES), F32),
        scratch_shapes=[pltpu.SemaphoreType.DMA, pltpu.SemaphoreType.DMA],
        compiler_params=pltpu.CompilerParams(has_side_effects=True))(buf)


def _rs_pair_add(buf, got, c):
    half = got.shape[1]
    tr = _tile(half, 400, 8)
    nb = half // tr

    def body(c_ref, a_ref, b_ref, o_ref):
        o_ref[...] = a_ref[...] + b_ref[...]

    return pl.pallas_call(
        body, name="rs_pair_add",
        grid_spec=pltpu.PrefetchScalarGridSpec(
            num_scalar_prefetch=1, grid=(N_CHIPS, nb),
            in_specs=[pl.BlockSpec((None, tr, LANES), lambda j, i, cr: (j, cr[0] * nb + i, 0)),
                      pl.BlockSpec((None, tr, LANES), lambda j, i, cr: (j, i, 0))],
            out_specs=pl.BlockSpec((None, tr, LANES), lambda j, i, cr: (j, i, 0))),
        out_shape=jax.ShapeDtypeStruct(got.shape, F32),
        compiler_params=_cparams(("arbitrary", "arbitrary")))(c, buf, got)


def _rs_to_chips(acc):
    half = acc.shape[1]
    any_spec = pl.BlockSpec(memory_space=pl.ANY)

    def body(acc_ref, got_ref, send_sems, recv_sems):
        x, y, c = _mesh_pos()
        copies = []
        for k, (qx, qy) in enumerate(_other_chips(x, y)):
            copies.append(pltpu.make_async_remote_copy(
                src_ref=acc_ref.at[2 * qx + qy], dst_ref=got_ref.at[k], send_sem=send_sems.at[k],
                recv_sem=recv_sems.at[k], device_id=(qx, qy, c), device_id_type=MESH))
        for cp in copies:
            cp.start()
        for cp in copies:
            cp.wait()

    return pl.pallas_call(
        body, name="rs_chip_exchange", in_specs=[any_spec], out_specs=any_spec,
        out_shape=jax.ShapeDtypeStruct((3, half, LANES), F32),
        scratch_shapes=[pltpu.SemaphoreType.DMA((3,)), pltpu.SemaphoreType.DMA((3,))],
        compiler_params=pltpu.CompilerParams(has_side_effects=True))(acc)


def _rs_chip_add(acc, got, p):
    half = acc.shape[1]
    tr = _tile(half, 400, 8)
    slot = (0, 1, 0, 2)

    def body(p_ref, own_ref, g0_ref, g1_ref, g2_ref, o_ref):
        me = p_ref[0]
        gots = (g0_ref, g1_ref, g2_ref)
        total = None
        for chip in range(N_CHIPS):
            val = own_ref[...]
            for e in (1, 2, 3):
                val = jnp.where((chip ^ me) == e, gots[slot[e]][...], val)
            total = val if total is None else total + val
        o_ref[...] = total

    gspec = lambda k: pl.BlockSpec((None, tr, LANES), lambda i, pr: (k, i, 0))
    return pl.pallas_call(
        body, name="rs_chip_add",
        grid_spec=pltpu.PrefetchScalarGridSpec(
            num_scalar_prefetch=1, grid=(half // tr,),
            in_specs=[pl.BlockSpec((None, tr, LANES), lambda i, pr: (pr[0], i, 0)), gspec(0), gspec(1), gspec(2)],
            out_specs=pl.BlockSpec((tr, LANES), lambda i, pr: (i, 0))),
        out_shape=jax.ShapeDtypeStruct((half, LANES), F32),
        compiler_params=_cparams(("arbitrary",)))(p, acc, got, got, got)


def _rs_share(res):
    half = res.shape[0]
    any_spec = pl.BlockSpec(memory_space=pl.ANY)

    def body(res_ref, out_ref, send_sem, recv_sem, local_sem):
        x, y, c = _mesh_pos()
        mine = out_ref.at[pl.ds(c * half, half), :]
        theirs = out_ref.at[pl.ds((1 - c) * half, half), :]
        local = pltpu.make_async_copy(res_ref, mine, local_sem)
        local.start()
        cp = pltpu.make_async_remote_copy(src_ref=res_ref, dst_ref=mine, send_sem=send_sem, recv_sem=recv_sem,
                                          device_id=(x, y, 1 - c), device_id_type=MESH)
        cp.start()
        cp.wait_send()
        pltpu.make_async_remote_copy(src_ref=res_ref, dst_ref=theirs, send_sem=send_sem, recv_sem=recv_sem,
                                     device_id=(x, y, 1 - c), device_id_type=MESH).wait_recv()
        local.wait()

    return pl.pallas_call(
        body, name="rs_sibling_share", in_specs=[any_spec], out_specs=any_spec,
        out_shape=jax.ShapeDtypeStruct((2 * half, LANES), F32),
        scratch_shapes=[pltpu.SemaphoreType.DMA, pltpu.SemaphoreType.DMA, pltpu.SemaphoreType.DMA],
        compiler_params=pltpu.CompilerParams(has_side_effects=True))(res)


def _reduce_scatter(buf):
    x, y, c = _mesh_pos()
    got = _rs_to_sibling(buf)
    acc = _rs_pair_add(buf, got, jnp.reshape(c, (1,)).astype(jnp.int32))
    got2 = _rs_to_chips(acc)
    res = _rs_chip_add(acc, got2, jnp.reshape(2 * x + y, (1,)).astype(jnp.int32))
    return _rs_share(res)


def _pad_rows(flat, rows):
    return jnp.concatenate([flat, jnp.zeros((rows * LANES - flat.shape[0],), flat.dtype)]).reshape(rows, LANES)


def _unshard(g4, shape, axis):
    a = g4.reshape(N_CHIPS, *shape)
    if axis == 0:
        return a.reshape(N_CHIPS * shape[0], shape[1])
    return jnp.transpose(a, (1, 0, 2)).reshape(shape[0], N_CHIPS * shape[1])


def _shard4(full, shape, axis):
    if axis == 0:
        return full.reshape(N_CHIPS, shape[0] * shape[1])
    a = full.reshape(shape[0], N_CHIPS, shape[1])
    return jnp.transpose(a, (1, 0, 2)).reshape(N_CHIPS, shape[0] * shape[1])


def kernel(x, meta_tokens, attn_norm_w, w_in, q_a_norm_w, w_q_b, kv_a_norm_w, w_kv_b, q_norm_w, k_norm_w, mla_out_norm_w, dn_conv_w, dn_A_log, dn_dt_bias, dn_out_norm_w, w_out, ffn_norm_w, w_gate, w_up, ffn_conv_w, ffn_conv_b, w_down, loss_target, m_meta_tokens, m_attn_norm_w, m_w_in, m_q_a_norm_w, m_w_q_b, m_kv_a_norm_w, m_w_kv_b, m_q_norm_w, m_k_norm_w, m_mla_out_norm_w, m_dn_conv_w, m_dn_A_log, m_dn_dt_bias, m_dn_out_norm_w, m_w_out, m_ffn_norm_w, m_w_gate, m_w_up, m_ffn_conv_w, m_ffn_conv_b, m_w_down, v_meta_tokens, v_attn_norm_w, v_w_in, v_q_a_norm_w, v_w_q_b, v_kv_a_norm_w, v_w_kv_b, v_q_norm_w, v_k_norm_w, v_mla_out_norm_w, v_dn_conv_w, v_dn_A_log, v_dn_dt_bias, v_dn_out_norm_w, v_w_out, v_ffn_norm_w, v_w_gate, v_w_up, v_ffn_conv_w, v_ffn_conv_b, v_w_down):
    local = dict(locals())
    w = {n: local[n] for n in WEIGHTS}
    m = {n: local["m_" + n] for n in WEIGHTS}
    v = {n: local["v_" + n] for n in WEIGHTS}
    big_shape = {n: s for n, s, _ in BIG}
    small_shape = {n: s for n, s, _ in SMALL_SHARDED}

    wb = _pad_rows(jnp.concatenate([w[n].reshape(-1).astype(BF16) for n, _, _ in BIG]), AG_ROWS)
    wf = _pad_rows(jnp.concatenate([w[n].reshape(-1) for n, _, _ in SMALL_SHARDED]), SMALL_ROWS)
    gb, gf = _all_gather(wb, wf)
    gb = gb.reshape(N_CHIPS, -1)
    gf = gf.reshape(N_CHIPS, -1)
    full = {}
    off = 0
    for n, s, ax in BIG:
        full[n] = _unshard(gb[:, off:off + s[0] * s[1]], s, ax)
        off += s[0] * s[1]
    off = 0
    for n, s, ax in SMALL_SHARDED:
        full[n] = _unshard(gf[:, off:off + s[0] * s[1]], s, ax)
        off += s[0] * s[1]
    for n, _ in REPLICATED:
        full[n] = w[n]

    sq, grad_x, g = _local_step(x[0], loss_target[0], full["meta_tokens"], full)
    loss = lax.psum(0.5 / D_MODEL * jnp.sum(sq), ("x", "y", "c"))

    rep = jnp.concatenate([g[n].reshape(-1) for n, _ in REPLICATED])
    rep = jnp.concatenate([rep, jnp.zeros(((RS_ROWS - BIG_ROWS - SMALL_ROWS) * LANES - rep.shape[0],), F32)])
    small = jnp.concatenate([_shard4(g[n], s, ax) for n, s, ax in SMALL_SHARDED], axis=1)
    small = jnp.concatenate([small, jnp.zeros((N_CHIPS, SMALL_ROWS * LANES - small.shape[1]), F32)], axis=1)
    strips = jnp.concatenate([_shard4(g[n], s, ax) for n, s, ax in BIG] + [small, jnp.broadcast_to(rep, (N_CHIPS, rep.shape[0]))],
                             axis=1)
    red = _reduce_scatter(strips.reshape(N_CHIPS, RS_ROWS, LANES)).reshape(-1)
    gs = {}
    off = 0
    for n, s, _ in BIG:
        gs[n] = red[off:off + s[0] * s[1]].reshape(s)
        off += s[0] * s[1]
    off = BIG_ROWS * LANES
    for n, s, _ in SMALL_SHARDED:
        gs[n] = red[off:off + s[0] * s[1]].reshape(s)
        off += s[0] * s[1]
    off = (BIG_ROWS + SMALL_ROWS) * LANES
    for n, cnt in REPLICATED:
        gs[n] = red[off:off + cnt].reshape(1, cnt)
        off += cnt

    delta, new_m, new_v = {}, {}, {}
    for n, s, _ in BIG:
        d2, m2, v2 = _adamw_call("adamw_" + n, w[n].reshape(s), gs[n], m[n].reshape(s), v[n].reshape(s))
        delta[n], new_m[n], new_v[n] = (a.reshape(w[n].shape) for a in (d2, m2, v2))
    small_names = [n for n, _, _ in SMALL_SHARDED] + [n for n, _ in REPLICATED]
    rows = SMALL_ROWS + REP_ROWS + 4
    pack = lambda d: _pad_rows(jnp.concatenate([d[n].reshape(-1) for n in small_names]), rows)
    d2, m2, v2 = _adamw_call("adamw_small", pack(w), pack(gs), pack(m), pack(v))
    off = 0
    for n in small_names:
        cnt = w[n].size
        for dst, src in ((delta, d2), (new_m, m2), (new_v, v2)):
            dst[n] = src.reshape(-1)[off:off + cnt].reshape(w[n].shape)
        off += cnt

    grad_out = [gs[n].reshape(w[n].shape) for n in WEIGHTS]
    return (loss, grad_x[None], *grad_out, *[delta[n] for n in WEIGHTS], *[new_m[n] for n in WEIGHTS],
            *[new_v[n] for n in WEIGHTS])
```

```python
import functools
import math

import jax
import jax.numpy as jnp
from jax import lax
from jax.experimental import pallas as pl
from jax.experimental.pallas import tpu as pltpu

F32 = jnp.float32
BF16 = jnp.bfloat16
HI = lax.Precision.HIGHEST
MESH = pl.DeviceIdType.MESH

N_META = 16
D_MODEL = 1024
HEADS = 4
HEAD = 128
ROPE = 64
QK_DIM = HEAD + ROPE
QK_PAD = 2 * HEAD
LORA = 256
DN_WIDTH = HEADS * HEAD
CHUNK = 64
D_FF = 2816
N_CHIPS = 4
FF_SHARD = D_FF // N_CHIPS
FF_BLOCK = 768
D_FF_P = N_CHIPS * FF_BLOCK
IN_COLS = 2632
IN_SHARD = IN_COLS // N_CHIPS
IN_SHARD_P = 672
IN_PAD = 2816
NORM_EPS = 1e-6
ROPE_THETA = 10000.0
LANES = 512

ADAM_LR, ADAM_B1, ADAM_B2, ADAM_EPS, ADAM_WD, ADAM_STEP = 0.001, 0.9, 0.999, 1e-08, 0.01, 10

VMEM_LIMIT = 56 * 1024 * 1024

BIG = (("w_in", (1024, 658), 1, IN_SHARD_P), ("w_q_b", (256, 192), 1, 192), ("w_kv_b", (256, 256), 1, 256),
       ("w_out", (256, 1024), 0, 256), ("w_gate", (1024, 704), 1, FF_BLOCK), ("w_up", (1024, 704), 1, FF_BLOCK),
       ("w_down", (704, 1024), 0, FF_BLOCK))
SMALL_SHARDED = (("meta_tokens", (16, 256), 1), ("dn_conv_w", (4, 384), 1), ("ffn_conv_w", (3, 704), 1))
REPLICATED = (("attn_norm_w", 1024), ("q_a_norm_w", 256), ("kv_a_norm_w", 256), ("q_norm_w", 192), ("k_norm_w", 192),
              ("mla_out_norm_w", 128), ("dn_A_log", 4), ("dn_dt_bias", 4), ("dn_out_norm_w", 128), ("ffn_norm_w", 1024),
              ("ffn_conv_b", 2816))
WEIGHTS = ("meta_tokens", "attn_norm_w", "w_in", "q_a_norm_w", "w_q_b", "kv_a_norm_w", "w_kv_b", "q_norm_w", "k_norm_w",
           "mla_out_norm_w", "dn_conv_w", "dn_A_log", "dn_dt_bias", "dn_out_norm_w", "w_out", "ffn_norm_w", "w_gate",
           "w_up", "ffn_conv_w", "ffn_conv_b", "w_down")

SMALL_ROWS = 16
REP_ROWS = 16


def _cparams(sem):
    return pltpu.CompilerParams(dimension_semantics=sem, vmem_limit_bytes=VMEM_LIMIT)


NN, NT, TN = ((1,), (0,)), ((1,), (1,)), ((0,), (0,))


def _shift_dims(dims, batch):
    if not batch:
        return (dims, ((), ()))
    return (((dims[0][0] + 1,), (dims[1][0] + 1,)), ((0,), (0,)))


def _make_mm(dims, exact, batch=False):
    def raw(a, b, d):
        if exact:
            return lax.dot_general(a.astype(F32), b.astype(F32), _shift_dims(d, batch), precision=HI,
                                   preferred_element_type=F32)
        return lax.dot_general(a.astype(BF16), b.astype(BF16), _shift_dims(d, batch), preferred_element_type=F32)

    @jax.custom_vjp
    def mm(a, b):
        return raw(a, b, dims)

    def fwd(a, b):
        return raw(a, b, dims), (a, b)

    def bwd(res, g):
        a, b = res
        if dims == NN:
            da, db = raw(g, b, NT), raw(a, g, TN)
        elif dims == NT:
            da, db = raw(g, b, NN), raw(g, a, TN)
        else:
            da, db = raw(b, g, NT), raw(a, g, NN)
        return da.astype(a.dtype), db.astype(b.dtype)

    mm.defvjp(fwd, bwd)
    return mm


_mm = _make_mm(NN, False)
_mm_nt = _make_mm(NT, False)
_mm_tn = _make_mm(TN, False)
_mmx = _make_mm(NN, True)
_bmm = _make_mm(NN, False, batch=True)
_bmm_nt = _make_mm(NT, False, batch=True)
_bmmx = _make_mm(NN, True, batch=True)


def _rms(x, w, n):
    ms = jnp.sum(x * x, axis=-1, keepdims=True) * (1.0 / n)
    return x * lax.rsqrt(ms + NORM_EPS) * w


def _silu(x):
    return x * jax.nn.sigmoid(x)


def _softplus(x):
    return jnp.maximum(x, 0.0) + jnp.log(1.0 + jnp.exp(-jnp.abs(x)))


def _rope(x, cos, sin, perm):
    return x * cos + _mmx(x, perm) * sin


def _mla_prep_fn(rows, consts):
    q_lat, kv_lat, k_pe, cos, sin = rows
    qn = _rms(q_lat, consts["qa_w"], LORA)
    kvn = _rms(kv_lat, consts["kva_w"], LORA)
    outs = []
    for h in range(HEADS):
        q_n = _mm_nt(qn, consts["wq_n"][h])
        q_r = _mm_nt(qn, consts["wq_r"][h])
        rs = lax.rsqrt((jnp.sum(q_n * q_n, -1, keepdims=True) + jnp.sum(q_r * q_r, -1, keepdims=True)) * (1.0 / QK_DIM)
                       + NORM_EPS)
        q_n = q_n * rs * consts["qn_n"]
        q_r = _rope(q_r * rs * consts["qn_r"], cos, sin, consts["perm"])
        k_n = _mm_nt(kvn, consts["wk_n"][h])
        v = _mm_nt(kvn, consts["wv"][h])
        rk = lax.rsqrt((jnp.sum(k_n * k_n, -1, keepdims=True) + jnp.sum(k_pe * k_pe, -1, keepdims=True)) * (1.0 / QK_DIM)
                       + NORM_EPS)
        k_n = k_n * rk * consts["kn_n"]
        k_r = _rope(k_pe * rk * consts["kn_r"], cos, sin, consts["perm"])
        outs += [q_n, q_r, k_n, k_r, v]
    return tuple(outs)


def _attn_fn(q, k, v, row0):
    s = _mm_nt(q, k) * (1.0 / math.sqrt(QK_DIM))
    qpos = row0 + lax.broadcasted_iota(jnp.int32, s.shape, 0)
    kpos = lax.broadcasted_iota(jnp.int32, s.shape, 1)
    s = jnp.where(kpos <= qpos, s, -1e30)
    m = lax.stop_gradient(jnp.max(s, axis=-1, keepdims=True))
    p = jnp.exp(s - m)
    p = p / jnp.sum(p, axis=-1, keepdims=True)
    return _mm(p, v)


def _dn_prep_fn(rows, consts):
    qc, kc, ab = rows
    a_b = _mmx(ab, consts["sel_a"])
    b_b = _mmx(ab, consts["sel_b"])
    beta = jax.nn.sigmoid(b_b)
    g = -jnp.exp(consts["alog"]) * _softplus(a_b + consts["dtb"])
    qs, ks = [], []
    for h in range(HEADS):
        q, k = qc[h], kc[h]
        qs.append(q * lax.rsqrt(jnp.sum(q * q, -1, keepdims=True) + NORM_EPS))
        ks.append(k * lax.rsqrt(jnp.sum(k * k, -1, keepdims=True) + NORM_EPS))
    return tuple(qs), tuple(ks), g, beta


def _dn_chunk_fn(q, k, v, gb, g64, bb):
    nb = q.shape[0]
    ri = lax.broadcasted_iota(jnp.int32, (nb, CHUNK, CHUNK), 1)
    ci = lax.broadcasted_iota(jnp.int32, (nb, CHUNK, CHUNK), 2)
    tri = ri >= ci
    strict = ri > ci
    tril = tri.astype(F32)
    eye = (ri == ci).astype(F32)
    ones = jnp.ones((nb, CHUNK, CHUNK), F32)
    gc = _bmmx(tril, gb)
    gc64 = _bmmx(tril, g64)
    grow = _bmmx(ones, eye * gc64)
    diff = gc64 - grow
    decay = jnp.where(tri, jnp.exp(jnp.where(tri, diff, 0.0)), 0.0)
    kb = k * bb
    vb = v * bb
    a = jnp.where(strict, _bmm_nt(kb, k) * decay, 0.0)
    x = -a
    tinv = eye + x
    for _ in range(5):
        x = _bmmx(x, x)
        tinv = tinv + _bmmx(tinv, x)
    u = _bmm(tinv, vb)
    w = _bmm(tinv, kb * jnp.exp(gc))
    qs = q * (1.0 / math.sqrt(HEAD))
    qk = _bmm_nt(qs, k) * decay
    qg = qs * jnp.exp(gc)
    glast = jnp.sum(gb, axis=1, keepdims=True)
    kdec = k * jnp.exp(glast - gc)
    eg = jnp.broadcast_to(jnp.exp(glast), gb.shape)
    return u, w, qg, kdec, eg, qk


def _dn_rec_fn(s, u, w, qg, qk, kdec, eg):
    v_new = u - _mm(w, s)
    o = _mm(qg, s) + _mm(qk, v_new)
    s_new = s * eg + _mm_tn(kdec, v_new)
    return s_new, o


def _dn_out_fn(o, z, w):
    return _rms(o, w, HEAD) * _silu(z)


def _row_tile(t):
    return t // 8 if (t // 8) % 16 == 0 else t


def _tile(n, pref, unit):
    best = n
    for cand in range(unit, min(n, pref) + 1, unit):
        if n % cand == 0:
            best = cand
    return best if best <= pref else n


def _rows_call(name, body, rows, consts, outs, accs, r):
    rows = [a if isinstance(a, tuple) else (a, a.shape[1], 0) for a in rows]
    t = rows[0][0].shape[0]
    zero = lambda nd: (lambda i: (0,) * nd)
    in_specs = [pl.BlockSpec((r, w), functools.partial(lambda i, b: (i, b), b=blk)) for _, w, blk in rows]
    rows = [a for a, _, _ in rows]
    in_specs += [pl.BlockSpec(a.shape, zero(a.ndim)) for a in consts]
    out_shape = [jax.ShapeDtypeStruct((t, w), dt) for w, dt in outs] + [jax.ShapeDtypeStruct(s, F32) for s in accs]
    out_specs = [pl.BlockSpec((r, w), lambda i: (i, 0)) for w, _ in outs] + [pl.BlockSpec(s, zero(len(s))) for s in accs]
    return pl.pallas_call(body, name=name, grid=(t // r,), in_specs=in_specs, out_specs=out_specs, out_shape=out_shape,
                          compiler_params=_cparams(("arbitrary",)))(*rows, *consts)


def _accumulate(ref, val):
    @pl.when(pl.program_id(0) == 0)
    def _():
        ref[...] = jnp.zeros(ref.shape, ref.dtype)

    ref[...] += val


def _matmul(name, a, b, dims, out_dtype, res=None):
    if dims == "nn":
        (m, k), n = a.shape, b.shape[1]
    elif dims == "nt":
        (m, k), n = a.shape, b.shape[0]
    else:
        (k, m), n = a.shape, b.shape[1]
    tm = _tile(m, 640, 16 if dims != "tn" else 128)
    tn = _tile(n, 1408, 128)
    if dims == "nn":
        a_spec, b_spec, dn = pl.BlockSpec((tm, k), lambda i, j: (i, 0)), pl.BlockSpec((k, tn), lambda i, j: (0, j)), NN
    elif dims == "nt":
        a_spec, b_spec, dn = pl.BlockSpec((tm, k), lambda i, j: (i, 0)), pl.BlockSpec((tn, k), lambda i, j: (j, 0)), NT
    else:
        a_spec, b_spec, dn = pl.BlockSpec((k, tm), lambda i, j: (0, i)), pl.BlockSpec((k, tn), lambda i, j: (0, j)), TN
    o_spec = pl.BlockSpec((tm, tn), lambda i, j: (i, j))

    def body(*refs):
        a_ref, b_ref, o_ref = refs[0], refs[1], refs[-1]
        acc = lax.dot_general(a_ref[...].astype(BF16), b_ref[...].astype(BF16), (dn, ((), ())),
                              preferred_element_type=F32)
        if res is not None:
            acc = acc + refs[2][...]
        o_ref[...] = acc.astype(out_dtype)

    ins = [a, b] + ([res] if res is not None else [])
    specs = [a_spec, b_spec] + ([o_spec] if res is not None else [])
    return pl.pallas_call(body, name=name, grid=(m // tm, n // tn), in_specs=specs, out_specs=o_spec,
                          out_shape=jax.ShapeDtypeStruct((m, n), out_dtype),
                          compiler_params=_cparams(("arbitrary", "arbitrary")))(*ins)


def _rms_fwd(name, h, w):
    n = h.shape[1]

    def body(h_ref, w_ref, o_ref):
        o_ref[...] = _rms(h_ref[...], w_ref[...], n).astype(BF16)

    return _rows_call(name, body, [h], [w], [(n, BF16)], [], _row_tile(h.shape[0]))[0]


def _rms_bwd(name, h, w, cts, resid):
    n = h.shape[1]
    nct = len(cts)

    def body(*refs):
        h_ref, ct_refs, r_ref, w_ref, dh_ref, dw_ref = refs[0], refs[1:1 + nct], refs[1 + nct], refs[2 + nct], refs[-2], refs[-1]
        ct = ct_refs[0][...].astype(F32)
        for c in ct_refs[1:]:
            ct = ct + c[...].astype(F32)
        _, vjp = jax.vjp(lambda x, ww: _rms(x, ww, n), h_ref[...], w_ref[...])
        dh, dw = vjp(ct)
        dh_ref[...] = dh + r_ref[...]
        _accumulate(dw_ref, dw)

    return _rows_call(name, body, [h, *cts, resid], [w], [(n, F32)], [(1, n)], _row_tile(h.shape[0]))


def _mla_consts_from_refs(qa, wq, kva, wkv, qn, kn, perm):
    f = lambda r: r[...].astype(F32)
    return dict(
        qa_w=f(qa), kva_w=f(kva), perm=f(perm),
        wq_n=[wq[h * QK_PAD:h * QK_PAD + HEAD, :].astype(F32) for h in range(HEADS)],
        wq_r=[wq[h * QK_PAD + HEAD:(h + 1) * QK_PAD, :].astype(F32) for h in range(HEADS)],
        wk_n=[wkv[h * QK_PAD:h * QK_PAD + HEAD, :].astype(F32) for h in range(HEADS)],
        wv=[wkv[h * QK_PAD + HEAD:(h + 1) * QK_PAD, :].astype(F32) for h in range(HEADS)],
        qn_n=qn[:, 0:HEAD], qn_r=qn[:, HEAD:QK_PAD], kn_n=kn[:, 0:HEAD], kn_r=kn[:, HEAD:QK_PAD])


def _mla_prep_fwd(q_lat, kv_lat, k_pe, cos, sin, qa, wq, kva, wkv, qn, kn, perm):
    def body(ql, kvl, kp, c, s, qa_r, wq_r, kva_r, wkv_r, qn_r, kn_r, p_r, q_out, k_out, v_out):
        consts = _mla_consts_from_refs(qa_r, wq_r, kva_r, wkv_r, qn_r, kn_r, p_r)
        outs = _mla_prep_fn((ql[...], kvl[...], kp[...], c[...], s[...]), consts)
        for h in range(HEADS):
            q_n, q_r, k_n, k_r, v = outs[5 * h:5 * h + 5]
            q_out[:, h * QK_PAD:h * QK_PAD + HEAD] = q_n.astype(BF16)
            q_out[:, h * QK_PAD + HEAD:(h + 1) * QK_PAD] = q_r.astype(BF16)
            k_out[:, h * QK_PAD:h * QK_PAD + HEAD] = k_n.astype(BF16)
            k_out[:, h * QK_PAD + HEAD:(h + 1) * QK_PAD] = k_r.astype(BF16)
            v_out[:, h * HEAD:(h + 1) * HEAD] = v.astype(BF16)

    return _rows_call("mla_prep_fwd", body, [q_lat, kv_lat, k_pe, cos, sin], [qa, wq, kva, wkv, qn, kn, perm],
                      [(HEADS * QK_PAD, BF16), (HEADS * QK_PAD, BF16), (DN_WIDTH, BF16)], [], _row_tile(cos.shape[0]))


def _mla_prep_bwd(q_lat, kv_lat, k_pe, cos, sin, dq, dk, dv, qa, wq, kva, wkv, qn, kn, perm):
    def body(ql, kvl, kp, c, s, dq_r, dk_r, dv_r, qa_r, wq_r, kva_r, wkv_r, qn_r, kn_r, p_r,
             dql, dkvl, dkp, dqa, dwq, dkva, dwkv, dqn, dkn):
        consts = _mla_consts_from_refs(qa_r, wq_r, kva_r, wkv_r, qn_r, kn_r, p_r)
        cc, ss, pm = c[...], s[...], consts.pop("perm")
        _, vjp = jax.vjp(lambda rows, cs: _mla_prep_fn((*rows, cc, ss), dict(cs, perm=pm)), (ql[...], kvl[...], kp[...]),
                         consts)
        cts = []
        for h in range(HEADS):
            cts += [dq_r[:, h * QK_PAD:h * QK_PAD + HEAD], dq_r[:, h * QK_PAD + HEAD:(h + 1) * QK_PAD],
                    dk_r[:, h * QK_PAD:h * QK_PAD + HEAD], dk_r[:, h * QK_PAD + HEAD:(h + 1) * QK_PAD],
                    dv_r[:, h * HEAD:(h + 1) * HEAD]]
        (d_ql, d_kvl, d_kp), dc = vjp(tuple(cts))
        dql[...] = d_ql.astype(BF16)
        dkvl[...] = d_kvl.astype(BF16)
        dkp[...] = d_kp.astype(BF16)
        first = pl.program_id(0) == 0

        def acc(ref, sl, val):
            @pl.when(first)
            def _():
                ref[sl] = val

            @pl.when(jnp.logical_not(first))
            def _():
                ref[sl] += val

        full = (slice(None), slice(None))
        acc(dqa, full, dc["qa_w"])
        acc(dkva, full, dc["kva_w"])
        for h in range(HEADS):
            acc(dwq, (slice(h * QK_PAD, h * QK_PAD + HEAD), slice(None)), dc["wq_n"][h])
            acc(dwq, (slice(h * QK_PAD + HEAD, (h + 1) * QK_PAD), slice(None)), dc["wq_r"][h])
            acc(dwkv, (slice(h * QK_PAD, h * QK_PAD + HEAD), slice(None)), dc["wk_n"][h])
            acc(dwkv, (slice(h * QK_PAD + HEAD, (h + 1) * QK_PAD), slice(None)), dc["wv"][h])
        acc(dqn, (slice(None), slice(0, HEAD)), dc["qn_n"])
        acc(dqn, (slice(None), slice(HEAD, QK_PAD)), dc["qn_r"])
        acc(dkn, (slice(None), slice(0, HEAD)), dc["kn_n"])
        acc(dkn, (slice(None), slice(HEAD, QK_PAD)), dc["kn_r"])

    return _rows_call("mla_prep_bwd", body, [q_lat, kv_lat, k_pe, cos, sin, dq, dk, dv],
                      [qa, wq, kva, wkv, qn, kn, perm],
                      [(LORA, BF16), (LORA, BF16), (HEAD, BF16)],
                      [(1, LORA), wq.shape, (1, LORA), wkv.shape, (1, QK_PAD), (1, QK_PAD)], _row_tile(cos.shape[0]))


def _attn_fwd(q, k, v):
    t = q.shape[0]
    rq = HEAD

    def body(q_ref, k_ref, v_ref, o_ref):
        o_ref[...] = _attn_fn(q_ref[...], k_ref[...], v_ref[...], pl.program_id(1) * rq)

    return pl.pallas_call(
        body, name="attn_fwd", grid=(HEADS, t // rq),
        in_specs=[pl.BlockSpec((rq, QK_PAD), lambda h, i: (i, h)), pl.BlockSpec((t, QK_PAD), lambda h, i: (0, h)),
                  pl.BlockSpec((t, HEAD), lambda h, i: (0, h))],
        out_specs=pl.BlockSpec((rq, HEAD), lambda h, i: (i, h)),
        out_shape=jax.ShapeDtypeStruct((t, HEADS * HEAD), F32),
        compiler_params=_cparams(("arbitrary", "arbitrary")))(q, k, v)


def _attn_bwd(q, k, v, do):
    t = q.shape[0]
    rq = HEAD

    def body(q_ref, k_ref, v_ref, do_ref, dq_ref, dk_ref, dv_ref):
        i = pl.program_id(1)
        _, vjp = jax.vjp(lambda a, b, c: _attn_fn(a, b, c, i * rq), q_ref[...].astype(F32), k_ref[...].astype(F32),
                         v_ref[...].astype(F32))
        dq, dk, dv = vjp(do_ref[...])
        dq_ref[...] = dq

        @pl.when(i == 0)
        def _():
            dk_ref[...] = dk
            dv_ref[...] = dv

        @pl.when(i != 0)
        def _():
            dk_ref[...] += dk
            dv_ref[...] += dv

    return pl.pallas_call(
        body, name="attn_bwd", grid=(HEADS, t // rq),
        in_specs=[pl.BlockSpec((rq, QK_PAD), lambda h, i: (i, h)), pl.BlockSpec((t, QK_PAD), lambda h, i: (0, h)),
                  pl.BlockSpec((t, HEAD), lambda h, i: (0, h)), pl.BlockSpec((rq, HEAD), lambda h, i: (i, h))],
        out_specs=[pl.BlockSpec((rq, QK_PAD), lambda h, i: (i, h)), pl.BlockSpec((t, QK_PAD), lambda h, i: (0, h)),
                   pl.BlockSpec((t, HEAD), lambda h, i: (0, h))],
        out_shape=[jax.ShapeDtypeStruct((t, HEADS * QK_PAD), F32), jax.ShapeDtypeStruct((t, HEADS * QK_PAD), F32),
                   jax.ShapeDtypeStruct((t, HEADS * HEAD), F32)],
        compiler_params=_cparams(("arbitrary", "arbitrary")))(q, k, v, do)


def _mix_out_fwd(o_mla, o_dn, z, w_mla, w_dn):
    def body(om_ref, od_ref, z_ref, wm_ref, wd_ref, o_ref):
        for h in range(HEADS):
            sl = slice(h * HEAD, (h + 1) * HEAD)
            o_ref[:, sl] = _rms(om_ref[:, sl], wm_ref[...], HEAD).astype(BF16)
            o_ref[:, DN_WIDTH + h * HEAD:DN_WIDTH + (h + 1) * HEAD] = _dn_out_fn(od_ref[:, sl], z_ref[:, sl],
                                                                                 wd_ref[...]).astype(BF16)

    return _rows_call("mix_out_fwd", body, [o_mla, o_dn, z], [w_mla, w_dn], [(2 * DN_WIDTH, BF16)], [],
                      _row_tile(o_mla.shape[0]))[0]


def _mix_out_bwd(o_mla, o_dn, z, dmixed, w_mla, w_dn):
    def body(om_ref, od_ref, z_ref, dm_ref, wm_ref, wd_ref, dom_ref, dod_ref, dz_ref, dwm_ref, dwd_ref):
        dwm = dwd = None
        for h in range(HEADS):
            sl = slice(h * HEAD, (h + 1) * HEAD)
            _, vjp = jax.vjp(lambda o, w: _rms(o, w, HEAD), om_ref[:, sl], wm_ref[...])
            do, dw = vjp(dm_ref[:, sl])
            dom_ref[:, sl] = do
            dwm = dw if dwm is None else dwm + dw
            _, vjp = jax.vjp(_dn_out_fn, od_ref[:, sl], z_ref[:, sl], wd_ref[...])
            do, dz, dw = vjp(dm_ref[:, DN_WIDTH + h * HEAD:DN_WIDTH + (h + 1) * HEAD])
            dod_ref[:, sl] = do
            dz_ref[:, sl] = dz.astype(BF16)
            dwd = dw if dwd is None else dwd + dw
        _accumulate(dwm_ref, dwm)
        _accumulate(dwd_ref, dwd)

    return _rows_call("mix_out_bwd", body, [o_mla, o_dn, z, dmixed], [w_mla, w_dn],
                      [(DN_WIDTH, F32), (DN_WIDTH, F32), (DN_WIDTH, BF16)], [(1, HEAD), (1, HEAD)],
                      _row_tile(o_mla.shape[0]))


def _shift_down(x, s):
    if s == 0:
        return x
    rows = lax.broadcasted_iota(jnp.int32, x.shape, 0)
    return jnp.where(rows >= s, pltpu.roll(x, s, 0), 0.0)


def _shift_up(x, s):
    if s == 0:
        return x
    t = x.shape[0]
    rows = lax.broadcasted_iota(jnp.int32, x.shape, 0)
    return jnp.where(rows < t - s, pltpu.roll(x, t - s, 0), 0.0)


def _col_call(name, body, cols, taps, outs, tap_outs, cw):
    t, c = cols[0].shape[0], taps[0].shape[1]
    in_specs = [pl.BlockSpec((t, cw), lambda j: (0, j)) for _ in cols]
    in_specs += [pl.BlockSpec((a.shape[0], cw), lambda j: (0, j)) for a in taps]
    out_shape = [jax.ShapeDtypeStruct((t, c), dt) for dt in outs] + [jax.ShapeDtypeStruct((n, c), F32) for n in tap_outs]
    out_specs = [pl.BlockSpec((t, cw), lambda j: (0, j)) for _ in outs]
    out_specs += [pl.BlockSpec((n, cw), lambda j: (0, j)) for n in tap_outs]
    return pl.pallas_call(body, name=name, grid=(c // cw,), in_specs=in_specs, out_specs=out_specs, out_shape=out_shape,
                          compiler_params=_cparams(("arbitrary",)))(*cols, *taps)


def _causal_conv(x, w_ref, width):
    acc = w_ref[width - 1:width, :] * x
    for j in range(width - 1):
        acc = acc + w_ref[j:j + 1, :] * _shift_down(x, width - 1 - j)
    return acc


def _causal_conv_bwd(x, dpre, w_ref, dx_ref, dw_ref, width):
    dx = w_ref[width - 1:width, :] * dpre
    dw_ref[width - 1:width, :] = jnp.sum(dpre * x, axis=0, keepdims=True)
    for j in range(width - 1):
        s = width - 1 - j
        dx = dx + w_ref[j:j + 1, :] * _shift_up(dpre, s)
        dw_ref[j:j + 1, :] = jnp.sum(dpre * _shift_down(x, s), axis=0, keepdims=True)
    dx_ref[...] = dx.astype(dx_ref.dtype)


def _dsilu(x):
    sg = jax.nn.sigmoid(x)
    return sg * (1.0 + x * (1.0 - sg))


def _dn_conv_fwd(x, w):
    def body(x_ref, w_ref, y_ref):
        y_ref[...] = _silu(_causal_conv(x_ref[...], w_ref, 4))

    return _col_call("dn_conv_fwd", body, [x], [w], [F32], [], 256)[0]


def _dn_conv_bwd(x, w, dy):
    def body(x_ref, dy_ref, w_ref, dx_ref, dw_ref):
        xv = x_ref[...]
        dpre = dy_ref[...] * _dsilu(_causal_conv(xv, w_ref, 4))
        _causal_conv_bwd(xv, dpre, w_ref, dx_ref, dw_ref, 4)

    return _col_call("dn_conv_bwd", body, [x, dy], [w], [BF16], [4], 256)


def _glu_fwd(gpre, up, w, b):
    def body(g_ref, u_ref, w_ref, b_ref, a_ref):
        gate = _causal_conv(g_ref[...], w_ref, 3) + b_ref[...]
        a_ref[...] = (_silu(gate) * u_ref[...]).astype(BF16)

    return _col_call("glu_fwd", body, [gpre, up], [w, b], [BF16], [], 256)[0]


def _glu_bwd(gpre, up, w, b, dact):
    def body(g_ref, u_ref, da_ref, w_ref, b_ref, dg_ref, du_ref, dw_ref, db_ref):
        gv = g_ref[...]
        gate = _causal_conv(gv, w_ref, 3) + b_ref[...]
        da = da_ref[...]
        du_ref[...] = (da * _silu(gate)).astype(BF16)
        dgate = da * u_ref[...] * _dsilu(gate)
        db_ref[...] = jnp.sum(dgate, axis=0, keepdims=True)
        _causal_conv_bwd(gv, dgate, w_ref, dg_ref, dw_ref, 3)

    return _col_call("glu_bwd", body, [gpre, up, dact], [w, b], [BF16, BF16], [3, 1], 256)


def _dn_prep_consts(sa, sb, al, dt):
    return dict(sel_a=sa[...], sel_b=sb[...], alog=al[...], dtb=dt[...])


def _dn_prep_fwd(conv, ab, sel_a, sel_b, alog, dtb):
    def body(c_ref, ab_ref, sa, sb, al, dt, q_out, k_out, g_out, b_out):
        qc = tuple(c_ref[:, h * HEAD:(h + 1) * HEAD] for h in range(HEADS))
        kc = tuple(c_ref[:, DN_WIDTH + h * HEAD:DN_WIDTH + (h + 1) * HEAD] for h in range(HEADS))
        qs, ks, g, beta = _dn_prep_fn((qc, kc, ab_ref[...]), _dn_prep_consts(sa, sb, al, dt))
        for h in range(HEADS):
            q_out[:, h * HEAD:(h + 1) * HEAD] = qs[h]
            k_out[:, h * HEAD:(h + 1) * HEAD] = ks[h]
        g_out[...] = g
        b_out[...] = beta

    return _rows_call("dn_prep_fwd", body, [conv, ab], [sel_a, sel_b, alog, dtb], [(DN_WIDTH, F32)] * 4, [],
                      _row_tile(conv.shape[0]))


def _dn_prep_bwd(conv, ab, dq, dk, dv, dg, db, sel_a, sel_b, alog, dtb):
    def body(c_ref, ab_ref, dq_r, dk_r, dv_r, dg_r, db_r, sa, sb, al, dt, dc_out, dab_out, dal_out, ddt_out):
        qc = tuple(c_ref[:, h * HEAD:(h + 1) * HEAD] for h in range(HEADS))
        kc = tuple(c_ref[:, DN_WIDTH + h * HEAD:DN_WIDTH + (h + 1) * HEAD] for h in range(HEADS))
        consts = _dn_prep_consts(sa, sb, al, dt)
        sel = dict(sel_a=consts["sel_a"], sel_b=consts["sel_b"])
        _, vjp = jax.vjp(lambda rows, ad: _dn_prep_fn(rows, {**sel, **ad}), (qc, kc, ab_ref[...]),
                         dict(alog=consts["alog"], dtb=consts["dtb"]))
        cq = tuple(dq_r[:, h * HEAD:(h + 1) * HEAD] for h in range(HEADS))
        ck = tuple(dk_r[:, h * HEAD:(h + 1) * HEAD] for h in range(HEADS))
        (dqc, dkc, dab), dad = vjp((cq, ck, dg_r[...], db_r[...]))
        for h in range(HEADS):
            dc_out[:, h * HEAD:(h + 1) * HEAD] = dqc[h]
            dc_out[:, DN_WIDTH + h * HEAD:DN_WIDTH + (h + 1) * HEAD] = dkc[h]
        dc_out[:, 2 * DN_WIDTH:3 * DN_WIDTH] = dv_r[...]
        dab_out[...] = dab.astype(BF16)
        _accumulate(dal_out, dad["alog"])
        _accumulate(ddt_out, dad["dtb"])

    return _rows_call("dn_prep_bwd", body, [conv, ab, dq, dk, dv, dg, db], [sel_a, sel_b, alog, dtb],
                      [(3 * DN_WIDTH, F32), (HEAD, BF16)], [(1, DN_WIDTH), (1, DN_WIDTH)], _row_tile(conv.shape[0]))


def _chunk_batch(t):
    nc = t // CHUNK
    return nc // 2 if nc % 2 == 0 else nc


def _dn_chunk_specs(t, nb):
    rows = nb * CHUNK
    hb = lambda h, b: (b, h)
    vb = lambda h, b: (b, 2 * HEADS + h)
    qk_spec = pl.BlockSpec((None, rows, CHUNK), lambda h, b: (h, b, 0))
    blk = pl.BlockSpec((rows, HEAD), hb)
    return rows, blk, pl.BlockSpec((rows, HEAD), vb), qk_spec


def _dn_chunk_fwd(qn, kn, conv, g, beta):
    t = qn.shape[0]
    nb = _chunk_batch(t)
    rows, blk, vblk, qk_spec = _dn_chunk_specs(t, nb)

    def body(q_ref, k_ref, v_ref, g_ref, b_ref, u_o, w_o, qg_o, kd_o, eg_o, qk_o):
        r3 = lambda x: x.reshape(nb, CHUNK, x.shape[-1])
        outs = _dn_chunk_fn(r3(q_ref[...]), r3(k_ref[...]), r3(v_ref[...]), r3(g_ref[...]), r3(g_ref[:, 0:CHUNK]),
                            r3(b_ref[...]))
        for o_ref, val in zip((u_o, w_o, qg_o, kd_o, eg_o, qk_o), outs):
            o_ref[...] = val.reshape(rows, val.shape[-1])

    return pl.pallas_call(
        body, name="dn_chunk_fwd", grid=(HEADS, t // rows), in_specs=[blk, blk, vblk, blk, blk],
        out_specs=[blk] * 5 + [qk_spec],
        out_shape=[jax.ShapeDtypeStruct((t, DN_WIDTH), F32)] * 5 + [jax.ShapeDtypeStruct((HEADS, t, CHUNK), F32)],
        compiler_params=_cparams(("arbitrary", "arbitrary")))(qn, kn, conv, g, beta)


def _dn_chunk_bwd(qn, kn, conv, g, beta, cts):
    t = qn.shape[0]
    nb = _chunk_batch(t)
    rows, blk, vblk, qk_spec = _dn_chunk_specs(t, nb)

    def body(q_ref, k_ref, v_ref, g_ref, b_ref, du, dw, dqg, dkd, deg, dqk, dq_o, dk_o, dv_o, dg_o, db_o):
        r3 = lambda x: x.reshape(nb, CHUNK, x.shape[-1])
        _, vjp = jax.vjp(_dn_chunk_fn, r3(q_ref[...]), r3(k_ref[...]), r3(v_ref[...]), r3(g_ref[...]),
                         r3(g_ref[:, 0:CHUNK]), r3(b_ref[...]))
        dq, dk, dv, dg, dg64, db = vjp(tuple(r3(c[...]) for c in (du, dw, dqg, dkd, deg, dqk)))
        for o_ref, val in zip((dq_o, dk_o, dv_o, dg_o, db_o), (dq, dk, dv, dg, db)):
            o_ref[...] = val.reshape(rows, HEAD)
        dg_o[:, 0:CHUNK] += dg64.reshape(rows, CHUNK)

    return pl.pallas_call(
        body, name="dn_chunk_bwd", grid=(HEADS, t // rows), in_specs=[blk, blk, vblk, blk, blk] + [blk] * 5 + [qk_spec],
        out_specs=[blk] * 5, out_shape=[jax.ShapeDtypeStruct((t, DN_WIDTH), F32)] * 5,
        compiler_params=_cparams(("arbitrary", "arbitrary")))(qn, kn, conv, g, beta, *cts)


def _dn_rec_fwd(u, w, qg, kd, eg, qk):
    t = u.shape[0]
    nc = t // CHUNK
    blk = pl.BlockSpec((CHUNK, DN_WIDTH), lambda c: (c, 0))
    qk_spec = pl.BlockSpec((HEADS, CHUNK, CHUNK), lambda c: (0, c, 0))
    s_spec = pl.BlockSpec((None, DN_WIDTH, HEAD), lambda c: (c, 0, 0))

    def body(u_ref, w_ref, qg_ref, kd_ref, eg_ref, qk_ref, o_ref, sall_ref, s_scr):
        @pl.when(pl.program_id(0) == 0)
        def _():
            s_scr[...] = jnp.zeros(s_scr.shape, F32)

        sall_ref[...] = s_scr[...]
        for h in range(HEADS):
            sl = slice(h * HEAD, (h + 1) * HEAD)
            s_new, o = _dn_rec_fn(s_scr[sl, :], u_ref[:, sl], w_ref[:, sl], qg_ref[:, sl], qk_ref[h], kd_ref[:, sl],
                                  eg_ref[0:1, sl])
            o_ref[:, sl] = o
            s_scr[sl, :] = s_new

    return pl.pallas_call(
        body, name="dn_rec_fwd", grid=(nc,), in_specs=[blk] * 5 + [qk_spec], out_specs=[blk, s_spec],
        out_shape=[jax.ShapeDtypeStruct((t, DN_WIDTH), F32), jax.ShapeDtypeStruct((nc, DN_WIDTH, HEAD), F32)],
        scratch_shapes=[pltpu.VMEM((DN_WIDTH, HEAD), F32)],
        compiler_params=_cparams(("arbitrary",)))(u, w, qg, kd, eg, qk)


def _dn_rec_bwd(u, w, qg, kd, eg, qk, sall, do):
    t = u.shape[0]
    nc = t // CHUNK
    blk = pl.BlockSpec((CHUNK, DN_WIDTH), lambda c: (nc - 1 - c, 0))
    qk_spec = pl.BlockSpec((HEADS, CHUNK, CHUNK), lambda c: (0, nc - 1 - c, 0))
    s_spec = pl.BlockSpec((None, DN_WIDTH, HEAD), lambda c: (nc - 1 - c, 0, 0))

    def body(u_ref, w_ref, qg_ref, kd_ref, eg_ref, qk_ref, s_ref, do_ref, du_o, dw_o, dqg_o, dkd_o, deg_o, dqk_o, ds_scr):
        @pl.when(pl.program_id(0) == 0)
        def _():
            ds_scr[...] = jnp.zeros(ds_scr.shape, F32)

        deg_o[...] = jnp.zeros(deg_o.shape, F32)
        for h in range(HEADS):
            sl = slice(h * HEAD, (h + 1) * HEAD)
            _, vjp = jax.vjp(_dn_rec_fn, s_ref[sl, :], u_ref[:, sl], w_ref[:, sl], qg_ref[:, sl], qk_ref[h],
                             kd_ref[:, sl], eg_ref[0:1, sl])
            ds, du, dw, dqg, dqk, dkd, deg = vjp((ds_scr[sl, :], do_ref[:, sl]))
            du_o[:, sl] = du
            dw_o[:, sl] = dw
            dqg_o[:, sl] = dqg
            dkd_o[:, sl] = dkd
            deg_o[0:1, sl] = deg
            dqk_o[h] = dqk
            ds_scr[sl, :] = ds

    return pl.pallas_call(
        body, name="dn_rec_bwd", grid=(nc,), in_specs=[blk] * 5 + [qk_spec, s_spec, blk],
        out_specs=[blk] * 5 + [qk_spec],
        out_shape=[jax.ShapeDtypeStruct((t, DN_WIDTH), F32)] * 5 + [jax.ShapeDtypeStruct((HEADS, t, CHUNK), F32)],
        scratch_shapes=[pltpu.VMEM((DN_WIDTH, HEAD), F32)],
        compiler_params=_cparams(("arbitrary",)))(u, w, qg, kd, eg, qk, sall, do)


def _loss_call(h2, tgt, n_valid):
    t, n = h2.shape
    r = _row_tile(t)

    def body(h_ref, t_ref, dy_ref, acc_ref):
        rows = pl.program_id(0) * r + lax.broadcasted_iota(jnp.int32, (r, n), 0)
        valid = jnp.logical_and(rows >= N_META, rows < n_valid)
        e = jnp.where(valid, h_ref[...] - t_ref[...], 0.0)
        dy_ref[...] = e * (1.0 / n)
        _accumulate(acc_ref, jnp.sum(e * e, axis=0, keepdims=True))

    return _rows_call("loss", body, [h2, tgt], [], [(n, F32)], [(1, n)], r)


def _adamw_call(name, w, g, m, v):
    rows, cols = w.shape
    tr = _tile(rows, 256, 8)

    def body(w_ref, g_ref, m_ref, v_ref, d_ref, m_out, v_out):
        gv = g_ref[...]
        m2 = ADAM_B1 * m_ref[...] + (1.0 - ADAM_B1) * gv
        v2 = ADAM_B2 * v_ref[...] + (1.0 - ADAM_B2) * (gv * gv)
        m_hat = m2 / (1.0 - ADAM_B1 ** ADAM_STEP)
        v_hat = v2 / (1.0 - ADAM_B2 ** ADAM_STEP)
        d_ref[...] = -ADAM_LR * (m_hat / (jnp.sqrt(v_hat) + ADAM_EPS) + ADAM_WD * w_ref[...])
        m_out[...] = m2
        v_out[...] = v2

    spec = pl.BlockSpec((tr, cols), lambda i: (i, 0))
    return pl.pallas_call(body, name=name, grid=(rows // tr,), in_specs=[spec] * 4, out_specs=[spec] * 3,
                          out_shape=[jax.ShapeDtypeStruct((rows, cols), F32)] * 3,
                          compiler_params=_cparams(("arbitrary",)))(w, g, m, v)


def _rope_tables(t):
    half = ROPE // 2
    inv_freq = ROPE_THETA ** (-jnp.arange(half, dtype=F32) / half)
    ang = jnp.arange(t, dtype=F32)[:, None] * inv_freq[None, :]
    z = jnp.zeros((t, HEAD - ROPE), F32)
    cos = jnp.concatenate([jnp.cos(ang), jnp.cos(ang), z], axis=1)
    sin = jnp.concatenate([jnp.sin(ang), jnp.sin(ang), z], axis=1)
    k = jnp.arange(HEAD)[:, None]
    l = jnp.arange(HEAD)[None, :]
    perm = jnp.where((l < half) & (k == l + half), -1.0, 0.0) + jnp.where((l >= half) & (l < ROPE) & (k == l - half), 1.0, 0.0)
    return cos, sin, perm.astype(F32)


def _win_to_pad(w):
    z = lambda n: jnp.zeros((n, w.shape[1]), w.dtype)
    return jnp.concatenate([w[576:2112], w[2112:2624], w[0:256], w[256:512], w[512:576], z(64), w[2624:2632], z(120)],
                           axis=0)


def _win_from_pad(g):
    return jnp.concatenate([g[2048:2304], g[2304:2560], g[2560:2624], g[0:1536], g[1536:2048], g[2688:2696]], axis=0)


def _qk_to_pad(w):
    w4 = w.reshape(HEADS, QK_DIM, w.shape[-1])
    return jnp.concatenate([w4, jnp.zeros((HEADS, QK_PAD - QK_DIM, w.shape[-1]), w.dtype)], axis=1).reshape(
        HEADS * QK_PAD, w.shape[-1])


def _qk_from_pad(g):
    return g.reshape(HEADS, QK_PAD, g.shape[-1])[:, :QK_DIM].reshape(HEADS * QK_DIM, g.shape[-1])


def _ff_to_pad(a, axis):
    shape = list(a.shape)
    shape[axis:axis + 1] = [N_CHIPS, FF_SHARD]
    a4 = a.reshape(shape)
    shape[axis + 1] = FF_BLOCK - FF_SHARD
    out = jnp.concatenate([a4, jnp.zeros(shape, a.dtype)], axis=axis + 1)
    shape[axis:axis + 2] = [D_FF_P]
    return out.reshape(shape)


def _ff_from_pad(a, axis):
    shape = list(a.shape)
    shape[axis:axis + 1] = [N_CHIPS, FF_BLOCK]
    a4 = lax.slice_in_dim(a.reshape(shape), 0, FF_SHARD, axis=axis + 1)
    shape[axis:axis + 2] = [D_FF]
    return a4.reshape(shape)


def _local_step(x, tgt, wt):
    s = x.shape[0]
    n_valid = N_META + s
    t = -(-n_valid // HEAD) * HEAD
    zpad = jnp.zeros((t - n_valid, D_MODEL), F32)
    h0 = jnp.concatenate([wt["meta_tokens"], x, zpad], axis=0)
    tgt_p = jnp.concatenate([jnp.zeros((N_META, D_MODEL), F32), tgt, zpad], axis=0)
    cos, sin, perm = _rope_tables(t)
    win, wq, wkv = wt["w_in_t"], wt["w_q_t"], wt["w_kv_t"]
    qn_w = jnp.concatenate([wt["q_norm_w"], jnp.zeros((1, QK_PAD - QK_DIM), F32)], axis=1)
    kn_w = jnp.concatenate([wt["k_norm_w"], jnp.zeros((1, QK_PAD - QK_DIM), F32)], axis=1)
    head_id = jnp.arange(DN_WIDTH)[None, :] // HEAD
    lane = jnp.arange(HEAD)[:, None]
    sel_a = (lane == head_id).astype(F32)
    sel_b = (lane == head_id + HEADS).astype(F32)
    alog = jnp.repeat(wt["dn_A_log"], HEAD, axis=1)
    dtb = jnp.repeat(wt["dn_dt_bias"], HEAD, axis=1)
    w_out, w_gate, w_up, w_down = wt["w_out"], wt["w_gate_t"], wt["w_up_t"], wt["w_down"]
    conv_w, conv_b = wt["ffn_conv_w"], wt["ffn_conv_b"]

    u = _rms_fwd("attn_norm_fwd", h0, wt["attn_norm_w"])
    proj = _matmul("in_proj", u, win, "nt", F32)
    z = (proj, DN_WIDTH, 3)
    q_lat, kv_lat, k_pe, ab = (proj, LORA, 8), (proj, LORA, 9), (proj, HEAD, 20), (proj, HEAD, 21)
    mla_consts = (wt["q_a_norm_w"], wq, wt["kv_a_norm_w"], wkv, qn_w, kn_w, perm)
    q, k, v = _mla_prep_fwd(q_lat, kv_lat, k_pe, cos, sin, *mla_consts)
    o_mla = _attn_fwd(q, k, v)
    conv = _dn_conv_fwd(proj, wt["dn_conv_w"])
    dn_consts = (sel_a, sel_b, alog, dtb)
    qn, kn, g, beta = _dn_prep_fwd(conv, ab, *dn_consts)
    cu, cw, cqg, ckd, ceg, cqk = _dn_chunk_fwd(qn, kn, conv, g, beta)
    o_dn, sall = _dn_rec_fwd(cu, cw, cqg, ckd, ceg, cqk)
    mixed = _mix_out_fwd(o_mla, o_dn, z, wt["mla_out_norm_w"], wt["dn_out_norm_w"])
    h1 = _matmul("out_proj", mixed, w_out, "nn", F32, res=h0)
    n2 = _rms_fwd("ffn_norm_fwd", h1, wt["ffn_norm_w"])
    gpre = _matmul("gate_proj", n2, w_gate, "nt", F32)
    up = _matmul("up_proj", n2, w_up, "nt", F32)
    act = _glu_fwd(gpre, up, conv_w, conv_b)
    h2 = _matmul("down_proj", act, w_down, "nn", F32, res=h1)
    dy, sq = _loss_call(h2, tgt_p, n_valid)

    grads = {}
    dy16 = dy.astype(BF16)
    dact = _matmul("down_dx", dy16, w_down, "nt", F32)
    grads["w_down"] = _matmul("down_dw", act, dy16, "tn", F32)
    dgpre, dup, grads["ffn_conv_w"], grads["ffn_conv_b"] = _glu_bwd(gpre, up, conv_w, conv_b, dact)
    grads["w_gate_t"] = _matmul("gate_dw", dgpre, n2, "tn", F32)
    grads["w_up_t"] = _matmul("up_dw", dup, n2, "tn", F32)
    dn2a = _matmul("gate_dx", dgpre, w_gate, "nn", F32)
    dn2b = _matmul("up_dx", dup, w_up, "nn", F32)
    dh1, grads["ffn_norm_w"] = _rms_bwd("ffn_norm_bwd", h1, wt["ffn_norm_w"], [dn2a, dn2b], dy)
    dh1_16 = dh1.astype(BF16)
    dmixed = _matmul("out_dx", dh1_16, w_out, "nt", F32)
    grads["w_out"] = _matmul("out_dw", mixed, dh1_16, "tn", F32)
    do_mla, do_dn, dz, grads["mla_out_norm_w"], grads["dn_out_norm_w"] = _mix_out_bwd(
        o_mla, o_dn, z, dmixed, wt["mla_out_norm_w"], wt["dn_out_norm_w"])
    rec_cts = _dn_rec_bwd(cu, cw, cqg, ckd, ceg, cqk, sall, do_dn)
    dqn, dkn, dv_dn, dg, dbeta = _dn_chunk_bwd(qn, kn, conv, g, beta, rec_cts)
    dconv, dab, dalog, ddtb = _dn_prep_bwd(conv, ab, dqn, dkn, dv_dn, dg, dbeta, *dn_consts)
    grads["dn_A_log"] = jnp.sum(dalog.reshape(HEADS, HEAD), axis=1)[None, :]
    grads["dn_dt_bias"] = jnp.sum(ddtb.reshape(HEADS, HEAD), axis=1)[None, :]
    ddn_pre, grads["dn_conv_w"] = _dn_conv_bwd(proj, wt["dn_conv_w"], dconv)
    dq, dk, dv = _attn_bwd(q, k, v, do_mla)
    dq_lat, dkv_lat, dk_pe, dqa, dwq, dkva, dwkv, dqnw, dknw = _mla_prep_bwd(q_lat, kv_lat, k_pe, cos, sin, dq, dk, dv,
                                                                               *mla_consts)
    grads["q_a_norm_w"], grads["kv_a_norm_w"] = dqa, dkva
    grads["w_q_t"], grads["w_kv_t"] = dwq, dwkv
    grads["q_norm_w"], grads["k_norm_w"] = dqnw[:, :QK_DIM], dknw[:, :QK_DIM]
    dproj = jnp.concatenate([ddn_pre, dz, dq_lat, dkv_lat, dk_pe, dab], axis=1)
    grads["w_in_t"] = _matmul("in_dw", dproj, u, "tn", F32)
    du = _matmul("in_dx", dproj, win, "nn", F32)
    dh0, grads["attn_norm_w"] = _rms_bwd("attn_norm_bwd", h0, wt["attn_norm_w"], [du], dh1)
    grads["meta_tokens"] = dh0[0:N_META]
    return sq, dh0[N_META:n_valid], grads


def _mesh_pos():
    return lax.axis_index("x"), lax.axis_index("y"), lax.axis_index("c")


def _other_chips(x, y):
    return [(1 - x, y), (x, 1 - y), (1 - x, 1 - y)]


def _all_gather(shards):
    any_spec = pl.BlockSpec(memory_space=pl.ANY)
    nbuf = len(shards)

    def body(*refs):
        srcs, dsts, (send_sems, recv_sems, local_sems) = refs[:nbuf], refs[nbuf:2 * nbuf], refs[2 * nbuf:]
        x, y, c = _mesh_pos()
        p = 2 * x + y
        sibling = (x, y, 1 - c)
        chips = _other_chips(x, y)
        bufs = tuple((s, d, s.shape[0] // 2) for s, d in zip(srcs, dsts))

        def half(ref, rows, which):
            return ref.at[pl.ds(which * rows, rows), :]

        local = [pltpu.make_async_copy(src, dst.at[p], local_sems.at[i]) for i, (src, dst, _) in enumerate(bufs)]
        for cp in local:
            cp.start()

        def copy(i, k, src, dst, to):
            return pltpu.make_async_remote_copy(src_ref=src, dst_ref=dst, send_sem=send_sems.at[i, k],
                                                recv_sem=recv_sems.at[i, k], device_id=to, device_id_type=MESH)

        sends = []
        for i, (src, dst, rows) in enumerate(bufs):
            for j, chip in enumerate(chips):
                sends.append(copy(i, j, half(src, rows, c), half(dst.at[p], rows, c), (*chip, c)))
        for cp in sends:
            cp.start()
        passed = []
        for i, (src, dst, rows) in enumerate(bufs):
            for j, (qx, qy) in enumerate(chips):
                block = half(dst.at[2 * qx + qy], rows, c)
                copy(i, j, block, block, (x, y, c)).wait_recv()
                fwd = copy(i, 3 + j, block, block, sibling)
                fwd.start()
                passed.append(fwd)
        for i, (src, dst, rows) in enumerate(bufs):
            for j, (qx, qy) in enumerate(chips):
                block = half(dst.at[2 * qx + qy], rows, 1 - c)
                copy(i, 3 + j, block, block, (x, y, c)).wait_recv()
        for cp in sends + passed:
            cp.wait_send()
        for cp in local:
            cp.wait()

    return pl.pallas_call(
        body, name="all_gather_weights", in_specs=[any_spec] * nbuf, out_specs=[any_spec] * nbuf,
        out_shape=[jax.ShapeDtypeStruct((N_CHIPS, *s.shape), s.dtype) for s in shards],
        scratch_shapes=[pltpu.SemaphoreType.DMA((nbuf, 6)), pltpu.SemaphoreType.DMA((nbuf, 6)),
                        pltpu.SemaphoreType.DMA((nbuf,))],
        compiler_params=pltpu.CompilerParams(has_side_effects=True))(*shards)


def _rs_to_sibling(bufs):
    any_spec = pl.BlockSpec(memory_space=pl.ANY)
    nbuf = len(bufs)

    def body(*refs):
        srcs, dsts, (send_sems, recv_sems) = refs[:nbuf], refs[nbuf:2 * nbuf], refs[2 * nbuf:]
        x, y, c = _mesh_pos()
        copies = []
        for i, (src, dst) in enumerate(zip(srcs, dsts)):
            half = src.shape[1] // 2
            copies.append(pltpu.make_async_remote_copy(
                src_ref=src.at[:, pl.ds((1 - c) * half, half), :], dst_ref=dst, send_sem=send_sems.at[i],
                recv_sem=recv_sems.at[i], device_id=(x, y, 1 - c), device_id_type=MESH))
        for cp in copies:
            cp.start()
        for cp in copies:
            cp.wait()

    return pl.pallas_call(
        body, name="rs_sibling_exchange", in_specs=[any_spec] * nbuf, out_specs=[any_spec] * nbuf,
        out_shape=[jax.ShapeDtypeStruct((N_CHIPS, b.shape[1] // 2, b.shape[2]), F32) for b in bufs],
        scratch_shapes=[pltpu.SemaphoreType.DMA((nbuf,)), pltpu.SemaphoreType.DMA((nbuf,))],
        compiler_params=pltpu.CompilerParams(has_side_effects=True))(*bufs)


def _rs_pair_add(name, buf, got, c):
    half, cols = got.shape[1], got.shape[2]

    def body(c_ref, a_ref, b_ref, o_ref):
        o_ref[...] = a_ref[...] + b_ref[...]

    return pl.pallas_call(
        body, name=name,
        grid_spec=pltpu.PrefetchScalarGridSpec(
            num_scalar_prefetch=1, grid=(N_CHIPS,),
            in_specs=[pl.BlockSpec((None, half, cols), lambda j, cr: (j, cr[0], 0)),
                      pl.BlockSpec((None, half, cols), lambda j, cr: (j, 0, 0))],
            out_specs=pl.BlockSpec((None, half, cols), lambda j, cr: (j, 0, 0))),
        out_shape=jax.ShapeDtypeStruct(got.shape, F32),
        compiler_params=_cparams(("arbitrary",)))(c, buf, got)


def _rs_to_chips(accs):
    any_spec = pl.BlockSpec(memory_space=pl.ANY)
    nbuf = len(accs)

    def body(*refs):
        srcs, dsts, (send_sems, recv_sems) = refs[:nbuf], refs[nbuf:2 * nbuf], refs[2 * nbuf:]
        x, y, c = _mesh_pos()
        copies = []
        for i, (src, dst) in enumerate(zip(srcs, dsts)):
            for k, (qx, qy) in enumerate(_other_chips(x, y)):
                copies.append(pltpu.make_async_remote_copy(
                    src_ref=src.at[2 * qx + qy], dst_ref=dst.at[k], send_sem=send_sems.at[i, k],
                    recv_sem=recv_sems.at[i, k], device_id=(qx, qy, c), device_id_type=MESH))
        for cp in copies:
            cp.start()
        for cp in copies:
            cp.wait()

    return pl.pallas_call(
        body, name="rs_chip_exchange", in_specs=[any_spec] * nbuf, out_specs=[any_spec] * nbuf,
        out_shape=[jax.ShapeDtypeStruct((3, a.shape[1], a.shape[2]), F32) for a in accs],
        scratch_shapes=[pltpu.SemaphoreType.DMA((nbuf, 3)), pltpu.SemaphoreType.DMA((nbuf, 3))],
        compiler_params=pltpu.CompilerParams(has_side_effects=True))(*accs)


def _rs_chip_add(name, acc, got, p):
    half, cols = acc.shape[1], acc.shape[2]
    tr = _tile(half, 128, 8)
    slot = (0, 1, 0, 2)

    def body(p_ref, own_ref, g0_ref, g1_ref, g2_ref, o_ref):
        me = p_ref[0]
        gots = (g0_ref, g1_ref, g2_ref)
        total = None
        for chip in range(N_CHIPS):
            val = own_ref[...]
            for e in (1, 2, 3):
                val = jnp.where((chip ^ me) == e, gots[slot[e]][...], val)
            total = val if total is None else total + val
        o_ref[...] = total

    gspec = lambda k: pl.BlockSpec((None, tr, cols), lambda i, pr: (k, i, 0))
    return pl.pallas_call(
        body, name=name,
        grid_spec=pltpu.PrefetchScalarGridSpec(
            num_scalar_prefetch=1, grid=(half // tr,),
            in_specs=[pl.BlockSpec((None, tr, cols), lambda i, pr: (pr[0], i, 0)), gspec(0), gspec(1), gspec(2)],
            out_specs=pl.BlockSpec((tr, cols), lambda i, pr: (i, 0))),
        out_shape=jax.ShapeDtypeStruct((half, cols), F32),
        compiler_params=_cparams(("arbitrary",)))(p, acc, got, got, got)


def _rs_share(ress):
    any_spec = pl.BlockSpec(memory_space=pl.ANY)
    nbuf = len(ress)

    def body(*refs):
        srcs, dsts, (send_sems, recv_sems, local_sems) = refs[:nbuf], refs[nbuf:2 * nbuf], refs[2 * nbuf:]
        x, y, c = _mesh_pos()
        local, sends, recvs = [], [], []
        for i, (src, dst) in enumerate(zip(srcs, dsts)):
            half = src.shape[0]
            mine = dst.at[pl.ds(c * half, half), :]
            theirs = dst.at[pl.ds((1 - c) * half, half), :]
            local.append(pltpu.make_async_copy(src, mine, local_sems.at[i]))
            sends.append(pltpu.make_async_remote_copy(src_ref=src, dst_ref=mine, send_sem=send_sems.at[i],
                                                      recv_sem=recv_sems.at[i], device_id=(x, y, 1 - c),
                                                      device_id_type=MESH))
            recvs.append(pltpu.make_async_remote_copy(src_ref=src, dst_ref=theirs, send_sem=send_sems.at[i],
                                                      recv_sem=recv_sems.at[i], device_id=(x, y, 1 - c),
                                                      device_id_type=MESH))
        for cp in local + sends:
            cp.start()
        for cp in sends:
            cp.wait_send()
        for cp in recvs:
            cp.wait_recv()
        for cp in local:
            cp.wait()

    return pl.pallas_call(
        body, name="rs_sibling_share", in_specs=[any_spec] * nbuf, out_specs=[any_spec] * nbuf,
        out_shape=[jax.ShapeDtypeStruct((2 * r.shape[0], r.shape[1]), F32) for r in ress],
        scratch_shapes=[pltpu.SemaphoreType.DMA((nbuf,)), pltpu.SemaphoreType.DMA((nbuf,)), pltpu.SemaphoreType.DMA((nbuf,))],
        compiler_params=pltpu.CompilerParams(has_side_effects=True))(*ress)


def _reduce_scatter(names, bufs):
    x, y, c = _mesh_pos()
    ci = jnp.reshape(c, (1,)).astype(jnp.int32)
    pi = jnp.reshape(2 * x + y, (1,)).astype(jnp.int32)
    gots = _rs_to_sibling(bufs)
    accs = [_rs_pair_add("rs_pair_add_" + n, b, g, ci) for n, b, g in zip(names, bufs, gots)]
    gots2 = _rs_to_chips(accs)
    ress = [_rs_chip_add("rs_chip_add_" + n, a, g, pi) for n, a, g in zip(names, accs, gots2)]
    return _rs_share(ress)


def _pad_rows(flat, rows):
    return jnp.concatenate([flat, jnp.zeros((rows * LANES - flat.shape[0],), flat.dtype)]).reshape(rows, LANES)


def _unshard(g4, shape, axis):
    a = g4.reshape(N_CHIPS, *shape)
    if axis == 0:
        return a.reshape(N_CHIPS * shape[0], shape[1])
    return jnp.transpose(a, (1, 0, 2)).reshape(shape[0], N_CHIPS * shape[1])


def _shard4(full, shape, axis):
    if axis == 0:
        return full.reshape(N_CHIPS, shape[0] * shape[1])
    a = full.reshape(shape[0], N_CHIPS, shape[1])
    return jnp.transpose(a, (1, 0, 2)).reshape(N_CHIPS, shape[0] * shape[1])


def _pad_axis0(a, rows):
    return jnp.concatenate([a, jnp.zeros((rows - a.shape[0], *a.shape[1:]), a.dtype)], axis=0)


def _pad_axis1(a, rows):
    return jnp.concatenate([a, jnp.zeros((a.shape[0], rows - a.shape[1], *a.shape[2:]), a.dtype)], axis=1)


def _shard_to_strip(name, w):
    _, (shape, axis, rows) = name, {n: (s, ax, r) for n, s, ax, r in BIG}[name]
    w2 = w.reshape(shape).astype(BF16)
    return _pad_axis0(w2.T if axis == 1 else w2, rows)


def _strips_to_local(gathered):
    g = gathered
    win = g["w_in"][:, :IN_SHARD].reshape(IN_COLS, D_MODEL)
    return dict(
        w_in_t=_win_to_pad(win), w_q_t=_qk_to_pad(g["w_q_b"].reshape(HEADS * QK_DIM, LORA)),
        w_kv_t=g["w_kv_b"].reshape(HEADS * QK_PAD, LORA), w_out=g["w_out"].reshape(D_MODEL, D_MODEL),
        w_gate_t=g["w_gate"].reshape(D_FF_P, D_MODEL), w_up_t=g["w_up"].reshape(D_FF_P, D_MODEL),
        w_down=g["w_down"].reshape(D_FF_P, D_MODEL))


def _local_to_strips(g):
    win = _pad_axis1(_win_from_pad(g["w_in_t"]).reshape(N_CHIPS, IN_SHARD, D_MODEL), IN_SHARD_P)
    return dict(
        w_in=win, w_q_b=_qk_from_pad(g["w_q_t"]).reshape(N_CHIPS, QK_DIM, LORA),
        w_kv_b=g["w_kv_t"].reshape(N_CHIPS, QK_PAD, LORA), w_out=g["w_out"].reshape(N_CHIPS, LORA, D_MODEL),
        w_gate=g["w_gate_t"].reshape(N_CHIPS, FF_BLOCK, D_MODEL), w_up=g["w_up_t"].reshape(N_CHIPS, FF_BLOCK, D_MODEL),
        w_down=g["w_down"].reshape(N_CHIPS, FF_BLOCK, D_MODEL))


def _strip_to_shard(name, strip):
    shape, axis = {n: (s, ax) for n, s, ax, _ in BIG}[name]
    rows = shape[axis]
    return strip[:rows].T if axis == 1 else strip[:rows]


def kernel(x, meta_tokens, attn_norm_w, w_in, q_a_norm_w, w_q_b, kv_a_norm_w, w_kv_b, q_norm_w, k_norm_w, mla_out_norm_w, dn_conv_w, dn_A_log, dn_dt_bias, dn_out_norm_w, w_out, ffn_norm_w, w_gate, w_up, ffn_conv_w, ffn_conv_b, w_down, loss_target, m_meta_tokens, m_attn_norm_w, m_w_in, m_q_a_norm_w, m_w_q_b, m_kv_a_norm_w, m_w_kv_b, m_q_norm_w, m_k_norm_w, m_mla_out_norm_w, m_dn_conv_w, m_dn_A_log, m_dn_dt_bias, m_dn_out_norm_w, m_w_out, m_ffn_norm_w, m_w_gate, m_w_up, m_ffn_conv_w, m_ffn_conv_b, m_w_down, v_meta_tokens, v_attn_norm_w, v_w_in, v_q_a_norm_w, v_w_q_b, v_kv_a_norm_w, v_w_kv_b, v_q_norm_w, v_k_norm_w, v_mla_out_norm_w, v_dn_conv_w, v_dn_A_log, v_dn_dt_bias, v_dn_out_norm_w, v_w_out, v_ffn_norm_w, v_w_gate, v_w_up, v_ffn_conv_w, v_ffn_conv_b, v_w_down):
    local = dict(locals())
    w = {n: local[n] for n in WEIGHTS}
    m = {n: local["m_" + n] for n in WEIGHTS}
    v = {n: local["v_" + n] for n in WEIGHTS}
    big_names = [n for n, _, _, _ in BIG]

    wf = _pad_rows(jnp.concatenate([w[n].reshape(-1) for n, _, _ in SMALL_SHARDED]), SMALL_ROWS)
    gathered = _all_gather([_shard_to_strip(n, w[n]) for n in big_names] + [wf])
    full = _strips_to_local(dict(zip(big_names, gathered[:-1])))
    gf = gathered[-1].reshape(N_CHIPS, -1)
    off = 0
    for n, s, ax in SMALL_SHARDED:
        full[n] = _unshard(gf[:, off:off + s[0] * s[1]], s, ax)
        off += s[0] * s[1]
    for n, _ in REPLICATED:
        full[n] = w[n]
    full["ffn_conv_w"] = _ff_to_pad(full["ffn_conv_w"], 1)
    full["ffn_conv_b"] = _ff_to_pad(full["ffn_conv_b"], 1)

    sq, grad_x, g = _local_step(x[0], loss_target[0], full)
    loss = lax.psum(0.5 / D_MODEL * jnp.sum(sq), ("x", "y", "c"))
    g["ffn_conv_w"] = _ff_from_pad(g["ffn_conv_w"], 1)
    g["ffn_conv_b"] = _ff_from_pad(g["ffn_conv_b"], 1)

    rep = jnp.concatenate([g[n].reshape(-1) for n, _ in REPLICATED])
    rep = jnp.concatenate([rep, jnp.zeros((REP_ROWS * LANES - rep.shape[0],), F32)])
    small = jnp.concatenate([_shard4(g[n], s, ax) for n, s, ax in SMALL_SHARDED], axis=1)
    small = jnp.concatenate([small, jnp.zeros((N_CHIPS, SMALL_ROWS * LANES - small.shape[1]), F32),
                             jnp.broadcast_to(rep, (N_CHIPS, rep.shape[0]))], axis=1)
    strips = _local_to_strips(g)
    reduced = _reduce_scatter(big_names + ["small"],
                              [strips[n] for n in big_names] + [small.reshape(N_CHIPS, SMALL_ROWS + REP_ROWS, LANES)])
    gs = {n: _strip_to_shard(n, r) for n, r in zip(big_names, reduced[:-1])}
    red = reduced[-1].reshape(-1)
    off = 0
    for n, s, _ in SMALL_SHARDED:
        gs[n] = red[off:off + s[0] * s[1]].reshape(s)
        off += s[0] * s[1]
    off = SMALL_ROWS * LANES
    for n, cnt in REPLICATED:
        gs[n] = red[off:off + cnt].reshape(1, cnt)
        off += cnt

    delta, new_m, new_v = {}, {}, {}
    for n, s, _, _ in BIG:
        d2, m2, v2 = _adamw_call("adamw_" + n, w[n].reshape(s), gs[n], m[n].reshape(s), v[n].reshape(s))
        delta[n], new_m[n], new_v[n] = (a.reshape(w[n].shape) for a in (d2, m2, v2))
    small_names = [n for n, _, _ in SMALL_SHARDED] + [n for n, _ in REPLICATED]
    rows = SMALL_ROWS + REP_ROWS
    pack = lambda d: _pad_rows(jnp.concatenate([d[n].reshape(-1) for n in small_names]), rows)
    d2, m2, v2 = _adamw_call("adamw_small", pack(w), pack(gs), pack(m), pack(v))
    off = 0
    for n in small_names:
        cnt = w[n].size
        for dst, src in ((delta, d2), (new_m, m2), (new_v, v2)):
            dst[n] = src.reshape(-1)[off:off + cnt].reshape(w[n].shape)
        off += cnt

    grad_out = [gs[n].reshape(w[n].shape) for n in WEIGHTS]
    return (loss, grad_x[None], *grad_out, *[delta[n] for n in WEIGHTS], *[new_m[n] for n in WEIGHTS],
            *[new_v[n] for n in WEIGHTS])
```

```python
import functools
import math

import jax
import jax.numpy as jnp
from jax import lax
from jax.experimental import pallas as pl
from jax.experimental.pallas import tpu as pltpu

F32 = jnp.float32
BF16 = jnp.bfloat16
HI = lax.Precision.HIGHEST
MESH = pl.DeviceIdType.MESH

N_META = 16
D_MODEL = 1024
HEADS = 4
HEAD = 128
ROPE = 64
QK_DIM = HEAD + ROPE
QK_PAD = 2 * HEAD
LORA = 256
DN_WIDTH = HEADS * HEAD
CHUNK = 64
D_FF = 2816
N_CHIPS = 4
FF_SHARD = D_FF // N_CHIPS
FF_BLOCK = 768
D_FF_P = N_CHIPS * FF_BLOCK
IN_COLS = 2632
IN_SHARD = IN_COLS // N_CHIPS
IN_SHARD_P = 672
IN_PAD = 2816
NORM_EPS = 1e-6
ROPE_THETA = 10000.0
LANES = 512

ADAM_LR, ADAM_B1, ADAM_B2, ADAM_EPS, ADAM_WD, ADAM_STEP = 0.001, 0.9, 0.999, 1e-08, 0.01, 10

VMEM_LIMIT = 56 * 1024 * 1024

BIG = (("w_in", (1024, 658), 1, IN_SHARD_P), ("w_q_b", (256, 192), 1, 192), ("w_kv_b", (256, 256), 1, 256),
       ("w_out", (256, 1024), 0, 256), ("w_gate", (1024, 704), 1, FF_BLOCK), ("w_up", (1024, 704), 1, FF_BLOCK),
       ("w_down", (704, 1024), 0, FF_BLOCK))
SMALL_SHARDED = (("meta_tokens", (16, 256), 1), ("dn_conv_w", (4, 384), 1), ("ffn_conv_w", (3, 704), 1))
REPLICATED = (("attn_norm_w", 1024), ("q_a_norm_w", 256), ("kv_a_norm_w", 256), ("q_norm_w", 192), ("k_norm_w", 192),
              ("mla_out_norm_w", 128), ("dn_A_log", 4), ("dn_dt_bias", 4), ("dn_out_norm_w", 128), ("ffn_norm_w", 1024),
              ("ffn_conv_b", 2816))
WEIGHTS = ("meta_tokens", "attn_norm_w", "w_in", "q_a_norm_w", "w_q_b", "kv_a_norm_w", "w_kv_b", "q_norm_w", "k_norm_w",
           "mla_out_norm_w", "dn_conv_w", "dn_A_log", "dn_dt_bias", "dn_out_norm_w", "w_out", "ffn_norm_w", "w_gate",
           "w_up", "ffn_conv_w", "ffn_conv_b", "w_down")

SMALL_ROWS = 16
REP_ROWS = 16


def _cparams(sem):
    return pltpu.CompilerParams(dimension_semantics=sem, vmem_limit_bytes=VMEM_LIMIT)


NN, NT, TN = ((1,), (0,)), ((1,), (1,)), ((0,), (0,))


def _shift_dims(dims, batch):
    if not batch:
        return (dims, ((), ()))
    return (((dims[0][0] + 1,), (dims[1][0] + 1,)), ((0,), (0,)))


def _make_mm(dims, exact, batch=False):
    def raw(a, b, d):
        if exact:
            return lax.dot_general(a.astype(F32), b.astype(F32), _shift_dims(d, batch), precision=HI,
                                   preferred_element_type=F32)
        return lax.dot_general(a.astype(BF16), b.astype(BF16), _shift_dims(d, batch), preferred_element_type=F32)

    @jax.custom_vjp
    def mm(a, b):
        return raw(a, b, dims)

    def fwd(a, b):
        return raw(a, b, dims), (a, b)

    def bwd(res, g):
        a, b = res
        if dims == NN:
            da, db = raw(g, b, NT), raw(a, g, TN)
        elif dims == NT:
            da, db = raw(g, b, NN), raw(g, a, TN)
        else:
            da, db = raw(b, g, NT), raw(a, g, NN)
        return da.astype(a.dtype), db.astype(b.dtype)

    mm.defvjp(fwd, bwd)
    return mm


_mm = _make_mm(NN, False)
_mm_nt = _make_mm(NT, False)
_mm_tn = _make_mm(TN, False)
_mmx = _make_mm(NN, True)
_bmm = _make_mm(NN, False, batch=True)
_bmm_nt = _make_mm(NT, False, batch=True)
_bmmx = _make_mm(NN, True, batch=True)


def _rms(x, w, n):
    ms = jnp.sum(x * x, axis=-1, keepdims=True) * (1.0 / n)
    return x * lax.rsqrt(ms + NORM_EPS) * w


def _silu(x):
    return x * jax.nn.sigmoid(x)


def _softplus(x):
    return jnp.maximum(x, 0.0) + jnp.log(1.0 + jnp.exp(-jnp.abs(x)))


def _rope(x, cos, sin, perm):
    return x * cos + _mmx(x, perm) * sin


def _mla_prep_fn(rows, consts):
    q_lat, kv_lat, k_pe, cos, sin = rows
    qn = _rms(q_lat, consts["qa_w"], LORA)
    kvn = _rms(kv_lat, consts["kva_w"], LORA)
    outs = []
    for h in range(HEADS):
        q_n = _mm_nt(qn, consts["wq_n"][h])
        q_r = _mm_nt(qn, consts["wq_r"][h])
        rs = lax.rsqrt((jnp.sum(q_n * q_n, -1, keepdims=True) + jnp.sum(q_r * q_r, -1, keepdims=True)) * (1.0 / QK_DIM)
                       + NORM_EPS)
        q_n = q_n * rs * consts["qn_n"]
        q_r = _rope(q_r * rs * consts["qn_r"], cos, sin, consts["perm"])
        k_n = _mm_nt(kvn, consts["wk_n"][h])
        v = _mm_nt(kvn, consts["wv"][h])
        rk = lax.rsqrt((jnp.sum(k_n * k_n, -1, keepdims=True) + jnp.sum(k_pe * k_pe, -1, keepdims=True)) * (1.0 / QK_DIM)
                       + NORM_EPS)
        k_n = k_n * rk * consts["kn_n"]
        k_r = _rope(k_pe * rk * consts["kn_r"], cos, sin, consts["perm"])
        outs += [q_n, q_r, k_n, k_r, v]
    return tuple(outs)


def _attn_fn(q, k, v, row0):
    s = _mm_nt(q, k) * (1.0 / math.sqrt(QK_DIM))
    qpos = row0 + lax.broadcasted_iota(jnp.int32, s.shape, 0)
    kpos = lax.broadcasted_iota(jnp.int32, s.shape, 1)
    s = jnp.where(kpos <= qpos, s, -1e30)
    m = lax.stop_gradient(jnp.max(s, axis=-1, keepdims=True))
    p = jnp.exp(s - m)
    p = p / jnp.sum(p, axis=-1, keepdims=True)
    return _mm(p, v)


def _dn_prep_fn(rows, consts):
    qc, kc, ab = rows
    a_b = _mmx(ab, consts["sel_a"])
    b_b = _mmx(ab, consts["sel_b"])
    beta = jax.nn.sigmoid(b_b)
    g = -jnp.exp(consts["alog"]) * _softplus(a_b + consts["dtb"])
    qs, ks = [], []
    for h in range(HEADS):
        q, k = qc[h], kc[h]
        qs.append(q * lax.rsqrt(jnp.sum(q * q, -1, keepdims=True) + NORM_EPS))
        ks.append(k * lax.rsqrt(jnp.sum(k * k, -1, keepdims=True) + NORM_EPS))
    return tuple(qs), tuple(ks), g, beta


def _dn_chunk_fn(q, k, v, gb, g64, bb):
    nb = q.shape[0]
    ri = lax.broadcasted_iota(jnp.int32, (nb, CHUNK, CHUNK), 1)
    ci = lax.broadcasted_iota(jnp.int32, (nb, CHUNK, CHUNK), 2)
    tri = ri >= ci
    strict = ri > ci
    tril = tri.astype(F32)
    eye = (ri == ci).astype(F32)
    ones = jnp.ones((nb, CHUNK, CHUNK), F32)
    gc = _bmmx(tril, gb)
    gc64 = _bmmx(tril, g64)
    grow = _bmmx(ones, eye * gc64)
    diff = gc64 - grow
    decay = jnp.where(tri, jnp.exp(jnp.where(tri, diff, 0.0)), 0.0)
    kb = k * bb
    vb = v * bb
    a = jnp.where(strict, _bmm_nt(kb, k) * decay, 0.0)
    x = -a
    tinv = eye + x
    for _ in range(5):
        x = _bmmx(x, x)
        tinv = tinv + _bmmx(tinv, x)
    u = _bmm(tinv, vb)
    w = _bmm(tinv, kb * jnp.exp(gc))
    qs = q * (1.0 / math.sqrt(HEAD))
    qk = _bmm_nt(qs, k) * decay
    qg = qs * jnp.exp(gc)
    glast = jnp.sum(gb, axis=1, keepdims=True)
    kdec = k * jnp.exp(glast - gc)
    eg = jnp.broadcast_to(jnp.exp(glast), gb.shape)
    return u, w, qg, kdec, eg, qk


def _dn_rec_fn(s, u, w, qg, qk, kdec, eg):
    v_new = u - _mm(w, s)
    o = _mm(qg, s) + _mm(qk, v_new)
    s_new = s * eg + _mm_tn(kdec, v_new)
    return s_new, o


def _dn_out_fn(o, z, w):
    return _rms(o, w, HEAD) * _silu(z)


def _row_tile(t):
    return t // 8 if (t // 8) % 16 == 0 else t


def _tile(n, pref, unit):
    best = n
    for cand in range(unit, min(n, pref) + 1, unit):
        if n % cand == 0:
            best = cand
    return best if best <= pref else n


def _rows_call(name, body, rows, consts, outs, accs, r):
    rows = [a if isinstance(a, tuple) else (a, a.shape[1], 0) for a in rows]
    t = rows[0][0].shape[0]
    zero = lambda nd: (lambda i: (0,) * nd)
    in_specs = [pl.BlockSpec((r, w), functools.partial(lambda i, b: (i, b), b=blk)) for _, w, blk in rows]
    rows = [a for a, _, _ in rows]
    in_specs += [pl.BlockSpec(a.shape, zero(a.ndim)) for a in consts]
    out_shape = [jax.ShapeDtypeStruct((t, w), dt) for w, dt in outs] + [jax.ShapeDtypeStruct(s, F32) for s in accs]
    out_specs = [pl.BlockSpec((r, w), lambda i: (i, 0)) for w, _ in outs] + [pl.BlockSpec(s, zero(len(s))) for s in accs]
    return pl.pallas_call(body, name=name, grid=(t // r,), in_specs=in_specs, out_specs=out_specs, out_shape=out_shape,
                          compiler_params=_cparams(("arbitrary",)))(*rows, *consts)


def _accumulate(ref, val):
    @pl.when(pl.program_id(0) == 0)
    def _():
        ref[...] = jnp.zeros(ref.shape, ref.dtype)

    ref[...] += val


def _matmul(name, a, b, dims, out_dtype, res=None):
    if dims == "nn":
        (m, k), n = a.shape, b.shape[1]
    elif dims == "nt":
        (m, k), n = a.shape, b.shape[0]
    else:
        (k, m), n = a.shape, b.shape[1]
    tm = _tile(m, 640, 16 if dims != "tn" else 128)
    tn = _tile(n, 1408, 128)
    if dims == "nn":
        a_spec, b_spec, dn = pl.BlockSpec((tm, k), lambda i, j: (i, 0)), pl.BlockSpec((k, tn), lambda i, j: (0, j)), NN
    elif dims == "nt":
        a_spec, b_spec, dn = pl.BlockSpec((tm, k), lambda i, j: (i, 0)), pl.BlockSpec((tn, k), lambda i, j: (j, 0)), NT
    else:
        a_spec, b_spec, dn = pl.BlockSpec((k, tm), lambda i, j: (0, i)), pl.BlockSpec((k, tn), lambda i, j: (0, j)), TN
    o_spec = pl.BlockSpec((tm, tn), lambda i, j: (i, j))

    def body(*refs):
        a_ref, b_ref, o_ref = refs[0], refs[1], refs[-1]
        acc = lax.dot_general(a_ref[...].astype(BF16), b_ref[...].astype(BF16), (dn, ((), ())),
                              preferred_element_type=F32)
        if res is not None:
            acc = acc + refs[2][...]
        o_ref[...] = acc.astype(out_dtype)

    ins = [a, b] + ([res] if res is not None else [])
    specs = [a_spec, b_spec] + ([o_spec] if res is not None else [])
    return pl.pallas_call(body, name=name, grid=(m // tm, n // tn), in_specs=specs, out_specs=o_spec,
                          out_shape=jax.ShapeDtypeStruct((m, n), out_dtype),
                          compiler_params=_cparams(("arbitrary", "arbitrary")))(*ins)


def _rms_fwd(name, h, w):
    n = h.shape[1]

    def body(h_ref, w_ref, o_ref):
        o_ref[...] = _rms(h_ref[...], w_ref[...], n).astype(BF16)

    return _rows_call(name, body, [h], [w], [(n, BF16)], [], _row_tile(h.shape[0]))[0]


def _rms_bwd(name, h, w, cts, resid):
    n = h.shape[1]
    nct = len(cts)

    def body(*refs):
        h_ref, ct_refs, r_ref, w_ref, dh_ref, dw_ref = refs[0], refs[1:1 + nct], refs[1 + nct], refs[2 + nct], refs[-2], refs[-1]
        ct = ct_refs[0][...].astype(F32)
        for c in ct_refs[1:]:
            ct = ct + c[...].astype(F32)
        _, vjp = jax.vjp(lambda x, ww: _rms(x, ww, n), h_ref[...], w_ref[...])
        dh, dw = vjp(ct)
        dh_ref[...] = dh + r_ref[...]
        _accumulate(dw_ref, dw)

    return _rows_call(name, body, [h, *cts, resid], [w], [(n, F32)], [(1, n)], _row_tile(h.shape[0]))


def _mla_consts_from_refs(qa, wq, kva, wkv, qn, kn, perm):
    f = lambda r: r[...].astype(F32)
    return dict(
        qa_w=f(qa), kva_w=f(kva), perm=f(perm),
        wq_n=[wq[h * QK_PAD:h * QK_PAD + HEAD, :].astype(F32) for h in range(HEADS)],
        wq_r=[wq[h * QK_PAD + HEAD:(h + 1) * QK_PAD, :].astype(F32) for h in range(HEADS)],
        wk_n=[wkv[h * QK_PAD:h * QK_PAD + HEAD, :].astype(F32) for h in range(HEADS)],
        wv=[wkv[h * QK_PAD + HEAD:(h + 1) * QK_PAD, :].astype(F32) for h in range(HEADS)],
        qn_n=qn[:, 0:HEAD], qn_r=qn[:, HEAD:QK_PAD], kn_n=kn[:, 0:HEAD], kn_r=kn[:, HEAD:QK_PAD])


def _mla_prep_fwd(q_lat, kv_lat, k_pe, cos, sin, qa, wq, kva, wkv, qn, kn, perm):
    def body(ql, kvl, kp, c, s, qa_r, wq_r, kva_r, wkv_r, qn_r, kn_r, p_r, q_out, k_out, v_out):
        consts = _mla_consts_from_refs(qa_r, wq_r, kva_r, wkv_r, qn_r, kn_r, p_r)
        outs = _mla_prep_fn((ql[...], kvl[...], kp[...], c[...], s[...]), consts)
        for h in range(HEADS):
            q_n, q_r, k_n, k_r, v = outs[5 * h:5 * h + 5]
            q_out[:, h * QK_PAD:h * QK_PAD + HEAD] = q_n.astype(BF16)
            q_out[:, h * QK_PAD + HEAD:(h + 1) * QK_PAD] = q_r.astype(BF16)
            k_out[:, h * QK_PAD:h * QK_PAD + HEAD] = k_n.astype(BF16)
            k_out[:, h * QK_PAD + HEAD:(h + 1) * QK_PAD] = k_r.astype(BF16)
            v_out[:, h * HEAD:(h + 1) * HEAD] = v.astype(BF16)

    return _rows_call("mla_prep_fwd", body, [q_lat, kv_lat, k_pe, cos, sin], [qa, wq, kva, wkv, qn, kn, perm],
                      [(HEADS * QK_PAD, BF16), (HEADS * QK_PAD, BF16), (DN_WIDTH, BF16)], [], _row_tile(cos.shape[0]))


def _mla_prep_bwd(q_lat, kv_lat, k_pe, cos, sin, dq, dk, dv, qa, wq, kva, wkv, qn, kn, perm):
    def body(ql, kvl, kp, c, s, dq_r, dk_r, dv_r, qa_r, wq_r, kva_r, wkv_r, qn_r, kn_r, p_r,
             dql, dkvl, dkp, dqa, dwq, dkva, dwkv, dqn, dkn):
        consts = _mla_consts_from_refs(qa_r, wq_r, kva_r, wkv_r, qn_r, kn_r, p_r)
        cc, ss, pm = c[...], s[...], consts.pop("perm")
        _, vjp = jax.vjp(lambda rows, cs: _mla_prep_fn((*rows, cc, ss), dict(cs, perm=pm)), (ql[...], kvl[...], kp[...]),
                         consts)
        cts = []
        for h in range(HEADS):
            cts += [dq_r[:, h * QK_PAD:h * QK_PAD + HEAD], dq_r[:, h * QK_PAD + HEAD:(h + 1) * QK_PAD],
                    dk_r[:, h * QK_PAD:h * QK_PAD + HEAD], dk_r[:, h * QK_PAD + HEAD:(h + 1) * QK_PAD],
                    dv_r[:, h * HEAD:(h + 1) * HEAD]]
        (d_ql, d_kvl, d_kp), dc = vjp(tuple(cts))
        dql[...] = d_ql.astype(BF16)
        dkvl[...] = d_kvl.astype(BF16)
        dkp[...] = d_kp.astype(BF16)
        first = pl.program_id(0) == 0

        def acc(ref, sl, val):
            @pl.when(first)
            def _():
                ref[sl] = val

            @pl.when(jnp.logical_not(first))
            def _():
                ref[sl] += val

        full = (slice(None), slice(None))
        acc(dqa, full, dc["qa_w"])
        acc(dkva, full, dc["kva_w"])
        for h in range(HEADS):
            acc(dwq, (slice(h * QK_PAD, h * QK_PAD + HEAD), slice(None)), dc["wq_n"][h])
            acc(dwq, (slice(h * QK_PAD + HEAD, (h + 1) * QK_PAD), slice(None)), dc["wq_r"][h])
            acc(dwkv, (slice(h * QK_PAD, h * QK_PAD + HEAD), slice(None)), dc["wk_n"][h])
            acc(dwkv, (slice(h * QK_PAD + HEAD, (h + 1) * QK_PAD), slice(None)), dc["wv"][h])
        acc(dqn, (slice(None), slice(0, HEAD)), dc["qn_n"])
        acc(dqn, (slice(None), slice(HEAD, QK_PAD)), dc["qn_r"])
        acc(dkn, (slice(None), slice(0, HEAD)), dc["kn_n"])
        acc(dkn, (slice(None), slice(HEAD, QK_PAD)), dc["kn_r"])

    return _rows_call("mla_prep_bwd", body, [q_lat, kv_lat, k_pe, cos, sin, dq, dk, dv],
                      [qa, wq, kva, wkv, qn, kn, perm],
                      [(LORA, BF16), (LORA, BF16), (HEAD, BF16)],
                      [(1, LORA), wq.shape, (1, LORA), wkv.shape, (1, QK_PAD), (1, QK_PAD)], _row_tile(cos.shape[0]))


def _attn_fwd(q, k, v):
    t = q.shape[0]
    rq = HEAD

    def body(q_ref, k_ref, v_ref, o_ref):
        o_ref[...] = _attn_fn(q_ref[...], k_ref[...], v_ref[...], pl.program_id(1) * rq)

    return pl.pallas_call(
        body, name="attn_fwd", grid=(HEADS, t // rq),
        in_specs=[pl.BlockSpec((rq, QK_PAD), lambda h, i: (i, h)), pl.BlockSpec((t, QK_PAD), lambda h, i: (0, h)),
                  pl.BlockSpec((t, HEAD), lambda h, i: (0, h))],
        out_specs=pl.BlockSpec((rq, HEAD), lambda h, i: (i, h)),
        out_shape=jax.ShapeDtypeStruct((t, HEADS * HEAD), F32),
        compiler_params=_cparams(("arbitrary", "arbitrary")))(q, k, v)


def _attn_bwd(q, k, v, do):
    t = q.shape[0]
    rq = HEAD

    def body(q_ref, k_ref, v_ref, do_ref, dq_ref, dk_ref, dv_ref):
        i = pl.program_id(1)
        _, vjp = jax.vjp(lambda a, b, c: _attn_fn(a, b, c, i * rq), q_ref[...].astype(F32), k_ref[...].astype(F32),
                         v_ref[...].astype(F32))
        dq, dk, dv = vjp(do_ref[...])
        dq_ref[...] = dq

        @pl.when(i == 0)
        def _():
            dk_ref[...] = dk
            dv_ref[...] = dv

        @pl.when(i != 0)
        def _():
            dk_ref[...] += dk
            dv_ref[...] += dv

    return pl.pallas_call(
        body, name="attn_bwd", grid=(HEADS, t // rq),
        in_specs=[pl.BlockSpec((rq, QK_PAD), lambda h, i: (i, h)), pl.BlockSpec((t, QK_PAD), lambda h, i: (0, h)),
                  pl.BlockSpec((t, HEAD), lambda h, i: (0, h)), pl.BlockSpec((rq, HEAD), lambda h, i: (i, h))],
        out_specs=[pl.BlockSpec((rq, QK_PAD), lambda h, i: (i, h)), pl.BlockSpec((t, QK_PAD), lambda h, i: (0, h)),
                   pl.BlockSpec((t, HEAD), lambda h, i: (0, h))],
        out_shape=[jax.ShapeDtypeStruct((t, HEADS * QK_PAD), F32), jax.ShapeDtypeStruct((t, HEADS * QK_PAD), F32),
                   jax.ShapeDtypeStruct((t, HEADS * HEAD), F32)],
        compiler_params=_cparams(("arbitrary", "arbitrary")))(q, k, v, do)


def _mix_out_fwd(o_mla, o_dn, z, w_mla, w_dn):
    def body(om_ref, od_ref, z_ref, wm_ref, wd_ref, o_ref):
        for h in range(HEADS):
            sl = slice(h * HEAD, (h + 1) * HEAD)
            o_ref[:, sl] = _rms(om_ref[:, sl], wm_ref[...], HEAD).astype(BF16)
            o_ref[:, DN_WIDTH + h * HEAD:DN_WIDTH + (h + 1) * HEAD] = _dn_out_fn(od_ref[:, sl], z_ref[:, sl],
                                                                                 wd_ref[...]).astype(BF16)

    return _rows_call("mix_out_fwd", body, [o_mla, o_dn, z], [w_mla, w_dn], [(2 * DN_WIDTH, BF16)], [],
                      _row_tile(o_mla.shape[0]))[0]


def _mix_out_bwd(o_mla, o_dn, z, dmixed, w_mla, w_dn):
    def body(om_ref, od_ref, z_ref, dm_ref, wm_ref, wd_ref, dom_ref, dod_ref, dz_ref, dwm_ref, dwd_ref):
        dwm = dwd = None
        for h in range(HEADS):
            sl = slice(h * HEAD, (h + 1) * HEAD)
            _, vjp = jax.vjp(lambda o, w: _rms(o, w, HEAD), om_ref[:, sl], wm_ref[...])
            do, dw = vjp(dm_ref[:, sl])
            dom_ref[:, sl] = do
            dwm = dw if dwm is None else dwm + dw
            _, vjp = jax.vjp(_dn_out_fn, od_ref[:, sl], z_ref[:, sl], wd_ref[...])
            do, dz, dw = vjp(dm_ref[:, DN_WIDTH + h * HEAD:DN_WIDTH + (h + 1) * HEAD])
            dod_ref[:, sl] = do
            dz_ref[:, sl] = dz.astype(BF16)
            dwd = dw if dwd is None else dwd + dw
        _accumulate(dwm_ref, dwm)
        _accumulate(dwd_ref, dwd)

    return _rows_call("mix_out_bwd", body, [o_mla, o_dn, z, dmixed], [w_mla, w_dn],
                      [(DN_WIDTH, F32), (DN_WIDTH, F32), (DN_WIDTH, BF16)], [(1, HEAD), (1, HEAD)],
                      _row_tile(o_mla.shape[0]))


def _shift_down(x, s):
    if s == 0:
        return x
    rows = lax.broadcasted_iota(jnp.int32, x.shape, 0)
    return jnp.where(rows >= s, pltpu.roll(x, s, 0), 0.0)


def _shift_up(x, s):
    if s == 0:
        return x
    t = x.shape[0]
    rows = lax.broadcasted_iota(jnp.int32, x.shape, 0)
    return jnp.where(rows < t - s, pltpu.roll(x, t - s, 0), 0.0)


def _col_call(name, body, cols, taps, outs, tap_outs, cw):
    t, c = cols[0].shape[0], taps[0].shape[1]
    in_specs = [pl.BlockSpec((t, cw), lambda j: (0, j)) for _ in cols]
    in_specs += [pl.BlockSpec((a.shape[0], cw), lambda j: (0, j)) for a in taps]
    out_shape = [jax.ShapeDtypeStruct((t, c), dt) for dt in outs] + [jax.ShapeDtypeStruct((n, c), F32) for n in tap_outs]
    out_specs = [pl.BlockSpec((t, cw), lambda j: (0, j)) for _ in outs]
    out_specs += [pl.BlockSpec((n, cw), lambda j: (0, j)) for n in tap_outs]
    return pl.pallas_call(body, name=name, grid=(c // cw,), in_specs=in_specs, out_specs=out_specs, out_shape=out_shape,
                          compiler_params=_cparams(("arbitrary",)))(*cols, *taps)


def _causal_conv(x, w_ref, width):
    acc = w_ref[width - 1:width, :] * x
    for j in range(width - 1):
        acc = acc + w_ref[j:j + 1, :] * _shift_down(x, width - 1 - j)
    return acc


def _causal_conv_bwd(x, dpre, w_ref, dx_ref, dw_ref, width):
    dx = w_ref[width - 1:width, :] * dpre
    dw_ref[width - 1:width, :] = jnp.sum(dpre * x, axis=0, keepdims=True)
    for j in range(width - 1):
        s = width - 1 - j
        dx = dx + w_ref[j:j + 1, :] * _shift_up(dpre, s)
        dw_ref[j:j + 1, :] = jnp.sum(dpre * _shift_down(x, s), axis=0, keepdims=True)
    dx_ref[...] = dx.astype(dx_ref.dtype)


def _dsilu(x):
    sg = jax.nn.sigmoid(x)
    return sg * (1.0 + x * (1.0 - sg))


def _dn_conv_fwd(x, w):
    def body(x_ref, w_ref, y_ref):
        y_ref[...] = _silu(_causal_conv(x_ref[...], w_ref, 4))

    return _col_call("dn_conv_fwd", body, [x], [w], [F32], [], 256)[0]


def _dn_conv_bwd(x, w, dy):
    def body(x_ref, dy_ref, w_ref, dx_ref, dw_ref):
        xv = x_ref[...]
        dpre = dy_ref[...] * _dsilu(_causal_conv(xv, w_ref, 4))
        _causal_conv_bwd(xv, dpre, w_ref, dx_ref, dw_ref, 4)

    return _col_call("dn_conv_bwd", body, [x, dy], [w], [BF16], [4], 256)


def _glu_fwd(gpre, up, w, b):
    def body(g_ref, u_ref, w_ref, b_ref, a_ref):
        gate = _causal_conv(g_ref[...], w_ref, 3) + b_ref[...]
        a_ref[...] = (_silu(gate) * u_ref[...]).astype(BF16)

    return _col_call("glu_fwd", body, [gpre, up], [w, b], [BF16], [], 256)[0]


def _glu_bwd(gpre, up, w, b, dact):
    def body(g_ref, u_ref, da_ref, w_ref, b_ref, dg_ref, du_ref, dw_ref, db_ref):
        gv = g_ref[...]
        gate = _causal_conv(gv, w_ref, 3) + b_ref[...]
        da = da_ref[...]
        du_ref[...] = (da * _silu(gate)).astype(BF16)
        dgate = da * u_ref[...] * _dsilu(gate)
        db_ref[...] = jnp.sum(dgate, axis=0, keepdims=True)
        _causal_conv_bwd(gv, dgate, w_ref, dg_ref, dw_ref, 3)

    return _col_call("glu_bwd", body, [gpre, up, dact], [w, b], [BF16, BF16], [3, 1], 256)


def _dn_prep_consts(sa, sb, al, dt):
    return dict(sel_a=sa[...], sel_b=sb[...], alog=al[...], dtb=dt[...])


def _dn_prep_fwd(conv, ab, sel_a, sel_b, alog, dtb):
    def body(c_ref, ab_ref, sa, sb, al, dt, q_out, k_out, g_out, b_out):
        qc = tuple(c_ref[:, h * HEAD:(h + 1) * HEAD] for h in range(HEADS))
        kc = tuple(c_ref[:, DN_WIDTH + h * HEAD:DN_WIDTH + (h + 1) * HEAD] for h in range(HEADS))
        qs, ks, g, beta = _dn_prep_fn((qc, kc, ab_ref[...]), _dn_prep_consts(sa, sb, al, dt))
        for h in range(HEADS):
            q_out[:, h * HEAD:(h + 1) * HEAD] = qs[h]
            k_out[:, h * HEAD:(h + 1) * HEAD] = ks[h]
        g_out[...] = g
        b_out[...] = beta

    return _rows_call("dn_prep_fwd", body, [conv, ab], [sel_a, sel_b, alog, dtb], [(DN_WIDTH, F32)] * 4, [],
                      _row_tile(conv.shape[0]))


def _dn_prep_bwd(conv, ab, dq, dk, dv, dg, db, sel_a, sel_b, alog, dtb):
    def body(c_ref, ab_ref, dq_r, dk_r, dv_r, dg_r, db_r, sa, sb, al, dt, dc_out, dab_out, dal_out, ddt_out):
        qc = tuple(c_ref[:, h * HEAD:(h + 1) * HEAD] for h in range(HEADS))
        kc = tuple(c_ref[:, DN_WIDTH + h * HEAD:DN_WIDTH + (h + 1) * HEAD] for h in range(HEADS))
        consts = _dn_prep_consts(sa, sb, al, dt)
        sel = dict(sel_a=consts["sel_a"], sel_b=consts["sel_b"])
        _, vjp = jax.vjp(lambda rows, ad: _dn_prep_fn(rows, {**sel, **ad}), (qc, kc, ab_ref[...]),
                         dict(alog=consts["alog"], dtb=consts["dtb"]))
        cq = tuple(dq_r[:, h * HEAD:(h + 1) * HEAD] for h in range(HEADS))
        ck = tuple(dk_r[:, h * HEAD:(h + 1) * HEAD] for h in range(HEADS))
        (dqc, dkc, dab), dad = vjp((cq, ck, dg_r[...], db_r[...]))
        for h in range(HEADS):
            dc_out[:, h * HEAD:(h + 1) * HEAD] = dqc[h]
            dc_out[:, DN_WIDTH + h * HEAD:DN_WIDTH + (h + 1) * HEAD] = dkc[h]
        dc_out[:, 2 * DN_WIDTH:3 * DN_WIDTH] = dv_r[...]
        dab_out[...] = dab.astype(BF16)
        _accumulate(dal_out, dad["alog"])
        _accumulate(ddt_out, dad["dtb"])

    return _rows_call("dn_prep_bwd", body, [conv, ab, dq, dk, dv, dg, db], [sel_a, sel_b, alog, dtb],
                      [(3 * DN_WIDTH, F32), (HEAD, BF16)], [(1, DN_WIDTH), (1, DN_WIDTH)], _row_tile(conv.shape[0]))


def _chunk_batch(t):
    nc = t // CHUNK
    return nc // 2 if nc % 2 == 0 else nc


def _dn_chunk_specs(t, nb):
    rows = nb * CHUNK
    hb = lambda h, b: (b, h)
    vb = lambda h, b: (b, 2 * HEADS + h)
    qk_spec = pl.BlockSpec((None, rows, CHUNK), lambda h, b: (h, b, 0))
    blk = pl.BlockSpec((rows, HEAD), hb)
    return rows, blk, pl.BlockSpec((rows, HEAD), vb), qk_spec


def _dn_chunk_fwd(qn, kn, conv, g, beta):
    t = qn.shape[0]
    nb = _chunk_batch(t)
    rows, blk, vblk, qk_spec = _dn_chunk_specs(t, nb)

    def body(q_ref, k_ref, v_ref, g_ref, b_ref, u_o, w_o, qg_o, kd_o, eg_o, qk_o):
        r3 = lambda x: x.reshape(nb, CHUNK, x.shape[-1])
        outs = _dn_chunk_fn(r3(q_ref[...]), r3(k_ref[...]), r3(v_ref[...]), r3(g_ref[...]), r3(g_ref[:, 0:CHUNK]),
                            r3(b_ref[...]))
        for o_ref, val in zip((u_o, w_o, qg_o, kd_o, eg_o, qk_o), outs):
            o_ref[...] = val.reshape(rows, val.shape[-1])

    return pl.pallas_call(
        body, name="dn_chunk_fwd", grid=(HEADS, t // rows), in_specs=[blk, blk, vblk, blk, blk],
        out_specs=[blk] * 5 + [qk_spec],
        out_shape=[jax.ShapeDtypeStruct((t, DN_WIDTH), F32)] * 5 + [jax.ShapeDtypeStruct((HEADS, t, CHUNK), F32)],
        compiler_params=_cparams(("arbitrary", "arbitrary")))(qn, kn, conv, g, beta)


def _dn_chunk_bwd(qn, kn, conv, g, beta, cts):
    t = qn.shape[0]
    nb = _chunk_batch(t)
    rows, blk, vblk, qk_spec = _dn_chunk_specs(t, nb)

    def body(q_ref, k_ref, v_ref, g_ref, b_ref, du, dw, dqg, dkd, deg, dqk, dq_o, dk_o, dv_o, dg_o, db_o):
        r3 = lambda x: x.reshape(nb, CHUNK, x.shape[-1])
        _, vjp = jax.vjp(_dn_chunk_fn, r3(q_ref[...]), r3(k_ref[...]), r3(v_ref[...]), r3(g_ref[...]),
                         r3(g_ref[:, 0:CHUNK]), r3(b_ref[...]))
        dq, dk, dv, dg, dg64, db = vjp(tuple(r3(c[...]) for c in (du, dw, dqg, dkd, deg, dqk)))
        for o_ref, val in zip((dq_o, dk_o, dv_o, dg_o, db_o), (dq, dk, dv, dg, db)):
            o_ref[...] = val.reshape(rows, HEAD)
        dg_o[:, 0:CHUNK] += dg64.reshape(rows, CHUNK)

    return pl.pallas_call(
        body, name="dn_chunk_bwd", grid=(HEADS, t // rows), in_specs=[blk, blk, vblk, blk, blk] + [blk] * 5 + [qk_spec],
        out_specs=[blk] * 5, out_shape=[jax.ShapeDtypeStruct((t, DN_WIDTH), F32)] * 5,
        compiler_params=_cparams(("arbitrary", "arbitrary")))(qn, kn, conv, g, beta, *cts)


def _dn_rec_fwd(u, w, qg, kd, eg, qk):
    t = u.shape[0]
    nc = t // CHUNK
    blk = pl.BlockSpec((CHUNK, DN_WIDTH), lambda c: (c, 0))
    qk_spec = pl.BlockSpec((HEADS, CHUNK, CHUNK), lambda c: (0, c, 0))
    s_spec = pl.BlockSpec((None, DN_WIDTH, HEAD), lambda c: (c, 0, 0))

    def body(u_ref, w_ref, qg_ref, kd_ref, eg_ref, qk_ref, o_ref, sall_ref, s_scr):
        @pl.when(pl.program_id(0) == 0)
        def _():
            s_scr[...] = jnp.zeros(s_scr.shape, F32)

        sall_ref[...] = s_scr[...]
        for h in range(HEADS):
            sl = slice(h * HEAD, (h + 1) * HEAD)
            s_new, o = _dn_rec_fn(s_scr[sl, :], u_ref[:, sl], w_ref[:, sl], qg_ref[:, sl], qk_ref[h], kd_ref[:, sl],
                                  eg_ref[0:1, sl])
            o_ref[:, sl] = o
            s_scr[sl, :] = s_new

    return pl.pallas_call(
        body, name="dn_rec_fwd", grid=(nc,), in_specs=[blk] * 5 + [qk_spec], out_specs=[blk, s_spec],
        out_shape=[jax.ShapeDtypeStruct((t, DN_WIDTH), F32), jax.ShapeDtypeStruct((nc, DN_WIDTH, HEAD), F32)],
        scratch_shapes=[pltpu.VMEM((DN_WIDTH, HEAD), F32)],
        compiler_params=_cparams(("arbitrary",)))(u, w, qg, kd, eg, qk)


def _dn_rec_bwd(u, w, qg, kd, eg, qk, sall, do):
    t = u.shape[0]
    nc = t // CHUNK
    blk = pl.BlockSpec((CHUNK, DN_WIDTH), lambda c: (nc - 1 - c, 0))
    qk_spec = pl.BlockSpec((HEADS, CHUNK, CHUNK), lambda c: (0, nc - 1 - c, 0))
    s_spec = pl.BlockSpec((None, DN_WIDTH, HEAD), lambda c: (nc - 1 - c, 0, 0))

    def body(u_ref, w_ref, qg_ref, kd_ref, eg_ref, qk_ref, s_ref, do_ref, du_o, dw_o, dqg_o, dkd_o, deg_o, dqk_o, ds_scr):
        @pl.when(pl.program_id(0) == 0)
        def _():
            ds_scr[...] = jnp.zeros(ds_scr.shape, F32)

        deg_o[...] = jnp.zeros(deg_o.shape, F32)
        for h in range(HEADS):
            sl = slice(h * HEAD, (h + 1) * HEAD)
            _, vjp = jax.vjp(_dn_rec_fn, s_ref[sl, :], u_ref[:, sl], w_ref[:, sl], qg_ref[:, sl], qk_ref[h],
                             kd_ref[:, sl], eg_ref[0:1, sl])
            ds, du, dw, dqg, dqk, dkd, deg = vjp((ds_scr[sl, :], do_ref[:, sl]))
            du_o[:, sl] = du
            dw_o[:, sl] = dw
            dqg_o[:, sl] = dqg
            dkd_o[:, sl] = dkd
            deg_o[0:1, sl] = deg
            dqk_o[h] = dqk
            ds_scr[sl, :] = ds

    return pl.pallas_call(
        body, name="dn_rec_bwd", grid=(nc,), in_specs=[blk] * 5 + [qk_spec, s_spec, blk],
        out_specs=[blk] * 5 + [qk_spec],
        out_shape=[jax.ShapeDtypeStruct((t, DN_WIDTH), F32)] * 5 + [jax.ShapeDtypeStruct((HEADS, t, CHUNK), F32)],
        scratch_shapes=[pltpu.VMEM((DN_WIDTH, HEAD), F32)],
        compiler_params=_cparams(("arbitrary",)))(u, w, qg, kd, eg, qk, sall, do)


def _loss_call(h2, tgt, n_valid):
    t, n = h2.shape
    r = _row_tile(t)

    def body(h_ref, t_ref, dy_ref, acc_ref):
        rows = pl.program_id(0) * r + lax.broadcasted_iota(jnp.int32, (r, n), 0)
        valid = jnp.logical_and(rows >= N_META, rows < n_valid)
        e = jnp.where(valid, h_ref[...] - t_ref[...], 0.0)
        dy_ref[...] = e * (1.0 / n)
        _accumulate(acc_ref, jnp.sum(e * e, axis=0, keepdims=True))

    return _rows_call("loss", body, [h2, tgt], [], [(n, F32)], [(1, n)], r)


def _adamw_call(name, w, g, m, v):
    rows, cols = w.shape
    tr = _tile(rows, 256, 8)

    def body(w_ref, g_ref, m_ref, v_ref, d_ref, m_out, v_out):
        gv = g_ref[...]
        m2 = ADAM_B1 * m_ref[...] + (1.0 - ADAM_B1) * gv
        v2 = ADAM_B2 * v_ref[...] + (1.0 - ADAM_B2) * (gv * gv)
        m_hat = m2 / (1.0 - ADAM_B1 ** ADAM_STEP)
        v_hat = v2 / (1.0 - ADAM_B2 ** ADAM_STEP)
        d_ref[...] = -ADAM_LR * (m_hat / (jnp.sqrt(v_hat) + ADAM_EPS) + ADAM_WD * w_ref[...])
        m_out[...] = m2
        v_out[...] = v2

    spec = pl.BlockSpec((tr, cols), lambda i: (i, 0))
    return pl.pallas_call(body, name=name, grid=(rows // tr,), in_specs=[spec] * 4, out_specs=[spec] * 3,
                          out_shape=[jax.ShapeDtypeStruct((rows, cols), F32)] * 3,
                          compiler_params=_cparams(("arbitrary",)))(w, g, m, v)


def _rope_tables(t):
    half = ROPE // 2
    inv_freq = ROPE_THETA ** (-jnp.arange(half, dtype=F32) / half)
    ang = jnp.arange(t, dtype=F32)[:, None] * inv_freq[None, :]
    z = jnp.zeros((t, HEAD - ROPE), F32)
    cos = jnp.concatenate([jnp.cos(ang), jnp.cos(ang), z], axis=1)
    sin = jnp.concatenate([jnp.sin(ang), jnp.sin(ang), z], axis=1)
    k = jnp.arange(HEAD)[:, None]
    l = jnp.arange(HEAD)[None, :]
    perm = jnp.where((l < half) & (k == l + half), -1.0, 0.0) + jnp.where((l >= half) & (l < ROPE) & (k == l - half), 1.0, 0.0)
    return cos, sin, perm.astype(F32)


def _win_to_pad(w):
    z = lambda n: jnp.zeros((n, w.shape[1]), w.dtype)
    return jnp.concatenate([w[576:2112], w[2112:2624], w[0:256], w[256:512], w[512:576], z(64), w[2624:2632], z(120)],
                           axis=0)


def _win_from_pad(g):
    return jnp.concatenate([g[2048:2304], g[2304:2560], g[2560:2624], g[0:1536], g[1536:2048], g[2688:2696]], axis=0)


def _qk_to_pad(w):
    w4 = w.reshape(HEADS, QK_DIM, w.shape[-1])
    return jnp.concatenate([w4, jnp.zeros((HEADS, QK_PAD - QK_DIM, w.shape[-1]), w.dtype)], axis=1).reshape(
        HEADS * QK_PAD, w.shape[-1])


def _qk_from_pad(g):
    return g.reshape(HEADS, QK_PAD, g.shape[-1])[:, :QK_DIM].reshape(HEADS * QK_DIM, g.shape[-1])


def _ff_to_pad(a, axis):
    shape = list(a.shape)
    shape[axis:axis + 1] = [N_CHIPS, FF_SHARD]
    a4 = a.reshape(shape)
    shape[axis + 1] = FF_BLOCK - FF_SHARD
    out = jnp.concatenate([a4, jnp.zeros(shape, a.dtype)], axis=axis + 1)
    shape[axis:axis + 2] = [D_FF_P]
    return out.reshape(shape)


def _ff_from_pad(a, axis):
    shape = list(a.shape)
    shape[axis:axis + 1] = [N_CHIPS, FF_BLOCK]
    a4 = lax.slice_in_dim(a.reshape(shape), 0, FF_SHARD, axis=axis + 1)
    shape[axis:axis + 2] = [D_FF]
    return a4.reshape(shape)


def _local_step(x, tgt, wt):
    s = x.shape[0]
    n_valid = N_META + s
    t = -(-n_valid // HEAD) * HEAD
    zpad = jnp.zeros((t - n_valid, D_MODEL), F32)
    h0 = jnp.concatenate([wt["meta_tokens"], x, zpad], axis=0)
    tgt_p = jnp.concatenate([jnp.zeros((N_META, D_MODEL), F32), tgt, zpad], axis=0)
    cos, sin, perm = _rope_tables(t)
    win, wq, wkv = wt["w_in_t"], wt["w_q_t"], wt["w_kv_t"]
    qn_w = jnp.concatenate([wt["q_norm_w"], jnp.zeros((1, QK_PAD - QK_DIM), F32)], axis=1)
    kn_w = jnp.concatenate([wt["k_norm_w"], jnp.zeros((1, QK_PAD - QK_DIM), F32)], axis=1)
    head_id = jnp.arange(DN_WIDTH)[None, :] // HEAD
    lane = jnp.arange(HEAD)[:, None]
    sel_a = (lane == head_id).astype(F32)
    sel_b = (lane == head_id + HEADS).astype(F32)
    alog = jnp.repeat(wt["dn_A_log"], HEAD, axis=1)
    dtb = jnp.repeat(wt["dn_dt_bias"], HEAD, axis=1)
    w_out, w_gate, w_up, w_down = wt["w_out"], wt["w_gate_t"], wt["w_up_t"], wt["w_down"]
    conv_w, conv_b = wt["ffn_conv_w"], wt["ffn_conv_b"]

    u = _rms_fwd("attn_norm_fwd", h0, wt["attn_norm_w"])
    proj = _matmul("in_proj", u, win, "nt", F32)
    z = (proj, DN_WIDTH, 3)
    q_lat, kv_lat, k_pe, ab = (proj, LORA, 8), (proj, LORA, 9), (proj, HEAD, 20), (proj, HEAD, 21)
    mla_consts = (wt["q_a_norm_w"], wq, wt["kv_a_norm_w"], wkv, qn_w, kn_w, perm)
    q, k, v = _mla_prep_fwd(q_lat, kv_lat, k_pe, cos, sin, *mla_consts)
    o_mla = _attn_fwd(q, k, v)
    conv = _dn_conv_fwd(proj, wt["dn_conv_w"])
    dn_consts = (sel_a, sel_b, alog, dtb)
    qn, kn, g, beta = _dn_prep_fwd(conv, ab, *dn_consts)
    cu, cw, cqg, ckd, ceg, cqk = _dn_chunk_fwd(qn, kn, conv, g, beta)
    o_dn, sall = _dn_rec_fwd(cu, cw, cqg, ckd, ceg, cqk)
    mixed = _mix_out_fwd(o_mla, o_dn, z, wt["mla_out_norm_w"], wt["dn_out_norm_w"])
    h1 = _matmul("out_proj", mixed, w_out, "nn", F32, res=h0)
    n2 = _rms_fwd("ffn_norm_fwd", h1, wt["ffn_norm_w"])
    gpre = _matmul("gate_proj", n2, w_gate, "nt", F32)
    up = _matmul("up_proj", n2, w_up, "nt", F32)
    act = _glu_fwd(gpre, up, conv_w, conv_b)
    h2 = _matmul("down_proj", act, w_down, "nn", F32, res=h1)
    dy, sq = _loss_call(h2, tgt_p, n_valid)

    grads = {}
    dy16 = dy.astype(BF16)
    dact = _matmul("down_dx", dy16, w_down, "nt", F32)
    grads["w_down"] = _matmul("down_dw", act, dy16, "tn", F32)
    dgpre, dup, grads["ffn_conv_w"], grads["ffn_conv_b"] = _glu_bwd(gpre, up, conv_w, conv_b, dact)
    grads["w_gate_t"] = _matmul("gate_dw", dgpre, n2, "tn", F32)
    grads["w_up_t"] = _matmul("up_dw", dup, n2, "tn", F32)
    dn2a = _matmul("gate_dx", dgpre, w_gate, "nn", F32)
    dn2b = _matmul("up_dx", dup, w_up, "nn", F32)
    dh1, grads["ffn_norm_w"] = _rms_bwd("ffn_norm_bwd", h1, wt["ffn_norm_w"], [dn2a, dn2b], dy)
    dh1_16 = dh1.astype(BF16)
    dmixed = _matmul("out_dx", dh1_16, w_out, "nt", F32)
    grads["w_out"] = _matmul("out_dw", mixed, dh1_16, "tn", F32)
    do_mla, do_dn, dz, grads["mla_out_norm_w"], grads["dn_out_norm_w"] = _mix_out_bwd(
        o_mla, o_dn, z, dmixed, wt["mla_out_norm_w"], wt["dn_out_norm_w"])
    rec_cts = _dn_rec_bwd(cu, cw, cqg, ckd, ceg, cqk, sall, do_dn)
    dqn, dkn, dv_dn, dg, dbeta = _dn_chunk_bwd(qn, kn, conv, g, beta, rec_cts)
    dconv, dab, dalog, ddtb = _dn_prep_bwd(conv, ab, dqn, dkn, dv_dn, dg, dbeta, *dn_consts)
    grads["dn_A_log"] = jnp.sum(dalog.reshape(HEADS, HEAD), axis=1)[None, :]
    grads["dn_dt_bias"] = jnp.sum(ddtb.reshape(HEADS, HEAD), axis=1)[None, :]
    ddn_pre, grads["dn_conv_w"] = _dn_conv_bwd(proj, wt["dn_conv_w"], dconv)
    dq, dk, dv = _attn_bwd(q, k, v, do_mla)
    dq_lat, dkv_lat, dk_pe, dqa, dwq, dkva, dwkv, dqnw, dknw = _mla_prep_bwd(q_lat, kv_lat, k_pe, cos, sin, dq, dk, dv,
                                                                               *mla_consts)
    grads["q_a_norm_w"], grads["kv_a_norm_w"] = dqa, dkva
    grads["w_q_t"], grads["w_kv_t"] = dwq, dwkv
    grads["q_norm_w"], grads["k_norm_w"] = dqnw[:, :QK_DIM], dknw[:, :QK_DIM]
    dproj = jnp.concatenate([ddn_pre, dz, dq_lat, dkv_lat, dk_pe, dab], axis=1)
    grads["w_in_t"] = _matmul("in_dw", dproj, u, "tn", F32)
    du = _matmul("in_dx", dproj, win, "nn", F32)
    dh0, grads["attn_norm_w"] = _rms_bwd("attn_norm_bwd", h0, wt["attn_norm_w"], [du], dh1)
    grads["meta_tokens"] = dh0[0:N_META]
    return sq, dh0[N_META:n_valid], grads


def _mesh_pos():
    return lax.axis_index("x"), lax.axis_index("y"), lax.axis_index("c")


def _other_chips(x, y):
    return [(1 - x, y), (x, 1 - y), (1 - x, 1 - y)]


def _all_gather(shards):
    any_spec = pl.BlockSpec(memory_space=pl.ANY)
    nbuf = len(shards)

    def body(*refs):
        srcs, dsts, (send_sems, recv_sems) = refs[:nbuf], refs[nbuf:2 * nbuf], refs[2 * nbuf:]
        x, y, c = _mesh_pos()
        p = 2 * x + y
        sibling = (x, y, 1 - c)
        chips = _other_chips(x, y)
        bufs = tuple((s, d, s.shape[0] // 2) for s, d in zip(srcs, dsts))

        def half(ref, rows, which):
            return ref.at[pl.ds(which * rows, rows), :]

        def copy(i, k, src, dst, to):
            return pltpu.make_async_remote_copy(src_ref=src, dst_ref=dst, send_sem=send_sems.at[i, k],
                                                recv_sem=recv_sems.at[i, k], device_id=to, device_id_type=MESH)

        sends = []
        for i, (src, dst, rows) in enumerate(bufs):
            for j, chip in enumerate(chips):
                sends.append(copy(i, j, half(src, rows, c), half(dst.at[p], rows, c), (*chip, c)))
        for cp in sends:
            cp.start()
        passed = []
        for i, (src, dst, rows) in enumerate(bufs):
            for j, (qx, qy) in enumerate(chips):
                block = half(dst.at[2 * qx + qy], rows, c)
                copy(i, j, block, block, (x, y, c)).wait_recv()
                fwd = copy(i, 3 + j, block, block, sibling)
                fwd.start()
                passed.append(fwd)
        for i, (src, dst, rows) in enumerate(bufs):
            for j, (qx, qy) in enumerate(chips):
                block = half(dst.at[2 * qx + qy], rows, 1 - c)
                copy(i, 3 + j, block, block, (x, y, c)).wait_recv()
        for cp in sends + passed:
            cp.wait_send()

    gathered = pl.pallas_call(
        body, name="all_gather_weights", in_specs=[any_spec] * nbuf, out_specs=[any_spec] * nbuf,
        out_shape=[jax.ShapeDtypeStruct((N_CHIPS, *s.shape), s.dtype) for s in shards],
        scratch_shapes=[pltpu.SemaphoreType.DMA((nbuf, 6)), pltpu.SemaphoreType.DMA((nbuf, 6))],
        compiler_params=pltpu.CompilerParams(has_side_effects=True))(*shards)
    p = 2 * lax.axis_index("x") + lax.axis_index("y")
    return [lax.dynamic_update_slice(g, s[None], (p, 0, 0)) for g, s in zip(gathered, shards)]


def _rs_to_sibling(bufs):
    any_spec = pl.BlockSpec(memory_space=pl.ANY)
    nbuf = len(bufs)

    def body(*refs):
        srcs, dsts, (send_sems, recv_sems) = refs[:nbuf], refs[nbuf:2 * nbuf], refs[2 * nbuf:]
        x, y, c = _mesh_pos()
        copies = []
        for i, (src, dst) in enumerate(zip(srcs, dsts)):
            half = src.shape[1] // 2
            copies.append(pltpu.make_async_remote_copy(
                src_ref=src.at[:, pl.ds((1 - c) * half, half), :], dst_ref=dst, send_sem=send_sems.at[i],
                recv_sem=recv_sems.at[i], device_id=(x, y, 1 - c), device_id_type=MESH))
        for cp in copies:
            cp.start()
        for cp in copies:
            cp.wait()

    return pl.pallas_call(
        body, name="rs_sibling_exchange", in_specs=[any_spec] * nbuf, out_specs=[any_spec] * nbuf,
        out_shape=[jax.ShapeDtypeStruct((N_CHIPS, b.shape[1] // 2, b.shape[2]), F32) for b in bufs],
        scratch_shapes=[pltpu.SemaphoreType.DMA((nbuf,)), pltpu.SemaphoreType.DMA((nbuf,))],
        compiler_params=pltpu.CompilerParams(has_side_effects=True))(*bufs)


def _rs_pair_add(name, buf, got, c, out_dtype):
    half, cols = got.shape[1], got.shape[2]

    def body(c_ref, a_ref, b_ref, o_ref):
        o_ref[...] = (a_ref[...] + b_ref[...]).astype(out_dtype)

    return pl.pallas_call(
        body, name=name,
        grid_spec=pltpu.PrefetchScalarGridSpec(
            num_scalar_prefetch=1, grid=(N_CHIPS,),
            in_specs=[pl.BlockSpec((None, half, cols), lambda j, cr: (j, cr[0], 0)),
                      pl.BlockSpec((None, half, cols), lambda j, cr: (j, 0, 0))],
            out_specs=pl.BlockSpec((None, half, cols), lambda j, cr: (j, 0, 0))),
        out_shape=jax.ShapeDtypeStruct(got.shape, out_dtype),
        compiler_params=_cparams(("arbitrary",)))(c, buf, got)


def _rs_to_chips(accs):
    any_spec = pl.BlockSpec(memory_space=pl.ANY)
    nbuf = len(accs)

    def body(*refs):
        srcs, dsts, (send_sems, recv_sems) = refs[:nbuf], refs[nbuf:2 * nbuf], refs[2 * nbuf:]
        x, y, c = _mesh_pos()
        copies = []
        for i, (src, dst) in enumerate(zip(srcs, dsts)):
            for k, (qx, qy) in enumerate(_other_chips(x, y)):
                copies.append(pltpu.make_async_remote_copy(
                    src_ref=src.at[2 * qx + qy], dst_ref=dst.at[k], send_sem=send_sems.at[i, k],
                    recv_sem=recv_sems.at[i, k], device_id=(qx, qy, c), device_id_type=MESH))
        for cp in copies:
            cp.start()
        for cp in copies:
            cp.wait()

    return pl.pallas_call(
        body, name="rs_chip_exchange", in_specs=[any_spec] * nbuf, out_specs=[any_spec] * nbuf,
        out_shape=[jax.ShapeDtypeStruct((3, a.shape[1], a.shape[2]), a.dtype) for a in accs],
        scratch_shapes=[pltpu.SemaphoreType.DMA((nbuf, 3)), pltpu.SemaphoreType.DMA((nbuf, 3))],
        compiler_params=pltpu.CompilerParams(has_side_effects=True))(*accs)


def _rs_chip_add(name, acc, got, p):
    half, cols = acc.shape[1], acc.shape[2]
    tr = _tile(half, 128, 8)
    slot = (0, 1, 0, 2)

    def body(p_ref, own_ref, g0_ref, g1_ref, g2_ref, o_ref):
        me = p_ref[0]
        gots = (g0_ref, g1_ref, g2_ref)
        total = None
        for chip in range(N_CHIPS):
            val = own_ref[...].astype(F32)
            for e in (1, 2, 3):
                val = jnp.where((chip ^ me) == e, gots[slot[e]][...].astype(F32), val)
            total = val if total is None else total + val
        o_ref[...] = total

    gspec = lambda k: pl.BlockSpec((None, tr, cols), lambda i, pr: (k, i, 0))
    return pl.pallas_call(
        body, name=name,
        grid_spec=pltpu.PrefetchScalarGridSpec(
            num_scalar_prefetch=1, grid=(half // tr,),
            in_specs=[pl.BlockSpec((None, tr, cols), lambda i, pr: (pr[0], i, 0)), gspec(0), gspec(1), gspec(2)],
            out_specs=pl.BlockSpec((tr, cols), lambda i, pr: (i, 0))),
        out_shape=jax.ShapeDtypeStruct((half, cols), F32),
        compiler_params=_cparams(("arbitrary",)))(p, acc, got, got, got)


def _rs_share(ress):
    any_spec = pl.BlockSpec(memory_space=pl.ANY)
    nbuf = len(ress)

    def body(*refs):
        srcs, dsts, (send_sems, recv_sems) = refs[:nbuf], refs[nbuf:2 * nbuf], refs[2 * nbuf:]
        x, y, c = _mesh_pos()
        copies = [pltpu.make_async_remote_copy(src_ref=src, dst_ref=dst, send_sem=send_sems.at[i], recv_sem=recv_sems.at[i],
                                               device_id=(x, y, 1 - c), device_id_type=MESH)
                  for i, (src, dst) in enumerate(zip(srcs, dsts))]
        for cp in copies:
            cp.start()
        for cp in copies:
            cp.wait()

    gots = pl.pallas_call(
        body, name="rs_sibling_share", in_specs=[any_spec] * nbuf, out_specs=[any_spec] * nbuf,
        out_shape=[jax.ShapeDtypeStruct(r.shape, F32) for r in ress],
        scratch_shapes=[pltpu.SemaphoreType.DMA((nbuf,)), pltpu.SemaphoreType.DMA((nbuf,))],
        compiler_params=pltpu.CompilerParams(has_side_effects=True))(*ress)
    south = lax.axis_index("c") == 0
    return [jnp.concatenate([jnp.where(south, r, g), jnp.where(south, g, r)], axis=0) for r, g in zip(ress, gots)]


def _reduce_scatter(names, bufs):
    x, y, c = _mesh_pos()
    ci = jnp.reshape(c, (1,)).astype(jnp.int32)
    pi = jnp.reshape(2 * x + y, (1,)).astype(jnp.int32)
    gots = _rs_to_sibling(bufs)
    accs = [_rs_pair_add("rs_pair_add_" + n, b, g, ci, F32 if n == "small" else BF16) for n, b, g in zip(names, bufs, gots)]
    gots2 = _rs_to_chips(accs)
    ress = [_rs_chip_add("rs_chip_add_" + n, a, g, pi) for n, a, g in zip(names, accs, gots2)]
    return _rs_share(ress)


def _pad_rows(flat, rows):
    return jnp.concatenate([flat, jnp.zeros((rows * LANES - flat.shape[0],), flat.dtype)]).reshape(rows, LANES)


def _unshard(g4, shape, axis):
    a = g4.reshape(N_CHIPS, *shape)
    if axis == 0:
        return a.reshape(N_CHIPS * shape[0], shape[1])
    return jnp.transpose(a, (1, 0, 2)).reshape(shape[0], N_CHIPS * shape[1])


def _shard4(full, shape, axis):
    if axis == 0:
        return full.reshape(N_CHIPS, shape[0] * shape[1])
    a = full.reshape(shape[0], N_CHIPS, shape[1])
    return jnp.transpose(a, (1, 0, 2)).reshape(N_CHIPS, shape[0] * shape[1])


def _pad_axis0(a, rows):
    return jnp.concatenate([a, jnp.zeros((rows - a.shape[0], *a.shape[1:]), a.dtype)], axis=0)


def _pad_axis1(a, rows):
    return jnp.concatenate([a, jnp.zeros((a.shape[0], rows - a.shape[1], *a.shape[2:]), a.dtype)], axis=1)


def _shard_to_strip(name, w):
    _, (shape, axis, rows) = name, {n: (s, ax, r) for n, s, ax, r in BIG}[name]
    w2 = w.reshape(shape).astype(BF16)
    return _pad_axis0(w2.T if axis == 1 else w2, rows)


def _strips_to_local(gathered):
    g = gathered
    win = g["w_in"][:, :IN_SHARD].reshape(IN_COLS, D_MODEL)
    return dict(
        w_in_t=_win_to_pad(win), w_q_t=_qk_to_pad(g["w_q_b"].reshape(HEADS * QK_DIM, LORA)),
        w_kv_t=g["w_kv_b"].reshape(HEADS * QK_PAD, LORA), w_out=g["w_out"].reshape(D_MODEL, D_MODEL),
        w_gate_t=g["w_gate"].reshape(D_FF_P, D_MODEL), w_up_t=g["w_up"].reshape(D_FF_P, D_MODEL),
        w_down=g["w_down"].reshape(D_FF_P, D_MODEL))


def _local_to_strips(g):
    win = _pad_axis1(_win_from_pad(g["w_in_t"]).reshape(N_CHIPS, IN_SHARD, D_MODEL), IN_SHARD_P)
    return dict(
        w_in=win, w_q_b=_qk_from_pad(g["w_q_t"]).reshape(N_CHIPS, QK_DIM, LORA),
        w_kv_b=g["w_kv_t"].reshape(N_CHIPS, QK_PAD, LORA), w_out=g["w_out"].reshape(N_CHIPS, LORA, D_MODEL),
        w_gate=g["w_gate_t"].reshape(N_CHIPS, FF_BLOCK, D_MODEL), w_up=g["w_up_t"].reshape(N_CHIPS, FF_BLOCK, D_MODEL),
        w_down=g["w_down"].reshape(N_CHIPS, FF_BLOCK, D_MODEL))


def _strip_to_shard(name, strip):
    shape, axis = {n: (s, ax) for n, s, ax, _ in BIG}[name]
    rows = shape[axis]
    return strip[:rows].T if axis == 1 else strip[:rows]


def kernel(x, meta_tokens, attn_norm_w, w_in, q_a_norm_w, w_q_b, kv_a_norm_w, w_kv_b, q_norm_w, k_norm_w, mla_out_norm_w, dn_conv_w, dn_A_log, dn_dt_bias, dn_out_norm_w, w_out, ffn_norm_w, w_gate, w_up, ffn_conv_w, ffn_conv_b, w_down, loss_target, m_meta_tokens, m_attn_norm_w, m_w_in, m_q_a_norm_w, m_w_q_b, m_kv_a_norm_w, m_w_kv_b, m_q_norm_w, m_k_norm_w, m_mla_out_norm_w, m_dn_conv_w, m_dn_A_log, m_dn_dt_bias, m_dn_out_norm_w, m_w_out, m_ffn_norm_w, m_w_gate, m_w_up, m_ffn_conv_w, m_ffn_conv_b, m_w_down, v_meta_tokens, v_attn_norm_w, v_w_in, v_q_a_norm_w, v_w_q_b, v_kv_a_norm_w, v_w_kv_b, v_q_norm_w, v_k_norm_w, v_mla_out_norm_w, v_dn_conv_w, v_dn_A_log, v_dn_dt_bias, v_dn_out_norm_w, v_w_out, v_ffn_norm_w, v_w_gate, v_w_up, v_ffn_conv_w, v_ffn_conv_b, v_w_down):
    local = dict(locals())
    w = {n: local[n] for n in WEIGHTS}
    m = {n: local["m_" + n] for n in WEIGHTS}
    v = {n: local["v_" + n] for n in WEIGHTS}
    big_names = [n for n, _, _, _ in BIG]

    wf = _pad_rows(jnp.concatenate([w[n].reshape(-1) for n, _, _ in SMALL_SHARDED]), SMALL_ROWS)
    gathered = _all_gather([_shard_to_strip(n, w[n]) for n in big_names] + [wf])
    full = _strips_to_local(dict(zip(big_names, gathered[:-1])))
    gf = gathered[-1].reshape(N_CHIPS, -1)
    off = 0
    for n, s, ax in SMALL_SHARDED:
        full[n] = _unshard(gf[:, off:off + s[0] * s[1]], s, ax)
        off += s[0] * s[1]
    for n, _ in REPLICATED:
        full[n] = w[n]
    full["ffn_conv_w"] = _ff_to_pad(full["ffn_conv_w"], 1)
    full["ffn_conv_b"] = _ff_to_pad(full["ffn_conv_b"], 1)

    sq, grad_x, g = _local_step(x[0], loss_target[0], full)
    loss = lax.psum(0.5 / D_MODEL * jnp.sum(sq), ("x", "y", "c"))
    g["ffn_conv_w"] = _ff_from_pad(g["ffn_conv_w"], 1)
    g["ffn_conv_b"] = _ff_from_pad(g["ffn_conv_b"], 1)

    rep = jnp.concatenate([g[n].reshape(-1) for n, _ in REPLICATED])
    rep = jnp.concatenate([rep, jnp.zeros((REP_ROWS * LANES - rep.shape[0],), F32)])
    small = jnp.concatenate([_shard4(g[n], s, ax) for n, s, ax in SMALL_SHARDED], axis=1)
    small = jnp.concatenate([small, jnp.zeros((N_CHIPS, SMALL_ROWS * LANES - small.shape[1]), F32),
                             jnp.broadcast_to(rep, (N_CHIPS, rep.shape[0]))], axis=1)
    strips = _local_to_strips(g)
    reduced = _reduce_scatter(big_names + ["small"],
                              [strips[n] for n in big_names] + [small.reshape(N_CHIPS, SMALL_ROWS + REP_ROWS, LANES)])
    gs = {n: _strip_to_shard(n, r) for n, r in zip(big_names, reduced[:-1])}
    red = reduced[-1].reshape(-1)
    off = 0
    for n, s, _ in SMALL_SHARDED:
        gs[n] = red[off:off + s[0] * s[1]].reshape(s)
        off += s[0] * s[1]
    off = SMALL_ROWS * LANES
    for n, cnt in REPLICATED:
        gs[n] = red[off:off + cnt].reshape(1, cnt)
        off += cnt

    delta, new_m, new_v = {}, {}, {}
    for n, s, _, _ in BIG:
        d2, m2, v2 = _adamw_call("adamw_" + n, w[n].reshape(s), gs[n], m[n].reshape(s), v[n].reshape(s))
        delta[n], new_m[n], new_v[n] = (a.reshape(w[n].shape) for a in (d2, m2, v2))
    small_names = [n for n, _, _ in SMALL_SHARDED] + [n for n, _ in REPLICATED]
    rows = SMALL_ROWS + REP_ROWS
    pack = lambda d: _pad_rows(jnp.concatenate([d[n].reshape(-1) for n in small_names]), rows)
    d2, m2, v2 = _adamw_call("adamw_small", pack(w), pack(gs), pack(m), pack(v))
    off = 0
    for n in small_names:
        cnt = w[n].size
        for dst, src in ((delta, d2), (new_m, m2), (new_v, v2)):
            dst[n] = src.reshape(-1)[off:off + cnt].reshape(w[n].shape)
        off += cnt

    grad_out = [gs[n].reshape(w[n].shape) for n in WEIGHTS]
    return (loss, grad_x[None], *grad_out, *[delta[n] for n in WEIGHTS], *[new_m[n] for n in WEIGHTS],
            *[new_v[n] for n in WEIGHTS])
```

```python
import functools
import math

import jax
import jax.numpy as jnp
from jax import lax
from jax.experimental import pallas as pl
from jax.experimental.pallas import tpu as pltpu

F32 = jnp.float32
BF16 = jnp.bfloat16
HI = lax.Precision.HIGHEST
MESH = pl.DeviceIdType.MESH

N_META = 16
D_MODEL = 1024
HEADS = 4
HEAD = 128
ROPE = 64
QK_DIM = HEAD + ROPE
QK_PAD = 2 * HEAD
LORA = 256
DN_WIDTH = HEADS * HEAD
CHUNK = 64
D_FF = 2816
N_CHIPS = 4
FF_SHARD = D_FF // N_CHIPS
FF_BLOCK = 768
D_FF_P = N_CHIPS * FF_BLOCK
IN_COLS = 2632
IN_SHARD = IN_COLS // N_CHIPS
IN_SHARD_P = 672
IN_PAD = 2816
NORM_EPS = 1e-6
ROPE_THETA = 10000.0
LANES = 512

ADAM_LR, ADAM_B1, ADAM_B2, ADAM_EPS, ADAM_WD, ADAM_STEP = 0.001, 0.9, 0.999, 1e-08, 0.01, 10

VMEM_LIMIT = 56 * 1024 * 1024

BIG = (("w_in", (1024, 658), 1, IN_SHARD_P), ("w_q_b", (256, 192), 1, 192), ("w_kv_b", (256, 256), 1, 256),
       ("w_out", (256, 1024), 0, 256), ("w_gate", (1024, 704), 1, FF_BLOCK), ("w_up", (1024, 704), 1, FF_BLOCK),
       ("w_down", (704, 1024), 0, FF_BLOCK))
SMALL_SHARDED = (("meta_tokens", (16, 256), 1), ("dn_conv_w", (4, 384), 1), ("ffn_conv_w", (3, 704), 1))
REPLICATED = (("attn_norm_w", 1024), ("q_a_norm_w", 256), ("kv_a_norm_w", 256), ("q_norm_w", 192), ("k_norm_w", 192),
              ("mla_out_norm_w", 128), ("dn_A_log", 4), ("dn_dt_bias", 4), ("dn_out_norm_w", 128), ("ffn_norm_w", 1024),
              ("ffn_conv_b", 2816))
WEIGHTS = ("meta_tokens", "attn_norm_w", "w_in", "q_a_norm_w", "w_q_b", "kv_a_norm_w", "w_kv_b", "q_norm_w", "k_norm_w",
           "mla_out_norm_w", "dn_conv_w", "dn_A_log", "dn_dt_bias", "dn_out_norm_w", "w_out", "ffn_norm_w", "w_gate",
           "w_up", "ffn_conv_w", "ffn_conv_b", "w_down")

SMALL_ROWS = 16
REP_ROWS = 16


def _cparams(sem):
    return pltpu.CompilerParams(dimension_semantics=sem, vmem_limit_bytes=VMEM_LIMIT)


NN, NT, TN = ((1,), (0,)), ((1,), (1,)), ((0,), (0,))


def _shift_dims(dims, batch):
    if not batch:
        return (dims, ((), ()))
    return (((dims[0][0] + 1,), (dims[1][0] + 1,)), ((0,), (0,)))


def _make_mm(dims, exact, batch=False):
    def raw(a, b, d):
        if exact:
            return lax.dot_general(a.astype(F32), b.astype(F32), _shift_dims(d, batch), precision=HI,
                                   preferred_element_type=F32)
        return lax.dot_general(a.astype(BF16), b.astype(BF16), _shift_dims(d, batch), preferred_element_type=F32)

    @jax.custom_vjp
    def mm(a, b):
        return raw(a, b, dims)

    def fwd(a, b):
        return raw(a, b, dims), (a, b)

    def bwd(res, g):
        a, b = res
        if dims == NN:
            da, db = raw(g, b, NT), raw(a, g, TN)
        elif dims == NT:
            da, db = raw(g, b, NN), raw(g, a, TN)
        else:
            da, db = raw(b, g, NT), raw(a, g, NN)
        return da.astype(a.dtype), db.astype(b.dtype)

    mm.defvjp(fwd, bwd)
    return mm


_mm = _make_mm(NN, False)
_mm_nt = _make_mm(NT, False)
_mm_tn = _make_mm(TN, False)
_mmx = _make_mm(NN, True)
_bmm = _make_mm(NN, False, batch=True)
_bmm_nt = _make_mm(NT, False, batch=True)
_bmmx = _make_mm(NN, True, batch=True)
_bmmx_nt = _make_mm(NT, True, batch=True)
_bmmx_tn = _make_mm(TN, True, batch=True)


@jax.custom_vjp
def _unit_lower_inv(a):
    n = a.shape[-1]
    eye = (lax.broadcasted_iota(jnp.int32, a.shape, 1) == lax.broadcasted_iota(jnp.int32, a.shape, 2)).astype(F32)
    x = -a
    t = eye + x
    for _ in range(max(n.bit_length() - 2, 0)):
        x = _bmmx(x, x)
        t = t + _bmmx(t, x)
    return t


def _unit_lower_inv_fwd(a):
    t = _unit_lower_inv(a)
    return t, t


def _unit_lower_inv_bwd(t, g):
    return (-_bmmx_tn(t, _bmmx_nt(g, t)),)


_unit_lower_inv.defvjp(_unit_lower_inv_fwd, _unit_lower_inv_bwd)


def _rms(x, w, n):
    ms = jnp.sum(x * x, axis=-1, keepdims=True) * (1.0 / n)
    return x * lax.rsqrt(ms + NORM_EPS) * w


def _silu(x):
    return x * jax.nn.sigmoid(x)


def _softplus(x):
    return jnp.maximum(x, 0.0) + jnp.log(1.0 + jnp.exp(-jnp.abs(x)))


def _rope(x, cos, sin, perm):
    return x * cos + _mmx(x, perm) * sin


def _mla_prep_fn(rows, consts):
    q_lat, kv_lat, k_pe, cos, sin = rows
    qn = _rms(q_lat, consts["qa_w"], LORA)
    kvn = _rms(kv_lat, consts["kva_w"], LORA)
    outs = []
    for h in range(HEADS):
        q_n = _mm_nt(qn, consts["wq_n"][h])
        q_r = _mm_nt(qn, consts["wq_r"][h])
        rs = lax.rsqrt((jnp.sum(q_n * q_n, -1, keepdims=True) + jnp.sum(q_r * q_r, -1, keepdims=True)) * (1.0 / QK_DIM)
                       + NORM_EPS)
        q_n = q_n * rs * consts["qn_n"]
        q_r = _rope(q_r * rs * consts["qn_r"], cos, sin, consts["perm"])
        k_n = _mm_nt(kvn, consts["wk_n"][h])
        v = _mm_nt(kvn, consts["wv"][h])
        rk = lax.rsqrt((jnp.sum(k_n * k_n, -1, keepdims=True) + jnp.sum(k_pe * k_pe, -1, keepdims=True)) * (1.0 / QK_DIM)
                       + NORM_EPS)
        k_n = k_n * rk * consts["kn_n"]
        k_r = _rope(k_pe * rk * consts["kn_r"], cos, sin, consts["perm"])
        outs += [q_n, q_r, k_n, k_r, v]
    return tuple(outs)


def _attn_fn(q, k, v, row0):
    s = _mm_nt(q, k) * (1.0 / math.sqrt(QK_DIM))
    qpos = row0 + lax.broadcasted_iota(jnp.int32, s.shape, 0)
    kpos = lax.broadcasted_iota(jnp.int32, s.shape, 1)
    s = jnp.where(kpos <= qpos, s, -1e30)
    m = lax.stop_gradient(jnp.max(s, axis=-1, keepdims=True))
    p = jnp.exp(s - m)
    p = p / jnp.sum(p, axis=-1, keepdims=True)
    return _mm(p, v)


def _dn_prep_fn(rows, consts):
    qc, kc, ab = rows
    a_b = _mmx(ab, consts["sel_a"])
    b_b = _mmx(ab, consts["sel_b"])
    beta = jax.nn.sigmoid(b_b)
    g = -jnp.exp(consts["alog"]) * _softplus(a_b + consts["dtb"])
    qs, ks = [], []
    for h in range(HEADS):
        q, k = qc[h], kc[h]
        qs.append(q * lax.rsqrt(jnp.sum(q * q, -1, keepdims=True) + NORM_EPS))
        ks.append(k * lax.rsqrt(jnp.sum(k * k, -1, keepdims=True) + NORM_EPS))
    return tuple(qs), tuple(ks), g, beta


def _dn_chunk_fn(q, k, v, gb, g64, bb):
    nb = q.shape[0]
    ri = lax.broadcasted_iota(jnp.int32, (nb, CHUNK, CHUNK), 1)
    ci = lax.broadcasted_iota(jnp.int32, (nb, CHUNK, CHUNK), 2)
    tri = ri >= ci
    strict = ri > ci
    tril = tri.astype(F32)
    eye = (ri == ci).astype(F32)
    ones = jnp.ones((nb, CHUNK, CHUNK), F32)
    gc = _bmmx(tril, gb)
    gc64 = _bmmx(tril, g64)
    grow = _bmmx(ones, eye * gc64)
    diff = gc64 - grow
    decay = jnp.where(tri, jnp.exp(jnp.where(tri, diff, 0.0)), 0.0)
    kb = k * bb
    vb = v * bb
    a = jnp.where(strict, _bmm_nt(kb, k) * decay, 0.0)
    tinv = _unit_lower_inv(a)
    u = _bmm(tinv, vb)
    w = _bmm(tinv, kb * jnp.exp(gc))
    qs = q * (1.0 / math.sqrt(HEAD))
    qk = _bmm_nt(qs, k) * decay
    qg = qs * jnp.exp(gc)
    glast = jnp.sum(gb, axis=1, keepdims=True)
    kdec = k * jnp.exp(glast - gc)
    eg = jnp.broadcast_to(jnp.exp(glast), gb.shape)
    return u, w, qg, kdec, eg, qk


def _dn_rec_fn(s, u, w, qg, qk, kdec, eg):
    v_new = u - _mm(w, s)
    o = _mm(qg, s) + _mm(qk, v_new)
    s_new = s * eg + _mm_tn(kdec, v_new)
    return s_new, o


def _dn_out_fn(o, z, w):
    return _rms(o, w, HEAD) * _silu(z)


def _row_tile(t):
    return t // 8 if (t // 8) % 16 == 0 else t


def _tile(n, pref, unit):
    best = n
    for cand in range(unit, min(n, pref) + 1, unit):
        if n % cand == 0:
            best = cand
    return best if best <= pref else n


def _rows_call(name, body, rows, consts, outs, accs, r):
    rows = [a if isinstance(a, tuple) else (a, a.shape[1], 0) for a in rows]
    t = rows[0][0].shape[0]
    zero = lambda nd: (lambda i: (0,) * nd)
    in_specs = [pl.BlockSpec((r, w), functools.partial(lambda i, b: (i, b), b=blk)) for _, w, blk in rows]
    rows = [a for a, _, _ in rows]
    in_specs += [pl.BlockSpec(a.shape, zero(a.ndim)) for a in consts]
    out_shape = [jax.ShapeDtypeStruct((t, w), dt) for w, dt in outs] + [jax.ShapeDtypeStruct(s, F32) for s in accs]
    out_specs = [pl.BlockSpec((r, w), lambda i: (i, 0)) for w, _ in outs] + [pl.BlockSpec(s, zero(len(s))) for s in accs]
    return pl.pallas_call(body, name=name, grid=(t // r,), in_specs=in_specs, out_specs=out_specs, out_shape=out_shape,
                          compiler_params=_cparams(("arbitrary",)))(*rows, *consts)


def _accumulate(ref, val):
    @pl.when(pl.program_id(0) == 0)
    def _():
        ref[...] = jnp.zeros(ref.shape, ref.dtype)

    ref[...] += val


def _matmul(name, a, b, dims, out_dtype, res=None):
    if dims == "nn":
        (m, k), n = a.shape, b.shape[1]
    elif dims == "nt":
        (m, k), n = a.shape, b.shape[0]
    else:
        (k, m), n = a.shape, b.shape[1]
    tm = _tile(m, 640, 16 if dims != "tn" else 128)
    tn = _tile(n, 1408, 128)
    if dims == "nn":
        a_spec, b_spec, dn = pl.BlockSpec((tm, k), lambda i, j: (i, 0)), pl.BlockSpec((k, tn), lambda i, j: (0, j)), NN
    elif dims == "nt":
        a_spec, b_spec, dn = pl.BlockSpec((tm, k), lambda i, j: (i, 0)), pl.BlockSpec((tn, k), lambda i, j: (j, 0)), NT
    else:
        a_spec, b_spec, dn = pl.BlockSpec((k, tm), lambda i, j: (0, i)), pl.BlockSpec((k, tn), lambda i, j: (0, j)), TN
    o_spec = pl.BlockSpec((tm, tn), lambda i, j: (i, j))

    def body(*refs):
        a_ref, b_ref, o_ref = refs[0], refs[1], refs[-1]
        acc = lax.dot_general(a_ref[...].astype(BF16), b_ref[...].astype(BF16), (dn, ((), ())),
                              preferred_element_type=F32)
        if res is not None:
            acc = acc + refs[2][...]
        o_ref[...] = acc.astype(out_dtype)

    ins = [a, b] + ([res] if res is not None else [])
    specs = [a_spec, b_spec] + ([o_spec] if res is not None else [])
    return pl.pallas_call(body, name=name, grid=(m // tm, n // tn), in_specs=specs, out_specs=o_spec,
                          out_shape=jax.ShapeDtypeStruct((m, n), out_dtype),
                          compiler_params=_cparams(("arbitrary", "arbitrary")))(*ins)


def _rms_fwd(name, h, w):
    n = h.shape[1]

    def body(h_ref, w_ref, o_ref):
        o_ref[...] = _rms(h_ref[...], w_ref[...], n).astype(BF16)

    return _rows_call(name, body, [h], [w], [(n, BF16)], [], _row_tile(h.shape[0]))[0]


def _rms_bwd(name, h, w, cts, resid):
    n = h.shape[1]
    nct = len(cts)

    def body(*refs):
        h_ref, ct_refs, r_ref, w_ref, dh_ref, dw_ref = refs[0], refs[1:1 + nct], refs[1 + nct], refs[2 + nct], refs[-2], refs[-1]
        ct = ct_refs[0][...].astype(F32)
        for c in ct_refs[1:]:
            ct = ct + c[...].astype(F32)
        _, vjp = jax.vjp(lambda x, ww: _rms(x, ww, n), h_ref[...], w_ref[...])
        dh, dw = vjp(ct)
        dh_ref[...] = dh + r_ref[...]
        _accumulate(dw_ref, dw)

    return _rows_call(name, body, [h, *cts, resid], [w], [(n, F32)], [(1, n)], _row_tile(h.shape[0]))


def _mla_consts_from_refs(qa, wq, kva, wkv, qn, kn, perm):
    f = lambda r: r[...].astype(F32)
    return dict(
        qa_w=f(qa), kva_w=f(kva), perm=f(perm),
        wq_n=[wq[h * QK_PAD:h * QK_PAD + HEAD, :].astype(F32) for h in range(HEADS)],
        wq_r=[wq[h * QK_PAD + HEAD:(h + 1) * QK_PAD, :].astype(F32) for h in range(HEADS)],
        wk_n=[wkv[h * QK_PAD:h * QK_PAD + HEAD, :].astype(F32) for h in range(HEADS)],
        wv=[wkv[h * QK_PAD + HEAD:(h + 1) * QK_PAD, :].astype(F32) for h in range(HEADS)],
        qn_n=qn[:, 0:HEAD], qn_r=qn[:, HEAD:QK_PAD], kn_n=kn[:, 0:HEAD], kn_r=kn[:, HEAD:QK_PAD])


def _mla_prep_fwd(q_lat, kv_lat, k_pe, cos, sin, qa, wq, kva, wkv, qn, kn, perm):
    def body(ql, kvl, kp, c, s, qa_r, wq_r, kva_r, wkv_r, qn_r, kn_r, p_r, q_out, k_out, v_out):
        consts = _mla_consts_from_refs(qa_r, wq_r, kva_r, wkv_r, qn_r, kn_r, p_r)
        outs = _mla_prep_fn((ql[...], kvl[...], kp[...], c[...], s[...]), consts)
        for h in range(HEADS):
            q_n, q_r, k_n, k_r, v = outs[5 * h:5 * h + 5]
            q_out[:, h * QK_PAD:h * QK_PAD + HEAD] = q_n.astype(BF16)
            q_out[:, h * QK_PAD + HEAD:(h + 1) * QK_PAD] = q_r.astype(BF16)
            k_out[:, h * QK_PAD:h * QK_PAD + HEAD] = k_n.astype(BF16)
            k_out[:, h * QK_PAD + HEAD:(h + 1) * QK_PAD] = k_r.astype(BF16)
            v_out[:, h * HEAD:(h + 1) * HEAD] = v.astype(BF16)

    return _rows_call("mla_prep_fwd", body, [q_lat, kv_lat, k_pe, cos, sin], [qa, wq, kva, wkv, qn, kn, perm],
                      [(HEADS * QK_PAD, BF16), (HEADS * QK_PAD, BF16), (DN_WIDTH, BF16)], [], _row_tile(cos.shape[0]))


def _mla_prep_bwd(q_lat, kv_lat, k_pe, cos, sin, dq, dk, dv, qa, wq, kva, wkv, qn, kn, perm):
    def body(ql, kvl, kp, c, s, dq_r, dk_r, dv_r, qa_r, wq_r, kva_r, wkv_r, qn_r, kn_r, p_r,
             dql, dkvl, dkp, dqa, dwq, dkva, dwkv, dqn, dkn):
        consts = _mla_consts_from_refs(qa_r, wq_r, kva_r, wkv_r, qn_r, kn_r, p_r)
        cc, ss, pm = c[...], s[...], consts.pop("perm")
        _, vjp = jax.vjp(lambda rows, cs: _mla_prep_fn((*rows, cc, ss), dict(cs, perm=pm)), (ql[...], kvl[...], kp[...]),
                         consts)
        cts = []
        for h in range(HEADS):
            cts += [dq_r[:, h * QK_PAD:h * QK_PAD + HEAD], dq_r[:, h * QK_PAD + HEAD:(h + 1) * QK_PAD],
                    dk_r[:, h * QK_PAD:h * QK_PAD + HEAD], dk_r[:, h * QK_PAD + HEAD:(h + 1) * QK_PAD],
                    dv_r[:, h * HEAD:(h + 1) * HEAD]]
        (d_ql, d_kvl, d_kp), dc = vjp(tuple(cts))
        dql[...] = d_ql.astype(BF16)
        dkvl[...] = d_kvl.astype(BF16)
        dkp[...] = d_kp.astype(BF16)
        first = pl.program_id(0) == 0

        def acc(ref, sl, val):
            @pl.when(first)
            def _():
                ref[sl] = val

            @pl.when(jnp.logical_not(first))
            def _():
                ref[sl] += val

        full = (slice(None), slice(None))
        acc(dqa, full, dc["qa_w"])
        acc(dkva, full, dc["kva_w"])
        for h in range(HEADS):
            acc(dwq, (slice(h * QK_PAD, h * QK_PAD + HEAD), slice(None)), dc["wq_n"][h])
            acc(dwq, (slice(h * QK_PAD + HEAD, (h + 1) * QK_PAD), slice(None)), dc["wq_r"][h])
            acc(dwkv, (slice(h * QK_PAD, h * QK_PAD + HEAD), slice(None)), dc["wk_n"][h])
            acc(dwkv, (slice(h * QK_PAD + HEAD, (h + 1) * QK_PAD), slice(None)), dc["wv"][h])
        acc(dqn, (slice(None), slice(0, HEAD)), dc["qn_n"])
        acc(dqn, (slice(None), slice(HEAD, QK_PAD)), dc["qn_r"])
        acc(dkn, (slice(None), slice(0, HEAD)), dc["kn_n"])
        acc(dkn, (slice(None), slice(HEAD, QK_PAD)), dc["kn_r"])

    return _rows_call("mla_prep_bwd", body, [q_lat, kv_lat, k_pe, cos, sin, dq, dk, dv],
                      [qa, wq, kva, wkv, qn, kn, perm],
                      [(LORA, BF16), (LORA, BF16), (HEAD, BF16)],
                      [(1, LORA), wq.shape, (1, LORA), wkv.shape, (1, QK_PAD), (1, QK_PAD)], _row_tile(cos.shape[0]))


ATTN_Q_ROWS = 256


def _attn_blocks(t):
    return [(r0, min(ATTN_Q_ROWS, t - r0)) for r0 in range(0, t, ATTN_Q_ROWS)]


def _attn_fwd(q, k, v):
    t = q.shape[0]

    def body(q_ref, k_ref, v_ref, o_ref):
        for r0, rows in _attn_blocks(t):
            ext = r0 + rows
            o_ref[r0:ext, :] = _attn_fn(q_ref[r0:ext, :], k_ref[0:ext, :], v_ref[0:ext, :], r0)

    return pl.pallas_call(
        body, name="attn_fwd", grid=(HEADS,),
        in_specs=[pl.BlockSpec((t, QK_PAD), lambda h: (0, h)), pl.BlockSpec((t, QK_PAD), lambda h: (0, h)),
                  pl.BlockSpec((t, HEAD), lambda h: (0, h))],
        out_specs=pl.BlockSpec((t, HEAD), lambda h: (0, h)),
        out_shape=jax.ShapeDtypeStruct((t, HEADS * HEAD), F32),
        compiler_params=_cparams(("arbitrary",)))(q, k, v)


def _attn_bwd(q, k, v, do):
    t = q.shape[0]

    def body(q_ref, k_ref, v_ref, do_ref, dq_ref, dk_ref, dv_ref):
        dk_ref[...] = jnp.zeros(dk_ref.shape, F32)
        dv_ref[...] = jnp.zeros(dv_ref.shape, F32)
        for r0, rows in _attn_blocks(t):
            ext = r0 + rows
            _, vjp = jax.vjp(functools.partial(_attn_fn, row0=r0), q_ref[r0:ext, :].astype(F32),
                             k_ref[0:ext, :].astype(F32), v_ref[0:ext, :].astype(F32))
            dq, dk, dv = vjp(do_ref[r0:ext, :])
            dq_ref[r0:ext, :] = dq
            dk_ref[0:ext, :] += dk
            dv_ref[0:ext, :] += dv

    qk_spec = pl.BlockSpec((t, QK_PAD), lambda h: (0, h))
    v_spec = pl.BlockSpec((t, HEAD), lambda h: (0, h))
    return pl.pallas_call(
        body, name="attn_bwd", grid=(HEADS,), in_specs=[qk_spec, qk_spec, v_spec, v_spec],
        out_specs=[qk_spec, qk_spec, v_spec],
        out_shape=[jax.ShapeDtypeStruct((t, HEADS * QK_PAD), F32), jax.ShapeDtypeStruct((t, HEADS * QK_PAD), F32),
                   jax.ShapeDtypeStruct((t, HEADS * HEAD), F32)],
        compiler_params=_cparams(("arbitrary",)))(q, k, v, do)


def _mix_out_fwd(o_mla, o_dn, z, w_mla, w_dn):
    def body(om_ref, od_ref, z_ref, wm_ref, wd_ref, o_ref):
        for h in range(HEADS):
            sl = slice(h * HEAD, (h + 1) * HEAD)
            o_ref[:, sl] = _rms(om_ref[:, sl], wm_ref[...], HEAD).astype(BF16)
            o_ref[:, DN_WIDTH + h * HEAD:DN_WIDTH + (h + 1) * HEAD] = _dn_out_fn(od_ref[:, sl], z_ref[:, sl],
                                                                                 wd_ref[...]).astype(BF16)

    return _rows_call("mix_out_fwd", body, [o_mla, o_dn, z], [w_mla, w_dn], [(2 * DN_WIDTH, BF16)], [],
                      _row_tile(o_mla.shape[0]))[0]


def _mix_out_bwd(o_mla, o_dn, z, dmixed, w_mla, w_dn):
    def body(om_ref, od_ref, z_ref, dm_ref, wm_ref, wd_ref, dom_ref, dod_ref, dz_ref, dwm_ref, dwd_ref):
        dwm = dwd = None
        for h in range(HEADS):
            sl = slice(h * HEAD, (h + 1) * HEAD)
            _, vjp = jax.vjp(lambda o, w: _rms(o, w, HEAD), om_ref[:, sl], wm_ref[...])
            do, dw = vjp(dm_ref[:, sl])
            dom_ref[:, sl] = do
            dwm = dw if dwm is None else dwm + dw
            _, vjp = jax.vjp(_dn_out_fn, od_ref[:, sl], z_ref[:, sl], wd_ref[...])
            do, dz, dw = vjp(dm_ref[:, DN_WIDTH + h * HEAD:DN_WIDTH + (h + 1) * HEAD])
            dod_ref[:, sl] = do
            dz_ref[:, sl] = dz.astype(BF16)
            dwd = dw if dwd is None else dwd + dw
        _accumulate(dwm_ref, dwm)
        _accumulate(dwd_ref, dwd)

    return _rows_call("mix_out_bwd", body, [o_mla, o_dn, z, dmixed], [w_mla, w_dn],
                      [(DN_WIDTH, F32), (DN_WIDTH, F32), (DN_WIDTH, BF16)], [(1, HEAD), (1, HEAD)],
                      _row_tile(o_mla.shape[0]))


def _shift_down(x, s):
    if s == 0:
        return x
    rows = lax.broadcasted_iota(jnp.int32, x.shape, 0)
    return jnp.where(rows >= s, pltpu.roll(x, s, 0), 0.0)


def _shift_up(x, s):
    if s == 0:
        return x
    t = x.shape[0]
    rows = lax.broadcasted_iota(jnp.int32, x.shape, 0)
    return jnp.where(rows < t - s, pltpu.roll(x, t - s, 0), 0.0)


def _col_call(name, body, cols, taps, outs, tap_outs, cw):
    t, c = cols[0].shape[0], taps[0].shape[1]
    in_specs = [pl.BlockSpec((t, cw), lambda j: (0, j)) for _ in cols]
    in_specs += [pl.BlockSpec((a.shape[0], cw), lambda j: (0, j)) for a in taps]
    out_shape = [jax.ShapeDtypeStruct((t, c), dt) for dt in outs] + [jax.ShapeDtypeStruct((n, c), F32) for n in tap_outs]
    out_specs = [pl.BlockSpec((t, cw), lambda j: (0, j)) for _ in outs]
    out_specs += [pl.BlockSpec((n, cw), lambda j: (0, j)) for n in tap_outs]
    return pl.pallas_call(body, name=name, grid=(c // cw,), in_specs=in_specs, out_specs=out_specs, out_shape=out_shape,
                          compiler_params=_cparams(("arbitrary",)))(*cols, *taps)


def _causal_conv(x, w_ref, width):
    acc = w_ref[width - 1:width, :] * x
    for j in range(width - 1):
        acc = acc + w_ref[j:j + 1, :] * _shift_down(x, width - 1 - j)
    return acc


def _causal_conv_bwd(x, dpre, w_ref, dx_ref, dw_ref, width):
    dx = w_ref[width - 1:width, :] * dpre
    dw_ref[width - 1:width, :] = jnp.sum(dpre * x, axis=0, keepdims=True)
    for j in range(width - 1):
        s = width - 1 - j
        dx = dx + w_ref[j:j + 1, :] * _shift_up(dpre, s)
        dw_ref[j:j + 1, :] = jnp.sum(dpre * _shift_down(x, s), axis=0, keepdims=True)
    dx_ref[...] = dx.astype(dx_ref.dtype)


def _dsilu(x):
    sg = jax.nn.sigmoid(x)
    return sg * (1.0 + x * (1.0 - sg))


def _dn_conv_fwd(x, w):
    def body(x_ref, w_ref, y_ref):
        y_ref[...] = _silu(_causal_conv(x_ref[...], w_ref, 4))

    return _col_call("dn_conv_fwd", body, [x], [w], [F32], [], 256)[0]


def _dn_conv_bwd(x, w, dy):
    def body(x_ref, dy_ref, w_ref, dx_ref, dw_ref):
        xv = x_ref[...]
        dpre = dy_ref[...] * _dsilu(_causal_conv(xv, w_ref, 4))
        _causal_conv_bwd(xv, dpre, w_ref, dx_ref, dw_ref, 4)

    return _col_call("dn_conv_bwd", body, [x, dy], [w], [BF16], [4], 256)


def _glu_fwd(gpre, up, w, b):
    def body(g_ref, u_ref, w_ref, b_ref, a_ref):
        gate = _causal_conv(g_ref[...], w_ref, 3) + b_ref[...]
        a_ref[...] = (_silu(gate) * u_ref[...]).astype(BF16)

    return _col_call("glu_fwd", body, [gpre, up], [w, b], [BF16], [], 256)[0]


def _glu_bwd(gpre, up, w, b, dact):
    def body(g_ref, u_ref, da_ref, w_ref, b_ref, dg_ref, du_ref, dw_ref, db_ref):
        gv = g_ref[...]
        gate = _causal_conv(gv, w_ref, 3) + b_ref[...]
        da = da_ref[...]
        du_ref[...] = (da * _silu(gate)).astype(BF16)
        dgate = da * u_ref[...] * _dsilu(gate)
        db_ref[...] = jnp.sum(dgate, axis=0, keepdims=True)
        _causal_conv_bwd(gv, dgate, w_ref, dg_ref, dw_ref, 3)

    return _col_call("glu_bwd", body, [gpre, up, dact], [w, b], [BF16, BF16], [3, 1], 256)


def _dn_prep_consts(sa, sb, al, dt):
    return dict(sel_a=sa[...], sel_b=sb[...], alog=al[...], dtb=dt[...])


def _dn_prep_fwd(conv, ab, sel_a, sel_b, alog, dtb):
    def body(c_ref, ab_ref, sa, sb, al, dt, q_out, k_out, g_out, b_out):
        qc = tuple(c_ref[:, h * HEAD:(h + 1) * HEAD] for h in range(HEADS))
        kc = tuple(c_ref[:, DN_WIDTH + h * HEAD:DN_WIDTH + (h + 1) * HEAD] for h in range(HEADS))
        qs, ks, g, beta = _dn_prep_fn((qc, kc, ab_ref[...]), _dn_prep_consts(sa, sb, al, dt))
        for h in range(HEADS):
            q_out[:, h * HEAD:(h + 1) * HEAD] = qs[h]
            k_out[:, h * HEAD:(h + 1) * HEAD] = ks[h]
        g_out[...] = g
        b_out[...] = beta

    return _rows_call("dn_prep_fwd", body, [conv, ab], [sel_a, sel_b, alog, dtb], [(DN_WIDTH, F32)] * 4, [],
                      _row_tile(conv.shape[0]))


def _dn_prep_bwd(conv, ab, dq, dk, dv, dg, db, sel_a, sel_b, alog, dtb):
    def body(c_ref, ab_ref, dq_r, dk_r, dv_r, dg_r, db_r, sa, sb, al, dt, dc_out, dab_out, dal_out, ddt_out):
        qc = tuple(c_ref[:, h * HEAD:(h + 1) * HEAD] for h in range(HEADS))
        kc = tuple(c_ref[:, DN_WIDTH + h * HEAD:DN_WIDTH + (h + 1) * HEAD] for h in range(HEADS))
        consts = _dn_prep_consts(sa, sb, al, dt)
        sel = dict(sel_a=consts["sel_a"], sel_b=consts["sel_b"])
        _, vjp = jax.vjp(lambda rows, ad: _dn_prep_fn(rows, {**sel, **ad}), (qc, kc, ab_ref[...]),
                         dict(alog=consts["alog"], dtb=consts["dtb"]))
        cq = tuple(dq_r[:, h * HEAD:(h + 1) * HEAD] for h in range(HEADS))
        ck = tuple(dk_r[:, h * HEAD:(h + 1) * HEAD] for h in range(HEADS))
        (dqc, dkc, dab), dad = vjp((cq, ck, dg_r[...], db_r[...]))
        for h in range(HEADS):
            dc_out[:, h * HEAD:(h + 1) * HEAD] = dqc[h]
            dc_out[:, DN_WIDTH + h * HEAD:DN_WIDTH + (h + 1) * HEAD] = dkc[h]
        dc_out[:, 2 * DN_WIDTH:3 * DN_WIDTH] = dv_r[...]
        dab_out[...] = dab.astype(BF16)
        _accumulate(dal_out, dad["alog"])
        _accumulate(ddt_out, dad["dtb"])

    return _rows_call("dn_prep_bwd", body, [conv, ab, dq, dk, dv, dg, db], [sel_a, sel_b, alog, dtb],
                      [(3 * DN_WIDTH, F32), (HEAD, BF16)], [(1, DN_WIDTH), (1, DN_WIDTH)], _row_tile(conv.shape[0]))


def _chunk_batch(t):
    nc = t // CHUNK
    return nc // 2 if nc % 2 == 0 else nc


def _dn_chunk_specs(t, nb):
    rows = nb * CHUNK
    hb = lambda h, b: (b, h)
    vb = lambda h, b: (b, 2 * HEADS + h)
    qk_spec = pl.BlockSpec((None, rows, CHUNK), lambda h, b: (h, b, 0))
    blk = pl.BlockSpec((rows, HEAD), hb)
    return rows, blk, pl.BlockSpec((rows, HEAD), vb), qk_spec


def _dn_chunk_fwd(qn, kn, conv, g, beta):
    t = qn.shape[0]
    nb = _chunk_batch(t)
    rows, blk, vblk, qk_spec = _dn_chunk_specs(t, nb)

    def body(q_ref, k_ref, v_ref, g_ref, b_ref, u_o, w_o, qg_o, kd_o, eg_o, qk_o):
        r3 = lambda x: x.reshape(nb, CHUNK, x.shape[-1])
        outs = _dn_chunk_fn(r3(q_ref[...]), r3(k_ref[...]), r3(v_ref[...]), r3(g_ref[...]), r3(g_ref[:, 0:CHUNK]),
                            r3(b_ref[...]))
        for o_ref, val in zip((u_o, w_o, qg_o, kd_o, eg_o, qk_o), outs):
            o_ref[...] = val.reshape(rows, val.shape[-1])

    return pl.pallas_call(
        body, name="dn_chunk_fwd", grid=(HEADS, t // rows), in_specs=[blk, blk, vblk, blk, blk],
        out_specs=[blk] * 5 + [qk_spec],
        out_shape=[jax.ShapeDtypeStruct((t, DN_WIDTH), F32)] * 5 + [jax.ShapeDtypeStruct((HEADS, t, CHUNK), F32)],
        compiler_params=_cparams(("arbitrary", "arbitrary")))(qn, kn, conv, g, beta)


def _dn_chunk_bwd(qn, kn, conv, g, beta, cts):
    t = qn.shape[0]
    nb = _chunk_batch(t)
    rows, blk, vblk, qk_spec = _dn_chunk_specs(t, nb)

    def body(q_ref, k_ref, v_ref, g_ref, b_ref, du, dw, dqg, dkd, deg, dqk, dq_o, dk_o, dv_o, dg_o, db_o):
        r3 = lambda x: x.reshape(nb, CHUNK, x.shape[-1])
        _, vjp = jax.vjp(_dn_chunk_fn, r3(q_ref[...]), r3(k_ref[...]), r3(v_ref[...]), r3(g_ref[...]),
                         r3(g_ref[:, 0:CHUNK]), r3(b_ref[...]))
        dq, dk, dv, dg, dg64, db = vjp(tuple(r3(c[...]) for c in (du, dw, dqg, dkd, deg, dqk)))
        for o_ref, val in zip((dq_o, dk_o, dv_o, dg_o, db_o), (dq, dk, dv, dg, db)):
            o_ref[...] = val.reshape(rows, HEAD)
        dg_o[:, 0:CHUNK] += dg64.reshape(rows, CHUNK)

    return pl.pallas_call(
        body, name="dn_chunk_bwd", grid=(HEADS, t // rows), in_specs=[blk, blk, vblk, blk, blk] + [blk] * 5 + [qk_spec],
        out_specs=[blk] * 5, out_shape=[jax.ShapeDtypeStruct((t, DN_WIDTH), F32)] * 5,
        compiler_params=_cparams(("arbitrary", "arbitrary")))(qn, kn, conv, g, beta, *cts)


def _dn_rec_fwd(u, w, qg, kd, eg, qk):
    t = u.shape[0]
    nc = t // CHUNK
    blk = pl.BlockSpec((CHUNK, DN_WIDTH), lambda c: (c, 0))
    qk_spec = pl.BlockSpec((HEADS, CHUNK, CHUNK), lambda c: (0, c, 0))
    s_spec = pl.BlockSpec((None, DN_WIDTH, HEAD), lambda c: (c, 0, 0))

    def body(u_ref, w_ref, qg_ref, kd_ref, eg_ref, qk_ref, o_ref, sall_ref, s_scr):
        @pl.when(pl.program_id(0) == 0)
        def _():
            s_scr[...] = jnp.zeros(s_scr.shape, F32)

        sall_ref[...] = s_scr[...]
        for h in range(HEADS):
            sl = slice(h * HEAD, (h + 1) * HEAD)
            s_new, o = _dn_rec_fn(s_scr[sl, :], u_ref[:, sl], w_ref[:, sl], qg_ref[:, sl], qk_ref[h], kd_ref[:, sl],
                                  eg_ref[0:1, sl])
            o_ref[:, sl] = o
            s_scr[sl, :] = s_new

    return pl.pallas_call(
        body, name="dn_rec_fwd", grid=(nc,), in_specs=[blk] * 5 + [qk_spec], out_specs=[blk, s_spec],
        out_shape=[jax.ShapeDtypeStruct((t, DN_WIDTH), F32), jax.ShapeDtypeStruct((nc, DN_WIDTH, HEAD), F32)],
        scratch_shapes=[pltpu.VMEM((DN_WIDTH, HEAD), F32)],
        compiler_params=_cparams(("arbitrary",)))(u, w, qg, kd, eg, qk)


def _dn_rec_bwd(u, w, qg, kd, eg, qk, sall, do):
    t = u.shape[0]
    nc = t // CHUNK
    blk = pl.BlockSpec((CHUNK, DN_WIDTH), lambda c: (nc - 1 - c, 0))
    qk_spec = pl.BlockSpec((HEADS, CHUNK, CHUNK), lambda c: (0, nc - 1 - c, 0))
    s_spec = pl.BlockSpec((None, DN_WIDTH, HEAD), lambda c: (nc - 1 - c, 0, 0))

    def body(u_ref, w_ref, qg_ref, kd_ref, eg_ref, qk_ref, s_ref, do_ref, du_o, dw_o, dqg_o, dkd_o, deg_o, dqk_o, ds_scr):
        @pl.when(pl.program_id(0) == 0)
        def _():
            ds_scr[...] = jnp.zeros(ds_scr.shape, F32)

        deg_o[...] = jnp.zeros(deg_o.shape, F32)
        for h in range(HEADS):
            sl = slice(h * HEAD, (h + 1) * HEAD)
            _, vjp = jax.vjp(_dn_rec_fn, s_ref[sl, :], u_ref[:, sl], w_ref[:, sl], qg_ref[:, sl], qk_ref[h],
                             kd_ref[:, sl], eg_ref[0:1, sl])
            ds, du, dw, dqg, dqk, dkd, deg = vjp((ds_scr[sl, :], do_ref[:, sl]))
            du_o[:, sl] = du
            dw_o[:, sl] = dw
            dqg_o[:, sl] = dqg
            dkd_o[:, sl] = dkd
            deg_o[0:1, sl] = deg
            dqk_o[h] = dqk
            ds_scr[sl, :] = ds

    return pl.pallas_call(
        body, name="dn_rec_bwd", grid=(nc,), in_specs=[blk] * 5 + [qk_spec, s_spec, blk],
        out_specs=[blk] * 5 + [qk_spec],
        out_shape=[jax.ShapeDtypeStruct((t, DN_WIDTH), F32)] * 5 + [jax.ShapeDtypeStruct((HEADS, t, CHUNK), F32)],
        scratch_shapes=[pltpu.VMEM((DN_WIDTH, HEAD), F32)],
        compiler_params=_cparams(("arbitrary",)))(u, w, qg, kd, eg, qk, sall, do)


def _loss_call(h2, tgt, n_valid):
    t, n = h2.shape
    r = _row_tile(t)

    def body(h_ref, t_ref, dy_ref, acc_ref):
        rows = pl.program_id(0) * r + lax.broadcasted_iota(jnp.int32, (r, n), 0)
        valid = jnp.logical_and(rows >= N_META, rows < n_valid)
        e = jnp.where(valid, h_ref[...] - t_ref[...], 0.0)
        dy_ref[...] = e * (1.0 / n)
        _accumulate(acc_ref, jnp.sum(e * e, axis=0, keepdims=True))

    return _rows_call("loss", body, [h2, tgt], [], [(n, F32)], [(1, n)], r)


def _adamw_call(name, w, g, m, v):
    rows, cols = w.shape
    tr = _tile(rows, 256, 8)

    def body(w_ref, g_ref, m_ref, v_ref, d_ref, m_out, v_out):
        gv = g_ref[...]
        m2 = ADAM_B1 * m_ref[...] + (1.0 - ADAM_B1) * gv
        v2 = ADAM_B2 * v_ref[...] + (1.0 - ADAM_B2) * (gv * gv)
        m_hat = m2 / (1.0 - ADAM_B1 ** ADAM_STEP)
        v_hat = v2 / (1.0 - ADAM_B2 ** ADAM_STEP)
        d_ref[...] = -ADAM_LR * (m_hat / (jnp.sqrt(v_hat) + ADAM_EPS) + ADAM_WD * w_ref[...])
        m_out[...] = m2
        v_out[...] = v2

    spec = pl.BlockSpec((tr, cols), lambda i: (i, 0))
    return pl.pallas_call(body, name=name, grid=(rows // tr,), in_specs=[spec] * 4, out_specs=[spec] * 3,
                          out_shape=[jax.ShapeDtypeStruct((rows, cols), F32)] * 3,
                          compiler_params=_cparams(("arbitrary",)))(w, g, m, v)


def _rope_tables(t):
    half = ROPE // 2
    inv_freq = ROPE_THETA ** (-jnp.arange(half, dtype=F32) / half)
    ang = jnp.arange(t, dtype=F32)[:, None] * inv_freq[None, :]
    z = jnp.zeros((t, HEAD - ROPE), F32)
    cos = jnp.concatenate([jnp.cos(ang), jnp.cos(ang), z], axis=1)
    sin = jnp.concatenate([jnp.sin(ang), jnp.sin(ang), z], axis=1)
    k = jnp.arange(HEAD)[:, None]
    l = jnp.arange(HEAD)[None, :]
    perm = jnp.where((l < half) & (k == l + half), -1.0, 0.0) + jnp.where((l >= half) & (l < ROPE) & (k == l - half), 1.0, 0.0)
    return cos, sin, perm.astype(F32)


def _win_to_pad(w):
    z = lambda n: jnp.zeros((n, w.shape[1]), w.dtype)
    return jnp.concatenate([w[576:2112], w[2112:2624], w[0:256], w[256:512], w[512:576], z(64), w[2624:2632], z(120)],
                           axis=0)


def _win_from_pad(g):
    return jnp.concatenate([g[2048:2304], g[2304:2560], g[2560:2624], g[0:1536], g[1536:2048], g[2688:2696]], axis=0)


def _qk_to_pad(w):
    w4 = w.reshape(HEADS, QK_DIM, w.shape[-1])
    return jnp.concatenate([w4, jnp.zeros((HEADS, QK_PAD - QK_DIM, w.shape[-1]), w.dtype)], axis=1).reshape(
        HEADS * QK_PAD, w.shape[-1])


def _qk_from_pad(g):
    return g.reshape(HEADS, QK_PAD, g.shape[-1])[:, :QK_DIM].reshape(HEADS * QK_DIM, g.shape[-1])


def _ff_to_pad(a, axis):
    shape = list(a.shape)
    shape[axis:axis + 1] = [N_CHIPS, FF_SHARD]
    a4 = a.reshape(shape)
    shape[axis + 1] = FF_BLOCK - FF_SHARD
    out = jnp.concatenate([a4, jnp.zeros(shape, a.dtype)], axis=axis + 1)
    shape[axis:axis + 2] = [D_FF_P]
    return out.reshape(shape)


def _ff_from_pad(a, axis):
    shape = list(a.shape)
    shape[axis:axis + 1] = [N_CHIPS, FF_BLOCK]
    a4 = lax.slice_in_dim(a.reshape(shape), 0, FF_SHARD, axis=axis + 1)
    shape[axis:axis + 2] = [D_FF]
    return a4.reshape(shape)


def _local_step(x, tgt, wt):
    s = x.shape[0]
    n_valid = N_META + s
    t = -(-n_valid // HEAD) * HEAD
    zpad = jnp.zeros((t - n_valid, D_MODEL), F32)
    h0 = jnp.concatenate([wt["meta_tokens"], x, zpad], axis=0)
    tgt_p = jnp.concatenate([jnp.zeros((N_META, D_MODEL), F32), tgt, zpad], axis=0)
    cos, sin, perm = _rope_tables(t)
    win, wq, wkv = wt["w_in_t"], wt["w_q_t"], wt["w_kv_t"]
    qn_w = jnp.concatenate([wt["q_norm_w"], jnp.zeros((1, QK_PAD - QK_DIM), F32)], axis=1)
    kn_w = jnp.concatenate([wt["k_norm_w"], jnp.zeros((1, QK_PAD - QK_DIM), F32)], axis=1)
    head_id = jnp.arange(DN_WIDTH)[None, :] // HEAD
    lane = jnp.arange(HEAD)[:, None]
    sel_a = (lane == head_id).astype(F32)
    sel_b = (lane == head_id + HEADS).astype(F32)
    alog = jnp.repeat(wt["dn_A_log"], HEAD, axis=1)
    dtb = jnp.repeat(wt["dn_dt_bias"], HEAD, axis=1)
    w_out, w_gate, w_up, w_down = wt["w_out"], wt["w_gate_t"], wt["w_up_t"], wt["w_down"]
    conv_w, conv_b = wt["ffn_conv_w"], wt["ffn_conv_b"]

    u = _rms_fwd("attn_norm_fwd", h0, wt["attn_norm_w"])
    proj = _matmul("in_proj", u, win, "nt", F32)
    z = (proj, DN_WIDTH, 3)
    q_lat, kv_lat, k_pe, ab = (proj, LORA, 8), (proj, LORA, 9), (proj, HEAD, 20), (proj, HEAD, 21)
    mla_consts = (wt["q_a_norm_w"], wq, wt["kv_a_norm_w"], wkv, qn_w, kn_w, perm)
    q, k, v = _mla_prep_fwd(q_lat, kv_lat, k_pe, cos, sin, *mla_consts)
    o_mla = _attn_fwd(q, k, v)
    conv = _dn_conv_fwd(proj, wt["dn_conv_w"])
    dn_consts = (sel_a, sel_b, alog, dtb)
    qn, kn, g, beta = _dn_prep_fwd(conv, ab, *dn_consts)
    cu, cw, cqg, ckd, ceg, cqk = _dn_chunk_fwd(qn, kn, conv, g, beta)
    o_dn, sall = _dn_rec_fwd(cu, cw, cqg, ckd, ceg, cqk)
    mixed = _mix_out_fwd(o_mla, o_dn, z, wt["mla_out_norm_w"], wt["dn_out_norm_w"])
    h1 = _matmul("out_proj", mixed, w_out, "nn", F32, res=h0)
    n2 = _rms_fwd("ffn_norm_fwd", h1, wt["ffn_norm_w"])
    gpre = _matmul("gate_proj", n2, w_gate, "nt", F32)
    up = _matmul("up_proj", n2, w_up, "nt", F32)
    act = _glu_fwd(gpre, up, conv_w, conv_b)
    h2 = _matmul("down_proj", act, w_down, "nn", F32, res=h1)
    dy, sq = _loss_call(h2, tgt_p, n_valid)

    grads = {}
    dy16 = dy.astype(BF16)
    dact = _matmul("down_dx", dy16, w_down, "nt", F32)
    grads["w_down"] = _matmul("down_dw", act, dy16, "tn", F32)
    dgpre, dup, grads["ffn_conv_w"], grads["ffn_conv_b"] = _glu_bwd(gpre, up, conv_w, conv_b, dact)
    grads["w_gate_t"] = _matmul("gate_dw", dgpre, n2, "tn", F32)
    grads["w_up_t"] = _matmul("up_dw", dup, n2, "tn", F32)
    dn2a = _matmul("gate_dx", dgpre, w_gate, "nn", F32)
    dn2b = _matmul("up_dx", dup, w_up, "nn", F32)
    dh1, grads["ffn_norm_w"] = _rms_bwd("ffn_norm_bwd", h1, wt["ffn_norm_w"], [dn2a, dn2b], dy)
    dh1_16 = dh1.astype(BF16)
    dmixed = _matmul("out_dx", dh1_16, w_out, "nt", F32)
    grads["w_out"] = _matmul("out_dw", mixed, dh1_16, "tn", F32)
    do_mla, do_dn, dz, grads["mla_out_norm_w"], grads["dn_out_norm_w"] = _mix_out_bwd(
        o_mla, o_dn, z, dmixed, wt["mla_out_norm_w"], wt["dn_out_norm_w"])
    rec_cts = _dn_rec_bwd(cu, cw, cqg, ckd, ceg, cqk, sall, do_dn)
    dqn, dkn, dv_dn, dg, dbeta = _dn_chunk_bwd(qn, kn, conv, g, beta, rec_cts)
    dconv, dab, dalog, ddtb = _dn_prep_bwd(conv, ab, dqn, dkn, dv_dn, dg, dbeta, *dn_consts)
    grads["dn_A_log"] = jnp.sum(dalog.reshape(HEADS, HEAD), axis=1)[None, :]
    grads["dn_dt_bias"] = jnp.sum(ddtb.reshape(HEADS, HEAD), axis=1)[None, :]
    ddn_pre, grads["dn_conv_w"] = _dn_conv_bwd(proj, wt["dn_conv_w"], dconv)
    dq, dk, dv = _attn_bwd(q, k, v, do_mla)
    dq_lat, dkv_lat, dk_pe, dqa, dwq, dkva, dwkv, dqnw, dknw = _mla_prep_bwd(q_lat, kv_lat, k_pe, cos, sin, dq, dk, dv,
                                                                               *mla_consts)
    grads["q_a_norm_w"], grads["kv_a_norm_w"] = dqa, dkva
    grads["w_q_t"], grads["w_kv_t"] = dwq, dwkv
    grads["q_norm_w"], grads["k_norm_w"] = dqnw[:, :QK_DIM], dknw[:, :QK_DIM]
    dproj = jnp.concatenate([ddn_pre, dz, dq_lat, dkv_lat, dk_pe, dab], axis=1)
    grads["w_in_t"] = _matmul("in_dw", dproj, u, "tn", F32)
    du = _matmul("in_dx", dproj, win, "nn", F32)
    dh0, grads["attn_norm_w"] = _rms_bwd("attn_norm_bwd", h0, wt["attn_norm_w"], [du], dh1)
    grads["meta_tokens"] = dh0[0:N_META]
    return sq, dh0[N_META:n_valid], grads


def _mesh_pos():
    return lax.axis_index("x"), lax.axis_index("y"), lax.axis_index("c")


def _other_chips(x, y):
    return [(1 - x, y), (x, 1 - y), (1 - x, 1 - y)]


def _all_gather(shards):
    any_spec = pl.BlockSpec(memory_space=pl.ANY)
    nbuf = len(shards)

    def body(*refs):
        srcs, dsts, (send_sems, recv_sems) = refs[:nbuf], refs[nbuf:2 * nbuf], refs[2 * nbuf:]
        x, y, c = _mesh_pos()
        p = 2 * x + y
        sibling = (x, y, 1 - c)
        chips = _other_chips(x, y)
        bufs = tuple((s, d, s.shape[0] // 2) for s, d in zip(srcs, dsts))

        def half(ref, rows, which):
            return ref.at[pl.ds(which * rows, rows), :]

        def copy(i, k, src, dst, to):
            return pltpu.make_async_remote_copy(src_ref=src, dst_ref=dst, send_sem=send_sems.at[i, k],
                                                recv_sem=recv_sems.at[i, k], device_id=to, device_id_type=MESH)

        sends = []
        for i, (src, dst, rows) in enumerate(bufs):
            for j, chip in enumerate(chips):
                sends.append(copy(i, j, half(src, rows, c), half(dst.at[p], rows, c), (*chip, c)))
        for cp in sends:
            cp.start()
        passed = []
        for i, (src, dst, rows) in enumerate(bufs):
            for j, (qx, qy) in enumerate(chips):
                block = half(dst.at[2 * qx + qy], rows, c)
                copy(i, j, block, block, (x, y, c)).wait_recv()
                fwd = copy(i, 3 + j, block, block, sibling)
                fwd.start()
                passed.append(fwd)
        for i, (src, dst, rows) in enumerate(bufs):
            for j, (qx, qy) in enumerate(chips):
                block = half(dst.at[2 * qx + qy], rows, 1 - c)
                copy(i, 3 + j, block, block, (x, y, c)).wait_recv()
        for cp in sends + passed:
            cp.wait_send()

    gathered = pl.pallas_call(
        body, name="all_gather_weights", in_specs=[any_spec] * nbuf, out_specs=[any_spec] * nbuf,
        out_shape=[jax.ShapeDtypeStruct((N_CHIPS, *s.shape), s.dtype) for s in shards],
        scratch_shapes=[pltpu.SemaphoreType.DMA((nbuf, 6)), pltpu.SemaphoreType.DMA((nbuf, 6))],
        compiler_params=pltpu.CompilerParams(has_side_effects=True))(*shards)
    p = 2 * lax.axis_index("x") + lax.axis_index("y")
    return [lax.dynamic_update_slice(g, s[None], (p, 0, 0)) for g, s in zip(gathered, shards)]


def _rs_to_sibling(bufs):
    any_spec = pl.BlockSpec(memory_space=pl.ANY)
    nbuf = len(bufs)

    def body(*refs):
        srcs, dsts, (send_sems, recv_sems) = refs[:nbuf], refs[nbuf:2 * nbuf], refs[2 * nbuf:]
        x, y, c = _mesh_pos()
        copies = []
        for i, (src, dst) in enumerate(zip(srcs, dsts)):
            half = src.shape[1] // 2
            copies.append(pltpu.make_async_remote_copy(
                src_ref=src.at[:, pl.ds((1 - c) * half, half), :], dst_ref=dst, send_sem=send_sems.at[i],
                recv_sem=recv_sems.at[i], device_id=(x, y, 1 - c), device_id_type=MESH))
        for cp in copies:
            cp.start()
        for cp in copies:
            cp.wait()

    return pl.pallas_call(
        body, name="rs_sibling_exchange", in_specs=[any_spec] * nbuf, out_specs=[any_spec] * nbuf,
        out_shape=[jax.ShapeDtypeStruct((N_CHIPS, b.shape[1] // 2, b.shape[2]), F32) for b in bufs],
        scratch_shapes=[pltpu.SemaphoreType.DMA((nbuf,)), pltpu.SemaphoreType.DMA((nbuf,))],
        compiler_params=pltpu.CompilerParams(has_side_effects=True))(*bufs)


def _rs_pair_add(name, buf, got, c, out_dtype):
    half, cols = got.shape[1], got.shape[2]

    def body(c_ref, a_ref, b_ref, o_ref):
        o_ref[...] = (a_ref[...] + b_ref[...]).astype(out_dtype)

    return pl.pallas_call(
        body, name=name,
        grid_spec=pltpu.PrefetchScalarGridSpec(
            num_scalar_prefetch=1, grid=(N_CHIPS,),
            in_specs=[pl.BlockSpec((None, half, cols), lambda j, cr: (j, cr[0], 0)),
                      pl.BlockSpec((None, half, cols), lambda j, cr: (j, 0, 0))],
            out_specs=pl.BlockSpec((None, half, cols), lambda j, cr: (j, 0, 0))),
        out_shape=jax.ShapeDtypeStruct(got.shape, out_dtype),
        compiler_params=_cparams(("arbitrary",)))(c, buf, got)


def _rs_to_chips(accs):
    any_spec = pl.BlockSpec(memory_space=pl.ANY)
    nbuf = len(accs)

    def body(*refs):
        srcs, dsts, (send_sems, recv_sems) = refs[:nbuf], refs[nbuf:2 * nbuf], refs[2 * nbuf:]
        x, y, c = _mesh_pos()
        copies = []
        for i, (src, dst) in enumerate(zip(srcs, dsts)):
            for k, (qx, qy) in enumerate(_other_chips(x, y)):
                copies.append(pltpu.make_async_remote_copy(
                    src_ref=src.at[2 * qx + qy], dst_ref=dst.at[k], send_sem=send_sems.at[i, k],
                    recv_sem=recv_sems.at[i, k], device_id=(qx, qy, c), device_id_type=MESH))
        for cp in copies:
            cp.start()
        for cp in copies:
            cp.wait()

    return pl.pallas_call(
        body, name="rs_chip_exchange", in_specs=[any_spec] * nbuf, out_specs=[any_spec] * nbuf,
        out_shape=[jax.ShapeDtypeStruct((3, a.shape[1], a.shape[2]), a.dtype) for a in accs],
        scratch_shapes=[pltpu.SemaphoreType.DMA((nbuf, 3)), pltpu.SemaphoreType.DMA((nbuf, 3))],
        compiler_params=pltpu.CompilerParams(has_side_effects=True))(*accs)


def _rs_chip_add(name, acc, got, p):
    half, cols = acc.shape[1], acc.shape[2]
    tr = _tile(half, 128, 8)
    slot = (0, 1, 0, 2)

    def body(p_ref, own_ref, g0_ref, g1_ref, g2_ref, o_ref):
        me = p_ref[0]
        gots = (g0_ref, g1_ref, g2_ref)
        total = None
        for chip in range(N_CHIPS):
            val = own_ref[...].astype(F32)
            for e in (1, 2, 3):
                val = jnp.where((chip ^ me) == e, gots[slot[e]][...].astype(F32), val)
            total = val if total is None else total + val
        o_ref[...] = total

    gspec = lambda k: pl.BlockSpec((None, tr, cols), lambda i, pr: (k, i, 0))
    return pl.pallas_call(
        body, name=name,
        grid_spec=pltpu.PrefetchScalarGridSpec(
            num_scalar_prefetch=1, grid=(half // tr,),
            in_specs=[pl.BlockSpec((None, tr, cols), lambda i, pr: (pr[0], i, 0)), gspec(0), gspec(1), gspec(2)],
            out_specs=pl.BlockSpec((tr, cols), lambda i, pr: (i, 0))),
        out_shape=jax.ShapeDtypeStruct((half, cols), F32),
        compiler_params=_cparams(("arbitrary",)))(p, acc, got, got, got)


def _rs_share(ress):
    any_spec = pl.BlockSpec(memory_space=pl.ANY)
    nbuf = len(ress)

    def body(*refs):
        srcs, dsts, (send_sems, recv_sems) = refs[:nbuf], refs[nbuf:2 * nbuf], refs[2 * nbuf:]
        x, y, c = _mesh_pos()
        copies = [pltpu.make_async_remote_copy(src_ref=src, dst_ref=dst, send_sem=send_sems.at[i], recv_sem=recv_sems.at[i],
                                               device_id=(x, y, 1 - c), device_id_type=MESH)
                  for i, (src, dst) in enumerate(zip(srcs, dsts))]
        for cp in copies:
            cp.start()
        for cp in copies:
            cp.wait()

    gots = pl.pallas_call(
        body, name="rs_sibling_share", in_specs=[any_spec] * nbuf, out_specs=[any_spec] * nbuf,
        out_shape=[jax.ShapeDtypeStruct(r.shape, F32) for r in ress],
        scratch_shapes=[pltpu.SemaphoreType.DMA((nbuf,)), pltpu.SemaphoreType.DMA((nbuf,))],
        compiler_params=pltpu.CompilerParams(has_side_effects=True))(*ress)
    south = lax.axis_index("c") == 0
    return [jnp.concatenate([jnp.where(south, r, g), jnp.where(south, g, r)], axis=0) for r, g in zip(ress, gots)]


def _reduce_scatter(names, bufs):
    x, y, c = _mesh_pos()
    ci = jnp.reshape(c, (1,)).astype(jnp.int32)
    pi = jnp.reshape(2 * x + y, (1,)).astype(jnp.int32)
    gots = _rs_to_sibling(bufs)
    accs = [_rs_pair_add("rs_pair_add_" + n, b, g, ci, F32 if n == "small" else BF16) for n, b, g in zip(names, bufs, gots)]
    gots2 = _rs_to_chips(accs)
    ress = [_rs_chip_add("rs_chip_add_" + n, a, g, pi) for n, a, g in zip(names, accs, gots2)]
    return _rs_share(ress)


def _pad_rows(flat, rows):
    return jnp.concatenate([flat, jnp.zeros((rows * LANES - flat.shape[0],), flat.dtype)]).reshape(rows, LANES)


def _unshard(g4, shape, axis):
    a = g4.reshape(N_CHIPS, *shape)
    if axis == 0:
        return a.reshape(N_CHIPS * shape[0], shape[1])
    return jnp.transpose(a, (1, 0, 2)).reshape(shape[0], N_CHIPS * shape[1])


def _shard4(full, shape, axis):
    if axis == 0:
        return full.reshape(N_CHIPS, shape[0] * shape[1])
    a = full.reshape(shape[0], N_CHIPS, shape[1])
    return jnp.transpose(a, (1, 0, 2)).reshape(N_CHIPS, shape[0] * shape[1])


def _pad_axis0(a, rows):
    return jnp.concatenate([a, jnp.zeros((rows - a.shape[0], *a.shape[1:]), a.dtype)], axis=0)


def _pad_axis1(a, rows):
    return jnp.concatenate([a, jnp.zeros((a.shape[0], rows - a.shape[1], *a.shape[2:]), a.dtype)], axis=1)


def _shard_to_strip(name, w):
    _, (shape, axis, rows) = name, {n: (s, ax, r) for n, s, ax, r in BIG}[name]
    w2 = w.reshape(shape).astype(BF16)
    return _pad_axis0(w2.T if axis == 1 else w2, rows)


def _strips_to_local(gathered):
    g = gathered
    win = g["w_in"][:, :IN_SHARD].reshape(IN_COLS, D_MODEL)
    return dict(
        w_in_t=_win_to_pad(win), w_q_t=_qk_to_pad(g["w_q_b"].reshape(HEADS * QK_DIM, LORA)),
        w_kv_t=g["w_kv_b"].reshape(HEADS * QK_PAD, LORA), w_out=g["w_out"].reshape(D_MODEL, D_MODEL),
        w_gate_t=g["w_gate"].reshape(D_FF_P, D_MODEL), w_up_t=g["w_up"].reshape(D_FF_P, D_MODEL),
        w_down=g["w_down"].reshape(D_FF_P, D_MODEL))


def _local_to_strips(g):
    win = _pad_axis1(_win_from_pad(g["w_in_t"]).reshape(N_CHIPS, IN_SHARD, D_MODEL), IN_SHARD_P)
    return dict(
        w_in=win, w_q_b=_qk_from_pad(g["w_q_t"]).reshape(N_CHIPS, QK_DIM, LORA),
        w_kv_b=g["w_kv_t"].reshape(N_CHIPS, QK_PAD, LORA), w_out=g["w_out"].reshape(N_CHIPS, LORA, D_MODEL),
        w_gate=g["w_gate_t"].reshape(N_CHIPS, FF_BLOCK, D_MODEL), w_up=g["w_up_t"].reshape(N_CHIPS, FF_BLOCK, D_MODEL),
        w_down=g["w_down"].reshape(N_CHIPS, FF_BLOCK, D_MODEL))


def _strip_to_shard(name, strip):
    shape, axis = {n: (s, ax) for n, s, ax, _ in BIG}[name]
    rows = shape[axis]
    return strip[:rows].T if axis == 1 else strip[:rows]


def kernel(x, meta_tokens, attn_norm_w, w_in, q_a_norm_w, w_q_b, kv_a_norm_w, w_kv_b, q_norm_w, k_norm_w, mla_out_norm_w, dn_conv_w, dn_A_log, dn_dt_bias, dn_out_norm_w, w_out, ffn_norm_w, w_gate, w_up, ffn_conv_w, ffn_conv_b, w_down, loss_target, m_meta_tokens, m_attn_norm_w, m_w_in, m_q_a_norm_w, m_w_q_b, m_kv_a_norm_w, m_w_kv_b, m_q_norm_w, m_k_norm_w, m_mla_out_norm_w, m_dn_conv_w, m_dn_A_log, m_dn_dt_bias, m_dn_out_norm_w, m_w_out, m_ffn_norm_w, m_w_gate, m_w_up, m_ffn_conv_w, m_ffn_conv_b, m_w_down, v_meta_tokens, v_attn_norm_w, v_w_in, v_q_a_norm_w, v_w_q_b, v_kv_a_norm_w, v_w_kv_b, v_q_norm_w, v_k_norm_w, v_mla_out_norm_w, v_dn_conv_w, v_dn_A_log, v_dn_dt_bias, v_dn_out_norm_w, v_w_out, v_ffn_norm_w, v_w_gate, v_w_up, v_ffn_conv_w, v_ffn_conv_b, v_w_down):
    local = dict(locals())
    w = {n: local[n] for n in WEIGHTS}
    m = {n: local["m_" + n] for n in WEIGHTS}
    v = {n: local["v_" + n] for n in WEIGHTS}
    big_names = [n for n, _, _, _ in BIG]

    wf = _pad_rows(jnp.concatenate([w[n].reshape(-1) for n, _, _ in SMALL_SHARDED]), SMALL_ROWS)
    gathered = _all_gather([_shard_to_strip(n, w[n]) for n in big_names] + [wf])
    full = _strips_to_local(dict(zip(big_names, gathered[:-1])))
    gf = gathered[-1].reshape(N_CHIPS, -1)
    off = 0
    for n, s, ax in SMALL_SHARDED:
        full[n] = _unshard(gf[:, off:off + s[0] * s[1]], s, ax)
        off += s[0] * s[1]
    for n, _ in REPLICATED:
        full[n] = w[n]
    full["ffn_conv_w"] = _ff_to_pad(full["ffn_conv_w"], 1)
    full["ffn_conv_b"] = _ff_to_pad(full["ffn_conv_b"], 1)

    sq, grad_x, g = _local_step(x[0], loss_target[0], full)
    loss = lax.psum(0.5 / D_MODEL * jnp.sum(sq), ("x", "y", "c"))
    g["ffn_conv_w"] = _ff_from_pad(g["ffn_conv_w"], 1)
    g["ffn_conv_b"] = _ff_from_pad(g["ffn_conv_b"], 1)

    rep = jnp.concatenate([g[n].reshape(-1) for n, _ in REPLICATED])
    rep = jnp.concatenate([rep, jnp.zeros((REP_ROWS * LANES - rep.shape[0],), F32)])
    small = jnp.concatenate([_shard4(g[n], s, ax) for n, s, ax in SMALL_SHARDED], axis=1)
    small = jnp.concatenate([small, jnp.zeros((N_CHIPS, SMALL_ROWS * LANES - small.shape[1]), F32),
                             jnp.broadcast_to(rep, (N_CHIPS, rep.shape[0]))], axis=1)
    strips = _local_to_strips(g)
    reduced = _reduce_scatter(big_names + ["small"],
                              [strips[n] for n in big_names] + [small.reshape(N_CHIPS, SMALL_ROWS + REP_ROWS, LANES)])
    gs = {n: _strip_to_shard(n, r) for n, r in zip(big_names, reduced[:-1])}
    red = reduced[-1].reshape(-1)
    off = 0
    for n, s, _ in SMALL_SHARDED:
        gs[n] = red[off:off + s[0] * s[1]].reshape(s)
        off += s[0] * s[1]
    off = SMALL_ROWS * LANES
    for n, cnt in REPLICATED:
        gs[n] = red[off:off + cnt].reshape(1, cnt)
        off += cnt

    delta, new_m, new_v = {}, {}, {}
    for n, s, _, _ in BIG:
        d2, m2, v2 = _adamw_call("adamw_" + n, w[n].reshape(s), gs[n], m[n].reshape(s), v[n].reshape(s))
        delta[n], new_m[n], new_v[n] = (a.reshape(w[n].shape) for a in (d2, m2, v2))
    small_names = [n for n, _, _ in SMALL_SHARDED] + [n for n, _ in REPLICATED]
    rows = SMALL_ROWS + REP_ROWS
    pack = lambda d: _pad_rows(jnp.concatenate([d[n].reshape(-1) for n in small_names]), rows)
    d2, m2, v2 = _adamw_call("adamw_small", pack(w), pack(gs), pack(m), pack(v))
    off = 0
    for n in small_names:
        cnt = w[n].size
        for dst, src in ((delta, d2), (new_m, m2), (new_v, v2)):
            dst[n] = src.reshape(-1)[off:off + cnt].reshape(w[n].shape)
        off += cnt

    grad_out = [gs[n].reshape(w[n].shape) for n in WEIGHTS]
    return (loss, grad_x[None], *grad_out, *[delta[n] for n in WEIGHTS], *[new_m[n] for n in WEIGHTS],
            *[new_v[n] for n in WEIGHTS])
```

```python
import functools
import math

import jax
import jax.numpy as jnp
from jax import lax
from jax.experimental import pallas as pl
from jax.experimental.pallas import tpu as pltpu

F32 = jnp.float32
BF16 = jnp.bfloat16
HI = lax.Precision.HIGHEST
MESH = pl.DeviceIdType.MESH

N_META = 16
D_MODEL = 1024
HEADS = 4
HEAD = 128
ROPE = 64
QK_DIM = HEAD + ROPE
QK_PAD = 2 * HEAD
LORA = 256
DN_WIDTH = HEADS * HEAD
CHUNK = 64
D_FF = 2816
N_CHIPS = 4
FF_SHARD = D_FF // N_CHIPS
FF_BLOCK = 768
D_FF_P = N_CHIPS * FF_BLOCK
IN_COLS = 2632
IN_SHARD = IN_COLS // N_CHIPS
IN_SHARD_P = 672
IN_PAD = 2816
NORM_EPS = 1e-6
ROPE_THETA = 10000.0
LANES = 512

ADAM_LR, ADAM_B1, ADAM_B2, ADAM_EPS, ADAM_WD, ADAM_STEP = 0.001, 0.9, 0.999, 1e-08, 0.01, 10

VMEM_LIMIT = 56 * 1024 * 1024

BIG = (("w_in", (1024, 658), 1, IN_SHARD_P), ("w_q_b", (256, 192), 1, 192), ("w_kv_b", (256, 256), 1, 256),
       ("w_out", (256, 1024), 0, 256), ("w_gate", (1024, 704), 1, FF_BLOCK), ("w_up", (1024, 704), 1, FF_BLOCK),
       ("w_down", (704, 1024), 0, FF_BLOCK))
SMALL_SHARDED = (("meta_tokens", (16, 256), 1), ("dn_conv_w", (4, 384), 1), ("ffn_conv_w", (3, 704), 1))
REPLICATED = (("attn_norm_w", 1024), ("q_a_norm_w", 256), ("kv_a_norm_w", 256), ("q_norm_w", 192), ("k_norm_w", 192),
              ("mla_out_norm_w", 128), ("dn_A_log", 4), ("dn_dt_bias", 4), ("dn_out_norm_w", 128), ("ffn_norm_w", 1024),
              ("ffn_conv_b", 2816))
WEIGHTS = ("meta_tokens", "attn_norm_w", "w_in", "q_a_norm_w", "w_q_b", "kv_a_norm_w", "w_kv_b", "q_norm_w", "k_norm_w",
           "mla_out_norm_w", "dn_conv_w", "dn_A_log", "dn_dt_bias", "dn_out_norm_w", "w_out", "ffn_norm_w", "w_gate",
           "w_up", "ffn_conv_w", "ffn_conv_b", "w_down")

SMALL_ROWS = 16
REP_ROWS = 16


def _cparams(sem):
    return pltpu.CompilerParams(dimension_semantics=sem, vmem_limit_bytes=VMEM_LIMIT)


class _Exchange:
    def __init__(self, prog, ins, out_shape, nsem):
        self.prog, self.ins, self.out_shape, self.nsem = prog, list(ins), list(out_shape), nsem
        self.outs = None

    def sems(self):
        return [pltpu.SemaphoreType.DMA((self.nsem,)), pltpu.SemaphoreType.DMA((self.nsem,))]

    def run(self, name):
        any_spec = pl.BlockSpec(memory_space=pl.ANY)
        n = len(self.ins)

        def body(*refs):
            start, finish = self.prog(refs[:n], refs[n:-2], refs[-2], refs[-1])
            start()
            finish()

        self.outs = pl.pallas_call(
            body, name=name, in_specs=[any_spec] * n, out_specs=[any_spec] * len(self.out_shape),
            out_shape=self.out_shape, scratch_shapes=self.sems(),
            compiler_params=pltpu.CompilerParams(has_side_effects=True))(*self.ins)
        return self.outs


def _pcall(body, name, grid, in_specs, out_specs, out_shape, args, sem, scratch_shapes=(), host=None):
    single = not isinstance(out_shape, (list, tuple))
    out_specs, out_shape = ([out_specs], [out_shape]) if single else (list(out_specs), list(out_shape))
    if host is None:
        outs = pl.pallas_call(body, name=name, grid=grid, in_specs=list(in_specs), out_specs=out_specs, out_shape=out_shape,
                              scratch_shapes=list(scratch_shapes), compiler_params=_cparams(sem))(*args)
        return outs[0] if single else outs
    any_spec = pl.BlockSpec(memory_space=pl.ANY)
    n_in, n_out, n_scr, nx_in, nx_out = len(in_specs), len(out_specs), len(scratch_shapes), len(host.ins), len(host.out_shape)

    def hosted(*refs):
        c_in, x_in = refs[:n_in], refs[n_in:n_in + nx_in]
        o0 = n_in + nx_in
        c_out, x_out = refs[o0:o0 + n_out], refs[o0 + n_out:o0 + n_out + nx_out]
        s0 = o0 + n_out + nx_out
        start, finish = host.prog(x_in, x_out, refs[s0 + n_scr], refs[s0 + n_scr + 1])
        first = functools.reduce(jnp.logical_and, [pl.program_id(d) == 0 for d in range(len(grid))])
        last = functools.reduce(jnp.logical_and, [pl.program_id(d) == grid[d] - 1 for d in range(len(grid))])
        pl.when(first)(start)
        body(*c_in, *c_out, *refs[s0:s0 + n_scr])
        pl.when(last)(finish)

    outs = pl.pallas_call(
        hosted, name=name, grid=grid, in_specs=list(in_specs) + [any_spec] * nx_in,
        out_specs=out_specs + [any_spec] * nx_out, out_shape=out_shape + host.out_shape,
        scratch_shapes=list(scratch_shapes) + host.sems(),
        compiler_params=pltpu.CompilerParams(dimension_semantics=sem, vmem_limit_bytes=VMEM_LIMIT, has_side_effects=True))(
            *args, *host.ins)
    host.outs = outs[n_out:]
    return outs[0] if single else outs[:n_out]


NN, NT, TN = ((1,), (0,)), ((1,), (1,)), ((0,), (0,))


def _shift_dims(dims, batch):
    if not batch:
        return (dims, ((), ()))
    return (((dims[0][0] + 1,), (dims[1][0] + 1,)), ((0,), (0,)))


def _make_mm(dims, exact, batch=False):
    def raw(a, b, d):
        if exact:
            return lax.dot_general(a.astype(F32), b.astype(F32), _shift_dims(d, batch), precision=HI,
                                   preferred_element_type=F32)
        return lax.dot_general(a.astype(BF16), b.astype(BF16), _shift_dims(d, batch), preferred_element_type=F32)

    @jax.custom_vjp
    def mm(a, b):
        return raw(a, b, dims)

    def fwd(a, b):
        return raw(a, b, dims), (a, b)

    def bwd(res, g):
        a, b = res
        if dims == NN:
            da, db = raw(g, b, NT), raw(a, g, TN)
        elif dims == NT:
            da, db = raw(g, b, NN), raw(g, a, TN)
        else:
            da, db = raw(b, g, NT), raw(a, g, NN)
        return da.astype(a.dtype), db.astype(b.dtype)

    mm.defvjp(fwd, bwd)
    return mm


_mm = _make_mm(NN, False)
_mm_nt = _make_mm(NT, False)
_mm_tn = _make_mm(TN, False)
_mmx = _make_mm(NN, True)
_bmm = _make_mm(NN, False, batch=True)
_bmm_nt = _make_mm(NT, False, batch=True)
_bmmx = _make_mm(NN, True, batch=True)
_bmmx_nt = _make_mm(NT, True, batch=True)
_bmmx_tn = _make_mm(TN, True, batch=True)


@jax.custom_vjp
def _unit_lower_inv(a):
    n = a.shape[-1]
    eye = (lax.broadcasted_iota(jnp.int32, a.shape, 1) == lax.broadcasted_iota(jnp.int32, a.shape, 2)).astype(F32)
    x = -a
    t = eye + x
    for _ in range(max(n.bit_length() - 2, 0)):
        x = _bmmx(x, x)
        t = t + _bmmx(t, x)
    return t


def _unit_lower_inv_fwd(a):
    t = _unit_lower_inv(a)
    return t, t


def _unit_lower_inv_bwd(t, g):
    return (-_bmmx_tn(t, _bmmx_nt(g, t)),)


_unit_lower_inv.defvjp(_unit_lower_inv_fwd, _unit_lower_inv_bwd)


def _rms(x, w, n):
    ms = jnp.sum(x * x, axis=-1, keepdims=True) * (1.0 / n)
    return x * lax.rsqrt(ms + NORM_EPS) * w


def _silu(x):
    return x * jax.nn.sigmoid(x)


def _softplus(x):
    return jnp.maximum(x, 0.0) + jnp.log(1.0 + jnp.exp(-jnp.abs(x)))


def _rope(x, cos, sin, perm):
    return x * cos + _mmx(x, perm) * sin


def _mla_prep_fn(rows, consts):
    q_lat, kv_lat, k_pe, cos, sin = rows
    qn = _rms(q_lat, consts["qa_w"], LORA)
    kvn = _rms(kv_lat, consts["kva_w"], LORA)
    outs = []
    for h in range(HEADS):
        q_n = _mm_nt(qn, consts["wq_n"][h])
        q_r = _mm_nt(qn, consts["wq_r"][h])
        rs = lax.rsqrt((jnp.sum(q_n * q_n, -1, keepdims=True) + jnp.sum(q_r * q_r, -1, keepdims=True)) * (1.0 / QK_DIM)
                       + NORM_EPS)
        q_n = q_n * rs * consts["qn_n"]
        q_r = _rope(q_r * rs * consts["qn_r"], cos, sin, consts["perm"])
        k_n = _mm_nt(kvn, consts["wk_n"][h])
        v = _mm_nt(kvn, consts["wv"][h])
        rk = lax.rsqrt((jnp.sum(k_n * k_n, -1, keepdims=True) + jnp.sum(k_pe * k_pe, -1, keepdims=True)) * (1.0 / QK_DIM)
                       + NORM_EPS)
        k_n = k_n * rk * consts["kn_n"]
        k_r = _rope(k_pe * rk * consts["kn_r"], cos, sin, consts["perm"])
        outs += [q_n, q_r, k_n, k_r, v]
    return tuple(outs)


def _attn_fn(q, k, v, row0):
    s = _mm_nt(q, k) * (1.0 / math.sqrt(QK_DIM))
    qpos = row0 + lax.broadcasted_iota(jnp.int32, s.shape, 0)
    kpos = lax.broadcasted_iota(jnp.int32, s.shape, 1)
    s = jnp.where(kpos <= qpos, s, -1e30)
    m = lax.stop_gradient(jnp.max(s, axis=-1, keepdims=True))
    p = jnp.exp(s - m)
    p = p / jnp.sum(p, axis=-1, keepdims=True)
    return _mm(p, v)


def _dn_prep_fn(rows, consts):
    qc, kc, ab = rows
    a_b = _mmx(ab, consts["sel_a"])
    b_b = _mmx(ab, consts["sel_b"])
    beta = jax.nn.sigmoid(b_b)
    g = -jnp.exp(consts["alog"]) * _softplus(a_b + consts["dtb"])
    qs, ks = [], []
    for h in range(HEADS):
        q, k = qc[h], kc[h]
        qs.append(q * lax.rsqrt(jnp.sum(q * q, -1, keepdims=True) + NORM_EPS))
        ks.append(k * lax.rsqrt(jnp.sum(k * k, -1, keepdims=True) + NORM_EPS))
    return tuple(qs), tuple(ks), g, beta


def _dn_chunk_fn(q, k, v, gb, g64, bb):
    nb = q.shape[0]
    ri = lax.broadcasted_iota(jnp.int32, (nb, CHUNK, CHUNK), 1)
    ci = lax.broadcasted_iota(jnp.int32, (nb, CHUNK, CHUNK), 2)
    tri = ri >= ci
    strict = ri > ci
    tril = tri.astype(F32)
    eye = (ri == ci).astype(F32)
    ones = jnp.ones((nb, CHUNK, CHUNK), F32)
    gc = _bmmx(tril, gb)
    gc64 = _bmmx(tril, g64)
    grow = _bmmx(ones, eye * gc64)
    diff = gc64 - grow
    decay = jnp.where(tri, jnp.exp(jnp.where(tri, diff, 0.0)), 0.0)
    kb = k * bb
    vb = v * bb
    a = jnp.where(strict, _bmm_nt(kb, k) * decay, 0.0)
    tinv = _unit_lower_inv(a)
    u = _bmm(tinv, vb)
    w = _bmm(tinv, kb * jnp.exp(gc))
    qs = q * (1.0 / math.sqrt(HEAD))
    qk = _bmm_nt(qs, k) * decay
    qg = qs * jnp.exp(gc)
    glast = jnp.sum(gb, axis=1, keepdims=True)
    kdec = k * jnp.exp(glast - gc)
    eg = jnp.broadcast_to(jnp.exp(glast), gb.shape)
    return u, w, qg, kdec, eg, qk


def _dn_rec_fn(s, u, w, qg, qk, kdec, eg):
    v_new = u - _mm(w, s)
    o = _mm(qg, s) + _mm(qk, v_new)
    s_new = s * eg + _mm_tn(kdec, v_new)
    return s_new, o


def _dn_out_fn(o, z, w):
    return _rms(o, w, HEAD) * _silu(z)


def _row_tile(t):
    return t // 8 if (t // 8) % 16 == 0 else t


def _tile(n, pref, unit):
    best = n
    for cand in range(unit, min(n, pref) + 1, unit):
        if n % cand == 0:
            best = cand
    return best if best <= pref else n


def _rows_call(name, body, rows, consts, outs, accs, r, host=None):
    rows = [a if isinstance(a, tuple) else (a, a.shape[1], 0) for a in rows]
    t = rows[0][0].shape[0]
    zero = lambda nd: (lambda i: (0,) * nd)
    in_specs = [pl.BlockSpec((r, w), functools.partial(lambda i, b: (i, b), b=blk)) for _, w, blk in rows]
    rows = [a for a, _, _ in rows]
    in_specs += [pl.BlockSpec(a.shape, zero(a.ndim)) for a in consts]
    out_shape = [jax.ShapeDtypeStruct((t, w), dt) for w, dt in outs] + [jax.ShapeDtypeStruct(s, F32) for s in accs]
    out_specs = [pl.BlockSpec((r, w), lambda i: (i, 0)) for w, _ in outs] + [pl.BlockSpec(s, zero(len(s))) for s in accs]
    return _pcall(body, name, (t // r,), in_specs, out_specs, out_shape, [*rows, *consts], ("arbitrary",), host=host)


def _accumulate(ref, val):
    @pl.when(pl.program_id(0) == 0)
    def _():
        ref[...] = jnp.zeros(ref.shape, ref.dtype)

    ref[...] += val


def _matmul(name, a, b, dims, out_dtype, res=None, host=None):
    if dims == "nn":
        (m, k), n = a.shape, b.shape[1]
    elif dims == "nt":
        (m, k), n = a.shape, b.shape[0]
    else:
        (k, m), n = a.shape, b.shape[1]
    tm = _tile(m, 640, 16 if dims != "tn" else 128)
    tn = _tile(n, 1408, 128)
    if dims == "nn":
        a_spec, b_spec, dn = pl.BlockSpec((tm, k), lambda i, j: (i, 0)), pl.BlockSpec((k, tn), lambda i, j: (0, j)), NN
    elif dims == "nt":
        a_spec, b_spec, dn = pl.BlockSpec((tm, k), lambda i, j: (i, 0)), pl.BlockSpec((tn, k), lambda i, j: (j, 0)), NT
    else:
        a_spec, b_spec, dn = pl.BlockSpec((k, tm), lambda i, j: (0, i)), pl.BlockSpec((k, tn), lambda i, j: (0, j)), TN
    o_spec = pl.BlockSpec((tm, tn), lambda i, j: (i, j))

    def body(*refs):
        a_ref, b_ref, o_ref = refs[0], refs[1], refs[-1]
        acc = lax.dot_general(a_ref[...].astype(BF16), b_ref[...].astype(BF16), (dn, ((), ())),
                              preferred_element_type=F32)
        if res is not None:
            acc = acc + refs[2][...]
        o_ref[...] = acc.astype(out_dtype)

    ins = [a, b] + ([res] if res is not None else [])
    specs = [a_spec, b_spec] + ([o_spec] if res is not None else [])
    return _pcall(body, name, (m // tm, n // tn), specs, o_spec, jax.ShapeDtypeStruct((m, n), out_dtype), ins,
                  ("arbitrary", "arbitrary"), host=host)


def _rms_fwd(name, h, w):
    n = h.shape[1]

    def body(h_ref, w_ref, o_ref):
        o_ref[...] = _rms(h_ref[...], w_ref[...], n).astype(BF16)

    return _rows_call(name, body, [h], [w], [(n, BF16)], [], _row_tile(h.shape[0]))[0]


def _rms_bwd(name, h, w, cts, resid, host=None):
    n = h.shape[1]
    nct = len(cts)

    def body(*refs):
        h_ref, ct_refs, r_ref, w_ref, dh_ref, dw_ref = refs[0], refs[1:1 + nct], refs[1 + nct], refs[2 + nct], refs[-2], refs[-1]
        ct = ct_refs[0][...].astype(F32)
        for c in ct_refs[1:]:
            ct = ct + c[...].astype(F32)
        _, vjp = jax.vjp(lambda x, ww: _rms(x, ww, n), h_ref[...], w_ref[...])
        dh, dw = vjp(ct)
        dh_ref[...] = dh + r_ref[...]
        _accumulate(dw_ref, dw)

    return _rows_call(name, body, [h, *cts, resid], [w], [(n, F32)], [(1, n)], _row_tile(h.shape[0]), host=host)


def _mla_consts_from_refs(qa, wq, kva, wkv, qn, kn, perm):
    f = lambda r: r[...].astype(F32)
    return dict(
        qa_w=f(qa), kva_w=f(kva), perm=f(perm),
        wq_n=[wq[h * QK_PAD:h * QK_PAD + HEAD, :].astype(F32) for h in range(HEADS)],
        wq_r=[wq[h * QK_PAD + HEAD:(h + 1) * QK_PAD, :].astype(F32) for h in range(HEADS)],
        wk_n=[wkv[h * QK_PAD:h * QK_PAD + HEAD, :].astype(F32) for h in range(HEADS)],
        wv=[wkv[h * QK_PAD + HEAD:(h + 1) * QK_PAD, :].astype(F32) for h in range(HEADS)],
        qn_n=qn[:, 0:HEAD], qn_r=qn[:, HEAD:QK_PAD], kn_n=kn[:, 0:HEAD], kn_r=kn[:, HEAD:QK_PAD])


def _mla_prep_fwd(q_lat, kv_lat, k_pe, cos, sin, qa, wq, kva, wkv, qn, kn, perm):
    def body(ql, kvl, kp, c, s, qa_r, wq_r, kva_r, wkv_r, qn_r, kn_r, p_r, q_out, k_out, v_out):
        consts = _mla_consts_from_refs(qa_r, wq_r, kva_r, wkv_r, qn_r, kn_r, p_r)
        outs = _mla_prep_fn((ql[...], kvl[...], kp[...], c[...], s[...]), consts)
        for h in range(HEADS):
            q_n, q_r, k_n, k_r, v = outs[5 * h:5 * h + 5]
            q_out[:, h * QK_PAD:h * QK_PAD + HEAD] = q_n.astype(BF16)
            q_out[:, h * QK_PAD + HEAD:(h + 1) * QK_PAD] = q_r.astype(BF16)
            k_out[:, h * QK_PAD:h * QK_PAD + HEAD] = k_n.astype(BF16)
            k_out[:, h * QK_PAD + HEAD:(h + 1) * QK_PAD] = k_r.astype(BF16)
            v_out[:, h * HEAD:(h + 1) * HEAD] = v.astype(BF16)

    return _rows_call("mla_prep_fwd", body, [q_lat, kv_lat, k_pe, cos, sin], [qa, wq, kva, wkv, qn, kn, perm],
                      [(HEADS * QK_PAD, BF16), (HEADS * QK_PAD, BF16), (DN_WIDTH, BF16)], [], _row_tile(cos.shape[0]))


def _mla_prep_bwd(q_lat, kv_lat, k_pe, cos, sin, dq, dk, dv, qa, wq, kva, wkv, qn, kn, perm, host=None):
    def body(ql, kvl, kp, c, s, dq_r, dk_r, dv_r, qa_r, wq_r, kva_r, wkv_r, qn_r, kn_r, p_r,
             dql, dkvl, dkp, dqa, dwq, dkva, dwkv, dqn, dkn):
        consts = _mla_consts_from_refs(qa_r, wq_r, kva_r, wkv_r, qn_r, kn_r, p_r)
        cc, ss, pm = c[...], s[...], consts.pop("perm")
        _, vjp = jax.vjp(lambda rows, cs: _mla_prep_fn((*rows, cc, ss), dict(cs, perm=pm)), (ql[...], kvl[...], kp[...]),
                         consts)
        cts = []
        for h in range(HEADS):
            cts += [dq_r[:, h * QK_PAD:h * QK_PAD + HEAD], dq_r[:, h * QK_PAD + HEAD:(h + 1) * QK_PAD],
                    dk_r[:, h * QK_PAD:h * QK_PAD + HEAD], dk_r[:, h * QK_PAD + HEAD:(h + 1) * QK_PAD],
                    dv_r[:, h * HEAD:(h + 1) * HEAD]]
        (d_ql, d_kvl, d_kp), dc = vjp(tuple(cts))
        dql[...] = d_ql.astype(BF16)
        dkvl[...] = d_kvl.astype(BF16)
        dkp[...] = d_kp.astype(BF16)
        first = pl.program_id(0) == 0

        def acc(ref, sl, val):
            @pl.when(first)
            def _():
                ref[sl] = val

            @pl.when(jnp.logical_not(first))
            def _():
                ref[sl] += val

        full = (slice(None), slice(None))
        acc(dqa, full, dc["qa_w"])
        acc(dkva, full, dc["kva_w"])
        for h in range(HEADS):
            acc(dwq, (slice(h * QK_PAD, h * QK_PAD + HEAD), slice(None)), dc["wq_n"][h])
            acc(dwq, (slice(h * QK_PAD + HEAD, (h + 1) * QK_PAD), slice(None)), dc["wq_r"][h])
            acc(dwkv, (slice(h * QK_PAD, h * QK_PAD + HEAD), slice(None)), dc["wk_n"][h])
            acc(dwkv, (slice(h * QK_PAD + HEAD, (h + 1) * QK_PAD), slice(None)), dc["wv"][h])
        acc(dqn, (slice(None), slice(0, HEAD)), dc["qn_n"])
        acc(dqn, (slice(None), slice(HEAD, QK_PAD)), dc["qn_r"])
        acc(dkn, (slice(None), slice(0, HEAD)), dc["kn_n"])
        acc(dkn, (slice(None), slice(HEAD, QK_PAD)), dc["kn_r"])

    return _rows_call("mla_prep_bwd", body, [q_lat, kv_lat, k_pe, cos, sin, dq, dk, dv],
                      [qa, wq, kva, wkv, qn, kn, perm],
                      [(LORA, BF16), (LORA, BF16), (HEAD, BF16)],
                      [(1, LORA), wq.shape, (1, LORA), wkv.shape, (1, QK_PAD), (1, QK_PAD)], _row_tile(cos.shape[0]),
                      host=host)


ATTN_Q_ROWS = 256


def _attn_blocks(t):
    return [(r0, min(ATTN_Q_ROWS, t - r0)) for r0 in range(0, t, ATTN_Q_ROWS)]


def _attn_fwd(q, k, v, host=None):
    t = q.shape[0]

    def body(q_ref, k_ref, v_ref, o_ref):
        for r0, rows in _attn_blocks(t):
            ext = r0 + rows
            o_ref[r0:ext, :] = _attn_fn(q_ref[r0:ext, :], k_ref[0:ext, :], v_ref[0:ext, :], r0)

    qk_spec = pl.BlockSpec((t, QK_PAD), lambda h: (0, h))
    v_spec = pl.BlockSpec((t, HEAD), lambda h: (0, h))
    return _pcall(body, "attn_fwd", (HEADS,), [qk_spec, qk_spec, v_spec], v_spec,
                  jax.ShapeDtypeStruct((t, HEADS * HEAD), F32), [q, k, v], ("arbitrary",), host=host)


def _attn_bwd(q, k, v, do, host=None):
    t = q.shape[0]

    def body(q_ref, k_ref, v_ref, do_ref, dq_ref, dk_ref, dv_ref):
        dk_ref[...] = jnp.zeros(dk_ref.shape, F32)
        dv_ref[...] = jnp.zeros(dv_ref.shape, F32)
        for r0, rows in _attn_blocks(t):
            ext = r0 + rows
            _, vjp = jax.vjp(functools.partial(_attn_fn, row0=r0), q_ref[r0:ext, :].astype(F32),
                             k_ref[0:ext, :].astype(F32), v_ref[0:ext, :].astype(F32))
            dq, dk, dv = vjp(do_ref[r0:ext, :])
            dq_ref[r0:ext, :] = dq
            dk_ref[0:ext, :] += dk
            dv_ref[0:ext, :] += dv

    qk_spec = pl.BlockSpec((t, QK_PAD), lambda h: (0, h))
    v_spec = pl.BlockSpec((t, HEAD), lambda h: (0, h))
    return _pcall(body, "attn_bwd", (HEADS,), [qk_spec, qk_spec, v_spec, v_spec], [qk_spec, qk_spec, v_spec],
                  [jax.ShapeDtypeStruct((t, HEADS * QK_PAD), F32), jax.ShapeDtypeStruct((t, HEADS * QK_PAD), F32),
                   jax.ShapeDtypeStruct((t, HEADS * HEAD), F32)], [q, k, v, do], ("arbitrary",), host=host)


def _mix_out_fwd(o_mla, o_dn, z, w_mla, w_dn):
    def body(om_ref, od_ref, z_ref, wm_ref, wd_ref, o_ref):
        for h in range(HEADS):
            sl = slice(h * HEAD, (h + 1) * HEAD)
            o_ref[:, sl] = _rms(om_ref[:, sl], wm_ref[...], HEAD).astype(BF16)
            o_ref[:, DN_WIDTH + h * HEAD:DN_WIDTH + (h + 1) * HEAD] = _dn_out_fn(od_ref[:, sl], z_ref[:, sl],
                                                                                 wd_ref[...]).astype(BF16)

    return _rows_call("mix_out_fwd", body, [o_mla, o_dn, z], [w_mla, w_dn], [(2 * DN_WIDTH, BF16)], [],
                      _row_tile(o_mla.shape[0]))[0]


def _mix_out_bwd(o_mla, o_dn, z, dmixed, w_mla, w_dn):
    def body(om_ref, od_ref, z_ref, dm_ref, wm_ref, wd_ref, dom_ref, dod_ref, dz_ref, dwm_ref, dwd_ref):
        dwm = dwd = None
        for h in range(HEADS):
            sl = slice(h * HEAD, (h + 1) * HEAD)
            _, vjp = jax.vjp(lambda o, w: _rms(o, w, HEAD), om_ref[:, sl], wm_ref[...])
            do, dw = vjp(dm_ref[:, sl])
            dom_ref[:, sl] = do
            dwm = dw if dwm is None else dwm + dw
            _, vjp = jax.vjp(_dn_out_fn, od_ref[:, sl], z_ref[:, sl], wd_ref[...])
            do, dz, dw = vjp(dm_ref[:, DN_WIDTH + h * HEAD:DN_WIDTH + (h + 1) * HEAD])
            dod_ref[:, sl] = do
            dz_ref[:, sl] = dz.astype(BF16)
            dwd = dw if dwd is None else dwd + dw
        _accumulate(dwm_ref, dwm)
        _accumulate(dwd_ref, dwd)

    return _rows_call("mix_out_bwd", body, [o_mla, o_dn, z, dmixed], [w_mla, w_dn],
                      [(DN_WIDTH, F32), (DN_WIDTH, F32), (DN_WIDTH, BF16)], [(1, HEAD), (1, HEAD)],
                      _row_tile(o_mla.shape[0]))


def _shift_down(x, s):
    if s == 0:
        return x
    rows = lax.broadcasted_iota(jnp.int32, x.shape, 0)
    return jnp.where(rows >= s, pltpu.roll(x, s, 0), 0.0)


def _shift_up(x, s):
    if s == 0:
        return x
    t = x.shape[0]
    rows = lax.broadcasted_iota(jnp.int32, x.shape, 0)
    return jnp.where(rows < t - s, pltpu.roll(x, t - s, 0), 0.0)


def _col_call(name, body, cols, taps, outs, tap_outs, cw):
    t, c = cols[0].shape[0], taps[0].shape[1]
    in_specs = [pl.BlockSpec((t, cw), lambda j: (0, j)) for _ in cols]
    in_specs += [pl.BlockSpec((a.shape[0], cw), lambda j: (0, j)) for a in taps]
    out_shape = [jax.ShapeDtypeStruct((t, c), dt) for dt in outs] + [jax.ShapeDtypeStruct((n, c), F32) for n in tap_outs]
    out_specs = [pl.BlockSpec((t, cw), lambda j: (0, j)) for _ in outs]
    out_specs += [pl.BlockSpec((n, cw), lambda j: (0, j)) for n in tap_outs]
    return pl.pallas_call(body, name=name, grid=(c // cw,), in_specs=in_specs, out_specs=out_specs, out_shape=out_shape,
                          compiler_params=_cparams(("arbitrary",)))(*cols, *taps)


def _causal_conv(x, w_ref, width):
    acc = w_ref[width - 1:width, :] * x
    for j in range(width - 1):
        acc = acc + w_ref[j:j + 1, :] * _shift_down(x, width - 1 - j)
    return acc


def _causal_conv_bwd(x, dpre, w_ref, dx_ref, dw_ref, width):
    dx = w_ref[width - 1:width, :] * dpre
    dw_ref[width - 1:width, :] = jnp.sum(dpre * x, axis=0, keepdims=True)
    for j in range(width - 1):
        s = width - 1 - j
        dx = dx + w_ref[j:j + 1, :] * _shift_up(dpre, s)
        dw_ref[j:j + 1, :] = jnp.sum(dpre * _shift_down(x, s), axis=0, keepdims=True)
    dx_ref[...] = dx.astype(dx_ref.dtype)


def _dsilu(x):
    sg = jax.nn.sigmoid(x)
    return sg * (1.0 + x * (1.0 - sg))


def _dn_conv_fwd(x, w):
    def body(x_ref, w_ref, y_ref):
        y_ref[...] = _silu(_causal_conv(x_ref[...], w_ref, 4))

    return _col_call("dn_conv_fwd", body, [x], [w], [F32], [], 256)[0]


def _dn_conv_bwd(x, w, dy):
    def body(x_ref, dy_ref, w_ref, dx_ref, dw_ref):
        xv = x_ref[...]
        dpre = dy_ref[...] * _dsilu(_causal_conv(xv, w_ref, 4))
        _causal_conv_bwd(xv, dpre, w_ref, dx_ref, dw_ref, 4)

    return _col_call("dn_conv_bwd", body, [x, dy], [w], [BF16], [4], 256)


def _glu_fwd(gpre, up, w, b):
    def body(g_ref, u_ref, w_ref, b_ref, a_ref):
        gate = _causal_conv(g_ref[...], w_ref, 3) + b_ref[...]
        a_ref[...] = (_silu(gate) * u_ref[...]).astype(BF16)

    return _col_call("glu_fwd", body, [gpre, up], [w, b], [BF16], [], 256)[0]


def _glu_bwd(gpre, up, w, b, dact):
    def body(g_ref, u_ref, da_ref, w_ref, b_ref, dg_ref, du_ref, dw_ref, db_ref):
        gv = g_ref[...]
        gate = _causal_conv(gv, w_ref, 3) + b_ref[...]
        da = da_ref[...]
        du_ref[...] = (da * _silu(gate)).astype(BF16)
        dgate = da * u_ref[...] * _dsilu(gate)
        db_ref[...] = jnp.sum(dgate, axis=0, keepdims=True)
        _causal_conv_bwd(gv, dgate, w_ref, dg_ref, dw_ref, 3)

    return _col_call("glu_bwd", body, [gpre, up, dact], [w, b], [BF16, BF16], [3, 1], 256)


def _dn_prep_consts(sa, sb, al, dt):
    return dict(sel_a=sa[...], sel_b=sb[...], alog=al[...], dtb=dt[...])


def _dn_prep_fwd(conv, ab, sel_a, sel_b, alog, dtb):
    def body(c_ref, ab_ref, sa, sb, al, dt, q_out, k_out, g_out, b_out):
        qc = tuple(c_ref[:, h * HEAD:(h + 1) * HEAD] for h in range(HEADS))
        kc = tuple(c_ref[:, DN_WIDTH + h * HEAD:DN_WIDTH + (h + 1) * HEAD] for h in range(HEADS))
        qs, ks, g, beta = _dn_prep_fn((qc, kc, ab_ref[...]), _dn_prep_consts(sa, sb, al, dt))
        for h in range(HEADS):
            q_out[:, h * HEAD:(h + 1) * HEAD] = qs[h]
            k_out[:, h * HEAD:(h + 1) * HEAD] = ks[h]
        g_out[...] = g
        b_out[...] = beta

    return _rows_call("dn_prep_fwd", body, [conv, ab], [sel_a, sel_b, alog, dtb], [(DN_WIDTH, F32)] * 4, [],
                      _row_tile(conv.shape[0]))


def _dn_prep_bwd(conv, ab, dq, dk, dv, dg, db, sel_a, sel_b, alog, dtb):
    def body(c_ref, ab_ref, dq_r, dk_r, dv_r, dg_r, db_r, sa, sb, al, dt, dc_out, dab_out, dal_out, ddt_out):
        qc = tuple(c_ref[:, h * HEAD:(h + 1) * HEAD] for h in range(HEADS))
        kc = tuple(c_ref[:, DN_WIDTH + h * HEAD:DN_WIDTH + (h + 1) * HEAD] for h in range(HEADS))
        consts = _dn_prep_consts(sa, sb, al, dt)
        sel = dict(sel_a=consts["sel_a"], sel_b=consts["sel_b"])
        _, vjp = jax.vjp(lambda rows, ad: _dn_prep_fn(rows, {**sel, **ad}), (qc, kc, ab_ref[...]),
                         dict(alog=consts["alog"], dtb=consts["dtb"]))
        cq = tuple(dq_r[:, h * HEAD:(h + 1) * HEAD] for h in range(HEADS))
        ck = tuple(dk_r[:, h * HEAD:(h + 1) * HEAD] for h in range(HEADS))
        (dqc, dkc, dab), dad = vjp((cq, ck, dg_r[...], db_r[...]))
        for h in range(HEADS):
            dc_out[:, h * HEAD:(h + 1) * HEAD] = dqc[h]
            dc_out[:, DN_WIDTH + h * HEAD:DN_WIDTH + (h + 1) * HEAD] = dkc[h]
        dc_out[:, 2 * DN_WIDTH:3 * DN_WIDTH] = dv_r[...]
        dab_out[...] = dab.astype(BF16)
        _accumulate(dal_out, dad["alog"])
        _accumulate(ddt_out, dad["dtb"])

    return _rows_call("dn_prep_bwd", body, [conv, ab, dq, dk, dv, dg, db], [sel_a, sel_b, alog, dtb],
                      [(3 * DN_WIDTH, F32), (HEAD, BF16)], [(1, DN_WIDTH), (1, DN_WIDTH)], _row_tile(conv.shape[0]))


def _chunk_batch(t):
    nc = t // CHUNK
    return nc // 2 if nc % 2 == 0 else nc


def _dn_chunk_specs(t, nb):
    rows = nb * CHUNK
    hb = lambda h, b: (b, h)
    vb = lambda h, b: (b, 2 * HEADS + h)
    qk_spec = pl.BlockSpec((None, rows, CHUNK), lambda h, b: (h, b, 0))
    blk = pl.BlockSpec((rows, HEAD), hb)
    return rows, blk, pl.BlockSpec((rows, HEAD), vb), qk_spec


def _dn_chunk_fwd(qn, kn, conv, g, beta, host=None):
    t = qn.shape[0]
    nb = _chunk_batch(t)
    rows, blk, vblk, qk_spec = _dn_chunk_specs(t, nb)

    def body(q_ref, k_ref, v_ref, g_ref, b_ref, u_o, w_o, qg_o, kd_o, eg_o, qk_o):
        r3 = lambda x: x.reshape(nb, CHUNK, x.shape[-1])
        outs = _dn_chunk_fn(r3(q_ref[...]), r3(k_ref[...]), r3(v_ref[...]), r3(g_ref[...]), r3(g_ref[:, 0:CHUNK]),
                            r3(b_ref[...]))
        for o_ref, val in zip((u_o, w_o, qg_o, kd_o, eg_o, qk_o), outs):
            o_ref[...] = val.reshape(rows, val.shape[-1])

    return _pcall(body, "dn_chunk_fwd", (HEADS, t // rows), [blk, blk, vblk, blk, blk], [blk] * 5 + [qk_spec],
                  [jax.ShapeDtypeStruct((t, DN_WIDTH), F32)] * 5 + [jax.ShapeDtypeStruct((HEADS, t, CHUNK), F32)],
                  [qn, kn, conv, g, beta], ("arbitrary", "arbitrary"), host=host)


def _dn_chunk_bwd(qn, kn, conv, g, beta, cts, host=None):
    t = qn.shape[0]
    nb = _chunk_batch(t)
    rows, blk, vblk, qk_spec = _dn_chunk_specs(t, nb)

    def body(q_ref, k_ref, v_ref, g_ref, b_ref, du, dw, dqg, dkd, deg, dqk, dq_o, dk_o, dv_o, dg_o, db_o):
        r3 = lambda x: x.reshape(nb, CHUNK, x.shape[-1])
        _, vjp = jax.vjp(_dn_chunk_fn, r3(q_ref[...]), r3(k_ref[...]), r3(v_ref[...]), r3(g_ref[...]),
                         r3(g_ref[:, 0:CHUNK]), r3(b_ref[...]))
        dq, dk, dv, dg, dg64, db = vjp(tuple(r3(c[...]) for c in (du, dw, dqg, dkd, deg, dqk)))
        for o_ref, val in zip((dq_o, dk_o, dv_o, dg_o, db_o), (dq, dk, dv, dg, db)):
            o_ref[...] = val.reshape(rows, HEAD)
        dg_o[:, 0:CHUNK] += dg64.reshape(rows, CHUNK)

    return _pcall(body, "dn_chunk_bwd", (HEADS, t // rows), [blk, blk, vblk, blk, blk] + [blk] * 5 + [qk_spec], [blk] * 5,
                  [jax.ShapeDtypeStruct((t, DN_WIDTH), F32)] * 5, [qn, kn, conv, g, beta, *cts], ("arbitrary", "arbitrary"),
                  host=host)


def _dn_rec_fwd(u, w, qg, kd, eg, qk, host=None):
    t = u.shape[0]
    nc = t // CHUNK
    blk = pl.BlockSpec((CHUNK, DN_WIDTH), lambda c: (c, 0))
    qk_spec = pl.BlockSpec((HEADS, CHUNK, CHUNK), lambda c: (0, c, 0))
    s_spec = pl.BlockSpec((None, DN_WIDTH, HEAD), lambda c: (c, 0, 0))

    def body(u_ref, w_ref, qg_ref, kd_ref, eg_ref, qk_ref, o_ref, sall_ref, s_scr):
        @pl.when(pl.program_id(0) == 0)
        def _():
            s_scr[...] = jnp.zeros(s_scr.shape, F32)

        sall_ref[...] = s_scr[...]
        for h in range(HEADS):
            sl = slice(h * HEAD, (h + 1) * HEAD)
            s_new, o = _dn_rec_fn(s_scr[sl, :], u_ref[:, sl], w_ref[:, sl], qg_ref[:, sl], qk_ref[h], kd_ref[:, sl],
                                  eg_ref[0:1, sl])
            o_ref[:, sl] = o
            s_scr[sl, :] = s_new

    return _pcall(body, "dn_rec_fwd", (nc,), [blk] * 5 + [qk_spec], [blk, s_spec],
                  [jax.ShapeDtypeStruct((t, DN_WIDTH), F32), jax.ShapeDtypeStruct((nc, DN_WIDTH, HEAD), F32)],
                  [u, w, qg, kd, eg, qk], ("arbitrary",), scratch_shapes=[pltpu.VMEM((DN_WIDTH, HEAD), F32)], host=host)


def _dn_rec_bwd(u, w, qg, kd, eg, qk, sall, do, host=None):
    t = u.shape[0]
    nc = t // CHUNK
    blk = pl.BlockSpec((CHUNK, DN_WIDTH), lambda c: (nc - 1 - c, 0))
    qk_spec = pl.BlockSpec((HEADS, CHUNK, CHUNK), lambda c: (0, nc - 1 - c, 0))
    s_spec = pl.BlockSpec((None, DN_WIDTH, HEAD), lambda c: (nc - 1 - c, 0, 0))

    def body(u_ref, w_ref, qg_ref, kd_ref, eg_ref, qk_ref, s_ref, do_ref, du_o, dw_o, dqg_o, dkd_o, deg_o, dqk_o, ds_scr):
        @pl.when(pl.program_id(0) == 0)
        def _():
            ds_scr[...] = jnp.zeros(ds_scr.shape, F32)

        deg_o[...] = jnp.zeros(deg_o.shape, F32)
        for h in range(HEADS):
            sl = slice(h * HEAD, (h + 1) * HEAD)
            _, vjp = jax.vjp(_dn_rec_fn, s_ref[sl, :], u_ref[:, sl], w_ref[:, sl], qg_ref[:, sl], qk_ref[h],
                             kd_ref[:, sl], eg_ref[0:1, sl])
            ds, du, dw, dqg, dqk, dkd, deg = vjp((ds_scr[sl, :], do_ref[:, sl]))
            du_o[:, sl] = du
            dw_o[:, sl] = dw
            dqg_o[:, sl] = dqg
            dkd_o[:, sl] = dkd
            deg_o[0:1, sl] = deg
            dqk_o[h] = dqk
            ds_scr[sl, :] = ds

    return _pcall(body, "dn_rec_bwd", (nc,), [blk] * 5 + [qk_spec, s_spec, blk], [blk] * 5 + [qk_spec],
                  [jax.ShapeDtypeStruct((t, DN_WIDTH), F32)] * 5 + [jax.ShapeDtypeStruct((HEADS, t, CHUNK), F32)],
                  [u, w, qg, kd, eg, qk, sall, do], ("arbitrary",), scratch_shapes=[pltpu.VMEM((DN_WIDTH, HEAD), F32)],
                  host=host)


def _loss_call(h2, tgt, n_valid):
    t, n = h2.shape
    r = _row_tile(t)

    def body(h_ref, t_ref, dy_ref, acc_ref):
        rows = pl.program_id(0) * r + lax.broadcasted_iota(jnp.int32, (r, n), 0)
        valid = jnp.logical_and(rows >= N_META, rows < n_valid)
        e = jnp.where(valid, h_ref[...] - t_ref[...], 0.0)
        dy_ref[...] = e * (1.0 / n)
        _accumulate(acc_ref, jnp.sum(e * e, axis=0, keepdims=True))

    return _rows_call("loss", body, [h2, tgt], [], [(n, F32)], [(1, n)], r)


def _adamw_call(name, w, g, m, v):
    rows, cols = w.shape
    tr = _tile(rows, 256, 8)

    def body(w_ref, g_ref, m_ref, v_ref, d_ref, m_out, v_out):
        gv = g_ref[...]
        m2 = ADAM_B1 * m_ref[...] + (1.0 - ADAM_B1) * gv
        v2 = ADAM_B2 * v_ref[...] + (1.0 - ADAM_B2) * (gv * gv)
        m_hat = m2 / (1.0 - ADAM_B1 ** ADAM_STEP)
        v_hat = v2 / (1.0 - ADAM_B2 ** ADAM_STEP)
        d_ref[...] = -ADAM_LR * (m_hat / (jnp.sqrt(v_hat) + ADAM_EPS) + ADAM_WD * w_ref[...])
        m_out[...] = m2
        v_out[...] = v2

    spec = pl.BlockSpec((tr, cols), lambda i: (i, 0))
    return pl.pallas_call(body, name=name, grid=(rows // tr,), in_specs=[spec] * 4, out_specs=[spec] * 3,
                          out_shape=[jax.ShapeDtypeStruct((rows, cols), F32)] * 3,
                          compiler_params=_cparams(("arbitrary",)))(w, g, m, v)


def _rope_tables(t):
    half = ROPE // 2
    inv_freq = ROPE_THETA ** (-jnp.arange(half, dtype=F32) / half)
    ang = jnp.arange(t, dtype=F32)[:, None] * inv_freq[None, :]
    z = jnp.zeros((t, HEAD - ROPE), F32)
    cos = jnp.concatenate([jnp.cos(ang), jnp.cos(ang), z], axis=1)
    sin = jnp.concatenate([jnp.sin(ang), jnp.sin(ang), z], axis=1)
    k = jnp.arange(HEAD)[:, None]
    l = jnp.arange(HEAD)[None, :]
    perm = jnp.where((l < half) & (k == l + half), -1.0, 0.0) + jnp.where((l >= half) & (l < ROPE) & (k == l - half), 1.0, 0.0)
    return cos, sin, perm.astype(F32)


def _win_to_pad(w):
    z = lambda n: jnp.zeros((n, w.shape[1]), w.dtype)
    return jnp.concatenate([w[576:2112], w[2112:2624], w[0:256], w[256:512], w[512:576], z(64), w[2624:2632], z(120)],
                           axis=0)


def _win_from_pad(g):
    return jnp.concatenate([g[2048:2304], g[2304:2560], g[2560:2624], g[0:1536], g[1536:2048], g[2688:2696]], axis=0)


def _qk_to_pad(w):
    w4 = w.reshape(HEADS, QK_DIM, w.shape[-1])
    return jnp.concatenate([w4, jnp.zeros((HEADS, QK_PAD - QK_DIM, w.shape[-1]), w.dtype)], axis=1).reshape(
        HEADS * QK_PAD, w.shape[-1])


def _qk_from_pad(g):
    return g.reshape(HEADS, QK_PAD, g.shape[-1])[:, :QK_DIM].reshape(HEADS * QK_DIM, g.shape[-1])


def _ff_to_pad(a, axis):
    shape = list(a.shape)
    shape[axis:axis + 1] = [N_CHIPS, FF_SHARD]
    a4 = a.reshape(shape)
    shape[axis + 1] = FF_BLOCK - FF_SHARD
    out = jnp.concatenate([a4, jnp.zeros(shape, a.dtype)], axis=axis + 1)
    shape[axis:axis + 2] = [D_FF_P]
    return out.reshape(shape)


def _ff_from_pad(a, axis):
    shape = list(a.shape)
    shape[axis:axis + 1] = [N_CHIPS, FF_BLOCK]
    a4 = lax.slice_in_dim(a.reshape(shape), 0, FF_SHARD, axis=axis + 1)
    shape[axis:axis + 2] = [D_FF]
    return a4.reshape(shape)


class _LocalPlan:
    def __init__(self, wt):
        self.wt, self.grads = wt, {}

    def weight(self, name):
        return self.wt[name]

    def host(self, point):
        return None

    def grad(self, name, value):
        self.grads[name] = value


def _local_step(x, tgt, wt, plan=None):
    plan = _LocalPlan(wt) if plan is None else plan
    s = x.shape[0]
    n_valid = N_META + s
    t = -(-n_valid // HEAD) * HEAD
    zpad = jnp.zeros((t - n_valid, D_MODEL), F32)
    h0 = jnp.concatenate([wt["meta_tokens"], x, zpad], axis=0)
    tgt_p = jnp.concatenate([jnp.zeros((N_META, D_MODEL), F32), tgt, zpad], axis=0)
    cos, sin, perm = _rope_tables(t)
    win, wq, wkv = wt["w_in_t"], wt["w_q_t"], wt["w_kv_t"]
    qn_w = jnp.concatenate([wt["q_norm_w"], jnp.zeros((1, QK_PAD - QK_DIM), F32)], axis=1)
    kn_w = jnp.concatenate([wt["k_norm_w"], jnp.zeros((1, QK_PAD - QK_DIM), F32)], axis=1)
    head_id = jnp.arange(DN_WIDTH)[None, :] // HEAD
    lane = jnp.arange(HEAD)[:, None]
    sel_a = (lane == head_id).astype(F32)
    sel_b = (lane == head_id + HEADS).astype(F32)
    alog = jnp.repeat(wt["dn_A_log"], HEAD, axis=1)
    dtb = jnp.repeat(wt["dn_dt_bias"], HEAD, axis=1)
    conv_w, conv_b = wt["ffn_conv_w"], wt["ffn_conv_b"]

    u = _rms_fwd("attn_norm_fwd", h0, wt["attn_norm_w"])
    proj = _matmul("in_proj", u, win, "nt", F32)
    z = (proj, DN_WIDTH, 3)
    q_lat, kv_lat, k_pe, ab = (proj, LORA, 8), (proj, LORA, 9), (proj, HEAD, 20), (proj, HEAD, 21)
    mla_consts = (wt["q_a_norm_w"], wq, wt["kv_a_norm_w"], wkv, qn_w, kn_w, perm)
    q, k, v = _mla_prep_fwd(q_lat, kv_lat, k_pe, cos, sin, *mla_consts)
    o_mla = _attn_fwd(q, k, v, host=plan.host("attn_fwd"))
    conv = _dn_conv_fwd(proj, wt["dn_conv_w"])
    dn_consts = (sel_a, sel_b, alog, dtb)
    qn, kn, g, beta = _dn_prep_fwd(conv, ab, *dn_consts)
    cu, cw, cqg, ckd, ceg, cqk = _dn_chunk_fwd(qn, kn, conv, g, beta, host=plan.host("dn_chunk_fwd"))
    o_dn, sall = _dn_rec_fwd(cu, cw, cqg, ckd, ceg, cqk, host=plan.host("dn_rec_fwd"))
    mixed = _mix_out_fwd(o_mla, o_dn, z, wt["mla_out_norm_w"], wt["dn_out_norm_w"])
    w_out = plan.weight("w_out")
    h1 = _matmul("out_proj", mixed, w_out, "nn", F32, res=h0)
    n2 = _rms_fwd("ffn_norm_fwd", h1, wt["ffn_norm_w"])
    w_gate, w_up, w_down = plan.weight("w_gate_t"), plan.weight("w_up_t"), plan.weight("w_down")
    gpre = _matmul("gate_proj", n2, w_gate, "nt", F32)
    up = _matmul("up_proj", n2, w_up, "nt", F32)
    act = _glu_fwd(gpre, up, conv_w, conv_b)
    h2 = _matmul("down_proj", act, w_down, "nn", F32, res=h1)
    dy, sq = _loss_call(h2, tgt_p, n_valid)

    grads = {}
    dy16 = dy.astype(BF16)
    dact = _matmul("down_dx", dy16, w_down, "nt", F32)
    plan.grad("w_down", _matmul("down_dw", act, dy16, "tn", F32))
    dgpre, dup, grads["ffn_conv_w"], grads["ffn_conv_b"] = _glu_bwd(gpre, up, conv_w, conv_b, dact)
    plan.grad("w_gate_t", _matmul("gate_dw", dgpre, n2, "tn", F32))
    plan.grad("w_up_t", _matmul("up_dw", dup, n2, "tn", F32))
    dn2a = _matmul("gate_dx", dgpre, w_gate, "nn", F32)
    dn2b = _matmul("up_dx", dup, w_up, "nn", F32)
    dh1, grads["ffn_norm_w"] = _rms_bwd("ffn_norm_bwd", h1, wt["ffn_norm_w"], [dn2a, dn2b], dy)
    dh1_16 = dh1.astype(BF16)
    dmixed = _matmul("out_dx", dh1_16, w_out, "nt", F32)
    plan.grad("w_out", _matmul("out_dw", mixed, dh1_16, "tn", F32))
    do_mla, do_dn, dz, grads["mla_out_norm_w"], grads["dn_out_norm_w"] = _mix_out_bwd(
        o_mla, o_dn, z, dmixed, wt["mla_out_norm_w"], wt["dn_out_norm_w"])
    rec_cts = _dn_rec_bwd(cu, cw, cqg, ckd, ceg, cqk, sall, do_dn, host=plan.host("dn_rec_bwd"))
    dqn, dkn, dv_dn, dg, dbeta = _dn_chunk_bwd(qn, kn, conv, g, beta, rec_cts, host=plan.host("dn_chunk_bwd"))
    dconv, dab, dalog, ddtb = _dn_prep_bwd(conv, ab, dqn, dkn, dv_dn, dg, dbeta, *dn_consts)
    grads["dn_A_log"] = jnp.sum(dalog.reshape(HEADS, HEAD), axis=1)[None, :]
    grads["dn_dt_bias"] = jnp.sum(ddtb.reshape(HEADS, HEAD), axis=1)[None, :]
    ddn_pre, grads["dn_conv_w"] = _dn_conv_bwd(proj, wt["dn_conv_w"], dconv)
    dq, dk, dv = _attn_bwd(q, k, v, do_mla, host=plan.host("attn_bwd"))
    dq_lat, dkv_lat, dk_pe, dqa, dwq, dkva, dwkv, dqnw, dknw = _mla_prep_bwd(
        q_lat, kv_lat, k_pe, cos, sin, dq, dk, dv, *mla_consts, host=plan.host("mla_prep_bwd"))
    grads["q_a_norm_w"], grads["kv_a_norm_w"] = dqa, dkva
    plan.grad("w_q_t", dwq)
    plan.grad("w_kv_t", dwkv)
    grads["q_norm_w"], grads["k_norm_w"] = dqnw[:, :QK_DIM], dknw[:, :QK_DIM]
    dproj = jnp.concatenate([ddn_pre, dz, dq_lat, dkv_lat, dk_pe, dab], axis=1)
    plan.grad("w_in_t", _matmul("in_dw", dproj, u, "tn", F32))
    du = _matmul("in_dx", dproj, win, "nn", F32, host=plan.host("in_dx"))
    dh0, grads["attn_norm_w"] = _rms_bwd("attn_norm_bwd", h0, wt["attn_norm_w"], [du], dh1, host=plan.host("attn_norm_bwd"))
    grads["meta_tokens"] = dh0[0:N_META]
    if isinstance(plan, _LocalPlan):
        grads.update(plan.grads)
    return sq, dh0[N_META:n_valid], grads


def _mesh_pos():
    return lax.axis_index("x"), lax.axis_index("y"), lax.axis_index("c")


def _other_chips(x, y):
    return [(1 - x, y), (x, 1 - y), (1 - x, 1 - y)]


def _remote(src, dst, send_sems, recv_sems, k, to):
    return pltpu.make_async_remote_copy(src_ref=src, dst_ref=dst, send_sem=send_sems.at[k], recv_sem=recv_sems.at[k],
                                        device_id=to, device_id_type=MESH)


def _copies_exchange(make, ins, out_shape, nsem):
    def prog(in_refs, out_refs, send_sems, recv_sems):
        copies = make(in_refs, out_refs, send_sems, recv_sems)

        def start():
            for cp in copies:
                cp.start()

        def finish():
            for cp in copies:
                cp.wait()

        return start, finish

    return _Exchange(prog, ins, out_shape, nsem)


def _all_gather(shards):
    def prog(srcs, dsts, send_sems, recv_sems):
        x, y, c = _mesh_pos()
        p = 2 * x + y
        sibling = (x, y, 1 - c)
        chips = _other_chips(x, y)
        bufs = tuple((s, d, s.shape[0] // 2) for s, d in zip(srcs, dsts))

        def half(ref, rows, which):
            return ref.at[pl.ds(which * rows, rows), :]

        def copy(i, k, src, dst, to):
            return _remote(src, dst, send_sems, recv_sems, 6 * i + k, to)

        sends = [copy(i, j, half(src, rows, c), half(dst.at[p], rows, c), (*chip, c))
                 for i, (src, dst, rows) in enumerate(bufs) for j, chip in enumerate(chips)]

        def start():
            for cp in sends:
                cp.start()

        def finish():
            passed = []
            for i, (src, dst, rows) in enumerate(bufs):
                for j, (qx, qy) in enumerate(chips):
                    block = half(dst.at[2 * qx + qy], rows, c)
                    copy(i, j, block, block, (x, y, c)).wait_recv()
                    fwd = copy(i, 3 + j, block, block, sibling)
                    fwd.start()
                    passed.append(fwd)
            for i, (src, dst, rows) in enumerate(bufs):
                for j, (qx, qy) in enumerate(chips):
                    block = half(dst.at[2 * qx + qy], rows, 1 - c)
                    copy(i, 3 + j, block, block, (x, y, c)).wait_recv()
            for cp in sends + passed:
                cp.wait_send()

        return start, finish

    return _Exchange(prog, shards, [jax.ShapeDtypeStruct((N_CHIPS, *s.shape), s.dtype) for s in shards], 6 * len(shards))


def _gathered(ex):
    p = 2 * lax.axis_index("x") + lax.axis_index("y")
    return [lax.dynamic_update_slice(g, s[None], (p, 0, 0)) for g, s in zip(ex.outs, ex.ins)]


def _rs_to_sibling(bufs):
    def make(srcs, dsts, send_sems, recv_sems):
        x, y, c = _mesh_pos()
        copies = []
        for i, (src, dst) in enumerate(zip(srcs, dsts)):
            half = src.shape[1] // 2
            copies.append(_remote(src.at[:, pl.ds((1 - c) * half, half), :], dst, send_sems, recv_sems, i, (x, y, 1 - c)))
        return copies

    return _copies_exchange(make, bufs, [jax.ShapeDtypeStruct((N_CHIPS, b.shape[1] // 2, b.shape[2]), F32) for b in bufs],
                            len(bufs))


def _rs_pair_add(name, buf, got, c, out_dtype):
    half, cols = got.shape[1], got.shape[2]

    def body(c_ref, a_ref, b_ref, o_ref):
        o_ref[...] = (a_ref[...] + b_ref[...]).astype(out_dtype)

    return pl.pallas_call(
        body, name=name,
        grid_spec=pltpu.PrefetchScalarGridSpec(
            num_scalar_prefetch=1, grid=(N_CHIPS,),
            in_specs=[pl.BlockSpec((None, half, cols), lambda j, cr: (j, cr[0], 0)),
                      pl.BlockSpec((None, half, cols), lambda j, cr: (j, 0, 0))],
            out_specs=pl.BlockSpec((None, half, cols), lambda j, cr: (j, 0, 0))),
        out_shape=jax.ShapeDtypeStruct(got.shape, out_dtype),
        compiler_params=_cparams(("arbitrary",)))(c, buf, got)


def _rs_to_chips(accs):
    def make(srcs, dsts, send_sems, recv_sems):
        x, y, c = _mesh_pos()
        return [_remote(src.at[2 * qx + qy], dst.at[k], send_sems, recv_sems, 3 * i + k, (qx, qy, c))
                for i, (src, dst) in enumerate(zip(srcs, dsts)) for k, (qx, qy) in enumerate(_other_chips(x, y))]

    return _copies_exchange(make, accs, [jax.ShapeDtypeStruct((3, a.shape[1], a.shape[2]), a.dtype) for a in accs],
                            3 * len(accs))


def _rs_chip_add(name, acc, got, p):
    half, cols = acc.shape[1], acc.shape[2]
    tr = _tile(half, 128, 8)
    slot = (0, 1, 0, 2)

    def body(p_ref, own_ref, g0_ref, g1_ref, g2_ref, o_ref):
        me = p_ref[0]
        gots = (g0_ref, g1_ref, g2_ref)
        total = None
        for chip in range(N_CHIPS):
            val = own_ref[...].astype(F32)
            for e in (1, 2, 3):
                val = jnp.where((chip ^ me) == e, gots[slot[e]][...].astype(F32), val)
            total = val if total is None else total + val
        o_ref[...] = total

    gspec = lambda k: pl.BlockSpec((None, tr, cols), lambda i, pr: (k, i, 0))
    return pl.pallas_call(
        body, name=name,
        grid_spec=pltpu.PrefetchScalarGridSpec(
            num_scalar_prefetch=1, grid=(half // tr,),
            in_specs=[pl.BlockSpec((None, tr, cols), lambda i, pr: (pr[0], i, 0)), gspec(0), gspec(1), gspec(2)],
            out_specs=pl.BlockSpec((tr, cols), lambda i, pr: (i, 0))),
        out_shape=jax.ShapeDtypeStruct((half, cols), F32),
        compiler_params=_cparams(("arbitrary",)))(p, acc, got, got, got)


def _rs_share(ress):
    def make(srcs, dsts, send_sems, recv_sems):
        x, y, c = _mesh_pos()
        return [_remote(src, dst, send_sems, recv_sems, i, (x, y, 1 - c)) for i, (src, dst) in enumerate(zip(srcs, dsts))]

    return _copies_exchange(make, ress, [jax.ShapeDtypeStruct(r.shape, F32) for r in ress], len(ress))


def _shared(ex):
    south = lax.axis_index("c") == 0
    return [jnp.concatenate([jnp.where(south, r, g), jnp.where(south, g, r)], axis=0) for r, g in zip(ex.ins, ex.outs)]


def _all_to_all_devices(vec):
    def make(srcs, dsts, send_sems, recv_sems):
        x, y, c = _mesh_pos()
        me = 4 * x + 2 * y + c
        copies = []
        for r in range(1, 8):
            px, py, pc = (1 - x if r & 4 else x), (1 - y if r & 2 else y), (1 - c if r & 1 else c)
            copies.append(_remote(srcs[0], dsts[0].at[me], send_sems, recv_sems, r - 1, (px, py, pc)))
        return copies

    return _copies_exchange(make, [vec], [jax.ShapeDtypeStruct((8, *vec.shape), vec.dtype)], 7)


def _sum_devices(stack):
    def body(s_ref, o_ref):
        total = s_ref[0]
        for d in range(1, 8):
            total = total + s_ref[d]
        o_ref[...] = total

    return pl.pallas_call(body, name="sum_devices", out_shape=jax.ShapeDtypeStruct(stack.shape[1:], F32),
                          compiler_params=pltpu.CompilerParams(vmem_limit_bytes=VMEM_LIMIT))(stack)


def _pad_rows(flat, rows):
    return jnp.concatenate([flat, jnp.zeros((rows * LANES - flat.shape[0],), flat.dtype)]).reshape(rows, LANES)


def _unshard(g4, shape, axis):
    a = g4.reshape(N_CHIPS, *shape)
    if axis == 0:
        return a.reshape(N_CHIPS * shape[0], shape[1])
    return jnp.transpose(a, (1, 0, 2)).reshape(shape[0], N_CHIPS * shape[1])


def _shard4(full, shape, axis):
    if axis == 0:
        return full.reshape(N_CHIPS, shape[0] * shape[1])
    a = full.reshape(shape[0], N_CHIPS, shape[1])
    return jnp.transpose(a, (1, 0, 2)).reshape(N_CHIPS, shape[0] * shape[1])


def _pad_axis0(a, rows):
    return jnp.concatenate([a, jnp.zeros((rows - a.shape[0], *a.shape[1:]), a.dtype)], axis=0)


def _pad_axis1(a, rows):
    return jnp.concatenate([a, jnp.zeros((a.shape[0], rows - a.shape[1], *a.shape[2:]), a.dtype)], axis=1)


def _shard_to_strip(name, w):
    _, (shape, axis, rows) = name, {n: (s, ax, r) for n, s, ax, r in BIG}[name]
    w2 = w.reshape(shape).astype(BF16)
    return _pad_axis0(w2.T if axis == 1 else w2, rows)


LOCAL_NAME = dict(w_in="w_in_t", w_q_b="w_q_t", w_kv_b="w_kv_t", w_out="w_out", w_gate="w_gate_t", w_up="w_up_t",
                  w_down="w_down")


def _strips_to_weight(name, g4):
    if name == "w_in":
        return _win_to_pad(g4[:, :IN_SHARD].reshape(IN_COLS, D_MODEL))
    if name == "w_q_b":
        return _qk_to_pad(g4.reshape(HEADS * QK_DIM, LORA))
    return g4.reshape(N_CHIPS * g4.shape[1], g4.shape[2])


def _grad_to_strips(name, g):
    if name == "w_in":
        return _pad_axis1(_win_from_pad(g).reshape(N_CHIPS, IN_SHARD, D_MODEL), IN_SHARD_P)
    if name == "w_q_b":
        return _qk_from_pad(g).reshape(N_CHIPS, QK_DIM, LORA)
    return g.reshape(N_CHIPS, g.shape[0] // N_CHIPS, g.shape[1])


class _MeshPlan:
    LATE = dict(attn_fwd=("w_up",), dn_chunk_fwd=("w_out", "w_gate"), dn_rec_fwd=("w_down",))
    GROUP_A = ("w_down", "w_gate", "w_up", "w_out")
    GROUP_B = ("w_in", "w_q_b", "w_kv_b")

    def __init__(self, w):
        x, y, c = _mesh_pos()
        self.ci = jnp.reshape(c, (1,)).astype(jnp.int32)
        self.pi = jnp.reshape(2 * x + y, (1,)).astype(jnp.int32)
        self.strip = {n: _shard_to_strip(n, w[n]) for n, _, _, _ in BIG}
        self.gathers, self.weights, self.g, self.acc, self.reduced = {}, {}, {}, {}, {}
        self.sib = self.chip = self.share = None

    def gather_first(self, small):
        names = ("w_in", "w_q_b", "w_kv_b")
        ex = _all_gather([self.strip[n] for n in names] + [small])
        ex.run("all_gather_first")
        got = _gathered(ex)
        for n, g4 in zip(names, got):
            self.weights[LOCAL_NAME[n]] = _strips_to_weight(n, g4)
        return got[-1]

    def weight(self, local_name):
        if local_name not in self.weights:
            for names, ex in self.gathers.values():
                for n, g4 in zip(names, _gathered(ex)):
                    self.weights[LOCAL_NAME[n]] = _strips_to_weight(n, g4)
            self.gathers = {}
        return self.weights[local_name]

    def grad(self, local_name, value):
        name = {v: k for k, v in LOCAL_NAME.items()}[local_name]
        self.g[name] = _grad_to_strips(name, value)

    def _pair_add(self, names, sib):
        for n, got in zip(names, sib.outs):
            self.acc[n] = _rs_pair_add("rs_pair_add_" + n, self.g[n], got, self.ci, BF16)

    def _chip_add(self, names, chip):
        return [_rs_chip_add("rs_chip_add_" + n, self.acc[n], got, self.pi) for n, got in zip(names, chip.outs)]

    def _take_shared(self, names, share):
        for n, strip in zip(names, _shared(share)):
            self.reduced[n] = _strip_to_shard(n, strip)

    def host(self, point):
        a, b = self.GROUP_A, self.GROUP_B
        if point in self.LATE:
            names = self.LATE[point]
            ex = _all_gather([self.strip[n] for n in names])
            self.gathers[point] = (names, ex)
            return ex
        if point == "dn_rec_bwd":
            self.sib = _rs_to_sibling([self.g[n] for n in a])
            return self.sib
        if point == "dn_chunk_bwd":
            self._pair_add(a, self.sib)
            self.chip1 = _rs_to_chips([self.acc[n] for n in a[:2]])
            return self.chip1
        if point == "attn_bwd":
            self.chip2 = _rs_to_chips([self.acc[n] for n in a[2:]])
            return self.chip2
        if point == "mla_prep_bwd":
            ress = self._chip_add(a[:2], self.chip1) + self._chip_add(a[2:], self.chip2)
            self.share = _rs_share(ress)
            return self.share
        if point == "in_dx":
            self._take_shared(a, self.share)
            self.sib = _rs_to_sibling([self.g[n] for n in b])
            return self.sib
        if point == "attn_norm_bwd":
            self._pair_add(b, self.sib)
            self.chip = _rs_to_chips([self.acc[n] for n in b])
            return self.chip
        return None

    def finish(self):
        share = _rs_share(self._chip_add(self.GROUP_B, self.chip))
        share.run("rs_share_last")
        self._take_shared(self.GROUP_B, share)
        return self.reduced


def _strip_to_shard(name, strip):
    shape, axis = {n: (s, ax) for n, s, ax, _ in BIG}[name]
    rows = shape[axis]
    return strip[:rows].T if axis == 1 else strip[:rows]


def kernel(x, meta_tokens, attn_norm_w, w_in, q_a_norm_w, w_q_b, kv_a_norm_w, w_kv_b, q_norm_w, k_norm_w, mla_out_norm_w, dn_conv_w, dn_A_log, dn_dt_bias, dn_out_norm_w, w_out, ffn_norm_w, w_gate, w_up, ffn_conv_w, ffn_conv_b, w_down, loss_target, m_meta_tokens, m_attn_norm_w, m_w_in, m_q_a_norm_w, m_w_q_b, m_kv_a_norm_w, m_w_kv_b, m_q_norm_w, m_k_norm_w, m_mla_out_norm_w, m_dn_conv_w, m_dn_A_log, m_dn_dt_bias, m_dn_out_norm_w, m_w_out, m_ffn_norm_w, m_w_gate, m_w_up, m_ffn_conv_w, m_ffn_conv_b, m_w_down, v_meta_tokens, v_attn_norm_w, v_w_in, v_q_a_norm_w, v_w_q_b, v_kv_a_norm_w, v_w_kv_b, v_q_norm_w, v_k_norm_w, v_mla_out_norm_w, v_dn_conv_w, v_dn_A_log, v_dn_dt_bias, v_dn_out_norm_w, v_w_out, v_ffn_norm_w, v_w_gate, v_w_up, v_ffn_conv_w, v_ffn_conv_b, v_w_down):
    local = dict(locals())
    w = {n: local[n] for n in WEIGHTS}
    m = {n: local["m_" + n] for n in WEIGHTS}
    v = {n: local["v_" + n] for n in WEIGHTS}
    p = 2 * lax.axis_index("x") + lax.axis_index("y")

    plan = _MeshPlan(w)
    wf = _pad_rows(jnp.concatenate([w[n].reshape(-1) for n, _, _ in SMALL_SHARDED]), SMALL_ROWS)
    gf = plan.gather_first(wf).reshape(N_CHIPS, -1)
    full = dict(plan.weights)
    off = 0
    for n, s, ax in SMALL_SHARDED:
        full[n] = _unshard(gf[:, off:off + s[0] * s[1]], s, ax)
        off += s[0] * s[1]
    for n, _ in REPLICATED:
        full[n] = w[n]
    full["ffn_conv_w"] = _ff_to_pad(full["ffn_conv_w"], 1)
    full["ffn_conv_b"] = _ff_to_pad(full["ffn_conv_b"], 1)

    sq, grad_x, g = _local_step(x[0], loss_target[0], full, plan)
    loss = lax.psum(0.5 / D_MODEL * jnp.sum(sq), ("x", "y", "c"))
    g["ffn_conv_w"] = _ff_from_pad(g["ffn_conv_w"], 1)
    g["ffn_conv_b"] = _ff_from_pad(g["ffn_conv_b"], 1)

    small_all = [n for n, _, _ in SMALL_SHARDED] + [n for n, _ in REPLICATED]
    vec = jnp.concatenate([g[n].reshape(-1) for n in small_all])
    vec = _pad_rows(vec, -(-vec.shape[0] // (8 * LANES)) * 8)
    a2a = _all_to_all_devices(vec)
    a2a.run("small_grads_exchange")
    me = 4 * lax.axis_index("x") + 2 * lax.axis_index("y") + lax.axis_index("c")
    red = _sum_devices(lax.dynamic_update_slice(a2a.outs[0], vec[None], (me, 0, 0))).reshape(-1)
    gs = dict(plan.finish())
    off = 0
    for n in small_all:
        tot = red[off:off + g[n].size].reshape(g[n].shape)
        off += g[n].size
        shard = {sn: (s, ax) for sn, s, ax in SMALL_SHARDED}.get(n)
        if shard is not None:
            tot = lax.dynamic_slice_in_dim(tot, p * shard[0][1], shard[0][1], axis=1)
        gs[n] = tot

    delta, new_m, new_v = {}, {}, {}
    for n, s, _, _ in BIG:
        d2, m2, v2 = _adamw_call("adamw_" + n, w[n].reshape(s), gs[n], m[n].reshape(s), v[n].reshape(s))
        delta[n], new_m[n], new_v[n] = (a.reshape(w[n].shape) for a in (d2, m2, v2))
    small_names = [n for n, _, _ in SMALL_SHARDED] + [n for n, _ in REPLICATED]
    rows = SMALL_ROWS + REP_ROWS
    pack = lambda d: _pad_rows(jnp.concatenate([d[n].reshape(-1) for n in small_names]), rows)
    d2, m2, v2 = _adamw_call("adamw_small", pack(w), pack(gs), pack(m), pack(v))
    off = 0
    for n in small_names:
        cnt = w[n].size
        for dst, src in ((delta, d2), (new_m, m2), (new_v, v2)):
            dst[n] = src.reshape(-1)[off:off + cnt].reshape(w[n].shape)
        off += cnt

    grad_out = [gs[n].reshape(w[n].shape) for n in WEIGHTS]
    return (loss, grad_x[None], *grad_out, *[delta[n] for n in WEIGHTS], *[new_m[n] for n in WEIGHTS],
            *[new_v[n] for n in WEIGHTS])
```

```python
import functools
import math

import jax
import jax.numpy as jnp
from jax import lax
from jax.experimental import pallas as pl
from jax.experimental.pallas import tpu as pltpu

F32 = jnp.float32
BF16 = jnp.bfloat16
HI = lax.Precision.HIGHEST
MESH = pl.DeviceIdType.MESH

N_META = 16
D_MODEL = 1024
HEADS = 4
HEAD = 128
ROPE = 64
QK_DIM = HEAD + ROPE
QK_PAD = 2 * HEAD
LORA = 256
DN_WIDTH = HEADS * HEAD
CHUNK = 64
D_FF = 2816
N_CHIPS = 4
FF_SHARD = D_FF // N_CHIPS
FF_BLOCK = 768
D_FF_P = N_CHIPS * FF_BLOCK
IN_COLS = 2632
IN_SHARD = IN_COLS // N_CHIPS
IN_SHARD_P = 672
IN_PAD = 2816
NORM_EPS = 1e-6
ROPE_THETA = 10000.0
LANES = 512

ADAM_LR, ADAM_B1, ADAM_B2, ADAM_EPS, ADAM_WD, ADAM_STEP = 0.001, 0.9, 0.999, 1e-08, 0.01, 10

VMEM_LIMIT = 56 * 1024 * 1024

BIG = (("w_in", (1024, 658), 1, IN_SHARD_P), ("w_q_b", (256, 192), 1, 192), ("w_kv_b", (256, 256), 1, 256),
       ("w_out", (256, 1024), 0, 256), ("w_gate", (1024, 704), 1, FF_BLOCK), ("w_up", (1024, 704), 1, FF_BLOCK),
       ("w_down", (704, 1024), 0, FF_BLOCK))
SMALL_SHARDED = (("meta_tokens", (16, 256), 1), ("dn_conv_w", (4, 384), 1), ("ffn_conv_w", (3, 704), 1))
REPLICATED = (("attn_norm_w", 1024), ("q_a_norm_w", 256), ("kv_a_norm_w", 256), ("q_norm_w", 192), ("k_norm_w", 192),
              ("mla_out_norm_w", 128), ("dn_A_log", 4), ("dn_dt_bias", 4), ("dn_out_norm_w", 128), ("ffn_norm_w", 1024),
              ("ffn_conv_b", 2816))
WEIGHTS = ("meta_tokens", "attn_norm_w", "w_in", "q_a_norm_w", "w_q_b", "kv_a_norm_w", "w_kv_b", "q_norm_w", "k_norm_w",
           "mla_out_norm_w", "dn_conv_w", "dn_A_log", "dn_dt_bias", "dn_out_norm_w", "w_out", "ffn_norm_w", "w_gate",
           "w_up", "ffn_conv_w", "ffn_conv_b", "w_down")

SMALL_ROWS = 16
REP_ROWS = 16


def _cparams(sem):
    return pltpu.CompilerParams(dimension_semantics=sem, vmem_limit_bytes=VMEM_LIMIT)


class _Exchange:
    def __init__(self, prog, ins, out_shape, nsem):
        self.prog, self.ins, self.out_shape, self.nsem = prog, list(ins), list(out_shape), nsem
        self.outs = None

    def sems(self):
        return [pltpu.SemaphoreType.DMA((self.nsem,)), pltpu.SemaphoreType.DMA((self.nsem,))]

    def run(self, name):
        any_spec = pl.BlockSpec(memory_space=pl.ANY)
        n = len(self.ins)

        def body(*refs):
            start, finish = self.prog(refs[:n], refs[n:-2], refs[-2], refs[-1])
            start()
            finish()

        self.outs = pl.pallas_call(
            body, name=name, in_specs=[any_spec] * n, out_specs=[any_spec] * len(self.out_shape),
            out_shape=self.out_shape, scratch_shapes=self.sems(),
            compiler_params=pltpu.CompilerParams(has_side_effects=True))(*self.ins)
        return self.outs


def _pcall(body, name, grid, in_specs, out_specs, out_shape, args, sem, scratch_shapes=(), host=None):
    single = not isinstance(out_shape, (list, tuple))
    out_specs, out_shape = ([out_specs], [out_shape]) if single else (list(out_specs), list(out_shape))
    if host is None:
        outs = pl.pallas_call(body, name=name, grid=grid, in_specs=list(in_specs), out_specs=out_specs, out_shape=out_shape,
                              scratch_shapes=list(scratch_shapes), compiler_params=_cparams(sem))(*args)
        return outs[0] if single else outs
    any_spec = pl.BlockSpec(memory_space=pl.ANY)
    n_in, n_out, n_scr, nx_in, nx_out = len(in_specs), len(out_specs), len(scratch_shapes), len(host.ins), len(host.out_shape)

    def hosted(*refs):
        c_in, x_in = refs[:n_in], refs[n_in:n_in + nx_in]
        o0 = n_in + nx_in
        c_out, x_out = refs[o0:o0 + n_out], refs[o0 + n_out:o0 + n_out + nx_out]
        s0 = o0 + n_out + nx_out
        start, finish = host.prog(x_in, x_out, refs[s0 + n_scr], refs[s0 + n_scr + 1])
        first = functools.reduce(jnp.logical_and, [pl.program_id(d) == 0 for d in range(len(grid))])
        last = functools.reduce(jnp.logical_and, [pl.program_id(d) == grid[d] - 1 for d in range(len(grid))])
        pl.when(first)(start)
        body(*c_in, *c_out, *refs[s0:s0 + n_scr])
        pl.when(last)(finish)

    outs = pl.pallas_call(
        hosted, name=name, grid=grid, in_specs=list(in_specs) + [any_spec] * nx_in,
        out_specs=out_specs + [any_spec] * nx_out, out_shape=out_shape + host.out_shape,
        scratch_shapes=list(scratch_shapes) + host.sems(),
        compiler_params=pltpu.CompilerParams(dimension_semantics=sem, vmem_limit_bytes=VMEM_LIMIT, has_side_effects=True))(
            *args, *host.ins)
    host.outs = outs[n_out:]
    return outs[0] if single else outs[:n_out]


NN, NT, TN = ((1,), (0,)), ((1,), (1,)), ((0,), (0,))


def _shift_dims(dims, batch):
    if not batch:
        return (dims, ((), ()))
    return (((dims[0][0] + 1,), (dims[1][0] + 1,)), ((0,), (0,)))


def _make_mm(dims, exact, batch=False):
    def raw(a, b, d):
        if exact:
            return lax.dot_general(a.astype(F32), b.astype(F32), _shift_dims(d, batch), precision=HI,
                                   preferred_element_type=F32)
        return lax.dot_general(a.astype(BF16), b.astype(BF16), _shift_dims(d, batch), preferred_element_type=F32)

    @jax.custom_vjp
    def mm(a, b):
        return raw(a, b, dims)

    def fwd(a, b):
        return raw(a, b, dims), (a, b)

    def bwd(res, g):
        a, b = res
        if dims == NN:
            da, db = raw(g, b, NT), raw(a, g, TN)
        elif dims == NT:
            da, db = raw(g, b, NN), raw(g, a, TN)
        else:
            da, db = raw(b, g, NT), raw(a, g, NN)
        return da.astype(a.dtype), db.astype(b.dtype)

    mm.defvjp(fwd, bwd)
    return mm


_mm = _make_mm(NN, False)
_mm_nt = _make_mm(NT, False)
_mm_tn = _make_mm(TN, False)
_mmx = _make_mm(NN, True)
_bmm = _make_mm(NN, False, batch=True)
_bmm_nt = _make_mm(NT, False, batch=True)
_bmmx = _make_mm(NN, True, batch=True)
_bmmx_nt = _make_mm(NT, True, batch=True)
_bmmx_tn = _make_mm(TN, True, batch=True)


@jax.custom_vjp
def _unit_lower_inv(a):
    n = a.shape[-1]
    eye = (lax.broadcasted_iota(jnp.int32, a.shape, 1) == lax.broadcasted_iota(jnp.int32, a.shape, 2)).astype(F32)
    x = -a
    t = eye + x
    for _ in range(max(n.bit_length() - 2, 0)):
        x = _bmmx(x, x)
        t = t + _bmmx(t, x)
    return t


def _unit_lower_inv_fwd(a):
    t = _unit_lower_inv(a)
    return t, t


def _unit_lower_inv_bwd(t, g):
    return (-_bmmx_tn(t, _bmmx_nt(g, t)),)


_unit_lower_inv.defvjp(_unit_lower_inv_fwd, _unit_lower_inv_bwd)


def _rms(x, w, n):
    ms = jnp.sum(x * x, axis=-1, keepdims=True) * (1.0 / n)
    return x * lax.rsqrt(ms + NORM_EPS) * w


def _silu(x):
    return x * jax.nn.sigmoid(x)


def _softplus(x):
    return jnp.maximum(x, 0.0) + jnp.log(1.0 + jnp.exp(-jnp.abs(x)))


def _rope(x, cos, sin, perm):
    return x * cos + _mmx(x, perm) * sin


def _mla_prep_fn(rows, consts):
    q_lat, kv_lat, k_pe, cos, sin = rows
    qn = _rms(q_lat, consts["qa_w"], LORA)
    kvn = _rms(kv_lat, consts["kva_w"], LORA)
    outs = []
    for h in range(HEADS):
        q_n = _mm_nt(qn, consts["wq_n"][h])
        q_r = _mm_nt(qn, consts["wq_r"][h])
        rs = lax.rsqrt((jnp.sum(q_n * q_n, -1, keepdims=True) + jnp.sum(q_r * q_r, -1, keepdims=True)) * (1.0 / QK_DIM)
                       + NORM_EPS)
        q_n = q_n * rs * consts["qn_n"]
        q_r = _rope(q_r * rs * consts["qn_r"], cos, sin, consts["perm"])
        k_n = _mm_nt(kvn, consts["wk_n"][h])
        v = _mm_nt(kvn, consts["wv"][h])
        rk = lax.rsqrt((jnp.sum(k_n * k_n, -1, keepdims=True) + jnp.sum(k_pe * k_pe, -1, keepdims=True)) * (1.0 / QK_DIM)
                       + NORM_EPS)
        k_n = k_n * rk * consts["kn_n"]
        k_r = _rope(k_pe * rk * consts["kn_r"], cos, sin, consts["perm"])
        outs += [q_n, q_r, k_n, k_r, v]
    return tuple(outs)


def _attn_fn(q, k, v, row0):
    s = _mm_nt(q, k) * (1.0 / math.sqrt(QK_DIM))
    qpos = row0 + lax.broadcasted_iota(jnp.int32, s.shape, 0)
    kpos = lax.broadcasted_iota(jnp.int32, s.shape, 1)
    s = jnp.where(kpos <= qpos, s, -1e30)
    m = lax.stop_gradient(jnp.max(s, axis=-1, keepdims=True))
    p = jnp.exp(s - m)
    p = p / jnp.sum(p, axis=-1, keepdims=True)
    return _mm(p, v)


def _dn_prep_fn(rows, consts):
    qc, kc, ab = rows
    a_b = _mmx(ab, consts["sel_a"])
    b_b = _mmx(ab, consts["sel_b"])
    beta = jax.nn.sigmoid(b_b)
    g = -jnp.exp(consts["alog"]) * _softplus(a_b + consts["dtb"])
    qs, ks = [], []
    for h in range(HEADS):
        q, k = qc[h], kc[h]
        qs.append(q * lax.rsqrt(jnp.sum(q * q, -1, keepdims=True) + NORM_EPS))
        ks.append(k * lax.rsqrt(jnp.sum(k * k, -1, keepdims=True) + NORM_EPS))
    return tuple(qs), tuple(ks), g, beta


def _dn_chunk_fn(q, k, v, gb, g64, bb):
    nb = q.shape[0]
    ri = lax.broadcasted_iota(jnp.int32, (nb, CHUNK, CHUNK), 1)
    ci = lax.broadcasted_iota(jnp.int32, (nb, CHUNK, CHUNK), 2)
    tri = ri >= ci
    strict = ri > ci
    tril = tri.astype(F32)
    eye = (ri == ci).astype(F32)
    ones = jnp.ones((nb, CHUNK, CHUNK), F32)
    gc = _bmmx(tril, gb)
    gc64 = _bmmx(tril, g64)
    grow = _bmmx(ones, eye * gc64)
    diff = gc64 - grow
    decay = jnp.where(tri, jnp.exp(jnp.where(tri, diff, 0.0)), 0.0)
    kb = k * bb
    vb = v * bb
    a = jnp.where(strict, _bmm_nt(kb, k) * decay, 0.0)
    tinv = _unit_lower_inv(a)
    u = _bmm(tinv, vb)
    w = _bmm(tinv, kb * jnp.exp(gc))
    qs = q * (1.0 / math.sqrt(HEAD))
    qk = _bmm_nt(qs, k) * decay
    qg = qs * jnp.exp(gc)
    glast = jnp.sum(gb, axis=1, keepdims=True)
    kdec = k * jnp.exp(glast - gc)
    eg = jnp.broadcast_to(jnp.exp(glast), gb.shape)
    return u, w, qg, kdec, eg, qk


def _dn_rec_fn(s, u, w, qg, qk, kdec, eg):
    v_new = u - _mm(w, s)
    o = _mm(qg, s) + _mm(qk, v_new)
    s_new = s * eg + _mm_tn(kdec, v_new)
    return s_new, o


def _dn_out_fn(o, z, w):
    return _rms(o, w, HEAD) * _silu(z)


def _row_tile(t):
    return t // 8 if (t // 8) % 16 == 0 else t


def _tile(n, pref, unit):
    best = n
    for cand in range(unit, min(n, pref) + 1, unit):
        if n % cand == 0:
            best = cand
    return best if best <= pref else n


def _rows_call(name, body, rows, consts, outs, accs, r, host=None):
    rows = [a if isinstance(a, tuple) else (a, a.shape[1], 0) for a in rows]
    t = rows[0][0].shape[0]
    zero = lambda nd: (lambda i: (0,) * nd)
    in_specs = [pl.BlockSpec((r, w), functools.partial(lambda i, b: (i, b), b=blk)) for _, w, blk in rows]
    rows = [a for a, _, _ in rows]
    in_specs += [pl.BlockSpec(a.shape, zero(a.ndim)) for a in consts]
    out_shape = [jax.ShapeDtypeStruct((t, w), dt) for w, dt in outs] + [jax.ShapeDtypeStruct(s, F32) for s in accs]
    out_specs = [pl.BlockSpec((r, w), lambda i: (i, 0)) for w, _ in outs] + [pl.BlockSpec(s, zero(len(s))) for s in accs]
    return _pcall(body, name, (t // r,), in_specs, out_specs, out_shape, [*rows, *consts], ("arbitrary",), host=host)


def _accumulate(ref, val):
    @pl.when(pl.program_id(0) == 0)
    def _():
        ref[...] = jnp.zeros(ref.shape, ref.dtype)

    ref[...] += val


def _matmul(name, a, b, dims, out_dtype, res=None, host=None):
    if dims == "nn":
        (m, k), n = a.shape, b.shape[1]
    elif dims == "nt":
        (m, k), n = a.shape, b.shape[0]
    else:
        (k, m), n = a.shape, b.shape[1]
    tm = _tile(m, 640, 16 if dims != "tn" else 128)
    tn = _tile(n, 1408, 128)
    if dims == "nn":
        a_spec, b_spec, dn = pl.BlockSpec((tm, k), lambda i, j: (i, 0)), pl.BlockSpec((k, tn), lambda i, j: (0, j)), NN
    elif dims == "nt":
        a_spec, b_spec, dn = pl.BlockSpec((tm, k), lambda i, j: (i, 0)), pl.BlockSpec((tn, k), lambda i, j: (j, 0)), NT
    else:
        a_spec, b_spec, dn = pl.BlockSpec((k, tm), lambda i, j: (0, i)), pl.BlockSpec((k, tn), lambda i, j: (0, j)), TN
    o_spec = pl.BlockSpec((tm, tn), lambda i, j: (i, j))

    def body(*refs):
        a_ref, b_ref, o_ref = refs[0], refs[1], refs[-1]
        acc = lax.dot_general(a_ref[...].astype(BF16), b_ref[...].astype(BF16), (dn, ((), ())),
                              preferred_element_type=F32)
        if res is not None:
            acc = acc + refs[2][...]
        o_ref[...] = acc.astype(out_dtype)

    ins = [a, b] + ([res] if res is not None else [])
    specs = [a_spec, b_spec] + ([o_spec] if res is not None else [])
    return _pcall(body, name, (m // tm, n // tn), specs, o_spec, jax.ShapeDtypeStruct((m, n), out_dtype), ins,
                  ("arbitrary", "arbitrary"), host=host)


def _rms_fwd(name, h, w):
    n = h.shape[1]

    def body(h_ref, w_ref, o_ref):
        o_ref[...] = _rms(h_ref[...], w_ref[...], n).astype(BF16)

    return _rows_call(name, body, [h], [w], [(n, BF16)], [], _row_tile(h.shape[0]))[0]


def _rms_bwd(name, h, w, cts, resid, host=None):
    n = h.shape[1]
    nct = len(cts)

    def body(*refs):
        h_ref, ct_refs, r_ref, w_ref = refs[0], refs[1:1 + nct], refs[1 + nct], refs[2 + nct]
        dh_ref, dh16_ref, dw_ref = refs[-3], refs[-2], refs[-1]
        ct = ct_refs[0][...].astype(F32)
        for c in ct_refs[1:]:
            ct = ct + c[...].astype(F32)
        _, vjp = jax.vjp(lambda x, ww: _rms(x, ww, n), h_ref[...], w_ref[...])
        dh, dw = vjp(ct)
        dh = dh + r_ref[...]
        dh_ref[...] = dh
        dh16_ref[...] = dh.astype(BF16)
        _accumulate(dw_ref, dw)

    return _rows_call(name, body, [h, *cts, resid], [w], [(n, F32), (n, BF16)], [(1, n)], _row_tile(h.shape[0]), host=host)


def _mla_consts_from_refs(qa, wq, kva, wkv, qn, kn, perm):
    f = lambda r: r[...].astype(F32)
    return dict(
        qa_w=f(qa), kva_w=f(kva), perm=f(perm),
        wq_n=[wq[h * QK_PAD:h * QK_PAD + HEAD, :].astype(F32) for h in range(HEADS)],
        wq_r=[wq[h * QK_PAD + HEAD:(h + 1) * QK_PAD, :].astype(F32) for h in range(HEADS)],
        wk_n=[wkv[h * QK_PAD:h * QK_PAD + HEAD, :].astype(F32) for h in range(HEADS)],
        wv=[wkv[h * QK_PAD + HEAD:(h + 1) * QK_PAD, :].astype(F32) for h in range(HEADS)],
        qn_n=qn[:, 0:HEAD], qn_r=qn[:, HEAD:QK_PAD], kn_n=kn[:, 0:HEAD], kn_r=kn[:, HEAD:QK_PAD])


def _mla_prep_fwd(q_lat, kv_lat, k_pe, cos, sin, qa, wq, kva, wkv, qn, kn, perm):
    def body(ql, kvl, kp, c, s, qa_r, wq_r, kva_r, wkv_r, qn_r, kn_r, p_r, q_out, k_out, v_out):
        consts = _mla_consts_from_refs(qa_r, wq_r, kva_r, wkv_r, qn_r, kn_r, p_r)
        outs = _mla_prep_fn((ql[...], kvl[...], kp[...], c[...], s[...]), consts)
        for h in range(HEADS):
            q_n, q_r, k_n, k_r, v = outs[5 * h:5 * h + 5]
            q_out[:, h * QK_PAD:h * QK_PAD + HEAD] = q_n.astype(BF16)
            q_out[:, h * QK_PAD + HEAD:(h + 1) * QK_PAD] = q_r.astype(BF16)
            k_out[:, h * QK_PAD:h * QK_PAD + HEAD] = k_n.astype(BF16)
            k_out[:, h * QK_PAD + HEAD:(h + 1) * QK_PAD] = k_r.astype(BF16)
            v_out[:, h * HEAD:(h + 1) * HEAD] = v.astype(BF16)

    return _rows_call("mla_prep_fwd", body, [q_lat, kv_lat, k_pe, cos, sin], [qa, wq, kva, wkv, qn, kn, perm],
                      [(HEADS * QK_PAD, BF16), (HEADS * QK_PAD, BF16), (DN_WIDTH, BF16)], [], _row_tile(cos.shape[0]))


def _mla_prep_bwd(q_lat, kv_lat, k_pe, cos, sin, dq, dk, dv, qa, wq, kva, wkv, qn, kn, perm, host=None):
    def body(ql, kvl, kp, c, s, dq_r, dk_r, dv_r, qa_r, wq_r, kva_r, wkv_r, qn_r, kn_r, p_r,
             dql, dkvl, dkp, dqa, dwq, dkva, dwkv, dqn, dkn):
        consts = _mla_consts_from_refs(qa_r, wq_r, kva_r, wkv_r, qn_r, kn_r, p_r)
        cc, ss, pm = c[...], s[...], consts.pop("perm")
        _, vjp = jax.vjp(lambda rows, cs: _mla_prep_fn((*rows, cc, ss), dict(cs, perm=pm)), (ql[...], kvl[...], kp[...]),
                         consts)
        cts = []
        for h in range(HEADS):
            cts += [dq_r[:, h * QK_PAD:h * QK_PAD + HEAD], dq_r[:, h * QK_PAD + HEAD:(h + 1) * QK_PAD],
                    dk_r[:, h * QK_PAD:h * QK_PAD + HEAD], dk_r[:, h * QK_PAD + HEAD:(h + 1) * QK_PAD],
                    dv_r[:, h * HEAD:(h + 1) * HEAD]]
        (d_ql, d_kvl, d_kp), dc = vjp(tuple(cts))
        dql[...] = d_ql.astype(BF16)
        dkvl[...] = d_kvl.astype(BF16)
        dkp[...] = d_kp.astype(BF16)
        first = pl.program_id(0) == 0

        def acc(ref, sl, val):
            @pl.when(first)
            def _():
                ref[sl] = val

            @pl.when(jnp.logical_not(first))
            def _():
                ref[sl] += val

        full = (slice(None), slice(None))
        acc(dqa, full, dc["qa_w"])
        acc(dkva, full, dc["kva_w"])
        for h in range(HEADS):
            acc(dwq, (slice(h * QK_PAD, h * QK_PAD + HEAD), slice(None)), dc["wq_n"][h])
            acc(dwq, (slice(h * QK_PAD + HEAD, (h + 1) * QK_PAD), slice(None)), dc["wq_r"][h])
            acc(dwkv, (slice(h * QK_PAD, h * QK_PAD + HEAD), slice(None)), dc["wk_n"][h])
            acc(dwkv, (slice(h * QK_PAD + HEAD, (h + 1) * QK_PAD), slice(None)), dc["wv"][h])
        acc(dqn, (slice(None), slice(0, HEAD)), dc["qn_n"])
        acc(dqn, (slice(None), slice(HEAD, QK_PAD)), dc["qn_r"])
        acc(dkn, (slice(None), slice(0, HEAD)), dc["kn_n"])
        acc(dkn, (slice(None), slice(HEAD, QK_PAD)), dc["kn_r"])

    return _rows_call("mla_prep_bwd", body, [q_lat, kv_lat, k_pe, cos, sin, dq, dk, dv],
                      [qa, wq, kva, wkv, qn, kn, perm],
                      [(LORA, BF16), (LORA, BF16), (HEAD, BF16)],
                      [(1, LORA), wq.shape, (1, LORA), wkv.shape, (1, QK_PAD), (1, QK_PAD)], _row_tile(cos.shape[0]),
                      host=host)


ATTN_Q_ROWS = 256


def _attn_blocks(t):
    return [(r0, min(ATTN_Q_ROWS, t - r0)) for r0 in range(0, t, ATTN_Q_ROWS)]


def _attn_fwd(q, k, v, host=None):
    t = q.shape[0]

    def body(q_ref, k_ref, v_ref, o_ref):
        for r0, rows in _attn_blocks(t):
            ext = r0 + rows
            o_ref[r0:ext, :] = _attn_fn(q_ref[r0:ext, :], k_ref[0:ext, :], v_ref[0:ext, :], r0)

    qk_spec = pl.BlockSpec((t, QK_PAD), lambda h: (0, h))
    v_spec = pl.BlockSpec((t, HEAD), lambda h: (0, h))
    return _pcall(body, "attn_fwd", (HEADS,), [qk_spec, qk_spec, v_spec], v_spec,
                  jax.ShapeDtypeStruct((t, HEADS * HEAD), F32), [q, k, v], ("arbitrary",), host=host)


def _attn_bwd(q, k, v, do, host=None):
    t = q.shape[0]

    def body(q_ref, k_ref, v_ref, do_ref, dq_ref, dk_ref, dv_ref):
        dk_ref[...] = jnp.zeros(dk_ref.shape, F32)
        dv_ref[...] = jnp.zeros(dv_ref.shape, F32)
        for r0, rows in _attn_blocks(t):
            ext = r0 + rows
            _, vjp = jax.vjp(functools.partial(_attn_fn, row0=r0), q_ref[r0:ext, :].astype(F32),
                             k_ref[0:ext, :].astype(F32), v_ref[0:ext, :].astype(F32))
            dq, dk, dv = vjp(do_ref[r0:ext, :])
            dq_ref[r0:ext, :] = dq
            dk_ref[0:ext, :] += dk
            dv_ref[0:ext, :] += dv

    qk_spec = pl.BlockSpec((t, QK_PAD), lambda h: (0, h))
    v_spec = pl.BlockSpec((t, HEAD), lambda h: (0, h))
    return _pcall(body, "attn_bwd", (HEADS,), [qk_spec, qk_spec, v_spec, v_spec], [qk_spec, qk_spec, v_spec],
                  [jax.ShapeDtypeStruct((t, HEADS * QK_PAD), F32), jax.ShapeDtypeStruct((t, HEADS * QK_PAD), F32),
                   jax.ShapeDtypeStruct((t, HEADS * HEAD), F32)], [q, k, v, do], ("arbitrary",), host=host)


def _mix_out_fwd(o_mla, o_dn, z, w_mla, w_dn):
    def body(om_ref, od_ref, z_ref, wm_ref, wd_ref, o_ref):
        for h in range(HEADS):
            sl = slice(h * HEAD, (h + 1) * HEAD)
            o_ref[:, sl] = _rms(om_ref[:, sl], wm_ref[...], HEAD).astype(BF16)
            o_ref[:, DN_WIDTH + h * HEAD:DN_WIDTH + (h + 1) * HEAD] = _dn_out_fn(od_ref[:, sl], z_ref[:, sl],
                                                                                 wd_ref[...]).astype(BF16)

    return _rows_call("mix_out_fwd", body, [o_mla, o_dn, z], [w_mla, w_dn], [(2 * DN_WIDTH, BF16)], [],
                      _row_tile(o_mla.shape[0]))[0]


def _mix_out_bwd(o_mla, o_dn, z, dmixed, w_mla, w_dn):
    def body(om_ref, od_ref, z_ref, dm_ref, wm_ref, wd_ref, dom_ref, dod_ref, dz_ref, dwm_ref, dwd_ref):
        dwm = dwd = None
        for h in range(HEADS):
            sl = slice(h * HEAD, (h + 1) * HEAD)
            _, vjp = jax.vjp(lambda o, w: _rms(o, w, HEAD), om_ref[:, sl], wm_ref[...])
            do, dw = vjp(dm_ref[:, sl])
            dom_ref[:, sl] = do
            dwm = dw if dwm is None else dwm + dw
            _, vjp = jax.vjp(_dn_out_fn, od_ref[:, sl], z_ref[:, sl], wd_ref[...])
            do, dz, dw = vjp(dm_ref[:, DN_WIDTH + h * HEAD:DN_WIDTH + (h + 1) * HEAD])
            dod_ref[:, sl] = do
            dz_ref[:, sl] = dz.astype(BF16)
            dwd = dw if dwd is None else dwd + dw
        _accumulate(dwm_ref, dwm)
        _accumulate(dwd_ref, dwd)

    return _rows_call("mix_out_bwd", body, [o_mla, o_dn, z, dmixed], [w_mla, w_dn],
                      [(DN_WIDTH, F32), (DN_WIDTH, F32), (DN_WIDTH, BF16)], [(1, HEAD), (1, HEAD)],
                      _row_tile(o_mla.shape[0]))


def _shift_down(x, s):
    if s == 0:
        return x
    rows = lax.broadcasted_iota(jnp.int32, x.shape, 0)
    return jnp.where(rows >= s, pltpu.roll(x, s, 0), 0.0)


def _shift_up(x, s):
    if s == 0:
        return x
    t = x.shape[0]
    rows = lax.broadcasted_iota(jnp.int32, x.shape, 0)
    return jnp.where(rows < t - s, pltpu.roll(x, t - s, 0), 0.0)


def _col_call(name, body, cols, taps, outs, tap_outs, cw):
    t, c = cols[0].shape[0], taps[0].shape[1]
    in_specs = [pl.BlockSpec((t, cw), lambda j: (0, j)) for _ in cols]
    in_specs += [pl.BlockSpec((a.shape[0], cw), lambda j: (0, j)) for a in taps]
    out_shape = [jax.ShapeDtypeStruct((t, c), dt) for dt in outs] + [jax.ShapeDtypeStruct((n, c), F32) for n in tap_outs]
    out_specs = [pl.BlockSpec((t, cw), lambda j: (0, j)) for _ in outs]
    out_specs += [pl.BlockSpec((n, cw), lambda j: (0, j)) for n in tap_outs]
    return pl.pallas_call(body, name=name, grid=(c // cw,), in_specs=in_specs, out_specs=out_specs, out_shape=out_shape,
                          compiler_params=_cparams(("arbitrary",)))(*cols, *taps)


def _causal_conv(x, w_ref, width):
    acc = w_ref[width - 1:width, :] * x
    for j in range(width - 1):
        acc = acc + w_ref[j:j + 1, :] * _shift_down(x, width - 1 - j)
    return acc


def _causal_conv_bwd(x, dpre, w_ref, dx_ref, dw_ref, width):
    dx = w_ref[width - 1:width, :] * dpre
    dw_ref[width - 1:width, :] = jnp.sum(dpre * x, axis=0, keepdims=True)
    for j in range(width - 1):
        s = width - 1 - j
        dx = dx + w_ref[j:j + 1, :] * _shift_up(dpre, s)
        dw_ref[j:j + 1, :] = jnp.sum(dpre * _shift_down(x, s), axis=0, keepdims=True)
    dx_ref[...] = dx.astype(dx_ref.dtype)


def _dsilu(x):
    sg = jax.nn.sigmoid(x)
    return sg * (1.0 + x * (1.0 - sg))


def _dn_conv_fwd(x, w):
    def body(x_ref, w_ref, y_ref):
        y_ref[...] = _silu(_causal_conv(x_ref[...], w_ref, 4))

    return _col_call("dn_conv_fwd", body, [x], [w], [F32], [], 256)[0]


def _dn_conv_bwd(x, w, dy):
    def body(x_ref, dy_ref, w_ref, dx_ref, dw_ref):
        xv = x_ref[...]
        dpre = dy_ref[...] * _dsilu(_causal_conv(xv, w_ref, 4))
        _causal_conv_bwd(xv, dpre, w_ref, dx_ref, dw_ref, 4)

    return _col_call("dn_conv_bwd", body, [x, dy], [w], [BF16], [4], 256)


def _glu_fwd(gpre, up, w, b):
    def body(g_ref, u_ref, w_ref, b_ref, a_ref):
        gate = _causal_conv(g_ref[...], w_ref, 3) + b_ref[...]
        a_ref[...] = (_silu(gate) * u_ref[...]).astype(BF16)

    return _col_call("glu_fwd", body, [gpre, up], [w, b], [BF16], [], 256)[0]


def _glu_bwd(gpre, up, w, b, dact):
    def body(g_ref, u_ref, da_ref, w_ref, b_ref, dg_ref, du_ref, dw_ref, db_ref):
        gv = g_ref[...]
        gate = _causal_conv(gv, w_ref, 3) + b_ref[...]
        da = da_ref[...]
        du_ref[...] = (da * _silu(gate)).astype(BF16)
        dgate = da * u_ref[...] * _dsilu(gate)
        db_ref[...] = jnp.sum(dgate, axis=0, keepdims=True)
        _causal_conv_bwd(gv, dgate, w_ref, dg_ref, dw_ref, 3)

    return _col_call("glu_bwd", body, [gpre, up, dact], [w, b], [BF16, BF16], [3, 1], 256)


def _dn_prep_consts(sa, sb, al, dt):
    return dict(sel_a=sa[...], sel_b=sb[...], alog=al[...], dtb=dt[...])


def _dn_prep_fwd(conv, ab, sel_a, sel_b, alog, dtb):
    def body(c_ref, ab_ref, sa, sb, al, dt, q_out, k_out, g_out, b_out):
        qc = tuple(c_ref[:, h * HEAD:(h + 1) * HEAD] for h in range(HEADS))
        kc = tuple(c_ref[:, DN_WIDTH + h * HEAD:DN_WIDTH + (h + 1) * HEAD] for h in range(HEADS))
        qs, ks, g, beta = _dn_prep_fn((qc, kc, ab_ref[...]), _dn_prep_consts(sa, sb, al, dt))
        for h in range(HEADS):
            q_out[:, h * HEAD:(h + 1) * HEAD] = qs[h]
            k_out[:, h * HEAD:(h + 1) * HEAD] = ks[h]
        g_out[...] = g
        b_out[...] = beta

    return _rows_call("dn_prep_fwd", body, [conv, ab], [sel_a, sel_b, alog, dtb], [(DN_WIDTH, F32)] * 4, [],
                      _row_tile(conv.shape[0]))


def _dn_prep_bwd(conv, ab, dq, dk, dv, dg, db, sel_a, sel_b, alog, dtb):
    def body(c_ref, ab_ref, dq_r, dk_r, dv_r, dg_r, db_r, sa, sb, al, dt, dc_out, dab_out, dal_out, ddt_out):
        qc = tuple(c_ref[:, h * HEAD:(h + 1) * HEAD] for h in range(HEADS))
        kc = tuple(c_ref[:, DN_WIDTH + h * HEAD:DN_WIDTH + (h + 1) * HEAD] for h in range(HEADS))
        consts = _dn_prep_consts(sa, sb, al, dt)
        sel = dict(sel_a=consts["sel_a"], sel_b=consts["sel_b"])
        _, vjp = jax.vjp(lambda rows, ad: _dn_prep_fn(rows, {**sel, **ad}), (qc, kc, ab_ref[...]),
                         dict(alog=consts["alog"], dtb=consts["dtb"]))
        cq = tuple(dq_r[:, h * HEAD:(h + 1) * HEAD] for h in range(HEADS))
        ck = tuple(dk_r[:, h * HEAD:(h + 1) * HEAD] for h in range(HEADS))
        (dqc, dkc, dab), dad = vjp((cq, ck, dg_r[...], db_r[...]))
        for h in range(HEADS):
            dc_out[:, h * HEAD:(h + 1) * HEAD] = dqc[h]
            dc_out[:, DN_WIDTH + h * HEAD:DN_WIDTH + (h + 1) * HEAD] = dkc[h]
        dc_out[:, 2 * DN_WIDTH:3 * DN_WIDTH] = dv_r[...]
        dab_out[...] = dab.astype(BF16)
        _accumulate(dal_out, dad["alog"])
        _accumulate(ddt_out, dad["dtb"])

    return _rows_call("dn_prep_bwd", body, [conv, ab, dq, dk, dv, dg, db], [sel_a, sel_b, alog, dtb],
                      [(3 * DN_WIDTH, F32), (HEAD, BF16)], [(1, DN_WIDTH), (1, DN_WIDTH)], _row_tile(conv.shape[0]))


def _chunk_batch(t):
    nc = t // CHUNK
    return nc // 2 if nc % 2 == 0 else nc


def _dn_chunk_specs(t, nb):
    rows = nb * CHUNK
    hb = lambda h, b: (b, h)
    vb = lambda h, b: (b, 2 * HEADS + h)
    qk_spec = pl.BlockSpec((None, rows, CHUNK), lambda h, b: (h, b, 0))
    blk = pl.BlockSpec((rows, HEAD), hb)
    return rows, blk, pl.BlockSpec((rows, HEAD), vb), qk_spec


def _dn_chunk_fwd(qn, kn, conv, g, beta, host=None):
    t = qn.shape[0]
    nb = _chunk_batch(t)
    rows, blk, vblk, qk_spec = _dn_chunk_specs(t, nb)

    def body(q_ref, k_ref, v_ref, g_ref, b_ref, u_o, w_o, qg_o, kd_o, eg_o, qk_o):
        r3 = lambda x: x.reshape(nb, CHUNK, x.shape[-1])
        outs = _dn_chunk_fn(r3(q_ref[...]), r3(k_ref[...]), r3(v_ref[...]), r3(g_ref[...]), r3(g_ref[:, 0:CHUNK]),
                            r3(b_ref[...]))
        for o_ref, val in zip((u_o, w_o, qg_o, kd_o, eg_o, qk_o), outs):
            o_ref[...] = val.reshape(rows, val.shape[-1])

    return _pcall(body, "dn_chunk_fwd", (HEADS, t // rows), [blk, blk, vblk, blk, blk], [blk] * 5 + [qk_spec],
                  [jax.ShapeDtypeStruct((t, DN_WIDTH), F32)] * 5 + [jax.ShapeDtypeStruct((HEADS, t, CHUNK), F32)],
                  [qn, kn, conv, g, beta], ("arbitrary", "arbitrary"), host=host)


def _dn_chunk_bwd(qn, kn, conv, g, beta, cts, host=None):
    t = qn.shape[0]
    nb = _chunk_batch(t)
    rows, blk, vblk, qk_spec = _dn_chunk_specs(t, nb)

    def body(q_ref, k_ref, v_ref, g_ref, b_ref, du, dw, dqg, dkd, deg, dqk, dq_o, dk_o, dv_o, dg_o, db_o):
        r3 = lambda x: x.reshape(nb, CHUNK, x.shape[-1])
        _, vjp = jax.vjp(_dn_chunk_fn, r3(q_ref[...]), r3(k_ref[...]), r3(v_ref[...]), r3(g_ref[...]),
                         r3(g_ref[:, 0:CHUNK]), r3(b_ref[...]))
        dq, dk, dv, dg, dg64, db = vjp(tuple(r3(c[...]) for c in (du, dw, dqg, dkd, deg, dqk)))
        for o_ref, val in zip((dq_o, dk_o, dv_o, dg_o, db_o), (dq, dk, dv, dg, db)):
            o_ref[...] = val.reshape(rows, HEAD)
        dg_o[:, 0:CHUNK] += dg64.reshape(rows, CHUNK)

    return _pcall(body, "dn_chunk_bwd", (HEADS, t // rows), [blk, blk, vblk, blk, blk] + [blk] * 5 + [qk_spec], [blk] * 5,
                  [jax.ShapeDtypeStruct((t, DN_WIDTH), F32)] * 5, [qn, kn, conv, g, beta, *cts], ("arbitrary", "arbitrary"),
                  host=host)


def _dn_rec_fwd(u, w, qg, kd, eg, qk, host=None):
    t = u.shape[0]
    nc = t // CHUNK
    blk = pl.BlockSpec((CHUNK, DN_WIDTH), lambda c: (c, 0))
    qk_spec = pl.BlockSpec((HEADS, CHUNK, CHUNK), lambda c: (0, c, 0))
    s_spec = pl.BlockSpec((None, DN_WIDTH, HEAD), lambda c: (c, 0, 0))

    def body(u_ref, w_ref, qg_ref, kd_ref, eg_ref, qk_ref, o_ref, sall_ref, s_scr):
        @pl.when(pl.program_id(0) == 0)
        def _():
            s_scr[...] = jnp.zeros(s_scr.shape, F32)

        sall_ref[...] = s_scr[...]
        for h in range(HEADS):
            sl = slice(h * HEAD, (h + 1) * HEAD)
            s_new, o = _dn_rec_fn(s_scr[sl, :], u_ref[:, sl], w_ref[:, sl], qg_ref[:, sl], qk_ref[h], kd_ref[:, sl],
                                  eg_ref[0:1, sl])
            o_ref[:, sl] = o
            s_scr[sl, :] = s_new

    return _pcall(body, "dn_rec_fwd", (nc,), [blk] * 5 + [qk_spec], [blk, s_spec],
                  [jax.ShapeDtypeStruct((t, DN_WIDTH), F32), jax.ShapeDtypeStruct((nc, DN_WIDTH, HEAD), F32)],
                  [u, w, qg, kd, eg, qk], ("arbitrary",), scratch_shapes=[pltpu.VMEM((DN_WIDTH, HEAD), F32)], host=host)


def _dn_rec_bwd(u, w, qg, kd, eg, qk, sall, do, host=None):
    t = u.shape[0]
    nc = t // CHUNK
    blk = pl.BlockSpec((CHUNK, DN_WIDTH), lambda c: (nc - 1 - c, 0))
    qk_spec = pl.BlockSpec((HEADS, CHUNK, CHUNK), lambda c: (0, nc - 1 - c, 0))
    s_spec = pl.BlockSpec((None, DN_WIDTH, HEAD), lambda c: (nc - 1 - c, 0, 0))

    def body(u_ref, w_ref, qg_ref, kd_ref, eg_ref, qk_ref, s_ref, do_ref, du_o, dw_o, dqg_o, dkd_o, deg_o, dqk_o, ds_scr):
        @pl.when(pl.program_id(0) == 0)
        def _():
            ds_scr[...] = jnp.zeros(ds_scr.shape, F32)

        deg_o[...] = jnp.zeros(deg_o.shape, F32)
        for h in range(HEADS):
            sl = slice(h * HEAD, (h + 1) * HEAD)
            _, vjp = jax.vjp(_dn_rec_fn, s_ref[sl, :], u_ref[:, sl], w_ref[:, sl], qg_ref[:, sl], qk_ref[h],
                             kd_ref[:, sl], eg_ref[0:1, sl])
            ds, du, dw, dqg, dqk, dkd, deg = vjp((ds_scr[sl, :], do_ref[:, sl]))
            du_o[:, sl] = du
            dw_o[:, sl] = dw
            dqg_o[:, sl] = dqg
            dkd_o[:, sl] = dkd
            deg_o[0:1, sl] = deg
            dqk_o[h] = dqk
            ds_scr[sl, :] = ds

    return _pcall(body, "dn_rec_bwd", (nc,), [blk] * 5 + [qk_spec, s_spec, blk], [blk] * 5 + [qk_spec],
                  [jax.ShapeDtypeStruct((t, DN_WIDTH), F32)] * 5 + [jax.ShapeDtypeStruct((HEADS, t, CHUNK), F32)],
                  [u, w, qg, kd, eg, qk, sall, do], ("arbitrary",), scratch_shapes=[pltpu.VMEM((DN_WIDTH, HEAD), F32)],
                  host=host)


def _loss_call(h2, tgt, n_valid):
    t, n = h2.shape
    r = _row_tile(t)

    def body(h_ref, t_ref, dy_ref, dy16_ref, acc_ref):
        rows = pl.program_id(0) * r + lax.broadcasted_iota(jnp.int32, (r, n), 0)
        valid = jnp.logical_and(rows >= N_META, rows < n_valid)
        e = jnp.where(valid, h_ref[...] - t_ref[...], 0.0)
        dy = e * (1.0 / n)
        dy_ref[...] = dy
        dy16_ref[...] = dy.astype(BF16)
        _accumulate(acc_ref, jnp.sum(e * e, axis=0, keepdims=True))

    return _rows_call("loss", body, [h2, tgt], [], [(n, F32), (n, BF16)], [(1, n)], r)


def _adamw_call(name, w, g, m, v):
    rows, cols = w.shape
    by_rows = rows % 8 == 0

    def body(w_ref, g_ref, m_ref, v_ref, g_out, d_ref, m_out, v_out):
        gv = g_ref[...] if by_rows else g_ref[0:rows, :]
        m2 = ADAM_B1 * m_ref[...] + (1.0 - ADAM_B1) * gv
        v2 = ADAM_B2 * v_ref[...] + (1.0 - ADAM_B2) * (gv * gv)
        m_hat = m2 / (1.0 - ADAM_B1 ** ADAM_STEP)
        v_hat = v2 / (1.0 - ADAM_B2 ** ADAM_STEP)
        g_out[...] = gv
        d_ref[...] = -ADAM_LR * (m_hat / (jnp.sqrt(v_hat) + ADAM_EPS) + ADAM_WD * w_ref[...])
        m_out[...] = m2
        v_out[...] = v2

    if by_rows:
        tr = _tile(rows, 256, 8)
        spec = g_spec = pl.BlockSpec((tr, cols), lambda i: (i, 0))
        grid = (rows // tr,)
    else:
        tc = _tile(cols, 256, 128)
        spec = pl.BlockSpec((rows, tc), lambda j: (0, j))
        g_spec = pl.BlockSpec((g.shape[0], tc), lambda j: (0, j))
        grid = (cols // tc,)
    return pl.pallas_call(body, name=name, grid=grid, in_specs=[spec, g_spec, spec, spec], out_specs=[spec] * 4,
                          out_shape=[jax.ShapeDtypeStruct((rows, cols), F32)] * 4,
                          compiler_params=_cparams(("arbitrary",)))(w, g, m, v)


def _rope_tables(t):
    half = ROPE // 2
    inv_freq = ROPE_THETA ** (-jnp.arange(half, dtype=F32) / half)
    ang = jnp.arange(t, dtype=F32)[:, None] * inv_freq[None, :]
    z = jnp.zeros((t, HEAD - ROPE), F32)
    cos = jnp.concatenate([jnp.cos(ang), jnp.cos(ang), z], axis=1)
    sin = jnp.concatenate([jnp.sin(ang), jnp.sin(ang), z], axis=1)
    k = jnp.arange(HEAD)[:, None]
    l = jnp.arange(HEAD)[None, :]
    perm = jnp.where((l < half) & (k == l + half), -1.0, 0.0) + jnp.where((l >= half) & (l < ROPE) & (k == l - half), 1.0, 0.0)
    return cos, sin, perm.astype(F32)


def _win_to_pad(w):
    z = lambda n: jnp.zeros((n, w.shape[1]), w.dtype)
    return jnp.concatenate([w[576:2112], w[2112:2624], w[0:256], w[256:512], w[512:576], z(64), w[2624:2632], z(120)],
                           axis=0)


def _win_from_pad(g):
    return jnp.concatenate([g[2048:2304], g[2304:2560], g[2560:2624], g[0:1536], g[1536:2048], g[2688:2696]], axis=0)


def _qk_to_pad(w):
    w4 = w.reshape(HEADS, QK_DIM, w.shape[-1])
    return jnp.concatenate([w4, jnp.zeros((HEADS, QK_PAD - QK_DIM, w.shape[-1]), w.dtype)], axis=1).reshape(
        HEADS * QK_PAD, w.shape[-1])


def _qk_from_pad(g):
    return g.reshape(HEADS, QK_PAD, g.shape[-1])[:, :QK_DIM].reshape(HEADS * QK_DIM, g.shape[-1])


def _ff_to_pad(a, axis):
    shape = list(a.shape)
    shape[axis:axis + 1] = [N_CHIPS, FF_SHARD]
    a4 = a.reshape(shape)
    shape[axis + 1] = FF_BLOCK - FF_SHARD
    out = jnp.concatenate([a4, jnp.zeros(shape, a.dtype)], axis=axis + 1)
    shape[axis:axis + 2] = [D_FF_P]
    return out.reshape(shape)


def _ff_from_pad(a, axis):
    shape = list(a.shape)
    shape[axis:axis + 1] = [N_CHIPS, FF_BLOCK]
    a4 = lax.slice_in_dim(a.reshape(shape), 0, FF_SHARD, axis=axis + 1)
    shape[axis:axis + 2] = [D_FF]
    return a4.reshape(shape)


class _LocalPlan:
    def __init__(self, wt):
        self.wt, self.grads = wt, {}

    def weight(self, name):
        return self.wt[name]

    def host(self, point):
        return None

    def grad(self, name, value):
        self.grads[name] = value


def _local_step(x, tgt, wt, plan=None):
    plan = _LocalPlan(wt) if plan is None else plan
    s = x.shape[0]
    n_valid = N_META + s
    t = -(-n_valid // HEAD) * HEAD
    zpad = jnp.zeros((t - n_valid, D_MODEL), F32)
    h0 = jnp.concatenate([wt["meta_tokens"], x, zpad], axis=0)
    tgt_p = jnp.concatenate([jnp.zeros((N_META, D_MODEL), F32), tgt, zpad], axis=0)
    cos, sin, perm = _rope_tables(t)
    win, wq, wkv = wt["w_in_t"], wt["w_q_t"], wt["w_kv_t"]
    qn_w = jnp.concatenate([wt["q_norm_w"], jnp.zeros((1, QK_PAD - QK_DIM), F32)], axis=1)
    kn_w = jnp.concatenate([wt["k_norm_w"], jnp.zeros((1, QK_PAD - QK_DIM), F32)], axis=1)
    head_id = jnp.arange(DN_WIDTH)[None, :] // HEAD
    lane = jnp.arange(HEAD)[:, None]
    sel_a = (lane == head_id).astype(F32)
    sel_b = (lane == head_id + HEADS).astype(F32)
    alog = jnp.repeat(wt["dn_A_log"], HEAD, axis=1)
    dtb = jnp.repeat(wt["dn_dt_bias"], HEAD, axis=1)
    conv_w, conv_b = wt["ffn_conv_w"], wt["ffn_conv_b"]

    u = _rms_fwd("attn_norm_fwd", h0, wt["attn_norm_w"])
    proj = _matmul("in_proj", u, win, "nt", F32)
    z = (proj, DN_WIDTH, 3)
    q_lat, kv_lat, k_pe, ab = (proj, LORA, 8), (proj, LORA, 9), (proj, HEAD, 20), (proj, HEAD, 21)
    mla_consts = (wt["q_a_norm_w"], wq, wt["kv_a_norm_w"], wkv, qn_w, kn_w, perm)
    q, k, v = _mla_prep_fwd(q_lat, kv_lat, k_pe, cos, sin, *mla_consts)
    o_mla = _attn_fwd(q, k, v, host=plan.host("attn_fwd"))
    conv = _dn_conv_fwd(proj, wt["dn_conv_w"])
    dn_consts = (sel_a, sel_b, alog, dtb)
    qn, kn, g, beta = _dn_prep_fwd(conv, ab, *dn_consts)
    cu, cw, cqg, ckd, ceg, cqk = _dn_chunk_fwd(qn, kn, conv, g, beta, host=plan.host("dn_chunk_fwd"))
    o_dn, sall = _dn_rec_fwd(cu, cw, cqg, ckd, ceg, cqk, host=plan.host("dn_rec_fwd"))
    mixed = _mix_out_fwd(o_mla, o_dn, z, wt["mla_out_norm_w"], wt["dn_out_norm_w"])
    w_out = plan.weight("w_out")
    h1 = _matmul("out_proj", mixed, w_out, "nn", F32, res=h0)
    n2 = _rms_fwd("ffn_norm_fwd", h1, wt["ffn_norm_w"])
    w_gate, w_up, w_down = plan.weight("w_gate_t"), plan.weight("w_up_t"), plan.weight("w_down")
    gpre = _matmul("gate_proj", n2, w_gate, "nt", F32)
    up = _matmul("up_proj", n2, w_up, "nt", F32)
    act = _glu_fwd(gpre, up, conv_w, conv_b)
    h2 = _matmul("down_proj", act, w_down, "nn", F32, res=h1)
    dy, dy16, sq = _loss_call(h2, tgt_p, n_valid)

    grads = {}
    dact = _matmul("down_dx", dy16, w_down, "nt", F32)
    plan.grad("w_down", _matmul("down_dw", act, dy16, "tn", F32))
    dgpre, dup, grads["ffn_conv_w"], grads["ffn_conv_b"] = _glu_bwd(gpre, up, conv_w, conv_b, dact)
    plan.grad("w_gate_t", _matmul("gate_dw", dgpre, n2, "tn", F32))
    plan.grad("w_up_t", _matmul("up_dw", dup, n2, "tn", F32))
    dn2a = _matmul("gate_dx", dgpre, w_gate, "nn", F32)
    dn2b = _matmul("up_dx", dup, w_up, "nn", F32)
    dh1, dh1_16, grads["ffn_norm_w"] = _rms_bwd("ffn_norm_bwd", h1, wt["ffn_norm_w"], [dn2a, dn2b], dy)
    dmixed = _matmul("out_dx", dh1_16, w_out, "nt", F32)
    plan.grad("w_out", _matmul("out_dw", mixed, dh1_16, "tn", F32))
    do_mla, do_dn, dz, grads["mla_out_norm_w"], grads["dn_out_norm_w"] = _mix_out_bwd(
        o_mla, o_dn, z, dmixed, wt["mla_out_norm_w"], wt["dn_out_norm_w"])
    rec_cts = _dn_rec_bwd(cu, cw, cqg, ckd, ceg, cqk, sall, do_dn, host=plan.host("dn_rec_bwd"))
    dqn, dkn, dv_dn, dg, dbeta = _dn_chunk_bwd(qn, kn, conv, g, beta, rec_cts, host=plan.host("dn_chunk_bwd"))
    dconv, dab, dalog, ddtb = _dn_prep_bwd(conv, ab, dqn, dkn, dv_dn, dg, dbeta, *dn_consts)
    grads["dn_A_log"] = jnp.sum(dalog.reshape(HEADS, HEAD), axis=1)[None, :]
    grads["dn_dt_bias"] = jnp.sum(ddtb.reshape(HEADS, HEAD), axis=1)[None, :]
    ddn_pre, grads["dn_conv_w"] = _dn_conv_bwd(proj, wt["dn_conv_w"], dconv)
    dq, dk, dv = _attn_bwd(q, k, v, do_mla, host=plan.host("attn_bwd"))
    dq_lat, dkv_lat, dk_pe, dqa, dwq, dkva, dwkv, dqnw, dknw = _mla_prep_bwd(
        q_lat, kv_lat, k_pe, cos, sin, dq, dk, dv, *mla_consts, host=plan.host("mla_prep_bwd"))
    grads["q_a_norm_w"], grads["kv_a_norm_w"] = dqa, dkva
    plan.grad("w_q_t", dwq)
    plan.grad("w_kv_t", dwkv)
    grads["q_norm_w"], grads["k_norm_w"] = dqnw[:, :QK_DIM], dknw[:, :QK_DIM]
    dproj = jnp.concatenate([ddn_pre, dz, dq_lat, dkv_lat, dk_pe, dab], axis=1)
    plan.grad("w_in_t", _matmul("in_dw", dproj, u, "tn", F32))
    du = _matmul("in_dx", dproj, win, "nn", F32, host=plan.host("in_dx"))
    dh0, _, grads["attn_norm_w"] = _rms_bwd("attn_norm_bwd", h0, wt["attn_norm_w"], [du], dh1,
                                            host=plan.host("attn_norm_bwd"))
    grads["meta_tokens"] = dh0[0:N_META]
    if isinstance(plan, _LocalPlan):
        grads.update(plan.grads)
    return sq, dh0[N_META:n_valid], grads


def _mesh_pos():
    return lax.axis_index("x"), lax.axis_index("y"), lax.axis_index("c")


def _other_chips(x, y):
    return [(1 - x, y), (x, 1 - y), (1 - x, 1 - y)]


def _remote(src, dst, send_sems, recv_sems, k, to):
    return pltpu.make_async_remote_copy(src_ref=src, dst_ref=dst, send_sem=send_sems.at[k], recv_sem=recv_sems.at[k],
                                        device_id=to, device_id_type=MESH)


def _copies_exchange(make, ins, out_shape, nsem):
    def prog(in_refs, out_refs, send_sems, recv_sems):
        copies = make(in_refs, out_refs, send_sems, recv_sems)

        def start():
            for cp in copies:
                cp.start()

        def finish():
            for cp in copies:
                cp.wait()

        return start, finish

    return _Exchange(prog, ins, out_shape, nsem)


def _all_gather(shards):
    def prog(srcs, dsts, send_sems, recv_sems):
        x, y, c = _mesh_pos()
        p = 2 * x + y
        sibling = (x, y, 1 - c)
        chips = _other_chips(x, y)
        bufs = tuple((s, d, s.shape[0] // 2) for s, d in zip(srcs, dsts))

        def half(ref, rows, which):
            return ref.at[pl.ds(which * rows, rows), :]

        def copy(i, k, src, dst, to):
            return _remote(src, dst, send_sems, recv_sems, 6 * i + k, to)

        sends = [copy(i, j, half(src, rows, c), half(dst.at[p], rows, c), (*chip, c))
                 for i, (src, dst, rows) in enumerate(bufs) for j, chip in enumerate(chips)]

        def start():
            for cp in sends:
                cp.start()

        def finish():
            passed = []
            for i, (src, dst, rows) in enumerate(bufs):
                for j, (qx, qy) in enumerate(chips):
                    block = half(dst.at[2 * qx + qy], rows, c)
                    copy(i, j, block, block, (x, y, c)).wait_recv()
                    fwd = copy(i, 3 + j, block, block, sibling)
                    fwd.start()
                    passed.append(fwd)
            for i, (src, dst, rows) in enumerate(bufs):
                for j, (qx, qy) in enumerate(chips):
                    block = half(dst.at[2 * qx + qy], rows, 1 - c)
                    copy(i, 3 + j, block, block, (x, y, c)).wait_recv()
            for cp in sends + passed:
                cp.wait_send()

        return start, finish

    return _Exchange(prog, shards, [jax.ShapeDtypeStruct((N_CHIPS, *s.shape), s.dtype) for s in shards], 6 * len(shards))


def _gathered(ex):
    p = 2 * lax.axis_index("x") + lax.axis_index("y")
    return [lax.dynamic_update_slice(g, s[None], (p, 0, 0)) for g, s in zip(ex.outs, ex.ins)]


def _rs_to_sibling(bufs):
    def make(srcs, dsts, send_sems, recv_sems):
        x, y, c = _mesh_pos()
        copies = []
        for i, (src, dst) in enumerate(zip(srcs, dsts)):
            half = src.shape[1] // 2
            copies.append(_remote(src.at[:, pl.ds((1 - c) * half, half), :], dst, send_sems, recv_sems, i, (x, y, 1 - c)))
        return copies

    return _copies_exchange(make, bufs, [jax.ShapeDtypeStruct((N_CHIPS, b.shape[1] // 2, b.shape[2]), F32) for b in bufs],
                            len(bufs))


def _rs_pair_add(name, buf, got, c, out_dtype):
    half, cols = got.shape[1], got.shape[2]

    def body(c_ref, a_ref, b_ref, o_ref):
        o_ref[...] = (a_ref[...] + b_ref[...]).astype(out_dtype)

    return pl.pallas_call(
        body, name=name,
        grid_spec=pltpu.PrefetchScalarGridSpec(
            num_scalar_prefetch=1, grid=(N_CHIPS,),
            in_specs=[pl.BlockSpec((None, half, cols), lambda j, cr: (j, cr[0], 0)),
                      pl.BlockSpec((None, half, cols), lambda j, cr: (j, 0, 0))],
            out_specs=pl.BlockSpec((None, half, cols), lambda j, cr: (j, 0, 0))),
        out_shape=jax.ShapeDtypeStruct(got.shape, out_dtype),
        compiler_params=_cparams(("arbitrary",)))(c, buf, got)


def _rs_to_chips(accs):
    def make(srcs, dsts, send_sems, recv_sems):
        x, y, c = _mesh_pos()
        return [_remote(src.at[2 * qx + qy], dst.at[k], send_sems, recv_sems, 3 * i + k, (qx, qy, c))
                for i, (src, dst) in enumerate(zip(srcs, dsts)) for k, (qx, qy) in enumerate(_other_chips(x, y))]

    return _copies_exchange(make, accs, [jax.ShapeDtypeStruct((3, a.shape[1], a.shape[2]), a.dtype) for a in accs],
                            3 * len(accs))


def _rs_chip_add(name, acc, got, p):
    half, cols = acc.shape[1], acc.shape[2]
    tr = _tile(half, 128, 8)
    slot = (0, 1, 0, 2)

    def body(p_ref, own_ref, g0_ref, g1_ref, g2_ref, o_ref):
        me = p_ref[0]
        gots = (g0_ref, g1_ref, g2_ref)
        total = None
        for chip in range(N_CHIPS):
            val = own_ref[...].astype(F32)
            for e in (1, 2, 3):
                val = jnp.where((chip ^ me) == e, gots[slot[e]][...].astype(F32), val)
            total = val if total is None else total + val
        o_ref[...] = total

    gspec = lambda k: pl.BlockSpec((None, tr, cols), lambda i, pr: (k, i, 0))
    return pl.pallas_call(
        body, name=name,
        grid_spec=pltpu.PrefetchScalarGridSpec(
            num_scalar_prefetch=1, grid=(half // tr,),
            in_specs=[pl.BlockSpec((None, tr, cols), lambda i, pr: (pr[0], i, 0)), gspec(0), gspec(1), gspec(2)],
            out_specs=pl.BlockSpec((tr, cols), lambda i, pr: (i, 0))),
        out_shape=jax.ShapeDtypeStruct((half, cols), F32),
        compiler_params=_cparams(("arbitrary",)))(p, acc, got, got, got)


def _rs_share(ress):
    def make(srcs, dsts, send_sems, recv_sems):
        x, y, c = _mesh_pos()
        return [_remote(src, dst, send_sems, recv_sems, i, (x, y, 1 - c)) for i, (src, dst) in enumerate(zip(srcs, dsts))]

    return _copies_exchange(make, ress, [jax.ShapeDtypeStruct(r.shape, F32) for r in ress], len(ress))


def _shared(ex):
    south = lax.axis_index("c") == 0
    return [jnp.concatenate([jnp.where(south, r, g), jnp.where(south, g, r)], axis=0) for r, g in zip(ex.ins, ex.outs)]


def _all_to_all_devices(vec):
    def make(srcs, dsts, send_sems, recv_sems):
        x, y, c = _mesh_pos()
        me = 4 * x + 2 * y + c
        copies = []
        for r in range(1, 8):
            px, py, pc = (1 - x if r & 4 else x), (1 - y if r & 2 else y), (1 - c if r & 1 else c)
            copies.append(_remote(srcs[0], dsts[0].at[me], send_sems, recv_sems, r - 1, (px, py, pc)))
        return copies

    return _copies_exchange(make, [vec], [jax.ShapeDtypeStruct((8, *vec.shape), vec.dtype)], 7)


def _sum_devices(stack):
    def body(s_ref, o_ref):
        total = s_ref[0]
        for d in range(1, 8):
            total = total + s_ref[d]
        o_ref[...] = total

    return pl.pallas_call(body, name="sum_devices", out_shape=jax.ShapeDtypeStruct(stack.shape[1:], F32),
                          compiler_params=pltpu.CompilerParams(vmem_limit_bytes=VMEM_LIMIT))(stack)


def _pad_rows(flat, rows):
    return jnp.concatenate([flat, jnp.zeros((rows * LANES - flat.shape[0],), flat.dtype)]).reshape(rows, LANES)


def _unshard(g4, shape, axis):
    a = g4.reshape(N_CHIPS, *shape)
    if axis == 0:
        return a.reshape(N_CHIPS * shape[0], shape[1])
    return jnp.transpose(a, (1, 0, 2)).reshape(shape[0], N_CHIPS * shape[1])


def _shard4(full, shape, axis):
    if axis == 0:
        return full.reshape(N_CHIPS, shape[0] * shape[1])
    a = full.reshape(shape[0], N_CHIPS, shape[1])
    return jnp.transpose(a, (1, 0, 2)).reshape(N_CHIPS, shape[0] * shape[1])


def _pad_axis0(a, rows):
    return jnp.concatenate([a, jnp.zeros((rows - a.shape[0], *a.shape[1:]), a.dtype)], axis=0)


def _pad_axis1(a, rows):
    return jnp.concatenate([a, jnp.zeros((a.shape[0], rows - a.shape[1], *a.shape[2:]), a.dtype)], axis=1)


def _shard_to_strip(name, w):
    _, (shape, axis, rows) = name, {n: (s, ax, r) for n, s, ax, r in BIG}[name]
    w2 = w.reshape(shape).astype(BF16)
    return _pad_axis0(w2.T if axis == 1 else w2, rows)


LOCAL_NAME = dict(w_in="w_in_t", w_q_b="w_q_t", w_kv_b="w_kv_t", w_out="w_out", w_gate="w_gate_t", w_up="w_up_t",
                  w_down="w_down")


WIN_SEGMENTS = ((576, 2112, 0), (2112, 2624, 1536), (0, 256, 2048), (256, 512, 2304), (512, 576, 2560), (2624, 2632, 2688))


def _strips_to_weight(name, g4):
    if name == "w_in":
        pieces, at = [], 0
        for a, b, local in WIN_SEGMENTS:
            if local > at:
                pieces.append(jnp.zeros((local - at, D_MODEL), g4.dtype))
            for q in range(N_CHIPS):
                s, e = max(a, q * IN_SHARD), min(b, (q + 1) * IN_SHARD)
                if s < e:
                    pieces.append(g4[q, s - q * IN_SHARD:e - q * IN_SHARD])
            at = local + b - a
        pieces.append(jnp.zeros((IN_PAD - at, D_MODEL), g4.dtype))
        return jnp.concatenate(pieces, axis=0)
    if name == "w_q_b":
        return _qk_to_pad(g4.reshape(HEADS * QK_DIM, LORA))
    return g4.reshape(N_CHIPS * g4.shape[1], g4.shape[2])


def _grad_to_strips(name, g):
    if name == "w_in":
        strips = []
        for q in range(N_CHIPS):
            pieces = []
            for a, b, local in sorted(WIN_SEGMENTS):
                s, e = max(a, q * IN_SHARD), min(b, (q + 1) * IN_SHARD)
                if s < e:
                    pieces.append(g[local + s - a:local + e - a])
            pieces.append(jnp.zeros((IN_SHARD_P - IN_SHARD, D_MODEL), g.dtype))
            strips.append(jnp.concatenate(pieces, axis=0))
        return jnp.stack(strips)
    if name == "w_q_b":
        return _qk_from_pad(g).reshape(N_CHIPS, QK_DIM, LORA)
    return g.reshape(N_CHIPS, g.shape[0] // N_CHIPS, g.shape[1])


class _MeshPlan:
    LATE = dict(attn_fwd=("w_up",), dn_chunk_fwd=("w_out", "w_gate"), dn_rec_fwd=("w_down",))
    GROUP_A = ("w_down", "w_gate", "w_up", "w_out")
    GROUP_B = ("w_in", "w_q_b", "w_kv_b")

    def __init__(self, w):
        x, y, c = _mesh_pos()
        self.ci = jnp.reshape(c, (1,)).astype(jnp.int32)
        self.pi = jnp.reshape(2 * x + y, (1,)).astype(jnp.int32)
        self.strip = {n: _shard_to_strip(n, w[n]) for n, _, _, _ in BIG}
        self.gathers, self.weights, self.g, self.acc, self.reduced = {}, {}, {}, {}, {}
        self.sib = self.chip = self.share = None

    def gather_first(self, small):
        names = ("w_in", "w_q_b", "w_kv_b")
        ex = _all_gather([self.strip[n] for n in names] + [small])
        ex.run("all_gather_first")
        got = _gathered(ex)
        for n, g4 in zip(names, got):
            self.weights[LOCAL_NAME[n]] = _strips_to_weight(n, g4)
        return got[-1]

    def weight(self, local_name):
        if local_name not in self.weights:
            for names, ex in self.gathers.values():
                for n, g4 in zip(names, _gathered(ex)):
                    self.weights[LOCAL_NAME[n]] = _strips_to_weight(n, g4)
            self.gathers = {}
        return self.weights[local_name]

    def grad(self, local_name, value):
        name = {v: k for k, v in LOCAL_NAME.items()}[local_name]
        self.g[name] = _grad_to_strips(name, value)

    def _pair_add(self, names, sib):
        for n, got in zip(names, sib.outs):
            self.acc[n] = _rs_pair_add("rs_pair_add_" + n, self.g[n], got, self.ci, BF16)

    def _chip_add(self, names, chip):
        return [_rs_chip_add("rs_chip_add_" + n, self.acc[n], got, self.pi) for n, got in zip(names, chip.outs)]

    def _take_shared(self, names, share):
        for n, strip in zip(names, _shared(share)):
            self.reduced[n] = strip

    def host(self, point):
        a, b = self.GROUP_A, self.GROUP_B
        if point in self.LATE:
            names = self.LATE[point]
            ex = _all_gather([self.strip[n] for n in names])
            self.gathers[point] = (names, ex)
            return ex
        if point == "dn_rec_bwd":
            self.sib = _rs_to_sibling([self.g[n] for n in a])
            return self.sib
        if point == "dn_chunk_bwd":
            self._pair_add(a, self.sib)
            self.chip1 = _rs_to_chips([self.acc[n] for n in a[:2]])
            return self.chip1
        if point == "attn_bwd":
            self.chip2 = _rs_to_chips([self.acc[n] for n in a[2:]])
            return self.chip2
        if point == "mla_prep_bwd":
            ress = self._chip_add(a[:2], self.chip1) + self._chip_add(a[2:], self.chip2)
            self.share = _rs_share(ress)
            return self.share
        if point == "in_dx":
            self._take_shared(a, self.share)
            self.sib = _rs_to_sibling([self.g[n] for n in b])
            return self.sib
        if point == "attn_norm_bwd":
            self._pair_add(b, self.sib)
            self.chip = _rs_to_chips([self.acc[n] for n in b])
            return self.chip
        return None

    def finish(self):
        share = _rs_share(self._chip_add(self.GROUP_B, self.chip))
        share.run("rs_share_last")
        self._take_shared(self.GROUP_B, share)
        return self.reduced


def _strip_to_shard(name, strip):
    shape, axis = {n: (s, ax) for n, s, ax, _ in BIG}[name]
    rows = shape[axis]
    return strip[:rows].T if axis == 1 else strip[:rows]


def kernel(x, meta_tokens, attn_norm_w, w_in, q_a_norm_w, w_q_b, kv_a_norm_w, w_kv_b, q_norm_w, k_norm_w, mla_out_norm_w, dn_conv_w, dn_A_log, dn_dt_bias, dn_out_norm_w, w_out, ffn_norm_w, w_gate, w_up, ffn_conv_w, ffn_conv_b, w_down, loss_target, m_meta_tokens, m_attn_norm_w, m_w_in, m_q_a_norm_w, m_w_q_b, m_kv_a_norm_w, m_w_kv_b, m_q_norm_w, m_k_norm_w, m_mla_out_norm_w, m_dn_conv_w, m_dn_A_log, m_dn_dt_bias, m_dn_out_norm_w, m_w_out, m_ffn_norm_w, m_w_gate, m_w_up, m_ffn_conv_w, m_ffn_conv_b, m_w_down, v_meta_tokens, v_attn_norm_w, v_w_in, v_q_a_norm_w, v_w_q_b, v_kv_a_norm_w, v_w_kv_b, v_q_norm_w, v_k_norm_w, v_mla_out_norm_w, v_dn_conv_w, v_dn_A_log, v_dn_dt_bias, v_dn_out_norm_w, v_w_out, v_ffn_norm_w, v_w_gate, v_w_up, v_ffn_conv_w, v_ffn_conv_b, v_w_down):
    local = dict(locals())
    w = {n: local[n] for n in WEIGHTS}
    m = {n: local["m_" + n] for n in WEIGHTS}
    v = {n: local["v_" + n] for n in WEIGHTS}
    p = 2 * lax.axis_index("x") + lax.axis_index("y")

    plan = _MeshPlan(w)
    wf = _pad_rows(jnp.concatenate([w[n].reshape(-1) for n, _, _ in SMALL_SHARDED]), SMALL_ROWS)
    gf = plan.gather_first(wf).reshape(N_CHIPS, -1)
    full = dict(plan.weights)
    off = 0
    for n, s, ax in SMALL_SHARDED:
        full[n] = _unshard(gf[:, off:off + s[0] * s[1]], s, ax)
        off += s[0] * s[1]
    for n, _ in REPLICATED:
        full[n] = w[n]
    full["ffn_conv_w"] = _ff_to_pad(full["ffn_conv_w"], 1)
    full["ffn_conv_b"] = _ff_to_pad(full["ffn_conv_b"], 1)

    sq, grad_x, g = _local_step(x[0], loss_target[0], full, plan)
    g["ffn_conv_w"] = _ff_from_pad(g["ffn_conv_w"], 1)
    g["ffn_conv_b"] = _ff_from_pad(g["ffn_conv_b"], 1)

    small_all = [n for n, _, _ in SMALL_SHARDED] + [n for n, _ in REPLICATED]
    vec = jnp.concatenate([g[n].reshape(-1) for n in small_all] + [jnp.reshape(0.5 / D_MODEL * jnp.sum(sq), (1,))])
    vec = _pad_rows(vec, -(-vec.shape[0] // (8 * LANES)) * 8)
    a2a = _all_to_all_devices(vec)
    a2a.run("small_grads_exchange")
    me = 4 * lax.axis_index("x") + 2 * lax.axis_index("y") + lax.axis_index("c")
    red = _sum_devices(lax.dynamic_update_slice(a2a.outs[0], vec[None], (me, 0, 0))).reshape(-1)
    strips = plan.finish()
    gs = {}
    off = 0
    for n in small_all:
        tot = red[off:off + g[n].size].reshape(g[n].shape)
        off += g[n].size
        shard = {sn: (s, ax) for sn, s, ax in SMALL_SHARDED}.get(n)
        if shard is not None:
            tot = lax.dynamic_slice_in_dim(tot, p * shard[0][1], shard[0][1], axis=1)
        gs[n] = tot
    loss = red[off]

    delta, new_m, new_v = {}, {}, {}
    for n, s, ax, _ in BIG:
        there = (lambda a, s=s: a.reshape(s).T) if ax == 1 else (lambda a, s=s: a.reshape(s))
        back = (lambda a, n=n: a.T.reshape(w[n].shape)) if ax == 1 else (lambda a, n=n: a.reshape(w[n].shape))
        g2, d2, m2, v2 = _adamw_call("adamw_" + n, there(w[n]), strips[n], there(m[n]), there(v[n]))
        gs[n], delta[n], new_m[n], new_v[n] = back(g2), back(d2), back(m2), back(v2)
    small_names = [n for n, _, _ in SMALL_SHARDED] + [n for n, _ in REPLICATED]
    rows = SMALL_ROWS + REP_ROWS
    pack = lambda d: _pad_rows(jnp.concatenate([d[n].reshape(-1) for n in small_names]), rows)
    _, d2, m2, v2 = _adamw_call("adamw_small", pack(w), pack(gs), pack(m), pack(v))
    off = 0
    for n in small_names:
        cnt = w[n].size
        for dst, src in ((delta, d2), (new_m, m2), (new_v, v2)):
            dst[n] = src.reshape(-1)[off:off + cnt].reshape(w[n].shape)
        off += cnt

    grad_out = [gs[n].reshape(w[n].shape) for n in WEIGHTS]
    return (loss, grad_x[None], *grad_out, *[delta[n] for n in WEIGHTS], *[new_m[n] for n in WEIGHTS],
            *[new_v[n] for n in WEIGHTS])
```

```python
import functools
import math

import jax
import jax.numpy as jnp
import numpy as np
from jax import lax
from jax.experimental import pallas as pl
from jax.experimental.pallas import tpu as pltpu

F32 = jnp.float32
BF16 = jnp.bfloat16
HI = lax.Precision.HIGHEST
MESH = pl.DeviceIdType.MESH

N_META = 16
D_MODEL = 1024
HEADS = 4
HEAD = 128
ROPE = 64
QK_DIM = HEAD + ROPE
QK_PAD = 2 * HEAD
LORA = 256
DN_WIDTH = HEADS * HEAD
CHUNK = 64
D_FF = 2816
N_CHIPS = 4
FF_SHARD = D_FF // N_CHIPS
FF_BLOCK = 768
D_FF_P = N_CHIPS * FF_BLOCK
IN_COLS = 2632
IN_SHARD = IN_COLS // N_CHIPS
IN_SHARD_P = 672
IN_PAD = 2816
NORM_EPS = 1e-6
ROPE_THETA = 10000.0
LANES = 512

ADAM_LR, ADAM_B1, ADAM_B2, ADAM_EPS, ADAM_WD, ADAM_STEP = 0.001, 0.9, 0.999, 1e-08, 0.01, 10

VMEM_LIMIT = 56 * 1024 * 1024

BIG = (("w_in", (1024, 658), 1, IN_SHARD_P), ("w_q_b", (256, 192), 1, 192), ("w_kv_b", (256, 256), 1, 256),
       ("w_out", (256, 1024), 0, 256), ("w_gate", (1024, 704), 1, FF_BLOCK), ("w_up", (1024, 704), 1, FF_BLOCK),
       ("w_down", (704, 1024), 0, FF_BLOCK))
SMALL_SHARDED = (("meta_tokens", (16, 256), 1), ("dn_conv_w", (4, 384), 1), ("ffn_conv_w", (3, 704), 1))
REPLICATED = (("attn_norm_w", 1024), ("q_a_norm_w", 256), ("kv_a_norm_w", 256), ("q_norm_w", 192), ("k_norm_w", 192),
              ("mla_out_norm_w", 128), ("dn_A_log", 4), ("dn_dt_bias", 4), ("dn_out_norm_w", 128), ("ffn_norm_w", 1024),
              ("ffn_conv_b", 2816))
WEIGHTS = ("meta_tokens", "attn_norm_w", "w_in", "q_a_norm_w", "w_q_b", "kv_a_norm_w", "w_kv_b", "q_norm_w", "k_norm_w",
           "mla_out_norm_w", "dn_conv_w", "dn_A_log", "dn_dt_bias", "dn_out_norm_w", "w_out", "ffn_norm_w", "w_gate",
           "w_up", "ffn_conv_w", "ffn_conv_b", "w_down")

SMALL_ROWS = 16
REP_ROWS = 16


def _cparams(sem):
    return pltpu.CompilerParams(dimension_semantics=sem, vmem_limit_bytes=VMEM_LIMIT)


class _Exchange:
    def __init__(self, prog, ins, out_shape, nsem):
        self.prog, self.ins, self.out_shape, self.nsem = prog, list(ins), list(out_shape), nsem
        self.outs = None

    def sems(self):
        return [pltpu.SemaphoreType.DMA((self.nsem,)), pltpu.SemaphoreType.DMA((self.nsem,))]

    def run(self, name):
        any_spec = pl.BlockSpec(memory_space=pl.ANY)
        n = len(self.ins)

        def body(*refs):
            start, finish = self.prog(refs[:n], refs[n:-2], refs[-2], refs[-1])
            start()
            finish()

        self.outs = pl.pallas_call(
            body, name=name, in_specs=[any_spec] * n, out_specs=[any_spec] * len(self.out_shape),
            out_shape=self.out_shape, scratch_shapes=self.sems(),
            compiler_params=pltpu.CompilerParams(has_side_effects=True))(*self.ins)
        return self.outs


class _SemView:
    def __init__(self, ref, off):
        self.ref, self.off = ref, off

    @property
    def at(self):
        return self

    def __getitem__(self, k):
        return self.ref.at[k + self.off]


def _merge_exchanges(first, second):
    n_in, n_out = len(first.ins), len(first.out_shape)

    def prog(in_refs, out_refs, send_sems, recv_sems):
        s1, f1 = first.prog(in_refs[:n_in], out_refs[:n_out], send_sems, recv_sems)
        s2, f2 = second.prog(in_refs[n_in:], out_refs[n_out:], _SemView(send_sems, first.nsem), _SemView(recv_sems, first.nsem))

        def start():
            s1()
            s2()

        def finish():
            f1()
            f2()

        return start, finish

    return _Exchange(prog, first.ins + second.ins, first.out_shape + second.out_shape, first.nsem + second.nsem)


def _pcall(body, name, grid, in_specs, out_specs, out_shape, args, sem, scratch_shapes=(), host=None):
    single = not isinstance(out_shape, (list, tuple))
    out_specs, out_shape = ([out_specs], [out_shape]) if single else (list(out_specs), list(out_shape))
    if host is None:
        outs = pl.pallas_call(body, name=name, grid=grid, in_specs=list(in_specs), out_specs=out_specs, out_shape=out_shape,
                              scratch_shapes=list(scratch_shapes), compiler_params=_cparams(sem))(*args)
        return outs[0] if single else outs
    any_spec = pl.BlockSpec(memory_space=pl.ANY)
    n_in, n_out, n_scr, nx_in, nx_out = len(in_specs), len(out_specs), len(scratch_shapes), len(host.ins), len(host.out_shape)

    def hosted(*refs):
        c_in, x_in = refs[:n_in], refs[n_in:n_in + nx_in]
        o0 = n_in + nx_in
        c_out, x_out = refs[o0:o0 + n_out], refs[o0 + n_out:o0 + n_out + nx_out]
        s0 = o0 + n_out + nx_out
        start, finish = host.prog(x_in, x_out, refs[s0 + n_scr], refs[s0 + n_scr + 1])
        first = functools.reduce(jnp.logical_and, [pl.program_id(d) == 0 for d in range(len(grid))])
        last = functools.reduce(jnp.logical_and, [pl.program_id(d) == grid[d] - 1 for d in range(len(grid))])
        pl.when(first)(start)
        body(*c_in, *c_out, *refs[s0:s0 + n_scr])
        pl.when(last)(finish)

    outs = pl.pallas_call(
        hosted, name=name, grid=grid, in_specs=list(in_specs) + [any_spec] * nx_in,
        out_specs=out_specs + [any_spec] * nx_out, out_shape=out_shape + host.out_shape,
        scratch_shapes=list(scratch_shapes) + host.sems(),
        compiler_params=pltpu.CompilerParams(dimension_semantics=sem, vmem_limit_bytes=VMEM_LIMIT, has_side_effects=True))(
            *args, *host.ins)
    host.outs = outs[n_out:]
    return outs[0] if single else outs[:n_out]


NN, NT, TN = ((1,), (0,)), ((1,), (1,)), ((0,), (0,))


def _shift_dims(dims, batch):
    if not batch:
        return (dims, ((), ()))
    return (((dims[0][0] + 1,), (dims[1][0] + 1,)), ((0,), (0,)))


def _make_mm(dims, exact, batch=False):
    def raw(a, b, d):
        if exact:
            return lax.dot_general(a.astype(F32), b.astype(F32), _shift_dims(d, batch), precision=HI,
                                   preferred_element_type=F32)
        return lax.dot_general(a.astype(BF16), b.astype(BF16), _shift_dims(d, batch), preferred_element_type=F32)

    @jax.custom_vjp
    def mm(a, b):
        return raw(a, b, dims)

    def fwd(a, b):
        return raw(a, b, dims), (a, b)

    def bwd(res, g):
        a, b = res
        if dims == NN:
            da, db = raw(g, b, NT), raw(a, g, TN)
        elif dims == NT:
            da, db = raw(g, b, NN), raw(g, a, TN)
        else:
            da, db = raw(b, g, NT), raw(a, g, NN)
        return da.astype(a.dtype), db.astype(b.dtype)

    mm.defvjp(fwd, bwd)
    return mm


_mm = _make_mm(NN, False)
_mm_nt = _make_mm(NT, False)
_mm_tn = _make_mm(TN, False)
_mmx = _make_mm(NN, True)
_bmm = _make_mm(NN, False, batch=True)
_bmm_nt = _make_mm(NT, False, batch=True)
_bmmx = _make_mm(NN, True, batch=True)
_bmmx_nt = _make_mm(NT, True, batch=True)
_bmmx_tn = _make_mm(TN, True, batch=True)


@jax.custom_vjp
def _unit_lower_inv(a):
    n = a.shape[-1]
    eye = (lax.broadcasted_iota(jnp.int32, a.shape, 1) == lax.broadcasted_iota(jnp.int32, a.shape, 2)).astype(F32)
    x = -a
    t = eye + x
    for _ in range(max(n.bit_length() - 2, 0)):
        x = _bmmx(x, x)
        t = t + _bmmx(t, x)
    return t


def _unit_lower_inv_fwd(a):
    t = _unit_lower_inv(a)
    return t, t


def _unit_lower_inv_bwd(t, g):
    return (-_bmmx_tn(t, _bmmx_nt(g, t)),)


_unit_lower_inv.defvjp(_unit_lower_inv_fwd, _unit_lower_inv_bwd)


def _rms(x, w, n):
    ms = jnp.sum(x * x, axis=-1, keepdims=True) * (1.0 / n)
    return x * lax.rsqrt(ms + NORM_EPS) * w


def _silu(x):
    return x * jax.nn.sigmoid(x)


def _softplus(x):
    return jnp.maximum(x, 0.0) + jnp.log(1.0 + jnp.exp(-jnp.abs(x)))


def _rope(x, cos, sin, perm):
    return x * cos + _mmx(x, perm) * sin


def _mla_prep_fn(rows, consts):
    q_lat, kv_lat, k_pe, cos, sin = rows
    qn = _rms(q_lat, consts["qa_w"], LORA)
    kvn = _rms(kv_lat, consts["kva_w"], LORA)
    outs = []
    for h in range(HEADS):
        q_n = _mm_nt(qn, consts["wq_n"][h])
        q_r = _mm_nt(qn, consts["wq_r"][h])
        rs = lax.rsqrt((jnp.sum(q_n * q_n, -1, keepdims=True) + jnp.sum(q_r * q_r, -1, keepdims=True)) * (1.0 / QK_DIM)
                       + NORM_EPS)
        q_n = q_n * rs * consts["qn_n"]
        q_r = _rope(q_r * rs * consts["qn_r"], cos, sin, consts["perm"])
        k_n = _mm_nt(kvn, consts["wk_n"][h])
        v = _mm_nt(kvn, consts["wv"][h])
        rk = lax.rsqrt((jnp.sum(k_n * k_n, -1, keepdims=True) + jnp.sum(k_pe * k_pe, -1, keepdims=True)) * (1.0 / QK_DIM)
                       + NORM_EPS)
        k_n = k_n * rk * consts["kn_n"]
        k_r = _rope(k_pe * rk * consts["kn_r"], cos, sin, consts["perm"])
        outs += [q_n, q_r, k_n, k_r, v]
    return tuple(outs)


def _attn_fn(q, k, v, row0):
    s = _mm_nt(q, k) * (1.0 / math.sqrt(QK_DIM))
    qpos = row0 + lax.broadcasted_iota(jnp.int32, s.shape, 0)
    kpos = lax.broadcasted_iota(jnp.int32, s.shape, 1)
    s = jnp.where(kpos <= qpos, s, -1e30)
    m = lax.stop_gradient(jnp.max(s, axis=-1, keepdims=True))
    p = jnp.exp(s - m)
    p = p / jnp.sum(p, axis=-1, keepdims=True)
    return _mm(p, v)


def _dn_prep_fn(rows, consts):
    qc, kc, ab = rows
    a_b = _mmx(ab, consts["sel_a"])
    b_b = _mmx(ab, consts["sel_b"])
    beta = jax.nn.sigmoid(b_b)
    g = -jnp.exp(consts["alog"]) * _softplus(a_b + consts["dtb"])
    qs, ks = [], []
    for h in range(HEADS):
        q, k = qc[h], kc[h]
        qs.append(q * lax.rsqrt(jnp.sum(q * q, -1, keepdims=True) + NORM_EPS))
        ks.append(k * lax.rsqrt(jnp.sum(k * k, -1, keepdims=True) + NORM_EPS))
    return tuple(qs), tuple(ks), g, beta


def _dn_chunk_fn(q, k, v, gb, g64, bb):
    nb = q.shape[0]
    ri = lax.broadcasted_iota(jnp.int32, (nb, CHUNK, CHUNK), 1)
    ci = lax.broadcasted_iota(jnp.int32, (nb, CHUNK, CHUNK), 2)
    tri = ri >= ci
    strict = ri > ci
    tril = tri.astype(F32)
    eye = (ri == ci).astype(F32)
    ones = jnp.ones((nb, CHUNK, CHUNK), F32)
    gc = _bmmx(tril, gb)
    gc64 = _bmmx(tril, g64)
    grow = _bmmx(ones, eye * gc64)
    diff = gc64 - grow
    decay = jnp.where(tri, jnp.exp(jnp.where(tri, diff, 0.0)), 0.0)
    kb = k * bb
    vb = v * bb
    a = jnp.where(strict, _bmm_nt(kb, k) * decay, 0.0)
    tinv = _unit_lower_inv(a)
    u = _bmm(tinv, vb)
    w = _bmm(tinv, kb * jnp.exp(gc))
    qs = q * (1.0 / math.sqrt(HEAD))
    qk = _bmm_nt(qs, k) * decay
    qg = qs * jnp.exp(gc)
    glast = jnp.sum(gb, axis=1, keepdims=True)
    kdec = k * jnp.exp(glast - gc)
    eg = jnp.broadcast_to(jnp.exp(glast), gb.shape)
    return u, w, qg, kdec, eg, qk


def _dn_rec_fn(s, u, w, qg, qk, kdec, eg):
    v_new = u - _mm(w, s)
    o = _mm(qg, s) + _mm(qk, v_new)
    s_new = s * eg + _mm_tn(kdec, v_new)
    return s_new, o


def _dn_out_fn(o, z, w):
    return _rms(o, w, HEAD) * _silu(z)


def _row_tile(t):
    return t // 8 if (t // 8) % 16 == 0 else t


def _tile(n, pref, unit):
    best = n
    for cand in range(unit, min(n, pref) + 1, unit):
        if n % cand == 0:
            best = cand
    return best if best <= pref else n


def _rows_call(name, body, rows, consts, outs, accs, r, host=None):
    rows = [a if isinstance(a, tuple) else (a, a.shape[1], 0) for a in rows]
    t = rows[0][0].shape[0]
    zero = lambda nd: (lambda i: (0,) * nd)
    in_specs = [pl.BlockSpec((r, w), functools.partial(lambda i, b: (i, b), b=blk)) for _, w, blk in rows]
    rows = [a for a, _, _ in rows]
    in_specs += [pl.BlockSpec(a.shape, zero(a.ndim)) for a in consts]
    out_shape = [jax.ShapeDtypeStruct((t, w), dt) for w, dt in outs] + [jax.ShapeDtypeStruct(s, F32) for s in accs]
    out_specs = [pl.BlockSpec((r, w), lambda i: (i, 0)) for w, _ in outs] + [pl.BlockSpec(s, zero(len(s))) for s in accs]
    return _pcall(body, name, (t // r,), in_specs, out_specs, out_shape, [*rows, *consts], ("arbitrary",), host=host)


def _accumulate(ref, val):
    @pl.when(pl.program_id(0) == 0)
    def _():
        ref[...] = jnp.zeros(ref.shape, ref.dtype)

    ref[...] += val


def _matmul(name, a, b, dims, out_dtype, res=None, host=None):
    if dims == "nn":
        (m, k), n = a.shape, b.shape[1]
    elif dims == "nt":
        (m, k), n = a.shape, b.shape[0]
    else:
        (k, m), n = a.shape, b.shape[1]
    tm = _tile(m, 1100, 16) if dims != "tn" else _tile(m, 640, 128)
    tn = _tile(n, 1408, 128)
    if dims == "nn":
        a_spec, b_spec, dn = pl.BlockSpec((tm, k), lambda i, j: (i, 0)), pl.BlockSpec((k, tn), lambda i, j: (0, j)), NN
    elif dims == "nt":
        a_spec, b_spec, dn = pl.BlockSpec((tm, k), lambda i, j: (i, 0)), pl.BlockSpec((tn, k), lambda i, j: (j, 0)), NT
    else:
        a_spec, b_spec, dn = pl.BlockSpec((k, tm), lambda i, j: (0, i)), pl.BlockSpec((k, tn), lambda i, j: (0, j)), TN
    o_spec = pl.BlockSpec((tm, tn), lambda i, j: (i, j))

    def body(*refs):
        a_ref, b_ref, o_ref = refs[0], refs[1], refs[-1]
        acc = lax.dot_general(a_ref[...].astype(BF16), b_ref[...].astype(BF16), (dn, ((), ())),
                              preferred_element_type=F32)
        if res is not None:
            acc = acc + refs[2][...]
        o_ref[...] = acc.astype(out_dtype)

    ins = [a, b] + ([res] if res is not None else [])
    specs = [a_spec, b_spec] + ([o_spec] if res is not None else [])
    return _pcall(body, name, (m // tm, n // tn), specs, o_spec, jax.ShapeDtypeStruct((m, n), out_dtype), ins,
                  ("arbitrary", "arbitrary"), host=host)


def _rms_fwd(name, h, w):
    n = h.shape[1]

    def body(h_ref, w_ref, o_ref):
        o_ref[...] = _rms(h_ref[...], w_ref[...], n).astype(BF16)

    return _rows_call(name, body, [h], [w], [(n, BF16)], [], _row_tile(h.shape[0]))[0]


def _rms_bwd(name, h, w, cts, resid, host=None):
    n = h.shape[1]
    nct = len(cts)

    def body(*refs):
        h_ref, ct_refs, r_ref, w_ref = refs[0], refs[1:1 + nct], refs[1 + nct], refs[2 + nct]
        dh_ref, dh16_ref, dw_ref = refs[-3], refs[-2], refs[-1]
        ct = ct_refs[0][...].astype(F32)
        for c in ct_refs[1:]:
            ct = ct + c[...].astype(F32)
        _, vjp = jax.vjp(lambda x, ww: _rms(x, ww, n), h_ref[...], w_ref[...])
        dh, dw = vjp(ct)
        dh = dh + r_ref[...]
        dh_ref[...] = dh
        dh16_ref[...] = dh.astype(BF16)
        _accumulate(dw_ref, dw)

    return _rows_call(name, body, [h, *cts, resid], [w], [(n, F32), (n, BF16)], [(1, n)], _row_tile(h.shape[0]), host=host)


def _mla_consts_from_refs(qa, wq, kva, wkv, qn, kn, perm):
    f = lambda r: r[...].astype(F32)
    return dict(
        qa_w=f(qa), kva_w=f(kva), perm=f(perm),
        wq_n=[wq[h * QK_PAD:h * QK_PAD + HEAD, :].astype(F32) for h in range(HEADS)],
        wq_r=[wq[h * QK_PAD + HEAD:(h + 1) * QK_PAD, :].astype(F32) for h in range(HEADS)],
        wk_n=[wkv[h * QK_PAD:h * QK_PAD + HEAD, :].astype(F32) for h in range(HEADS)],
        wv=[wkv[h * QK_PAD + HEAD:(h + 1) * QK_PAD, :].astype(F32) for h in range(HEADS)],
        qn_n=qn[:, 0:HEAD], qn_r=qn[:, HEAD:QK_PAD], kn_n=kn[:, 0:HEAD], kn_r=kn[:, HEAD:QK_PAD])


def _mla_prep_fwd(q_lat, kv_lat, k_pe, cos, sin, qa, wq, kva, wkv, qn, kn, perm):
    def body(ql, kvl, kp, c, s, qa_r, wq_r, kva_r, wkv_r, qn_r, kn_r, p_r, q_out, k_out, v_out):
        consts = _mla_consts_from_refs(qa_r, wq_r, kva_r, wkv_r, qn_r, kn_r, p_r)
        outs = _mla_prep_fn((ql[...], kvl[...], kp[...], c[...], s[...]), consts)
        for h in range(HEADS):
            q_n, q_r, k_n, k_r, v = outs[5 * h:5 * h + 5]
            q_out[:, h * QK_PAD:h * QK_PAD + HEAD] = q_n.astype(BF16)
            q_out[:, h * QK_PAD + HEAD:(h + 1) * QK_PAD] = q_r.astype(BF16)
            k_out[:, h * QK_PAD:h * QK_PAD + HEAD] = k_n.astype(BF16)
            k_out[:, h * QK_PAD + HEAD:(h + 1) * QK_PAD] = k_r.astype(BF16)
            v_out[:, h * HEAD:(h + 1) * HEAD] = v.astype(BF16)

    return _rows_call("mla_prep_fwd", body, [q_lat, kv_lat, k_pe, cos, sin], [qa, wq, kva, wkv, qn, kn, perm],
                      [(HEADS * QK_PAD, BF16), (HEADS * QK_PAD, BF16), (DN_WIDTH, BF16)], [], _row_tile(cos.shape[0]))


def _mla_prep_bwd(q_lat, kv_lat, k_pe, cos, sin, dq, dk, dv, qa, wq, kva, wkv, qn, kn, perm, host=None):
    def body(ql, kvl, kp, c, s, dq_r, dk_r, dv_r, qa_r, wq_r, kva_r, wkv_r, qn_r, kn_r, p_r,
             dql, dkvl, dkp, dqa, dwq, dkva, dwkv, dqn, dkn):
        consts = _mla_consts_from_refs(qa_r, wq_r, kva_r, wkv_r, qn_r, kn_r, p_r)
        cc, ss, pm = c[...], s[...], consts.pop("perm")
        _, vjp = jax.vjp(lambda rows, cs: _mla_prep_fn((*rows, cc, ss), dict(cs, perm=pm)), (ql[...], kvl[...], kp[...]),
                         consts)
        cts = []
        for h in range(HEADS):
            cts += [dq_r[:, h * QK_PAD:h * QK_PAD + HEAD], dq_r[:, h * QK_PAD + HEAD:(h + 1) * QK_PAD],
                    dk_r[:, h * QK_PAD:h * QK_PAD + HEAD], dk_r[:, h * QK_PAD + HEAD:(h + 1) * QK_PAD],
                    dv_r[:, h * HEAD:(h + 1) * HEAD]]
        (d_ql, d_kvl, d_kp), dc = vjp(tuple(cts))
        dql[...] = d_ql.astype(BF16)
        dkvl[...] = d_kvl.astype(BF16)
        dkp[...] = d_kp.astype(BF16)
        first = pl.program_id(0) == 0

        def acc(ref, sl, val):
            @pl.when(first)
            def _():
                ref[sl] = val

            @pl.when(jnp.logical_not(first))
            def _():
                ref[sl] += val

        full = (slice(None), slice(None))
        acc(dqa, full, dc["qa_w"])
        acc(dkva, full, dc["kva_w"])
        for h in range(HEADS):
            acc(dwq, (slice(h * QK_PAD, h * QK_PAD + HEAD), slice(None)), dc["wq_n"][h])
            acc(dwq, (slice(h * QK_PAD + HEAD, (h + 1) * QK_PAD), slice(None)), dc["wq_r"][h])
            acc(dwkv, (slice(h * QK_PAD, h * QK_PAD + HEAD), slice(None)), dc["wk_n"][h])
            acc(dwkv, (slice(h * QK_PAD + HEAD, (h + 1) * QK_PAD), slice(None)), dc["wv"][h])
        acc(dqn, (slice(None), slice(0, HEAD)), dc["qn_n"])
        acc(dqn, (slice(None), slice(HEAD, QK_PAD)), dc["qn_r"])
        acc(dkn, (slice(None), slice(0, HEAD)), dc["kn_n"])
        acc(dkn, (slice(None), slice(HEAD, QK_PAD)), dc["kn_r"])

    return _rows_call("mla_prep_bwd", body, [q_lat, kv_lat, k_pe, cos, sin, dq, dk, dv],
                      [qa, wq, kva, wkv, qn, kn, perm],
                      [(LORA, BF16), (LORA, BF16), (HEAD, BF16)],
                      [(1, LORA), wq.shape, (1, LORA), wkv.shape, (1, QK_PAD), (1, QK_PAD)], _row_tile(cos.shape[0]),
                      host=host)


ATTN_Q_ROWS = 256


def _attn_blocks(t):
    return [(r0, min(ATTN_Q_ROWS, t - r0)) for r0 in range(0, t, ATTN_Q_ROWS)]


def _attn_fwd(q, k, v, host=None):
    t = q.shape[0]

    def body(q_ref, k_ref, v_ref, o_ref):
        for r0, rows in _attn_blocks(t):
            ext = r0 + rows
            o_ref[r0:ext, :] = _attn_fn(q_ref[r0:ext, :], k_ref[0:ext, :], v_ref[0:ext, :], r0)

    qk_spec = pl.BlockSpec((t, QK_PAD), lambda h: (0, h))
    v_spec = pl.BlockSpec((t, HEAD), lambda h: (0, h))
    return _pcall(body, "attn_fwd", (HEADS,), [qk_spec, qk_spec, v_spec], v_spec,
                  jax.ShapeDtypeStruct((t, HEADS * HEAD), F32), [q, k, v], ("arbitrary",), host=host)


def _attn_bwd(q, k, v, do, host=None):
    t = q.shape[0]

    def body(q_ref, k_ref, v_ref, do_ref, dq_ref, dk_ref, dv_ref):
        dk_ref[...] = jnp.zeros(dk_ref.shape, F32)
        dv_ref[...] = jnp.zeros(dv_ref.shape, F32)
        for r0, rows in _attn_blocks(t):
            ext = r0 + rows
            _, vjp = jax.vjp(functools.partial(_attn_fn, row0=r0), q_ref[r0:ext, :].astype(F32),
                             k_ref[0:ext, :].astype(F32), v_ref[0:ext, :].astype(F32))
            dq, dk, dv = vjp(do_ref[r0:ext, :])
            dq_ref[r0:ext, :] = dq
            dk_ref[0:ext, :] += dk
            dv_ref[0:ext, :] += dv

    qk_spec = pl.BlockSpec((t, QK_PAD), lambda h: (0, h))
    v_spec = pl.BlockSpec((t, HEAD), lambda h: (0, h))
    return _pcall(body, "attn_bwd", (HEADS,), [qk_spec, qk_spec, v_spec, v_spec], [qk_spec, qk_spec, v_spec],
                  [jax.ShapeDtypeStruct((t, HEADS * QK_PAD), F32), jax.ShapeDtypeStruct((t, HEADS * QK_PAD), F32),
                   jax.ShapeDtypeStruct((t, HEADS * HEAD), F32)], [q, k, v, do], ("arbitrary",), host=host)


def _mix_out_fwd(o_mla, o_dn, z, w_mla, w_dn):
    def body(om_ref, od_ref, z_ref, wm_ref, wd_ref, o_ref):
        for h in range(HEADS):
            sl = slice(h * HEAD, (h + 1) * HEAD)
            o_ref[:, sl] = _rms(om_ref[:, sl], wm_ref[...], HEAD).astype(BF16)
            o_ref[:, DN_WIDTH + h * HEAD:DN_WIDTH + (h + 1) * HEAD] = _dn_out_fn(od_ref[:, sl], z_ref[:, sl],
                                                                                 wd_ref[...]).astype(BF16)

    return _rows_call("mix_out_fwd", body, [o_mla, o_dn, z], [w_mla, w_dn], [(2 * DN_WIDTH, BF16)], [],
                      _row_tile(o_mla.shape[0]))[0]


def _mix_out_bwd(o_mla, o_dn, z, dmixed, w_mla, w_dn):
    def body(om_ref, od_ref, z_ref, dm_ref, wm_ref, wd_ref, dom_ref, dod_ref, dz_ref, dwm_ref, dwd_ref):
        dwm = dwd = None
        for h in range(HEADS):
            sl = slice(h * HEAD, (h + 1) * HEAD)
            _, vjp = jax.vjp(lambda o, w: _rms(o, w, HEAD), om_ref[:, sl], wm_ref[...])
            do, dw = vjp(dm_ref[:, sl])
            dom_ref[:, sl] = do
            dwm = dw if dwm is None else dwm + dw
            _, vjp = jax.vjp(_dn_out_fn, od_ref[:, sl], z_ref[:, sl], wd_ref[...])
            do, dz, dw = vjp(dm_ref[:, DN_WIDTH + h * HEAD:DN_WIDTH + (h + 1) * HEAD])
            dod_ref[:, sl] = do
            dz_ref[:, sl] = dz.astype(BF16)
            dwd = dw if dwd is None else dwd + dw
        _accumulate(dwm_ref, dwm)
        _accumulate(dwd_ref, dwd)

    return _rows_call("mix_out_bwd", body, [o_mla, o_dn, z, dmixed], [w_mla, w_dn],
                      [(DN_WIDTH, F32), (DN_WIDTH, F32), (DN_WIDTH, BF16)], [(1, HEAD), (1, HEAD)],
                      _row_tile(o_mla.shape[0]))


def _shift_down(x, s):
    if s == 0:
        return x
    rows = lax.broadcasted_iota(jnp.int32, x.shape, 0)
    return jnp.where(rows >= s, pltpu.roll(x, s, 0), 0.0)


def _shift_up(x, s):
    if s == 0:
        return x
    t = x.shape[0]
    rows = lax.broadcasted_iota(jnp.int32, x.shape, 0)
    return jnp.where(rows < t - s, pltpu.roll(x, t - s, 0), 0.0)


def _col_call(name, body, cols, taps, outs, tap_outs, cw):
    t, c = cols[0].shape[0], taps[0].shape[1]
    in_specs = [pl.BlockSpec((t, cw), lambda j: (0, j)) for _ in cols]
    in_specs += [pl.BlockSpec((a.shape[0], cw), lambda j: (0, j)) for a in taps]
    out_shape = [jax.ShapeDtypeStruct((t, c), dt) for dt in outs] + [jax.ShapeDtypeStruct((n, c), F32) for n in tap_outs]
    out_specs = [pl.BlockSpec((t, cw), lambda j: (0, j)) for _ in outs]
    out_specs += [pl.BlockSpec((n, cw), lambda j: (0, j)) for n in tap_outs]
    return pl.pallas_call(body, name=name, grid=(c // cw,), in_specs=in_specs, out_specs=out_specs, out_shape=out_shape,
                          compiler_params=_cparams(("arbitrary",)))(*cols, *taps)


def _causal_conv(x, w_ref, width):
    acc = w_ref[width - 1:width, :] * x
    for j in range(width - 1):
        acc = acc + w_ref[j:j + 1, :] * _shift_down(x, width - 1 - j)
    return acc


def _causal_conv_bwd(x, dpre, w_ref, dx_ref, dw_ref, width):
    dx = w_ref[width - 1:width, :] * dpre
    dw_ref[width - 1:width, :] = jnp.sum(dpre * x, axis=0, keepdims=True)
    for j in range(width - 1):
        s = width - 1 - j
        dx = dx + w_ref[j:j + 1, :] * _shift_up(dpre, s)
        dw_ref[j:j + 1, :] = jnp.sum(dpre * _shift_down(x, s), axis=0, keepdims=True)
    dx_ref[...] = dx.astype(dx_ref.dtype)


def _dsilu(x):
    sg = jax.nn.sigmoid(x)
    return sg * (1.0 + x * (1.0 - sg))


def _dn_conv_fwd(x, w):
    def body(x_ref, w_ref, y_ref):
        y_ref[...] = _silu(_causal_conv(x_ref[...], w_ref, 4))

    return _col_call("dn_conv_fwd", body, [x], [w], [F32], [], 256)[0]


def _dn_conv_bwd(x, w, dy):
    def body(x_ref, dy_ref, w_ref, dx_ref, dw_ref):
        xv = x_ref[...]
        dpre = dy_ref[...] * _dsilu(_causal_conv(xv, w_ref, 4))
        _causal_conv_bwd(xv, dpre, w_ref, dx_ref, dw_ref, 4)

    return _col_call("dn_conv_bwd", body, [x, dy], [w], [BF16], [4], 256)


def _glu_fwd(gpre, up, w, b):
    def body(g_ref, u_ref, w_ref, b_ref, a_ref):
        gate = _causal_conv(g_ref[...], w_ref, 3) + b_ref[...]
        a_ref[...] = (_silu(gate) * u_ref[...]).astype(BF16)

    return _col_call("glu_fwd", body, [gpre, up], [w, b], [BF16], [], 256)[0]


def _glu_bwd(gpre, up, w, b, dact):
    def body(g_ref, u_ref, da_ref, w_ref, b_ref, dg_ref, du_ref, dw_ref, db_ref):
        gv = g_ref[...]
        gate = _causal_conv(gv, w_ref, 3) + b_ref[...]
        da = da_ref[...]
        sg = jax.nn.sigmoid(gate)
        du_ref[...] = (da * (gate * sg)).astype(BF16)
        dgate = da * u_ref[...] * (sg * (1.0 + gate * (1.0 - sg)))
        db_ref[...] = jnp.sum(dgate, axis=0, keepdims=True)
        _causal_conv_bwd(gv, dgate, w_ref, dg_ref, dw_ref, 3)

    return _col_call("glu_bwd", body, [gpre, up, dact], [w, b], [BF16, BF16], [3, 1], 256)


def _dn_prep_consts(sa, sb, al, dt):
    return dict(sel_a=sa[...], sel_b=sb[...], alog=al[...], dtb=dt[...])


def _dn_prep_fwd(conv, ab, sel_a, sel_b, alog, dtb):
    def body(c_ref, ab_ref, sa, sb, al, dt, q_out, k_out, g_out, b_out):
        qc = tuple(c_ref[:, h * HEAD:(h + 1) * HEAD] for h in range(HEADS))
        kc = tuple(c_ref[:, DN_WIDTH + h * HEAD:DN_WIDTH + (h + 1) * HEAD] for h in range(HEADS))
        qs, ks, g, beta = _dn_prep_fn((qc, kc, ab_ref[...]), _dn_prep_consts(sa, sb, al, dt))
        for h in range(HEADS):
            q_out[:, h * HEAD:(h + 1) * HEAD] = qs[h]
            k_out[:, h * HEAD:(h + 1) * HEAD] = ks[h]
        g_out[...] = g
        b_out[...] = beta

    return _rows_call("dn_prep_fwd", body, [conv, ab], [sel_a, sel_b, alog, dtb], [(DN_WIDTH, F32)] * 4, [],
                      _row_tile(conv.shape[0]))


def _dn_prep_bwd(conv, ab, dq, dk, dv, dg, db, sel_a, sel_b, alog, dtb):
    def body(c_ref, ab_ref, dq_r, dk_r, dv_r, dg_r, db_r, sa, sb, al, dt, dc_out, dab_out, dal_out, ddt_out):
        qc = tuple(c_ref[:, h * HEAD:(h + 1) * HEAD] for h in range(HEADS))
        kc = tuple(c_ref[:, DN_WIDTH + h * HEAD:DN_WIDTH + (h + 1) * HEAD] for h in range(HEADS))
        consts = _dn_prep_consts(sa, sb, al, dt)
        sel = dict(sel_a=consts["sel_a"], sel_b=consts["sel_b"])
        _, vjp = jax.vjp(lambda rows, ad: _dn_prep_fn(rows, {**sel, **ad}), (qc, kc, ab_ref[...]),
                         dict(alog=consts["alog"], dtb=consts["dtb"]))
        cq = tuple(dq_r[:, h * HEAD:(h + 1) * HEAD] for h in range(HEADS))
        ck = tuple(dk_r[:, h * HEAD:(h + 1) * HEAD] for h in range(HEADS))
        (dqc, dkc, dab), dad = vjp((cq, ck, dg_r[...], db_r[...]))
        for h in range(HEADS):
            dc_out[:, h * HEAD:(h + 1) * HEAD] = dqc[h]
            dc_out[:, DN_WIDTH + h * HEAD:DN_WIDTH + (h + 1) * HEAD] = dkc[h]
        dc_out[:, 2 * DN_WIDTH:3 * DN_WIDTH] = dv_r[...]
        dab_out[...] = dab.astype(BF16)
        _accumulate(dal_out, dad["alog"])
        _accumulate(ddt_out, dad["dtb"])

    return _rows_call("dn_prep_bwd", body, [conv, ab, dq, dk, dv, dg, db], [sel_a, sel_b, alog, dtb],
                      [(3 * DN_WIDTH, F32), (HEAD, BF16)], [(1, DN_WIDTH), (1, DN_WIDTH)], _row_tile(conv.shape[0]))


def _chunk_batch(t):
    nc = t // CHUNK
    return nc // 2 if nc % 2 == 0 else nc


def _dn_chunk_specs(t, nb):
    rows = nb * CHUNK
    hb = lambda h, b: (b, h)
    vb = lambda h, b: (b, 2 * HEADS + h)
    qk_spec = pl.BlockSpec((None, rows, CHUNK), lambda h, b: (h, b, 0))
    blk = pl.BlockSpec((rows, HEAD), hb)
    return rows, blk, pl.BlockSpec((rows, HEAD), vb), qk_spec


def _dn_chunk_fwd(qn, kn, conv, g, beta, host=None):
    t = qn.shape[0]
    nb = _chunk_batch(t)
    rows, blk, vblk, qk_spec = _dn_chunk_specs(t, nb)

    def body(q_ref, k_ref, v_ref, g_ref, b_ref, u_o, w_o, qg_o, kd_o, eg_o, qk_o):
        r3 = lambda x: x.reshape(nb, CHUNK, x.shape[-1])
        outs = _dn_chunk_fn(r3(q_ref[...]), r3(k_ref[...]), r3(v_ref[...]), r3(g_ref[...]), r3(g_ref[:, 0:CHUNK]),
                            r3(b_ref[...]))
        for o_ref, val in zip((u_o, w_o, qg_o, kd_o, eg_o, qk_o), outs):
            o_ref[...] = val.reshape(rows, val.shape[-1])

    return _pcall(body, "dn_chunk_fwd", (HEADS, t // rows), [blk, blk, vblk, blk, blk], [blk] * 5 + [qk_spec],
                  [jax.ShapeDtypeStruct((t, DN_WIDTH), F32)] * 5 + [jax.ShapeDtypeStruct((HEADS, t, CHUNK), F32)],
                  [qn, kn, conv, g, beta], ("arbitrary", "arbitrary"), host=host)


def _dn_chunk_bwd(qn, kn, conv, g, beta, cts, host=None):
    t = qn.shape[0]
    nb = _chunk_batch(t)
    rows, blk, vblk, qk_spec = _dn_chunk_specs(t, nb)

    def body(q_ref, k_ref, v_ref, g_ref, b_ref, du, dw, dqg, dkd, deg, dqk, dq_o, dk_o, dv_o, dg_o, db_o):
        r3 = lambda x: x.reshape(nb, CHUNK, x.shape[-1])
        _, vjp = jax.vjp(_dn_chunk_fn, r3(q_ref[...]), r3(k_ref[...]), r3(v_ref[...]), r3(g_ref[...]),
                         r3(g_ref[:, 0:CHUNK]), r3(b_ref[...]))
        dq, dk, dv, dg, dg64, db = vjp(tuple(r3(c[...]) for c in (du, dw, dqg, dkd, deg, dqk)))
        for o_ref, val in zip((dq_o, dk_o, dv_o, dg_o, db_o), (dq, dk, dv, dg, db)):
            o_ref[...] = val.reshape(rows, HEAD)
        dg_o[:, 0:CHUNK] += dg64.reshape(rows, CHUNK)

    return _pcall(body, "dn_chunk_bwd", (HEADS, t // rows), [blk, blk, vblk, blk, blk] + [blk] * 5 + [qk_spec], [blk] * 5,
                  [jax.ShapeDtypeStruct((t, DN_WIDTH), F32)] * 5, [qn, kn, conv, g, beta, *cts], ("arbitrary", "arbitrary"),
                  host=host)


def _dn_rec_fwd(u, w, qg, kd, eg, qk, host=None):
    t = u.shape[0]
    nc = t // CHUNK
    blk = pl.BlockSpec((CHUNK, DN_WIDTH), lambda c: (c, 0))
    qk_spec = pl.BlockSpec((HEADS, CHUNK, CHUNK), lambda c: (0, c, 0))
    s_spec = pl.BlockSpec((None, DN_WIDTH, HEAD), lambda c: (c, 0, 0))

    def body(u_ref, w_ref, qg_ref, kd_ref, eg_ref, qk_ref, o_ref, sall_ref, s_scr):
        @pl.when(pl.program_id(0) == 0)
        def _():
            s_scr[...] = jnp.zeros(s_scr.shape, F32)

        sall_ref[...] = s_scr[...]
        for h in range(HEADS):
            sl = slice(h * HEAD, (h + 1) * HEAD)
            s_new, o = _dn_rec_fn(s_scr[sl, :], u_ref[:, sl], w_ref[:, sl], qg_ref[:, sl], qk_ref[h], kd_ref[:, sl],
                                  eg_ref[0:1, sl])
            o_ref[:, sl] = o
            s_scr[sl, :] = s_new

    return _pcall(body, "dn_rec_fwd", (nc,), [blk] * 5 + [qk_spec], [blk, s_spec],
                  [jax.ShapeDtypeStruct((t, DN_WIDTH), F32), jax.ShapeDtypeStruct((nc, DN_WIDTH, HEAD), F32)],
                  [u, w, qg, kd, eg, qk], ("arbitrary",), scratch_shapes=[pltpu.VMEM((DN_WIDTH, HEAD), F32)], host=host)


def _dn_rec_bwd(u, w, qg, kd, eg, qk, sall, do, host=None):
    t = u.shape[0]
    nc = t // CHUNK
    blk = pl.BlockSpec((CHUNK, DN_WIDTH), lambda c: (nc - 1 - c, 0))
    qk_spec = pl.BlockSpec((HEADS, CHUNK, CHUNK), lambda c: (0, nc - 1 - c, 0))
    s_spec = pl.BlockSpec((None, DN_WIDTH, HEAD), lambda c: (nc - 1 - c, 0, 0))

    def body(u_ref, w_ref, qg_ref, kd_ref, eg_ref, qk_ref, s_ref, do_ref, du_o, dw_o, dqg_o, dkd_o, deg_o, dqk_o, ds_scr):
        @pl.when(pl.program_id(0) == 0)
        def _():
            ds_scr[...] = jnp.zeros(ds_scr.shape, F32)

        deg_o[...] = jnp.zeros(deg_o.shape, F32)
        for h in range(HEADS):
            sl = slice(h * HEAD, (h + 1) * HEAD)
            _, vjp = jax.vjp(_dn_rec_fn, s_ref[sl, :], u_ref[:, sl], w_ref[:, sl], qg_ref[:, sl], qk_ref[h],
                             kd_ref[:, sl], eg_ref[0:1, sl])
            ds, du, dw, dqg, dqk, dkd, deg = vjp((ds_scr[sl, :], do_ref[:, sl]))
            du_o[:, sl] = du
            dw_o[:, sl] = dw
            dqg_o[:, sl] = dqg
            dkd_o[:, sl] = dkd
            deg_o[0:1, sl] = deg
            dqk_o[h] = dqk
            ds_scr[sl, :] = ds

    return _pcall(body, "dn_rec_bwd", (nc,), [blk] * 5 + [qk_spec, s_spec, blk], [blk] * 5 + [qk_spec],
                  [jax.ShapeDtypeStruct((t, DN_WIDTH), F32)] * 5 + [jax.ShapeDtypeStruct((HEADS, t, CHUNK), F32)],
                  [u, w, qg, kd, eg, qk, sall, do], ("arbitrary",), scratch_shapes=[pltpu.VMEM((DN_WIDTH, HEAD), F32)],
                  host=host)


def _loss_call(h2, tgt, n_valid):
    t, n = h2.shape
    r = _row_tile(t)

    def body(h_ref, t_ref, dy_ref, dy16_ref, acc_ref):
        rows = pl.program_id(0) * r + lax.broadcasted_iota(jnp.int32, (r, n), 0)
        valid = jnp.logical_and(rows >= N_META, rows < n_valid)
        e = jnp.where(valid, h_ref[...] - t_ref[...], 0.0)
        dy = e * (1.0 / n)
        dy_ref[...] = dy
        dy16_ref[...] = dy.astype(BF16)
        _accumulate(acc_ref, jnp.sum(e * e, axis=0, keepdims=True))

    return _rows_call("loss", body, [h2, tgt], [], [(n, F32), (n, BF16)], [(1, n)], r)


def _adamw_call(name, w, g, m, v):
    rows, cols = w.shape
    by_rows = rows % 8 == 0

    def body(w_ref, g_ref, m_ref, v_ref, g_out, d_ref, m_out, v_out):
        gv = g_ref[...] if by_rows else g_ref[0:rows, :]
        m2 = ADAM_B1 * m_ref[...] + (1.0 - ADAM_B1) * gv
        v2 = ADAM_B2 * v_ref[...] + (1.0 - ADAM_B2) * (gv * gv)
        m_hat = m2 / (1.0 - ADAM_B1 ** ADAM_STEP)
        v_hat = v2 / (1.0 - ADAM_B2 ** ADAM_STEP)
        g_out[...] = gv
        d_ref[...] = -ADAM_LR * (m_hat / (jnp.sqrt(v_hat) + ADAM_EPS) + ADAM_WD * w_ref[...])
        m_out[...] = m2
        v_out[...] = v2

    if by_rows:
        tr = _tile(rows, 256, 8)
        spec = g_spec = pl.BlockSpec((tr, cols), lambda i: (i, 0))
        grid = (rows // tr,)
    else:
        tc = _tile(cols, 256, 128)
        spec = pl.BlockSpec((rows, tc), lambda j: (0, j))
        g_spec = pl.BlockSpec((g.shape[0], tc), lambda j: (0, j))
        grid = (cols // tc,)
    return pl.pallas_call(body, name=name, grid=grid, in_specs=[spec, g_spec, spec, spec], out_specs=[spec] * 4,
                          out_shape=[jax.ShapeDtypeStruct((rows, cols), F32)] * 4,
                          compiler_params=_cparams(("arbitrary",)))(w, g, m, v)


def _rope_tables(t):
    half = ROPE // 2
    inv_freq = np.float32(ROPE_THETA) ** (-np.arange(half, dtype=np.float32) / np.float32(half))
    ang = np.arange(t, dtype=np.float32)[:, None] * inv_freq[None, :].astype(np.float32)
    z = np.zeros((t, HEAD - ROPE), np.float32)
    cos = np.concatenate([np.cos(ang), np.cos(ang), z], axis=1).astype(np.float32)
    sin = np.concatenate([np.sin(ang), np.sin(ang), z], axis=1).astype(np.float32)
    k = np.arange(HEAD)[:, None]
    l = np.arange(HEAD)[None, :]
    perm = np.where((l < half) & (k == l + half), -1.0, 0.0) + np.where((l >= half) & (l < ROPE) & (k == l - half), 1.0, 0.0)
    return jnp.asarray(cos), jnp.asarray(sin), jnp.asarray(perm.astype(np.float32))


def _win_to_pad(w):
    z = lambda n: jnp.zeros((n, w.shape[1]), w.dtype)
    return jnp.concatenate([w[576:2112], w[2112:2624], w[0:256], w[256:512], w[512:576], z(64), w[2624:2632], z(120)],
                           axis=0)


def _win_from_pad(g):
    return jnp.concatenate([g[2048:2304], g[2304:2560], g[2560:2624], g[0:1536], g[1536:2048], g[2688:2696]], axis=0)


def _qk_to_pad(w):
    w4 = w.reshape(HEADS, QK_DIM, w.shape[-1])
    return jnp.concatenate([w4, jnp.zeros((HEADS, QK_PAD - QK_DIM, w.shape[-1]), w.dtype)], axis=1).reshape(
        HEADS * QK_PAD, w.shape[-1])


def _qk_from_pad(g):
    return g.reshape(HEADS, QK_PAD, g.shape[-1])[:, :QK_DIM].reshape(HEADS * QK_DIM, g.shape[-1])


def _ff_to_pad(a, axis):
    shape = list(a.shape)
    shape[axis:axis + 1] = [N_CHIPS, FF_SHARD]
    a4 = a.reshape(shape)
    shape[axis + 1] = FF_BLOCK - FF_SHARD
    out = jnp.concatenate([a4, jnp.zeros(shape, a.dtype)], axis=axis + 1)
    shape[axis:axis + 2] = [D_FF_P]
    return out.reshape(shape)


def _ff_from_pad(a, axis):
    shape = list(a.shape)
    shape[axis:axis + 1] = [N_CHIPS, FF_BLOCK]
    a4 = lax.slice_in_dim(a.reshape(shape), 0, FF_SHARD, axis=axis + 1)
    shape[axis:axis + 2] = [D_FF]
    return a4.reshape(shape)


class _LocalPlan:
    def __init__(self, wt):
        self.wt, self.grads = wt, {}

    def weight(self, name):
        return self.wt[name]

    def host(self, point):
        return None

    def grad(self, name, value):
        self.grads[name] = value


def _local_step(x, tgt, wt, plan=None):
    plan = _LocalPlan(wt) if plan is None else plan
    s = x.shape[0]
    n_valid = N_META + s
    t = -(-n_valid // HEAD) * HEAD
    zpad = jnp.zeros((t - n_valid, D_MODEL), F32)
    h0 = jnp.concatenate([wt["meta_tokens"], x, zpad], axis=0)
    tgt_p = jnp.concatenate([jnp.zeros((N_META, D_MODEL), F32), tgt, zpad], axis=0)
    cos, sin, perm = _rope_tables(t)
    win, wq, wkv = wt["w_in_t"], wt["w_q_t"], wt["w_kv_t"]
    qn_w = jnp.concatenate([wt["q_norm_w"], jnp.zeros((1, QK_PAD - QK_DIM), F32)], axis=1)
    kn_w = jnp.concatenate([wt["k_norm_w"], jnp.zeros((1, QK_PAD - QK_DIM), F32)], axis=1)
    head_id = jnp.arange(DN_WIDTH)[None, :] // HEAD
    lane = jnp.arange(HEAD)[:, None]
    sel_a = (lane == head_id).astype(F32)
    sel_b = (lane == head_id + HEADS).astype(F32)
    alog = jnp.repeat(wt["dn_A_log"], HEAD, axis=1)
    dtb = jnp.repeat(wt["dn_dt_bias"], HEAD, axis=1)
    conv_w, conv_b = wt["ffn_conv_w"], wt["ffn_conv_b"]

    u = _rms_fwd("attn_norm_fwd", h0, wt["attn_norm_w"])
    proj = _matmul("in_proj", u, win, "nt", F32)
    z = (proj, DN_WIDTH, 3)
    q_lat, kv_lat, k_pe, ab = (proj, LORA, 8), (proj, LORA, 9), (proj, HEAD, 20), (proj, HEAD, 21)
    mla_consts = (wt["q_a_norm_w"], wq, wt["kv_a_norm_w"], wkv, qn_w, kn_w, perm)
    q, k, v = _mla_prep_fwd(q_lat, kv_lat, k_pe, cos, sin, *mla_consts)
    o_mla = _attn_fwd(q, k, v, host=plan.host("attn_fwd"))
    conv = _dn_conv_fwd(proj, wt["dn_conv_w"])
    dn_consts = (sel_a, sel_b, alog, dtb)
    qn, kn, g, beta = _dn_prep_fwd(conv, ab, *dn_consts)
    cu, cw, cqg, ckd, ceg, cqk = _dn_chunk_fwd(qn, kn, conv, g, beta, host=plan.host("dn_chunk_fwd"))
    o_dn, sall = _dn_rec_fwd(cu, cw, cqg, ckd, ceg, cqk, host=plan.host("dn_rec_fwd"))
    mixed = _mix_out_fwd(o_mla, o_dn, z, wt["mla_out_norm_w"], wt["dn_out_norm_w"])
    w_out = plan.weight("w_out")
    h1 = _matmul("out_proj", mixed, w_out, "nn", F32, res=h0)
    n2 = _rms_fwd("ffn_norm_fwd", h1, wt["ffn_norm_w"])
    w_gate, w_up, w_down = plan.weight("w_gate_t"), plan.weight("w_up_t"), plan.weight("w_down")
    gpre = _matmul("gate_proj", n2, w_gate, "nt", F32)
    up = _matmul("up_proj", n2, w_up, "nt", F32)
    act = _glu_fwd(gpre, up, conv_w, conv_b)
    h2 = _matmul("down_proj", act, w_down, "nn", F32, res=h1)
    dy, dy16, sq = _loss_call(h2, tgt_p, n_valid)

    grads = {}
    dact = _matmul("down_dx", dy16, w_down, "nt", F32)
    plan.grad("w_down", _matmul("down_dw", act, dy16, "tn", F32))
    dgpre, dup, grads["ffn_conv_w"], grads["ffn_conv_b"] = _glu_bwd(gpre, up, conv_w, conv_b, dact)
    plan.grad("w_gate_t", _matmul("gate_dw", dgpre, n2, "tn", F32))
    plan.grad("w_up_t", _matmul("up_dw", dup, n2, "tn", F32))
    dn2a = _matmul("gate_dx", dgpre, w_gate, "nn", F32)
    dn2b = _matmul("up_dx", dup, w_up, "nn", F32)
    dh1, dh1_16, grads["ffn_norm_w"] = _rms_bwd("ffn_norm_bwd", h1, wt["ffn_norm_w"], [dn2a, dn2b], dy)
    dmixed = _matmul("out_dx", dh1_16, w_out, "nt", F32)
    plan.grad("w_out", _matmul("out_dw", mixed, dh1_16, "tn", F32))
    do_mla, do_dn, dz, grads["mla_out_norm_w"], grads["dn_out_norm_w"] = _mix_out_bwd(
        o_mla, o_dn, z, dmixed, wt["mla_out_norm_w"], wt["dn_out_norm_w"])
    rec_cts = _dn_rec_bwd(cu, cw, cqg, ckd, ceg, cqk, sall, do_dn, host=plan.host("dn_rec_bwd"))
    dqn, dkn, dv_dn, dg, dbeta = _dn_chunk_bwd(qn, kn, conv, g, beta, rec_cts, host=plan.host("dn_chunk_bwd"))
    dconv, dab, dalog, ddtb = _dn_prep_bwd(conv, ab, dqn, dkn, dv_dn, dg, dbeta, *dn_consts)
    grads["dn_A_log"] = jnp.sum(dalog.reshape(HEADS, HEAD), axis=1)[None, :]
    grads["dn_dt_bias"] = jnp.sum(ddtb.reshape(HEADS, HEAD), axis=1)[None, :]
    ddn_pre, grads["dn_conv_w"] = _dn_conv_bwd(proj, wt["dn_conv_w"], dconv)
    dq, dk, dv = _attn_bwd(q, k, v, do_mla, host=plan.host("attn_bwd"))
    dq_lat, dkv_lat, dk_pe, dqa, dwq, dkva, dwkv, dqnw, dknw = _mla_prep_bwd(
        q_lat, kv_lat, k_pe, cos, sin, dq, dk, dv, *mla_consts, host=plan.host("mla_prep_bwd"))
    grads["q_a_norm_w"], grads["kv_a_norm_w"] = dqa, dkva
    plan.grad("w_q_t", dwq)
    plan.grad("w_kv_t", dwkv)
    grads["q_norm_w"], grads["k_norm_w"] = dqnw[:, :QK_DIM], dknw[:, :QK_DIM]
    dproj = jnp.concatenate([ddn_pre, dz, dq_lat, dkv_lat, dk_pe, dab], axis=1)
    plan.grad("w_in_t", _matmul("in_dw", dproj, u, "tn", F32))
    du = _matmul("in_dx", dproj, win, "nn", F32, host=plan.host("in_dx"))
    dh0, _, grads["attn_norm_w"] = _rms_bwd("attn_norm_bwd", h0, wt["attn_norm_w"], [du], dh1,
                                            host=plan.host("attn_norm_bwd"))
    grads["meta_tokens"] = dh0[0:N_META]
    if isinstance(plan, _LocalPlan):
        grads.update(plan.grads)
    return sq, dh0[N_META:n_valid], grads


def _mesh_pos():
    return lax.axis_index("x"), lax.axis_index("y"), lax.axis_index("c")


def _other_chips(x, y):
    return [(1 - x, y), (x, 1 - y), (1 - x, 1 - y)]


def _remote(src, dst, send_sems, recv_sems, k, to):
    return pltpu.make_async_remote_copy(src_ref=src, dst_ref=dst, send_sem=send_sems.at[k], recv_sem=recv_sems.at[k],
                                        device_id=to, device_id_type=MESH)


def _copies_exchange(make, ins, out_shape, nsem):
    def prog(in_refs, out_refs, send_sems, recv_sems):
        copies = make(in_refs, out_refs, send_sems, recv_sems)

        def start():
            for cp in copies:
                cp.start()

        def finish():
            for cp in copies:
                cp.wait()

        return start, finish

    return _Exchange(prog, ins, out_shape, nsem)


def _all_gather(shards):
    def prog(srcs, dsts, send_sems, recv_sems):
        x, y, c = _mesh_pos()
        p = 2 * x + y
        sibling = (x, y, 1 - c)
        chips = _other_chips(x, y)
        bufs = tuple((s, d, s.shape[0] // 2) for s, d in zip(srcs, dsts))

        def half(ref, rows, which):
            return ref.at[pl.ds(which * rows, rows), :]

        def copy(i, k, src, dst, to):
            return _remote(src, dst, send_sems, recv_sems, 6 * i + k, to)

        sends = [copy(i, j, half(src, rows, c), half(dst.at[p], rows, c), (*chip, c))
                 for i, (src, dst, rows) in enumerate(bufs) for j, chip in enumerate(chips)]

        def start():
            for cp in sends:
                cp.start()

        def finish():
            passed = []
            for i, (src, dst, rows) in enumerate(bufs):
                for j, (qx, qy) in enumerate(chips):
                    block = half(dst.at[2 * qx + qy], rows, c)
                    copy(i, j, block, block, (x, y, c)).wait_recv()
                    fwd = copy(i, 3 + j, block, block, sibling)
                    fwd.start()
                    passed.append(fwd)
            for i, (src, dst, rows) in enumerate(bufs):
                for j, (qx, qy) in enumerate(chips):
                    block = half(dst.at[2 * qx + qy], rows, 1 - c)
                    copy(i, 3 + j, block, block, (x, y, c)).wait_recv()
            for cp in sends + passed:
                cp.wait_send()

        return start, finish

    return _Exchange(prog, shards, [jax.ShapeDtypeStruct((N_CHIPS, *s.shape), s.dtype) for s in shards], 6 * len(shards))


def _gathered(ex):
    p = 2 * lax.axis_index("x") + lax.axis_index("y")
    return [lax.dynamic_update_slice(g, s[None], (p, 0, 0)) for g, s in zip(ex.outs, ex.ins)]


def _rs_to_sibling(bufs):
    def make(srcs, dsts, send_sems, recv_sems):
        x, y, c = _mesh_pos()
        copies = []
        for i, (src, dst) in enumerate(zip(srcs, dsts)):
            half = src.shape[1] // 2
            copies.append(_remote(src.at[:, pl.ds((1 - c) * half, half), :], dst, send_sems, recv_sems, i, (x, y, 1 - c)))
        return copies

    return _copies_exchange(make, bufs, [jax.ShapeDtypeStruct((N_CHIPS, b.shape[1] // 2, b.shape[2]), F32) for b in bufs],
                            len(bufs))


def _rs_pair_add(name, buf, got, c, out_dtype):
    half, cols = got.shape[1], got.shape[2]

    def body(c_ref, a_ref, b_ref, o_ref):
        o_ref[...] = (a_ref[...] + b_ref[...]).astype(out_dtype)

    return pl.pallas_call(
        body, name=name,
        grid_spec=pltpu.PrefetchScalarGridSpec(
            num_scalar_prefetch=1, grid=(N_CHIPS,),
            in_specs=[pl.BlockSpec((None, half, cols), lambda j, cr: (j, cr[0], 0)),
                      pl.BlockSpec((None, half, cols), lambda j, cr: (j, 0, 0))],
            out_specs=pl.BlockSpec((None, half, cols), lambda j, cr: (j, 0, 0))),
        out_shape=jax.ShapeDtypeStruct(got.shape, out_dtype),
        compiler_params=_cparams(("arbitrary",)))(c, buf, got)


def _rs_to_chips(accs):
    def make(srcs, dsts, send_sems, recv_sems):
        x, y, c = _mesh_pos()
        return [_remote(src.at[2 * qx + qy], dst.at[k], send_sems, recv_sems, 3 * i + k, (qx, qy, c))
                for i, (src, dst) in enumerate(zip(srcs, dsts)) for k, (qx, qy) in enumerate(_other_chips(x, y))]

    return _copies_exchange(make, accs, [jax.ShapeDtypeStruct((3, a.shape[1], a.shape[2]), a.dtype) for a in accs],
                            3 * len(accs))


def _rs_chip_add(name, acc, got, p):
    half, cols = acc.shape[1], acc.shape[2]
    tr = _tile(half, 128, 8)
    slot = (0, 1, 0, 2)

    def body(p_ref, own_ref, g0_ref, g1_ref, g2_ref, o_ref):
        me = p_ref[0]
        gots = (g0_ref, g1_ref, g2_ref)
        total = None
        for chip in range(N_CHIPS):
            val = own_ref[...].astype(F32)
            for e in (1, 2, 3):
                val = jnp.where((chip ^ me) == e, gots[slot[e]][...].astype(F32), val)
            total = val if total is None else total + val
        o_ref[...] = total

    gspec = lambda k: pl.BlockSpec((None, tr, cols), lambda i, pr: (k, i, 0))
    return pl.pallas_call(
        body, name=name,
        grid_spec=pltpu.PrefetchScalarGridSpec(
            num_scalar_prefetch=1, grid=(half // tr,),
            in_specs=[pl.BlockSpec((None, tr, cols), lambda i, pr: (pr[0], i, 0)), gspec(0), gspec(1), gspec(2)],
            out_specs=pl.BlockSpec((tr, cols), lambda i, pr: (i, 0))),
        out_shape=jax.ShapeDtypeStruct((half, cols), F32),
        compiler_params=_cparams(("arbitrary",)))(p, acc, got, got, got)


def _rs_share(ress):
    def make(srcs, dsts, send_sems, recv_sems):
        x, y, c = _mesh_pos()
        return [_remote(src, dst, send_sems, recv_sems, i, (x, y, 1 - c)) for i, (src, dst) in enumerate(zip(srcs, dsts))]

    return _copies_exchange(make, ress, [jax.ShapeDtypeStruct(r.shape, F32) for r in ress], len(ress))


def _shared(ex):
    south = lax.axis_index("c") == 0
    return [jnp.concatenate([jnp.where(south, r, g), jnp.where(south, g, r)], axis=0) for r, g in zip(ex.ins, ex.outs)]


def _all_to_all_devices(vec):
    def make(srcs, dsts, send_sems, recv_sems):
        x, y, c = _mesh_pos()
        me = 4 * x + 2 * y + c
        copies = []
        for r in range(1, 8):
            px, py, pc = (1 - x if r & 4 else x), (1 - y if r & 2 else y), (1 - c if r & 1 else c)
            copies.append(_remote(srcs[0], dsts[0].at[me], send_sems, recv_sems, r - 1, (px, py, pc)))
        return copies

    return _copies_exchange(make, [vec], [jax.ShapeDtypeStruct((8, *vec.shape), vec.dtype)], 7)


def _sum_devices(stack):
    def body(s_ref, o_ref):
        total = s_ref[0]
        for d in range(1, 8):
            total = total + s_ref[d]
        o_ref[...] = total

    return pl.pallas_call(body, name="sum_devices", out_shape=jax.ShapeDtypeStruct(stack.shape[1:], F32),
                          compiler_params=pltpu.CompilerParams(vmem_limit_bytes=VMEM_LIMIT))(stack)


def _pad_rows(flat, rows):
    return jnp.concatenate([flat, jnp.zeros((rows * LANES - flat.shape[0],), flat.dtype)]).reshape(rows, LANES)


def _unshard(g4, shape, axis):
    a = g4.reshape(N_CHIPS, *shape)
    if axis == 0:
        return a.reshape(N_CHIPS * shape[0], shape[1])
    return jnp.transpose(a, (1, 0, 2)).reshape(shape[0], N_CHIPS * shape[1])


def _shard4(full, shape, axis):
    if axis == 0:
        return full.reshape(N_CHIPS, shape[0] * shape[1])
    a = full.reshape(shape[0], N_CHIPS, shape[1])
    return jnp.transpose(a, (1, 0, 2)).reshape(N_CHIPS, shape[0] * shape[1])


def _pad_axis0(a, rows):
    return jnp.concatenate([a, jnp.zeros((rows - a.shape[0], *a.shape[1:]), a.dtype)], axis=0)


def _pad_axis1(a, rows):
    return jnp.concatenate([a, jnp.zeros((a.shape[0], rows - a.shape[1], *a.shape[2:]), a.dtype)], axis=1)


def _shard_to_strip(name, w):
    _, (shape, axis, rows) = name, {n: (s, ax, r) for n, s, ax, r in BIG}[name]
    w2 = w.reshape(shape).astype(BF16)
    return _pad_axis0(w2.T if axis == 1 else w2, rows)


LOCAL_NAME = dict(w_in="w_in_t", w_q_b="w_q_t", w_kv_b="w_kv_t", w_out="w_out", w_gate="w_gate_t", w_up="w_up_t",
                  w_down="w_down")


WIN_SEGMENTS = ((576, 2112, 0), (2112, 2624, 1536), (0, 256, 2048), (256, 512, 2304), (512, 576, 2560), (2624, 2632, 2688))


def _strips_to_weight(name, g4):
    if name == "w_in":
        return _win_to_pad(g4[:, :IN_SHARD].reshape(IN_COLS, D_MODEL))
    if name == "w_q_b":
        return _qk_to_pad(g4.reshape(HEADS * QK_DIM, LORA))
    return g4.reshape(N_CHIPS * g4.shape[1], g4.shape[2])


def _grad_to_strips(name, g):
    if name == "w_in":
        strips = []
        for q in range(N_CHIPS):
            pieces = []
            for a, b, local in sorted(WIN_SEGMENTS):
                s, e = max(a, q * IN_SHARD), min(b, (q + 1) * IN_SHARD)
                if s < e:
                    pieces.append(g[local + s - a:local + e - a])
            pieces.append(jnp.zeros((IN_SHARD_P - IN_SHARD, D_MODEL), g.dtype))
            strips.append(jnp.concatenate(pieces, axis=0))
        return jnp.stack(strips)
    if name == "w_q_b":
        return _qk_from_pad(g).reshape(N_CHIPS, QK_DIM, LORA)
    return g.reshape(N_CHIPS, g.shape[0] // N_CHIPS, g.shape[1])


class _MeshPlan:
    LATE = dict(attn_fwd=("w_up",), dn_chunk_fwd=("w_out", "w_gate"), dn_rec_fwd=("w_down",))
    GROUP_A = ("w_down", "w_gate", "w_up", "w_out")
    GROUP_B = ("w_in", "w_q_b", "w_kv_b")

    def __init__(self, w):
        x, y, c = _mesh_pos()
        self.ci = jnp.reshape(c, (1,)).astype(jnp.int32)
        self.pi = jnp.reshape(2 * x + y, (1,)).astype(jnp.int32)
        self.strip = {n: _shard_to_strip(n, w[n]) for n, _, _, _ in BIG}
        self.gathers, self.weights, self.g, self.acc, self.reduced = {}, {}, {}, {}, {}
        self.sib = self.chip = self.share = None

    def gather_first(self, small):
        names = ("w_in", "w_q_b", "w_kv_b")
        ex = _all_gather([self.strip[n] for n in names] + [small])
        ex.run("all_gather_first")
        got = _gathered(ex)
        for n, g4 in zip(names, got):
            self.weights[LOCAL_NAME[n]] = _strips_to_weight(n, g4)
        return got[-1]

    def weight(self, local_name):
        if local_name not in self.weights:
            for names, ex in self.gathers.values():
                for n, g4 in zip(names, _gathered(ex)):
                    self.weights[LOCAL_NAME[n]] = _strips_to_weight(n, g4)
            self.gathers = {}
        return self.weights[local_name]

    def grad(self, local_name, value):
        name = {v: k for k, v in LOCAL_NAME.items()}[local_name]
        self.g[name] = _grad_to_strips(name, value)

    def _pair_add(self, names, sib):
        for n, got in zip(names, sib.outs):
            self.acc[n] = _rs_pair_add("rs_pair_add_" + n, self.g[n], got, self.ci, BF16)

    def _chip_add(self, names, chip):
        return [_rs_chip_add("rs_chip_add_" + n, self.acc[n], got, self.pi) for n, got in zip(names, chip.outs)]

    def _take_shared(self, names, share):
        for n, strip in zip(names, _shared(share)):
            self.reduced[n] = strip

    def host(self, point):
        a, b = self.GROUP_A, self.GROUP_B
        if point in self.LATE:
            names = self.LATE[point]
            ex = _all_gather([self.strip[n] for n in names])
            self.gathers[point] = (names, ex)
            return ex
        if point == "dn_rec_bwd":
            self.sib = _rs_to_sibling([self.g[n] for n in a])
            return self.sib
        if point == "dn_chunk_bwd":
            self._pair_add(a, self.sib)
            self.chip1 = _rs_to_chips([self.acc[n] for n in a[:2]])
            return self.chip1
        if point == "attn_bwd":
            self.chip2 = _rs_to_chips([self.acc[n] for n in a[2:]])
            return self.chip2
        if point == "mla_prep_bwd":
            ress = self._chip_add(a[:2], self.chip1) + self._chip_add(a[2:], self.chip2)
            self.share = _rs_share(ress)
            return self.share
        if point == "in_dx":
            self._take_shared(a, self.share)
            self.sib = _rs_to_sibling([self.g[n] for n in b])
            return self.sib
        return None

    def last_chip_exchange(self):
        self._pair_add(self.GROUP_B, self.sib)
        self.chip = _rs_to_chips([self.acc[n] for n in self.GROUP_B])
        return self.chip

    def finish(self):
        share = _rs_share(self._chip_add(self.GROUP_B, self.chip))
        share.run("rs_share_last")
        self._take_shared(self.GROUP_B, share)
        return self.reduced


def _strip_to_shard(name, strip):
    shape, axis = {n: (s, ax) for n, s, ax, _ in BIG}[name]
    rows = shape[axis]
    return strip[:rows].T if axis == 1 else strip[:rows]


def kernel(x, meta_tokens, attn_norm_w, w_in, q_a_norm_w, w_q_b, kv_a_norm_w, w_kv_b, q_norm_w, k_norm_w, mla_out_norm_w, dn_conv_w, dn_A_log, dn_dt_bias, dn_out_norm_w, w_out, ffn_norm_w, w_gate, w_up, ffn_conv_w, ffn_conv_b, w_down, loss_target, m_meta_tokens, m_attn_norm_w, m_w_in, m_q_a_norm_w, m_w_q_b, m_kv_a_norm_w, m_w_kv_b, m_q_norm_w, m_k_norm_w, m_mla_out_norm_w, m_dn_conv_w, m_dn_A_log, m_dn_dt_bias, m_dn_out_norm_w, m_w_out, m_ffn_norm_w, m_w_gate, m_w_up, m_ffn_conv_w, m_ffn_conv_b, m_w_down, v_meta_tokens, v_attn_norm_w, v_w_in, v_q_a_norm_w, v_w_q_b, v_kv_a_norm_w, v_w_kv_b, v_q_norm_w, v_k_norm_w, v_mla_out_norm_w, v_dn_conv_w, v_dn_A_log, v_dn_dt_bias, v_dn_out_norm_w, v_w_out, v_ffn_norm_w, v_w_gate, v_w_up, v_ffn_conv_w, v_ffn_conv_b, v_w_down):
    local = dict(locals())
    w = {n: local[n] for n in WEIGHTS}
    m = {n: local["m_" + n] for n in WEIGHTS}
    v = {n: local["v_" + n] for n in WEIGHTS}
    p = 2 * lax.axis_index("x") + lax.axis_index("y")

    plan = _MeshPlan(w)
    wf = _pad_rows(jnp.concatenate([w[n].reshape(-1) for n, _, _ in SMALL_SHARDED]), SMALL_ROWS)
    gf = plan.gather_first(wf).reshape(N_CHIPS, -1)
    full = dict(plan.weights)
    off = 0
    for n, s, ax in SMALL_SHARDED:
        full[n] = _unshard(gf[:, off:off + s[0] * s[1]], s, ax)
        off += s[0] * s[1]
    for n, _ in REPLICATED:
        full[n] = w[n]
    full["ffn_conv_w"] = _ff_to_pad(full["ffn_conv_w"], 1)
    full["ffn_conv_b"] = _ff_to_pad(full["ffn_conv_b"], 1)

    sq, grad_x, g = _local_step(x[0], loss_target[0], full, plan)
    g["ffn_conv_w"] = _ff_from_pad(g["ffn_conv_w"], 1)
    g["ffn_conv_b"] = _ff_from_pad(g["ffn_conv_b"], 1)

    small_all = [n for n, _, _ in SMALL_SHARDED] + [n for n, _ in REPLICATED]
    vec = jnp.concatenate([g[n].reshape(-1) for n in small_all] + [jnp.reshape(0.5 / D_MODEL * jnp.sum(sq), (1,))])
    vec = _pad_rows(vec, -(-vec.shape[0] // (8 * LANES)) * 8)
    a2a = _all_to_all_devices(vec)
    chip = plan.last_chip_exchange()
    outs = _merge_exchanges(chip, a2a).run("last_exchange")
    chip.outs, a2a.outs = outs[:len(chip.out_shape)], outs[len(chip.out_shape):]
    me = 4 * lax.axis_index("x") + 2 * lax.axis_index("y") + lax.axis_index("c")
    red = _sum_devices(lax.dynamic_update_slice(a2a.outs[0], vec[None], (me, 0, 0))).reshape(-1)
    strips = plan.finish()
    gs = {}
    off = 0
    for n in small_all:
        tot = red[off:off + g[n].size].reshape(g[n].shape)
        off += g[n].size
        shard = {sn: (s, ax) for sn, s, ax in SMALL_SHARDED}.get(n)
        if shard is not None:
            tot = lax.dynamic_slice_in_dim(tot, p * shard[0][1], shard[0][1], axis=1)
        gs[n] = tot
    loss = red[off]

    delta, new_m, new_v = {}, {}, {}
    for n, s, ax, _ in BIG:
        flip = ax == 1 and s[1] % 8 == 0
        there = (lambda a, s=s: a.reshape(s).T) if flip else (lambda a, s=s: a.reshape(s))
        back = (lambda a, n=n: a.T.reshape(w[n].shape)) if flip else (lambda a, n=n: a.reshape(w[n].shape))
        strip = strips[n] if flip or ax == 0 else strips[n][:s[1]].T
        g2, d2, m2, v2 = _adamw_call("adamw_" + n, there(w[n]), strip, there(m[n]), there(v[n]))
        gs[n], delta[n], new_m[n], new_v[n] = back(g2), back(d2), back(m2), back(v2)
    small_names = [n for n, _, _ in SMALL_SHARDED] + [n for n, _ in REPLICATED]
    rows = SMALL_ROWS + REP_ROWS
    pack = lambda d: _pad_rows(jnp.concatenate([d[n].reshape(-1) for n in small_names]), rows)
    _, d2, m2, v2 = _adamw_call("adamw_small", pack(w), pack(gs), pack(m), pack(v))
    off = 0
    for n in small_names:
        cnt = w[n].size
        for dst, src in ((delta, d2), (new_m, m2), (new_v, v2)):
            dst[n] = src.reshape(-1)[off:off + cnt].reshape(w[n].shape)
        off += cnt

    grad_out = [gs[n].reshape(w[n].shape) for n in WEIGHTS]
    return (loss, grad_x[None], *grad_out, *[delta[n] for n in WEIGHTS], *[new_m[n] for n in WEIGHTS],
            *[new_v[n] for n in WEIGHTS])
```

```python
import functools
import math

import jax
import jax.numpy as jnp
import numpy as np
from jax import lax
from jax.experimental import pallas as pl
from jax.experimental.pallas import tpu as pltpu

F32 = jnp.float32
BF16 = jnp.bfloat16
HI = lax.Precision.HIGHEST
MESH = pl.DeviceIdType.MESH

N_META = 16
D_MODEL = 1024
HEADS = 4
HEAD = 128
ROPE = 64
QK_DIM = HEAD + ROPE
QK_PAD = 2 * HEAD
LORA = 256
DN_WIDTH = HEADS * HEAD
CHUNK = 64
D_FF = 2816
N_CHIPS = 4
FF_SHARD = D_FF // N_CHIPS
FF_BLOCK = 768
D_FF_P = N_CHIPS * FF_BLOCK
IN_COLS = 2632
IN_SHARD = IN_COLS // N_CHIPS
IN_SHARD_P = 672
IN_PAD = 2816
NORM_EPS = 1e-6
ROPE_THETA = 10000.0
LANES = 512

ADAM_LR, ADAM_B1, ADAM_B2, ADAM_EPS, ADAM_WD, ADAM_STEP = 0.001, 0.9, 0.999, 1e-08, 0.01, 10

VMEM_LIMIT = 56 * 1024 * 1024

BIG = (("w_in", (1024, 658), 1, IN_SHARD_P), ("w_q_b", (256, 192), 1, 192), ("w_kv_b", (256, 256), 1, 256),
       ("w_out", (256, 1024), 0, 256), ("w_gate", (1024, 704), 1, FF_BLOCK), ("w_up", (1024, 704), 1, FF_BLOCK),
       ("w_down", (704, 1024), 0, FF_BLOCK))
SMALL_SHARDED = (("meta_tokens", (16, 256), 1), ("dn_conv_w", (4, 384), 1), ("ffn_conv_w", (3, 704), 1))
REPLICATED = (("attn_norm_w", 1024), ("q_a_norm_w", 256), ("kv_a_norm_w", 256), ("q_norm_w", 192), ("k_norm_w", 192),
              ("mla_out_norm_w", 128), ("dn_A_log", 4), ("dn_dt_bias", 4), ("dn_out_norm_w", 128), ("ffn_norm_w", 1024),
              ("ffn_conv_b", 2816))
WEIGHTS = ("meta_tokens", "attn_norm_w", "w_in", "q_a_norm_w", "w_q_b", "kv_a_norm_w", "w_kv_b", "q_norm_w", "k_norm_w",
           "mla_out_norm_w", "dn_conv_w", "dn_A_log", "dn_dt_bias", "dn_out_norm_w", "w_out", "ffn_norm_w", "w_gate",
           "w_up", "ffn_conv_w", "ffn_conv_b", "w_down")

SMALL_ROWS = 16
REP_ROWS = 16


def _cparams(sem):
    return pltpu.CompilerParams(dimension_semantics=sem, vmem_limit_bytes=VMEM_LIMIT)


class _Exchange:
    def __init__(self, prog, ins, out_shape, nsem):
        self.prog, self.ins, self.out_shape, self.nsem = prog, list(ins), list(out_shape), nsem
        self.outs = None

    def sems(self):
        return [pltpu.SemaphoreType.DMA((self.nsem,)), pltpu.SemaphoreType.DMA((self.nsem,))]

    def run(self, name):
        any_spec = pl.BlockSpec(memory_space=pl.ANY)
        n = len(self.ins)

        def body(*refs):
            start, finish = self.prog(refs[:n], refs[n:-2], refs[-2], refs[-1])
            start()
            finish()

        self.outs = pl.pallas_call(
            body, name=name, in_specs=[any_spec] * n, out_specs=[any_spec] * len(self.out_shape),
            out_shape=self.out_shape, scratch_shapes=self.sems(),
            compiler_params=pltpu.CompilerParams(has_side_effects=True))(*self.ins)
        return self.outs


def _pcall(body, name, grid, in_specs, out_specs, out_shape, args, sem, scratch_shapes=(), host=None):
    single = not isinstance(out_shape, (list, tuple))
    out_specs, out_shape = ([out_specs], [out_shape]) if single else (list(out_specs), list(out_shape))
    if host is None:
        outs = pl.pallas_call(body, name=name, grid=grid, in_specs=list(in_specs), out_specs=out_specs, out_shape=out_shape,
                              scratch_shapes=list(scratch_shapes), compiler_params=_cparams(sem))(*args)
        return outs[0] if single else outs
    any_spec = pl.BlockSpec(memory_space=pl.ANY)
    n_in, n_out, n_scr, nx_in, nx_out = len(in_specs), len(out_specs), len(scratch_shapes), len(host.ins), len(host.out_shape)

    def hosted(*refs):
        c_in, x_in = refs[:n_in], refs[n_in:n_in + nx_in]
        o0 = n_in + nx_in
        c_out, x_out = refs[o0:o0 + n_out], refs[o0 + n_out:o0 + n_out + nx_out]
        s0 = o0 + n_out + nx_out
        start, finish = host.prog(x_in, x_out, refs[s0 + n_scr], refs[s0 + n_scr + 1])
        first = functools.reduce(jnp.logical_and, [pl.program_id(d) == 0 for d in range(len(grid))])
        last = functools.reduce(jnp.logical_and, [pl.program_id(d) == grid[d] - 1 for d in range(len(grid))])
        pl.when(first)(start)
        body(*c_in, *c_out, *refs[s0:s0 + n_scr])
        pl.when(last)(finish)

    outs = pl.pallas_call(
        hosted, name=name, grid=grid, in_specs=list(in_specs) + [any_spec] * nx_in,
        out_specs=out_specs + [any_spec] * nx_out, out_shape=out_shape + host.out_shape,
        scratch_shapes=list(scratch_shapes) + host.sems(),
        compiler_params=pltpu.CompilerParams(dimension_semantics=sem, vmem_limit_bytes=VMEM_LIMIT, has_side_effects=True))(
            *args, *host.ins)
    host.outs = outs[n_out:]
    return outs[0] if single else outs[:n_out]


NN, NT, TN = ((1,), (0,)), ((1,), (1,)), ((0,), (0,))


def _shift_dims(dims, batch):
    if not batch:
        return (dims, ((), ()))
    return (((dims[0][0] + 1,), (dims[1][0] + 1,)), ((0,), (0,)))


def _make_mm(dims, exact, batch=False):
    def raw(a, b, d):
        dn = _shift_dims(d, batch)
        if exact == "split_lhs":
            ah, bh = a.astype(BF16), b.astype(BF16)
            al = (a - ah.astype(F32)).astype(BF16)
            return lax.dot_general(ah, bh, dn, preferred_element_type=F32) + lax.dot_general(al, bh, dn,
                                                                                              preferred_element_type=F32)
        if exact == "split":
            ah, bh = a.astype(BF16), b.astype(BF16)
            al, bl = (a - ah.astype(F32)).astype(BF16), (b - bh.astype(F32)).astype(BF16)
            dot = lambda p, q: lax.dot_general(p, q, dn, preferred_element_type=F32)
            return dot(ah, bh) + (dot(ah, bl) + dot(al, bh))
        if exact:
            return lax.dot_general(a.astype(F32), b.astype(F32), dn, precision=HI, preferred_element_type=F32)
        return lax.dot_general(a.astype(BF16), b.astype(BF16), dn, preferred_element_type=F32)

    @jax.custom_vjp
    def mm(a, b):
        return raw(a, b, dims)

    def fwd(a, b):
        return raw(a, b, dims), (a, b)

    def bwd(res, g):
        a, b = res
        if dims == NN:
            da, db = raw(g, b, NT), raw(a, g, TN)
        elif dims == NT:
            da, db = raw(g, b, NN), raw(g, a, TN)
        else:
            da, db = raw(b, g, NT), raw(a, g, NN)
        return da.astype(a.dtype), db.astype(b.dtype)

    mm.defvjp(fwd, bwd)
    return mm


_mm = _make_mm(NN, False)
_mm_nt = _make_mm(NT, False)
_mm_tn = _make_mm(TN, False)
_mmx = _make_mm(NN, "split_lhs")
_bmm = _make_mm(NN, False, batch=True)
_bmm_nt = _make_mm(NT, False, batch=True)
_bmmx = _make_mm(NN, True, batch=True)
_bmms = _make_mm(NN, "split", batch=True)
_bmms_nt = _make_mm(NT, "split", batch=True)
_bmms_tn = _make_mm(TN, "split", batch=True)


@jax.custom_vjp
def _unit_lower_inv(a):
    n = a.shape[-1]
    eye = (lax.broadcasted_iota(jnp.int32, a.shape, 1) == lax.broadcasted_iota(jnp.int32, a.shape, 2)).astype(F32)
    x = -a
    t = eye + x
    for _ in range(max(n.bit_length() - 2, 0)):
        x = _bmms(x, x)
        t = t + _bmms(t, x)
    return t


def _unit_lower_inv_fwd(a):
    t = _unit_lower_inv(a)
    return t, t


def _unit_lower_inv_bwd(t, g):
    return (-_bmms_tn(t, _bmms_nt(g, t)),)


_unit_lower_inv.defvjp(_unit_lower_inv_fwd, _unit_lower_inv_bwd)


def _rms(x, w, n):
    ms = jnp.sum(x * x, axis=-1, keepdims=True) * (1.0 / n)
    return x * lax.rsqrt(ms + NORM_EPS) * w


def _silu(x):
    return x * jax.nn.sigmoid(x)


def _softplus(x):
    return jnp.maximum(x, 0.0) + jnp.log(1.0 + jnp.exp(-jnp.abs(x)))


def _rope(x, cos, sin, perm):
    return x * cos + _mmx(x, perm) * sin


def _mla_prep_fn(rows, consts):
    q_lat, kv_lat, k_pe, cos, sin = rows
    qn = _rms(q_lat, consts["qa_w"], LORA)
    kvn = _rms(kv_lat, consts["kva_w"], LORA)
    outs = []
    for h in range(HEADS):
        q_n = _mm_nt(qn, consts["wq_n"][h])
        q_r = _mm_nt(qn, consts["wq_r"][h])
        rs = lax.rsqrt((jnp.sum(q_n * q_n, -1, keepdims=True) + jnp.sum(q_r * q_r, -1, keepdims=True)) * (1.0 / QK_DIM)
                       + NORM_EPS)
        q_n = q_n * rs * consts["qn_n"]
        q_r = _rope(q_r * rs * consts["qn_r"], cos, sin, consts["perm"])
        k_n = _mm_nt(kvn, consts["wk_n"][h])
        v = _mm_nt(kvn, consts["wv"][h])
        rk = lax.rsqrt((jnp.sum(k_n * k_n, -1, keepdims=True) + jnp.sum(k_pe * k_pe, -1, keepdims=True)) * (1.0 / QK_DIM)
                       + NORM_EPS)
        k_n = k_n * rk * consts["kn_n"]
        k_r = _rope(k_pe * rk * consts["kn_r"], cos, sin, consts["perm"])
        outs += [q_n, q_r, k_n, k_r, v]
    return tuple(outs)


def _attn_fn(q, k, v, row0):
    s = _mm_nt(q, k) * (1.0 / math.sqrt(QK_DIM))
    qpos = row0 + lax.broadcasted_iota(jnp.int32, s.shape, 0)
    kpos = lax.broadcasted_iota(jnp.int32, s.shape, 1)
    s = jnp.where(kpos <= qpos, s, -1e30)
    m = lax.stop_gradient(jnp.max(s, axis=-1, keepdims=True))
    p = jnp.exp(s - m)
    p = p / jnp.sum(p, axis=-1, keepdims=True)
    return _mm(p, v)


def _dn_prep_fn(rows, consts):
    qc, kc, ab = rows
    a_b = _mmx(ab, consts["sel_a"])
    b_b = _mmx(ab, consts["sel_b"])
    beta = jax.nn.sigmoid(b_b)
    g = -jnp.exp(consts["alog"]) * _softplus(a_b + consts["dtb"])
    qs, ks = [], []
    for h in range(HEADS):
        q, k = qc[h], kc[h]
        qs.append(q * lax.rsqrt(jnp.sum(q * q, -1, keepdims=True) + NORM_EPS))
        ks.append(k * lax.rsqrt(jnp.sum(k * k, -1, keepdims=True) + NORM_EPS))
    return tuple(qs), tuple(ks), g, beta


def _dn_chunk_fn(q, k, v, gb, g64, bb):
    nb = q.shape[0]
    ri = lax.broadcasted_iota(jnp.int32, (nb, CHUNK, CHUNK), 1)
    ci = lax.broadcasted_iota(jnp.int32, (nb, CHUNK, CHUNK), 2)
    tri = ri >= ci
    strict = ri > ci
    tril = tri.astype(F32)
    eye = (ri == ci).astype(F32)
    ones = jnp.ones((nb, CHUNK, CHUNK), F32)
    gc = _bmmx(tril, gb)
    gc64 = _bmmx(tril, g64)
    grow = _bmmx(ones, eye * gc64)
    diff = gc64 - grow
    decay = jnp.where(tri, jnp.exp(jnp.where(tri, diff, 0.0)), 0.0)
    kb = k * bb
    vb = v * bb
    a = jnp.where(strict, _bmm_nt(kb, k) * decay, 0.0)
    tinv = _unit_lower_inv(a)
    u = _bmm(tinv, vb)
    w = _bmm(tinv, kb * jnp.exp(gc))
    qs = q * (1.0 / math.sqrt(HEAD))
    qk = _bmm_nt(qs, k) * decay
    qg = qs * jnp.exp(gc)
    glast = jnp.sum(gb, axis=1, keepdims=True)
    kdec = k * jnp.exp(glast - gc)
    eg = jnp.broadcast_to(jnp.exp(glast), gb.shape)
    return u, w, qg, kdec, eg, qk


def _dn_rec_fn(s, u, w, qg, qk, kdec, eg):
    v_new = u - _mm(w, s)
    o = _mm(qg, s) + _mm(qk, v_new)
    s_new = s * eg + _mm_tn(kdec, v_new)
    return s_new, o


def _dn_out_fn(o, z, w):
    return _rms(o, w, HEAD) * _silu(z)


def _row_tile(t):
    return t // 8 if (t // 8) % 16 == 0 else t


def _tile(n, pref, unit):
    best = n
    for cand in range(unit, min(n, pref) + 1, unit):
        if n % cand == 0:
            best = cand
    return best if best <= pref else n


def _rows_call(name, body, rows, consts, outs, accs, r, host=None):
    rows = [a if isinstance(a, tuple) else (a, a.shape[1], 0) for a in rows]
    t = rows[0][0].shape[0]
    zero = lambda nd: (lambda i: (0,) * nd)
    in_specs = [pl.BlockSpec((r, w), functools.partial(lambda i, b: (i, b), b=blk)) for _, w, blk in rows]
    rows = [a for a, _, _ in rows]
    in_specs += [pl.BlockSpec(a.shape, zero(a.ndim)) for a in consts]
    out_shape = [jax.ShapeDtypeStruct((t, w), dt) for w, dt in outs] + [jax.ShapeDtypeStruct(s, F32) for s in accs]
    out_specs = [pl.BlockSpec((r, w), lambda i: (i, 0)) for w, _ in outs] + [pl.BlockSpec(s, zero(len(s))) for s in accs]
    return _pcall(body, name, (t // r,), in_specs, out_specs, out_shape, [*rows, *consts], ("arbitrary",), host=host)


def _accumulate(ref, val):
    @pl.when(pl.program_id(0) == 0)
    def _():
        ref[...] = jnp.zeros(ref.shape, ref.dtype)

    ref[...] += val


def _matmul(name, a, b, dims, out_dtype, res=None, host=None):
    if dims == "nn":
        (m, k), n = a.shape, b.shape[1]
    elif dims == "nt":
        (m, k), n = a.shape, b.shape[0]
    else:
        (k, m), n = a.shape, b.shape[1]
    tm = _tile(m, 1100, 16) if dims != "tn" else _tile(m, 640, 128)
    tn = _tile(n, 1408, 128)
    if dims == "nn":
        a_spec, b_spec, dn = pl.BlockSpec((tm, k), lambda i, j: (i, 0)), pl.BlockSpec((k, tn), lambda i, j: (0, j)), NN
    elif dims == "nt":
        a_spec, b_spec, dn = pl.BlockSpec((tm, k), lambda i, j: (i, 0)), pl.BlockSpec((tn, k), lambda i, j: (j, 0)), NT
    else:
        a_spec, b_spec, dn = pl.BlockSpec((k, tm), lambda i, j: (0, i)), pl.BlockSpec((k, tn), lambda i, j: (0, j)), TN
    o_spec = pl.BlockSpec((tm, tn), lambda i, j: (i, j))

    def body(*refs):
        a_ref, b_ref, o_ref = refs[0], refs[1], refs[-1]
        acc = lax.dot_general(a_ref[...].astype(BF16), b_ref[...].astype(BF16), (dn, ((), ())),
                              preferred_element_type=F32)
        if res is not None:
            acc = acc + refs[2][...]
        o_ref[...] = acc.astype(out_dtype)

    ins = [a, b] + ([res] if res is not None else [])
    specs = [a_spec, b_spec] + ([o_spec] if res is not None else [])
    return _pcall(body, name, (m // tm, n // tn), specs, o_spec, jax.ShapeDtypeStruct((m, n), out_dtype), ins,
                  ("arbitrary", "arbitrary"), host=host)


def _rms_fwd(name, h, w):
    n = h.shape[1]

    def body(h_ref, w_ref, o_ref):
        o_ref[...] = _rms(h_ref[...], w_ref[...], n).astype(BF16)

    return _rows_call(name, body, [h], [w], [(n, BF16)], [], _row_tile(h.shape[0]))[0]


def _rms_bwd(name, h, w, cts, resid, host=None):
    n = h.shape[1]
    nct = len(cts)

    def body(*refs):
        h_ref, ct_refs, r_ref, w_ref = refs[0], refs[1:1 + nct], refs[1 + nct], refs[2 + nct]
        dh_ref, dh16_ref, dw_ref = refs[-3], refs[-2], refs[-1]
        ct = ct_refs[0][...].astype(F32)
        for c in ct_refs[1:]:
            ct = ct + c[...].astype(F32)
        _, vjp = jax.vjp(lambda x, ww: _rms(x, ww, n), h_ref[...], w_ref[...])
        dh, dw = vjp(ct)
        dh = dh + r_ref[...]
        dh_ref[...] = dh
        dh16_ref[...] = dh.astype(BF16)
        _accumulate(dw_ref, dw)

    return _rows_call(name, body, [h, *cts, resid], [w], [(n, F32), (n, BF16)], [(1, n)], _row_tile(h.shape[0]), host=host)


def _mla_consts_from_refs(qa, wq, kva, wkv, qn, kn, perm):
    f = lambda r: r[...].astype(F32)
    return dict(
        qa_w=f(qa), kva_w=f(kva), perm=f(perm),
        wq_n=[wq[h * QK_PAD:h * QK_PAD + HEAD, :].astype(F32) for h in range(HEADS)],
        wq_r=[wq[h * QK_PAD + HEAD:(h + 1) * QK_PAD, :].astype(F32) for h in range(HEADS)],
        wk_n=[wkv[h * QK_PAD:h * QK_PAD + HEAD, :].astype(F32) for h in range(HEADS)],
        wv=[wkv[h * QK_PAD + HEAD:(h + 1) * QK_PAD, :].astype(F32) for h in range(HEADS)],
        qn_n=qn[:, 0:HEAD], qn_r=qn[:, HEAD:QK_PAD], kn_n=kn[:, 0:HEAD], kn_r=kn[:, HEAD:QK_PAD])


def _mla_prep_fwd(q_lat, kv_lat, k_pe, cos, sin, qa, wq, kva, wkv, qn, kn, perm):
    def body(ql, kvl, kp, c, s, qa_r, wq_r, kva_r, wkv_r, qn_r, kn_r, p_r, q_out, k_out, v_out):
        consts = _mla_consts_from_refs(qa_r, wq_r, kva_r, wkv_r, qn_r, kn_r, p_r)
        outs = _mla_prep_fn((ql[...], kvl[...], kp[...], c[...], s[...]), consts)
        for h in range(HEADS):
            q_n, q_r, k_n, k_r, v = outs[5 * h:5 * h + 5]
            q_out[:, h * QK_PAD:h * QK_PAD + HEAD] = q_n.astype(BF16)
            q_out[:, h * QK_PAD + HEAD:(h + 1) * QK_PAD] = q_r.astype(BF16)
            k_out[:, h * QK_PAD:h * QK_PAD + HEAD] = k_n.astype(BF16)
            k_out[:, h * QK_PAD + HEAD:(h + 1) * QK_PAD] = k_r.astype(BF16)
            v_out[:, h * HEAD:(h + 1) * HEAD] = v.astype(BF16)

    return _rows_call("mla_prep_fwd", body, [q_lat, kv_lat, k_pe, cos, sin], [qa, wq, kva, wkv, qn, kn, perm],
                      [(HEADS * QK_PAD, BF16), (HEADS * QK_PAD, BF16), (DN_WIDTH, BF16)], [], _row_tile(cos.shape[0]))


def _mla_prep_bwd(q_lat, kv_lat, k_pe, cos, sin, dq, dk, dv, qa, wq, kva, wkv, qn, kn, perm, host=None):
    def body(ql, kvl, kp, c, s, dq_r, dk_r, dv_r, qa_r, wq_r, kva_r, wkv_r, qn_r, kn_r, p_r,
             dql, dkvl, dkp, dqa, dwq, dkva, dwkv, dqn, dkn):
        consts = _mla_consts_from_refs(qa_r, wq_r, kva_r, wkv_r, qn_r, kn_r, p_r)
        cc, ss, pm = c[...], s[...], consts.pop("perm")
        _, vjp = jax.vjp(lambda rows, cs: _mla_prep_fn((*rows, cc, ss), dict(cs, perm=pm)), (ql[...], kvl[...], kp[...]),
                         consts)
        cts = []
        for h in range(HEADS):
            cts += [dq_r[:, h * QK_PAD:h * QK_PAD + HEAD], dq_r[:, h * QK_PAD + HEAD:(h + 1) * QK_PAD],
                    dk_r[:, h * QK_PAD:h * QK_PAD + HEAD], dk_r[:, h * QK_PAD + HEAD:(h + 1) * QK_PAD],
                    dv_r[:, h * HEAD:(h + 1) * HEAD]]
        (d_ql, d_kvl, d_kp), dc = vjp(tuple(cts))
        dql[...] = d_ql.astype(BF16)
        dkvl[...] = d_kvl.astype(BF16)
        dkp[...] = d_kp.astype(BF16)
        first = pl.program_id(0) == 0

        def acc(ref, sl, val):
            @pl.when(first)
            def _():
                ref[sl] = val

            @pl.when(jnp.logical_not(first))
            def _():
                ref[sl] += val

        full = (slice(None), slice(None))
        acc(dqa, full, dc["qa_w"])
        acc(dkva, full, dc["kva_w"])
        for h in range(HEADS):
            acc(dwq, (slice(h * QK_PAD, h * QK_PAD + HEAD), slice(None)), dc["wq_n"][h])
            acc(dwq, (slice(h * QK_PAD + HEAD, (h + 1) * QK_PAD), slice(None)), dc["wq_r"][h])
            acc(dwkv, (slice(h * QK_PAD, h * QK_PAD + HEAD), slice(None)), dc["wk_n"][h])
            acc(dwkv, (slice(h * QK_PAD + HEAD, (h + 1) * QK_PAD), slice(None)), dc["wv"][h])
        acc(dqn, (slice(None), slice(0, HEAD)), dc["qn_n"])
        acc(dqn, (slice(None), slice(HEAD, QK_PAD)), dc["qn_r"])
        acc(dkn, (slice(None), slice(0, HEAD)), dc["kn_n"])
        acc(dkn, (slice(None), slice(HEAD, QK_PAD)), dc["kn_r"])

    return _rows_call("mla_prep_bwd", body, [q_lat, kv_lat, k_pe, cos, sin, dq, dk, dv],
                      [qa, wq, kva, wkv, qn, kn, perm],
                      [(LORA, BF16), (LORA, BF16), (HEAD, BF16)],
                      [(1, LORA), wq.shape, (1, LORA), wkv.shape, (1, QK_PAD), (1, QK_PAD)], _row_tile(cos.shape[0]),
                      host=host)


ATTN_Q_ROWS = 256


def _attn_blocks(t):
    return [(r0, min(ATTN_Q_ROWS, t - r0)) for r0 in range(0, t, ATTN_Q_ROWS)]


def _attn_fwd(q, k, v, host=None):
    t = q.shape[0]

    def body(q_ref, k_ref, v_ref, o_ref):
        for r0, rows in _attn_blocks(t):
            ext = r0 + rows
            o_ref[r0:ext, :] = _attn_fn(q_ref[r0:ext, :], k_ref[0:ext, :], v_ref[0:ext, :], r0)

    qk_spec = pl.BlockSpec((t, QK_PAD), lambda h: (0, h))
    v_spec = pl.BlockSpec((t, HEAD), lambda h: (0, h))
    return _pcall(body, "attn_fwd", (HEADS,), [qk_spec, qk_spec, v_spec], v_spec,
                  jax.ShapeDtypeStruct((t, HEADS * HEAD), F32), [q, k, v], ("arbitrary",), host=host)


def _attn_bwd(q, k, v, do, host=None):
    t = q.shape[0]

    def body(q_ref, k_ref, v_ref, do_ref, dq_ref, dk_ref, dv_ref):
        dk_ref[...] = jnp.zeros(dk_ref.shape, F32)
        dv_ref[...] = jnp.zeros(dv_ref.shape, F32)
        for r0, rows in _attn_blocks(t):
            ext = r0 + rows
            _, vjp = jax.vjp(functools.partial(_attn_fn, row0=r0), q_ref[r0:ext, :].astype(F32),
                             k_ref[0:ext, :].astype(F32), v_ref[0:ext, :].astype(F32))
            dq, dk, dv = vjp(do_ref[r0:ext, :])
            dq_ref[r0:ext, :] = dq
            dk_ref[0:ext, :] += dk
            dv_ref[0:ext, :] += dv

    qk_spec = pl.BlockSpec((t, QK_PAD), lambda h: (0, h))
    v_spec = pl.BlockSpec((t, HEAD), lambda h: (0, h))
    return _pcall(body, "attn_bwd", (HEADS,), [qk_spec, qk_spec, v_spec, v_spec], [qk_spec, qk_spec, v_spec],
                  [jax.ShapeDtypeStruct((t, HEADS * QK_PAD), F32), jax.ShapeDtypeStruct((t, HEADS * QK_PAD), F32),
                   jax.ShapeDtypeStruct((t, HEADS * HEAD), F32)], [q, k, v, do], ("arbitrary",), host=host)


def _mix_out_fwd(o_mla, o_dn, z, w_mla, w_dn):
    def body(om_ref, od_ref, z_ref, wm_ref, wd_ref, o_ref):
        for h in range(HEADS):
            sl = slice(h * HEAD, (h + 1) * HEAD)
            o_ref[:, sl] = _rms(om_ref[:, sl], wm_ref[...], HEAD).astype(BF16)
            o_ref[:, DN_WIDTH + h * HEAD:DN_WIDTH + (h + 1) * HEAD] = _dn_out_fn(od_ref[:, sl], z_ref[:, sl],
                                                                                 wd_ref[...]).astype(BF16)

    return _rows_call("mix_out_fwd", body, [o_mla, o_dn, z], [w_mla, w_dn], [(2 * DN_WIDTH, BF16)], [],
                      _row_tile(o_mla.shape[0]))[0]


def _mix_out_bwd(o_mla, o_dn, z, dmixed, w_mla, w_dn):
    def body(om_ref, od_ref, z_ref, dm_ref, wm_ref, wd_ref, dom_ref, dod_ref, dz_ref, dwm_ref, dwd_ref):
        dwm = dwd = None
        for h in range(HEADS):
            sl = slice(h * HEAD, (h + 1) * HEAD)
            _, vjp = jax.vjp(lambda o, w: _rms(o, w, HEAD), om_ref[:, sl], wm_ref[...])
            do, dw = vjp(dm_ref[:, sl])
            dom_ref[:, sl] = do
            dwm = dw if dwm is None else dwm + dw
            _, vjp = jax.vjp(_dn_out_fn, od_ref[:, sl], z_ref[:, sl], wd_ref[...])
            do, dz, dw = vjp(dm_ref[:, DN_WIDTH + h * HEAD:DN_WIDTH + (h + 1) * HEAD])
            dod_ref[:, sl] = do
            dz_ref[:, sl] = dz.astype(BF16)
            dwd = dw if dwd is None else dwd + dw
        _accumulate(dwm_ref, dwm)
        _accumulate(dwd_ref, dwd)

    return _rows_call("mix_out_bwd", body, [o_mla, o_dn, z, dmixed], [w_mla, w_dn],
                      [(DN_WIDTH, F32), (DN_WIDTH, F32), (DN_WIDTH, BF16)], [(1, HEAD), (1, HEAD)],
                      _row_tile(o_mla.shape[0]))


def _shift_down(x, s):
    if s == 0:
        return x
    rows = lax.broadcasted_iota(jnp.int32, x.shape, 0)
    return jnp.where(rows >= s, pltpu.roll(x, s, 0), 0.0)


def _shift_up(x, s):
    if s == 0:
        return x
    t = x.shape[0]
    rows = lax.broadcasted_iota(jnp.int32, x.shape, 0)
    return jnp.where(rows < t - s, pltpu.roll(x, t - s, 0), 0.0)


def _col_call(name, body, cols, taps, outs, tap_outs, cw):
    t, c = cols[0].shape[0], taps[0].shape[1]
    in_specs = [pl.BlockSpec((t, cw), lambda j: (0, j)) for _ in cols]
    in_specs += [pl.BlockSpec((a.shape[0], cw), lambda j: (0, j)) for a in taps]
    out_shape = [jax.ShapeDtypeStruct((t, c), dt) for dt in outs] + [jax.ShapeDtypeStruct((n, c), F32) for n in tap_outs]
    out_specs = [pl.BlockSpec((t, cw), lambda j: (0, j)) for _ in outs]
    out_specs += [pl.BlockSpec((n, cw), lambda j: (0, j)) for n in tap_outs]
    return pl.pallas_call(body, name=name, grid=(c // cw,), in_specs=in_specs, out_specs=out_specs, out_shape=out_shape,
                          compiler_params=_cparams(("arbitrary",)))(*cols, *taps)


def _causal_conv(x, w_ref, width):
    acc = w_ref[width - 1:width, :] * x
    for j in range(width - 1):
        acc = acc + w_ref[j:j + 1, :] * _shift_down(x, width - 1 - j)
    return acc


def _causal_conv_bwd(x, dpre, w_ref, dx_ref, dw_ref, width):
    dx = w_ref[width - 1:width, :] * dpre
    dw_ref[width - 1:width, :] = jnp.sum(dpre * x, axis=0, keepdims=True)
    for j in range(width - 1):
        s = width - 1 - j
        dx = dx + w_ref[j:j + 1, :] * _shift_up(dpre, s)
        dw_ref[j:j + 1, :] = jnp.sum(dpre * _shift_down(x, s), axis=0, keepdims=True)
    dx_ref[...] = dx.astype(dx_ref.dtype)


def _dsilu(x):
    sg = jax.nn.sigmoid(x)
    return sg * (1.0 + x * (1.0 - sg))


def _dn_conv_fwd(x, w):
    def body(x_ref, w_ref, y_ref):
        y_ref[...] = _silu(_causal_conv(x_ref[...], w_ref, 4))

    return _col_call("dn_conv_fwd", body, [x], [w], [F32], [], 256)[0]


def _dn_conv_bwd(x, w, dy):
    def body(x_ref, dy_ref, w_ref, dx_ref, dw_ref):
        xv = x_ref[...]
        dpre = dy_ref[...] * _dsilu(_causal_conv(xv, w_ref, 4))
        _causal_conv_bwd(xv, dpre, w_ref, dx_ref, dw_ref, 4)

    return _col_call("dn_conv_bwd", body, [x, dy], [w], [BF16], [4], 256)


def _glu_fwd(gpre, up, w, b):
    def body(g_ref, u_ref, w_ref, b_ref, a_ref):
        gate = _causal_conv(g_ref[...], w_ref, 3) + b_ref[...]
        a_ref[...] = (_silu(gate) * u_ref[...]).astype(BF16)

    return _col_call("glu_fwd", body, [gpre, up], [w, b], [BF16], [], 256)[0]


def _glu_bwd(gpre, up, w, b, dact):
    def body(g_ref, u_ref, da_ref, w_ref, b_ref, dg_ref, du_ref, dw_ref, db_ref):
        gv = g_ref[...]
        gate = _causal_conv(gv, w_ref, 3) + b_ref[...]
        da = da_ref[...]
        sg = jax.nn.sigmoid(gate)
        du_ref[...] = (da * (gate * sg)).astype(BF16)
        dgate = da * u_ref[...] * (sg * (1.0 + gate * (1.0 - sg)))
        db_ref[...] = jnp.sum(dgate, axis=0, keepdims=True)
        _causal_conv_bwd(gv, dgate, w_ref, dg_ref, dw_ref, 3)

    return _col_call("glu_bwd", body, [gpre, up, dact], [w, b], [BF16, BF16], [3, 1], 256)


def _dn_prep_consts(sa, sb, al, dt):
    return dict(sel_a=sa[...], sel_b=sb[...], alog=al[...], dtb=dt[...])


def _dn_prep_fwd(conv, ab, sel_a, sel_b, alog, dtb):
    def body(c_ref, ab_ref, sa, sb, al, dt, q_out, k_out, g_out, b_out):
        qc = tuple(c_ref[:, h * HEAD:(h + 1) * HEAD] for h in range(HEADS))
        kc = tuple(c_ref[:, DN_WIDTH + h * HEAD:DN_WIDTH + (h + 1) * HEAD] for h in range(HEADS))
        qs, ks, g, beta = _dn_prep_fn((qc, kc, ab_ref[...]), _dn_prep_consts(sa, sb, al, dt))
        for h in range(HEADS):
            q_out[:, h * HEAD:(h + 1) * HEAD] = qs[h]
            k_out[:, h * HEAD:(h + 1) * HEAD] = ks[h]
        g_out[...] = g
        b_out[...] = beta

    return _rows_call("dn_prep_fwd", body, [conv, ab], [sel_a, sel_b, alog, dtb], [(DN_WIDTH, F32)] * 4, [],
                      _row_tile(conv.shape[0]))


def _dn_prep_bwd(conv, ab, dq, dk, dv, dg, db, sel_a, sel_b, alog, dtb):
    def body(c_ref, ab_ref, dq_r, dk_r, dv_r, dg_r, db_r, sa, sb, al, dt, dc_out, dab_out, dal_out, ddt_out):
        qc = tuple(c_ref[:, h * HEAD:(h + 1) * HEAD] for h in range(HEADS))
        kc = tuple(c_ref[:, DN_WIDTH + h * HEAD:DN_WIDTH + (h + 1) * HEAD] for h in range(HEADS))
        consts = _dn_prep_consts(sa, sb, al, dt)
        sel = dict(sel_a=consts["sel_a"], sel_b=consts["sel_b"])
        _, vjp = jax.vjp(lambda rows, ad: _dn_prep_fn(rows, {**sel, **ad}), (qc, kc, ab_ref[...]),
                         dict(alog=consts["alog"], dtb=consts["dtb"]))
        cq = tuple(dq_r[:, h * HEAD:(h + 1) * HEAD] for h in range(HEADS))
        ck = tuple(dk_r[:, h * HEAD:(h + 1) * HEAD] for h in range(HEADS))
        (dqc, dkc, dab), dad = vjp((cq, ck, dg_r[...], db_r[...]))
        for h in range(HEADS):
            dc_out[:, h * HEAD:(h + 1) * HEAD] = dqc[h]
            dc_out[:, DN_WIDTH + h * HEAD:DN_WIDTH + (h + 1) * HEAD] = dkc[h]
        dc_out[:, 2 * DN_WIDTH:3 * DN_WIDTH] = dv_r[...]
        dab_out[...] = dab.astype(BF16)
        _accumulate(dal_out, dad["alog"])
        _accumulate(ddt_out, dad["dtb"])

    return _rows_call("dn_prep_bwd", body, [conv, ab, dq, dk, dv, dg, db], [sel_a, sel_b, alog, dtb],
                      [(3 * DN_WIDTH, F32), (HEAD, BF16)], [(1, DN_WIDTH), (1, DN_WIDTH)], _row_tile(conv.shape[0]))


def _chunk_batch(t):
    nc = t // CHUNK
    return nc // 2 if nc % 2 == 0 else nc


def _dn_chunk_specs(t, nb):
    rows = nb * CHUNK
    hb = lambda h, b: (b, h)
    vb = lambda h, b: (b, 2 * HEADS + h)
    qk_spec = pl.BlockSpec((None, rows, CHUNK), lambda h, b: (h, b, 0))
    blk = pl.BlockSpec((rows, HEAD), hb)
    return rows, blk, pl.BlockSpec((rows, HEAD), vb), qk_spec


def _dn_chunk_fwd(qn, kn, conv, g, beta, host=None):
    t = qn.shape[0]
    nb = _chunk_batch(t)
    rows, blk, vblk, qk_spec = _dn_chunk_specs(t, nb)

    def body(q_ref, k_ref, v_ref, g_ref, b_ref, u_o, w_o, qg_o, kd_o, eg_o, qk_o):
        r3 = lambda x: x.reshape(nb, CHUNK, x.shape[-1])
        outs = _dn_chunk_fn(r3(q_ref[...]), r3(k_ref[...]), r3(v_ref[...]), r3(g_ref[...]), r3(g_ref[:, 0:CHUNK]),
                            r3(b_ref[...]))
        for o_ref, val in zip((u_o, w_o, qg_o, kd_o, eg_o, qk_o), outs):
            o_ref[...] = val.reshape(rows, val.shape[-1])

    return _pcall(body, "dn_chunk_fwd", (HEADS, t // rows), [blk, blk, vblk, blk, blk], [blk] * 5 + [qk_spec],
                  [jax.ShapeDtypeStruct((t, DN_WIDTH), F32)] * 5 + [jax.ShapeDtypeStruct((HEADS, t, CHUNK), F32)],
                  [qn, kn, conv, g, beta], ("arbitrary", "arbitrary"), host=host)


def _dn_chunk_bwd(qn, kn, conv, g, beta, cts, host=None):
    t = qn.shape[0]
    nb = _chunk_batch(t)
    rows, blk, vblk, qk_spec = _dn_chunk_specs(t, nb)

    def body(q_ref, k_ref, v_ref, g_ref, b_ref, du, dw, dqg, dkd, deg, dqk, dq_o, dk_o, dv_o, dg_o, db_o):
        r3 = lambda x: x.reshape(nb, CHUNK, x.shape[-1])
        _, vjp = jax.vjp(_dn_chunk_fn, r3(q_ref[...]), r3(k_ref[...]), r3(v_ref[...]), r3(g_ref[...]),
                         r3(g_ref[:, 0:CHUNK]), r3(b_ref[...]))
        dq, dk, dv, dg, dg64, db = vjp(tuple(r3(c[...]) for c in (du, dw, dqg, dkd, deg, dqk)))
        for o_ref, val in zip((dq_o, dk_o, dv_o, dg_o, db_o), (dq, dk, dv, dg, db)):
            o_ref[...] = val.reshape(rows, HEAD)
        dg_o[:, 0:CHUNK] += dg64.reshape(rows, CHUNK)

    return _pcall(body, "dn_chunk_bwd", (HEADS, t // rows), [blk, blk, vblk, blk, blk] + [blk] * 5 + [qk_spec], [blk] * 5,
                  [jax.ShapeDtypeStruct((t, DN_WIDTH), F32)] * 5, [qn, kn, conv, g, beta, *cts], ("arbitrary", "arbitrary"),
                  host=host)


def _dn_rec_fwd(u, w, qg, kd, eg, qk, host=None):
    t = u.shape[0]
    nc = t // CHUNK
    blk = pl.BlockSpec((CHUNK, DN_WIDTH), lambda c: (c, 0))
    qk_spec = pl.BlockSpec((HEADS, CHUNK, CHUNK), lambda c: (0, c, 0))
    s_spec = pl.BlockSpec((None, DN_WIDTH, HEAD), lambda c: (c, 0, 0))

    def body(u_ref, w_ref, qg_ref, kd_ref, eg_ref, qk_ref, o_ref, sall_ref, s_scr):
        @pl.when(pl.program_id(0) == 0)
        def _():
            s_scr[...] = jnp.zeros(s_scr.shape, F32)

        sall_ref[...] = s_scr[...]
        for h in range(HEADS):
            sl = slice(h * HEAD, (h + 1) * HEAD)
            s_new, o = _dn_rec_fn(s_scr[sl, :], u_ref[:, sl], w_ref[:, sl], qg_ref[:, sl], qk_ref[h], kd_ref[:, sl],
                                  eg_ref[0:1, sl])
            o_ref[:, sl] = o
            s_scr[sl, :] = s_new

    return _pcall(body, "dn_rec_fwd", (nc,), [blk] * 5 + [qk_spec], [blk, s_spec],
                  [jax.ShapeDtypeStruct((t, DN_WIDTH), F32), jax.ShapeDtypeStruct((nc, DN_WIDTH, HEAD), F32)],
                  [u, w, qg, kd, eg, qk], ("arbitrary",), scratch_shapes=[pltpu.VMEM((DN_WIDTH, HEAD), F32)], host=host)


def _dn_rec_bwd(u, w, qg, kd, eg, qk, sall, do, host=None):
    t = u.shape[0]
    nc = t // CHUNK
    blk = pl.BlockSpec((CHUNK, DN_WIDTH), lambda c: (nc - 1 - c, 0))
    qk_spec = pl.BlockSpec((HEADS, CHUNK, CHUNK), lambda c: (0, nc - 1 - c, 0))
    s_spec = pl.BlockSpec((None, DN_WIDTH, HEAD), lambda c: (nc - 1 - c, 0, 0))

    def body(u_ref, w_ref, qg_ref, kd_ref, eg_ref, qk_ref, s_ref, do_ref, du_o, dw_o, dqg_o, dkd_o, deg_o, dqk_o, ds_scr):
        @pl.when(pl.program_id(0) == 0)
        def _():
            ds_scr[...] = jnp.zeros(ds_scr.shape, F32)

        deg_o[...] = jnp.zeros(deg_o.shape, F32)
        for h in range(HEADS):
            sl = slice(h * HEAD, (h + 1) * HEAD)
            _, vjp = jax.vjp(_dn_rec_fn, s_ref[sl, :], u_ref[:, sl], w_ref[:, sl], qg_ref[:, sl], qk_ref[h],
                             kd_ref[:, sl], eg_ref[0:1, sl])
            ds, du, dw, dqg, dqk, dkd, deg = vjp((ds_scr[sl, :], do_ref[:, sl]))
            du_o[:, sl] = du
            dw_o[:, sl] = dw
            dqg_o[:, sl] = dqg
            dkd_o[:, sl] = dkd
            deg_o[0:1, sl] = deg
            dqk_o[h] = dqk
            ds_scr[sl, :] = ds

    return _pcall(body, "dn_rec_bwd", (nc,), [blk] * 5 + [qk_spec, s_spec, blk], [blk] * 5 + [qk_spec],
                  [jax.ShapeDtypeStruct((t, DN_WIDTH), F32)] * 5 + [jax.ShapeDtypeStruct((HEADS, t, CHUNK), F32)],
                  [u, w, qg, kd, eg, qk, sall, do], ("arbitrary",), scratch_shapes=[pltpu.VMEM((DN_WIDTH, HEAD), F32)],
                  host=host)


def _loss_call(h2, tgt, n_valid):
    t, n = h2.shape
    r = _row_tile(t)

    def body(h_ref, t_ref, dy_ref, dy16_ref, acc_ref):
        rows = pl.program_id(0) * r + lax.broadcasted_iota(jnp.int32, (r, n), 0)
        valid = jnp.logical_and(rows >= N_META, rows < n_valid)
        e = jnp.where(valid, h_ref[...] - t_ref[...], 0.0)
        dy = e * (1.0 / n)
        dy_ref[...] = dy
        dy16_ref[...] = dy.astype(BF16)
        _accumulate(acc_ref, jnp.sum(e * e, axis=0, keepdims=True))

    return _rows_call("loss", body, [h2, tgt], [], [(n, F32), (n, BF16)], [(1, n)], r)


def _adamw_call(name, w, g, m, v, host=None):
    rows, cols = w.shape
    by_rows = rows % 8 == 0

    def body(w_ref, g_ref, m_ref, v_ref, g_out, d_ref, m_out, v_out):
        gv = g_ref[...] if by_rows else g_ref[0:rows, :]
        m2 = ADAM_B1 * m_ref[...] + (1.0 - ADAM_B1) * gv
        v2 = ADAM_B2 * v_ref[...] + (1.0 - ADAM_B2) * (gv * gv)
        m_hat = m2 / (1.0 - ADAM_B1 ** ADAM_STEP)
        v_hat = v2 / (1.0 - ADAM_B2 ** ADAM_STEP)
        g_out[...] = gv
        d_ref[...] = -ADAM_LR * (m_hat / (jnp.sqrt(v_hat) + ADAM_EPS) + ADAM_WD * w_ref[...])
        m_out[...] = m2
        v_out[...] = v2

    if by_rows:
        tr = _tile(rows, 256, 8)
        spec = g_spec = pl.BlockSpec((tr, cols), lambda i: (i, 0))
        grid = (rows // tr,)
    else:
        tc = _tile(cols, 256, 128)
        spec = pl.BlockSpec((rows, tc), lambda j: (0, j))
        g_spec = pl.BlockSpec((g.shape[0], tc), lambda j: (0, j))
        grid = (cols // tc,)
    return _pcall(body, name, grid, [spec, g_spec, spec, spec], [spec] * 4, [jax.ShapeDtypeStruct((rows, cols), F32)] * 4,
                  [w, g, m, v], ("arbitrary",), host=host)


def _rope_tables(t):
    half = ROPE // 2
    inv_freq = np.float32(ROPE_THETA) ** (-np.arange(half, dtype=np.float32) / np.float32(half))
    ang = np.arange(t, dtype=np.float32)[:, None] * inv_freq[None, :].astype(np.float32)
    z = np.zeros((t, HEAD - ROPE), np.float32)
    cos = np.concatenate([np.cos(ang), np.cos(ang), z], axis=1).astype(np.float32)
    sin = np.concatenate([np.sin(ang), np.sin(ang), z], axis=1).astype(np.float32)
    k = np.arange(HEAD)[:, None]
    l = np.arange(HEAD)[None, :]
    perm = np.where((l < half) & (k == l + half), -1.0, 0.0) + np.where((l >= half) & (l < ROPE) & (k == l - half), 1.0, 0.0)
    return jnp.asarray(cos), jnp.asarray(sin), jnp.asarray(perm.astype(np.float32))


def _win_to_pad(w):
    z = lambda n: jnp.zeros((n, w.shape[1]), w.dtype)
    return jnp.concatenate([w[576:2112], w[2112:2624], w[0:256], w[256:512], w[512:576], z(64), w[2624:2632], z(120)],
                           axis=0)


def _win_from_pad(g):
    return jnp.concatenate([g[2048:2304], g[2304:2560], g[2560:2624], g[0:1536], g[1536:2048], g[2688:2696]], axis=0)


def _qk_to_pad(w):
    w4 = w.reshape(HEADS, QK_DIM, w.shape[-1])
    return jnp.concatenate([w4, jnp.zeros((HEADS, QK_PAD - QK_DIM, w.shape[-1]), w.dtype)], axis=1).reshape(
        HEADS * QK_PAD, w.shape[-1])


def _qk_from_pad(g):
    return g.reshape(HEADS, QK_PAD, g.shape[-1])[:, :QK_DIM].reshape(HEADS * QK_DIM, g.shape[-1])


def _ff_to_pad(a, axis):
    shape = list(a.shape)
    shape[axis:axis + 1] = [N_CHIPS, FF_SHARD]
    a4 = a.reshape(shape)
    shape[axis + 1] = FF_BLOCK - FF_SHARD
    out = jnp.concatenate([a4, jnp.zeros(shape, a.dtype)], axis=axis + 1)
    shape[axis:axis + 2] = [D_FF_P]
    return out.reshape(shape)


def _ff_from_pad(a, axis):
    shape = list(a.shape)
    shape[axis:axis + 1] = [N_CHIPS, FF_BLOCK]
    a4 = lax.slice_in_dim(a.reshape(shape), 0, FF_SHARD, axis=axis + 1)
    shape[axis:axis + 2] = [D_FF]
    return a4.reshape(shape)


class _LocalPlan:
    def __init__(self, wt):
        self.wt, self.grads = wt, {}

    def weight(self, name):
        return self.wt[name]

    def host(self, point):
        return None

    def grad(self, name, value):
        self.grads[name] = value


def _local_step(x, tgt, wt, plan=None):
    plan = _LocalPlan(wt) if plan is None else plan
    s = x.shape[0]
    n_valid = N_META + s
    t = -(-n_valid // HEAD) * HEAD
    zpad = jnp.zeros((t - n_valid, D_MODEL), F32)
    h0 = jnp.concatenate([wt["meta_tokens"], x, zpad], axis=0)
    tgt_p = jnp.concatenate([jnp.zeros((N_META, D_MODEL), F32), tgt, zpad], axis=0)
    cos, sin, perm = _rope_tables(t)
    win, wq, wkv = wt["w_in_t"], wt["w_q_t"], wt["w_kv_t"]
    qn_w = jnp.concatenate([wt["q_norm_w"], jnp.zeros((1, QK_PAD - QK_DIM), F32)], axis=1)
    kn_w = jnp.concatenate([wt["k_norm_w"], jnp.zeros((1, QK_PAD - QK_DIM), F32)], axis=1)
    head_id = jnp.arange(DN_WIDTH)[None, :] // HEAD
    lane = jnp.arange(HEAD)[:, None]
    sel_a = (lane == head_id).astype(F32)
    sel_b = (lane == head_id + HEADS).astype(F32)
    alog = jnp.repeat(wt["dn_A_log"], HEAD, axis=1)
    dtb = jnp.repeat(wt["dn_dt_bias"], HEAD, axis=1)
    conv_w, conv_b = wt["ffn_conv_w"], wt["ffn_conv_b"]

    u = _rms_fwd("attn_norm_fwd", h0, wt["attn_norm_w"])
    proj = _matmul("in_proj", u, win, "nt", F32)
    z = (proj, DN_WIDTH, 3)
    q_lat, kv_lat, k_pe, ab = (proj, LORA, 8), (proj, LORA, 9), (proj, HEAD, 20), (proj, HEAD, 21)
    mla_consts = (wt["q_a_norm_w"], wq, wt["kv_a_norm_w"], wkv, qn_w, kn_w, perm)
    q, k, v = _mla_prep_fwd(q_lat, kv_lat, k_pe, cos, sin, *mla_consts)
    o_mla = _attn_fwd(q, k, v, host=plan.host("attn_fwd"))
    conv = _dn_conv_fwd(proj, wt["dn_conv_w"])
    dn_consts = (sel_a, sel_b, alog, dtb)
    qn, kn, g, beta = _dn_prep_fwd(conv, ab, *dn_consts)
    cu, cw, cqg, ckd, ceg, cqk = _dn_chunk_fwd(qn, kn, conv, g, beta, host=plan.host("dn_chunk_fwd"))
    o_dn, sall = _dn_rec_fwd(cu, cw, cqg, ckd, ceg, cqk, host=plan.host("dn_rec_fwd"))
    mixed = _mix_out_fwd(o_mla, o_dn, z, wt["mla_out_norm_w"], wt["dn_out_norm_w"])
    w_out = plan.weight("w_out")
    h1 = _matmul("out_proj", mixed, w_out, "nn", F32, res=h0)
    n2 = _rms_fwd("ffn_norm_fwd", h1, wt["ffn_norm_w"])
    w_gate, w_up, w_down = plan.weight("w_gate_t"), plan.weight("w_up_t"), plan.weight("w_down")
    gpre = _matmul("gate_proj", n2, w_gate, "nt", F32)
    up = _matmul("up_proj", n2, w_up, "nt", F32)
    act = _glu_fwd(gpre, up, conv_w, conv_b)
    h2 = _matmul("down_proj", act, w_down, "nn", F32, res=h1)
    dy, dy16, sq = _loss_call(h2, tgt_p, n_valid)

    grads = {}
    dact = _matmul("down_dx", dy16, w_down, "nt", F32)
    plan.grad("w_down", _matmul("down_dw", act, dy16, "tn", F32))
    dgpre, dup, grads["ffn_conv_w"], grads["ffn_conv_b"] = _glu_bwd(gpre, up, conv_w, conv_b, dact)
    plan.grad("w_gate_t", _matmul("gate_dw", dgpre, n2, "tn", F32))
    plan.grad("w_up_t", _matmul("up_dw", dup, n2, "tn", F32))
    dn2a = _matmul("gate_dx", dgpre, w_gate, "nn", F32)
    dn2b = _matmul("up_dx", dup, w_up, "nn", F32)
    dh1, dh1_16, grads["ffn_norm_w"] = _rms_bwd("ffn_norm_bwd", h1, wt["ffn_norm_w"], [dn2a, dn2b], dy)
    dmixed = _matmul("out_dx", dh1_16, w_out, "nt", F32)
    plan.grad("w_out", _matmul("out_dw", mixed, dh1_16, "tn", F32))
    do_mla, do_dn, dz, grads["mla_out_norm_w"], grads["dn_out_norm_w"] = _mix_out_bwd(
        o_mla, o_dn, z, dmixed, wt["mla_out_norm_w"], wt["dn_out_norm_w"])
    rec_cts = _dn_rec_bwd(cu, cw, cqg, ckd, ceg, cqk, sall, do_dn, host=plan.host("dn_rec_bwd"))
    dqn, dkn, dv_dn, dg, dbeta = _dn_chunk_bwd(qn, kn, conv, g, beta, rec_cts, host=plan.host("dn_chunk_bwd"))
    dconv, dab, dalog, ddtb = _dn_prep_bwd(conv, ab, dqn, dkn, dv_dn, dg, dbeta, *dn_consts)
    grads["dn_A_log"] = jnp.sum(dalog.reshape(HEADS, HEAD), axis=1)[None, :]
    grads["dn_dt_bias"] = jnp.sum(ddtb.reshape(HEADS, HEAD), axis=1)[None, :]
    ddn_pre, grads["dn_conv_w"] = _dn_conv_bwd(proj, wt["dn_conv_w"], dconv)
    dq, dk, dv = _attn_bwd(q, k, v, do_mla, host=plan.host("attn_bwd"))
    dq_lat, dkv_lat, dk_pe, dqa, dwq, dkva, dwkv, dqnw, dknw = _mla_prep_bwd(
        q_lat, kv_lat, k_pe, cos, sin, dq, dk, dv, *mla_consts, host=plan.host("mla_prep_bwd"))
    grads["q_a_norm_w"], grads["kv_a_norm_w"] = dqa, dkva
    plan.grad("w_q_t", dwq)
    plan.grad("w_kv_t", dwkv)
    grads["q_norm_w"], grads["k_norm_w"] = dqnw[:, :QK_DIM], dknw[:, :QK_DIM]
    dproj = jnp.concatenate([ddn_pre, dz, dq_lat, dkv_lat, dk_pe, dab], axis=1)
    plan.grad("w_in_t", _matmul("in_dw", dproj, u, "tn", F32))
    du = _matmul("in_dx", dproj, win, "nn", F32, host=plan.host("in_dx"))
    dh0, _, grads["attn_norm_w"] = _rms_bwd("attn_norm_bwd", h0, wt["attn_norm_w"], [du], dh1,
                                            host=plan.host("attn_norm_bwd"))
    grads["meta_tokens"] = dh0[0:N_META]
    if isinstance(plan, _LocalPlan):
        grads.update(plan.grads)
    return sq, dh0[N_META:n_valid], grads


def _mesh_pos():
    return lax.axis_index("x"), lax.axis_index("y"), lax.axis_index("c")


def _other_chips(x, y):
    return [(1 - x, y), (x, 1 - y), (1 - x, 1 - y)]


def _remote(src, dst, send_sems, recv_sems, k, to):
    return pltpu.make_async_remote_copy(src_ref=src, dst_ref=dst, send_sem=send_sems.at[k], recv_sem=recv_sems.at[k],
                                        device_id=to, device_id_type=MESH)


def _copies_exchange(make, ins, out_shape, nsem):
    def prog(in_refs, out_refs, send_sems, recv_sems):
        copies = make(in_refs, out_refs, send_sems, recv_sems)

        def start():
            for cp in copies:
                cp.start()

        def finish():
            for cp in copies:
                cp.wait()

        return start, finish

    return _Exchange(prog, ins, out_shape, nsem)


def _all_gather(shards):
    def prog(srcs, dsts, send_sems, recv_sems):
        x, y, c = _mesh_pos()
        p = 2 * x + y
        sibling = (x, y, 1 - c)
        chips = _other_chips(x, y)
        bufs = tuple((s, d, s.shape[0] // 2) for s, d in zip(srcs, dsts))

        def half(ref, rows, which):
            return ref.at[pl.ds(which * rows, rows), :]

        def copy(i, k, src, dst, to):
            return _remote(src, dst, send_sems, recv_sems, 6 * i + k, to)

        sends = [copy(i, j, half(src, rows, c), half(dst.at[p], rows, c), (*chip, c))
                 for i, (src, dst, rows) in enumerate(bufs) for j, chip in enumerate(chips)]

        def start():
            for cp in sends:
                cp.start()

        def finish():
            passed = []
            for i, (src, dst, rows) in enumerate(bufs):
                for j, (qx, qy) in enumerate(chips):
                    block = half(dst.at[2 * qx + qy], rows, c)
                    copy(i, j, block, block, (x, y, c)).wait_recv()
                    fwd = copy(i, 3 + j, block, block, sibling)
                    fwd.start()
                    passed.append(fwd)
            for i, (src, dst, rows) in enumerate(bufs):
                for j, (qx, qy) in enumerate(chips):
                    block = half(dst.at[2 * qx + qy], rows, 1 - c)
                    copy(i, 3 + j, block, block, (x, y, c)).wait_recv()
            for cp in sends + passed:
                cp.wait_send()

        return start, finish

    return _Exchange(prog, shards, [jax.ShapeDtypeStruct((N_CHIPS, *s.shape), s.dtype) for s in shards], 6 * len(shards))


def _gathered(ex):
    p = 2 * lax.axis_index("x") + lax.axis_index("y")
    return [lax.dynamic_update_slice(g, s[None], (p, 0, 0)) for g, s in zip(ex.outs, ex.ins)]


def _rs_to_sibling(bufs):
    def make(srcs, dsts, send_sems, recv_sems):
        x, y, c = _mesh_pos()
        copies = []
        for i, (src, dst) in enumerate(zip(srcs, dsts)):
            half = src.shape[1] // 2
            copies.append(_remote(src.at[:, pl.ds((1 - c) * half, half), :], dst, send_sems, recv_sems, i, (x, y, 1 - c)))
        return copies

    return _copies_exchange(make, bufs, [jax.ShapeDtypeStruct((N_CHIPS, b.shape[1] // 2, b.shape[2]), F32) for b in bufs],
                            len(bufs))


def _rs_pair_add(name, bufs, gots, c, out_dtype):
    n = len(bufs)

    def body(c_ref, *refs):
        for a_ref, b_ref, o_ref in zip(refs[:n], refs[n:2 * n], refs[2 * n:]):
            o_ref[...] = (a_ref[...] + b_ref[...]).astype(out_dtype)

    mine = [pl.BlockSpec((None, g.shape[1], g.shape[2]), lambda j, cr: (j, cr[0], 0)) for g in gots]
    whole = [pl.BlockSpec((None, g.shape[1], g.shape[2]), lambda j, cr: (j, 0, 0)) for g in gots]
    return pl.pallas_call(
        body, name=name,
        grid_spec=pltpu.PrefetchScalarGridSpec(num_scalar_prefetch=1, grid=(N_CHIPS,), in_specs=mine + whole, out_specs=whole),
        out_shape=[jax.ShapeDtypeStruct(g.shape, out_dtype) for g in gots],
        compiler_params=_cparams(("arbitrary",)))(c, *bufs, *gots)


def _rs_to_chips(accs):
    def make(srcs, dsts, send_sems, recv_sems):
        x, y, c = _mesh_pos()
        return [_remote(src.at[2 * qx + qy], dst.at[k], send_sems, recv_sems, 3 * i + k, (qx, qy, c))
                for i, (src, dst) in enumerate(zip(srcs, dsts)) for k, (qx, qy) in enumerate(_other_chips(x, y))]

    return _copies_exchange(make, accs, [jax.ShapeDtypeStruct((3, a.shape[1], a.shape[2]), a.dtype) for a in accs],
                            3 * len(accs))


def _rs_chip_add(name, accs, gots, p):
    n = len(accs)
    slot = (0, 1, 0, 2)

    def body(p_ref, *refs):
        me = p_ref[0]
        for own_ref, got_ref, o_ref in zip(refs[:n], refs[n:2 * n], refs[2 * n:]):
            total = None
            for chip in range(N_CHIPS):
                val = own_ref[...].astype(F32)
                for e in (1, 2, 3):
                    val = jnp.where((chip ^ me) == e, got_ref[slot[e]].astype(F32), val)
                total = val if total is None else total + val
            o_ref[...] = total

    own = [pl.BlockSpec((None, a.shape[1], a.shape[2]), lambda i, pr: (pr[0], 0, 0)) for a in accs]
    got = [pl.BlockSpec(g.shape, lambda i, pr: (0, 0, 0)) for g in gots]
    out = [pl.BlockSpec((a.shape[1], a.shape[2]), lambda i, pr: (0, 0)) for a in accs]
    return pl.pallas_call(
        body, name=name,
        grid_spec=pltpu.PrefetchScalarGridSpec(num_scalar_prefetch=1, grid=(1,), in_specs=own + got, out_specs=out),
        out_shape=[jax.ShapeDtypeStruct((a.shape[1], a.shape[2]), F32) for a in accs],
        compiler_params=_cparams(("arbitrary",)))(p, *accs, *gots)


def _rs_share(ress):
    def make(srcs, dsts, send_sems, recv_sems):
        x, y, c = _mesh_pos()
        return [_remote(src, dst, send_sems, recv_sems, i, (x, y, 1 - c)) for i, (src, dst) in enumerate(zip(srcs, dsts))]

    return _copies_exchange(make, ress, [jax.ShapeDtypeStruct(r.shape, F32) for r in ress], len(ress))


def _shared(ex):
    south = lax.axis_index("c") == 0
    return [jnp.concatenate([jnp.where(south, r, g), jnp.where(south, g, r)], axis=0) for r, g in zip(ex.ins, ex.outs)]


def _all_to_all_devices(vec):
    def make(srcs, dsts, send_sems, recv_sems):
        x, y, c = _mesh_pos()
        me = 4 * x + 2 * y + c
        copies = []
        for r in range(1, 8):
            px, py, pc = (1 - x if r & 4 else x), (1 - y if r & 2 else y), (1 - c if r & 1 else c)
            copies.append(_remote(srcs[0], dsts[0].at[me], send_sems, recv_sems, r - 1, (px, py, pc)))
        return copies

    return _copies_exchange(make, [vec], [jax.ShapeDtypeStruct((8, *vec.shape), vec.dtype)], 7)


def _sum_devices(stack):
    def body(s_ref, o_ref):
        total = s_ref[0]
        for d in range(1, 8):
            total = total + s_ref[d]
        o_ref[...] = total

    return pl.pallas_call(body, name="sum_devices", out_shape=jax.ShapeDtypeStruct(stack.shape[1:], F32),
                          compiler_params=pltpu.CompilerParams(vmem_limit_bytes=VMEM_LIMIT))(stack)


def _pad_rows(flat, rows):
    return jnp.concatenate([flat, jnp.zeros((rows * LANES - flat.shape[0],), flat.dtype)]).reshape(rows, LANES)


def _unshard(g4, shape, axis):
    a = g4.reshape(N_CHIPS, *shape)
    if axis == 0:
        return a.reshape(N_CHIPS * shape[0], shape[1])
    return jnp.transpose(a, (1, 0, 2)).reshape(shape[0], N_CHIPS * shape[1])


def _shard4(full, shape, axis):
    if axis == 0:
        return full.reshape(N_CHIPS, shape[0] * shape[1])
    a = full.reshape(shape[0], N_CHIPS, shape[1])
    return jnp.transpose(a, (1, 0, 2)).reshape(N_CHIPS, shape[0] * shape[1])


def _pad_axis0(a, rows):
    return jnp.concatenate([a, jnp.zeros((rows - a.shape[0], *a.shape[1:]), a.dtype)], axis=0)


def _pad_axis1(a, rows):
    return jnp.concatenate([a, jnp.zeros((a.shape[0], rows - a.shape[1], *a.shape[2:]), a.dtype)], axis=1)


def _shard_to_strip(name, w):
    _, (shape, axis, rows) = name, {n: (s, ax, r) for n, s, ax, r in BIG}[name]
    w2 = w.reshape(shape).astype(BF16)
    return _pad_axis0(w2.T if axis == 1 else w2, rows)


LOCAL_NAME = dict(w_in="w_in_t", w_q_b="w_q_t", w_kv_b="w_kv_t", w_out="w_out", w_gate="w_gate_t", w_up="w_up_t",
                  w_down="w_down")


WIN_SEGMENTS = ((576, 2112, 0), (2112, 2624, 1536), (0, 256, 2048), (256, 512, 2304), (512, 576, 2560), (2624, 2632, 2688))


def _strips_to_weight(name, g4):
    if name == "w_in":
        return _win_to_pad(g4[:, :IN_SHARD].reshape(IN_COLS, D_MODEL))
    if name == "w_q_b":
        return _qk_to_pad(g4.reshape(HEADS * QK_DIM, LORA))
    return g4.reshape(N_CHIPS * g4.shape[1], g4.shape[2])


def _grad_to_strips(name, g):
    if name == "w_in":
        strips = []
        for q in range(N_CHIPS):
            pieces = []
            for a, b, local in sorted(WIN_SEGMENTS):
                s, e = max(a, q * IN_SHARD), min(b, (q + 1) * IN_SHARD)
                if s < e:
                    pieces.append(g[local + s - a:local + e - a])
            pieces.append(jnp.zeros((IN_SHARD_P - IN_SHARD, D_MODEL), g.dtype))
            strips.append(jnp.concatenate(pieces, axis=0))
        return jnp.stack(strips)
    if name == "w_q_b":
        return _qk_from_pad(g).reshape(N_CHIPS, QK_DIM, LORA)
    return g.reshape(N_CHIPS, g.shape[0] // N_CHIPS, g.shape[1])


class _MeshPlan:
    LATE = dict(attn_fwd=("w_up",), dn_chunk_fwd=("w_out", "w_gate"), dn_rec_fwd=("w_down",))
    GROUP_A = ("w_down", "w_gate", "w_up", "w_out")
    GROUP_B = ("w_in", "w_q_b", "w_kv_b")

    def __init__(self, w):
        x, y, c = _mesh_pos()
        self.ci = jnp.reshape(c, (1,)).astype(jnp.int32)
        self.pi = jnp.reshape(2 * x + y, (1,)).astype(jnp.int32)
        self.strip = {n: _shard_to_strip(n, w[n]) for n, _, _, _ in BIG}
        self.gathers, self.weights, self.g, self.acc, self.reduced = {}, {}, {}, {}, {}
        self.sib = self.chip = self.share = None

    def gather_first(self, small):
        names = ("w_in", "w_q_b", "w_kv_b")
        ex = _all_gather([self.strip[n] for n in names] + [small])
        ex.run("all_gather_first")
        got = _gathered(ex)
        for n, g4 in zip(names, got):
            self.weights[LOCAL_NAME[n]] = _strips_to_weight(n, g4)
        return got[-1]

    def weight(self, local_name):
        if local_name not in self.weights:
            for names, ex in self.gathers.values():
                for n, g4 in zip(names, _gathered(ex)):
                    self.weights[LOCAL_NAME[n]] = _strips_to_weight(n, g4)
            self.gathers = {}
        return self.weights[local_name]

    def grad(self, local_name, value):
        name = {v: k for k, v in LOCAL_NAME.items()}[local_name]
        self.g[name] = _grad_to_strips(name, value)

    def _pair_add(self, names, sib):
        accs = _rs_pair_add("rs_pair_add_" + names[0], [self.g[n] for n in names], sib.outs, self.ci, BF16)
        self.acc.update(zip(names, accs))

    def _chip_add(self, names, chip):
        return _rs_chip_add("rs_chip_add_" + names[0], [self.acc[n] for n in names], chip.outs, self.pi)

    def _take_shared(self, names, share):
        for n, strip in zip(names, _shared(share)):
            self.reduced[n] = strip

    def host(self, point):
        a, b = self.GROUP_A, self.GROUP_B
        if point in self.LATE:
            names = self.LATE[point]
            ex = _all_gather([self.strip[n] for n in names])
            self.gathers[point] = (names, ex)
            return ex
        if point == "dn_rec_bwd":
            self.sib = _rs_to_sibling([self.g[n] for n in a])
            return self.sib
        if point == "dn_chunk_bwd":
            self._pair_add(a, self.sib)
            self.chip1 = _rs_to_chips([self.acc[n] for n in a[:2]])
            return self.chip1
        if point == "attn_bwd":
            self.chip2 = _rs_to_chips([self.acc[n] for n in a[2:]])
            return self.chip2
        if point == "mla_prep_bwd":
            ress = self._chip_add(a[:2], self.chip1) + self._chip_add(a[2:], self.chip2)
            self.share = _rs_share(ress)
            return self.share
        if point == "in_dx":
            self._take_shared(a, self.share)
            self.sib = _rs_to_sibling([self.g[n] for n in b])
            return self.sib
        if point == "attn_norm_bwd":
            self._pair_add(b, self.sib)
            self.chip = _rs_to_chips([self.acc[n] for n in b])
            return self.chip
        return None

    def last_share(self):
        self.share = _rs_share(self._chip_add(self.GROUP_B, self.chip))
        return self.share

    def finish(self):
        self._take_shared(self.GROUP_B, self.share)
        return self.reduced


def _strip_to_shard(name, strip):
    shape, axis = {n: (s, ax) for n, s, ax, _ in BIG}[name]
    rows = shape[axis]
    return strip[:rows].T if axis == 1 else strip[:rows]


def kernel(x, meta_tokens, attn_norm_w, w_in, q_a_norm_w, w_q_b, kv_a_norm_w, w_kv_b, q_norm_w, k_norm_w, mla_out_norm_w, dn_conv_w, dn_A_log, dn_dt_bias, dn_out_norm_w, w_out, ffn_norm_w, w_gate, w_up, ffn_conv_w, ffn_conv_b, w_down, loss_target, m_meta_tokens, m_attn_norm_w, m_w_in, m_q_a_norm_w, m_w_q_b, m_kv_a_norm_w, m_w_kv_b, m_q_norm_w, m_k_norm_w, m_mla_out_norm_w, m_dn_conv_w, m_dn_A_log, m_dn_dt_bias, m_dn_out_norm_w, m_w_out, m_ffn_norm_w, m_w_gate, m_w_up, m_ffn_conv_w, m_ffn_conv_b, m_w_down, v_meta_tokens, v_attn_norm_w, v_w_in, v_q_a_norm_w, v_w_q_b, v_kv_a_norm_w, v_w_kv_b, v_q_norm_w, v_k_norm_w, v_mla_out_norm_w, v_dn_conv_w, v_dn_A_log, v_dn_dt_bias, v_dn_out_norm_w, v_w_out, v_ffn_norm_w, v_w_gate, v_w_up, v_ffn_conv_w, v_ffn_conv_b, v_w_down):
    local = dict(locals())
    w = {n: local[n] for n in WEIGHTS}
    m = {n: local["m_" + n] for n in WEIGHTS}
    v = {n: local["v_" + n] for n in WEIGHTS}
    p = 2 * lax.axis_index("x") + lax.axis_index("y")

    plan = _MeshPlan(w)
    wf = _pad_rows(jnp.concatenate([w[n].reshape(-1) for n, _, _ in SMALL_SHARDED]), SMALL_ROWS)
    gf = plan.gather_first(wf).reshape(N_CHIPS, -1)
    full = dict(plan.weights)
    off = 0
    for n, s, ax in SMALL_SHARDED:
        full[n] = _unshard(gf[:, off:off + s[0] * s[1]], s, ax)
        off += s[0] * s[1]
    for n, _ in REPLICATED:
        full[n] = w[n]
    full["ffn_conv_w"] = _ff_to_pad(full["ffn_conv_w"], 1)
    full["ffn_conv_b"] = _ff_to_pad(full["ffn_conv_b"], 1)

    sq, grad_x, g = _local_step(x[0], loss_target[0], full, plan)
    g["ffn_conv_w"] = _ff_from_pad(g["ffn_conv_w"], 1)
    g["ffn_conv_b"] = _ff_from_pad(g["ffn_conv_b"], 1)

    small_all = [n for n, _, _ in SMALL_SHARDED] + [n for n, _ in REPLICATED]
    vec = jnp.concatenate([g[n].reshape(-1) for n in small_all] + [jnp.reshape(0.5 / D_MODEL * jnp.sum(sq), (1,))])
    vec = _pad_rows(vec, -(-vec.shape[0] // (8 * LANES)) * 8)
    a2a = _all_to_all_devices(vec)

    gs, delta, new_m, new_v = {}, {}, {}, {}
    big = {n: (s, ax) for n, s, ax, _ in BIG}

    def adamw_big(n, strips, host=None):
        s, ax = big[n]
        flip = ax == 1 and s[1] % 8 == 0
        there = (lambda a: a.reshape(s).T) if flip else (lambda a: a.reshape(s))
        back = (lambda a: a.T.reshape(w[n].shape)) if flip else (lambda a: a.reshape(w[n].shape))
        strip = strips[n] if flip or ax == 0 else strips[n][:s[1]].T
        g2, d2, m2, v2 = _adamw_call("adamw_" + n, there(w[n]), strip, there(m[n]), there(v[n]), host=host)
        gs[n], delta[n], new_m[n], new_v[n] = back(g2), back(d2), back(m2), back(v2)

    adamw_big("w_down", plan.reduced, host=a2a)
    adamw_big("w_gate", plan.reduced, host=plan.last_share())
    adamw_big("w_up", plan.reduced)
    adamw_big("w_out", plan.reduced)
    strips = plan.finish()
    for n in plan.GROUP_B:
        adamw_big(n, strips)
    me = 4 * lax.axis_index("x") + 2 * lax.axis_index("y") + lax.axis_index("c")
    red = _sum_devices(lax.dynamic_update_slice(a2a.outs[0], vec[None], (me, 0, 0))).reshape(-1)
    off = 0
    for n in small_all:
        tot = red[off:off + g[n].size].reshape(g[n].shape)
        off += g[n].size
        shard = {sn: (s, ax) for sn, s, ax in SMALL_SHARDED}.get(n)
        if shard is not None:
            tot = lax.dynamic_slice_in_dim(tot, p * shard[0][1], shard[0][1], axis=1)
        gs[n] = tot
    loss = red[off]
    small_names = [n for n, _, _ in SMALL_SHARDED] + [n for n, _ in REPLICATED]
    rows = SMALL_ROWS + REP_ROWS
    pack = lambda d: _pad_rows(jnp.concatenate([d[n].reshape(-1) for n in small_names]), rows)
    _, d2, m2, v2 = _adamw_call("adamw_small", pack(w), pack(gs), pack(m), pack(v))
    off = 0
    for n in small_names:
        cnt = w[n].size
        for dst, src in ((delta, d2), (new_m, m2), (new_v, v2)):
            dst[n] = src.reshape(-1)[off:off + cnt].reshape(w[n].shape)
        off += cnt

    grad_out = [gs[n].reshape(w[n].shape) for n in WEIGHTS]
    return (loss, grad_x[None], *grad_out, *[delta[n] for n in WEIGHTS], *[new_m[n] for n in WEIGHTS],
            *[new_v[n] for n in WEIGHTS])
```

```python
import functools
import math

import jax
import jax.numpy as jnp
import numpy as np
from jax import lax
from jax.experimental import pallas as pl
from jax.experimental.pallas import tpu as pltpu

F32 = jnp.float32
BF16 = jnp.bfloat16
HI = lax.Precision.HIGHEST
MESH = pl.DeviceIdType.MESH

N_META = 16
D_MODEL = 1024
HEADS = 4
HEAD = 128
ROPE = 64
QK_DIM = HEAD + ROPE
QK_PAD = 2 * HEAD
LORA = 256
DN_WIDTH = HEADS * HEAD
CHUNK = 64
D_FF = 2816
N_CHIPS = 4
FF_SHARD = D_FF // N_CHIPS
FF_BLOCK = 768
D_FF_P = N_CHIPS * FF_BLOCK
IN_COLS = 2632
IN_SHARD = IN_COLS // N_CHIPS
IN_SHARD_P = 672
IN_PAD = 2816
NORM_EPS = 1e-6
ROPE_THETA = 10000.0
LANES = 512

ADAM_LR, ADAM_B1, ADAM_B2, ADAM_EPS, ADAM_WD, ADAM_STEP = 0.001, 0.9, 0.999, 1e-08, 0.01, 10

VMEM_LIMIT = 56 * 1024 * 1024

BIG = (("w_in", (1024, 658), 1, IN_SHARD_P), ("w_q_b", (256, 192), 1, 192), ("w_kv_b", (256, 256), 1, 256),
       ("w_out", (256, 1024), 0, 256), ("w_gate", (1024, 704), 1, FF_BLOCK), ("w_up", (1024, 704), 1, FF_BLOCK),
       ("w_down", (704, 1024), 0, FF_BLOCK))
SMALL_SHARDED = (("meta_tokens", (16, 256), 1), ("dn_conv_w", (4, 384), 1), ("ffn_conv_w", (3, 704), 1))
REPLICATED = (("attn_norm_w", 1024), ("q_a_norm_w", 256), ("kv_a_norm_w", 256), ("q_norm_w", 192), ("k_norm_w", 192),
              ("mla_out_norm_w", 128), ("dn_A_log", 4), ("dn_dt_bias", 4), ("dn_out_norm_w", 128), ("ffn_norm_w", 1024),
              ("ffn_conv_b", 2816))
WEIGHTS = ("meta_tokens", "attn_norm_w", "w_in", "q_a_norm_w", "w_q_b", "kv_a_norm_w", "w_kv_b", "q_norm_w", "k_norm_w",
           "mla_out_norm_w", "dn_conv_w", "dn_A_log", "dn_dt_bias", "dn_out_norm_w", "w_out", "ffn_norm_w", "w_gate",
           "w_up", "ffn_conv_w", "ffn_conv_b", "w_down")

SMALL_ROWS = 16
REP_ROWS = 16


def _cparams(sem):
    return pltpu.CompilerParams(dimension_semantics=sem, vmem_limit_bytes=VMEM_LIMIT)


class _Exchange:
    def __init__(self, prog, ins, out_shape, nsem):
        self.prog, self.ins, self.out_shape, self.nsem = prog, list(ins), list(out_shape), nsem
        self.outs = None

    def sems(self):
        return [pltpu.SemaphoreType.DMA((self.nsem,)), pltpu.SemaphoreType.DMA((self.nsem,))]

    def run(self, name):
        any_spec = pl.BlockSpec(memory_space=pl.ANY)
        n = len(self.ins)

        def body(*refs):
            start, finish = self.prog(refs[:n], refs[n:-2], refs[-2], refs[-1])
            start()
            finish()

        self.outs = pl.pallas_call(
            body, name=name, in_specs=[any_spec] * n, out_specs=[any_spec] * len(self.out_shape),
            out_shape=self.out_shape, scratch_shapes=self.sems(),
            compiler_params=pltpu.CompilerParams(has_side_effects=True))(*self.ins)
        return self.outs


def _pcall(body, name, grid, in_specs, out_specs, out_shape, args, sem, scratch_shapes=(), host=None):
    single = not isinstance(out_shape, (list, tuple))
    out_specs, out_shape = ([out_specs], [out_shape]) if single else (list(out_specs), list(out_shape))
    if host is None:
        outs = pl.pallas_call(body, name=name, grid=grid, in_specs=list(in_specs), out_specs=out_specs, out_shape=out_shape,
                              scratch_shapes=list(scratch_shapes), compiler_params=_cparams(sem))(*args)
        return outs[0] if single else outs
    any_spec = pl.BlockSpec(memory_space=pl.ANY)
    n_in, n_out, n_scr, nx_in, nx_out = len(in_specs), len(out_specs), len(scratch_shapes), len(host.ins), len(host.out_shape)

    def hosted(*refs):
        c_in, x_in = refs[:n_in], refs[n_in:n_in + nx_in]
        o0 = n_in + nx_in
        c_out, x_out = refs[o0:o0 + n_out], refs[o0 + n_out:o0 + n_out + nx_out]
        s0 = o0 + n_out + nx_out
        start, finish = host.prog(x_in, x_out, refs[s0 + n_scr], refs[s0 + n_scr + 1])
        first = functools.reduce(jnp.logical_and, [pl.program_id(d) == 0 for d in range(len(grid))])
        last = functools.reduce(jnp.logical_and, [pl.program_id(d) == grid[d] - 1 for d in range(len(grid))])
        pl.when(first)(start)
        body(*c_in, *c_out, *refs[s0:s0 + n_scr])
        pl.when(last)(finish)

    outs = pl.pallas_call(
        hosted, name=name, grid=grid, in_specs=list(in_specs) + [any_spec] * nx_in,
        out_specs=out_specs + [any_spec] * nx_out, out_shape=out_shape + host.out_shape,
        scratch_shapes=list(scratch_shapes) + host.sems(),
        compiler_params=pltpu.CompilerParams(dimension_semantics=sem, vmem_limit_bytes=VMEM_LIMIT, has_side_effects=True))(
            *args, *host.ins)
    host.outs = outs[n_out:]
    return outs[0] if single else outs[:n_out]


NN, NT, TN = ((1,), (0,)), ((1,), (1,)), ((0,), (0,))


def _shift_dims(dims, batch):
    if not batch:
        return (dims, ((), ()))
    return (((dims[0][0] + 1,), (dims[1][0] + 1,)), ((0,), (0,)))


def _make_mm(dims, exact, batch=False):
    def raw(a, b, d):
        dn = _shift_dims(d, batch)
        if exact == "split_lhs":
            ah, bh = a.astype(BF16), b.astype(BF16)
            al = (a - ah.astype(F32)).astype(BF16)
            return lax.dot_general(ah, bh, dn, preferred_element_type=F32) + lax.dot_general(al, bh, dn,
                                                                                              preferred_element_type=F32)
        if exact == "split":
            ah, bh = a.astype(BF16), b.astype(BF16)
            al, bl = (a - ah.astype(F32)).astype(BF16), (b - bh.astype(F32)).astype(BF16)
            dot = lambda p, q: lax.dot_general(p, q, dn, preferred_element_type=F32)
            return dot(ah, bh) + (dot(ah, bl) + dot(al, bh))
        if exact:
            return lax.dot_general(a.astype(F32), b.astype(F32), dn, precision=HI, preferred_element_type=F32)
        return lax.dot_general(a.astype(BF16), b.astype(BF16), dn, preferred_element_type=F32)

    @jax.custom_vjp
    def mm(a, b):
        return raw(a, b, dims)

    def fwd(a, b):
        return raw(a, b, dims), (a, b)

    def bwd(res, g):
        a, b = res
        if dims == NN:
            da, db = raw(g, b, NT), raw(a, g, TN)
        elif dims == NT:
            da, db = raw(g, b, NN), raw(g, a, TN)
        else:
            da, db = raw(b, g, NT), raw(a, g, NN)
        return da.astype(a.dtype), db.astype(b.dtype)

    mm.defvjp(fwd, bwd)
    return mm


_mm = _make_mm(NN, False)
_mm_nt = _make_mm(NT, False)
_mm_tn = _make_mm(TN, False)
_mmx = _make_mm(NN, "split_lhs")
_bmm = _make_mm(NN, False, batch=True)
_bmm_nt = _make_mm(NT, False, batch=True)
_bmm_tn = _make_mm(TN, False, batch=True)
_bmmx = _make_mm(NN, True, batch=True)
_bmms = _make_mm(NN, "split", batch=True)
_bmms_nt = _make_mm(NT, "split", batch=True)
_bmms_tn = _make_mm(TN, "split", batch=True)


@jax.custom_vjp
def _unit_lower_inv(a):
    n = a.shape[-1]
    eye = (lax.broadcasted_iota(jnp.int32, a.shape, 1) == lax.broadcasted_iota(jnp.int32, a.shape, 2)).astype(F32)
    x = -a
    t = eye + x
    for _ in range(max(n.bit_length() - 2, 0)):
        x = _bmms(x, x)
        t = t + _bmms(t, x)
    return t


def _unit_lower_inv_fwd(a):
    t = _unit_lower_inv(a)
    return t, t


def _unit_lower_inv_bwd(t, g):
    return (-_bmms_tn(t, _bmms_nt(g, t)),)


_unit_lower_inv.defvjp(_unit_lower_inv_fwd, _unit_lower_inv_bwd)


def _rms(x, w, n):
    ms = jnp.sum(x * x, axis=-1, keepdims=True) * (1.0 / n)
    return x * lax.rsqrt(ms + NORM_EPS) * w


def _silu(x):
    return x * jax.nn.sigmoid(x)


def _softplus(x):
    return jnp.maximum(x, 0.0) + jnp.log(1.0 + jnp.exp(-jnp.abs(x)))


def _rope(x, cos, sin, perm):
    return x * cos + _mmx(x, perm) * sin


def _mla_prep_fn(rows, consts):
    q_lat, kv_lat, k_pe, cos, sin = rows
    qn = _rms(q_lat, consts["qa_w"], LORA)
    kvn = _rms(kv_lat, consts["kva_w"], LORA)
    outs = []
    for h in range(HEADS):
        q_n = _mm_nt(qn, consts["wq_n"][h])
        q_r = _mm_nt(qn, consts["wq_r"][h])
        rs = lax.rsqrt((jnp.sum(q_n * q_n, -1, keepdims=True) + jnp.sum(q_r * q_r, -1, keepdims=True)) * (1.0 / QK_DIM)
                       + NORM_EPS)
        q_n = q_n * rs * consts["qn_n"]
        q_r = _rope(q_r * rs * consts["qn_r"], cos, sin, consts["perm"])
        k_n = _mm_nt(kvn, consts["wk_n"][h])
        v = _mm_nt(kvn, consts["wv"][h])
        rk = lax.rsqrt((jnp.sum(k_n * k_n, -1, keepdims=True) + jnp.sum(k_pe * k_pe, -1, keepdims=True)) * (1.0 / QK_DIM)
                       + NORM_EPS)
        k_n = k_n * rk * consts["kn_n"]
        k_r = _rope(k_pe * rk * consts["kn_r"], cos, sin, consts["perm"])
        outs += [q_n, q_r, k_n, k_r, v]
    return tuple(outs)


def _attn_fn(q, k, v, row0):
    s = _mm_nt(q, k) * (1.0 / math.sqrt(QK_DIM))
    qpos = row0 + lax.broadcasted_iota(jnp.int32, s.shape, 0)
    kpos = lax.broadcasted_iota(jnp.int32, s.shape, 1)
    s = jnp.where(kpos <= qpos, s, -1e30)
    m = lax.stop_gradient(jnp.max(s, axis=-1, keepdims=True))
    p = jnp.exp(s - m)
    p = p / jnp.sum(p, axis=-1, keepdims=True)
    return _mm(p, v)


def _dn_prep_fn(rows, consts):
    qc, kc, ab = rows
    a_b = _mmx(ab, consts["sel_a"])
    b_b = _mmx(ab, consts["sel_b"])
    beta = jax.nn.sigmoid(b_b)
    g = -jnp.exp(consts["alog"]) * _softplus(a_b + consts["dtb"])
    qs, ks = [], []
    for h in range(HEADS):
        q, k = qc[h], kc[h]
        qs.append(q * lax.rsqrt(jnp.sum(q * q, -1, keepdims=True) + NORM_EPS))
        ks.append(k * lax.rsqrt(jnp.sum(k * k, -1, keepdims=True) + NORM_EPS))
    return tuple(qs), tuple(ks), g, beta


def _dn_chunk_fn(q, k, v, gb, g64, bb):
    nb = q.shape[0]
    ri = lax.broadcasted_iota(jnp.int32, (nb, CHUNK, CHUNK), 1)
    ci = lax.broadcasted_iota(jnp.int32, (nb, CHUNK, CHUNK), 2)
    tri = ri >= ci
    strict = ri > ci
    tril = tri.astype(F32)
    eye = (ri == ci).astype(F32)
    ones = jnp.ones((nb, CHUNK, CHUNK), F32)
    gc = _bmmx(tril, gb)
    gc64 = _bmmx(tril, g64)
    grow = _bmmx(ones, eye * gc64)
    diff = gc64 - grow
    decay = jnp.where(tri, jnp.exp(jnp.where(tri, diff, 0.0)), 0.0)
    kb = k * bb
    vb = v * bb
    a = jnp.where(strict, _bmm_nt(kb, k) * decay, 0.0)
    tinv = _unit_lower_inv(a)
    u = _bmm(tinv, vb)
    w = _bmm(tinv, kb * jnp.exp(gc))
    qs = q * (1.0 / math.sqrt(HEAD))
    qk = _bmm_nt(qs, k) * decay
    qg = qs * jnp.exp(gc)
    glast = jnp.sum(gb, axis=1, keepdims=True)
    kdec = k * jnp.exp(glast - gc)
    n_mat = _bmm_tn(kdec, w)
    b_mat = _bmm_tn(kdec, u)
    q_eff = qg - _bmm(qk, w)
    o_own = _bmm(qk, u)
    return n_mat, b_mat, q_eff, o_own, jnp.exp(glast)


def _dn_rec_fn(s, n_mat, b_mat, eg):
    return s * eg - _mm(n_mat, s) + b_mat


def _dn_o_fn(s, q_eff, o_own):
    return _bmm(q_eff, s) + o_own


def _dn_out_fn(o, z, w):
    return _rms(o, w, HEAD) * _silu(z)


def _row_tile(t):
    return t // 8 if (t // 8) % 16 == 0 else t


def _tile(n, pref, unit):
    best = n
    for cand in range(unit, min(n, pref) + 1, unit):
        if n % cand == 0:
            best = cand
    return best if best <= pref else n


def _rows_call(name, body, rows, consts, outs, accs, r, host=None):
    rows = [a if isinstance(a, tuple) else (a, a.shape[1], 0) for a in rows]
    t = rows[0][0].shape[0]
    zero = lambda nd: (lambda i: (0,) * nd)
    in_specs = [pl.BlockSpec((r, w), functools.partial(lambda i, b: (i, b), b=blk)) for _, w, blk in rows]
    rows = [a for a, _, _ in rows]
    in_specs += [pl.BlockSpec(a.shape, zero(a.ndim)) for a in consts]
    out_shape = [jax.ShapeDtypeStruct((t, w), dt) for w, dt in outs] + [jax.ShapeDtypeStruct(s, F32) for s in accs]
    out_specs = [pl.BlockSpec((r, w), lambda i: (i, 0)) for w, _ in outs] + [pl.BlockSpec(s, zero(len(s))) for s in accs]
    return _pcall(body, name, (t // r,), in_specs, out_specs, out_shape, [*rows, *consts], ("arbitrary",), host=host)


def _accumulate(ref, val):
    @pl.when(pl.program_id(0) == 0)
    def _():
        ref[...] = jnp.zeros(ref.shape, ref.dtype)

    ref[...] += val


def _matmul(name, a, b, dims, out_dtype, res=None, host=None):
    if dims == "nn":
        (m, k), n = a.shape, b.shape[1]
    elif dims == "nt":
        (m, k), n = a.shape, b.shape[0]
    else:
        (k, m), n = a.shape, b.shape[1]
    tm = _tile(m, 1100, 16) if dims != "tn" else _tile(m, 640, 128)
    tn = _tile(n, 1408, 128)
    if dims == "nn":
        a_spec, b_spec, dn = pl.BlockSpec((tm, k), lambda i, j: (i, 0)), pl.BlockSpec((k, tn), lambda i, j: (0, j)), NN
    elif dims == "nt":
        a_spec, b_spec, dn = pl.BlockSpec((tm, k), lambda i, j: (i, 0)), pl.BlockSpec((tn, k), lambda i, j: (j, 0)), NT
    else:
        a_spec, b_spec, dn = pl.BlockSpec((k, tm), lambda i, j: (0, i)), pl.BlockSpec((k, tn), lambda i, j: (0, j)), TN
    o_spec = pl.BlockSpec((tm, tn), lambda i, j: (i, j))

    def body(*refs):
        a_ref, b_ref, o_ref = refs[0], refs[1], refs[-1]
        acc = lax.dot_general(a_ref[...].astype(BF16), b_ref[...].astype(BF16), (dn, ((), ())),
                              preferred_element_type=F32)
        if res is not None:
            acc = acc + refs[2][...]
        o_ref[...] = acc.astype(out_dtype)

    ins = [a, b] + ([res] if res is not None else [])
    specs = [a_spec, b_spec] + ([o_spec] if res is not None else [])
    return _pcall(body, name, (m // tm, n // tn), specs, o_spec, jax.ShapeDtypeStruct((m, n), out_dtype), ins,
                  ("arbitrary", "arbitrary"), host=host)


def _rms_fwd(name, h, w):
    n = h.shape[1]

    def body(h_ref, w_ref, o_ref):
        o_ref[...] = _rms(h_ref[...], w_ref[...], n).astype(BF16)

    return _rows_call(name, body, [h], [w], [(n, BF16)], [], _row_tile(h.shape[0]))[0]


def _rms_bwd(name, h, w, cts, resid, host=None):
    n = h.shape[1]
    nct = len(cts)

    def body(*refs):
        h_ref, ct_refs, r_ref, w_ref = refs[0], refs[1:1 + nct], refs[1 + nct], refs[2 + nct]
        dh_ref, dh16_ref, dw_ref = refs[-3], refs[-2], refs[-1]
        ct = ct_refs[0][...].astype(F32)
        for c in ct_refs[1:]:
            ct = ct + c[...].astype(F32)
        _, vjp = jax.vjp(lambda x, ww: _rms(x, ww, n), h_ref[...], w_ref[...])
        dh, dw = vjp(ct)
        dh = dh + r_ref[...]
        dh_ref[...] = dh
        dh16_ref[...] = dh.astype(BF16)
        _accumulate(dw_ref, dw)

    return _rows_call(name, body, [h, *cts, resid], [w], [(n, F32), (n, BF16)], [(1, n)], _row_tile(h.shape[0]), host=host)


def _mla_consts_from_refs(qa, wq, kva, wkv, qn, kn, perm):
    f = lambda r: r[...].astype(F32)
    return dict(
        qa_w=f(qa), kva_w=f(kva), perm=f(perm),
        wq_n=[wq[h * QK_PAD:h * QK_PAD + HEAD, :].astype(F32) for h in range(HEADS)],
        wq_r=[wq[h * QK_PAD + HEAD:(h + 1) * QK_PAD, :].astype(F32) for h in range(HEADS)],
        wk_n=[wkv[h * QK_PAD:h * QK_PAD + HEAD, :].astype(F32) for h in range(HEADS)],
        wv=[wkv[h * QK_PAD + HEAD:(h + 1) * QK_PAD, :].astype(F32) for h in range(HEADS)],
        qn_n=qn[:, 0:HEAD], qn_r=qn[:, HEAD:QK_PAD], kn_n=kn[:, 0:HEAD], kn_r=kn[:, HEAD:QK_PAD])


def _mla_prep_fwd(q_lat, kv_lat, k_pe, cos, sin, qa, wq, kva, wkv, qn, kn, perm):
    def body(ql, kvl, kp, c, s, qa_r, wq_r, kva_r, wkv_r, qn_r, kn_r, p_r, q_out, k_out, v_out):
        consts = _mla_consts_from_refs(qa_r, wq_r, kva_r, wkv_r, qn_r, kn_r, p_r)
        outs = _mla_prep_fn((ql[...], kvl[...], kp[...], c[...], s[...]), consts)
        for h in range(HEADS):
            q_n, q_r, k_n, k_r, v = outs[5 * h:5 * h + 5]
            q_out[:, h * QK_PAD:h * QK_PAD + HEAD] = q_n.astype(BF16)
            q_out[:, h * QK_PAD + HEAD:(h + 1) * QK_PAD] = q_r.astype(BF16)
            k_out[:, h * QK_PAD:h * QK_PAD + HEAD] = k_n.astype(BF16)
            k_out[:, h * QK_PAD + HEAD:(h + 1) * QK_PAD] = k_r.astype(BF16)
            v_out[:, h * HEAD:(h + 1) * HEAD] = v.astype(BF16)

    return _rows_call("mla_prep_fwd", body, [q_lat, kv_lat, k_pe, cos, sin], [qa, wq, kva, wkv, qn, kn, perm],
                      [(HEADS * QK_PAD, BF16), (HEADS * QK_PAD, BF16), (DN_WIDTH, BF16)], [], _row_tile(cos.shape[0]))


def _mla_prep_bwd(q_lat, kv_lat, k_pe, cos, sin, dq, dk, dv, qa, wq, kva, wkv, qn, kn, perm, host=None):
    def body(ql, kvl, kp, c, s, dq_r, dk_r, dv_r, qa_r, wq_r, kva_r, wkv_r, qn_r, kn_r, p_r,
             dql, dkvl, dkp, dqa, dwq, dkva, dwkv, dqn, dkn):
        consts = _mla_consts_from_refs(qa_r, wq_r, kva_r, wkv_r, qn_r, kn_r, p_r)
        cc, ss, pm = c[...], s[...], consts.pop("perm")
        _, vjp = jax.vjp(lambda rows, cs: _mla_prep_fn((*rows, cc, ss), dict(cs, perm=pm)), (ql[...], kvl[...], kp[...]),
                         consts)
        cts = []
        for h in range(HEADS):
            cts += [dq_r[:, h * QK_PAD:h * QK_PAD + HEAD], dq_r[:, h * QK_PAD + HEAD:(h + 1) * QK_PAD],
                    dk_r[:, h * QK_PAD:h * QK_PAD + HEAD], dk_r[:, h * QK_PAD + HEAD:(h + 1) * QK_PAD],
                    dv_r[:, h * HEAD:(h + 1) * HEAD]]
        (d_ql, d_kvl, d_kp), dc = vjp(tuple(cts))
        dql[...] = d_ql.astype(BF16)
        dkvl[...] = d_kvl.astype(BF16)
        dkp[...] = d_kp.astype(BF16)
        first = pl.program_id(0) == 0

        def acc(ref, sl, val):
            @pl.when(first)
            def _():
                ref[sl] = val

            @pl.when(jnp.logical_not(first))
            def _():
                ref[sl] += val

        full = (slice(None), slice(None))
        acc(dqa, full, dc["qa_w"])
        acc(dkva, full, dc["kva_w"])
        for h in range(HEADS):
            acc(dwq, (slice(h * QK_PAD, h * QK_PAD + HEAD), slice(None)), dc["wq_n"][h])
            acc(dwq, (slice(h * QK_PAD + HEAD, (h + 1) * QK_PAD), slice(None)), dc["wq_r"][h])
            acc(dwkv, (slice(h * QK_PAD, h * QK_PAD + HEAD), slice(None)), dc["wk_n"][h])
            acc(dwkv, (slice(h * QK_PAD + HEAD, (h + 1) * QK_PAD), slice(None)), dc["wv"][h])
        acc(dqn, (slice(None), slice(0, HEAD)), dc["qn_n"])
        acc(dqn, (slice(None), slice(HEAD, QK_PAD)), dc["qn_r"])
        acc(dkn, (slice(None), slice(0, HEAD)), dc["kn_n"])
        acc(dkn, (slice(None), slice(HEAD, QK_PAD)), dc["kn_r"])

    return _rows_call("mla_prep_bwd", body, [q_lat, kv_lat, k_pe, cos, sin, dq, dk, dv],
                      [qa, wq, kva, wkv, qn, kn, perm],
                      [(LORA, BF16), (LORA, BF16), (HEAD, BF16)],
                      [(1, LORA), wq.shape, (1, LORA), wkv.shape, (1, QK_PAD), (1, QK_PAD)], _row_tile(cos.shape[0]),
                      host=host)


ATTN_Q_ROWS = 256


def _attn_blocks(t):
    return [(r0, min(ATTN_Q_ROWS, t - r0)) for r0 in range(0, t, ATTN_Q_ROWS)]


def _attn_fwd(q, k, v, host=None):
    t = q.shape[0]

    def body(q_ref, k_ref, v_ref, o_ref):
        for r0, rows in _attn_blocks(t):
            ext = r0 + rows
            o_ref[r0:ext, :] = _attn_fn(q_ref[r0:ext, :], k_ref[0:ext, :], v_ref[0:ext, :], r0)

    qk_spec = pl.BlockSpec((t, QK_PAD), lambda h: (0, h))
    v_spec = pl.BlockSpec((t, HEAD), lambda h: (0, h))
    return _pcall(body, "attn_fwd", (HEADS,), [qk_spec, qk_spec, v_spec], v_spec,
                  jax.ShapeDtypeStruct((t, HEADS * HEAD), F32), [q, k, v], ("arbitrary",), host=host)


def _attn_bwd(q, k, v, do, host=None):
    t = q.shape[0]

    def body(q_ref, k_ref, v_ref, do_ref, dq_ref, dk_ref, dv_ref):
        dk_ref[...] = jnp.zeros(dk_ref.shape, F32)
        dv_ref[...] = jnp.zeros(dv_ref.shape, F32)
        for r0, rows in _attn_blocks(t):
            ext = r0 + rows
            _, vjp = jax.vjp(functools.partial(_attn_fn, row0=r0), q_ref[r0:ext, :].astype(F32),
                             k_ref[0:ext, :].astype(F32), v_ref[0:ext, :].astype(F32))
            dq, dk, dv = vjp(do_ref[r0:ext, :])
            dq_ref[r0:ext, :] = dq
            dk_ref[0:ext, :] += dk
            dv_ref[0:ext, :] += dv

    qk_spec = pl.BlockSpec((t, QK_PAD), lambda h: (0, h))
    v_spec = pl.BlockSpec((t, HEAD), lambda h: (0, h))
    return _pcall(body, "attn_bwd", (HEADS,), [qk_spec, qk_spec, v_spec, v_spec], [qk_spec, qk_spec, v_spec],
                  [jax.ShapeDtypeStruct((t, HEADS * QK_PAD), F32), jax.ShapeDtypeStruct((t, HEADS * QK_PAD), F32),
                   jax.ShapeDtypeStruct((t, HEADS * HEAD), F32)], [q, k, v, do], ("arbitrary",), host=host)


def _mix_out_fwd(o_mla, o_dn, z, w_mla, w_dn):
    def body(om_ref, od_ref, z_ref, wm_ref, wd_ref, o_ref):
        for h in range(HEADS):
            sl = slice(h * HEAD, (h + 1) * HEAD)
            o_ref[:, sl] = _rms(om_ref[:, sl], wm_ref[...], HEAD).astype(BF16)
            o_ref[:, DN_WIDTH + h * HEAD:DN_WIDTH + (h + 1) * HEAD] = _dn_out_fn(od_ref[:, sl], z_ref[:, sl],
                                                                                 wd_ref[...]).astype(BF16)

    return _rows_call("mix_out_fwd", body, [o_mla, o_dn, z], [w_mla, w_dn], [(2 * DN_WIDTH, BF16)], [],
                      _row_tile(o_mla.shape[0]))[0]


def _mix_out_bwd(o_mla, o_dn, z, dmixed, w_mla, w_dn, host=None):
    def body(om_ref, od_ref, z_ref, dm_ref, wm_ref, wd_ref, dom_ref, dod_ref, dz_ref, dwm_ref, dwd_ref):
        dwm = dwd = None
        for h in range(HEADS):
            sl = slice(h * HEAD, (h + 1) * HEAD)
            _, vjp = jax.vjp(lambda o, w: _rms(o, w, HEAD), om_ref[:, sl], wm_ref[...])
            do, dw = vjp(dm_ref[:, sl])
            dom_ref[:, sl] = do
            dwm = dw if dwm is None else dwm + dw
            _, vjp = jax.vjp(_dn_out_fn, od_ref[:, sl], z_ref[:, sl], wd_ref[...])
            do, dz, dw = vjp(dm_ref[:, DN_WIDTH + h * HEAD:DN_WIDTH + (h + 1) * HEAD])
            dod_ref[:, sl] = do
            dz_ref[:, sl] = dz.astype(BF16)
            dwd = dw if dwd is None else dwd + dw
        _accumulate(dwm_ref, dwm)
        _accumulate(dwd_ref, dwd)

    return _rows_call("mix_out_bwd", body, [o_mla, o_dn, z, dmixed], [w_mla, w_dn],
                      [(DN_WIDTH, F32), (DN_WIDTH, F32), (DN_WIDTH, BF16)], [(1, HEAD), (1, HEAD)],
                      _row_tile(o_mla.shape[0]), host=host)


def _shift_down(x, s):
    if s == 0:
        return x
    rows = lax.broadcasted_iota(jnp.int32, x.shape, 0)
    return jnp.where(rows >= s, pltpu.roll(x, s, 0), 0.0)


def _shift_up(x, s):
    if s == 0:
        return x
    t = x.shape[0]
    rows = lax.broadcasted_iota(jnp.int32, x.shape, 0)
    return jnp.where(rows < t - s, pltpu.roll(x, t - s, 0), 0.0)


def _col_call(name, body, cols, taps, outs, tap_outs, cw, host=None):
    t, c = cols[0].shape[0], taps[0].shape[1]
    in_specs = [pl.BlockSpec((t, cw), lambda j: (0, j)) for _ in cols]
    in_specs += [pl.BlockSpec((a.shape[0], cw), lambda j: (0, j)) for a in taps]
    out_shape = [jax.ShapeDtypeStruct((t, c), dt) for dt in outs] + [jax.ShapeDtypeStruct((n, c), F32) for n in tap_outs]
    out_specs = [pl.BlockSpec((t, cw), lambda j: (0, j)) for _ in outs]
    out_specs += [pl.BlockSpec((n, cw), lambda j: (0, j)) for n in tap_outs]
    return _pcall(body, name, (c // cw,), in_specs, out_specs, out_shape, [*cols, *taps], ("arbitrary",), host=host)


def _causal_conv(x, w_ref, width):
    acc = w_ref[width - 1:width, :] * x
    for j in range(width - 1):
        acc = acc + w_ref[j:j + 1, :] * _shift_down(x, width - 1 - j)
    return acc


def _causal_conv_bwd(x, dpre, w_ref, dx_ref, dw_ref, width):
    dx = w_ref[width - 1:width, :] * dpre
    dw_ref[width - 1:width, :] = jnp.sum(dpre * x, axis=0, keepdims=True)
    for j in range(width - 1):
        s = width - 1 - j
        dx = dx + w_ref[j:j + 1, :] * _shift_up(dpre, s)
        dw_ref[j:j + 1, :] = jnp.sum(dpre * _shift_down(x, s), axis=0, keepdims=True)
    dx_ref[...] = dx.astype(dx_ref.dtype)


def _dsilu(x):
    sg = jax.nn.sigmoid(x)
    return sg * (1.0 + x * (1.0 - sg))


def _dn_conv_fwd(x, w):
    def body(x_ref, w_ref, y_ref):
        y_ref[...] = _silu(_causal_conv(x_ref[...], w_ref, 4))

    return _col_call("dn_conv_fwd", body, [x], [w], [F32], [], 256)[0]


def _dn_conv_bwd(x, w, dy):
    def body(x_ref, dy_ref, w_ref, dx_ref, dw_ref):
        xv = x_ref[...]
        dpre = dy_ref[...] * _dsilu(_causal_conv(xv, w_ref, 4))
        _causal_conv_bwd(xv, dpre, w_ref, dx_ref, dw_ref, 4)

    return _col_call("dn_conv_bwd", body, [x, dy], [w], [BF16], [4], 256)


def _glu_fwd(gpre, up, w, b, host=None):
    def body(g_ref, u_ref, w_ref, b_ref, a_ref):
        gate = _causal_conv(g_ref[...], w_ref, 3) + b_ref[...]
        a_ref[...] = (_silu(gate) * u_ref[...]).astype(BF16)

    return _col_call("glu_fwd", body, [gpre, up], [w, b], [BF16], [], 256, host=host)[0]


def _glu_bwd(gpre, up, w, b, dact):
    def body(g_ref, u_ref, da_ref, w_ref, b_ref, dg_ref, du_ref, dw_ref, db_ref):
        gv = g_ref[...]
        gate = _causal_conv(gv, w_ref, 3) + b_ref[...]
        da = da_ref[...]
        sg = jax.nn.sigmoid(gate)
        du_ref[...] = (da * (gate * sg)).astype(BF16)
        dgate = da * u_ref[...] * (sg * (1.0 + gate * (1.0 - sg)))
        db_ref[...] = jnp.sum(dgate, axis=0, keepdims=True)
        _causal_conv_bwd(gv, dgate, w_ref, dg_ref, dw_ref, 3)

    return _col_call("glu_bwd", body, [gpre, up, dact], [w, b], [BF16, BF16], [3, 1], 256)


def _dn_prep_consts(sa, sb, al, dt):
    return dict(sel_a=sa[...], sel_b=sb[...], alog=al[...], dtb=dt[...])


def _dn_prep_fwd(conv, ab, sel_a, sel_b, alog, dtb):
    def body(c_ref, ab_ref, sa, sb, al, dt, q_out, k_out, g_out, b_out):
        qc = tuple(c_ref[:, h * HEAD:(h + 1) * HEAD] for h in range(HEADS))
        kc = tuple(c_ref[:, DN_WIDTH + h * HEAD:DN_WIDTH + (h + 1) * HEAD] for h in range(HEADS))
        qs, ks, g, beta = _dn_prep_fn((qc, kc, ab_ref[...]), _dn_prep_consts(sa, sb, al, dt))
        for h in range(HEADS):
            q_out[:, h * HEAD:(h + 1) * HEAD] = qs[h]
            k_out[:, h * HEAD:(h + 1) * HEAD] = ks[h]
        g_out[...] = g
        b_out[...] = beta

    return _rows_call("dn_prep_fwd", body, [conv, ab], [sel_a, sel_b, alog, dtb], [(DN_WIDTH, F32)] * 4, [],
                      _row_tile(conv.shape[0]))


def _dn_prep_bwd(conv, ab, dq, dk, dv, dg, db, sel_a, sel_b, alog, dtb):
    def body(c_ref, ab_ref, dq_r, dk_r, dv_r, dg_r, db_r, sa, sb, al, dt, dc_out, dab_out, dal_out, ddt_out):
        qc = tuple(c_ref[:, h * HEAD:(h + 1) * HEAD] for h in range(HEADS))
        kc = tuple(c_ref[:, DN_WIDTH + h * HEAD:DN_WIDTH + (h + 1) * HEAD] for h in range(HEADS))
        consts = _dn_prep_consts(sa, sb, al, dt)
        sel = dict(sel_a=consts["sel_a"], sel_b=consts["sel_b"])
        _, vjp = jax.vjp(lambda rows, ad: _dn_prep_fn(rows, {**sel, **ad}), (qc, kc, ab_ref[...]),
                         dict(alog=consts["alog"], dtb=consts["dtb"]))
        cq = tuple(dq_r[:, h * HEAD:(h + 1) * HEAD] for h in range(HEADS))
        ck = tuple(dk_r[:, h * HEAD:(h + 1) * HEAD] for h in range(HEADS))
        (dqc, dkc, dab), dad = vjp((cq, ck, dg_r[...], db_r[...]))
        for h in range(HEADS):
            dc_out[:, h * HEAD:(h + 1) * HEAD] = dqc[h]
            dc_out[:, DN_WIDTH + h * HEAD:DN_WIDTH + (h + 1) * HEAD] = dkc[h]
        dc_out[:, 2 * DN_WIDTH:3 * DN_WIDTH] = dv_r[...]
        dab_out[...] = dab.astype(BF16)
        _accumulate(dal_out, dad["alog"])
        _accumulate(ddt_out, dad["dtb"])

    return _rows_call("dn_prep_bwd", body, [conv, ab, dq, dk, dv, dg, db], [sel_a, sel_b, alog, dtb],
                      [(3 * DN_WIDTH, F32), (HEAD, BF16)], [(1, DN_WIDTH), (1, DN_WIDTH)], _row_tile(conv.shape[0]))


def _chunk_batch(t):
    nc = t // CHUNK
    return nc // 2 if nc % 2 == 0 else nc


def _dn_chunk_specs(t, nb):
    rows = nb * CHUNK
    blk = pl.BlockSpec((rows, HEAD), lambda h, b: (b, h))
    vblk = pl.BlockSpec((rows, HEAD), lambda h, b: (b, 2 * HEADS + h))
    mat = pl.BlockSpec((nb, HEAD, HEAD), lambda h, b: (b, h, 0))
    return rows, blk, vblk, mat


def _dn_chunk_fwd(qn, kn, conv, g, beta, host=None):
    t = qn.shape[0]
    nb = _chunk_batch(t)
    rows, blk, vblk, mat = _dn_chunk_specs(t, nb)

    def body(q_ref, k_ref, v_ref, g_ref, b_ref, n_o, b_o, qe_o, oo_o, eg_o):
        r3 = lambda x: x.reshape(nb, CHUNK, x.shape[-1])
        n_mat, b_mat, q_eff, o_own, eg = _dn_chunk_fn(r3(q_ref[...]), r3(k_ref[...]), r3(v_ref[...]), r3(g_ref[...]),
                                                      r3(g_ref[:, 0:CHUNK]), r3(b_ref[...]))
        n_o[...] = n_mat
        b_o[...] = b_mat
        qe_o[...] = q_eff.reshape(rows, HEAD)
        oo_o[...] = o_own.reshape(rows, HEAD)
        eg_o[...] = jnp.broadcast_to(eg, (nb, HEAD, HEAD))

    nc = t // CHUNK
    mats = jax.ShapeDtypeStruct((nc, DN_WIDTH, HEAD), F32)
    rowsd = jax.ShapeDtypeStruct((t, DN_WIDTH), F32)
    return _pcall(body, "dn_chunk_fwd", (HEADS, t // rows), [blk, blk, vblk, blk, blk], [mat, mat, blk, blk, mat],
                  [mats, mats, rowsd, rowsd, mats], [qn, kn, conv, g, beta], ("arbitrary", "arbitrary"), host=host)


def _dn_chunk_bwd(qn, kn, conv, g, beta, sall, gall, dq_eff, do, host=None):
    t = qn.shape[0]
    nb = _chunk_batch(t)
    rows, blk, vblk, mat = _dn_chunk_specs(t, nb)

    def body(q_ref, k_ref, v_ref, g_ref, b_ref, s_ref, ga_ref, dqe_ref, do_ref, dq_o, dk_o, dv_o, dg_o, db_o):
        r3 = lambda x: x.reshape(nb, CHUNK, x.shape[-1])
        _, vjp = jax.vjp(_dn_chunk_fn, r3(q_ref[...]), r3(k_ref[...]), r3(v_ref[...]), r3(g_ref[...]),
                         r3(g_ref[:, 0:CHUNK]), r3(b_ref[...]))
        s, ga = s_ref[...], ga_ref[...]
        d_n = -_bmm_nt(ga, s)
        d_eg = jnp.sum(ga * s, axis=1, keepdims=True)
        dq, dk, dv, dg, dg64, db = vjp((d_n, ga, r3(dqe_ref[...]), r3(do_ref[...]), d_eg))
        for o_ref, val in zip((dq_o, dk_o, dv_o, dg_o, db_o), (dq, dk, dv, dg, db)):
            o_ref[...] = val.reshape(rows, HEAD)
        dg_o[:, 0:CHUNK] += dg64.reshape(rows, CHUNK)

    return _pcall(body, "dn_chunk_bwd", (HEADS, t // rows), [blk, blk, vblk, blk, blk, mat, mat, blk, blk], [blk] * 5,
                  [jax.ShapeDtypeStruct((t, DN_WIDTH), F32)] * 5, [qn, kn, conv, g, beta, sall, gall, dq_eff, do],
                  ("arbitrary", "arbitrary"), host=host)


def _dn_rec_fwd(n_mat, b_mat, eg, host=None):
    nc = n_mat.shape[0]
    spec = pl.BlockSpec((None, DN_WIDTH, HEAD), lambda c: (c, 0, 0))

    def body(n_ref, b_ref, eg_ref, sall_ref, s_scr):
        @pl.when(pl.program_id(0) == 0)
        def _():
            s_scr[...] = jnp.zeros(s_scr.shape, F32)

        sall_ref[...] = s_scr[...]
        for h in range(HEADS):
            sl = slice(h * HEAD, (h + 1) * HEAD)
            s_scr[sl, :] = _dn_rec_fn(s_scr[sl, :], n_ref[sl, :], b_ref[sl, :], eg_ref[h * HEAD:h * HEAD + 1, :])

    return _pcall(body, "dn_rec_fwd", (nc,), [spec] * 3, spec, jax.ShapeDtypeStruct((nc, DN_WIDTH, HEAD), F32),
                  [n_mat, b_mat, eg], ("arbitrary",), scratch_shapes=[pltpu.VMEM((DN_WIDTH, HEAD), F32)], host=host)


def _dn_rec_bwd(n_mat, eg, ds_out, host=None):
    nc = n_mat.shape[0]
    spec = pl.BlockSpec((None, DN_WIDTH, HEAD), lambda c: (nc - 1 - c, 0, 0))

    def body(n_ref, eg_ref, dso_ref, gall_ref, g_scr):
        @pl.when(pl.program_id(0) == 0)
        def _():
            g_scr[...] = jnp.zeros(g_scr.shape, F32)

        gall_ref[...] = g_scr[...]
        for h in range(HEADS):
            sl = slice(h * HEAD, (h + 1) * HEAD)
            gv = g_scr[sl, :]
            g_scr[sl, :] = gv * eg_ref[h * HEAD:h * HEAD + 1, :] - _mm_tn(n_ref[sl, :], gv) + dso_ref[sl, :]

    return _pcall(body, "dn_rec_bwd", (nc,), [spec] * 3, spec, jax.ShapeDtypeStruct((nc, DN_WIDTH, HEAD), F32),
                  [n_mat, eg, ds_out], ("arbitrary",), scratch_shapes=[pltpu.VMEM((DN_WIDTH, HEAD), F32)], host=host)


def _dn_o_fwd(sall, q_eff, o_own):
    t = q_eff.shape[0]
    nb = _chunk_batch(t)
    rows, blk, _, mat = _dn_chunk_specs(t, nb)

    def body(s_ref, qe_ref, oo_ref, o_ref):
        r3 = lambda x: x.reshape(nb, CHUNK, HEAD)
        o_ref[...] = _dn_o_fn(s_ref[...], r3(qe_ref[...]), r3(oo_ref[...])).reshape(rows, HEAD)

    return _pcall(body, "dn_o_fwd", (HEADS, t // rows), [mat, blk, blk], blk, jax.ShapeDtypeStruct((t, DN_WIDTH), F32),
                  [sall, q_eff, o_own], ("arbitrary", "arbitrary"))


def _dn_o_bwd(sall, q_eff, do, host=None):
    t = q_eff.shape[0]
    nb = _chunk_batch(t)
    rows, blk, _, mat = _dn_chunk_specs(t, nb)

    def body(s_ref, qe_ref, do_ref, dqe_ref, ds_ref):
        r3 = lambda x: x.reshape(nb, CHUNK, HEAD)
        dov = r3(do_ref[...])
        dqe_ref[...] = _bmm_nt(dov, s_ref[...]).reshape(rows, HEAD)
        ds_ref[...] = _bmm_tn(r3(qe_ref[...]), dov)

    nc = t // CHUNK
    return _pcall(body, "dn_o_bwd", (HEADS, t // rows), [mat, blk, blk], [blk, mat],
                  [jax.ShapeDtypeStruct((t, DN_WIDTH), F32), jax.ShapeDtypeStruct((nc, DN_WIDTH, HEAD), F32)],
                  [sall, q_eff, do], ("arbitrary", "arbitrary"), host=host)


def _loss_call(h2, tgt, n_valid):
    t, n = h2.shape
    r = _row_tile(t)

    def body(h_ref, t_ref, dy_ref, dy16_ref, acc_ref):
        rows = pl.program_id(0) * r + lax.broadcasted_iota(jnp.int32, (r, n), 0)
        valid = jnp.logical_and(rows >= N_META, rows < n_valid)
        e = jnp.where(valid, h_ref[...] - t_ref[...], 0.0)
        dy = e * (1.0 / n)
        dy_ref[...] = dy
        dy16_ref[...] = dy.astype(BF16)
        _accumulate(acc_ref, jnp.sum(e * e, axis=0, keepdims=True))

    return _rows_call("loss", body, [h2, tgt], [], [(n, F32), (n, BF16)], [(1, n)], r)


def _adamw_call(name, w, g, m, v, host=None):
    rows, cols = w.shape
    by_rows = rows % 8 == 0

    def body(w_ref, g_ref, m_ref, v_ref, g_out, d_ref, m_out, v_out):
        gv = g_ref[...] if by_rows else g_ref[0:rows, :]
        m2 = ADAM_B1 * m_ref[...] + (1.0 - ADAM_B1) * gv
        v2 = ADAM_B2 * v_ref[...] + (1.0 - ADAM_B2) * (gv * gv)
        m_hat = m2 / (1.0 - ADAM_B1 ** ADAM_STEP)
        v_hat = v2 / (1.0 - ADAM_B2 ** ADAM_STEP)
        g_out[...] = gv
        d_ref[...] = -ADAM_LR * (m_hat / (jnp.sqrt(v_hat) + ADAM_EPS) + ADAM_WD * w_ref[...])
        m_out[...] = m2
        v_out[...] = v2

    if by_rows:
        tr = _tile(rows, 256, 8)
        spec = g_spec = pl.BlockSpec((tr, cols), lambda i: (i, 0))
        grid = (rows // tr,)
    else:
        tc = _tile(cols, 256, 128)
        spec = pl.BlockSpec((rows, tc), lambda j: (0, j))
        g_spec = pl.BlockSpec((g.shape[0], tc), lambda j: (0, j))
        grid = (cols // tc,)
    return _pcall(body, name, grid, [spec, g_spec, spec, spec], [spec] * 4, [jax.ShapeDtypeStruct((rows, cols), F32)] * 4,
                  [w, g, m, v], ("arbitrary",), host=host)


def _rope_tables(t):
    half = ROPE // 2
    inv_freq = np.float32(ROPE_THETA) ** (-np.arange(half, dtype=np.float32) / np.float32(half))
    ang = np.arange(t, dtype=np.float32)[:, None] * inv_freq[None, :].astype(np.float32)
    z = np.zeros((t, HEAD - ROPE), np.float32)
    cos = np.concatenate([np.cos(ang), np.cos(ang), z], axis=1).astype(np.float32)
    sin = np.concatenate([np.sin(ang), np.sin(ang), z], axis=1).astype(np.float32)
    k = np.arange(HEAD)[:, None]
    l = np.arange(HEAD)[None, :]
    perm = np.where((l < half) & (k == l + half), -1.0, 0.0) + np.where((l >= half) & (l < ROPE) & (k == l - half), 1.0, 0.0)
    return jnp.asarray(cos), jnp.asarray(sin), jnp.asarray(perm.astype(np.float32))


def _win_to_pad(w):
    z = lambda n: jnp.zeros((n, w.shape[1]), w.dtype)
    return jnp.concatenate([w[576:2112], w[2112:2624], w[0:256], w[256:512], w[512:576], z(64), w[2624:2632], z(120)],
                           axis=0)


def _win_from_pad(g):
    return jnp.concatenate([g[2048:2304], g[2304:2560], g[2560:2624], g[0:1536], g[1536:2048], g[2688:2696]], axis=0)


def _qk_to_pad(w):
    w4 = w.reshape(HEADS, QK_DIM, w.shape[-1])
    return jnp.concatenate([w4, jnp.zeros((HEADS, QK_PAD - QK_DIM, w.shape[-1]), w.dtype)], axis=1).reshape(
        HEADS * QK_PAD, w.shape[-1])


def _qk_from_pad(g):
    return g.reshape(HEADS, QK_PAD, g.shape[-1])[:, :QK_DIM].reshape(HEADS * QK_DIM, g.shape[-1])


def _ff_to_pad(a, axis):
    shape = list(a.shape)
    shape[axis:axis + 1] = [N_CHIPS, FF_SHARD]
    a4 = a.reshape(shape)
    shape[axis + 1] = FF_BLOCK - FF_SHARD
    out = jnp.concatenate([a4, jnp.zeros(shape, a.dtype)], axis=axis + 1)
    shape[axis:axis + 2] = [D_FF_P]
    return out.reshape(shape)


def _ff_from_pad(a, axis):
    shape = list(a.shape)
    shape[axis:axis + 1] = [N_CHIPS, FF_BLOCK]
    a4 = lax.slice_in_dim(a.reshape(shape), 0, FF_SHARD, axis=axis + 1)
    shape[axis:axis + 2] = [D_FF]
    return a4.reshape(shape)


class _LocalPlan:
    def __init__(self, wt):
        self.wt, self.grads = wt, {}

    def weight(self, name):
        return self.wt[name]

    def host(self, point):
        return None

    def grad(self, name, value):
        self.grads[name] = value


def _local_step(x, tgt, wt, plan=None):
    plan = _LocalPlan(wt) if plan is None else plan
    s = x.shape[0]
    n_valid = N_META + s
    t = -(-n_valid // HEAD) * HEAD
    zpad = jnp.zeros((t - n_valid, D_MODEL), F32)
    h0 = jnp.concatenate([wt["meta_tokens"], x, zpad], axis=0)
    tgt_p = jnp.concatenate([jnp.zeros((N_META, D_MODEL), F32), tgt, zpad], axis=0)
    cos, sin, perm = _rope_tables(t)
    win, wq, wkv = wt["w_in_t"], wt["w_q_t"], wt["w_kv_t"]
    qn_w = jnp.concatenate([wt["q_norm_w"], jnp.zeros((1, QK_PAD - QK_DIM), F32)], axis=1)
    kn_w = jnp.concatenate([wt["k_norm_w"], jnp.zeros((1, QK_PAD - QK_DIM), F32)], axis=1)
    head_id = jnp.arange(DN_WIDTH)[None, :] // HEAD
    lane = jnp.arange(HEAD)[:, None]
    sel_a = (lane == head_id).astype(F32)
    sel_b = (lane == head_id + HEADS).astype(F32)
    alog = jnp.repeat(wt["dn_A_log"], HEAD, axis=1)
    dtb = jnp.repeat(wt["dn_dt_bias"], HEAD, axis=1)
    conv_w, conv_b = wt["ffn_conv_w"], wt["ffn_conv_b"]

    u = _rms_fwd("attn_norm_fwd", h0, wt["attn_norm_w"])
    proj = _matmul("in_proj", u, win, "nt", F32)
    z = (proj, DN_WIDTH, 3)
    q_lat, kv_lat, k_pe, ab = (proj, LORA, 8), (proj, LORA, 9), (proj, HEAD, 20), (proj, HEAD, 21)
    mla_consts = (wt["q_a_norm_w"], wq, wt["kv_a_norm_w"], wkv, qn_w, kn_w, perm)
    q, k, v = _mla_prep_fwd(q_lat, kv_lat, k_pe, cos, sin, *mla_consts)
    o_mla = _attn_fwd(q, k, v, host=plan.host("attn_fwd"))
    conv = _dn_conv_fwd(proj, wt["dn_conv_w"])
    dn_consts = (sel_a, sel_b, alog, dtb)
    qn, kn, g, beta = _dn_prep_fwd(conv, ab, *dn_consts)
    n_mat, b_mat, q_eff, o_own, eg = _dn_chunk_fwd(qn, kn, conv, g, beta, host=plan.host("dn_chunk_fwd"))
    sall = _dn_rec_fwd(n_mat, b_mat, eg)
    o_dn = _dn_o_fwd(sall, q_eff, o_own)
    mixed = _mix_out_fwd(o_mla, o_dn, z, wt["mla_out_norm_w"], wt["dn_out_norm_w"])
    w_out = plan.weight("w_out")
    h1 = _matmul("out_proj", mixed, w_out, "nn", F32, res=h0)
    n2 = _rms_fwd("ffn_norm_fwd", h1, wt["ffn_norm_w"])
    w_gate, w_up = plan.weight("w_gate_t"), plan.weight("w_up_t")
    gpre = _matmul("gate_proj", n2, w_gate, "nt", F32)
    up = _matmul("up_proj", n2, w_up, "nt", F32)
    act = _glu_fwd(gpre, up, conv_w, conv_b, host=plan.host("glu_fwd"))
    w_down = plan.weight("w_down")
    h2 = _matmul("down_proj", act, w_down, "nn", F32, res=h1)
    dy, dy16, sq = _loss_call(h2, tgt_p, n_valid)

    grads = {}
    dact = _matmul("down_dx", dy16, w_down, "nt", F32)
    plan.grad("w_down", _matmul("down_dw", act, dy16, "tn", F32))
    dgpre, dup, grads["ffn_conv_w"], grads["ffn_conv_b"] = _glu_bwd(gpre, up, conv_w, conv_b, dact)
    plan.grad("w_gate_t", _matmul("gate_dw", dgpre, n2, "tn", F32))
    plan.grad("w_up_t", _matmul("up_dw", dup, n2, "tn", F32))
    dn2a = _matmul("gate_dx", dgpre, w_gate, "nn", F32, host=plan.host("gate_dx"))
    dn2b = _matmul("up_dx", dup, w_up, "nn", F32, host=plan.host("up_dx"))
    dh1, dh1_16, grads["ffn_norm_w"] = _rms_bwd("ffn_norm_bwd", h1, wt["ffn_norm_w"], [dn2a, dn2b], dy)
    dmixed = _matmul("out_dx", dh1_16, w_out, "nt", F32)
    plan.grad("w_out", _matmul("out_dw", mixed, dh1_16, "tn", F32))
    do_mla, do_dn, dz, grads["mla_out_norm_w"], grads["dn_out_norm_w"] = _mix_out_bwd(
        o_mla, o_dn, z, dmixed, wt["mla_out_norm_w"], wt["dn_out_norm_w"], host=plan.host("mix_out_bwd"))
    dq_eff, ds_out = _dn_o_bwd(sall, q_eff, do_dn)
    gall = _dn_rec_bwd(n_mat, eg, ds_out)
    dqn, dkn, dv_dn, dg, dbeta = _dn_chunk_bwd(qn, kn, conv, g, beta, sall, gall, dq_eff, do_dn,
                                               host=plan.host("dn_chunk_bwd"))
    dconv, dab, dalog, ddtb = _dn_prep_bwd(conv, ab, dqn, dkn, dv_dn, dg, dbeta, *dn_consts)
    grads["dn_A_log"] = jnp.sum(dalog.reshape(HEADS, HEAD), axis=1)[None, :]
    grads["dn_dt_bias"] = jnp.sum(ddtb.reshape(HEADS, HEAD), axis=1)[None, :]
    ddn_pre, grads["dn_conv_w"] = _dn_conv_bwd(proj, wt["dn_conv_w"], dconv)
    dq, dk, dv = _attn_bwd(q, k, v, do_mla, host=plan.host("attn_bwd"))
    dq_lat, dkv_lat, dk_pe, dqa, dwq, dkva, dwkv, dqnw, dknw = _mla_prep_bwd(
        q_lat, kv_lat, k_pe, cos, sin, dq, dk, dv, *mla_consts, host=plan.host("mla_prep_bwd"))
    grads["q_a_norm_w"], grads["kv_a_norm_w"] = dqa, dkva
    plan.grad("w_q_t", dwq)
    plan.grad("w_kv_t", dwkv)
    grads["q_norm_w"], grads["k_norm_w"] = dqnw[:, :QK_DIM], dknw[:, :QK_DIM]
    dproj = jnp.concatenate([ddn_pre, dz, dq_lat, dkv_lat, dk_pe, dab], axis=1)
    plan.grad("w_in_t", _matmul("in_dw", dproj, u, "tn", F32))
    du = _matmul("in_dx", dproj, win, "nn", F32, host=plan.host("in_dx"))
    dh0, _, grads["attn_norm_w"] = _rms_bwd("attn_norm_bwd", h0, wt["attn_norm_w"], [du], dh1,
                                            host=plan.host("attn_norm_bwd"))
    grads["meta_tokens"] = dh0[0:N_META]
    if isinstance(plan, _LocalPlan):
        grads.update(plan.grads)
    return sq, dh0[N_META:n_valid], grads


def _mesh_pos():
    return lax.axis_index("x"), lax.axis_index("y"), lax.axis_index("c")


def _other_chips(x, y):
    return [(1 - x, y), (x, 1 - y), (1 - x, 1 - y)]


def _remote(src, dst, send_sems, recv_sems, k, to):
    return pltpu.make_async_remote_copy(src_ref=src, dst_ref=dst, send_sem=send_sems.at[k], recv_sem=recv_sems.at[k],
                                        device_id=to, device_id_type=MESH)


def _copies_exchange(make, ins, out_shape, nsem):
    def prog(in_refs, out_refs, send_sems, recv_sems):
        copies = make(in_refs, out_refs, send_sems, recv_sems)

        def start():
            for cp in copies:
                cp.start()

        def finish():
            for cp in copies:
                cp.wait()

        return start, finish

    return _Exchange(prog, ins, out_shape, nsem)


def _all_gather(shards):
    def prog(srcs, dsts, send_sems, recv_sems):
        x, y, c = _mesh_pos()
        p = 2 * x + y
        sibling = (x, y, 1 - c)
        chips = _other_chips(x, y)
        bufs = tuple((s, d, s.shape[0] // 2) for s, d in zip(srcs, dsts))

        def half(ref, rows, which):
            return ref.at[pl.ds(which * rows, rows), :]

        def copy(i, k, src, dst, to):
            return _remote(src, dst, send_sems, recv_sems, 6 * i + k, to)

        sends = [copy(i, j, half(src, rows, c), half(dst.at[p], rows, c), (*chip, c))
                 for i, (src, dst, rows) in enumerate(bufs) for j, chip in enumerate(chips)]

        def start():
            for cp in sends:
                cp.start()

        def finish():
            passed = []
            for i, (src, dst, rows) in enumerate(bufs):
                for j, (qx, qy) in enumerate(chips):
                    block = half(dst.at[2 * qx + qy], rows, c)
                    copy(i, j, block, block, (x, y, c)).wait_recv()
                    fwd = copy(i, 3 + j, block, block, sibling)
                    fwd.start()
                    passed.append(fwd)
            for i, (src, dst, rows) in enumerate(bufs):
                for j, (qx, qy) in enumerate(chips):
                    block = half(dst.at[2 * qx + qy], rows, 1 - c)
                    copy(i, 3 + j, block, block, (x, y, c)).wait_recv()
            for cp in sends + passed:
                cp.wait_send()

        return start, finish

    return _Exchange(prog, shards, [jax.ShapeDtypeStruct((N_CHIPS, *s.shape), s.dtype) for s in shards], 6 * len(shards))


def _gathered(ex):
    p = 2 * lax.axis_index("x") + lax.axis_index("y")
    return [lax.dynamic_update_slice(g, s[None], (p, 0, 0)) for g, s in zip(ex.outs, ex.ins)]


def _rs_to_sibling(bufs):
    def make(srcs, dsts, send_sems, recv_sems):
        x, y, c = _mesh_pos()
        copies = []
        for i, (src, dst) in enumerate(zip(srcs, dsts)):
            half = src.shape[1] // 2
            copies.append(_remote(src.at[:, pl.ds((1 - c) * half, half), :], dst, send_sems, recv_sems, i, (x, y, 1 - c)))
        return copies

    return _copies_exchange(make, bufs, [jax.ShapeDtypeStruct((N_CHIPS, b.shape[1] // 2, b.shape[2]), F32) for b in bufs],
                            len(bufs))


def _rs_pair_add(name, bufs, gots, c, out_dtype):
    n = len(bufs)

    def body(c_ref, *refs):
        for a_ref, b_ref, o_ref in zip(refs[:n], refs[n:2 * n], refs[2 * n:]):
            o_ref[...] = (a_ref[...] + b_ref[...]).astype(out_dtype)

    mine = [pl.BlockSpec((None, g.shape[1], g.shape[2]), lambda j, cr: (j, cr[0], 0)) for g in gots]
    whole = [pl.BlockSpec((None, g.shape[1], g.shape[2]), lambda j, cr: (j, 0, 0)) for g in gots]
    return pl.pallas_call(
        body, name=name,
        grid_spec=pltpu.PrefetchScalarGridSpec(num_scalar_prefetch=1, grid=(N_CHIPS,), in_specs=mine + whole, out_specs=whole),
        out_shape=[jax.ShapeDtypeStruct(g.shape, out_dtype) for g in gots],
        compiler_params=_cparams(("arbitrary",)))(c, *bufs, *gots)


def _rs_to_chips(accs):
    def make(srcs, dsts, send_sems, recv_sems):
        x, y, c = _mesh_pos()
        return [_remote(src.at[2 * qx + qy], dst.at[k], send_sems, recv_sems, 3 * i + k, (qx, qy, c))
                for i, (src, dst) in enumerate(zip(srcs, dsts)) for k, (qx, qy) in enumerate(_other_chips(x, y))]

    return _copies_exchange(make, accs, [jax.ShapeDtypeStruct((3, a.shape[1], a.shape[2]), a.dtype) for a in accs],
                            3 * len(accs))


def _rs_chip_add(name, accs, gots, p):
    n = len(accs)
    slot = (0, 1, 0, 2)

    def body(p_ref, *refs):
        me = p_ref[0]
        for own_ref, got_ref, o_ref in zip(refs[:n], refs[n:2 * n], refs[2 * n:]):
            total = None
            for chip in range(N_CHIPS):
                val = own_ref[...].astype(F32)
                for e in (1, 2, 3):
                    val = jnp.where((chip ^ me) == e, got_ref[slot[e]].astype(F32), val)
                total = val if total is None else total + val
            o_ref[...] = total

    own = [pl.BlockSpec((None, a.shape[1], a.shape[2]), lambda i, pr: (pr[0], 0, 0)) for a in accs]
    got = [pl.BlockSpec(g.shape, lambda i, pr: (0, 0, 0)) for g in gots]
    out = [pl.BlockSpec((a.shape[1], a.shape[2]), lambda i, pr: (0, 0)) for a in accs]
    return pl.pallas_call(
        body, name=name,
        grid_spec=pltpu.PrefetchScalarGridSpec(num_scalar_prefetch=1, grid=(1,), in_specs=own + got, out_specs=out),
        out_shape=[jax.ShapeDtypeStruct((a.shape[1], a.shape[2]), F32) for a in accs],
        compiler_params=_cparams(("arbitrary",)))(p, *accs, *gots)


def _rs_share(ress):
    def make(srcs, dsts, send_sems, recv_sems):
        x, y, c = _mesh_pos()
        return [_remote(src, dst, send_sems, recv_sems, i, (x, y, 1 - c)) for i, (src, dst) in enumerate(zip(srcs, dsts))]

    return _copies_exchange(make, ress, [jax.ShapeDtypeStruct(r.shape, F32) for r in ress], len(ress))


def _shared(ex):
    south = lax.axis_index("c") == 0
    return [jnp.concatenate([jnp.where(south, r, g), jnp.where(south, g, r)], axis=0) for r, g in zip(ex.ins, ex.outs)]


def _all_to_all_devices(vec):
    def make(srcs, dsts, send_sems, recv_sems):
        x, y, c = _mesh_pos()
        me = 4 * x + 2 * y + c
        copies = []
        for r in range(1, 8):
            px, py, pc = (1 - x if r & 4 else x), (1 - y if r & 2 else y), (1 - c if r & 1 else c)
            copies.append(_remote(srcs[0], dsts[0].at[me], send_sems, recv_sems, r - 1, (px, py, pc)))
        return copies

    return _copies_exchange(make, [vec], [jax.ShapeDtypeStruct((8, *vec.shape), vec.dtype)], 7)


def _sum_devices(stack):
    def body(s_ref, o_ref):
        total = s_ref[0]
        for d in range(1, 8):
            total = total + s_ref[d]
        o_ref[...] = total

    return pl.pallas_call(body, name="sum_devices", out_shape=jax.ShapeDtypeStruct(stack.shape[1:], F32),
                          compiler_params=pltpu.CompilerParams(vmem_limit_bytes=VMEM_LIMIT))(stack)


def _pad_rows(flat, rows):
    return jnp.concatenate([flat, jnp.zeros((rows * LANES - flat.shape[0],), flat.dtype)]).reshape(rows, LANES)


def _unshard(g4, shape, axis):
    a = g4.reshape(N_CHIPS, *shape)
    if axis == 0:
        return a.reshape(N_CHIPS * shape[0], shape[1])
    return jnp.transpose(a, (1, 0, 2)).reshape(shape[0], N_CHIPS * shape[1])


def _shard4(full, shape, axis):
    if axis == 0:
        return full.reshape(N_CHIPS, shape[0] * shape[1])
    a = full.reshape(shape[0], N_CHIPS, shape[1])
    return jnp.transpose(a, (1, 0, 2)).reshape(N_CHIPS, shape[0] * shape[1])


def _pad_axis0(a, rows):
    return jnp.concatenate([a, jnp.zeros((rows - a.shape[0], *a.shape[1:]), a.dtype)], axis=0)


def _pad_axis1(a, rows):
    return jnp.concatenate([a, jnp.zeros((a.shape[0], rows - a.shape[1], *a.shape[2:]), a.dtype)], axis=1)


def _shard_to_strip(name, w):
    _, (shape, axis, rows) = name, {n: (s, ax, r) for n, s, ax, r in BIG}[name]
    w2 = w.reshape(shape).astype(BF16)
    return _pad_axis0(w2.T if axis == 1 else w2, rows)


LOCAL_NAME = dict(w_in="w_in_t", w_q_b="w_q_t", w_kv_b="w_kv_t", w_out="w_out", w_gate="w_gate_t", w_up="w_up_t",
                  w_down="w_down")


WIN_SEGMENTS = ((576, 2112, 0), (2112, 2624, 1536), (0, 256, 2048), (256, 512, 2304), (512, 576, 2560), (2624, 2632, 2688))


def _strips_to_weight(name, g4):
    if name == "w_in":
        return _win_to_pad(g4[:, :IN_SHARD].reshape(IN_COLS, D_MODEL))
    if name == "w_q_b":
        return _qk_to_pad(g4.reshape(HEADS * QK_DIM, LORA))
    return g4.reshape(N_CHIPS * g4.shape[1], g4.shape[2])


def _grad_to_strips(name, g):
    if name == "w_in":
        strips = []
        for q in range(N_CHIPS):
            pieces = []
            for a, b, local in sorted(WIN_SEGMENTS):
                s, e = max(a, q * IN_SHARD), min(b, (q + 1) * IN_SHARD)
                if s < e:
                    pieces.append(g[local + s - a:local + e - a])
            pieces.append(jnp.zeros((IN_SHARD_P - IN_SHARD, D_MODEL), g.dtype))
            strips.append(jnp.concatenate(pieces, axis=0))
        return jnp.stack(strips)
    if name == "w_q_b":
        return _qk_from_pad(g).reshape(N_CHIPS, QK_DIM, LORA)
    return g.reshape(N_CHIPS, g.shape[0] // N_CHIPS, g.shape[1])


class _MeshPlan:
    LATE = dict(attn_fwd=("w_up",), dn_chunk_fwd=("w_out", "w_gate"), glu_fwd=("w_down",))
    GROUP_A = ("w_down", "w_gate", "w_up", "w_out")
    GROUP_B = ("w_in", "w_q_b", "w_kv_b")

    def __init__(self, w):
        x, y, c = _mesh_pos()
        self.ci = jnp.reshape(c, (1,)).astype(jnp.int32)
        self.pi = jnp.reshape(2 * x + y, (1,)).astype(jnp.int32)
        self.strip = {n: _shard_to_strip(n, w[n]) for n, _, _, _ in BIG}
        self.gathers, self.weights, self.g, self.acc, self.reduced = {}, {}, {}, {}, {}
        self.sibs, self.sib, self.chip, self.share = [], None, None, None

    def gather_first(self, small):
        names = ("w_in", "w_q_b", "w_kv_b")
        ex = _all_gather([self.strip[n] for n in names] + [small])
        ex.run("all_gather_first")
        got = _gathered(ex)
        for n, g4 in zip(names, got):
            self.weights[LOCAL_NAME[n]] = _strips_to_weight(n, g4)
        return got[-1]

    def weight(self, local_name):
        if local_name not in self.weights:
            for point, (names, ex) in list(self.gathers.items()):
                if ex.outs is not None:
                    for n, g4 in zip(names, _gathered(ex)):
                        self.weights[LOCAL_NAME[n]] = _strips_to_weight(n, g4)
                    del self.gathers[point]
        return self.weights[local_name]

    def grad(self, local_name, value):
        name = {v: k for k, v in LOCAL_NAME.items()}[local_name]
        self.g[name] = _grad_to_strips(name, value)

    def _pair_add(self, names, gots):
        accs = _rs_pair_add("rs_pair_add_" + names[0], [self.g[n] for n in names], gots, self.ci, BF16)
        self.acc.update(zip(names, accs))

    def _chip_add(self, names, chip):
        return _rs_chip_add("rs_chip_add_" + names[0], [self.acc[n] for n in names], chip.outs, self.pi)

    def _take_shared(self, names, share):
        for n, strip in zip(names, _shared(share)):
            self.reduced[n] = strip

    def host(self, point):
        a, b = self.GROUP_A, self.GROUP_B
        if point in self.LATE:
            names = self.LATE[point]
            ex = _all_gather([self.strip[n] for n in names])
            self.gathers[point] = (names, ex)
            return ex
        if point in ("gate_dx", "up_dx", "mix_out_bwd"):
            names = dict(gate_dx=a[:2], up_dx=a[2:3], mix_out_bwd=a[3:])[point]
            ex = _rs_to_sibling([self.g[n] for n in names])
            self.sibs.append(ex)
            return ex
        if point == "dn_chunk_bwd":
            self._pair_add(a, [o for ex in self.sibs for o in ex.outs])
            self.chip1 = _rs_to_chips([self.acc[n] for n in a[:2]])
            return self.chip1
        if point == "attn_bwd":
            self.chip2 = _rs_to_chips([self.acc[n] for n in a[2:]])
            return self.chip2
        if point == "mla_prep_bwd":
            ress = self._chip_add(a[:2], self.chip1) + self._chip_add(a[2:], self.chip2)
            self.share = _rs_share(ress)
            return self.share
        if point == "in_dx":
            self._take_shared(a, self.share)
            self.sib = _rs_to_sibling([self.g[n] for n in b])
            return self.sib
        if point == "attn_norm_bwd":
            self._pair_add(b, self.sib.outs)
            self.chip = _rs_to_chips([self.acc[n] for n in b])
            return self.chip
        return None

    def last_share(self):
        self.share = _rs_share(self._chip_add(self.GROUP_B, self.chip))
        return self.share

    def finish(self):
        self._take_shared(self.GROUP_B, self.share)
        return self.reduced


def _strip_to_shard(name, strip):
    shape, axis = {n: (s, ax) for n, s, ax, _ in BIG}[name]
    rows = shape[axis]
    return strip[:rows].T if axis == 1 else strip[:rows]


def kernel(x, meta_tokens, attn_norm_w, w_in, q_a_norm_w, w_q_b, kv_a_norm_w, w_kv_b, q_norm_w, k_norm_w, mla_out_norm_w, dn_conv_w, dn_A_log, dn_dt_bias, dn_out_norm_w, w_out, ffn_norm_w, w_gate, w_up, ffn_conv_w, ffn_conv_b, w_down, loss_target, m_meta_tokens, m_attn_norm_w, m_w_in, m_q_a_norm_w, m_w_q_b, m_kv_a_norm_w, m_w_kv_b, m_q_norm_w, m_k_norm_w, m_mla_out_norm_w, m_dn_conv_w, m_dn_A_log, m_dn_dt_bias, m_dn_out_norm_w, m_w_out, m_ffn_norm_w, m_w_gate, m_w_up, m_ffn_conv_w, m_ffn_conv_b, m_w_down, v_meta_tokens, v_attn_norm_w, v_w_in, v_q_a_norm_w, v_w_q_b, v_kv_a_norm_w, v_w_kv_b, v_q_norm_w, v_k_norm_w, v_mla_out_norm_w, v_dn_conv_w, v_dn_A_log, v_dn_dt_bias, v_dn_out_norm_w, v_w_out, v_ffn_norm_w, v_w_gate, v_w_up, v_ffn_conv_w, v_ffn_conv_b, v_w_down):
    local = dict(locals())
    w = {n: local[n] for n in WEIGHTS}
    m = {n: local["m_" + n] for n in WEIGHTS}
    v = {n: local["v_" + n] for n in WEIGHTS}
    p = 2 * lax.axis_index("x") + lax.axis_index("y")

    plan = _MeshPlan(w)
    wf = _pad_rows(jnp.concatenate([w[n].reshape(-1) for n, _, _ in SMALL_SHARDED]), SMALL_ROWS)
    gf = plan.gather_first(wf).reshape(N_CHIPS, -1)
    full = dict(plan.weights)
    off = 0
    for n, s, ax in SMALL_SHARDED:
        full[n] = _unshard(gf[:, off:off + s[0] * s[1]], s, ax)
        off += s[0] * s[1]
    for n, _ in REPLICATED:
        full[n] = w[n]
    full["ffn_conv_w"] = _ff_to_pad(full["ffn_conv_w"], 1)
    full["ffn_conv_b"] = _ff_to_pad(full["ffn_conv_b"], 1)

    sq, grad_x, g = _local_step(x[0], loss_target[0], full, plan)
    g["ffn_conv_w"] = _ff_from_pad(g["ffn_conv_w"], 1)
    g["ffn_conv_b"] = _ff_from_pad(g["ffn_conv_b"], 1)

    small_all = [n for n, _, _ in SMALL_SHARDED] + [n for n, _ in REPLICATED]
    vec = jnp.concatenate([g[n].reshape(-1) for n in small_all] + [jnp.reshape(0.5 / D_MODEL * jnp.sum(sq), (1,))])
    vec = _pad_rows(vec, -(-vec.shape[0] // (8 * LANES)) * 8)
    a2a = _all_to_all_devices(vec)

    gs, delta, new_m, new_v = {}, {}, {}, {}
    big = {n: (s, ax) for n, s, ax, _ in BIG}

    def adamw_big(n, strips, host=None):
        s, ax = big[n]
        flip = ax == 1 and s[1] % 8 == 0
        there = (lambda a: a.reshape(s).T) if flip else (lambda a: a.reshape(s))
        back = (lambda a: a.T.reshape(w[n].shape)) if flip else (lambda a: a.reshape(w[n].shape))
        strip = strips[n] if flip or ax == 0 else strips[n][:s[1]].T
        g2, d2, m2, v2 = _adamw_call("adamw_" + n, there(w[n]), strip, there(m[n]), there(v[n]), host=host)
        gs[n], delta[n], new_m[n], new_v[n] = back(g2), back(d2), back(m2), back(v2)

    adamw_big("w_down", plan.reduced, host=a2a)
    adamw_big("w_gate", plan.reduced, host=plan.last_share())
    adamw_big("w_up", plan.reduced)
    adamw_big("w_out", plan.reduced)
    strips = plan.finish()
    for n in plan.GROUP_B:
        adamw_big(n, strips)
    me = 4 * lax.axis_index("x") + 2 * lax.axis_index("y") + lax.axis_index("c")
    red = _sum_devices(lax.dynamic_update_slice(a2a.outs[0], vec[None], (me, 0, 0))).reshape(-1)
    off = 0
    for n in small_all:
        tot = red[off:off + g[n].size].reshape(g[n].shape)
        off += g[n].size
        shard = {sn: (s, ax) for sn, s, ax in SMALL_SHARDED}.get(n)
        if shard is not None:
            tot = lax.dynamic_slice_in_dim(tot, p * shard[0][1], shard[0][1], axis=1)
        gs[n] = tot
    loss = red[off]
    small_names = [n for n, _, _ in SMALL_SHARDED] + [n for n, _ in REPLICATED]
    rows = SMALL_ROWS + REP_ROWS
    pack = lambda d: _pad_rows(jnp.concatenate([d[n].reshape(-1) for n in small_names]), rows)
    _, d2, m2, v2 = _adamw_call("adamw_small", pack(w), pack(gs), pack(m), pack(v))
    off = 0
    for n in small_names:
        cnt = w[n].size
        for dst, src in ((delta, d2), (new_m, m2), (new_v, v2)):
            dst[n] = src.reshape(-1)[off:off + cnt].reshape(w[n].shape)
        off += cnt

    grad_out = [gs[n].reshape(w[n].shape) for n in WEIGHTS]
    return (loss, grad_x[None], *grad_out, *[delta[n] for n in WEIGHTS], *[new_m[n] for n in WEIGHTS],
            *[new_v[n] for n in WEIGHTS])
```

```python
import functools
import math

import jax
import jax.numpy as jnp
import numpy as np
from jax import lax
from jax.experimental import pallas as pl
from jax.experimental.pallas import tpu as pltpu

F32 = jnp.float32
BF16 = jnp.bfloat16
HI = lax.Precision.HIGHEST
MESH = pl.DeviceIdType.MESH

N_META = 16
D_MODEL = 1024
HEADS = 4
HEAD = 128
ROPE = 64
QK_DIM = HEAD + ROPE
QK_PAD = 2 * HEAD
LORA = 256
DN_WIDTH = HEADS * HEAD
CHUNK = 64
D_FF = 2816
N_CHIPS = 4
FF_SHARD = D_FF // N_CHIPS
FF_BLOCK = 768
D_FF_P = N_CHIPS * FF_BLOCK
IN_COLS = 2632
IN_SHARD = IN_COLS // N_CHIPS
IN_SHARD_P = 672
IN_PAD = 2816
NORM_EPS = 1e-6
ROPE_THETA = 10000.0
LANES = 512

ADAM_LR, ADAM_B1, ADAM_B2, ADAM_EPS, ADAM_WD, ADAM_STEP = 0.001, 0.9, 0.999, 1e-08, 0.01, 10

VMEM_LIMIT = 56 * 1024 * 1024

BIG = (("w_in", (1024, 658), 1, IN_SHARD_P), ("w_q_b", (256, 192), 1, 192), ("w_kv_b", (256, 256), 1, 256),
       ("w_out", (256, 1024), 0, 256), ("w_gate", (1024, 704), 1, FF_BLOCK), ("w_up", (1024, 704), 1, FF_BLOCK),
       ("w_down", (704, 1024), 0, FF_BLOCK))
SMALL_SHARDED = (("meta_tokens", (16, 256), 1), ("dn_conv_w", (4, 384), 1), ("ffn_conv_w", (3, 704), 1))
REPLICATED = (("attn_norm_w", 1024), ("q_a_norm_w", 256), ("kv_a_norm_w", 256), ("q_norm_w", 192), ("k_norm_w", 192),
              ("mla_out_norm_w", 128), ("dn_A_log", 4), ("dn_dt_bias", 4), ("dn_out_norm_w", 128), ("ffn_norm_w", 1024),
              ("ffn_conv_b", 2816))
WEIGHTS = ("meta_tokens", "attn_norm_w", "w_in", "q_a_norm_w", "w_q_b", "kv_a_norm_w", "w_kv_b", "q_norm_w", "k_norm_w",
           "mla_out_norm_w", "dn_conv_w", "dn_A_log", "dn_dt_bias", "dn_out_norm_w", "w_out", "ffn_norm_w", "w_gate",
           "w_up", "ffn_conv_w", "ffn_conv_b", "w_down")

SMALL_ROWS = 16
REP_ROWS = 16


def _cparams(sem):
    return pltpu.CompilerParams(dimension_semantics=sem, vmem_limit_bytes=VMEM_LIMIT)


class _Exchange:
    def __init__(self, prog, ins, out_shape, nsem):
        self.prog, self.ins, self.out_shape, self.nsem = prog, list(ins), list(out_shape), nsem
        self.outs = None

    def sems(self):
        return [pltpu.SemaphoreType.DMA((self.nsem,)), pltpu.SemaphoreType.DMA((self.nsem,))]

    def run(self, name):
        any_spec = pl.BlockSpec(memory_space=pl.ANY)
        n = len(self.ins)

        def body(*refs):
            start, finish = self.prog(refs[:n], refs[n:-2], refs[-2], refs[-1])
            start()
            finish()

        self.outs = pl.pallas_call(
            body, name=name, in_specs=[any_spec] * n, out_specs=[any_spec] * len(self.out_shape),
            out_shape=self.out_shape, scratch_shapes=self.sems(),
            compiler_params=pltpu.CompilerParams(has_side_effects=True))(*self.ins)
        return self.outs


def _pcall(body, name, grid, in_specs, out_specs, out_shape, args, sem, scratch_shapes=(), host=None):
    single = not isinstance(out_shape, (list, tuple))
    out_specs, out_shape = ([out_specs], [out_shape]) if single else (list(out_specs), list(out_shape))
    if host is None:
        outs = pl.pallas_call(body, name=name, grid=grid, in_specs=list(in_specs), out_specs=out_specs, out_shape=out_shape,
                              scratch_shapes=list(scratch_shapes), compiler_params=_cparams(sem))(*args)
        return outs[0] if single else outs
    any_spec = pl.BlockSpec(memory_space=pl.ANY)
    n_in, n_out, n_scr, nx_in, nx_out = len(in_specs), len(out_specs), len(scratch_shapes), len(host.ins), len(host.out_shape)

    def hosted(*refs):
        c_in, x_in = refs[:n_in], refs[n_in:n_in + nx_in]
        o0 = n_in + nx_in
        c_out, x_out = refs[o0:o0 + n_out], refs[o0 + n_out:o0 + n_out + nx_out]
        s0 = o0 + n_out + nx_out
        start, finish = host.prog(x_in, x_out, refs[s0 + n_scr], refs[s0 + n_scr + 1])
        first = functools.reduce(jnp.logical_and, [pl.program_id(d) == 0 for d in range(len(grid))])
        last = functools.reduce(jnp.logical_and, [pl.program_id(d) == grid[d] - 1 for d in range(len(grid))])
        pl.when(first)(start)
        body(*c_in, *c_out, *refs[s0:s0 + n_scr])
        pl.when(last)(finish)

    outs = pl.pallas_call(
        hosted, name=name, grid=grid, in_specs=list(in_specs) + [any_spec] * nx_in,
        out_specs=out_specs + [any_spec] * nx_out, out_shape=out_shape + host.out_shape,
        scratch_shapes=list(scratch_shapes) + host.sems(),
        compiler_params=pltpu.CompilerParams(dimension_semantics=sem, vmem_limit_bytes=VMEM_LIMIT, has_side_effects=True))(
            *args, *host.ins)
    host.outs = outs[n_out:]
    return outs[0] if single else outs[:n_out]


NN, NT, TN = ((1,), (0,)), ((1,), (1,)), ((0,), (0,))


def _shift_dims(dims, batch):
    if not batch:
        return (dims, ((), ()))
    return (((dims[0][0] + 1,), (dims[1][0] + 1,)), ((0,), (0,)))


def _make_mm(dims, exact, batch=False):
    def raw(a, b, d):
        dn = _shift_dims(d, batch)
        if exact == "split_lhs":
            ah, bh = a.astype(BF16), b.astype(BF16)
            al = (a - ah.astype(F32)).astype(BF16)
            return lax.dot_general(ah, bh, dn, preferred_element_type=F32) + lax.dot_general(al, bh, dn,
                                                                                              preferred_element_type=F32)
        if exact == "split":
            ah, bh = a.astype(BF16), b.astype(BF16)
            al, bl = (a - ah.astype(F32)).astype(BF16), (b - bh.astype(F32)).astype(BF16)
            dot = lambda p, q: lax.dot_general(p, q, dn, preferred_element_type=F32)
            return dot(ah, bh) + (dot(ah, bl) + dot(al, bh))
        if exact:
            return lax.dot_general(a.astype(F32), b.astype(F32), dn, precision=HI, preferred_element_type=F32)
        return lax.dot_general(a.astype(BF16), b.astype(BF16), dn, preferred_element_type=F32)

    @jax.custom_vjp
    def mm(a, b):
        return raw(a, b, dims)

    def fwd(a, b):
        return raw(a, b, dims), (a, b)

    def bwd(res, g):
        a, b = res
        if dims == NN:
            da, db = raw(g, b, NT), raw(a, g, TN)
        elif dims == NT:
            da, db = raw(g, b, NN), raw(g, a, TN)
        else:
            da, db = raw(b, g, NT), raw(a, g, NN)
        return da.astype(a.dtype), db.astype(b.dtype)

    mm.defvjp(fwd, bwd)
    return mm


_mm = _make_mm(NN, False)
_mm_nt = _make_mm(NT, False)
_mm_tn = _make_mm(TN, False)
_mmx = _make_mm(NN, "split_lhs")
_bmm = _make_mm(NN, False, batch=True)
_bmm_nt = _make_mm(NT, False, batch=True)
_bmm_tn = _make_mm(TN, False, batch=True)
_bmmx = _make_mm(NN, True, batch=True)
_bmms = _make_mm(NN, "split", batch=True)
_bmms_nt = _make_mm(NT, "split", batch=True)
_bmms_tn = _make_mm(TN, "split", batch=True)


@jax.custom_vjp
def _unit_lower_inv(a):
    n = a.shape[-1]
    eye = (lax.broadcasted_iota(jnp.int32, a.shape, 1) == lax.broadcasted_iota(jnp.int32, a.shape, 2)).astype(F32)
    x = -a
    t = eye + x
    for _ in range(max(n.bit_length() - 2, 0)):
        x = _bmms(x, x)
        t = t + _bmms(t, x)
    return t


def _unit_lower_inv_fwd(a):
    t = _unit_lower_inv(a)
    return t, t


def _unit_lower_inv_bwd(t, g):
    return (-_bmms_tn(t, _bmms_nt(g, t)),)


_unit_lower_inv.defvjp(_unit_lower_inv_fwd, _unit_lower_inv_bwd)


def _rms(x, w, n):
    ms = jnp.sum(x * x, axis=-1, keepdims=True) * (1.0 / n)
    return x * lax.rsqrt(ms + NORM_EPS) * w


def _silu(x):
    return x * jax.nn.sigmoid(x)


def _softplus(x):
    return jnp.maximum(x, 0.0) + jnp.log(1.0 + jnp.exp(-jnp.abs(x)))


def _rope(x, cos, sin, perm):
    return x * cos + _mmx(x, perm) * sin


def _mla_prep_fn(rows, consts):
    q_lat, kv_lat, k_pe, cos, sin = rows
    qn = _rms(q_lat, consts["qa_w"], LORA)
    kvn = _rms(kv_lat, consts["kva_w"], LORA)
    outs = []
    for h in range(HEADS):
        q_n = _mm_nt(qn, consts["wq_n"][h])
        q_r = _mm_nt(qn, consts["wq_r"][h])
        rs = lax.rsqrt((jnp.sum(q_n * q_n, -1, keepdims=True) + jnp.sum(q_r * q_r, -1, keepdims=True)) * (1.0 / QK_DIM)
                       + NORM_EPS)
        q_n = q_n * rs * consts["qn_n"]
        q_r = _rope(q_r * rs * consts["qn_r"], cos, sin, consts["perm"])
        k_n = _mm_nt(kvn, consts["wk_n"][h])
        v = _mm_nt(kvn, consts["wv"][h])
        rk = lax.rsqrt((jnp.sum(k_n * k_n, -1, keepdims=True) + jnp.sum(k_pe * k_pe, -1, keepdims=True)) * (1.0 / QK_DIM)
                       + NORM_EPS)
        k_n = k_n * rk * consts["kn_n"]
        k_r = _rope(k_pe * rk * consts["kn_r"], cos, sin, consts["perm"])
        outs += [q_n, q_r, k_n, k_r, v]
    return tuple(outs)


def _attn_fn(q, k, v, row0):
    s = _mm_nt(q, k) * (1.0 / math.sqrt(QK_DIM))
    qpos = row0 + lax.broadcasted_iota(jnp.int32, s.shape, 0)
    kpos = lax.broadcasted_iota(jnp.int32, s.shape, 1)
    s = jnp.where(kpos <= qpos, s, -1e30)
    m = lax.stop_gradient(jnp.max(s, axis=-1, keepdims=True))
    p = jnp.exp(s - m)
    p = p / jnp.sum(p, axis=-1, keepdims=True)
    return _mm(p, v)


def _dn_prep_fn(rows, consts):
    qc, kc, ab = rows
    a_b = _mmx(ab, consts["sel_a"])
    b_b = _mmx(ab, consts["sel_b"])
    beta = jax.nn.sigmoid(b_b)
    g = -jnp.exp(consts["alog"]) * _softplus(a_b + consts["dtb"])
    qs, ks = [], []
    for h in range(HEADS):
        q, k = qc[h], kc[h]
        qs.append(q * lax.rsqrt(jnp.sum(q * q, -1, keepdims=True) + NORM_EPS))
        ks.append(k * lax.rsqrt(jnp.sum(k * k, -1, keepdims=True) + NORM_EPS))
    return tuple(qs), tuple(ks), g, beta


def _dn_chunk_fn(q, k, v, gb, g64, bb):
    nb = q.shape[0]
    ri = lax.broadcasted_iota(jnp.int32, (nb, CHUNK, CHUNK), 1)
    ci = lax.broadcasted_iota(jnp.int32, (nb, CHUNK, CHUNK), 2)
    tri = ri >= ci
    strict = ri > ci
    tril = tri.astype(F32)
    eye = (ri == ci).astype(F32)
    ones = jnp.ones((nb, CHUNK, CHUNK), F32)
    gc = _bmmx(tril, gb)
    gc64 = _bmmx(tril, g64)
    grow = _bmmx(ones, eye * gc64)
    diff = gc64 - grow
    decay = jnp.where(tri, jnp.exp(jnp.where(tri, diff, 0.0)), 0.0)
    kb = k * bb
    vb = v * bb
    a = jnp.where(strict, _bmm_nt(kb, k) * decay, 0.0)
    tinv = _unit_lower_inv(a)
    u = _bmm(tinv, vb)
    w = _bmm(tinv, kb * jnp.exp(gc))
    qs = q * (1.0 / math.sqrt(HEAD))
    qk = _bmm_nt(qs, k) * decay
    qg = qs * jnp.exp(gc)
    glast = jnp.sum(gb, axis=1, keepdims=True)
    kdec = k * jnp.exp(glast - gc)
    n_mat = _bmm_tn(kdec, w)
    b_mat = _bmm_tn(kdec, u)
    q_eff = qg - _bmm(qk, w)
    o_own = _bmm(qk, u)
    return n_mat, b_mat, q_eff, o_own, jnp.exp(glast)


def _dn_rec_fn(s, n_mat, b_mat, eg):
    return s * eg - _mm(n_mat, s) + b_mat


def _dn_o_fn(s, q_eff, o_own):
    return _bmm(q_eff, s) + o_own


def _dn_out_fn(o, z, w):
    return _rms(o, w, HEAD) * _silu(z)


def _row_tile(t):
    return t // 8 if (t // 8) % 16 == 0 else t


def _tile(n, pref, unit):
    best = n
    for cand in range(unit, min(n, pref) + 1, unit):
        if n % cand == 0:
            best = cand
    return best if best <= pref else n


def _rows_call(name, body, rows, consts, outs, accs, r, host=None):
    rows = [a if isinstance(a, tuple) else (a, a.shape[1], 0) for a in rows]
    t = rows[0][0].shape[0]
    zero = lambda nd: (lambda i: (0,) * nd)
    in_specs = [pl.BlockSpec((r, w), functools.partial(lambda i, b: (i, b), b=blk)) for _, w, blk in rows]
    rows = [a for a, _, _ in rows]
    in_specs += [pl.BlockSpec(a.shape, zero(a.ndim)) for a in consts]
    out_shape = [jax.ShapeDtypeStruct((t, w), dt) for w, dt in outs] + [jax.ShapeDtypeStruct(s, F32) for s in accs]
    out_specs = [pl.BlockSpec((r, w), lambda i: (i, 0)) for w, _ in outs] + [pl.BlockSpec(s, zero(len(s))) for s in accs]
    return _pcall(body, name, (t // r,), in_specs, out_specs, out_shape, [*rows, *consts], ("arbitrary",), host=host)


def _accumulate(ref, val):
    @pl.when(pl.program_id(0) == 0)
    def _():
        ref[...] = jnp.zeros(ref.shape, ref.dtype)

    ref[...] += val


def _matmul(name, a, b, dims, out_dtype, res=None, host=None):
    if dims == "nn":
        (m, k), n = a.shape, b.shape[1]
    elif dims == "nt":
        (m, k), n = a.shape, b.shape[0]
    else:
        (k, m), n = a.shape, b.shape[1]
    tm = _tile(m, 1100, 16) if dims != "tn" else _tile(m, 640, 128)
    tn = _tile(n, 1408, 128)
    if dims == "nn":
        a_spec, b_spec, dn = pl.BlockSpec((tm, k), lambda i, j: (i, 0)), pl.BlockSpec((k, tn), lambda i, j: (0, j)), NN
    elif dims == "nt":
        a_spec, b_spec, dn = pl.BlockSpec((tm, k), lambda i, j: (i, 0)), pl.BlockSpec((tn, k), lambda i, j: (j, 0)), NT
    else:
        a_spec, b_spec, dn = pl.BlockSpec((k, tm), lambda i, j: (0, i)), pl.BlockSpec((k, tn), lambda i, j: (0, j)), TN
    o_spec = pl.BlockSpec((tm, tn), lambda i, j: (i, j))

    def body(*refs):
        a_ref, b_ref, o_ref = refs[0], refs[1], refs[-1]
        acc = lax.dot_general(a_ref[...].astype(BF16), b_ref[...].astype(BF16), (dn, ((), ())),
                              preferred_element_type=F32)
        if res is not None:
            acc = acc + refs[2][...]
        o_ref[...] = acc.astype(out_dtype)

    ins = [a, b] + ([res] if res is not None else [])
    specs = [a_spec, b_spec] + ([o_spec] if res is not None else [])
    return _pcall(body, name, (m // tm, n // tn), specs, o_spec, jax.ShapeDtypeStruct((m, n), out_dtype), ins,
                  ("arbitrary", "arbitrary"), host=host)


def _rms_fwd(name, h, w):
    n = h.shape[1]

    def body(h_ref, w_ref, o_ref):
        o_ref[...] = _rms(h_ref[...], w_ref[...], n).astype(BF16)

    return _rows_call(name, body, [h], [w], [(n, BF16)], [], _row_tile(h.shape[0]))[0]


def _rms_bwd(name, h, w, cts, resid, host=None):
    n = h.shape[1]
    nct = len(cts)

    def body(*refs):
        h_ref, ct_refs, r_ref, w_ref = refs[0], refs[1:1 + nct], refs[1 + nct], refs[2 + nct]
        dh_ref, dh16_ref, dw_ref = refs[-3], refs[-2], refs[-1]
        ct = ct_refs[0][...].astype(F32)
        for c in ct_refs[1:]:
            ct = ct + c[...].astype(F32)
        _, vjp = jax.vjp(lambda x, ww: _rms(x, ww, n), h_ref[...], w_ref[...])
        dh, dw = vjp(ct)
        dh = dh + r_ref[...]
        dh_ref[...] = dh
        dh16_ref[...] = dh.astype(BF16)
        _accumulate(dw_ref, dw)

    return _rows_call(name, body, [h, *cts, resid], [w], [(n, F32), (n, BF16)], [(1, n)], _row_tile(h.shape[0]), host=host)


def _mla_consts_from_refs(qa, wq, kva, wkv, qn, kn, perm):
    f = lambda r: r[...].astype(F32)
    return dict(
        qa_w=f(qa), kva_w=f(kva), perm=f(perm),
        wq_n=[wq[h * QK_PAD:h * QK_PAD + HEAD, :].astype(F32) for h in range(HEADS)],
        wq_r=[wq[h * QK_PAD + HEAD:(h + 1) * QK_PAD, :].astype(F32) for h in range(HEADS)],
        wk_n=[wkv[h * QK_PAD:h * QK_PAD + HEAD, :].astype(F32) for h in range(HEADS)],
        wv=[wkv[h * QK_PAD + HEAD:(h + 1) * QK_PAD, :].astype(F32) for h in range(HEADS)],
        qn_n=qn[:, 0:HEAD], qn_r=qn[:, HEAD:QK_PAD], kn_n=kn[:, 0:HEAD], kn_r=kn[:, HEAD:QK_PAD])


def _mla_prep_fwd(q_lat, kv_lat, k_pe, cos, sin, qa, wq, kva, wkv, qn, kn, perm):
    def body(ql, kvl, kp, c, s, qa_r, wq_r, kva_r, wkv_r, qn_r, kn_r, p_r, q_out, k_out, v_out):
        consts = _mla_consts_from_refs(qa_r, wq_r, kva_r, wkv_r, qn_r, kn_r, p_r)
        outs = _mla_prep_fn((ql[...], kvl[...], kp[...], c[...], s[...]), consts)
        for h in range(HEADS):
            q_n, q_r, k_n, k_r, v = outs[5 * h:5 * h + 5]
            q_out[:, h * QK_PAD:h * QK_PAD + HEAD] = q_n.astype(BF16)
            q_out[:, h * QK_PAD + HEAD:(h + 1) * QK_PAD] = q_r.astype(BF16)
            k_out[:, h * QK_PAD:h * QK_PAD + HEAD] = k_n.astype(BF16)
            k_out[:, h * QK_PAD + HEAD:(h + 1) * QK_PAD] = k_r.astype(BF16)
            v_out[:, h * HEAD:(h + 1) * HEAD] = v.astype(BF16)

    return _rows_call("mla_prep_fwd", body, [q_lat, kv_lat, k_pe, cos, sin], [qa, wq, kva, wkv, qn, kn, perm],
                      [(HEADS * QK_PAD, BF16), (HEADS * QK_PAD, BF16), (DN_WIDTH, BF16)], [], _row_tile(cos.shape[0]))


def _mla_prep_bwd(q_lat, kv_lat, k_pe, cos, sin, dq, dk, dv, qa, wq, kva, wkv, qn, kn, perm, host=None):
    def body(ql, kvl, kp, c, s, dq_r, dk_r, dv_r, qa_r, wq_r, kva_r, wkv_r, qn_r, kn_r, p_r,
             dql, dkvl, dkp, dqa, dwq, dkva, dwkv, dqn, dkn):
        consts = _mla_consts_from_refs(qa_r, wq_r, kva_r, wkv_r, qn_r, kn_r, p_r)
        cc, ss, pm = c[...], s[...], consts.pop("perm")
        _, vjp = jax.vjp(lambda rows, cs: _mla_prep_fn((*rows, cc, ss), dict(cs, perm=pm)), (ql[...], kvl[...], kp[...]),
                         consts)
        cts = []
        for h in range(HEADS):
            cts += [dq_r[:, h * QK_PAD:h * QK_PAD + HEAD], dq_r[:, h * QK_PAD + HEAD:(h + 1) * QK_PAD],
                    dk_r[:, h * QK_PAD:h * QK_PAD + HEAD], dk_r[:, h * QK_PAD + HEAD:(h + 1) * QK_PAD],
                    dv_r[:, h * HEAD:(h + 1) * HEAD]]
        (d_ql, d_kvl, d_kp), dc = vjp(tuple(cts))
        dql[...] = d_ql.astype(BF16)
        dkvl[...] = d_kvl.astype(BF16)
        dkp[...] = d_kp.astype(BF16)
        first = pl.program_id(0) == 0

        def acc(ref, sl, val):
            @pl.when(first)
            def _():
                ref[sl] = val

            @pl.when(jnp.logical_not(first))
            def _():
                ref[sl] += val

        full = (slice(None), slice(None))
        acc(dqa, full, dc["qa_w"])
        acc(dkva, full, dc["kva_w"])
        for h in range(HEADS):
            acc(dwq, (slice(h * QK_PAD, h * QK_PAD + HEAD), slice(None)), dc["wq_n"][h])
            acc(dwq, (slice(h * QK_PAD + HEAD, (h + 1) * QK_PAD), slice(None)), dc["wq_r"][h])
            acc(dwkv, (slice(h * QK_PAD, h * QK_PAD + HEAD), slice(None)), dc["wk_n"][h])
            acc(dwkv, (slice(h * QK_PAD + HEAD, (h + 1) * QK_PAD), slice(None)), dc["wv"][h])
        acc(dqn, (slice(None), slice(0, HEAD)), dc["qn_n"])
        acc(dqn, (slice(None), slice(HEAD, QK_PAD)), dc["qn_r"])
        acc(dkn, (slice(None), slice(0, HEAD)), dc["kn_n"])
        acc(dkn, (slice(None), slice(HEAD, QK_PAD)), dc["kn_r"])

    return _rows_call("mla_prep_bwd", body, [q_lat, kv_lat, k_pe, cos, sin, dq, dk, dv],
                      [qa, wq, kva, wkv, qn, kn, perm],
                      [(LORA, BF16), (LORA, BF16), (HEAD, BF16)],
                      [(1, LORA), wq.shape, (1, LORA), wkv.shape, (1, QK_PAD), (1, QK_PAD)], _row_tile(cos.shape[0]),
                      host=host)


ATTN_Q_ROWS = 256


def _attn_blocks(t):
    return [(r0, min(ATTN_Q_ROWS, t - r0)) for r0 in range(0, t, ATTN_Q_ROWS)]


def _attn_fwd(q, k, v, host=None):
    t = q.shape[0]

    def body(q_ref, k_ref, v_ref, o_ref):
        for r0, rows in _attn_blocks(t):
            ext = r0 + rows
            o_ref[r0:ext, :] = _attn_fn(q_ref[r0:ext, :], k_ref[0:ext, :], v_ref[0:ext, :], r0)

    qk_spec = pl.BlockSpec((t, QK_PAD), lambda h: (0, h))
    v_spec = pl.BlockSpec((t, HEAD), lambda h: (0, h))
    return _pcall(body, "attn_fwd", (HEADS,), [qk_spec, qk_spec, v_spec], v_spec,
                  jax.ShapeDtypeStruct((t, HEADS * HEAD), F32), [q, k, v], ("arbitrary",), host=host)


def _attn_bwd(q, k, v, do, host=None):
    t = q.shape[0]

    def body(q_ref, k_ref, v_ref, do_ref, dq_ref, dk_ref, dv_ref):
        dk_ref[...] = jnp.zeros(dk_ref.shape, F32)
        dv_ref[...] = jnp.zeros(dv_ref.shape, F32)
        for r0, rows in _attn_blocks(t):
            ext = r0 + rows
            _, vjp = jax.vjp(functools.partial(_attn_fn, row0=r0), q_ref[r0:ext, :].astype(F32),
                             k_ref[0:ext, :].astype(F32), v_ref[0:ext, :].astype(F32))
            dq, dk, dv = vjp(do_ref[r0:ext, :])
            dq_ref[r0:ext, :] = dq
            dk_ref[0:ext, :] += dk
            dv_ref[0:ext, :] += dv

    qk_spec = pl.BlockSpec((t, QK_PAD), lambda h: (0, h))
    v_spec = pl.BlockSpec((t, HEAD), lambda h: (0, h))
    return _pcall(body, "attn_bwd", (HEADS,), [qk_spec, qk_spec, v_spec, v_spec], [qk_spec, qk_spec, v_spec],
                  [jax.ShapeDtypeStruct((t, HEADS * QK_PAD), F32), jax.ShapeDtypeStruct((t, HEADS * QK_PAD), F32),
                   jax.ShapeDtypeStruct((t, HEADS * HEAD), F32)], [q, k, v, do], ("arbitrary",), host=host)


def _mix_out_fwd(o_mla, o_dn, z, w_mla, w_dn):
    def body(om_ref, od_ref, z_ref, wm_ref, wd_ref, o_ref):
        for h in range(HEADS):
            sl = slice(h * HEAD, (h + 1) * HEAD)
            o_ref[:, sl] = _rms(om_ref[:, sl], wm_ref[...], HEAD).astype(BF16)
            o_ref[:, DN_WIDTH + h * HEAD:DN_WIDTH + (h + 1) * HEAD] = _dn_out_fn(od_ref[:, sl], z_ref[:, sl],
                                                                                 wd_ref[...]).astype(BF16)

    return _rows_call("mix_out_fwd", body, [o_mla, o_dn, z], [w_mla, w_dn], [(2 * DN_WIDTH, BF16)], [],
                      _row_tile(o_mla.shape[0]))[0]


def _mix_out_bwd(o_mla, o_dn, z, dmixed, w_mla, w_dn, host=None):
    def body(om_ref, od_ref, z_ref, dm_ref, wm_ref, wd_ref, dom_ref, dod_ref, dz_ref, dwm_ref, dwd_ref):
        dwm = dwd = None
        for h in range(HEADS):
            sl = slice(h * HEAD, (h + 1) * HEAD)
            _, vjp = jax.vjp(lambda o, w: _rms(o, w, HEAD), om_ref[:, sl], wm_ref[...])
            do, dw = vjp(dm_ref[:, sl])
            dom_ref[:, sl] = do
            dwm = dw if dwm is None else dwm + dw
            _, vjp = jax.vjp(_dn_out_fn, od_ref[:, sl], z_ref[:, sl], wd_ref[...])
            do, dz, dw = vjp(dm_ref[:, DN_WIDTH + h * HEAD:DN_WIDTH + (h + 1) * HEAD])
            dod_ref[:, sl] = do
            dz_ref[:, sl] = dz.astype(BF16)
            dwd = dw if dwd is None else dwd + dw
        _accumulate(dwm_ref, dwm)
        _accumulate(dwd_ref, dwd)

    return _rows_call("mix_out_bwd", body, [o_mla, o_dn, z, dmixed], [w_mla, w_dn],
                      [(DN_WIDTH, F32), (DN_WIDTH, F32), (DN_WIDTH, BF16)], [(1, HEAD), (1, HEAD)],
                      _row_tile(o_mla.shape[0]), host=host)


def _shift_down(x, s):
    if s == 0:
        return x
    rows = lax.broadcasted_iota(jnp.int32, x.shape, 0)
    return jnp.where(rows >= s, pltpu.roll(x, s, 0), 0.0)


def _shift_up(x, s):
    if s == 0:
        return x
    t = x.shape[0]
    rows = lax.broadcasted_iota(jnp.int32, x.shape, 0)
    return jnp.where(rows < t - s, pltpu.roll(x, t - s, 0), 0.0)


def _col_call(name, body, cols, taps, outs, tap_outs, cw, host=None):
    t, c = cols[0].shape[0], taps[0].shape[1]
    in_specs = [pl.BlockSpec((t, cw), lambda j: (0, j)) for _ in cols]
    in_specs += [pl.BlockSpec((a.shape[0], cw), lambda j: (0, j)) for a in taps]
    out_shape = [jax.ShapeDtypeStruct((t, c), dt) for dt in outs] + [jax.ShapeDtypeStruct((n, c), F32) for n in tap_outs]
    out_specs = [pl.BlockSpec((t, cw), lambda j: (0, j)) for _ in outs]
    out_specs += [pl.BlockSpec((n, cw), lambda j: (0, j)) for n in tap_outs]
    return _pcall(body, name, (c // cw,), in_specs, out_specs, out_shape, [*cols, *taps], ("arbitrary",), host=host)


def _causal_conv(x, w_ref, width):
    acc = w_ref[width - 1:width, :] * x
    for j in range(width - 1):
        acc = acc + w_ref[j:j + 1, :] * _shift_down(x, width - 1 - j)
    return acc


def _causal_conv_bwd(x, dpre, w_ref, dx_ref, dw_ref, width):
    dx = w_ref[width - 1:width, :] * dpre
    dw_ref[width - 1:width, :] = jnp.sum(dpre * x, axis=0, keepdims=True)
    for j in range(width - 1):
        s = width - 1 - j
        dx = dx + w_ref[j:j + 1, :] * _shift_up(dpre, s)
        dw_ref[j:j + 1, :] = jnp.sum(dpre * _shift_down(x, s), axis=0, keepdims=True)
    dx_ref[...] = dx.astype(dx_ref.dtype)


def _dsilu(x):
    sg = jax.nn.sigmoid(x)
    return sg * (1.0 + x * (1.0 - sg))


def _dn_conv_fwd(x, w):
    def body(x_ref, w_ref, y_ref):
        y_ref[...] = _silu(_causal_conv(x_ref[...], w_ref, 4))

    return _col_call("dn_conv_fwd", body, [x], [w], [F32], [], 256)[0]


def _dn_conv_bwd(x, w, dy):
    def body(x_ref, dy_ref, w_ref, dx_ref, dw_ref):
        xv = x_ref[...]
        dpre = dy_ref[...] * _dsilu(_causal_conv(xv, w_ref, 4))
        _causal_conv_bwd(xv, dpre, w_ref, dx_ref, dw_ref, 4)

    return _col_call("dn_conv_bwd", body, [x, dy], [w], [BF16], [4], 256)


def _glu_fwd(gpre, up, w, b, host=None):
    def body(g_ref, u_ref, w_ref, b_ref, a_ref):
        gate = _causal_conv(g_ref[...], w_ref, 3) + b_ref[...]
        a_ref[...] = (_silu(gate) * u_ref[...]).astype(BF16)

    return _col_call("glu_fwd", body, [gpre, up], [w, b], [BF16], [], 256, host=host)[0]


def _glu_bwd(gpre, up, w, b, dact):
    def body(g_ref, u_ref, da_ref, w_ref, b_ref, dg_ref, du_ref, dw_ref, db_ref):
        gv = g_ref[...]
        gate = _causal_conv(gv, w_ref, 3) + b_ref[...]
        da = da_ref[...]
        sg = jax.nn.sigmoid(gate)
        du_ref[...] = (da * (gate * sg)).astype(BF16)
        dgate = da * u_ref[...] * (sg * (1.0 + gate * (1.0 - sg)))
        db_ref[...] = jnp.sum(dgate, axis=0, keepdims=True)
        _causal_conv_bwd(gv, dgate, w_ref, dg_ref, dw_ref, 3)

    return _col_call("glu_bwd", body, [gpre, up, dact], [w, b], [BF16, BF16], [3, 1], 256)


def _dn_prep_consts(sa, sb, al, dt):
    return dict(sel_a=sa[...], sel_b=sb[...], alog=al[...], dtb=dt[...])


def _dn_prep_fwd(conv, ab, sel_a, sel_b, alog, dtb):
    def body(c_ref, ab_ref, sa, sb, al, dt, q_out, k_out, g_out, b_out):
        qc = tuple(c_ref[:, h * HEAD:(h + 1) * HEAD] for h in range(HEADS))
        kc = tuple(c_ref[:, DN_WIDTH + h * HEAD:DN_WIDTH + (h + 1) * HEAD] for h in range(HEADS))
        qs, ks, g, beta = _dn_prep_fn((qc, kc, ab_ref[...]), _dn_prep_consts(sa, sb, al, dt))
        for h in range(HEADS):
            q_out[:, h * HEAD:(h + 1) * HEAD] = qs[h]
            k_out[:, h * HEAD:(h + 1) * HEAD] = ks[h]
        g_out[...] = g
        b_out[...] = beta

    return _rows_call("dn_prep_fwd", body, [conv, ab], [sel_a, sel_b, alog, dtb], [(DN_WIDTH, F32)] * 4, [],
                      _row_tile(conv.shape[0]))


def _dn_prep_bwd(conv, ab, dq, dk, dv, dg, db, sel_a, sel_b, alog, dtb):
    def body(c_ref, ab_ref, dq_r, dk_r, dv_r, dg_r, db_r, sa, sb, al, dt, dc_out, dab_out, dal_out, ddt_out):
        qc = tuple(c_ref[:, h * HEAD:(h + 1) * HEAD] for h in range(HEADS))
        kc = tuple(c_ref[:, DN_WIDTH + h * HEAD:DN_WIDTH + (h + 1) * HEAD] for h in range(HEADS))
        consts = _dn_prep_consts(sa, sb, al, dt)
        sel = dict(sel_a=consts["sel_a"], sel_b=consts["sel_b"])
        _, vjp = jax.vjp(lambda rows, ad: _dn_prep_fn(rows, {**sel, **ad}), (qc, kc, ab_ref[...]),
                         dict(alog=consts["alog"], dtb=consts["dtb"]))
        cq = tuple(dq_r[:, h * HEAD:(h + 1) * HEAD] for h in range(HEADS))
        ck = tuple(dk_r[:, h * HEAD:(h + 1) * HEAD] for h in range(HEADS))
        (dqc, dkc, dab), dad = vjp((cq, ck, dg_r[...], db_r[...]))
        for h in range(HEADS):
            dc_out[:, h * HEAD:(h + 1) * HEAD] = dqc[h]
            dc_out[:, DN_WIDTH + h * HEAD:DN_WIDTH + (h + 1) * HEAD] = dkc[h]
        dc_out[:, 2 * DN_WIDTH:3 * DN_WIDTH] = dv_r[...]
        dab_out[...] = dab.astype(BF16)
        _accumulate(dal_out, dad["alog"])
        _accumulate(ddt_out, dad["dtb"])

    return _rows_call("dn_prep_bwd", body, [conv, ab, dq, dk, dv, dg, db], [sel_a, sel_b, alog, dtb],
                      [(3 * DN_WIDTH, F32), (HEAD, BF16)], [(1, DN_WIDTH), (1, DN_WIDTH)], _row_tile(conv.shape[0]))


def _chunk_batch(t):
    nc = t // CHUNK
    return nc // 2 if nc % 2 == 0 else nc


def _dn_chunk_specs(t, nb):
    rows = nb * CHUNK
    blk = pl.BlockSpec((rows, HEAD), lambda h, b: (b, h))
    vblk = pl.BlockSpec((rows, HEAD), lambda h, b: (b, 2 * HEADS + h))
    mat = pl.BlockSpec((nb, HEAD, HEAD), lambda h, b: (b, h, 0))
    return rows, blk, vblk, mat


def _dn_chunk_fwd(qn, kn, conv, g, beta, host=None):
    t = qn.shape[0]
    nb = _chunk_batch(t)
    rows, blk, vblk, mat = _dn_chunk_specs(t, nb)

    def body(q_ref, k_ref, v_ref, g_ref, b_ref, n_o, b_o, qe_o, oo_o, eg_o):
        r3 = lambda x: x.reshape(nb, CHUNK, x.shape[-1])
        n_mat, b_mat, q_eff, o_own, eg = _dn_chunk_fn(r3(q_ref[...]), r3(k_ref[...]), r3(v_ref[...]), r3(g_ref[...]),
                                                      r3(g_ref[:, 0:CHUNK]), r3(b_ref[...]))
        n_o[...] = n_mat
        b_o[...] = b_mat
        qe_o[...] = q_eff.reshape(rows, HEAD)
        oo_o[...] = o_own.reshape(rows, HEAD)
        eg_o[...] = jnp.broadcast_to(eg, (nb, HEAD, HEAD))

    nc = t // CHUNK
    mats = jax.ShapeDtypeStruct((nc, DN_WIDTH, HEAD), F32)
    rowsd = jax.ShapeDtypeStruct((t, DN_WIDTH), F32)
    return _pcall(body, "dn_chunk_fwd", (HEADS, t // rows), [blk, blk, vblk, blk, blk], [mat, mat, blk, blk, mat],
                  [mats, mats, rowsd, rowsd, mats], [qn, kn, conv, g, beta], ("arbitrary", "arbitrary"), host=host)


def _dn_chunk_bwd(qn, kn, conv, g, beta, sall, gall, dq_eff, do, host=None):
    t = qn.shape[0]
    nb = _chunk_batch(t)
    rows, blk, vblk, mat = _dn_chunk_specs(t, nb)

    def body(q_ref, k_ref, v_ref, g_ref, b_ref, s_ref, ga_ref, dqe_ref, do_ref, dq_o, dk_o, dv_o, dg_o, db_o):
        r3 = lambda x: x.reshape(nb, CHUNK, x.shape[-1])
        _, vjp = jax.vjp(_dn_chunk_fn, r3(q_ref[...]), r3(k_ref[...]), r3(v_ref[...]), r3(g_ref[...]),
                         r3(g_ref[:, 0:CHUNK]), r3(b_ref[...]))
        s, ga = s_ref[...], ga_ref[...]
        d_n = -_bmm_nt(ga, s)
        d_eg = jnp.sum(ga * s, axis=1, keepdims=True)
        dq, dk, dv, dg, dg64, db = vjp((d_n, ga, r3(dqe_ref[...]), r3(do_ref[...]), d_eg))
        for o_ref, val in zip((dq_o, dk_o, dv_o, dg_o, db_o), (dq, dk, dv, dg, db)):
            o_ref[...] = val.reshape(rows, HEAD)
        dg_o[:, 0:CHUNK] += dg64.reshape(rows, CHUNK)

    return _pcall(body, "dn_chunk_bwd", (HEADS, t // rows), [blk, blk, vblk, blk, blk, mat, mat, blk, blk], [blk] * 5,
                  [jax.ShapeDtypeStruct((t, DN_WIDTH), F32)] * 5, [qn, kn, conv, g, beta, sall, gall, dq_eff, do],
                  ("arbitrary", "arbitrary"), host=host)


def _dn_rec_fwd(n_mat, b_mat, eg, host=None):
    nc = n_mat.shape[0]
    nb = _chunk_batch(nc * CHUNK)
    spec = pl.BlockSpec((nb, DN_WIDTH, HEAD), lambda i: (i, 0, 0))

    def body(n_ref, b_ref, eg_ref, sall_ref, s_scr):
        @pl.when(pl.program_id(0) == 0)
        def _():
            s_scr[...] = jnp.zeros(s_scr.shape, F32)

        for j in range(nb):
            sall_ref[j] = s_scr[...]
            for h in range(HEADS):
                sl = slice(h * HEAD, (h + 1) * HEAD)
                s_scr[sl, :] = _dn_rec_fn(s_scr[sl, :], n_ref[j, sl, :], b_ref[j, sl, :],
                                          eg_ref[j, h * HEAD:h * HEAD + 1, :])

    return _pcall(body, "dn_rec_fwd", (nc // nb,), [spec] * 3, spec, jax.ShapeDtypeStruct((nc, DN_WIDTH, HEAD), F32),
                  [n_mat, b_mat, eg], ("arbitrary",), scratch_shapes=[pltpu.VMEM((DN_WIDTH, HEAD), F32)], host=host)


def _dn_rec_bwd(n_mat, eg, ds_out, host=None):
    nc = n_mat.shape[0]
    nb = _chunk_batch(nc * CHUNK)
    steps = nc // nb
    spec = pl.BlockSpec((nb, DN_WIDTH, HEAD), lambda i: (steps - 1 - i, 0, 0))

    def body(n_ref, eg_ref, dso_ref, gall_ref, g_scr):
        @pl.when(pl.program_id(0) == 0)
        def _():
            g_scr[...] = jnp.zeros(g_scr.shape, F32)

        for j in reversed(range(nb)):
            gall_ref[j] = g_scr[...]
            for h in range(HEADS):
                sl = slice(h * HEAD, (h + 1) * HEAD)
                gv = g_scr[sl, :]
                g_scr[sl, :] = (gv * eg_ref[j, h * HEAD:h * HEAD + 1, :] - _mm_tn(n_ref[j, sl, :], gv)
                                + dso_ref[j, sl, :])

    return _pcall(body, "dn_rec_bwd", (steps,), [spec] * 3, spec, jax.ShapeDtypeStruct((nc, DN_WIDTH, HEAD), F32),
                  [n_mat, eg, ds_out], ("arbitrary",), scratch_shapes=[pltpu.VMEM((DN_WIDTH, HEAD), F32)], host=host)


def _dn_o_fwd(sall, q_eff, o_own):
    t = q_eff.shape[0]
    nb = _chunk_batch(t)
    rows, blk, _, mat = _dn_chunk_specs(t, nb)

    def body(s_ref, qe_ref, oo_ref, o_ref):
        r3 = lambda x: x.reshape(nb, CHUNK, HEAD)
        o_ref[...] = _dn_o_fn(s_ref[...], r3(qe_ref[...]), r3(oo_ref[...])).reshape(rows, HEAD)

    return _pcall(body, "dn_o_fwd", (HEADS, t // rows), [mat, blk, blk], blk, jax.ShapeDtypeStruct((t, DN_WIDTH), F32),
                  [sall, q_eff, o_own], ("arbitrary", "arbitrary"))


def _dn_o_bwd(sall, q_eff, do, host=None):
    t = q_eff.shape[0]
    nb = _chunk_batch(t)
    rows, blk, _, mat = _dn_chunk_specs(t, nb)

    def body(s_ref, qe_ref, do_ref, dqe_ref, ds_ref):
        r3 = lambda x: x.reshape(nb, CHUNK, HEAD)
        dov = r3(do_ref[...])
        dqe_ref[...] = _bmm_nt(dov, s_ref[...]).reshape(rows, HEAD)
        ds_ref[...] = _bmm_tn(r3(qe_ref[...]), dov)

    nc = t // CHUNK
    return _pcall(body, "dn_o_bwd", (HEADS, t // rows), [mat, blk, blk], [blk, mat],
                  [jax.ShapeDtypeStruct((t, DN_WIDTH), F32), jax.ShapeDtypeStruct((nc, DN_WIDTH, HEAD), F32)],
                  [sall, q_eff, do], ("arbitrary", "arbitrary"), host=host)


def _loss_call(h2, tgt, n_valid):
    t, n = h2.shape
    r = _row_tile(t)

    def body(h_ref, t_ref, dy_ref, dy16_ref, acc_ref):
        rows = pl.program_id(0) * r + lax.broadcasted_iota(jnp.int32, (r, n), 0)
        valid = jnp.logical_and(rows >= N_META, rows < n_valid)
        e = jnp.where(valid, h_ref[...] - t_ref[...], 0.0)
        dy = e * (1.0 / n)
        dy_ref[...] = dy
        dy16_ref[...] = dy.astype(BF16)
        _accumulate(acc_ref, jnp.sum(e * e, axis=0, keepdims=True))

    return _rows_call("loss", body, [h2, tgt], [], [(n, F32), (n, BF16)], [(1, n)], r)


def _adamw_call(name, w, g, m, v, host=None):
    rows, cols = w.shape
    by_rows = rows % 8 == 0

    def body(w_ref, g_ref, m_ref, v_ref, g_out, d_ref, m_out, v_out):
        gv = g_ref[...] if by_rows else g_ref[0:rows, :]
        m2 = ADAM_B1 * m_ref[...] + (1.0 - ADAM_B1) * gv
        v2 = ADAM_B2 * v_ref[...] + (1.0 - ADAM_B2) * (gv * gv)
        m_hat = m2 / (1.0 - ADAM_B1 ** ADAM_STEP)
        v_hat = v2 / (1.0 - ADAM_B2 ** ADAM_STEP)
        g_out[...] = gv
        d_ref[...] = -ADAM_LR * (m_hat / (jnp.sqrt(v_hat) + ADAM_EPS) + ADAM_WD * w_ref[...])
        m_out[...] = m2
        v_out[...] = v2

    if by_rows:
        tr = _tile(rows, 256, 8)
        spec = g_spec = pl.BlockSpec((tr, cols), lambda i: (i, 0))
        grid = (rows // tr,)
    else:
        tc = _tile(cols, 256, 128)
        spec = pl.BlockSpec((rows, tc), lambda j: (0, j))
        g_spec = pl.BlockSpec((g.shape[0], tc), lambda j: (0, j))
        grid = (cols // tc,)
    return _pcall(body, name, grid, [spec, g_spec, spec, spec], [spec] * 4, [jax.ShapeDtypeStruct((rows, cols), F32)] * 4,
                  [w, g, m, v], ("arbitrary",), host=host)


def _rope_tables(t):
    half = ROPE // 2
    inv_freq = np.float32(ROPE_THETA) ** (-np.arange(half, dtype=np.float32) / np.float32(half))
    ang = np.arange(t, dtype=np.float32)[:, None] * inv_freq[None, :].astype(np.float32)
    z = np.zeros((t, HEAD - ROPE), np.float32)
    cos = np.concatenate([np.cos(ang), np.cos(ang), z], axis=1).astype(np.float32)
    sin = np.concatenate([np.sin(ang), np.sin(ang), z], axis=1).astype(np.float32)
    k = np.arange(HEAD)[:, None]
    l = np.arange(HEAD)[None, :]
    perm = np.where((l < half) & (k == l + half), -1.0, 0.0) + np.where((l >= half) & (l < ROPE) & (k == l - half), 1.0, 0.0)
    return jnp.asarray(cos), jnp.asarray(sin), jnp.asarray(perm.astype(np.float32))


def _win_to_pad(w):
    z = lambda n: jnp.zeros((n, w.shape[1]), w.dtype)
    return jnp.concatenate([w[576:2112], w[2112:2624], w[0:256], w[256:512], w[512:576], z(64), w[2624:2632], z(120)],
                           axis=0)


def _win_from_pad(g):
    return jnp.concatenate([g[2048:2304], g[2304:2560], g[2560:2624], g[0:1536], g[1536:2048], g[2688:2696]], axis=0)


def _qk_to_pad(w):
    w4 = w.reshape(HEADS, QK_DIM, w.shape[-1])
    return jnp.concatenate([w4, jnp.zeros((HEADS, QK_PAD - QK_DIM, w.shape[-1]), w.dtype)], axis=1).reshape(
        HEADS * QK_PAD, w.shape[-1])


def _qk_from_pad(g):
    return g.reshape(HEADS, QK_PAD, g.shape[-1])[:, :QK_DIM].reshape(HEADS * QK_DIM, g.shape[-1])


def _ff_to_pad(a, axis):
    shape = list(a.shape)
    shape[axis:axis + 1] = [N_CHIPS, FF_SHARD]
    a4 = a.reshape(shape)
    shape[axis + 1] = FF_BLOCK - FF_SHARD
    out = jnp.concatenate([a4, jnp.zeros(shape, a.dtype)], axis=axis + 1)
    shape[axis:axis + 2] = [D_FF_P]
    return out.reshape(shape)


def _ff_from_pad(a, axis):
    shape = list(a.shape)
    shape[axis:axis + 1] = [N_CHIPS, FF_BLOCK]
    a4 = lax.slice_in_dim(a.reshape(shape), 0, FF_SHARD, axis=axis + 1)
    shape[axis:axis + 2] = [D_FF]
    return a4.reshape(shape)


class _LocalPlan:
    def __init__(self, wt):
        self.wt, self.grads = wt, {}

    def weight(self, name):
        return self.wt[name]

    def host(self, point):
        return None

    def grad(self, name, value):
        self.grads[name] = value


def _local_step(x, tgt, wt, plan=None):
    plan = _LocalPlan(wt) if plan is None else plan
    s = x.shape[0]
    n_valid = N_META + s
    t = -(-n_valid // HEAD) * HEAD
    zpad = jnp.zeros((t - n_valid, D_MODEL), F32)
    h0 = jnp.concatenate([wt["meta_tokens"], x, zpad], axis=0)
    tgt_p = jnp.concatenate([jnp.zeros((N_META, D_MODEL), F32), tgt, zpad], axis=0)
    cos, sin, perm = _rope_tables(t)
    win, wq, wkv = wt["w_in_t"], wt["w_q_t"], wt["w_kv_t"]
    qn_w = jnp.concatenate([wt["q_norm_w"], jnp.zeros((1, QK_PAD - QK_DIM), F32)], axis=1)
    kn_w = jnp.concatenate([wt["k_norm_w"], jnp.zeros((1, QK_PAD - QK_DIM), F32)], axis=1)
    head_id = jnp.arange(DN_WIDTH)[None, :] // HEAD
    lane = jnp.arange(HEAD)[:, None]
    sel_a = (lane == head_id).astype(F32)
    sel_b = (lane == head_id + HEADS).astype(F32)
    alog = jnp.repeat(wt["dn_A_log"], HEAD, axis=1)
    dtb = jnp.repeat(wt["dn_dt_bias"], HEAD, axis=1)
    conv_w, conv_b = wt["ffn_conv_w"], wt["ffn_conv_b"]

    u = _rms_fwd("attn_norm_fwd", h0, wt["attn_norm_w"])
    proj = _matmul("in_proj", u, win, "nt", F32)
    z = (proj, DN_WIDTH, 3)
    q_lat, kv_lat, k_pe, ab = (proj, LORA, 8), (proj, LORA, 9), (proj, HEAD, 20), (proj, HEAD, 21)
    mla_consts = (wt["q_a_norm_w"], wq, wt["kv_a_norm_w"], wkv, qn_w, kn_w, perm)
    q, k, v = _mla_prep_fwd(q_lat, kv_lat, k_pe, cos, sin, *mla_consts)
    o_mla = _attn_fwd(q, k, v, host=plan.host("attn_fwd"))
    conv = _dn_conv_fwd(proj, wt["dn_conv_w"])
    dn_consts = (sel_a, sel_b, alog, dtb)
    qn, kn, g, beta = _dn_prep_fwd(conv, ab, *dn_consts)
    n_mat, b_mat, q_eff, o_own, eg = _dn_chunk_fwd(qn, kn, conv, g, beta, host=plan.host("dn_chunk_fwd"))
    sall = _dn_rec_fwd(n_mat, b_mat, eg)
    o_dn = _dn_o_fwd(sall, q_eff, o_own)
    mixed = _mix_out_fwd(o_mla, o_dn, z, wt["mla_out_norm_w"], wt["dn_out_norm_w"])
    w_out = plan.weight("w_out")
    h1 = _matmul("out_proj", mixed, w_out, "nn", F32, res=h0)
    n2 = _rms_fwd("ffn_norm_fwd", h1, wt["ffn_norm_w"])
    w_gate, w_up = plan.weight("w_gate_t"), plan.weight("w_up_t")
    gpre = _matmul("gate_proj", n2, w_gate, "nt", F32, host=plan.host("gate_proj"))
    up = _matmul("up_proj", n2, w_up, "nt", F32, host=plan.host("up_proj"))
    act = _glu_fwd(gpre, up, conv_w, conv_b)
    w_down = plan.weight("w_down")
    h2 = _matmul("down_proj", act, w_down, "nn", F32, res=h1)
    dy, dy16, sq = _loss_call(h2, tgt_p, n_valid)

    grads = {}
    dact = _matmul("down_dx", dy16, w_down, "nt", F32)
    plan.grad("w_down", _matmul("down_dw", act, dy16, "tn", F32))
    dgpre, dup, grads["ffn_conv_w"], grads["ffn_conv_b"] = _glu_bwd(gpre, up, conv_w, conv_b, dact)
    plan.grad("w_gate_t", _matmul("gate_dw", dgpre, n2, "tn", F32))
    plan.grad("w_up_t", _matmul("up_dw", dup, n2, "tn", F32))
    dn2a = _matmul("gate_dx", dgpre, w_gate, "nn", F32, host=plan.host("gate_dx"))
    dn2b = _matmul("up_dx", dup, w_up, "nn", F32, host=plan.host("up_dx"))
    dh1, dh1_16, grads["ffn_norm_w"] = _rms_bwd("ffn_norm_bwd", h1, wt["ffn_norm_w"], [dn2a, dn2b], dy)
    dmixed = _matmul("out_dx", dh1_16, w_out, "nt", F32)
    plan.grad("w_out", _matmul("out_dw", mixed, dh1_16, "tn", F32))
    do_mla, do_dn, dz, grads["mla_out_norm_w"], grads["dn_out_norm_w"] = _mix_out_bwd(
        o_mla, o_dn, z, dmixed, wt["mla_out_norm_w"], wt["dn_out_norm_w"], host=plan.host("mix_out_bwd"))
    dq_eff, ds_out = _dn_o_bwd(sall, q_eff, do_dn)
    gall = _dn_rec_bwd(n_mat, eg, ds_out)
    dqn, dkn, dv_dn, dg, dbeta = _dn_chunk_bwd(qn, kn, conv, g, beta, sall, gall, dq_eff, do_dn,
                                               host=plan.host("dn_chunk_bwd"))
    dconv, dab, dalog, ddtb = _dn_prep_bwd(conv, ab, dqn, dkn, dv_dn, dg, dbeta, *dn_consts)
    grads["dn_A_log"] = jnp.sum(dalog.reshape(HEADS, HEAD), axis=1)[None, :]
    grads["dn_dt_bias"] = jnp.sum(ddtb.reshape(HEADS, HEAD), axis=1)[None, :]
    ddn_pre, grads["dn_conv_w"] = _dn_conv_bwd(proj, wt["dn_conv_w"], dconv)
    dq, dk, dv = _attn_bwd(q, k, v, do_mla, host=plan.host("attn_bwd"))
    dq_lat, dkv_lat, dk_pe, dqa, dwq, dkva, dwkv, dqnw, dknw = _mla_prep_bwd(
        q_lat, kv_lat, k_pe, cos, sin, dq, dk, dv, *mla_consts, host=plan.host("mla_prep_bwd"))
    grads["q_a_norm_w"], grads["kv_a_norm_w"] = dqa, dkva
    plan.grad("w_q_t", dwq)
    plan.grad("w_kv_t", dwkv)
    grads["q_norm_w"], grads["k_norm_w"] = dqnw[:, :QK_DIM], dknw[:, :QK_DIM]
    dproj = jnp.concatenate([ddn_pre, dz, dq_lat, dkv_lat, dk_pe, dab], axis=1)
    plan.grad("w_in_t", _matmul("in_dw", dproj, u, "tn", F32))
    du = _matmul("in_dx", dproj, win, "nn", F32, host=plan.host("in_dx"))
    dh0, _, grads["attn_norm_w"] = _rms_bwd("attn_norm_bwd", h0, wt["attn_norm_w"], [du], dh1,
                                            host=plan.host("attn_norm_bwd"))
    grads["meta_tokens"] = dh0[0:N_META]
    if isinstance(plan, _LocalPlan):
        grads.update(plan.grads)
    return sq, dh0[N_META:n_valid], grads


def _mesh_pos():
    return lax.axis_index("x"), lax.axis_index("y"), lax.axis_index("c")


def _other_chips(x, y):
    return [(1 - x, y), (x, 1 - y), (1 - x, 1 - y)]


def _remote(src, dst, send_sems, recv_sems, k, to):
    return pltpu.make_async_remote_copy(src_ref=src, dst_ref=dst, send_sem=send_sems.at[k], recv_sem=recv_sems.at[k],
                                        device_id=to, device_id_type=MESH)


def _copies_exchange(make, ins, out_shape, nsem):
    def prog(in_refs, out_refs, send_sems, recv_sems):
        copies = make(in_refs, out_refs, send_sems, recv_sems)

        def start():
            for cp in copies:
                cp.start()

        def finish():
            for cp in copies:
                cp.wait()

        return start, finish

    return _Exchange(prog, ins, out_shape, nsem)


def _all_gather(shards):
    def prog(srcs, dsts, send_sems, recv_sems):
        x, y, c = _mesh_pos()
        p = 2 * x + y
        sibling = (x, y, 1 - c)
        chips = _other_chips(x, y)
        bufs = tuple((s, d, s.shape[0] // 2) for s, d in zip(srcs, dsts))

        def half(ref, rows, which):
            return ref.at[pl.ds(which * rows, rows), :]

        def copy(i, k, src, dst, to):
            return _remote(src, dst, send_sems, recv_sems, 6 * i + k, to)

        sends = [copy(i, j, half(src, rows, c), half(dst.at[p], rows, c), (*chip, c))
                 for i, (src, dst, rows) in enumerate(bufs) for j, chip in enumerate(chips)]

        def start():
            for cp in sends:
                cp.start()

        def finish():
            passed = []
            for i, (src, dst, rows) in enumerate(bufs):
                for j, (qx, qy) in enumerate(chips):
                    block = half(dst.at[2 * qx + qy], rows, c)
                    copy(i, j, block, block, (x, y, c)).wait_recv()
                    fwd = copy(i, 3 + j, block, block, sibling)
                    fwd.start()
                    passed.append(fwd)
            for i, (src, dst, rows) in enumerate(bufs):
                for j, (qx, qy) in enumerate(chips):
                    block = half(dst.at[2 * qx + qy], rows, 1 - c)
                    copy(i, 3 + j, block, block, (x, y, c)).wait_recv()
            for cp in sends + passed:
                cp.wait_send()

        return start, finish

    return _Exchange(prog, shards, [jax.ShapeDtypeStruct((N_CHIPS, *s.shape), s.dtype) for s in shards], 6 * len(shards))


def _gathered(ex):
    p = 2 * lax.axis_index("x") + lax.axis_index("y")
    return [lax.dynamic_update_slice(g, s[None], (p, 0, 0)) for g, s in zip(ex.outs, ex.ins)]


def _rs_to_sibling(bufs):
    def make(srcs, dsts, send_sems, recv_sems):
        x, y, c = _mesh_pos()
        copies = []
        for i, (src, dst) in enumerate(zip(srcs, dsts)):
            half = src.shape[1] // 2
            copies.append(_remote(src.at[:, pl.ds((1 - c) * half, half), :], dst, send_sems, recv_sems, i, (x, y, 1 - c)))
        return copies

    return _copies_exchange(make, bufs, [jax.ShapeDtypeStruct((N_CHIPS, b.shape[1] // 2, b.shape[2]), F32) for b in bufs],
                            len(bufs))


def _rs_pair_add(name, bufs, gots, c, out_dtype):
    n = len(bufs)

    def body(c_ref, *refs):
        for a_ref, b_ref, o_ref in zip(refs[:n], refs[n:2 * n], refs[2 * n:]):
            o_ref[...] = (a_ref[...] + b_ref[...]).astype(out_dtype)

    mine = [pl.BlockSpec((None, g.shape[1], g.shape[2]), lambda j, cr: (j, cr[0], 0)) for g in gots]
    whole = [pl.BlockSpec((None, g.shape[1], g.shape[2]), lambda j, cr: (j, 0, 0)) for g in gots]
    return pl.pallas_call(
        body, name=name,
        grid_spec=pltpu.PrefetchScalarGridSpec(num_scalar_prefetch=1, grid=(N_CHIPS,), in_specs=mine + whole, out_specs=whole),
        out_shape=[jax.ShapeDtypeStruct(g.shape, out_dtype) for g in gots],
        compiler_params=_cparams(("arbitrary",)))(c, *bufs, *gots)


def _rs_to_chips(accs):
    def make(srcs, dsts, send_sems, recv_sems):
        x, y, c = _mesh_pos()
        return [_remote(src.at[2 * qx + qy], dst.at[k], send_sems, recv_sems, 3 * i + k, (qx, qy, c))
                for i, (src, dst) in enumerate(zip(srcs, dsts)) for k, (qx, qy) in enumerate(_other_chips(x, y))]

    return _copies_exchange(make, accs, [jax.ShapeDtypeStruct((3, a.shape[1], a.shape[2]), a.dtype) for a in accs],
                            3 * len(accs))


def _rs_chip_add(name, accs, gots, p):
    n = len(accs)
    slot = (0, 1, 0, 2)

    def body(p_ref, *refs):
        me = p_ref[0]
        for own_ref, got_ref, o_ref in zip(refs[:n], refs[n:2 * n], refs[2 * n:]):
            total = None
            for chip in range(N_CHIPS):
                val = own_ref[...].astype(F32)
                for e in (1, 2, 3):
                    val = jnp.where((chip ^ me) == e, got_ref[slot[e]].astype(F32), val)
                total = val if total is None else total + val
            o_ref[...] = total

    own = [pl.BlockSpec((None, a.shape[1], a.shape[2]), lambda i, pr: (pr[0], 0, 0)) for a in accs]
    got = [pl.BlockSpec(g.shape, lambda i, pr: (0, 0, 0)) for g in gots]
    out = [pl.BlockSpec((a.shape[1], a.shape[2]), lambda i, pr: (0, 0)) for a in accs]
    return pl.pallas_call(
        body, name=name,
        grid_spec=pltpu.PrefetchScalarGridSpec(num_scalar_prefetch=1, grid=(1,), in_specs=own + got, out_specs=out),
        out_shape=[jax.ShapeDtypeStruct((a.shape[1], a.shape[2]), F32) for a in accs],
        compiler_params=_cparams(("arbitrary",)))(p, *accs, *gots)


def _rs_share(ress):
    def make(srcs, dsts, send_sems, recv_sems):
        x, y, c = _mesh_pos()
        return [_remote(src, dst, send_sems, recv_sems, i, (x, y, 1 - c)) for i, (src, dst) in enumerate(zip(srcs, dsts))]

    return _copies_exchange(make, ress, [jax.ShapeDtypeStruct(r.shape, F32) for r in ress], len(ress))


def _shared(ex):
    south = lax.axis_index("c") == 0
    return [jnp.concatenate([jnp.where(south, r, g), jnp.where(south, g, r)], axis=0) for r, g in zip(ex.ins, ex.outs)]


def _all_to_all_devices(vec):
    def make(srcs, dsts, send_sems, recv_sems):
        x, y, c = _mesh_pos()
        me = 4 * x + 2 * y + c
        copies = []
        for r in range(1, 8):
            px, py, pc = (1 - x if r & 4 else x), (1 - y if r & 2 else y), (1 - c if r & 1 else c)
            copies.append(_remote(srcs[0], dsts[0].at[me], send_sems, recv_sems, r - 1, (px, py, pc)))
        return copies

    return _copies_exchange(make, [vec], [jax.ShapeDtypeStruct((8, *vec.shape), vec.dtype)], 7)


def _sum_devices(stack):
    def body(s_ref, o_ref):
        total = s_ref[0]
        for d in range(1, 8):
            total = total + s_ref[d]
        o_ref[...] = total

    return pl.pallas_call(body, name="sum_devices", out_shape=jax.ShapeDtypeStruct(stack.shape[1:], F32),
                          compiler_params=pltpu.CompilerParams(vmem_limit_bytes=VMEM_LIMIT))(stack)


def _pad_rows(flat, rows):
    return jnp.concatenate([flat, jnp.zeros((rows * LANES - flat.shape[0],), flat.dtype)]).reshape(rows, LANES)


def _unshard(g4, shape, axis):
    a = g4.reshape(N_CHIPS, *shape)
    if axis == 0:
        return a.reshape(N_CHIPS * shape[0], shape[1])
    return jnp.transpose(a, (1, 0, 2)).reshape(shape[0], N_CHIPS * shape[1])


def _shard4(full, shape, axis):
    if axis == 0:
        return full.reshape(N_CHIPS, shape[0] * shape[1])
    a = full.reshape(shape[0], N_CHIPS, shape[1])
    return jnp.transpose(a, (1, 0, 2)).reshape(N_CHIPS, shape[0] * shape[1])


def _pad_axis0(a, rows):
    return jnp.concatenate([a, jnp.zeros((rows - a.shape[0], *a.shape[1:]), a.dtype)], axis=0)


def _pad_axis1(a, rows):
    return jnp.concatenate([a, jnp.zeros((a.shape[0], rows - a.shape[1], *a.shape[2:]), a.dtype)], axis=1)


def _shard_to_strip(name, w):
    _, (shape, axis, rows) = name, {n: (s, ax, r) for n, s, ax, r in BIG}[name]
    w2 = w.reshape(shape).astype(BF16)
    return _pad_axis0(w2.T if axis == 1 else w2, rows)


LOCAL_NAME = dict(w_in="w_in_t", w_q_b="w_q_t", w_kv_b="w_kv_t", w_out="w_out", w_gate="w_gate_t", w_up="w_up_t",
                  w_down="w_down")


WIN_SEGMENTS = ((576, 2112, 0), (2112, 2624, 1536), (0, 256, 2048), (256, 512, 2304), (512, 576, 2560), (2624, 2632, 2688))


def _strips_to_weight(name, g4):
    if name == "w_in":
        return _win_to_pad(g4[:, :IN_SHARD].reshape(IN_COLS, D_MODEL))
    if name == "w_q_b":
        return _qk_to_pad(g4.reshape(HEADS * QK_DIM, LORA))
    return g4.reshape(N_CHIPS * g4.shape[1], g4.shape[2])


def _grad_to_strips(name, g):
    if name == "w_in":
        strips = []
        for q in range(N_CHIPS):
            pieces = []
            for a, b, local in sorted(WIN_SEGMENTS):
                s, e = max(a, q * IN_SHARD), min(b, (q + 1) * IN_SHARD)
                if s < e:
                    pieces.append(g[local + s - a:local + e - a])
            pieces.append(jnp.zeros((IN_SHARD_P - IN_SHARD, D_MODEL), g.dtype))
            strips.append(jnp.concatenate(pieces, axis=0))
        return jnp.stack(strips)
    if name == "w_q_b":
        return _qk_from_pad(g).reshape(N_CHIPS, QK_DIM, LORA)
    return g.reshape(N_CHIPS, g.shape[0] // N_CHIPS, g.shape[1])


class _MeshPlan:
    LATE = dict(attn_fwd=("w_up",), dn_chunk_fwd=("w_out", "w_gate"), gate_proj=("w_down/0",), up_proj=("w_down/1",))
    GROUP_A = ("w_down", "w_gate", "w_up", "w_out")
    GROUP_B = ("w_in", "w_q_b", "w_kv_b")

    def __init__(self, w):
        x, y, c = _mesh_pos()
        self.ci = jnp.reshape(c, (1,)).astype(jnp.int32)
        self.pi = jnp.reshape(2 * x + y, (1,)).astype(jnp.int32)
        self.strip = {n: _shard_to_strip(n, w[n]) for n, _, _, _ in BIG}
        self.gathers, self.weights, self.g, self.acc, self.reduced = {}, {}, {}, {}, {}
        self.sibs, self.sib, self.chip, self.share, self.halves = [], None, None, None, [None, None]

    def gather_first(self, small):
        names = ("w_in", "w_q_b", "w_kv_b")
        ex = _all_gather([self.strip[n] for n in names] + [small])
        ex.run("all_gather_first")
        got = _gathered(ex)
        for n, g4 in zip(names, got):
            self.weights[LOCAL_NAME[n]] = _strips_to_weight(n, g4)
        return got[-1]

    def weight(self, local_name):
        if local_name not in self.weights:
            for point, (names, ex) in list(self.gathers.items()):
                if ex.outs is not None:
                    for n, g4 in zip(names, _gathered(ex)):
                        if "/" in n:
                            n, half = n.split("/")
                            self.halves[int(half)] = g4
                            if None in self.halves:
                                continue
                            g4 = jnp.concatenate(self.halves, axis=1)
                        self.weights[LOCAL_NAME[n]] = _strips_to_weight(n, g4)
                    del self.gathers[point]
        return self.weights[local_name]

    def _shard(self, name):
        if "/" not in name:
            return self.strip[name]
        name, half = name.split("/")
        rows = self.strip[name].shape[0] // 2
        return self.strip[name][int(half) * rows:(int(half) + 1) * rows]

    def grad(self, local_name, value):
        name = {v: k for k, v in LOCAL_NAME.items()}[local_name]
        self.g[name] = _grad_to_strips(name, value)

    def _pair_add(self, names, gots):
        accs = _rs_pair_add("rs_pair_add_" + names[0], [self.g[n] for n in names], gots, self.ci, BF16)
        self.acc.update(zip(names, accs))

    def _chip_add(self, names, chip):
        return _rs_chip_add("rs_chip_add_" + names[0], [self.acc[n] for n in names], chip.outs, self.pi)

    def _take_shared(self, names, share):
        for n, strip in zip(names, _shared(share)):
            self.reduced[n] = strip

    def host(self, point):
        a, b = self.GROUP_A, self.GROUP_B
        if point in self.LATE:
            names = self.LATE[point]
            ex = _all_gather([self._shard(n) for n in names])
            self.gathers[point] = (names, ex)
            return ex
        if point in ("gate_dx", "up_dx", "mix_out_bwd"):
            names = dict(gate_dx=a[:2], up_dx=a[2:3], mix_out_bwd=a[3:])[point]
            ex = _rs_to_sibling([self.g[n] for n in names])
            self.sibs.append(ex)
            return ex
        if point == "dn_chunk_bwd":
            self._pair_add(a, [o for ex in self.sibs for o in ex.outs])
            self.chip1 = _rs_to_chips([self.acc[n] for n in a[:2]])
            return self.chip1
        if point == "attn_bwd":
            self.chip2 = _rs_to_chips([self.acc[n] for n in a[2:]])
            return self.chip2
        if point == "mla_prep_bwd":
            ress = self._chip_add(a[:2], self.chip1) + self._chip_add(a[2:], self.chip2)
            self.share = _rs_share(ress)
            return self.share
        if point == "in_dx":
            self._take_shared(a, self.share)
            self.sib = _rs_to_sibling([self.g[n] for n in b])
            return self.sib
        if point == "attn_norm_bwd":
            self._pair_add(b, self.sib.outs)
            self.chip = _rs_to_chips([self.acc[n] for n in b])
            return self.chip
        return None

    def last_share(self):
        self.share = _rs_share(self._chip_add(self.GROUP_B, self.chip))
        return self.share

    def finish(self):
        self._take_shared(self.GROUP_B, self.share)
        return self.reduced


def _strip_to_shard(name, strip):
    shape, axis = {n: (s, ax) for n, s, ax, _ in BIG}[name]
    rows = shape[axis]
    return strip[:rows].T if axis == 1 else strip[:rows]


def kernel(x, meta_tokens, attn_norm_w, w_in, q_a_norm_w, w_q_b, kv_a_norm_w, w_kv_b, q_norm_w, k_norm_w, mla_out_norm_w, dn_conv_w, dn_A_log, dn_dt_bias, dn_out_norm_w, w_out, ffn_norm_w, w_gate, w_up, ffn_conv_w, ffn_conv_b, w_down, loss_target, m_meta_tokens, m_attn_norm_w, m_w_in, m_q_a_norm_w, m_w_q_b, m_kv_a_norm_w, m_w_kv_b, m_q_norm_w, m_k_norm_w, m_mla_out_norm_w, m_dn_conv_w, m_dn_A_log, m_dn_dt_bias, m_dn_out_norm_w, m_w_out, m_ffn_norm_w, m_w_gate, m_w_up, m_ffn_conv_w, m_ffn_conv_b, m_w_down, v_meta_tokens, v_attn_norm_w, v_w_in, v_q_a_norm_w, v_w_q_b, v_kv_a_norm_w, v_w_kv_b, v_q_norm_w, v_k_norm_w, v_mla_out_norm_w, v_dn_conv_w, v_dn_A_log, v_dn_dt_bias, v_dn_out_norm_w, v_w_out, v_ffn_norm_w, v_w_gate, v_w_up, v_ffn_conv_w, v_ffn_conv_b, v_w_down):
    local = dict(locals())
    w = {n: local[n] for n in WEIGHTS}
    m = {n: local["m_" + n] for n in WEIGHTS}
    v = {n: local["v_" + n] for n in WEIGHTS}
    p = 2 * lax.axis_index("x") + lax.axis_index("y")

    plan = _MeshPlan(w)
    wf = _pad_rows(jnp.concatenate([w[n].reshape(-1) for n, _, _ in SMALL_SHARDED]), SMALL_ROWS)
    gf = plan.gather_first(wf).reshape(N_CHIPS, -1)
    full = dict(plan.weights)
    off = 0
    for n, s, ax in SMALL_SHARDED:
        full[n] = _unshard(gf[:, off:off + s[0] * s[1]], s, ax)
        off += s[0] * s[1]
    for n, _ in REPLICATED:
        full[n] = w[n]
    full["ffn_conv_w"] = _ff_to_pad(full["ffn_conv_w"], 1)
    full["ffn_conv_b"] = _ff_to_pad(full["ffn_conv_b"], 1)

    sq, grad_x, g = _local_step(x[0], loss_target[0], full, plan)
    g["ffn_conv_w"] = _ff_from_pad(g["ffn_conv_w"], 1)
    g["ffn_conv_b"] = _ff_from_pad(g["ffn_conv_b"], 1)

    small_all = [n for n, _, _ in SMALL_SHARDED] + [n for n, _ in REPLICATED]
    vec = jnp.concatenate([g[n].reshape(-1) for n in small_all] + [jnp.reshape(0.5 / D_MODEL * jnp.sum(sq), (1,))])
    vec = _pad_rows(vec, -(-vec.shape[0] // (8 * LANES)) * 8)
    a2a = _all_to_all_devices(vec)

    gs, delta, new_m, new_v = {}, {}, {}, {}
    big = {n: (s, ax) for n, s, ax, _ in BIG}

    def adamw_big(n, strips, host=None):
        s, ax = big[n]
        flip = ax == 1 and s[1] % 8 == 0
        there = (lambda a: a.reshape(s).T) if flip else (lambda a: a.reshape(s))
        back = (lambda a: a.T.reshape(w[n].shape)) if flip else (lambda a: a.reshape(w[n].shape))
        strip = strips[n] if flip or ax == 0 else strips[n][:s[1]].T
        g2, d2, m2, v2 = _adamw_call("adamw_" + n, there(w[n]), strip, there(m[n]), there(v[n]), host=host)
        gs[n], delta[n], new_m[n], new_v[n] = back(g2), back(d2), back(m2), back(v2)

    adamw_big("w_down", plan.reduced, host=a2a)
    adamw_big("w_gate", plan.reduced, host=plan.last_share())
    adamw_big("w_up", plan.reduced)
    adamw_big("w_out", plan.reduced)
    strips = plan.finish()
    for n in plan.GROUP_B:
        adamw_big(n, strips)
    me = 4 * lax.axis_index("x") + 2 * lax.axis_index("y") + lax.axis_index("c")
    red = _sum_devices(lax.dynamic_update_slice(a2a.outs[0], vec[None], (me, 0, 0))).reshape(-1)
    off = 0
    for n in small_all:
        tot = red[off:off + g[n].size].reshape(g[n].shape)
        off += g[n].size
        shard = {sn: (s, ax) for sn, s, ax in SMALL_SHARDED}.get(n)
        if shard is not None:
            tot = lax.dynamic_slice_in_dim(tot, p * shard[0][1], shard[0][1], axis=1)
        gs[n] = tot
    loss = red[off]
    small_names = [n for n, _, _ in SMALL_SHARDED] + [n for n, _ in REPLICATED]
    rows = SMALL_ROWS + REP_ROWS
    pack = lambda d: _pad_rows(jnp.concatenate([d[n].reshape(-1) for n in small_names]), rows)
    _, d2, m2, v2 = _adamw_call("adamw_small", pack(w), pack(gs), pack(m), pack(v))
    off = 0
    for n in small_names:
        cnt = w[n].size
        for dst, src in ((delta, d2), (new_m, m2), (new_v, v2)):
            dst[n] = src.reshape(-1)[off:off + cnt].reshape(w[n].shape)
        off += cnt

    grad_out = [gs[n].reshape(w[n].shape) for n in WEIGHTS]
    return (loss, grad_x[None], *grad_out, *[delta[n] for n in WEIGHTS], *[new_m[n] for n in WEIGHTS],
            *[new_v[n] for n in WEIGHTS])
```

```python
import functools
import math

import jax
import jax.numpy as jnp
import numpy as np
from jax import lax
from jax.experimental import pallas as pl
from jax.experimental.pallas import tpu as pltpu

F32 = jnp.float32
BF16 = jnp.bfloat16
HI = lax.Precision.HIGHEST
MESH = pl.DeviceIdType.MESH

N_META = 16
D_MODEL = 1024
HEADS = 4
HEAD = 128
ROPE = 64
QK_DIM = HEAD + ROPE
QK_PAD = 2 * HEAD
LORA = 256
DN_WIDTH = HEADS * HEAD
CHUNK = 64
D_FF = 2816
N_CHIPS = 4
FF_SHARD = D_FF // N_CHIPS
FF_BLOCK = 768
D_FF_P = N_CHIPS * FF_BLOCK
IN_COLS = 2632
IN_SHARD = IN_COLS // N_CHIPS
IN_SHARD_P = 672
IN_PAD = 2816
NORM_EPS = 1e-6
ROPE_THETA = 10000.0
LANES = 512

ADAM_LR, ADAM_B1, ADAM_B2, ADAM_EPS, ADAM_WD, ADAM_STEP = 0.001, 0.9, 0.999, 1e-08, 0.01, 10

VMEM_LIMIT = 56 * 1024 * 1024

BIG = (("w_in", (1024, 658), 1, IN_SHARD_P), ("w_q_b", (256, 192), 1, 192), ("w_kv_b", (256, 256), 1, 256),
       ("w_out", (256, 1024), 0, 256), ("w_gate", (1024, 704), 1, FF_BLOCK), ("w_up", (1024, 704), 1, FF_BLOCK),
       ("w_down", (704, 1024), 0, FF_BLOCK))
SMALL_SHARDED = (("meta_tokens", (16, 256), 1), ("dn_conv_w", (4, 384), 1), ("ffn_conv_w", (3, 704), 1))
REPLICATED = (("attn_norm_w", 1024), ("q_a_norm_w", 256), ("kv_a_norm_w", 256), ("q_norm_w", 192), ("k_norm_w", 192),
              ("mla_out_norm_w", 128), ("dn_A_log", 4), ("dn_dt_bias", 4), ("dn_out_norm_w", 128), ("ffn_norm_w", 1024),
              ("ffn_conv_b", 2816))
WEIGHTS = ("meta_tokens", "attn_norm_w", "w_in", "q_a_norm_w", "w_q_b", "kv_a_norm_w", "w_kv_b", "q_norm_w", "k_norm_w",
           "mla_out_norm_w", "dn_conv_w", "dn_A_log", "dn_dt_bias", "dn_out_norm_w", "w_out", "ffn_norm_w", "w_gate",
           "w_up", "ffn_conv_w", "ffn_conv_b", "w_down")

SMALL_ROWS = 16
REP_ROWS = 16


def _cparams(sem):
    return pltpu.CompilerParams(dimension_semantics=sem, vmem_limit_bytes=VMEM_LIMIT)


class _Exchange:
    def __init__(self, prog, ins, out_shape, nsem, peers=None, cid=None):
        self.prog, self.ins, self.out_shape, self.nsem = prog, list(ins), list(out_shape), nsem
        self.peers, self.cid = peers, cid
        self.outs = None

    def sems(self):
        return [pltpu.SemaphoreType.DMA((self.nsem,)), pltpu.SemaphoreType.DMA((self.nsem,))]

    def programs(self, in_refs, out_refs, send_sems, recv_sems):
        start, finish = self.prog(in_refs, out_refs, send_sems, recv_sems)
        if self.cid is None:
            return start, finish
        peers = self.peers()

        def shake_and_start():
            barrier = pltpu.get_barrier_semaphore()
            for peer in peers:
                pl.semaphore_signal(barrier, inc=1, device_id=peer, device_id_type=MESH)
            pl.semaphore_wait(barrier, len(peers))
            start()

        return shake_and_start, finish

    def cparams(self, **kw):
        return pltpu.CompilerParams(has_side_effects=True, collective_id=self.cid, **kw)

    def run(self, name):
        any_spec = pl.BlockSpec(memory_space=pl.ANY)
        n = len(self.ins)

        def body(*refs):
            start, finish = self.programs(refs[:n], refs[n:-2], refs[-2], refs[-1])
            start()
            finish()

        self.outs = pl.pallas_call(
            body, name=name, in_specs=[any_spec] * n, out_specs=[any_spec] * len(self.out_shape),
            out_shape=self.out_shape, scratch_shapes=self.sems(), compiler_params=self.cparams())(*self.ins)
        return self.outs


def _pcall(body, name, grid, in_specs, out_specs, out_shape, args, sem, scratch_shapes=(), host=None):
    single = not isinstance(out_shape, (list, tuple))
    out_specs, out_shape = ([out_specs], [out_shape]) if single else (list(out_specs), list(out_shape))
    if host is None:
        outs = pl.pallas_call(body, name=name, grid=grid, in_specs=list(in_specs), out_specs=out_specs, out_shape=out_shape,
                              scratch_shapes=list(scratch_shapes), compiler_params=_cparams(sem))(*args)
        return outs[0] if single else outs
    any_spec = pl.BlockSpec(memory_space=pl.ANY)
    n_in, n_out, n_scr, nx_in, nx_out = len(in_specs), len(out_specs), len(scratch_shapes), len(host.ins), len(host.out_shape)

    def hosted(*refs):
        c_in, x_in = refs[:n_in], refs[n_in:n_in + nx_in]
        o0 = n_in + nx_in
        c_out, x_out = refs[o0:o0 + n_out], refs[o0 + n_out:o0 + n_out + nx_out]
        s0 = o0 + n_out + nx_out
        start, finish = host.programs(x_in, x_out, refs[s0 + n_scr], refs[s0 + n_scr + 1])
        first = functools.reduce(jnp.logical_and, [pl.program_id(d) == 0 for d in range(len(grid))])
        last = functools.reduce(jnp.logical_and, [pl.program_id(d) == grid[d] - 1 for d in range(len(grid))])
        pl.when(first)(start)
        body(*c_in, *c_out, *refs[s0:s0 + n_scr])
        pl.when(last)(finish)

    outs = pl.pallas_call(
        hosted, name=name, grid=grid, in_specs=list(in_specs) + [any_spec] * nx_in,
        out_specs=out_specs + [any_spec] * nx_out, out_shape=out_shape + host.out_shape,
        scratch_shapes=list(scratch_shapes) + host.sems(),
        compiler_params=host.cparams(dimension_semantics=sem, vmem_limit_bytes=VMEM_LIMIT))(*args, *host.ins)
    host.outs = outs[n_out:]
    return outs[0] if single else outs[:n_out]


NN, NT, TN = ((1,), (0,)), ((1,), (1,)), ((0,), (0,))


def _shift_dims(dims, batch):
    if not batch:
        return (dims, ((), ()))
    return (((dims[0][0] + 1,), (dims[1][0] + 1,)), ((0,), (0,)))


def _make_mm(dims, exact, batch=False):
    def raw(a, b, d):
        dn = _shift_dims(d, batch)
        if exact == "split_lhs":
            ah, bh = a.astype(BF16), b.astype(BF16)
            al = (a - ah.astype(F32)).astype(BF16)
            return lax.dot_general(ah, bh, dn, preferred_element_type=F32) + lax.dot_general(al, bh, dn,
                                                                                              preferred_element_type=F32)
        if exact == "split":
            ah, bh = a.astype(BF16), b.astype(BF16)
            al, bl = (a - ah.astype(F32)).astype(BF16), (b - bh.astype(F32)).astype(BF16)
            dot = lambda p, q: lax.dot_general(p, q, dn, preferred_element_type=F32)
            return dot(ah, bh) + (dot(ah, bl) + dot(al, bh))
        if exact:
            return lax.dot_general(a.astype(F32), b.astype(F32), dn, precision=HI, preferred_element_type=F32)
        return lax.dot_general(a.astype(BF16), b.astype(BF16), dn, preferred_element_type=F32)

    @jax.custom_vjp
    def mm(a, b):
        return raw(a, b, dims)

    def fwd(a, b):
        return raw(a, b, dims), (a, b)

    def bwd(res, g):
        a, b = res
        if dims == NN:
            da, db = raw(g, b, NT), raw(a, g, TN)
        elif dims == NT:
            da, db = raw(g, b, NN), raw(g, a, TN)
        else:
            da, db = raw(b, g, NT), raw(a, g, NN)
        return da.astype(a.dtype), db.astype(b.dtype)

    mm.defvjp(fwd, bwd)
    return mm


_mm = _make_mm(NN, False)
_mm_nt = _make_mm(NT, False)
_mm_tn = _make_mm(TN, False)
_mmx = _make_mm(NN, "split_lhs")
_bmm = _make_mm(NN, False, batch=True)
_bmm_nt = _make_mm(NT, False, batch=True)
_bmm_tn = _make_mm(TN, False, batch=True)
_bmmx = _make_mm(NN, True, batch=True)
_bmms = _make_mm(NN, "split", batch=True)
_bmms_nt = _make_mm(NT, "split", batch=True)
_bmms_tn = _make_mm(TN, "split", batch=True)


@jax.custom_vjp
def _unit_lower_inv(a):
    n = a.shape[-1]
    eye = (lax.broadcasted_iota(jnp.int32, a.shape, 1) == lax.broadcasted_iota(jnp.int32, a.shape, 2)).astype(F32)
    x = -a
    t = eye + x
    for _ in range(max(n.bit_length() - 2, 0)):
        x = _bmms(x, x)
        t = t + _bmms(t, x)
    return t


def _unit_lower_inv_fwd(a):
    t = _unit_lower_inv(a)
    return t, t


def _unit_lower_inv_bwd(t, g):
    return (-_bmms_tn(t, _bmms_nt(g, t)),)


_unit_lower_inv.defvjp(_unit_lower_inv_fwd, _unit_lower_inv_bwd)


def _rms(x, w, n):
    ms = jnp.sum(x * x, axis=-1, keepdims=True) * (1.0 / n)
    return x * lax.rsqrt(ms + NORM_EPS) * w


def _silu(x):
    return x * jax.nn.sigmoid(x)


def _softplus(x):
    return jnp.maximum(x, 0.0) + jnp.log(1.0 + jnp.exp(-jnp.abs(x)))


def _rope(x, cos, sin, perm):
    return x * cos + _mmx(x, perm) * sin


def _mla_prep_fn(rows, consts):
    q_lat, kv_lat, k_pe, cos, sin = rows
    qn = _rms(q_lat, consts["qa_w"], LORA)
    kvn = _rms(kv_lat, consts["kva_w"], LORA)
    outs = []
    for h in range(HEADS):
        q_n = _mm_nt(qn, consts["wq_n"][h])
        q_r = _mm_nt(qn, consts["wq_r"][h])
        rs = lax.rsqrt((jnp.sum(q_n * q_n, -1, keepdims=True) + jnp.sum(q_r * q_r, -1, keepdims=True)) * (1.0 / QK_DIM)
                       + NORM_EPS)
        q_n = q_n * rs * consts["qn_n"]
        q_r = _rope(q_r * rs * consts["qn_r"], cos, sin, consts["perm"])
        k_n = _mm_nt(kvn, consts["wk_n"][h])
        v = _mm_nt(kvn, consts["wv"][h])
        rk = lax.rsqrt((jnp.sum(k_n * k_n, -1, keepdims=True) + jnp.sum(k_pe * k_pe, -1, keepdims=True)) * (1.0 / QK_DIM)
                       + NORM_EPS)
        k_n = k_n * rk * consts["kn_n"]
        k_r = _rope(k_pe * rk * consts["kn_r"], cos, sin, consts["perm"])
        outs += [q_n, q_r, k_n, k_r, v]
    return tuple(outs)


def _attn_fn(q, k, v, row0):
    s = _mm_nt(q, k) * (1.0 / math.sqrt(QK_DIM))
    qpos = row0 + lax.broadcasted_iota(jnp.int32, s.shape, 0)
    kpos = lax.broadcasted_iota(jnp.int32, s.shape, 1)
    s = jnp.where(kpos <= qpos, s, -1e30)
    m = lax.stop_gradient(jnp.max(s, axis=-1, keepdims=True))
    p = jnp.exp(s - m)
    p = p / jnp.sum(p, axis=-1, keepdims=True)
    return _mm(p, v)


def _dn_prep_fn(rows, consts):
    qc, kc, ab = rows
    a_b = _mmx(ab, consts["sel_a"])
    b_b = _mmx(ab, consts["sel_b"])
    beta = jax.nn.sigmoid(b_b)
    g = -jnp.exp(consts["alog"]) * _softplus(a_b + consts["dtb"])
    qs, ks = [], []
    for h in range(HEADS):
        q, k = qc[h], kc[h]
        qs.append(q * lax.rsqrt(jnp.sum(q * q, -1, keepdims=True) + NORM_EPS))
        ks.append(k * lax.rsqrt(jnp.sum(k * k, -1, keepdims=True) + NORM_EPS))
    return tuple(qs), tuple(ks), g, beta


def _dn_chunk_fn(q, k, v, gb, g64, bb):
    nb = q.shape[0]
    ri = lax.broadcasted_iota(jnp.int32, (nb, CHUNK, CHUNK), 1)
    ci = lax.broadcasted_iota(jnp.int32, (nb, CHUNK, CHUNK), 2)
    tri = ri >= ci
    strict = ri > ci
    tril = tri.astype(F32)
    eye = (ri == ci).astype(F32)
    ones = jnp.ones((nb, CHUNK, CHUNK), F32)
    gc = _bmmx(tril, gb)
    gc64 = _bmmx(tril, g64)
    grow = _bmmx(ones, eye * gc64)
    diff = gc64 - grow
    decay = jnp.where(tri, jnp.exp(jnp.where(tri, diff, 0.0)), 0.0)
    kb = k * bb
    vb = v * bb
    a = jnp.where(strict, _bmm_nt(kb, k) * decay, 0.0)
    tinv = _unit_lower_inv(a)
    u = _bmm(tinv, vb)
    w = _bmm(tinv, kb * jnp.exp(gc))
    qs = q * (1.0 / math.sqrt(HEAD))
    qk = _bmm_nt(qs, k) * decay
    qg = qs * jnp.exp(gc)
    glast = jnp.sum(gb, axis=1, keepdims=True)
    kdec = k * jnp.exp(glast - gc)
    n_mat = _bmm_tn(kdec, w)
    b_mat = _bmm_tn(kdec, u)
    q_eff = qg - _bmm(qk, w)
    o_own = _bmm(qk, u)
    return n_mat, b_mat, q_eff, o_own, jnp.exp(glast)


def _dn_rec_fn(s, n_mat, b_mat, eg):
    return s * eg - _mm(n_mat, s) + b_mat


def _dn_o_fn(s, q_eff, o_own):
    return _bmm(q_eff, s) + o_own


def _dn_out_fn(o, z, w):
    return _rms(o, w, HEAD) * _silu(z)


def _row_tile(t):
    return t // 8 if (t // 8) % 16 == 0 else t


def _tile(n, pref, unit):
    best = n
    for cand in range(unit, min(n, pref) + 1, unit):
        if n % cand == 0:
            best = cand
    return best if best <= pref else n


def _rows_call(name, body, rows, consts, outs, accs, r, host=None):
    rows = [a if isinstance(a, tuple) else (a, a.shape[1], 0) for a in rows]
    t = rows[0][0].shape[0]
    zero = lambda nd: (lambda i: (0,) * nd)
    in_specs = [pl.BlockSpec((r, w), functools.partial(lambda i, b: (i, b), b=blk)) for _, w, blk in rows]
    rows = [a for a, _, _ in rows]
    in_specs += [pl.BlockSpec(a.shape, zero(a.ndim)) for a in consts]
    out_shape = [jax.ShapeDtypeStruct((t, w), dt) for w, dt in outs] + [jax.ShapeDtypeStruct(s, F32) for s in accs]
    out_specs = [pl.BlockSpec((r, w), lambda i: (i, 0)) for w, _ in outs] + [pl.BlockSpec(s, zero(len(s))) for s in accs]
    return _pcall(body, name, (t // r,), in_specs, out_specs, out_shape, [*rows, *consts], ("arbitrary",), host=host)


def _accumulate(ref, val):
    @pl.when(pl.program_id(0) == 0)
    def _():
        ref[...] = jnp.zeros(ref.shape, ref.dtype)

    ref[...] += val


def _matmul(name, a, b, dims, out_dtype, res=None, host=None):
    if dims == "nn":
        (m, k), n = a.shape, b.shape[1]
    elif dims == "nt":
        (m, k), n = a.shape, b.shape[0]
    else:
        (k, m), n = a.shape, b.shape[1]
    tm = _tile(m, 1100, 16) if dims != "tn" else _tile(m, 640, 128)
    tn = _tile(n, 1408, 128)
    if dims == "nn":
        a_spec, b_spec, dn = pl.BlockSpec((tm, k), lambda i, j: (i, 0)), pl.BlockSpec((k, tn), lambda i, j: (0, j)), NN
    elif dims == "nt":
        a_spec, b_spec, dn = pl.BlockSpec((tm, k), lambda i, j: (i, 0)), pl.BlockSpec((tn, k), lambda i, j: (j, 0)), NT
    else:
        a_spec, b_spec, dn = pl.BlockSpec((k, tm), lambda i, j: (0, i)), pl.BlockSpec((k, tn), lambda i, j: (0, j)), TN
    o_spec = pl.BlockSpec((tm, tn), lambda i, j: (i, j))

    def body(*refs):
        a_ref, b_ref, o_ref = refs[0], refs[1], refs[-1]
        acc = lax.dot_general(a_ref[...].astype(BF16), b_ref[...].astype(BF16), (dn, ((), ())),
                              preferred_element_type=F32)
        if res is not None:
            acc = acc + refs[2][...]
        o_ref[...] = acc.astype(out_dtype)

    ins = [a, b] + ([res] if res is not None else [])
    specs = [a_spec, b_spec] + ([o_spec] if res is not None else [])
    return _pcall(body, name, (m // tm, n // tn), specs, o_spec, jax.ShapeDtypeStruct((m, n), out_dtype), ins,
                  ("arbitrary", "arbitrary"), host=host)


def _rms_fwd(name, h, w):
    n = h.shape[1]

    def body(h_ref, w_ref, o_ref):
        o_ref[...] = _rms(h_ref[...], w_ref[...], n).astype(BF16)

    return _rows_call(name, body, [h], [w], [(n, BF16)], [], _row_tile(h.shape[0]))[0]


def _rms_bwd(name, h, w, cts, resid, host=None):
    n = h.shape[1]
    nct = len(cts)

    def body(*refs):
        h_ref, ct_refs, r_ref, w_ref = refs[0], refs[1:1 + nct], refs[1 + nct], refs[2 + nct]
        dh_ref, dh16_ref, dw_ref = refs[-3], refs[-2], refs[-1]
        ct = ct_refs[0][...].astype(F32)
        for c in ct_refs[1:]:
            ct = ct + c[...].astype(F32)
        _, vjp = jax.vjp(lambda x, ww: _rms(x, ww, n), h_ref[...], w_ref[...])
        dh, dw = vjp(ct)
        dh = dh + r_ref[...]
        dh_ref[...] = dh
        dh16_ref[...] = dh.astype(BF16)
        _accumulate(dw_ref, dw)

    return _rows_call(name, body, [h, *cts, resid], [w], [(n, F32), (n, BF16)], [(1, n)], _row_tile(h.shape[0]), host=host)


def _mla_consts_from_refs(qa, wq, kva, wkv, qn, kn, perm):
    f = lambda r: r[...].astype(F32)
    return dict(
        qa_w=f(qa), kva_w=f(kva), perm=f(perm),
        wq_n=[wq[h * QK_PAD:h * QK_PAD + HEAD, :].astype(F32) for h in range(HEADS)],
        wq_r=[wq[h * QK_PAD + HEAD:(h + 1) * QK_PAD, :].astype(F32) for h in range(HEADS)],
        wk_n=[wkv[h * QK_PAD:h * QK_PAD + HEAD, :].astype(F32) for h in range(HEADS)],
        wv=[wkv[h * QK_PAD + HEAD:(h + 1) * QK_PAD, :].astype(F32) for h in range(HEADS)],
        qn_n=qn[:, 0:HEAD], qn_r=qn[:, HEAD:QK_PAD], kn_n=kn[:, 0:HEAD], kn_r=kn[:, HEAD:QK_PAD])


def _mla_prep_fwd(q_lat, kv_lat, k_pe, cos, sin, qa, wq, kva, wkv, qn, kn, perm):
    def body(ql, kvl, kp, c, s, qa_r, wq_r, kva_r, wkv_r, qn_r, kn_r, p_r, q_out, k_out, v_out):
        consts = _mla_consts_from_refs(qa_r, wq_r, kva_r, wkv_r, qn_r, kn_r, p_r)
        outs = _mla_prep_fn((ql[...], kvl[...], kp[...], c[...], s[...]), consts)
        for h in range(HEADS):
            q_n, q_r, k_n, k_r, v = outs[5 * h:5 * h + 5]
            q_out[:, h * QK_PAD:h * QK_PAD + HEAD] = q_n.astype(BF16)
            q_out[:, h * QK_PAD + HEAD:(h + 1) * QK_PAD] = q_r.astype(BF16)
            k_out[:, h * QK_PAD:h * QK_PAD + HEAD] = k_n.astype(BF16)
            k_out[:, h * QK_PAD + HEAD:(h + 1) * QK_PAD] = k_r.astype(BF16)
            v_out[:, h * HEAD:(h + 1) * HEAD] = v.astype(BF16)

    return _rows_call("mla_prep_fwd", body, [q_lat, kv_lat, k_pe, cos, sin], [qa, wq, kva, wkv, qn, kn, perm],
                      [(HEADS * QK_PAD, BF16), (HEADS * QK_PAD, BF16), (DN_WIDTH, BF16)], [], _row_tile(cos.shape[0]))


def _mla_prep_bwd(q_lat, kv_lat, k_pe, cos, sin, dq, dk, dv, qa, wq, kva, wkv, qn, kn, perm, host=None):
    def body(ql, kvl, kp, c, s, dq_r, dk_r, dv_r, qa_r, wq_r, kva_r, wkv_r, qn_r, kn_r, p_r,
             dql, dkvl, dkp, dqa, dwq, dkva, dwkv, dqn, dkn):
        consts = _mla_consts_from_refs(qa_r, wq_r, kva_r, wkv_r, qn_r, kn_r, p_r)
        cc, ss, pm = c[...], s[...], consts.pop("perm")
        _, vjp = jax.vjp(lambda rows, cs: _mla_prep_fn((*rows, cc, ss), dict(cs, perm=pm)), (ql[...], kvl[...], kp[...]),
                         consts)
        cts = []
        for h in range(HEADS):
            cts += [dq_r[:, h * QK_PAD:h * QK_PAD + HEAD], dq_r[:, h * QK_PAD + HEAD:(h + 1) * QK_PAD],
                    dk_r[:, h * QK_PAD:h * QK_PAD + HEAD], dk_r[:, h * QK_PAD + HEAD:(h + 1) * QK_PAD],
                    dv_r[:, h * HEAD:(h + 1) * HEAD]]
        (d_ql, d_kvl, d_kp), dc = vjp(tuple(cts))
        dql[...] = d_ql.astype(BF16)
        dkvl[...] = d_kvl.astype(BF16)
        dkp[...] = d_kp.astype(BF16)
        first = pl.program_id(0) == 0

        def acc(ref, sl, val):
            @pl.when(first)
            def _():
                ref[sl] = val

            @pl.when(jnp.logical_not(first))
            def _():
                ref[sl] += val

        full = (slice(None), slice(None))
        acc(dqa, full, dc["qa_w"])
        acc(dkva, full, dc["kva_w"])
        for h in range(HEADS):
            acc(dwq, (slice(h * QK_PAD, h * QK_PAD + HEAD), slice(None)), dc["wq_n"][h])
            acc(dwq, (slice(h * QK_PAD + HEAD, (h + 1) * QK_PAD), slice(None)), dc["wq_r"][h])
            acc(dwkv, (slice(h * QK_PAD, h * QK_PAD + HEAD), slice(None)), dc["wk_n"][h])
            acc(dwkv, (slice(h * QK_PAD + HEAD, (h + 1) * QK_PAD), slice(None)), dc["wv"][h])
        acc(dqn, (slice(None), slice(0, HEAD)), dc["qn_n"])
        acc(dqn, (slice(None), slice(HEAD, QK_PAD)), dc["qn_r"])
        acc(dkn, (slice(None), slice(0, HEAD)), dc["kn_n"])
        acc(dkn, (slice(None), slice(HEAD, QK_PAD)), dc["kn_r"])

    return _rows_call("mla_prep_bwd", body, [q_lat, kv_lat, k_pe, cos, sin, dq, dk, dv],
                      [qa, wq, kva, wkv, qn, kn, perm],
                      [(LORA, BF16), (LORA, BF16), (HEAD, BF16)],
                      [(1, LORA), wq.shape, (1, LORA), wkv.shape, (1, QK_PAD), (1, QK_PAD)], _row_tile(cos.shape[0]),
                      host=host)


ATTN_Q_ROWS = 256


def _attn_blocks(t):
    return [(r0, min(ATTN_Q_ROWS, t - r0)) for r0 in range(0, t, ATTN_Q_ROWS)]


def _attn_fwd(q, k, v, host=None):
    t = q.shape[0]

    def body(q_ref, k_ref, v_ref, o_ref):
        for r0, rows in _attn_blocks(t):
            ext = r0 + rows
            o_ref[r0:ext, :] = _attn_fn(q_ref[r0:ext, :], k_ref[0:ext, :], v_ref[0:ext, :], r0)

    qk_spec = pl.BlockSpec((t, QK_PAD), lambda h: (0, h))
    v_spec = pl.BlockSpec((t, HEAD), lambda h: (0, h))
    return _pcall(body, "attn_fwd", (HEADS,), [qk_spec, qk_spec, v_spec], v_spec,
                  jax.ShapeDtypeStruct((t, HEADS * HEAD), F32), [q, k, v], ("arbitrary",), host=host)


def _attn_bwd(q, k, v, do, host=None):
    t = q.shape[0]

    def body(q_ref, k_ref, v_ref, do_ref, dq_ref, dk_ref, dv_ref):
        dk_ref[...] = jnp.zeros(dk_ref.shape, F32)
        dv_ref[...] = jnp.zeros(dv_ref.shape, F32)
        for r0, rows in _attn_blocks(t):
            ext = r0 + rows
            _, vjp = jax.vjp(functools.partial(_attn_fn, row0=r0), q_ref[r0:ext, :].astype(F32),
                             k_ref[0:ext, :].astype(F32), v_ref[0:ext, :].astype(F32))
            dq, dk, dv = vjp(do_ref[r0:ext, :])
            dq_ref[r0:ext, :] = dq
            dk_ref[0:ext, :] += dk
            dv_ref[0:ext, :] += dv

    qk_spec = pl.BlockSpec((t, QK_PAD), lambda h: (0, h))
    v_spec = pl.BlockSpec((t, HEAD), lambda h: (0, h))
    return _pcall(body, "attn_bwd", (HEADS,), [qk_spec, qk_spec, v_spec, v_spec], [qk_spec, qk_spec, v_spec],
                  [jax.ShapeDtypeStruct((t, HEADS * QK_PAD), F32), jax.ShapeDtypeStruct((t, HEADS * QK_PAD), F32),
                   jax.ShapeDtypeStruct((t, HEADS * HEAD), F32)], [q, k, v, do], ("arbitrary",), host=host)


def _mix_out_fwd(o_mla, o_dn, z, w_mla, w_dn):
    def body(om_ref, od_ref, z_ref, wm_ref, wd_ref, o_ref):
        for h in range(HEADS):
            sl = slice(h * HEAD, (h + 1) * HEAD)
            o_ref[:, sl] = _rms(om_ref[:, sl], wm_ref[...], HEAD).astype(BF16)
            o_ref[:, DN_WIDTH + h * HEAD:DN_WIDTH + (h + 1) * HEAD] = _dn_out_fn(od_ref[:, sl], z_ref[:, sl],
                                                                                 wd_ref[...]).astype(BF16)

    return _rows_call("mix_out_fwd", body, [o_mla, o_dn, z], [w_mla, w_dn], [(2 * DN_WIDTH, BF16)], [],
                      _row_tile(o_mla.shape[0]))[0]


def _mix_out_bwd(o_mla, o_dn, z, dmixed, w_mla, w_dn, host=None):
    def body(om_ref, od_ref, z_ref, dm_ref, wm_ref, wd_ref, dom_ref, dod_ref, dz_ref, dwm_ref, dwd_ref):
        dwm = dwd = None
        for h in range(HEADS):
            sl = slice(h * HEAD, (h + 1) * HEAD)
            _, vjp = jax.vjp(lambda o, w: _rms(o, w, HEAD), om_ref[:, sl], wm_ref[...])
            do, dw = vjp(dm_ref[:, sl])
            dom_ref[:, sl] = do
            dwm = dw if dwm is None else dwm + dw
            _, vjp = jax.vjp(_dn_out_fn, od_ref[:, sl], z_ref[:, sl], wd_ref[...])
            do, dz, dw = vjp(dm_ref[:, DN_WIDTH + h * HEAD:DN_WIDTH + (h + 1) * HEAD])
            dod_ref[:, sl] = do
            dz_ref[:, sl] = dz.astype(BF16)
            dwd = dw if dwd is None else dwd + dw
        _accumulate(dwm_ref, dwm)
        _accumulate(dwd_ref, dwd)

    return _rows_call("mix_out_bwd", body, [o_mla, o_dn, z, dmixed], [w_mla, w_dn],
                      [(DN_WIDTH, F32), (DN_WIDTH, F32), (DN_WIDTH, BF16)], [(1, HEAD), (1, HEAD)],
                      _row_tile(o_mla.shape[0]), host=host)


def _shift_down(x, s):
    if s == 0:
        return x
    rows = lax.broadcasted_iota(jnp.int32, x.shape, 0)
    return jnp.where(rows >= s, pltpu.roll(x, s, 0), 0.0)


def _shift_up(x, s):
    if s == 0:
        return x
    t = x.shape[0]
    rows = lax.broadcasted_iota(jnp.int32, x.shape, 0)
    return jnp.where(rows < t - s, pltpu.roll(x, t - s, 0), 0.0)


def _col_call(name, body, cols, taps, outs, tap_outs, cw, host=None):
    t, c = cols[0].shape[0], taps[0].shape[1]
    in_specs = [pl.BlockSpec((t, cw), lambda j: (0, j)) for _ in cols]
    in_specs += [pl.BlockSpec((a.shape[0], cw), lambda j: (0, j)) for a in taps]
    out_shape = [jax.ShapeDtypeStruct((t, c), dt) for dt in outs] + [jax.ShapeDtypeStruct((n, c), F32) for n in tap_outs]
    out_specs = [pl.BlockSpec((t, cw), lambda j: (0, j)) for _ in outs]
    out_specs += [pl.BlockSpec((n, cw), lambda j: (0, j)) for n in tap_outs]
    return _pcall(body, name, (c // cw,), in_specs, out_specs, out_shape, [*cols, *taps], ("arbitrary",), host=host)


def _causal_conv(x, w_ref, width):
    acc = w_ref[width - 1:width, :] * x
    for j in range(width - 1):
        acc = acc + w_ref[j:j + 1, :] * _shift_down(x, width - 1 - j)
    return acc


def _causal_conv_bwd(x, dpre, w_ref, dx_ref, dw_ref, width):
    dx = w_ref[width - 1:width, :] * dpre
    dw_ref[width - 1:width, :] = jnp.sum(dpre * x, axis=0, keepdims=True)
    for j in range(width - 1):
        s = width - 1 - j
        dx = dx + w_ref[j:j + 1, :] * _shift_up(dpre, s)
        dw_ref[j:j + 1, :] = jnp.sum(dpre * _shift_down(x, s), axis=0, keepdims=True)
    dx_ref[...] = dx.astype(dx_ref.dtype)


def _dsilu(x):
    sg = jax.nn.sigmoid(x)
    return sg * (1.0 + x * (1.0 - sg))


def _dn_conv_fwd(x, w):
    def body(x_ref, w_ref, y_ref):
        y_ref[...] = _silu(_causal_conv(x_ref[...], w_ref, 4))

    return _col_call("dn_conv_fwd", body, [x], [w], [F32], [], 256)[0]


def _dn_conv_bwd(x, w, dy):
    def body(x_ref, dy_ref, w_ref, dx_ref, dw_ref):
        xv = x_ref[...]
        dpre = dy_ref[...] * _dsilu(_causal_conv(xv, w_ref, 4))
        _causal_conv_bwd(xv, dpre, w_ref, dx_ref, dw_ref, 4)

    return _col_call("dn_conv_bwd", body, [x, dy], [w], [BF16], [4], 256)


def _glu_fwd(gpre, up, w, b, host=None):
    def body(g_ref, u_ref, w_ref, b_ref, a_ref):
        gate = _causal_conv(g_ref[...], w_ref, 3) + b_ref[...]
        a_ref[...] = (_silu(gate) * u_ref[...]).astype(BF16)

    return _col_call("glu_fwd", body, [gpre, up], [w, b], [BF16], [], 256, host=host)[0]


def _glu_bwd(gpre, up, w, b, dact):
    def body(g_ref, u_ref, da_ref, w_ref, b_ref, dg_ref, du_ref, dw_ref, db_ref):
        gv = g_ref[...]
        gate = _causal_conv(gv, w_ref, 3) + b_ref[...]
        da = da_ref[...]
        sg = jax.nn.sigmoid(gate)
        du_ref[...] = (da * (gate * sg)).astype(BF16)
        dgate = da * u_ref[...] * (sg * (1.0 + gate * (1.0 - sg)))
        db_ref[...] = jnp.sum(dgate, axis=0, keepdims=True)
        _causal_conv_bwd(gv, dgate, w_ref, dg_ref, dw_ref, 3)

    return _col_call("glu_bwd", body, [gpre, up, dact], [w, b], [BF16, BF16], [3, 1], 256)


def _dn_prep_consts(sa, sb, al, dt):
    return dict(sel_a=sa[...], sel_b=sb[...], alog=al[...], dtb=dt[...])


def _dn_prep_fwd(conv, ab, sel_a, sel_b, alog, dtb):
    def body(c_ref, ab_ref, sa, sb, al, dt, q_out, k_out, g_out, b_out):
        qc = tuple(c_ref[:, h * HEAD:(h + 1) * HEAD] for h in range(HEADS))
        kc = tuple(c_ref[:, DN_WIDTH + h * HEAD:DN_WIDTH + (h + 1) * HEAD] for h in range(HEADS))
        qs, ks, g, beta = _dn_prep_fn((qc, kc, ab_ref[...]), _dn_prep_consts(sa, sb, al, dt))
        for h in range(HEADS):
            q_out[:, h * HEAD:(h + 1) * HEAD] = qs[h]
            k_out[:, h * HEAD:(h + 1) * HEAD] = ks[h]
        g_out[...] = g
        b_out[...] = beta

    return _rows_call("dn_prep_fwd", body, [conv, ab], [sel_a, sel_b, alog, dtb], [(DN_WIDTH, F32)] * 4, [],
                      _row_tile(conv.shape[0]))


def _dn_prep_bwd(conv, ab, dq, dk, dv, dg, db, sel_a, sel_b, alog, dtb):
    def body(c_ref, ab_ref, dq_r, dk_r, dv_r, dg_r, db_r, sa, sb, al, dt, dc_out, dab_out, dal_out, ddt_out):
        qc = tuple(c_ref[:, h * HEAD:(h + 1) * HEAD] for h in range(HEADS))
        kc = tuple(c_ref[:, DN_WIDTH + h * HEAD:DN_WIDTH + (h + 1) * HEAD] for h in range(HEADS))
        consts = _dn_prep_consts(sa, sb, al, dt)
        sel = dict(sel_a=consts["sel_a"], sel_b=consts["sel_b"])
        _, vjp = jax.vjp(lambda rows, ad: _dn_prep_fn(rows, {**sel, **ad}), (qc, kc, ab_ref[...]),
                         dict(alog=consts["alog"], dtb=consts["dtb"]))
        cq = tuple(dq_r[:, h * HEAD:(h + 1) * HEAD] for h in range(HEADS))
        ck = tuple(dk_r[:, h * HEAD:(h + 1) * HEAD] for h in range(HEADS))
        (dqc, dkc, dab), dad = vjp((cq, ck, dg_r[...], db_r[...]))
        for h in range(HEADS):
            dc_out[:, h * HEAD:(h + 1) * HEAD] = dqc[h]
            dc_out[:, DN_WIDTH + h * HEAD:DN_WIDTH + (h + 1) * HEAD] = dkc[h]
        dc_out[:, 2 * DN_WIDTH:3 * DN_WIDTH] = dv_r[...]
        dab_out[...] = dab.astype(BF16)
        _accumulate(dal_out, dad["alog"])
        _accumulate(ddt_out, dad["dtb"])

    return _rows_call("dn_prep_bwd", body, [conv, ab, dq, dk, dv, dg, db], [sel_a, sel_b, alog, dtb],
                      [(3 * DN_WIDTH, F32), (HEAD, BF16)], [(1, DN_WIDTH), (1, DN_WIDTH)], _row_tile(conv.shape[0]))


def _chunk_batch(t):
    nc = t // CHUNK
    return nc // 2 if nc % 2 == 0 else nc


def _dn_chunk_specs(t, nb):
    rows = nb * CHUNK
    blk = pl.BlockSpec((rows, HEAD), lambda h, b: (b, h))
    vblk = pl.BlockSpec((rows, HEAD), lambda h, b: (b, 2 * HEADS + h))
    mat = pl.BlockSpec((nb, HEAD, HEAD), lambda h, b: (b, h, 0))
    return rows, blk, vblk, mat


def _dn_chunk_fwd(qn, kn, conv, g, beta, host=None):
    t = qn.shape[0]
    nb = _chunk_batch(t)
    rows, blk, vblk, mat = _dn_chunk_specs(t, nb)

    def body(q_ref, k_ref, v_ref, g_ref, b_ref, n_o, b_o, qe_o, oo_o, eg_o):
        r3 = lambda x: x.reshape(nb, CHUNK, x.shape[-1])
        n_mat, b_mat, q_eff, o_own, eg = _dn_chunk_fn(r3(q_ref[...]), r3(k_ref[...]), r3(v_ref[...]), r3(g_ref[...]),
                                                      r3(g_ref[:, 0:CHUNK]), r3(b_ref[...]))
        n_o[...] = n_mat
        b_o[...] = b_mat
        qe_o[...] = q_eff.reshape(rows, HEAD)
        oo_o[...] = o_own.reshape(rows, HEAD)
        eg_o[...] = jnp.broadcast_to(eg, (nb, HEAD, HEAD))

    nc = t // CHUNK
    mats = jax.ShapeDtypeStruct((nc, DN_WIDTH, HEAD), F32)
    rowsd = jax.ShapeDtypeStruct((t, DN_WIDTH), F32)
    return _pcall(body, "dn_chunk_fwd", (HEADS, t // rows), [blk, blk, vblk, blk, blk], [mat, mat, blk, blk, mat],
                  [mats, mats, rowsd, rowsd, mats], [qn, kn, conv, g, beta], ("arbitrary", "arbitrary"), host=host)


def _dn_chunk_bwd(qn, kn, conv, g, beta, sall, gall, dq_eff, do, host=None):
    t = qn.shape[0]
    nb = _chunk_batch(t)
    rows, blk, vblk, mat = _dn_chunk_specs(t, nb)

    def body(q_ref, k_ref, v_ref, g_ref, b_ref, s_ref, ga_ref, dqe_ref, do_ref, dq_o, dk_o, dv_o, dg_o, db_o):
        r3 = lambda x: x.reshape(nb, CHUNK, x.shape[-1])
        _, vjp = jax.vjp(_dn_chunk_fn, r3(q_ref[...]), r3(k_ref[...]), r3(v_ref[...]), r3(g_ref[...]),
                         r3(g_ref[:, 0:CHUNK]), r3(b_ref[...]))
        s, ga = s_ref[...], ga_ref[...]
        d_n = -_bmm_nt(ga, s)
        d_eg = jnp.sum(ga * s, axis=1, keepdims=True)
        dq, dk, dv, dg, dg64, db = vjp((d_n, ga, r3(dqe_ref[...]), r3(do_ref[...]), d_eg))
        for o_ref, val in zip((dq_o, dk_o, dv_o, dg_o, db_o), (dq, dk, dv, dg, db)):
            o_ref[...] = val.reshape(rows, HEAD)
        dg_o[:, 0:CHUNK] += dg64.reshape(rows, CHUNK)

    return _pcall(body, "dn_chunk_bwd", (HEADS, t // rows), [blk, blk, vblk, blk, blk, mat, mat, blk, blk], [blk] * 5,
                  [jax.ShapeDtypeStruct((t, DN_WIDTH), F32)] * 5, [qn, kn, conv, g, beta, sall, gall, dq_eff, do],
                  ("arbitrary", "arbitrary"), host=host)


def _dn_rec_fwd(n_mat, b_mat, eg, host=None):
    nc = n_mat.shape[0]
    nb = _chunk_batch(nc * CHUNK)
    spec = pl.BlockSpec((nb, DN_WIDTH, HEAD), lambda i: (i, 0, 0))

    def body(n_ref, b_ref, eg_ref, sall_ref, s_scr):
        @pl.when(pl.program_id(0) == 0)
        def _():
            s_scr[...] = jnp.zeros(s_scr.shape, F32)

        for j in range(nb):
            sall_ref[j] = s_scr[...]
            for h in range(HEADS):
                sl = slice(h * HEAD, (h + 1) * HEAD)
                s_scr[sl, :] = _dn_rec_fn(s_scr[sl, :], n_ref[j, sl, :], b_ref[j, sl, :],
                                          eg_ref[j, h * HEAD:h * HEAD + 1, :])

    return _pcall(body, "dn_rec_fwd", (nc // nb,), [spec] * 3, spec, jax.ShapeDtypeStruct((nc, DN_WIDTH, HEAD), F32),
                  [n_mat, b_mat, eg], ("arbitrary",), scratch_shapes=[pltpu.VMEM((DN_WIDTH, HEAD), F32)], host=host)


def _dn_rec_bwd(n_mat, eg, ds_out, host=None):
    nc = n_mat.shape[0]
    nb = _chunk_batch(nc * CHUNK)
    steps = nc // nb
    spec = pl.BlockSpec((nb, DN_WIDTH, HEAD), lambda i: (steps - 1 - i, 0, 0))

    def body(n_ref, eg_ref, dso_ref, gall_ref, g_scr):
        @pl.when(pl.program_id(0) == 0)
        def _():
            g_scr[...] = jnp.zeros(g_scr.shape, F32)

        for j in reversed(range(nb)):
            gall_ref[j] = g_scr[...]
            for h in range(HEADS):
                sl = slice(h * HEAD, (h + 1) * HEAD)
                gv = g_scr[sl, :]
                g_scr[sl, :] = (gv * eg_ref[j, h * HEAD:h * HEAD + 1, :] - _mm_tn(n_ref[j, sl, :], gv)
                                + dso_ref[j, sl, :])

    return _pcall(body, "dn_rec_bwd", (steps,), [spec] * 3, spec, jax.ShapeDtypeStruct((nc, DN_WIDTH, HEAD), F32),
                  [n_mat, eg, ds_out], ("arbitrary",), scratch_shapes=[pltpu.VMEM((DN_WIDTH, HEAD), F32)], host=host)


def _dn_o_fwd(sall, q_eff, o_own):
    t = q_eff.shape[0]
    nb = _chunk_batch(t)
    rows, blk, _, mat = _dn_chunk_specs(t, nb)

    def body(s_ref, qe_ref, oo_ref, o_ref):
        r3 = lambda x: x.reshape(nb, CHUNK, HEAD)
        o_ref[...] = _dn_o_fn(s_ref[...], r3(qe_ref[...]), r3(oo_ref[...])).reshape(rows, HEAD)

    return _pcall(body, "dn_o_fwd", (HEADS, t // rows), [mat, blk, blk], blk, jax.ShapeDtypeStruct((t, DN_WIDTH), F32),
                  [sall, q_eff, o_own], ("arbitrary", "arbitrary"))


def _dn_o_bwd(sall, q_eff, do, host=None):
    t = q_eff.shape[0]
    nb = _chunk_batch(t)
    rows, blk, _, mat = _dn_chunk_specs(t, nb)

    def body(s_ref, qe_ref, do_ref, dqe_ref, ds_ref):
        r3 = lambda x: x.reshape(nb, CHUNK, HEAD)
        dov = r3(do_ref[...])
        dqe_ref[...] = _bmm_nt(dov, s_ref[...]).reshape(rows, HEAD)
        ds_ref[...] = _bmm_tn(r3(qe_ref[...]), dov)

    nc = t // CHUNK
    return _pcall(body, "dn_o_bwd", (HEADS, t // rows), [mat, blk, blk], [blk, mat],
                  [jax.ShapeDtypeStruct((t, DN_WIDTH), F32), jax.ShapeDtypeStruct((nc, DN_WIDTH, HEAD), F32)],
                  [sall, q_eff, do], ("arbitrary", "arbitrary"), host=host)


def _loss_call(h2, tgt, n_valid):
    t, n = h2.shape
    r = _row_tile(t)

    def body(h_ref, t_ref, dy_ref, dy16_ref, acc_ref):
        rows = pl.program_id(0) * r + lax.broadcasted_iota(jnp.int32, (r, n), 0)
        valid = jnp.logical_and(rows >= N_META, rows < n_valid)
        e = jnp.where(valid, h_ref[...] - t_ref[...], 0.0)
        dy = e * (1.0 / n)
        dy_ref[...] = dy
        dy16_ref[...] = dy.astype(BF16)
        _accumulate(acc_ref, jnp.sum(e * e, axis=0, keepdims=True))

    return _rows_call("loss", body, [h2, tgt], [], [(n, F32), (n, BF16)], [(1, n)], r)


def _adamw_call(name, w, g, m, v, host=None):
    rows, cols = w.shape
    by_rows = rows % 8 == 0

    def body(w_ref, g_ref, m_ref, v_ref, g_out, d_ref, m_out, v_out):
        gv = g_ref[...] if by_rows else g_ref[0:rows, :]
        m2 = ADAM_B1 * m_ref[...] + (1.0 - ADAM_B1) * gv
        v2 = ADAM_B2 * v_ref[...] + (1.0 - ADAM_B2) * (gv * gv)
        m_hat = m2 / (1.0 - ADAM_B1 ** ADAM_STEP)
        v_hat = v2 / (1.0 - ADAM_B2 ** ADAM_STEP)
        g_out[...] = gv
        d_ref[...] = -ADAM_LR * (m_hat / (jnp.sqrt(v_hat) + ADAM_EPS) + ADAM_WD * w_ref[...])
        m_out[...] = m2
        v_out[...] = v2

    if by_rows:
        tr = _tile(rows, 256, 8)
        spec = g_spec = pl.BlockSpec((tr, cols), lambda i: (i, 0))
        grid = (rows // tr,)
    else:
        tc = _tile(cols, 256, 128)
        spec = pl.BlockSpec((rows, tc), lambda j: (0, j))
        g_spec = pl.BlockSpec((g.shape[0], tc), lambda j: (0, j))
        grid = (cols // tc,)
    return _pcall(body, name, grid, [spec, g_spec, spec, spec], [spec] * 4, [jax.ShapeDtypeStruct((rows, cols), F32)] * 4,
                  [w, g, m, v], ("arbitrary",), host=host)


def _rope_tables(t):
    half = ROPE // 2
    inv_freq = np.float32(ROPE_THETA) ** (-np.arange(half, dtype=np.float32) / np.float32(half))
    ang = np.arange(t, dtype=np.float32)[:, None] * inv_freq[None, :].astype(np.float32)
    z = np.zeros((t, HEAD - ROPE), np.float32)
    cos = np.concatenate([np.cos(ang), np.cos(ang), z], axis=1).astype(np.float32)
    sin = np.concatenate([np.sin(ang), np.sin(ang), z], axis=1).astype(np.float32)
    k = np.arange(HEAD)[:, None]
    l = np.arange(HEAD)[None, :]
    perm = np.where((l < half) & (k == l + half), -1.0, 0.0) + np.where((l >= half) & (l < ROPE) & (k == l - half), 1.0, 0.0)
    return jnp.asarray(cos), jnp.asarray(sin), jnp.asarray(perm.astype(np.float32))


def _win_to_pad(w):
    z = lambda n: jnp.zeros((n, w.shape[1]), w.dtype)
    return jnp.concatenate([w[576:2112], w[2112:2624], w[0:256], w[256:512], w[512:576], z(64), w[2624:2632], z(120)],
                           axis=0)


def _win_from_pad(g):
    return jnp.concatenate([g[2048:2304], g[2304:2560], g[2560:2624], g[0:1536], g[1536:2048], g[2688:2696]], axis=0)


def _qk_to_pad(w):
    w4 = w.reshape(HEADS, QK_DIM, w.shape[-1])
    return jnp.concatenate([w4, jnp.zeros((HEADS, QK_PAD - QK_DIM, w.shape[-1]), w.dtype)], axis=1).reshape(
        HEADS * QK_PAD, w.shape[-1])


def _qk_from_pad(g):
    return g.reshape(HEADS, QK_PAD, g.shape[-1])[:, :QK_DIM].reshape(HEADS * QK_DIM, g.shape[-1])


def _ff_to_pad(a, axis):
    shape = list(a.shape)
    shape[axis:axis + 1] = [N_CHIPS, FF_SHARD]
    a4 = a.reshape(shape)
    shape[axis + 1] = FF_BLOCK - FF_SHARD
    out = jnp.concatenate([a4, jnp.zeros(shape, a.dtype)], axis=axis + 1)
    shape[axis:axis + 2] = [D_FF_P]
    return out.reshape(shape)


def _ff_from_pad(a, axis):
    shape = list(a.shape)
    shape[axis:axis + 1] = [N_CHIPS, FF_BLOCK]
    a4 = lax.slice_in_dim(a.reshape(shape), 0, FF_SHARD, axis=axis + 1)
    shape[axis:axis + 2] = [D_FF]
    return a4.reshape(shape)


class _LocalPlan:
    def __init__(self, wt):
        self.wt, self.grads = wt, {}

    def weight(self, name):
        return self.wt[name]

    def host(self, point):
        return None

    def grad(self, name, value):
        self.grads[name] = value


def _local_step(x, tgt, wt, plan=None):
    plan = _LocalPlan(wt) if plan is None else plan
    s = x.shape[0]
    n_valid = N_META + s
    t = -(-n_valid // HEAD) * HEAD
    zpad = jnp.zeros((t - n_valid, D_MODEL), F32)
    h0 = jnp.concatenate([wt["meta_tokens"], x, zpad], axis=0)
    tgt_p = jnp.concatenate([jnp.zeros((N_META, D_MODEL), F32), tgt, zpad], axis=0)
    cos, sin, perm = _rope_tables(t)
    win, wq, wkv = wt["w_in_t"], wt["w_q_t"], wt["w_kv_t"]
    qn_w = jnp.concatenate([wt["q_norm_w"], jnp.zeros((1, QK_PAD - QK_DIM), F32)], axis=1)
    kn_w = jnp.concatenate([wt["k_norm_w"], jnp.zeros((1, QK_PAD - QK_DIM), F32)], axis=1)
    head_id = jnp.arange(DN_WIDTH)[None, :] // HEAD
    lane = jnp.arange(HEAD)[:, None]
    sel_a = (lane == head_id).astype(F32)
    sel_b = (lane == head_id + HEADS).astype(F32)
    alog = jnp.repeat(wt["dn_A_log"], HEAD, axis=1)
    dtb = jnp.repeat(wt["dn_dt_bias"], HEAD, axis=1)
    conv_w, conv_b = wt["ffn_conv_w"], wt["ffn_conv_b"]

    u = _rms_fwd("attn_norm_fwd", h0, wt["attn_norm_w"])
    proj = _matmul("in_proj", u, win, "nt", F32)
    z = (proj, DN_WIDTH, 3)
    q_lat, kv_lat, k_pe, ab = (proj, LORA, 8), (proj, LORA, 9), (proj, HEAD, 20), (proj, HEAD, 21)
    mla_consts = (wt["q_a_norm_w"], wq, wt["kv_a_norm_w"], wkv, qn_w, kn_w, perm)
    q, k, v = _mla_prep_fwd(q_lat, kv_lat, k_pe, cos, sin, *mla_consts)
    o_mla = _attn_fwd(q, k, v, host=plan.host("attn_fwd"))
    conv = _dn_conv_fwd(proj, wt["dn_conv_w"])
    dn_consts = (sel_a, sel_b, alog, dtb)
    qn, kn, g, beta = _dn_prep_fwd(conv, ab, *dn_consts)
    n_mat, b_mat, q_eff, o_own, eg = _dn_chunk_fwd(qn, kn, conv, g, beta, host=plan.host("dn_chunk_fwd"))
    sall = _dn_rec_fwd(n_mat, b_mat, eg)
    o_dn = _dn_o_fwd(sall, q_eff, o_own)
    mixed = _mix_out_fwd(o_mla, o_dn, z, wt["mla_out_norm_w"], wt["dn_out_norm_w"])
    w_out = plan.weight("w_out")
    h1 = _matmul("out_proj", mixed, w_out, "nn", F32, res=h0)
    n2 = _rms_fwd("ffn_norm_fwd", h1, wt["ffn_norm_w"])
    w_gate, w_up = plan.weight("w_gate_t"), plan.weight("w_up_t")
    gpre = _matmul("gate_proj", n2, w_gate, "nt", F32, host=plan.host("gate_proj"))
    up = _matmul("up_proj", n2, w_up, "nt", F32, host=plan.host("up_proj"))
    act = _glu_fwd(gpre, up, conv_w, conv_b)
    w_down = plan.weight("w_down")
    h2 = _matmul("down_proj", act, w_down, "nn", F32, res=h1)
    dy, dy16, sq = _loss_call(h2, tgt_p, n_valid)

    grads = {}
    dact = _matmul("down_dx", dy16, w_down, "nt", F32)
    plan.grad("w_down", _matmul("down_dw", act, dy16, "tn", F32))
    dgpre, dup, grads["ffn_conv_w"], grads["ffn_conv_b"] = _glu_bwd(gpre, up, conv_w, conv_b, dact)
    plan.grad("w_gate_t", _matmul("gate_dw", dgpre, n2, "tn", F32))
    plan.grad("w_up_t", _matmul("up_dw", dup, n2, "tn", F32))
    dn2a = _matmul("gate_dx", dgpre, w_gate, "nn", F32, host=plan.host("gate_dx"))
    dn2b = _matmul("up_dx", dup, w_up, "nn", F32, host=plan.host("up_dx"))
    dh1, dh1_16, grads["ffn_norm_w"] = _rms_bwd("ffn_norm_bwd", h1, wt["ffn_norm_w"], [dn2a, dn2b], dy)
    dmixed = _matmul("out_dx", dh1_16, w_out, "nt", F32)
    plan.grad("w_out", _matmul("out_dw", mixed, dh1_16, "tn", F32))
    do_mla, do_dn, dz, grads["mla_out_norm_w"], grads["dn_out_norm_w"] = _mix_out_bwd(
        o_mla, o_dn, z, dmixed, wt["mla_out_norm_w"], wt["dn_out_norm_w"], host=plan.host("mix_out_bwd"))
    dq_eff, ds_out = _dn_o_bwd(sall, q_eff, do_dn)
    gall = _dn_rec_bwd(n_mat, eg, ds_out)
    dqn, dkn, dv_dn, dg, dbeta = _dn_chunk_bwd(qn, kn, conv, g, beta, sall, gall, dq_eff, do_dn,
                                               host=plan.host("dn_chunk_bwd"))
    dconv, dab, dalog, ddtb = _dn_prep_bwd(conv, ab, dqn, dkn, dv_dn, dg, dbeta, *dn_consts)
    grads["dn_A_log"] = jnp.sum(dalog.reshape(HEADS, HEAD), axis=1)[None, :]
    grads["dn_dt_bias"] = jnp.sum(ddtb.reshape(HEADS, HEAD), axis=1)[None, :]
    ddn_pre, grads["dn_conv_w"] = _dn_conv_bwd(proj, wt["dn_conv_w"], dconv)
    dq, dk, dv = _attn_bwd(q, k, v, do_mla, host=plan.host("attn_bwd"))
    dq_lat, dkv_lat, dk_pe, dqa, dwq, dkva, dwkv, dqnw, dknw = _mla_prep_bwd(
        q_lat, kv_lat, k_pe, cos, sin, dq, dk, dv, *mla_consts, host=plan.host("mla_prep_bwd"))
    grads["q_a_norm_w"], grads["kv_a_norm_w"] = dqa, dkva
    plan.grad("w_q_t", dwq)
    plan.grad("w_kv_t", dwkv)
    grads["q_norm_w"], grads["k_norm_w"] = dqnw[:, :QK_DIM], dknw[:, :QK_DIM]
    dproj = jnp.concatenate([ddn_pre, dz, dq_lat, dkv_lat, dk_pe, dab], axis=1)
    plan.grad("w_in_t", _matmul("in_dw", dproj, u, "tn", F32))
    du = _matmul("in_dx", dproj, win, "nn", F32, host=plan.host("in_dx"))
    dh0, _, grads["attn_norm_w"] = _rms_bwd("attn_norm_bwd", h0, wt["attn_norm_w"], [du], dh1,
                                            host=plan.host("attn_norm_bwd"))
    grads["meta_tokens"] = dh0[0:N_META]
    if isinstance(plan, _LocalPlan):
        grads.update(plan.grads)
    return sq, dh0[N_META:n_valid], grads


def _mesh_pos():
    return lax.axis_index("x"), lax.axis_index("y"), lax.axis_index("c")


def _other_chips(x, y):
    return [(1 - x, y), (x, 1 - y), (1 - x, 1 - y)]


def _remote(src, dst, send_sems, recv_sems, k, to):
    return pltpu.make_async_remote_copy(src_ref=src, dst_ref=dst, send_sem=send_sems.at[k], recv_sem=recv_sems.at[k],
                                        device_id=to, device_id_type=MESH)


SIBLING_ID, CHIPS_ID, GATHER_ID, ALL_ID = 1, 2, 3, 4


def _sibling_peer():
    x, y, c = _mesh_pos()
    return [(x, y, 1 - c)]


def _chip_peers():
    x, y, c = _mesh_pos()
    return [(qx, qy, c) for qx, qy in _other_chips(x, y)]


def _copies_exchange(make, ins, out_shape, nsem, peers=None, cid=None):
    def prog(in_refs, out_refs, send_sems, recv_sems):
        copies = make(in_refs, out_refs, send_sems, recv_sems)

        def start():
            for cp in copies:
                cp.start()

        def finish():
            for cp in copies:
                cp.wait()

        return start, finish

    return _Exchange(prog, ins, out_shape, nsem, peers, cid)


def _all_gather(shards):
    def prog(srcs, dsts, send_sems, recv_sems):
        x, y, c = _mesh_pos()
        p = 2 * x + y
        sibling = (x, y, 1 - c)
        chips = _other_chips(x, y)
        bufs = tuple((s, d, s.shape[0] // 2) for s, d in zip(srcs, dsts))

        def half(ref, rows, which):
            return ref.at[pl.ds(which * rows, rows), :]

        def copy(i, k, src, dst, to):
            return _remote(src, dst, send_sems, recv_sems, 6 * i + k, to)

        sends = [copy(i, j, half(src, rows, c), half(dst.at[p], rows, c), (*chip, c))
                 for i, (src, dst, rows) in enumerate(bufs) for j, chip in enumerate(chips)]

        def start():
            for cp in sends:
                cp.start()

        def finish():
            passed = []
            for i, (src, dst, rows) in enumerate(bufs):
                for j, (qx, qy) in enumerate(chips):
                    block = half(dst.at[2 * qx + qy], rows, c)
                    copy(i, j, block, block, (x, y, c)).wait_recv()
                    fwd = copy(i, 3 + j, block, block, sibling)
                    fwd.start()
                    passed.append(fwd)
            for i, (src, dst, rows) in enumerate(bufs):
                for j, (qx, qy) in enumerate(chips):
                    block = half(dst.at[2 * qx + qy], rows, 1 - c)
                    copy(i, 3 + j, block, block, (x, y, c)).wait_recv()
            for cp in sends + passed:
                cp.wait_send()

        return start, finish

    return _Exchange(prog, shards, [jax.ShapeDtypeStruct((N_CHIPS, *s.shape), s.dtype) for s in shards], 6 * len(shards),
                     lambda: _sibling_peer() + _chip_peers(), GATHER_ID)


def _gathered(ex):
    p = 2 * lax.axis_index("x") + lax.axis_index("y")
    return [lax.dynamic_update_slice(g, s[None], (p, 0, 0)) for g, s in zip(ex.outs, ex.ins)]


def _rs_to_sibling(bufs):
    def make(srcs, dsts, send_sems, recv_sems):
        x, y, c = _mesh_pos()
        copies = []
        for i, (src, dst) in enumerate(zip(srcs, dsts)):
            half = src.shape[1] // 2
            copies.append(_remote(src.at[:, pl.ds((1 - c) * half, half), :], dst, send_sems, recv_sems, i, (x, y, 1 - c)))
        return copies

    return _copies_exchange(make, bufs, [jax.ShapeDtypeStruct((N_CHIPS, b.shape[1] // 2, b.shape[2]), F32) for b in bufs],
                            len(bufs), _sibling_peer, SIBLING_ID)


def _rs_pair_add(name, bufs, gots, c, out_dtype):
    n = len(bufs)

    def body(c_ref, *refs):
        for a_ref, b_ref, o_ref in zip(refs[:n], refs[n:2 * n], refs[2 * n:]):
            o_ref[...] = (a_ref[...] + b_ref[...]).astype(out_dtype)

    mine = [pl.BlockSpec((None, g.shape[1], g.shape[2]), lambda j, cr: (j, cr[0], 0)) for g in gots]
    whole = [pl.BlockSpec((None, g.shape[1], g.shape[2]), lambda j, cr: (j, 0, 0)) for g in gots]
    return pl.pallas_call(
        body, name=name,
        grid_spec=pltpu.PrefetchScalarGridSpec(num_scalar_prefetch=1, grid=(N_CHIPS,), in_specs=mine + whole, out_specs=whole),
        out_shape=[jax.ShapeDtypeStruct(g.shape, out_dtype) for g in gots],
        compiler_params=_cparams(("arbitrary",)))(c, *bufs, *gots)


def _rs_to_chips(accs):
    def make(srcs, dsts, send_sems, recv_sems):
        x, y, c = _mesh_pos()
        return [_remote(src.at[2 * qx + qy], dst.at[k], send_sems, recv_sems, 3 * i + k, (qx, qy, c))
                for i, (src, dst) in enumerate(zip(srcs, dsts)) for k, (qx, qy) in enumerate(_other_chips(x, y))]

    return _copies_exchange(make, accs, [jax.ShapeDtypeStruct((3, a.shape[1], a.shape[2]), a.dtype) for a in accs],
                            3 * len(accs), _chip_peers, CHIPS_ID)


def _rs_chip_add(name, accs, gots, p):
    n = len(accs)
    slot = (0, 1, 0, 2)

    def body(p_ref, *refs):
        me = p_ref[0]
        for own_ref, got_ref, o_ref in zip(refs[:n], refs[n:2 * n], refs[2 * n:]):
            total = None
            for chip in range(N_CHIPS):
                val = own_ref[...].astype(F32)
                for e in (1, 2, 3):
                    val = jnp.where((chip ^ me) == e, got_ref[slot[e]].astype(F32), val)
                total = val if total is None else total + val
            o_ref[...] = total

    own = [pl.BlockSpec((None, a.shape[1], a.shape[2]), lambda i, pr: (pr[0], 0, 0)) for a in accs]
    got = [pl.BlockSpec(g.shape, lambda i, pr: (0, 0, 0)) for g in gots]
    out = [pl.BlockSpec((a.shape[1], a.shape[2]), lambda i, pr: (0, 0)) for a in accs]
    return pl.pallas_call(
        body, name=name,
        grid_spec=pltpu.PrefetchScalarGridSpec(num_scalar_prefetch=1, grid=(1,), in_specs=own + got, out_specs=out),
        out_shape=[jax.ShapeDtypeStruct((a.shape[1], a.shape[2]), F32) for a in accs],
        compiler_params=_cparams(("arbitrary",)))(p, *accs, *gots)


def _rs_share(ress):
    def make(srcs, dsts, send_sems, recv_sems):
        x, y, c = _mesh_pos()
        return [_remote(src, dst, send_sems, recv_sems, i, (x, y, 1 - c)) for i, (src, dst) in enumerate(zip(srcs, dsts))]

    return _copies_exchange(make, ress, [jax.ShapeDtypeStruct(r.shape, F32) for r in ress], len(ress), _sibling_peer,
                            SIBLING_ID)


def _shared(ex):
    south = lax.axis_index("c") == 0
    return [jnp.concatenate([jnp.where(south, r, g), jnp.where(south, g, r)], axis=0) for r, g in zip(ex.ins, ex.outs)]


def _all_to_all_devices(vec):
    def others():
        x, y, c = _mesh_pos()
        return [((1 - x if r & 4 else x), (1 - y if r & 2 else y), (1 - c if r & 1 else c)) for r in range(1, 8)]

    def make(srcs, dsts, send_sems, recv_sems):
        x, y, c = _mesh_pos()
        me = 4 * x + 2 * y + c
        return [_remote(srcs[0], dsts[0].at[me], send_sems, recv_sems, r, peer) for r, peer in enumerate(others())]

    return _copies_exchange(make, [vec], [jax.ShapeDtypeStruct((8, *vec.shape), vec.dtype)], 7, others, ALL_ID)


def _sum_devices(stack):
    def body(s_ref, o_ref):
        total = s_ref[0]
        for d in range(1, 8):
            total = total + s_ref[d]
        o_ref[...] = total

    return pl.pallas_call(body, name="sum_devices", out_shape=jax.ShapeDtypeStruct(stack.shape[1:], F32),
                          compiler_params=pltpu.CompilerParams(vmem_limit_bytes=VMEM_LIMIT))(stack)


def _pad_rows(flat, rows):
    return jnp.concatenate([flat, jnp.zeros((rows * LANES - flat.shape[0],), flat.dtype)]).reshape(rows, LANES)


def _unshard(g4, shape, axis):
    a = g4.reshape(N_CHIPS, *shape)
    if axis == 0:
        return a.reshape(N_CHIPS * shape[0], shape[1])
    return jnp.transpose(a, (1, 0, 2)).reshape(shape[0], N_CHIPS * shape[1])


def _shard4(full, shape, axis):
    if axis == 0:
        return full.reshape(N_CHIPS, shape[0] * shape[1])
    a = full.reshape(shape[0], N_CHIPS, shape[1])
    return jnp.transpose(a, (1, 0, 2)).reshape(N_CHIPS, shape[0] * shape[1])


def _pad_axis0(a, rows):
    return jnp.concatenate([a, jnp.zeros((rows - a.shape[0], *a.shape[1:]), a.dtype)], axis=0)


def _pad_axis1(a, rows):
    return jnp.concatenate([a, jnp.zeros((a.shape[0], rows - a.shape[1], *a.shape[2:]), a.dtype)], axis=1)


def _shard_to_strip(name, w):
    _, (shape, axis, rows) = name, {n: (s, ax, r) for n, s, ax, r in BIG}[name]
    w2 = w.reshape(shape).astype(BF16)
    return _pad_axis0(w2.T if axis == 1 else w2, rows)


LOCAL_NAME = dict(w_in="w_in_t", w_q_b="w_q_t", w_kv_b="w_kv_t", w_out="w_out", w_gate="w_gate_t", w_up="w_up_t",
                  w_down="w_down")


WIN_SEGMENTS = ((576, 2112, 0), (2112, 2624, 1536), (0, 256, 2048), (256, 512, 2304), (512, 576, 2560), (2624, 2632, 2688))


def _strips_to_weight(name, g4):
    if name == "w_in":
        return _win_to_pad(g4[:, :IN_SHARD].reshape(IN_COLS, D_MODEL))
    if name == "w_q_b":
        return _qk_to_pad(g4.reshape(HEADS * QK_DIM, LORA))
    return g4.reshape(N_CHIPS * g4.shape[1], g4.shape[2])


def _grad_to_strips(name, g):
    if name == "w_in":
        strips = []
        for q in range(N_CHIPS):
            pieces = []
            for a, b, local in sorted(WIN_SEGMENTS):
                s, e = max(a, q * IN_SHARD), min(b, (q + 1) * IN_SHARD)
                if s < e:
                    pieces.append(g[local + s - a:local + e - a])
            pieces.append(jnp.zeros((IN_SHARD_P - IN_SHARD, D_MODEL), g.dtype))
            strips.append(jnp.concatenate(pieces, axis=0))
        return jnp.stack(strips)
    if name == "w_q_b":
        return _qk_from_pad(g).reshape(N_CHIPS, QK_DIM, LORA)
    return g.reshape(N_CHIPS, g.shape[0] // N_CHIPS, g.shape[1])


class _MeshPlan:
    LATE = dict(attn_fwd=("w_up",), dn_chunk_fwd=("w_out", "w_gate"), gate_proj=("w_down/0",), up_proj=("w_down/1",))
    GROUP_A = ("w_down", "w_gate", "w_up", "w_out")
    GROUP_B = ("w_in", "w_q_b", "w_kv_b")

    def __init__(self, w):
        x, y, c = _mesh_pos()
        self.ci = jnp.reshape(c, (1,)).astype(jnp.int32)
        self.pi = jnp.reshape(2 * x + y, (1,)).astype(jnp.int32)
        self.strip = {n: _shard_to_strip(n, w[n]) for n, _, _, _ in BIG}
        self.gathers, self.weights, self.g, self.acc, self.reduced = {}, {}, {}, {}, {}
        self.sibs, self.sib, self.chip, self.share, self.halves = [], None, None, None, [None, None]

    def gather_first(self, small):
        names = ("w_in", "w_q_b", "w_kv_b")
        ex = _all_gather([self.strip[n] for n in names] + [small])
        ex.run("all_gather_first")
        got = _gathered(ex)
        for n, g4 in zip(names, got):
            self.weights[LOCAL_NAME[n]] = _strips_to_weight(n, g4)
        return got[-1]

    def weight(self, local_name):
        if local_name not in self.weights:
            for point, (names, ex) in list(self.gathers.items()):
                if ex.outs is not None:
                    for n, g4 in zip(names, _gathered(ex)):
                        if "/" in n:
                            n, half = n.split("/")
                            self.halves[int(half)] = g4
                            if None in self.halves:
                                continue
                            g4 = jnp.concatenate(self.halves, axis=1)
                        self.weights[LOCAL_NAME[n]] = _strips_to_weight(n, g4)
                    del self.gathers[point]
        return self.weights[local_name]

    def _shard(self, name):
        if "/" not in name:
            return self.strip[name]
        name, half = name.split("/")
        rows = self.strip[name].shape[0] // 2
        return self.strip[name][int(half) * rows:(int(half) + 1) * rows]

    def grad(self, local_name, value):
        name = {v: k for k, v in LOCAL_NAME.items()}[local_name]
        self.g[name] = _grad_to_strips(name, value)

    def _pair_add(self, names, gots):
        accs = _rs_pair_add("rs_pair_add_" + names[0], [self.g[n] for n in names], gots, self.ci, BF16)
        self.acc.update(zip(names, accs))

    def _chip_add(self, names, chip):
        return _rs_chip_add("rs_chip_add_" + names[0], [self.acc[n] for n in names], chip.outs, self.pi)

    def _take_shared(self, names, share):
        for n, strip in zip(names, _shared(share)):
            self.reduced[n] = strip

    def host(self, point):
        a, b = self.GROUP_A, self.GROUP_B
        if point in self.LATE:
            names = self.LATE[point]
            ex = _all_gather([self._shard(n) for n in names])
            self.gathers[point] = (names, ex)
            return ex
        if point in ("gate_dx", "up_dx", "mix_out_bwd"):
            names = dict(gate_dx=a[:2], up_dx=a[2:3], mix_out_bwd=a[3:])[point]
            ex = _rs_to_sibling([self.g[n] for n in names])
            self.sibs.append(ex)
            return ex
        if point == "dn_chunk_bwd":
            self._pair_add(a, [o for ex in self.sibs for o in ex.outs])
            self.chip1 = _rs_to_chips([self.acc[n] for n in a[:2]])
            return self.chip1
        if point == "attn_bwd":
            self.chip2 = _rs_to_chips([self.acc[n] for n in a[2:]])
            return self.chip2
        if point == "mla_prep_bwd":
            ress = self._chip_add(a[:2], self.chip1) + self._chip_add(a[2:], self.chip2)
            self.share = _rs_share(ress)
            return self.share
        if point == "in_dx":
            self._take_shared(a, self.share)
            self.sib = _rs_to_sibling([self.g[n] for n in b])
            return self.sib
        if point == "attn_norm_bwd":
            self._pair_add(b, self.sib.outs)
            self.chip = _rs_to_chips([self.acc[n] for n in b])
            return self.chip
        return None

    def last_share(self):
        self.share = _rs_share(self._chip_add(self.GROUP_B, self.chip))
        return self.share

    def finish(self):
        self._take_shared(self.GROUP_B, self.share)
        return self.reduced


def _strip_to_shard(name, strip):
    shape, axis = {n: (s, ax) for n, s, ax, _ in BIG}[name]
    rows = shape[axis]
    return strip[:rows].T if axis == 1 else strip[:rows]


def kernel(x, meta_tokens, attn_norm_w, w_in, q_a_norm_w, w_q_b, kv_a_norm_w, w_kv_b, q_norm_w, k_norm_w, mla_out_norm_w, dn_conv_w, dn_A_log, dn_dt_bias, dn_out_norm_w, w_out, ffn_norm_w, w_gate, w_up, ffn_conv_w, ffn_conv_b, w_down, loss_target, m_meta_tokens, m_attn_norm_w, m_w_in, m_q_a_norm_w, m_w_q_b, m_kv_a_norm_w, m_w_kv_b, m_q_norm_w, m_k_norm_w, m_mla_out_norm_w, m_dn_conv_w, m_dn_A_log, m_dn_dt_bias, m_dn_out_norm_w, m_w_out, m_ffn_norm_w, m_w_gate, m_w_up, m_ffn_conv_w, m_ffn_conv_b, m_w_down, v_meta_tokens, v_attn_norm_w, v_w_in, v_q_a_norm_w, v_w_q_b, v_kv_a_norm_w, v_w_kv_b, v_q_norm_w, v_k_norm_w, v_mla_out_norm_w, v_dn_conv_w, v_dn_A_log, v_dn_dt_bias, v_dn_out_norm_w, v_w_out, v_ffn_norm_w, v_w_gate, v_w_up, v_ffn_conv_w, v_ffn_conv_b, v_w_down):
    local = dict(locals())
    w = {n: local[n] for n in WEIGHTS}
    m = {n: local["m_" + n] for n in WEIGHTS}
    v = {n: local["v_" + n] for n in WEIGHTS}
    p = 2 * lax.axis_index("x") + lax.axis_index("y")

    plan = _MeshPlan(w)
    wf = _pad_rows(jnp.concatenate([w[n].reshape(-1) for n, _, _ in SMALL_SHARDED]), SMALL_ROWS)
    gf = plan.gather_first(wf).reshape(N_CHIPS, -1)
    full = dict(plan.weights)
    off = 0
    for n, s, ax in SMALL_SHARDED:
        full[n] = _unshard(gf[:, off:off + s[0] * s[1]], s, ax)
        off += s[0] * s[1]
    for n, _ in REPLICATED:
        full[n] = w[n]
    full["ffn_conv_w"] = _ff_to_pad(full["ffn_conv_w"], 1)
    full["ffn_conv_b"] = _ff_to_pad(full["ffn_conv_b"], 1)

    sq, grad_x, g = _local_step(x[0], loss_target[0], full, plan)
    g["ffn_conv_w"] = _ff_from_pad(g["ffn_conv_w"], 1)
    g["ffn_conv_b"] = _ff_from_pad(g["ffn_conv_b"], 1)

    small_all = [n for n, _, _ in SMALL_SHARDED] + [n for n, _ in REPLICATED]
    vec = jnp.concatenate([g[n].reshape(-1) for n in small_all] + [jnp.reshape(0.5 / D_MODEL * jnp.sum(sq), (1,))])
    vec = _pad_rows(vec, -(-vec.shape[0] // (8 * LANES)) * 8)
    a2a = _all_to_all_devices(vec)

    gs, delta, new_m, new_v = {}, {}, {}, {}
    big = {n: (s, ax) for n, s, ax, _ in BIG}

    def adamw_big(n, strips, host=None):
        s, ax = big[n]
        flip = ax == 1 and s[1] % 8 == 0
        there = (lambda a: a.reshape(s).T) if flip else (lambda a: a.reshape(s))
        back = (lambda a: a.T.reshape(w[n].shape)) if flip else (lambda a: a.reshape(w[n].shape))
        strip = strips[n] if flip or ax == 0 else strips[n][:s[1]].T
        g2, d2, m2, v2 = _adamw_call("adamw_" + n, there(w[n]), strip, there(m[n]), there(v[n]), host=host)
        gs[n], delta[n], new_m[n], new_v[n] = back(g2), back(d2), back(m2), back(v2)

    adamw_big("w_down", plan.reduced, host=a2a)
    adamw_big("w_gate", plan.reduced, host=plan.last_share())
    adamw_big("w_up", plan.reduced)
    adamw_big("w_out", plan.reduced)
    strips = plan.finish()
    for n in plan.GROUP_B:
        adamw_big(n, strips)
    me = 4 * lax.axis_index("x") + 2 * lax.axis_index("y") + lax.axis_index("c")
    red = _sum_devices(lax.dynamic_update_slice(a2a.outs[0], vec[None], (me, 0, 0))).reshape(-1)
    off = 0
    for n in small_all:
        tot = red[off:off + g[n].size].reshape(g[n].shape)
        off += g[n].size
        shard = {sn: (s, ax) for sn, s, ax in SMALL_SHARDED}.get(n)
        if shard is not None:
            tot = lax.dynamic_slice_in_dim(tot, p * shard[0][1], shard[0][1], axis=1)
        gs[n] = tot
    loss = red[off]
    small_names = [n for n, _, _ in SMALL_SHARDED] + [n for n, _ in REPLICATED]
    rows = SMALL_ROWS + REP_ROWS
    pack = lambda d: _pad_rows(jnp.concatenate([d[n].reshape(-1) for n in small_names]), rows)
    _, d2, m2, v2 = _adamw_call("adamw_small", pack(w), pack(gs), pack(m), pack(v))
    off = 0
    for n in small_names:
        cnt = w[n].size
        for dst, src in ((delta, d2), (new_m, m2), (new_v, v2)):
            dst[n] = src.reshape(-1)[off:off + cnt].reshape(w[n].shape)
        off += cnt

    grad_out = [gs[n].reshape(w[n].shape) for n in WEIGHTS]
    return (loss, grad_x[None], *grad_out, *[delta[n] for n in WEIGHTS], *[new_m[n] for n in WEIGHTS],
            *[new_v[n] for n in WEIGHTS])
```

```python
import functools
import math

import jax
import jax.numpy as jnp
import numpy as np
from jax import lax
from jax.experimental import pallas as pl
from jax.experimental.pallas import tpu as pltpu

F32 = jnp.float32
BF16 = jnp.bfloat16
HI = lax.Precision.HIGHEST
MESH = pl.DeviceIdType.MESH

N_META = 16
D_MODEL = 1024
HEADS = 4
HEAD = 128
ROPE = 64
QK_DIM = HEAD + ROPE
QK_PAD = 2 * HEAD
LORA = 256
DN_WIDTH = HEADS * HEAD
CHUNK = 64
D_FF = 2816
N_CHIPS = 4
FF_SHARD = D_FF // N_CHIPS
FF_BLOCK = 768
D_FF_P = N_CHIPS * FF_BLOCK
IN_COLS = 2632
IN_SHARD = IN_COLS // N_CHIPS
IN_SHARD_P = 672
IN_PAD = 2816
NORM_EPS = 1e-6
ROPE_THETA = 10000.0
LANES = 512

ADAM_LR, ADAM_B1, ADAM_B2, ADAM_EPS, ADAM_WD, ADAM_STEP = 0.001, 0.9, 0.999, 1e-08, 0.01, 10

VMEM_LIMIT = 56 * 1024 * 1024

BIG = (("w_in", (1024, 658), 1, IN_SHARD_P), ("w_q_b", (256, 192), 1, 192), ("w_kv_b", (256, 256), 1, 256),
       ("w_out", (256, 1024), 0, 256), ("w_gate", (1024, 704), 1, FF_BLOCK), ("w_up", (1024, 704), 1, FF_BLOCK),
       ("w_down", (704, 1024), 0, FF_BLOCK))
SMALL_SHARDED = (("meta_tokens", (16, 256), 1), ("dn_conv_w", (4, 384), 1), ("ffn_conv_w", (3, 704), 1))
REPLICATED = (("attn_norm_w", 1024), ("q_a_norm_w", 256), ("kv_a_norm_w", 256), ("q_norm_w", 192), ("k_norm_w", 192),
              ("mla_out_norm_w", 128), ("dn_A_log", 4), ("dn_dt_bias", 4), ("dn_out_norm_w", 128), ("ffn_norm_w", 1024),
              ("ffn_conv_b", 2816))
WEIGHTS = ("meta_tokens", "attn_norm_w", "w_in", "q_a_norm_w", "w_q_b", "kv_a_norm_w", "w_kv_b", "q_norm_w", "k_norm_w",
           "mla_out_norm_w", "dn_conv_w", "dn_A_log", "dn_dt_bias", "dn_out_norm_w", "w_out", "ffn_norm_w", "w_gate",
           "w_up", "ffn_conv_w", "ffn_conv_b", "w_down")

SMALL_ROWS = 16
REP_ROWS = 16


def _cparams(sem):
    return pltpu.CompilerParams(dimension_semantics=sem, vmem_limit_bytes=VMEM_LIMIT)


class _Exchange:
    def __init__(self, prog, ins, out_shape, nsem, peers=None, cid=None):
        self.prog, self.ins, self.out_shape, self.nsem = prog, list(ins), list(out_shape), nsem
        self.peers, self.cid = peers, cid
        self.outs = None

    def sems(self):
        return [pltpu.SemaphoreType.DMA((self.nsem,)), pltpu.SemaphoreType.DMA((self.nsem,))]

    def programs(self, in_refs, out_refs, send_sems, recv_sems):
        start, finish = self.prog(in_refs, out_refs, send_sems, recv_sems)
        if self.cid is None:
            return start, finish
        peers = self.peers()

        def shake_and_start():
            barrier = pltpu.get_barrier_semaphore()
            for peer in peers:
                pl.semaphore_signal(barrier, inc=1, device_id=peer, device_id_type=MESH)
            pl.semaphore_wait(barrier, len(peers))
            start()

        return shake_and_start, finish

    def cparams(self, **kw):
        return pltpu.CompilerParams(has_side_effects=True, collective_id=self.cid, **kw)

    def run(self, name):
        any_spec = pl.BlockSpec(memory_space=pl.ANY)
        n = len(self.ins)

        def body(*refs):
            start, finish = self.programs(refs[:n], refs[n:-2], refs[-2], refs[-1])
            start()
            finish()

        self.outs = pl.pallas_call(
            body, name=name, in_specs=[any_spec] * n, out_specs=[any_spec] * len(self.out_shape),
            out_shape=self.out_shape, scratch_shapes=self.sems(), compiler_params=self.cparams())(*self.ins)
        return self.outs


def _pcall(body, name, grid, in_specs, out_specs, out_shape, args, sem, scratch_shapes=(), host=None):
    single = not isinstance(out_shape, (list, tuple))
    out_specs, out_shape = ([out_specs], [out_shape]) if single else (list(out_specs), list(out_shape))
    if host is None:
        outs = pl.pallas_call(body, name=name, grid=grid, in_specs=list(in_specs), out_specs=out_specs, out_shape=out_shape,
                              scratch_shapes=list(scratch_shapes), compiler_params=_cparams(sem))(*args)
        return outs[0] if single else outs
    any_spec = pl.BlockSpec(memory_space=pl.ANY)
    n_in, n_out, n_scr, nx_in, nx_out = len(in_specs), len(out_specs), len(scratch_shapes), len(host.ins), len(host.out_shape)

    def hosted(*refs):
        c_in, x_in = refs[:n_in], refs[n_in:n_in + nx_in]
        o0 = n_in + nx_in
        c_out, x_out = refs[o0:o0 + n_out], refs[o0 + n_out:o0 + n_out + nx_out]
        s0 = o0 + n_out + nx_out
        start, finish = host.programs(x_in, x_out, refs[s0 + n_scr], refs[s0 + n_scr + 1])
        first = functools.reduce(jnp.logical_and, [pl.program_id(d) == 0 for d in range(len(grid))])
        last = functools.reduce(jnp.logical_and, [pl.program_id(d) == grid[d] - 1 for d in range(len(grid))])
        pl.when(first)(start)
        body(*c_in, *c_out, *refs[s0:s0 + n_scr])
        pl.when(last)(finish)

    outs = pl.pallas_call(
        hosted, name=name, grid=grid, in_specs=list(in_specs) + [any_spec] * nx_in,
        out_specs=out_specs + [any_spec] * nx_out, out_shape=out_shape + host.out_shape,
        scratch_shapes=list(scratch_shapes) + host.sems(),
        compiler_params=host.cparams(dimension_semantics=sem, vmem_limit_bytes=VMEM_LIMIT))(*args, *host.ins)
    host.outs = outs[n_out:]
    return outs[0] if single else outs[:n_out]


NN, NT, TN = ((1,), (0,)), ((1,), (1,)), ((0,), (0,))


def _shift_dims(dims, batch):
    if not batch:
        return (dims, ((), ()))
    return (((dims[0][0] + 1,), (dims[1][0] + 1,)), ((0,), (0,)))


def _make_mm(dims, exact, batch=False):
    def raw(a, b, d):
        dn = _shift_dims(d, batch)
        if exact == "split_lhs":
            ah, bh = a.astype(BF16), b.astype(BF16)
            al = (a - ah.astype(F32)).astype(BF16)
            return lax.dot_general(ah, bh, dn, preferred_element_type=F32) + lax.dot_general(al, bh, dn,
                                                                                              preferred_element_type=F32)
        if exact == "split":
            ah, bh = a.astype(BF16), b.astype(BF16)
            al, bl = (a - ah.astype(F32)).astype(BF16), (b - bh.astype(F32)).astype(BF16)
            dot = lambda p, q: lax.dot_general(p, q, dn, preferred_element_type=F32)
            return dot(ah, bh) + (dot(ah, bl) + dot(al, bh))
        if exact:
            return lax.dot_general(a.astype(F32), b.astype(F32), dn, precision=HI, preferred_element_type=F32)
        return lax.dot_general(a.astype(BF16), b.astype(BF16), dn, preferred_element_type=F32)

    @jax.custom_vjp
    def mm(a, b):
        return raw(a, b, dims)

    def fwd(a, b):
        return raw(a, b, dims), (a, b)

    def bwd(res, g):
        a, b = res
        if dims == NN:
            da, db = raw(g, b, NT), raw(a, g, TN)
        elif dims == NT:
            da, db = raw(g, b, NN), raw(g, a, TN)
        else:
            da, db = raw(b, g, NT), raw(a, g, NN)
        return da.astype(a.dtype), db.astype(b.dtype)

    mm.defvjp(fwd, bwd)
    return mm


_mm = _make_mm(NN, False)
_mm_nt = _make_mm(NT, False)
_mm_tn = _make_mm(TN, False)
_mmx = _make_mm(NN, "split_lhs")
_bmm = _make_mm(NN, False, batch=True)
_bmm_nt = _make_mm(NT, False, batch=True)
_bmm_tn = _make_mm(TN, False, batch=True)
_bmmx = _make_mm(NN, True, batch=True)
_bmms = _make_mm(NN, "split", batch=True)
_bmms_nt = _make_mm(NT, "split", batch=True)
_bmms_tn = _make_mm(TN, "split", batch=True)


@jax.custom_vjp
def _unit_lower_inv(a):
    n = a.shape[-1]
    eye = (lax.broadcasted_iota(jnp.int32, a.shape, 1) == lax.broadcasted_iota(jnp.int32, a.shape, 2)).astype(F32)
    x = -a
    t = eye + x
    for _ in range(max(n.bit_length() - 2, 0)):
        x = _bmms(x, x)
        t = t + _bmms(t, x)
    return t


def _unit_lower_inv_fwd(a):
    t = _unit_lower_inv(a)
    return t, t


def _unit_lower_inv_bwd(t, g):
    return (-_bmms_tn(t, _bmms_nt(g, t)),)


_unit_lower_inv.defvjp(_unit_lower_inv_fwd, _unit_lower_inv_bwd)


def _rms(x, w, n):
    ms = jnp.sum(x * x, axis=-1, keepdims=True) * (1.0 / n)
    return x * lax.rsqrt(ms + NORM_EPS) * w


def _silu(x):
    return x * jax.nn.sigmoid(x)


def _softplus(x):
    return jnp.maximum(x, 0.0) + jnp.log(1.0 + jnp.exp(-jnp.abs(x)))


def _rope(x, cos, sin, perm):
    return x * cos + _mmx(x, perm) * sin


def _mla_prep_fn(rows, consts):
    q_lat, kv_lat, k_pe, cos, sin = rows
    qn = _rms(q_lat, consts["qa_w"], LORA)
    kvn = _rms(kv_lat, consts["kva_w"], LORA)
    outs = []
    for h in range(HEADS):
        q_n = _mm_nt(qn, consts["wq_n"][h])
        q_r = _mm_nt(qn, consts["wq_r"][h])
        rs = lax.rsqrt((jnp.sum(q_n * q_n, -1, keepdims=True) + jnp.sum(q_r * q_r, -1, keepdims=True)) * (1.0 / QK_DIM)
                       + NORM_EPS)
        q_n = q_n * rs * consts["qn_n"]
        q_r = _rope(q_r * rs * consts["qn_r"], cos, sin, consts["perm"])
        k_n = _mm_nt(kvn, consts["wk_n"][h])
        v = _mm_nt(kvn, consts["wv"][h])
        rk = lax.rsqrt((jnp.sum(k_n * k_n, -1, keepdims=True) + jnp.sum(k_pe * k_pe, -1, keepdims=True)) * (1.0 / QK_DIM)
                       + NORM_EPS)
        k_n = k_n * rk * consts["kn_n"]
        k_r = _rope(k_pe * rk * consts["kn_r"], cos, sin, consts["perm"])
        outs += [q_n, q_r, k_n, k_r, v]
    return tuple(outs)


def _attn_fn(q, k, v, row0):
    s = _mm_nt(q, k) * (1.0 / math.sqrt(QK_DIM))
    qpos = row0 + lax.broadcasted_iota(jnp.int32, s.shape, 0)
    kpos = lax.broadcasted_iota(jnp.int32, s.shape, 1)
    s = jnp.where(kpos <= qpos, s, -1e30)
    m = lax.stop_gradient(jnp.max(s, axis=-1, keepdims=True))
    p = jnp.exp(s - m)
    p = p / jnp.sum(p, axis=-1, keepdims=True)
    return _mm(p, v)


def _dn_prep_fn(rows, consts):
    qc, kc, ab = rows
    a_b = _mmx(ab, consts["sel_a"])
    b_b = _mmx(ab, consts["sel_b"])
    beta = jax.nn.sigmoid(b_b)
    g = -jnp.exp(consts["alog"]) * _softplus(a_b + consts["dtb"])
    qs, ks = [], []
    for h in range(HEADS):
        q, k = qc[h], kc[h]
        qs.append(q * lax.rsqrt(jnp.sum(q * q, -1, keepdims=True) + NORM_EPS))
        ks.append(k * lax.rsqrt(jnp.sum(k * k, -1, keepdims=True) + NORM_EPS))
    return tuple(qs), tuple(ks), g, beta


def _dn_chunk_fn(q, k, v, gb, g64, bb):
    nb = q.shape[0]
    ri = lax.broadcasted_iota(jnp.int32, (nb, CHUNK, CHUNK), 1)
    ci = lax.broadcasted_iota(jnp.int32, (nb, CHUNK, CHUNK), 2)
    tri = ri >= ci
    strict = ri > ci
    tril = tri.astype(F32)
    eye = (ri == ci).astype(F32)
    ones = jnp.ones((nb, CHUNK, CHUNK), F32)
    gc = _bmmx(tril, gb)
    gc64 = _bmmx(tril, g64)
    grow = _bmmx(ones, eye * gc64)
    diff = gc64 - grow
    decay = jnp.where(tri, jnp.exp(jnp.where(tri, diff, 0.0)), 0.0)
    kb = k * bb
    vb = v * bb
    a = jnp.where(strict, _bmm_nt(kb, k) * decay, 0.0)
    tinv = _unit_lower_inv(a)
    u = _bmm(tinv, vb)
    w = _bmm(tinv, kb * jnp.exp(gc))
    qs = q * (1.0 / math.sqrt(HEAD))
    qk = _bmm_nt(qs, k) * decay
    qg = qs * jnp.exp(gc)
    glast = jnp.sum(gb, axis=1, keepdims=True)
    kdec = k * jnp.exp(glast - gc)
    n_mat = _bmm_tn(kdec, w)
    b_mat = _bmm_tn(kdec, u)
    q_eff = qg - _bmm(qk, w)
    o_own = _bmm(qk, u)
    return n_mat, b_mat, q_eff, o_own, jnp.exp(glast)


def _dn_rec_fn(s, n_mat, b_mat, eg):
    return s * eg - _mm(n_mat, s) + b_mat


def _dn_o_fn(s, q_eff, o_own):
    return _bmm(q_eff, s) + o_own


def _dn_out_fn(o, z, w):
    return _rms(o, w, HEAD) * _silu(z)


def _row_tile(t, parts=8):
    return t // parts if (t // parts) % 16 == 0 else t


def _tile(n, pref, unit):
    best = n
    for cand in range(unit, min(n, pref) + 1, unit):
        if n % cand == 0:
            best = cand
    return best if best <= pref else n


def _rows_call(name, body, rows, consts, outs, accs, r, host=None):
    rows = [a if isinstance(a, tuple) else (a, a.shape[1], 0) for a in rows]
    t = rows[0][0].shape[0]
    zero = lambda nd: (lambda i: (0,) * nd)
    in_specs = [pl.BlockSpec((r, w), functools.partial(lambda i, b: (i, b), b=blk)) for _, w, blk in rows]
    rows = [a for a, _, _ in rows]
    in_specs += [pl.BlockSpec(a.shape, zero(a.ndim)) for a in consts]
    out_shape = [jax.ShapeDtypeStruct((t, w), dt) for w, dt in outs] + [jax.ShapeDtypeStruct(s, F32) for s in accs]
    out_specs = [pl.BlockSpec((r, w), lambda i: (i, 0)) for w, _ in outs] + [pl.BlockSpec(s, zero(len(s))) for s in accs]
    return _pcall(body, name, (t // r,), in_specs, out_specs, out_shape, [*rows, *consts], ("arbitrary",), host=host)


def _accumulate(ref, val):
    @pl.when(pl.program_id(0) == 0)
    def _():
        ref[...] = jnp.zeros(ref.shape, ref.dtype)

    ref[...] += val


def _matmul(name, a, b, dims, out_dtype, res=None, host=None):
    if dims == "nn":
        (m, k), n = a.shape, b.shape[1]
    elif dims == "nt":
        (m, k), n = a.shape, b.shape[0]
    else:
        (k, m), n = a.shape, b.shape[1]
    tm = _tile(m, 1100, 16) if dims != "tn" else _tile(m, 640, 128)
    tn = _tile(n, 1408, 128)
    if dims == "nn":
        a_spec, b_spec, dn = pl.BlockSpec((tm, k), lambda i, j: (i, 0)), pl.BlockSpec((k, tn), lambda i, j: (0, j)), NN
    elif dims == "nt":
        a_spec, b_spec, dn = pl.BlockSpec((tm, k), lambda i, j: (i, 0)), pl.BlockSpec((tn, k), lambda i, j: (j, 0)), NT
    else:
        a_spec, b_spec, dn = pl.BlockSpec((k, tm), lambda i, j: (0, i)), pl.BlockSpec((k, tn), lambda i, j: (0, j)), TN
    o_spec = pl.BlockSpec((tm, tn), lambda i, j: (i, j))

    def body(*refs):
        a_ref, b_ref, o_ref = refs[0], refs[1], refs[-1]
        acc = lax.dot_general(a_ref[...].astype(BF16), b_ref[...].astype(BF16), (dn, ((), ())),
                              preferred_element_type=F32)
        if res is not None:
            acc = acc + refs[2][...]
        o_ref[...] = acc.astype(out_dtype)

    ins = [a, b] + ([res] if res is not None else [])
    specs = [a_spec, b_spec] + ([o_spec] if res is not None else [])
    return _pcall(body, name, (m // tm, n // tn), specs, o_spec, jax.ShapeDtypeStruct((m, n), out_dtype), ins,
                  ("arbitrary", "arbitrary"), host=host)


def _rms_fwd(name, h, w):
    n = h.shape[1]

    def body(h_ref, w_ref, o_ref):
        o_ref[...] = _rms(h_ref[...], w_ref[...], n).astype(BF16)

    return _rows_call(name, body, [h], [w], [(n, BF16)], [], _row_tile(h.shape[0]))[0]


def _rms_bwd(name, h, w, cts, resid, host=None):
    n = h.shape[1]
    nct = len(cts)

    def body(*refs):
        h_ref, ct_refs, r_ref, w_ref = refs[0], refs[1:1 + nct], refs[1 + nct], refs[2 + nct]
        dh_ref, dh16_ref, dw_ref = refs[-3], refs[-2], refs[-1]
        ct = ct_refs[0][...].astype(F32)
        for c in ct_refs[1:]:
            ct = ct + c[...].astype(F32)
        _, vjp = jax.vjp(lambda x, ww: _rms(x, ww, n), h_ref[...], w_ref[...])
        dh, dw = vjp(ct)
        dh = dh + r_ref[...]
        dh_ref[...] = dh
        dh16_ref[...] = dh.astype(BF16)
        _accumulate(dw_ref, dw)

    return _rows_call(name, body, [h, *cts, resid], [w], [(n, F32), (n, BF16)], [(1, n)], _row_tile(h.shape[0]), host=host)


def _mla_consts_from_refs(qa, wq, kva, wkv, qn, kn, perm):
    f = lambda r: r[...].astype(F32)
    return dict(
        qa_w=f(qa), kva_w=f(kva), perm=f(perm),
        wq_n=[wq[h * QK_PAD:h * QK_PAD + HEAD, :].astype(F32) for h in range(HEADS)],
        wq_r=[wq[h * QK_PAD + HEAD:(h + 1) * QK_PAD, :].astype(F32) for h in range(HEADS)],
        wk_n=[wkv[h * QK_PAD:h * QK_PAD + HEAD, :].astype(F32) for h in range(HEADS)],
        wv=[wkv[h * QK_PAD + HEAD:(h + 1) * QK_PAD, :].astype(F32) for h in range(HEADS)],
        qn_n=qn[:, 0:HEAD], qn_r=qn[:, HEAD:QK_PAD], kn_n=kn[:, 0:HEAD], kn_r=kn[:, HEAD:QK_PAD])


def _mla_prep_fwd(q_lat, kv_lat, k_pe, cos, sin, qa, wq, kva, wkv, qn, kn, perm):
    def body(ql, kvl, kp, c, s, qa_r, wq_r, kva_r, wkv_r, qn_r, kn_r, p_r, q_out, k_out, v_out):
        consts = _mla_consts_from_refs(qa_r, wq_r, kva_r, wkv_r, qn_r, kn_r, p_r)
        outs = _mla_prep_fn((ql[...], kvl[...], kp[...], c[...], s[...]), consts)
        for h in range(HEADS):
            q_n, q_r, k_n, k_r, v = outs[5 * h:5 * h + 5]
            q_out[:, h * QK_PAD:h * QK_PAD + HEAD] = q_n.astype(BF16)
            q_out[:, h * QK_PAD + HEAD:(h + 1) * QK_PAD] = q_r.astype(BF16)
            k_out[:, h * QK_PAD:h * QK_PAD + HEAD] = k_n.astype(BF16)
            k_out[:, h * QK_PAD + HEAD:(h + 1) * QK_PAD] = k_r.astype(BF16)
            v_out[:, h * HEAD:(h + 1) * HEAD] = v.astype(BF16)

    return _rows_call("mla_prep_fwd", body, [q_lat, kv_lat, k_pe, cos, sin], [qa, wq, kva, wkv, qn, kn, perm],
                      [(HEADS * QK_PAD, BF16), (HEADS * QK_PAD, BF16), (DN_WIDTH, BF16)], [], _row_tile(cos.shape[0], 4))


def _mla_prep_bwd(q_lat, kv_lat, k_pe, cos, sin, dq, dk, dv, qa, wq, kva, wkv, qn, kn, perm, host=None):
    def body(ql, kvl, kp, c, s, dq_r, dk_r, dv_r, qa_r, wq_r, kva_r, wkv_r, qn_r, kn_r, p_r,
             dql, dkvl, dkp, dqa, dwq, dkva, dwkv, dqn, dkn):
        consts = _mla_consts_from_refs(qa_r, wq_r, kva_r, wkv_r, qn_r, kn_r, p_r)
        cc, ss, pm = c[...], s[...], consts.pop("perm")
        _, vjp = jax.vjp(lambda rows, cs: _mla_prep_fn((*rows, cc, ss), dict(cs, perm=pm)), (ql[...], kvl[...], kp[...]),
                         consts)
        cts = []
        for h in range(HEADS):
            cts += [dq_r[:, h * QK_PAD:h * QK_PAD + HEAD], dq_r[:, h * QK_PAD + HEAD:(h + 1) * QK_PAD],
                    dk_r[:, h * QK_PAD:h * QK_PAD + HEAD], dk_r[:, h * QK_PAD + HEAD:(h + 1) * QK_PAD],
                    dv_r[:, h * HEAD:(h + 1) * HEAD]]
        (d_ql, d_kvl, d_kp), dc = vjp(tuple(cts))
        dql[...] = d_ql.astype(BF16)
        dkvl[...] = d_kvl.astype(BF16)
        dkp[...] = d_kp.astype(BF16)
        first = pl.program_id(0) == 0

        def acc(ref, sl, val):
            @pl.when(first)
            def _():
                ref[sl] = val

            @pl.when(jnp.logical_not(first))
            def _():
                ref[sl] += val

        full = (slice(None), slice(None))
        acc(dqa, full, dc["qa_w"])
        acc(dkva, full, dc["kva_w"])
        for h in range(HEADS):
            acc(dwq, (slice(h * QK_PAD, h * QK_PAD + HEAD), slice(None)), dc["wq_n"][h])
            acc(dwq, (slice(h * QK_PAD + HEAD, (h + 1) * QK_PAD), slice(None)), dc["wq_r"][h])
            acc(dwkv, (slice(h * QK_PAD, h * QK_PAD + HEAD), slice(None)), dc["wk_n"][h])
            acc(dwkv, (slice(h * QK_PAD + HEAD, (h + 1) * QK_PAD), slice(None)), dc["wv"][h])
        acc(dqn, (slice(None), slice(0, HEAD)), dc["qn_n"])
        acc(dqn, (slice(None), slice(HEAD, QK_PAD)), dc["qn_r"])
        acc(dkn, (slice(None), slice(0, HEAD)), dc["kn_n"])
        acc(dkn, (slice(None), slice(HEAD, QK_PAD)), dc["kn_r"])

    return _rows_call("mla_prep_bwd", body, [q_lat, kv_lat, k_pe, cos, sin, dq, dk, dv],
                      [qa, wq, kva, wkv, qn, kn, perm],
                      [(LORA, BF16), (LORA, BF16), (HEAD, BF16)],
                      [(1, LORA), wq.shape, (1, LORA), wkv.shape, (1, QK_PAD), (1, QK_PAD)], _row_tile(cos.shape[0], 4),
                      host=host)


ATTN_Q_ROWS = 256


def _attn_blocks(t):
    return [(r0, min(ATTN_Q_ROWS, t - r0)) for r0 in range(0, t, ATTN_Q_ROWS)]


def _attn_fwd(q, k, v, host=None):
    t = q.shape[0]

    def body(q_ref, k_ref, v_ref, o_ref):
        for r0, rows in _attn_blocks(t):
            ext = r0 + rows
            o_ref[r0:ext, :] = _attn_fn(q_ref[r0:ext, :], k_ref[0:ext, :], v_ref[0:ext, :], r0)

    qk_spec = pl.BlockSpec((t, QK_PAD), lambda h: (0, h))
    v_spec = pl.BlockSpec((t, HEAD), lambda h: (0, h))
    return _pcall(body, "attn_fwd", (HEADS,), [qk_spec, qk_spec, v_spec], v_spec,
                  jax.ShapeDtypeStruct((t, HEADS * HEAD), F32), [q, k, v], ("arbitrary",), host=host)


def _attn_bwd(q, k, v, do, host=None):
    t = q.shape[0]

    def body(q_ref, k_ref, v_ref, do_ref, dq_ref, dk_ref, dv_ref):
        dk_ref[...] = jnp.zeros(dk_ref.shape, F32)
        dv_ref[...] = jnp.zeros(dv_ref.shape, F32)
        for r0, rows in _attn_blocks(t):
            ext = r0 + rows
            _, vjp = jax.vjp(functools.partial(_attn_fn, row0=r0), q_ref[r0:ext, :].astype(F32),
                             k_ref[0:ext, :].astype(F32), v_ref[0:ext, :].astype(F32))
            dq, dk, dv = vjp(do_ref[r0:ext, :])
            dq_ref[r0:ext, :] = dq
            dk_ref[0:ext, :] += dk
            dv_ref[0:ext, :] += dv

    qk_spec = pl.BlockSpec((t, QK_PAD), lambda h: (0, h))
    v_spec = pl.BlockSpec((t, HEAD), lambda h: (0, h))
    return _pcall(body, "attn_bwd", (HEADS,), [qk_spec, qk_spec, v_spec, v_spec], [qk_spec, qk_spec, v_spec],
                  [jax.ShapeDtypeStruct((t, HEADS * QK_PAD), F32), jax.ShapeDtypeStruct((t, HEADS * QK_PAD), F32),
                   jax.ShapeDtypeStruct((t, HEADS * HEAD), F32)], [q, k, v, do], ("arbitrary",), host=host)


def _mix_out_fwd(o_mla, o_dn, z, w_mla, w_dn):
    def body(om_ref, od_ref, z_ref, wm_ref, wd_ref, o_ref):
        for h in range(HEADS):
            sl = slice(h * HEAD, (h + 1) * HEAD)
            o_ref[:, sl] = _rms(om_ref[:, sl], wm_ref[...], HEAD).astype(BF16)
            o_ref[:, DN_WIDTH + h * HEAD:DN_WIDTH + (h + 1) * HEAD] = _dn_out_fn(od_ref[:, sl], z_ref[:, sl],
                                                                                 wd_ref[...]).astype(BF16)

    return _rows_call("mix_out_fwd", body, [o_mla, o_dn, z], [w_mla, w_dn], [(2 * DN_WIDTH, BF16)], [],
                      _row_tile(o_mla.shape[0]))[0]


def _mix_out_bwd(o_mla, o_dn, z, dmixed, w_mla, w_dn, host=None):
    def body(om_ref, od_ref, z_ref, dm_ref, wm_ref, wd_ref, dom_ref, dod_ref, dz_ref, dwm_ref, dwd_ref):
        dwm = dwd = None
        for h in range(HEADS):
            sl = slice(h * HEAD, (h + 1) * HEAD)
            _, vjp = jax.vjp(lambda o, w: _rms(o, w, HEAD), om_ref[:, sl], wm_ref[...])
            do, dw = vjp(dm_ref[:, sl])
            dom_ref[:, sl] = do
            dwm = dw if dwm is None else dwm + dw
            _, vjp = jax.vjp(_dn_out_fn, od_ref[:, sl], z_ref[:, sl], wd_ref[...])
            do, dz, dw = vjp(dm_ref[:, DN_WIDTH + h * HEAD:DN_WIDTH + (h + 1) * HEAD])
            dod_ref[:, sl] = do
            dz_ref[:, sl] = dz.astype(BF16)
            dwd = dw if dwd is None else dwd + dw
        _accumulate(dwm_ref, dwm)
        _accumulate(dwd_ref, dwd)

    return _rows_call("mix_out_bwd", body, [o_mla, o_dn, z, dmixed], [w_mla, w_dn],
                      [(DN_WIDTH, F32), (DN_WIDTH, F32), (DN_WIDTH, BF16)], [(1, HEAD), (1, HEAD)],
                      _row_tile(o_mla.shape[0]), host=host)


def _shift_down(x, s):
    if s == 0:
        return x
    rows = lax.broadcasted_iota(jnp.int32, x.shape, 0)
    return jnp.where(rows >= s, pltpu.roll(x, s, 0), 0.0)


def _shift_up(x, s):
    if s == 0:
        return x
    t = x.shape[0]
    rows = lax.broadcasted_iota(jnp.int32, x.shape, 0)
    return jnp.where(rows < t - s, pltpu.roll(x, t - s, 0), 0.0)


def _col_call(name, body, cols, taps, outs, tap_outs, cw, host=None):
    t, c = cols[0].shape[0], taps[0].shape[1]
    in_specs = [pl.BlockSpec((t, cw), lambda j: (0, j)) for _ in cols]
    in_specs += [pl.BlockSpec((a.shape[0], cw), lambda j: (0, j)) for a in taps]
    out_shape = [jax.ShapeDtypeStruct((t, c), dt) for dt in outs] + [jax.ShapeDtypeStruct((n, c), F32) for n in tap_outs]
    out_specs = [pl.BlockSpec((t, cw), lambda j: (0, j)) for _ in outs]
    out_specs += [pl.BlockSpec((n, cw), lambda j: (0, j)) for n in tap_outs]
    return _pcall(body, name, (c // cw,), in_specs, out_specs, out_shape, [*cols, *taps], ("arbitrary",), host=host)


def _causal_conv(x, w_ref, width):
    acc = w_ref[width - 1:width, :] * x
    for j in range(width - 1):
        acc = acc + w_ref[j:j + 1, :] * _shift_down(x, width - 1 - j)
    return acc


def _causal_conv_bwd(x, dpre, w_ref, dx_ref, dw_ref, width):
    dx = w_ref[width - 1:width, :] * dpre
    dw_ref[width - 1:width, :] = jnp.sum(dpre * x, axis=0, keepdims=True)
    for j in range(width - 1):
        s = width - 1 - j
        dx = dx + w_ref[j:j + 1, :] * _shift_up(dpre, s)
        dw_ref[j:j + 1, :] = jnp.sum(dpre * _shift_down(x, s), axis=0, keepdims=True)
    dx_ref[...] = dx.astype(dx_ref.dtype)


def _dsilu(x):
    sg = jax.nn.sigmoid(x)
    return sg * (1.0 + x * (1.0 - sg))


def _dn_conv_fwd(x, w):
    def body(x_ref, w_ref, y_ref):
        y_ref[...] = _silu(_causal_conv(x_ref[...], w_ref, 4))

    return _col_call("dn_conv_fwd", body, [x], [w], [F32], [], 256)[0]


def _dn_conv_bwd(x, w, dy):
    def body(x_ref, dy_ref, w_ref, dx_ref, dw_ref):
        xv = x_ref[...]
        dpre = dy_ref[...] * _dsilu(_causal_conv(xv, w_ref, 4))
        _causal_conv_bwd(xv, dpre, w_ref, dx_ref, dw_ref, 4)

    return _col_call("dn_conv_bwd", body, [x, dy], [w], [BF16], [4], 256)


def _glu_fwd(gpre, up, w, b, host=None):
    def body(g_ref, u_ref, w_ref, b_ref, a_ref):
        gate = _causal_conv(g_ref[...], w_ref, 3) + b_ref[...]
        a_ref[...] = (_silu(gate) * u_ref[...]).astype(BF16)

    return _col_call("glu_fwd", body, [gpre, up], [w, b], [BF16], [], 256, host=host)[0]


def _glu_bwd(gpre, up, w, b, dact):
    def body(g_ref, u_ref, da_ref, w_ref, b_ref, dg_ref, du_ref, dw_ref, db_ref):
        gv = g_ref[...]
        gate = _causal_conv(gv, w_ref, 3) + b_ref[...]
        da = da_ref[...]
        sg = jax.nn.sigmoid(gate)
        du_ref[...] = (da * (gate * sg)).astype(BF16)
        dgate = da * u_ref[...] * (sg * (1.0 + gate * (1.0 - sg)))
        db_ref[...] = jnp.sum(dgate, axis=0, keepdims=True)
        _causal_conv_bwd(gv, dgate, w_ref, dg_ref, dw_ref, 3)

    return _col_call("glu_bwd", body, [gpre, up, dact], [w, b], [BF16, BF16], [3, 1], 256)


def _dn_prep_consts(sa, sb, al, dt):
    return dict(sel_a=sa[...], sel_b=sb[...], alog=al[...], dtb=dt[...])


def _dn_prep_fwd(conv, ab, sel_a, sel_b, alog, dtb):
    def body(c_ref, ab_ref, sa, sb, al, dt, q_out, k_out, g_out, b_out):
        qc = tuple(c_ref[:, h * HEAD:(h + 1) * HEAD] for h in range(HEADS))
        kc = tuple(c_ref[:, DN_WIDTH + h * HEAD:DN_WIDTH + (h + 1) * HEAD] for h in range(HEADS))
        qs, ks, g, beta = _dn_prep_fn((qc, kc, ab_ref[...]), _dn_prep_consts(sa, sb, al, dt))
        for h in range(HEADS):
            q_out[:, h * HEAD:(h + 1) * HEAD] = qs[h]
            k_out[:, h * HEAD:(h + 1) * HEAD] = ks[h]
        g_out[...] = g
        b_out[...] = beta

    return _rows_call("dn_prep_fwd", body, [conv, ab], [sel_a, sel_b, alog, dtb], [(DN_WIDTH, F32)] * 4, [],
                      _row_tile(conv.shape[0]))


def _dn_prep_bwd(conv, ab, dq, dk, dv, dg, db, sel_a, sel_b, alog, dtb):
    def body(c_ref, ab_ref, dq_r, dk_r, dv_r, dg_r, db_r, sa, sb, al, dt, dc_out, dab_out, dal_out, ddt_out):
        qc = tuple(c_ref[:, h * HEAD:(h + 1) * HEAD] for h in range(HEADS))
        kc = tuple(c_ref[:, DN_WIDTH + h * HEAD:DN_WIDTH + (h + 1) * HEAD] for h in range(HEADS))
        consts = _dn_prep_consts(sa, sb, al, dt)
        sel = dict(sel_a=consts["sel_a"], sel_b=consts["sel_b"])
        _, vjp = jax.vjp(lambda rows, ad: _dn_prep_fn(rows, {**sel, **ad}), (qc, kc, ab_ref[...]),
                         dict(alog=consts["alog"], dtb=consts["dtb"]))
        cq = tuple(dq_r[:, h * HEAD:(h + 1) * HEAD] for h in range(HEADS))
        ck = tuple(dk_r[:, h * HEAD:(h + 1) * HEAD] for h in range(HEADS))
        (dqc, dkc, dab), dad = vjp((cq, ck, dg_r[...], db_r[...]))
        for h in range(HEADS):
            dc_out[:, h * HEAD:(h + 1) * HEAD] = dqc[h]
            dc_out[:, DN_WIDTH + h * HEAD:DN_WIDTH + (h + 1) * HEAD] = dkc[h]
        dc_out[:, 2 * DN_WIDTH:3 * DN_WIDTH] = dv_r[...]
        dab_out[...] = dab.astype(BF16)
        _accumulate(dal_out, dad["alog"])
        _accumulate(ddt_out, dad["dtb"])

    return _rows_call("dn_prep_bwd", body, [conv, ab, dq, dk, dv, dg, db], [sel_a, sel_b, alog, dtb],
                      [(3 * DN_WIDTH, F32), (HEAD, BF16)], [(1, DN_WIDTH), (1, DN_WIDTH)], _row_tile(conv.shape[0]))


def _chunk_batch(t):
    nc = t // CHUNK
    return nc // 2 if nc % 2 == 0 else nc


def _dn_chunk_specs(t, nb):
    rows = nb * CHUNK
    blk = pl.BlockSpec((rows, HEAD), lambda h, b: (b, h))
    vblk = pl.BlockSpec((rows, HEAD), lambda h, b: (b, 2 * HEADS + h))
    mat = pl.BlockSpec((nb, HEAD, HEAD), lambda h, b: (b, h, 0))
    return rows, blk, vblk, mat


def _dn_chunk_fwd(qn, kn, conv, g, beta, host=None):
    t = qn.shape[0]
    nb = _chunk_batch(t)
    rows, blk, vblk, mat = _dn_chunk_specs(t, nb)

    def body(q_ref, k_ref, v_ref, g_ref, b_ref, n_o, b_o, qe_o, oo_o, eg_o):
        r3 = lambda x: x.reshape(nb, CHUNK, x.shape[-1])
        n_mat, b_mat, q_eff, o_own, eg = _dn_chunk_fn(r3(q_ref[...]), r3(k_ref[...]), r3(v_ref[...]), r3(g_ref[...]),
                                                      r3(g_ref[:, 0:CHUNK]), r3(b_ref[...]))
        n_o[...] = n_mat
        b_o[...] = b_mat
        qe_o[...] = q_eff.reshape(rows, HEAD)
        oo_o[...] = o_own.reshape(rows, HEAD)
        eg_o[...] = jnp.broadcast_to(eg, (nb, HEAD, HEAD))

    nc = t // CHUNK
    mats = jax.ShapeDtypeStruct((nc, DN_WIDTH, HEAD), F32)
    rowsd = jax.ShapeDtypeStruct((t, DN_WIDTH), F32)
    return _pcall(body, "dn_chunk_fwd", (HEADS, t // rows), [blk, blk, vblk, blk, blk], [mat, mat, blk, blk, mat],
                  [mats, mats, rowsd, rowsd, mats], [qn, kn, conv, g, beta], ("arbitrary", "arbitrary"), host=host)


def _dn_chunk_bwd(qn, kn, conv, g, beta, sall, gall, dq_eff, do, host=None):
    t = qn.shape[0]
    nb = _chunk_batch(t)
    rows, blk, vblk, mat = _dn_chunk_specs(t, nb)

    def body(q_ref, k_ref, v_ref, g_ref, b_ref, s_ref, ga_ref, dqe_ref, do_ref, dq_o, dk_o, dv_o, dg_o, db_o):
        r3 = lambda x: x.reshape(nb, CHUNK, x.shape[-1])
        _, vjp = jax.vjp(_dn_chunk_fn, r3(q_ref[...]), r3(k_ref[...]), r3(v_ref[...]), r3(g_ref[...]),
                         r3(g_ref[:, 0:CHUNK]), r3(b_ref[...]))
        s, ga = s_ref[...], ga_ref[...]
        d_n = -_bmm_nt(ga, s)
        d_eg = jnp.sum(ga * s, axis=1, keepdims=True)
        dq, dk, dv, dg, dg64, db = vjp((d_n, ga, r3(dqe_ref[...]), r3(do_ref[...]), d_eg))
        for o_ref, val in zip((dq_o, dk_o, dv_o, dg_o, db_o), (dq, dk, dv, dg, db)):
            o_ref[...] = val.reshape(rows, HEAD)
        dg_o[:, 0:CHUNK] += dg64.reshape(rows, CHUNK)

    return _pcall(body, "dn_chunk_bwd", (HEADS, t // rows), [blk, blk, vblk, blk, blk, mat, mat, blk, blk], [blk] * 5,
                  [jax.ShapeDtypeStruct((t, DN_WIDTH), F32)] * 5, [qn, kn, conv, g, beta, sall, gall, dq_eff, do],
                  ("arbitrary", "arbitrary"), host=host)


def _dn_rec_fwd(n_mat, b_mat, eg, host=None):
    nc = n_mat.shape[0]
    nb = _chunk_batch(nc * CHUNK)
    spec = pl.BlockSpec((nb, DN_WIDTH, HEAD), lambda i: (i, 0, 0))

    def body(n_ref, b_ref, eg_ref, sall_ref, s_scr):
        @pl.when(pl.program_id(0) == 0)
        def _():
            s_scr[...] = jnp.zeros(s_scr.shape, F32)

        for j in range(nb):
            sall_ref[j] = s_scr[...]
            for h in range(HEADS):
                sl = slice(h * HEAD, (h + 1) * HEAD)
                s_scr[sl, :] = _dn_rec_fn(s_scr[sl, :], n_ref[j, sl, :], b_ref[j, sl, :],
                                          eg_ref[j, h * HEAD:h * HEAD + 1, :])

    return _pcall(body, "dn_rec_fwd", (nc // nb,), [spec] * 3, spec, jax.ShapeDtypeStruct((nc, DN_WIDTH, HEAD), F32),
                  [n_mat, b_mat, eg], ("arbitrary",), scratch_shapes=[pltpu.VMEM((DN_WIDTH, HEAD), F32)], host=host)


def _dn_rec_bwd(n_mat, eg, ds_out, host=None):
    nc = n_mat.shape[0]
    nb = _chunk_batch(nc * CHUNK)
    steps = nc // nb
    spec = pl.BlockSpec((nb, DN_WIDTH, HEAD), lambda i: (steps - 1 - i, 0, 0))

    def body(n_ref, eg_ref, dso_ref, gall_ref, g_scr):
        @pl.when(pl.program_id(0) == 0)
        def _():
            g_scr[...] = jnp.zeros(g_scr.shape, F32)

        for j in reversed(range(nb)):
            gall_ref[j] = g_scr[...]
            for h in range(HEADS):
                sl = slice(h * HEAD, (h + 1) * HEAD)
                gv = g_scr[sl, :]
                g_scr[sl, :] = (gv * eg_ref[j, h * HEAD:h * HEAD + 1, :] - _mm_tn(n_ref[j, sl, :], gv)
                                + dso_ref[j, sl, :])

    return _pcall(body, "dn_rec_bwd", (steps,), [spec] * 3, spec, jax.ShapeDtypeStruct((nc, DN_WIDTH, HEAD), F32),
                  [n_mat, eg, ds_out], ("arbitrary",), scratch_shapes=[pltpu.VMEM((DN_WIDTH, HEAD), F32)], host=host)


def _dn_o_fwd(sall, q_eff, o_own):
    t = q_eff.shape[0]
    nb = _chunk_batch(t)
    rows, blk, _, mat = _dn_chunk_specs(t, nb)

    def body(s_ref, qe_ref, oo_ref, o_ref):
        r3 = lambda x: x.reshape(nb, CHUNK, HEAD)
        o_ref[...] = _dn_o_fn(s_ref[...], r3(qe_ref[...]), r3(oo_ref[...])).reshape(rows, HEAD)

    return _pcall(body, "dn_o_fwd", (HEADS, t // rows), [mat, blk, blk], blk, jax.ShapeDtypeStruct((t, DN_WIDTH), F32),
                  [sall, q_eff, o_own], ("arbitrary", "arbitrary"))


def _dn_o_bwd(sall, q_eff, do, host=None):
    t = q_eff.shape[0]
    nb = _chunk_batch(t)
    rows, blk, _, mat = _dn_chunk_specs(t, nb)

    def body(s_ref, qe_ref, do_ref, dqe_ref, ds_ref):
        r3 = lambda x: x.reshape(nb, CHUNK, HEAD)
        dov = r3(do_ref[...])
        dqe_ref[...] = _bmm_nt(dov, s_ref[...]).reshape(rows, HEAD)
        ds_ref[...] = _bmm_tn(r3(qe_ref[...]), dov)

    nc = t // CHUNK
    return _pcall(body, "dn_o_bwd", (HEADS, t // rows), [mat, blk, blk], [blk, mat],
                  [jax.ShapeDtypeStruct((t, DN_WIDTH), F32), jax.ShapeDtypeStruct((nc, DN_WIDTH, HEAD), F32)],
                  [sall, q_eff, do], ("arbitrary", "arbitrary"), host=host)


def _loss_call(h2, tgt, n_valid):
    t, n = h2.shape
    r = _row_tile(t)

    def body(h_ref, t_ref, dy_ref, dy16_ref, acc_ref):
        rows = pl.program_id(0) * r + lax.broadcasted_iota(jnp.int32, (r, n), 0)
        valid = jnp.logical_and(rows >= N_META, rows < n_valid)
        e = jnp.where(valid, h_ref[...] - t_ref[...], 0.0)
        dy = e * (1.0 / n)
        dy_ref[...] = dy
        dy16_ref[...] = dy.astype(BF16)
        _accumulate(acc_ref, jnp.sum(e * e, axis=0, keepdims=True))

    return _rows_call("loss", body, [h2, tgt], [], [(n, F32), (n, BF16)], [(1, n)], r)


def _adamw_update(w, g, m, v):
    m2 = ADAM_B1 * m + (1.0 - ADAM_B1) * g
    v2 = ADAM_B2 * v + (1.0 - ADAM_B2) * (g * g)
    m_hat = m2 / (1.0 - ADAM_B1 ** ADAM_STEP)
    v_hat = v2 / (1.0 - ADAM_B2 ** ADAM_STEP)
    return -ADAM_LR * (m_hat / (jnp.sqrt(v_hat) + ADAM_EPS) + ADAM_WD * w), m2, v2


def _adamw_small(ws, gs, ms, vs):
    n = len(ws)

    def body(*refs):
        for i in range(n):
            d, m2, v2 = _adamw_update(refs[i][...], refs[n + i][...], refs[2 * n + i][...], refs[3 * n + i][...])
            refs[4 * n + i][...] = d
            refs[5 * n + i][...] = m2
            refs[6 * n + i][...] = v2

    shapes = [jax.ShapeDtypeStruct(a.shape, F32) for a in ws]
    outs = pl.pallas_call(body, name="adamw_small", out_shape=shapes * 3,
                          compiler_params=pltpu.CompilerParams(vmem_limit_bytes=VMEM_LIMIT))(*ws, *gs, *ms, *vs)
    return outs[:n], outs[n:2 * n], outs[2 * n:]


def _adamw_call(name, w, g, m, v, host=None):
    rows, cols = w.shape
    by_rows = rows % 8 == 0

    def body(w_ref, g_ref, m_ref, v_ref, g_out, d_ref, m_out, v_out):
        gv = g_ref[...] if by_rows else g_ref[0:rows, :]
        g_out[...] = gv
        d_ref[...], m_out[...], v_out[...] = _adamw_update(w_ref[...], gv, m_ref[...], v_ref[...])

    if by_rows:
        tr = _tile(rows, 256, 8)
        spec = g_spec = pl.BlockSpec((tr, cols), lambda i: (i, 0))
        grid = (rows // tr,)
    else:
        tc = _tile(cols, 256, 128)
        spec = pl.BlockSpec((rows, tc), lambda j: (0, j))
        g_spec = pl.BlockSpec((g.shape[0], tc), lambda j: (0, j))
        grid = (cols // tc,)
    return _pcall(body, name, grid, [spec, g_spec, spec, spec], [spec] * 4, [jax.ShapeDtypeStruct((rows, cols), F32)] * 4,
                  [w, g, m, v], ("arbitrary",), host=host)


def _rope_tables(t):
    half = ROPE // 2
    inv_freq = np.float32(ROPE_THETA) ** (-np.arange(half, dtype=np.float32) / np.float32(half))
    ang = np.arange(t, dtype=np.float32)[:, None] * inv_freq[None, :].astype(np.float32)
    z = np.zeros((t, HEAD - ROPE), np.float32)
    cos = np.concatenate([np.cos(ang), np.cos(ang), z], axis=1).astype(np.float32)
    sin = np.concatenate([np.sin(ang), np.sin(ang), z], axis=1).astype(np.float32)
    k = np.arange(HEAD)[:, None]
    l = np.arange(HEAD)[None, :]
    perm = np.where((l < half) & (k == l + half), -1.0, 0.0) + np.where((l >= half) & (l < ROPE) & (k == l - half), 1.0, 0.0)
    return jnp.asarray(cos), jnp.asarray(sin), jnp.asarray(perm.astype(np.float32))


def _win_to_pad(w):
    z = lambda n: jnp.zeros((n, w.shape[1]), w.dtype)
    return jnp.concatenate([w[576:2112], w[2112:2624], w[0:256], w[256:512], w[512:576], z(64), w[2624:2632], z(120)],
                           axis=0)


def _win_from_pad(g):
    return jnp.concatenate([g[2048:2304], g[2304:2560], g[2560:2624], g[0:1536], g[1536:2048], g[2688:2696]], axis=0)


def _qk_to_pad(w):
    w4 = w.reshape(HEADS, QK_DIM, w.shape[-1])
    return jnp.concatenate([w4, jnp.zeros((HEADS, QK_PAD - QK_DIM, w.shape[-1]), w.dtype)], axis=1).reshape(
        HEADS * QK_PAD, w.shape[-1])


def _qk_from_pad(g):
    return g.reshape(HEADS, QK_PAD, g.shape[-1])[:, :QK_DIM].reshape(HEADS * QK_DIM, g.shape[-1])


def _ff_to_pad(a, axis):
    shape = list(a.shape)
    shape[axis:axis + 1] = [N_CHIPS, FF_SHARD]
    a4 = a.reshape(shape)
    shape[axis + 1] = FF_BLOCK - FF_SHARD
    out = jnp.concatenate([a4, jnp.zeros(shape, a.dtype)], axis=axis + 1)
    shape[axis:axis + 2] = [D_FF_P]
    return out.reshape(shape)


def _ff_from_pad(a, axis):
    shape = list(a.shape)
    shape[axis:axis + 1] = [N_CHIPS, FF_BLOCK]
    a4 = lax.slice_in_dim(a.reshape(shape), 0, FF_SHARD, axis=axis + 1)
    shape[axis:axis + 2] = [D_FF]
    return a4.reshape(shape)


class _LocalPlan:
    def __init__(self, wt):
        self.wt, self.grads = wt, {}

    def weight(self, name):
        return self.wt[name]

    def host(self, point):
        return None

    def grad(self, name, value):
        self.grads[name] = value


def _local_step(x, tgt, wt, plan=None):
    plan = _LocalPlan(wt) if plan is None else plan
    s = x.shape[0]
    n_valid = N_META + s
    t = -(-n_valid // HEAD) * HEAD
    zpad = jnp.zeros((t - n_valid, D_MODEL), F32)
    h0 = jnp.concatenate([wt["meta_tokens"], x, zpad], axis=0)
    tgt_p = jnp.concatenate([jnp.zeros((N_META, D_MODEL), F32), tgt, zpad], axis=0)
    cos, sin, perm = _rope_tables(t)
    win, wq, wkv = wt["w_in_t"], wt["w_q_t"], wt["w_kv_t"]
    qn_w = jnp.concatenate([wt["q_norm_w"], jnp.zeros((1, QK_PAD - QK_DIM), F32)], axis=1)
    kn_w = jnp.concatenate([wt["k_norm_w"], jnp.zeros((1, QK_PAD - QK_DIM), F32)], axis=1)
    head_id = jnp.arange(DN_WIDTH)[None, :] // HEAD
    lane = jnp.arange(HEAD)[:, None]
    sel_a = (lane == head_id).astype(F32)
    sel_b = (lane == head_id + HEADS).astype(F32)
    alog = jnp.repeat(wt["dn_A_log"], HEAD, axis=1)
    dtb = jnp.repeat(wt["dn_dt_bias"], HEAD, axis=1)
    conv_w, conv_b = wt["ffn_conv_w"], wt["ffn_conv_b"]

    u = _rms_fwd("attn_norm_fwd", h0, wt["attn_norm_w"])
    proj = _matmul("in_proj", u, win, "nt", F32)
    z = (proj, DN_WIDTH, 3)
    q_lat, kv_lat, k_pe, ab = (proj, LORA, 8), (proj, LORA, 9), (proj, HEAD, 20), (proj, HEAD, 21)
    mla_consts = (wt["q_a_norm_w"], wq, wt["kv_a_norm_w"], wkv, qn_w, kn_w, perm)
    q, k, v = _mla_prep_fwd(q_lat, kv_lat, k_pe, cos, sin, *mla_consts)
    o_mla = _attn_fwd(q, k, v, host=plan.host("attn_fwd"))
    conv = _dn_conv_fwd(proj, wt["dn_conv_w"])
    dn_consts = (sel_a, sel_b, alog, dtb)
    qn, kn, g, beta = _dn_prep_fwd(conv, ab, *dn_consts)
    n_mat, b_mat, q_eff, o_own, eg = _dn_chunk_fwd(qn, kn, conv, g, beta, host=plan.host("dn_chunk_fwd"))
    sall = _dn_rec_fwd(n_mat, b_mat, eg)
    o_dn = _dn_o_fwd(sall, q_eff, o_own)
    mixed = _mix_out_fwd(o_mla, o_dn, z, wt["mla_out_norm_w"], wt["dn_out_norm_w"])
    w_out = plan.weight("w_out")
    h1 = _matmul("out_proj", mixed, w_out, "nn", F32, res=h0)
    n2 = _rms_fwd("ffn_norm_fwd", h1, wt["ffn_norm_w"])
    w_gate, w_up = plan.weight("w_gate_t"), plan.weight("w_up_t")
    gpre = _matmul("gate_proj", n2, w_gate, "nt", F32, host=plan.host("gate_proj"))
    up = _matmul("up_proj", n2, w_up, "nt", F32, host=plan.host("up_proj"))
    act = _glu_fwd(gpre, up, conv_w, conv_b)
    w_down = plan.weight("w_down")
    h2 = _matmul("down_proj", act, w_down, "nn", F32, res=h1)
    dy, dy16, sq = _loss_call(h2, tgt_p, n_valid)

    grads = {}
    dact = _matmul("down_dx", dy16, w_down, "nt", F32)
    plan.grad("w_down", _matmul("down_dw", act, dy16, "tn", F32))
    dgpre, dup, grads["ffn_conv_w"], grads["ffn_conv_b"] = _glu_bwd(gpre, up, conv_w, conv_b, dact)
    plan.grad("w_gate_t", _matmul("gate_dw", dgpre, n2, "tn", F32))
    plan.grad("w_up_t", _matmul("up_dw", dup, n2, "tn", F32))
    dn2a = _matmul("gate_dx", dgpre, w_gate, "nn", F32, host=plan.host("gate_dx"))
    dn2b = _matmul("up_dx", dup, w_up, "nn", F32, host=plan.host("up_dx"))
    dh1, dh1_16, grads["ffn_norm_w"] = _rms_bwd("ffn_norm_bwd", h1, wt["ffn_norm_w"], [dn2a, dn2b], dy)
    dmixed = _matmul("out_dx", dh1_16, w_out, "nt", F32)
    plan.grad("w_out", _matmul("out_dw", mixed, dh1_16, "tn", F32))
    do_mla, do_dn, dz, grads["mla_out_norm_w"], grads["dn_out_norm_w"] = _mix_out_bwd(
        o_mla, o_dn, z, dmixed, wt["mla_out_norm_w"], wt["dn_out_norm_w"], host=plan.host("mix_out_bwd"))
    dq_eff, ds_out = _dn_o_bwd(sall, q_eff, do_dn)
    gall = _dn_rec_bwd(n_mat, eg, ds_out)
    dqn, dkn, dv_dn, dg, dbeta = _dn_chunk_bwd(qn, kn, conv, g, beta, sall, gall, dq_eff, do_dn,
                                               host=plan.host("dn_chunk_bwd"))
    dconv, dab, dalog, ddtb = _dn_prep_bwd(conv, ab, dqn, dkn, dv_dn, dg, dbeta, *dn_consts)
    grads["dn_A_log"] = jnp.sum(dalog.reshape(HEADS, HEAD), axis=1)[None, :]
    grads["dn_dt_bias"] = jnp.sum(ddtb.reshape(HEADS, HEAD), axis=1)[None, :]
    ddn_pre, grads["dn_conv_w"] = _dn_conv_bwd(proj, wt["dn_conv_w"], dconv)
    dq, dk, dv = _attn_bwd(q, k, v, do_mla, host=plan.host("attn_bwd"))
    dq_lat, dkv_lat, dk_pe, dqa, dwq, dkva, dwkv, dqnw, dknw = _mla_prep_bwd(
        q_lat, kv_lat, k_pe, cos, sin, dq, dk, dv, *mla_consts, host=plan.host("mla_prep_bwd"))
    grads["q_a_norm_w"], grads["kv_a_norm_w"] = dqa, dkva
    plan.grad("w_q_t", dwq)
    plan.grad("w_kv_t", dwkv)
    grads["q_norm_w"], grads["k_norm_w"] = dqnw[:, :QK_DIM], dknw[:, :QK_DIM]
    dproj = jnp.concatenate([ddn_pre, dz, dq_lat, dkv_lat, dk_pe, dab], axis=1)
    plan.grad("w_in_t", _matmul("in_dw", dproj, u, "tn", F32))
    du = _matmul("in_dx", dproj, win, "nn", F32, host=plan.host("in_dx"))
    dh0, _, grads["attn_norm_w"] = _rms_bwd("attn_norm_bwd", h0, wt["attn_norm_w"], [du], dh1,
                                            host=plan.host("attn_norm_bwd"))
    grads["meta_tokens"] = dh0[0:N_META]
    if isinstance(plan, _LocalPlan):
        grads.update(plan.grads)
    return sq, dh0[N_META:n_valid], grads


def _mesh_pos():
    return lax.axis_index("x"), lax.axis_index("y"), lax.axis_index("c")


def _other_chips(x, y):
    return [(1 - x, y), (x, 1 - y), (1 - x, 1 - y)]


def _remote(src, dst, send_sems, recv_sems, k, to):
    return pltpu.make_async_remote_copy(src_ref=src, dst_ref=dst, send_sem=send_sems.at[k], recv_sem=recv_sems.at[k],
                                        device_id=to, device_id_type=MESH)


SIBLING_ID, CHIPS_ID, GATHER_ID, ALL_ID = 1, 2, 3, 4


def _sibling_peer():
    x, y, c = _mesh_pos()
    return [(x, y, 1 - c)]


def _chip_peers():
    x, y, c = _mesh_pos()
    return [(qx, qy, c) for qx, qy in _other_chips(x, y)]


def _copies_exchange(make, ins, out_shape, nsem, peers=None, cid=None):
    def prog(in_refs, out_refs, send_sems, recv_sems):
        copies = make(in_refs, out_refs, send_sems, recv_sems)

        def start():
            for cp in copies:
                cp.start()

        def finish():
            for cp in copies:
                cp.wait()

        return start, finish

    return _Exchange(prog, ins, out_shape, nsem, peers, cid)


def _all_gather(shards):
    def prog(srcs, dsts, send_sems, recv_sems):
        x, y, c = _mesh_pos()
        p = 2 * x + y
        sibling = (x, y, 1 - c)
        chips = _other_chips(x, y)
        bufs = tuple((s, d, s.shape[0] // 2) for s, d in zip(srcs, dsts))

        def half(ref, rows, which):
            return ref.at[pl.ds(which * rows, rows), :]

        def copy(i, k, src, dst, to):
            return _remote(src, dst, send_sems, recv_sems, 6 * i + k, to)

        sends = [copy(i, j, half(src, rows, c), half(dst.at[p], rows, c), (*chip, c))
                 for i, (src, dst, rows) in enumerate(bufs) for j, chip in enumerate(chips)]

        def start():
            for cp in sends:
                cp.start()

        def finish():
            passed = []
            for i, (src, dst, rows) in enumerate(bufs):
                for j, (qx, qy) in enumerate(chips):
                    block = half(dst.at[2 * qx + qy], rows, c)
                    copy(i, j, block, block, (x, y, c)).wait_recv()
                    fwd = copy(i, 3 + j, block, block, sibling)
                    fwd.start()
                    passed.append(fwd)
            for i, (src, dst, rows) in enumerate(bufs):
                for j, (qx, qy) in enumerate(chips):
                    block = half(dst.at[2 * qx + qy], rows, 1 - c)
                    copy(i, 3 + j, block, block, (x, y, c)).wait_recv()
            for cp in sends + passed:
                cp.wait_send()

        return start, finish

    return _Exchange(prog, shards, [jax.ShapeDtypeStruct((N_CHIPS, *s.shape), s.dtype) for s in shards], 6 * len(shards),
                     lambda: _sibling_peer() + _chip_peers(), GATHER_ID)


def _gathered(ex):
    p = 2 * lax.axis_index("x") + lax.axis_index("y")
    return [lax.dynamic_update_slice(g, s[None], (p, 0, 0)) for g, s in zip(ex.outs, ex.ins)]


def _rs_to_sibling(bufs):
    def make(srcs, dsts, send_sems, recv_sems):
        x, y, c = _mesh_pos()
        copies = []
        for i, (src, dst) in enumerate(zip(srcs, dsts)):
            half = src.shape[1] // 2
            copies.append(_remote(src.at[:, pl.ds((1 - c) * half, half), :], dst, send_sems, recv_sems, i, (x, y, 1 - c)))
        return copies

    return _copies_exchange(make, bufs, [jax.ShapeDtypeStruct((N_CHIPS, b.shape[1] // 2, b.shape[2]), F32) for b in bufs],
                            len(bufs), _sibling_peer, SIBLING_ID)


def _rs_pair_add(name, bufs, gots, c, out_dtype):
    n = len(bufs)

    def body(c_ref, *refs):
        for a_ref, b_ref, o_ref in zip(refs[:n], refs[n:2 * n], refs[2 * n:]):
            o_ref[...] = (a_ref[...] + b_ref[...]).astype(out_dtype)

    mine = [pl.BlockSpec((None, g.shape[1], g.shape[2]), lambda j, cr: (j, cr[0], 0)) for g in gots]
    whole = [pl.BlockSpec((None, g.shape[1], g.shape[2]), lambda j, cr: (j, 0, 0)) for g in gots]
    return pl.pallas_call(
        body, name=name,
        grid_spec=pltpu.PrefetchScalarGridSpec(num_scalar_prefetch=1, grid=(N_CHIPS,), in_specs=mine + whole, out_specs=whole),
        out_shape=[jax.ShapeDtypeStruct(g.shape, out_dtype) for g in gots],
        compiler_params=_cparams(("arbitrary",)))(c, *bufs, *gots)


def _rs_to_chips(accs):
    def make(srcs, dsts, send_sems, recv_sems):
        x, y, c = _mesh_pos()
        return [_remote(src.at[2 * qx + qy], dst.at[k], send_sems, recv_sems, 3 * i + k, (qx, qy, c))
                for i, (src, dst) in enumerate(zip(srcs, dsts)) for k, (qx, qy) in enumerate(_other_chips(x, y))]

    return _copies_exchange(make, accs, [jax.ShapeDtypeStruct((3, a.shape[1], a.shape[2]), a.dtype) for a in accs],
                            3 * len(accs), _chip_peers, CHIPS_ID)


def _rs_chip_add(name, accs, gots, p):
    n = len(accs)
    slot = (0, 1, 0, 2)

    def body(p_ref, *refs):
        me = p_ref[0]
        for own_ref, got_ref, o_ref in zip(refs[:n], refs[n:2 * n], refs[2 * n:]):
            total = None
            for chip in range(N_CHIPS):
                val = own_ref[...].astype(F32)
                for e in (1, 2, 3):
                    val = jnp.where((chip ^ me) == e, got_ref[slot[e]].astype(F32), val)
                total = val if total is None else total + val
            o_ref[...] = total

    own = [pl.BlockSpec((None, a.shape[1], a.shape[2]), lambda i, pr: (pr[0], 0, 0)) for a in accs]
    got = [pl.BlockSpec(g.shape, lambda i, pr: (0, 0, 0)) for g in gots]
    out = [pl.BlockSpec((a.shape[1], a.shape[2]), lambda i, pr: (0, 0)) for a in accs]
    return pl.pallas_call(
        body, name=name,
        grid_spec=pltpu.PrefetchScalarGridSpec(num_scalar_prefetch=1, grid=(1,), in_specs=own + got, out_specs=out),
        out_shape=[jax.ShapeDtypeStruct((a.shape[1], a.shape[2]), F32) for a in accs],
        compiler_params=_cparams(("arbitrary",)))(p, *accs, *gots)


def _rs_share(ress):
    def make(srcs, dsts, send_sems, recv_sems):
        x, y, c = _mesh_pos()
        return [_remote(src, dst, send_sems, recv_sems, i, (x, y, 1 - c)) for i, (src, dst) in enumerate(zip(srcs, dsts))]

    return _copies_exchange(make, ress, [jax.ShapeDtypeStruct(r.shape, F32) for r in ress], len(ress), _sibling_peer,
                            SIBLING_ID)


def _shared(ex):
    south = lax.axis_index("c") == 0
    return [jnp.concatenate([jnp.where(south, r, g), jnp.where(south, g, r)], axis=0) for r, g in zip(ex.ins, ex.outs)]


def _all_to_all_devices(vec):
    def others():
        x, y, c = _mesh_pos()
        return [((1 - x if r & 4 else x), (1 - y if r & 2 else y), (1 - c if r & 1 else c)) for r in range(1, 8)]

    def make(srcs, dsts, send_sems, recv_sems):
        x, y, c = _mesh_pos()
        me = 4 * x + 2 * y + c
        return [_remote(srcs[0], dsts[0].at[me], send_sems, recv_sems, r, peer) for r, peer in enumerate(others())]

    return _copies_exchange(make, [vec], [jax.ShapeDtypeStruct((8, *vec.shape), vec.dtype)], 7, others, ALL_ID)


def _sum_devices(stack):
    def body(s_ref, o_ref):
        total = s_ref[0]
        for d in range(1, 8):
            total = total + s_ref[d]
        o_ref[...] = total

    return pl.pallas_call(body, name="sum_devices", out_shape=jax.ShapeDtypeStruct(stack.shape[1:], F32),
                          compiler_params=pltpu.CompilerParams(vmem_limit_bytes=VMEM_LIMIT))(stack)


def _pad_rows(flat, rows):
    return jnp.concatenate([flat, jnp.zeros((rows * LANES - flat.shape[0],), flat.dtype)]).reshape(rows, LANES)


def _unshard(g4, shape, axis):
    a = g4.reshape(N_CHIPS, *shape)
    if axis == 0:
        return a.reshape(N_CHIPS * shape[0], shape[1])
    return jnp.transpose(a, (1, 0, 2)).reshape(shape[0], N_CHIPS * shape[1])


def _shard4(full, shape, axis):
    if axis == 0:
        return full.reshape(N_CHIPS, shape[0] * shape[1])
    a = full.reshape(shape[0], N_CHIPS, shape[1])
    return jnp.transpose(a, (1, 0, 2)).reshape(N_CHIPS, shape[0] * shape[1])


def _pad_axis0(a, rows):
    return jnp.concatenate([a, jnp.zeros((rows - a.shape[0], *a.shape[1:]), a.dtype)], axis=0)


def _pad_axis1(a, rows):
    return jnp.concatenate([a, jnp.zeros((a.shape[0], rows - a.shape[1], *a.shape[2:]), a.dtype)], axis=1)


def _shard_to_strip(name, w):
    _, (shape, axis, rows) = name, {n: (s, ax, r) for n, s, ax, r in BIG}[name]
    w2 = w.reshape(shape).astype(BF16)
    if name == "w_in":
        return w2
    return _pad_axis0(w2.T if axis == 1 else w2, rows)


LOCAL_NAME = dict(w_in="w_in_t", w_q_b="w_q_t", w_kv_b="w_kv_t", w_out="w_out", w_gate="w_gate_t", w_up="w_up_t",
                  w_down="w_down")


WIN_SEGMENTS = ((576, 2112, 0), (2112, 2624, 1536), (0, 256, 2048), (256, 512, 2304), (512, 576, 2560), (2624, 2632, 2688))


def _strips_to_weight(name, g4):
    if name == "w_in":
        return _win_to_pad(jnp.transpose(g4, (0, 2, 1)).reshape(IN_COLS, D_MODEL))
    if name == "w_q_b":
        return _qk_to_pad(g4.reshape(HEADS * QK_DIM, LORA))
    return g4.reshape(N_CHIPS * g4.shape[1], g4.shape[2])


def _grad_to_strips(name, g):
    if name == "w_in":
        strips = []
        for q in range(N_CHIPS):
            pieces = []
            for a, b, local in sorted(WIN_SEGMENTS):
                s, e = max(a, q * IN_SHARD), min(b, (q + 1) * IN_SHARD)
                if s < e:
                    pieces.append(g[local + s - a:local + e - a])
            pieces.append(jnp.zeros((IN_SHARD_P - IN_SHARD, D_MODEL), g.dtype))
            strips.append(jnp.concatenate(pieces, axis=0))
        return jnp.stack(strips)
    if name == "w_q_b":
        return _qk_from_pad(g).reshape(N_CHIPS, QK_DIM, LORA)
    return g.reshape(N_CHIPS, g.shape[0] // N_CHIPS, g.shape[1])


class _MeshPlan:
    LATE = dict(attn_fwd=("w_up",), dn_chunk_fwd=("w_out", "w_gate"), gate_proj=("w_down/0",), up_proj=("w_down/1",))
    GROUP_A = ("w_down", "w_gate", "w_up", "w_out")
    GROUP_B = ("w_in", "w_q_b", "w_kv_b")

    def __init__(self, w):
        x, y, c = _mesh_pos()
        self.ci = jnp.reshape(c, (1,)).astype(jnp.int32)
        self.pi = jnp.reshape(2 * x + y, (1,)).astype(jnp.int32)
        self.strip = {n: _shard_to_strip(n, w[n]) for n, _, _, _ in BIG}
        self.gathers, self.weights, self.g, self.acc, self.reduced = {}, {}, {}, {}, {}
        self.sibs, self.sib, self.chip, self.share, self.halves = [], None, None, None, [None, None]

    def gather_first(self, small):
        names = ("w_in", "w_q_b", "w_kv_b")
        ex = _all_gather([self.strip[n] for n in names] + [small])
        ex.run("all_gather_first")
        got = _gathered(ex)
        for n, g4 in zip(names, got):
            self.weights[LOCAL_NAME[n]] = _strips_to_weight(n, g4)
        return got[-1]

    def weight(self, local_name):
        if local_name not in self.weights:
            for point, (names, ex) in list(self.gathers.items()):
                if ex.outs is not None:
                    for n, g4 in zip(names, _gathered(ex)):
                        if "/" in n:
                            n, half = n.split("/")
                            self.halves[int(half)] = g4
                            if None in self.halves:
                                continue
                            g4 = jnp.concatenate(self.halves, axis=1)
                        self.weights[LOCAL_NAME[n]] = _strips_to_weight(n, g4)
                    del self.gathers[point]
        return self.weights[local_name]

    def _shard(self, name):
        if "/" not in name:
            return self.strip[name]
        name, half = name.split("/")
        rows = self.strip[name].shape[0] // 2
        return self.strip[name][int(half) * rows:(int(half) + 1) * rows]

    def grad(self, local_name, value):
        name = {v: k for k, v in LOCAL_NAME.items()}[local_name]
        self.g[name] = _grad_to_strips(name, value)

    def _pair_add(self, names, gots):
        accs = _rs_pair_add("rs_pair_add_" + names[0], [self.g[n] for n in names], gots, self.ci, BF16)
        self.acc.update(zip(names, accs))

    def _chip_add(self, names, chip):
        return _rs_chip_add("rs_chip_add_" + names[0], [self.acc[n] for n in names], chip.outs, self.pi)

    def _take_shared(self, names, share):
        for n, strip in zip(names, _shared(share)):
            self.reduced[n] = strip

    def host(self, point):
        a, b = self.GROUP_A, self.GROUP_B
        if point in self.LATE:
            names = self.LATE[point]
            ex = _all_gather([self._shard(n) for n in names])
            self.gathers[point] = (names, ex)
            return ex
        if point in ("gate_dx", "up_dx", "mix_out_bwd"):
            names = dict(gate_dx=a[:2], up_dx=a[2:3], mix_out_bwd=a[3:])[point]
            ex = _rs_to_sibling([self.g[n] for n in names])
            self.sibs.append(ex)
            return ex
        if point == "dn_chunk_bwd":
            self._pair_add(a, [o for ex in self.sibs for o in ex.outs])
            self.chip1 = _rs_to_chips([self.acc[n] for n in a[:2]])
            return self.chip1
        if point == "attn_bwd":
            self.chip2 = _rs_to_chips([self.acc[n] for n in a[2:]])
            return self.chip2
        if point == "mla_prep_bwd":
            ress = self._chip_add(a[:2], self.chip1) + self._chip_add(a[2:], self.chip2)
            self.share = _rs_share(ress)
            return self.share
        if point == "in_dx":
            self._take_shared(a, self.share)
            self.sib = _rs_to_sibling([self.g[n] for n in b])
            return self.sib
        if point == "attn_norm_bwd":
            self._pair_add(b, self.sib.outs)
            self.chip = _rs_to_chips([self.acc[n] for n in b])
            return self.chip
        return None

    def last_share(self):
        self.share = _rs_share(self._chip_add(self.GROUP_B, self.chip))
        return self.share

    def finish(self):
        self._take_shared(self.GROUP_B, self.share)
        return self.reduced


def _strip_to_shard(name, strip):
    shape, axis = {n: (s, ax) for n, s, ax, _ in BIG}[name]
    rows = shape[axis]
    return strip[:rows].T if axis == 1 else strip[:rows]


def kernel(x, meta_tokens, attn_norm_w, w_in, q_a_norm_w, w_q_b, kv_a_norm_w, w_kv_b, q_norm_w, k_norm_w, mla_out_norm_w, dn_conv_w, dn_A_log, dn_dt_bias, dn_out_norm_w, w_out, ffn_norm_w, w_gate, w_up, ffn_conv_w, ffn_conv_b, w_down, loss_target, m_meta_tokens, m_attn_norm_w, m_w_in, m_q_a_norm_w, m_w_q_b, m_kv_a_norm_w, m_w_kv_b, m_q_norm_w, m_k_norm_w, m_mla_out_norm_w, m_dn_conv_w, m_dn_A_log, m_dn_dt_bias, m_dn_out_norm_w, m_w_out, m_ffn_norm_w, m_w_gate, m_w_up, m_ffn_conv_w, m_ffn_conv_b, m_w_down, v_meta_tokens, v_attn_norm_w, v_w_in, v_q_a_norm_w, v_w_q_b, v_kv_a_norm_w, v_w_kv_b, v_q_norm_w, v_k_norm_w, v_mla_out_norm_w, v_dn_conv_w, v_dn_A_log, v_dn_dt_bias, v_dn_out_norm_w, v_w_out, v_ffn_norm_w, v_w_gate, v_w_up, v_ffn_conv_w, v_ffn_conv_b, v_w_down):
    local = dict(locals())
    w = {n: local[n] for n in WEIGHTS}
    m = {n: local["m_" + n] for n in WEIGHTS}
    v = {n: local["v_" + n] for n in WEIGHTS}
    p = 2 * lax.axis_index("x") + lax.axis_index("y")

    plan = _MeshPlan(w)
    wf = _pad_rows(jnp.concatenate([w[n].reshape(-1) for n, _, _ in SMALL_SHARDED]), SMALL_ROWS)
    gf = plan.gather_first(wf).reshape(N_CHIPS, -1)
    full = dict(plan.weights)
    off = 0
    for n, s, ax in SMALL_SHARDED:
        full[n] = _unshard(gf[:, off:off + s[0] * s[1]], s, ax)
        off += s[0] * s[1]
    for n, _ in REPLICATED:
        full[n] = w[n]
    full["ffn_conv_w"] = _ff_to_pad(full["ffn_conv_w"], 1)
    full["ffn_conv_b"] = _ff_to_pad(full["ffn_conv_b"], 1)

    sq, grad_x, g = _local_step(x[0], loss_target[0], full, plan)
    g["ffn_conv_w"] = _ff_from_pad(g["ffn_conv_w"], 1)
    g["ffn_conv_b"] = _ff_from_pad(g["ffn_conv_b"], 1)

    small_all = [n for n, _, _ in SMALL_SHARDED] + [n for n, _ in REPLICATED]
    vec = jnp.concatenate([g[n].reshape(-1) for n in small_all] + [jnp.reshape(0.5 / D_MODEL * jnp.sum(sq), (1,))])
    vec = _pad_rows(vec, -(-vec.shape[0] // (8 * LANES)) * 8)
    a2a = _all_to_all_devices(vec)

    gs, delta, new_m, new_v = {}, {}, {}, {}
    big = {n: (s, ax) for n, s, ax, _ in BIG}

    def adamw_big(n, strips, host=None):
        s, ax = big[n]
        flip = ax == 1 and s[1] % 8 == 0
        there = (lambda a: a.reshape(s).T) if flip else (lambda a: a.reshape(s))
        back = (lambda a: a.T.reshape(w[n].shape)) if flip else (lambda a: a.reshape(w[n].shape))
        strip = strips[n] if flip or ax == 0 else strips[n][:s[1]].T
        g2, d2, m2, v2 = _adamw_call("adamw_" + n, there(w[n]), strip, there(m[n]), there(v[n]), host=host)
        gs[n], delta[n], new_m[n], new_v[n] = back(g2), back(d2), back(m2), back(v2)

    adamw_big("w_down", plan.reduced, host=a2a)
    adamw_big("w_gate", plan.reduced, host=plan.last_share())
    adamw_big("w_up", plan.reduced)
    adamw_big("w_out", plan.reduced)
    strips = plan.finish()
    for n in plan.GROUP_B:
        adamw_big(n, strips)
    me = 4 * lax.axis_index("x") + 2 * lax.axis_index("y") + lax.axis_index("c")
    red = _sum_devices(lax.dynamic_update_slice(a2a.outs[0], vec[None], (me, 0, 0))).reshape(-1)
    off = 0
    for n in small_all:
        tot = red[off:off + g[n].size].reshape(g[n].shape)
        off += g[n].size
        shard = {sn: (s, ax) for sn, s, ax in SMALL_SHARDED}.get(n)
        if shard is not None:
            tot = lax.dynamic_slice_in_dim(tot, p * shard[0][1], shard[0][1], axis=1)
        gs[n] = tot
    loss = red[off]
    two_d = lambda a: a.reshape(a.shape[-2], a.shape[-1])
    outs = _adamw_small([two_d(w[n]) for n in small_all], [two_d(gs[n]) for n in small_all],
                        [two_d(m[n]) for n in small_all], [two_d(v[n]) for n in small_all])
    for i, n in enumerate(small_all):
        for dst, src in ((delta, outs[0]), (new_m, outs[1]), (new_v, outs[2])):
            dst[n] = src[i].reshape(w[n].shape)

    grad_out = [gs[n].reshape(w[n].shape) for n in WEIGHTS]
    return (loss, grad_x[None], *grad_out, *[delta[n] for n in WEIGHTS], *[new_m[n] for n in WEIGHTS],
            *[new_v[n] for n in WEIGHTS])
```

```python
import functools
import math

import jax
import jax.numpy as jnp
import numpy as np
from jax import lax
from jax.experimental import pallas as pl
from jax.experimental.pallas import tpu as pltpu

F32 = jnp.float32
BF16 = jnp.bfloat16
HI = lax.Precision.HIGHEST
MESH = pl.DeviceIdType.MESH

N_META = 16
D_MODEL = 1024
HEADS = 4
HEAD = 128
ROPE = 64
QK_DIM = HEAD + ROPE
QK_PAD = 2 * HEAD
LORA = 256
DN_WIDTH = HEADS * HEAD
CHUNK = 64
D_FF = 2816
N_CHIPS = 4
FF_SHARD = D_FF // N_CHIPS
FF_BLOCK = 768
D_FF_P = N_CHIPS * FF_BLOCK
IN_COLS = 2632
IN_SHARD = IN_COLS // N_CHIPS
IN_SHARD_P = 672
IN_PAD = 2816
NORM_EPS = 1e-6
ROPE_THETA = 10000.0
LANES = 512

ADAM_LR, ADAM_B1, ADAM_B2, ADAM_EPS, ADAM_WD, ADAM_STEP = 0.001, 0.9, 0.999, 1e-08, 0.01, 10

VMEM_LIMIT = 56 * 1024 * 1024

BIG = (("w_in", (1024, 658), 1, IN_SHARD_P), ("w_q_b", (256, 192), 1, 192), ("w_kv_b", (256, 256), 1, 256),
       ("w_out", (256, 1024), 0, 256), ("w_gate", (1024, 704), 1, FF_BLOCK), ("w_up", (1024, 704), 1, FF_BLOCK),
       ("w_down", (704, 1024), 0, FF_BLOCK))
SMALL_SHARDED = (("meta_tokens", (16, 256), 1), ("dn_conv_w", (4, 384), 1), ("ffn_conv_w", (3, 704), 1))
REPLICATED = (("attn_norm_w", 1024), ("q_a_norm_w", 256), ("kv_a_norm_w", 256), ("q_norm_w", 192), ("k_norm_w", 192),
              ("mla_out_norm_w", 128), ("dn_A_log", 4), ("dn_dt_bias", 4), ("dn_out_norm_w", 128), ("ffn_norm_w", 1024),
              ("ffn_conv_b", 2816))
WEIGHTS = ("meta_tokens", "attn_norm_w", "w_in", "q_a_norm_w", "w_q_b", "kv_a_norm_w", "w_kv_b", "q_norm_w", "k_norm_w",
           "mla_out_norm_w", "dn_conv_w", "dn_A_log", "dn_dt_bias", "dn_out_norm_w", "w_out", "ffn_norm_w", "w_gate",
           "w_up", "ffn_conv_w", "ffn_conv_b", "w_down")

SMALL_ROWS = 16
REP_ROWS = 16


def _cparams(sem):
    return pltpu.CompilerParams(dimension_semantics=sem, vmem_limit_bytes=VMEM_LIMIT)


class _Exchange:
    def __init__(self, prog, ins, out_shape, nsem, peers=None, cid=None):
        self.prog, self.ins, self.out_shape, self.nsem = prog, list(ins), list(out_shape), nsem
        self.peers, self.cid = peers, cid
        self.outs = None

    def sems(self):
        return [pltpu.SemaphoreType.DMA((self.nsem,)), pltpu.SemaphoreType.DMA((self.nsem,))]

    def programs(self, in_refs, out_refs, send_sems, recv_sems):
        start, finish = self.prog(in_refs, out_refs, send_sems, recv_sems)
        if self.cid is None:
            return start, finish
        peers = self.peers()

        def shake_and_start():
            barrier = pltpu.get_barrier_semaphore()
            for peer in peers:
                pl.semaphore_signal(barrier, inc=1, device_id=peer, device_id_type=MESH)
            pl.semaphore_wait(barrier, len(peers))
            start()

        return shake_and_start, finish

    def cparams(self, **kw):
        return pltpu.CompilerParams(has_side_effects=True, collective_id=self.cid, **kw)

    def run(self, name):
        any_spec = pl.BlockSpec(memory_space=pl.ANY)
        n = len(self.ins)

        def body(*refs):
            start, finish = self.programs(refs[:n], refs[n:-2], refs[-2], refs[-1])
            start()
            finish()

        self.outs = pl.pallas_call(
            body, name=name, in_specs=[any_spec] * n, out_specs=[any_spec] * len(self.out_shape),
            out_shape=self.out_shape, scratch_shapes=self.sems(), compiler_params=self.cparams())(*self.ins)
        return self.outs


def _pcall(body, name, grid, in_specs, out_specs, out_shape, args, sem, scratch_shapes=(), host=None):
    single = not isinstance(out_shape, (list, tuple))
    out_specs, out_shape = ([out_specs], [out_shape]) if single else (list(out_specs), list(out_shape))
    if host is None:
        outs = pl.pallas_call(body, name=name, grid=grid, in_specs=list(in_specs), out_specs=out_specs, out_shape=out_shape,
                              scratch_shapes=list(scratch_shapes), compiler_params=_cparams(sem))(*args)
        return outs[0] if single else outs
    any_spec = pl.BlockSpec(memory_space=pl.ANY)
    n_in, n_out, n_scr, nx_in, nx_out = len(in_specs), len(out_specs), len(scratch_shapes), len(host.ins), len(host.out_shape)

    def hosted(*refs):
        c_in, x_in = refs[:n_in], refs[n_in:n_in + nx_in]
        o0 = n_in + nx_in
        c_out, x_out = refs[o0:o0 + n_out], refs[o0 + n_out:o0 + n_out + nx_out]
        s0 = o0 + n_out + nx_out
        start, finish = host.programs(x_in, x_out, refs[s0 + n_scr], refs[s0 + n_scr + 1])
        first = functools.reduce(jnp.logical_and, [pl.program_id(d) == 0 for d in range(len(grid))])
        last = functools.reduce(jnp.logical_and, [pl.program_id(d) == grid[d] - 1 for d in range(len(grid))])
        pl.when(first)(start)
        body(*c_in, *c_out, *refs[s0:s0 + n_scr])
        pl.when(last)(finish)

    outs = pl.pallas_call(
        hosted, name=name, grid=grid, in_specs=list(in_specs) + [any_spec] * nx_in,
        out_specs=out_specs + [any_spec] * nx_out, out_shape=out_shape + host.out_shape,
        scratch_shapes=list(scratch_shapes) + host.sems(),
        compiler_params=host.cparams(dimension_semantics=sem, vmem_limit_bytes=VMEM_LIMIT))(*args, *host.ins)
    host.outs = outs[n_out:]
    return outs[0] if single else outs[:n_out]


NN, NT, TN = ((1,), (0,)), ((1,), (1,)), ((0,), (0,))


def _shift_dims(dims, batch):
    if not batch:
        return (dims, ((), ()))
    return (((dims[0][0] + 1,), (dims[1][0] + 1,)), ((0,), (0,)))


def _make_mm(dims, exact, batch=False):
    def raw(a, b, d):
        dn = _shift_dims(d, batch)
        if exact == "split_lhs":
            ah, bh = a.astype(BF16), b.astype(BF16)
            al = (a - ah.astype(F32)).astype(BF16)
            return lax.dot_general(ah, bh, dn, preferred_element_type=F32) + lax.dot_general(al, bh, dn,
                                                                                              preferred_element_type=F32)
        if exact == "split":
            ah, bh = a.astype(BF16), b.astype(BF16)
            al, bl = (a - ah.astype(F32)).astype(BF16), (b - bh.astype(F32)).astype(BF16)
            dot = lambda p, q: lax.dot_general(p, q, dn, preferred_element_type=F32)
            return dot(ah, bh) + (dot(ah, bl) + dot(al, bh))
        if exact:
            return lax.dot_general(a.astype(F32), b.astype(F32), dn, precision=HI, preferred_element_type=F32)
        return lax.dot_general(a.astype(BF16), b.astype(BF16), dn, preferred_element_type=F32)

    @jax.custom_vjp
    def mm(a, b):
        return raw(a, b, dims)

    def fwd(a, b):
        return raw(a, b, dims), (a, b)

    def bwd(res, g):
        a, b = res
        if dims == NN:
            da, db = raw(g, b, NT), raw(a, g, TN)
        elif dims == NT:
            da, db = raw(g, b, NN), raw(g, a, TN)
        else:
            da, db = raw(b, g, NT), raw(a, g, NN)
        return da.astype(a.dtype), db.astype(b.dtype)

    mm.defvjp(fwd, bwd)
    return mm


_mm = _make_mm(NN, False)
_mm_nt = _make_mm(NT, False)
_mm_tn = _make_mm(TN, False)
_mmx = _make_mm(NN, "split_lhs")
_bmm = _make_mm(NN, False, batch=True)
_bmm_nt = _make_mm(NT, False, batch=True)
_bmm_tn = _make_mm(TN, False, batch=True)
_bmmx = _make_mm(NN, True, batch=True)
_bmms = _make_mm(NN, "split", batch=True)
_bmms_nt = _make_mm(NT, "split", batch=True)
_bmms_tn = _make_mm(TN, "split", batch=True)


@jax.custom_vjp
def _unit_lower_inv(a):
    n = a.shape[-1]
    eye = (lax.broadcasted_iota(jnp.int32, a.shape, 1) == lax.broadcasted_iota(jnp.int32, a.shape, 2)).astype(F32)
    x = -a
    t = eye + x
    for _ in range(max(n.bit_length() - 2, 0)):
        x = _bmms(x, x)
        t = t + _bmms(t, x)
    return t


def _unit_lower_inv_fwd(a):
    t = _unit_lower_inv(a)
    return t, t


def _unit_lower_inv_bwd(t, g):
    return (-_bmms_tn(t, _bmms_nt(g, t)),)


_unit_lower_inv.defvjp(_unit_lower_inv_fwd, _unit_lower_inv_bwd)


def _scan_chunk_rows(x, reverse):
    nb, c, w = x.shape
    y = x.reshape(nb * c, w)
    pos = lax.broadcasted_iota(jnp.int32, y.shape, 0) % c
    step = 1
    while step < c:
        if reverse:
            y = y + jnp.where(pos < c - step, pltpu.roll(y, nb * c - step, 0), 0.0)
        else:
            y = y + jnp.where(pos >= step, pltpu.roll(y, step, 0), 0.0)
        step *= 2
    return y.reshape(nb, c, w)


@jax.custom_vjp
def _chunk_cumsum(x):
    return _scan_chunk_rows(x, False)


_chunk_cumsum.defvjp(lambda x: (_scan_chunk_rows(x, False), None), lambda _, g: (_scan_chunk_rows(g, True),))


def _rms(x, w, n):
    ms = jnp.sum(x * x, axis=-1, keepdims=True) * (1.0 / n)
    return x * lax.rsqrt(ms + NORM_EPS) * w


def _silu(x):
    return x * jax.nn.sigmoid(x)


def _softplus(x):
    return jnp.maximum(x, 0.0) + jnp.log(1.0 + jnp.exp(-jnp.abs(x)))


def _rope(x, cos, sin, perm):
    return x * cos + _mmx(x, perm) * sin


def _mla_prep_fn(rows, consts):
    q_lat, kv_lat, k_pe, cos, sin = rows
    qn = _rms(q_lat, consts["qa_w"], LORA)
    kvn = _rms(kv_lat, consts["kva_w"], LORA)
    outs = []
    for h in range(HEADS):
        q_n = _mm_nt(qn, consts["wq_n"][h])
        q_r = _mm_nt(qn, consts["wq_r"][h])
        rs = lax.rsqrt((jnp.sum(q_n * q_n, -1, keepdims=True) + jnp.sum(q_r * q_r, -1, keepdims=True)) * (1.0 / QK_DIM)
                       + NORM_EPS)
        q_n = q_n * rs * consts["qn_n"]
        q_r = _rope(q_r * rs * consts["qn_r"], cos, sin, consts["perm"])
        k_n = _mm_nt(kvn, consts["wk_n"][h])
        v = _mm_nt(kvn, consts["wv"][h])
        rk = lax.rsqrt((jnp.sum(k_n * k_n, -1, keepdims=True) + jnp.sum(k_pe * k_pe, -1, keepdims=True)) * (1.0 / QK_DIM)
                       + NORM_EPS)
        k_n = k_n * rk * consts["kn_n"]
        k_r = _rope(k_pe * rk * consts["kn_r"], cos, sin, consts["perm"])
        outs += [q_n, q_r, k_n, k_r, v]
    return tuple(outs)


def _attn_fn(q, k, v, row0):
    s = _mm_nt(q, k) * (1.0 / math.sqrt(QK_DIM))
    qpos = row0 + lax.broadcasted_iota(jnp.int32, s.shape, 0)
    kpos = lax.broadcasted_iota(jnp.int32, s.shape, 1)
    s = jnp.where(kpos <= qpos, s, -1e30)
    m = lax.stop_gradient(jnp.max(s, axis=-1, keepdims=True))
    p = jnp.exp(s - m)
    p = p / jnp.sum(p, axis=-1, keepdims=True)
    return _mm(p, v)


def _dn_prep_fn(rows, consts):
    qc, kc, ab = rows
    a_b = _mmx(ab, consts["sel_a"])
    b_b = _mmx(ab, consts["sel_b"])
    beta = jax.nn.sigmoid(b_b)
    g = -jnp.exp(consts["alog"]) * _softplus(a_b + consts["dtb"])
    qs, ks = [], []
    for h in range(HEADS):
        q, k = qc[h], kc[h]
        qs.append(q * lax.rsqrt(jnp.sum(q * q, -1, keepdims=True) + NORM_EPS))
        ks.append(k * lax.rsqrt(jnp.sum(k * k, -1, keepdims=True) + NORM_EPS))
    return tuple(qs), tuple(ks), g, beta


def _dn_chunk_fn(q, k, v, gb, g64, bb):
    nb = q.shape[0]
    ri = lax.broadcasted_iota(jnp.int32, (nb, CHUNK, CHUNK), 1)
    ci = lax.broadcasted_iota(jnp.int32, (nb, CHUNK, CHUNK), 2)
    tri = ri >= ci
    strict = ri > ci
    tril = tri.astype(F32)
    eye = (ri == ci).astype(F32)
    ones = jnp.ones((nb, CHUNK, CHUNK), F32)
    gc = _chunk_cumsum(gb)
    gc64 = _chunk_cumsum(g64)
    grow = _bmmx(ones, eye * gc64)
    diff = gc64 - grow
    decay = jnp.where(tri, jnp.exp(jnp.where(tri, diff, 0.0)), 0.0)
    kb = k * bb
    vb = v * bb
    a = jnp.where(strict, _bmm_nt(kb, k) * decay, 0.0)
    tinv = _unit_lower_inv(a)
    u = _bmm(tinv, vb)
    w = _bmm(tinv, kb * jnp.exp(gc))
    qs = q * (1.0 / math.sqrt(HEAD))
    qk = _bmm_nt(qs, k) * decay
    qg = qs * jnp.exp(gc)
    glast = jnp.sum(gb, axis=1, keepdims=True)
    kdec = k * jnp.exp(glast - gc)
    n_mat = _bmm_tn(kdec, w)
    b_mat = _bmm_tn(kdec, u)
    q_eff = qg - _bmm(qk, w)
    o_own = _bmm(qk, u)
    return n_mat, b_mat, q_eff, o_own, jnp.exp(glast)


def _dn_rec_fn(s, n_mat, b_mat, eg):
    return s * eg - _mm(n_mat, s) + b_mat


def _dn_o_fn(s, q_eff, o_own):
    return _bmm(q_eff, s) + o_own


def _dn_out_fn(o, z, w):
    return _rms(o, w, HEAD) * _silu(z)


def _row_tile(t, parts=8):
    return t // parts if (t // parts) % 16 == 0 else t


def _tile(n, pref, unit):
    best = n
    for cand in range(unit, min(n, pref) + 1, unit):
        if n % cand == 0:
            best = cand
    return best if best <= pref else n


def _rows_call(name, body, rows, consts, outs, accs, r, host=None):
    rows = [a if isinstance(a, tuple) else (a, a.shape[1], 0) for a in rows]
    t = rows[0][0].shape[0]
    zero = lambda nd: (lambda i: (0,) * nd)
    in_specs = [pl.BlockSpec((r, w), functools.partial(lambda i, b: (i, b), b=blk)) for _, w, blk in rows]
    rows = [a for a, _, _ in rows]
    in_specs += [pl.BlockSpec(a.shape, zero(a.ndim)) for a in consts]
    out_shape = [jax.ShapeDtypeStruct((t, w), dt) for w, dt in outs] + [jax.ShapeDtypeStruct(s, F32) for s in accs]
    out_specs = [pl.BlockSpec((r, w), lambda i: (i, 0)) for w, _ in outs] + [pl.BlockSpec(s, zero(len(s))) for s in accs]
    return _pcall(body, name, (t // r,), in_specs, out_specs, out_shape, [*rows, *consts], ("arbitrary",), host=host)


def _accumulate(ref, val):
    @pl.when(pl.program_id(0) == 0)
    def _():
        ref[...] = jnp.zeros(ref.shape, ref.dtype)

    ref[...] += val


def _matmul(name, a, b, dims, out_dtype, res=None, host=None):
    if dims == "nn":
        (m, k), n = a.shape, b.shape[1]
    elif dims == "nt":
        (m, k), n = a.shape, b.shape[0]
    else:
        (k, m), n = a.shape, b.shape[1]
    tm = _tile(m, 1100, 16) if dims != "tn" else _tile(m, 640, 128)
    tn = _tile(n, 1408, 128)
    if dims == "nn":
        a_spec, b_spec, dn = pl.BlockSpec((tm, k), lambda i, j: (i, 0)), pl.BlockSpec((k, tn), lambda i, j: (0, j)), NN
    elif dims == "nt":
        a_spec, b_spec, dn = pl.BlockSpec((tm, k), lambda i, j: (i, 0)), pl.BlockSpec((tn, k), lambda i, j: (j, 0)), NT
    else:
        a_spec, b_spec, dn = pl.BlockSpec((k, tm), lambda i, j: (0, i)), pl.BlockSpec((k, tn), lambda i, j: (0, j)), TN
    o_spec = pl.BlockSpec((tm, tn), lambda i, j: (i, j))

    def body(*refs):
        a_ref, b_ref, o_ref = refs[0], refs[1], refs[-1]
        acc = lax.dot_general(a_ref[...].astype(BF16), b_ref[...].astype(BF16), (dn, ((), ())),
                              preferred_element_type=F32)
        if res is not None:
            acc = acc + refs[2][...]
        o_ref[...] = acc.astype(out_dtype)

    ins = [a, b] + ([res] if res is not None else [])
    specs = [a_spec, b_spec] + ([o_spec] if res is not None else [])
    return _pcall(body, name, (m // tm, n // tn), specs, o_spec, jax.ShapeDtypeStruct((m, n), out_dtype), ins,
                  ("arbitrary", "arbitrary"), host=host)


def _rms_fwd(name, h, w):
    n = h.shape[1]

    def body(h_ref, w_ref, o_ref):
        o_ref[...] = _rms(h_ref[...], w_ref[...], n).astype(BF16)

    return _rows_call(name, body, [h], [w], [(n, BF16)], [], _row_tile(h.shape[0]))[0]


def _rms_bwd(name, h, w, cts, resid, host=None):
    n = h.shape[1]
    nct = len(cts)

    def body(*refs):
        h_ref, ct_refs, r_ref, w_ref = refs[0], refs[1:1 + nct], refs[1 + nct], refs[2 + nct]
        dh_ref, dh16_ref, dw_ref = refs[-3], refs[-2], refs[-1]
        ct = ct_refs[0][...].astype(F32)
        for c in ct_refs[1:]:
            ct = ct + c[...].astype(F32)
        _, vjp = jax.vjp(lambda x, ww: _rms(x, ww, n), h_ref[...], w_ref[...])
        dh, dw = vjp(ct)
        dh = dh + r_ref[...]
        dh_ref[...] = dh
        dh16_ref[...] = dh.astype(BF16)
        _accumulate(dw_ref, dw)

    return _rows_call(name, body, [h, *cts, resid], [w], [(n, F32), (n, BF16)], [(1, n)], _row_tile(h.shape[0]), host=host)


def _mla_consts_from_refs(qa, wq, kva, wkv, qn, kn, perm):
    f = lambda r: r[...].astype(F32)
    return dict(
        qa_w=f(qa), kva_w=f(kva), perm=f(perm),
        wq_n=[wq[h * QK_PAD:h * QK_PAD + HEAD, :].astype(F32) for h in range(HEADS)],
        wq_r=[wq[h * QK_PAD + HEAD:(h + 1) * QK_PAD, :].astype(F32) for h in range(HEADS)],
        wk_n=[wkv[h * QK_PAD:h * QK_PAD + HEAD, :].astype(F32) for h in range(HEADS)],
        wv=[wkv[h * QK_PAD + HEAD:(h + 1) * QK_PAD, :].astype(F32) for h in range(HEADS)],
        qn_n=qn[:, 0:HEAD], qn_r=qn[:, HEAD:QK_PAD], kn_n=kn[:, 0:HEAD], kn_r=kn[:, HEAD:QK_PAD])


def _mla_prep_fwd(q_lat, kv_lat, k_pe, cos, sin, qa, wq, kva, wkv, qn, kn, perm):
    def body(ql, kvl, kp, c, s, qa_r, wq_r, kva_r, wkv_r, qn_r, kn_r, p_r, q_out, k_out, v_out):
        consts = _mla_consts_from_refs(qa_r, wq_r, kva_r, wkv_r, qn_r, kn_r, p_r)
        outs = _mla_prep_fn((ql[...], kvl[...], kp[...], c[...], s[...]), consts)
        for h in range(HEADS):
            q_n, q_r, k_n, k_r, v = outs[5 * h:5 * h + 5]
            q_out[:, h * QK_PAD:h * QK_PAD + HEAD] = q_n.astype(BF16)
            q_out[:, h * QK_PAD + HEAD:(h + 1) * QK_PAD] = q_r.astype(BF16)
            k_out[:, h * QK_PAD:h * QK_PAD + HEAD] = k_n.astype(BF16)
            k_out[:, h * QK_PAD + HEAD:(h + 1) * QK_PAD] = k_r.astype(BF16)
            v_out[:, h * HEAD:(h + 1) * HEAD] = v.astype(BF16)

    return _rows_call("mla_prep_fwd", body, [q_lat, kv_lat, k_pe, cos, sin], [qa, wq, kva, wkv, qn, kn, perm],
                      [(HEADS * QK_PAD, BF16), (HEADS * QK_PAD, BF16), (DN_WIDTH, BF16)], [], _row_tile(cos.shape[0], 4))


def _mla_prep_bwd(q_lat, kv_lat, k_pe, cos, sin, dq, dk, dv, qa, wq, kva, wkv, qn, kn, perm, host=None):
    def body(ql, kvl, kp, c, s, dq_r, dk_r, dv_r, qa_r, wq_r, kva_r, wkv_r, qn_r, kn_r, p_r,
             dql, dkvl, dkp, dqa, dwq, dkva, dwkv, dqn, dkn):
        consts = _mla_consts_from_refs(qa_r, wq_r, kva_r, wkv_r, qn_r, kn_r, p_r)
        cc, ss, pm = c[...], s[...], consts.pop("perm")
        _, vjp = jax.vjp(lambda rows, cs: _mla_prep_fn((*rows, cc, ss), dict(cs, perm=pm)), (ql[...], kvl[...], kp[...]),
                         consts)
        cts = []
        for h in range(HEADS):
            cts += [dq_r[:, h * QK_PAD:h * QK_PAD + HEAD], dq_r[:, h * QK_PAD + HEAD:(h + 1) * QK_PAD],
                    dk_r[:, h * QK_PAD:h * QK_PAD + HEAD], dk_r[:, h * QK_PAD + HEAD:(h + 1) * QK_PAD],
                    dv_r[:, h * HEAD:(h + 1) * HEAD]]
        (d_ql, d_kvl, d_kp), dc = vjp(tuple(cts))
        dql[...] = d_ql.astype(BF16)
        dkvl[...] = d_kvl.astype(BF16)
        dkp[...] = d_kp.astype(BF16)
        first = pl.program_id(0) == 0

        def acc(ref, sl, val):
            @pl.when(first)
            def _():
                ref[sl] = val

            @pl.when(jnp.logical_not(first))
            def _():
                ref[sl] += val

        full = (slice(None), slice(None))
        acc(dqa, full, dc["qa_w"])
        acc(dkva, full, dc["kva_w"])
        for h in range(HEADS):
            acc(dwq, (slice(h * QK_PAD, h * QK_PAD + HEAD), slice(None)), dc["wq_n"][h])
            acc(dwq, (slice(h * QK_PAD + HEAD, (h + 1) * QK_PAD), slice(None)), dc["wq_r"][h])
            acc(dwkv, (slice(h * QK_PAD, h * QK_PAD + HEAD), slice(None)), dc["wk_n"][h])
            acc(dwkv, (slice(h * QK_PAD + HEAD, (h + 1) * QK_PAD), slice(None)), dc["wv"][h])
        acc(dqn, (slice(None), slice(0, HEAD)), dc["qn_n"])
        acc(dqn, (slice(None), slice(HEAD, QK_PAD)), dc["qn_r"])
        acc(dkn, (slice(None), slice(0, HEAD)), dc["kn_n"])
        acc(dkn, (slice(None), slice(HEAD, QK_PAD)), dc["kn_r"])

    return _rows_call("mla_prep_bwd", body, [q_lat, kv_lat, k_pe, cos, sin, dq, dk, dv],
                      [qa, wq, kva, wkv, qn, kn, perm],
                      [(LORA, BF16), (LORA, BF16), (HEAD, BF16)],
                      [(1, LORA), wq.shape, (1, LORA), wkv.shape, (1, QK_PAD), (1, QK_PAD)], _row_tile(cos.shape[0], 4),
                      host=host)


ATTN_Q_ROWS = 256


def _attn_blocks(t):
    return [(r0, min(ATTN_Q_ROWS, t - r0)) for r0 in range(0, t, ATTN_Q_ROWS)]


def _attn_fwd(q, k, v, host=None):
    t = q.shape[0]

    def body(q_ref, k_ref, v_ref, o_ref):
        for r0, rows in _attn_blocks(t):
            ext = r0 + rows
            o_ref[r0:ext, :] = _attn_fn(q_ref[r0:ext, :], k_ref[0:ext, :], v_ref[0:ext, :], r0)

    qk_spec = pl.BlockSpec((t, QK_PAD), lambda h: (0, h))
    v_spec = pl.BlockSpec((t, HEAD), lambda h: (0, h))
    return _pcall(body, "attn_fwd", (HEADS,), [qk_spec, qk_spec, v_spec], v_spec,
                  jax.ShapeDtypeStruct((t, HEADS * HEAD), F32), [q, k, v], ("arbitrary",), host=host)


def _attn_bwd(q, k, v, do, host=None):
    t = q.shape[0]

    def body(q_ref, k_ref, v_ref, do_ref, dq_ref, dk_ref, dv_ref):
        dk_ref[...] = jnp.zeros(dk_ref.shape, F32)
        dv_ref[...] = jnp.zeros(dv_ref.shape, F32)
        for r0, rows in _attn_blocks(t):
            ext = r0 + rows
            _, vjp = jax.vjp(functools.partial(_attn_fn, row0=r0), q_ref[r0:ext, :].astype(F32),
                             k_ref[0:ext, :].astype(F32), v_ref[0:ext, :].astype(F32))
            dq, dk, dv = vjp(do_ref[r0:ext, :])
            dq_ref[r0:ext, :] = dq
            dk_ref[0:ext, :] += dk
            dv_ref[0:ext, :] += dv

    qk_spec = pl.BlockSpec((t, QK_PAD), lambda h: (0, h))
    v_spec = pl.BlockSpec((t, HEAD), lambda h: (0, h))
    return _pcall(body, "attn_bwd", (HEADS,), [qk_spec, qk_spec, v_spec, v_spec], [qk_spec, qk_spec, v_spec],
                  [jax.ShapeDtypeStruct((t, HEADS * QK_PAD), F32), jax.ShapeDtypeStruct((t, HEADS * QK_PAD), F32),
                   jax.ShapeDtypeStruct((t, HEADS * HEAD), F32)], [q, k, v, do], ("arbitrary",), host=host)


def _mix_out_fwd(o_mla, o_dn, z, w_mla, w_dn):
    def body(om_ref, od_ref, z_ref, wm_ref, wd_ref, o_ref):
        for h in range(HEADS):
            sl = slice(h * HEAD, (h + 1) * HEAD)
            o_ref[:, sl] = _rms(om_ref[:, sl], wm_ref[...], HEAD).astype(BF16)
            o_ref[:, DN_WIDTH + h * HEAD:DN_WIDTH + (h + 1) * HEAD] = _dn_out_fn(od_ref[:, sl], z_ref[:, sl],
                                                                                 wd_ref[...]).astype(BF16)

    return _rows_call("mix_out_fwd", body, [o_mla, o_dn, z], [w_mla, w_dn], [(2 * DN_WIDTH, BF16)], [],
                      _row_tile(o_mla.shape[0]))[0]


def _mix_out_bwd(o_mla, o_dn, z, dmixed, w_mla, w_dn, host=None):
    def body(om_ref, od_ref, z_ref, dm_ref, wm_ref, wd_ref, dom_ref, dod_ref, dz_ref, dwm_ref, dwd_ref):
        dwm = dwd = None
        for h in range(HEADS):
            sl = slice(h * HEAD, (h + 1) * HEAD)
            _, vjp = jax.vjp(lambda o, w: _rms(o, w, HEAD), om_ref[:, sl], wm_ref[...])
            do, dw = vjp(dm_ref[:, sl])
            dom_ref[:, sl] = do
            dwm = dw if dwm is None else dwm + dw
            _, vjp = jax.vjp(_dn_out_fn, od_ref[:, sl], z_ref[:, sl], wd_ref[...])
            do, dz, dw = vjp(dm_ref[:, DN_WIDTH + h * HEAD:DN_WIDTH + (h + 1) * HEAD])
            dod_ref[:, sl] = do
            dz_ref[:, sl] = dz.astype(BF16)
            dwd = dw if dwd is None else dwd + dw
        _accumulate(dwm_ref, dwm)
        _accumulate(dwd_ref, dwd)

    return _rows_call("mix_out_bwd", body, [o_mla, o_dn, z, dmixed], [w_mla, w_dn],
                      [(DN_WIDTH, F32), (DN_WIDTH, F32), (DN_WIDTH, BF16)], [(1, HEAD), (1, HEAD)],
                      _row_tile(o_mla.shape[0]), host=host)


def _shift_down(x, s):
    if s == 0:
        return x
    rows = lax.broadcasted_iota(jnp.int32, x.shape, 0)
    return jnp.where(rows >= s, pltpu.roll(x, s, 0), 0.0)


def _shift_up(x, s):
    if s == 0:
        return x
    t = x.shape[0]
    rows = lax.broadcasted_iota(jnp.int32, x.shape, 0)
    return jnp.where(rows < t - s, pltpu.roll(x, t - s, 0), 0.0)


def _col_call(name, body, cols, taps, outs, tap_outs, cw, host=None):
    t, c = cols[0].shape[0], taps[0].shape[1]
    in_specs = [pl.BlockSpec((t, cw), lambda j: (0, j)) for _ in cols]
    in_specs += [pl.BlockSpec((a.shape[0], cw), lambda j: (0, j)) for a in taps]
    out_shape = [jax.ShapeDtypeStruct((t, c), dt) for dt in outs] + [jax.ShapeDtypeStruct((n, c), F32) for n in tap_outs]
    out_specs = [pl.BlockSpec((t, cw), lambda j: (0, j)) for _ in outs]
    out_specs += [pl.BlockSpec((n, cw), lambda j: (0, j)) for n in tap_outs]
    return _pcall(body, name, (c // cw,), in_specs, out_specs, out_shape, [*cols, *taps], ("arbitrary",), host=host)


CONV_ROWS = 32
HALO = 8


def _window(ref, r0, before, after, t):
    cw = ref.shape[1]
    if isinstance(r0, int) and (r0 == 0 or r0 + CONV_ROWS == t):
        lo, hi = max(r0 - before, 0), min(r0 + CONV_ROWS + after, t)
        parts = [ref[lo:hi, :].astype(F32)]
        if r0 - before < 0:
            parts.insert(0, jnp.zeros((before - r0, cw), F32))
        if r0 + CONV_ROWS + after > t:
            parts.append(jnp.zeros((r0 + CONV_ROWS + after - t, cw), F32))
        return jnp.concatenate(parts, axis=0) if len(parts) > 1 else parts[0]
    return ref[pl.ds(pl.multiple_of(r0 - before, 8), before + CONV_ROWS + after), :].astype(F32)


def _chunk_rows(r0):
    return pl.ds(r0 if isinstance(r0, int) else pl.multiple_of(r0, CONV_ROWS), CONV_ROWS)


def _for_row_chunks(t, step, carry):
    n = t // CONV_ROWS
    carry = step(0, carry)
    if n > 2:
        carry = lax.fori_loop(1, n - 1, lambda i, c: step(i * CONV_ROWS, c), carry)
    return step(t - CONV_ROWS, carry) if n > 1 else carry


def _conv_rows(win, taps, rows):
    width = len(taps)
    acc = taps[width - 1] * win[HALO:HALO + rows]
    for j in range(width - 1):
        acc = acc + taps[j] * pltpu.roll(win, width - 1 - j, 0)[HALO:HALO + rows]
    return acc


def _conv_rows_bwd(win, dpre, taps):
    width, n = len(taps), dpre.shape[0]
    dx = taps[width - 1] * dpre[0:CONV_ROWS]
    dws = [jnp.sum(dpre[0:CONV_ROWS] * win[HALO:HALO + CONV_ROWS], axis=0, keepdims=True)]
    for j in range(width - 2, -1, -1):
        s = width - 1 - j
        dx = dx + taps[j] * pltpu.roll(dpre, n - s, 0)[0:CONV_ROWS]
        dws.insert(0, jnp.sum(dpre[0:CONV_ROWS] * pltpu.roll(win, s, 0)[HALO:HALO + CONV_ROWS], axis=0, keepdims=True))
    return dx, dws


def _dn_conv_fwd(x, w):
    t = x.shape[0]

    def body(x_ref, w_ref, y_ref):
        taps = [w_ref[j:j + 1, :] for j in range(4)]

        def step(r0, carry):
            y_ref[_chunk_rows(r0), :] = _silu(_conv_rows(_window(x_ref, r0, HALO, 0, t), taps, CONV_ROWS))
            return carry

        _for_row_chunks(t, step, 0)

    return _col_call("dn_conv_fwd", body, [x], [w], [F32], [], 256)[0]


def _dn_conv_bwd(x, w, dy):
    t = x.shape[0]

    def body(x_ref, dy_ref, w_ref, dx_ref, dw_ref):
        taps = [w_ref[j:j + 1, :] for j in range(4)]

        def step(r0, dws):
            win = _window(x_ref, r0, HALO, HALO, t)
            pre = _conv_rows(win, taps, CONV_ROWS + HALO)
            sg = jax.nn.sigmoid(pre)
            dpre = _window(dy_ref, r0, 0, HALO, t) * (sg * (1.0 + pre * (1.0 - sg)))
            dx, new = _conv_rows_bwd(win, dpre, taps)
            dx_ref[_chunk_rows(r0), :] = dx.astype(dx_ref.dtype)
            return [a + b for a, b in zip(dws, new)]

        dws = _for_row_chunks(t, step, [jnp.zeros((1, x_ref.shape[1]), F32)] * 4)
        for j in range(4):
            dw_ref[j:j + 1, :] = dws[j]

    return _col_call("dn_conv_bwd", body, [x, dy], [w], [BF16], [4], 256)


def _glu_fwd(gpre, up, w, b, host=None):
    t = gpre.shape[0]

    def body(g_ref, u_ref, w_ref, b_ref, a_ref):
        taps, bias = [w_ref[j:j + 1, :] for j in range(3)], b_ref[...]

        def step(r0, carry):
            gate = _conv_rows(_window(g_ref, r0, HALO, 0, t), taps, CONV_ROWS) + bias
            a_ref[_chunk_rows(r0), :] = (_silu(gate) * u_ref[_chunk_rows(r0), :]).astype(BF16)
            return carry

        _for_row_chunks(t, step, 0)

    return _col_call("glu_fwd", body, [gpre, up], [w, b], [BF16], [], 256, host=host)[0]


def _glu_bwd(gpre, up, w, b, dact):
    t = gpre.shape[0]

    def body(g_ref, u_ref, da_ref, w_ref, b_ref, dg_ref, du_ref, dw_ref, db_ref):
        taps, bias = [w_ref[j:j + 1, :] for j in range(3)], b_ref[...]

        def step(r0, acc):
            win = _window(g_ref, r0, HALO, HALO, t)
            gate = _conv_rows(win, taps, CONV_ROWS + HALO) + bias
            da = _window(da_ref, r0, 0, HALO, t)
            sg = jax.nn.sigmoid(gate)
            du_ref[_chunk_rows(r0), :] = (da[0:CONV_ROWS] * (gate * sg)[0:CONV_ROWS]).astype(BF16)
            dgate = da * _window(u_ref, r0, 0, HALO, t) * (sg * (1.0 + gate * (1.0 - sg)))
            dg, new = _conv_rows_bwd(win, dgate, taps)
            dg_ref[_chunk_rows(r0), :] = dg.astype(BF16)
            new.append(jnp.sum(dgate[0:CONV_ROWS], axis=0, keepdims=True))
            return [a + c for a, c in zip(acc, new)]

        acc = _for_row_chunks(t, step, [jnp.zeros((1, g_ref.shape[1]), F32)] * 4)
        for j in range(3):
            dw_ref[j:j + 1, :] = acc[j]
        db_ref[...] = acc[3]

    return _col_call("glu_bwd", body, [gpre, up, dact], [w, b], [BF16, BF16], [3, 1], 256)


def _dn_prep_consts(sa, sb, al, dt):
    return dict(sel_a=sa[...], sel_b=sb[...], alog=al[...], dtb=dt[...])


def _dn_prep_fwd(conv, ab, sel_a, sel_b, alog, dtb):
    def body(c_ref, ab_ref, sa, sb, al, dt, q_out, k_out, g_out, b_out):
        qc = tuple(c_ref[:, h * HEAD:(h + 1) * HEAD] for h in range(HEADS))
        kc = tuple(c_ref[:, DN_WIDTH + h * HEAD:DN_WIDTH + (h + 1) * HEAD] for h in range(HEADS))
        qs, ks, g, beta = _dn_prep_fn((qc, kc, ab_ref[...]), _dn_prep_consts(sa, sb, al, dt))
        for h in range(HEADS):
            q_out[:, h * HEAD:(h + 1) * HEAD] = qs[h]
            k_out[:, h * HEAD:(h + 1) * HEAD] = ks[h]
        g_out[...] = g
        b_out[...] = beta

    return _rows_call("dn_prep_fwd", body, [conv, ab], [sel_a, sel_b, alog, dtb], [(DN_WIDTH, F32)] * 4, [],
                      _row_tile(conv.shape[0]))


def _dn_prep_bwd(conv, ab, dq, dk, dv, dg, db, sel_a, sel_b, alog, dtb):
    def body(c_ref, ab_ref, dq_r, dk_r, dv_r, dg_r, db_r, sa, sb, al, dt, dc_out, dab_out, dal_out, ddt_out):
        qc = tuple(c_ref[:, h * HEAD:(h + 1) * HEAD] for h in range(HEADS))
        kc = tuple(c_ref[:, DN_WIDTH + h * HEAD:DN_WIDTH + (h + 1) * HEAD] for h in range(HEADS))
        consts = _dn_prep_consts(sa, sb, al, dt)
        sel = dict(sel_a=consts["sel_a"], sel_b=consts["sel_b"])
        _, vjp = jax.vjp(lambda rows, ad: _dn_prep_fn(rows, {**sel, **ad}), (qc, kc, ab_ref[...]),
                         dict(alog=consts["alog"], dtb=consts["dtb"]))
        cq = tuple(dq_r[:, h * HEAD:(h + 1) * HEAD] for h in range(HEADS))
        ck = tuple(dk_r[:, h * HEAD:(h + 1) * HEAD] for h in range(HEADS))
        (dqc, dkc, dab), dad = vjp((cq, ck, dg_r[...], db_r[...]))
        for h in range(HEADS):
            dc_out[:, h * HEAD:(h + 1) * HEAD] = dqc[h]
            dc_out[:, DN_WIDTH + h * HEAD:DN_WIDTH + (h + 1) * HEAD] = dkc[h]
        dc_out[:, 2 * DN_WIDTH:3 * DN_WIDTH] = dv_r[...]
        dab_out[...] = dab.astype(BF16)
        _accumulate(dal_out, dad["alog"])
        _accumulate(ddt_out, dad["dtb"])

    return _rows_call("dn_prep_bwd", body, [conv, ab, dq, dk, dv, dg, db], [sel_a, sel_b, alog, dtb],
                      [(3 * DN_WIDTH, F32), (HEAD, BF16)], [(1, DN_WIDTH), (1, DN_WIDTH)], _row_tile(conv.shape[0]))


def _chunk_batch(t):
    nc = t // CHUNK
    return nc // 2 if nc % 2 == 0 else nc


def _dn_chunk_specs(t, nb):
    rows = nb * CHUNK
    blk = pl.BlockSpec((rows, HEAD), lambda h, b: (b, h))
    vblk = pl.BlockSpec((rows, HEAD), lambda h, b: (b, 2 * HEADS + h))
    mat = pl.BlockSpec((nb, HEAD, HEAD), lambda h, b: (b, h, 0))
    return rows, blk, vblk, mat


def _dn_chunk_fwd(qn, kn, conv, g, beta, host=None):
    t = qn.shape[0]
    nb = _chunk_batch(t)
    rows, blk, vblk, mat = _dn_chunk_specs(t, nb)

    def body(q_ref, k_ref, v_ref, g_ref, b_ref, n_o, b_o, qe_o, oo_o, eg_o):
        r3 = lambda x: x.reshape(nb, CHUNK, x.shape[-1])
        n_mat, b_mat, q_eff, o_own, eg = _dn_chunk_fn(r3(q_ref[...]), r3(k_ref[...]), r3(v_ref[...]), r3(g_ref[...]),
                                                      r3(g_ref[:, 0:CHUNK]), r3(b_ref[...]))
        n_o[...] = n_mat
        b_o[...] = b_mat
        qe_o[...] = q_eff.reshape(rows, HEAD)
        oo_o[...] = o_own.reshape(rows, HEAD)
        eg_o[...] = jnp.broadcast_to(eg, (nb, HEAD, HEAD))

    nc = t // CHUNK
    mats = jax.ShapeDtypeStruct((nc, DN_WIDTH, HEAD), F32)
    rowsd = jax.ShapeDtypeStruct((t, DN_WIDTH), F32)
    return _pcall(body, "dn_chunk_fwd", (HEADS, t // rows), [blk, blk, vblk, blk, blk], [mat, mat, blk, blk, mat],
                  [mats, mats, rowsd, rowsd, mats], [qn, kn, conv, g, beta], ("arbitrary", "arbitrary"), host=host)


def _dn_chunk_bwd(qn, kn, conv, g, beta, sall, gall, dq_eff, do, host=None):
    t = qn.shape[0]
    nb = _chunk_batch(t)
    rows, blk, vblk, mat = _dn_chunk_specs(t, nb)

    def body(q_ref, k_ref, v_ref, g_ref, b_ref, s_ref, ga_ref, dqe_ref, do_ref, dq_o, dk_o, dv_o, dg_o, db_o):
        r3 = lambda x: x.reshape(nb, CHUNK, x.shape[-1])
        _, vjp = jax.vjp(_dn_chunk_fn, r3(q_ref[...]), r3(k_ref[...]), r3(v_ref[...]), r3(g_ref[...]),
                         r3(g_ref[:, 0:CHUNK]), r3(b_ref[...]))
        s, ga = s_ref[...], ga_ref[...]
        d_n = -_bmm_nt(ga, s)
        d_eg = jnp.sum(ga * s, axis=1, keepdims=True)
        dq, dk, dv, dg, dg64, db = vjp((d_n, ga, r3(dqe_ref[...]), r3(do_ref[...]), d_eg))
        for o_ref, val in zip((dq_o, dk_o, dv_o, dg_o, db_o), (dq, dk, dv, dg, db)):
            o_ref[...] = val.reshape(rows, HEAD)
        dg_o[:, 0:CHUNK] += dg64.reshape(rows, CHUNK)

    return _pcall(body, "dn_chunk_bwd", (HEADS, t // rows), [blk, blk, vblk, blk, blk, mat, mat, blk, blk], [blk] * 5,
                  [jax.ShapeDtypeStruct((t, DN_WIDTH), F32)] * 5, [qn, kn, conv, g, beta, sall, gall, dq_eff, do],
                  ("arbitrary", "arbitrary"), host=host)


def _dn_rec_fwd(n_mat, b_mat, eg, host=None):
    nc = n_mat.shape[0]
    nb = _chunk_batch(nc * CHUNK)
    spec = pl.BlockSpec((nb, DN_WIDTH, HEAD), lambda i: (i, 0, 0))

    def body(n_ref, b_ref, eg_ref, sall_ref, s_scr):
        @pl.when(pl.program_id(0) == 0)
        def _():
            s_scr[...] = jnp.zeros(s_scr.shape, F32)

        for j in range(nb):
            sall_ref[j] = s_scr[...]
            for h in range(HEADS):
                sl = slice(h * HEAD, (h + 1) * HEAD)
                s_scr[sl, :] = _dn_rec_fn(s_scr[sl, :], n_ref[j, sl, :], b_ref[j, sl, :],
                                          eg_ref[j, h * HEAD:h * HEAD + 1, :])

    return _pcall(body, "dn_rec_fwd", (nc // nb,), [spec] * 3, spec, jax.ShapeDtypeStruct((nc, DN_WIDTH, HEAD), F32),
                  [n_mat, b_mat, eg], ("arbitrary",), scratch_shapes=[pltpu.VMEM((DN_WIDTH, HEAD), F32)], host=host)


def _dn_rec_bwd(n_mat, eg, ds_out, host=None):
    nc = n_mat.shape[0]
    nb = _chunk_batch(nc * CHUNK)
    steps = nc // nb
    spec = pl.BlockSpec((nb, DN_WIDTH, HEAD), lambda i: (steps - 1 - i, 0, 0))

    def body(n_ref, eg_ref, dso_ref, gall_ref, g_scr):
        @pl.when(pl.program_id(0) == 0)
        def _():
            g_scr[...] = jnp.zeros(g_scr.shape, F32)

        for j in reversed(range(nb)):
            gall_ref[j] = g_scr[...]
            for h in range(HEADS):
                sl = slice(h * HEAD, (h + 1) * HEAD)
                gv = g_scr[sl, :]
                g_scr[sl, :] = (gv * eg_ref[j, h * HEAD:h * HEAD + 1, :] - _mm_tn(n_ref[j, sl, :], gv)
                                + dso_ref[j, sl, :])

    return _pcall(body, "dn_rec_bwd", (steps,), [spec] * 3, spec, jax.ShapeDtypeStruct((nc, DN_WIDTH, HEAD), F32),
                  [n_mat, eg, ds_out], ("arbitrary",), scratch_shapes=[pltpu.VMEM((DN_WIDTH, HEAD), F32)], host=host)


def _dn_o_fwd(sall, q_eff, o_own):
    t = q_eff.shape[0]
    nb = _chunk_batch(t)
    rows, blk, _, mat = _dn_chunk_specs(t, nb)

    def body(s_ref, qe_ref, oo_ref, o_ref):
        r3 = lambda x: x.reshape(nb, CHUNK, HEAD)
        o_ref[...] = _dn_o_fn(s_ref[...], r3(qe_ref[...]), r3(oo_ref[...])).reshape(rows, HEAD)

    return _pcall(body, "dn_o_fwd", (HEADS, t // rows), [mat, blk, blk], blk, jax.ShapeDtypeStruct((t, DN_WIDTH), F32),
                  [sall, q_eff, o_own], ("arbitrary", "arbitrary"))


def _dn_o_bwd(sall, q_eff, do, host=None):
    t = q_eff.shape[0]
    nb = _chunk_batch(t)
    rows, blk, _, mat = _dn_chunk_specs(t, nb)

    def body(s_ref, qe_ref, do_ref, dqe_ref, ds_ref):
        r3 = lambda x: x.reshape(nb, CHUNK, HEAD)
        dov = r3(do_ref[...])
        dqe_ref[...] = _bmm_nt(dov, s_ref[...]).reshape(rows, HEAD)
        ds_ref[...] = _bmm_tn(r3(qe_ref[...]), dov)

    nc = t // CHUNK
    return _pcall(body, "dn_o_bwd", (HEADS, t // rows), [mat, blk, blk], [blk, mat],
                  [jax.ShapeDtypeStruct((t, DN_WIDTH), F32), jax.ShapeDtypeStruct((nc, DN_WIDTH, HEAD), F32)],
                  [sall, q_eff, do], ("arbitrary", "arbitrary"), host=host)


def _loss_call(h2, tgt, n_valid):
    t, n = h2.shape
    r = _row_tile(t)

    def body(h_ref, t_ref, dy_ref, dy16_ref, acc_ref):
        rows = pl.program_id(0) * r + lax.broadcasted_iota(jnp.int32, (r, n), 0)
        valid = jnp.logical_and(rows >= N_META, rows < n_valid)
        e = jnp.where(valid, h_ref[...] - t_ref[...], 0.0)
        dy = e * (1.0 / n)
        dy_ref[...] = dy
        dy16_ref[...] = dy.astype(BF16)
        _accumulate(acc_ref, jnp.sum(e * e, axis=0, keepdims=True))

    return _rows_call("loss", body, [h2, tgt], [], [(n, F32), (n, BF16)], [(1, n)], r)


def _adamw_update(w, g, m, v):
    m2 = ADAM_B1 * m + (1.0 - ADAM_B1) * g
    v2 = ADAM_B2 * v + (1.0 - ADAM_B2) * (g * g)
    m_hat = m2 / (1.0 - ADAM_B1 ** ADAM_STEP)
    v_hat = v2 / (1.0 - ADAM_B2 ** ADAM_STEP)
    return -ADAM_LR * (m_hat / (jnp.sqrt(v_hat) + ADAM_EPS) + ADAM_WD * w), m2, v2


def _adamw_small(ws, gs, ms, vs):
    n = len(ws)

    def body(*refs):
        for i in range(n):
            d, m2, v2 = _adamw_update(refs[i][...], refs[n + i][...], refs[2 * n + i][...], refs[3 * n + i][...])
            refs[4 * n + i][...] = d
            refs[5 * n + i][...] = m2
            refs[6 * n + i][...] = v2

    shapes = [jax.ShapeDtypeStruct(a.shape, F32) for a in ws]
    outs = pl.pallas_call(body, name="adamw_small", out_shape=shapes * 3,
                          compiler_params=pltpu.CompilerParams(vmem_limit_bytes=VMEM_LIMIT))(*ws, *gs, *ms, *vs)
    return outs[:n], outs[n:2 * n], outs[2 * n:]


def _adamw_call(name, w, g, m, v, host=None):
    rows, cols = w.shape
    by_rows = rows % 8 == 0

    def body(w_ref, g_ref, m_ref, v_ref, g_out, d_ref, m_out, v_out):
        gv = g_ref[...] if by_rows else g_ref[0:rows, :]
        g_out[...] = gv
        d_ref[...], m_out[...], v_out[...] = _adamw_update(w_ref[...], gv, m_ref[...], v_ref[...])

    if by_rows:
        tr = _tile(rows, 256, 8)
        spec = g_spec = pl.BlockSpec((tr, cols), lambda i: (i, 0))
        grid = (rows // tr,)
    else:
        tc = _tile(cols, 256, 128)
        spec = pl.BlockSpec((rows, tc), lambda j: (0, j))
        g_spec = pl.BlockSpec((g.shape[0], tc), lambda j: (0, j))
        grid = (cols // tc,)
    return _pcall(body, name, grid, [spec, g_spec, spec, spec], [spec] * 4, [jax.ShapeDtypeStruct((rows, cols), F32)] * 4,
                  [w, g, m, v], ("arbitrary",), host=host)


def _rope_tables(t):
    half = ROPE // 2
    inv_freq = np.float32(ROPE_THETA) ** (-np.arange(half, dtype=np.float32) / np.float32(half))
    ang = np.arange(t, dtype=np.float32)[:, None] * inv_freq[None, :].astype(np.float32)
    z = np.zeros((t, HEAD - ROPE), np.float32)
    cos = np.concatenate([np.cos(ang), np.cos(ang), z], axis=1).astype(np.float32)
    sin = np.concatenate([np.sin(ang), np.sin(ang), z], axis=1).astype(np.float32)
    k = np.arange(HEAD)[:, None]
    l = np.arange(HEAD)[None, :]
    perm = np.where((l < half) & (k == l + half), -1.0, 0.0) + np.where((l >= half) & (l < ROPE) & (k == l - half), 1.0, 0.0)
    return jnp.asarray(cos), jnp.asarray(sin), jnp.asarray(perm.astype(np.float32))


def _win_to_pad(w):
    z = lambda n: jnp.zeros((n, w.shape[1]), w.dtype)
    return jnp.concatenate([w[576:2112], w[2112:2624], w[0:256], w[256:512], w[512:576], z(64), w[2624:2632], z(120)],
                           axis=0)


def _win_from_pad(g):
    return jnp.concatenate([g[2048:2304], g[2304:2560], g[2560:2624], g[0:1536], g[1536:2048], g[2688:2696]], axis=0)


def _qk_to_pad(w):
    w4 = w.reshape(HEADS, QK_DIM, w.shape[-1])
    return jnp.concatenate([w4, jnp.zeros((HEADS, QK_PAD - QK_DIM, w.shape[-1]), w.dtype)], axis=1).reshape(
        HEADS * QK_PAD, w.shape[-1])


def _qk_from_pad(g):
    return g.reshape(HEADS, QK_PAD, g.shape[-1])[:, :QK_DIM].reshape(HEADS * QK_DIM, g.shape[-1])


def _ff_to_pad(a, axis):
    shape = list(a.shape)
    shape[axis:axis + 1] = [N_CHIPS, FF_SHARD]
    a4 = a.reshape(shape)
    shape[axis + 1] = FF_BLOCK - FF_SHARD
    out = jnp.concatenate([a4, jnp.zeros(shape, a.dtype)], axis=axis + 1)
    shape[axis:axis + 2] = [D_FF_P]
    return out.reshape(shape)


def _ff_from_pad(a, axis):
    shape = list(a.shape)
    shape[axis:axis + 1] = [N_CHIPS, FF_BLOCK]
    a4 = lax.slice_in_dim(a.reshape(shape), 0, FF_SHARD, axis=axis + 1)
    shape[axis:axis + 2] = [D_FF]
    return a4.reshape(shape)


class _LocalPlan:
    def __init__(self, wt):
        self.wt, self.grads = wt, {}

    def weight(self, name):
        return self.wt[name]

    def host(self, point):
        return None

    def grad(self, name, value):
        self.grads[name] = value


def _local_step(x, tgt, wt, plan=None):
    plan = _LocalPlan(wt) if plan is None else plan
    s = x.shape[0]
    n_valid = N_META + s
    t = -(-n_valid // HEAD) * HEAD
    zpad = jnp.zeros((t - n_valid, D_MODEL), F32)
    h0 = jnp.concatenate([wt["meta_tokens"], x, zpad], axis=0)
    tgt_p = jnp.concatenate([jnp.zeros((N_META, D_MODEL), F32), tgt, zpad], axis=0)
    cos, sin, perm = _rope_tables(t)
    win, wq, wkv = wt["w_in_t"], wt["w_q_t"], wt["w_kv_t"]
    qn_w = jnp.concatenate([wt["q_norm_w"], jnp.zeros((1, QK_PAD - QK_DIM), F32)], axis=1)
    kn_w = jnp.concatenate([wt["k_norm_w"], jnp.zeros((1, QK_PAD - QK_DIM), F32)], axis=1)
    head_id = jnp.arange(DN_WIDTH)[None, :] // HEAD
    lane = jnp.arange(HEAD)[:, None]
    sel_a = (lane == head_id).astype(F32)
    sel_b = (lane == head_id + HEADS).astype(F32)
    alog = jnp.repeat(wt["dn_A_log"], HEAD, axis=1)
    dtb = jnp.repeat(wt["dn_dt_bias"], HEAD, axis=1)
    conv_w, conv_b = wt["ffn_conv_w"], wt["ffn_conv_b"]

    u = _rms_fwd("attn_norm_fwd", h0, wt["attn_norm_w"])
    proj = _matmul("in_proj", u, win, "nt", F32)
    z = (proj, DN_WIDTH, 3)
    q_lat, kv_lat, k_pe, ab = (proj, LORA, 8), (proj, LORA, 9), (proj, HEAD, 20), (proj, HEAD, 21)
    mla_consts = (wt["q_a_norm_w"], wq, wt["kv_a_norm_w"], wkv, qn_w, kn_w, perm)
    q, k, v = _mla_prep_fwd(q_lat, kv_lat, k_pe, cos, sin, *mla_consts)
    o_mla = _attn_fwd(q, k, v, host=plan.host("attn_fwd"))
    conv = _dn_conv_fwd(proj, wt["dn_conv_w"])
    dn_consts = (sel_a, sel_b, alog, dtb)
    qn, kn, g, beta = _dn_prep_fwd(conv, ab, *dn_consts)
    n_mat, b_mat, q_eff, o_own, eg = _dn_chunk_fwd(qn, kn, conv, g, beta, host=plan.host("dn_chunk_fwd"))
    sall = _dn_rec_fwd(n_mat, b_mat, eg)
    o_dn = _dn_o_fwd(sall, q_eff, o_own)
    mixed = _mix_out_fwd(o_mla, o_dn, z, wt["mla_out_norm_w"], wt["dn_out_norm_w"])
    w_out = plan.weight("w_out")
    h1 = _matmul("out_proj", mixed, w_out, "nn", F32, res=h0)
    n2 = _rms_fwd("ffn_norm_fwd", h1, wt["ffn_norm_w"])
    w_gate, w_up = plan.weight("w_gate_t"), plan.weight("w_up_t")
    gpre = _matmul("gate_proj", n2, w_gate, "nt", F32, host=plan.host("gate_proj"))
    up = _matmul("up_proj", n2, w_up, "nt", F32, host=plan.host("up_proj"))
    act = _glu_fwd(gpre, up, conv_w, conv_b)
    w_down = plan.weight("w_down")
    h2 = _matmul("down_proj", act, w_down, "nn", F32, res=h1)
    dy, dy16, sq = _loss_call(h2, tgt_p, n_valid)

    grads = {}
    dact = _matmul("down_dx", dy16, w_down, "nt", F32)
    plan.grad("w_down", _matmul("down_dw", act, dy16, "tn", F32))
    dgpre, dup, grads["ffn_conv_w"], grads["ffn_conv_b"] = _glu_bwd(gpre, up, conv_w, conv_b, dact)
    plan.grad("w_gate_t", _matmul("gate_dw", dgpre, n2, "tn", F32))
    plan.grad("w_up_t", _matmul("up_dw", dup, n2, "tn", F32))
    dn2a = _matmul("gate_dx", dgpre, w_gate, "nn", F32, host=plan.host("gate_dx"))
    dn2b = _matmul("up_dx", dup, w_up, "nn", F32, host=plan.host("up_dx"))
    dh1, dh1_16, grads["ffn_norm_w"] = _rms_bwd("ffn_norm_bwd", h1, wt["ffn_norm_w"], [dn2a, dn2b], dy)
    dmixed = _matmul("out_dx", dh1_16, w_out, "nt", F32)
    plan.grad("w_out", _matmul("out_dw", mixed, dh1_16, "tn", F32))
    do_mla, do_dn, dz, grads["mla_out_norm_w"], grads["dn_out_norm_w"] = _mix_out_bwd(
        o_mla, o_dn, z, dmixed, wt["mla_out_norm_w"], wt["dn_out_norm_w"], host=plan.host("mix_out_bwd"))
    dq_eff, ds_out = _dn_o_bwd(sall, q_eff, do_dn)
    gall = _dn_rec_bwd(n_mat, eg, ds_out)
    dqn, dkn, dv_dn, dg, dbeta = _dn_chunk_bwd(qn, kn, conv, g, beta, sall, gall, dq_eff, do_dn,
                                               host=plan.host("dn_chunk_bwd"))
    dconv, dab, dalog, ddtb = _dn_prep_bwd(conv, ab, dqn, dkn, dv_dn, dg, dbeta, *dn_consts)
    grads["dn_A_log"] = jnp.sum(dalog.reshape(HEADS, HEAD), axis=1)[None, :]
    grads["dn_dt_bias"] = jnp.sum(ddtb.reshape(HEADS, HEAD), axis=1)[None, :]
    ddn_pre, grads["dn_conv_w"] = _dn_conv_bwd(proj, wt["dn_conv_w"], dconv)
    dq, dk, dv = _attn_bwd(q, k, v, do_mla, host=plan.host("attn_bwd"))
    dq_lat, dkv_lat, dk_pe, dqa, dwq, dkva, dwkv, dqnw, dknw = _mla_prep_bwd(
        q_lat, kv_lat, k_pe, cos, sin, dq, dk, dv, *mla_consts, host=plan.host("mla_prep_bwd"))
    grads["q_a_norm_w"], grads["kv_a_norm_w"] = dqa, dkva
    plan.grad("w_q_t", dwq)
    plan.grad("w_kv_t", dwkv)
    grads["q_norm_w"], grads["k_norm_w"] = dqnw[:, :QK_DIM], dknw[:, :QK_DIM]
    dproj = jnp.concatenate([ddn_pre, dz, dq_lat, dkv_lat, dk_pe, dab], axis=1)
    plan.grad("w_in_t", _matmul("in_dw", dproj, u, "tn", F32))
    du = _matmul("in_dx", dproj, win, "nn", F32, host=plan.host("in_dx"))
    dh0, _, grads["attn_norm_w"] = _rms_bwd("attn_norm_bwd", h0, wt["attn_norm_w"], [du], dh1,
                                            host=plan.host("attn_norm_bwd"))
    grads["meta_tokens"] = dh0[0:N_META]
    if isinstance(plan, _LocalPlan):
        grads.update(plan.grads)
    return sq, dh0[N_META:n_valid], grads


def _mesh_pos():
    return lax.axis_index("x"), lax.axis_index("y"), lax.axis_index("c")


def _other_chips(x, y):
    return [(1 - x, y), (x, 1 - y), (1 - x, 1 - y)]


def _remote(src, dst, send_sems, recv_sems, k, to):
    return pltpu.make_async_remote_copy(src_ref=src, dst_ref=dst, send_sem=send_sems.at[k], recv_sem=recv_sems.at[k],
                                        device_id=to, device_id_type=MESH)


SIBLING_ID, CHIPS_ID, GATHER_ID, ALL_ID = 1, 2, 3, 4


def _sibling_peer():
    x, y, c = _mesh_pos()
    return [(x, y, 1 - c)]


def _chip_peers():
    x, y, c = _mesh_pos()
    return [(qx, qy, c) for qx, qy in _other_chips(x, y)]


def _copies_exchange(make, ins, out_shape, nsem, peers=None, cid=None):
    def prog(in_refs, out_refs, send_sems, recv_sems):
        copies = make(in_refs, out_refs, send_sems, recv_sems)

        def start():
            for cp in copies:
                cp.start()

        def finish():
            for cp in copies:
                cp.wait()

        return start, finish

    return _Exchange(prog, ins, out_shape, nsem, peers, cid)


def _all_gather(shards):
    def prog(srcs, dsts, send_sems, recv_sems):
        x, y, c = _mesh_pos()
        p = 2 * x + y
        sibling = (x, y, 1 - c)
        chips = _other_chips(x, y)
        bufs = tuple((s, d, s.shape[0] // 2) for s, d in zip(srcs, dsts))

        def half(ref, rows, which):
            return ref.at[pl.ds(which * rows, rows), :]

        def copy(i, k, src, dst, to):
            return _remote(src, dst, send_sems, recv_sems, 6 * i + k, to)

        sends = [copy(i, j, half(src, rows, c), half(dst.at[p], rows, c), (*chip, c))
                 for i, (src, dst, rows) in enumerate(bufs) for j, chip in enumerate(chips)]

        def start():
            for cp in sends:
                cp.start()

        def finish():
            passed = []
            for i, (src, dst, rows) in enumerate(bufs):
                for j, (qx, qy) in enumerate(chips):
                    block = half(dst.at[2 * qx + qy], rows, c)
                    copy(i, j, block, block, (x, y, c)).wait_recv()
                    fwd = copy(i, 3 + j, block, block, sibling)
                    fwd.start()
                    passed.append(fwd)
            for i, (src, dst, rows) in enumerate(bufs):
                for j, (qx, qy) in enumerate(chips):
                    block = half(dst.at[2 * qx + qy], rows, 1 - c)
                    copy(i, 3 + j, block, block, (x, y, c)).wait_recv()
            for cp in sends + passed:
                cp.wait_send()

        return start, finish

    return _Exchange(prog, shards, [jax.ShapeDtypeStruct((N_CHIPS, *s.shape), s.dtype) for s in shards], 6 * len(shards),
                     lambda: _sibling_peer() + _chip_peers(), GATHER_ID)


def _gathered(ex):
    p = 2 * lax.axis_index("x") + lax.axis_index("y")
    return [lax.dynamic_update_slice(g, s[None], (p, 0, 0)) for g, s in zip(ex.outs, ex.ins)]


def _rs_to_sibling(bufs):
    def make(srcs, dsts, send_sems, recv_sems):
        x, y, c = _mesh_pos()
        copies = []
        for i, (src, dst) in enumerate(zip(srcs, dsts)):
            half = src.shape[1] // 2
            copies.append(_remote(src.at[:, pl.ds((1 - c) * half, half), :], dst, send_sems, recv_sems, i, (x, y, 1 - c)))
        return copies

    return _copies_exchange(make, bufs, [jax.ShapeDtypeStruct((N_CHIPS, b.shape[1] // 2, b.shape[2]), F32) for b in bufs],
                            len(bufs), _sibling_peer, SIBLING_ID)


def _rs_pair_add(name, bufs, gots, c, out_dtype):
    n = len(bufs)

    def body(c_ref, *refs):
        for a_ref, b_ref, o_ref in zip(refs[:n], refs[n:2 * n], refs[2 * n:]):
            o_ref[...] = (a_ref[...] + b_ref[...]).astype(out_dtype)

    mine = [pl.BlockSpec((None, g.shape[1], g.shape[2]), lambda j, cr: (j, cr[0], 0)) for g in gots]
    whole = [pl.BlockSpec((None, g.shape[1], g.shape[2]), lambda j, cr: (j, 0, 0)) for g in gots]
    return pl.pallas_call(
        body, name=name,
        grid_spec=pltpu.PrefetchScalarGridSpec(num_scalar_prefetch=1, grid=(N_CHIPS,), in_specs=mine + whole, out_specs=whole),
        out_shape=[jax.ShapeDtypeStruct(g.shape, out_dtype) for g in gots],
        compiler_params=_cparams(("arbitrary",)))(c, *bufs, *gots)


def _rs_to_chips(accs):
    def make(srcs, dsts, send_sems, recv_sems):
        x, y, c = _mesh_pos()
        return [_remote(src.at[2 * qx + qy], dst.at[k], send_sems, recv_sems, 3 * i + k, (qx, qy, c))
                for i, (src, dst) in enumerate(zip(srcs, dsts)) for k, (qx, qy) in enumerate(_other_chips(x, y))]

    return _copies_exchange(make, accs, [jax.ShapeDtypeStruct((3, a.shape[1], a.shape[2]), a.dtype) for a in accs],
                            3 * len(accs), _chip_peers, CHIPS_ID)


def _rs_chip_add(name, accs, gots, p):
    n = len(accs)
    slot = (0, 1, 0, 2)

    def body(p_ref, *refs):
        me = p_ref[0]
        for own_ref, got_ref, o_ref in zip(refs[:n], refs[n:2 * n], refs[2 * n:]):
            total = None
            for chip in range(N_CHIPS):
                val = own_ref[...].astype(F32)
                for e in (1, 2, 3):
                    val = jnp.where((chip ^ me) == e, got_ref[slot[e]].astype(F32), val)
                total = val if total is None else total + val
            o_ref[...] = total

    own = [pl.BlockSpec((None, a.shape[1], a.shape[2]), lambda i, pr: (pr[0], 0, 0)) for a in accs]
    got = [pl.BlockSpec(g.shape, lambda i, pr: (0, 0, 0)) for g in gots]
    out = [pl.BlockSpec((a.shape[1], a.shape[2]), lambda i, pr: (0, 0)) for a in accs]
    return pl.pallas_call(
        body, name=name,
        grid_spec=pltpu.PrefetchScalarGridSpec(num_scalar_prefetch=1, grid=(1,), in_specs=own + got, out_specs=out),
        out_shape=[jax.ShapeDtypeStruct((a.shape[1], a.shape[2]), F32) for a in accs],
        compiler_params=_cparams(("arbitrary",)))(p, *accs, *gots)


def _rs_share(ress):
    def make(srcs, dsts, send_sems, recv_sems):
        x, y, c = _mesh_pos()
        return [_remote(src, dst, send_sems, recv_sems, i, (x, y, 1 - c)) for i, (src, dst) in enumerate(zip(srcs, dsts))]

    return _copies_exchange(make, ress, [jax.ShapeDtypeStruct(r.shape, F32) for r in ress], len(ress), _sibling_peer,
                            SIBLING_ID)


def _shared(ex):
    south = lax.axis_index("c") == 0
    return [jnp.concatenate([jnp.where(south, r, g), jnp.where(south, g, r)], axis=0) for r, g in zip(ex.ins, ex.outs)]


def _all_to_all_devices(vec):
    def others():
        x, y, c = _mesh_pos()
        return [((1 - x if r & 4 else x), (1 - y if r & 2 else y), (1 - c if r & 1 else c)) for r in range(1, 8)]

    def make(srcs, dsts, send_sems, recv_sems):
        x, y, c = _mesh_pos()
        me = 4 * x + 2 * y + c
        return [_remote(srcs[0], dsts[0].at[me], send_sems, recv_sems, r, peer) for r, peer in enumerate(others())]

    return _copies_exchange(make, [vec], [jax.ShapeDtypeStruct((8, *vec.shape), vec.dtype)], 7, others, ALL_ID)


def _sum_devices(stack):
    def body(s_ref, o_ref):
        total = s_ref[0]
        for d in range(1, 8):
            total = total + s_ref[d]
        o_ref[...] = total

    return pl.pallas_call(body, name="sum_devices", out_shape=jax.ShapeDtypeStruct(stack.shape[1:], F32),
                          compiler_params=pltpu.CompilerParams(vmem_limit_bytes=VMEM_LIMIT))(stack)


def _pad_rows(flat, rows):
    return jnp.concatenate([flat, jnp.zeros((rows * LANES - flat.shape[0],), flat.dtype)]).reshape(rows, LANES)


def _unshard(g4, shape, axis):
    a = g4.reshape(N_CHIPS, *shape)
    if axis == 0:
        return a.reshape(N_CHIPS * shape[0], shape[1])
    return jnp.transpose(a, (1, 0, 2)).reshape(shape[0], N_CHIPS * shape[1])


def _shard4(full, shape, axis):
    if axis == 0:
        return full.reshape(N_CHIPS, shape[0] * shape[1])
    a = full.reshape(shape[0], N_CHIPS, shape[1])
    return jnp.transpose(a, (1, 0, 2)).reshape(N_CHIPS, shape[0] * shape[1])


def _pad_axis0(a, rows):
    return jnp.concatenate([a, jnp.zeros((rows - a.shape[0], *a.shape[1:]), a.dtype)], axis=0)


def _pad_axis1(a, rows):
    return jnp.concatenate([a, jnp.zeros((a.shape[0], rows - a.shape[1], *a.shape[2:]), a.dtype)], axis=1)


def _shard_to_strip(name, w):
    _, (shape, axis, rows) = name, {n: (s, ax, r) for n, s, ax, r in BIG}[name]
    w2 = w.reshape(shape).astype(BF16)
    if name == "w_in":
        return w2
    return _pad_axis0(w2.T if axis == 1 else w2, rows)


LOCAL_NAME = dict(w_in="w_in_t", w_q_b="w_q_t", w_kv_b="w_kv_t", w_out="w_out", w_gate="w_gate_t", w_up="w_up_t",
                  w_down="w_down")


WIN_SEGMENTS = ((576, 2112, 0), (2112, 2624, 1536), (0, 256, 2048), (256, 512, 2304), (512, 576, 2560), (2624, 2632, 2688))


def _strips_to_weight(name, g4):
    if name == "w_in":
        return _win_to_pad(jnp.transpose(g4, (0, 2, 1)).reshape(IN_COLS, D_MODEL))
    if name == "w_q_b":
        return _qk_to_pad(g4.reshape(HEADS * QK_DIM, LORA))
    return g4.reshape(N_CHIPS * g4.shape[1], g4.shape[2])


def _grad_to_strips(name, g):
    if name == "w_in":
        strips = []
        for q in range(N_CHIPS):
            pieces = []
            for a, b, local in sorted(WIN_SEGMENTS):
                s, e = max(a, q * IN_SHARD), min(b, (q + 1) * IN_SHARD)
                if s < e:
                    pieces.append(g[local + s - a:local + e - a])
            pieces.append(jnp.zeros((IN_SHARD_P - IN_SHARD, D_MODEL), g.dtype))
            strips.append(jnp.concatenate(pieces, axis=0))
        return jnp.stack(strips)
    if name == "w_q_b":
        return _qk_from_pad(g).reshape(N_CHIPS, QK_DIM, LORA)
    return g.reshape(N_CHIPS, g.shape[0] // N_CHIPS, g.shape[1])


class _MeshPlan:
    LATE = dict(attn_fwd=("w_up",), dn_chunk_fwd=("w_out", "w_gate"), gate_proj=("w_down/0",), up_proj=("w_down/1",))
    GROUP_A = ("w_down", "w_gate", "w_up", "w_out")
    GROUP_B = ("w_in", "w_q_b", "w_kv_b")

    def __init__(self, w):
        x, y, c = _mesh_pos()
        self.ci = jnp.reshape(c, (1,)).astype(jnp.int32)
        self.pi = jnp.reshape(2 * x + y, (1,)).astype(jnp.int32)
        self.strip = {n: _shard_to_strip(n, w[n]) for n, _, _, _ in BIG}
        self.gathers, self.weights, self.g, self.acc, self.reduced = {}, {}, {}, {}, {}
        self.sibs, self.sib, self.chip, self.share, self.halves = [], None, None, None, [None, None]

    def gather_first(self, small):
        names = ("w_in", "w_q_b", "w_kv_b")
        ex = _all_gather([self.strip[n] for n in names] + [small])
        ex.run("all_gather_first")
        got = _gathered(ex)
        for n, g4 in zip(names, got):
            self.weights[LOCAL_NAME[n]] = _strips_to_weight(n, g4)
        return got[-1]

    def weight(self, local_name):
        if local_name not in self.weights:
            for point, (names, ex) in list(self.gathers.items()):
                if ex.outs is not None:
                    for n, g4 in zip(names, _gathered(ex)):
                        if "/" in n:
                            n, half = n.split("/")
                            self.halves[int(half)] = g4
                            if None in self.halves:
                                continue
                            g4 = jnp.concatenate(self.halves, axis=1)
                        self.weights[LOCAL_NAME[n]] = _strips_to_weight(n, g4)
                    del self.gathers[point]
        return self.weights[local_name]

    def _shard(self, name):
        if "/" not in name:
            return self.strip[name]
        name, half = name.split("/")
        rows = self.strip[name].shape[0] // 2
        return self.strip[name][int(half) * rows:(int(half) + 1) * rows]

    def grad(self, local_name, value):
        name = {v: k for k, v in LOCAL_NAME.items()}[local_name]
        self.g[name] = _grad_to_strips(name, value)

    def _pair_add(self, names, gots):
        accs = _rs_pair_add("rs_pair_add_" + names[0], [self.g[n] for n in names], gots, self.ci, BF16)
        self.acc.update(zip(names, accs))

    def _chip_add(self, names, chip):
        return _rs_chip_add("rs_chip_add_" + names[0], [self.acc[n] for n in names], chip.outs, self.pi)

    def _take_shared(self, names, share):
        for n, strip in zip(names, _shared(share)):
            self.reduced[n] = strip

    def host(self, point):
        a, b = self.GROUP_A, self.GROUP_B
        if point in self.LATE:
            names = self.LATE[point]
            ex = _all_gather([self._shard(n) for n in names])
            self.gathers[point] = (names, ex)
            return ex
        if point in ("gate_dx", "up_dx", "mix_out_bwd"):
            names = dict(gate_dx=a[:2], up_dx=a[2:3], mix_out_bwd=a[3:])[point]
            ex = _rs_to_sibling([self.g[n] for n in names])
            self.sibs.append(ex)
            return ex
        if point == "dn_chunk_bwd":
            self._pair_add(a, [o for ex in self.sibs for o in ex.outs])
            self.chip1 = _rs_to_chips([self.acc[n] for n in a[:2]])
            return self.chip1
        if point == "attn_bwd":
            self.chip2 = _rs_to_chips([self.acc[n] for n in a[2:]])
            return self.chip2
        if point == "mla_prep_bwd":
            ress = self._chip_add(a[:2], self.chip1) + self._chip_add(a[2:], self.chip2)
            self.share = _rs_share(ress)
            return self.share
        if point == "in_dx":
            self._take_shared(a, self.share)
            self.sib = _rs_to_sibling([self.g[n] for n in b])
            return self.sib
        if point == "attn_norm_bwd":
            self._pair_add(b, self.sib.outs)
            self.chip = _rs_to_chips([self.acc[n] for n in b])
            return self.chip
        return None

    def last_share(self):
        self.share = _rs_share(self._chip_add(self.GROUP_B, self.chip))
        return self.share

    def finish(self):
        self._take_shared(self.GROUP_B, self.share)
        return self.reduced


def _strip_to_shard(name, strip):
    shape, axis = {n: (s, ax) for n, s, ax, _ in BIG}[name]
    rows = shape[axis]
    return strip[:rows].T if axis == 1 else strip[:rows]


def kernel(x, meta_tokens, attn_norm_w, w_in, q_a_norm_w, w_q_b, kv_a_norm_w, w_kv_b, q_norm_w, k_norm_w, mla_out_norm_w, dn_conv_w, dn_A_log, dn_dt_bias, dn_out_norm_w, w_out, ffn_norm_w, w_gate, w_up, ffn_conv_w, ffn_conv_b, w_down, loss_target, m_meta_tokens, m_attn_norm_w, m_w_in, m_q_a_norm_w, m_w_q_b, m_kv_a_norm_w, m_w_kv_b, m_q_norm_w, m_k_norm_w, m_mla_out_norm_w, m_dn_conv_w, m_dn_A_log, m_dn_dt_bias, m_dn_out_norm_w, m_w_out, m_ffn_norm_w, m_w_gate, m_w_up, m_ffn_conv_w, m_ffn_conv_b, m_w_down, v_meta_tokens, v_attn_norm_w, v_w_in, v_q_a_norm_w, v_w_q_b, v_kv_a_norm_w, v_w_kv_b, v_q_norm_w, v_k_norm_w, v_mla_out_norm_w, v_dn_conv_w, v_dn_A_log, v_dn_dt_bias, v_dn_out_norm_w, v_w_out, v_ffn_norm_w, v_w_gate, v_w_up, v_ffn_conv_w, v_ffn_conv_b, v_w_down):
    local = dict(locals())
    w = {n: local[n] for n in WEIGHTS}
    m = {n: local["m_" + n] for n in WEIGHTS}
    v = {n: local["v_" + n] for n in WEIGHTS}
    p = 2 * lax.axis_index("x") + lax.axis_index("y")

    plan = _MeshPlan(w)
    wf = _pad_rows(jnp.concatenate([w[n].reshape(-1) for n, _, _ in SMALL_SHARDED]), SMALL_ROWS)
    gf = plan.gather_first(wf).reshape(N_CHIPS, -1)
    full = dict(plan.weights)
    off = 0
    for n, s, ax in SMALL_SHARDED:
        full[n] = _unshard(gf[:, off:off + s[0] * s[1]], s, ax)
        off += s[0] * s[1]
    for n, _ in REPLICATED:
        full[n] = w[n]
    full["ffn_conv_w"] = _ff_to_pad(full["ffn_conv_w"], 1)
    full["ffn_conv_b"] = _ff_to_pad(full["ffn_conv_b"], 1)

    sq, grad_x, g = _local_step(x[0], loss_target[0], full, plan)
    g["ffn_conv_w"] = _ff_from_pad(g["ffn_conv_w"], 1)
    g["ffn_conv_b"] = _ff_from_pad(g["ffn_conv_b"], 1)

    small_all = [n for n, _, _ in SMALL_SHARDED] + [n for n, _ in REPLICATED]
    vec = jnp.concatenate([g[n].reshape(-1) for n in small_all] + [jnp.reshape(0.5 / D_MODEL * jnp.sum(sq), (1,))])
    vec = _pad_rows(vec, -(-vec.shape[0] // (8 * LANES)) * 8)
    a2a = _all_to_all_devices(vec)

    gs, delta, new_m, new_v = {}, {}, {}, {}
    big = {n: (s, ax) for n, s, ax, _ in BIG}

    def adamw_big(n, strips, host=None):
        s, ax = big[n]
        flip = ax == 1 and s[1] % 8 == 0
        there = (lambda a: a.reshape(s).T) if flip else (lambda a: a.reshape(s))
        back = (lambda a: a.T.reshape(w[n].shape)) if flip else (lambda a: a.reshape(w[n].shape))
        strip = strips[n] if flip or ax == 0 else strips[n][:s[1]].T
        g2, d2, m2, v2 = _adamw_call("adamw_" + n, there(w[n]), strip, there(m[n]), there(v[n]), host=host)
        gs[n], delta[n], new_m[n], new_v[n] = back(g2), back(d2), back(m2), back(v2)

    adamw_big("w_down", plan.reduced, host=a2a)
    adamw_big("w_gate", plan.reduced, host=plan.last_share())
    adamw_big("w_up", plan.reduced)
    adamw_big("w_out", plan.reduced)
    strips = plan.finish()
    for n in plan.GROUP_B:
        adamw_big(n, strips)
    me = 4 * lax.axis_index("x") + 2 * lax.axis_index("y") + lax.axis_index("c")
    red = _sum_devices(lax.dynamic_update_slice(a2a.outs[0], vec[None], (me, 0, 0))).reshape(-1)
    off = 0
    for n in small_all:
        tot = red[off:off + g[n].size].reshape(g[n].shape)
        off += g[n].size
        shard = {sn: (s, ax) for sn, s, ax in SMALL_SHARDED}.get(n)
        if shard is not None:
            tot = lax.dynamic_slice_in_dim(tot, p * shard[0][1], shard[0][1], axis=1)
        gs[n] = tot
    loss = red[off]
    two_d = lambda a: a.reshape(a.shape[-2], a.shape[-1])
    outs = _adamw_small([two_d(w[n]) for n in small_all], [two_d(gs[n]) for n in small_all],
                        [two_d(m[n]) for n in small_all], [two_d(v[n]) for n in small_all])
    for i, n in enumerate(small_all):
        for dst, src in ((delta, outs[0]), (new_m, outs[1]), (new_v, outs[2])):
            dst[n] = src[i].reshape(w[n].shape)

    grad_out = [gs[n].reshape(w[n].shape) for n in WEIGHTS]
    return (loss, grad_x[None], *grad_out, *[delta[n] for n in WEIGHTS], *[new_m[n] for n in WEIGHTS],
            *[new_v[n] for n in WEIGHTS])
```

```python
import functools
import math

import jax
import jax.numpy as jnp
import numpy as np
from jax import lax
from jax.experimental import pallas as pl
from jax.experimental.pallas import tpu as pltpu

F32 = jnp.float32
BF16 = jnp.bfloat16
HI = lax.Precision.HIGHEST
MESH = pl.DeviceIdType.MESH

N_META = 16
D_MODEL = 1024
HEADS = 4
HEAD = 128
ROPE = 64
QK_DIM = HEAD + ROPE
QK_PAD = 2 * HEAD
LORA = 256
DN_WIDTH = HEADS * HEAD
CHUNK = 64
D_FF = 2816
N_CHIPS = 4
FF_SHARD = D_FF // N_CHIPS
FF_BLOCK = 768
D_FF_P = N_CHIPS * FF_BLOCK
IN_COLS = 2632
IN_SHARD = IN_COLS // N_CHIPS
IN_SHARD_P = 672
IN_PAD = 2816
NORM_EPS = 1e-6
ROPE_THETA = 10000.0
LANES = 512

ADAM_LR, ADAM_B1, ADAM_B2, ADAM_EPS, ADAM_WD, ADAM_STEP = 0.001, 0.9, 0.999, 1e-08, 0.01, 10

VMEM_LIMIT = 56 * 1024 * 1024

BIG = (("w_in", (1024, 658), 1, IN_SHARD_P), ("w_q_b", (256, 192), 1, 192), ("w_kv_b", (256, 256), 1, 256),
       ("w_out", (256, 1024), 0, 256), ("w_gate", (1024, 704), 1, FF_BLOCK), ("w_up", (1024, 704), 1, FF_BLOCK),
       ("w_down", (704, 1024), 0, FF_BLOCK))
SMALL_SHARDED = (("meta_tokens", (16, 256), 1), ("dn_conv_w", (4, 384), 1), ("ffn_conv_w", (3, 704), 1))
REPLICATED = (("attn_norm_w", 1024), ("q_a_norm_w", 256), ("kv_a_norm_w", 256), ("q_norm_w", 192), ("k_norm_w", 192),
              ("mla_out_norm_w", 128), ("dn_A_log", 4), ("dn_dt_bias", 4), ("dn_out_norm_w", 128), ("ffn_norm_w", 1024),
              ("ffn_conv_b", 2816))
WEIGHTS = ("meta_tokens", "attn_norm_w", "w_in", "q_a_norm_w", "w_q_b", "kv_a_norm_w", "w_kv_b", "q_norm_w", "k_norm_w",
           "mla_out_norm_w", "dn_conv_w", "dn_A_log", "dn_dt_bias", "dn_out_norm_w", "w_out", "ffn_norm_w", "w_gate",
           "w_up", "ffn_conv_w", "ffn_conv_b", "w_down")

SMALL_ROWS = 16
REP_ROWS = 16


def _cparams(sem):
    return pltpu.CompilerParams(dimension_semantics=sem, vmem_limit_bytes=VMEM_LIMIT)


class _Exchange:
    def __init__(self, prog, ins, out_shape, nsem, peers=None, cid=None):
        self.prog, self.ins, self.out_shape, self.nsem = prog, list(ins), list(out_shape), nsem
        self.peers, self.cid = peers, cid
        self.outs = None

    def sems(self):
        return [pltpu.SemaphoreType.DMA((self.nsem,)), pltpu.SemaphoreType.DMA((self.nsem,))]

    def programs(self, in_refs, out_refs, send_sems, recv_sems):
        start, finish = self.prog(in_refs, out_refs, send_sems, recv_sems)
        if self.cid is None:
            return start, finish
        peers = self.peers()

        def shake_and_start():
            barrier = pltpu.get_barrier_semaphore()
            for peer in peers:
                pl.semaphore_signal(barrier, inc=1, device_id=peer, device_id_type=MESH)
            pl.semaphore_wait(barrier, len(peers))
            start()

        return shake_and_start, finish

    def cparams(self, **kw):
        return pltpu.CompilerParams(has_side_effects=True, collective_id=self.cid, **kw)

    def run(self, name):
        any_spec = pl.BlockSpec(memory_space=pl.ANY)
        n = len(self.ins)

        def body(*refs):
            start, finish = self.programs(refs[:n], refs[n:-2], refs[-2], refs[-1])
            start()
            finish()

        self.outs = pl.pallas_call(
            body, name=name, in_specs=[any_spec] * n, out_specs=[any_spec] * len(self.out_shape),
            out_shape=self.out_shape, scratch_shapes=self.sems(), compiler_params=self.cparams())(*self.ins)
        return self.outs


def _pcall(body, name, grid, in_specs, out_specs, out_shape, args, sem, scratch_shapes=(), host=None):
    single = not isinstance(out_shape, (list, tuple))
    out_specs, out_shape = ([out_specs], [out_shape]) if single else (list(out_specs), list(out_shape))
    if host is None:
        outs = pl.pallas_call(body, name=name, grid=grid, in_specs=list(in_specs), out_specs=out_specs, out_shape=out_shape,
                              scratch_shapes=list(scratch_shapes), compiler_params=_cparams(sem))(*args)
        return outs[0] if single else outs
    any_spec = pl.BlockSpec(memory_space=pl.ANY)
    n_in, n_out, n_scr, nx_in, nx_out = len(in_specs), len(out_specs), len(scratch_shapes), len(host.ins), len(host.out_shape)

    def hosted(*refs):
        c_in, x_in = refs[:n_in], refs[n_in:n_in + nx_in]
        o0 = n_in + nx_in
        c_out, x_out = refs[o0:o0 + n_out], refs[o0 + n_out:o0 + n_out + nx_out]
        s0 = o0 + n_out + nx_out
        start, finish = host.programs(x_in, x_out, refs[s0 + n_scr], refs[s0 + n_scr + 1])
        first = functools.reduce(jnp.logical_and, [pl.program_id(d) == 0 for d in range(len(grid))])
        last = functools.reduce(jnp.logical_and, [pl.program_id(d) == grid[d] - 1 for d in range(len(grid))])
        pl.when(first)(start)
        body(*c_in, *c_out, *refs[s0:s0 + n_scr])
        pl.when(last)(finish)

    outs = pl.pallas_call(
        hosted, name=name, grid=grid, in_specs=list(in_specs) + [any_spec] * nx_in,
        out_specs=out_specs + [any_spec] * nx_out, out_shape=out_shape + host.out_shape,
        scratch_shapes=list(scratch_shapes) + host.sems(),
        compiler_params=host.cparams(dimension_semantics=sem, vmem_limit_bytes=VMEM_LIMIT))(*args, *host.ins)
    host.outs = outs[n_out:]
    return outs[0] if single else outs[:n_out]


NN, NT, TN = ((1,), (0,)), ((1,), (1,)), ((0,), (0,))


def _shift_dims(dims, batch):
    if not batch:
        return (dims, ((), ()))
    return (((dims[0][0] + 1,), (dims[1][0] + 1,)), ((0,), (0,)))


def _make_mm(dims, exact, batch=False):
    def raw(a, b, d):
        dn = _shift_dims(d, batch)
        if exact == "split_lhs":
            ah, bh = a.astype(BF16), b.astype(BF16)
            al = (a - ah.astype(F32)).astype(BF16)
            return lax.dot_general(ah, bh, dn, preferred_element_type=F32) + lax.dot_general(al, bh, dn,
                                                                                              preferred_element_type=F32)
        if exact == "split":
            ah, bh = a.astype(BF16), b.astype(BF16)
            al, bl = (a - ah.astype(F32)).astype(BF16), (b - bh.astype(F32)).astype(BF16)
            dot = lambda p, q: lax.dot_general(p, q, dn, preferred_element_type=F32)
            return dot(ah, bh) + (dot(ah, bl) + dot(al, bh))
        if exact:
            return lax.dot_general(a.astype(F32), b.astype(F32), dn, precision=HI, preferred_element_type=F32)
        return lax.dot_general(a.astype(BF16), b.astype(BF16), dn, preferred_element_type=F32)

    @jax.custom_vjp
    def mm(a, b):
        return raw(a, b, dims)

    def fwd(a, b):
        return raw(a, b, dims), (a, b)

    def bwd(res, g):
        a, b = res
        if dims == NN:
            da, db = raw(g, b, NT), raw(a, g, TN)
        elif dims == NT:
            da, db = raw(g, b, NN), raw(g, a, TN)
        else:
            da, db = raw(b, g, NT), raw(a, g, NN)
        return da.astype(a.dtype), db.astype(b.dtype)

    mm.defvjp(fwd, bwd)
    return mm


_mm = _make_mm(NN, False)
_mm_nt = _make_mm(NT, False)
_mm_tn = _make_mm(TN, False)
_mmx = _make_mm(NN, "split_lhs")
_bmm = _make_mm(NN, False, batch=True)
_bmm_nt = _make_mm(NT, False, batch=True)
_bmm_tn = _make_mm(TN, False, batch=True)
_bmmx = _make_mm(NN, True, batch=True)
_bmms = _make_mm(NN, "split", batch=True)
_bmms_nt = _make_mm(NT, "split", batch=True)
_bmms_tn = _make_mm(TN, "split", batch=True)


@jax.custom_vjp
def _unit_lower_inv(a):
    n = a.shape[-1]
    eye = (lax.broadcasted_iota(jnp.int32, a.shape, 1) == lax.broadcasted_iota(jnp.int32, a.shape, 2)).astype(F32)
    x = -a
    t = eye + x
    for _ in range(max(n.bit_length() - 2, 0)):
        x = _bmms(x, x)
        t = t + _bmms(t, x)
    return t


def _unit_lower_inv_fwd(a):
    t = _unit_lower_inv(a)
    return t, t


def _unit_lower_inv_bwd(t, g):
    return (-_bmms_tn(t, _bmms_nt(g, t)),)


_unit_lower_inv.defvjp(_unit_lower_inv_fwd, _unit_lower_inv_bwd)


def _scan_chunk_rows(x, reverse):
    nb, c, w = x.shape
    y = x.reshape(nb * c, w)
    pos = lax.broadcasted_iota(jnp.int32, y.shape, 0) % c
    step = 1
    while step < c:
        if reverse:
            y = y + jnp.where(pos < c - step, pltpu.roll(y, nb * c - step, 0), 0.0)
        else:
            y = y + jnp.where(pos >= step, pltpu.roll(y, step, 0), 0.0)
        step *= 2
    return y.reshape(nb, c, w)


@jax.custom_vjp
def _chunk_cumsum(x):
    return _scan_chunk_rows(x, False)


_chunk_cumsum.defvjp(lambda x: (_scan_chunk_rows(x, False), None), lambda _, g: (_scan_chunk_rows(g, True),))


def _rms(x, w, n):
    ms = jnp.sum(x * x, axis=-1, keepdims=True) * (1.0 / n)
    return x * lax.rsqrt(ms + NORM_EPS) * w


def _silu(x):
    return x * jax.nn.sigmoid(x)


def _softplus(x):
    return jnp.maximum(x, 0.0) + jnp.log(1.0 + jnp.exp(-jnp.abs(x)))


def _rope(x, cos, sin, perm):
    return x * cos + _mmx(x, perm) * sin


def _mla_prep_fn(rows, consts):
    q_lat, kv_lat, k_pe, cos, sin = rows
    qn = _rms(q_lat, consts["qa_w"], LORA)
    kvn = _rms(kv_lat, consts["kva_w"], LORA)
    outs = []
    for h in range(HEADS):
        q_n = _mm_nt(qn, consts["wq_n"][h])
        q_r = _mm_nt(qn, consts["wq_r"][h])
        rs = lax.rsqrt((jnp.sum(q_n * q_n, -1, keepdims=True) + jnp.sum(q_r * q_r, -1, keepdims=True)) * (1.0 / QK_DIM)
                       + NORM_EPS)
        q_n = q_n * rs * consts["qn_n"]
        q_r = _rope(q_r * rs * consts["qn_r"], cos, sin, consts["perm"])
        k_n = _mm_nt(kvn, consts["wk_n"][h])
        v = _mm_nt(kvn, consts["wv"][h])
        rk = lax.rsqrt((jnp.sum(k_n * k_n, -1, keepdims=True) + jnp.sum(k_pe * k_pe, -1, keepdims=True)) * (1.0 / QK_DIM)
                       + NORM_EPS)
        k_n = k_n * rk * consts["kn_n"]
        k_r = _rope(k_pe * rk * consts["kn_r"], cos, sin, consts["perm"])
        outs += [q_n, q_r, k_n, k_r, v]
    return tuple(outs)


def _attn_fn(q, k, v, row0):
    s = _mm_nt(q, k) * (1.0 / math.sqrt(QK_DIM))
    qpos = row0 + lax.broadcasted_iota(jnp.int32, s.shape, 0)
    kpos = lax.broadcasted_iota(jnp.int32, s.shape, 1)
    s = jnp.where(kpos <= qpos, s, -1e30)
    m = lax.stop_gradient(jnp.max(s, axis=-1, keepdims=True))
    p = jnp.exp(s - m)
    p = p / jnp.sum(p, axis=-1, keepdims=True)
    return _mm(p, v)


def _dn_prep_fn(rows, consts):
    qc, kc, ab = rows
    a_b = _mmx(ab, consts["sel_a"])
    b_b = _mmx(ab, consts["sel_b"])
    beta = jax.nn.sigmoid(b_b)
    g = -jnp.exp(consts["alog"]) * _softplus(a_b + consts["dtb"])
    qs, ks = [], []
    for h in range(HEADS):
        q, k = qc[h], kc[h]
        qs.append(q * lax.rsqrt(jnp.sum(q * q, -1, keepdims=True) + NORM_EPS))
        ks.append(k * lax.rsqrt(jnp.sum(k * k, -1, keepdims=True) + NORM_EPS))
    return tuple(qs), tuple(ks), g, beta


def _dn_chunk_fn(q, k, v, gb, g64, bb):
    nb = q.shape[0]
    ri = lax.broadcasted_iota(jnp.int32, (nb, CHUNK, CHUNK), 1)
    ci = lax.broadcasted_iota(jnp.int32, (nb, CHUNK, CHUNK), 2)
    tri = ri >= ci
    strict = ri > ci
    tril = tri.astype(F32)
    eye = (ri == ci).astype(F32)
    ones = jnp.ones((nb, CHUNK, CHUNK), F32)
    gc = _chunk_cumsum(gb)
    gc64 = _chunk_cumsum(g64)
    grow = _bmmx(ones, eye * gc64)
    diff = gc64 - grow
    decay = jnp.where(tri, jnp.exp(jnp.where(tri, diff, 0.0)), 0.0)
    kb = k * bb
    vb = v * bb
    a = jnp.where(strict, _bmm_nt(kb, k) * decay, 0.0)
    tinv = _unit_lower_inv(a)
    u = _bmm(tinv, vb)
    w = _bmm(tinv, kb * jnp.exp(gc))
    qs = q * (1.0 / math.sqrt(HEAD))
    qk = _bmm_nt(qs, k) * decay
    qg = qs * jnp.exp(gc)
    glast = jnp.sum(gb, axis=1, keepdims=True)
    kdec = k * jnp.exp(glast - gc)
    n_mat = _bmm_tn(kdec, w)
    b_mat = _bmm_tn(kdec, u)
    q_eff = qg - _bmm(qk, w)
    o_own = _bmm(qk, u)
    return n_mat, b_mat, q_eff, o_own, jnp.exp(glast)


def _dn_rec_fn(s, n_mat, b_mat, eg):
    return s * eg - _mm(n_mat, s) + b_mat


def _dn_o_fn(s, q_eff, o_own):
    return _bmm(q_eff, s) + o_own


def _dn_out_fn(o, z, w):
    return _rms(o, w, HEAD) * _silu(z)


def _row_tile(t, parts=8):
    return t // parts if (t // parts) % 16 == 0 else t


def _tile(n, pref, unit):
    best = n
    for cand in range(unit, min(n, pref) + 1, unit):
        if n % cand == 0:
            best = cand
    return best if best <= pref else n


def _rows_call(name, body, rows, consts, outs, accs, r, host=None):
    rows = [a if isinstance(a, tuple) else (a, a.shape[1], 0) for a in rows]
    t = rows[0][0].shape[0]
    zero = lambda nd: (lambda i: (0,) * nd)
    in_specs = [pl.BlockSpec((r, w), functools.partial(lambda i, b: (i, b), b=blk)) for _, w, blk in rows]
    rows = [a for a, _, _ in rows]
    in_specs += [pl.BlockSpec(a.shape, zero(a.ndim)) for a in consts]
    out_shape = [jax.ShapeDtypeStruct((t, w), dt) for w, dt in outs] + [jax.ShapeDtypeStruct(s, F32) for s in accs]
    out_specs = [pl.BlockSpec((r, w), lambda i: (i, 0)) for w, _ in outs] + [pl.BlockSpec(s, zero(len(s))) for s in accs]
    return _pcall(body, name, (t // r,), in_specs, out_specs, out_shape, [*rows, *consts], ("arbitrary",), host=host)


def _accumulate(ref, val):
    @pl.when(pl.program_id(0) == 0)
    def _():
        ref[...] = jnp.zeros(ref.shape, ref.dtype)

    ref[...] += val


def _matmul(name, a, b, dims, out_dtype, res=None, host=None):
    if dims == "nn":
        (m, k), n = a.shape, b.shape[1]
    elif dims == "nt":
        (m, k), n = a.shape, b.shape[0]
    else:
        (k, m), n = a.shape, b.shape[1]
    tm = _tile(m, 1100, 16) if dims != "tn" else _tile(m, 640, 128)
    tn = _tile(n, 1408, 128)
    if dims == "nn":
        a_spec, b_spec, dn = pl.BlockSpec((tm, k), lambda i, j: (i, 0)), pl.BlockSpec((k, tn), lambda i, j: (0, j)), NN
    elif dims == "nt":
        a_spec, b_spec, dn = pl.BlockSpec((tm, k), lambda i, j: (i, 0)), pl.BlockSpec((tn, k), lambda i, j: (j, 0)), NT
    else:
        a_spec, b_spec, dn = pl.BlockSpec((k, tm), lambda i, j: (0, i)), pl.BlockSpec((k, tn), lambda i, j: (0, j)), TN
    o_spec = pl.BlockSpec((tm, tn), lambda i, j: (i, j))

    def body(*refs):
        a_ref, b_ref, o_ref = refs[0], refs[1], refs[-1]
        acc = lax.dot_general(a_ref[...].astype(BF16), b_ref[...].astype(BF16), (dn, ((), ())),
                              preferred_element_type=F32)
        if res is not None:
            acc = acc + refs[2][...]
        o_ref[...] = acc.astype(out_dtype)

    ins = [a, b] + ([res] if res is not None else [])
    specs = [a_spec, b_spec] + ([o_spec] if res is not None else [])
    return _pcall(body, name, (m // tm, n // tn), specs, o_spec, jax.ShapeDtypeStruct((m, n), out_dtype), ins,
                  ("arbitrary", "arbitrary"), host=host)


def _rms_fwd(name, h, w, host=None):
    n = h.shape[1]

    def body(h_ref, w_ref, o_ref):
        o_ref[...] = _rms(h_ref[...], w_ref[...], n).astype(BF16)

    return _rows_call(name, body, [h], [w], [(n, BF16)], [], _row_tile(h.shape[0]), host=host)[0]


def _rms_bwd(name, h, w, cts, resid, host=None):
    n = h.shape[1]
    nct = len(cts)

    def body(*refs):
        h_ref, ct_refs, r_ref, w_ref = refs[0], refs[1:1 + nct], refs[1 + nct], refs[2 + nct]
        dh_ref, dh16_ref, dw_ref = refs[-3], refs[-2], refs[-1]
        ct = ct_refs[0][...].astype(F32)
        for c in ct_refs[1:]:
            ct = ct + c[...].astype(F32)
        _, vjp = jax.vjp(lambda x, ww: _rms(x, ww, n), h_ref[...], w_ref[...])
        dh, dw = vjp(ct)
        dh = dh + r_ref[...]
        dh_ref[...] = dh
        dh16_ref[...] = dh.astype(BF16)
        _accumulate(dw_ref, dw)

    return _rows_call(name, body, [h, *cts, resid], [w], [(n, F32), (n, BF16)], [(1, n)], _row_tile(h.shape[0]), host=host)


def _mla_consts_from_refs(qa, wq, kva, wkv, qn, kn, perm):
    f = lambda r: r[...].astype(F32)
    return dict(
        qa_w=f(qa), kva_w=f(kva), perm=f(perm),
        wq_n=[wq[h * QK_PAD:h * QK_PAD + HEAD, :].astype(F32) for h in range(HEADS)],
        wq_r=[wq[h * QK_PAD + HEAD:(h + 1) * QK_PAD, :].astype(F32) for h in range(HEADS)],
        wk_n=[wkv[h * QK_PAD:h * QK_PAD + HEAD, :].astype(F32) for h in range(HEADS)],
        wv=[wkv[h * QK_PAD + HEAD:(h + 1) * QK_PAD, :].astype(F32) for h in range(HEADS)],
        qn_n=qn[:, 0:HEAD], qn_r=qn[:, HEAD:QK_PAD], kn_n=kn[:, 0:HEAD], kn_r=kn[:, HEAD:QK_PAD])


def _mla_prep_fwd(q_lat, kv_lat, k_pe, cos, sin, qa, wq, kva, wkv, qn, kn, perm):
    def body(ql, kvl, kp, c, s, qa_r, wq_r, kva_r, wkv_r, qn_r, kn_r, p_r, q_out, k_out, v_out):
        consts = _mla_consts_from_refs(qa_r, wq_r, kva_r, wkv_r, qn_r, kn_r, p_r)
        outs = _mla_prep_fn((ql[...], kvl[...], kp[...], c[...], s[...]), consts)
        for h in range(HEADS):
            q_n, q_r, k_n, k_r, v = outs[5 * h:5 * h + 5]
            q_out[:, h * QK_PAD:h * QK_PAD + HEAD] = q_n.astype(BF16)
            q_out[:, h * QK_PAD + HEAD:(h + 1) * QK_PAD] = q_r.astype(BF16)
            k_out[:, h * QK_PAD:h * QK_PAD + HEAD] = k_n.astype(BF16)
            k_out[:, h * QK_PAD + HEAD:(h + 1) * QK_PAD] = k_r.astype(BF16)
            v_out[:, h * HEAD:(h + 1) * HEAD] = v.astype(BF16)

    return _rows_call("mla_prep_fwd", body, [q_lat, kv_lat, k_pe, cos, sin], [qa, wq, kva, wkv, qn, kn, perm],
                      [(HEADS * QK_PAD, BF16), (HEADS * QK_PAD, BF16), (DN_WIDTH, BF16)], [], _row_tile(cos.shape[0], 4))


def _mla_prep_bwd(q_lat, kv_lat, k_pe, cos, sin, dq, dk, dv, qa, wq, kva, wkv, qn, kn, perm, host=None):
    def body(ql, kvl, kp, c, s, dq_r, dk_r, dv_r, qa_r, wq_r, kva_r, wkv_r, qn_r, kn_r, p_r,
             dql, dkvl, dkp, dqa, dwq, dkva, dwkv, dqn, dkn):
        consts = _mla_consts_from_refs(qa_r, wq_r, kva_r, wkv_r, qn_r, kn_r, p_r)
        cc, ss, pm = c[...], s[...], consts.pop("perm")
        _, vjp = jax.vjp(lambda rows, cs: _mla_prep_fn((*rows, cc, ss), dict(cs, perm=pm)), (ql[...], kvl[...], kp[...]),
                         consts)
        cts = []
        for h in range(HEADS):
            cts += [dq_r[:, h * QK_PAD:h * QK_PAD + HEAD], dq_r[:, h * QK_PAD + HEAD:(h + 1) * QK_PAD],
                    dk_r[:, h * QK_PAD:h * QK_PAD + HEAD], dk_r[:, h * QK_PAD + HEAD:(h + 1) * QK_PAD],
                    dv_r[:, h * HEAD:(h + 1) * HEAD]]
        (d_ql, d_kvl, d_kp), dc = vjp(tuple(cts))
        dql[...] = d_ql.astype(BF16)
        dkvl[...] = d_kvl.astype(BF16)
        dkp[...] = d_kp.astype(BF16)
        first = pl.program_id(0) == 0

        def acc(ref, sl, val):
            @pl.when(first)
            def _():
                ref[sl] = val

            @pl.when(jnp.logical_not(first))
            def _():
                ref[sl] += val

        full = (slice(None), slice(None))
        acc(dqa, full, dc["qa_w"])
        acc(dkva, full, dc["kva_w"])
        for h in range(HEADS):
            acc(dwq, (slice(h * QK_PAD, h * QK_PAD + HEAD), slice(None)), dc["wq_n"][h])
            acc(dwq, (slice(h * QK_PAD + HEAD, (h + 1) * QK_PAD), slice(None)), dc["wq_r"][h])
            acc(dwkv, (slice(h * QK_PAD, h * QK_PAD + HEAD), slice(None)), dc["wk_n"][h])
            acc(dwkv, (slice(h * QK_PAD + HEAD, (h + 1) * QK_PAD), slice(None)), dc["wv"][h])
        acc(dqn, (slice(None), slice(0, HEAD)), dc["qn_n"])
        acc(dqn, (slice(None), slice(HEAD, QK_PAD)), dc["qn_r"])
        acc(dkn, (slice(None), slice(0, HEAD)), dc["kn_n"])
        acc(dkn, (slice(None), slice(HEAD, QK_PAD)), dc["kn_r"])

    return _rows_call("mla_prep_bwd", body, [q_lat, kv_lat, k_pe, cos, sin, dq, dk, dv],
                      [qa, wq, kva, wkv, qn, kn, perm],
                      [(LORA, BF16), (LORA, BF16), (HEAD, BF16)],
                      [(1, LORA), wq.shape, (1, LORA), wkv.shape, (1, QK_PAD), (1, QK_PAD)], _row_tile(cos.shape[0], 4),
                      host=host)


ATTN_Q_ROWS = 256


def _attn_blocks(t):
    return [(r0, min(ATTN_Q_ROWS, t - r0)) for r0 in range(0, t, ATTN_Q_ROWS)]


def _attn_fwd(q, k, v, host=None):
    t = q.shape[0]

    def body(q_ref, k_ref, v_ref, o_ref):
        for r0, rows in _attn_blocks(t):
            ext = r0 + rows
            o_ref[r0:ext, :] = _attn_fn(q_ref[r0:ext, :], k_ref[0:ext, :], v_ref[0:ext, :], r0)

    qk_spec = pl.BlockSpec((t, QK_PAD), lambda h: (0, h))
    v_spec = pl.BlockSpec((t, HEAD), lambda h: (0, h))
    return _pcall(body, "attn_fwd", (HEADS,), [qk_spec, qk_spec, v_spec], v_spec,
                  jax.ShapeDtypeStruct((t, HEADS * HEAD), F32), [q, k, v], ("arbitrary",), host=host)


def _attn_bwd(q, k, v, do, host=None):
    t = q.shape[0]

    def body(q_ref, k_ref, v_ref, do_ref, dq_ref, dk_ref, dv_ref):
        dk_ref[...] = jnp.zeros(dk_ref.shape, F32)
        dv_ref[...] = jnp.zeros(dv_ref.shape, F32)
        for r0, rows in _attn_blocks(t):
            ext = r0 + rows
            _, vjp = jax.vjp(functools.partial(_attn_fn, row0=r0), q_ref[r0:ext, :].astype(F32),
                             k_ref[0:ext, :].astype(F32), v_ref[0:ext, :].astype(F32))
            dq, dk, dv = vjp(do_ref[r0:ext, :])
            dq_ref[r0:ext, :] = dq
            dk_ref[0:ext, :] += dk
            dv_ref[0:ext, :] += dv

    qk_spec = pl.BlockSpec((t, QK_PAD), lambda h: (0, h))
    v_spec = pl.BlockSpec((t, HEAD), lambda h: (0, h))
    return _pcall(body, "attn_bwd", (HEADS,), [qk_spec, qk_spec, v_spec, v_spec], [qk_spec, qk_spec, v_spec],
                  [jax.ShapeDtypeStruct((t, HEADS * QK_PAD), F32), jax.ShapeDtypeStruct((t, HEADS * QK_PAD), F32),
                   jax.ShapeDtypeStruct((t, HEADS * HEAD), F32)], [q, k, v, do], ("arbitrary",), host=host)


def _mix_out_fwd(o_mla, o_dn, z, w_mla, w_dn):
    def body(om_ref, od_ref, z_ref, wm_ref, wd_ref, o_ref):
        for h in range(HEADS):
            sl = slice(h * HEAD, (h + 1) * HEAD)
            o_ref[:, sl] = _rms(om_ref[:, sl], wm_ref[...], HEAD).astype(BF16)
            o_ref[:, DN_WIDTH + h * HEAD:DN_WIDTH + (h + 1) * HEAD] = _dn_out_fn(od_ref[:, sl], z_ref[:, sl],
                                                                                 wd_ref[...]).astype(BF16)

    return _rows_call("mix_out_fwd", body, [o_mla, o_dn, z], [w_mla, w_dn], [(2 * DN_WIDTH, BF16)], [],
                      _row_tile(o_mla.shape[0]))[0]


def _mix_out_bwd(o_mla, o_dn, z, dmixed, w_mla, w_dn, host=None):
    def body(om_ref, od_ref, z_ref, dm_ref, wm_ref, wd_ref, dom_ref, dod_ref, dz_ref, dwm_ref, dwd_ref):
        dwm = dwd = None
        for h in range(HEADS):
            sl = slice(h * HEAD, (h + 1) * HEAD)
            _, vjp = jax.vjp(lambda o, w: _rms(o, w, HEAD), om_ref[:, sl], wm_ref[...])
            do, dw = vjp(dm_ref[:, sl])
            dom_ref[:, sl] = do
            dwm = dw if dwm is None else dwm + dw
            _, vjp = jax.vjp(_dn_out_fn, od_ref[:, sl], z_ref[:, sl], wd_ref[...])
            do, dz, dw = vjp(dm_ref[:, DN_WIDTH + h * HEAD:DN_WIDTH + (h + 1) * HEAD])
            dod_ref[:, sl] = do
            dz_ref[:, sl] = dz.astype(BF16)
            dwd = dw if dwd is None else dwd + dw
        _accumulate(dwm_ref, dwm)
        _accumulate(dwd_ref, dwd)

    return _rows_call("mix_out_bwd", body, [o_mla, o_dn, z, dmixed], [w_mla, w_dn],
                      [(DN_WIDTH, F32), (DN_WIDTH, F32), (DN_WIDTH, BF16)], [(1, HEAD), (1, HEAD)],
                      _row_tile(o_mla.shape[0]), host=host)


def _shift_down(x, s):
    if s == 0:
        return x
    rows = lax.broadcasted_iota(jnp.int32, x.shape, 0)
    return jnp.where(rows >= s, pltpu.roll(x, s, 0), 0.0)


def _shift_up(x, s):
    if s == 0:
        return x
    t = x.shape[0]
    rows = lax.broadcasted_iota(jnp.int32, x.shape, 0)
    return jnp.where(rows < t - s, pltpu.roll(x, t - s, 0), 0.0)


def _col_call(name, body, cols, taps, outs, tap_outs, cw, host=None):
    t, c = cols[0].shape[0], taps[0].shape[1]
    in_specs = [pl.BlockSpec((t, cw), lambda j: (0, j)) for _ in cols]
    in_specs += [pl.BlockSpec((a.shape[0], cw), lambda j: (0, j)) for a in taps]
    out_shape = [jax.ShapeDtypeStruct((t, c), dt) for dt in outs] + [jax.ShapeDtypeStruct((n, c), F32) for n in tap_outs]
    out_specs = [pl.BlockSpec((t, cw), lambda j: (0, j)) for _ in outs]
    out_specs += [pl.BlockSpec((n, cw), lambda j: (0, j)) for n in tap_outs]
    return _pcall(body, name, (c // cw,), in_specs, out_specs, out_shape, [*cols, *taps], ("arbitrary",), host=host)


def _causal_conv(x, w_ref, width):
    acc = w_ref[width - 1:width, :] * x
    for j in range(width - 1):
        acc = acc + w_ref[j:j + 1, :] * _shift_down(x, width - 1 - j)
    return acc


def _causal_conv_bwd(x, dpre, w_ref, dx_ref, dw_ref, width):
    dx = w_ref[width - 1:width, :] * dpre
    dw_ref[width - 1:width, :] = jnp.sum(dpre * x, axis=0, keepdims=True)
    for j in range(width - 1):
        s = width - 1 - j
        dx = dx + w_ref[j:j + 1, :] * _shift_up(dpre, s)
        dw_ref[j:j + 1, :] = jnp.sum(dpre * _shift_down(x, s), axis=0, keepdims=True)
    dx_ref[...] = dx.astype(dx_ref.dtype)


def _dsilu(x):
    sg = jax.nn.sigmoid(x)
    return sg * (1.0 + x * (1.0 - sg))


def _dn_conv_fwd(x, w):
    def body(x_ref, w_ref, y_ref):
        y_ref[...] = _silu(_causal_conv(x_ref[...], w_ref, 4))

    return _col_call("dn_conv_fwd", body, [x], [w], [F32], [], 256)[0]


def _dn_conv_bwd(x, w, dy):
    def body(x_ref, dy_ref, w_ref, dx_ref, dw_ref):
        xv = x_ref[...]
        dpre = dy_ref[...] * _dsilu(_causal_conv(xv, w_ref, 4))
        _causal_conv_bwd(xv, dpre, w_ref, dx_ref, dw_ref, 4)

    return _col_call("dn_conv_bwd", body, [x, dy], [w], [BF16], [4], 256)


def _glu_fwd(gpre, up, w, b, host=None):
    def body(g_ref, u_ref, w_ref, b_ref, a_ref):
        gate = _causal_conv(g_ref[...], w_ref, 3) + b_ref[...]
        a_ref[...] = (_silu(gate) * u_ref[...]).astype(BF16)

    return _col_call("glu_fwd", body, [gpre, up], [w, b], [BF16], [], 256, host=host)[0]


def _glu_bwd(gpre, up, w, b, dact):
    def body(g_ref, u_ref, da_ref, w_ref, b_ref, dg_ref, du_ref, dw_ref, db_ref):
        gv = g_ref[...]
        gate = _causal_conv(gv, w_ref, 3) + b_ref[...]
        da = da_ref[...]
        sg = jax.nn.sigmoid(gate)
        du_ref[...] = (da * (gate * sg)).astype(BF16)
        dgate = da * u_ref[...] * (sg * (1.0 + gate * (1.0 - sg)))
        db_ref[...] = jnp.sum(dgate, axis=0, keepdims=True)
        _causal_conv_bwd(gv, dgate, w_ref, dg_ref, dw_ref, 3)

    return _col_call("glu_bwd", body, [gpre, up, dact], [w, b], [BF16, BF16], [3, 1], 256)


def _dn_prep_consts(sa, sb, al, dt):
    return dict(sel_a=sa[...], sel_b=sb[...], alog=al[...], dtb=dt[...])


def _dn_prep_fwd(conv, ab, sel_a, sel_b, alog, dtb):
    def body(c_ref, ab_ref, sa, sb, al, dt, q_out, k_out, g_out, b_out):
        qc = tuple(c_ref[:, h * HEAD:(h + 1) * HEAD] for h in range(HEADS))
        kc = tuple(c_ref[:, DN_WIDTH + h * HEAD:DN_WIDTH + (h + 1) * HEAD] for h in range(HEADS))
        qs, ks, g, beta = _dn_prep_fn((qc, kc, ab_ref[...]), _dn_prep_consts(sa, sb, al, dt))
        for h in range(HEADS):
            q_out[:, h * HEAD:(h + 1) * HEAD] = qs[h]
            k_out[:, h * HEAD:(h + 1) * HEAD] = ks[h]
        g_out[...] = g
        b_out[...] = beta

    return _rows_call("dn_prep_fwd", body, [conv, ab], [sel_a, sel_b, alog, dtb], [(DN_WIDTH, F32)] * 4, [],
                      _row_tile(conv.shape[0]))


def _dn_prep_bwd(conv, ab, dq, dk, dv, dg, db, sel_a, sel_b, alog, dtb):
    def body(c_ref, ab_ref, dq_r, dk_r, dv_r, dg_r, db_r, sa, sb, al, dt, dc_out, dab_out, dal_out, ddt_out):
        qc = tuple(c_ref[:, h * HEAD:(h + 1) * HEAD] for h in range(HEADS))
        kc = tuple(c_ref[:, DN_WIDTH + h * HEAD:DN_WIDTH + (h + 1) * HEAD] for h in range(HEADS))
        consts = _dn_prep_consts(sa, sb, al, dt)
        sel = dict(sel_a=consts["sel_a"], sel_b=consts["sel_b"])
        _, vjp = jax.vjp(lambda rows, ad: _dn_prep_fn(rows, {**sel, **ad}), (qc, kc, ab_ref[...]),
                         dict(alog=consts["alog"], dtb=consts["dtb"]))
        cq = tuple(dq_r[:, h * HEAD:(h + 1) * HEAD] for h in range(HEADS))
        ck = tuple(dk_r[:, h * HEAD:(h + 1) * HEAD] for h in range(HEADS))
        (dqc, dkc, dab), dad = vjp((cq, ck, dg_r[...], db_r[...]))
        for h in range(HEADS):
            dc_out[:, h * HEAD:(h + 1) * HEAD] = dqc[h]
            dc_out[:, DN_WIDTH + h * HEAD:DN_WIDTH + (h + 1) * HEAD] = dkc[h]
        dc_out[:, 2 * DN_WIDTH:3 * DN_WIDTH] = dv_r[...]
        dab_out[...] = dab.astype(BF16)
        _accumulate(dal_out, dad["alog"])
        _accumulate(ddt_out, dad["dtb"])

    return _rows_call("dn_prep_bwd", body, [conv, ab, dq, dk, dv, dg, db], [sel_a, sel_b, alog, dtb],
                      [(3 * DN_WIDTH, F32), (HEAD, BF16)], [(1, DN_WIDTH), (1, DN_WIDTH)], _row_tile(conv.shape[0]))


def _chunk_batch(t):
    nc = t // CHUNK
    return nc // 2 if nc % 2 == 0 else nc


def _dn_chunk_specs(t, nb):
    rows = nb * CHUNK
    blk = pl.BlockSpec((rows, HEAD), lambda h, b: (b, h))
    vblk = pl.BlockSpec((rows, HEAD), lambda h, b: (b, 2 * HEADS + h))
    mat = pl.BlockSpec((nb, HEAD, HEAD), lambda h, b: (b, h, 0))
    return rows, blk, vblk, mat


def _dn_chunk_fwd(qn, kn, conv, g, beta, host=None):
    t = qn.shape[0]
    nb = _chunk_batch(t)
    rows, blk, vblk, mat = _dn_chunk_specs(t, nb)

    def body(q_ref, k_ref, v_ref, g_ref, b_ref, n_o, b_o, qe_o, oo_o, eg_o):
        r3 = lambda x: x.reshape(nb, CHUNK, x.shape[-1])
        n_mat, b_mat, q_eff, o_own, eg = _dn_chunk_fn(r3(q_ref[...]), r3(k_ref[...]), r3(v_ref[...]), r3(g_ref[...]),
                                                      r3(g_ref[:, 0:CHUNK]), r3(b_ref[...]))
        n_o[...] = n_mat
        b_o[...] = b_mat
        qe_o[...] = q_eff.reshape(rows, HEAD)
        oo_o[...] = o_own.reshape(rows, HEAD)
        eg_o[...] = jnp.broadcast_to(eg, (nb, HEAD, HEAD))

    nc = t // CHUNK
    mats = jax.ShapeDtypeStruct((nc, DN_WIDTH, HEAD), F32)
    rowsd = jax.ShapeDtypeStruct((t, DN_WIDTH), F32)
    return _pcall(body, "dn_chunk_fwd", (HEADS, t // rows), [blk, blk, vblk, blk, blk], [mat, mat, blk, blk, mat],
                  [mats, mats, rowsd, rowsd, mats], [qn, kn, conv, g, beta], ("arbitrary", "arbitrary"), host=host)


def _dn_chunk_bwd(qn, kn, conv, g, beta, sall, gall, dq_eff, do, host=None):
    t = qn.shape[0]
    nb = _chunk_batch(t)
    rows, blk, vblk, mat = _dn_chunk_specs(t, nb)

    def body(q_ref, k_ref, v_ref, g_ref, b_ref, s_ref, ga_ref, dqe_ref, do_ref, dq_o, dk_o, dv_o, dg_o, db_o):
        r3 = lambda x: x.reshape(nb, CHUNK, x.shape[-1])
        _, vjp = jax.vjp(_dn_chunk_fn, r3(q_ref[...]), r3(k_ref[...]), r3(v_ref[...]), r3(g_ref[...]),
                         r3(g_ref[:, 0:CHUNK]), r3(b_ref[...]))
        s, ga = s_ref[...], ga_ref[...]
        d_n = -_bmm_nt(ga, s)
        d_eg = jnp.sum(ga * s, axis=1, keepdims=True)
        dq, dk, dv, dg, dg64, db = vjp((d_n, ga, r3(dqe_ref[...]), r3(do_ref[...]), d_eg))
        for o_ref, val in zip((dq_o, dk_o, dv_o, dg_o, db_o), (dq, dk, dv, dg, db)):
            o_ref[...] = val.reshape(rows, HEAD)
        dg_o[:, 0:CHUNK] += dg64.reshape(rows, CHUNK)

    return _pcall(body, "dn_chunk_bwd", (HEADS, t // rows), [blk, blk, vblk, blk, blk, mat, mat, blk, blk], [blk] * 5,
                  [jax.ShapeDtypeStruct((t, DN_WIDTH), F32)] * 5, [qn, kn, conv, g, beta, sall, gall, dq_eff, do],
                  ("arbitrary", "arbitrary"), host=host)


def _dn_rec_fwd(n_mat, b_mat, eg, host=None):
    nc = n_mat.shape[0]
    nb = _chunk_batch(nc * CHUNK)
    spec = pl.BlockSpec((nb, DN_WIDTH, HEAD), lambda i: (i, 0, 0))

    def body(n_ref, b_ref, eg_ref, sall_ref, s_scr):
        @pl.when(pl.program_id(0) == 0)
        def _():
            s_scr[...] = jnp.zeros(s_scr.shape, F32)

        for j in range(nb):
            sall_ref[j] = s_scr[...]
            for h in range(HEADS):
                sl = slice(h * HEAD, (h + 1) * HEAD)
                s_scr[sl, :] = _dn_rec_fn(s_scr[sl, :], n_ref[j, sl, :], b_ref[j, sl, :],
                                          eg_ref[j, h * HEAD:h * HEAD + 1, :])

    return _pcall(body, "dn_rec_fwd", (nc // nb,), [spec] * 3, spec, jax.ShapeDtypeStruct((nc, DN_WIDTH, HEAD), F32),
                  [n_mat, b_mat, eg], ("arbitrary",), scratch_shapes=[pltpu.VMEM((DN_WIDTH, HEAD), F32)], host=host)


def _dn_rec_bwd(n_mat, eg, ds_out, host=None):
    nc = n_mat.shape[0]
    nb = _chunk_batch(nc * CHUNK)
    steps = nc // nb
    spec = pl.BlockSpec((nb, DN_WIDTH, HEAD), lambda i: (steps - 1 - i, 0, 0))

    def body(n_ref, eg_ref, dso_ref, gall_ref, g_scr):
        @pl.when(pl.program_id(0) == 0)
        def _():
            g_scr[...] = jnp.zeros(g_scr.shape, F32)

        for j in reversed(range(nb)):
            gall_ref[j] = g_scr[...]
            for h in range(HEADS):
                sl = slice(h * HEAD, (h + 1) * HEAD)
                gv = g_scr[sl, :]
                g_scr[sl, :] = (gv * eg_ref[j, h * HEAD:h * HEAD + 1, :] - _mm_tn(n_ref[j, sl, :], gv)
                                + dso_ref[j, sl, :])

    return _pcall(body, "dn_rec_bwd", (steps,), [spec] * 3, spec, jax.ShapeDtypeStruct((nc, DN_WIDTH, HEAD), F32),
                  [n_mat, eg, ds_out], ("arbitrary",), scratch_shapes=[pltpu.VMEM((DN_WIDTH, HEAD), F32)], host=host)


def _dn_o_fwd(sall, q_eff, o_own):
    t = q_eff.shape[0]
    nb = _chunk_batch(t)
    rows, blk, _, mat = _dn_chunk_specs(t, nb)

    def body(s_ref, qe_ref, oo_ref, o_ref):
        r3 = lambda x: x.reshape(nb, CHUNK, HEAD)
        o_ref[...] = _dn_o_fn(s_ref[...], r3(qe_ref[...]), r3(oo_ref[...])).reshape(rows, HEAD)

    return _pcall(body, "dn_o_fwd", (HEADS, t // rows), [mat, blk, blk], blk, jax.ShapeDtypeStruct((t, DN_WIDTH), F32),
                  [sall, q_eff, o_own], ("arbitrary", "arbitrary"))


def _dn_o_bwd(sall, q_eff, do, host=None):
    t = q_eff.shape[0]
    nb = _chunk_batch(t)
    rows, blk, _, mat = _dn_chunk_specs(t, nb)

    def body(s_ref, qe_ref, do_ref, dqe_ref, ds_ref):
        r3 = lambda x: x.reshape(nb, CHUNK, HEAD)
        dov = r3(do_ref[...])
        dqe_ref[...] = _bmm_nt(dov, s_ref[...]).reshape(rows, HEAD)
        ds_ref[...] = _bmm_tn(r3(qe_ref[...]), dov)

    nc = t // CHUNK
    return _pcall(body, "dn_o_bwd", (HEADS, t // rows), [mat, blk, blk], [blk, mat],
                  [jax.ShapeDtypeStruct((t, DN_WIDTH), F32), jax.ShapeDtypeStruct((nc, DN_WIDTH, HEAD), F32)],
                  [sall, q_eff, do], ("arbitrary", "arbitrary"), host=host)


def _loss_call(h2, tgt, n_valid):
    t, n = h2.shape
    r = _row_tile(t)

    def body(h_ref, t_ref, dy_ref, dy16_ref, acc_ref):
        rows = pl.program_id(0) * r + lax.broadcasted_iota(jnp.int32, (r, n), 0)
        valid = jnp.logical_and(rows >= N_META, rows < n_valid)
        e = jnp.where(valid, h_ref[...] - t_ref[...], 0.0)
        dy = e * (1.0 / n)
        dy_ref[...] = dy
        dy16_ref[...] = dy.astype(BF16)
        _accumulate(acc_ref, jnp.sum(e * e, axis=0, keepdims=True))

    return _rows_call("loss", body, [h2, tgt], [], [(n, F32), (n, BF16)], [(1, n)], r)


def _adamw_update(w, g, m, v):
    m2 = ADAM_B1 * m + (1.0 - ADAM_B1) * g
    v2 = ADAM_B2 * v + (1.0 - ADAM_B2) * (g * g)
    m_hat = m2 / (1.0 - ADAM_B1 ** ADAM_STEP)
    v_hat = v2 / (1.0 - ADAM_B2 ** ADAM_STEP)
    return -ADAM_LR * (m_hat / (jnp.sqrt(v_hat) + ADAM_EPS) + ADAM_WD * w), m2, v2


def _adamw_small(ws, gs, ms, vs):
    n = len(ws)

    def body(*refs):
        for i in range(n):
            d, m2, v2 = _adamw_update(refs[i][...], refs[n + i][...], refs[2 * n + i][...], refs[3 * n + i][...])
            refs[4 * n + i][...] = d
            refs[5 * n + i][...] = m2
            refs[6 * n + i][...] = v2

    shapes = [jax.ShapeDtypeStruct(a.shape, F32) for a in ws]
    outs = pl.pallas_call(body, name="adamw_small", out_shape=shapes * 3,
                          compiler_params=pltpu.CompilerParams(vmem_limit_bytes=VMEM_LIMIT))(*ws, *gs, *ms, *vs)
    return outs[:n], outs[n:2 * n], outs[2 * n:]


def _adamw_call(name, w, g, m, v, host=None):
    rows, cols = w.shape
    by_rows = rows % 8 == 0

    def body(w_ref, g_ref, m_ref, v_ref, g_out, d_ref, m_out, v_out):
        gv = g_ref[...] if by_rows else g_ref[0:rows, :]
        g_out[...] = gv
        d_ref[...], m_out[...], v_out[...] = _adamw_update(w_ref[...], gv, m_ref[...], v_ref[...])

    if by_rows:
        tr = _tile(rows, 256, 8)
        spec = g_spec = pl.BlockSpec((tr, cols), lambda i: (i, 0))
        grid = (rows // tr,)
    else:
        tc = _tile(cols, 256, 128)
        spec = pl.BlockSpec((rows, tc), lambda j: (0, j))
        g_spec = pl.BlockSpec((g.shape[0], tc), lambda j: (0, j))
        grid = (cols // tc,)
    return _pcall(body, name, grid, [spec, g_spec, spec, spec], [spec] * 4, [jax.ShapeDtypeStruct((rows, cols), F32)] * 4,
                  [w, g, m, v], ("arbitrary",), host=host)


def _rope_tables(t):
    half = ROPE // 2
    inv_freq = np.float32(ROPE_THETA) ** (-np.arange(half, dtype=np.float32) / np.float32(half))
    ang = np.arange(t, dtype=np.float32)[:, None] * inv_freq[None, :].astype(np.float32)
    z = np.zeros((t, HEAD - ROPE), np.float32)
    cos = np.concatenate([np.cos(ang), np.cos(ang), z], axis=1).astype(np.float32)
    sin = np.concatenate([np.sin(ang), np.sin(ang), z], axis=1).astype(np.float32)
    k = np.arange(HEAD)[:, None]
    l = np.arange(HEAD)[None, :]
    perm = np.where((l < half) & (k == l + half), -1.0, 0.0) + np.where((l >= half) & (l < ROPE) & (k == l - half), 1.0, 0.0)
    return jnp.asarray(cos), jnp.asarray(sin), jnp.asarray(perm.astype(np.float32))


def _win_to_pad(w):
    z = lambda n: jnp.zeros((n, w.shape[1]), w.dtype)
    return jnp.concatenate([w[576:2112], w[2112:2624], w[0:256], w[256:512], w[512:576], z(64), w[2624:2632], z(120)],
                           axis=0)


def _win_from_pad(g):
    return jnp.concatenate([g[2048:2304], g[2304:2560], g[2560:2624], g[0:1536], g[1536:2048], g[2688:2696]], axis=0)


def _qk_to_pad(w):
    w4 = w.reshape(HEADS, QK_DIM, w.shape[-1])
    return jnp.concatenate([w4, jnp.zeros((HEADS, QK_PAD - QK_DIM, w.shape[-1]), w.dtype)], axis=1).reshape(
        HEADS * QK_PAD, w.shape[-1])


def _qk_from_pad(g):
    return g.reshape(HEADS, QK_PAD, g.shape[-1])[:, :QK_DIM].reshape(HEADS * QK_DIM, g.shape[-1])


def _ff_to_pad(a, axis):
    shape = list(a.shape)
    shape[axis:axis + 1] = [N_CHIPS, FF_SHARD]
    a4 = a.reshape(shape)
    shape[axis + 1] = FF_BLOCK - FF_SHARD
    out = jnp.concatenate([a4, jnp.zeros(shape, a.dtype)], axis=axis + 1)
    shape[axis:axis + 2] = [D_FF_P]
    return out.reshape(shape)


def _ff_from_pad(a, axis):
    shape = list(a.shape)
    shape[axis:axis + 1] = [N_CHIPS, FF_BLOCK]
    a4 = lax.slice_in_dim(a.reshape(shape), 0, FF_SHARD, axis=axis + 1)
    shape[axis:axis + 2] = [D_FF]
    return a4.reshape(shape)


class _LocalPlan:
    def __init__(self, wt):
        self.wt, self.grads = wt, {}

    def weight(self, name):
        return self.wt[name]

    def host(self, point):
        return None

    def grad(self, name, value):
        self.grads[name] = value


def _local_step(x, tgt, wt, plan=None):
    plan = _LocalPlan(wt) if plan is None else plan
    s = x.shape[0]
    n_valid = N_META + s
    t = -(-n_valid // HEAD) * HEAD
    zpad = jnp.zeros((t - n_valid, D_MODEL), F32)
    h0 = jnp.concatenate([wt["meta_tokens"], x, zpad], axis=0)
    tgt_p = jnp.concatenate([jnp.zeros((N_META, D_MODEL), F32), tgt, zpad], axis=0)
    cos, sin, perm = _rope_tables(t)
    qn_w = jnp.concatenate([wt["q_norm_w"], jnp.zeros((1, QK_PAD - QK_DIM), F32)], axis=1)
    kn_w = jnp.concatenate([wt["k_norm_w"], jnp.zeros((1, QK_PAD - QK_DIM), F32)], axis=1)
    head_id = jnp.arange(DN_WIDTH)[None, :] // HEAD
    lane = jnp.arange(HEAD)[:, None]
    sel_a = (lane == head_id).astype(F32)
    sel_b = (lane == head_id + HEADS).astype(F32)
    alog = jnp.repeat(wt["dn_A_log"], HEAD, axis=1)
    dtb = jnp.repeat(wt["dn_dt_bias"], HEAD, axis=1)
    conv_w, conv_b = wt["ffn_conv_w"], wt["ffn_conv_b"]

    u = _rms_fwd("attn_norm_fwd", h0, wt["attn_norm_w"], host=plan.host("attn_norm_fwd"))
    win, wq, wkv = plan.weight("w_in_t"), plan.weight("w_q_t"), plan.weight("w_kv_t")
    proj = _matmul("in_proj", u, win, "nt", F32)
    z = (proj, DN_WIDTH, 3)
    q_lat, kv_lat, k_pe, ab = (proj, LORA, 8), (proj, LORA, 9), (proj, HEAD, 20), (proj, HEAD, 21)
    mla_consts = (wt["q_a_norm_w"], wq, wt["kv_a_norm_w"], wkv, qn_w, kn_w, perm)
    q, k, v = _mla_prep_fwd(q_lat, kv_lat, k_pe, cos, sin, *mla_consts)
    o_mla = _attn_fwd(q, k, v, host=plan.host("attn_fwd"))
    conv = _dn_conv_fwd(proj, wt["dn_conv_w"])
    dn_consts = (sel_a, sel_b, alog, dtb)
    qn, kn, g, beta = _dn_prep_fwd(conv, ab, *dn_consts)
    n_mat, b_mat, q_eff, o_own, eg = _dn_chunk_fwd(qn, kn, conv, g, beta, host=plan.host("dn_chunk_fwd"))
    sall = _dn_rec_fwd(n_mat, b_mat, eg)
    o_dn = _dn_o_fwd(sall, q_eff, o_own)
    mixed = _mix_out_fwd(o_mla, o_dn, z, wt["mla_out_norm_w"], wt["dn_out_norm_w"])
    w_out = plan.weight("w_out")
    h1 = _matmul("out_proj", mixed, w_out, "nn", F32, res=h0)
    n2 = _rms_fwd("ffn_norm_fwd", h1, wt["ffn_norm_w"])
    w_gate, w_up = plan.weight("w_gate_t"), plan.weight("w_up_t")
    gpre = _matmul("gate_proj", n2, w_gate, "nt", F32, host=plan.host("gate_proj"))
    up = _matmul("up_proj", n2, w_up, "nt", F32, host=plan.host("up_proj"))
    act = _glu_fwd(gpre, up, conv_w, conv_b)
    w_down = plan.weight("w_down")
    h2 = _matmul("down_proj", act, w_down, "nn", F32, res=h1)
    dy, dy16, sq = _loss_call(h2, tgt_p, n_valid)

    grads = {}
    dact = _matmul("down_dx", dy16, w_down, "nt", F32)
    plan.grad("w_down", _matmul("down_dw", act, dy16, "tn", F32))
    dgpre, dup, grads["ffn_conv_w"], grads["ffn_conv_b"] = _glu_bwd(gpre, up, conv_w, conv_b, dact)
    plan.grad("w_gate_t", _matmul("gate_dw", dgpre, n2, "tn", F32))
    plan.grad("w_up_t", _matmul("up_dw", dup, n2, "tn", F32))
    dn2a = _matmul("gate_dx", dgpre, w_gate, "nn", F32, host=plan.host("gate_dx"))
    dn2b = _matmul("up_dx", dup, w_up, "nn", F32, host=plan.host("up_dx"))
    dh1, dh1_16, grads["ffn_norm_w"] = _rms_bwd("ffn_norm_bwd", h1, wt["ffn_norm_w"], [dn2a, dn2b], dy)
    dmixed = _matmul("out_dx", dh1_16, w_out, "nt", F32)
    plan.grad("w_out", _matmul("out_dw", mixed, dh1_16, "tn", F32))
    do_mla, do_dn, dz, grads["mla_out_norm_w"], grads["dn_out_norm_w"] = _mix_out_bwd(
        o_mla, o_dn, z, dmixed, wt["mla_out_norm_w"], wt["dn_out_norm_w"], host=plan.host("mix_out_bwd"))
    dq_eff, ds_out = _dn_o_bwd(sall, q_eff, do_dn)
    gall = _dn_rec_bwd(n_mat, eg, ds_out)
    dqn, dkn, dv_dn, dg, dbeta = _dn_chunk_bwd(qn, kn, conv, g, beta, sall, gall, dq_eff, do_dn,
                                               host=plan.host("dn_chunk_bwd"))
    dconv, dab, dalog, ddtb = _dn_prep_bwd(conv, ab, dqn, dkn, dv_dn, dg, dbeta, *dn_consts)
    grads["dn_A_log"] = jnp.sum(dalog.reshape(HEADS, HEAD), axis=1)[None, :]
    grads["dn_dt_bias"] = jnp.sum(ddtb.reshape(HEADS, HEAD), axis=1)[None, :]
    ddn_pre, grads["dn_conv_w"] = _dn_conv_bwd(proj, wt["dn_conv_w"], dconv)
    dq, dk, dv = _attn_bwd(q, k, v, do_mla, host=plan.host("attn_bwd"))
    dq_lat, dkv_lat, dk_pe, dqa, dwq, dkva, dwkv, dqnw, dknw = _mla_prep_bwd(
        q_lat, kv_lat, k_pe, cos, sin, dq, dk, dv, *mla_consts, host=plan.host("mla_prep_bwd"))
    grads["q_a_norm_w"], grads["kv_a_norm_w"] = dqa, dkva
    plan.grad("w_q_t", dwq)
    plan.grad("w_kv_t", dwkv)
    grads["q_norm_w"], grads["k_norm_w"] = dqnw[:, :QK_DIM], dknw[:, :QK_DIM]
    dproj = jnp.concatenate([ddn_pre, dz, dq_lat, dkv_lat, dk_pe, dab], axis=1)
    plan.grad("w_in_t", _matmul("in_dw", dproj, u, "tn", F32))
    du = _matmul("in_dx", dproj, win, "nn", F32, host=plan.host("in_dx"))
    dh0, _, grads["attn_norm_w"] = _rms_bwd("attn_norm_bwd", h0, wt["attn_norm_w"], [du], dh1,
                                            host=plan.host("attn_norm_bwd"))
    grads["meta_tokens"] = dh0[0:N_META]
    if isinstance(plan, _LocalPlan):
        grads.update(plan.grads)
    return sq, dh0[N_META:n_valid], grads


def _mesh_pos():
    return lax.axis_index("x"), lax.axis_index("y"), lax.axis_index("c")


def _other_chips(x, y):
    return [(1 - x, y), (x, 1 - y), (1 - x, 1 - y)]


def _remote(src, dst, send_sems, recv_sems, k, to):
    return pltpu.make_async_remote_copy(src_ref=src, dst_ref=dst, send_sem=send_sems.at[k], recv_sem=recv_sems.at[k],
                                        device_id=to, device_id_type=MESH)


SIBLING_ID, CHIPS_ID, GATHER_ID, ALL_ID = 1, 2, 3, 4


def _sibling_peer():
    x, y, c = _mesh_pos()
    return [(x, y, 1 - c)]


def _chip_peers():
    x, y, c = _mesh_pos()
    return [(qx, qy, c) for qx, qy in _other_chips(x, y)]


def _copies_exchange(make, ins, out_shape, nsem, peers=None, cid=None):
    def prog(in_refs, out_refs, send_sems, recv_sems):
        copies = make(in_refs, out_refs, send_sems, recv_sems)

        def start():
            for cp in copies:
                cp.start()

        def finish():
            for cp in copies:
                cp.wait()

        return start, finish

    return _Exchange(prog, ins, out_shape, nsem, peers, cid)


def _all_gather(shards):
    def prog(srcs, dsts, send_sems, recv_sems):
        x, y, c = _mesh_pos()
        p = 2 * x + y
        sibling = (x, y, 1 - c)
        chips = _other_chips(x, y)
        bufs = tuple((s, d, s.shape[0] // 2) for s, d in zip(srcs, dsts))

        def half(ref, rows, which):
            return ref.at[pl.ds(which * rows, rows), :]

        def copy(i, k, src, dst, to):
            return _remote(src, dst, send_sems, recv_sems, 6 * i + k, to)

        sends = [copy(i, j, half(src, rows, c), half(dst.at[p], rows, c), (*chip, c))
                 for i, (src, dst, rows) in enumerate(bufs) for j, chip in enumerate(chips)]

        def start():
            for cp in sends:
                cp.start()

        def finish():
            passed = []
            for i, (src, dst, rows) in enumerate(bufs):
                for j, (qx, qy) in enumerate(chips):
                    block = half(dst.at[2 * qx + qy], rows, c)
                    copy(i, j, block, block, (x, y, c)).wait_recv()
                    fwd = copy(i, 3 + j, block, block, sibling)
                    fwd.start()
                    passed.append(fwd)
            for i, (src, dst, rows) in enumerate(bufs):
                for j, (qx, qy) in enumerate(chips):
                    block = half(dst.at[2 * qx + qy], rows, 1 - c)
                    copy(i, 3 + j, block, block, (x, y, c)).wait_recv()
            for cp in sends + passed:
                cp.wait_send()

        return start, finish

    return _Exchange(prog, shards, [jax.ShapeDtypeStruct((N_CHIPS, *s.shape), s.dtype) for s in shards], 6 * len(shards),
                     lambda: _sibling_peer() + _chip_peers(), GATHER_ID)


def _gathered(ex):
    p = 2 * lax.axis_index("x") + lax.axis_index("y")
    return [lax.dynamic_update_slice(g, s[None], (p, 0, 0)) for g, s in zip(ex.outs, ex.ins)]


def _rs_to_sibling(bufs):
    def make(srcs, dsts, send_sems, recv_sems):
        x, y, c = _mesh_pos()
        copies = []
        for i, (src, dst) in enumerate(zip(srcs, dsts)):
            half = src.shape[1] // 2
            copies.append(_remote(src.at[:, pl.ds((1 - c) * half, half), :], dst, send_sems, recv_sems, i, (x, y, 1 - c)))
        return copies

    return _copies_exchange(make, bufs, [jax.ShapeDtypeStruct((N_CHIPS, b.shape[1] // 2, b.shape[2]), F32) for b in bufs],
                            len(bufs), _sibling_peer, SIBLING_ID)


def _rs_pair_add(name, bufs, gots, c, out_dtype):
    n = len(bufs)

    def body(c_ref, *refs):
        for a_ref, b_ref, o_ref in zip(refs[:n], refs[n:2 * n], refs[2 * n:]):
            o_ref[...] = (a_ref[...] + b_ref[...]).astype(out_dtype)

    mine = [pl.BlockSpec((None, g.shape[1], g.shape[2]), lambda j, cr: (j, cr[0], 0)) for g in gots]
    whole = [pl.BlockSpec((None, g.shape[1], g.shape[2]), lambda j, cr: (j, 0, 0)) for g in gots]
    return pl.pallas_call(
        body, name=name,
        grid_spec=pltpu.PrefetchScalarGridSpec(num_scalar_prefetch=1, grid=(N_CHIPS,), in_specs=mine + whole, out_specs=whole),
        out_shape=[jax.ShapeDtypeStruct(g.shape, out_dtype) for g in gots],
        compiler_params=_cparams(("arbitrary",)))(c, *bufs, *gots)


def _rs_to_chips(accs):
    def make(srcs, dsts, send_sems, recv_sems):
        x, y, c = _mesh_pos()
        return [_remote(src.at[2 * qx + qy], dst.at[k], send_sems, recv_sems, 3 * i + k, (qx, qy, c))
                for i, (src, dst) in enumerate(zip(srcs, dsts)) for k, (qx, qy) in enumerate(_other_chips(x, y))]

    return _copies_exchange(make, accs, [jax.ShapeDtypeStruct((3, a.shape[1], a.shape[2]), a.dtype) for a in accs],
                            3 * len(accs), _chip_peers, CHIPS_ID)


def _rs_chip_add(name, accs, gots, p):
    n = len(accs)
    slot = (0, 1, 0, 2)

    def body(p_ref, *refs):
        me = p_ref[0]
        for own_ref, got_ref, o_ref in zip(refs[:n], refs[n:2 * n], refs[2 * n:]):
            total = None
            for chip in range(N_CHIPS):
                val = own_ref[...].astype(F32)
                for e in (1, 2, 3):
                    val = jnp.where((chip ^ me) == e, got_ref[slot[e]].astype(F32), val)
                total = val if total is None else total + val
            o_ref[...] = total

    own = [pl.BlockSpec((None, a.shape[1], a.shape[2]), lambda i, pr: (pr[0], 0, 0)) for a in accs]
    got = [pl.BlockSpec(g.shape, lambda i, pr: (0, 0, 0)) for g in gots]
    out = [pl.BlockSpec((a.shape[1], a.shape[2]), lambda i, pr: (0, 0)) for a in accs]
    return pl.pallas_call(
        body, name=name,
        grid_spec=pltpu.PrefetchScalarGridSpec(num_scalar_prefetch=1, grid=(1,), in_specs=own + got, out_specs=out),
        out_shape=[jax.ShapeDtypeStruct((a.shape[1], a.shape[2]), F32) for a in accs],
        compiler_params=_cparams(("arbitrary",)))(p, *accs, *gots)


def _rs_share(ress):
    def make(srcs, dsts, send_sems, recv_sems):
        x, y, c = _mesh_pos()
        return [_remote(src, dst, send_sems, recv_sems, i, (x, y, 1 - c)) for i, (src, dst) in enumerate(zip(srcs, dsts))]

    return _copies_exchange(make, ress, [jax.ShapeDtypeStruct(r.shape, F32) for r in ress], len(ress), _sibling_peer,
                            SIBLING_ID)


def _shared(ex):
    south = lax.axis_index("c") == 0
    return [jnp.concatenate([jnp.where(south, r, g), jnp.where(south, g, r)], axis=0) for r, g in zip(ex.ins, ex.outs)]


def _all_to_all_devices(vec):
    def others():
        x, y, c = _mesh_pos()
        return [((1 - x if r & 4 else x), (1 - y if r & 2 else y), (1 - c if r & 1 else c)) for r in range(1, 8)]

    def make(srcs, dsts, send_sems, recv_sems):
        x, y, c = _mesh_pos()
        me = 4 * x + 2 * y + c
        return [_remote(srcs[0], dsts[0].at[me], send_sems, recv_sems, r, peer) for r, peer in enumerate(others())]

    return _copies_exchange(make, [vec], [jax.ShapeDtypeStruct((8, *vec.shape), vec.dtype)], 7, others, ALL_ID)


def _sum_devices(stack):
    def body(s_ref, o_ref):
        total = s_ref[0]
        for d in range(1, 8):
            total = total + s_ref[d]
        o_ref[...] = total

    return pl.pallas_call(body, name="sum_devices", out_shape=jax.ShapeDtypeStruct(stack.shape[1:], F32),
                          compiler_params=pltpu.CompilerParams(vmem_limit_bytes=VMEM_LIMIT))(stack)


def _pad_rows(flat, rows):
    return jnp.concatenate([flat, jnp.zeros((rows * LANES - flat.shape[0],), flat.dtype)]).reshape(rows, LANES)


def _unshard(g4, shape, axis):
    a = g4.reshape(N_CHIPS, *shape)
    if axis == 0:
        return a.reshape(N_CHIPS * shape[0], shape[1])
    return jnp.transpose(a, (1, 0, 2)).reshape(shape[0], N_CHIPS * shape[1])


def _shard4(full, shape, axis):
    if axis == 0:
        return full.reshape(N_CHIPS, shape[0] * shape[1])
    a = full.reshape(shape[0], N_CHIPS, shape[1])
    return jnp.transpose(a, (1, 0, 2)).reshape(N_CHIPS, shape[0] * shape[1])


def _pad_axis0(a, rows):
    return jnp.concatenate([a, jnp.zeros((rows - a.shape[0], *a.shape[1:]), a.dtype)], axis=0)


def _pad_axis1(a, rows):
    return jnp.concatenate([a, jnp.zeros((a.shape[0], rows - a.shape[1], *a.shape[2:]), a.dtype)], axis=1)


def _shard_to_strip(name, w):
    _, (shape, axis, rows) = name, {n: (s, ax, r) for n, s, ax, r in BIG}[name]
    w2 = w.reshape(shape).astype(BF16)
    if name == "w_in":
        return w2
    return _pad_axis0(w2.T if axis == 1 else w2, rows)


LOCAL_NAME = dict(w_in="w_in_t", w_q_b="w_q_t", w_kv_b="w_kv_t", w_out="w_out", w_gate="w_gate_t", w_up="w_up_t",
                  w_down="w_down")


WIN_SEGMENTS = ((576, 2112, 0), (2112, 2624, 1536), (0, 256, 2048), (256, 512, 2304), (512, 576, 2560), (2624, 2632, 2688))


def _strips_to_weight(name, g4):
    if name == "w_in":
        return _win_to_pad(jnp.transpose(g4, (0, 2, 1)).reshape(IN_COLS, D_MODEL))
    if name == "w_q_b":
        return _qk_to_pad(g4.reshape(HEADS * QK_DIM, LORA))
    return g4.reshape(N_CHIPS * g4.shape[1], g4.shape[2])


def _grad_to_strips(name, g):
    if name == "w_in":
        strips = []
        for q in range(N_CHIPS):
            pieces = []
            for a, b, local in sorted(WIN_SEGMENTS):
                s, e = max(a, q * IN_SHARD), min(b, (q + 1) * IN_SHARD)
                if s < e:
                    pieces.append(g[local + s - a:local + e - a])
            pieces.append(jnp.zeros((IN_SHARD_P - IN_SHARD, D_MODEL), g.dtype))
            strips.append(jnp.concatenate(pieces, axis=0))
        return jnp.stack(strips)
    if name == "w_q_b":
        return _qk_from_pad(g).reshape(N_CHIPS, QK_DIM, LORA)
    return g.reshape(N_CHIPS, g.shape[0] // N_CHIPS, g.shape[1])


class _MeshPlan:
    LATE = dict(attn_norm_fwd=("w_in", "w_q_b", "w_kv_b"), attn_fwd=("w_up",), dn_chunk_fwd=("w_out", "w_gate"),
                gate_proj=("w_down/0",), up_proj=("w_down/1",))
    GROUP_A = ("w_down", "w_gate", "w_up", "w_out")
    GROUP_B = ("w_in", "w_q_b", "w_kv_b")

    def __init__(self, w):
        x, y, c = _mesh_pos()
        self.ci = jnp.reshape(c, (1,)).astype(jnp.int32)
        self.pi = jnp.reshape(2 * x + y, (1,)).astype(jnp.int32)
        self.strip = {n: _shard_to_strip(n, w[n]) for n, _, _, _ in BIG}
        self.gathers, self.weights, self.g, self.acc, self.reduced = {}, {}, {}, {}, {}
        self.sibs, self.sib, self.chip, self.share, self.halves = [], None, None, None, [None, None]

    def gather_small(self, small):
        ex = _all_gather([small])
        ex.run("all_gather_small")
        return _gathered(ex)[0]

    def weight(self, local_name):
        if local_name not in self.weights:
            for point, (names, ex) in list(self.gathers.items()):
                if ex.outs is not None:
                    for n, g4 in zip(names, _gathered(ex)):
                        if "/" in n:
                            n, half = n.split("/")
                            self.halves[int(half)] = g4
                            if None in self.halves:
                                continue
                            g4 = jnp.concatenate(self.halves, axis=1)
                        self.weights[LOCAL_NAME[n]] = _strips_to_weight(n, g4)
                    del self.gathers[point]
        return self.weights[local_name]

    def _shard(self, name):
        if "/" not in name:
            return self.strip[name]
        name, half = name.split("/")
        rows = self.strip[name].shape[0] // 2
        return self.strip[name][int(half) * rows:(int(half) + 1) * rows]

    def grad(self, local_name, value):
        name = {v: k for k, v in LOCAL_NAME.items()}[local_name]
        self.g[name] = _grad_to_strips(name, value)

    def _pair_add(self, names, gots):
        accs = _rs_pair_add("rs_pair_add_" + names[0], [self.g[n] for n in names], gots, self.ci, BF16)
        self.acc.update(zip(names, accs))

    def _chip_add(self, names, chip):
        return _rs_chip_add("rs_chip_add_" + names[0], [self.acc[n] for n in names], chip.outs, self.pi)

    def _take_shared(self, names, share):
        for n, strip in zip(names, _shared(share)):
            self.reduced[n] = strip

    def host(self, point):
        a, b = self.GROUP_A, self.GROUP_B
        if point in self.LATE:
            names = self.LATE[point]
            ex = _all_gather([self._shard(n) for n in names])
            self.gathers[point] = (names, ex)
            return ex
        if point in ("gate_dx", "up_dx", "mix_out_bwd"):
            names = dict(gate_dx=a[:2], up_dx=a[2:3], mix_out_bwd=a[3:])[point]
            ex = _rs_to_sibling([self.g[n] for n in names])
            self.sibs.append(ex)
            return ex
        if point == "dn_chunk_bwd":
            self._pair_add(a, [o for ex in self.sibs for o in ex.outs])
            self.chip1 = _rs_to_chips([self.acc[n] for n in a[:2]])
            return self.chip1
        if point == "attn_bwd":
            self.chip2 = _rs_to_chips([self.acc[n] for n in a[2:]])
            return self.chip2
        if point == "mla_prep_bwd":
            ress = self._chip_add(a[:2], self.chip1) + self._chip_add(a[2:], self.chip2)
            self.share = _rs_share(ress)
            return self.share
        if point == "in_dx":
            self._take_shared(a, self.share)
            self.sib = _rs_to_sibling([self.g[n] for n in b])
            return self.sib
        if point == "attn_norm_bwd":
            self._pair_add(b, self.sib.outs)
            self.chip = _rs_to_chips([self.acc[n] for n in b])
            return self.chip
        return None

    def last_share(self):
        self.share = _rs_share(self._chip_add(self.GROUP_B, self.chip))
        return self.share

    def finish(self):
        self._take_shared(self.GROUP_B, self.share)
        return self.reduced


def _strip_to_shard(name, strip):
    shape, axis = {n: (s, ax) for n, s, ax, _ in BIG}[name]
    rows = shape[axis]
    return strip[:rows].T if axis == 1 else strip[:rows]


def kernel(x, meta_tokens, attn_norm_w, w_in, q_a_norm_w, w_q_b, kv_a_norm_w, w_kv_b, q_norm_w, k_norm_w, mla_out_norm_w, dn_conv_w, dn_A_log, dn_dt_bias, dn_out_norm_w, w_out, ffn_norm_w, w_gate, w_up, ffn_conv_w, ffn_conv_b, w_down, loss_target, m_meta_tokens, m_attn_norm_w, m_w_in, m_q_a_norm_w, m_w_q_b, m_kv_a_norm_w, m_w_kv_b, m_q_norm_w, m_k_norm_w, m_mla_out_norm_w, m_dn_conv_w, m_dn_A_log, m_dn_dt_bias, m_dn_out_norm_w, m_w_out, m_ffn_norm_w, m_w_gate, m_w_up, m_ffn_conv_w, m_ffn_conv_b, m_w_down, v_meta_tokens, v_attn_norm_w, v_w_in, v_q_a_norm_w, v_w_q_b, v_kv_a_norm_w, v_w_kv_b, v_q_norm_w, v_k_norm_w, v_mla_out_norm_w, v_dn_conv_w, v_dn_A_log, v_dn_dt_bias, v_dn_out_norm_w, v_w_out, v_ffn_norm_w, v_w_gate, v_w_up, v_ffn_conv_w, v_ffn_conv_b, v_w_down):
    local = dict(locals())
    w = {n: local[n] for n in WEIGHTS}
    m = {n: local["m_" + n] for n in WEIGHTS}
    v = {n: local["v_" + n] for n in WEIGHTS}
    p = 2 * lax.axis_index("x") + lax.axis_index("y")

    plan = _MeshPlan(w)
    wf = _pad_rows(jnp.concatenate([w[n].reshape(-1) for n, _, _ in SMALL_SHARDED]), SMALL_ROWS)
    gf = plan.gather_small(wf).reshape(N_CHIPS, -1)
    full = {}
    off = 0
    for n, s, ax in SMALL_SHARDED:
        full[n] = _unshard(gf[:, off:off + s[0] * s[1]], s, ax)
        off += s[0] * s[1]
    for n, _ in REPLICATED:
        full[n] = w[n]
    full["ffn_conv_w"] = _ff_to_pad(full["ffn_conv_w"], 1)
    full["ffn_conv_b"] = _ff_to_pad(full["ffn_conv_b"], 1)

    sq, grad_x, g = _local_step(x[0], loss_target[0], full, plan)
    g["ffn_conv_w"] = _ff_from_pad(g["ffn_conv_w"], 1)
    g["ffn_conv_b"] = _ff_from_pad(g["ffn_conv_b"], 1)

    small_all = [n for n, _, _ in SMALL_SHARDED] + [n for n, _ in REPLICATED]
    vec = jnp.concatenate([g[n].reshape(-1) for n in small_all] + [jnp.reshape(0.5 / D_MODEL * jnp.sum(sq), (1,))])
    vec = _pad_rows(vec, -(-vec.shape[0] // (8 * LANES)) * 8)
    a2a = _all_to_all_devices(vec)

    gs, delta, new_m, new_v = {}, {}, {}, {}
    big = {n: (s, ax) for n, s, ax, _ in BIG}

    def adamw_big(n, strips, host=None):
        s, ax = big[n]
        flip = ax == 1 and s[1] % 8 == 0
        there = (lambda a: a.reshape(s).T) if flip else (lambda a: a.reshape(s))
        back = (lambda a: a.T.reshape(w[n].shape)) if flip else (lambda a: a.reshape(w[n].shape))
        strip = strips[n] if flip or ax == 0 else strips[n][:s[1]].T
        g2, d2, m2, v2 = _adamw_call("adamw_" + n, there(w[n]), strip, there(m[n]), there(v[n]), host=host)
        gs[n], delta[n], new_m[n], new_v[n] = back(g2), back(d2), back(m2), back(v2)

    adamw_big("w_down", plan.reduced, host=a2a)
    adamw_big("w_gate", plan.reduced, host=plan.last_share())
    adamw_big("w_up", plan.reduced)
    adamw_big("w_out", plan.reduced)
    strips = plan.finish()
    for n in plan.GROUP_B:
        adamw_big(n, strips)
    me = 4 * lax.axis_index("x") + 2 * lax.axis_index("y") + lax.axis_index("c")
    red = _sum_devices(lax.dynamic_update_slice(a2a.outs[0], vec[None], (me, 0, 0))).reshape(-1)
    off = 0
    for n in small_all:
        tot = red[off:off + g[n].size].reshape(g[n].shape)
        off += g[n].size
        shard = {sn: (s, ax) for sn, s, ax in SMALL_SHARDED}.get(n)
        if shard is not None:
            tot = lax.dynamic_slice_in_dim(tot, p * shard[0][1], shard[0][1], axis=1)
        gs[n] = tot
    loss = red[off]
    two_d = lambda a: a.reshape(a.shape[-2], a.shape[-1])
    outs = _adamw_small([two_d(w[n]) for n in small_all], [two_d(gs[n]) for n in small_all],
                        [two_d(m[n]) for n in small_all], [two_d(v[n]) for n in small_all])
    for i, n in enumerate(small_all):
        for dst, src in ((delta, outs[0]), (new_m, outs[1]), (new_v, outs[2])):
            dst[n] = src[i].reshape(w[n].shape)

    grad_out = [gs[n].reshape(w[n].shape) for n in WEIGHTS]
    return (loss, grad_x[None], *grad_out, *[delta[n] for n in WEIGHTS], *[new_m[n] for n in WEIGHTS],
            *[new_v[n] for n in WEIGHTS])
```

```python
import functools
import math

import jax
import jax.numpy as jnp
import numpy as np
from jax import lax
from jax.experimental import pallas as pl
from jax.experimental.pallas import tpu as pltpu

F32 = jnp.float32
BF16 = jnp.bfloat16
HI = lax.Precision.HIGHEST
MESH = pl.DeviceIdType.MESH

N_META = 16
D_MODEL = 1024
HEADS = 4
HEAD = 128
ROPE = 64
QK_DIM = HEAD + ROPE
QK_PAD = 2 * HEAD
LORA = 256
DN_WIDTH = HEADS * HEAD
CHUNK = 64
D_FF = 2816
N_CHIPS = 4
FF_SHARD = D_FF // N_CHIPS
FF_BLOCK = 768
D_FF_P = N_CHIPS * FF_BLOCK
IN_COLS = 2632
IN_SHARD = IN_COLS // N_CHIPS
IN_SHARD_P = 672
IN_PAD = 2816
NORM_EPS = 1e-6
ROPE_THETA = 10000.0
LANES = 512

ADAM_LR, ADAM_B1, ADAM_B2, ADAM_EPS, ADAM_WD, ADAM_STEP = 0.001, 0.9, 0.999, 1e-08, 0.01, 10

VMEM_LIMIT = 56 * 1024 * 1024

BIG = (("w_in", (1024, 658), 1, IN_SHARD_P), ("w_q_b", (256, 192), 1, 192), ("w_kv_b", (256, 256), 1, 256),
       ("w_out", (256, 1024), 0, 256), ("w_gate", (1024, 704), 1, FF_BLOCK), ("w_up", (1024, 704), 1, FF_BLOCK),
       ("w_down", (704, 1024), 0, FF_BLOCK))
SMALL_SHARDED = (("meta_tokens", (16, 256), 1), ("dn_conv_w", (4, 384), 1), ("ffn_conv_w", (3, 704), 1))
REPLICATED = (("attn_norm_w", 1024), ("q_a_norm_w", 256), ("kv_a_norm_w", 256), ("q_norm_w", 192), ("k_norm_w", 192),
              ("mla_out_norm_w", 128), ("dn_A_log", 4), ("dn_dt_bias", 4), ("dn_out_norm_w", 128), ("ffn_norm_w", 1024),
              ("ffn_conv_b", 2816))
WEIGHTS = ("meta_tokens", "attn_norm_w", "w_in", "q_a_norm_w", "w_q_b", "kv_a_norm_w", "w_kv_b", "q_norm_w", "k_norm_w",
           "mla_out_norm_w", "dn_conv_w", "dn_A_log", "dn_dt_bias", "dn_out_norm_w", "w_out", "ffn_norm_w", "w_gate",
           "w_up", "ffn_conv_w", "ffn_conv_b", "w_down")

SMALL_ROWS = 16
REP_ROWS = 16


def _cparams(sem):
    return pltpu.CompilerParams(dimension_semantics=sem, vmem_limit_bytes=VMEM_LIMIT)


class _Exchange:
    def __init__(self, prog, ins, out_shape, nsem, peers=None, cid=None):
        self.prog, self.ins, self.out_shape, self.nsem = prog, list(ins), list(out_shape), nsem
        self.peers, self.cid = peers, cid
        self.outs = None

    def sems(self):
        return [pltpu.SemaphoreType.DMA((self.nsem,)), pltpu.SemaphoreType.DMA((self.nsem,))]

    def programs(self, in_refs, out_refs, send_sems, recv_sems):
        start, finish = self.prog(in_refs, out_refs, send_sems, recv_sems)
        if self.cid is None:
            return start, finish
        peers = self.peers()

        def shake_and_start():
            barrier = pltpu.get_barrier_semaphore()
            for peer in peers:
                pl.semaphore_signal(barrier, inc=1, device_id=peer, device_id_type=MESH)
            pl.semaphore_wait(barrier, len(peers))
            start()

        return shake_and_start, finish

    def cparams(self, **kw):
        return pltpu.CompilerParams(has_side_effects=True, collective_id=self.cid, **kw)

    def run(self, name):
        any_spec = pl.BlockSpec(memory_space=pl.ANY)
        n = len(self.ins)

        def body(*refs):
            start, finish = self.programs(refs[:n], refs[n:-2], refs[-2], refs[-1])
            start()
            finish()

        self.outs = pl.pallas_call(
            body, name=name, in_specs=[any_spec] * n, out_specs=[any_spec] * len(self.out_shape),
            out_shape=self.out_shape, scratch_shapes=self.sems(), compiler_params=self.cparams())(*self.ins)
        return self.outs


def _pcall(body, name, grid, in_specs, out_specs, out_shape, args, sem, scratch_shapes=(), host=None):
    single = not isinstance(out_shape, (list, tuple))
    out_specs, out_shape = ([out_specs], [out_shape]) if single else (list(out_specs), list(out_shape))
    if host is None:
        outs = pl.pallas_call(body, name=name, grid=grid, in_specs=list(in_specs), out_specs=out_specs, out_shape=out_shape,
                              scratch_shapes=list(scratch_shapes), compiler_params=_cparams(sem))(*args)
        return outs[0] if single else outs
    any_spec = pl.BlockSpec(memory_space=pl.ANY)
    n_in, n_out, n_scr, nx_in, nx_out = len(in_specs), len(out_specs), len(scratch_shapes), len(host.ins), len(host.out_shape)

    def hosted(*refs):
        c_in, x_in = refs[:n_in], refs[n_in:n_in + nx_in]
        o0 = n_in + nx_in
        c_out, x_out = refs[o0:o0 + n_out], refs[o0 + n_out:o0 + n_out + nx_out]
        s0 = o0 + n_out + nx_out
        start, finish = host.programs(x_in, x_out, refs[s0 + n_scr], refs[s0 + n_scr + 1])
        first = functools.reduce(jnp.logical_and, [pl.program_id(d) == 0 for d in range(len(grid))])
        last = functools.reduce(jnp.logical_and, [pl.program_id(d) == grid[d] - 1 for d in range(len(grid))])
        pl.when(first)(start)
        body(*c_in, *c_out, *refs[s0:s0 + n_scr])
        pl.when(last)(finish)

    outs = pl.pallas_call(
        hosted, name=name, grid=grid, in_specs=list(in_specs) + [any_spec] * nx_in,
        out_specs=out_specs + [any_spec] * nx_out, out_shape=out_shape + host.out_shape,
        scratch_shapes=list(scratch_shapes) + host.sems(),
        compiler_params=host.cparams(dimension_semantics=sem, vmem_limit_bytes=VMEM_LIMIT))(*args, *host.ins)
    host.outs = outs[n_out:]
    return outs[0] if single else outs[:n_out]


NN, NT, TN = ((1,), (0,)), ((1,), (1,)), ((0,), (0,))


def _shift_dims(dims, batch):
    if not batch:
        return (dims, ((), ()))
    return (((dims[0][0] + 1,), (dims[1][0] + 1,)), ((0,), (0,)))


def _make_mm(dims, exact, batch=False):
    def raw(a, b, d):
        dn = _shift_dims(d, batch)
        if exact == "split_lhs":
            ah, bh = a.astype(BF16), b.astype(BF16)
            al = (a - ah.astype(F32)).astype(BF16)
            return lax.dot_general(ah, bh, dn, preferred_element_type=F32) + lax.dot_general(al, bh, dn,
                                                                                              preferred_element_type=F32)
        if exact == "split":
            ah, bh = a.astype(BF16), b.astype(BF16)
            al, bl = (a - ah.astype(F32)).astype(BF16), (b - bh.astype(F32)).astype(BF16)
            dot = lambda p, q: lax.dot_general(p, q, dn, preferred_element_type=F32)
            return dot(ah, bh) + (dot(ah, bl) + dot(al, bh))
        if exact:
            return lax.dot_general(a.astype(F32), b.astype(F32), dn, precision=HI, preferred_element_type=F32)
        return lax.dot_general(a.astype(BF16), b.astype(BF16), dn, preferred_element_type=F32)

    @jax.custom_vjp
    def mm(a, b):
        return raw(a, b, dims)

    def fwd(a, b):
        return raw(a, b, dims), (a, b)

    def bwd(res, g):
        a, b = res
        if dims == NN:
            da, db = raw(g, b, NT), raw(a, g, TN)
        elif dims == NT:
            da, db = raw(g, b, NN), raw(g, a, TN)
        else:
            da, db = raw(b, g, NT), raw(a, g, NN)
        return da.astype(a.dtype), db.astype(b.dtype)

    mm.defvjp(fwd, bwd)
    return mm


_mm = _make_mm(NN, False)
_mm_nt = _make_mm(NT, False)
_mm_tn = _make_mm(TN, False)
_mmx = _make_mm(NN, "split_lhs")
_bmm = _make_mm(NN, False, batch=True)
_bmm_nt = _make_mm(NT, False, batch=True)
_bmm_tn = _make_mm(TN, False, batch=True)
_bmmx = _make_mm(NN, True, batch=True)
_bmms = _make_mm(NN, "split", batch=True)
_bmms_nt = _make_mm(NT, "split", batch=True)
_bmms_tn = _make_mm(TN, "split", batch=True)


@jax.custom_vjp
def _unit_lower_inv(a):
    n = a.shape[-1]
    eye = (lax.broadcasted_iota(jnp.int32, a.shape, 1) == lax.broadcasted_iota(jnp.int32, a.shape, 2)).astype(F32)
    x = -a
    t = eye + x
    for _ in range(max(n.bit_length() - 2, 0)):
        x = _bmms(x, x)
        t = t + _bmms(t, x)
    return t


def _unit_lower_inv_fwd(a):
    t = _unit_lower_inv(a)
    return t, t


def _unit_lower_inv_bwd(t, g):
    return (-_bmms_tn(t, _bmms_nt(g, t)),)


_unit_lower_inv.defvjp(_unit_lower_inv_fwd, _unit_lower_inv_bwd)


def _scan_chunk_rows(x, reverse):
    nb, c, w = x.shape
    y = x.reshape(nb * c, w)
    pos = lax.broadcasted_iota(jnp.int32, y.shape, 0) % c
    step = 1
    while step < c:
        if reverse:
            y = y + jnp.where(pos < c - step, pltpu.roll(y, nb * c - step, 0), 0.0)
        else:
            y = y + jnp.where(pos >= step, pltpu.roll(y, step, 0), 0.0)
        step *= 2
    return y.reshape(nb, c, w)


@jax.custom_vjp
def _chunk_cumsum(x):
    return _scan_chunk_rows(x, False)


_chunk_cumsum.defvjp(lambda x: (_scan_chunk_rows(x, False), None), lambda _, g: (_scan_chunk_rows(g, True),))


def _rms(x, w, n):
    ms = jnp.sum(x * x, axis=-1, keepdims=True) * (1.0 / n)
    return x * lax.rsqrt(ms + NORM_EPS) * w


def _silu(x):
    return x * jax.nn.sigmoid(x)


def _softplus(x):
    return jnp.maximum(x, 0.0) + jnp.log(1.0 + jnp.exp(-jnp.abs(x)))


def _rope(x, cos, sin, perm):
    return x * cos + _mmx(x, perm) * sin


def _mla_prep_fn(rows, consts):
    q_lat, kv_lat, k_pe, cos, sin = rows
    qn = _rms(q_lat, consts["qa_w"], LORA)
    kvn = _rms(kv_lat, consts["kva_w"], LORA)
    outs = []
    for h in range(HEADS):
        q_n = _mm_nt(qn, consts["wq_n"][h])
        q_r = _mm_nt(qn, consts["wq_r"][h])
        rs = lax.rsqrt((jnp.sum(q_n * q_n, -1, keepdims=True) + jnp.sum(q_r * q_r, -1, keepdims=True)) * (1.0 / QK_DIM)
                       + NORM_EPS)
        q_n = q_n * rs * consts["qn_n"]
        q_r = _rope(q_r * rs * consts["qn_r"], cos, sin, consts["perm"])
        k_n = _mm_nt(kvn, consts["wk_n"][h])
        v = _mm_nt(kvn, consts["wv"][h])
        rk = lax.rsqrt((jnp.sum(k_n * k_n, -1, keepdims=True) + jnp.sum(k_pe * k_pe, -1, keepdims=True)) * (1.0 / QK_DIM)
                       + NORM_EPS)
        k_n = k_n * rk * consts["kn_n"]
        k_r = _rope(k_pe * rk * consts["kn_r"], cos, sin, consts["perm"])
        outs += [q_n, q_r, k_n, k_r, v]
    return tuple(outs)


def _attn_fn(q, k, v, row0):
    s = _mm_nt(q, k) * (1.0 / math.sqrt(QK_DIM))
    qpos = row0 + lax.broadcasted_iota(jnp.int32, s.shape, 0)
    kpos = lax.broadcasted_iota(jnp.int32, s.shape, 1)
    s = jnp.where(kpos <= qpos, s, -1e30)
    m = lax.stop_gradient(jnp.max(s, axis=-1, keepdims=True))
    p = jnp.exp(s - m)
    p = p / jnp.sum(p, axis=-1, keepdims=True)
    return _mm(p, v)


def _dn_prep_fn(rows, consts):
    qc, kc, ab = rows
    a_b = _mmx(ab, consts["sel_a"])
    b_b = _mmx(ab, consts["sel_b"])
    beta = jax.nn.sigmoid(b_b)
    g = -jnp.exp(consts["alog"]) * _softplus(a_b + consts["dtb"])
    qs, ks = [], []
    for h in range(HEADS):
        q, k = qc[h], kc[h]
        qs.append(q * lax.rsqrt(jnp.sum(q * q, -1, keepdims=True) + NORM_EPS))
        ks.append(k * lax.rsqrt(jnp.sum(k * k, -1, keepdims=True) + NORM_EPS))
    return tuple(qs), tuple(ks), g, beta


def _dn_chunk_fn(q, k, v, gb, g64, bb):
    nb = q.shape[0]
    ri = lax.broadcasted_iota(jnp.int32, (nb, CHUNK, CHUNK), 1)
    ci = lax.broadcasted_iota(jnp.int32, (nb, CHUNK, CHUNK), 2)
    tri = ri >= ci
    strict = ri > ci
    tril = tri.astype(F32)
    eye = (ri == ci).astype(F32)
    ones = jnp.ones((nb, CHUNK, CHUNK), F32)
    gc = _chunk_cumsum(gb)
    gc64 = _chunk_cumsum(g64)
    grow = _bmmx(ones, eye * gc64)
    diff = gc64 - grow
    decay = jnp.where(tri, jnp.exp(jnp.where(tri, diff, 0.0)), 0.0)
    kb = k * bb
    vb = v * bb
    a = jnp.where(strict, _bmm_nt(kb, k) * decay, 0.0)
    tinv = _unit_lower_inv(a)
    u = _bmm(tinv, vb)
    w = _bmm(tinv, kb * jnp.exp(gc))
    qs = q * (1.0 / math.sqrt(HEAD))
    qk = _bmm_nt(qs, k) * decay
    qg = qs * jnp.exp(gc)
    glast = jnp.sum(gb, axis=1, keepdims=True)
    kdec = k * jnp.exp(glast - gc)
    n_mat = _bmm_tn(kdec, w)
    b_mat = _bmm_tn(kdec, u)
    q_eff = qg - _bmm(qk, w)
    o_own = _bmm(qk, u)
    return n_mat, b_mat, q_eff, o_own, jnp.exp(glast)


def _dn_rec_fn(s, n_mat, b_mat, eg):
    return s * eg - _mm(n_mat, s) + b_mat


def _dn_o_fn(s, q_eff, o_own):
    return _bmm(q_eff, s) + o_own


def _dn_out_fn(o, z, w):
    return _rms(o, w, HEAD) * _silu(z)


def _row_tile(t, parts=8):
    return t // parts if (t // parts) % 16 == 0 else t


def _tile(n, pref, unit):
    best = n
    for cand in range(unit, min(n, pref) + 1, unit):
        if n % cand == 0:
            best = cand
    return best if best <= pref else n


def _rows_call(name, body, rows, consts, outs, accs, r, host=None):
    rows = [a if isinstance(a, tuple) else (a, a.shape[1], 0) for a in rows]
    t = rows[0][0].shape[0]
    zero = lambda nd: (lambda i: (0,) * nd)
    in_specs = [pl.BlockSpec((r, w), functools.partial(lambda i, b: (i, b), b=blk)) for _, w, blk in rows]
    rows = [a for a, _, _ in rows]
    in_specs += [pl.BlockSpec(a.shape, zero(a.ndim)) for a in consts]
    out_shape = [jax.ShapeDtypeStruct((t, w), dt) for w, dt in outs] + [jax.ShapeDtypeStruct(s, F32) for s in accs]
    out_specs = [pl.BlockSpec((r, w), lambda i: (i, 0)) for w, _ in outs] + [pl.BlockSpec(s, zero(len(s))) for s in accs]
    return _pcall(body, name, (t // r,), in_specs, out_specs, out_shape, [*rows, *consts], ("arbitrary",), host=host)


def _accumulate(ref, val):
    @pl.when(pl.program_id(0) == 0)
    def _():
        ref[...] = jnp.zeros(ref.shape, ref.dtype)

    ref[...] += val


def _matmul(name, a, b, dims, out_dtype, res=None, host=None):
    if dims == "nn":
        (m, k), n = a.shape, b.shape[1]
    elif dims == "nt":
        (m, k), n = a.shape, b.shape[0]
    else:
        (k, m), n = a.shape, b.shape[1]
    tm = _tile(m, 1100, 16) if dims != "tn" else _tile(m, 640, 128)
    tn = _tile(n, 1408, 128)
    if dims == "nn":
        a_spec, b_spec, dn = pl.BlockSpec((tm, k), lambda i, j: (i, 0)), pl.BlockSpec((k, tn), lambda i, j: (0, j)), NN
    elif dims == "nt":
        a_spec, b_spec, dn = pl.BlockSpec((tm, k), lambda i, j: (i, 0)), pl.BlockSpec((tn, k), lambda i, j: (j, 0)), NT
    else:
        a_spec, b_spec, dn = pl.BlockSpec((k, tm), lambda i, j: (0, i)), pl.BlockSpec((k, tn), lambda i, j: (0, j)), TN
    o_spec = pl.BlockSpec((tm, tn), lambda i, j: (i, j))

    def body(*refs):
        a_ref, b_ref, o_ref = refs[0], refs[1], refs[-1]
        acc = lax.dot_general(a_ref[...].astype(BF16), b_ref[...].astype(BF16), (dn, ((), ())),
                              preferred_element_type=F32)
        if res is not None:
            acc = acc + refs[2][...]
        o_ref[...] = acc.astype(out_dtype)

    ins = [a, b] + ([res] if res is not None else [])
    specs = [a_spec, b_spec] + ([o_spec] if res is not None else [])
    return _pcall(body, name, (m // tm, n // tn), specs, o_spec, jax.ShapeDtypeStruct((m, n), out_dtype), ins,
                  ("arbitrary", "arbitrary"), host=host)


def _rms_fwd(name, h, w, host=None):
    n = h.shape[1]

    def body(h_ref, w_ref, o_ref):
        o_ref[...] = _rms(h_ref[...], w_ref[...], n).astype(BF16)

    return _rows_call(name, body, [h], [w], [(n, BF16)], [], _row_tile(h.shape[0]), host=host)[0]


def _rms_bwd(name, h, w, cts, resid, host=None):
    n = h.shape[1]
    nct = len(cts)

    def body(*refs):
        h_ref, ct_refs, r_ref, w_ref = refs[0], refs[1:1 + nct], refs[1 + nct], refs[2 + nct]
        dh_ref, dh16_ref, dw_ref = refs[-3], refs[-2], refs[-1]
        ct = ct_refs[0][...].astype(F32)
        for c in ct_refs[1:]:
            ct = ct + c[...].astype(F32)
        _, vjp = jax.vjp(lambda x, ww: _rms(x, ww, n), h_ref[...], w_ref[...])
        dh, dw = vjp(ct)
        dh = dh + r_ref[...]
        dh_ref[...] = dh
        dh16_ref[...] = dh.astype(BF16)
        _accumulate(dw_ref, dw)

    return _rows_call(name, body, [h, *cts, resid], [w], [(n, F32), (n, BF16)], [(1, n)], _row_tile(h.shape[0]), host=host)


def _mla_consts_from_refs(qa, wq, kva, wkv, qn, kn, perm):
    f = lambda r: r[...].astype(F32)
    return dict(
        qa_w=f(qa), kva_w=f(kva), perm=f(perm),
        wq_n=[wq[h * QK_PAD:h * QK_PAD + HEAD, :].astype(F32) for h in range(HEADS)],
        wq_r=[wq[h * QK_PAD + HEAD:(h + 1) * QK_PAD, :].astype(F32) for h in range(HEADS)],
        wk_n=[wkv[h * QK_PAD:h * QK_PAD + HEAD, :].astype(F32) for h in range(HEADS)],
        wv=[wkv[h * QK_PAD + HEAD:(h + 1) * QK_PAD, :].astype(F32) for h in range(HEADS)],
        qn_n=qn[:, 0:HEAD], qn_r=qn[:, HEAD:QK_PAD], kn_n=kn[:, 0:HEAD], kn_r=kn[:, HEAD:QK_PAD])


def _mla_prep_fwd(q_lat, kv_lat, k_pe, cos, sin, qa, wq, kva, wkv, qn, kn, perm):
    def body(ql, kvl, kp, c, s, qa_r, wq_r, kva_r, wkv_r, qn_r, kn_r, p_r, q_out, k_out, v_out):
        consts = _mla_consts_from_refs(qa_r, wq_r, kva_r, wkv_r, qn_r, kn_r, p_r)
        outs = _mla_prep_fn((ql[...], kvl[...], kp[...], c[...], s[...]), consts)
        for h in range(HEADS):
            q_n, q_r, k_n, k_r, v = outs[5 * h:5 * h + 5]
            q_out[:, h * QK_PAD:h * QK_PAD + HEAD] = q_n.astype(BF16)
            q_out[:, h * QK_PAD + HEAD:(h + 1) * QK_PAD] = q_r.astype(BF16)
            k_out[:, h * QK_PAD:h * QK_PAD + HEAD] = k_n.astype(BF16)
            k_out[:, h * QK_PAD + HEAD:(h + 1) * QK_PAD] = k_r.astype(BF16)
            v_out[:, h * HEAD:(h + 1) * HEAD] = v.astype(BF16)

    return _rows_call("mla_prep_fwd", body, [q_lat, kv_lat, k_pe, cos, sin], [qa, wq, kva, wkv, qn, kn, perm],
                      [(HEADS * QK_PAD, BF16), (HEADS * QK_PAD, BF16), (DN_WIDTH, BF16)], [], _row_tile(cos.shape[0], 4))


def _mla_prep_bwd(q_lat, kv_lat, k_pe, cos, sin, dq, dk, dv, qa, wq, kva, wkv, qn, kn, perm, host=None):
    def body(ql, kvl, kp, c, s, dq_r, dk_r, dv_r, qa_r, wq_r, kva_r, wkv_r, qn_r, kn_r, p_r,
             dql, dkvl, dkp, dqa, dwq, dkva, dwkv, dqn, dkn):
        consts = _mla_consts_from_refs(qa_r, wq_r, kva_r, wkv_r, qn_r, kn_r, p_r)
        cc, ss, pm = c[...], s[...], consts.pop("perm")
        _, vjp = jax.vjp(lambda rows, cs: _mla_prep_fn((*rows, cc, ss), dict(cs, perm=pm)), (ql[...], kvl[...], kp[...]),
                         consts)
        cts = []
        for h in range(HEADS):
            cts += [dq_r[:, h * QK_PAD:h * QK_PAD + HEAD], dq_r[:, h * QK_PAD + HEAD:(h + 1) * QK_PAD],
                    dk_r[:, h * QK_PAD:h * QK_PAD + HEAD], dk_r[:, h * QK_PAD + HEAD:(h + 1) * QK_PAD],
                    dv_r[:, h * HEAD:(h + 1) * HEAD]]
        (d_ql, d_kvl, d_kp), dc = vjp(tuple(cts))
        dql[...] = d_ql.astype(BF16)
        dkvl[...] = d_kvl.astype(BF16)
        dkp[...] = d_kp.astype(BF16)
        first = pl.program_id(0) == 0

        def acc(ref, sl, val):
            @pl.when(first)
            def _():
                ref[sl] = val

            @pl.when(jnp.logical_not(first))
            def _():
                ref[sl] += val

        full = (slice(None), slice(None))
        acc(dqa, full, dc["qa_w"])
        acc(dkva, full, dc["kva_w"])
        for h in range(HEADS):
            acc(dwq, (slice(h * QK_PAD, h * QK_PAD + HEAD), slice(None)), dc["wq_n"][h])
            acc(dwq, (slice(h * QK_PAD + HEAD, (h + 1) * QK_PAD), slice(None)), dc["wq_r"][h])
            acc(dwkv, (slice(h * QK_PAD, h * QK_PAD + HEAD), slice(None)), dc["wk_n"][h])
            acc(dwkv, (slice(h * QK_PAD + HEAD, (h + 1) * QK_PAD), slice(None)), dc["wv"][h])
        acc(dqn, (slice(None), slice(0, HEAD)), dc["qn_n"])
        acc(dqn, (slice(None), slice(HEAD, QK_PAD)), dc["qn_r"])
        acc(dkn, (slice(None), slice(0, HEAD)), dc["kn_n"])
        acc(dkn, (slice(None), slice(HEAD, QK_PAD)), dc["kn_r"])

    return _rows_call("mla_prep_bwd", body, [q_lat, kv_lat, k_pe, cos, sin, dq, dk, dv],
                      [qa, wq, kva, wkv, qn, kn, perm],
                      [(LORA, BF16), (LORA, BF16), (HEAD, BF16)],
                      [(1, LORA), wq.shape, (1, LORA), wkv.shape, (1, QK_PAD), (1, QK_PAD)], _row_tile(cos.shape[0], 4),
                      host=host)


ATTN_Q_ROWS = 256


def _attn_blocks(t):
    return [(r0, min(ATTN_Q_ROWS, t - r0)) for r0 in range(0, t, ATTN_Q_ROWS)]


def _attn_fwd(q, k, v, host=None):
    t = q.shape[0]

    def body(q_ref, k_ref, v_ref, o_ref):
        for r0, rows in _attn_blocks(t):
            ext = r0 + rows
            o_ref[r0:ext, :] = _attn_fn(q_ref[r0:ext, :], k_ref[0:ext, :], v_ref[0:ext, :], r0)

    qk_spec = pl.BlockSpec((t, QK_PAD), lambda h: (0, h))
    v_spec = pl.BlockSpec((t, HEAD), lambda h: (0, h))
    return _pcall(body, "attn_fwd", (HEADS,), [qk_spec, qk_spec, v_spec], v_spec,
                  jax.ShapeDtypeStruct((t, HEADS * HEAD), F32), [q, k, v], ("arbitrary",), host=host)


def _attn_bwd(q, k, v, do, host=None):
    t = q.shape[0]

    def body(q_ref, k_ref, v_ref, do_ref, dq_ref, dk_ref, dv_ref):
        dk_ref[...] = jnp.zeros(dk_ref.shape, F32)
        dv_ref[...] = jnp.zeros(dv_ref.shape, F32)
        for r0, rows in _attn_blocks(t):
            ext = r0 + rows
            _, vjp = jax.vjp(functools.partial(_attn_fn, row0=r0), q_ref[r0:ext, :].astype(F32),
                             k_ref[0:ext, :].astype(F32), v_ref[0:ext, :].astype(F32))
            dq, dk, dv = vjp(do_ref[r0:ext, :])
            dq_ref[r0:ext, :] = dq
            dk_ref[0:ext, :] += dk
            dv_ref[0:ext, :] += dv

    qk_spec = pl.BlockSpec((t, QK_PAD), lambda h: (0, h))
    v_spec = pl.BlockSpec((t, HEAD), lambda h: (0, h))
    return _pcall(body, "attn_bwd", (HEADS,), [qk_spec, qk_spec, v_spec, v_spec], [qk_spec, qk_spec, v_spec],
                  [jax.ShapeDtypeStruct((t, HEADS * QK_PAD), F32), jax.ShapeDtypeStruct((t, HEADS * QK_PAD), F32),
                   jax.ShapeDtypeStruct((t, HEADS * HEAD), F32)], [q, k, v, do], ("arbitrary",), host=host)


def _mix_out_fwd(o_mla, o_dn, z, w_mla, w_dn):
    def body(om_ref, od_ref, z_ref, wm_ref, wd_ref, o_ref):
        for h in range(HEADS):
            sl = slice(h * HEAD, (h + 1) * HEAD)
            o_ref[:, sl] = _rms(om_ref[:, sl], wm_ref[...], HEAD).astype(BF16)
            o_ref[:, DN_WIDTH + h * HEAD:DN_WIDTH + (h + 1) * HEAD] = _dn_out_fn(od_ref[:, sl], z_ref[:, sl],
                                                                                 wd_ref[...]).astype(BF16)

    return _rows_call("mix_out_fwd", body, [o_mla, o_dn, z], [w_mla, w_dn], [(2 * DN_WIDTH, BF16)], [],
                      _row_tile(o_mla.shape[0]))[0]


def _mix_out_bwd(o_mla, o_dn, z, dmixed, w_mla, w_dn, host=None):
    def body(om_ref, od_ref, z_ref, dm_ref, wm_ref, wd_ref, dom_ref, dod_ref, dz_ref, dwm_ref, dwd_ref):
        dwm = dwd = None
        for h in range(HEADS):
            sl = slice(h * HEAD, (h + 1) * HEAD)
            _, vjp = jax.vjp(lambda o, w: _rms(o, w, HEAD), om_ref[:, sl], wm_ref[...])
            do, dw = vjp(dm_ref[:, sl].astype(F32))
            dom_ref[:, sl] = do
            dwm = dw if dwm is None else dwm + dw
            _, vjp = jax.vjp(_dn_out_fn, od_ref[:, sl], z_ref[:, sl], wd_ref[...])
            do, dz, dw = vjp(dm_ref[:, DN_WIDTH + h * HEAD:DN_WIDTH + (h + 1) * HEAD].astype(F32))
            dod_ref[:, sl] = do
            dz_ref[:, sl] = dz.astype(BF16)
            dwd = dw if dwd is None else dwd + dw
        _accumulate(dwm_ref, dwm)
        _accumulate(dwd_ref, dwd)

    return _rows_call("mix_out_bwd", body, [o_mla, o_dn, z, dmixed], [w_mla, w_dn],
                      [(DN_WIDTH, F32), (DN_WIDTH, F32), (DN_WIDTH, BF16)], [(1, HEAD), (1, HEAD)],
                      _row_tile(o_mla.shape[0]), host=host)


def _shift_down(x, s):
    if s == 0:
        return x
    rows = lax.broadcasted_iota(jnp.int32, x.shape, 0)
    return jnp.where(rows >= s, pltpu.roll(x, s, 0), 0.0)


def _shift_up(x, s):
    if s == 0:
        return x
    t = x.shape[0]
    rows = lax.broadcasted_iota(jnp.int32, x.shape, 0)
    return jnp.where(rows < t - s, pltpu.roll(x, t - s, 0), 0.0)


def _col_call(name, body, cols, taps, outs, tap_outs, cw, host=None):
    t, c = cols[0].shape[0], taps[0].shape[1]
    in_specs = [pl.BlockSpec((t, cw), lambda j: (0, j)) for _ in cols]
    in_specs += [pl.BlockSpec((a.shape[0], cw), lambda j: (0, j)) for a in taps]
    out_shape = [jax.ShapeDtypeStruct((t, c), dt) for dt in outs] + [jax.ShapeDtypeStruct((n, c), F32) for n in tap_outs]
    out_specs = [pl.BlockSpec((t, cw), lambda j: (0, j)) for _ in outs]
    out_specs += [pl.BlockSpec((n, cw), lambda j: (0, j)) for n in tap_outs]
    return _pcall(body, name, (c // cw,), in_specs, out_specs, out_shape, [*cols, *taps], ("arbitrary",), host=host)


def _causal_conv(x, w_ref, width):
    acc = w_ref[width - 1:width, :] * x
    for j in range(width - 1):
        acc = acc + w_ref[j:j + 1, :] * _shift_down(x, width - 1 - j)
    return acc


def _causal_conv_bwd(x, dpre, w_ref, dx_ref, dw_ref, width):
    dx = w_ref[width - 1:width, :] * dpre
    dw_ref[width - 1:width, :] = jnp.sum(dpre * x, axis=0, keepdims=True)
    for j in range(width - 1):
        s = width - 1 - j
        dx = dx + w_ref[j:j + 1, :] * _shift_up(dpre, s)
        dw_ref[j:j + 1, :] = jnp.sum(dpre * _shift_down(x, s), axis=0, keepdims=True)
    dx_ref[...] = dx.astype(dx_ref.dtype)


def _dsilu(x):
    sg = jax.nn.sigmoid(x)
    return sg * (1.0 + x * (1.0 - sg))


def _dn_conv_fwd(x, w):
    def body(x_ref, w_ref, y_ref):
        y_ref[...] = _silu(_causal_conv(x_ref[...], w_ref, 4))

    return _col_call("dn_conv_fwd", body, [x], [w], [F32], [], 256)[0]


def _dn_conv_bwd(x, w, dy):
    def body(x_ref, dy_ref, w_ref, dx_ref, dw_ref):
        xv = x_ref[...]
        dpre = dy_ref[...] * _dsilu(_causal_conv(xv, w_ref, 4))
        _causal_conv_bwd(xv, dpre, w_ref, dx_ref, dw_ref, 4)

    return _col_call("dn_conv_bwd", body, [x, dy], [w], [BF16], [4], 256)


def _glu_fwd(gpre, up, w, b, host=None):
    def body(g_ref, u_ref, w_ref, b_ref, a_ref):
        gate = _causal_conv(g_ref[...].astype(F32), w_ref, 3) + b_ref[...]
        a_ref[...] = (_silu(gate) * u_ref[...].astype(F32)).astype(BF16)

    return _col_call("glu_fwd", body, [gpre, up], [w, b], [BF16], [], 256, host=host)[0]


def _glu_bwd(gpre, up, w, b, dact):
    def body(g_ref, u_ref, da_ref, w_ref, b_ref, dg_ref, du_ref, dw_ref, db_ref):
        gv = g_ref[...].astype(F32)
        gate = _causal_conv(gv, w_ref, 3) + b_ref[...]
        da = da_ref[...].astype(F32)
        sg = jax.nn.sigmoid(gate)
        du_ref[...] = (da * (gate * sg)).astype(BF16)
        dgate = da * u_ref[...].astype(F32) * (sg * (1.0 + gate * (1.0 - sg)))
        db_ref[...] = jnp.sum(dgate, axis=0, keepdims=True)
        _causal_conv_bwd(gv, dgate, w_ref, dg_ref, dw_ref, 3)

    return _col_call("glu_bwd", body, [gpre, up, dact], [w, b], [BF16, BF16], [3, 1], 256)


def _dn_prep_consts(sa, sb, al, dt):
    return dict(sel_a=sa[...], sel_b=sb[...], alog=al[...], dtb=dt[...])


def _dn_prep_fwd(conv, ab, sel_a, sel_b, alog, dtb):
    def body(c_ref, ab_ref, sa, sb, al, dt, q_out, k_out, g_out, b_out):
        qc = tuple(c_ref[:, h * HEAD:(h + 1) * HEAD] for h in range(HEADS))
        kc = tuple(c_ref[:, DN_WIDTH + h * HEAD:DN_WIDTH + (h + 1) * HEAD] for h in range(HEADS))
        qs, ks, g, beta = _dn_prep_fn((qc, kc, ab_ref[...]), _dn_prep_consts(sa, sb, al, dt))
        for h in range(HEADS):
            q_out[:, h * HEAD:(h + 1) * HEAD] = qs[h]
            k_out[:, h * HEAD:(h + 1) * HEAD] = ks[h]
        g_out[...] = g
        b_out[...] = beta

    return _rows_call("dn_prep_fwd", body, [conv, ab], [sel_a, sel_b, alog, dtb], [(DN_WIDTH, F32)] * 4, [],
                      _row_tile(conv.shape[0]))


def _dn_prep_bwd(conv, ab, dq, dk, dv, dg, db, sel_a, sel_b, alog, dtb):
    def body(c_ref, ab_ref, dq_r, dk_r, dv_r, dg_r, db_r, sa, sb, al, dt, dc_out, dab_out, dal_out, ddt_out):
        qc = tuple(c_ref[:, h * HEAD:(h + 1) * HEAD] for h in range(HEADS))
        kc = tuple(c_ref[:, DN_WIDTH + h * HEAD:DN_WIDTH + (h + 1) * HEAD] for h in range(HEADS))
        consts = _dn_prep_consts(sa, sb, al, dt)
        sel = dict(sel_a=consts["sel_a"], sel_b=consts["sel_b"])
        _, vjp = jax.vjp(lambda rows, ad: _dn_prep_fn(rows, {**sel, **ad}), (qc, kc, ab_ref[...]),
                         dict(alog=consts["alog"], dtb=consts["dtb"]))
        cq = tuple(dq_r[:, h * HEAD:(h + 1) * HEAD] for h in range(HEADS))
        ck = tuple(dk_r[:, h * HEAD:(h + 1) * HEAD] for h in range(HEADS))
        (dqc, dkc, dab), dad = vjp((cq, ck, dg_r[...], db_r[...]))
        for h in range(HEADS):
            dc_out[:, h * HEAD:(h + 1) * HEAD] = dqc[h]
            dc_out[:, DN_WIDTH + h * HEAD:DN_WIDTH + (h + 1) * HEAD] = dkc[h]
        dc_out[:, 2 * DN_WIDTH:3 * DN_WIDTH] = dv_r[...]
        dab_out[...] = dab.astype(BF16)
        _accumulate(dal_out, dad["alog"])
        _accumulate(ddt_out, dad["dtb"])

    return _rows_call("dn_prep_bwd", body, [conv, ab, dq, dk, dv, dg, db], [sel_a, sel_b, alog, dtb],
                      [(3 * DN_WIDTH, F32), (HEAD, BF16)], [(1, DN_WIDTH), (1, DN_WIDTH)], _row_tile(conv.shape[0]))


def _chunk_batch(t):
    nc = t // CHUNK
    return nc // 2 if nc % 2 == 0 else nc


def _dn_chunk_specs(t, nb):
    rows = nb * CHUNK
    blk = pl.BlockSpec((rows, HEAD), lambda h, b: (b, h))
    vblk = pl.BlockSpec((rows, HEAD), lambda h, b: (b, 2 * HEADS + h))
    mat = pl.BlockSpec((nb, HEAD, HEAD), lambda h, b: (b, h, 0))
    return rows, blk, vblk, mat


def _dn_chunk_fwd(qn, kn, conv, g, beta, host=None):
    t = qn.shape[0]
    nb = _chunk_batch(t)
    rows, blk, vblk, mat = _dn_chunk_specs(t, nb)

    def body(q_ref, k_ref, v_ref, g_ref, b_ref, n_o, b_o, qe_o, oo_o, eg_o):
        r3 = lambda x: x.reshape(nb, CHUNK, x.shape[-1])
        n_mat, b_mat, q_eff, o_own, eg = _dn_chunk_fn(r3(q_ref[...]), r3(k_ref[...]), r3(v_ref[...]), r3(g_ref[...]),
                                                      r3(g_ref[:, 0:CHUNK]), r3(b_ref[...]))
        n_o[...] = n_mat
        b_o[...] = b_mat
        qe_o[...] = q_eff.reshape(rows, HEAD)
        oo_o[...] = o_own.reshape(rows, HEAD)
        eg_o[...] = jnp.broadcast_to(eg, (nb, HEAD, HEAD))

    nc = t // CHUNK
    mats = jax.ShapeDtypeStruct((nc, DN_WIDTH, HEAD), F32)
    rowsd = jax.ShapeDtypeStruct((t, DN_WIDTH), F32)
    return _pcall(body, "dn_chunk_fwd", (HEADS, t // rows), [blk, blk, vblk, blk, blk], [mat, mat, blk, blk, mat],
                  [mats, mats, rowsd, rowsd, mats], [qn, kn, conv, g, beta], ("arbitrary", "arbitrary"), host=host)


def _dn_chunk_bwd(qn, kn, conv, g, beta, sall, gall, dq_eff, do, host=None):
    t = qn.shape[0]
    nb = _chunk_batch(t)
    rows, blk, vblk, mat = _dn_chunk_specs(t, nb)

    def body(q_ref, k_ref, v_ref, g_ref, b_ref, s_ref, ga_ref, dqe_ref, do_ref, dq_o, dk_o, dv_o, dg_o, db_o):
        r3 = lambda x: x.reshape(nb, CHUNK, x.shape[-1])
        _, vjp = jax.vjp(_dn_chunk_fn, r3(q_ref[...]), r3(k_ref[...]), r3(v_ref[...]), r3(g_ref[...]),
                         r3(g_ref[:, 0:CHUNK]), r3(b_ref[...]))
        s, ga = s_ref[...], ga_ref[...]
        d_n = -_bmm_nt(ga, s)
        d_eg = jnp.sum(ga * s, axis=1, keepdims=True)
        dq, dk, dv, dg, dg64, db = vjp((d_n, ga, r3(dqe_ref[...]), r3(do_ref[...]), d_eg))
        for o_ref, val in zip((dq_o, dk_o, dv_o, dg_o, db_o), (dq, dk, dv, dg, db)):
            o_ref[...] = val.reshape(rows, HEAD)
        dg_o[:, 0:CHUNK] += dg64.reshape(rows, CHUNK)

    return _pcall(body, "dn_chunk_bwd", (HEADS, t // rows), [blk, blk, vblk, blk, blk, mat, mat, blk, blk], [blk] * 5,
                  [jax.ShapeDtypeStruct((t, DN_WIDTH), F32)] * 5, [qn, kn, conv, g, beta, sall, gall, dq_eff, do],
                  ("arbitrary", "arbitrary"), host=host)


def _dn_rec_fwd(n_mat, b_mat, eg, host=None):
    nc = n_mat.shape[0]
    nb = _chunk_batch(nc * CHUNK)
    spec = pl.BlockSpec((nb, DN_WIDTH, HEAD), lambda i: (i, 0, 0))

    def body(n_ref, b_ref, eg_ref, sall_ref, s_scr):
        @pl.when(pl.program_id(0) == 0)
        def _():
            s_scr[...] = jnp.zeros(s_scr.shape, F32)

        for j in range(nb):
            sall_ref[j] = s_scr[...]
            for h in range(HEADS):
                sl = slice(h * HEAD, (h + 1) * HEAD)
                s_scr[sl, :] = _dn_rec_fn(s_scr[sl, :], n_ref[j, sl, :], b_ref[j, sl, :],
                                          eg_ref[j, h * HEAD:h * HEAD + 1, :])

    return _pcall(body, "dn_rec_fwd", (nc // nb,), [spec] * 3, spec, jax.ShapeDtypeStruct((nc, DN_WIDTH, HEAD), F32),
                  [n_mat, b_mat, eg], ("arbitrary",), scratch_shapes=[pltpu.VMEM((DN_WIDTH, HEAD), F32)], host=host)


def _dn_rec_bwd(n_mat, eg, ds_out, host=None):
    nc = n_mat.shape[0]
    nb = _chunk_batch(nc * CHUNK)
    steps = nc // nb
    spec = pl.BlockSpec((nb, DN_WIDTH, HEAD), lambda i: (steps - 1 - i, 0, 0))

    def body(n_ref, eg_ref, dso_ref, gall_ref, g_scr):
        @pl.when(pl.program_id(0) == 0)
        def _():
            g_scr[...] = jnp.zeros(g_scr.shape, F32)

        for j in reversed(range(nb)):
            gall_ref[j] = g_scr[...]
            for h in range(HEADS):
                sl = slice(h * HEAD, (h + 1) * HEAD)
                gv = g_scr[sl, :]
                g_scr[sl, :] = (gv * eg_ref[j, h * HEAD:h * HEAD + 1, :] - _mm_tn(n_ref[j, sl, :], gv)
                                + dso_ref[j, sl, :])

    return _pcall(body, "dn_rec_bwd", (steps,), [spec] * 3, spec, jax.ShapeDtypeStruct((nc, DN_WIDTH, HEAD), F32),
                  [n_mat, eg, ds_out], ("arbitrary",), scratch_shapes=[pltpu.VMEM((DN_WIDTH, HEAD), F32)], host=host)


def _dn_o_fwd(sall, q_eff, o_own):
    t = q_eff.shape[0]
    nb = _chunk_batch(t)
    rows, blk, _, mat = _dn_chunk_specs(t, nb)

    def body(s_ref, qe_ref, oo_ref, o_ref):
        r3 = lambda x: x.reshape(nb, CHUNK, HEAD)
        o_ref[...] = _dn_o_fn(s_ref[...], r3(qe_ref[...]), r3(oo_ref[...])).reshape(rows, HEAD)

    return _pcall(body, "dn_o_fwd", (HEADS, t // rows), [mat, blk, blk], blk, jax.ShapeDtypeStruct((t, DN_WIDTH), F32),
                  [sall, q_eff, o_own], ("arbitrary", "arbitrary"))


def _dn_o_bwd(sall, q_eff, do, host=None):
    t = q_eff.shape[0]
    nb = _chunk_batch(t)
    rows, blk, _, mat = _dn_chunk_specs(t, nb)

    def body(s_ref, qe_ref, do_ref, dqe_ref, ds_ref):
        r3 = lambda x: x.reshape(nb, CHUNK, HEAD)
        dov = r3(do_ref[...])
        dqe_ref[...] = _bmm_nt(dov, s_ref[...]).reshape(rows, HEAD)
        ds_ref[...] = _bmm_tn(r3(qe_ref[...]), dov)

    nc = t // CHUNK
    return _pcall(body, "dn_o_bwd", (HEADS, t // rows), [mat, blk, blk], [blk, mat],
                  [jax.ShapeDtypeStruct((t, DN_WIDTH), F32), jax.ShapeDtypeStruct((nc, DN_WIDTH, HEAD), F32)],
                  [sall, q_eff, do], ("arbitrary", "arbitrary"), host=host)


def _loss_call(h2, tgt, n_valid):
    t, n = h2.shape
    r = _row_tile(t)

    def body(h_ref, t_ref, dy_ref, dy16_ref, acc_ref):
        rows = pl.program_id(0) * r + lax.broadcasted_iota(jnp.int32, (r, n), 0)
        valid = jnp.logical_and(rows >= N_META, rows < n_valid)
        e = jnp.where(valid, h_ref[...] - t_ref[...], 0.0)
        dy = e * (1.0 / n)
        dy_ref[...] = dy
        dy16_ref[...] = dy.astype(BF16)
        _accumulate(acc_ref, jnp.sum(e * e, axis=0, keepdims=True))

    return _rows_call("loss", body, [h2, tgt], [], [(n, F32), (n, BF16)], [(1, n)], r)


def _adamw_update(w, g, m, v):
    m2 = ADAM_B1 * m + (1.0 - ADAM_B1) * g
    v2 = ADAM_B2 * v + (1.0 - ADAM_B2) * (g * g)
    m_hat = m2 / (1.0 - ADAM_B1 ** ADAM_STEP)
    v_hat = v2 / (1.0 - ADAM_B2 ** ADAM_STEP)
    return -ADAM_LR * (m_hat / (jnp.sqrt(v_hat) + ADAM_EPS) + ADAM_WD * w), m2, v2


def _adamw_small(ws, gs, ms, vs):
    n = len(ws)

    def body(*refs):
        for i in range(n):
            d, m2, v2 = _adamw_update(refs[i][...], refs[n + i][...], refs[2 * n + i][...], refs[3 * n + i][...])
            refs[4 * n + i][...] = d
            refs[5 * n + i][...] = m2
            refs[6 * n + i][...] = v2

    shapes = [jax.ShapeDtypeStruct(a.shape, F32) for a in ws]
    outs = pl.pallas_call(body, name="adamw_small", out_shape=shapes * 3,
                          compiler_params=pltpu.CompilerParams(vmem_limit_bytes=VMEM_LIMIT))(*ws, *gs, *ms, *vs)
    return outs[:n], outs[n:2 * n], outs[2 * n:]


def _adamw_call(name, w, g, m, v, host=None):
    rows, cols = w.shape
    by_rows = rows % 8 == 0

    def body(w_ref, g_ref, m_ref, v_ref, g_out, d_ref, m_out, v_out):
        gv = g_ref[...] if by_rows else g_ref[0:rows, :]
        g_out[...] = gv
        d_ref[...], m_out[...], v_out[...] = _adamw_update(w_ref[...], gv, m_ref[...], v_ref[...])

    if by_rows:
        tr = _tile(rows, 256, 8)
        spec = g_spec = pl.BlockSpec((tr, cols), lambda i: (i, 0))
        grid = (rows // tr,)
    else:
        tc = _tile(cols, 256, 128)
        spec = pl.BlockSpec((rows, tc), lambda j: (0, j))
        g_spec = pl.BlockSpec((g.shape[0], tc), lambda j: (0, j))
        grid = (cols // tc,)
    return _pcall(body, name, grid, [spec, g_spec, spec, spec], [spec] * 4, [jax.ShapeDtypeStruct((rows, cols), F32)] * 4,
                  [w, g, m, v], ("arbitrary",), host=host)


def _rope_tables(t):
    half = ROPE // 2
    inv_freq = np.float32(ROPE_THETA) ** (-np.arange(half, dtype=np.float32) / np.float32(half))
    ang = np.arange(t, dtype=np.float32)[:, None] * inv_freq[None, :].astype(np.float32)
    z = np.zeros((t, HEAD - ROPE), np.float32)
    cos = np.concatenate([np.cos(ang), np.cos(ang), z], axis=1).astype(np.float32)
    sin = np.concatenate([np.sin(ang), np.sin(ang), z], axis=1).astype(np.float32)
    k = np.arange(HEAD)[:, None]
    l = np.arange(HEAD)[None, :]
    perm = np.where((l < half) & (k == l + half), -1.0, 0.0) + np.where((l >= half) & (l < ROPE) & (k == l - half), 1.0, 0.0)
    return jnp.asarray(cos), jnp.asarray(sin), jnp.asarray(perm.astype(np.float32))


def _win_to_pad(w):
    z = lambda n: jnp.zeros((n, w.shape[1]), w.dtype)
    return jnp.concatenate([w[576:2112], w[2112:2624], w[0:256], w[256:512], w[512:576], z(64), w[2624:2632], z(120)],
                           axis=0)


def _win_from_pad(g):
    return jnp.concatenate([g[2048:2304], g[2304:2560], g[2560:2624], g[0:1536], g[1536:2048], g[2688:2696]], axis=0)


def _qk_to_pad(w):
    w4 = w.reshape(HEADS, QK_DIM, w.shape[-1])
    return jnp.concatenate([w4, jnp.zeros((HEADS, QK_PAD - QK_DIM, w.shape[-1]), w.dtype)], axis=1).reshape(
        HEADS * QK_PAD, w.shape[-1])


def _qk_from_pad(g):
    return g.reshape(HEADS, QK_PAD, g.shape[-1])[:, :QK_DIM].reshape(HEADS * QK_DIM, g.shape[-1])


def _ff_to_pad(a, axis):
    shape = list(a.shape)
    shape[axis:axis + 1] = [N_CHIPS, FF_SHARD]
    a4 = a.reshape(shape)
    shape[axis + 1] = FF_BLOCK - FF_SHARD
    out = jnp.concatenate([a4, jnp.zeros(shape, a.dtype)], axis=axis + 1)
    shape[axis:axis + 2] = [D_FF_P]
    return out.reshape(shape)


def _ff_from_pad(a, axis):
    shape = list(a.shape)
    shape[axis:axis + 1] = [N_CHIPS, FF_BLOCK]
    a4 = lax.slice_in_dim(a.reshape(shape), 0, FF_SHARD, axis=axis + 1)
    shape[axis:axis + 2] = [D_FF]
    return a4.reshape(shape)


class _LocalPlan:
    def __init__(self, wt):
        self.wt, self.grads = wt, {}

    def weight(self, name):
        return self.wt[name]

    def host(self, point):
        return None

    def grad(self, name, value):
        self.grads[name] = value


def _local_step(x, tgt, wt, plan=None):
    plan = _LocalPlan(wt) if plan is None else plan
    s = x.shape[0]
    n_valid = N_META + s
    t = -(-n_valid // HEAD) * HEAD
    zpad = jnp.zeros((t - n_valid, D_MODEL), F32)
    h0 = jnp.concatenate([wt["meta_tokens"], x, zpad], axis=0)
    tgt_p = jnp.concatenate([jnp.zeros((N_META, D_MODEL), F32), tgt, zpad], axis=0)
    cos, sin, perm = _rope_tables(t)
    qn_w = jnp.concatenate([wt["q_norm_w"], jnp.zeros((1, QK_PAD - QK_DIM), F32)], axis=1)
    kn_w = jnp.concatenate([wt["k_norm_w"], jnp.zeros((1, QK_PAD - QK_DIM), F32)], axis=1)
    head_id = jnp.arange(DN_WIDTH)[None, :] // HEAD
    lane = jnp.arange(HEAD)[:, None]
    sel_a = (lane == head_id).astype(F32)
    sel_b = (lane == head_id + HEADS).astype(F32)
    alog = jnp.repeat(wt["dn_A_log"], HEAD, axis=1)
    dtb = jnp.repeat(wt["dn_dt_bias"], HEAD, axis=1)
    conv_w, conv_b = wt["ffn_conv_w"], wt["ffn_conv_b"]

    u = _rms_fwd("attn_norm_fwd", h0, wt["attn_norm_w"], host=plan.host("attn_norm_fwd"))
    win, wq, wkv = plan.weight("w_in_t"), plan.weight("w_q_t"), plan.weight("w_kv_t")
    proj = _matmul("in_proj", u, win, "nt", F32)
    z = (proj, DN_WIDTH, 3)
    q_lat, kv_lat, k_pe, ab = (proj, LORA, 8), (proj, LORA, 9), (proj, HEAD, 20), (proj, HEAD, 21)
    mla_consts = (wt["q_a_norm_w"], wq, wt["kv_a_norm_w"], wkv, qn_w, kn_w, perm)
    q, k, v = _mla_prep_fwd(q_lat, kv_lat, k_pe, cos, sin, *mla_consts)
    o_mla = _attn_fwd(q, k, v, host=plan.host("attn_fwd"))
    conv = _dn_conv_fwd(proj, wt["dn_conv_w"])
    dn_consts = (sel_a, sel_b, alog, dtb)
    qn, kn, g, beta = _dn_prep_fwd(conv, ab, *dn_consts)
    n_mat, b_mat, q_eff, o_own, eg = _dn_chunk_fwd(qn, kn, conv, g, beta, host=plan.host("dn_chunk_fwd"))
    sall = _dn_rec_fwd(n_mat, b_mat, eg)
    o_dn = _dn_o_fwd(sall, q_eff, o_own)
    mixed = _mix_out_fwd(o_mla, o_dn, z, wt["mla_out_norm_w"], wt["dn_out_norm_w"])
    w_out = plan.weight("w_out")
    h1 = _matmul("out_proj", mixed, w_out, "nn", F32, res=h0)
    n2 = _rms_fwd("ffn_norm_fwd", h1, wt["ffn_norm_w"])
    w_gate, w_up = plan.weight("w_gate_t"), plan.weight("w_up_t")
    gpre = _matmul("gate_proj", n2, w_gate, "nt", BF16, host=plan.host("gate_proj"))
    up = _matmul("up_proj", n2, w_up, "nt", BF16, host=plan.host("up_proj"))
    act = _glu_fwd(gpre, up, conv_w, conv_b)
    w_down = plan.weight("w_down")
    h2 = _matmul("down_proj", act, w_down, "nn", F32, res=h1)
    dy, dy16, sq = _loss_call(h2, tgt_p, n_valid)

    grads = {}
    dact = _matmul("down_dx", dy16, w_down, "nt", BF16)
    plan.grad("w_down", _matmul("down_dw", act, dy16, "tn", BF16))
    dgpre, dup, grads["ffn_conv_w"], grads["ffn_conv_b"] = _glu_bwd(gpre, up, conv_w, conv_b, dact)
    plan.grad("w_gate_t", _matmul("gate_dw", dgpre, n2, "tn", BF16))
    plan.grad("w_up_t", _matmul("up_dw", dup, n2, "tn", BF16))
    dn2a = _matmul("gate_dx", dgpre, w_gate, "nn", BF16, host=plan.host("gate_dx"))
    dn2b = _matmul("up_dx", dup, w_up, "nn", BF16, host=plan.host("up_dx"))
    dh1, dh1_16, grads["ffn_norm_w"] = _rms_bwd("ffn_norm_bwd", h1, wt["ffn_norm_w"], [dn2a, dn2b], dy)
    dmixed = _matmul("out_dx", dh1_16, w_out, "nt", BF16)
    plan.grad("w_out", _matmul("out_dw", mixed, dh1_16, "tn", BF16))
    do_mla, do_dn, dz, grads["mla_out_norm_w"], grads["dn_out_norm_w"] = _mix_out_bwd(
        o_mla, o_dn, z, dmixed, wt["mla_out_norm_w"], wt["dn_out_norm_w"], host=plan.host("mix_out_bwd"))
    dq_eff, ds_out = _dn_o_bwd(sall, q_eff, do_dn)
    gall = _dn_rec_bwd(n_mat, eg, ds_out)
    dqn, dkn, dv_dn, dg, dbeta = _dn_chunk_bwd(qn, kn, conv, g, beta, sall, gall, dq_eff, do_dn,
                                               host=plan.host("dn_chunk_bwd"))
    dconv, dab, dalog, ddtb = _dn_prep_bwd(conv, ab, dqn, dkn, dv_dn, dg, dbeta, *dn_consts)
    grads["dn_A_log"] = jnp.sum(dalog.reshape(HEADS, HEAD), axis=1)[None, :]
    grads["dn_dt_bias"] = jnp.sum(ddtb.reshape(HEADS, HEAD), axis=1)[None, :]
    ddn_pre, grads["dn_conv_w"] = _dn_conv_bwd(proj, wt["dn_conv_w"], dconv)
    dq, dk, dv = _attn_bwd(q, k, v, do_mla, host=plan.host("attn_bwd"))
    dq_lat, dkv_lat, dk_pe, dqa, dwq, dkva, dwkv, dqnw, dknw = _mla_prep_bwd(
        q_lat, kv_lat, k_pe, cos, sin, dq, dk, dv, *mla_consts, host=plan.host("mla_prep_bwd"))
    grads["q_a_norm_w"], grads["kv_a_norm_w"] = dqa, dkva
    plan.grad("w_q_t", dwq)
    plan.grad("w_kv_t", dwkv)
    grads["q_norm_w"], grads["k_norm_w"] = dqnw[:, :QK_DIM], dknw[:, :QK_DIM]
    dproj = jnp.concatenate([ddn_pre, dz, dq_lat, dkv_lat, dk_pe, dab], axis=1)
    plan.grad("w_in_t", _matmul("in_dw", dproj, u, "tn", F32))
    du = _matmul("in_dx", dproj, win, "nn", BF16, host=plan.host("in_dx"))
    dh0, _, grads["attn_norm_w"] = _rms_bwd("attn_norm_bwd", h0, wt["attn_norm_w"], [du], dh1,
                                            host=plan.host("attn_norm_bwd"))
    grads["meta_tokens"] = dh0[0:N_META]
    if isinstance(plan, _LocalPlan):
        grads.update(plan.grads)
    return sq, dh0[N_META:n_valid], grads


def _mesh_pos():
    return lax.axis_index("x"), lax.axis_index("y"), lax.axis_index("c")


def _other_chips(x, y):
    return [(1 - x, y), (x, 1 - y), (1 - x, 1 - y)]


def _remote(src, dst, send_sems, recv_sems, k, to):
    return pltpu.make_async_remote_copy(src_ref=src, dst_ref=dst, send_sem=send_sems.at[k], recv_sem=recv_sems.at[k],
                                        device_id=to, device_id_type=MESH)


SIBLING_ID, CHIPS_ID, GATHER_ID, ALL_ID = 1, 2, 3, 4


def _sibling_peer():
    x, y, c = _mesh_pos()
    return [(x, y, 1 - c)]


def _chip_peers():
    x, y, c = _mesh_pos()
    return [(qx, qy, c) for qx, qy in _other_chips(x, y)]


def _copies_exchange(make, ins, out_shape, nsem, peers=None, cid=None):
    def prog(in_refs, out_refs, send_sems, recv_sems):
        copies = make(in_refs, out_refs, send_sems, recv_sems)

        def start():
            for cp in copies:
                cp.start()

        def finish():
            for cp in copies:
                cp.wait()

        return start, finish

    return _Exchange(prog, ins, out_shape, nsem, peers, cid)


def _all_gather(shards):
    def prog(srcs, dsts, send_sems, recv_sems):
        x, y, c = _mesh_pos()
        p = 2 * x + y
        sibling = (x, y, 1 - c)
        chips = _other_chips(x, y)
        bufs = tuple((s, d, s.shape[0] // 2) for s, d in zip(srcs, dsts))

        def half(ref, rows, which):
            return ref.at[pl.ds(which * rows, rows), :]

        def copy(i, k, src, dst, to):
            return _remote(src, dst, send_sems, recv_sems, 6 * i + k, to)

        sends = [copy(i, j, half(src, rows, c), half(dst.at[p], rows, c), (*chip, c))
                 for i, (src, dst, rows) in enumerate(bufs) for j, chip in enumerate(chips)]

        def start():
            for cp in sends:
                cp.start()

        def finish():
            passed = []
            for i, (src, dst, rows) in enumerate(bufs):
                for j, (qx, qy) in enumerate(chips):
                    block = half(dst.at[2 * qx + qy], rows, c)
                    copy(i, j, block, block, (x, y, c)).wait_recv()
                    fwd = copy(i, 3 + j, block, block, sibling)
                    fwd.start()
                    passed.append(fwd)
            for i, (src, dst, rows) in enumerate(bufs):
                for j, (qx, qy) in enumerate(chips):
                    block = half(dst.at[2 * qx + qy], rows, 1 - c)
                    copy(i, 3 + j, block, block, (x, y, c)).wait_recv()
            for cp in sends + passed:
                cp.wait_send()

        return start, finish

    return _Exchange(prog, shards, [jax.ShapeDtypeStruct((N_CHIPS, *s.shape), s.dtype) for s in shards], 6 * len(shards),
                     lambda: _sibling_peer() + _chip_peers(), GATHER_ID)


def _gathered(ex):
    p = 2 * lax.axis_index("x") + lax.axis_index("y")
    return [lax.dynamic_update_slice(g, s[None], (p, 0, 0)) for g, s in zip(ex.outs, ex.ins)]


def _rs_to_sibling(bufs):
    def make(srcs, dsts, send_sems, recv_sems):
        x, y, c = _mesh_pos()
        copies = []
        for i, (src, dst) in enumerate(zip(srcs, dsts)):
            half = src.shape[1] // 2
            copies.append(_remote(src.at[:, pl.ds((1 - c) * half, half), :], dst, send_sems, recv_sems, i, (x, y, 1 - c)))
        return copies

    return _copies_exchange(make, bufs,
                            [jax.ShapeDtypeStruct((N_CHIPS, b.shape[1] // 2, b.shape[2]), b.dtype) for b in bufs],
                            len(bufs), _sibling_peer, SIBLING_ID)


def _rs_pair_add(name, bufs, gots, c, out_dtype):
    n = len(bufs)

    def body(c_ref, *refs):
        for a_ref, b_ref, o_ref in zip(refs[:n], refs[n:2 * n], refs[2 * n:]):
            o_ref[...] = (a_ref[...].astype(F32) + b_ref[...].astype(F32)).astype(out_dtype)

    mine = [pl.BlockSpec((None, g.shape[1], g.shape[2]), lambda j, cr: (j, cr[0], 0)) for g in gots]
    whole = [pl.BlockSpec((None, g.shape[1], g.shape[2]), lambda j, cr: (j, 0, 0)) for g in gots]
    return pl.pallas_call(
        body, name=name,
        grid_spec=pltpu.PrefetchScalarGridSpec(num_scalar_prefetch=1, grid=(N_CHIPS,), in_specs=mine + whole, out_specs=whole),
        out_shape=[jax.ShapeDtypeStruct(g.shape, out_dtype) for g in gots],
        compiler_params=_cparams(("arbitrary",)))(c, *bufs, *gots)


def _rs_to_chips(accs):
    def make(srcs, dsts, send_sems, recv_sems):
        x, y, c = _mesh_pos()
        return [_remote(src.at[2 * qx + qy], dst.at[k], send_sems, recv_sems, 3 * i + k, (qx, qy, c))
                for i, (src, dst) in enumerate(zip(srcs, dsts)) for k, (qx, qy) in enumerate(_other_chips(x, y))]

    return _copies_exchange(make, accs, [jax.ShapeDtypeStruct((3, a.shape[1], a.shape[2]), a.dtype) for a in accs],
                            3 * len(accs), _chip_peers, CHIPS_ID)


def _rs_chip_add(name, accs, gots, p):
    n = len(accs)
    slot = (0, 1, 0, 2)

    def body(p_ref, *refs):
        me = p_ref[0]
        for own_ref, got_ref, o_ref in zip(refs[:n], refs[n:2 * n], refs[2 * n:]):
            total = None
            for chip in range(N_CHIPS):
                val = own_ref[...].astype(F32)
                for e in (1, 2, 3):
                    val = jnp.where((chip ^ me) == e, got_ref[slot[e]].astype(F32), val)
                total = val if total is None else total + val
            o_ref[...] = total

    own = [pl.BlockSpec((None, a.shape[1], a.shape[2]), lambda i, pr: (pr[0], 0, 0)) for a in accs]
    got = [pl.BlockSpec(g.shape, lambda i, pr: (0, 0, 0)) for g in gots]
    out = [pl.BlockSpec((a.shape[1], a.shape[2]), lambda i, pr: (0, 0)) for a in accs]
    return pl.pallas_call(
        body, name=name,
        grid_spec=pltpu.PrefetchScalarGridSpec(num_scalar_prefetch=1, grid=(1,), in_specs=own + got, out_specs=out),
        out_shape=[jax.ShapeDtypeStruct((a.shape[1], a.shape[2]), F32) for a in accs],
        compiler_params=_cparams(("arbitrary",)))(p, *accs, *gots)


def _rs_share(ress):
    def make(srcs, dsts, send_sems, recv_sems):
        x, y, c = _mesh_pos()
        return [_remote(src, dst, send_sems, recv_sems, i, (x, y, 1 - c)) for i, (src, dst) in enumerate(zip(srcs, dsts))]

    return _copies_exchange(make, ress, [jax.ShapeDtypeStruct(r.shape, F32) for r in ress], len(ress), _sibling_peer,
                            SIBLING_ID)


def _shared(ex):
    south = lax.axis_index("c") == 0
    return [jnp.concatenate([jnp.where(south, r, g), jnp.where(south, g, r)], axis=0) for r, g in zip(ex.ins, ex.outs)]


def _all_to_all_devices(vec):
    def others():
        x, y, c = _mesh_pos()
        return [((1 - x if r & 4 else x), (1 - y if r & 2 else y), (1 - c if r & 1 else c)) for r in range(1, 8)]

    def make(srcs, dsts, send_sems, recv_sems):
        x, y, c = _mesh_pos()
        me = 4 * x + 2 * y + c
        return [_remote(srcs[0], dsts[0].at[me], send_sems, recv_sems, r, peer) for r, peer in enumerate(others())]

    return _copies_exchange(make, [vec], [jax.ShapeDtypeStruct((8, *vec.shape), vec.dtype)], 7, others, ALL_ID)


def _sum_devices(stack):
    def body(s_ref, o_ref):
        total = s_ref[0]
        for d in range(1, 8):
            total = total + s_ref[d]
        o_ref[...] = total

    return pl.pallas_call(body, name="sum_devices", out_shape=jax.ShapeDtypeStruct(stack.shape[1:], F32),
                          compiler_params=pltpu.CompilerParams(vmem_limit_bytes=VMEM_LIMIT))(stack)


def _pad_rows(flat, rows):
    return jnp.concatenate([flat, jnp.zeros((rows * LANES - flat.shape[0],), flat.dtype)]).reshape(rows, LANES)


def _unshard(g4, shape, axis):
    a = g4.reshape(N_CHIPS, *shape)
    if axis == 0:
        return a.reshape(N_CHIPS * shape[0], shape[1])
    return jnp.transpose(a, (1, 0, 2)).reshape(shape[0], N_CHIPS * shape[1])


def _shard4(full, shape, axis):
    if axis == 0:
        return full.reshape(N_CHIPS, shape[0] * shape[1])
    a = full.reshape(shape[0], N_CHIPS, shape[1])
    return jnp.transpose(a, (1, 0, 2)).reshape(N_CHIPS, shape[0] * shape[1])


def _pad_axis0(a, rows):
    return jnp.concatenate([a, jnp.zeros((rows - a.shape[0], *a.shape[1:]), a.dtype)], axis=0)


def _pad_axis1(a, rows):
    return jnp.concatenate([a, jnp.zeros((a.shape[0], rows - a.shape[1], *a.shape[2:]), a.dtype)], axis=1)


def _shard_to_strip(name, w):
    _, (shape, axis, rows) = name, {n: (s, ax, r) for n, s, ax, r in BIG}[name]
    w2 = w.reshape(shape).astype(BF16)
    if name == "w_in":
        return w2
    return _pad_axis0(w2.T if axis == 1 else w2, rows)


LOCAL_NAME = dict(w_in="w_in_t", w_q_b="w_q_t", w_kv_b="w_kv_t", w_out="w_out", w_gate="w_gate_t", w_up="w_up_t",
                  w_down="w_down")


WIN_SEGMENTS = ((576, 2112, 0), (2112, 2624, 1536), (0, 256, 2048), (256, 512, 2304), (512, 576, 2560), (2624, 2632, 2688))


def _strips_to_weight(name, g4):
    if name == "w_in":
        return _win_to_pad(jnp.transpose(g4, (0, 2, 1)).reshape(IN_COLS, D_MODEL))
    if name == "w_q_b":
        return _qk_to_pad(g4.reshape(HEADS * QK_DIM, LORA))
    return g4.reshape(N_CHIPS * g4.shape[1], g4.shape[2])


def _grad_to_strips(name, g):
    if name == "w_in":
        strips = []
        for q in range(N_CHIPS):
            pieces = []
            for a, b, local in sorted(WIN_SEGMENTS):
                s, e = max(a, q * IN_SHARD), min(b, (q + 1) * IN_SHARD)
                if s < e:
                    pieces.append(g[local + s - a:local + e - a])
            pieces.append(jnp.zeros((IN_SHARD_P - IN_SHARD, D_MODEL), g.dtype))
            strips.append(jnp.concatenate(pieces, axis=0))
        return jnp.stack(strips)
    if name == "w_q_b":
        return _qk_from_pad(g).reshape(N_CHIPS, QK_DIM, LORA)
    return g.reshape(N_CHIPS, g.shape[0] // N_CHIPS, g.shape[1])


class _MeshPlan:
    LATE = dict(attn_norm_fwd=("w_in", "w_q_b", "w_kv_b"), attn_fwd=("w_up",), dn_chunk_fwd=("w_out", "w_gate"),
                gate_proj=("w_down/0",), up_proj=("w_down/1",))
    GROUP_A = ("w_down", "w_gate", "w_up", "w_out")
    GROUP_B = ("w_in", "w_q_b", "w_kv_b")

    def __init__(self, w):
        x, y, c = _mesh_pos()
        self.ci = jnp.reshape(c, (1,)).astype(jnp.int32)
        self.pi = jnp.reshape(2 * x + y, (1,)).astype(jnp.int32)
        self.strip = {n: _shard_to_strip(n, w[n]) for n, _, _, _ in BIG}
        self.gathers, self.weights, self.g, self.acc, self.reduced = {}, {}, {}, {}, {}
        self.sibs, self.sib, self.chip, self.share, self.halves = [], None, None, None, [None, None]

    def gather_small(self, small):
        ex = _all_gather([small])
        ex.run("all_gather_small")
        return _gathered(ex)[0]

    def weight(self, local_name):
        if local_name not in self.weights:
            for point, (names, ex) in list(self.gathers.items()):
                if ex.outs is not None:
                    for n, g4 in zip(names, _gathered(ex)):
                        if "/" in n:
                            n, half = n.split("/")
                            self.halves[int(half)] = g4
                            if None in self.halves:
                                continue
                            g4 = jnp.concatenate(self.halves, axis=1)
                        self.weights[LOCAL_NAME[n]] = _strips_to_weight(n, g4)
                    del self.gathers[point]
        return self.weights[local_name]

    def _shard(self, name):
        if "/" not in name:
            return self.strip[name]
        name, half = name.split("/")
        rows = self.strip[name].shape[0] // 2
        return self.strip[name][int(half) * rows:(int(half) + 1) * rows]

    def grad(self, local_name, value):
        name = {v: k for k, v in LOCAL_NAME.items()}[local_name]
        self.g[name] = _grad_to_strips(name, value)

    def _pair_add(self, names, gots):
        accs = _rs_pair_add("rs_pair_add_" + names[0], [self.g[n] for n in names], gots, self.ci, BF16)
        self.acc.update(zip(names, accs))

    def _chip_add(self, names, chip):
        return _rs_chip_add("rs_chip_add_" + names[0], [self.acc[n] for n in names], chip.outs, self.pi)

    def _take_shared(self, names, share):
        for n, strip in zip(names, _shared(share)):
            self.reduced[n] = strip

    def host(self, point):
        a, b = self.GROUP_A, self.GROUP_B
        if point in self.LATE:
            names = self.LATE[point]
            ex = _all_gather([self._shard(n) for n in names])
            self.gathers[point] = (names, ex)
            return ex
        if point in ("gate_dx", "up_dx", "mix_out_bwd"):
            names = dict(gate_dx=a[:2], up_dx=a[2:3], mix_out_bwd=a[3:])[point]
            ex = _rs_to_sibling([self.g[n] for n in names])
            self.sibs.append(ex)
            return ex
        if point == "dn_chunk_bwd":
            self._pair_add(a, [o for ex in self.sibs for o in ex.outs])
            self.chip1 = _rs_to_chips([self.acc[n] for n in a[:2]])
            return self.chip1
        if point == "attn_bwd":
            self.chip2 = _rs_to_chips([self.acc[n] for n in a[2:]])
            return self.chip2
        if point == "mla_prep_bwd":
            ress = self._chip_add(a[:2], self.chip1) + self._chip_add(a[2:], self.chip2)
            self.share = _rs_share(ress)
            return self.share
        if point == "in_dx":
            self._take_shared(a, self.share)
            self.sib = _rs_to_sibling([self.g[n] for n in b])
            return self.sib
        if point == "attn_norm_bwd":
            self._pair_add(b, self.sib.outs)
            self.chip = _rs_to_chips([self.acc[n] for n in b])
            return self.chip
        return None

    def last_share(self):
        self.share = _rs_share(self._chip_add(self.GROUP_B, self.chip))
        return self.share

    def finish(self):
        self._take_shared(self.GROUP_B, self.share)
        return self.reduced


def _strip_to_shard(name, strip):
    shape, axis = {n: (s, ax) for n, s, ax, _ in BIG}[name]
    rows = shape[axis]
    return strip[:rows].T if axis == 1 else strip[:rows]


def kernel(x, meta_tokens, attn_norm_w, w_in, q_a_norm_w, w_q_b, kv_a_norm_w, w_kv_b, q_norm_w, k_norm_w, mla_out_norm_w, dn_conv_w, dn_A_log, dn_dt_bias, dn_out_norm_w, w_out, ffn_norm_w, w_gate, w_up, ffn_conv_w, ffn_conv_b, w_down, loss_target, m_meta_tokens, m_attn_norm_w, m_w_in, m_q_a_norm_w, m_w_q_b, m_kv_a_norm_w, m_w_kv_b, m_q_norm_w, m_k_norm_w, m_mla_out_norm_w, m_dn_conv_w, m_dn_A_log, m_dn_dt_bias, m_dn_out_norm_w, m_w_out, m_ffn_norm_w, m_w_gate, m_w_up, m_ffn_conv_w, m_ffn_conv_b, m_w_down, v_meta_tokens, v_attn_norm_w, v_w_in, v_q_a_norm_w, v_w_q_b, v_kv_a_norm_w, v_w_kv_b, v_q_norm_w, v_k_norm_w, v_mla_out_norm_w, v_dn_conv_w, v_dn_A_log, v_dn_dt_bias, v_dn_out_norm_w, v_w_out, v_ffn_norm_w, v_w_gate, v_w_up, v_ffn_conv_w, v_ffn_conv_b, v_w_down):
    local = dict(locals())
    w = {n: local[n] for n in WEIGHTS}
    m = {n: local["m_" + n] for n in WEIGHTS}
    v = {n: local["v_" + n] for n in WEIGHTS}
    p = 2 * lax.axis_index("x") + lax.axis_index("y")

    plan = _MeshPlan(w)
    wf = _pad_rows(jnp.concatenate([w[n].reshape(-1) for n, _, _ in SMALL_SHARDED]), SMALL_ROWS)
    gf = plan.gather_small(wf).reshape(N_CHIPS, -1)
    full = {}
    off = 0
    for n, s, ax in SMALL_SHARDED:
        full[n] = _unshard(gf[:, off:off + s[0] * s[1]], s, ax)
        off += s[0] * s[1]
    for n, _ in REPLICATED:
        full[n] = w[n]
    full["ffn_conv_w"] = _ff_to_pad(full["ffn_conv_w"], 1)
    full["ffn_conv_b"] = _ff_to_pad(full["ffn_conv_b"], 1)

    sq, grad_x, g = _local_step(x[0], loss_target[0], full, plan)
    g["ffn_conv_w"] = _ff_from_pad(g["ffn_conv_w"], 1)
    g["ffn_conv_b"] = _ff_from_pad(g["ffn_conv_b"], 1)

    small_all = [n for n, _, _ in SMALL_SHARDED] + [n for n, _ in REPLICATED]
    vec = jnp.concatenate([g[n].reshape(-1) for n in small_all] + [jnp.reshape(0.5 / D_MODEL * jnp.sum(sq), (1,))])
    vec = _pad_rows(vec, -(-vec.shape[0] // (8 * LANES)) * 8)
    a2a = _all_to_all_devices(vec)

    gs, delta, new_m, new_v = {}, {}, {}, {}
    big = {n: (s, ax) for n, s, ax, _ in BIG}

    def adamw_big(n, strips, host=None):
        s, ax = big[n]
        flip = ax == 1 and s[1] % 8 == 0
        there = (lambda a: a.reshape(s).T) if flip else (lambda a: a.reshape(s))
        back = (lambda a: a.T.reshape(w[n].shape)) if flip else (lambda a: a.reshape(w[n].shape))
        strip = strips[n] if flip or ax == 0 else strips[n][:s[1]].T
        g2, d2, m2, v2 = _adamw_call("adamw_" + n, there(w[n]), strip, there(m[n]), there(v[n]), host=host)
        gs[n], delta[n], new_m[n], new_v[n] = back(g2), back(d2), back(m2), back(v2)

    adamw_big("w_down", plan.reduced, host=a2a)
    adamw_big("w_gate", plan.reduced, host=plan.last_share())
    adamw_big("w_up", plan.reduced)
    adamw_big("w_out", plan.reduced)
    strips = plan.finish()
    for n in plan.GROUP_B:
        adamw_big(n, strips)
    me = 4 * lax.axis_index("x") + 2 * lax.axis_index("y") + lax.axis_index("c")
    red = _sum_devices(lax.dynamic_update_slice(a2a.outs[0], vec[None], (me, 0, 0))).reshape(-1)
    off = 0
    for n in small_all:
        tot = red[off:off + g[n].size].reshape(g[n].shape)
        off += g[n].size
        shard = {sn: (s, ax) for sn, s, ax in SMALL_SHARDED}.get(n)
        if shard is not None:
            tot = lax.dynamic_slice_in_dim(tot, p * shard[0][1], shard[0][1], axis=1)
        gs[n] = tot
    loss = red[off]
    two_d = lambda a: a.reshape(a.shape[-2], a.shape[-1])
    outs = _adamw_small([two_d(w[n]) for n in small_all], [two_d(gs[n]) for n in small_all],
                        [two_d(m[n]) for n in small_all], [two_d(v[n]) for n in small_all])
    for i, n in enumerate(small_all):
        for dst, src in ((delta, outs[0]), (new_m, outs[1]), (new_v, outs[2])):
            dst[n] = src[i].reshape(w[n].shape)

    grad_out = [gs[n].reshape(w[n].shape) for n in WEIGHTS]
    return (loss, grad_x[None], *grad_out, *[delta[n] for n in WEIGHTS], *[new_m[n] for n in WEIGHTS],
            *[new_v[n] for n in WEIGHTS])
```

```python
import functools
import math

import jax
import jax.numpy as jnp
import numpy as np
from jax import lax
from jax.experimental import pallas as pl
from jax.experimental.pallas import tpu as pltpu

F32 = jnp.float32
BF16 = jnp.bfloat16
HI = lax.Precision.HIGHEST
MESH = pl.DeviceIdType.MESH

N_META = 16
D_MODEL = 1024
HEADS = 4
HEAD = 128
ROPE = 64
QK_DIM = HEAD + ROPE
QK_PAD = 2 * HEAD
LORA = 256
DN_WIDTH = HEADS * HEAD
CHUNK = 64
D_FF = 2816
N_CHIPS = 4
FF_SHARD = D_FF // N_CHIPS
FF_BLOCK = 768
D_FF_P = N_CHIPS * FF_BLOCK
IN_COLS = 2632
IN_SHARD = IN_COLS // N_CHIPS
IN_SHARD_P = 672
IN_PAD = 2816
NORM_EPS = 1e-6
ROPE_THETA = 10000.0
LANES = 512

ADAM_LR, ADAM_B1, ADAM_B2, ADAM_EPS, ADAM_WD, ADAM_STEP = 0.001, 0.9, 0.999, 1e-08, 0.01, 10

VMEM_LIMIT = 56 * 1024 * 1024

BIG = (("w_in", (1024, 658), 1, IN_SHARD_P), ("w_q_b", (256, 192), 1, 192), ("w_kv_b", (256, 256), 1, 256),
       ("w_out", (256, 1024), 0, 256), ("w_gate", (1024, 704), 1, FF_BLOCK), ("w_up", (1024, 704), 1, FF_BLOCK),
       ("w_down", (704, 1024), 0, FF_BLOCK))
SMALL_SHARDED = (("meta_tokens", (16, 256), 1), ("dn_conv_w", (4, 384), 1), ("ffn_conv_w", (3, 704), 1))
REPLICATED = (("attn_norm_w", 1024), ("q_a_norm_w", 256), ("kv_a_norm_w", 256), ("q_norm_w", 192), ("k_norm_w", 192),
              ("mla_out_norm_w", 128), ("dn_A_log", 4), ("dn_dt_bias", 4), ("dn_out_norm_w", 128), ("ffn_norm_w", 1024),
              ("ffn_conv_b", 2816))
WEIGHTS = ("meta_tokens", "attn_norm_w", "w_in", "q_a_norm_w", "w_q_b", "kv_a_norm_w", "w_kv_b", "q_norm_w", "k_norm_w",
           "mla_out_norm_w", "dn_conv_w", "dn_A_log", "dn_dt_bias", "dn_out_norm_w", "w_out", "ffn_norm_w", "w_gate",
           "w_up", "ffn_conv_w", "ffn_conv_b", "w_down")

SMALL_ROWS = 16
REP_ROWS = 16


def _cparams(sem):
    return pltpu.CompilerParams(dimension_semantics=sem, vmem_limit_bytes=VMEM_LIMIT)


class _Exchange:
    def __init__(self, prog, ins, out_shape, nsem, peers=None, cid=None):
        self.prog, self.ins, self.out_shape, self.nsem = prog, list(ins), list(out_shape), nsem
        self.peers, self.cid = peers, cid
        self.outs = None

    def sems(self):
        return [pltpu.SemaphoreType.DMA((self.nsem,)), pltpu.SemaphoreType.DMA((self.nsem,))]

    def programs(self, in_refs, out_refs, send_sems, recv_sems):
        start, finish = self.prog(in_refs, out_refs, send_sems, recv_sems)
        if self.cid is None:
            return start, finish
        peers = self.peers()

        def shake_and_start():
            barrier = pltpu.get_barrier_semaphore()
            for peer in peers:
                pl.semaphore_signal(barrier, inc=1, device_id=peer, device_id_type=MESH)
            pl.semaphore_wait(barrier, len(peers))
            start()

        return shake_and_start, finish

    def cparams(self, **kw):
        return pltpu.CompilerParams(has_side_effects=True, collective_id=self.cid, **kw)

    def run(self, name):
        any_spec = pl.BlockSpec(memory_space=pl.ANY)
        n = len(self.ins)

        def body(*refs):
            start, finish = self.programs(refs[:n], refs[n:-2], refs[-2], refs[-1])
            start()
            finish()

        self.outs = pl.pallas_call(
            body, name=name, in_specs=[any_spec] * n, out_specs=[any_spec] * len(self.out_shape),
            out_shape=self.out_shape, scratch_shapes=self.sems(), compiler_params=self.cparams())(*self.ins)
        return self.outs


def _pcall(body, name, grid, in_specs, out_specs, out_shape, args, sem, scratch_shapes=(), host=None):
    single = not isinstance(out_shape, (list, tuple))
    out_specs, out_shape = ([out_specs], [out_shape]) if single else (list(out_specs), list(out_shape))
    if host is None:
        outs = pl.pallas_call(body, name=name, grid=grid, in_specs=list(in_specs), out_specs=out_specs, out_shape=out_shape,
                              scratch_shapes=list(scratch_shapes), compiler_params=_cparams(sem))(*args)
        return outs[0] if single else outs
    any_spec = pl.BlockSpec(memory_space=pl.ANY)
    n_in, n_out, n_scr, nx_in, nx_out = len(in_specs), len(out_specs), len(scratch_shapes), len(host.ins), len(host.out_shape)

    def hosted(*refs):
        c_in, x_in = refs[:n_in], refs[n_in:n_in + nx_in]
        o0 = n_in + nx_in
        c_out, x_out = refs[o0:o0 + n_out], refs[o0 + n_out:o0 + n_out + nx_out]
        s0 = o0 + n_out + nx_out
        start, finish = host.programs(x_in, x_out, refs[s0 + n_scr], refs[s0 + n_scr + 1])
        first = functools.reduce(jnp.logical_and, [pl.program_id(d) == 0 for d in range(len(grid))])
        last = functools.reduce(jnp.logical_and, [pl.program_id(d) == grid[d] - 1 for d in range(len(grid))])
        pl.when(first)(start)
        body(*c_in, *c_out, *refs[s0:s0 + n_scr])
        pl.when(last)(finish)

    outs = pl.pallas_call(
        hosted, name=name, grid=grid, in_specs=list(in_specs) + [any_spec] * nx_in,
        out_specs=out_specs + [any_spec] * nx_out, out_shape=out_shape + host.out_shape,
        scratch_shapes=list(scratch_shapes) + host.sems(),
        compiler_params=host.cparams(dimension_semantics=sem, vmem_limit_bytes=VMEM_LIMIT))(*args, *host.ins)
    host.outs = outs[n_out:]
    return outs[0] if single else outs[:n_out]


NN, NT, TN = ((1,), (0,)), ((1,), (1,)), ((0,), (0,))


def _shift_dims(dims, batch):
    if not batch:
        return (dims, ((), ()))
    return (((dims[0][0] + 1,), (dims[1][0] + 1,)), ((0,), (0,)))


def _make_mm(dims, exact, batch=False):
    def raw(a, b, d):
        dn = _shift_dims(d, batch)
        if exact == "split_lhs":
            ah, bh = a.astype(BF16), b.astype(BF16)
            al = (a - ah.astype(F32)).astype(BF16)
            return lax.dot_general(ah, bh, dn, preferred_element_type=F32) + lax.dot_general(al, bh, dn,
                                                                                              preferred_element_type=F32)
        if exact == "split":
            ah, bh = a.astype(BF16), b.astype(BF16)
            al, bl = (a - ah.astype(F32)).astype(BF16), (b - bh.astype(F32)).astype(BF16)
            dot = lambda p, q: lax.dot_general(p, q, dn, preferred_element_type=F32)
            return dot(ah, bh) + (dot(ah, bl) + dot(al, bh))
        if exact:
            return lax.dot_general(a.astype(F32), b.astype(F32), dn, precision=HI, preferred_element_type=F32)
        return lax.dot_general(a.astype(BF16), b.astype(BF16), dn, preferred_element_type=F32)

    @jax.custom_vjp
    def mm(a, b):
        return raw(a, b, dims)

    def fwd(a, b):
        return raw(a, b, dims), (a, b)

    def bwd(res, g):
        a, b = res
        if dims == NN:
            da, db = raw(g, b, NT), raw(a, g, TN)
        elif dims == NT:
            da, db = raw(g, b, NN), raw(g, a, TN)
        else:
            da, db = raw(b, g, NT), raw(a, g, NN)
        return da.astype(a.dtype), db.astype(b.dtype)

    mm.defvjp(fwd, bwd)
    return mm


_mm = _make_mm(NN, False)
_mm_nt = _make_mm(NT, False)
_mm_tn = _make_mm(TN, False)
_mmx = _make_mm(NN, "split_lhs")
_bmm = _make_mm(NN, False, batch=True)
_bmm_nt = _make_mm(NT, False, batch=True)
_bmm_tn = _make_mm(TN, False, batch=True)
_bmmx = _make_mm(NN, True, batch=True)
_bmms = _make_mm(NN, "split", batch=True)
_bmms_nt = _make_mm(NT, "split", batch=True)
_bmms_tn = _make_mm(TN, "split", batch=True)


@jax.custom_vjp
def _unit_lower_inv(a):
    n = a.shape[-1]
    eye = (lax.broadcasted_iota(jnp.int32, a.shape, 1) == lax.broadcasted_iota(jnp.int32, a.shape, 2)).astype(F32)
    x = -a
    t = eye + x
    for _ in range(max(n.bit_length() - 2, 0)):
        x = _bmms(x, x)
        t = t + _bmms(t, x)
    return t


def _unit_lower_inv_fwd(a):
    t = _unit_lower_inv(a)
    return t, t


def _unit_lower_inv_bwd(t, g):
    return (-_bmms_tn(t, _bmms_nt(g, t)),)


_unit_lower_inv.defvjp(_unit_lower_inv_fwd, _unit_lower_inv_bwd)


def _scan_chunk_rows(x, reverse):
    nb, c, w = x.shape
    y = x.reshape(nb * c, w)
    pos = lax.broadcasted_iota(jnp.int32, y.shape, 0) % c
    step = 1
    while step < c:
        if reverse:
            y = y + jnp.where(pos < c - step, pltpu.roll(y, nb * c - step, 0), 0.0)
        else:
            y = y + jnp.where(pos >= step, pltpu.roll(y, step, 0), 0.0)
        step *= 2
    return y.reshape(nb, c, w)


@jax.custom_vjp
def _chunk_cumsum(x):
    return _scan_chunk_rows(x, False)


_chunk_cumsum.defvjp(lambda x: (_scan_chunk_rows(x, False), None), lambda _, g: (_scan_chunk_rows(g, True),))


def _rms(x, w, n):
    ms = jnp.sum(x * x, axis=-1, keepdims=True) * (1.0 / n)
    return x * lax.rsqrt(ms + NORM_EPS) * w


def _silu(x):
    return x * jax.nn.sigmoid(x)


def _softplus(x):
    return jnp.maximum(x, 0.0) + jnp.log(1.0 + jnp.exp(-jnp.abs(x)))


def _rope(x, cos, sin, perm):
    return x * cos + _mmx(x, perm) * sin


def _mla_prep_fn(rows, consts):
    q_lat, kv_lat, k_pe, cos, sin = rows
    qn = _rms(q_lat, consts["qa_w"], LORA)
    kvn = _rms(kv_lat, consts["kva_w"], LORA)
    outs = []
    for h in range(HEADS):
        q_n = _mm_nt(qn, consts["wq_n"][h])
        q_r = _mm_nt(qn, consts["wq_r"][h])
        rs = lax.rsqrt((jnp.sum(q_n * q_n, -1, keepdims=True) + jnp.sum(q_r * q_r, -1, keepdims=True)) * (1.0 / QK_DIM)
                       + NORM_EPS)
        q_n = q_n * rs * consts["qn_n"]
        q_r = _rope(q_r * rs * consts["qn_r"], cos, sin, consts["perm"])
        k_n = _mm_nt(kvn, consts["wk_n"][h])
        v = _mm_nt(kvn, consts["wv"][h])
        rk = lax.rsqrt((jnp.sum(k_n * k_n, -1, keepdims=True) + jnp.sum(k_pe * k_pe, -1, keepdims=True)) * (1.0 / QK_DIM)
                       + NORM_EPS)
        k_n = k_n * rk * consts["kn_n"]
        k_r = _rope(k_pe * rk * consts["kn_r"], cos, sin, consts["perm"])
        outs += [q_n, q_r, k_n, k_r, v]
    return tuple(outs)


def _attn_fn(q, k, v, row0):
    s = _mm_nt(q, k) * (1.0 / math.sqrt(QK_DIM))
    qpos = row0 + lax.broadcasted_iota(jnp.int32, s.shape, 0)
    kpos = lax.broadcasted_iota(jnp.int32, s.shape, 1)
    s = jnp.where(kpos <= qpos, s, -1e30)
    m = lax.stop_gradient(jnp.max(s, axis=-1, keepdims=True))
    p = jnp.exp(s - m)
    p = p / jnp.sum(p, axis=-1, keepdims=True)
    return _mm(p, v)


def _dn_prep_fn(rows, consts):
    qc, kc, ab = rows
    a_b = _mmx(ab, consts["sel_a"])
    b_b = _mmx(ab, consts["sel_b"])
    beta = jax.nn.sigmoid(b_b)
    g = -jnp.exp(consts["alog"]) * _softplus(a_b + consts["dtb"])
    qs, ks = [], []
    for h in range(HEADS):
        q, k = qc[h], kc[h]
        qs.append(q * lax.rsqrt(jnp.sum(q * q, -1, keepdims=True) + NORM_EPS))
        ks.append(k * lax.rsqrt(jnp.sum(k * k, -1, keepdims=True) + NORM_EPS))
    return tuple(qs), tuple(ks), g, beta


def _dn_chunk_fn(q, k, v, gb, g64, bb):
    nb = q.shape[0]
    ri = lax.broadcasted_iota(jnp.int32, (nb, CHUNK, CHUNK), 1)
    ci = lax.broadcasted_iota(jnp.int32, (nb, CHUNK, CHUNK), 2)
    tri = ri >= ci
    strict = ri > ci
    tril = tri.astype(F32)
    eye = (ri == ci).astype(F32)
    ones = jnp.ones((nb, CHUNK, CHUNK), F32)
    gc = _chunk_cumsum(gb)
    gc64 = _chunk_cumsum(g64)
    grow = _bmmx(ones, eye * gc64)
    diff = gc64 - grow
    decay = jnp.where(tri, jnp.exp(jnp.where(tri, diff, 0.0)), 0.0)
    kb = k * bb
    vb = v * bb
    a = jnp.where(strict, _bmm_nt(kb, k) * decay, 0.0)
    tinv = _unit_lower_inv(a)
    u = _bmm(tinv, vb)
    w = _bmm(tinv, kb * jnp.exp(gc))
    qs = q * (1.0 / math.sqrt(HEAD))
    qk = _bmm_nt(qs, k) * decay
    qg = qs * jnp.exp(gc)
    glast = jnp.sum(gb, axis=1, keepdims=True)
    kdec = k * jnp.exp(glast - gc)
    n_mat = _bmm_tn(kdec, w)
    b_mat = _bmm_tn(kdec, u)
    q_eff = qg - _bmm(qk, w)
    o_own = _bmm(qk, u)
    return n_mat, b_mat, q_eff, o_own, jnp.exp(glast)


def _dn_rec_fn(s, n_mat, b_mat, eg):
    return s * eg - _mm(n_mat, s) + b_mat


def _dn_o_fn(s, q_eff, o_own):
    return _bmm(q_eff, s) + o_own


def _dn_out_fn(o, z, w):
    return _rms(o, w, HEAD) * _silu(z)


def _row_tile(t, parts=8):
    return t // parts if (t // parts) % 16 == 0 else t


def _tile(n, pref, unit):
    best = n
    for cand in range(unit, min(n, pref) + 1, unit):
        if n % cand == 0:
            best = cand
    return best if best <= pref else n


def _rows_call(name, body, rows, consts, outs, accs, r, host=None):
    rows = [a if isinstance(a, tuple) else (a, a.shape[1], 0) for a in rows]
    t = rows[0][0].shape[0]
    zero = lambda nd: (lambda i: (0,) * nd)
    in_specs = [pl.BlockSpec((r, w), functools.partial(lambda i, b: (i, b), b=blk)) for _, w, blk in rows]
    rows = [a for a, _, _ in rows]
    in_specs += [pl.BlockSpec(a.shape, zero(a.ndim)) for a in consts]
    out_shape = [jax.ShapeDtypeStruct((t, w), dt) for w, dt in outs] + [jax.ShapeDtypeStruct(s, F32) for s in accs]
    out_specs = [pl.BlockSpec((r, w), lambda i: (i, 0)) for w, _ in outs] + [pl.BlockSpec(s, zero(len(s))) for s in accs]
    return _pcall(body, name, (t // r,), in_specs, out_specs, out_shape, [*rows, *consts], ("arbitrary",), host=host)


def _accumulate(ref, val):
    @pl.when(pl.program_id(0) == 0)
    def _():
        ref[...] = jnp.zeros(ref.shape, ref.dtype)

    ref[...] += val


def _matmul(name, a, b, dims, out_dtype, res=None, host=None):
    if dims == "nn":
        (m, k), n = a.shape, b.shape[1]
    elif dims == "nt":
        (m, k), n = a.shape, b.shape[0]
    else:
        (k, m), n = a.shape, b.shape[1]
    tm = _tile(m, 1100, 16) if dims != "tn" else _tile(m, 640, 128)
    tn = _tile(n, 1408, 128)
    if dims == "nn":
        a_spec, b_spec, dn = pl.BlockSpec((tm, k), lambda i, j: (i, 0)), pl.BlockSpec((k, tn), lambda i, j: (0, j)), NN
    elif dims == "nt":
        a_spec, b_spec, dn = pl.BlockSpec((tm, k), lambda i, j: (i, 0)), pl.BlockSpec((tn, k), lambda i, j: (j, 0)), NT
    else:
        a_spec, b_spec, dn = pl.BlockSpec((k, tm), lambda i, j: (0, i)), pl.BlockSpec((k, tn), lambda i, j: (0, j)), TN
    o_spec = pl.BlockSpec((tm, tn), lambda i, j: (i, j))

    def body(*refs):
        a_ref, b_ref, o_ref = refs[0], refs[1], refs[-1]
        acc = lax.dot_general(a_ref[...].astype(BF16), b_ref[...].astype(BF16), (dn, ((), ())),
                              preferred_element_type=F32)
        if res is not None:
            acc = acc + refs[2][...]
        o_ref[...] = acc.astype(out_dtype)

    ins = [a, b] + ([res] if res is not None else [])
    specs = [a_spec, b_spec] + ([o_spec] if res is not None else [])
    return _pcall(body, name, (m // tm, n // tn), specs, o_spec, jax.ShapeDtypeStruct((m, n), out_dtype), ins,
                  ("arbitrary", "arbitrary"), host=host)


def _rms_fwd(name, h, w, host=None):
    n = h.shape[1]

    def body(h_ref, w_ref, o_ref):
        o_ref[...] = _rms(h_ref[...], w_ref[...], n).astype(BF16)

    return _rows_call(name, body, [h], [w], [(n, BF16)], [], _row_tile(h.shape[0]), host=host)[0]


def _rms_bwd(name, h, w, cts, resid, host=None):
    n = h.shape[1]
    nct = len(cts)

    def body(*refs):
        h_ref, ct_refs, r_ref, w_ref = refs[0], refs[1:1 + nct], refs[1 + nct], refs[2 + nct]
        dh_ref, dh16_ref, dw_ref = refs[-3], refs[-2], refs[-1]
        ct = ct_refs[0][...].astype(F32)
        for c in ct_refs[1:]:
            ct = ct + c[...].astype(F32)
        _, vjp = jax.vjp(lambda x, ww: _rms(x, ww, n), h_ref[...], w_ref[...])
        dh, dw = vjp(ct)
        dh = dh + r_ref[...]
        dh_ref[...] = dh
        dh16_ref[...] = dh.astype(BF16)
        _accumulate(dw_ref, dw)

    return _rows_call(name, body, [h, *cts, resid], [w], [(n, F32), (n, BF16)], [(1, n)], _row_tile(h.shape[0]), host=host)


def _mla_consts_from_refs(qa, wq, kva, wkv, qn, kn, perm):
    f = lambda r: r[...].astype(F32)
    return dict(
        qa_w=f(qa), kva_w=f(kva), perm=f(perm),
        wq_n=[wq[h * QK_PAD:h * QK_PAD + HEAD, :].astype(F32) for h in range(HEADS)],
        wq_r=[wq[h * QK_PAD + HEAD:(h + 1) * QK_PAD, :].astype(F32) for h in range(HEADS)],
        wk_n=[wkv[h * QK_PAD:h * QK_PAD + HEAD, :].astype(F32) for h in range(HEADS)],
        wv=[wkv[h * QK_PAD + HEAD:(h + 1) * QK_PAD, :].astype(F32) for h in range(HEADS)],
        qn_n=qn[:, 0:HEAD], qn_r=qn[:, HEAD:QK_PAD], kn_n=kn[:, 0:HEAD], kn_r=kn[:, HEAD:QK_PAD])


def _mla_prep_fwd(q_lat, kv_lat, k_pe, cos, sin, qa, wq, kva, wkv, qn, kn, perm):
    def body(ql, kvl, kp, c, s, qa_r, wq_r, kva_r, wkv_r, qn_r, kn_r, p_r, q_out, k_out, v_out):
        consts = _mla_consts_from_refs(qa_r, wq_r, kva_r, wkv_r, qn_r, kn_r, p_r)
        outs = _mla_prep_fn((ql[...], kvl[...], kp[...], c[...], s[...]), consts)
        for h in range(HEADS):
            q_n, q_r, k_n, k_r, v = outs[5 * h:5 * h + 5]
            q_out[:, h * QK_PAD:h * QK_PAD + HEAD] = q_n.astype(BF16)
            q_out[:, h * QK_PAD + HEAD:(h + 1) * QK_PAD] = q_r.astype(BF16)
            k_out[:, h * QK_PAD:h * QK_PAD + HEAD] = k_n.astype(BF16)
            k_out[:, h * QK_PAD + HEAD:(h + 1) * QK_PAD] = k_r.astype(BF16)
            v_out[:, h * HEAD:(h + 1) * HEAD] = v.astype(BF16)

    return _rows_call("mla_prep_fwd", body, [q_lat, kv_lat, k_pe, cos, sin], [qa, wq, kva, wkv, qn, kn, perm],
                      [(HEADS * QK_PAD, BF16), (HEADS * QK_PAD, BF16), (DN_WIDTH, BF16)], [], _row_tile(cos.shape[0], 4))


def _mla_prep_bwd(q_lat, kv_lat, k_pe, cos, sin, dq, dk, dv, qa, wq, kva, wkv, qn, kn, perm, host=None):
    def body(ql, kvl, kp, c, s, dq_r, dk_r, dv_r, qa_r, wq_r, kva_r, wkv_r, qn_r, kn_r, p_r,
             dql, dkvl, dkp, dqa, dwq, dkva, dwkv, dqn, dkn):
        consts = _mla_consts_from_refs(qa_r, wq_r, kva_r, wkv_r, qn_r, kn_r, p_r)
        cc, ss, pm = c[...], s[...], consts.pop("perm")
        _, vjp = jax.vjp(lambda rows, cs: _mla_prep_fn((*rows, cc, ss), dict(cs, perm=pm)), (ql[...], kvl[...], kp[...]),
                         consts)
        cts = []
        for h in range(HEADS):
            cts += [dq_r[:, h * QK_PAD:h * QK_PAD + HEAD], dq_r[:, h * QK_PAD + HEAD:(h + 1) * QK_PAD],
                    dk_r[:, h * QK_PAD:h * QK_PAD + HEAD], dk_r[:, h * QK_PAD + HEAD:(h + 1) * QK_PAD],
                    dv_r[:, h * HEAD:(h + 1) * HEAD]]
        (d_ql, d_kvl, d_kp), dc = vjp(tuple(cts))
        dql[...] = d_ql.astype(BF16)
        dkvl[...] = d_kvl.astype(BF16)
        dkp[...] = d_kp.astype(BF16)
        first = pl.program_id(0) == 0

        def acc(ref, sl, val):
            @pl.when(first)
            def _():
                ref[sl] = val

            @pl.when(jnp.logical_not(first))
            def _():
                ref[sl] += val

        full = (slice(None), slice(None))
        acc(dqa, full, dc["qa_w"])
        acc(dkva, full, dc["kva_w"])
        for h in range(HEADS):
            acc(dwq, (slice(h * QK_PAD, h * QK_PAD + HEAD), slice(None)), dc["wq_n"][h])
            acc(dwq, (slice(h * QK_PAD + HEAD, (h + 1) * QK_PAD), slice(None)), dc["wq_r"][h])
            acc(dwkv, (slice(h * QK_PAD, h * QK_PAD + HEAD), slice(None)), dc["wk_n"][h])
            acc(dwkv, (slice(h * QK_PAD + HEAD, (h + 1) * QK_PAD), slice(None)), dc["wv"][h])
        acc(dqn, (slice(None), slice(0, HEAD)), dc["qn_n"])
        acc(dqn, (slice(None), slice(HEAD, QK_PAD)), dc["qn_r"])
        acc(dkn, (slice(None), slice(0, HEAD)), dc["kn_n"])
        acc(dkn, (slice(None), slice(HEAD, QK_PAD)), dc["kn_r"])

    return _rows_call("mla_prep_bwd", body, [q_lat, kv_lat, k_pe, cos, sin, dq, dk, dv],
                      [qa, wq, kva, wkv, qn, kn, perm],
                      [(LORA, BF16), (LORA, BF16), (HEAD, BF16)],
                      [(1, LORA), wq.shape, (1, LORA), wkv.shape, (1, QK_PAD), (1, QK_PAD)], _row_tile(cos.shape[0], 4),
                      host=host)


ATTN_Q_ROWS = 256


def _attn_blocks(t):
    return [(r0, min(ATTN_Q_ROWS, t - r0)) for r0 in range(0, t, ATTN_Q_ROWS)]


def _attn_fwd(q, k, v, host=None):
    t = q.shape[0]

    def body(q_ref, k_ref, v_ref, o_ref):
        for r0, rows in _attn_blocks(t):
            ext = r0 + rows
            o_ref[r0:ext, :] = _attn_fn(q_ref[r0:ext, :], k_ref[0:ext, :], v_ref[0:ext, :], r0)

    qk_spec = pl.BlockSpec((t, QK_PAD), lambda h: (0, h))
    v_spec = pl.BlockSpec((t, HEAD), lambda h: (0, h))
    return _pcall(body, "attn_fwd", (HEADS,), [qk_spec, qk_spec, v_spec], v_spec,
                  jax.ShapeDtypeStruct((t, HEADS * HEAD), F32), [q, k, v], ("arbitrary",), host=host)


def _attn_bwd(q, k, v, do, host=None):
    t = q.shape[0]

    def body(q_ref, k_ref, v_ref, do_ref, dq_ref, dk_ref, dv_ref):
        dk_ref[...] = jnp.zeros(dk_ref.shape, F32)
        dv_ref[...] = jnp.zeros(dv_ref.shape, F32)
        for r0, rows in _attn_blocks(t):
            ext = r0 + rows
            _, vjp = jax.vjp(functools.partial(_attn_fn, row0=r0), q_ref[r0:ext, :].astype(F32),
                             k_ref[0:ext, :].astype(F32), v_ref[0:ext, :].astype(F32))
            dq, dk, dv = vjp(do_ref[r0:ext, :])
            dq_ref[r0:ext, :] = dq
            dk_ref[0:ext, :] += dk
            dv_ref[0:ext, :] += dv

    qk_spec = pl.BlockSpec((t, QK_PAD), lambda h: (0, h))
    v_spec = pl.BlockSpec((t, HEAD), lambda h: (0, h))
    return _pcall(body, "attn_bwd", (HEADS,), [qk_spec, qk_spec, v_spec, v_spec], [qk_spec, qk_spec, v_spec],
                  [jax.ShapeDtypeStruct((t, HEADS * QK_PAD), F32), jax.ShapeDtypeStruct((t, HEADS * QK_PAD), F32),
                   jax.ShapeDtypeStruct((t, HEADS * HEAD), F32)], [q, k, v, do], ("arbitrary",), host=host)


def _mix_out_fwd(o_mla, o_dn, z, w_mla, w_dn):
    def body(om_ref, od_ref, z_ref, wm_ref, wd_ref, o_ref):
        for h in range(HEADS):
            sl = slice(h * HEAD, (h + 1) * HEAD)
            o_ref[:, sl] = _rms(om_ref[:, sl], wm_ref[...], HEAD).astype(BF16)
            o_ref[:, DN_WIDTH + h * HEAD:DN_WIDTH + (h + 1) * HEAD] = _dn_out_fn(od_ref[:, sl], z_ref[:, sl],
                                                                                 wd_ref[...]).astype(BF16)

    return _rows_call("mix_out_fwd", body, [o_mla, o_dn, z], [w_mla, w_dn], [(2 * DN_WIDTH, BF16)], [],
                      _row_tile(o_mla.shape[0]))[0]


def _mix_out_bwd(o_mla, o_dn, z, dmixed, w_mla, w_dn, host=None):
    def body(om_ref, od_ref, z_ref, dm_ref, wm_ref, wd_ref, dom_ref, dod_ref, dz_ref, dwm_ref, dwd_ref):
        dwm = dwd = None
        for h in range(HEADS):
            sl = slice(h * HEAD, (h + 1) * HEAD)
            _, vjp = jax.vjp(lambda o, w: _rms(o, w, HEAD), om_ref[:, sl], wm_ref[...])
            do, dw = vjp(dm_ref[:, sl].astype(F32))
            dom_ref[:, sl] = do
            dwm = dw if dwm is None else dwm + dw
            _, vjp = jax.vjp(_dn_out_fn, od_ref[:, sl], z_ref[:, sl], wd_ref[...])
            do, dz, dw = vjp(dm_ref[:, DN_WIDTH + h * HEAD:DN_WIDTH + (h + 1) * HEAD].astype(F32))
            dod_ref[:, sl] = do
            dz_ref[:, sl] = dz.astype(BF16)
            dwd = dw if dwd is None else dwd + dw
        _accumulate(dwm_ref, dwm)
        _accumulate(dwd_ref, dwd)

    return _rows_call("mix_out_bwd", body, [o_mla, o_dn, z, dmixed], [w_mla, w_dn],
                      [(DN_WIDTH, F32), (DN_WIDTH, F32), (DN_WIDTH, BF16)], [(1, HEAD), (1, HEAD)],
                      _row_tile(o_mla.shape[0]), host=host)


def _shift_down(x, s):
    if s == 0:
        return x
    rows = lax.broadcasted_iota(jnp.int32, x.shape, 0)
    return jnp.where(rows >= s, pltpu.roll(x, s, 0), 0.0)


def _shift_up(x, s):
    if s == 0:
        return x
    t = x.shape[0]
    rows = lax.broadcasted_iota(jnp.int32, x.shape, 0)
    return jnp.where(rows < t - s, pltpu.roll(x, t - s, 0), 0.0)


def _col_call(name, body, cols, taps, outs, tap_outs, cw, host=None):
    t, c = cols[0].shape[0], taps[0].shape[1]
    in_specs = [pl.BlockSpec((t, cw), lambda j: (0, j)) for _ in cols]
    in_specs += [pl.BlockSpec((a.shape[0], cw), lambda j: (0, j)) for a in taps]
    out_shape = [jax.ShapeDtypeStruct((t, c), dt) for dt in outs] + [jax.ShapeDtypeStruct((n, c), F32) for n in tap_outs]
    out_specs = [pl.BlockSpec((t, cw), lambda j: (0, j)) for _ in outs]
    out_specs += [pl.BlockSpec((n, cw), lambda j: (0, j)) for n in tap_outs]
    return _pcall(body, name, (c // cw,), in_specs, out_specs, out_shape, [*cols, *taps], ("arbitrary",), host=host)


def _causal_conv(x, w_ref, width):
    acc = w_ref[width - 1:width, :] * x
    for j in range(width - 1):
        acc = acc + w_ref[j:j + 1, :] * _shift_down(x, width - 1 - j)
    return acc


def _causal_conv_bwd(x, dpre, w_ref, dx_ref, dw_ref, width):
    dx = w_ref[width - 1:width, :] * dpre
    dw_ref[width - 1:width, :] = jnp.sum(dpre * x, axis=0, keepdims=True)
    for j in range(width - 1):
        s = width - 1 - j
        dx = dx + w_ref[j:j + 1, :] * _shift_up(dpre, s)
        dw_ref[j:j + 1, :] = jnp.sum(dpre * _shift_down(x, s), axis=0, keepdims=True)
    dx_ref[...] = dx.astype(dx_ref.dtype)


def _dsilu(x):
    sg = jax.nn.sigmoid(x)
    return sg * (1.0 + x * (1.0 - sg))


def _dn_conv_fwd(x, w):
    def body(x_ref, w_ref, y_ref):
        y_ref[...] = _silu(_causal_conv(x_ref[...], w_ref, 4))

    return _col_call("dn_conv_fwd", body, [x], [w], [F32], [], 256)[0]


def _dn_conv_bwd(x, w, dy):
    def body(x_ref, dy_ref, w_ref, dx_ref, dw_ref):
        xv = x_ref[...]
        dpre = dy_ref[...] * _dsilu(_causal_conv(xv, w_ref, 4))
        _causal_conv_bwd(xv, dpre, w_ref, dx_ref, dw_ref, 4)

    return _col_call("dn_conv_bwd", body, [x, dy], [w], [BF16], [4], 256)


def _glu_fwd(gpre, up, w, b, host=None):
    def body(g_ref, u_ref, w_ref, b_ref, a_ref):
        gate = _causal_conv(g_ref[...].astype(F32), w_ref, 3) + b_ref[...]
        a_ref[...] = (_silu(gate) * u_ref[...].astype(F32)).astype(BF16)

    return _col_call("glu_fwd", body, [gpre, up], [w, b], [BF16], [], 256, host=host)[0]


def _glu_bwd(gpre, up, w, b, dact):
    def body(g_ref, u_ref, da_ref, w_ref, b_ref, dg_ref, du_ref, dw_ref, db_ref):
        gv = g_ref[...].astype(F32)
        gate = _causal_conv(gv, w_ref, 3) + b_ref[...]
        da = da_ref[...].astype(F32)
        sg = jax.nn.sigmoid(gate)
        du_ref[...] = (da * (gate * sg)).astype(BF16)
        dgate = da * u_ref[...].astype(F32) * (sg * (1.0 + gate * (1.0 - sg)))
        db_ref[...] = jnp.sum(dgate, axis=0, keepdims=True)
        _causal_conv_bwd(gv, dgate, w_ref, dg_ref, dw_ref, 3)

    return _col_call("glu_bwd", body, [gpre, up, dact], [w, b], [BF16, BF16], [3, 1], 256)


def _dn_prep_consts(sa, sb, al, dt):
    return dict(sel_a=sa[...], sel_b=sb[...], alog=al[...], dtb=dt[...])


def _dn_prep_fwd(conv, ab, sel_a, sel_b, alog, dtb):
    def body(c_ref, ab_ref, sa, sb, al, dt, q_out, k_out, g_out, b_out):
        qc = tuple(c_ref[:, h * HEAD:(h + 1) * HEAD] for h in range(HEADS))
        kc = tuple(c_ref[:, DN_WIDTH + h * HEAD:DN_WIDTH + (h + 1) * HEAD] for h in range(HEADS))
        qs, ks, g, beta = _dn_prep_fn((qc, kc, ab_ref[...]), _dn_prep_consts(sa, sb, al, dt))
        for h in range(HEADS):
            q_out[:, h * HEAD:(h + 1) * HEAD] = qs[h]
            k_out[:, h * HEAD:(h + 1) * HEAD] = ks[h]
        g_out[...] = g
        b_out[...] = beta

    return _rows_call("dn_prep_fwd", body, [conv, ab], [sel_a, sel_b, alog, dtb], [(DN_WIDTH, F32)] * 4, [],
                      _row_tile(conv.shape[0]))


def _dn_prep_bwd(conv, ab, dq, dk, dv, dg, db, sel_a, sel_b, alog, dtb):
    def body(c_ref, ab_ref, dq_r, dk_r, dv_r, dg_r, db_r, sa, sb, al, dt, dc_out, dab_out, dal_out, ddt_out):
        qc = tuple(c_ref[:, h * HEAD:(h + 1) * HEAD] for h in range(HEADS))
        kc = tuple(c_ref[:, DN_WIDTH + h * HEAD:DN_WIDTH + (h + 1) * HEAD] for h in range(HEADS))
        consts = _dn_prep_consts(sa, sb, al, dt)
        sel = dict(sel_a=consts["sel_a"], sel_b=consts["sel_b"])
        _, vjp = jax.vjp(lambda rows, ad: _dn_prep_fn(rows, {**sel, **ad}), (qc, kc, ab_ref[...]),
                         dict(alog=consts["alog"], dtb=consts["dtb"]))
        cq = tuple(dq_r[:, h * HEAD:(h + 1) * HEAD] for h in range(HEADS))
        ck = tuple(dk_r[:, h * HEAD:(h + 1) * HEAD] for h in range(HEADS))
        (dqc, dkc, dab), dad = vjp((cq, ck, dg_r[...], db_r[...]))
        for h in range(HEADS):
            dc_out[:, h * HEAD:(h + 1) * HEAD] = dqc[h]
            dc_out[:, DN_WIDTH + h * HEAD:DN_WIDTH + (h + 1) * HEAD] = dkc[h]
        dc_out[:, 2 * DN_WIDTH:3 * DN_WIDTH] = dv_r[...]
        dab_out[...] = dab.astype(BF16)
        _accumulate(dal_out, dad["alog"])
        _accumulate(ddt_out, dad["dtb"])

    return _rows_call("dn_prep_bwd", body, [conv, ab, dq, dk, dv, dg, db], [sel_a, sel_b, alog, dtb],
                      [(3 * DN_WIDTH, F32), (HEAD, BF16)], [(1, DN_WIDTH), (1, DN_WIDTH)], _row_tile(conv.shape[0]))


def _chunk_batch(t):
    nc = t // CHUNK
    return nc // 2 if nc % 2 == 0 else nc


def _dn_chunk_specs(t, nb):
    rows = nb * CHUNK
    blk = pl.BlockSpec((rows, HEAD), lambda h, b: (b, h))
    vblk = pl.BlockSpec((rows, HEAD), lambda h, b: (b, 2 * HEADS + h))
    mat = pl.BlockSpec((nb, HEAD, HEAD), lambda h, b: (b, h, 0))
    return rows, blk, vblk, mat


def _dn_chunk_fwd(qn, kn, conv, g, beta, host=None):
    t = qn.shape[0]
    nb = _chunk_batch(t)
    rows, blk, vblk, mat = _dn_chunk_specs(t, nb)

    def body(q_ref, k_ref, v_ref, g_ref, b_ref, n_o, b_o, qe_o, oo_o, eg_o):
        r3 = lambda x: x.reshape(nb, CHUNK, x.shape[-1])
        n_mat, b_mat, q_eff, o_own, eg = _dn_chunk_fn(r3(q_ref[...]), r3(k_ref[...]), r3(v_ref[...]), r3(g_ref[...]),
                                                      r3(g_ref[:, 0:CHUNK]), r3(b_ref[...]))
        n_o[...] = n_mat
        b_o[...] = b_mat
        qe_o[...] = q_eff.reshape(rows, HEAD)
        oo_o[...] = o_own.reshape(rows, HEAD)
        eg_o[...] = jnp.broadcast_to(eg, (nb, HEAD, HEAD))

    nc = t // CHUNK
    mats = jax.ShapeDtypeStruct((nc, DN_WIDTH, HEAD), F32)
    rowsd = jax.ShapeDtypeStruct((t, DN_WIDTH), F32)
    return _pcall(body, "dn_chunk_fwd", (HEADS, t // rows), [blk, blk, vblk, blk, blk], [mat, mat, blk, blk, mat],
                  [mats, mats, rowsd, rowsd, mats], [qn, kn, conv, g, beta], ("arbitrary", "arbitrary"), host=host)


def _dn_chunk_bwd(qn, kn, conv, g, beta, sall, gall, dq_eff, do, host=None):
    t = qn.shape[0]
    nb = _chunk_batch(t)
    rows, blk, vblk, mat = _dn_chunk_specs(t, nb)

    def body(q_ref, k_ref, v_ref, g_ref, b_ref, s_ref, ga_ref, dqe_ref, do_ref, dq_o, dk_o, dv_o, dg_o, db_o):
        r3 = lambda x: x.reshape(nb, CHUNK, x.shape[-1])
        _, vjp = jax.vjp(_dn_chunk_fn, r3(q_ref[...]), r3(k_ref[...]), r3(v_ref[...]), r3(g_ref[...]),
                         r3(g_ref[:, 0:CHUNK]), r3(b_ref[...]))
        s, ga = s_ref[...], ga_ref[...]
        d_n = -_bmm_nt(ga, s)
        d_eg = jnp.sum(ga * s, axis=1, keepdims=True)
        dq, dk, dv, dg, dg64, db = vjp((d_n, ga, r3(dqe_ref[...]), r3(do_ref[...]), d_eg))
        for o_ref, val in zip((dq_o, dk_o, dv_o, dg_o, db_o), (dq, dk, dv, dg, db)):
            o_ref[...] = val.reshape(rows, HEAD)
        dg_o[:, 0:CHUNK] += dg64.reshape(rows, CHUNK)

    return _pcall(body, "dn_chunk_bwd", (HEADS, t // rows), [blk, blk, vblk, blk, blk, mat, mat, blk, blk], [blk] * 5,
                  [jax.ShapeDtypeStruct((t, DN_WIDTH), F32)] * 5, [qn, kn, conv, g, beta, sall, gall, dq_eff, do],
                  ("arbitrary", "arbitrary"), host=host)


def _dn_rec_fwd(n_mat, b_mat, eg, host=None):
    nc = n_mat.shape[0]
    nb = _chunk_batch(nc * CHUNK)
    spec = pl.BlockSpec((nb, DN_WIDTH, HEAD), lambda i: (i, 0, 0))

    def body(n_ref, b_ref, eg_ref, sall_ref, s_scr):
        @pl.when(pl.program_id(0) == 0)
        def _():
            s_scr[...] = jnp.zeros(s_scr.shape, F32)

        for j in range(nb):
            sall_ref[j] = s_scr[...]
            for h in range(HEADS):
                sl = slice(h * HEAD, (h + 1) * HEAD)
                s_scr[sl, :] = _dn_rec_fn(s_scr[sl, :], n_ref[j, sl, :], b_ref[j, sl, :],
                                          eg_ref[j, h * HEAD:h * HEAD + 1, :])

    return _pcall(body, "dn_rec_fwd", (nc // nb,), [spec] * 3, spec, jax.ShapeDtypeStruct((nc, DN_WIDTH, HEAD), F32),
                  [n_mat, b_mat, eg], ("arbitrary",), scratch_shapes=[pltpu.VMEM((DN_WIDTH, HEAD), F32)], host=host)


def _dn_rec_bwd(n_mat, eg, ds_out, host=None):
    nc = n_mat.shape[0]
    nb = _chunk_batch(nc * CHUNK)
    steps = nc // nb
    spec = pl.BlockSpec((nb, DN_WIDTH, HEAD), lambda i: (steps - 1 - i, 0, 0))

    def body(n_ref, eg_ref, dso_ref, gall_ref, g_scr):
        @pl.when(pl.program_id(0) == 0)
        def _():
            g_scr[...] = jnp.zeros(g_scr.shape, F32)

        for j in reversed(range(nb)):
            gall_ref[j] = g_scr[...]
            for h in range(HEADS):
                sl = slice(h * HEAD, (h + 1) * HEAD)
                gv = g_scr[sl, :]
                g_scr[sl, :] = (gv * eg_ref[j, h * HEAD:h * HEAD + 1, :] - _mm_tn(n_ref[j, sl, :], gv)
                                + dso_ref[j, sl, :])

    return _pcall(body, "dn_rec_bwd", (steps,), [spec] * 3, spec, jax.ShapeDtypeStruct((nc, DN_WIDTH, HEAD), F32),
                  [n_mat, eg, ds_out], ("arbitrary",), scratch_shapes=[pltpu.VMEM((DN_WIDTH, HEAD), F32)], host=host)


def _dn_o_fwd(sall, q_eff, o_own):
    t = q_eff.shape[0]
    nb = _chunk_batch(t)
    rows, blk, _, mat = _dn_chunk_specs(t, nb)

    def body(s_ref, qe_ref, oo_ref, o_ref):
        r3 = lambda x: x.reshape(nb, CHUNK, HEAD)
        o_ref[...] = _dn_o_fn(s_ref[...], r3(qe_ref[...]), r3(oo_ref[...])).reshape(rows, HEAD)

    return _pcall(body, "dn_o_fwd", (HEADS, t // rows), [mat, blk, blk], blk, jax.ShapeDtypeStruct((t, DN_WIDTH), F32),
                  [sall, q_eff, o_own], ("arbitrary", "arbitrary"))


def _dn_o_bwd(sall, q_eff, do, host=None):
    t = q_eff.shape[0]
    nb = _chunk_batch(t)
    rows, blk, _, mat = _dn_chunk_specs(t, nb)

    def body(s_ref, qe_ref, do_ref, dqe_ref, ds_ref):
        r3 = lambda x: x.reshape(nb, CHUNK, HEAD)
        dov = r3(do_ref[...])
        dqe_ref[...] = _bmm_nt(dov, s_ref[...]).reshape(rows, HEAD)
        ds_ref[...] = _bmm_tn(r3(qe_ref[...]), dov)

    nc = t // CHUNK
    return _pcall(body, "dn_o_bwd", (HEADS, t // rows), [mat, blk, blk], [blk, mat],
                  [jax.ShapeDtypeStruct((t, DN_WIDTH), F32), jax.ShapeDtypeStruct((nc, DN_WIDTH, HEAD), F32)],
                  [sall, q_eff, do], ("arbitrary", "arbitrary"), host=host)


def _loss_call(h2, tgt, n_valid):
    t, n = h2.shape
    r = _row_tile(t)

    def body(h_ref, t_ref, dy_ref, dy16_ref, acc_ref):
        rows = pl.program_id(0) * r + lax.broadcasted_iota(jnp.int32, (r, n), 0)
        valid = jnp.logical_and(rows >= N_META, rows < n_valid)
        e = jnp.where(valid, h_ref[...] - t_ref[...], 0.0)
        dy = e * (1.0 / n)
        dy_ref[...] = dy
        dy16_ref[...] = dy.astype(BF16)
        _accumulate(acc_ref, jnp.sum(e * e, axis=0, keepdims=True))

    return _rows_call("loss", body, [h2, tgt], [], [(n, F32), (n, BF16)], [(1, n)], r)


def _adamw_update(w, g, m, v):
    m2 = ADAM_B1 * m + (1.0 - ADAM_B1) * g
    v2 = ADAM_B2 * v + (1.0 - ADAM_B2) * (g * g)
    m_hat = m2 / (1.0 - ADAM_B1 ** ADAM_STEP)
    v_hat = v2 / (1.0 - ADAM_B2 ** ADAM_STEP)
    return -ADAM_LR * (m_hat / (jnp.sqrt(v_hat) + ADAM_EPS) + ADAM_WD * w), m2, v2


def _adamw_small(ws, gs, ms, vs):
    n = len(ws)

    def body(*refs):
        for i in range(n):
            d, m2, v2 = _adamw_update(refs[i][...], refs[n + i][...], refs[2 * n + i][...], refs[3 * n + i][...])
            refs[4 * n + i][...] = d
            refs[5 * n + i][...] = m2
            refs[6 * n + i][...] = v2

    shapes = [jax.ShapeDtypeStruct(a.shape, F32) for a in ws]
    outs = pl.pallas_call(body, name="adamw_small", out_shape=shapes * 3,
                          compiler_params=pltpu.CompilerParams(vmem_limit_bytes=VMEM_LIMIT))(*ws, *gs, *ms, *vs)
    return outs[:n], outs[n:2 * n], outs[2 * n:]


def _adamw_call(name, w, g, m, v, host=None):
    rows, cols = w.shape
    by_rows = rows % 8 == 0

    def body(w_ref, g_ref, m_ref, v_ref, g_out, d_ref, m_out, v_out):
        gv = g_ref[...] if by_rows else g_ref[0:rows, :]
        g_out[...] = gv
        d_ref[...], m_out[...], v_out[...] = _adamw_update(w_ref[...], gv, m_ref[...], v_ref[...])

    if by_rows:
        tr = _tile(rows, 256, 8)
        spec = g_spec = pl.BlockSpec((tr, cols), lambda i: (i, 0))
        grid = (rows // tr,)
    else:
        tc = _tile(cols, 256, 128)
        spec = pl.BlockSpec((rows, tc), lambda j: (0, j))
        g_spec = pl.BlockSpec((g.shape[0], tc), lambda j: (0, j))
        grid = (cols // tc,)
    return _pcall(body, name, grid, [spec, g_spec, spec, spec], [spec] * 4, [jax.ShapeDtypeStruct((rows, cols), F32)] * 4,
                  [w, g, m, v], ("arbitrary",), host=host)


def _rope_tables(t):
    half = ROPE // 2
    inv_freq = np.float32(ROPE_THETA) ** (-np.arange(half, dtype=np.float32) / np.float32(half))
    ang = np.arange(t, dtype=np.float32)[:, None] * inv_freq[None, :].astype(np.float32)
    z = np.zeros((t, HEAD - ROPE), np.float32)
    cos = np.concatenate([np.cos(ang), np.cos(ang), z], axis=1).astype(np.float32)
    sin = np.concatenate([np.sin(ang), np.sin(ang), z], axis=1).astype(np.float32)
    k = np.arange(HEAD)[:, None]
    l = np.arange(HEAD)[None, :]
    perm = np.where((l < half) & (k == l + half), -1.0, 0.0) + np.where((l >= half) & (l < ROPE) & (k == l - half), 1.0, 0.0)
    return jnp.asarray(cos), jnp.asarray(sin), jnp.asarray(perm.astype(np.float32))


def _win_to_pad(w):
    z = lambda n: jnp.zeros((n, w.shape[1]), w.dtype)
    return jnp.concatenate([w[576:2112], w[2112:2624], w[0:256], w[256:512], w[512:576], z(64), w[2624:2632], z(120)],
                           axis=0)


def _win_from_pad(g):
    return jnp.concatenate([g[2048:2304], g[2304:2560], g[2560:2624], g[0:1536], g[1536:2048], g[2688:2696]], axis=0)


def _qk_to_pad(w):
    w4 = w.reshape(HEADS, QK_DIM, w.shape[-1])
    return jnp.concatenate([w4, jnp.zeros((HEADS, QK_PAD - QK_DIM, w.shape[-1]), w.dtype)], axis=1).reshape(
        HEADS * QK_PAD, w.shape[-1])


def _qk_from_pad(g):
    return g.reshape(HEADS, QK_PAD, g.shape[-1])[:, :QK_DIM].reshape(HEADS * QK_DIM, g.shape[-1])


def _ff_to_pad(a, axis):
    shape = list(a.shape)
    shape[axis:axis + 1] = [N_CHIPS, FF_SHARD]
    a4 = a.reshape(shape)
    shape[axis + 1] = FF_BLOCK - FF_SHARD
    out = jnp.concatenate([a4, jnp.zeros(shape, a.dtype)], axis=axis + 1)
    shape[axis:axis + 2] = [D_FF_P]
    return out.reshape(shape)


def _ff_from_pad(a, axis):
    shape = list(a.shape)
    shape[axis:axis + 1] = [N_CHIPS, FF_BLOCK]
    a4 = lax.slice_in_dim(a.reshape(shape), 0, FF_SHARD, axis=axis + 1)
    shape[axis:axis + 2] = [D_FF]
    return a4.reshape(shape)


class _LocalPlan:
    def __init__(self, wt):
        self.wt, self.grads = wt, {}

    def weight(self, name):
        return self.wt[name]

    def host(self, point):
        return None

    def grad(self, name, value):
        self.grads[name] = value

    def early_small(self, grads):
        pass


def _local_step(x, tgt, wt, plan=None):
    plan = _LocalPlan(wt) if plan is None else plan
    s = x.shape[0]
    n_valid = N_META + s
    t = -(-n_valid // HEAD) * HEAD
    zpad = jnp.zeros((t - n_valid, D_MODEL), F32)
    h0 = jnp.concatenate([wt["meta_tokens"], x, zpad], axis=0)
    tgt_p = jnp.concatenate([jnp.zeros((N_META, D_MODEL), F32), tgt, zpad], axis=0)
    cos, sin, perm = _rope_tables(t)
    qn_w = jnp.concatenate([wt["q_norm_w"], jnp.zeros((1, QK_PAD - QK_DIM), F32)], axis=1)
    kn_w = jnp.concatenate([wt["k_norm_w"], jnp.zeros((1, QK_PAD - QK_DIM), F32)], axis=1)
    head_id = jnp.arange(DN_WIDTH)[None, :] // HEAD
    lane = jnp.arange(HEAD)[:, None]
    sel_a = (lane == head_id).astype(F32)
    sel_b = (lane == head_id + HEADS).astype(F32)
    alog = jnp.repeat(wt["dn_A_log"], HEAD, axis=1)
    dtb = jnp.repeat(wt["dn_dt_bias"], HEAD, axis=1)
    conv_w, conv_b = wt["ffn_conv_w"], wt["ffn_conv_b"]

    u = _rms_fwd("attn_norm_fwd", h0, wt["attn_norm_w"], host=plan.host("attn_norm_fwd"))
    win, wq, wkv = plan.weight("w_in_t"), plan.weight("w_q_t"), plan.weight("w_kv_t")
    proj = _matmul("in_proj", u, win, "nt", F32)
    z = (proj, DN_WIDTH, 3)
    q_lat, kv_lat, k_pe, ab = (proj, LORA, 8), (proj, LORA, 9), (proj, HEAD, 20), (proj, HEAD, 21)
    mla_consts = (wt["q_a_norm_w"], wq, wt["kv_a_norm_w"], wkv, qn_w, kn_w, perm)
    q, k, v = _mla_prep_fwd(q_lat, kv_lat, k_pe, cos, sin, *mla_consts)
    o_mla = _attn_fwd(q, k, v, host=plan.host("attn_fwd"))
    conv = _dn_conv_fwd(proj, wt["dn_conv_w"])
    dn_consts = (sel_a, sel_b, alog, dtb)
    qn, kn, g, beta = _dn_prep_fwd(conv, ab, *dn_consts)
    n_mat, b_mat, q_eff, o_own, eg = _dn_chunk_fwd(qn, kn, conv, g, beta, host=plan.host("dn_chunk_fwd"))
    sall = _dn_rec_fwd(n_mat, b_mat, eg)
    o_dn = _dn_o_fwd(sall, q_eff, o_own)
    mixed = _mix_out_fwd(o_mla, o_dn, z, wt["mla_out_norm_w"], wt["dn_out_norm_w"])
    w_out = plan.weight("w_out")
    h1 = _matmul("out_proj", mixed, w_out, "nn", F32, res=h0)
    n2 = _rms_fwd("ffn_norm_fwd", h1, wt["ffn_norm_w"])
    w_gate, w_up = plan.weight("w_gate_t"), plan.weight("w_up_t")
    gpre = _matmul("gate_proj", n2, w_gate, "nt", BF16, host=plan.host("gate_proj"))
    up = _matmul("up_proj", n2, w_up, "nt", BF16, host=plan.host("up_proj"))
    act = _glu_fwd(gpre, up, conv_w, conv_b)
    w_down = plan.weight("w_down")
    h2 = _matmul("down_proj", act, w_down, "nn", F32, res=h1)
    dy, dy16, sq = _loss_call(h2, tgt_p, n_valid)

    grads = {}
    dact = _matmul("down_dx", dy16, w_down, "nt", BF16)
    plan.grad("w_down", _matmul("down_dw", act, dy16, "tn", BF16))
    dgpre, dup, grads["ffn_conv_w"], grads["ffn_conv_b"] = _glu_bwd(gpre, up, conv_w, conv_b, dact)
    plan.grad("w_gate_t", _matmul("gate_dw", dgpre, n2, "tn", BF16))
    plan.grad("w_up_t", _matmul("up_dw", dup, n2, "tn", BF16))
    dn2a = _matmul("gate_dx", dgpre, w_gate, "nn", BF16, host=plan.host("gate_dx"))
    dn2b = _matmul("up_dx", dup, w_up, "nn", BF16, host=plan.host("up_dx"))
    dh1, dh1_16, grads["ffn_norm_w"] = _rms_bwd("ffn_norm_bwd", h1, wt["ffn_norm_w"], [dn2a, dn2b], dy)
    dmixed = _matmul("out_dx", dh1_16, w_out, "nt", BF16)
    plan.grad("w_out", _matmul("out_dw", mixed, dh1_16, "tn", BF16))
    do_mla, do_dn, dz, grads["mla_out_norm_w"], grads["dn_out_norm_w"] = _mix_out_bwd(
        o_mla, o_dn, z, dmixed, wt["mla_out_norm_w"], wt["dn_out_norm_w"], host=plan.host("mix_out_bwd"))
    dq_eff, ds_out = _dn_o_bwd(sall, q_eff, do_dn)
    gall = _dn_rec_bwd(n_mat, eg, ds_out)
    dqn, dkn, dv_dn, dg, dbeta = _dn_chunk_bwd(qn, kn, conv, g, beta, sall, gall, dq_eff, do_dn,
                                               host=plan.host("dn_chunk_bwd"))
    dconv, dab, dalog, ddtb = _dn_prep_bwd(conv, ab, dqn, dkn, dv_dn, dg, dbeta, *dn_consts)
    grads["dn_A_log"] = jnp.sum(dalog.reshape(HEADS, HEAD), axis=1)[None, :]
    grads["dn_dt_bias"] = jnp.sum(ddtb.reshape(HEADS, HEAD), axis=1)[None, :]
    ddn_pre, grads["dn_conv_w"] = _dn_conv_bwd(proj, wt["dn_conv_w"], dconv)
    dq, dk, dv = _attn_bwd(q, k, v, do_mla, host=plan.host("attn_bwd"))
    dq_lat, dkv_lat, dk_pe, dqa, dwq, dkva, dwkv, dqnw, dknw = _mla_prep_bwd(
        q_lat, kv_lat, k_pe, cos, sin, dq, dk, dv, *mla_consts, host=plan.host("mla_prep_bwd"))
    grads["q_a_norm_w"], grads["kv_a_norm_w"] = dqa, dkva
    plan.grad("w_q_t", dwq)
    plan.grad("w_kv_t", dwkv)
    grads["q_norm_w"], grads["k_norm_w"] = dqnw[:, :QK_DIM], dknw[:, :QK_DIM]
    dproj = jnp.concatenate([ddn_pre, dz, dq_lat, dkv_lat, dk_pe, dab], axis=1)
    plan.early_small(grads)
    plan.grad("w_in_t", _matmul("in_dw", dproj, u, "tn", F32, host=plan.host("in_dw")))
    du = _matmul("in_dx", dproj, win, "nn", BF16, host=plan.host("in_dx"))
    dh0, _, grads["attn_norm_w"] = _rms_bwd("attn_norm_bwd", h0, wt["attn_norm_w"], [du], dh1,
                                            host=plan.host("attn_norm_bwd"))
    grads["meta_tokens"] = dh0[0:N_META]
    if isinstance(plan, _LocalPlan):
        grads.update(plan.grads)
    return sq, dh0[N_META:n_valid], grads


def _mesh_pos():
    return lax.axis_index("x"), lax.axis_index("y"), lax.axis_index("c")


def _other_chips(x, y):
    return [(1 - x, y), (x, 1 - y), (1 - x, 1 - y)]


def _remote(src, dst, send_sems, recv_sems, k, to):
    return pltpu.make_async_remote_copy(src_ref=src, dst_ref=dst, send_sem=send_sems.at[k], recv_sem=recv_sems.at[k],
                                        device_id=to, device_id_type=MESH)


SIBLING_ID, CHIPS_ID, GATHER_ID, ALL_ID = 1, 2, 3, 4


def _sibling_peer():
    x, y, c = _mesh_pos()
    return [(x, y, 1 - c)]


def _chip_peers():
    x, y, c = _mesh_pos()
    return [(qx, qy, c) for qx, qy in _other_chips(x, y)]


def _copies_exchange(make, ins, out_shape, nsem, peers=None, cid=None):
    def prog(in_refs, out_refs, send_sems, recv_sems):
        copies = make(in_refs, out_refs, send_sems, recv_sems)

        def start():
            for cp in copies:
                cp.start()

        def finish():
            for cp in copies:
                cp.wait()

        return start, finish

    return _Exchange(prog, ins, out_shape, nsem, peers, cid)


def _all_gather(shards):
    def prog(srcs, dsts, send_sems, recv_sems):
        x, y, c = _mesh_pos()
        p = 2 * x + y
        sibling = (x, y, 1 - c)
        chips = _other_chips(x, y)
        bufs = tuple((s, d, s.shape[0] // 2) for s, d in zip(srcs, dsts))

        def half(ref, rows, which):
            return ref.at[pl.ds(which * rows, rows), :]

        def copy(i, k, src, dst, to):
            return _remote(src, dst, send_sems, recv_sems, 6 * i + k, to)

        sends = [copy(i, j, half(src, rows, c), half(dst.at[p], rows, c), (*chip, c))
                 for i, (src, dst, rows) in enumerate(bufs) for j, chip in enumerate(chips)]

        def start():
            for cp in sends:
                cp.start()

        def finish():
            passed = []
            for i, (src, dst, rows) in enumerate(bufs):
                for j, (qx, qy) in enumerate(chips):
                    block = half(dst.at[2 * qx + qy], rows, c)
                    copy(i, j, block, block, (x, y, c)).wait_recv()
                    fwd = copy(i, 3 + j, block, block, sibling)
                    fwd.start()
                    passed.append(fwd)
            for i, (src, dst, rows) in enumerate(bufs):
                for j, (qx, qy) in enumerate(chips):
                    block = half(dst.at[2 * qx + qy], rows, 1 - c)
                    copy(i, 3 + j, block, block, (x, y, c)).wait_recv()
            for cp in sends + passed:
                cp.wait_send()

        return start, finish

    return _Exchange(prog, shards, [jax.ShapeDtypeStruct((N_CHIPS, *s.shape), s.dtype) for s in shards], 6 * len(shards),
                     lambda: _sibling_peer() + _chip_peers(), GATHER_ID)


def _gathered(ex):
    p = 2 * lax.axis_index("x") + lax.axis_index("y")
    return [lax.dynamic_update_slice(g, s[None], (p, 0, 0)) for g, s in zip(ex.outs, ex.ins)]


def _rs_to_sibling(bufs):
    def make(srcs, dsts, send_sems, recv_sems):
        x, y, c = _mesh_pos()
        copies = []
        for i, (src, dst) in enumerate(zip(srcs, dsts)):
            half = src.shape[1] // 2
            copies.append(_remote(src.at[:, pl.ds((1 - c) * half, half), :], dst, send_sems, recv_sems, i, (x, y, 1 - c)))
        return copies

    return _copies_exchange(make, bufs,
                            [jax.ShapeDtypeStruct((N_CHIPS, b.shape[1] // 2, b.shape[2]), b.dtype) for b in bufs],
                            len(bufs), _sibling_peer, SIBLING_ID)


def _rs_pair_add(name, bufs, gots, c, out_dtype):
    n = len(bufs)

    def body(c_ref, *refs):
        for a_ref, b_ref, o_ref in zip(refs[:n], refs[n:2 * n], refs[2 * n:]):
            o_ref[...] = (a_ref[...].astype(F32) + b_ref[...].astype(F32)).astype(out_dtype)

    mine = [pl.BlockSpec((None, g.shape[1], g.shape[2]), lambda j, cr: (j, cr[0], 0)) for g in gots]
    whole = [pl.BlockSpec((None, g.shape[1], g.shape[2]), lambda j, cr: (j, 0, 0)) for g in gots]
    return pl.pallas_call(
        body, name=name,
        grid_spec=pltpu.PrefetchScalarGridSpec(num_scalar_prefetch=1, grid=(N_CHIPS,), in_specs=mine + whole, out_specs=whole),
        out_shape=[jax.ShapeDtypeStruct(g.shape, out_dtype) for g in gots],
        compiler_params=_cparams(("arbitrary",)))(c, *bufs, *gots)


def _rs_to_chips(accs):
    def make(srcs, dsts, send_sems, recv_sems):
        x, y, c = _mesh_pos()
        return [_remote(src.at[2 * qx + qy], dst.at[k], send_sems, recv_sems, 3 * i + k, (qx, qy, c))
                for i, (src, dst) in enumerate(zip(srcs, dsts)) for k, (qx, qy) in enumerate(_other_chips(x, y))]

    return _copies_exchange(make, accs, [jax.ShapeDtypeStruct((3, a.shape[1], a.shape[2]), a.dtype) for a in accs],
                            3 * len(accs), _chip_peers, CHIPS_ID)


def _rs_chip_add(name, accs, gots, p):
    n = len(accs)
    slot = (0, 1, 0, 2)

    def body(p_ref, *refs):
        me = p_ref[0]
        for own_ref, got_ref, o_ref in zip(refs[:n], refs[n:2 * n], refs[2 * n:]):
            total = None
            for chip in range(N_CHIPS):
                val = own_ref[...].astype(F32)
                for e in (1, 2, 3):
                    val = jnp.where((chip ^ me) == e, got_ref[slot[e]].astype(F32), val)
                total = val if total is None else total + val
            o_ref[...] = total

    own = [pl.BlockSpec((None, a.shape[1], a.shape[2]), lambda i, pr: (pr[0], 0, 0)) for a in accs]
    got = [pl.BlockSpec(g.shape, lambda i, pr: (0, 0, 0)) for g in gots]
    out = [pl.BlockSpec((a.shape[1], a.shape[2]), lambda i, pr: (0, 0)) for a in accs]
    return pl.pallas_call(
        body, name=name,
        grid_spec=pltpu.PrefetchScalarGridSpec(num_scalar_prefetch=1, grid=(1,), in_specs=own + got, out_specs=out),
        out_shape=[jax.ShapeDtypeStruct((a.shape[1], a.shape[2]), F32) for a in accs],
        compiler_params=_cparams(("arbitrary",)))(p, *accs, *gots)


def _rs_share(ress):
    def make(srcs, dsts, send_sems, recv_sems):
        x, y, c = _mesh_pos()
        return [_remote(src, dst, send_sems, recv_sems, i, (x, y, 1 - c)) for i, (src, dst) in enumerate(zip(srcs, dsts))]

    return _copies_exchange(make, ress, [jax.ShapeDtypeStruct(r.shape, F32) for r in ress], len(ress), _sibling_peer,
                            SIBLING_ID)


def _shared(ex):
    south = lax.axis_index("c") == 0
    return [jnp.concatenate([jnp.where(south, r, g), jnp.where(south, g, r)], axis=0) for r, g in zip(ex.ins, ex.outs)]


def _all_to_all_devices(vec):
    def others():
        x, y, c = _mesh_pos()
        return [((1 - x if r & 4 else x), (1 - y if r & 2 else y), (1 - c if r & 1 else c)) for r in range(1, 8)]

    def make(srcs, dsts, send_sems, recv_sems):
        x, y, c = _mesh_pos()
        me = 4 * x + 2 * y + c
        return [_remote(srcs[0], dsts[0].at[me], send_sems, recv_sems, r, peer) for r, peer in enumerate(others())]

    return _copies_exchange(make, [vec], [jax.ShapeDtypeStruct((8, *vec.shape), vec.dtype)], 7, others, ALL_ID)


def _sum_devices(stack):
    def body(s_ref, o_ref):
        total = s_ref[0]
        for d in range(1, 8):
            total = total + s_ref[d]
        o_ref[...] = total

    return pl.pallas_call(body, name="sum_devices", out_shape=jax.ShapeDtypeStruct(stack.shape[1:], F32),
                          compiler_params=pltpu.CompilerParams(vmem_limit_bytes=VMEM_LIMIT))(stack)


def _pad_rows(flat, rows):
    return jnp.concatenate([flat, jnp.zeros((rows * LANES - flat.shape[0],), flat.dtype)]).reshape(rows, LANES)


def _unshard(g4, shape, axis):
    a = g4.reshape(N_CHIPS, *shape)
    if axis == 0:
        return a.reshape(N_CHIPS * shape[0], shape[1])
    return jnp.transpose(a, (1, 0, 2)).reshape(shape[0], N_CHIPS * shape[1])


def _shard4(full, shape, axis):
    if axis == 0:
        return full.reshape(N_CHIPS, shape[0] * shape[1])
    a = full.reshape(shape[0], N_CHIPS, shape[1])
    return jnp.transpose(a, (1, 0, 2)).reshape(N_CHIPS, shape[0] * shape[1])


def _pad_axis0(a, rows):
    return jnp.concatenate([a, jnp.zeros((rows - a.shape[0], *a.shape[1:]), a.dtype)], axis=0)


def _pad_axis1(a, rows):
    return jnp.concatenate([a, jnp.zeros((a.shape[0], rows - a.shape[1], *a.shape[2:]), a.dtype)], axis=1)


def _shard_to_strip(name, w):
    _, (shape, axis, rows) = name, {n: (s, ax, r) for n, s, ax, r in BIG}[name]
    w2 = w.reshape(shape).astype(BF16)
    if name == "w_in":
        return w2
    return _pad_axis0(w2.T if axis == 1 else w2, rows)


LOCAL_NAME = dict(w_in="w_in_t", w_q_b="w_q_t", w_kv_b="w_kv_t", w_out="w_out", w_gate="w_gate_t", w_up="w_up_t",
                  w_down="w_down")


WIN_SEGMENTS = ((576, 2112, 0), (2112, 2624, 1536), (0, 256, 2048), (256, 512, 2304), (512, 576, 2560), (2624, 2632, 2688))


def _strips_to_weight(name, g4):
    if name == "w_in":
        return _win_to_pad(jnp.transpose(g4, (0, 2, 1)).reshape(IN_COLS, D_MODEL))
    if name == "w_q_b":
        return _qk_to_pad(g4.reshape(HEADS * QK_DIM, LORA))
    return g4.reshape(N_CHIPS * g4.shape[1], g4.shape[2])


def _grad_to_strips(name, g):
    if name == "w_in":
        strips = []
        for q in range(N_CHIPS):
            pieces = []
            for a, b, local in sorted(WIN_SEGMENTS):
                s, e = max(a, q * IN_SHARD), min(b, (q + 1) * IN_SHARD)
                if s < e:
                    pieces.append(g[local + s - a:local + e - a])
            pieces.append(jnp.zeros((IN_SHARD_P - IN_SHARD, D_MODEL), g.dtype))
            strips.append(jnp.concatenate(pieces, axis=0))
        return jnp.stack(strips)
    if name == "w_q_b":
        return _qk_from_pad(g).reshape(N_CHIPS, QK_DIM, LORA)
    return g.reshape(N_CHIPS, g.shape[0] // N_CHIPS, g.shape[1])


class _MeshPlan:
    LATE = dict(attn_norm_fwd=("w_in", "w_q_b", "w_kv_b"), attn_fwd=("w_up",), dn_chunk_fwd=("w_out", "w_gate"),
                gate_proj=("w_down/0",), up_proj=("w_down/1",))
    GROUP_A = ("w_down", "w_gate", "w_up", "w_out")
    GROUP_B = ("w_in", "w_q_b", "w_kv_b")
    LATE_SMALL = ("meta_tokens", "attn_norm_w")

    def __init__(self, w):
        x, y, c = _mesh_pos()
        self.ci = jnp.reshape(c, (1,)).astype(jnp.int32)
        self.pi = jnp.reshape(2 * x + y, (1,)).astype(jnp.int32)
        self.strip = {n: _shard_to_strip(n, w[n]) for n, _, _, _ in BIG}
        self.gathers, self.weights, self.g, self.acc, self.reduced = {}, {}, {}, {}, {}
        self.sibs, self.sib, self.chip, self.share, self.halves = [], None, None, None, [None, None]

    def gather_small(self, small):
        ex = _all_gather([small])
        ex.run("all_gather_small")
        return _gathered(ex)[0]

    def weight(self, local_name):
        if local_name not in self.weights:
            for point, (names, ex) in list(self.gathers.items()):
                if ex.outs is not None:
                    for n, g4 in zip(names, _gathered(ex)):
                        if "/" in n:
                            n, half = n.split("/")
                            self.halves[int(half)] = g4
                            if None in self.halves:
                                continue
                            g4 = jnp.concatenate(self.halves, axis=1)
                        self.weights[LOCAL_NAME[n]] = _strips_to_weight(n, g4)
                    del self.gathers[point]
        return self.weights[local_name]

    def _shard(self, name):
        if "/" not in name:
            return self.strip[name]
        name, half = name.split("/")
        rows = self.strip[name].shape[0] // 2
        return self.strip[name][int(half) * rows:(int(half) + 1) * rows]

    def grad(self, local_name, value):
        name = {v: k for k, v in LOCAL_NAME.items()}[local_name]
        self.g[name] = _grad_to_strips(name, value)

    def early_small(self, grads):
        names = [n for n, _, _ in SMALL_SHARDED] + [n for n, _ in REPLICATED]
        unpad = lambda n: _ff_from_pad(grads[n], 1) if n in ("ffn_conv_w", "ffn_conv_b") else grads[n]
        vec = jnp.concatenate([unpad(n).reshape(-1) for n in names if n not in self.LATE_SMALL])
        vec = _pad_rows(vec, -(-vec.shape[0] // (8 * LANES)) * 8)
        self.small_exchange = (_all_to_all_devices(vec), vec)

    def _pair_add(self, names, gots):
        accs = _rs_pair_add("rs_pair_add_" + names[0], [self.g[n] for n in names], gots, self.ci, BF16)
        self.acc.update(zip(names, accs))

    def _chip_add(self, names, chip):
        return _rs_chip_add("rs_chip_add_" + names[0], [self.acc[n] for n in names], chip.outs, self.pi)

    def _take_shared(self, names, share):
        for n, strip in zip(names, _shared(share)):
            self.reduced[n] = strip

    def host(self, point):
        a, b = self.GROUP_A, self.GROUP_B
        if point in self.LATE:
            names = self.LATE[point]
            ex = _all_gather([self._shard(n) for n in names])
            self.gathers[point] = (names, ex)
            return ex
        if point in ("gate_dx", "up_dx", "mix_out_bwd"):
            names = dict(gate_dx=a[:2], up_dx=a[2:3], mix_out_bwd=a[3:])[point]
            ex = _rs_to_sibling([self.g[n] for n in names])
            self.sibs.append(ex)
            return ex
        if point == "dn_chunk_bwd":
            self._pair_add(a, [o for ex in self.sibs for o in ex.outs])
            self.chip1 = _rs_to_chips([self.acc[n] for n in a[:2]])
            return self.chip1
        if point == "attn_bwd":
            self.chip2 = _rs_to_chips([self.acc[n] for n in a[2:]])
            return self.chip2
        if point == "mla_prep_bwd":
            ress = self._chip_add(a[:2], self.chip1) + self._chip_add(a[2:], self.chip2)
            self.share = _rs_share(ress)
            return self.share
        if point == "in_dw":
            return self.small_exchange[0]
        if point == "in_dx":
            self._take_shared(a, self.share)
            self.sib = _rs_to_sibling([self.g[n] for n in b])
            return self.sib
        if point == "attn_norm_bwd":
            self._pair_add(b, self.sib.outs)
            self.chip = _rs_to_chips([self.acc[n] for n in b])
            return self.chip
        return None

    def last_share(self):
        self.share = _rs_share(self._chip_add(self.GROUP_B, self.chip))
        return self.share

    def finish(self):
        self._take_shared(self.GROUP_B, self.share)
        return self.reduced


def _strip_to_shard(name, strip):
    shape, axis = {n: (s, ax) for n, s, ax, _ in BIG}[name]
    rows = shape[axis]
    return strip[:rows].T if axis == 1 else strip[:rows]


def kernel(x, meta_tokens, attn_norm_w, w_in, q_a_norm_w, w_q_b, kv_a_norm_w, w_kv_b, q_norm_w, k_norm_w, mla_out_norm_w, dn_conv_w, dn_A_log, dn_dt_bias, dn_out_norm_w, w_out, ffn_norm_w, w_gate, w_up, ffn_conv_w, ffn_conv_b, w_down, loss_target, m_meta_tokens, m_attn_norm_w, m_w_in, m_q_a_norm_w, m_w_q_b, m_kv_a_norm_w, m_w_kv_b, m_q_norm_w, m_k_norm_w, m_mla_out_norm_w, m_dn_conv_w, m_dn_A_log, m_dn_dt_bias, m_dn_out_norm_w, m_w_out, m_ffn_norm_w, m_w_gate, m_w_up, m_ffn_conv_w, m_ffn_conv_b, m_w_down, v_meta_tokens, v_attn_norm_w, v_w_in, v_q_a_norm_w, v_w_q_b, v_kv_a_norm_w, v_w_kv_b, v_q_norm_w, v_k_norm_w, v_mla_out_norm_w, v_dn_conv_w, v_dn_A_log, v_dn_dt_bias, v_dn_out_norm_w, v_w_out, v_ffn_norm_w, v_w_gate, v_w_up, v_ffn_conv_w, v_ffn_conv_b, v_w_down):
    local = dict(locals())
    w = {n: local[n] for n in WEIGHTS}
    m = {n: local["m_" + n] for n in WEIGHTS}
    v = {n: local["v_" + n] for n in WEIGHTS}
    p = 2 * lax.axis_index("x") + lax.axis_index("y")

    plan = _MeshPlan(w)
    wf = _pad_rows(jnp.concatenate([w[n].reshape(-1) for n, _, _ in SMALL_SHARDED]), SMALL_ROWS)
    gf = plan.gather_small(wf).reshape(N_CHIPS, -1)
    full = {}
    off = 0
    for n, s, ax in SMALL_SHARDED:
        full[n] = _unshard(gf[:, off:off + s[0] * s[1]], s, ax)
        off += s[0] * s[1]
    for n, _ in REPLICATED:
        full[n] = w[n]
    full["ffn_conv_w"] = _ff_to_pad(full["ffn_conv_w"], 1)
    full["ffn_conv_b"] = _ff_to_pad(full["ffn_conv_b"], 1)

    sq, grad_x, g = _local_step(x[0], loss_target[0], full, plan)
    g["ffn_conv_w"] = _ff_from_pad(g["ffn_conv_w"], 1)
    g["ffn_conv_b"] = _ff_from_pad(g["ffn_conv_b"], 1)

    small_all = [n for n, _, _ in SMALL_SHARDED] + [n for n, _ in REPLICATED]
    late = jnp.concatenate([g[n].reshape(-1) for n in plan.LATE_SMALL] + [jnp.reshape(0.5 / D_MODEL * jnp.sum(sq), (1,))])
    late = _pad_rows(late, -(-late.shape[0] // (8 * LANES)) * 8)
    a2a = _all_to_all_devices(late)
    a2a.run("late_small_grads_exchange")
    share = plan.last_share()
    share.run("rs_share_last")

    gs, delta, new_m, new_v = {}, {}, {}, {}
    big = {n: (s, ax) for n, s, ax, _ in BIG}

    def adamw_big(n, strips, host=None):
        s, ax = big[n]
        flip = ax == 1 and s[1] % 8 == 0
        there = (lambda a: a.reshape(s).T) if flip else (lambda a: a.reshape(s))
        back = (lambda a: a.T.reshape(w[n].shape)) if flip else (lambda a: a.reshape(w[n].shape))
        strip = strips[n] if flip or ax == 0 else strips[n][:s[1]].T
        g2, d2, m2, v2 = _adamw_call("adamw_" + n, there(w[n]), strip, there(m[n]), there(v[n]), host=host)
        gs[n], delta[n], new_m[n], new_v[n] = back(g2), back(d2), back(m2), back(v2)

    strips = plan.finish()
    for n in plan.GROUP_A + plan.GROUP_B:
        adamw_big(n, strips)
    me = 4 * lax.axis_index("x") + 2 * lax.axis_index("y") + lax.axis_index("c")
    early, early_vec = plan.small_exchange
    stack = jnp.concatenate([lax.dynamic_update_slice(early.outs[0], early_vec[None], (me, 0, 0)),
                             lax.dynamic_update_slice(a2a.outs[0], late[None], (me, 0, 0))], axis=1)
    red = _sum_devices(stack).reshape(-1)
    offs, off = {}, 0
    for n in [n for n in small_all if n not in plan.LATE_SMALL]:
        offs[n] = off
        off += g[n].size
    off = early_vec.size
    for n in plan.LATE_SMALL:
        offs[n] = off
        off += g[n].size
    loss = red[off]
    for n in small_all:
        tot = red[offs[n]:offs[n] + g[n].size].reshape(g[n].shape)
        shard = {sn: (s, ax) for sn, s, ax in SMALL_SHARDED}.get(n)
        if shard is not None:
            tot = lax.dynamic_slice_in_dim(tot, p * shard[0][1], shard[0][1], axis=1)
        gs[n] = tot
    two_d = lambda a: a.reshape(a.shape[-2], a.shape[-1])
    outs = _adamw_small([two_d(w[n]) for n in small_all], [two_d(gs[n]) for n in small_all],
                        [two_d(m[n]) for n in small_all], [two_d(v[n]) for n in small_all])
    for i, n in enumerate(small_all):
        for dst, src in ((delta, outs[0]), (new_m, outs[1]), (new_v, outs[2])):
            dst[n] = src[i].reshape(w[n].shape)

    grad_out = [gs[n].reshape(w[n].shape) for n in WEIGHTS]
    return (loss, grad_x[None], *grad_out, *[delta[n] for n in WEIGHTS], *[new_m[n] for n in WEIGHTS],
            *[new_v[n] for n in WEIGHTS])
```

```python
import functools
import math

import jax
import jax.numpy as jnp
import numpy as np
from jax import lax
from jax.experimental import pallas as pl
from jax.experimental.pallas import tpu as pltpu

F32 = jnp.float32
BF16 = jnp.bfloat16
HI = lax.Precision.HIGHEST
MESH = pl.DeviceIdType.MESH

N_META = 16
D_MODEL = 1024
HEADS = 4
HEAD = 128
ROPE = 64
QK_DIM = HEAD + ROPE
QK_PAD = 2 * HEAD
LORA = 256
DN_WIDTH = HEADS * HEAD
CHUNK = 64
D_FF = 2816
N_CHIPS = 4
FF_SHARD = D_FF // N_CHIPS
FF_BLOCK = 768
D_FF_P = N_CHIPS * FF_BLOCK
IN_COLS = 2632
IN_SHARD = IN_COLS // N_CHIPS
IN_SHARD_P = 672
IN_PAD = 2816
NORM_EPS = 1e-6
ROPE_THETA = 10000.0
LANES = 512

ADAM_LR, ADAM_B1, ADAM_B2, ADAM_EPS, ADAM_WD, ADAM_STEP = 0.001, 0.9, 0.999, 1e-08, 0.01, 10

VMEM_LIMIT = 56 * 1024 * 1024
MATMUL_VMEM_BUDGET = 40 * 1024 * 1024

BIG = (("w_in", (1024, 658), 1, IN_SHARD_P), ("w_q_b", (256, 192), 1, 192), ("w_kv_b", (256, 256), 1, 256),
       ("w_out", (256, 1024), 0, 256), ("w_gate", (1024, 704), 1, FF_BLOCK), ("w_up", (1024, 704), 1, FF_BLOCK),
       ("w_down", (704, 1024), 0, FF_BLOCK))
SMALL_SHARDED = (("meta_tokens", (16, 256), 1), ("dn_conv_w", (4, 384), 1), ("ffn_conv_w", (3, 704), 1))
REPLICATED = (("attn_norm_w", 1024), ("q_a_norm_w", 256), ("kv_a_norm_w", 256), ("q_norm_w", 192), ("k_norm_w", 192),
              ("mla_out_norm_w", 128), ("dn_A_log", 4), ("dn_dt_bias", 4), ("dn_out_norm_w", 128), ("ffn_norm_w", 1024),
              ("ffn_conv_b", 2816))
WEIGHTS = ("meta_tokens", "attn_norm_w", "w_in", "q_a_norm_w", "w_q_b", "kv_a_norm_w", "w_kv_b", "q_norm_w", "k_norm_w",
           "mla_out_norm_w", "dn_conv_w", "dn_A_log", "dn_dt_bias", "dn_out_norm_w", "w_out", "ffn_norm_w", "w_gate",
           "w_up", "ffn_conv_w", "ffn_conv_b", "w_down")

SMALL_ROWS = 16
REP_ROWS = 16


def _cparams(sem):
    return pltpu.CompilerParams(dimension_semantics=sem, vmem_limit_bytes=VMEM_LIMIT)


class _Exchange:
    def __init__(self, prog, ins, out_shape, nsem, peers=None, cid=None):
        self.prog, self.ins, self.out_shape, self.nsem = prog, list(ins), list(out_shape), nsem
        self.peers, self.cid = peers, cid
        self.outs = None

    def sems(self):
        return [pltpu.SemaphoreType.DMA((self.nsem,)), pltpu.SemaphoreType.DMA((self.nsem,))]

    def programs(self, in_refs, out_refs, send_sems, recv_sems):
        start, finish = self.prog(in_refs, out_refs, send_sems, recv_sems)
        if self.cid is None:
            return start, finish
        peers = self.peers()

        def shake_and_start():
            barrier = pltpu.get_barrier_semaphore()
            for peer in peers:
                pl.semaphore_signal(barrier, inc=1, device_id=peer, device_id_type=MESH)
            pl.semaphore_wait(barrier, len(peers))
            start()

        return shake_and_start, finish

    def cparams(self, **kw):
        return pltpu.CompilerParams(has_side_effects=True, collective_id=self.cid, **kw)

    def run(self, name):
        any_spec = pl.BlockSpec(memory_space=pl.ANY)
        n = len(self.ins)

        def body(*refs):
            start, finish = self.programs(refs[:n], refs[n:-2], refs[-2], refs[-1])
            start()
            finish()

        self.outs = pl.pallas_call(
            body, name=name, in_specs=[any_spec] * n, out_specs=[any_spec] * len(self.out_shape),
            out_shape=self.out_shape, scratch_shapes=self.sems(), compiler_params=self.cparams())(*self.ins)
        return self.outs


def _pcall(body, name, grid, in_specs, out_specs, out_shape, args, sem, scratch_shapes=(), host=None):
    single = not isinstance(out_shape, (list, tuple))
    out_specs, out_shape = ([out_specs], [out_shape]) if single else (list(out_specs), list(out_shape))
    if host is None:
        outs = pl.pallas_call(body, name=name, grid=grid, in_specs=list(in_specs), out_specs=out_specs, out_shape=out_shape,
                              scratch_shapes=list(scratch_shapes), compiler_params=_cparams(sem))(*args)
        return outs[0] if single else outs
    any_spec = pl.BlockSpec(memory_space=pl.ANY)
    n_in, n_out, n_scr, nx_in, nx_out = len(in_specs), len(out_specs), len(scratch_shapes), len(host.ins), len(host.out_shape)

    def hosted(*refs):
        c_in, x_in = refs[:n_in], refs[n_in:n_in + nx_in]
        o0 = n_in + nx_in
        c_out, x_out = refs[o0:o0 + n_out], refs[o0 + n_out:o0 + n_out + nx_out]
        s0 = o0 + n_out + nx_out
        start, finish = host.programs(x_in, x_out, refs[s0 + n_scr], refs[s0 + n_scr + 1])
        first = functools.reduce(jnp.logical_and, [pl.program_id(d) == 0 for d in range(len(grid))])
        last = functools.reduce(jnp.logical_and, [pl.program_id(d) == grid[d] - 1 for d in range(len(grid))])
        pl.when(first)(start)
        body(*c_in, *c_out, *refs[s0:s0 + n_scr])
        pl.when(last)(finish)

    outs = pl.pallas_call(
        hosted, name=name, grid=grid, in_specs=list(in_specs) + [any_spec] * nx_in,
        out_specs=out_specs + [any_spec] * nx_out, out_shape=out_shape + host.out_shape,
        scratch_shapes=list(scratch_shapes) + host.sems(),
        compiler_params=host.cparams(dimension_semantics=sem, vmem_limit_bytes=VMEM_LIMIT))(*args, *host.ins)
    host.outs = outs[n_out:]
    return outs[0] if single else outs[:n_out]


NN, NT, TN = ((1,), (0,)), ((1,), (1,)), ((0,), (0,))


def _shift_dims(dims, batch):
    if not batch:
        return (dims, ((), ()))
    return (((dims[0][0] + 1,), (dims[1][0] + 1,)), ((0,), (0,)))


def _make_mm(dims, exact, batch=False):
    def raw(a, b, d):
        dn = _shift_dims(d, batch)
        if exact == "split_lhs":
            ah, bh = a.astype(BF16), b.astype(BF16)
            al = (a - ah.astype(F32)).astype(BF16)
            return lax.dot_general(ah, bh, dn, preferred_element_type=F32) + lax.dot_general(al, bh, dn,
                                                                                              preferred_element_type=F32)
        if exact == "split":
            ah, bh = a.astype(BF16), b.astype(BF16)
            al, bl = (a - ah.astype(F32)).astype(BF16), (b - bh.astype(F32)).astype(BF16)
            dot = lambda p, q: lax.dot_general(p, q, dn, preferred_element_type=F32)
            return dot(ah, bh) + (dot(ah, bl) + dot(al, bh))
        if exact:
            return lax.dot_general(a.astype(F32), b.astype(F32), dn, precision=HI, preferred_element_type=F32)
        return lax.dot_general(a.astype(BF16), b.astype(BF16), dn, preferred_element_type=F32)

    @jax.custom_vjp
    def mm(a, b):
        return raw(a, b, dims)

    def fwd(a, b):
        return raw(a, b, dims), (a, b)

    def bwd(res, g):
        a, b = res
        if dims == NN:
            da, db = raw(g, b, NT), raw(a, g, TN)
        elif dims == NT:
            da, db = raw(g, b, NN), raw(g, a, TN)
        else:
            da, db = raw(b, g, NT), raw(a, g, NN)
        return da.astype(a.dtype), db.astype(b.dtype)

    mm.defvjp(fwd, bwd)
    return mm


_mm = _make_mm(NN, False)
_mm_nt = _make_mm(NT, False)
_mm_tn = _make_mm(TN, False)
_mmx = _make_mm(NN, "split_lhs")
_bmm = _make_mm(NN, False, batch=True)
_bmm_nt = _make_mm(NT, False, batch=True)
_bmm_tn = _make_mm(TN, False, batch=True)
_bmmx = _make_mm(NN, True, batch=True)
_bmms = _make_mm(NN, "split", batch=True)
_bmms_nt = _make_mm(NT, "split", batch=True)
_bmms_tn = _make_mm(TN, "split", batch=True)


@jax.custom_vjp
def _unit_lower_inv(a):
    n = a.shape[-1]
    eye = (lax.broadcasted_iota(jnp.int32, a.shape, 1) == lax.broadcasted_iota(jnp.int32, a.shape, 2)).astype(F32)
    x = -a
    t = eye + x
    for _ in range(max(n.bit_length() - 2, 0)):
        x = _bmms(x, x)
        t = t + _bmms(t, x)
    return t


def _unit_lower_inv_fwd(a):
    t = _unit_lower_inv(a)
    return t, t


def _unit_lower_inv_bwd(t, g):
    return (-_bmms_tn(t, _bmms_nt(g, t)),)


_unit_lower_inv.defvjp(_unit_lower_inv_fwd, _unit_lower_inv_bwd)


def _scan_chunk_rows(x, reverse):
    nb, c, w = x.shape
    y = x.reshape(nb * c, w)
    pos = lax.broadcasted_iota(jnp.int32, y.shape, 0) % c
    step = 1
    while step < c:
        if reverse:
            y = y + jnp.where(pos < c - step, pltpu.roll(y, nb * c - step, 0), 0.0)
        else:
            y = y + jnp.where(pos >= step, pltpu.roll(y, step, 0), 0.0)
        step *= 2
    return y.reshape(nb, c, w)


@jax.custom_vjp
def _chunk_cumsum(x):
    return _scan_chunk_rows(x, False)


_chunk_cumsum.defvjp(lambda x: (_scan_chunk_rows(x, False), None), lambda _, g: (_scan_chunk_rows(g, True),))


def _rms(x, w, n):
    ms = jnp.sum(x * x, axis=-1, keepdims=True) * (1.0 / n)
    return x * lax.rsqrt(ms + NORM_EPS) * w


def _silu(x):
    return x * jax.nn.sigmoid(x)


def _softplus(x):
    return jnp.maximum(x, 0.0) + jnp.log(1.0 + jnp.exp(-jnp.abs(x)))


def _rope(x, cos, sin, perm):
    return x * cos + _mmx(x, perm) * sin


def _mla_prep_fn(rows, consts):
    q_lat, kv_lat, k_pe, cos, sin = rows
    qn = _rms(q_lat, consts["qa_w"], LORA)
    kvn = _rms(kv_lat, consts["kva_w"], LORA)
    outs = []
    for h in range(HEADS):
        q_n = _mm_nt(qn, consts["wq_n"][h])
        q_r = _mm_nt(qn, consts["wq_r"][h])
        rs = lax.rsqrt((jnp.sum(q_n * q_n, -1, keepdims=True) + jnp.sum(q_r * q_r, -1, keepdims=True)) * (1.0 / QK_DIM)
                       + NORM_EPS)
        q_n = q_n * rs * consts["qn_n"]
        q_r = _rope(q_r * rs * consts["qn_r"], cos, sin, consts["perm"])
        k_n = _mm_nt(kvn, consts["wk_n"][h])
        v = _mm_nt(kvn, consts["wv"][h])
        rk = lax.rsqrt((jnp.sum(k_n * k_n, -1, keepdims=True) + jnp.sum(k_pe * k_pe, -1, keepdims=True)) * (1.0 / QK_DIM)
                       + NORM_EPS)
        k_n = k_n * rk * consts["kn_n"]
        k_r = _rope(k_pe * rk * consts["kn_r"], cos, sin, consts["perm"])
        outs += [q_n, q_r, k_n, k_r, v]
    return tuple(outs)


def _attn_fn(q, k, v, row0):
    s = _mm_nt(q, k) * (1.0 / math.sqrt(QK_DIM))
    qpos = row0 + lax.broadcasted_iota(jnp.int32, s.shape, 0)
    kpos = lax.broadcasted_iota(jnp.int32, s.shape, 1)
    s = jnp.where(kpos <= qpos, s, -1e30)
    m = lax.stop_gradient(jnp.max(s, axis=-1, keepdims=True))
    p = jnp.exp(s - m)
    p = p / jnp.sum(p, axis=-1, keepdims=True)
    return _mm(p, v)


def _dn_prep_fn(rows, consts):
    qc, kc, ab = rows
    a_b = _mmx(ab, consts["sel_a"])
    b_b = _mmx(ab, consts["sel_b"])
    beta = jax.nn.sigmoid(b_b)
    g = -jnp.exp(consts["alog"]) * _softplus(a_b + consts["dtb"])
    qs, ks = [], []
    for h in range(HEADS):
        q, k = qc[h], kc[h]
        qs.append(q * lax.rsqrt(jnp.sum(q * q, -1, keepdims=True) + NORM_EPS))
        ks.append(k * lax.rsqrt(jnp.sum(k * k, -1, keepdims=True) + NORM_EPS))
    return tuple(qs), tuple(ks), g, beta


def _dn_chunk_fn(q, k, v, gb, g64, bb):
    nb = q.shape[0]
    ri = lax.broadcasted_iota(jnp.int32, (nb, CHUNK, CHUNK), 1)
    ci = lax.broadcasted_iota(jnp.int32, (nb, CHUNK, CHUNK), 2)
    tri = ri >= ci
    strict = ri > ci
    tril = tri.astype(F32)
    eye = (ri == ci).astype(F32)
    ones = jnp.ones((nb, CHUNK, CHUNK), F32)
    gc = _chunk_cumsum(gb)
    gc64 = _chunk_cumsum(g64)
    grow = _bmmx(ones, eye * gc64)
    diff = gc64 - grow
    decay = jnp.where(tri, jnp.exp(jnp.where(tri, diff, 0.0)), 0.0)
    kb = k * bb
    vb = v * bb
    a = jnp.where(strict, _bmm_nt(kb, k) * decay, 0.0)
    tinv = _unit_lower_inv(a)
    u = _bmm(tinv, vb)
    w = _bmm(tinv, kb * jnp.exp(gc))
    qs = q * (1.0 / math.sqrt(HEAD))
    qk = _bmm_nt(qs, k) * decay
    qg = qs * jnp.exp(gc)
    glast = jnp.sum(gb, axis=1, keepdims=True)
    kdec = k * jnp.exp(glast - gc)
    n_mat = _bmm_tn(kdec, w)
    b_mat = _bmm_tn(kdec, u)
    q_eff = qg - _bmm(qk, w)
    o_own = _bmm(qk, u)
    return n_mat, b_mat, q_eff, o_own, jnp.exp(glast)


def _dn_rec_fn(s, n_mat, b_mat, eg):
    return s * eg - _mm(n_mat, s) + b_mat


def _dn_o_fn(s, q_eff, o_own):
    return _bmm(q_eff, s) + o_own


def _dn_out_fn(o, z, w):
    return _rms(o, w, HEAD) * _silu(z)


def _row_tile(t, parts=8):
    return t // parts if (t // parts) % 16 == 0 else t


def _tile(n, pref, unit):
    best = n
    for cand in range(unit, min(n, pref) + 1, unit):
        if n % cand == 0:
            best = cand
    return best if best <= pref else n


def _rows_call(name, body, rows, consts, outs, accs, r, host=None):
    rows = [a if isinstance(a, tuple) else (a, a.shape[1], 0) for a in rows]
    t = rows[0][0].shape[0]
    zero = lambda nd: (lambda i: (0,) * nd)
    in_specs = [pl.BlockSpec((r, w), functools.partial(lambda i, b: (i, b), b=blk)) for _, w, blk in rows]
    rows = [a for a, _, _ in rows]
    in_specs += [pl.BlockSpec(a.shape, zero(a.ndim)) for a in consts]
    out_shape = [jax.ShapeDtypeStruct((t, w), dt) for w, dt in outs] + [jax.ShapeDtypeStruct(s, F32) for s in accs]
    out_specs = [pl.BlockSpec((r, w), lambda i: (i, 0)) for w, _ in outs] + [pl.BlockSpec(s, zero(len(s))) for s in accs]
    return _pcall(body, name, (t // r,), in_specs, out_specs, out_shape, [*rows, *consts], ("arbitrary",), host=host)


def _accumulate(ref, val):
    @pl.when(pl.program_id(0) == 0)
    def _():
        ref[...] = jnp.zeros(ref.shape, ref.dtype)

    ref[...] += val


def _matmul(name, a, b, dims, out_dtype, res=None, host=None):
    if dims == "nn":
        (m, k), n = a.shape, b.shape[1]
    elif dims == "nt":
        (m, k), n = a.shape, b.shape[0]
    else:
        (k, m), n = a.shape, b.shape[1]
    if dims == "tn":
        tm, tn = _tile(m, 640, 128), _tile(n, 1408, 128)
    else:
        tm = m
        per_col = 2 * (2 * k + m * jnp.dtype(out_dtype).itemsize + (4 * m if res is not None else 0))
        budget = MATMUL_VMEM_BUDGET - 2 * 2 * m * k
        tn = _tile(n, max(128, min(512, budget // per_col // 128 * 128)), 128)
    if dims == "nn":
        a_spec, b_spec, dn = pl.BlockSpec((tm, k), lambda i, j: (i, 0)), pl.BlockSpec((k, tn), lambda i, j: (0, j)), NN
    elif dims == "nt":
        a_spec, b_spec, dn = pl.BlockSpec((tm, k), lambda i, j: (i, 0)), pl.BlockSpec((tn, k), lambda i, j: (j, 0)), NT
    else:
        a_spec, b_spec, dn = pl.BlockSpec((k, tm), lambda i, j: (0, i)), pl.BlockSpec((k, tn), lambda i, j: (0, j)), TN
    o_spec = pl.BlockSpec((tm, tn), lambda i, j: (i, j))

    def body(*refs):
        a_ref, b_ref, o_ref = refs[0], refs[1], refs[-1]
        acc = lax.dot_general(a_ref[...].astype(BF16), b_ref[...].astype(BF16), (dn, ((), ())),
                              preferred_element_type=F32)
        if res is not None:
            acc = acc + refs[2][...]
        o_ref[...] = acc.astype(out_dtype)

    ins = [a, b] + ([res] if res is not None else [])
    specs = [a_spec, b_spec] + ([o_spec] if res is not None else [])
    return _pcall(body, name, (m // tm, n // tn), specs, o_spec, jax.ShapeDtypeStruct((m, n), out_dtype), ins,
                  ("arbitrary", "arbitrary"), host=host)


def _rms_fwd(name, h, w, host=None):
    n = h.shape[1]

    def body(h_ref, w_ref, o_ref):
        o_ref[...] = _rms(h_ref[...], w_ref[...], n).astype(BF16)

    return _rows_call(name, body, [h], [w], [(n, BF16)], [], _row_tile(h.shape[0]), host=host)[0]


def _rms_bwd(name, h, w, cts, resid, host=None):
    n = h.shape[1]
    nct = len(cts)

    def body(*refs):
        h_ref, ct_refs, r_ref, w_ref = refs[0], refs[1:1 + nct], refs[1 + nct], refs[2 + nct]
        dh_ref, dh16_ref, dw_ref = refs[-3], refs[-2], refs[-1]
        ct = ct_refs[0][...].astype(F32)
        for c in ct_refs[1:]:
            ct = ct + c[...].astype(F32)
        _, vjp = jax.vjp(lambda x, ww: _rms(x, ww, n), h_ref[...], w_ref[...])
        dh, dw = vjp(ct)
        dh = dh + r_ref[...]
        dh_ref[...] = dh
        dh16_ref[...] = dh.astype(BF16)
        _accumulate(dw_ref, dw)

    return _rows_call(name, body, [h, *cts, resid], [w], [(n, F32), (n, BF16)], [(1, n)], _row_tile(h.shape[0]), host=host)


def _mla_consts_from_refs(qa, wq, kva, wkv, qn, kn, perm):
    f = lambda r: r[...].astype(F32)
    return dict(
        qa_w=f(qa), kva_w=f(kva), perm=f(perm),
        wq_n=[wq[h * QK_PAD:h * QK_PAD + HEAD, :].astype(F32) for h in range(HEADS)],
        wq_r=[wq[h * QK_PAD + HEAD:(h + 1) * QK_PAD, :].astype(F32) for h in range(HEADS)],
        wk_n=[wkv[h * QK_PAD:h * QK_PAD + HEAD, :].astype(F32) for h in range(HEADS)],
        wv=[wkv[h * QK_PAD + HEAD:(h + 1) * QK_PAD, :].astype(F32) for h in range(HEADS)],
        qn_n=qn[:, 0:HEAD], qn_r=qn[:, HEAD:QK_PAD], kn_n=kn[:, 0:HEAD], kn_r=kn[:, HEAD:QK_PAD])


def _mla_prep_fwd(q_lat, kv_lat, k_pe, cos, sin, qa, wq, kva, wkv, qn, kn, perm):
    def body(ql, kvl, kp, c, s, qa_r, wq_r, kva_r, wkv_r, qn_r, kn_r, p_r, q_out, k_out, v_out):
        consts = _mla_consts_from_refs(qa_r, wq_r, kva_r, wkv_r, qn_r, kn_r, p_r)
        outs = _mla_prep_fn((ql[...], kvl[...], kp[...], c[...], s[...]), consts)
        for h in range(HEADS):
            q_n, q_r, k_n, k_r, v = outs[5 * h:5 * h + 5]
            q_out[:, h * QK_PAD:h * QK_PAD + HEAD] = q_n.astype(BF16)
            q_out[:, h * QK_PAD + HEAD:(h + 1) * QK_PAD] = q_r.astype(BF16)
            k_out[:, h * QK_PAD:h * QK_PAD + HEAD] = k_n.astype(BF16)
            k_out[:, h * QK_PAD + HEAD:(h + 1) * QK_PAD] = k_r.astype(BF16)
            v_out[:, h * HEAD:(h + 1) * HEAD] = v.astype(BF16)

    return _rows_call("mla_prep_fwd", body, [q_lat, kv_lat, k_pe, cos, sin], [qa, wq, kva, wkv, qn, kn, perm],
                      [(HEADS * QK_PAD, BF16), (HEADS * QK_PAD, BF16), (DN_WIDTH, BF16)], [], _row_tile(cos.shape[0], 4))


def _mla_prep_bwd(q_lat, kv_lat, k_pe, cos, sin, dq, dk, dv, qa, wq, kva, wkv, qn, kn, perm, host=None):
    def body(ql, kvl, kp, c, s, dq_r, dk_r, dv_r, qa_r, wq_r, kva_r, wkv_r, qn_r, kn_r, p_r,
             dql, dkvl, dkp, dqa, dwq, dkva, dwkv, dqn, dkn):
        consts = _mla_consts_from_refs(qa_r, wq_r, kva_r, wkv_r, qn_r, kn_r, p_r)
        cc, ss, pm = c[...], s[...], consts.pop("perm")
        _, vjp = jax.vjp(lambda rows, cs: _mla_prep_fn((*rows, cc, ss), dict(cs, perm=pm)), (ql[...], kvl[...], kp[...]),
                         consts)
        cts = []
        for h in range(HEADS):
            cts += [dq_r[:, h * QK_PAD:h * QK_PAD + HEAD], dq_r[:, h * QK_PAD + HEAD:(h + 1) * QK_PAD],
                    dk_r[:, h * QK_PAD:h * QK_PAD + HEAD], dk_r[:, h * QK_PAD + HEAD:(h + 1) * QK_PAD],
                    dv_r[:, h * HEAD:(h + 1) * HEAD]]
        (d_ql, d_kvl, d_kp), dc = vjp(tuple(cts))
        dql[...] = d_ql.astype(BF16)
        dkvl[...] = d_kvl.astype(BF16)
        dkp[...] = d_kp.astype(BF16)
        first = pl.program_id(0) == 0

        def acc(ref, sl, val):
            @pl.when(first)
            def _():
                ref[sl] = val

            @pl.when(jnp.logical_not(first))
            def _():
                ref[sl] += val

        full = (slice(None), slice(None))
        acc(dqa, full, dc["qa_w"])
        acc(dkva, full, dc["kva_w"])
        for h in range(HEADS):
            acc(dwq, (slice(h * QK_PAD, h * QK_PAD + HEAD), slice(None)), dc["wq_n"][h])
            acc(dwq, (slice(h * QK_PAD + HEAD, (h + 1) * QK_PAD), slice(None)), dc["wq_r"][h])
            acc(dwkv, (slice(h * QK_PAD, h * QK_PAD + HEAD), slice(None)), dc["wk_n"][h])
            acc(dwkv, (slice(h * QK_PAD + HEAD, (h + 1) * QK_PAD), slice(None)), dc["wv"][h])
        acc(dqn, (slice(None), slice(0, HEAD)), dc["qn_n"])
        acc(dqn, (slice(None), slice(HEAD, QK_PAD)), dc["qn_r"])
        acc(dkn, (slice(None), slice(0, HEAD)), dc["kn_n"])
        acc(dkn, (slice(None), slice(HEAD, QK_PAD)), dc["kn_r"])

    return _rows_call("mla_prep_bwd", body, [q_lat, kv_lat, k_pe, cos, sin, dq, dk, dv],
                      [qa, wq, kva, wkv, qn, kn, perm],
                      [(LORA, BF16), (LORA, BF16), (HEAD, BF16)],
                      [(1, LORA), wq.shape, (1, LORA), wkv.shape, (1, QK_PAD), (1, QK_PAD)], _row_tile(cos.shape[0], 4),
                      host=host)


ATTN_Q_ROWS = 256


def _attn_blocks(t):
    return [(r0, min(ATTN_Q_ROWS, t - r0)) for r0 in range(0, t, ATTN_Q_ROWS)]


def _attn_fwd(q, k, v, host=None):
    t = q.shape[0]

    def body(q_ref, k_ref, v_ref, o_ref):
        for r0, rows in _attn_blocks(t):
            ext = r0 + rows
            o_ref[r0:ext, :] = _attn_fn(q_ref[r0:ext, :], k_ref[0:ext, :], v_ref[0:ext, :], r0)

    qk_spec = pl.BlockSpec((t, QK_PAD), lambda h: (0, h))
    v_spec = pl.BlockSpec((t, HEAD), lambda h: (0, h))
    return _pcall(body, "attn_fwd", (HEADS,), [qk_spec, qk_spec, v_spec], v_spec,
                  jax.ShapeDtypeStruct((t, HEADS * HEAD), F32), [q, k, v], ("arbitrary",), host=host)


def _attn_bwd(q, k, v, do, host=None):
    t = q.shape[0]

    def body(q_ref, k_ref, v_ref, do_ref, dq_ref, dk_ref, dv_ref):
        dk_ref[...] = jnp.zeros(dk_ref.shape, F32)
        dv_ref[...] = jnp.zeros(dv_ref.shape, F32)
        for r0, rows in _attn_blocks(t):
            ext = r0 + rows
            _, vjp = jax.vjp(functools.partial(_attn_fn, row0=r0), q_ref[r0:ext, :].astype(F32),
                             k_ref[0:ext, :].astype(F32), v_ref[0:ext, :].astype(F32))
            dq, dk, dv = vjp(do_ref[r0:ext, :])
            dq_ref[r0:ext, :] = dq
            dk_ref[0:ext, :] += dk
            dv_ref[0:ext, :] += dv

    qk_spec = pl.BlockSpec((t, QK_PAD), lambda h: (0, h))
    v_spec = pl.BlockSpec((t, HEAD), lambda h: (0, h))
    return _pcall(body, "attn_bwd", (HEADS,), [qk_spec, qk_spec, v_spec, v_spec], [qk_spec, qk_spec, v_spec],
                  [jax.ShapeDtypeStruct((t, HEADS * QK_PAD), F32), jax.ShapeDtypeStruct((t, HEADS * QK_PAD), F32),
                   jax.ShapeDtypeStruct((t, HEADS * HEAD), F32)], [q, k, v, do], ("arbitrary",), host=host)


def _mix_out_fwd(o_mla, o_dn, z, w_mla, w_dn):
    def body(om_ref, od_ref, z_ref, wm_ref, wd_ref, o_ref):
        for h in range(HEADS):
            sl = slice(h * HEAD, (h + 1) * HEAD)
            o_ref[:, sl] = _rms(om_ref[:, sl], wm_ref[...], HEAD).astype(BF16)
            o_ref[:, DN_WIDTH + h * HEAD:DN_WIDTH + (h + 1) * HEAD] = _dn_out_fn(od_ref[:, sl], z_ref[:, sl],
                                                                                 wd_ref[...]).astype(BF16)

    return _rows_call("mix_out_fwd", body, [o_mla, o_dn, z], [w_mla, w_dn], [(2 * DN_WIDTH, BF16)], [],
                      _row_tile(o_mla.shape[0]))[0]


def _mix_out_bwd(o_mla, o_dn, z, dmixed, w_mla, w_dn, host=None):
    def body(om_ref, od_ref, z_ref, dm_ref, wm_ref, wd_ref, dom_ref, dod_ref, dz_ref, dwm_ref, dwd_ref):
        dwm = dwd = None
        for h in range(HEADS):
            sl = slice(h * HEAD, (h + 1) * HEAD)
            _, vjp = jax.vjp(lambda o, w: _rms(o, w, HEAD), om_ref[:, sl], wm_ref[...])
            do, dw = vjp(dm_ref[:, sl].astype(F32))
            dom_ref[:, sl] = do
            dwm = dw if dwm is None else dwm + dw
            _, vjp = jax.vjp(_dn_out_fn, od_ref[:, sl], z_ref[:, sl], wd_ref[...])
            do, dz, dw = vjp(dm_ref[:, DN_WIDTH + h * HEAD:DN_WIDTH + (h + 1) * HEAD].astype(F32))
            dod_ref[:, sl] = do
            dz_ref[:, sl] = dz.astype(BF16)
            dwd = dw if dwd is None else dwd + dw
        _accumulate(dwm_ref, dwm)
        _accumulate(dwd_ref, dwd)

    return _rows_call("mix_out_bwd", body, [o_mla, o_dn, z, dmixed], [w_mla, w_dn],
                      [(DN_WIDTH, F32), (DN_WIDTH, F32), (DN_WIDTH, BF16)], [(1, HEAD), (1, HEAD)],
                      _row_tile(o_mla.shape[0]), host=host)


def _shift_down(x, s):
    if s == 0:
        return x
    rows = lax.broadcasted_iota(jnp.int32, x.shape, 0)
    return jnp.where(rows >= s, pltpu.roll(x, s, 0), 0.0)


def _shift_up(x, s):
    if s == 0:
        return x
    t = x.shape[0]
    rows = lax.broadcasted_iota(jnp.int32, x.shape, 0)
    return jnp.where(rows < t - s, pltpu.roll(x, t - s, 0), 0.0)


def _col_call(name, body, cols, taps, outs, tap_outs, cw, host=None):
    t, c = cols[0].shape[0], taps[0].shape[1]
    in_specs = [pl.BlockSpec((t, cw), lambda j: (0, j)) for _ in cols]
    in_specs += [pl.BlockSpec((a.shape[0], cw), lambda j: (0, j)) for a in taps]
    out_shape = [jax.ShapeDtypeStruct((t, c), dt) for dt in outs] + [jax.ShapeDtypeStruct((n, c), F32) for n in tap_outs]
    out_specs = [pl.BlockSpec((t, cw), lambda j: (0, j)) for _ in outs]
    out_specs += [pl.BlockSpec((n, cw), lambda j: (0, j)) for n in tap_outs]
    return _pcall(body, name, (c // cw,), in_specs, out_specs, out_shape, [*cols, *taps], ("arbitrary",), host=host)


def _causal_conv(x, w_ref, width):
    acc = w_ref[width - 1:width, :] * x
    for j in range(width - 1):
        acc = acc + w_ref[j:j + 1, :] * _shift_down(x, width - 1 - j)
    return acc


def _causal_conv_bwd(x, dpre, w_ref, dx_ref, dw_ref, width):
    dx = w_ref[width - 1:width, :] * dpre
    dw_ref[width - 1:width, :] = jnp.sum(dpre * x, axis=0, keepdims=True)
    for j in range(width - 1):
        s = width - 1 - j
        dx = dx + w_ref[j:j + 1, :] * _shift_up(dpre, s)
        dw_ref[j:j + 1, :] = jnp.sum(dpre * _shift_down(x, s), axis=0, keepdims=True)
    dx_ref[...] = dx.astype(dx_ref.dtype)


def _dsilu(x):
    sg = jax.nn.sigmoid(x)
    return sg * (1.0 + x * (1.0 - sg))


def _dn_conv_fwd(x, w):
    def body(x_ref, w_ref, y_ref):
        y_ref[...] = _silu(_causal_conv(x_ref[...], w_ref, 4))

    return _col_call("dn_conv_fwd", body, [x], [w], [F32], [], 256)[0]


def _dn_conv_bwd(x, w, dy):
    def body(x_ref, dy_ref, w_ref, dx_ref, dw_ref):
        xv = x_ref[...]
        dpre = dy_ref[...] * _dsilu(_causal_conv(xv, w_ref, 4))
        _causal_conv_bwd(xv, dpre, w_ref, dx_ref, dw_ref, 4)

    return _col_call("dn_conv_bwd", body, [x, dy], [w], [BF16], [4], 256)


def _glu_fwd(gpre, up, w, b, host=None):
    def body(g_ref, u_ref, w_ref, b_ref, a_ref):
        gate = _causal_conv(g_ref[...].astype(F32), w_ref, 3) + b_ref[...]
        a_ref[...] = (_silu(gate) * u_ref[...].astype(F32)).astype(BF16)

    return _col_call("glu_fwd", body, [gpre, up], [w, b], [BF16], [], 256, host=host)[0]


def _glu_bwd(gpre, up, w, b, dact):
    def body(g_ref, u_ref, da_ref, w_ref, b_ref, dg_ref, du_ref, dw_ref, db_ref):
        gv = g_ref[...].astype(F32)
        gate = _causal_conv(gv, w_ref, 3) + b_ref[...]
        da = da_ref[...].astype(F32)
        sg = jax.nn.sigmoid(gate)
        du_ref[...] = (da * (gate * sg)).astype(BF16)
        dgate = da * u_ref[...].astype(F32) * (sg * (1.0 + gate * (1.0 - sg)))
        db_ref[...] = jnp.sum(dgate, axis=0, keepdims=True)
        _causal_conv_bwd(gv, dgate, w_ref, dg_ref, dw_ref, 3)

    return _col_call("glu_bwd", body, [gpre, up, dact], [w, b], [BF16, BF16], [3, 1], 256)


def _dn_prep_consts(sa, sb, al, dt):
    return dict(sel_a=sa[...], sel_b=sb[...], alog=al[...], dtb=dt[...])


def _dn_prep_fwd(conv, ab, sel_a, sel_b, alog, dtb):
    def body(c_ref, ab_ref, sa, sb, al, dt, q_out, k_out, g_out, b_out):
        qc = tuple(c_ref[:, h * HEAD:(h + 1) * HEAD] for h in range(HEADS))
        kc = tuple(c_ref[:, DN_WIDTH + h * HEAD:DN_WIDTH + (h + 1) * HEAD] for h in range(HEADS))
        qs, ks, g, beta = _dn_prep_fn((qc, kc, ab_ref[...]), _dn_prep_consts(sa, sb, al, dt))
        for h in range(HEADS):
            q_out[:, h * HEAD:(h + 1) * HEAD] = qs[h]
            k_out[:, h * HEAD:(h + 1) * HEAD] = ks[h]
        g_out[...] = g
        b_out[...] = beta

    return _rows_call("dn_prep_fwd", body, [conv, ab], [sel_a, sel_b, alog, dtb], [(DN_WIDTH, F32)] * 4, [],
                      _row_tile(conv.shape[0]))


def _dn_prep_bwd(conv, ab, dq, dk, dv, dg, db, sel_a, sel_b, alog, dtb):
    def body(c_ref, ab_ref, dq_r, dk_r, dv_r, dg_r, db_r, sa, sb, al, dt, dc_out, dab_out, dal_out, ddt_out):
        qc = tuple(c_ref[:, h * HEAD:(h + 1) * HEAD] for h in range(HEADS))
        kc = tuple(c_ref[:, DN_WIDTH + h * HEAD:DN_WIDTH + (h + 1) * HEAD] for h in range(HEADS))
        consts = _dn_prep_consts(sa, sb, al, dt)
        sel = dict(sel_a=consts["sel_a"], sel_b=consts["sel_b"])
        _, vjp = jax.vjp(lambda rows, ad: _dn_prep_fn(rows, {**sel, **ad}), (qc, kc, ab_ref[...]),
                         dict(alog=consts["alog"], dtb=consts["dtb"]))
        cq = tuple(dq_r[:, h * HEAD:(h + 1) * HEAD] for h in range(HEADS))
        ck = tuple(dk_r[:, h * HEAD:(h + 1) * HEAD] for h in range(HEADS))
        (dqc, dkc, dab), dad = vjp((cq, ck, dg_r[...], db_r[...]))
        for h in range(HEADS):
            dc_out[:, h * HEAD:(h + 1) * HEAD] = dqc[h]
            dc_out[:, DN_WIDTH + h * HEAD:DN_WIDTH + (h + 1) * HEAD] = dkc[h]
        dc_out[:, 2 * DN_WIDTH:3 * DN_WIDTH] = dv_r[...]
        dab_out[...] = dab.astype(BF16)
        _accumulate(dal_out, dad["alog"])
        _accumulate(ddt_out, dad["dtb"])

    return _rows_call("dn_prep_bwd", body, [conv, ab, dq, dk, dv, dg, db], [sel_a, sel_b, alog, dtb],
                      [(3 * DN_WIDTH, F32), (HEAD, BF16)], [(1, DN_WIDTH), (1, DN_WIDTH)], _row_tile(conv.shape[0]))


def _chunk_batch(t):
    nc = t // CHUNK
    return nc // 2 if nc % 2 == 0 else nc


def _dn_chunk_specs(t, nb):
    rows = nb * CHUNK
    blk = pl.BlockSpec((rows, HEAD), lambda h, b: (b, h))
    vblk = pl.BlockSpec((rows, HEAD), lambda h, b: (b, 2 * HEADS + h))
    mat = pl.BlockSpec((nb, HEAD, HEAD), lambda h, b: (b, h, 0))
    return rows, blk, vblk, mat


def _dn_chunk_fwd(qn, kn, conv, g, beta, host=None):
    t = qn.shape[0]
    nb = _chunk_batch(t)
    rows, blk, vblk, mat = _dn_chunk_specs(t, nb)

    def body(q_ref, k_ref, v_ref, g_ref, b_ref, n_o, b_o, qe_o, oo_o, eg_o):
        r3 = lambda x: x.reshape(nb, CHUNK, x.shape[-1])
        n_mat, b_mat, q_eff, o_own, eg = _dn_chunk_fn(r3(q_ref[...]), r3(k_ref[...]), r3(v_ref[...]), r3(g_ref[...]),
                                                      r3(g_ref[:, 0:CHUNK]), r3(b_ref[...]))
        n_o[...] = n_mat
        b_o[...] = b_mat
        qe_o[...] = q_eff.reshape(rows, HEAD)
        oo_o[...] = o_own.reshape(rows, HEAD)
        eg_o[...] = jnp.broadcast_to(eg, (nb, HEAD, HEAD))

    nc = t // CHUNK
    mats = jax.ShapeDtypeStruct((nc, DN_WIDTH, HEAD), F32)
    rowsd = jax.ShapeDtypeStruct((t, DN_WIDTH), F32)
    return _pcall(body, "dn_chunk_fwd", (HEADS, t // rows), [blk, blk, vblk, blk, blk], [mat, mat, blk, blk, mat],
                  [mats, mats, rowsd, rowsd, mats], [qn, kn, conv, g, beta], ("arbitrary", "arbitrary"), host=host)


def _dn_chunk_bwd(qn, kn, conv, g, beta, sall, gall, dq_eff, do, host=None):
    t = qn.shape[0]
    nb = _chunk_batch(t)
    rows, blk, vblk, mat = _dn_chunk_specs(t, nb)

    def body(q_ref, k_ref, v_ref, g_ref, b_ref, s_ref, ga_ref, dqe_ref, do_ref, dq_o, dk_o, dv_o, dg_o, db_o):
        r3 = lambda x: x.reshape(nb, CHUNK, x.shape[-1])
        _, vjp = jax.vjp(_dn_chunk_fn, r3(q_ref[...]), r3(k_ref[...]), r3(v_ref[...]), r3(g_ref[...]),
                         r3(g_ref[:, 0:CHUNK]), r3(b_ref[...]))
        s, ga = s_ref[...], ga_ref[...]
        d_n = -_bmm_nt(ga, s)
        d_eg = jnp.sum(ga * s, axis=1, keepdims=True)
        dq, dk, dv, dg, dg64, db = vjp((d_n, ga, r3(dqe_ref[...]), r3(do_ref[...]), d_eg))
        for o_ref, val in zip((dq_o, dk_o, dv_o, dg_o, db_o), (dq, dk, dv, dg, db)):
            o_ref[...] = val.reshape(rows, HEAD)
        dg_o[:, 0:CHUNK] += dg64.reshape(rows, CHUNK)

    return _pcall(body, "dn_chunk_bwd", (HEADS, t // rows), [blk, blk, vblk, blk, blk, mat, mat, blk, blk], [blk] * 5,
                  [jax.ShapeDtypeStruct((t, DN_WIDTH), F32)] * 5, [qn, kn, conv, g, beta, sall, gall, dq_eff, do],
                  ("arbitrary", "arbitrary"), host=host)


def _dn_rec_fwd(n_mat, b_mat, eg, host=None):
    nc = n_mat.shape[0]
    nb = _chunk_batch(nc * CHUNK)
    spec = pl.BlockSpec((nb, DN_WIDTH, HEAD), lambda i: (i, 0, 0))

    def body(n_ref, b_ref, eg_ref, sall_ref, s_scr):
        @pl.when(pl.program_id(0) == 0)
        def _():
            s_scr[...] = jnp.zeros(s_scr.shape, F32)

        for j in range(nb):
            sall_ref[j] = s_scr[...]
            for h in range(HEADS):
                sl = slice(h * HEAD, (h + 1) * HEAD)
                s_scr[sl, :] = _dn_rec_fn(s_scr[sl, :], n_ref[j, sl, :], b_ref[j, sl, :],
                                          eg_ref[j, h * HEAD:h * HEAD + 1, :])

    return _pcall(body, "dn_rec_fwd", (nc // nb,), [spec] * 3, spec, jax.ShapeDtypeStruct((nc, DN_WIDTH, HEAD), F32),
                  [n_mat, b_mat, eg], ("arbitrary",), scratch_shapes=[pltpu.VMEM((DN_WIDTH, HEAD), F32)], host=host)


def _dn_rec_bwd(n_mat, eg, ds_out, host=None):
    nc = n_mat.shape[0]
    nb = _chunk_batch(nc * CHUNK)
    steps = nc // nb
    spec = pl.BlockSpec((nb, DN_WIDTH, HEAD), lambda i: (steps - 1 - i, 0, 0))

    def body(n_ref, eg_ref, dso_ref, gall_ref, g_scr):
        @pl.when(pl.program_id(0) == 0)
        def _():
            g_scr[...] = jnp.zeros(g_scr.shape, F32)

        for j in reversed(range(nb)):
            gall_ref[j] = g_scr[...]
            for h in range(HEADS):
                sl = slice(h * HEAD, (h + 1) * HEAD)
                gv = g_scr[sl, :]
                g_scr[sl, :] = (gv * eg_ref[j, h * HEAD:h * HEAD + 1, :] - _mm_tn(n_ref[j, sl, :], gv)
                                + dso_ref[j, sl, :])

    return _pcall(body, "dn_rec_bwd", (steps,), [spec] * 3, spec, jax.ShapeDtypeStruct((nc, DN_WIDTH, HEAD), F32),
                  [n_mat, eg, ds_out], ("arbitrary",), scratch_shapes=[pltpu.VMEM((DN_WIDTH, HEAD), F32)], host=host)


def _dn_o_fwd(sall, q_eff, o_own):
    t = q_eff.shape[0]
    nb = _chunk_batch(t)
    rows, blk, _, mat = _dn_chunk_specs(t, nb)

    def body(s_ref, qe_ref, oo_ref, o_ref):
        r3 = lambda x: x.reshape(nb, CHUNK, HEAD)
        o_ref[...] = _dn_o_fn(s_ref[...], r3(qe_ref[...]), r3(oo_ref[...])).reshape(rows, HEAD)

    return _pcall(body, "dn_o_fwd", (HEADS, t // rows), [mat, blk, blk], blk, jax.ShapeDtypeStruct((t, DN_WIDTH), F32),
                  [sall, q_eff, o_own], ("arbitrary", "arbitrary"))


def _dn_o_bwd(sall, q_eff, do, host=None):
    t = q_eff.shape[0]
    nb = _chunk_batch(t)
    rows, blk, _, mat = _dn_chunk_specs(t, nb)

    def body(s_ref, qe_ref, do_ref, dqe_ref, ds_ref):
        r3 = lambda x: x.reshape(nb, CHUNK, HEAD)
        dov = r3(do_ref[...])
        dqe_ref[...] = _bmm_nt(dov, s_ref[...]).reshape(rows, HEAD)
        ds_ref[...] = _bmm_tn(r3(qe_ref[...]), dov)

    nc = t // CHUNK
    return _pcall(body, "dn_o_bwd", (HEADS, t // rows), [mat, blk, blk], [blk, mat],
                  [jax.ShapeDtypeStruct((t, DN_WIDTH), F32), jax.ShapeDtypeStruct((nc, DN_WIDTH, HEAD), F32)],
                  [sall, q_eff, do], ("arbitrary", "arbitrary"), host=host)


def _loss_call(h2, tgt, n_valid):
    t, n = h2.shape
    r = _row_tile(t)

    def body(h_ref, t_ref, dy_ref, dy16_ref, acc_ref):
        rows = pl.program_id(0) * r + lax.broadcasted_iota(jnp.int32, (r, n), 0)
        valid = jnp.logical_and(rows >= N_META, rows < n_valid)
        e = jnp.where(valid, h_ref[...] - t_ref[...], 0.0)
        dy = e * (1.0 / n)
        dy_ref[...] = dy
        dy16_ref[...] = dy.astype(BF16)
        _accumulate(acc_ref, jnp.sum(e * e, axis=0, keepdims=True))

    return _rows_call("loss", body, [h2, tgt], [], [(n, F32), (n, BF16)], [(1, n)], r)


def _adamw_update(w, g, m, v):
    m2 = ADAM_B1 * m + (1.0 - ADAM_B1) * g
    v2 = ADAM_B2 * v + (1.0 - ADAM_B2) * (g * g)
    m_hat = m2 / (1.0 - ADAM_B1 ** ADAM_STEP)
    v_hat = v2 / (1.0 - ADAM_B2 ** ADAM_STEP)
    return -ADAM_LR * (m_hat / (jnp.sqrt(v_hat) + ADAM_EPS) + ADAM_WD * w), m2, v2


def _adamw_small(ws, gs, ms, vs):
    n = len(ws)

    def body(*refs):
        for i in range(n):
            d, m2, v2 = _adamw_update(refs[i][...], refs[n + i][...], refs[2 * n + i][...], refs[3 * n + i][...])
            refs[4 * n + i][...] = d
            refs[5 * n + i][...] = m2
            refs[6 * n + i][...] = v2

    shapes = [jax.ShapeDtypeStruct(a.shape, F32) for a in ws]
    outs = pl.pallas_call(body, name="adamw_small", out_shape=shapes * 3,
                          compiler_params=pltpu.CompilerParams(vmem_limit_bytes=VMEM_LIMIT))(*ws, *gs, *ms, *vs)
    return outs[:n], outs[n:2 * n], outs[2 * n:]


def _adamw_call(name, w, g, m, v, host=None):
    rows, cols = w.shape
    by_rows = rows % 8 == 0

    def body(w_ref, g_ref, m_ref, v_ref, g_out, d_ref, m_out, v_out):
        gv = g_ref[...] if by_rows else g_ref[0:rows, :]
        g_out[...] = gv
        d_ref[...], m_out[...], v_out[...] = _adamw_update(w_ref[...], gv, m_ref[...], v_ref[...])

    if by_rows:
        tr = _tile(rows, 256, 8)
        spec = g_spec = pl.BlockSpec((tr, cols), lambda i: (i, 0))
        grid = (rows // tr,)
    else:
        tc = _tile(cols, 256, 128)
        spec = pl.BlockSpec((rows, tc), lambda j: (0, j))
        g_spec = pl.BlockSpec((g.shape[0], tc), lambda j: (0, j))
        grid = (cols // tc,)
    return _pcall(body, name, grid, [spec, g_spec, spec, spec], [spec] * 4, [jax.ShapeDtypeStruct((rows, cols), F32)] * 4,
                  [w, g, m, v], ("arbitrary",), host=host)


def _rope_tables(t):
    half = ROPE // 2
    inv_freq = np.float32(ROPE_THETA) ** (-np.arange(half, dtype=np.float32) / np.float32(half))
    ang = np.arange(t, dtype=np.float32)[:, None] * inv_freq[None, :].astype(np.float32)
    z = np.zeros((t, HEAD - ROPE), np.float32)
    cos = np.concatenate([np.cos(ang), np.cos(ang), z], axis=1).astype(np.float32)
    sin = np.concatenate([np.sin(ang), np.sin(ang), z], axis=1).astype(np.float32)
    k = np.arange(HEAD)[:, None]
    l = np.arange(HEAD)[None, :]
    perm = np.where((l < half) & (k == l + half), -1.0, 0.0) + np.where((l >= half) & (l < ROPE) & (k == l - half), 1.0, 0.0)
    return jnp.asarray(cos), jnp.asarray(sin), jnp.asarray(perm.astype(np.float32))


def _win_to_pad(w):
    z = lambda n: jnp.zeros((n, w.shape[1]), w.dtype)
    return jnp.concatenate([w[576:2112], w[2112:2624], w[0:256], w[256:512], w[512:576], z(64), w[2624:2632], z(120)],
                           axis=0)


def _win_from_pad(g):
    return jnp.concatenate([g[2048:2304], g[2304:2560], g[2560:2624], g[0:1536], g[1536:2048], g[2688:2696]], axis=0)


def _qk_to_pad(w):
    w4 = w.reshape(HEADS, QK_DIM, w.shape[-1])
    return jnp.concatenate([w4, jnp.zeros((HEADS, QK_PAD - QK_DIM, w.shape[-1]), w.dtype)], axis=1).reshape(
        HEADS * QK_PAD, w.shape[-1])


def _qk_from_pad(g):
    return g.reshape(HEADS, QK_PAD, g.shape[-1])[:, :QK_DIM].reshape(HEADS * QK_DIM, g.shape[-1])


def _ff_to_pad(a, axis):
    shape = list(a.shape)
    shape[axis:axis + 1] = [N_CHIPS, FF_SHARD]
    a4 = a.reshape(shape)
    shape[axis + 1] = FF_BLOCK - FF_SHARD
    out = jnp.concatenate([a4, jnp.zeros(shape, a.dtype)], axis=axis + 1)
    shape[axis:axis + 2] = [D_FF_P]
    return out.reshape(shape)


def _ff_from_pad(a, axis):
    shape = list(a.shape)
    shape[axis:axis + 1] = [N_CHIPS, FF_BLOCK]
    a4 = lax.slice_in_dim(a.reshape(shape), 0, FF_SHARD, axis=axis + 1)
    shape[axis:axis + 2] = [D_FF]
    return a4.reshape(shape)


class _LocalPlan:
    def __init__(self, wt):
        self.wt, self.grads = wt, {}

    def weight(self, name):
        return self.wt[name]

    def host(self, point):
        return None

    def grad(self, name, value):
        self.grads[name] = value


def _local_step(x, tgt, wt, plan=None):
    plan = _LocalPlan(wt) if plan is None else plan
    s = x.shape[0]
    n_valid = N_META + s
    t = -(-n_valid // HEAD) * HEAD
    zpad = jnp.zeros((t - n_valid, D_MODEL), F32)
    h0 = jnp.concatenate([wt["meta_tokens"], x, zpad], axis=0)
    tgt_p = jnp.concatenate([jnp.zeros((N_META, D_MODEL), F32), tgt, zpad], axis=0)
    cos, sin, perm = _rope_tables(t)
    qn_w = jnp.concatenate([wt["q_norm_w"], jnp.zeros((1, QK_PAD - QK_DIM), F32)], axis=1)
    kn_w = jnp.concatenate([wt["k_norm_w"], jnp.zeros((1, QK_PAD - QK_DIM), F32)], axis=1)
    head_id = jnp.arange(DN_WIDTH)[None, :] // HEAD
    lane = jnp.arange(HEAD)[:, None]
    sel_a = (lane == head_id).astype(F32)
    sel_b = (lane == head_id + HEADS).astype(F32)
    alog = jnp.repeat(wt["dn_A_log"], HEAD, axis=1)
    dtb = jnp.repeat(wt["dn_dt_bias"], HEAD, axis=1)
    conv_w, conv_b = wt["ffn_conv_w"], wt["ffn_conv_b"]

    u = _rms_fwd("attn_norm_fwd", h0, wt["attn_norm_w"], host=plan.host("attn_norm_fwd"))
    win, wq, wkv = plan.weight("w_in_t"), plan.weight("w_q_t"), plan.weight("w_kv_t")
    proj = _matmul("in_proj", u, win, "nt", F32)
    z = (proj, DN_WIDTH, 3)
    q_lat, kv_lat, k_pe, ab = (proj, LORA, 8), (proj, LORA, 9), (proj, HEAD, 20), (proj, HEAD, 21)
    mla_consts = (wt["q_a_norm_w"], wq, wt["kv_a_norm_w"], wkv, qn_w, kn_w, perm)
    q, k, v = _mla_prep_fwd(q_lat, kv_lat, k_pe, cos, sin, *mla_consts)
    o_mla = _attn_fwd(q, k, v, host=plan.host("attn_fwd"))
    conv = _dn_conv_fwd(proj, wt["dn_conv_w"])
    dn_consts = (sel_a, sel_b, alog, dtb)
    qn, kn, g, beta = _dn_prep_fwd(conv, ab, *dn_consts)
    n_mat, b_mat, q_eff, o_own, eg = _dn_chunk_fwd(qn, kn, conv, g, beta, host=plan.host("dn_chunk_fwd"))
    sall = _dn_rec_fwd(n_mat, b_mat, eg)
    o_dn = _dn_o_fwd(sall, q_eff, o_own)
    mixed = _mix_out_fwd(o_mla, o_dn, z, wt["mla_out_norm_w"], wt["dn_out_norm_w"])
    w_out = plan.weight("w_out")
    h1 = _matmul("out_proj", mixed, w_out, "nn", F32, res=h0)
    n2 = _rms_fwd("ffn_norm_fwd", h1, wt["ffn_norm_w"])
    w_gate, w_up = plan.weight("w_gate_t"), plan.weight("w_up_t")
    gpre = _matmul("gate_proj", n2, w_gate, "nt", BF16, host=plan.host("gate_proj"))
    up = _matmul("up_proj", n2, w_up, "nt", BF16, host=plan.host("up_proj"))
    act = _glu_fwd(gpre, up, conv_w, conv_b)
    w_down = plan.weight("w_down")
    h2 = _matmul("down_proj", act, w_down, "nn", F32, res=h1)
    dy, dy16, sq = _loss_call(h2, tgt_p, n_valid)

    grads = {}
    dact = _matmul("down_dx", dy16, w_down, "nt", BF16)
    plan.grad("w_down", _matmul("down_dw", act, dy16, "tn", BF16))
    dgpre, dup, grads["ffn_conv_w"], grads["ffn_conv_b"] = _glu_bwd(gpre, up, conv_w, conv_b, dact)
    plan.grad("w_gate_t", _matmul("gate_dw", dgpre, n2, "tn", BF16))
    plan.grad("w_up_t", _matmul("up_dw", dup, n2, "tn", BF16))
    dn2a = _matmul("gate_dx", dgpre, w_gate, "nn", BF16, host=plan.host("gate_dx"))
    dn2b = _matmul("up_dx", dup, w_up, "nn", BF16, host=plan.host("up_dx"))
    dh1, dh1_16, grads["ffn_norm_w"] = _rms_bwd("ffn_norm_bwd", h1, wt["ffn_norm_w"], [dn2a, dn2b], dy)
    dmixed = _matmul("out_dx", dh1_16, w_out, "nt", BF16)
    plan.grad("w_out", _matmul("out_dw", mixed, dh1_16, "tn", BF16))
    do_mla, do_dn, dz, grads["mla_out_norm_w"], grads["dn_out_norm_w"] = _mix_out_bwd(
        o_mla, o_dn, z, dmixed, wt["mla_out_norm_w"], wt["dn_out_norm_w"], host=plan.host("mix_out_bwd"))
    dq_eff, ds_out = _dn_o_bwd(sall, q_eff, do_dn)
    gall = _dn_rec_bwd(n_mat, eg, ds_out)
    dqn, dkn, dv_dn, dg, dbeta = _dn_chunk_bwd(qn, kn, conv, g, beta, sall, gall, dq_eff, do_dn,
                                               host=plan.host("dn_chunk_bwd"))
    dconv, dab, dalog, ddtb = _dn_prep_bwd(conv, ab, dqn, dkn, dv_dn, dg, dbeta, *dn_consts)
    grads["dn_A_log"] = jnp.sum(dalog.reshape(HEADS, HEAD), axis=1)[None, :]
    grads["dn_dt_bias"] = jnp.sum(ddtb.reshape(HEADS, HEAD), axis=1)[None, :]
    ddn_pre, grads["dn_conv_w"] = _dn_conv_bwd(proj, wt["dn_conv_w"], dconv)
    dq, dk, dv = _attn_bwd(q, k, v, do_mla, host=plan.host("attn_bwd"))
    dq_lat, dkv_lat, dk_pe, dqa, dwq, dkva, dwkv, dqnw, dknw = _mla_prep_bwd(
        q_lat, kv_lat, k_pe, cos, sin, dq, dk, dv, *mla_consts, host=plan.host("mla_prep_bwd"))
    grads["q_a_norm_w"], grads["kv_a_norm_w"] = dqa, dkva
    plan.grad("w_q_t", dwq)
    plan.grad("w_kv_t", dwkv)
    grads["q_norm_w"], grads["k_norm_w"] = dqnw[:, :QK_DIM], dknw[:, :QK_DIM]
    dproj = jnp.concatenate([ddn_pre, dz, dq_lat, dkv_lat, dk_pe, dab], axis=1)
    plan.grad("w_in_t", _matmul("in_dw", dproj, u, "tn", F32))
    du = _matmul("in_dx", dproj, win, "nn", BF16, host=plan.host("in_dx"))
    dh0, _, grads["attn_norm_w"] = _rms_bwd("attn_norm_bwd", h0, wt["attn_norm_w"], [du], dh1,
                                            host=plan.host("attn_norm_bwd"))
    grads["meta_tokens"] = dh0[0:N_META]
    if isinstance(plan, _LocalPlan):
        grads.update(plan.grads)
    return sq, dh0[N_META:n_valid], grads


def _mesh_pos():
    return lax.axis_index("x"), lax.axis_index("y"), lax.axis_index("c")


def _other_chips(x, y):
    return [(1 - x, y), (x, 1 - y), (1 - x, 1 - y)]


def _remote(src, dst, send_sems, recv_sems, k, to):
    return pltpu.make_async_remote_copy(src_ref=src, dst_ref=dst, send_sem=send_sems.at[k], recv_sem=recv_sems.at[k],
                                        device_id=to, device_id_type=MESH)


SIBLING_ID, CHIPS_ID, GATHER_ID, ALL_ID = 1, 2, 3, 4


def _sibling_peer():
    x, y, c = _mesh_pos()
    return [(x, y, 1 - c)]


def _chip_peers():
    x, y, c = _mesh_pos()
    return [(qx, qy, c) for qx, qy in _other_chips(x, y)]


def _copies_exchange(make, ins, out_shape, nsem, peers=None, cid=None):
    def prog(in_refs, out_refs, send_sems, recv_sems):
        copies = make(in_refs, out_refs, send_sems, recv_sems)

        def start():
            for cp in copies:
                cp.start()

        def finish():
            for cp in copies:
                cp.wait()

        return start, finish

    return _Exchange(prog, ins, out_shape, nsem, peers, cid)


def _all_gather(shards):
    def prog(srcs, dsts, send_sems, recv_sems):
        x, y, c = _mesh_pos()
        p = 2 * x + y
        sibling = (x, y, 1 - c)
        chips = _other_chips(x, y)
        bufs = tuple((s, d, s.shape[0] // 2) for s, d in zip(srcs, dsts))

        def half(ref, rows, which):
            return ref.at[pl.ds(which * rows, rows), :]

        def copy(i, k, src, dst, to):
            return _remote(src, dst, send_sems, recv_sems, 6 * i + k, to)

        sends = [copy(i, j, half(src, rows, c), half(dst.at[p], rows, c), (*chip, c))
                 for i, (src, dst, rows) in enumerate(bufs) for j, chip in enumerate(chips)]

        def start():
            for cp in sends:
                cp.start()

        def finish():
            passed = []
            for i, (src, dst, rows) in enumerate(bufs):
                for j, (qx, qy) in enumerate(chips):
                    block = half(dst.at[2 * qx + qy], rows, c)
                    copy(i, j, block, block, (x, y, c)).wait_recv()
                    fwd = copy(i, 3 + j, block, block, sibling)
                    fwd.start()
                    passed.append(fwd)
            for i, (src, dst, rows) in enumerate(bufs):
                for j, (qx, qy) in enumerate(chips):
                    block = half(dst.at[2 * qx + qy], rows, 1 - c)
                    copy(i, 3 + j, block, block, (x, y, c)).wait_recv()
            for cp in sends + passed:
                cp.wait_send()

        return start, finish

    return _Exchange(prog, shards, [jax.ShapeDtypeStruct((N_CHIPS, *s.shape), s.dtype) for s in shards], 6 * len(shards),
                     lambda: _sibling_peer() + _chip_peers(), GATHER_ID)


def _gathered(ex):
    p = 2 * lax.axis_index("x") + lax.axis_index("y")
    return [lax.dynamic_update_slice(g, s[None], (p, 0, 0)) for g, s in zip(ex.outs, ex.ins)]


def _rs_to_sibling(bufs):
    def make(srcs, dsts, send_sems, recv_sems):
        x, y, c = _mesh_pos()
        copies = []
        for i, (src, dst) in enumerate(zip(srcs, dsts)):
            half = src.shape[1] // 2
            copies.append(_remote(src.at[:, pl.ds((1 - c) * half, half), :], dst, send_sems, recv_sems, i, (x, y, 1 - c)))
        return copies

    return _copies_exchange(make, bufs,
                            [jax.ShapeDtypeStruct((N_CHIPS, b.shape[1] // 2, b.shape[2]), b.dtype) for b in bufs],
                            len(bufs), _sibling_peer, SIBLING_ID)


def _rs_pair_add(name, bufs, gots, c, out_dtype):
    n = len(bufs)

    def body(c_ref, *refs):
        for a_ref, b_ref, o_ref in zip(refs[:n], refs[n:2 * n], refs[2 * n:]):
            o_ref[...] = (a_ref[...].astype(F32) + b_ref[...].astype(F32)).astype(out_dtype)

    mine = [pl.BlockSpec((None, g.shape[1], g.shape[2]), lambda j, cr: (j, cr[0], 0)) for g in gots]
    whole = [pl.BlockSpec((None, g.shape[1], g.shape[2]), lambda j, cr: (j, 0, 0)) for g in gots]
    return pl.pallas_call(
        body, name=name,
        grid_spec=pltpu.PrefetchScalarGridSpec(num_scalar_prefetch=1, grid=(N_CHIPS,), in_specs=mine + whole, out_specs=whole),
        out_shape=[jax.ShapeDtypeStruct(g.shape, out_dtype) for g in gots],
        compiler_params=_cparams(("arbitrary",)))(c, *bufs, *gots)


def _rs_to_chips(accs):
    def make(srcs, dsts, send_sems, recv_sems):
        x, y, c = _mesh_pos()
        return [_remote(src.at[2 * qx + qy], dst.at[k], send_sems, recv_sems, 3 * i + k, (qx, qy, c))
                for i, (src, dst) in enumerate(zip(srcs, dsts)) for k, (qx, qy) in enumerate(_other_chips(x, y))]

    return _copies_exchange(make, accs, [jax.ShapeDtypeStruct((3, a.shape[1], a.shape[2]), a.dtype) for a in accs],
                            3 * len(accs), _chip_peers, CHIPS_ID)


def _rs_chip_add(name, accs, gots, p):
    n = len(accs)
    slot = (0, 1, 0, 2)

    def body(p_ref, *refs):
        me = p_ref[0]
        for own_ref, got_ref, o_ref in zip(refs[:n], refs[n:2 * n], refs[2 * n:]):
            total = None
            for chip in range(N_CHIPS):
                val = own_ref[...].astype(F32)
                for e in (1, 2, 3):
                    val = jnp.where((chip ^ me) == e, got_ref[slot[e]].astype(F32), val)
                total = val if total is None else total + val
            o_ref[...] = total

    own = [pl.BlockSpec((None, a.shape[1], a.shape[2]), lambda i, pr: (pr[0], 0, 0)) for a in accs]
    got = [pl.BlockSpec(g.shape, lambda i, pr: (0, 0, 0)) for g in gots]
    out = [pl.BlockSpec((a.shape[1], a.shape[2]), lambda i, pr: (0, 0)) for a in accs]
    return pl.pallas_call(
        body, name=name,
        grid_spec=pltpu.PrefetchScalarGridSpec(num_scalar_prefetch=1, grid=(1,), in_specs=own + got, out_specs=out),
        out_shape=[jax.ShapeDtypeStruct((a.shape[1], a.shape[2]), F32) for a in accs],
        compiler_params=_cparams(("arbitrary",)))(p, *accs, *gots)


def _rs_share(ress):
    def make(srcs, dsts, send_sems, recv_sems):
        x, y, c = _mesh_pos()
        return [_remote(src, dst, send_sems, recv_sems, i, (x, y, 1 - c)) for i, (src, dst) in enumerate(zip(srcs, dsts))]

    return _copies_exchange(make, ress, [jax.ShapeDtypeStruct(r.shape, F32) for r in ress], len(ress), _sibling_peer,
                            SIBLING_ID)


def _shared(ex):
    south = lax.axis_index("c") == 0
    return [jnp.concatenate([jnp.where(south, r, g), jnp.where(south, g, r)], axis=0) for r, g in zip(ex.ins, ex.outs)]


def _all_to_all_devices(vec):
    def others():
        x, y, c = _mesh_pos()
        return [((1 - x if r & 4 else x), (1 - y if r & 2 else y), (1 - c if r & 1 else c)) for r in range(1, 8)]

    def make(srcs, dsts, send_sems, recv_sems):
        x, y, c = _mesh_pos()
        me = 4 * x + 2 * y + c
        return [_remote(srcs[0], dsts[0].at[me], send_sems, recv_sems, r, peer) for r, peer in enumerate(others())]

    return _copies_exchange(make, [vec], [jax.ShapeDtypeStruct((8, *vec.shape), vec.dtype)], 7, others, ALL_ID)


def _sum_devices(stack):
    def body(s_ref, o_ref):
        total = s_ref[0]
        for d in range(1, 8):
            total = total + s_ref[d]
        o_ref[...] = total

    return pl.pallas_call(body, name="sum_devices", out_shape=jax.ShapeDtypeStruct(stack.shape[1:], F32),
                          compiler_params=pltpu.CompilerParams(vmem_limit_bytes=VMEM_LIMIT))(stack)


def _pad_rows(flat, rows):
    return jnp.concatenate([flat, jnp.zeros((rows * LANES - flat.shape[0],), flat.dtype)]).reshape(rows, LANES)


def _unshard(g4, shape, axis):
    a = g4.reshape(N_CHIPS, *shape)
    if axis == 0:
        return a.reshape(N_CHIPS * shape[0], shape[1])
    return jnp.transpose(a, (1, 0, 2)).reshape(shape[0], N_CHIPS * shape[1])


def _shard4(full, shape, axis):
    if axis == 0:
        return full.reshape(N_CHIPS, shape[0] * shape[1])
    a = full.reshape(shape[0], N_CHIPS, shape[1])
    return jnp.transpose(a, (1, 0, 2)).reshape(N_CHIPS, shape[0] * shape[1])


def _pad_axis0(a, rows):
    return jnp.concatenate([a, jnp.zeros((rows - a.shape[0], *a.shape[1:]), a.dtype)], axis=0)


def _pad_axis1(a, rows):
    return jnp.concatenate([a, jnp.zeros((a.shape[0], rows - a.shape[1], *a.shape[2:]), a.dtype)], axis=1)


def _shard_to_strip(name, w):
    _, (shape, axis, rows) = name, {n: (s, ax, r) for n, s, ax, r in BIG}[name]
    w2 = w.reshape(shape).astype(BF16)
    if name == "w_in":
        return w2
    return _pad_axis0(w2.T if axis == 1 else w2, rows)


LOCAL_NAME = dict(w_in="w_in_t", w_q_b="w_q_t", w_kv_b="w_kv_t", w_out="w_out", w_gate="w_gate_t", w_up="w_up_t",
                  w_down="w_down")


WIN_SEGMENTS = ((576, 2112, 0), (2112, 2624, 1536), (0, 256, 2048), (256, 512, 2304), (512, 576, 2560), (2624, 2632, 2688))


def _strips_to_weight(name, g4):
    if name == "w_in":
        return _win_to_pad(jnp.transpose(g4, (0, 2, 1)).reshape(IN_COLS, D_MODEL))
    if name == "w_q_b":
        return _qk_to_pad(g4.reshape(HEADS * QK_DIM, LORA))
    return g4.reshape(N_CHIPS * g4.shape[1], g4.shape[2])


def _grad_to_strips(name, g):
    if name == "w_in":
        strips = []
        for q in range(N_CHIPS):
            pieces = []
            for a, b, local in sorted(WIN_SEGMENTS):
                s, e = max(a, q * IN_SHARD), min(b, (q + 1) * IN_SHARD)
                if s < e:
                    pieces.append(g[local + s - a:local + e - a])
            pieces.append(jnp.zeros((IN_SHARD_P - IN_SHARD, D_MODEL), g.dtype))
            strips.append(jnp.concatenate(pieces, axis=0))
        return jnp.stack(strips)
    if name == "w_q_b":
        return _qk_from_pad(g).reshape(N_CHIPS, QK_DIM, LORA)
    return g.reshape(N_CHIPS, g.shape[0] // N_CHIPS, g.shape[1])


class _MeshPlan:
    LATE = dict(attn_norm_fwd=("w_in", "w_q_b", "w_kv_b"), attn_fwd=("w_up",), dn_chunk_fwd=("w_out", "w_gate"),
                gate_proj=("w_down/0",), up_proj=("w_down/1",))
    GROUP_A = ("w_down", "w_gate", "w_up", "w_out")
    GROUP_B = ("w_in", "w_q_b", "w_kv_b")

    def __init__(self, w):
        x, y, c = _mesh_pos()
        self.ci = jnp.reshape(c, (1,)).astype(jnp.int32)
        self.pi = jnp.reshape(2 * x + y, (1,)).astype(jnp.int32)
        self.strip = {n: _shard_to_strip(n, w[n]) for n, _, _, _ in BIG}
        self.gathers, self.weights, self.g, self.acc, self.reduced = {}, {}, {}, {}, {}
        self.sibs, self.sib, self.chip, self.share, self.halves = [], None, None, None, [None, None]

    def gather_small(self, small):
        ex = _all_gather([small])
        ex.run("all_gather_small")
        return _gathered(ex)[0]

    def weight(self, local_name):
        if local_name not in self.weights:
            for point, (names, ex) in list(self.gathers.items()):
                if ex.outs is not None:
                    for n, g4 in zip(names, _gathered(ex)):
                        if "/" in n:
                            n, half = n.split("/")
                            self.halves[int(half)] = g4
                            if None in self.halves:
                                continue
                            g4 = jnp.concatenate(self.halves, axis=1)
                        self.weights[LOCAL_NAME[n]] = _strips_to_weight(n, g4)
                    del self.gathers[point]
        return self.weights[local_name]

    def _shard(self, name):
        if "/" not in name:
            return self.strip[name]
        name, half = name.split("/")
        rows = self.strip[name].shape[0] // 2
        return self.strip[name][int(half) * rows:(int(half) + 1) * rows]

    def grad(self, local_name, value):
        name = {v: k for k, v in LOCAL_NAME.items()}[local_name]
        self.g[name] = _grad_to_strips(name, value)

    def _pair_add(self, names, gots):
        accs = _rs_pair_add("rs_pair_add_" + names[0], [self.g[n] for n in names], gots, self.ci, BF16)
        self.acc.update(zip(names, accs))

    def _chip_add(self, names, chip):
        return _rs_chip_add("rs_chip_add_" + names[0], [self.acc[n] for n in names], chip.outs, self.pi)

    def _take_shared(self, names, share):
        for n, strip in zip(names, _shared(share)):
            self.reduced[n] = strip

    def host(self, point):
        a, b = self.GROUP_A, self.GROUP_B
        if point in self.LATE:
            names = self.LATE[point]
            ex = _all_gather([self._shard(n) for n in names])
            self.gathers[point] = (names, ex)
            return ex
        if point in ("gate_dx", "up_dx", "mix_out_bwd"):
            names = dict(gate_dx=a[:2], up_dx=a[2:3], mix_out_bwd=a[3:])[point]
            ex = _rs_to_sibling([self.g[n] for n in names])
            self.sibs.append(ex)
            return ex
        if point == "dn_chunk_bwd":
            self._pair_add(a, [o for ex in self.sibs for o in ex.outs])
            self.chip1 = _rs_to_chips([self.acc[n] for n in a[:2]])
            return self.chip1
        if point == "attn_bwd":
            self.chip2 = _rs_to_chips([self.acc[n] for n in a[2:]])
            return self.chip2
        if point == "mla_prep_bwd":
            ress = self._chip_add(a[:2], self.chip1) + self._chip_add(a[2:], self.chip2)
            self.share = _rs_share(ress)
            return self.share
        if point == "in_dx":
            self._take_shared(a, self.share)
            self.sib = _rs_to_sibling([self.g[n] for n in b])
            return self.sib
        if point == "attn_norm_bwd":
            self._pair_add(b, self.sib.outs)
            self.chip = _rs_to_chips([self.acc[n] for n in b])
            return self.chip
        return None

    def last_share(self):
        self.share = _rs_share(self._chip_add(self.GROUP_B, self.chip))
        return self.share

    def finish(self):
        self._take_shared(self.GROUP_B, self.share)
        return self.reduced


def _strip_to_shard(name, strip):
    shape, axis = {n: (s, ax) for n, s, ax, _ in BIG}[name]
    rows = shape[axis]
    return strip[:rows].T if axis == 1 else strip[:rows]


def kernel(x, meta_tokens, attn_norm_w, w_in, q_a_norm_w, w_q_b, kv_a_norm_w, w_kv_b, q_norm_w, k_norm_w, mla_out_norm_w, dn_conv_w, dn_A_log, dn_dt_bias, dn_out_norm_w, w_out, ffn_norm_w, w_gate, w_up, ffn_conv_w, ffn_conv_b, w_down, loss_target, m_meta_tokens, m_attn_norm_w, m_w_in, m_q_a_norm_w, m_w_q_b, m_kv_a_norm_w, m_w_kv_b, m_q_norm_w, m_k_norm_w, m_mla_out_norm_w, m_dn_conv_w, m_dn_A_log, m_dn_dt_bias, m_dn_out_norm_w, m_w_out, m_ffn_norm_w, m_w_gate, m_w_up, m_ffn_conv_w, m_ffn_conv_b, m_w_down, v_meta_tokens, v_attn_norm_w, v_w_in, v_q_a_norm_w, v_w_q_b, v_kv_a_norm_w, v_w_kv_b, v_q_norm_w, v_k_norm_w, v_mla_out_norm_w, v_dn_conv_w, v_dn_A_log, v_dn_dt_bias, v_dn_out_norm_w, v_w_out, v_ffn_norm_w, v_w_gate, v_w_up, v_ffn_conv_w, v_ffn_conv_b, v_w_down):
    local = dict(locals())
    w = {n: local[n] for n in WEIGHTS}
    m = {n: local["m_" + n] for n in WEIGHTS}
    v = {n: local["v_" + n] for n in WEIGHTS}
    p = 2 * lax.axis_index("x") + lax.axis_index("y")

    plan = _MeshPlan(w)
    wf = _pad_rows(jnp.concatenate([w[n].reshape(-1) for n, _, _ in SMALL_SHARDED]), SMALL_ROWS)
    gf = plan.gather_small(wf).reshape(N_CHIPS, -1)
    full = {}
    off = 0
    for n, s, ax in SMALL_SHARDED:
        full[n] = _unshard(gf[:, off:off + s[0] * s[1]], s, ax)
        off += s[0] * s[1]
    for n, _ in REPLICATED:
        full[n] = w[n]
    full["ffn_conv_w"] = _ff_to_pad(full["ffn_conv_w"], 1)
    full["ffn_conv_b"] = _ff_to_pad(full["ffn_conv_b"], 1)

    sq, grad_x, g = _local_step(x[0], loss_target[0], full, plan)
    g["ffn_conv_w"] = _ff_from_pad(g["ffn_conv_w"], 1)
    g["ffn_conv_b"] = _ff_from_pad(g["ffn_conv_b"], 1)

    small_all = [n for n, _, _ in SMALL_SHARDED] + [n for n, _ in REPLICATED]
    vec = jnp.concatenate([g[n].reshape(-1) for n in small_all] + [jnp.reshape(0.5 / D_MODEL * jnp.sum(sq), (1,))])
    vec = _pad_rows(vec, -(-vec.shape[0] // (8 * LANES)) * 8)
    a2a = _all_to_all_devices(vec)

    gs, delta, new_m, new_v = {}, {}, {}, {}
    big = {n: (s, ax) for n, s, ax, _ in BIG}

    def adamw_big(n, strips, host=None):
        s, ax = big[n]
        flip = ax == 1 and s[1] % 8 == 0
        there = (lambda a: a.reshape(s).T) if flip else (lambda a: a.reshape(s))
        back = (lambda a: a.T.reshape(w[n].shape)) if flip else (lambda a: a.reshape(w[n].shape))
        strip = strips[n] if flip or ax == 0 else strips[n][:s[1]].T
        g2, d2, m2, v2 = _adamw_call("adamw_" + n, there(w[n]), strip, there(m[n]), there(v[n]), host=host)
        gs[n], delta[n], new_m[n], new_v[n] = back(g2), back(d2), back(m2), back(v2)

    adamw_big("w_down", plan.reduced, host=a2a)
    adamw_big("w_gate", plan.reduced, host=plan.last_share())
    adamw_big("w_up", plan.reduced)
    adamw_big("w_out", plan.reduced)
    strips = plan.finish()
    for n in plan.GROUP_B:
        adamw_big(n, strips)
    me = 4 * lax.axis_index("x") + 2 * lax.axis_index("y") + lax.axis_index("c")
    red = _sum_devices(lax.dynamic_update_slice(a2a.outs[0], vec[None], (me, 0, 0))).reshape(-1)
    off = 0
    for n in small_all:
        tot = red[off:off + g[n].size].reshape(g[n].shape)
        off += g[n].size
        shard = {sn: (s, ax) for sn, s, ax in SMALL_SHARDED}.get(n)
        if shard is not None:
            tot = lax.dynamic_slice_in_dim(tot, p * shard[0][1], shard[0][1], axis=1)
        gs[n] = tot
    loss = red[off]
    two_d = lambda a: a.reshape(a.shape[-2], a.shape[-1])
    outs = _adamw_small([two_d(w[n]) for n in small_all], [two_d(gs[n]) for n in small_all],
                        [two_d(m[n]) for n in small_all], [two_d(v[n]) for n in small_all])
    for i, n in enumerate(small_all):
        for dst, src in ((delta, outs[0]), (new_m, outs[1]), (new_v, outs[2])):
            dst[n] = src[i].reshape(w[n].shape)

    grad_out = [gs[n].reshape(w[n].shape) for n in WEIGHTS]
    return (loss, grad_x[None], *grad_out, *[delta[n] for n in WEIGHTS], *[new_m[n] for n in WEIGHTS],
            *[new_v[n] for n in WEIGHTS])
```

```python
import functools
import math

import jax
import jax.numpy as jnp
import numpy as np
from jax import lax
from jax.experimental import pallas as pl
from jax.experimental.pallas import tpu as pltpu

F32 = jnp.float32
BF16 = jnp.bfloat16
HI = lax.Precision.HIGHEST
MESH = pl.DeviceIdType.MESH

N_META = 16
D_MODEL = 1024
HEADS = 4
HEAD = 128
ROPE = 64
QK_DIM = HEAD + ROPE
QK_PAD = 2 * HEAD
LORA = 256
DN_WIDTH = HEADS * HEAD
CHUNK = 64
D_FF = 2816
N_CHIPS = 4
FF_SHARD = D_FF // N_CHIPS
FF_BLOCK = 768
D_FF_P = N_CHIPS * FF_BLOCK
IN_COLS = 2632
IN_SHARD = IN_COLS // N_CHIPS
IN_SHARD_P = 672
IN_PAD = 2816
NORM_EPS = 1e-6
ROPE_THETA = 10000.0
LANES = 512

ADAM_LR, ADAM_B1, ADAM_B2, ADAM_EPS, ADAM_WD, ADAM_STEP = 0.001, 0.9, 0.999, 1e-08, 0.01, 10

VMEM_LIMIT = 56 * 1024 * 1024

BIG = (("w_in", (1024, 658), 1, IN_SHARD_P), ("w_q_b", (256, 192), 1, 192), ("w_kv_b", (256, 256), 1, 256),
       ("w_out", (256, 1024), 0, 256), ("w_gate", (1024, 704), 1, FF_BLOCK), ("w_up", (1024, 704), 1, FF_BLOCK),
       ("w_down", (704, 1024), 0, FF_BLOCK))
SMALL_SHARDED = (("meta_tokens", (16, 256), 1), ("dn_conv_w", (4, 384), 1), ("ffn_conv_w", (3, 704), 1))
REPLICATED = (("attn_norm_w", 1024), ("q_a_norm_w", 256), ("kv_a_norm_w", 256), ("q_norm_w", 192), ("k_norm_w", 192),
              ("mla_out_norm_w", 128), ("dn_A_log", 4), ("dn_dt_bias", 4), ("dn_out_norm_w", 128), ("ffn_norm_w", 1024),
              ("ffn_conv_b", 2816))
WEIGHTS = ("meta_tokens", "attn_norm_w", "w_in", "q_a_norm_w", "w_q_b", "kv_a_norm_w", "w_kv_b", "q_norm_w", "k_norm_w",
           "mla_out_norm_w", "dn_conv_w", "dn_A_log", "dn_dt_bias", "dn_out_norm_w", "w_out", "ffn_norm_w", "w_gate",
           "w_up", "ffn_conv_w", "ffn_conv_b", "w_down")

SMALL_ROWS = 16
REP_ROWS = 16


def _cparams(sem):
    return pltpu.CompilerParams(dimension_semantics=sem, vmem_limit_bytes=VMEM_LIMIT)


class _Exchange:
    def __init__(self, prog, ins, out_shape, nsem, peers=None, cid=None):
        self.prog, self.ins, self.out_shape, self.nsem = prog, list(ins), list(out_shape), nsem
        self.peers, self.cid = peers, cid
        self.outs = None

    def sems(self):
        return [pltpu.SemaphoreType.DMA((self.nsem,)), pltpu.SemaphoreType.DMA((self.nsem,))]

    def programs(self, in_refs, out_refs, send_sems, recv_sems):
        start, finish = self.prog(in_refs, out_refs, send_sems, recv_sems)
        if self.cid is None:
            return start, finish
        peers = self.peers()

        def shake_and_start():
            barrier = pltpu.get_barrier_semaphore()
            for peer in peers:
                pl.semaphore_signal(barrier, inc=1, device_id=peer, device_id_type=MESH)
            pl.semaphore_wait(barrier, len(peers))
            start()

        return shake_and_start, finish

    def cparams(self, **kw):
        return pltpu.CompilerParams(has_side_effects=True, collective_id=self.cid, **kw)

    def run(self, name):
        any_spec = pl.BlockSpec(memory_space=pl.ANY)
        n = len(self.ins)

        def body(*refs):
            start, finish = self.programs(refs[:n], refs[n:-2], refs[-2], refs[-1])
            start()
            finish()

        self.outs = pl.pallas_call(
            body, name=name, in_specs=[any_spec] * n, out_specs=[any_spec] * len(self.out_shape),
            out_shape=self.out_shape, scratch_shapes=self.sems(), compiler_params=self.cparams())(*self.ins)
        return self.outs


def _pcall(body, name, grid, in_specs, out_specs, out_shape, args, sem, scratch_shapes=(), host=None):
    single = not isinstance(out_shape, (list, tuple))
    out_specs, out_shape = ([out_specs], [out_shape]) if single else (list(out_specs), list(out_shape))
    if host is None:
        outs = pl.pallas_call(body, name=name, grid=grid, in_specs=list(in_specs), out_specs=out_specs, out_shape=out_shape,
                              scratch_shapes=list(scratch_shapes), compiler_params=_cparams(sem))(*args)
        return outs[0] if single else outs
    any_spec = pl.BlockSpec(memory_space=pl.ANY)
    n_in, n_out, n_scr, nx_in, nx_out = len(in_specs), len(out_specs), len(scratch_shapes), len(host.ins), len(host.out_shape)

    def hosted(*refs):
        c_in, x_in = refs[:n_in], refs[n_in:n_in + nx_in]
        o0 = n_in + nx_in
        c_out, x_out = refs[o0:o0 + n_out], refs[o0 + n_out:o0 + n_out + nx_out]
        s0 = o0 + n_out + nx_out
        start, finish = host.programs(x_in, x_out, refs[s0 + n_scr], refs[s0 + n_scr + 1])
        first = functools.reduce(jnp.logical_and, [pl.program_id(d) == 0 for d in range(len(grid))])
        last = functools.reduce(jnp.logical_and, [pl.program_id(d) == grid[d] - 1 for d in range(len(grid))])
        pl.when(first)(start)
        body(*c_in, *c_out, *refs[s0:s0 + n_scr])
        pl.when(last)(finish)

    outs = pl.pallas_call(
        hosted, name=name, grid=grid, in_specs=list(in_specs) + [any_spec] * nx_in,
        out_specs=out_specs + [any_spec] * nx_out, out_shape=out_shape + host.out_shape,
        scratch_shapes=list(scratch_shapes) + host.sems(),
        compiler_params=host.cparams(dimension_semantics=sem, vmem_limit_bytes=VMEM_LIMIT))(*args, *host.ins)
    host.outs = outs[n_out:]
    return outs[0] if single else outs[:n_out]


NN, NT, TN = ((1,), (0,)), ((1,), (1,)), ((0,), (0,))


def _shift_dims(dims, batch):
    if not batch:
        return (dims, ((), ()))
    return (((dims[0][0] + 1,), (dims[1][0] + 1,)), ((0,), (0,)))


def _make_mm(dims, exact, batch=False):
    def raw(a, b, d):
        dn = _shift_dims(d, batch)
        if exact == "split_lhs":
            ah, bh = a.astype(BF16), b.astype(BF16)
            al = (a - ah.astype(F32)).astype(BF16)
            return lax.dot_general(ah, bh, dn, preferred_element_type=F32) + lax.dot_general(al, bh, dn,
                                                                                              preferred_element_type=F32)
        if exact == "split":
            ah, bh = a.astype(BF16), b.astype(BF16)
            al, bl = (a - ah.astype(F32)).astype(BF16), (b - bh.astype(F32)).astype(BF16)
            dot = lambda p, q: lax.dot_general(p, q, dn, preferred_element_type=F32)
            return dot(ah, bh) + (dot(ah, bl) + dot(al, bh))
        if exact:
            return lax.dot_general(a.astype(F32), b.astype(F32), dn, precision=HI, preferred_element_type=F32)
        return lax.dot_general(a.astype(BF16), b.astype(BF16), dn, preferred_element_type=F32)

    @jax.custom_vjp
    def mm(a, b):
        return raw(a, b, dims)

    def fwd(a, b):
        return raw(a, b, dims), (a, b)

    def bwd(res, g):
        a, b = res
        if dims == NN:
            da, db = raw(g, b, NT), raw(a, g, TN)
        elif dims == NT:
            da, db = raw(g, b, NN), raw(g, a, TN)
        else:
            da, db = raw(b, g, NT), raw(a, g, NN)
        return da.astype(a.dtype), db.astype(b.dtype)

    mm.defvjp(fwd, bwd)
    return mm


_mm = _make_mm(NN, False)
_mm_nt = _make_mm(NT, False)
_mm_tn = _make_mm(TN, False)
_mmx = _make_mm(NN, "split_lhs")
_bmm = _make_mm(NN, False, batch=True)
_bmm_nt = _make_mm(NT, False, batch=True)
_bmm_tn = _make_mm(TN, False, batch=True)
_bmmx = _make_mm(NN, True, batch=True)
_bmms = _make_mm(NN, "split", batch=True)
_bmms_nt = _make_mm(NT, "split", batch=True)
_bmms_tn = _make_mm(TN, "split", batch=True)


@jax.custom_vjp
def _unit_lower_inv(a):
    n = a.shape[-1]
    eye = (lax.broadcasted_iota(jnp.int32, a.shape, 1) == lax.broadcasted_iota(jnp.int32, a.shape, 2)).astype(F32)
    x = -a
    t = eye + x
    for _ in range(max(n.bit_length() - 2, 0)):
        x = _bmms(x, x)
        t = t + _bmms(t, x)
    return t


def _unit_lower_inv_fwd(a):
    t = _unit_lower_inv(a)
    return t, t


def _unit_lower_inv_bwd(t, g):
    return (-_bmms_tn(t, _bmms_nt(g, t)),)


_unit_lower_inv.defvjp(_unit_lower_inv_fwd, _unit_lower_inv_bwd)


def _scan_chunk_rows(x, reverse):
    nb, c, w = x.shape
    y = x.reshape(nb * c, w)
    pos = lax.broadcasted_iota(jnp.int32, y.shape, 0) % c
    step = 1
    while step < c:
        if reverse:
            y = y + jnp.where(pos < c - step, pltpu.roll(y, nb * c - step, 0), 0.0)
        else:
            y = y + jnp.where(pos >= step, pltpu.roll(y, step, 0), 0.0)
        step *= 2
    return y.reshape(nb, c, w)


@jax.custom_vjp
def _chunk_cumsum(x):
    return _scan_chunk_rows(x, False)


_chunk_cumsum.defvjp(lambda x: (_scan_chunk_rows(x, False), None), lambda _, g: (_scan_chunk_rows(g, True),))


def _rms(x, w, n):
    ms = jnp.sum(x * x, axis=-1, keepdims=True) * (1.0 / n)
    return x * lax.rsqrt(ms + NORM_EPS) * w


def _silu(x):
    return x * jax.nn.sigmoid(x)


def _softplus(x):
    return jnp.maximum(x, 0.0) + jnp.log(1.0 + jnp.exp(-jnp.abs(x)))


def _rope(x, cos, sin, perm):
    return x * cos + _mmx(x, perm) * sin


def _mla_prep_fn(rows, consts):
    q_lat, kv_lat, k_pe, cos, sin = rows
    qn = _rms(q_lat, consts["qa_w"], LORA)
    kvn = _rms(kv_lat, consts["kva_w"], LORA)
    outs = []
    for h in range(HEADS):
        q_n = _mm_nt(qn, consts["wq_n"][h])
        q_r = _mm_nt(qn, consts["wq_r"][h])
        rs = lax.rsqrt((jnp.sum(q_n * q_n, -1, keepdims=True) + jnp.sum(q_r * q_r, -1, keepdims=True)) * (1.0 / QK_DIM)
                       + NORM_EPS)
        q_n = q_n * rs * consts["qn_n"]
        q_r = _rope(q_r * rs * consts["qn_r"], cos, sin, consts["perm"])
        k_n = _mm_nt(kvn, consts["wk_n"][h])
        v = _mm_nt(kvn, consts["wv"][h])
        rk = lax.rsqrt((jnp.sum(k_n * k_n, -1, keepdims=True) + jnp.sum(k_pe * k_pe, -1, keepdims=True)) * (1.0 / QK_DIM)
                       + NORM_EPS)
        k_n = k_n * rk * consts["kn_n"]
        k_r = _rope(k_pe * rk * consts["kn_r"], cos, sin, consts["perm"])
        outs += [q_n, q_r, k_n, k_r, v]
    return tuple(outs)


def _attn_fn(q, k, v, row0):
    s = _mm_nt(q, k) * (1.0 / math.sqrt(QK_DIM))
    qpos = row0 + lax.broadcasted_iota(jnp.int32, s.shape, 0)
    kpos = lax.broadcasted_iota(jnp.int32, s.shape, 1)
    s = jnp.where(kpos <= qpos, s, -1e30)
    m = lax.stop_gradient(jnp.max(s, axis=-1, keepdims=True))
    p = jnp.exp(s - m)
    p = p / jnp.sum(p, axis=-1, keepdims=True)
    return _mm(p, v)


def _dn_prep_fn(rows, consts):
    qc, kc, ab = rows
    a_b = _mmx(ab, consts["sel_a"])
    b_b = _mmx(ab, consts["sel_b"])
    beta = jax.nn.sigmoid(b_b)
    g = -jnp.exp(consts["alog"]) * _softplus(a_b + consts["dtb"])
    qs, ks = [], []
    for h in range(HEADS):
        q, k = qc[h], kc[h]
        qs.append(q * lax.rsqrt(jnp.sum(q * q, -1, keepdims=True) + NORM_EPS))
        ks.append(k * lax.rsqrt(jnp.sum(k * k, -1, keepdims=True) + NORM_EPS))
    return tuple(qs), tuple(ks), g, beta


def _dn_chunk_fn(q, k, v, gb, g64, bb):
    nb = q.shape[0]
    ri = lax.broadcasted_iota(jnp.int32, (nb, CHUNK, CHUNK), 1)
    ci = lax.broadcasted_iota(jnp.int32, (nb, CHUNK, CHUNK), 2)
    tri = ri >= ci
    strict = ri > ci
    tril = tri.astype(F32)
    eye = (ri == ci).astype(F32)
    ones = jnp.ones((nb, CHUNK, CHUNK), F32)
    gc = _chunk_cumsum(gb)
    gc64 = _chunk_cumsum(g64)
    grow = _bmmx(ones, eye * gc64)
    diff = gc64 - grow
    decay = jnp.where(tri, jnp.exp(jnp.where(tri, diff, 0.0)), 0.0)
    kb = k * bb
    vb = v * bb
    a = jnp.where(strict, _bmm_nt(kb, k) * decay, 0.0)
    tinv = _unit_lower_inv(a)
    u = _bmm(tinv, vb)
    w = _bmm(tinv, kb * jnp.exp(gc))
    qs = q * (1.0 / math.sqrt(HEAD))
    qk = _bmm_nt(qs, k) * decay
    qg = qs * jnp.exp(gc)
    glast = jnp.sum(gb, axis=1, keepdims=True)
    kdec = k * jnp.exp(glast - gc)
    n_mat = _bmm_tn(kdec, w)
    b_mat = _bmm_tn(kdec, u)
    q_eff = qg - _bmm(qk, w)
    o_own = _bmm(qk, u)
    return n_mat, b_mat, q_eff, o_own, jnp.exp(glast)


def _dn_rec_fn(s, n_mat, b_mat, eg):
    return s * eg - _mm(n_mat, s) + b_mat


def _dn_o_fn(s, q_eff, o_own):
    return _bmm(q_eff, s) + o_own


def _dn_out_fn(o, z, w):
    return _rms(o, w, HEAD) * _silu(z)


def _row_tile(t, parts=8):
    return t // parts if (t // parts) % 16 == 0 else t


def _tile(n, pref, unit):
    best = n
    for cand in range(unit, min(n, pref) + 1, unit):
        if n % cand == 0:
            best = cand
    return best if best <= pref else n


def _rows_call(name, body, rows, consts, outs, accs, r, host=None):
    rows = [a if isinstance(a, tuple) else (a, a.shape[1], 0) for a in rows]
    t = rows[0][0].shape[0]
    zero = lambda nd: (lambda i: (0,) * nd)
    in_specs = [pl.BlockSpec((r, w), functools.partial(lambda i, b: (i, b), b=blk)) for _, w, blk in rows]
    rows = [a for a, _, _ in rows]
    in_specs += [pl.BlockSpec(a.shape, zero(a.ndim)) for a in consts]
    out_shape = [jax.ShapeDtypeStruct((t, w), dt) for w, dt in outs] + [jax.ShapeDtypeStruct(s, F32) for s in accs]
    out_specs = [pl.BlockSpec((r, w), lambda i: (i, 0)) for w, _ in outs] + [pl.BlockSpec(s, zero(len(s))) for s in accs]
    return _pcall(body, name, (t // r,), in_specs, out_specs, out_shape, [*rows, *consts], ("arbitrary",), host=host)


def _accumulate(ref, val):
    @pl.when(pl.program_id(0) == 0)
    def _():
        ref[...] = jnp.zeros(ref.shape, ref.dtype)

    ref[...] += val


def _matmul(name, a, b, dims, out_dtype, res=None, host=None):
    if dims == "nn":
        (m, k), n = a.shape, b.shape[1]
    elif dims == "nt":
        (m, k), n = a.shape, b.shape[0]
    else:
        (k, m), n = a.shape, b.shape[1]
    tm = _tile(m, 1100, 16) if dims != "tn" else _tile(m, 640, 128)
    tn = _tile(n, 1408, 128)
    if dims == "nn":
        a_spec, b_spec, dn = pl.BlockSpec((tm, k), lambda i, j: (i, 0)), pl.BlockSpec((k, tn), lambda i, j: (0, j)), NN
    elif dims == "nt":
        a_spec, b_spec, dn = pl.BlockSpec((tm, k), lambda i, j: (i, 0)), pl.BlockSpec((tn, k), lambda i, j: (j, 0)), NT
    else:
        a_spec, b_spec, dn = pl.BlockSpec((k, tm), lambda i, j: (0, i)), pl.BlockSpec((k, tn), lambda i, j: (0, j)), TN
    o_spec = pl.BlockSpec((tm, tn), lambda i, j: (i, j))

    def body(*refs):
        a_ref, b_ref, o_ref = refs[0], refs[1], refs[-1]
        acc = lax.dot_general(a_ref[...].astype(BF16), b_ref[...].astype(BF16), (dn, ((), ())),
                              preferred_element_type=F32)
        if res is not None:
            acc = acc + refs[2][...]
        o_ref[...] = acc.astype(out_dtype)

    ins = [a, b] + ([res] if res is not None else [])
    specs = [a_spec, b_spec] + ([o_spec] if res is not None else [])
    return _pcall(body, name, (m // tm, n // tn), specs, o_spec, jax.ShapeDtypeStruct((m, n), out_dtype), ins,
                  ("arbitrary", "arbitrary"), host=host)


def _rms_fwd(name, h, w, host=None):
    n = h.shape[1]

    def body(h_ref, w_ref, o_ref):
        o_ref[...] = _rms(h_ref[...], w_ref[...], n).astype(BF16)

    return _rows_call(name, body, [h], [w], [(n, BF16)], [], _row_tile(h.shape[0]), host=host)[0]


def _rms_bwd(name, h, w, cts, resid, host=None):
    n = h.shape[1]
    nct = len(cts)

    def body(*refs):
        h_ref, ct_refs, r_ref, w_ref = refs[0], refs[1:1 + nct], refs[1 + nct], refs[2 + nct]
        dh_ref, dh16_ref, dw_ref = refs[-3], refs[-2], refs[-1]
        ct = ct_refs[0][...].astype(F32)
        for c in ct_refs[1:]:
            ct = ct + c[...].astype(F32)
        _, vjp = jax.vjp(lambda x, ww: _rms(x, ww, n), h_ref[...], w_ref[...])
        dh, dw = vjp(ct)
        dh = dh + r_ref[...]
        dh_ref[...] = dh
        dh16_ref[...] = dh.astype(BF16)
        _accumulate(dw_ref, dw)

    return _rows_call(name, body, [h, *cts, resid], [w], [(n, F32), (n, BF16)], [(1, n)], _row_tile(h.shape[0]), host=host)


def _mla_consts_from_refs(qa, wq, kva, wkv, qn, kn, perm):
    f = lambda r: r[...].astype(F32)
    return dict(
        qa_w=f(qa), kva_w=f(kva), perm=f(perm),
        wq_n=[wq[h * QK_PAD:h * QK_PAD + HEAD, :].astype(F32) for h in range(HEADS)],
        wq_r=[wq[h * QK_PAD + HEAD:(h + 1) * QK_PAD, :].astype(F32) for h in range(HEADS)],
        wk_n=[wkv[h * QK_PAD:h * QK_PAD + HEAD, :].astype(F32) for h in range(HEADS)],
        wv=[wkv[h * QK_PAD + HEAD:(h + 1) * QK_PAD, :].astype(F32) for h in range(HEADS)],
        qn_n=qn[:, 0:HEAD], qn_r=qn[:, HEAD:QK_PAD], kn_n=kn[:, 0:HEAD], kn_r=kn[:, HEAD:QK_PAD])


def _mla_prep_fwd(q_lat, kv_lat, k_pe, cos, sin, qa, wq, kva, wkv, qn, kn, perm):
    def body(ql, kvl, kp, c, s, qa_r, wq_r, kva_r, wkv_r, qn_r, kn_r, p_r, q_out, k_out, v_out):
        consts = _mla_consts_from_refs(qa_r, wq_r, kva_r, wkv_r, qn_r, kn_r, p_r)
        outs = _mla_prep_fn((ql[...], kvl[...], kp[...], c[...], s[...]), consts)
        for h in range(HEADS):
            q_n, q_r, k_n, k_r, v = outs[5 * h:5 * h + 5]
            q_out[:, h * QK_PAD:h * QK_PAD + HEAD] = q_n.astype(BF16)
            q_out[:, h * QK_PAD + HEAD:(h + 1) * QK_PAD] = q_r.astype(BF16)
            k_out[:, h * QK_PAD:h * QK_PAD + HEAD] = k_n.astype(BF16)
            k_out[:, h * QK_PAD + HEAD:(h + 1) * QK_PAD] = k_r.astype(BF16)
            v_out[:, h * HEAD:(h + 1) * HEAD] = v.astype(BF16)

    return _rows_call("mla_prep_fwd", body, [q_lat, kv_lat, k_pe, cos, sin], [qa, wq, kva, wkv, qn, kn, perm],
                      [(HEADS * QK_PAD, BF16), (HEADS * QK_PAD, BF16), (DN_WIDTH, BF16)], [], _row_tile(cos.shape[0], 4))


def _mla_prep_bwd(q_lat, kv_lat, k_pe, cos, sin, dq, dk, dv, qa, wq, kva, wkv, qn, kn, perm, host=None):
    def body(ql, kvl, kp, c, s, dq_r, dk_r, dv_r, qa_r, wq_r, kva_r, wkv_r, qn_r, kn_r, p_r,
             dql, dkvl, dkp, dqa, dwq, dkva, dwkv, dqn, dkn):
        consts = _mla_consts_from_refs(qa_r, wq_r, kva_r, wkv_r, qn_r, kn_r, p_r)
        cc, ss, pm = c[...], s[...], consts.pop("perm")
        _, vjp = jax.vjp(lambda rows, cs: _mla_prep_fn((*rows, cc, ss), dict(cs, perm=pm)), (ql[...], kvl[...], kp[...]),
                         consts)
        cts = []
        for h in range(HEADS):
            cts += [dq_r[:, h * QK_PAD:h * QK_PAD + HEAD], dq_r[:, h * QK_PAD + HEAD:(h + 1) * QK_PAD],
                    dk_r[:, h * QK_PAD:h * QK_PAD + HEAD], dk_r[:, h * QK_PAD + HEAD:(h + 1) * QK_PAD],
                    dv_r[:, h * HEAD:(h + 1) * HEAD]]
        (d_ql, d_kvl, d_kp), dc = vjp(tuple(cts))
        dql[...] = d_ql.astype(BF16)
        dkvl[...] = d_kvl.astype(BF16)
        dkp[...] = d_kp.astype(BF16)
        first = pl.program_id(0) == 0

        def acc(ref, sl, val):
            @pl.when(first)
            def _():
                ref[sl] = val

            @pl.when(jnp.logical_not(first))
            def _():
                ref[sl] += val

        full = (slice(None), slice(None))
        acc(dqa, full, dc["qa_w"])
        acc(dkva, full, dc["kva_w"])
        for h in range(HEADS):
            acc(dwq, (slice(h * QK_PAD, h * QK_PAD + HEAD), slice(None)), dc["wq_n"][h])
            acc(dwq, (slice(h * QK_PAD + HEAD, (h + 1) * QK_PAD), slice(None)), dc["wq_r"][h])
            acc(dwkv, (slice(h * QK_PAD, h * QK_PAD + HEAD), slice(None)), dc["wk_n"][h])
            acc(dwkv, (slice(h * QK_PAD + HEAD, (h + 1) * QK_PAD), slice(None)), dc["wv"][h])
        acc(dqn, (slice(None), slice(0, HEAD)), dc["qn_n"])
        acc(dqn, (slice(None), slice(HEAD, QK_PAD)), dc["qn_r"])
        acc(dkn, (slice(None), slice(0, HEAD)), dc["kn_n"])
        acc(dkn, (slice(None), slice(HEAD, QK_PAD)), dc["kn_r"])

    return _rows_call("mla_prep_bwd", body, [q_lat, kv_lat, k_pe, cos, sin, dq, dk, dv],
                      [qa, wq, kva, wkv, qn, kn, perm],
                      [(LORA, BF16), (LORA, BF16), (HEAD, BF16)],
                      [(1, LORA), wq.shape, (1, LORA), wkv.shape, (1, QK_PAD), (1, QK_PAD)], _row_tile(cos.shape[0], 4),
                      host=host)


ATTN_Q_ROWS = 256


def _attn_blocks(t):
    return [(r0, min(ATTN_Q_ROWS, t - r0)) for r0 in range(0, t, ATTN_Q_ROWS)]


def _attn_fwd(q, k, v, host=None):
    t = q.shape[0]

    def body(q_ref, k_ref, v_ref, o_ref):
        for r0, rows in _attn_blocks(t):
            ext = r0 + rows
            o_ref[r0:ext, :] = _attn_fn(q_ref[r0:ext, :], k_ref[0:ext, :], v_ref[0:ext, :], r0)

    qk_spec = pl.BlockSpec((t, QK_PAD), lambda h: (0, h))
    v_spec = pl.BlockSpec((t, HEAD), lambda h: (0, h))
    return _pcall(body, "attn_fwd", (HEADS,), [qk_spec, qk_spec, v_spec], v_spec,
                  jax.ShapeDtypeStruct((t, HEADS * HEAD), F32), [q, k, v], ("arbitrary",), host=host)


def _attn_bwd(q, k, v, do, host=None):
    t = q.shape[0]

    def body(q_ref, k_ref, v_ref, do_ref, dq_ref, dk_ref, dv_ref):
        dk_ref[...] = jnp.zeros(dk_ref.shape, F32)
        dv_ref[...] = jnp.zeros(dv_ref.shape, F32)
        for r0, rows in _attn_blocks(t):
            ext = r0 + rows
            _, vjp = jax.vjp(functools.partial(_attn_fn, row0=r0), q_ref[r0:ext, :].astype(F32),
                             k_ref[0:ext, :].astype(F32), v_ref[0:ext, :].astype(F32))
            dq, dk, dv = vjp(do_ref[r0:ext, :])
            dq_ref[r0:ext, :] = dq
            dk_ref[0:ext, :] += dk
            dv_ref[0:ext, :] += dv

    qk_spec = pl.BlockSpec((t, QK_PAD), lambda h: (0, h))
    v_spec = pl.BlockSpec((t, HEAD), lambda h: (0, h))
    return _pcall(body, "attn_bwd", (HEADS,), [qk_spec, qk_spec, v_spec, v_spec], [qk_spec, qk_spec, v_spec],
                  [jax.ShapeDtypeStruct((t, HEADS * QK_PAD), F32), jax.ShapeDtypeStruct((t, HEADS * QK_PAD), F32),
                   jax.ShapeDtypeStruct((t, HEADS * HEAD), F32)], [q, k, v, do], ("arbitrary",), host=host)


def _mix_out_proj(o_mla, o_dn, z, w_mla, w_dn, w_out, h0, w_ffn):
    def body(om_ref, od_ref, z_ref, h0_ref, wm_ref, wd_ref, wo_ref, wf_ref, mixed_ref, h1_ref, n2_ref):
        for h in range(HEADS):
            sl = slice(h * HEAD, (h + 1) * HEAD)
            mixed_ref[:, sl] = _rms(om_ref[:, sl], wm_ref[...], HEAD).astype(BF16)
            mixed_ref[:, DN_WIDTH + h * HEAD:DN_WIDTH + (h + 1) * HEAD] = _dn_out_fn(od_ref[:, sl], z_ref[:, sl],
                                                                                     wd_ref[...]).astype(BF16)
        h1 = _mm(mixed_ref[...], wo_ref[...]) + h0_ref[...]
        h1_ref[...] = h1
        n2_ref[...] = _rms(h1, wf_ref[...], D_MODEL).astype(BF16)

    return _rows_call("mix_out_proj", body, [o_mla, o_dn, z, h0], [w_mla, w_dn, w_out, w_ffn],
                      [(D_MODEL, BF16), (D_MODEL, F32), (D_MODEL, BF16)], [], _row_tile(o_mla.shape[0], 4))


def _down_proj_loss(act, w_down, h1, tgt, n_valid):
    t, n = h1.shape
    r = _row_tile(t, 4)

    def body(a_ref, h_ref, t_ref, w_ref, dy_ref, dy16_ref, acc_ref):
        h2 = _mm(a_ref[...], w_ref[...]) + h_ref[...]
        rows = pl.program_id(0) * r + lax.broadcasted_iota(jnp.int32, (r, n), 0)
        valid = jnp.logical_and(rows >= N_META, rows < n_valid)
        e = jnp.where(valid, h2 - t_ref[...], 0.0)
        dy = e * (1.0 / n)
        dy_ref[...] = dy
        dy16_ref[...] = dy.astype(BF16)
        _accumulate(acc_ref, jnp.sum(e * e, axis=0, keepdims=True))

    return _rows_call("down_proj_loss", body, [act, h1, tgt], [w_down], [(n, F32), (n, BF16)], [(1, n)], r)


def _ffn_in_bwd(dgpre, dup, w_gate_t, w_up_t, h1, dy, w_ffn, host=None):
    n = h1.shape[1]

    def body(dg_ref, du_ref, h_ref, dy_ref, wg_ref, wu_ref, w_ref, dh_ref, dh16_ref, dw_ref):
        ct = _mm(dg_ref[...], wg_ref[...]) + _mm(du_ref[...], wu_ref[...])
        _, vjp = jax.vjp(lambda x, ww: _rms(x, ww, n), h_ref[...], w_ref[...])
        dh, dw = vjp(ct)
        dh = dh + dy_ref[...]
        dh_ref[...] = dh
        dh16_ref[...] = dh.astype(BF16)
        _accumulate(dw_ref, dw)

    return _rows_call("ffn_in_bwd", body, [dgpre, dup, h1, dy], [w_gate_t, w_up_t, w_ffn], [(n, F32), (n, BF16)], [(1, n)],
                      _row_tile(h1.shape[0]), host=host)


def _mix_out_bwd(o_mla, o_dn, z, dh1, w_out, w_mla, w_dn, host=None):
    def body(om_ref, od_ref, z_ref, dh_ref, wo_ref, wm_ref, wd_ref, dom_ref, dod_ref, dz_ref, dwm_ref, dwd_ref):
        dwm = dwd = None
        for h in range(HEADS):
            sl = slice(h * HEAD, (h + 1) * HEAD)
            _, vjp = jax.vjp(lambda o, w: _rms(o, w, HEAD), om_ref[:, sl], wm_ref[...])
            do, dw = vjp(_mm_nt(dh_ref[...], wo_ref[sl, :]))
            dom_ref[:, sl] = do
            dwm = dw if dwm is None else dwm + dw
            _, vjp = jax.vjp(_dn_out_fn, od_ref[:, sl], z_ref[:, sl], wd_ref[...])
            do, dz, dw = vjp(_mm_nt(dh_ref[...], wo_ref[DN_WIDTH + h * HEAD:DN_WIDTH + (h + 1) * HEAD, :]))
            dod_ref[:, sl] = do
            dz_ref[:, sl] = dz.astype(BF16)
            dwd = dw if dwd is None else dwd + dw
        _accumulate(dwm_ref, dwm)
        _accumulate(dwd_ref, dwd)

    return _rows_call("mix_out_bwd", body, [o_mla, o_dn, z, dh1], [w_out, w_mla, w_dn],
                      [(DN_WIDTH, F32), (DN_WIDTH, F32), (DN_WIDTH, BF16)], [(1, HEAD), (1, HEAD)],
                      _row_tile(o_mla.shape[0], 4), host=host)


def _shift_down(x, s):
    if s == 0:
        return x
    rows = lax.broadcasted_iota(jnp.int32, x.shape, 0)
    return jnp.where(rows >= s, pltpu.roll(x, s, 0), 0.0)


def _shift_up(x, s):
    if s == 0:
        return x
    t = x.shape[0]
    rows = lax.broadcasted_iota(jnp.int32, x.shape, 0)
    return jnp.where(rows < t - s, pltpu.roll(x, t - s, 0), 0.0)


def _col_call(name, body, cols, taps, outs, tap_outs, cw, host=None):
    t, c = cols[0].shape[0], taps[0].shape[1]
    in_specs = [pl.BlockSpec((t, cw), lambda j: (0, j)) for _ in cols]
    in_specs += [pl.BlockSpec((a.shape[0], cw), lambda j: (0, j)) for a in taps]
    out_shape = [jax.ShapeDtypeStruct((t, c), dt) for dt in outs] + [jax.ShapeDtypeStruct((n, c), F32) for n in tap_outs]
    out_specs = [pl.BlockSpec((t, cw), lambda j: (0, j)) for _ in outs]
    out_specs += [pl.BlockSpec((n, cw), lambda j: (0, j)) for n in tap_outs]
    return _pcall(body, name, (c // cw,), in_specs, out_specs, out_shape, [*cols, *taps], ("arbitrary",), host=host)


def _causal_conv(x, w_ref, width):
    acc = w_ref[width - 1:width, :] * x
    for j in range(width - 1):
        acc = acc + w_ref[j:j + 1, :] * _shift_down(x, width - 1 - j)
    return acc


def _causal_conv_bwd(x, dpre, w_ref, dx_ref, dw_ref, width):
    dx = w_ref[width - 1:width, :] * dpre
    dw_ref[width - 1:width, :] = jnp.sum(dpre * x, axis=0, keepdims=True)
    for j in range(width - 1):
        s = width - 1 - j
        dx = dx + w_ref[j:j + 1, :] * _shift_up(dpre, s)
        dw_ref[j:j + 1, :] = jnp.sum(dpre * _shift_down(x, s), axis=0, keepdims=True)
    dx_ref[...] = dx.astype(dx_ref.dtype)


def _dsilu(x):
    sg = jax.nn.sigmoid(x)
    return sg * (1.0 + x * (1.0 - sg))


def _dn_conv_fwd(x, w):
    def body(x_ref, w_ref, y_ref):
        y_ref[...] = _silu(_causal_conv(x_ref[...], w_ref, 4))

    return _col_call("dn_conv_fwd", body, [x], [w], [F32], [], 256)[0]


def _dn_conv_bwd(x, w, dy):
    def body(x_ref, dy_ref, w_ref, dx_ref, dw_ref):
        xv = x_ref[...]
        dpre = dy_ref[...] * _dsilu(_causal_conv(xv, w_ref, 4))
        _causal_conv_bwd(xv, dpre, w_ref, dx_ref, dw_ref, 4)

    return _col_call("dn_conv_bwd", body, [x, dy], [w], [BF16], [4], 256)


def _glu_fwd(gpre, up, w, b, host=None):
    def body(g_ref, u_ref, w_ref, b_ref, a_ref):
        gate = _causal_conv(g_ref[...].astype(F32), w_ref, 3) + b_ref[...]
        a_ref[...] = (_silu(gate) * u_ref[...].astype(F32)).astype(BF16)

    return _col_call("glu_fwd", body, [gpre, up], [w, b], [BF16], [], 256, host=host)[0]


def _glu_bwd(gpre, up, w, b, dact):
    def body(g_ref, u_ref, da_ref, w_ref, b_ref, dg_ref, du_ref, dw_ref, db_ref):
        gv = g_ref[...].astype(F32)
        gate = _causal_conv(gv, w_ref, 3) + b_ref[...]
        da = da_ref[...].astype(F32)
        sg = jax.nn.sigmoid(gate)
        du_ref[...] = (da * (gate * sg)).astype(BF16)
        dgate = da * u_ref[...].astype(F32) * (sg * (1.0 + gate * (1.0 - sg)))
        db_ref[...] = jnp.sum(dgate, axis=0, keepdims=True)
        _causal_conv_bwd(gv, dgate, w_ref, dg_ref, dw_ref, 3)

    return _col_call("glu_bwd", body, [gpre, up, dact], [w, b], [BF16, BF16], [3, 1], 256)


def _dn_prep_consts(sa, sb, al, dt):
    return dict(sel_a=sa[...], sel_b=sb[...], alog=al[...], dtb=dt[...])


def _dn_prep_fwd(conv, ab, sel_a, sel_b, alog, dtb):
    def body(c_ref, ab_ref, sa, sb, al, dt, q_out, k_out, g_out, b_out):
        qc = tuple(c_ref[:, h * HEAD:(h + 1) * HEAD] for h in range(HEADS))
        kc = tuple(c_ref[:, DN_WIDTH + h * HEAD:DN_WIDTH + (h + 1) * HEAD] for h in range(HEADS))
        qs, ks, g, beta = _dn_prep_fn((qc, kc, ab_ref[...]), _dn_prep_consts(sa, sb, al, dt))
        for h in range(HEADS):
            q_out[:, h * HEAD:(h + 1) * HEAD] = qs[h]
            k_out[:, h * HEAD:(h + 1) * HEAD] = ks[h]
        g_out[...] = g
        b_out[...] = beta

    return _rows_call("dn_prep_fwd", body, [conv, ab], [sel_a, sel_b, alog, dtb], [(DN_WIDTH, F32)] * 4, [],
                      _row_tile(conv.shape[0]))


def _dn_prep_bwd(conv, ab, dq, dk, dv, dg, db, sel_a, sel_b, alog, dtb):
    def body(c_ref, ab_ref, dq_r, dk_r, dv_r, dg_r, db_r, sa, sb, al, dt, dc_out, dab_out, dal_out, ddt_out):
        qc = tuple(c_ref[:, h * HEAD:(h + 1) * HEAD] for h in range(HEADS))
        kc = tuple(c_ref[:, DN_WIDTH + h * HEAD:DN_WIDTH + (h + 1) * HEAD] for h in range(HEADS))
        consts = _dn_prep_consts(sa, sb, al, dt)
        sel = dict(sel_a=consts["sel_a"], sel_b=consts["sel_b"])
        _, vjp = jax.vjp(lambda rows, ad: _dn_prep_fn(rows, {**sel, **ad}), (qc, kc, ab_ref[...]),
                         dict(alog=consts["alog"], dtb=consts["dtb"]))
        cq = tuple(dq_r[:, h * HEAD:(h + 1) * HEAD] for h in range(HEADS))
        ck = tuple(dk_r[:, h * HEAD:(h + 1) * HEAD] for h in range(HEADS))
        (dqc, dkc, dab), dad = vjp((cq, ck, dg_r[...], db_r[...]))
        for h in range(HEADS):
            dc_out[:, h * HEAD:(h + 1) * HEAD] = dqc[h]
            dc_out[:, DN_WIDTH + h * HEAD:DN_WIDTH + (h + 1) * HEAD] = dkc[h]
        dc_out[:, 2 * DN_WIDTH:3 * DN_WIDTH] = dv_r[...]
        dab_out[...] = dab.astype(BF16)
        _accumulate(dal_out, dad["alog"])
        _accumulate(ddt_out, dad["dtb"])

    return _rows_call("dn_prep_bwd", body, [conv, ab, dq, dk, dv, dg, db], [sel_a, sel_b, alog, dtb],
                      [(3 * DN_WIDTH, F32), (HEAD, BF16)], [(1, DN_WIDTH), (1, DN_WIDTH)], _row_tile(conv.shape[0]))


def _chunk_batch(t):
    nc = t // CHUNK
    return nc // 2 if nc % 2 == 0 else nc


def _dn_chunk_specs(t, nb):
    rows = nb * CHUNK
    blk = pl.BlockSpec((rows, HEAD), lambda h, b: (b, h))
    vblk = pl.BlockSpec((rows, HEAD), lambda h, b: (b, 2 * HEADS + h))
    mat = pl.BlockSpec((nb, HEAD, HEAD), lambda h, b: (b, h, 0))
    return rows, blk, vblk, mat


def _dn_chunk_fwd(qn, kn, conv, g, beta, host=None):
    t = qn.shape[0]
    nb = _chunk_batch(t)
    rows, blk, vblk, mat = _dn_chunk_specs(t, nb)

    def body(q_ref, k_ref, v_ref, g_ref, b_ref, n_o, b_o, qe_o, oo_o, eg_o):
        r3 = lambda x: x.reshape(nb, CHUNK, x.shape[-1])
        n_mat, b_mat, q_eff, o_own, eg = _dn_chunk_fn(r3(q_ref[...]), r3(k_ref[...]), r3(v_ref[...]), r3(g_ref[...]),
                                                      r3(g_ref[:, 0:CHUNK]), r3(b_ref[...]))
        n_o[...] = n_mat
        b_o[...] = b_mat
        qe_o[...] = q_eff.reshape(rows, HEAD)
        oo_o[...] = o_own.reshape(rows, HEAD)
        eg_o[...] = jnp.broadcast_to(eg, (nb, HEAD, HEAD))

    nc = t // CHUNK
    mats = jax.ShapeDtypeStruct((nc, DN_WIDTH, HEAD), F32)
    rowsd = jax.ShapeDtypeStruct((t, DN_WIDTH), F32)
    return _pcall(body, "dn_chunk_fwd", (HEADS, t // rows), [blk, blk, vblk, blk, blk], [mat, mat, blk, blk, mat],
                  [mats, mats, rowsd, rowsd, mats], [qn, kn, conv, g, beta], ("arbitrary", "arbitrary"), host=host)


def _dn_chunk_bwd(qn, kn, conv, g, beta, sall, gall, dq_eff, do, host=None):
    t = qn.shape[0]
    nb = _chunk_batch(t)
    rows, blk, vblk, mat = _dn_chunk_specs(t, nb)

    def body(q_ref, k_ref, v_ref, g_ref, b_ref, s_ref, ga_ref, dqe_ref, do_ref, dq_o, dk_o, dv_o, dg_o, db_o):
        r3 = lambda x: x.reshape(nb, CHUNK, x.shape[-1])
        _, vjp = jax.vjp(_dn_chunk_fn, r3(q_ref[...]), r3(k_ref[...]), r3(v_ref[...]), r3(g_ref[...]),
                         r3(g_ref[:, 0:CHUNK]), r3(b_ref[...]))
        s, ga = s_ref[...], ga_ref[...]
        d_n = -_bmm_nt(ga, s)
        d_eg = jnp.sum(ga * s, axis=1, keepdims=True)
        dq, dk, dv, dg, dg64, db = vjp((d_n, ga, r3(dqe_ref[...]), r3(do_ref[...]), d_eg))
        for o_ref, val in zip((dq_o, dk_o, dv_o, dg_o, db_o), (dq, dk, dv, dg, db)):
            o_ref[...] = val.reshape(rows, HEAD)
        dg_o[:, 0:CHUNK] += dg64.reshape(rows, CHUNK)

    return _pcall(body, "dn_chunk_bwd", (HEADS, t // rows), [blk, blk, vblk, blk, blk, mat, mat, blk, blk], [blk] * 5,
                  [jax.ShapeDtypeStruct((t, DN_WIDTH), F32)] * 5, [qn, kn, conv, g, beta, sall, gall, dq_eff, do],
                  ("arbitrary", "arbitrary"), host=host)


def _dn_rec_fwd(n_mat, b_mat, eg, host=None):
    nc = n_mat.shape[0]
    nb = _chunk_batch(nc * CHUNK)
    spec = pl.BlockSpec((nb, DN_WIDTH, HEAD), lambda i: (i, 0, 0))

    def body(n_ref, b_ref, eg_ref, sall_ref, s_scr):
        @pl.when(pl.program_id(0) == 0)
        def _():
            s_scr[...] = jnp.zeros(s_scr.shape, F32)

        for j in range(nb):
            sall_ref[j] = s_scr[...]
            for h in range(HEADS):
                sl = slice(h * HEAD, (h + 1) * HEAD)
                s_scr[sl, :] = _dn_rec_fn(s_scr[sl, :], n_ref[j, sl, :], b_ref[j, sl, :],
                                          eg_ref[j, h * HEAD:h * HEAD + 1, :])

    return _pcall(body, "dn_rec_fwd", (nc // nb,), [spec] * 3, spec, jax.ShapeDtypeStruct((nc, DN_WIDTH, HEAD), F32),
                  [n_mat, b_mat, eg], ("arbitrary",), scratch_shapes=[pltpu.VMEM((DN_WIDTH, HEAD), F32)], host=host)


def _dn_rec_bwd(n_mat, eg, ds_out, host=None):
    nc = n_mat.shape[0]
    nb = _chunk_batch(nc * CHUNK)
    steps = nc // nb
    spec = pl.BlockSpec((nb, DN_WIDTH, HEAD), lambda i: (steps - 1 - i, 0, 0))

    def body(n_ref, eg_ref, dso_ref, gall_ref, g_scr):
        @pl.when(pl.program_id(0) == 0)
        def _():
            g_scr[...] = jnp.zeros(g_scr.shape, F32)

        for j in reversed(range(nb)):
            gall_ref[j] = g_scr[...]
            for h in range(HEADS):
                sl = slice(h * HEAD, (h + 1) * HEAD)
                gv = g_scr[sl, :]
                g_scr[sl, :] = (gv * eg_ref[j, h * HEAD:h * HEAD + 1, :] - _mm_tn(n_ref[j, sl, :], gv)
                                + dso_ref[j, sl, :])

    return _pcall(body, "dn_rec_bwd", (steps,), [spec] * 3, spec, jax.ShapeDtypeStruct((nc, DN_WIDTH, HEAD), F32),
                  [n_mat, eg, ds_out], ("arbitrary",), scratch_shapes=[pltpu.VMEM((DN_WIDTH, HEAD), F32)], host=host)


def _dn_o_fwd(sall, q_eff, o_own):
    t = q_eff.shape[0]
    nb = _chunk_batch(t)
    rows, blk, _, mat = _dn_chunk_specs(t, nb)

    def body(s_ref, qe_ref, oo_ref, o_ref):
        r3 = lambda x: x.reshape(nb, CHUNK, HEAD)
        o_ref[...] = _dn_o_fn(s_ref[...], r3(qe_ref[...]), r3(oo_ref[...])).reshape(rows, HEAD)

    return _pcall(body, "dn_o_fwd", (HEADS, t // rows), [mat, blk, blk], blk, jax.ShapeDtypeStruct((t, DN_WIDTH), F32),
                  [sall, q_eff, o_own], ("arbitrary", "arbitrary"))


def _dn_o_bwd(sall, q_eff, do, host=None):
    t = q_eff.shape[0]
    nb = _chunk_batch(t)
    rows, blk, _, mat = _dn_chunk_specs(t, nb)

    def body(s_ref, qe_ref, do_ref, dqe_ref, ds_ref):
        r3 = lambda x: x.reshape(nb, CHUNK, HEAD)
        dov = r3(do_ref[...])
        dqe_ref[...] = _bmm_nt(dov, s_ref[...]).reshape(rows, HEAD)
        ds_ref[...] = _bmm_tn(r3(qe_ref[...]), dov)

    nc = t // CHUNK
    return _pcall(body, "dn_o_bwd", (HEADS, t // rows), [mat, blk, blk], [blk, mat],
                  [jax.ShapeDtypeStruct((t, DN_WIDTH), F32), jax.ShapeDtypeStruct((nc, DN_WIDTH, HEAD), F32)],
                  [sall, q_eff, do], ("arbitrary", "arbitrary"), host=host)


def _adamw_update(w, g, m, v):
    m2 = ADAM_B1 * m + (1.0 - ADAM_B1) * g
    v2 = ADAM_B2 * v + (1.0 - ADAM_B2) * (g * g)
    m_hat = m2 / (1.0 - ADAM_B1 ** ADAM_STEP)
    v_hat = v2 / (1.0 - ADAM_B2 ** ADAM_STEP)
    return -ADAM_LR * (m_hat / (jnp.sqrt(v_hat) + ADAM_EPS) + ADAM_WD * w), m2, v2


def _adamw_small(ws, gs, ms, vs):
    n = len(ws)

    def body(*refs):
        for i in range(n):
            d, m2, v2 = _adamw_update(refs[i][...], refs[n + i][...], refs[2 * n + i][...], refs[3 * n + i][...])
            refs[4 * n + i][...] = d
            refs[5 * n + i][...] = m2
            refs[6 * n + i][...] = v2

    shapes = [jax.ShapeDtypeStruct(a.shape, F32) for a in ws]
    outs = pl.pallas_call(body, name="adamw_small", out_shape=shapes * 3,
                          compiler_params=pltpu.CompilerParams(vmem_limit_bytes=VMEM_LIMIT))(*ws, *gs, *ms, *vs)
    return outs[:n], outs[n:2 * n], outs[2 * n:]


def _adamw_call(name, w, g, m, v, host=None):
    rows, cols = w.shape
    by_rows = rows % 8 == 0

    def body(w_ref, g_ref, m_ref, v_ref, g_out, d_ref, m_out, v_out):
        gv = g_ref[...] if by_rows else g_ref[0:rows, :]
        g_out[...] = gv
        d_ref[...], m_out[...], v_out[...] = _adamw_update(w_ref[...], gv, m_ref[...], v_ref[...])

    if by_rows:
        tr = _tile(rows, 256, 8)
        spec = g_spec = pl.BlockSpec((tr, cols), lambda i: (i, 0))
        grid = (rows // tr,)
    else:
        tc = _tile(cols, 256, 128)
        spec = pl.BlockSpec((rows, tc), lambda j: (0, j))
        g_spec = pl.BlockSpec((g.shape[0], tc), lambda j: (0, j))
        grid = (cols // tc,)
    return _pcall(body, name, grid, [spec, g_spec, spec, spec], [spec] * 4, [jax.ShapeDtypeStruct((rows, cols), F32)] * 4,
                  [w, g, m, v], ("arbitrary",), host=host)


def _rope_tables(t):
    half = ROPE // 2
    inv_freq = np.float32(ROPE_THETA) ** (-np.arange(half, dtype=np.float32) / np.float32(half))
    ang = np.arange(t, dtype=np.float32)[:, None] * inv_freq[None, :].astype(np.float32)
    z = np.zeros((t, HEAD - ROPE), np.float32)
    cos = np.concatenate([np.cos(ang), np.cos(ang), z], axis=1).astype(np.float32)
    sin = np.concatenate([np.sin(ang), np.sin(ang), z], axis=1).astype(np.float32)
    k = np.arange(HEAD)[:, None]
    l = np.arange(HEAD)[None, :]
    perm = np.where((l < half) & (k == l + half), -1.0, 0.0) + np.where((l >= half) & (l < ROPE) & (k == l - half), 1.0, 0.0)
    return jnp.asarray(cos), jnp.asarray(sin), jnp.asarray(perm.astype(np.float32))


def _win_to_pad(w):
    z = lambda n: jnp.zeros((n, w.shape[1]), w.dtype)
    return jnp.concatenate([w[576:2112], w[2112:2624], w[0:256], w[256:512], w[512:576], z(64), w[2624:2632], z(120)],
                           axis=0)


def _win_from_pad(g):
    return jnp.concatenate([g[2048:2304], g[2304:2560], g[2560:2624], g[0:1536], g[1536:2048], g[2688:2696]], axis=0)


def _qk_to_pad(w):
    w4 = w.reshape(HEADS, QK_DIM, w.shape[-1])
    return jnp.concatenate([w4, jnp.zeros((HEADS, QK_PAD - QK_DIM, w.shape[-1]), w.dtype)], axis=1).reshape(
        HEADS * QK_PAD, w.shape[-1])


def _qk_from_pad(g):
    return g.reshape(HEADS, QK_PAD, g.shape[-1])[:, :QK_DIM].reshape(HEADS * QK_DIM, g.shape[-1])


def _ff_to_pad(a, axis):
    shape = list(a.shape)
    shape[axis:axis + 1] = [N_CHIPS, FF_SHARD]
    a4 = a.reshape(shape)
    shape[axis + 1] = FF_BLOCK - FF_SHARD
    out = jnp.concatenate([a4, jnp.zeros(shape, a.dtype)], axis=axis + 1)
    shape[axis:axis + 2] = [D_FF_P]
    return out.reshape(shape)


def _ff_from_pad(a, axis):
    shape = list(a.shape)
    shape[axis:axis + 1] = [N_CHIPS, FF_BLOCK]
    a4 = lax.slice_in_dim(a.reshape(shape), 0, FF_SHARD, axis=axis + 1)
    shape[axis:axis + 2] = [D_FF]
    return a4.reshape(shape)


class _LocalPlan:
    def __init__(self, wt):
        self.wt, self.grads = wt, {}

    def weight(self, name):
        return self.wt[name]

    def host(self, point):
        return None

    def grad(self, name, value):
        self.grads[name] = value


def _local_step(x, tgt, wt, plan=None):
    plan = _LocalPlan(wt) if plan is None else plan
    s = x.shape[0]
    n_valid = N_META + s
    t = -(-n_valid // HEAD) * HEAD
    zpad = jnp.zeros((t - n_valid, D_MODEL), F32)
    h0 = jnp.concatenate([wt["meta_tokens"], x, zpad], axis=0)
    tgt_p = jnp.concatenate([jnp.zeros((N_META, D_MODEL), F32), tgt, zpad], axis=0)
    cos, sin, perm = _rope_tables(t)
    qn_w = jnp.concatenate([wt["q_norm_w"], jnp.zeros((1, QK_PAD - QK_DIM), F32)], axis=1)
    kn_w = jnp.concatenate([wt["k_norm_w"], jnp.zeros((1, QK_PAD - QK_DIM), F32)], axis=1)
    head_id = jnp.arange(DN_WIDTH)[None, :] // HEAD
    lane = jnp.arange(HEAD)[:, None]
    sel_a = (lane == head_id).astype(F32)
    sel_b = (lane == head_id + HEADS).astype(F32)
    alog = jnp.repeat(wt["dn_A_log"], HEAD, axis=1)
    dtb = jnp.repeat(wt["dn_dt_bias"], HEAD, axis=1)
    conv_w, conv_b = wt["ffn_conv_w"], wt["ffn_conv_b"]

    u = _rms_fwd("attn_norm_fwd", h0, wt["attn_norm_w"], host=plan.host("attn_norm_fwd"))
    win, wq, wkv = plan.weight("w_in_t"), plan.weight("w_q_t"), plan.weight("w_kv_t")
    proj = _matmul("in_proj", u, win, "nt", F32)
    z = (proj, DN_WIDTH, 3)
    q_lat, kv_lat, k_pe, ab = (proj, LORA, 8), (proj, LORA, 9), (proj, HEAD, 20), (proj, HEAD, 21)
    mla_consts = (wt["q_a_norm_w"], wq, wt["kv_a_norm_w"], wkv, qn_w, kn_w, perm)
    q, k, v = _mla_prep_fwd(q_lat, kv_lat, k_pe, cos, sin, *mla_consts)
    o_mla = _attn_fwd(q, k, v, host=plan.host("attn_fwd"))
    conv = _dn_conv_fwd(proj, wt["dn_conv_w"])
    dn_consts = (sel_a, sel_b, alog, dtb)
    qn, kn, g, beta = _dn_prep_fwd(conv, ab, *dn_consts)
    n_mat, b_mat, q_eff, o_own, eg = _dn_chunk_fwd(qn, kn, conv, g, beta, host=plan.host("dn_chunk_fwd"))
    sall = _dn_rec_fwd(n_mat, b_mat, eg)
    o_dn = _dn_o_fwd(sall, q_eff, o_own)
    w_out = plan.weight("w_out")
    mixed, h1, n2 = _mix_out_proj(o_mla, o_dn, z, wt["mla_out_norm_w"], wt["dn_out_norm_w"], w_out, h0, wt["ffn_norm_w"])
    w_gate, w_up = plan.weight("w_gate_t"), plan.weight("w_up_t")
    gpre = _matmul("gate_proj", n2, w_gate, "nt", BF16, host=plan.host("gate_proj"))
    up = _matmul("up_proj", n2, w_up, "nt", BF16, host=plan.host("up_proj"))
    act = _glu_fwd(gpre, up, conv_w, conv_b)
    w_down = plan.weight("w_down")
    dy, dy16, sq = _down_proj_loss(act, w_down, h1, tgt_p, n_valid)

    grads = {}
    dact = _matmul("down_dx", dy16, w_down, "nt", BF16)
    plan.grad("w_down", _matmul("down_dw", act, dy16, "tn", BF16))
    dgpre, dup, grads["ffn_conv_w"], grads["ffn_conv_b"] = _glu_bwd(gpre, up, conv_w, conv_b, dact)
    plan.grad("w_gate_t", _matmul("gate_dw", dgpre, n2, "tn", BF16))
    plan.grad("w_up_t", _matmul("up_dw", dup, n2, "tn", BF16))
    dh1, dh1_16, grads["ffn_norm_w"] = _ffn_in_bwd(dgpre, dup, w_gate, w_up, h1, dy, wt["ffn_norm_w"],
                                                   host=plan.host("ffn_in_bwd"))
    plan.grad("w_out", _matmul("out_dw", mixed, dh1_16, "tn", BF16))
    do_mla, do_dn, dz, grads["mla_out_norm_w"], grads["dn_out_norm_w"] = _mix_out_bwd(
        o_mla, o_dn, z, dh1_16, w_out, wt["mla_out_norm_w"], wt["dn_out_norm_w"], host=plan.host("mix_out_bwd"))
    dq_eff, ds_out = _dn_o_bwd(sall, q_eff, do_dn)
    gall = _dn_rec_bwd(n_mat, eg, ds_out)
    dqn, dkn, dv_dn, dg, dbeta = _dn_chunk_bwd(qn, kn, conv, g, beta, sall, gall, dq_eff, do_dn,
                                               host=plan.host("dn_chunk_bwd"))
    dconv, dab, dalog, ddtb = _dn_prep_bwd(conv, ab, dqn, dkn, dv_dn, dg, dbeta, *dn_consts)
    grads["dn_A_log"] = jnp.sum(dalog.reshape(HEADS, HEAD), axis=1)[None, :]
    grads["dn_dt_bias"] = jnp.sum(ddtb.reshape(HEADS, HEAD), axis=1)[None, :]
    ddn_pre, grads["dn_conv_w"] = _dn_conv_bwd(proj, wt["dn_conv_w"], dconv)
    dq, dk, dv = _attn_bwd(q, k, v, do_mla, host=plan.host("attn_bwd"))
    dq_lat, dkv_lat, dk_pe, dqa, dwq, dkva, dwkv, dqnw, dknw = _mla_prep_bwd(
        q_lat, kv_lat, k_pe, cos, sin, dq, dk, dv, *mla_consts, host=plan.host("mla_prep_bwd"))
    grads["q_a_norm_w"], grads["kv_a_norm_w"] = dqa, dkva
    plan.grad("w_q_t", dwq)
    plan.grad("w_kv_t", dwkv)
    grads["q_norm_w"], grads["k_norm_w"] = dqnw[:, :QK_DIM], dknw[:, :QK_DIM]
    dproj = jnp.concatenate([ddn_pre, dz, dq_lat, dkv_lat, dk_pe, dab], axis=1)
    plan.grad("w_in_t", _matmul("in_dw", dproj, u, "tn", F32))
    du = _matmul("in_dx", dproj, win, "nn", BF16, host=plan.host("in_dx"))
    dh0, _, grads["attn_norm_w"] = _rms_bwd("attn_norm_bwd", h0, wt["attn_norm_w"], [du], dh1,
                                            host=plan.host("attn_norm_bwd"))
    grads["meta_tokens"] = dh0[0:N_META]
    if isinstance(plan, _LocalPlan):
        grads.update(plan.grads)
    return sq, dh0[N_META:n_valid], grads


def _mesh_pos():
    return lax.axis_index("x"), lax.axis_index("y"), lax.axis_index("c")


def _other_chips(x, y):
    return [(1 - x, y), (x, 1 - y), (1 - x, 1 - y)]


def _remote(src, dst, send_sems, recv_sems, k, to):
    return pltpu.make_async_remote_copy(src_ref=src, dst_ref=dst, send_sem=send_sems.at[k], recv_sem=recv_sems.at[k],
                                        device_id=to, device_id_type=MESH)


SIBLING_ID, CHIPS_ID, GATHER_ID, ALL_ID = 1, 2, 3, 4


def _sibling_peer():
    x, y, c = _mesh_pos()
    return [(x, y, 1 - c)]


def _chip_peers():
    x, y, c = _mesh_pos()
    return [(qx, qy, c) for qx, qy in _other_chips(x, y)]


def _copies_exchange(make, ins, out_shape, nsem, peers=None, cid=None):
    def prog(in_refs, out_refs, send_sems, recv_sems):
        copies = make(in_refs, out_refs, send_sems, recv_sems)

        def start():
            for cp in copies:
                cp.start()

        def finish():
            for cp in copies:
                cp.wait()

        return start, finish

    return _Exchange(prog, ins, out_shape, nsem, peers, cid)


def _all_gather(shards):
    def prog(srcs, dsts, send_sems, recv_sems):
        x, y, c = _mesh_pos()
        p = 2 * x + y
        sibling = (x, y, 1 - c)
        chips = _other_chips(x, y)
        bufs = tuple((s, d, s.shape[0] // 2) for s, d in zip(srcs, dsts))

        def half(ref, rows, which):
            return ref.at[pl.ds(which * rows, rows), :]

        def copy(i, k, src, dst, to):
            return _remote(src, dst, send_sems, recv_sems, 6 * i + k, to)

        sends = [copy(i, j, half(src, rows, c), half(dst.at[p], rows, c), (*chip, c))
                 for i, (src, dst, rows) in enumerate(bufs) for j, chip in enumerate(chips)]

        def start():
            for cp in sends:
                cp.start()

        def finish():
            passed = []
            for i, (src, dst, rows) in enumerate(bufs):
                for j, (qx, qy) in enumerate(chips):
                    block = half(dst.at[2 * qx + qy], rows, c)
                    copy(i, j, block, block, (x, y, c)).wait_recv()
                    fwd = copy(i, 3 + j, block, block, sibling)
                    fwd.start()
                    passed.append(fwd)
            for i, (src, dst, rows) in enumerate(bufs):
                for j, (qx, qy) in enumerate(chips):
                    block = half(dst.at[2 * qx + qy], rows, 1 - c)
                    copy(i, 3 + j, block, block, (x, y, c)).wait_recv()
            for cp in sends + passed:
                cp.wait_send()

        return start, finish

    return _Exchange(prog, shards, [jax.ShapeDtypeStruct((N_CHIPS, *s.shape), s.dtype) for s in shards], 6 * len(shards),
                     lambda: _sibling_peer() + _chip_peers(), GATHER_ID)


def _gathered(ex):
    p = 2 * lax.axis_index("x") + lax.axis_index("y")
    return [lax.dynamic_update_slice(g, s[None], (p, 0, 0)) for g, s in zip(ex.outs, ex.ins)]


def _rs_to_sibling(bufs):
    def make(srcs, dsts, send_sems, recv_sems):
        x, y, c = _mesh_pos()
        copies = []
        for i, (src, dst) in enumerate(zip(srcs, dsts)):
            half = src.shape[1] // 2
            copies.append(_remote(src.at[:, pl.ds((1 - c) * half, half), :], dst, send_sems, recv_sems, i, (x, y, 1 - c)))
        return copies

    return _copies_exchange(make, bufs,
                            [jax.ShapeDtypeStruct((N_CHIPS, b.shape[1] // 2, b.shape[2]), b.dtype) for b in bufs],
                            len(bufs), _sibling_peer, SIBLING_ID)


def _rs_pair_add(name, bufs, gots, c, out_dtype):
    n = len(bufs)

    def body(c_ref, *refs):
        for a_ref, b_ref, o_ref in zip(refs[:n], refs[n:2 * n], refs[2 * n:]):
            o_ref[...] = (a_ref[...].astype(F32) + b_ref[...].astype(F32)).astype(out_dtype)

    mine = [pl.BlockSpec((None, g.shape[1], g.shape[2]), lambda j, cr: (j, cr[0], 0)) for g in gots]
    whole = [pl.BlockSpec((None, g.shape[1], g.shape[2]), lambda j, cr: (j, 0, 0)) for g in gots]
    return pl.pallas_call(
        body, name=name,
        grid_spec=pltpu.PrefetchScalarGridSpec(num_scalar_prefetch=1, grid=(N_CHIPS,), in_specs=mine + whole, out_specs=whole),
        out_shape=[jax.ShapeDtypeStruct(g.shape, out_dtype) for g in gots],
        compiler_params=_cparams(("arbitrary",)))(c, *bufs, *gots)


def _rs_to_chips(accs):
    def make(srcs, dsts, send_sems, recv_sems):
        x, y, c = _mesh_pos()
        return [_remote(src.at[2 * qx + qy], dst.at[k], send_sems, recv_sems, 3 * i + k, (qx, qy, c))
                for i, (src, dst) in enumerate(zip(srcs, dsts)) for k, (qx, qy) in enumerate(_other_chips(x, y))]

    return _copies_exchange(make, accs, [jax.ShapeDtypeStruct((3, a.shape[1], a.shape[2]), a.dtype) for a in accs],
                            3 * len(accs), _chip_peers, CHIPS_ID)


def _rs_chip_add(name, accs, gots, p):
    n = len(accs)
    slot = (0, 1, 0, 2)

    def body(p_ref, *refs):
        me = p_ref[0]
        for own_ref, got_ref, o_ref in zip(refs[:n], refs[n:2 * n], refs[2 * n:]):
            total = None
            for chip in range(N_CHIPS):
                val = own_ref[...].astype(F32)
                for e in (1, 2, 3):
                    val = jnp.where((chip ^ me) == e, got_ref[slot[e]].astype(F32), val)
                total = val if total is None else total + val
            o_ref[...] = total

    own = [pl.BlockSpec((None, a.shape[1], a.shape[2]), lambda i, pr: (pr[0], 0, 0)) for a in accs]
    got = [pl.BlockSpec(g.shape, lambda i, pr: (0, 0, 0)) for g in gots]
    out = [pl.BlockSpec((a.shape[1], a.shape[2]), lambda i, pr: (0, 0)) for a in accs]
    return pl.pallas_call(
        body, name=name,
        grid_spec=pltpu.PrefetchScalarGridSpec(num_scalar_prefetch=1, grid=(1,), in_specs=own + got, out_specs=out),
        out_shape=[jax.ShapeDtypeStruct((a.shape[1], a.shape[2]), F32) for a in accs],
        compiler_params=_cparams(("arbitrary",)))(p, *accs, *gots)


def _rs_share(ress):
    def make(srcs, dsts, send_sems, recv_sems):
        x, y, c = _mesh_pos()
        return [_remote(src, dst, send_sems, recv_sems, i, (x, y, 1 - c)) for i, (src, dst) in enumerate(zip(srcs, dsts))]

    return _copies_exchange(make, ress, [jax.ShapeDtypeStruct(r.shape, F32) for r in ress], len(ress), _sibling_peer,
                            SIBLING_ID)


def _shared(ex):
    south = lax.axis_index("c") == 0
    return [jnp.concatenate([jnp.where(south, r, g), jnp.where(south, g, r)], axis=0) for r, g in zip(ex.ins, ex.outs)]


def _all_to_all_devices(vec):
    def others():
        x, y, c = _mesh_pos()
        return [((1 - x if r & 4 else x), (1 - y if r & 2 else y), (1 - c if r & 1 else c)) for r in range(1, 8)]

    def make(srcs, dsts, send_sems, recv_sems):
        x, y, c = _mesh_pos()
        me = 4 * x + 2 * y + c
        return [_remote(srcs[0], dsts[0].at[me], send_sems, recv_sems, r, peer) for r, peer in enumerate(others())]

    return _copies_exchange(make, [vec], [jax.ShapeDtypeStruct((8, *vec.shape), vec.dtype)], 7, others, ALL_ID)


def _sum_devices(stack):
    def body(s_ref, o_ref):
        total = s_ref[0]
        for d in range(1, 8):
            total = total + s_ref[d]
        o_ref[...] = total

    return pl.pallas_call(body, name="sum_devices", out_shape=jax.ShapeDtypeStruct(stack.shape[1:], F32),
                          compiler_params=pltpu.CompilerParams(vmem_limit_bytes=VMEM_LIMIT))(stack)


def _pad_rows(flat, rows):
    return jnp.concatenate([flat, jnp.zeros((rows * LANES - flat.shape[0],), flat.dtype)]).reshape(rows, LANES)


def _unshard(g4, shape, axis):
    a = g4.reshape(N_CHIPS, *shape)
    if axis == 0:
        return a.reshape(N_CHIPS * shape[0], shape[1])
    return jnp.transpose(a, (1, 0, 2)).reshape(shape[0], N_CHIPS * shape[1])


def _shard4(full, shape, axis):
    if axis == 0:
        return full.reshape(N_CHIPS, shape[0] * shape[1])
    a = full.reshape(shape[0], N_CHIPS, shape[1])
    return jnp.transpose(a, (1, 0, 2)).reshape(N_CHIPS, shape[0] * shape[1])


def _pad_axis0(a, rows):
    return jnp.concatenate([a, jnp.zeros((rows - a.shape[0], *a.shape[1:]), a.dtype)], axis=0)


def _pad_axis1(a, rows):
    return jnp.concatenate([a, jnp.zeros((a.shape[0], rows - a.shape[1], *a.shape[2:]), a.dtype)], axis=1)


def _shard_to_strip(name, w):
    _, (shape, axis, rows) = name, {n: (s, ax, r) for n, s, ax, r in BIG}[name]
    w2 = w.reshape(shape).astype(BF16)
    if name == "w_in":
        return w2
    return _pad_axis0(w2.T if axis == 1 else w2, rows)


LOCAL_NAME = dict(w_in="w_in_t", w_q_b="w_q_t", w_kv_b="w_kv_t", w_out="w_out", w_gate="w_gate_t", w_up="w_up_t",
                  w_down="w_down")


WIN_SEGMENTS = ((576, 2112, 0), (2112, 2624, 1536), (0, 256, 2048), (256, 512, 2304), (512, 576, 2560), (2624, 2632, 2688))


def _strips_to_weight(name, g4):
    if name == "w_in":
        return _win_to_pad(jnp.transpose(g4, (0, 2, 1)).reshape(IN_COLS, D_MODEL))
    if name == "w_q_b":
        return _qk_to_pad(g4.reshape(HEADS * QK_DIM, LORA))
    return g4.reshape(N_CHIPS * g4.shape[1], g4.shape[2])


def _grad_to_strips(name, g):
    if name == "w_in":
        strips = []
        for q in range(N_CHIPS):
            pieces = []
            for a, b, local in sorted(WIN_SEGMENTS):
                s, e = max(a, q * IN_SHARD), min(b, (q + 1) * IN_SHARD)
                if s < e:
                    pieces.append(g[local + s - a:local + e - a])
            pieces.append(jnp.zeros((IN_SHARD_P - IN_SHARD, D_MODEL), g.dtype))
            strips.append(jnp.concatenate(pieces, axis=0))
        return jnp.stack(strips)
    if name == "w_q_b":
        return _qk_from_pad(g).reshape(N_CHIPS, QK_DIM, LORA)
    return g.reshape(N_CHIPS, g.shape[0] // N_CHIPS, g.shape[1])


class _MeshPlan:
    LATE = dict(attn_norm_fwd=("w_in", "w_q_b", "w_kv_b"), attn_fwd=("w_up",), dn_chunk_fwd=("w_out", "w_gate"),
                gate_proj=("w_down/0",), up_proj=("w_down/1",))
    GROUP_A = ("w_down", "w_gate", "w_up", "w_out")
    GROUP_B = ("w_in", "w_q_b", "w_kv_b")

    def __init__(self, w):
        x, y, c = _mesh_pos()
        self.ci = jnp.reshape(c, (1,)).astype(jnp.int32)
        self.pi = jnp.reshape(2 * x + y, (1,)).astype(jnp.int32)
        self.strip = {n: _shard_to_strip(n, w[n]) for n, _, _, _ in BIG}
        self.gathers, self.weights, self.g, self.acc, self.reduced = {}, {}, {}, {}, {}
        self.sibs, self.sib, self.chip, self.share, self.halves = [], None, None, None, [None, None]

    def gather_small(self, small):
        ex = _all_gather([small])
        ex.run("all_gather_small")
        return _gathered(ex)[0]

    def weight(self, local_name):
        if local_name not in self.weights:
            for point, (names, ex) in list(self.gathers.items()):
                if ex.outs is not None:
                    for n, g4 in zip(names, _gathered(ex)):
                        if "/" in n:
                            n, half = n.split("/")
                            self.halves[int(half)] = g4
                            if None in self.halves:
                                continue
                            g4 = jnp.concatenate(self.halves, axis=1)
                        self.weights[LOCAL_NAME[n]] = _strips_to_weight(n, g4)
                    del self.gathers[point]
        return self.weights[local_name]

    def _shard(self, name):
        if "/" not in name:
            return self.strip[name]
        name, half = name.split("/")
        rows = self.strip[name].shape[0] // 2
        return self.strip[name][int(half) * rows:(int(half) + 1) * rows]

    def grad(self, local_name, value):
        name = {v: k for k, v in LOCAL_NAME.items()}[local_name]
        self.g[name] = _grad_to_strips(name, value)

    def _pair_add(self, names, gots):
        accs = _rs_pair_add("rs_pair_add_" + names[0], [self.g[n] for n in names], gots, self.ci, BF16)
        self.acc.update(zip(names, accs))

    def _chip_add(self, names, chip):
        return _rs_chip_add("rs_chip_add_" + names[0], [self.acc[n] for n in names], chip.outs, self.pi)

    def _take_shared(self, names, share):
        for n, strip in zip(names, _shared(share)):
            self.reduced[n] = strip

    def host(self, point):
        a, b = self.GROUP_A, self.GROUP_B
        if point in self.LATE:
            names = self.LATE[point]
            ex = _all_gather([self._shard(n) for n in names])
            self.gathers[point] = (names, ex)
            return ex
        if point in ("ffn_in_bwd", "mix_out_bwd"):
            names = dict(ffn_in_bwd=a[:3], mix_out_bwd=a[3:])[point]
            ex = _rs_to_sibling([self.g[n] for n in names])
            self.sibs.append(ex)
            return ex
        if point == "dn_chunk_bwd":
            self._pair_add(a, [o for ex in self.sibs for o in ex.outs])
            self.chip1 = _rs_to_chips([self.acc[n] for n in a[:2]])
            return self.chip1
        if point == "attn_bwd":
            self.chip2 = _rs_to_chips([self.acc[n] for n in a[2:]])
            return self.chip2
        if point == "mla_prep_bwd":
            ress = self._chip_add(a[:2], self.chip1) + self._chip_add(a[2:], self.chip2)
            self.share = _rs_share(ress)
            return self.share
        if point == "in_dx":
            self._take_shared(a, self.share)
            self.sib = _rs_to_sibling([self.g[n] for n in b])
            return self.sib
        if point == "attn_norm_bwd":
            self._pair_add(b, self.sib.outs)
            self.chip = _rs_to_chips([self.acc[n] for n in b])
            return self.chip
        return None

    def last_share(self):
        self.share = _rs_share(self._chip_add(self.GROUP_B, self.chip))
        return self.share

    def finish(self):
        self._take_shared(self.GROUP_B, self.share)
        return self.reduced


def _strip_to_shard(name, strip):
    shape, axis = {n: (s, ax) for n, s, ax, _ in BIG}[name]
    rows = shape[axis]
    return strip[:rows].T if axis == 1 else strip[:rows]


def kernel(x, meta_tokens, attn_norm_w, w_in, q_a_norm_w, w_q_b, kv_a_norm_w, w_kv_b, q_norm_w, k_norm_w, mla_out_norm_w, dn_conv_w, dn_A_log, dn_dt_bias, dn_out_norm_w, w_out, ffn_norm_w, w_gate, w_up, ffn_conv_w, ffn_conv_b, w_down, loss_target, m_meta_tokens, m_attn_norm_w, m_w_in, m_q_a_norm_w, m_w_q_b, m_kv_a_norm_w, m_w_kv_b, m_q_norm_w, m_k_norm_w, m_mla_out_norm_w, m_dn_conv_w, m_dn_A_log, m_dn_dt_bias, m_dn_out_norm_w, m_w_out, m_ffn_norm_w, m_w_gate, m_w_up, m_ffn_conv_w, m_ffn_conv_b, m_w_down, v_meta_tokens, v_attn_norm_w, v_w_in, v_q_a_norm_w, v_w_q_b, v_kv_a_norm_w, v_w_kv_b, v_q_norm_w, v_k_norm_w, v_mla_out_norm_w, v_dn_conv_w, v_dn_A_log, v_dn_dt_bias, v_dn_out_norm_w, v_w_out, v_ffn_norm_w, v_w_gate, v_w_up, v_ffn_conv_w, v_ffn_conv_b, v_w_down):
    local = dict(locals())
    w = {n: local[n] for n in WEIGHTS}
    m = {n: local["m_" + n] for n in WEIGHTS}
    v = {n: local["v_" + n] for n in WEIGHTS}
    p = 2 * lax.axis_index("x") + lax.axis_index("y")

    plan = _MeshPlan(w)
    wf = _pad_rows(jnp.concatenate([w[n].reshape(-1) for n, _, _ in SMALL_SHARDED]), SMALL_ROWS)
    gf = plan.gather_small(wf).reshape(N_CHIPS, -1)
    full = {}
    off = 0
    for n, s, ax in SMALL_SHARDED:
        full[n] = _unshard(gf[:, off:off + s[0] * s[1]], s, ax)
        off += s[0] * s[1]
    for n, _ in REPLICATED:
        full[n] = w[n]
    full["ffn_conv_w"] = _ff_to_pad(full["ffn_conv_w"], 1)
    full["ffn_conv_b"] = _ff_to_pad(full["ffn_conv_b"], 1)

    sq, grad_x, g = _local_step(x[0], loss_target[0], full, plan)
    g["ffn_conv_w"] = _ff_from_pad(g["ffn_conv_w"], 1)
    g["ffn_conv_b"] = _ff_from_pad(g["ffn_conv_b"], 1)

    small_all = [n for n, _, _ in SMALL_SHARDED] + [n for n, _ in REPLICATED]
    vec = jnp.concatenate([g[n].reshape(-1) for n in small_all] + [jnp.reshape(0.5 / D_MODEL * jnp.sum(sq), (1,))])
    vec = _pad_rows(vec, -(-vec.shape[0] // (8 * LANES)) * 8)
    a2a = _all_to_all_devices(vec)

    gs, delta, new_m, new_v = {}, {}, {}, {}
    big = {n: (s, ax) for n, s, ax, _ in BIG}

    def adamw_big(n, strips, host=None):
        s, ax = big[n]
        flip = ax == 1 and s[1] % 8 == 0
        there = (lambda a: a.reshape(s).T) if flip else (lambda a: a.reshape(s))
        back = (lambda a: a.T.reshape(w[n].shape)) if flip else (lambda a: a.reshape(w[n].shape))
        strip = strips[n] if flip or ax == 0 else strips[n][:s[1]].T
        g2, d2, m2, v2 = _adamw_call("adamw_" + n, there(w[n]), strip, there(m[n]), there(v[n]), host=host)
        gs[n], delta[n], new_m[n], new_v[n] = back(g2), back(d2), back(m2), back(v2)

    adamw_big("w_down", plan.reduced, host=a2a)
    adamw_big("w_gate", plan.reduced, host=plan.last_share())
    adamw_big("w_up", plan.reduced)
    adamw_big("w_out", plan.reduced)
    strips = plan.finish()
    for n in plan.GROUP_B:
        adamw_big(n, strips)
    me = 4 * lax.axis_index("x") + 2 * lax.axis_index("y") + lax.axis_index("c")
    red = _sum_devices(lax.dynamic_update_slice(a2a.outs[0], vec[None], (me, 0, 0))).reshape(-1)
    off = 0
    for n in small_all:
        tot = red[off:off + g[n].size].reshape(g[n].shape)
        off += g[n].size
        shard = {sn: (s, ax) for sn, s, ax in SMALL_SHARDED}.get(n)
        if shard is not None:
            tot = lax.dynamic_slice_in_dim(tot, p * shard[0][1], shard[0][1], axis=1)
        gs[n] = tot
    loss = red[off]
    two_d = lambda a: a.reshape(a.shape[-2], a.shape[-1])
    outs = _adamw_small([two_d(w[n]) for n in small_all], [two_d(gs[n]) for n in small_all],
                        [two_d(m[n]) for n in small_all], [two_d(v[n]) for n in small_all])
    for i, n in enumerate(small_all):
        for dst, src in ((delta, outs[0]), (new_m, outs[1]), (new_v, outs[2])):
            dst[n] = src[i].reshape(w[n].shape)

    grad_out = [gs[n].reshape(w[n].shape) for n in WEIGHTS]
    return (loss, grad_x[None], *grad_out, *[delta[n] for n in WEIGHTS], *[new_m[n] for n in WEIGHTS],
            *[new_v[n] for n in WEIGHTS])
```

```python
import functools
import math

import jax
import jax.numpy as jnp
import numpy as np
from jax import lax
from jax.experimental import pallas as pl
from jax.experimental.pallas import tpu as pltpu

F32 = jnp.float32
BF16 = jnp.bfloat16
HI = lax.Precision.HIGHEST
MESH = pl.DeviceIdType.MESH

N_META = 16
D_MODEL = 1024
HEADS = 4
HEAD = 128
ROPE = 64
QK_DIM = HEAD + ROPE
QK_PAD = 2 * HEAD
LORA = 256
DN_WIDTH = HEADS * HEAD
CHUNK = 64
D_FF = 2816
N_CHIPS = 4
FF_SHARD = D_FF // N_CHIPS
FF_BLOCK = 768
D_FF_P = N_CHIPS * FF_BLOCK
IN_COLS = 2632
IN_SHARD = IN_COLS // N_CHIPS
IN_SHARD_P = 672
IN_PAD = 2816
NORM_EPS = 1e-6
ROPE_THETA = 10000.0
LANES = 512

ADAM_LR, ADAM_B1, ADAM_B2, ADAM_EPS, ADAM_WD, ADAM_STEP = 0.001, 0.9, 0.999, 1e-08, 0.01, 10

VMEM_LIMIT = 56 * 1024 * 1024

BIG = (("w_in", (1024, 658), 1, IN_SHARD_P), ("w_q_b", (256, 192), 1, 192), ("w_kv_b", (256, 256), 1, 256),
       ("w_out", (256, 1024), 0, 256), ("w_gate", (1024, 704), 1, FF_BLOCK), ("w_up", (1024, 704), 1, FF_BLOCK),
       ("w_down", (704, 1024), 0, FF_BLOCK))
SMALL_SHARDED = (("meta_tokens", (16, 256), 1), ("dn_conv_w", (4, 384), 1), ("ffn_conv_w", (3, 704), 1))
REPLICATED = (("attn_norm_w", 1024), ("q_a_norm_w", 256), ("kv_a_norm_w", 256), ("q_norm_w", 192), ("k_norm_w", 192),
              ("mla_out_norm_w", 128), ("dn_A_log", 4), ("dn_dt_bias", 4), ("dn_out_norm_w", 128), ("ffn_norm_w", 1024),
              ("ffn_conv_b", 2816))
WEIGHTS = ("meta_tokens", "attn_norm_w", "w_in", "q_a_norm_w", "w_q_b", "kv_a_norm_w", "w_kv_b", "q_norm_w", "k_norm_w",
           "mla_out_norm_w", "dn_conv_w", "dn_A_log", "dn_dt_bias", "dn_out_norm_w", "w_out", "ffn_norm_w", "w_gate",
           "w_up", "ffn_conv_w", "ffn_conv_b", "w_down")

SMALL_ROWS = 16
REP_ROWS = 16


def _cparams(sem):
    return pltpu.CompilerParams(dimension_semantics=sem, vmem_limit_bytes=VMEM_LIMIT)


class _Exchange:
    def __init__(self, prog, ins, out_shape, nsem, peers=None, cid=None):
        self.prog, self.ins, self.out_shape, self.nsem = prog, list(ins), list(out_shape), nsem
        self.peers, self.cid = peers, cid
        self.outs = None

    def sems(self):
        return [pltpu.SemaphoreType.DMA((self.nsem,)), pltpu.SemaphoreType.DMA((self.nsem,))]

    def programs(self, in_refs, out_refs, send_sems, recv_sems):
        start, finish = self.prog(in_refs, out_refs, send_sems, recv_sems)
        if self.cid is None:
            return start, finish
        peers = self.peers()

        def shake_and_start():
            barrier = pltpu.get_barrier_semaphore()
            for peer in peers:
                pl.semaphore_signal(barrier, inc=1, device_id=peer, device_id_type=MESH)
            pl.semaphore_wait(barrier, len(peers))
            start()

        return shake_and_start, finish

    def cparams(self, **kw):
        return pltpu.CompilerParams(has_side_effects=True, collective_id=self.cid, **kw)

    def run(self, name):
        any_spec = pl.BlockSpec(memory_space=pl.ANY)
        n = len(self.ins)

        def body(*refs):
            start, finish = self.programs(refs[:n], refs[n:-2], refs[-2], refs[-1])
            start()
            finish()

        self.outs = pl.pallas_call(
            body, name=name, in_specs=[any_spec] * n, out_specs=[any_spec] * len(self.out_shape),
            out_shape=self.out_shape, scratch_shapes=self.sems(), compiler_params=self.cparams())(*self.ins)
        return self.outs


def _pcall(body, name, grid, in_specs, out_specs, out_shape, args, sem, scratch_shapes=(), host=None):
    single = not isinstance(out_shape, (list, tuple))
    out_specs, out_shape = ([out_specs], [out_shape]) if single else (list(out_specs), list(out_shape))
    if host is None:
        outs = pl.pallas_call(body, name=name, grid=grid, in_specs=list(in_specs), out_specs=out_specs, out_shape=out_shape,
                              scratch_shapes=list(scratch_shapes), compiler_params=_cparams(sem))(*args)
        return outs[0] if single else outs
    any_spec = pl.BlockSpec(memory_space=pl.ANY)
    n_in, n_out, n_scr, nx_in, nx_out = len(in_specs), len(out_specs), len(scratch_shapes), len(host.ins), len(host.out_shape)

    def hosted(*refs):
        c_in, x_in = refs[:n_in], refs[n_in:n_in + nx_in]
        o0 = n_in + nx_in
        c_out, x_out = refs[o0:o0 + n_out], refs[o0 + n_out:o0 + n_out + nx_out]
        s0 = o0 + n_out + nx_out
        start, finish = host.programs(x_in, x_out, refs[s0 + n_scr], refs[s0 + n_scr + 1])
        first = functools.reduce(jnp.logical_and, [pl.program_id(d) == 0 for d in range(len(grid))])
        last = functools.reduce(jnp.logical_and, [pl.program_id(d) == grid[d] - 1 for d in range(len(grid))])
        pl.when(first)(start)
        body(*c_in, *c_out, *refs[s0:s0 + n_scr])
        pl.when(last)(finish)

    outs = pl.pallas_call(
        hosted, name=name, grid=grid, in_specs=list(in_specs) + [any_spec] * nx_in,
        out_specs=out_specs + [any_spec] * nx_out, out_shape=out_shape + host.out_shape,
        scratch_shapes=list(scratch_shapes) + host.sems(),
        compiler_params=host.cparams(dimension_semantics=sem, vmem_limit_bytes=VMEM_LIMIT))(*args, *host.ins)
    host.outs = outs[n_out:]
    return outs[0] if single else outs[:n_out]


NN, NT, TN = ((1,), (0,)), ((1,), (1,)), ((0,), (0,))


def _shift_dims(dims, batch):
    if not batch:
        return (dims, ((), ()))
    return (((dims[0][0] + 1,), (dims[1][0] + 1,)), ((0,), (0,)))


def _make_mm(dims, exact, batch=False):
    def raw(a, b, d):
        dn = _shift_dims(d, batch)
        if exact == "split_lhs":
            ah, bh = a.astype(BF16), b.astype(BF16)
            al = (a - ah.astype(F32)).astype(BF16)
            return lax.dot_general(ah, bh, dn, preferred_element_type=F32) + lax.dot_general(al, bh, dn,
                                                                                              preferred_element_type=F32)
        if exact == "split":
            ah, bh = a.astype(BF16), b.astype(BF16)
            al, bl = (a - ah.astype(F32)).astype(BF16), (b - bh.astype(F32)).astype(BF16)
            dot = lambda p, q: lax.dot_general(p, q, dn, preferred_element_type=F32)
            return dot(ah, bh) + (dot(ah, bl) + dot(al, bh))
        if exact:
            return lax.dot_general(a.astype(F32), b.astype(F32), dn, precision=HI, preferred_element_type=F32)
        return lax.dot_general(a.astype(BF16), b.astype(BF16), dn, preferred_element_type=F32)

    @jax.custom_vjp
    def mm(a, b):
        return raw(a, b, dims)

    def fwd(a, b):
        return raw(a, b, dims), (a, b)

    def bwd(res, g):
        a, b = res
        if dims == NN:
            da, db = raw(g, b, NT), raw(a, g, TN)
        elif dims == NT:
            da, db = raw(g, b, NN), raw(g, a, TN)
        else:
            da, db = raw(b, g, NT), raw(a, g, NN)
        return da.astype(a.dtype), db.astype(b.dtype)

    mm.defvjp(fwd, bwd)
    return mm


_mm = _make_mm(NN, False)
_mm_nt = _make_mm(NT, False)
_mm_tn = _make_mm(TN, False)
_mmx = _make_mm(NN, "split_lhs")
_bmm = _make_mm(NN, False, batch=True)
_bmm_nt = _make_mm(NT, False, batch=True)
_bmm_tn = _make_mm(TN, False, batch=True)
_bmmx = _make_mm(NN, True, batch=True)
_bmms = _make_mm(NN, "split", batch=True)
_bmms_nt = _make_mm(NT, "split", batch=True)
_bmms_tn = _make_mm(TN, "split", batch=True)


@jax.custom_vjp
def _unit_lower_inv(a):
    n = a.shape[-1]
    eye = (lax.broadcasted_iota(jnp.int32, a.shape, 1) == lax.broadcasted_iota(jnp.int32, a.shape, 2)).astype(F32)
    x = -a
    t = eye + x
    for _ in range(max(n.bit_length() - 2, 0)):
        x = _bmms(x, x)
        t = t + _bmms(t, x)
    return t


def _unit_lower_inv_fwd(a):
    t = _unit_lower_inv(a)
    return t, t


def _unit_lower_inv_bwd(t, g):
    return (-_bmms_tn(t, _bmms_nt(g, t)),)


_unit_lower_inv.defvjp(_unit_lower_inv_fwd, _unit_lower_inv_bwd)


def _scan_chunk_rows(x, reverse):
    nb, c, w = x.shape
    y = x.reshape(nb * c, w)
    pos = lax.broadcasted_iota(jnp.int32, y.shape, 0) % c
    step = 1
    while step < c:
        if reverse:
            y = y + jnp.where(pos < c - step, pltpu.roll(y, nb * c - step, 0), 0.0)
        else:
            y = y + jnp.where(pos >= step, pltpu.roll(y, step, 0), 0.0)
        step *= 2
    return y.reshape(nb, c, w)


@jax.custom_vjp
def _chunk_cumsum(x):
    return _scan_chunk_rows(x, False)


_chunk_cumsum.defvjp(lambda x: (_scan_chunk_rows(x, False), None), lambda _, g: (_scan_chunk_rows(g, True),))


def _rms(x, w, n):
    ms = jnp.sum(x * x, axis=-1, keepdims=True) * (1.0 / n)
    return x * lax.rsqrt(ms + NORM_EPS) * w


def _silu(x):
    return x * jax.nn.sigmoid(x)


def _softplus(x):
    return jnp.maximum(x, 0.0) + jnp.log(1.0 + jnp.exp(-jnp.abs(x)))


def _rope(x, cos, sin, perm):
    return x * cos + _mmx(x, perm) * sin


def _mla_prep_fn(rows, consts):
    q_lat, kv_lat, k_pe, cos, sin = rows
    qn = _rms(q_lat, consts["qa_w"], LORA)
    kvn = _rms(kv_lat, consts["kva_w"], LORA)
    outs = []
    for h in range(HEADS):
        q_n = _mm_nt(qn, consts["wq_n"][h])
        q_r = _mm_nt(qn, consts["wq_r"][h])
        rs = lax.rsqrt((jnp.sum(q_n * q_n, -1, keepdims=True) + jnp.sum(q_r * q_r, -1, keepdims=True)) * (1.0 / QK_DIM)
                       + NORM_EPS)
        q_n = q_n * rs * consts["qn_n"]
        q_r = _rope(q_r * rs * consts["qn_r"], cos, sin, consts["perm"])
        k_n = _mm_nt(kvn, consts["wk_n"][h])
        v = _mm_nt(kvn, consts["wv"][h])
        rk = lax.rsqrt((jnp.sum(k_n * k_n, -1, keepdims=True) + jnp.sum(k_pe * k_pe, -1, keepdims=True)) * (1.0 / QK_DIM)
                       + NORM_EPS)
        k_n = k_n * rk * consts["kn_n"]
        k_r = _rope(k_pe * rk * consts["kn_r"], cos, sin, consts["perm"])
        outs += [q_n, q_r, k_n, k_r, v]
    return tuple(outs)


def _attn_fn(q, k, v, row0):
    s = _mm_nt(q, k) * (1.0 / math.sqrt(QK_DIM))
    qpos = row0 + lax.broadcasted_iota(jnp.int32, s.shape, 0)
    kpos = lax.broadcasted_iota(jnp.int32, s.shape, 1)
    s = jnp.where(kpos <= qpos, s, -1e30)
    m = lax.stop_gradient(jnp.max(s, axis=-1, keepdims=True))
    p = jnp.exp(s - m)
    p = p / jnp.sum(p, axis=-1, keepdims=True)
    return _mm(p, v)


def _dn_prep_fn(rows, consts):
    qc, kc, ab = rows
    a_b = _mmx(ab, consts["sel_a"])
    b_b = _mmx(ab, consts["sel_b"])
    beta = jax.nn.sigmoid(b_b)
    g = -jnp.exp(consts["alog"]) * _softplus(a_b + consts["dtb"])
    qs, ks = [], []
    for h in range(HEADS):
        q, k = qc[h], kc[h]
        qs.append(q * lax.rsqrt(jnp.sum(q * q, -1, keepdims=True) + NORM_EPS))
        ks.append(k * lax.rsqrt(jnp.sum(k * k, -1, keepdims=True) + NORM_EPS))
    return tuple(qs), tuple(ks), g, beta


def _dn_chunk_fn(q, k, v, gb, g64, bb):
    nb = q.shape[0]
    ri = lax.broadcasted_iota(jnp.int32, (nb, CHUNK, CHUNK), 1)
    ci = lax.broadcasted_iota(jnp.int32, (nb, CHUNK, CHUNK), 2)
    tri = ri >= ci
    strict = ri > ci
    tril = tri.astype(F32)
    eye = (ri == ci).astype(F32)
    ones = jnp.ones((nb, CHUNK, CHUNK), F32)
    gc = _chunk_cumsum(gb)
    gc64 = _chunk_cumsum(g64)
    grow = _bmmx(ones, eye * gc64)
    diff = gc64 - grow
    decay = jnp.where(tri, jnp.exp(jnp.where(tri, diff, 0.0)), 0.0)
    kb = k * bb
    vb = v * bb
    a = jnp.where(strict, _bmm_nt(kb, k) * decay, 0.0)
    tinv = _unit_lower_inv(a)
    u = _bmm(tinv, vb)
    w = _bmm(tinv, kb * jnp.exp(gc))
    qs = q * (1.0 / math.sqrt(HEAD))
    qk = _bmm_nt(qs, k) * decay
    qg = qs * jnp.exp(gc)
    glast = jnp.sum(gb, axis=1, keepdims=True)
    kdec = k * jnp.exp(glast - gc)
    n_mat = _bmm_tn(kdec, w)
    b_mat = _bmm_tn(kdec, u)
    q_eff = qg - _bmm(qk, w)
    o_own = _bmm(qk, u)
    return n_mat, b_mat, q_eff, o_own, jnp.exp(glast)


def _dn_rec_fn(s, n_mat, b_mat, eg):
    return s * eg - _mm(n_mat, s) + b_mat


def _dn_o_fn(s, q_eff, o_own):
    return _bmm(q_eff, s) + o_own


def _dn_out_fn(o, z, w):
    return _rms(o, w, HEAD) * _silu(z)


def _row_tile(t, parts=8):
    return t // parts if (t // parts) % 16 == 0 else t


def _tile(n, pref, unit):
    best = n
    for cand in range(unit, min(n, pref) + 1, unit):
        if n % cand == 0:
            best = cand
    return best if best <= pref else n


def _rows_call(name, body, rows, consts, outs, accs, r, host=None):
    rows = [a if isinstance(a, tuple) else (a, a.shape[1], 0) for a in rows]
    t = rows[0][0].shape[0]
    zero = lambda nd: (lambda i: (0,) * nd)
    in_specs = [pl.BlockSpec((r, w), functools.partial(lambda i, b: (i, b), b=blk)) for _, w, blk in rows]
    rows = [a for a, _, _ in rows]
    in_specs += [pl.BlockSpec(a.shape, zero(a.ndim)) for a in consts]
    out_shape = [jax.ShapeDtypeStruct((t, w), dt) for w, dt in outs] + [jax.ShapeDtypeStruct(s, F32) for s in accs]
    out_specs = [pl.BlockSpec((r, w), lambda i: (i, 0)) for w, _ in outs] + [pl.BlockSpec(s, zero(len(s))) for s in accs]
    return _pcall(body, name, (t // r,), in_specs, out_specs, out_shape, [*rows, *consts], ("arbitrary",), host=host)


def _accumulate(ref, val):
    @pl.when(pl.program_id(0) == 0)
    def _():
        ref[...] = jnp.zeros(ref.shape, ref.dtype)

    ref[...] += val


def _matmul(name, a, b, dims, out_dtype, res=None, host=None):
    if dims == "nn":
        (m, k), n = a.shape, b.shape[1]
    elif dims == "nt":
        (m, k), n = a.shape, b.shape[0]
    else:
        (k, m), n = a.shape, b.shape[1]
    tm = _tile(m, 1100, 16) if dims != "tn" else _tile(m, 640, 128)
    tn = _tile(n, 1408, 128)
    if dims == "nn":
        a_spec, b_spec, dn = pl.BlockSpec((tm, k), lambda i, j: (i, 0)), pl.BlockSpec((k, tn), lambda i, j: (0, j)), NN
    elif dims == "nt":
        a_spec, b_spec, dn = pl.BlockSpec((tm, k), lambda i, j: (i, 0)), pl.BlockSpec((tn, k), lambda i, j: (j, 0)), NT
    else:
        a_spec, b_spec, dn = pl.BlockSpec((k, tm), lambda i, j: (0, i)), pl.BlockSpec((k, tn), lambda i, j: (0, j)), TN
    o_spec = pl.BlockSpec((tm, tn), lambda i, j: (i, j))

    def body(*refs):
        a_ref, b_ref, o_ref = refs[0], refs[1], refs[-1]
        acc = lax.dot_general(a_ref[...].astype(BF16), b_ref[...].astype(BF16), (dn, ((), ())),
                              preferred_element_type=F32)
        if res is not None:
            acc = acc + refs[2][...]
        o_ref[...] = acc.astype(out_dtype)

    ins = [a, b] + ([res] if res is not None else [])
    specs = [a_spec, b_spec] + ([o_spec] if res is not None else [])
    return _pcall(body, name, (m // tm, n // tn), specs, o_spec, jax.ShapeDtypeStruct((m, n), out_dtype), ins,
                  ("arbitrary", "arbitrary"), host=host)


def _rms_fwd(name, h, w, host=None):
    n = h.shape[1]

    def body(h_ref, w_ref, o_ref):
        o_ref[...] = _rms(h_ref[...], w_ref[...], n).astype(BF16)

    return _rows_call(name, body, [h], [w], [(n, BF16)], [], _row_tile(h.shape[0]), host=host)[0]


def _rms_bwd(name, h, w, cts, resid, host=None):
    n = h.shape[1]
    nct = len(cts)

    def body(*refs):
        h_ref, ct_refs, r_ref, w_ref = refs[0], refs[1:1 + nct], refs[1 + nct], refs[2 + nct]
        dh_ref, dh16_ref, dw_ref = refs[-3], refs[-2], refs[-1]
        ct = ct_refs[0][...].astype(F32)
        for c in ct_refs[1:]:
            ct = ct + c[...].astype(F32)
        _, vjp = jax.vjp(lambda x, ww: _rms(x, ww, n), h_ref[...], w_ref[...])
        dh, dw = vjp(ct)
        dh = dh + r_ref[...]
        dh_ref[...] = dh
        dh16_ref[...] = dh.astype(BF16)
        _accumulate(dw_ref, dw)

    return _rows_call(name, body, [h, *cts, resid], [w], [(n, F32), (n, BF16)], [(1, n)], _row_tile(h.shape[0]), host=host)


def _mla_consts_from_refs(qa, wq, kva, wkv, qn, kn, perm):
    f = lambda r: r[...].astype(F32)
    return dict(
        qa_w=f(qa), kva_w=f(kva), perm=f(perm),
        wq_n=[wq[h * QK_PAD:h * QK_PAD + HEAD, :].astype(F32) for h in range(HEADS)],
        wq_r=[wq[h * QK_PAD + HEAD:(h + 1) * QK_PAD, :].astype(F32) for h in range(HEADS)],
        wk_n=[wkv[h * QK_PAD:h * QK_PAD + HEAD, :].astype(F32) for h in range(HEADS)],
        wv=[wkv[h * QK_PAD + HEAD:(h + 1) * QK_PAD, :].astype(F32) for h in range(HEADS)],
        qn_n=qn[:, 0:HEAD], qn_r=qn[:, HEAD:QK_PAD], kn_n=kn[:, 0:HEAD], kn_r=kn[:, HEAD:QK_PAD])


def _mla_prep_fwd(q_lat, kv_lat, k_pe, cos, sin, qa, wq, kva, wkv, qn, kn, perm):
    def body(ql, kvl, kp, c, s, qa_r, wq_r, kva_r, wkv_r, qn_r, kn_r, p_r, q_out, k_out, v_out):
        consts = _mla_consts_from_refs(qa_r, wq_r, kva_r, wkv_r, qn_r, kn_r, p_r)
        outs = _mla_prep_fn((ql[...], kvl[...], kp[...], c[...], s[...]), consts)
        for h in range(HEADS):
            q_n, q_r, k_n, k_r, v = outs[5 * h:5 * h + 5]
            q_out[:, h * QK_PAD:h * QK_PAD + HEAD] = q_n.astype(BF16)
            q_out[:, h * QK_PAD + HEAD:(h + 1) * QK_PAD] = q_r.astype(BF16)
            k_out[:, h * QK_PAD:h * QK_PAD + HEAD] = k_n.astype(BF16)
            k_out[:, h * QK_PAD + HEAD:(h + 1) * QK_PAD] = k_r.astype(BF16)
            v_out[:, h * HEAD:(h + 1) * HEAD] = v.astype(BF16)

    return _rows_call("mla_prep_fwd", body, [q_lat, kv_lat, k_pe, cos, sin], [qa, wq, kva, wkv, qn, kn, perm],
                      [(HEADS * QK_PAD, BF16), (HEADS * QK_PAD, BF16), (DN_WIDTH, BF16)], [], _row_tile(cos.shape[0], 4))


def _mla_prep_bwd(q_lat, kv_lat, k_pe, cos, sin, dq, dk, dv, qa, wq, kva, wkv, qn, kn, perm, host=None):
    def body(ql, kvl, kp, c, s, dq_r, dk_r, dv_r, qa_r, wq_r, kva_r, wkv_r, qn_r, kn_r, p_r,
             dql, dkvl, dkp, dqa, dwq, dkva, dwkv, dqn, dkn):
        consts = _mla_consts_from_refs(qa_r, wq_r, kva_r, wkv_r, qn_r, kn_r, p_r)
        cc, ss, pm = c[...], s[...], consts.pop("perm")
        _, vjp = jax.vjp(lambda rows, cs: _mla_prep_fn((*rows, cc, ss), dict(cs, perm=pm)), (ql[...], kvl[...], kp[...]),
                         consts)
        cts = []
        for h in range(HEADS):
            cts += [dq_r[:, h * QK_PAD:h * QK_PAD + HEAD], dq_r[:, h * QK_PAD + HEAD:(h + 1) * QK_PAD],
                    dk_r[:, h * QK_PAD:h * QK_PAD + HEAD], dk_r[:, h * QK_PAD + HEAD:(h + 1) * QK_PAD],
                    dv_r[:, h * HEAD:(h + 1) * HEAD]]
        (d_ql, d_kvl, d_kp), dc = vjp(tuple(cts))
        dql[...] = d_ql.astype(BF16)
        dkvl[...] = d_kvl.astype(BF16)
        dkp[...] = d_kp.astype(BF16)
        first = pl.program_id(0) == 0

        def acc(ref, sl, val):
            @pl.when(first)
            def _():
                ref[sl] = val

            @pl.when(jnp.logical_not(first))
            def _():
                ref[sl] += val

        full = (slice(None), slice(None))
        acc(dqa, full, dc["qa_w"])
        acc(dkva, full, dc["kva_w"])
        for h in range(HEADS):
            acc(dwq, (slice(h * QK_PAD, h * QK_PAD + HEAD), slice(None)), dc["wq_n"][h])
            acc(dwq, (slice(h * QK_PAD + HEAD, (h + 1) * QK_PAD), slice(None)), dc["wq_r"][h])
            acc(dwkv, (slice(h * QK_PAD, h * QK_PAD + HEAD), slice(None)), dc["wk_n"][h])
            acc(dwkv, (slice(h * QK_PAD + HEAD, (h + 1) * QK_PAD), slice(None)), dc["wv"][h])
        acc(dqn, (slice(None), slice(0, HEAD)), dc["qn_n"])
        acc(dqn, (slice(None), slice(HEAD, QK_PAD)), dc["qn_r"])
        acc(dkn, (slice(None), slice(0, HEAD)), dc["kn_n"])
        acc(dkn, (slice(None), slice(HEAD, QK_PAD)), dc["kn_r"])

    return _rows_call("mla_prep_bwd", body, [q_lat, kv_lat, k_pe, cos, sin, dq, dk, dv],
                      [qa, wq, kva, wkv, qn, kn, perm],
                      [(LORA, BF16), (LORA, BF16), (HEAD, BF16)],
                      [(1, LORA), wq.shape, (1, LORA), wkv.shape, (1, QK_PAD), (1, QK_PAD)], _row_tile(cos.shape[0], 4),
                      host=host)


ATTN_Q_ROWS = 256


def _attn_blocks(t):
    return [(r0, min(ATTN_Q_ROWS, t - r0)) for r0 in range(0, t, ATTN_Q_ROWS)]


def _attn_fwd(q, k, v, host=None):
    t = q.shape[0]

    def body(q_ref, k_ref, v_ref, o_ref):
        for r0, rows in _attn_blocks(t):
            ext = r0 + rows
            o_ref[r0:ext, :] = _attn_fn(q_ref[r0:ext, :], k_ref[0:ext, :], v_ref[0:ext, :], r0)

    qk_spec = pl.BlockSpec((t, QK_PAD), lambda h: (0, h))
    v_spec = pl.BlockSpec((t, HEAD), lambda h: (0, h))
    return _pcall(body, "attn_fwd", (HEADS,), [qk_spec, qk_spec, v_spec], v_spec,
                  jax.ShapeDtypeStruct((t, HEADS * HEAD), F32), [q, k, v], ("arbitrary",), host=host)


def _attn_bwd(q, k, v, do, host=None):
    t = q.shape[0]

    def body(q_ref, k_ref, v_ref, do_ref, dq_ref, dk_ref, dv_ref):
        dk_ref[...] = jnp.zeros(dk_ref.shape, F32)
        dv_ref[...] = jnp.zeros(dv_ref.shape, F32)
        for r0, rows in _attn_blocks(t):
            ext = r0 + rows
            _, vjp = jax.vjp(functools.partial(_attn_fn, row0=r0), q_ref[r0:ext, :].astype(F32),
                             k_ref[0:ext, :].astype(F32), v_ref[0:ext, :].astype(F32))
            dq, dk, dv = vjp(do_ref[r0:ext, :])
            dq_ref[r0:ext, :] = dq
            dk_ref[0:ext, :] += dk
            dv_ref[0:ext, :] += dv

    qk_spec = pl.BlockSpec((t, QK_PAD), lambda h: (0, h))
    v_spec = pl.BlockSpec((t, HEAD), lambda h: (0, h))
    return _pcall(body, "attn_bwd", (HEADS,), [qk_spec, qk_spec, v_spec, v_spec], [qk_spec, qk_spec, v_spec],
                  [jax.ShapeDtypeStruct((t, HEADS * QK_PAD), F32), jax.ShapeDtypeStruct((t, HEADS * QK_PAD), F32),
                   jax.ShapeDtypeStruct((t, HEADS * HEAD), F32)], [q, k, v, do], ("arbitrary",), host=host)


def _mix_out_proj(o_mla, o_dn, z, w_mla, w_dn, w_out, h0, w_ffn):
    def body(om_ref, od_ref, z_ref, h0_ref, wm_ref, wd_ref, wo_ref, wf_ref, mixed_ref, h1_ref, n2_ref):
        for h in range(HEADS):
            sl = slice(h * HEAD, (h + 1) * HEAD)
            mixed_ref[:, sl] = _rms(om_ref[:, sl], wm_ref[...], HEAD).astype(BF16)
            mixed_ref[:, DN_WIDTH + h * HEAD:DN_WIDTH + (h + 1) * HEAD] = _dn_out_fn(od_ref[:, sl], z_ref[:, sl],
                                                                                     wd_ref[...]).astype(BF16)
        h1 = _mm(mixed_ref[...], wo_ref[...]) + h0_ref[...]
        h1_ref[...] = h1
        n2_ref[...] = _rms(h1, wf_ref[...], D_MODEL).astype(BF16)

    return _rows_call("mix_out_proj", body, [o_mla, o_dn, z, h0], [w_mla, w_dn, w_out, w_ffn],
                      [(D_MODEL, BF16), (D_MODEL, F32), (D_MODEL, BF16)], [], _row_tile(o_mla.shape[0], 4))


def _down_proj_loss(act, w_down, h1, tgt, n_valid):
    t, n = h1.shape
    r = _row_tile(t, 4)

    def body(a_ref, h_ref, t_ref, w_ref, dy_ref, dy16_ref, acc_ref):
        h2 = _mm(a_ref[...], w_ref[...]) + h_ref[...]
        rows = pl.program_id(0) * r + lax.broadcasted_iota(jnp.int32, (r, n), 0)
        valid = jnp.logical_and(rows >= N_META, rows < n_valid)
        e = jnp.where(valid, h2 - t_ref[...], 0.0)
        dy = e * (1.0 / n)
        dy_ref[...] = dy
        dy16_ref[...] = dy.astype(BF16)
        _accumulate(acc_ref, jnp.sum(e * e, axis=0, keepdims=True))

    return _rows_call("down_proj_loss", body, [act, h1, tgt], [w_down], [(n, F32), (n, BF16)], [(1, n)], r)


def _ffn_in_bwd(dgpre, dup, w_gate_t, w_up_t, h1, dy, w_ffn, host=None):
    n = h1.shape[1]

    def body(dg_ref, du_ref, h_ref, dy_ref, wg_ref, wu_ref, w_ref, dh_ref, dh16_ref, dw_ref):
        ct = _mm(dg_ref[...], wg_ref[...]) + _mm(du_ref[...], wu_ref[...])
        _, vjp = jax.vjp(lambda x, ww: _rms(x, ww, n), h_ref[...], w_ref[...])
        dh, dw = vjp(ct)
        dh = dh + dy_ref[...]
        dh_ref[...] = dh
        dh16_ref[...] = dh.astype(BF16)
        _accumulate(dw_ref, dw)

    return _rows_call("ffn_in_bwd", body, [dgpre, dup, h1, dy], [w_gate_t, w_up_t, w_ffn], [(n, F32), (n, BF16)], [(1, n)],
                      _row_tile(h1.shape[0]), host=host)


def _mix_out_bwd(o_mla, o_dn, z, dh1, w_out, w_mla, w_dn, host=None):
    def body(om_ref, od_ref, z_ref, dh_ref, wo_ref, wm_ref, wd_ref, dom_ref, dod_ref, dz_ref, dwm_ref, dwd_ref):
        dwm = dwd = None
        for h in range(HEADS):
            sl = slice(h * HEAD, (h + 1) * HEAD)
            _, vjp = jax.vjp(lambda o, w: _rms(o, w, HEAD), om_ref[:, sl], wm_ref[...])
            do, dw = vjp(_mm_nt(dh_ref[...], wo_ref[sl, :]))
            dom_ref[:, sl] = do
            dwm = dw if dwm is None else dwm + dw
            _, vjp = jax.vjp(_dn_out_fn, od_ref[:, sl], z_ref[:, sl], wd_ref[...])
            do, dz, dw = vjp(_mm_nt(dh_ref[...], wo_ref[DN_WIDTH + h * HEAD:DN_WIDTH + (h + 1) * HEAD, :]))
            dod_ref[:, sl] = do
            dz_ref[:, sl] = dz.astype(BF16)
            dwd = dw if dwd is None else dwd + dw
        _accumulate(dwm_ref, dwm)
        _accumulate(dwd_ref, dwd)

    return _rows_call("mix_out_bwd", body, [o_mla, o_dn, z, dh1], [w_out, w_mla, w_dn],
                      [(DN_WIDTH, F32), (DN_WIDTH, F32), (DN_WIDTH, BF16)], [(1, HEAD), (1, HEAD)],
                      _row_tile(o_mla.shape[0], 4), host=host)


def _shift_down(x, s):
    if s == 0:
        return x
    rows = lax.broadcasted_iota(jnp.int32, x.shape, 0)
    return jnp.where(rows >= s, pltpu.roll(x, s, 0), 0.0)


def _shift_up(x, s):
    if s == 0:
        return x
    t = x.shape[0]
    rows = lax.broadcasted_iota(jnp.int32, x.shape, 0)
    return jnp.where(rows < t - s, pltpu.roll(x, t - s, 0), 0.0)


def _col_call(name, body, cols, taps, outs, tap_outs, cw, host=None):
    t, c = cols[0].shape[0], taps[0].shape[1]
    in_specs = [pl.BlockSpec((t, cw), lambda j: (0, j)) for _ in cols]
    in_specs += [pl.BlockSpec((a.shape[0], cw), lambda j: (0, j)) for a in taps]
    out_shape = [jax.ShapeDtypeStruct((t, c), dt) for dt in outs] + [jax.ShapeDtypeStruct((n, c), F32) for n in tap_outs]
    out_specs = [pl.BlockSpec((t, cw), lambda j: (0, j)) for _ in outs]
    out_specs += [pl.BlockSpec((n, cw), lambda j: (0, j)) for n in tap_outs]
    return _pcall(body, name, (c // cw,), in_specs, out_specs, out_shape, [*cols, *taps], ("arbitrary",), host=host)


def _causal_conv(x, w_ref, width):
    acc = w_ref[width - 1:width, :] * x
    for j in range(width - 1):
        acc = acc + w_ref[j:j + 1, :] * _shift_down(x, width - 1 - j)
    return acc


def _causal_conv_bwd(x, dpre, w_ref, dx_ref, dw_ref, width):
    dx = w_ref[width - 1:width, :] * dpre
    dw_ref[width - 1:width, :] = jnp.sum(dpre * x, axis=0, keepdims=True)
    for j in range(width - 1):
        s = width - 1 - j
        dx = dx + w_ref[j:j + 1, :] * _shift_up(dpre, s)
        dw_ref[j:j + 1, :] = jnp.sum(dpre * _shift_down(x, s), axis=0, keepdims=True)
    dx_ref[...] = dx.astype(dx_ref.dtype)


def _dsilu(x):
    sg = jax.nn.sigmoid(x)
    return sg * (1.0 + x * (1.0 - sg))


def _dn_conv_fwd(x, w):
    def body(x_ref, w_ref, y_ref):
        y_ref[...] = _silu(_causal_conv(x_ref[...], w_ref, 4))

    return _col_call("dn_conv_fwd", body, [x], [w], [F32], [], 256)[0]


def _dn_conv_bwd(x, w, dy):
    def body(x_ref, dy_ref, w_ref, dx_ref, dw_ref):
        xv = x_ref[...]
        dpre = dy_ref[...] * _dsilu(_causal_conv(xv, w_ref, 4))
        _causal_conv_bwd(xv, dpre, w_ref, dx_ref, dw_ref, 4)

    return _col_call("dn_conv_bwd", body, [x, dy], [w], [BF16], [4], 256)


def _ffn_glu_fwd(n2, w_gate_t, w_up_t, w, b, host=None):
    t, k = n2.shape
    c, cw = w_gate_t.shape[0], 256

    def body(n_ref, wg_ref, wu_ref, w_ref, b_ref, g_ref, u_ref, a_ref):
        nv = n_ref[...]
        g16 = _mm_nt(nv, wg_ref[...]).astype(BF16)
        u16 = _mm_nt(nv, wu_ref[...]).astype(BF16)
        g_ref[...] = g16
        u_ref[...] = u16
        gate = _causal_conv(g16.astype(F32), w_ref, 3) + b_ref[...]
        a_ref[...] = (_silu(gate) * u16.astype(F32)).astype(BF16)

    wspec = pl.BlockSpec((cw, k), lambda j: (j, 0))
    col = pl.BlockSpec((t, cw), lambda j: (0, j))
    in_specs = [pl.BlockSpec((t, k), lambda j: (0, 0)), wspec, wspec, pl.BlockSpec((w.shape[0], cw), lambda j: (0, j)),
                pl.BlockSpec((1, cw), lambda j: (0, j))]
    return _pcall(body, "ffn_glu_fwd", (c // cw,), in_specs, [col] * 3, [jax.ShapeDtypeStruct((t, c), BF16)] * 3,
                  [n2, w_gate_t, w_up_t, w, b], ("arbitrary",), host=host)


def _glu_bwd(gpre, up, w, b, dact):
    def body(g_ref, u_ref, da_ref, w_ref, b_ref, dg_ref, du_ref, dw_ref, db_ref):
        gv = g_ref[...].astype(F32)
        gate = _causal_conv(gv, w_ref, 3) + b_ref[...]
        da = da_ref[...].astype(F32)
        sg = jax.nn.sigmoid(gate)
        du_ref[...] = (da * (gate * sg)).astype(BF16)
        dgate = da * u_ref[...].astype(F32) * (sg * (1.0 + gate * (1.0 - sg)))
        db_ref[...] = jnp.sum(dgate, axis=0, keepdims=True)
        _causal_conv_bwd(gv, dgate, w_ref, dg_ref, dw_ref, 3)

    return _col_call("glu_bwd", body, [gpre, up, dact], [w, b], [BF16, BF16], [3, 1], 256)


def _dn_prep_consts(sa, sb, al, dt):
    return dict(sel_a=sa[...], sel_b=sb[...], alog=al[...], dtb=dt[...])


def _dn_prep_fwd(conv, ab, sel_a, sel_b, alog, dtb):
    def body(c_ref, ab_ref, sa, sb, al, dt, q_out, k_out, g_out, b_out):
        qc = tuple(c_ref[:, h * HEAD:(h + 1) * HEAD] for h in range(HEADS))
        kc = tuple(c_ref[:, DN_WIDTH + h * HEAD:DN_WIDTH + (h + 1) * HEAD] for h in range(HEADS))
        qs, ks, g, beta = _dn_prep_fn((qc, kc, ab_ref[...]), _dn_prep_consts(sa, sb, al, dt))
        for h in range(HEADS):
            q_out[:, h * HEAD:(h + 1) * HEAD] = qs[h]
            k_out[:, h * HEAD:(h + 1) * HEAD] = ks[h]
        g_out[...] = g
        b_out[...] = beta

    return _rows_call("dn_prep_fwd", body, [conv, ab], [sel_a, sel_b, alog, dtb], [(DN_WIDTH, F32)] * 4, [],
                      _row_tile(conv.shape[0]))


def _dn_prep_bwd(conv, ab, dq, dk, dv, dg, db, sel_a, sel_b, alog, dtb):
    def body(c_ref, ab_ref, dq_r, dk_r, dv_r, dg_r, db_r, sa, sb, al, dt, dc_out, dab_out, dal_out, ddt_out):
        qc = tuple(c_ref[:, h * HEAD:(h + 1) * HEAD] for h in range(HEADS))
        kc = tuple(c_ref[:, DN_WIDTH + h * HEAD:DN_WIDTH + (h + 1) * HEAD] for h in range(HEADS))
        consts = _dn_prep_consts(sa, sb, al, dt)
        sel = dict(sel_a=consts["sel_a"], sel_b=consts["sel_b"])
        _, vjp = jax.vjp(lambda rows, ad: _dn_prep_fn(rows, {**sel, **ad}), (qc, kc, ab_ref[...]),
                         dict(alog=consts["alog"], dtb=consts["dtb"]))
        cq = tuple(dq_r[:, h * HEAD:(h + 1) * HEAD] for h in range(HEADS))
        ck = tuple(dk_r[:, h * HEAD:(h + 1) * HEAD] for h in range(HEADS))
        (dqc, dkc, dab), dad = vjp((cq, ck, dg_r[...], db_r[...]))
        for h in range(HEADS):
            dc_out[:, h * HEAD:(h + 1) * HEAD] = dqc[h]
            dc_out[:, DN_WIDTH + h * HEAD:DN_WIDTH + (h + 1) * HEAD] = dkc[h]
        dc_out[:, 2 * DN_WIDTH:3 * DN_WIDTH] = dv_r[...]
        dab_out[...] = dab.astype(BF16)
        _accumulate(dal_out, dad["alog"])
        _accumulate(ddt_out, dad["dtb"])

    return _rows_call("dn_prep_bwd", body, [conv, ab, dq, dk, dv, dg, db], [sel_a, sel_b, alog, dtb],
                      [(3 * DN_WIDTH, F32), (HEAD, BF16)], [(1, DN_WIDTH), (1, DN_WIDTH)], _row_tile(conv.shape[0]))


def _chunk_batch(t):
    nc = t // CHUNK
    return nc // 2 if nc % 2 == 0 else nc


def _dn_chunk_specs(t, nb):
    rows = nb * CHUNK
    blk = pl.BlockSpec((rows, HEAD), lambda h, b: (b, h))
    vblk = pl.BlockSpec((rows, HEAD), lambda h, b: (b, 2 * HEADS + h))
    mat = pl.BlockSpec((nb, HEAD, HEAD), lambda h, b: (b, h, 0))
    return rows, blk, vblk, mat


def _dn_chunk_fwd(qn, kn, conv, g, beta, host=None):
    t = qn.shape[0]
    nb = _chunk_batch(t)
    rows, blk, vblk, mat = _dn_chunk_specs(t, nb)

    def body(q_ref, k_ref, v_ref, g_ref, b_ref, n_o, b_o, qe_o, oo_o, eg_o):
        r3 = lambda x: x.reshape(nb, CHUNK, x.shape[-1])
        n_mat, b_mat, q_eff, o_own, eg = _dn_chunk_fn(r3(q_ref[...]), r3(k_ref[...]), r3(v_ref[...]), r3(g_ref[...]),
                                                      r3(g_ref[:, 0:CHUNK]), r3(b_ref[...]))
        n_o[...] = n_mat
        b_o[...] = b_mat
        qe_o[...] = q_eff.reshape(rows, HEAD)
        oo_o[...] = o_own.reshape(rows, HEAD)
        eg_o[...] = jnp.broadcast_to(eg, (nb, HEAD, HEAD))

    nc = t // CHUNK
    mats = jax.ShapeDtypeStruct((nc, DN_WIDTH, HEAD), F32)
    rowsd = jax.ShapeDtypeStruct((t, DN_WIDTH), F32)
    return _pcall(body, "dn_chunk_fwd", (HEADS, t // rows), [blk, blk, vblk, blk, blk], [mat, mat, blk, blk, mat],
                  [mats, mats, rowsd, rowsd, mats], [qn, kn, conv, g, beta], ("arbitrary", "arbitrary"), host=host)


def _dn_chunk_bwd(qn, kn, conv, g, beta, sall, gall, dq_eff, do, host=None):
    t = qn.shape[0]
    nb = _chunk_batch(t)
    rows, blk, vblk, mat = _dn_chunk_specs(t, nb)

    def body(q_ref, k_ref, v_ref, g_ref, b_ref, s_ref, ga_ref, dqe_ref, do_ref, dq_o, dk_o, dv_o, dg_o, db_o):
        r3 = lambda x: x.reshape(nb, CHUNK, x.shape[-1])
        _, vjp = jax.vjp(_dn_chunk_fn, r3(q_ref[...]), r3(k_ref[...]), r3(v_ref[...]), r3(g_ref[...]),
                         r3(g_ref[:, 0:CHUNK]), r3(b_ref[...]))
        s, ga = s_ref[...], ga_ref[...]
        d_n = -_bmm_nt(ga, s)
        d_eg = jnp.sum(ga * s, axis=1, keepdims=True)
        dq, dk, dv, dg, dg64, db = vjp((d_n, ga, r3(dqe_ref[...]), r3(do_ref[...]), d_eg))
        for o_ref, val in zip((dq_o, dk_o, dv_o, dg_o, db_o), (dq, dk, dv, dg, db)):
            o_ref[...] = val.reshape(rows, HEAD)
        dg_o[:, 0:CHUNK] += dg64.reshape(rows, CHUNK)

    return _pcall(body, "dn_chunk_bwd", (HEADS, t // rows), [blk, blk, vblk, blk, blk, mat, mat, blk, blk], [blk] * 5,
                  [jax.ShapeDtypeStruct((t, DN_WIDTH), F32)] * 5, [qn, kn, conv, g, beta, sall, gall, dq_eff, do],
                  ("arbitrary", "arbitrary"), host=host)


def _dn_rec_fwd(n_mat, b_mat, eg, host=None):
    nc = n_mat.shape[0]
    nb = _chunk_batch(nc * CHUNK)
    spec = pl.BlockSpec((nb, DN_WIDTH, HEAD), lambda i: (i, 0, 0))

    def body(n_ref, b_ref, eg_ref, sall_ref, s_scr):
        @pl.when(pl.program_id(0) == 0)
        def _():
            s_scr[...] = jnp.zeros(s_scr.shape, F32)

        for j in range(nb):
            sall_ref[j] = s_scr[...]
            for h in range(HEADS):
                sl = slice(h * HEAD, (h + 1) * HEAD)
                s_scr[sl, :] = _dn_rec_fn(s_scr[sl, :], n_ref[j, sl, :], b_ref[j, sl, :],
                                          eg_ref[j, h * HEAD:h * HEAD + 1, :])

    return _pcall(body, "dn_rec_fwd", (nc // nb,), [spec] * 3, spec, jax.ShapeDtypeStruct((nc, DN_WIDTH, HEAD), F32),
                  [n_mat, b_mat, eg], ("arbitrary",), scratch_shapes=[pltpu.VMEM((DN_WIDTH, HEAD), F32)], host=host)


def _dn_rec_bwd(n_mat, eg, ds_out, host=None):
    nc = n_mat.shape[0]
    nb = _chunk_batch(nc * CHUNK)
    steps = nc // nb
    spec = pl.BlockSpec((nb, DN_WIDTH, HEAD), lambda i: (steps - 1 - i, 0, 0))

    def body(n_ref, eg_ref, dso_ref, gall_ref, g_scr):
        @pl.when(pl.program_id(0) == 0)
        def _():
            g_scr[...] = jnp.zeros(g_scr.shape, F32)

        for j in reversed(range(nb)):
            gall_ref[j] = g_scr[...]
            for h in range(HEADS):
                sl = slice(h * HEAD, (h + 1) * HEAD)
                gv = g_scr[sl, :]
                g_scr[sl, :] = (gv * eg_ref[j, h * HEAD:h * HEAD + 1, :] - _mm_tn(n_ref[j, sl, :], gv)
                                + dso_ref[j, sl, :])

    return _pcall(body, "dn_rec_bwd", (steps,), [spec] * 3, spec, jax.ShapeDtypeStruct((nc, DN_WIDTH, HEAD), F32),
                  [n_mat, eg, ds_out], ("arbitrary",), scratch_shapes=[pltpu.VMEM((DN_WIDTH, HEAD), F32)], host=host)


def _dn_o_fwd(sall, q_eff, o_own):
    t = q_eff.shape[0]
    nb = _chunk_batch(t)
    rows, blk, _, mat = _dn_chunk_specs(t, nb)

    def body(s_ref, qe_ref, oo_ref, o_ref):
        r3 = lambda x: x.reshape(nb, CHUNK, HEAD)
        o_ref[...] = _dn_o_fn(s_ref[...], r3(qe_ref[...]), r3(oo_ref[...])).reshape(rows, HEAD)

    return _pcall(body, "dn_o_fwd", (HEADS, t // rows), [mat, blk, blk], blk, jax.ShapeDtypeStruct((t, DN_WIDTH), F32),
                  [sall, q_eff, o_own], ("arbitrary", "arbitrary"))


def _dn_o_bwd(sall, q_eff, do, host=None):
    t = q_eff.shape[0]
    nb = _chunk_batch(t)
    rows, blk, _, mat = _dn_chunk_specs(t, nb)

    def body(s_ref, qe_ref, do_ref, dqe_ref, ds_ref):
        r3 = lambda x: x.reshape(nb, CHUNK, HEAD)
        dov = r3(do_ref[...])
        dqe_ref[...] = _bmm_nt(dov, s_ref[...]).reshape(rows, HEAD)
        ds_ref[...] = _bmm_tn(r3(qe_ref[...]), dov)

    nc = t // CHUNK
    return _pcall(body, "dn_o_bwd", (HEADS, t // rows), [mat, blk, blk], [blk, mat],
                  [jax.ShapeDtypeStruct((t, DN_WIDTH), F32), jax.ShapeDtypeStruct((nc, DN_WIDTH, HEAD), F32)],
                  [sall, q_eff, do], ("arbitrary", "arbitrary"), host=host)


def _adamw_update(w, g, m, v):
    m2 = ADAM_B1 * m + (1.0 - ADAM_B1) * g
    v2 = ADAM_B2 * v + (1.0 - ADAM_B2) * (g * g)
    m_hat = m2 / (1.0 - ADAM_B1 ** ADAM_STEP)
    v_hat = v2 / (1.0 - ADAM_B2 ** ADAM_STEP)
    return -ADAM_LR * (m_hat / (jnp.sqrt(v_hat) + ADAM_EPS) + ADAM_WD * w), m2, v2


def _adamw_small(ws, gs, ms, vs):
    n = len(ws)

    def body(*refs):
        for i in range(n):
            d, m2, v2 = _adamw_update(refs[i][...], refs[n + i][...], refs[2 * n + i][...], refs[3 * n + i][...])
            refs[4 * n + i][...] = d
            refs[5 * n + i][...] = m2
            refs[6 * n + i][...] = v2

    shapes = [jax.ShapeDtypeStruct(a.shape, F32) for a in ws]
    outs = pl.pallas_call(body, name="adamw_small", out_shape=shapes * 3,
                          compiler_params=pltpu.CompilerParams(vmem_limit_bytes=VMEM_LIMIT))(*ws, *gs, *ms, *vs)
    return outs[:n], outs[n:2 * n], outs[2 * n:]


def _adamw_call(name, w, g, m, v, host=None):
    rows, cols = w.shape
    by_rows = rows % 8 == 0

    def body(w_ref, g_ref, m_ref, v_ref, g_out, d_ref, m_out, v_out):
        gv = g_ref[...] if by_rows else g_ref[0:rows, :]
        g_out[...] = gv
        d_ref[...], m_out[...], v_out[...] = _adamw_update(w_ref[...], gv, m_ref[...], v_ref[...])

    if by_rows:
        tr = _tile(rows, 256, 8)
        spec = g_spec = pl.BlockSpec((tr, cols), lambda i: (i, 0))
        grid = (rows // tr,)
    else:
        tc = _tile(cols, 256, 128)
        spec = pl.BlockSpec((rows, tc), lambda j: (0, j))
        g_spec = pl.BlockSpec((g.shape[0], tc), lambda j: (0, j))
        grid = (cols // tc,)
    return _pcall(body, name, grid, [spec, g_spec, spec, spec], [spec] * 4, [jax.ShapeDtypeStruct((rows, cols), F32)] * 4,
                  [w, g, m, v], ("arbitrary",), host=host)


def _rope_tables(t):
    half = ROPE // 2
    inv_freq = np.float32(ROPE_THETA) ** (-np.arange(half, dtype=np.float32) / np.float32(half))
    ang = np.arange(t, dtype=np.float32)[:, None] * inv_freq[None, :].astype(np.float32)
    z = np.zeros((t, HEAD - ROPE), np.float32)
    cos = np.concatenate([np.cos(ang), np.cos(ang), z], axis=1).astype(np.float32)
    sin = np.concatenate([np.sin(ang), np.sin(ang), z], axis=1).astype(np.float32)
    k = np.arange(HEAD)[:, None]
    l = np.arange(HEAD)[None, :]
    perm = np.where((l < half) & (k == l + half), -1.0, 0.0) + np.where((l >= half) & (l < ROPE) & (k == l - half), 1.0, 0.0)
    return jnp.asarray(cos), jnp.asarray(sin), jnp.asarray(perm.astype(np.float32))


def _win_to_pad(w):
    z = lambda n: jnp.zeros((n, w.shape[1]), w.dtype)
    return jnp.concatenate([w[576:2112], w[2112:2624], w[0:256], w[256:512], w[512:576], z(64), w[2624:2632], z(120)],
                           axis=0)


def _win_from_pad(g):
    return jnp.concatenate([g[2048:2304], g[2304:2560], g[2560:2624], g[0:1536], g[1536:2048], g[2688:2696]], axis=0)


def _qk_to_pad(w):
    w4 = w.reshape(HEADS, QK_DIM, w.shape[-1])
    return jnp.concatenate([w4, jnp.zeros((HEADS, QK_PAD - QK_DIM, w.shape[-1]), w.dtype)], axis=1).reshape(
        HEADS * QK_PAD, w.shape[-1])


def _qk_from_pad(g):
    return g.reshape(HEADS, QK_PAD, g.shape[-1])[:, :QK_DIM].reshape(HEADS * QK_DIM, g.shape[-1])


def _ff_to_pad(a, axis):
    shape = list(a.shape)
    shape[axis:axis + 1] = [N_CHIPS, FF_SHARD]
    a4 = a.reshape(shape)
    shape[axis + 1] = FF_BLOCK - FF_SHARD
    out = jnp.concatenate([a4, jnp.zeros(shape, a.dtype)], axis=axis + 1)
    shape[axis:axis + 2] = [D_FF_P]
    return out.reshape(shape)


def _ff_from_pad(a, axis):
    shape = list(a.shape)
    shape[axis:axis + 1] = [N_CHIPS, FF_BLOCK]
    a4 = lax.slice_in_dim(a.reshape(shape), 0, FF_SHARD, axis=axis + 1)
    shape[axis:axis + 2] = [D_FF]
    return a4.reshape(shape)


class _LocalPlan:
    def __init__(self, wt):
        self.wt, self.grads = wt, {}

    def weight(self, name):
        return self.wt[name]

    def host(self, point):
        return None

    def grad(self, name, value):
        self.grads[name] = value


def _local_step(x, tgt, wt, plan=None):
    plan = _LocalPlan(wt) if plan is None else plan
    s = x.shape[0]
    n_valid = N_META + s
    t = -(-n_valid // HEAD) * HEAD
    zpad = jnp.zeros((t - n_valid, D_MODEL), F32)
    h0 = jnp.concatenate([wt["meta_tokens"], x, zpad], axis=0)
    tgt_p = jnp.concatenate([jnp.zeros((N_META, D_MODEL), F32), tgt, zpad], axis=0)
    cos, sin, perm = _rope_tables(t)
    qn_w = jnp.concatenate([wt["q_norm_w"], jnp.zeros((1, QK_PAD - QK_DIM), F32)], axis=1)
    kn_w = jnp.concatenate([wt["k_norm_w"], jnp.zeros((1, QK_PAD - QK_DIM), F32)], axis=1)
    head_id = jnp.arange(DN_WIDTH)[None, :] // HEAD
    lane = jnp.arange(HEAD)[:, None]
    sel_a = (lane == head_id).astype(F32)
    sel_b = (lane == head_id + HEADS).astype(F32)
    alog = jnp.repeat(wt["dn_A_log"], HEAD, axis=1)
    dtb = jnp.repeat(wt["dn_dt_bias"], HEAD, axis=1)
    conv_w, conv_b = wt["ffn_conv_w"], wt["ffn_conv_b"]

    u = _rms_fwd("attn_norm_fwd", h0, wt["attn_norm_w"], host=plan.host("attn_norm_fwd"))
    win, wq, wkv = plan.weight("w_in_t"), plan.weight("w_q_t"), plan.weight("w_kv_t")
    proj = _matmul("in_proj", u, win, "nt", F32)
    z = (proj, DN_WIDTH, 3)
    q_lat, kv_lat, k_pe, ab = (proj, LORA, 8), (proj, LORA, 9), (proj, HEAD, 20), (proj, HEAD, 21)
    mla_consts = (wt["q_a_norm_w"], wq, wt["kv_a_norm_w"], wkv, qn_w, kn_w, perm)
    q, k, v = _mla_prep_fwd(q_lat, kv_lat, k_pe, cos, sin, *mla_consts)
    o_mla = _attn_fwd(q, k, v, host=plan.host("attn_fwd"))
    conv = _dn_conv_fwd(proj, wt["dn_conv_w"])
    dn_consts = (sel_a, sel_b, alog, dtb)
    qn, kn, g, beta = _dn_prep_fwd(conv, ab, *dn_consts)
    n_mat, b_mat, q_eff, o_own, eg = _dn_chunk_fwd(qn, kn, conv, g, beta, host=plan.host("dn_chunk_fwd"))
    sall = _dn_rec_fwd(n_mat, b_mat, eg)
    o_dn = _dn_o_fwd(sall, q_eff, o_own)
    w_out = plan.weight("w_out")
    mixed, h1, n2 = _mix_out_proj(o_mla, o_dn, z, wt["mla_out_norm_w"], wt["dn_out_norm_w"], w_out, h0, wt["ffn_norm_w"])
    w_gate, w_up = plan.weight("w_gate_t"), plan.weight("w_up_t")
    gpre, up, act = _ffn_glu_fwd(n2, w_gate, w_up, conv_w, conv_b, host=plan.host("ffn_glu_fwd"))
    w_down = plan.weight("w_down")
    dy, dy16, sq = _down_proj_loss(act, w_down, h1, tgt_p, n_valid)

    grads = {}
    dact = _matmul("down_dx", dy16, w_down, "nt", BF16)
    plan.grad("w_down", _matmul("down_dw", act, dy16, "tn", BF16))
    dgpre, dup, grads["ffn_conv_w"], grads["ffn_conv_b"] = _glu_bwd(gpre, up, conv_w, conv_b, dact)
    plan.grad("w_gate_t", _matmul("gate_dw", dgpre, n2, "tn", BF16))
    plan.grad("w_up_t", _matmul("up_dw", dup, n2, "tn", BF16))
    dh1, dh1_16, grads["ffn_norm_w"] = _ffn_in_bwd(dgpre, dup, w_gate, w_up, h1, dy, wt["ffn_norm_w"],
                                                   host=plan.host("ffn_in_bwd"))
    plan.grad("w_out", _matmul("out_dw", mixed, dh1_16, "tn", BF16))
    do_mla, do_dn, dz, grads["mla_out_norm_w"], grads["dn_out_norm_w"] = _mix_out_bwd(
        o_mla, o_dn, z, dh1_16, w_out, wt["mla_out_norm_w"], wt["dn_out_norm_w"], host=plan.host("mix_out_bwd"))
    dq_eff, ds_out = _dn_o_bwd(sall, q_eff, do_dn)
    gall = _dn_rec_bwd(n_mat, eg, ds_out)
    dqn, dkn, dv_dn, dg, dbeta = _dn_chunk_bwd(qn, kn, conv, g, beta, sall, gall, dq_eff, do_dn,
                                               host=plan.host("dn_chunk_bwd"))
    dconv, dab, dalog, ddtb = _dn_prep_bwd(conv, ab, dqn, dkn, dv_dn, dg, dbeta, *dn_consts)
    grads["dn_A_log"] = jnp.sum(dalog.reshape(HEADS, HEAD), axis=1)[None, :]
    grads["dn_dt_bias"] = jnp.sum(ddtb.reshape(HEADS, HEAD), axis=1)[None, :]
    ddn_pre, grads["dn_conv_w"] = _dn_conv_bwd(proj, wt["dn_conv_w"], dconv)
    dq, dk, dv = _attn_bwd(q, k, v, do_mla, host=plan.host("attn_bwd"))
    dq_lat, dkv_lat, dk_pe, dqa, dwq, dkva, dwkv, dqnw, dknw = _mla_prep_bwd(
        q_lat, kv_lat, k_pe, cos, sin, dq, dk, dv, *mla_consts, host=plan.host("mla_prep_bwd"))
    grads["q_a_norm_w"], grads["kv_a_norm_w"] = dqa, dkva
    plan.grad("w_q_t", dwq)
    plan.grad("w_kv_t", dwkv)
    grads["q_norm_w"], grads["k_norm_w"] = dqnw[:, :QK_DIM], dknw[:, :QK_DIM]
    dproj = jnp.concatenate([ddn_pre, dz, dq_lat, dkv_lat, dk_pe, dab], axis=1)
    plan.grad("w_in_t", _matmul("in_dw", dproj, u, "tn", F32))
    du = _matmul("in_dx", dproj, win, "nn", BF16, host=plan.host("in_dx"))
    dh0, _, grads["attn_norm_w"] = _rms_bwd("attn_norm_bwd", h0, wt["attn_norm_w"], [du], dh1,
                                            host=plan.host("attn_norm_bwd"))
    grads["meta_tokens"] = dh0[0:N_META]
    if isinstance(plan, _LocalPlan):
        grads.update(plan.grads)
    return sq, dh0[N_META:n_valid], grads


def _mesh_pos():
    return lax.axis_index("x"), lax.axis_index("y"), lax.axis_index("c")


def _other_chips(x, y):
    return [(1 - x, y), (x, 1 - y), (1 - x, 1 - y)]


def _remote(src, dst, send_sems, recv_sems, k, to):
    return pltpu.make_async_remote_copy(src_ref=src, dst_ref=dst, send_sem=send_sems.at[k], recv_sem=recv_sems.at[k],
                                        device_id=to, device_id_type=MESH)


SIBLING_ID, CHIPS_ID, GATHER_ID, ALL_ID = 1, 2, 3, 4


def _sibling_peer():
    x, y, c = _mesh_pos()
    return [(x, y, 1 - c)]


def _chip_peers():
    x, y, c = _mesh_pos()
    return [(qx, qy, c) for qx, qy in _other_chips(x, y)]


def _copies_exchange(make, ins, out_shape, nsem, peers=None, cid=None):
    def prog(in_refs, out_refs, send_sems, recv_sems):
        copies = make(in_refs, out_refs, send_sems, recv_sems)

        def start():
            for cp in copies:
                cp.start()

        def finish():
            for cp in copies:
                cp.wait()

        return start, finish

    return _Exchange(prog, ins, out_shape, nsem, peers, cid)


def _all_gather(shards):
    def prog(srcs, dsts, send_sems, recv_sems):
        x, y, c = _mesh_pos()
        p = 2 * x + y
        sibling = (x, y, 1 - c)
        chips = _other_chips(x, y)
        bufs = tuple((s, d, s.shape[0] // 2) for s, d in zip(srcs, dsts))

        def half(ref, rows, which):
            return ref.at[pl.ds(which * rows, rows), :]

        def copy(i, k, src, dst, to):
            return _remote(src, dst, send_sems, recv_sems, 6 * i + k, to)

        sends = [copy(i, j, half(src, rows, c), half(dst.at[p], rows, c), (*chip, c))
                 for i, (src, dst, rows) in enumerate(bufs) for j, chip in enumerate(chips)]

        def start():
            for cp in sends:
                cp.start()

        def finish():
            passed = []
            for i, (src, dst, rows) in enumerate(bufs):
                for j, (qx, qy) in enumerate(chips):
                    block = half(dst.at[2 * qx + qy], rows, c)
                    copy(i, j, block, block, (x, y, c)).wait_recv()
                    fwd = copy(i, 3 + j, block, block, sibling)
                    fwd.start()
                    passed.append(fwd)
            for i, (src, dst, rows) in enumerate(bufs):
                for j, (qx, qy) in enumerate(chips):
                    block = half(dst.at[2 * qx + qy], rows, 1 - c)
                    copy(i, 3 + j, block, block, (x, y, c)).wait_recv()
            for cp in sends + passed:
                cp.wait_send()

        return start, finish

    return _Exchange(prog, shards, [jax.ShapeDtypeStruct((N_CHIPS, *s.shape), s.dtype) for s in shards], 6 * len(shards),
                     lambda: _sibling_peer() + _chip_peers(), GATHER_ID)


def _gathered(ex):
    p = 2 * lax.axis_index("x") + lax.axis_index("y")
    return [lax.dynamic_update_slice(g, s[None], (p, 0, 0)) for g, s in zip(ex.outs, ex.ins)]


def _rs_to_sibling(bufs):
    def make(srcs, dsts, send_sems, recv_sems):
        x, y, c = _mesh_pos()
        copies = []
        for i, (src, dst) in enumerate(zip(srcs, dsts)):
            half = src.shape[1] // 2
            copies.append(_remote(src.at[:, pl.ds((1 - c) * half, half), :], dst, send_sems, recv_sems, i, (x, y, 1 - c)))
        return copies

    return _copies_exchange(make, bufs,
                            [jax.ShapeDtypeStruct((N_CHIPS, b.shape[1] // 2, b.shape[2]), b.dtype) for b in bufs],
                            len(bufs), _sibling_peer, SIBLING_ID)


def _rs_pair_add(name, bufs, gots, c, out_dtype):
    n = len(bufs)

    def body(c_ref, *refs):
        for a_ref, b_ref, o_ref in zip(refs[:n], refs[n:2 * n], refs[2 * n:]):
            o_ref[...] = (a_ref[...].astype(F32) + b_ref[...].astype(F32)).astype(out_dtype)

    mine = [pl.BlockSpec((None, g.shape[1], g.shape[2]), lambda j, cr: (j, cr[0], 0)) for g in gots]
    whole = [pl.BlockSpec((None, g.shape[1], g.shape[2]), lambda j, cr: (j, 0, 0)) for g in gots]
    return pl.pallas_call(
        body, name=name,
        grid_spec=pltpu.PrefetchScalarGridSpec(num_scalar_prefetch=1, grid=(N_CHIPS,), in_specs=mine + whole, out_specs=whole),
        out_shape=[jax.ShapeDtypeStruct(g.shape, out_dtype) for g in gots],
        compiler_params=_cparams(("arbitrary",)))(c, *bufs, *gots)


def _rs_to_chips(accs):
    def make(srcs, dsts, send_sems, recv_sems):
        x, y, c = _mesh_pos()
        return [_remote(src.at[2 * qx + qy], dst.at[k], send_sems, recv_sems, 3 * i + k, (qx, qy, c))
                for i, (src, dst) in enumerate(zip(srcs, dsts)) for k, (qx, qy) in enumerate(_other_chips(x, y))]

    return _copies_exchange(make, accs, [jax.ShapeDtypeStruct((3, a.shape[1], a.shape[2]), a.dtype) for a in accs],
                            3 * len(accs), _chip_peers, CHIPS_ID)


def _rs_chip_add(name, accs, gots, p):
    n = len(accs)
    slot = (0, 1, 0, 2)

    def body(p_ref, *refs):
        me = p_ref[0]
        for own_ref, got_ref, o_ref in zip(refs[:n], refs[n:2 * n], refs[2 * n:]):
            total = None
            for chip in range(N_CHIPS):
                val = own_ref[...].astype(F32)
                for e in (1, 2, 3):
                    val = jnp.where((chip ^ me) == e, got_ref[slot[e]].astype(F32), val)
                total = val if total is None else total + val
            o_ref[...] = total

    own = [pl.BlockSpec((None, a.shape[1], a.shape[2]), lambda i, pr: (pr[0], 0, 0)) for a in accs]
    got = [pl.BlockSpec(g.shape, lambda i, pr: (0, 0, 0)) for g in gots]
    out = [pl.BlockSpec((a.shape[1], a.shape[2]), lambda i, pr: (0, 0)) for a in accs]
    return pl.pallas_call(
        body, name=name,
        grid_spec=pltpu.PrefetchScalarGridSpec(num_scalar_prefetch=1, grid=(1,), in_specs=own + got, out_specs=out),
        out_shape=[jax.ShapeDtypeStruct((a.shape[1], a.shape[2]), F32) for a in accs],
        compiler_params=_cparams(("arbitrary",)))(p, *accs, *gots)


def _rs_share(ress):
    def make(srcs, dsts, send_sems, recv_sems):
        x, y, c = _mesh_pos()
        return [_remote(src, dst, send_sems, recv_sems, i, (x, y, 1 - c)) for i, (src, dst) in enumerate(zip(srcs, dsts))]

    return _copies_exchange(make, ress, [jax.ShapeDtypeStruct(r.shape, F32) for r in ress], len(ress), _sibling_peer,
                            SIBLING_ID)


def _shared(ex):
    south = lax.axis_index("c") == 0
    return [jnp.concatenate([jnp.where(south, r, g), jnp.where(south, g, r)], axis=0) for r, g in zip(ex.ins, ex.outs)]


def _all_to_all_devices(vec):
    def others():
        x, y, c = _mesh_pos()
        return [((1 - x if r & 4 else x), (1 - y if r & 2 else y), (1 - c if r & 1 else c)) for r in range(1, 8)]

    def make(srcs, dsts, send_sems, recv_sems):
        x, y, c = _mesh_pos()
        me = 4 * x + 2 * y + c
        return [_remote(srcs[0], dsts[0].at[me], send_sems, recv_sems, r, peer) for r, peer in enumerate(others())]

    return _copies_exchange(make, [vec], [jax.ShapeDtypeStruct((8, *vec.shape), vec.dtype)], 7, others, ALL_ID)


def _sum_devices(stack):
    def body(s_ref, o_ref):
        total = s_ref[0]
        for d in range(1, 8):
            total = total + s_ref[d]
        o_ref[...] = total

    return pl.pallas_call(body, name="sum_devices", out_shape=jax.ShapeDtypeStruct(stack.shape[1:], F32),
                          compiler_params=pltpu.CompilerParams(vmem_limit_bytes=VMEM_LIMIT))(stack)


def _pad_rows(flat, rows):
    return jnp.concatenate([flat, jnp.zeros((rows * LANES - flat.shape[0],), flat.dtype)]).reshape(rows, LANES)


def _unshard(g4, shape, axis):
    a = g4.reshape(N_CHIPS, *shape)
    if axis == 0:
        return a.reshape(N_CHIPS * shape[0], shape[1])
    return jnp.transpose(a, (1, 0, 2)).reshape(shape[0], N_CHIPS * shape[1])


def _shard4(full, shape, axis):
    if axis == 0:
        return full.reshape(N_CHIPS, shape[0] * shape[1])
    a = full.reshape(shape[0], N_CHIPS, shape[1])
    return jnp.transpose(a, (1, 0, 2)).reshape(N_CHIPS, shape[0] * shape[1])


def _pad_axis0(a, rows):
    return jnp.concatenate([a, jnp.zeros((rows - a.shape[0], *a.shape[1:]), a.dtype)], axis=0)


def _pad_axis1(a, rows):
    return jnp.concatenate([a, jnp.zeros((a.shape[0], rows - a.shape[1], *a.shape[2:]), a.dtype)], axis=1)


def _shard_to_strip(name, w):
    _, (shape, axis, rows) = name, {n: (s, ax, r) for n, s, ax, r in BIG}[name]
    w2 = w.reshape(shape).astype(BF16)
    if name == "w_in":
        return w2
    return _pad_axis0(w2.T if axis == 1 else w2, rows)


LOCAL_NAME = dict(w_in="w_in_t", w_q_b="w_q_t", w_kv_b="w_kv_t", w_out="w_out", w_gate="w_gate_t", w_up="w_up_t",
                  w_down="w_down")


WIN_SEGMENTS = ((576, 2112, 0), (2112, 2624, 1536), (0, 256, 2048), (256, 512, 2304), (512, 576, 2560), (2624, 2632, 2688))


def _strips_to_weight(name, g4):
    if name == "w_in":
        return _win_to_pad(jnp.transpose(g4, (0, 2, 1)).reshape(IN_COLS, D_MODEL))
    if name == "w_q_b":
        return _qk_to_pad(g4.reshape(HEADS * QK_DIM, LORA))
    return g4.reshape(N_CHIPS * g4.shape[1], g4.shape[2])


def _grad_to_strips(name, g):
    if name == "w_in":
        strips = []
        for q in range(N_CHIPS):
            pieces = []
            for a, b, local in sorted(WIN_SEGMENTS):
                s, e = max(a, q * IN_SHARD), min(b, (q + 1) * IN_SHARD)
                if s < e:
                    pieces.append(g[local + s - a:local + e - a])
            pieces.append(jnp.zeros((IN_SHARD_P - IN_SHARD, D_MODEL), g.dtype))
            strips.append(jnp.concatenate(pieces, axis=0))
        return jnp.stack(strips)
    if name == "w_q_b":
        return _qk_from_pad(g).reshape(N_CHIPS, QK_DIM, LORA)
    return g.reshape(N_CHIPS, g.shape[0] // N_CHIPS, g.shape[1])


class _MeshPlan:
    LATE = dict(attn_norm_fwd=("w_in", "w_q_b", "w_kv_b"), attn_fwd=("w_up",), dn_chunk_fwd=("w_out", "w_gate"),
                ffn_glu_fwd=("w_down",))
    GROUP_A = ("w_down", "w_gate", "w_up", "w_out")
    GROUP_B = ("w_in", "w_q_b", "w_kv_b")

    def __init__(self, w):
        x, y, c = _mesh_pos()
        self.ci = jnp.reshape(c, (1,)).astype(jnp.int32)
        self.pi = jnp.reshape(2 * x + y, (1,)).astype(jnp.int32)
        self.strip = {n: _shard_to_strip(n, w[n]) for n, _, _, _ in BIG}
        self.gathers, self.weights, self.g, self.acc, self.reduced = {}, {}, {}, {}, {}
        self.sibs, self.sib, self.chip, self.share, self.halves = [], None, None, None, [None, None]

    def gather_small(self, small):
        ex = _all_gather([small])
        ex.run("all_gather_small")
        return _gathered(ex)[0]

    def weight(self, local_name):
        if local_name not in self.weights:
            for point, (names, ex) in list(self.gathers.items()):
                if ex.outs is not None:
                    for n, g4 in zip(names, _gathered(ex)):
                        if "/" in n:
                            n, half = n.split("/")
                            self.halves[int(half)] = g4
                            if None in self.halves:
                                continue
                            g4 = jnp.concatenate(self.halves, axis=1)
                        self.weights[LOCAL_NAME[n]] = _strips_to_weight(n, g4)
                    del self.gathers[point]
        return self.weights[local_name]

    def _shard(self, name):
        if "/" not in name:
            return self.strip[name]
        name, half = name.split("/")
        rows = self.strip[name].shape[0] // 2
        return self.strip[name][int(half) * rows:(int(half) + 1) * rows]

    def grad(self, local_name, value):
        name = {v: k for k, v in LOCAL_NAME.items()}[local_name]
        self.g[name] = _grad_to_strips(name, value)

    def _pair_add(self, names, gots):
        accs = _rs_pair_add("rs_pair_add_" + names[0], [self.g[n] for n in names], gots, self.ci, BF16)
        self.acc.update(zip(names, accs))

    def _chip_add(self, names, chip):
        return _rs_chip_add("rs_chip_add_" + names[0], [self.acc[n] for n in names], chip.outs, self.pi)

    def _take_shared(self, names, share):
        for n, strip in zip(names, _shared(share)):
            self.reduced[n] = strip

    def host(self, point):
        a, b = self.GROUP_A, self.GROUP_B
        if point in self.LATE:
            names = self.LATE[point]
            ex = _all_gather([self._shard(n) for n in names])
            self.gathers[point] = (names, ex)
            return ex
        if point in ("ffn_in_bwd", "mix_out_bwd"):
            names = dict(ffn_in_bwd=a[:3], mix_out_bwd=a[3:])[point]
            ex = _rs_to_sibling([self.g[n] for n in names])
            self.sibs.append(ex)
            return ex
        if point == "dn_chunk_bwd":
            self._pair_add(a, [o for ex in self.sibs for o in ex.outs])
            self.chip1 = _rs_to_chips([self.acc[n] for n in a[:2]])
            return self.chip1
        if point == "attn_bwd":
            self.chip2 = _rs_to_chips([self.acc[n] for n in a[2:]])
            return self.chip2
        if point == "mla_prep_bwd":
            ress = self._chip_add(a[:2], self.chip1) + self._chip_add(a[2:], self.chip2)
            self.share = _rs_share(ress)
            return self.share
        if point == "in_dx":
            self._take_shared(a, self.share)
            self.sib = _rs_to_sibling([self.g[n] for n in b])
            return self.sib
        if point == "attn_norm_bwd":
            self._pair_add(b, self.sib.outs)
            self.chip = _rs_to_chips([self.acc[n] for n in b])
            return self.chip
        return None

    def last_share(self):
        self.share = _rs_share(self._chip_add(self.GROUP_B, self.chip))
        return self.share

    def finish(self):
        self._take_shared(self.GROUP_B, self.share)
        return self.reduced


def _strip_to_shard(name, strip):
    shape, axis = {n: (s, ax) for n, s, ax, _ in BIG}[name]
    rows = shape[axis]
    return strip[:rows].T if axis == 1 else strip[:rows]


def kernel(x, meta_tokens, attn_norm_w, w_in, q_a_norm_w, w_q_b, kv_a_norm_w, w_kv_b, q_norm_w, k_norm_w, mla_out_norm_w, dn_conv_w, dn_A_log, dn_dt_bias, dn_out_norm_w, w_out, ffn_norm_w, w_gate, w_up, ffn_conv_w, ffn_conv_b, w_down, loss_target, m_meta_tokens, m_attn_norm_w, m_w_in, m_q_a_norm_w, m_w_q_b, m_kv_a_norm_w, m_w_kv_b, m_q_norm_w, m_k_norm_w, m_mla_out_norm_w, m_dn_conv_w, m_dn_A_log, m_dn_dt_bias, m_dn_out_norm_w, m_w_out, m_ffn_norm_w, m_w_gate, m_w_up, m_ffn_conv_w, m_ffn_conv_b, m_w_down, v_meta_tokens, v_attn_norm_w, v_w_in, v_q_a_norm_w, v_w_q_b, v_kv_a_norm_w, v_w_kv_b, v_q_norm_w, v_k_norm_w, v_mla_out_norm_w, v_dn_conv_w, v_dn_A_log, v_dn_dt_bias, v_dn_out_norm_w, v_w_out, v_ffn_norm_w, v_w_gate, v_w_up, v_ffn_conv_w, v_ffn_conv_b, v_w_down):
    local = dict(locals())
    w = {n: local[n] for n in WEIGHTS}
    m = {n: local["m_" + n] for n in WEIGHTS}
    v = {n: local["v_" + n] for n in WEIGHTS}
    p = 2 * lax.axis_index("x") + lax.axis_index("y")

    plan = _MeshPlan(w)
    wf = _pad_rows(jnp.concatenate([w[n].reshape(-1) for n, _, _ in SMALL_SHARDED]), SMALL_ROWS)
    gf = plan.gather_small(wf).reshape(N_CHIPS, -1)
    full = {}
    off = 0
    for n, s, ax in SMALL_SHARDED:
        full[n] = _unshard(gf[:, off:off + s[0] * s[1]], s, ax)
        off += s[0] * s[1]
    for n, _ in REPLICATED:
        full[n] = w[n]
    full["ffn_conv_w"] = _ff_to_pad(full["ffn_conv_w"], 1)
    full["ffn_conv_b"] = _ff_to_pad(full["ffn_conv_b"], 1)

    sq, grad_x, g = _local_step(x[0], loss_target[0], full, plan)
    g["ffn_conv_w"] = _ff_from_pad(g["ffn_conv_w"], 1)
    g["ffn_conv_b"] = _ff_from_pad(g["ffn_conv_b"], 1)

    small_all = [n for n, _, _ in SMALL_SHARDED] + [n for n, _ in REPLICATED]
    vec = jnp.concatenate([g[n].reshape(-1) for n in small_all] + [jnp.reshape(0.5 / D_MODEL * jnp.sum(sq), (1,))])
    vec = _pad_rows(vec, -(-vec.shape[0] // (8 * LANES)) * 8)
    a2a = _all_to_all_devices(vec)

    gs, delta, new_m, new_v = {}, {}, {}, {}
    big = {n: (s, ax) for n, s, ax, _ in BIG}

    def adamw_big(n, strips, host=None):
        s, ax = big[n]
        flip = ax == 1 and s[1] % 8 == 0
        there = (lambda a: a.reshape(s).T) if flip else (lambda a: a.reshape(s))
        back = (lambda a: a.T.reshape(w[n].shape)) if flip else (lambda a: a.reshape(w[n].shape))
        strip = strips[n] if flip or ax == 0 else strips[n][:s[1]].T
        g2, d2, m2, v2 = _adamw_call("adamw_" + n, there(w[n]), strip, there(m[n]), there(v[n]), host=host)
        gs[n], delta[n], new_m[n], new_v[n] = back(g2), back(d2), back(m2), back(v2)

    adamw_big("w_down", plan.reduced, host=a2a)
    adamw_big("w_gate", plan.reduced, host=plan.last_share())
    adamw_big("w_up", plan.reduced)
    adamw_big("w_out", plan.reduced)
    strips = plan.finish()
    for n in plan.GROUP_B:
        adamw_big(n, strips)
    me = 4 * lax.axis_index("x") + 2 * lax.axis_index("y") + lax.axis_index("c")
    red = _sum_devices(lax.dynamic_update_slice(a2a.outs[0], vec[None], (me, 0, 0))).reshape(-1)
    off = 0
    for n in small_all:
        tot = red[off:off + g[n].size].reshape(g[n].shape)
        off += g[n].size
        shard = {sn: (s, ax) for sn, s, ax in SMALL_SHARDED}.get(n)
        if shard is not None:
            tot = lax.dynamic_slice_in_dim(tot, p * shard[0][1], shard[0][1], axis=1)
        gs[n] = tot
    loss = red[off]
    two_d = lambda a: a.reshape(a.shape[-2], a.shape[-1])
    outs = _adamw_small([two_d(w[n]) for n in small_all], [two_d(gs[n]) for n in small_all],
                        [two_d(m[n]) for n in small_all], [two_d(v[n]) for n in small_all])
    for i, n in enumerate(small_all):
        for dst, src in ((delta, outs[0]), (new_m, outs[1]), (new_v, outs[2])):
            dst[n] = src[i].reshape(w[n].shape)

    grad_out = [gs[n].reshape(w[n].shape) for n in WEIGHTS]
    return (loss, grad_x[None], *grad_out, *[delta[n] for n in WEIGHTS], *[new_m[n] for n in WEIGHTS],
            *[new_v[n] for n in WEIGHTS])
```

```python
import functools
import math

import jax
import jax.numpy as jnp
import numpy as np
from jax import lax
from jax.experimental import pallas as pl
from jax.experimental.pallas import tpu as pltpu

F32 = jnp.float32
BF16 = jnp.bfloat16
HI = lax.Precision.HIGHEST
MESH = pl.DeviceIdType.MESH

N_META = 16
D_MODEL = 1024
HEADS = 4
HEAD = 128
ROPE = 64
QK_DIM = HEAD + ROPE
QK_PAD = 2 * HEAD
LORA = 256
DN_WIDTH = HEADS * HEAD
CHUNK = 64
D_FF = 2816
N_CHIPS = 4
FF_SHARD = D_FF // N_CHIPS
FF_BLOCK = 768
D_FF_P = N_CHIPS * FF_BLOCK
IN_COLS = 2632
IN_SHARD = IN_COLS // N_CHIPS
IN_SHARD_P = 672
IN_PAD = 2816
NORM_EPS = 1e-6
ROPE_THETA = 10000.0
LANES = 512

ADAM_LR, ADAM_B1, ADAM_B2, ADAM_EPS, ADAM_WD, ADAM_STEP = 0.001, 0.9, 0.999, 1e-08, 0.01, 10

VMEM_LIMIT = 56 * 1024 * 1024

BIG = (("w_in", (1024, 658), 1, IN_SHARD_P), ("w_q_b", (256, 192), 1, 192), ("w_kv_b", (256, 256), 1, 256),
       ("w_out", (256, 1024), 0, 256), ("w_gate", (1024, 704), 1, FF_BLOCK), ("w_up", (1024, 704), 1, FF_BLOCK),
       ("w_down", (704, 1024), 0, FF_BLOCK))
SMALL_SHARDED = (("meta_tokens", (16, 256), 1), ("dn_conv_w", (4, 384), 1), ("ffn_conv_w", (3, 704), 1))
REPLICATED = (("attn_norm_w", 1024), ("q_a_norm_w", 256), ("kv_a_norm_w", 256), ("q_norm_w", 192), ("k_norm_w", 192),
              ("mla_out_norm_w", 128), ("dn_A_log", 4), ("dn_dt_bias", 4), ("dn_out_norm_w", 128), ("ffn_norm_w", 1024),
              ("ffn_conv_b", 2816))
WEIGHTS = ("meta_tokens", "attn_norm_w", "w_in", "q_a_norm_w", "w_q_b", "kv_a_norm_w", "w_kv_b", "q_norm_w", "k_norm_w",
           "mla_out_norm_w", "dn_conv_w", "dn_A_log", "dn_dt_bias", "dn_out_norm_w", "w_out", "ffn_norm_w", "w_gate",
           "w_up", "ffn_conv_w", "ffn_conv_b", "w_down")

SMALL_ROWS = 16
REP_ROWS = 16


def _cparams(sem):
    return pltpu.CompilerParams(dimension_semantics=sem, vmem_limit_bytes=VMEM_LIMIT)


class _Exchange:
    def __init__(self, prog, ins, out_shape, nsem, peers=None, cid=None):
        self.prog, self.ins, self.out_shape, self.nsem = prog, list(ins), list(out_shape), nsem
        self.peers, self.cid = peers, cid
        self.outs = None

    def sems(self):
        return [pltpu.SemaphoreType.DMA((self.nsem,)), pltpu.SemaphoreType.DMA((self.nsem,))]

    def programs(self, in_refs, out_refs, send_sems, recv_sems):
        start, finish = self.prog(in_refs, out_refs, send_sems, recv_sems)
        if self.cid is None:
            return start, finish
        peers = self.peers()

        def shake_and_start():
            barrier = pltpu.get_barrier_semaphore()
            for peer in peers:
                pl.semaphore_signal(barrier, inc=1, device_id=peer, device_id_type=MESH)
            pl.semaphore_wait(barrier, len(peers))
            start()

        return shake_and_start, finish

    def cparams(self, **kw):
        return pltpu.CompilerParams(has_side_effects=True, collective_id=self.cid, **kw)

    def run(self, name):
        any_spec = pl.BlockSpec(memory_space=pl.ANY)
        n = len(self.ins)

        def body(*refs):
            start, finish = self.programs(refs[:n], refs[n:-2], refs[-2], refs[-1])
            start()
            finish()

        self.outs = pl.pallas_call(
            body, name=name, in_specs=[any_spec] * n, out_specs=[any_spec] * len(self.out_shape),
            out_shape=self.out_shape, scratch_shapes=self.sems(), compiler_params=self.cparams())(*self.ins)
        return self.outs


def _pcall(body, name, grid, in_specs, out_specs, out_shape, args, sem, scratch_shapes=(), host=None):
    single = not isinstance(out_shape, (list, tuple))
    out_specs, out_shape = ([out_specs], [out_shape]) if single else (list(out_specs), list(out_shape))
    if host is None:
        outs = pl.pallas_call(body, name=name, grid=grid, in_specs=list(in_specs), out_specs=out_specs, out_shape=out_shape,
                              scratch_shapes=list(scratch_shapes), compiler_params=_cparams(sem))(*args)
        return outs[0] if single else outs
    any_spec = pl.BlockSpec(memory_space=pl.ANY)
    n_in, n_out, n_scr, nx_in, nx_out = len(in_specs), len(out_specs), len(scratch_shapes), len(host.ins), len(host.out_shape)

    def hosted(*refs):
        c_in, x_in = refs[:n_in], refs[n_in:n_in + nx_in]
        o0 = n_in + nx_in
        c_out, x_out = refs[o0:o0 + n_out], refs[o0 + n_out:o0 + n_out + nx_out]
        s0 = o0 + n_out + nx_out
        start, finish = host.programs(x_in, x_out, refs[s0 + n_scr], refs[s0 + n_scr + 1])
        first = functools.reduce(jnp.logical_and, [pl.program_id(d) == 0 for d in range(len(grid))])
        last = functools.reduce(jnp.logical_and, [pl.program_id(d) == grid[d] - 1 for d in range(len(grid))])
        pl.when(first)(start)
        body(*c_in, *c_out, *refs[s0:s0 + n_scr])
        pl.when(last)(finish)

    outs = pl.pallas_call(
        hosted, name=name, grid=grid, in_specs=list(in_specs) + [any_spec] * nx_in,
        out_specs=out_specs + [any_spec] * nx_out, out_shape=out_shape + host.out_shape,
        scratch_shapes=list(scratch_shapes) + host.sems(),
        compiler_params=host.cparams(dimension_semantics=sem, vmem_limit_bytes=VMEM_LIMIT))(*args, *host.ins)
    host.outs = outs[n_out:]
    return outs[0] if single else outs[:n_out]


NN, NT, TN = ((1,), (0,)), ((1,), (1,)), ((0,), (0,))


def _shift_dims(dims, batch):
    if not batch:
        return (dims, ((), ()))
    return (((dims[0][0] + 1,), (dims[1][0] + 1,)), ((0,), (0,)))


def _make_mm(dims, exact, batch=False):
    def raw(a, b, d):
        dn = _shift_dims(d, batch)
        if exact == "split_lhs":
            ah, bh = a.astype(BF16), b.astype(BF16)
            al = (a - ah.astype(F32)).astype(BF16)
            return lax.dot_general(ah, bh, dn, preferred_element_type=F32) + lax.dot_general(al, bh, dn,
                                                                                              preferred_element_type=F32)
        if exact == "split":
            ah, bh = a.astype(BF16), b.astype(BF16)
            al, bl = (a - ah.astype(F32)).astype(BF16), (b - bh.astype(F32)).astype(BF16)
            dot = lambda p, q: lax.dot_general(p, q, dn, preferred_element_type=F32)
            return dot(ah, bh) + (dot(ah, bl) + dot(al, bh))
        if exact:
            return lax.dot_general(a.astype(F32), b.astype(F32), dn, precision=HI, preferred_element_type=F32)
        return lax.dot_general(a.astype(BF16), b.astype(BF16), dn, preferred_element_type=F32)

    @jax.custom_vjp
    def mm(a, b):
        return raw(a, b, dims)

    def fwd(a, b):
        return raw(a, b, dims), (a, b)

    def bwd(res, g):
        a, b = res
        if dims == NN:
            da, db = raw(g, b, NT), raw(a, g, TN)
        elif dims == NT:
            da, db = raw(g, b, NN), raw(g, a, TN)
        else:
            da, db = raw(b, g, NT), raw(a, g, NN)
        return da.astype(a.dtype), db.astype(b.dtype)

    mm.defvjp(fwd, bwd)
    return mm


_mm = _make_mm(NN, False)
_mm_nt = _make_mm(NT, False)
_mm_tn = _make_mm(TN, False)
_mmx = _make_mm(NN, "split_lhs")
_bmm = _make_mm(NN, False, batch=True)
_bmm_nt = _make_mm(NT, False, batch=True)
_bmm_tn = _make_mm(TN, False, batch=True)
_bmmx = _make_mm(NN, True, batch=True)
_bmms = _make_mm(NN, "split", batch=True)
_bmms_nt = _make_mm(NT, "split", batch=True)
_bmms_tn = _make_mm(TN, "split", batch=True)


@jax.custom_vjp
def _unit_lower_inv(a):
    n = a.shape[-1]
    eye = (lax.broadcasted_iota(jnp.int32, a.shape, 1) == lax.broadcasted_iota(jnp.int32, a.shape, 2)).astype(F32)
    x = -a
    t = eye + x
    for _ in range(max(n.bit_length() - 2, 0)):
        x = _bmms(x, x)
        t = t + _bmms(t, x)
    return t


def _unit_lower_inv_fwd(a):
    t = _unit_lower_inv(a)
    return t, t


def _unit_lower_inv_bwd(t, g):
    return (-_bmms_tn(t, _bmms_nt(g, t)),)


_unit_lower_inv.defvjp(_unit_lower_inv_fwd, _unit_lower_inv_bwd)


def _scan_chunk_rows(x, reverse):
    nb, c, w = x.shape
    y = x.reshape(nb * c, w)
    pos = lax.broadcasted_iota(jnp.int32, y.shape, 0) % c
    step = 1
    while step < c:
        if reverse:
            y = y + jnp.where(pos < c - step, pltpu.roll(y, nb * c - step, 0), 0.0)
        else:
            y = y + jnp.where(pos >= step, pltpu.roll(y, step, 0), 0.0)
        step *= 2
    return y.reshape(nb, c, w)


@jax.custom_vjp
def _chunk_cumsum(x):
    return _scan_chunk_rows(x, False)


_chunk_cumsum.defvjp(lambda x: (_scan_chunk_rows(x, False), None), lambda _, g: (_scan_chunk_rows(g, True),))


def _rms(x, w, n):
    ms = jnp.sum(x * x, axis=-1, keepdims=True) * (1.0 / n)
    return x * lax.rsqrt(ms + NORM_EPS) * w


def _silu(x):
    return x * jax.nn.sigmoid(x)


def _softplus(x):
    return jnp.maximum(x, 0.0) + jnp.log(1.0 + jnp.exp(-jnp.abs(x)))


def _rope(x, cos, sin, perm):
    return x * cos + _mmx(x, perm) * sin


def _mla_prep_fn(rows, consts):
    q_lat, kv_lat, k_pe, cos, sin = rows
    qn = _rms(q_lat, consts["qa_w"], LORA)
    kvn = _rms(kv_lat, consts["kva_w"], LORA)
    outs = []
    for h in range(HEADS):
        q_n = _mm_nt(qn, consts["wq_n"][h])
        q_r = _mm_nt(qn, consts["wq_r"][h])
        rs = lax.rsqrt((jnp.sum(q_n * q_n, -1, keepdims=True) + jnp.sum(q_r * q_r, -1, keepdims=True)) * (1.0 / QK_DIM)
                       + NORM_EPS)
        q_n = q_n * rs * consts["qn_n"]
        q_r = _rope(q_r * rs * consts["qn_r"], cos, sin, consts["perm"])
        k_n = _mm_nt(kvn, consts["wk_n"][h])
        v = _mm_nt(kvn, consts["wv"][h])
        rk = lax.rsqrt((jnp.sum(k_n * k_n, -1, keepdims=True) + jnp.sum(k_pe * k_pe, -1, keepdims=True)) * (1.0 / QK_DIM)
                       + NORM_EPS)
        k_n = k_n * rk * consts["kn_n"]
        k_r = _rope(k_pe * rk * consts["kn_r"], cos, sin, consts["perm"])
        outs += [q_n, q_r, k_n, k_r, v]
    return tuple(outs)


def _attn_fn(q, k, v, row0):
    s = _mm_nt(q, k) * (1.0 / math.sqrt(QK_DIM))
    qpos = row0 + lax.broadcasted_iota(jnp.int32, s.shape, 0)
    kpos = lax.broadcasted_iota(jnp.int32, s.shape, 1)
    s = jnp.where(kpos <= qpos, s, -1e30)
    m = lax.stop_gradient(jnp.max(s, axis=-1, keepdims=True))
    p = jnp.exp(s - m)
    p = p / jnp.sum(p, axis=-1, keepdims=True)
    return _mm(p, v)


def _dn_prep_fn(rows, consts):
    qc, kc, ab = rows
    a_b = _mmx(ab, consts["sel_a"])
    b_b = _mmx(ab, consts["sel_b"])
    beta = jax.nn.sigmoid(b_b)
    g = -jnp.exp(consts["alog"]) * _softplus(a_b + consts["dtb"])
    qs, ks = [], []
    for h in range(HEADS):
        q, k = qc[h], kc[h]
        qs.append(q * lax.rsqrt(jnp.sum(q * q, -1, keepdims=True) + NORM_EPS))
        ks.append(k * lax.rsqrt(jnp.sum(k * k, -1, keepdims=True) + NORM_EPS))
    return tuple(qs), tuple(ks), g, beta


def _dn_chunk_fn(q, k, v, gb, g64, bb):
    nb = q.shape[0]
    ri = lax.broadcasted_iota(jnp.int32, (nb, CHUNK, CHUNK), 1)
    ci = lax.broadcasted_iota(jnp.int32, (nb, CHUNK, CHUNK), 2)
    tri = ri >= ci
    strict = ri > ci
    tril = tri.astype(F32)
    eye = (ri == ci).astype(F32)
    ones = jnp.ones((nb, CHUNK, CHUNK), F32)
    gc = _chunk_cumsum(gb)
    gc64 = _chunk_cumsum(g64)
    grow = _bmmx(ones, eye * gc64)
    diff = gc64 - grow
    decay = jnp.where(tri, jnp.exp(jnp.where(tri, diff, 0.0)), 0.0)
    kb = k * bb
    vb = v * bb
    a = jnp.where(strict, _bmm_nt(kb, k) * decay, 0.0)
    tinv = _unit_lower_inv(a)
    u = _bmm(tinv, vb)
    w = _bmm(tinv, kb * jnp.exp(gc))
    qs = q * (1.0 / math.sqrt(HEAD))
    qk = _bmm_nt(qs, k) * decay
    qg = qs * jnp.exp(gc)
    glast = jnp.sum(gb, axis=1, keepdims=True)
    kdec = k * jnp.exp(glast - gc)
    n_mat = _bmm_tn(kdec, w)
    b_mat = _bmm_tn(kdec, u)
    q_eff = qg - _bmm(qk, w)
    o_own = _bmm(qk, u)
    return n_mat, b_mat, q_eff, o_own, jnp.exp(glast)


def _dn_rec_fn(s, n_mat, b_mat, eg):
    return s * eg - _mm(n_mat, s) + b_mat


def _dn_o_fn(s, q_eff, o_own):
    return _bmm(q_eff, s) + o_own


def _dn_out_fn(o, z, w):
    return _rms(o, w, HEAD) * _silu(z)


def _row_tile(t, parts=8):
    return t // parts if (t // parts) % 16 == 0 else t


def _tile(n, pref, unit):
    best = n
    for cand in range(unit, min(n, pref) + 1, unit):
        if n % cand == 0:
            best = cand
    return best if best <= pref else n


def _rows_call(name, body, rows, consts, outs, accs, r, host=None):
    rows = [a if isinstance(a, tuple) else (a, a.shape[1], 0) for a in rows]
    t = rows[0][0].shape[0]
    zero = lambda nd: (lambda i: (0,) * nd)
    in_specs = [pl.BlockSpec((r, w), functools.partial(lambda i, b: (i, b), b=blk)) for _, w, blk in rows]
    rows = [a for a, _, _ in rows]
    in_specs += [pl.BlockSpec(a.shape, zero(a.ndim)) for a in consts]
    out_shape = [jax.ShapeDtypeStruct((t, w), dt) for w, dt in outs] + [jax.ShapeDtypeStruct(s, F32) for s in accs]
    out_specs = [pl.BlockSpec((r, w), lambda i: (i, 0)) for w, _ in outs] + [pl.BlockSpec(s, zero(len(s))) for s in accs]
    return _pcall(body, name, (t // r,), in_specs, out_specs, out_shape, [*rows, *consts], ("arbitrary",), host=host)


def _accumulate(ref, val):
    @pl.when(pl.program_id(0) == 0)
    def _():
        ref[...] = jnp.zeros(ref.shape, ref.dtype)

    ref[...] += val


def _matmul(name, a, b, dims, out_dtype, res=None, host=None):
    if dims == "nn":
        (m, k), n = a.shape, b.shape[1]
    elif dims == "nt":
        (m, k), n = a.shape, b.shape[0]
    else:
        (k, m), n = a.shape, b.shape[1]
    tm = _tile(m, 1100, 16) if dims != "tn" else _tile(m, 640, 128)
    tn = _tile(n, 1408, 128)
    if dims == "nn":
        a_spec, b_spec, dn = pl.BlockSpec((tm, k), lambda i, j: (i, 0)), pl.BlockSpec((k, tn), lambda i, j: (0, j)), NN
    elif dims == "nt":
        a_spec, b_spec, dn = pl.BlockSpec((tm, k), lambda i, j: (i, 0)), pl.BlockSpec((tn, k), lambda i, j: (j, 0)), NT
    else:
        a_spec, b_spec, dn = pl.BlockSpec((k, tm), lambda i, j: (0, i)), pl.BlockSpec((k, tn), lambda i, j: (0, j)), TN
    o_spec = pl.BlockSpec((tm, tn), lambda i, j: (i, j))

    def body(*refs):
        a_ref, b_ref, o_ref = refs[0], refs[1], refs[-1]
        acc = lax.dot_general(a_ref[...].astype(BF16), b_ref[...].astype(BF16), (dn, ((), ())),
                              preferred_element_type=F32)
        if res is not None:
            acc = acc + refs[2][...]
        o_ref[...] = acc.astype(out_dtype)

    ins = [a, b] + ([res] if res is not None else [])
    specs = [a_spec, b_spec] + ([o_spec] if res is not None else [])
    return _pcall(body, name, (m // tm, n // tn), specs, o_spec, jax.ShapeDtypeStruct((m, n), out_dtype), ins,
                  ("arbitrary", "arbitrary"), host=host)


def _rms_fwd(name, h, w, host=None):
    n = h.shape[1]

    def body(h_ref, w_ref, o_ref):
        o_ref[...] = _rms(h_ref[...], w_ref[...], n).astype(BF16)

    return _rows_call(name, body, [h], [w], [(n, BF16)], [], _row_tile(h.shape[0]), host=host)[0]


def _rms_bwd(name, h, w, cts, resid, host=None):
    n = h.shape[1]
    nct = len(cts)

    def body(*refs):
        h_ref, ct_refs, r_ref, w_ref = refs[0], refs[1:1 + nct], refs[1 + nct], refs[2 + nct]
        dh_ref, dh16_ref, dw_ref = refs[-3], refs[-2], refs[-1]
        ct = ct_refs[0][...].astype(F32)
        for c in ct_refs[1:]:
            ct = ct + c[...].astype(F32)
        _, vjp = jax.vjp(lambda x, ww: _rms(x, ww, n), h_ref[...], w_ref[...])
        dh, dw = vjp(ct)
        dh = dh + r_ref[...]
        dh_ref[...] = dh
        dh16_ref[...] = dh.astype(BF16)
        _accumulate(dw_ref, dw)

    return _rows_call(name, body, [h, *cts, resid], [w], [(n, F32), (n, BF16)], [(1, n)], _row_tile(h.shape[0]), host=host)


def _mla_consts_from_refs(qa, wq, kva, wkv, qn, kn, perm):
    f = lambda r: r[...].astype(F32)
    return dict(
        qa_w=f(qa), kva_w=f(kva), perm=f(perm),
        wq_n=[wq[h * QK_PAD:h * QK_PAD + HEAD, :].astype(F32) for h in range(HEADS)],
        wq_r=[wq[h * QK_PAD + HEAD:(h + 1) * QK_PAD, :].astype(F32) for h in range(HEADS)],
        wk_n=[wkv[h * QK_PAD:h * QK_PAD + HEAD, :].astype(F32) for h in range(HEADS)],
        wv=[wkv[h * QK_PAD + HEAD:(h + 1) * QK_PAD, :].astype(F32) for h in range(HEADS)],
        qn_n=qn[:, 0:HEAD], qn_r=qn[:, HEAD:QK_PAD], kn_n=kn[:, 0:HEAD], kn_r=kn[:, HEAD:QK_PAD])


def _mla_prep_fwd(q_lat, kv_lat, k_pe, cos, sin, qa, wq, kva, wkv, qn, kn, perm):
    def body(ql, kvl, kp, c, s, qa_r, wq_r, kva_r, wkv_r, qn_r, kn_r, p_r, q_out, k_out, v_out):
        consts = _mla_consts_from_refs(qa_r, wq_r, kva_r, wkv_r, qn_r, kn_r, p_r)
        outs = _mla_prep_fn((ql[...], kvl[...], kp[...], c[...], s[...]), consts)
        for h in range(HEADS):
            q_n, q_r, k_n, k_r, v = outs[5 * h:5 * h + 5]
            q_out[:, h * QK_PAD:h * QK_PAD + HEAD] = q_n.astype(BF16)
            q_out[:, h * QK_PAD + HEAD:(h + 1) * QK_PAD] = q_r.astype(BF16)
            k_out[:, h * QK_PAD:h * QK_PAD + HEAD] = k_n.astype(BF16)
            k_out[:, h * QK_PAD + HEAD:(h + 1) * QK_PAD] = k_r.astype(BF16)
            v_out[:, h * HEAD:(h + 1) * HEAD] = v.astype(BF16)

    return _rows_call("mla_prep_fwd", body, [q_lat, kv_lat, k_pe, cos, sin], [qa, wq, kva, wkv, qn, kn, perm],
                      [(HEADS * QK_PAD, BF16), (HEADS * QK_PAD, BF16), (DN_WIDTH, BF16)], [], _row_tile(cos.shape[0], 4))


def _mla_prep_bwd(q_lat, kv_lat, k_pe, cos, sin, dq, dk, dv, qa, wq, kva, wkv, qn, kn, perm, host=None):
    def body(ql, kvl, kp, c, s, dq_r, dk_r, dv_r, qa_r, wq_r, kva_r, wkv_r, qn_r, kn_r, p_r,
             dql, dkvl, dkp, dqa, dwq, dkva, dwkv, dqn, dkn):
        consts = _mla_consts_from_refs(qa_r, wq_r, kva_r, wkv_r, qn_r, kn_r, p_r)
        cc, ss, pm = c[...], s[...], consts.pop("perm")
        _, vjp = jax.vjp(lambda rows, cs: _mla_prep_fn((*rows, cc, ss), dict(cs, perm=pm)), (ql[...], kvl[...], kp[...]),
                         consts)
        cts = []
        for h in range(HEADS):
            cts += [dq_r[:, h * QK_PAD:h * QK_PAD + HEAD], dq_r[:, h * QK_PAD + HEAD:(h + 1) * QK_PAD],
                    dk_r[:, h * QK_PAD:h * QK_PAD + HEAD], dk_r[:, h * QK_PAD + HEAD:(h + 1) * QK_PAD],
                    dv_r[:, h * HEAD:(h + 1) * HEAD]]
        (d_ql, d_kvl, d_kp), dc = vjp(tuple(cts))
        dql[...] = d_ql.astype(BF16)
        dkvl[...] = d_kvl.astype(BF16)
        dkp[...] = d_kp.astype(BF16)
        first = pl.program_id(0) == 0

        def acc(ref, sl, val):
            @pl.when(first)
            def _():
                ref[sl] = val

            @pl.when(jnp.logical_not(first))
            def _():
                ref[sl] += val

        full = (slice(None), slice(None))
        acc(dqa, full, dc["qa_w"])
        acc(dkva, full, dc["kva_w"])
        for h in range(HEADS):
            acc(dwq, (slice(h * QK_PAD, h * QK_PAD + HEAD), slice(None)), dc["wq_n"][h])
            acc(dwq, (slice(h * QK_PAD + HEAD, (h + 1) * QK_PAD), slice(None)), dc["wq_r"][h])
            acc(dwkv, (slice(h * QK_PAD, h * QK_PAD + HEAD), slice(None)), dc["wk_n"][h])
            acc(dwkv, (slice(h * QK_PAD + HEAD, (h + 1) * QK_PAD), slice(None)), dc["wv"][h])
        acc(dqn, (slice(None), slice(0, HEAD)), dc["qn_n"])
        acc(dqn, (slice(None), slice(HEAD, QK_PAD)), dc["qn_r"])
        acc(dkn, (slice(None), slice(0, HEAD)), dc["kn_n"])
        acc(dkn, (slice(None), slice(HEAD, QK_PAD)), dc["kn_r"])

    return _rows_call("mla_prep_bwd", body, [q_lat, kv_lat, k_pe, cos, sin, dq, dk, dv],
                      [qa, wq, kva, wkv, qn, kn, perm],
                      [(LORA, BF16), (LORA, BF16), (HEAD, BF16)],
                      [(1, LORA), wq.shape, (1, LORA), wkv.shape, (1, QK_PAD), (1, QK_PAD)], _row_tile(cos.shape[0], 4),
                      host=host)


ATTN_Q_ROWS = 256


def _attn_blocks(t):
    return [(r0, min(ATTN_Q_ROWS, t - r0)) for r0 in range(0, t, ATTN_Q_ROWS)]


def _attn_fwd(q, k, v, host=None):
    t = q.shape[0]

    def body(q_ref, k_ref, v_ref, o_ref):
        for r0, rows in _attn_blocks(t):
            ext = r0 + rows
            o_ref[r0:ext, :] = _attn_fn(q_ref[r0:ext, :], k_ref[0:ext, :], v_ref[0:ext, :], r0)

    qk_spec = pl.BlockSpec((t, QK_PAD), lambda h: (0, h))
    v_spec = pl.BlockSpec((t, HEAD), lambda h: (0, h))
    return _pcall(body, "attn_fwd", (HEADS,), [qk_spec, qk_spec, v_spec], v_spec,
                  jax.ShapeDtypeStruct((t, HEADS * HEAD), F32), [q, k, v], ("arbitrary",), host=host)


def _attn_bwd(q, k, v, do, host=None):
    t = q.shape[0]

    def body(q_ref, k_ref, v_ref, do_ref, dq_ref, dk_ref, dv_ref):
        dk_ref[...] = jnp.zeros(dk_ref.shape, F32)
        dv_ref[...] = jnp.zeros(dv_ref.shape, F32)
        for r0, rows in _attn_blocks(t):
            ext = r0 + rows
            _, vjp = jax.vjp(functools.partial(_attn_fn, row0=r0), q_ref[r0:ext, :].astype(F32),
                             k_ref[0:ext, :].astype(F32), v_ref[0:ext, :].astype(F32))
            dq, dk, dv = vjp(do_ref[r0:ext, :])
            dq_ref[r0:ext, :] = dq
            dk_ref[0:ext, :] += dk
            dv_ref[0:ext, :] += dv

    qk_spec = pl.BlockSpec((t, QK_PAD), lambda h: (0, h))
    v_spec = pl.BlockSpec((t, HEAD), lambda h: (0, h))
    return _pcall(body, "attn_bwd", (HEADS,), [qk_spec, qk_spec, v_spec, v_spec], [qk_spec, qk_spec, v_spec],
                  [jax.ShapeDtypeStruct((t, HEADS * QK_PAD), F32), jax.ShapeDtypeStruct((t, HEADS * QK_PAD), F32),
                   jax.ShapeDtypeStruct((t, HEADS * HEAD), F32)], [q, k, v, do], ("arbitrary",), host=host)


def _mix_out_proj(o_mla, o_dn, z, w_mla, w_dn, w_out, h0, w_ffn):
    def body(om_ref, od_ref, z_ref, h0_ref, wm_ref, wd_ref, wo_ref, wf_ref, mixed_ref, h1_ref, n2_ref):
        for h in range(HEADS):
            sl = slice(h * HEAD, (h + 1) * HEAD)
            mixed_ref[:, sl] = _rms(om_ref[:, sl], wm_ref[...], HEAD).astype(BF16)
            mixed_ref[:, DN_WIDTH + h * HEAD:DN_WIDTH + (h + 1) * HEAD] = _dn_out_fn(od_ref[:, sl], z_ref[:, sl],
                                                                                     wd_ref[...]).astype(BF16)
        h1 = _mm(mixed_ref[...], wo_ref[...]) + h0_ref[...]
        h1_ref[...] = h1
        n2_ref[...] = _rms(h1, wf_ref[...], D_MODEL).astype(BF16)

    return _rows_call("mix_out_proj", body, [o_mla, o_dn, z, h0], [w_mla, w_dn, w_out, w_ffn],
                      [(D_MODEL, BF16), (D_MODEL, F32), (D_MODEL, BF16)], [], _row_tile(o_mla.shape[0], 4))


def _down_proj_loss(act, w_down, h1, tgt, n_valid):
    t, n = h1.shape
    r = _row_tile(t, 4)

    def body(a_ref, h_ref, t_ref, w_ref, dy_ref, dy16_ref, acc_ref):
        h2 = _mm(a_ref[...], w_ref[...]) + h_ref[...]
        rows = pl.program_id(0) * r + lax.broadcasted_iota(jnp.int32, (r, n), 0)
        valid = jnp.logical_and(rows >= N_META, rows < n_valid)
        e = jnp.where(valid, h2 - t_ref[...], 0.0)
        dy = e * (1.0 / n)
        dy_ref[...] = dy
        dy16_ref[...] = dy.astype(BF16)
        _accumulate(acc_ref, jnp.sum(e * e, axis=0, keepdims=True))

    return _rows_call("down_proj_loss", body, [act, h1, tgt], [w_down], [(n, F32), (n, BF16)], [(1, n)], r)


def _ffn_in_bwd(dgpre, dup, w_gate_t, w_up_t, h1, dy, w_ffn, host=None):
    n = h1.shape[1]

    def body(dg_ref, du_ref, h_ref, dy_ref, wg_ref, wu_ref, w_ref, dh_ref, dh16_ref, dw_ref):
        ct = _mm(dg_ref[...], wg_ref[...]) + _mm(du_ref[...], wu_ref[...])
        _, vjp = jax.vjp(lambda x, ww: _rms(x, ww, n), h_ref[...], w_ref[...])
        dh, dw = vjp(ct)
        dh = dh + dy_ref[...]
        dh_ref[...] = dh
        dh16_ref[...] = dh.astype(BF16)
        _accumulate(dw_ref, dw)

    return _rows_call("ffn_in_bwd", body, [dgpre, dup, h1, dy], [w_gate_t, w_up_t, w_ffn], [(n, F32), (n, BF16)], [(1, n)],
                      _row_tile(h1.shape[0]), host=host)


def _mix_out_bwd(o_mla, o_dn, z, dh1, w_out, w_mla, w_dn, host=None):
    def body(om_ref, od_ref, z_ref, dh_ref, wo_ref, wm_ref, wd_ref, dom_ref, dod_ref, dz_ref, dwm_ref, dwd_ref):
        dwm = dwd = None
        for h in range(HEADS):
            sl = slice(h * HEAD, (h + 1) * HEAD)
            _, vjp = jax.vjp(lambda o, w: _rms(o, w, HEAD), om_ref[:, sl], wm_ref[...])
            do, dw = vjp(_mm_nt(dh_ref[...], wo_ref[sl, :]))
            dom_ref[:, sl] = do
            dwm = dw if dwm is None else dwm + dw
            _, vjp = jax.vjp(_dn_out_fn, od_ref[:, sl], z_ref[:, sl], wd_ref[...])
            do, dz, dw = vjp(_mm_nt(dh_ref[...], wo_ref[DN_WIDTH + h * HEAD:DN_WIDTH + (h + 1) * HEAD, :]))
            dod_ref[:, sl] = do
            dz_ref[:, sl] = dz.astype(BF16)
            dwd = dw if dwd is None else dwd + dw
        _accumulate(dwm_ref, dwm)
        _accumulate(dwd_ref, dwd)

    return _rows_call("mix_out_bwd", body, [o_mla, o_dn, z, dh1], [w_out, w_mla, w_dn],
                      [(DN_WIDTH, F32), (DN_WIDTH, F32), (DN_WIDTH, BF16)], [(1, HEAD), (1, HEAD)],
                      _row_tile(o_mla.shape[0], 4), host=host)


def _shift_down(x, s):
    if s == 0:
        return x
    rows = lax.broadcasted_iota(jnp.int32, x.shape, 0)
    return jnp.where(rows >= s, pltpu.roll(x, s, 0), 0.0)


def _shift_up(x, s):
    if s == 0:
        return x
    t = x.shape[0]
    rows = lax.broadcasted_iota(jnp.int32, x.shape, 0)
    return jnp.where(rows < t - s, pltpu.roll(x, t - s, 0), 0.0)


def _col_call(name, body, cols, taps, outs, tap_outs, cw, host=None):
    t, c = cols[0].shape[0], taps[0].shape[1]
    in_specs = [pl.BlockSpec((t, cw), lambda j: (0, j)) for _ in cols]
    in_specs += [pl.BlockSpec((a.shape[0], cw), lambda j: (0, j)) for a in taps]
    out_shape = [jax.ShapeDtypeStruct((t, c), dt) for dt in outs] + [jax.ShapeDtypeStruct((n, c), F32) for n in tap_outs]
    out_specs = [pl.BlockSpec((t, cw), lambda j: (0, j)) for _ in outs]
    out_specs += [pl.BlockSpec((n, cw), lambda j: (0, j)) for n in tap_outs]
    return _pcall(body, name, (c // cw,), in_specs, out_specs, out_shape, [*cols, *taps], ("arbitrary",), host=host)


def _causal_conv(x, w_ref, width):
    acc = w_ref[width - 1:width, :] * x
    for j in range(width - 1):
        acc = acc + w_ref[j:j + 1, :] * _shift_down(x, width - 1 - j)
    return acc


def _causal_conv_bwd(x, dpre, w_ref, dx_ref, dw_ref, width):
    dx = w_ref[width - 1:width, :] * dpre
    dw_ref[width - 1:width, :] = jnp.sum(dpre * x, axis=0, keepdims=True)
    for j in range(width - 1):
        s = width - 1 - j
        dx = dx + w_ref[j:j + 1, :] * _shift_up(dpre, s)
        dw_ref[j:j + 1, :] = jnp.sum(dpre * _shift_down(x, s), axis=0, keepdims=True)
    dx_ref[...] = dx.astype(dx_ref.dtype)


def _dsilu(x):
    sg = jax.nn.sigmoid(x)
    return sg * (1.0 + x * (1.0 - sg))


def _dn_conv_fwd(x, w):
    def body(x_ref, w_ref, y_ref):
        y_ref[...] = _silu(_causal_conv(x_ref[...], w_ref, 4))

    return _col_call("dn_conv_fwd", body, [x], [w], [F32], [], 256)[0]


def _dn_conv_bwd(x, w, dy):
    def body(x_ref, dy_ref, w_ref, dx_ref, dw_ref):
        xv = x_ref[...]
        dpre = dy_ref[...] * _dsilu(_causal_conv(xv, w_ref, 4))
        _causal_conv_bwd(xv, dpre, w_ref, dx_ref, dw_ref, 4)

    return _col_call("dn_conv_bwd", body, [x, dy], [w], [BF16], [4], 256)


def _ffn_glu_fwd(n2, w_gate_t, w_up_t, w, b, host=None):
    t, k = n2.shape
    c, cw = w_gate_t.shape[0], 256

    def body(n_ref, wg_ref, wu_ref, w_ref, b_ref, g_ref, u_ref, a_ref):
        nv = n_ref[...]
        g16 = _mm_nt(nv, wg_ref[...]).astype(BF16)
        u16 = _mm_nt(nv, wu_ref[...]).astype(BF16)
        g_ref[...] = g16
        u_ref[...] = u16
        gate = _causal_conv(g16.astype(F32), w_ref, 3) + b_ref[...]
        a_ref[...] = (_silu(gate) * u16.astype(F32)).astype(BF16)

    wspec = pl.BlockSpec((cw, k), lambda j: (j, 0))
    col = pl.BlockSpec((t, cw), lambda j: (0, j))
    in_specs = [pl.BlockSpec((t, k), lambda j: (0, 0)), wspec, wspec, pl.BlockSpec((w.shape[0], cw), lambda j: (0, j)),
                pl.BlockSpec((1, cw), lambda j: (0, j))]
    return _pcall(body, "ffn_glu_fwd", (c // cw,), in_specs, [col] * 3, [jax.ShapeDtypeStruct((t, c), BF16)] * 3,
                  [n2, w_gate_t, w_up_t, w, b], ("arbitrary",), host=host)


def _ffn_glu_bwd(gpre, up, dy16, w_down, w, b):
    t, k = dy16.shape
    c, cw = w_down.shape[0], 256

    def body(g_ref, u_ref, dy_ref, wd_ref, w_ref, b_ref, dg_ref, du_ref, dw_ref, db_ref):
        gv = g_ref[...].astype(F32)
        gate = _causal_conv(gv, w_ref, 3) + b_ref[...]
        da = _mm_nt(dy_ref[...], wd_ref[...])
        sg = jax.nn.sigmoid(gate)
        du_ref[...] = (da * (gate * sg)).astype(BF16)
        dgate = da * u_ref[...].astype(F32) * (sg * (1.0 + gate * (1.0 - sg)))
        db_ref[...] = jnp.sum(dgate, axis=0, keepdims=True)
        _causal_conv_bwd(gv, dgate, w_ref, dg_ref, dw_ref, 3)

    col = pl.BlockSpec((t, cw), lambda j: (0, j))
    taps = lambda rows: pl.BlockSpec((rows, cw), lambda j: (0, j))
    in_specs = [col, col, pl.BlockSpec((t, k), lambda j: (0, 0)), pl.BlockSpec((cw, k), lambda j: (j, 0)),
                taps(w.shape[0]), taps(1)]
    return _pcall(body, "ffn_glu_bwd", (c // cw,), in_specs, [col, col, taps(w.shape[0]), taps(1)],
                  [jax.ShapeDtypeStruct((t, c), BF16)] * 2 + [jax.ShapeDtypeStruct((w.shape[0], c), F32),
                                                             jax.ShapeDtypeStruct((1, c), F32)],
                  [gpre, up, dy16, w_down, w, b], ("arbitrary",))


def _dn_prep_consts(sa, sb, al, dt):
    return dict(sel_a=sa[...], sel_b=sb[...], alog=al[...], dtb=dt[...])


def _dn_prep_fwd(conv, ab, sel_a, sel_b, alog, dtb):
    def body(c_ref, ab_ref, sa, sb, al, dt, q_out, k_out, g_out, b_out):
        qc = tuple(c_ref[:, h * HEAD:(h + 1) * HEAD] for h in range(HEADS))
        kc = tuple(c_ref[:, DN_WIDTH + h * HEAD:DN_WIDTH + (h + 1) * HEAD] for h in range(HEADS))
        qs, ks, g, beta = _dn_prep_fn((qc, kc, ab_ref[...]), _dn_prep_consts(sa, sb, al, dt))
        for h in range(HEADS):
            q_out[:, h * HEAD:(h + 1) * HEAD] = qs[h]
            k_out[:, h * HEAD:(h + 1) * HEAD] = ks[h]
        g_out[...] = g
        b_out[...] = beta

    return _rows_call("dn_prep_fwd", body, [conv, ab], [sel_a, sel_b, alog, dtb], [(DN_WIDTH, F32)] * 4, [],
                      _row_tile(conv.shape[0]))


def _dn_prep_bwd(conv, ab, dq, dk, dv, dg, db, sel_a, sel_b, alog, dtb):
    def body(c_ref, ab_ref, dq_r, dk_r, dv_r, dg_r, db_r, sa, sb, al, dt, dc_out, dab_out, dal_out, ddt_out):
        qc = tuple(c_ref[:, h * HEAD:(h + 1) * HEAD] for h in range(HEADS))
        kc = tuple(c_ref[:, DN_WIDTH + h * HEAD:DN_WIDTH + (h + 1) * HEAD] for h in range(HEADS))
        consts = _dn_prep_consts(sa, sb, al, dt)
        sel = dict(sel_a=consts["sel_a"], sel_b=consts["sel_b"])
        _, vjp = jax.vjp(lambda rows, ad: _dn_prep_fn(rows, {**sel, **ad}), (qc, kc, ab_ref[...]),
                         dict(alog=consts["alog"], dtb=consts["dtb"]))
        cq = tuple(dq_r[:, h * HEAD:(h + 1) * HEAD] for h in range(HEADS))
        ck = tuple(dk_r[:, h * HEAD:(h + 1) * HEAD] for h in range(HEADS))
        (dqc, dkc, dab), dad = vjp((cq, ck, dg_r[...], db_r[...]))
        for h in range(HEADS):
            dc_out[:, h * HEAD:(h + 1) * HEAD] = dqc[h]
            dc_out[:, DN_WIDTH + h * HEAD:DN_WIDTH + (h + 1) * HEAD] = dkc[h]
        dc_out[:, 2 * DN_WIDTH:3 * DN_WIDTH] = dv_r[...]
        dab_out[...] = dab.astype(BF16)
        _accumulate(dal_out, dad["alog"])
        _accumulate(ddt_out, dad["dtb"])

    return _rows_call("dn_prep_bwd", body, [conv, ab, dq, dk, dv, dg, db], [sel_a, sel_b, alog, dtb],
                      [(3 * DN_WIDTH, F32), (HEAD, BF16)], [(1, DN_WIDTH), (1, DN_WIDTH)], _row_tile(conv.shape[0]))


def _chunk_batch(t):
    nc = t // CHUNK
    return nc // 2 if nc % 2 == 0 else nc


def _dn_chunk_specs(t, nb):
    rows = nb * CHUNK
    blk = pl.BlockSpec((rows, HEAD), lambda h, b: (b, h))
    vblk = pl.BlockSpec((rows, HEAD), lambda h, b: (b, 2 * HEADS + h))
    mat = pl.BlockSpec((nb, HEAD, HEAD), lambda h, b: (b, h, 0))
    return rows, blk, vblk, mat


def _dn_chunk_fwd(qn, kn, conv, g, beta, host=None):
    t = qn.shape[0]
    nb = _chunk_batch(t)
    rows, blk, vblk, mat = _dn_chunk_specs(t, nb)

    def body(q_ref, k_ref, v_ref, g_ref, b_ref, n_o, b_o, qe_o, oo_o, eg_o):
        r3 = lambda x: x.reshape(nb, CHUNK, x.shape[-1])
        n_mat, b_mat, q_eff, o_own, eg = _dn_chunk_fn(r3(q_ref[...]), r3(k_ref[...]), r3(v_ref[...]), r3(g_ref[...]),
                                                      r3(g_ref[:, 0:CHUNK]), r3(b_ref[...]))
        n_o[...] = n_mat
        b_o[...] = b_mat
        qe_o[...] = q_eff.reshape(rows, HEAD)
        oo_o[...] = o_own.reshape(rows, HEAD)
        eg_o[...] = jnp.broadcast_to(eg, (nb, HEAD, HEAD))

    nc = t // CHUNK
    mats = jax.ShapeDtypeStruct((nc, DN_WIDTH, HEAD), F32)
    rowsd = jax.ShapeDtypeStruct((t, DN_WIDTH), F32)
    return _pcall(body, "dn_chunk_fwd", (HEADS, t // rows), [blk, blk, vblk, blk, blk], [mat, mat, blk, blk, mat],
                  [mats, mats, rowsd, rowsd, mats], [qn, kn, conv, g, beta], ("arbitrary", "arbitrary"), host=host)


def _dn_chunk_bwd(qn, kn, conv, g, beta, sall, gall, dq_eff, do, host=None):
    t = qn.shape[0]
    nb = _chunk_batch(t)
    rows, blk, vblk, mat = _dn_chunk_specs(t, nb)

    def body(q_ref, k_ref, v_ref, g_ref, b_ref, s_ref, ga_ref, dqe_ref, do_ref, dq_o, dk_o, dv_o, dg_o, db_o):
        r3 = lambda x: x.reshape(nb, CHUNK, x.shape[-1])
        _, vjp = jax.vjp(_dn_chunk_fn, r3(q_ref[...]), r3(k_ref[...]), r3(v_ref[...]), r3(g_ref[...]),
                         r3(g_ref[:, 0:CHUNK]), r3(b_ref[...]))
        s, ga = s_ref[...], ga_ref[...]
        d_n = -_bmm_nt(ga, s)
        d_eg = jnp.sum(ga * s, axis=1, keepdims=True)
        dq, dk, dv, dg, dg64, db = vjp((d_n, ga, r3(dqe_ref[...]), r3(do_ref[...]), d_eg))
        for o_ref, val in zip((dq_o, dk_o, dv_o, dg_o, db_o), (dq, dk, dv, dg, db)):
            o_ref[...] = val.reshape(rows, HEAD)
        dg_o[:, 0:CHUNK] += dg64.reshape(rows, CHUNK)

    return _pcall(body, "dn_chunk_bwd", (HEADS, t // rows), [blk, blk, vblk, blk, blk, mat, mat, blk, blk], [blk] * 5,
                  [jax.ShapeDtypeStruct((t, DN_WIDTH), F32)] * 5, [qn, kn, conv, g, beta, sall, gall, dq_eff, do],
                  ("arbitrary", "arbitrary"), host=host)


def _dn_rec_fwd(n_mat, b_mat, eg, host=None):
    nc = n_mat.shape[0]
    nb = _chunk_batch(nc * CHUNK)
    spec = pl.BlockSpec((nb, DN_WIDTH, HEAD), lambda i: (i, 0, 0))

    def body(n_ref, b_ref, eg_ref, sall_ref, s_scr):
        @pl.when(pl.program_id(0) == 0)
        def _():
            s_scr[...] = jnp.zeros(s_scr.shape, F32)

        for j in range(nb):
            sall_ref[j] = s_scr[...]
            for h in range(HEADS):
                sl = slice(h * HEAD, (h + 1) * HEAD)
                s_scr[sl, :] = _dn_rec_fn(s_scr[sl, :], n_ref[j, sl, :], b_ref[j, sl, :],
                                          eg_ref[j, h * HEAD:h * HEAD + 1, :])

    return _pcall(body, "dn_rec_fwd", (nc // nb,), [spec] * 3, spec, jax.ShapeDtypeStruct((nc, DN_WIDTH, HEAD), F32),
                  [n_mat, b_mat, eg], ("arbitrary",), scratch_shapes=[pltpu.VMEM((DN_WIDTH, HEAD), F32)], host=host)


def _dn_rec_bwd(n_mat, eg, ds_out, host=None):
    nc = n_mat.shape[0]
    nb = _chunk_batch(nc * CHUNK)
    steps = nc // nb
    spec = pl.BlockSpec((nb, DN_WIDTH, HEAD), lambda i: (steps - 1 - i, 0, 0))

    def body(n_ref, eg_ref, dso_ref, gall_ref, g_scr):
        @pl.when(pl.program_id(0) == 0)
        def _():
            g_scr[...] = jnp.zeros(g_scr.shape, F32)

        for j in reversed(range(nb)):
            gall_ref[j] = g_scr[...]
            for h in range(HEADS):
                sl = slice(h * HEAD, (h + 1) * HEAD)
                gv = g_scr[sl, :]
                g_scr[sl, :] = (gv * eg_ref[j, h * HEAD:h * HEAD + 1, :] - _mm_tn(n_ref[j, sl, :], gv)
                                + dso_ref[j, sl, :])

    return _pcall(body, "dn_rec_bwd", (steps,), [spec] * 3, spec, jax.ShapeDtypeStruct((nc, DN_WIDTH, HEAD), F32),
                  [n_mat, eg, ds_out], ("arbitrary",), scratch_shapes=[pltpu.VMEM((DN_WIDTH, HEAD), F32)], host=host)


def _dn_o_fwd(sall, q_eff, o_own):
    t = q_eff.shape[0]
    nb = _chunk_batch(t)
    rows, blk, _, mat = _dn_chunk_specs(t, nb)

    def body(s_ref, qe_ref, oo_ref, o_ref):
        r3 = lambda x: x.reshape(nb, CHUNK, HEAD)
        o_ref[...] = _dn_o_fn(s_ref[...], r3(qe_ref[...]), r3(oo_ref[...])).reshape(rows, HEAD)

    return _pcall(body, "dn_o_fwd", (HEADS, t // rows), [mat, blk, blk], blk, jax.ShapeDtypeStruct((t, DN_WIDTH), F32),
                  [sall, q_eff, o_own], ("arbitrary", "arbitrary"))


def _dn_o_bwd(sall, q_eff, do, host=None):
    t = q_eff.shape[0]
    nb = _chunk_batch(t)
    rows, blk, _, mat = _dn_chunk_specs(t, nb)

    def body(s_ref, qe_ref, do_ref, dqe_ref, ds_ref):
        r3 = lambda x: x.reshape(nb, CHUNK, HEAD)
        dov = r3(do_ref[...])
        dqe_ref[...] = _bmm_nt(dov, s_ref[...]).reshape(rows, HEAD)
        ds_ref[...] = _bmm_tn(r3(qe_ref[...]), dov)

    nc = t // CHUNK
    return _pcall(body, "dn_o_bwd", (HEADS, t // rows), [mat, blk, blk], [blk, mat],
                  [jax.ShapeDtypeStruct((t, DN_WIDTH), F32), jax.ShapeDtypeStruct((nc, DN_WIDTH, HEAD), F32)],
                  [sall, q_eff, do], ("arbitrary", "arbitrary"), host=host)


def _adamw_update(w, g, m, v):
    m2 = ADAM_B1 * m + (1.0 - ADAM_B1) * g
    v2 = ADAM_B2 * v + (1.0 - ADAM_B2) * (g * g)
    m_hat = m2 / (1.0 - ADAM_B1 ** ADAM_STEP)
    v_hat = v2 / (1.0 - ADAM_B2 ** ADAM_STEP)
    return -ADAM_LR * (m_hat / (jnp.sqrt(v_hat) + ADAM_EPS) + ADAM_WD * w), m2, v2


def _adamw_small(ws, gs, ms, vs):
    n = len(ws)

    def body(*refs):
        for i in range(n):
            d, m2, v2 = _adamw_update(refs[i][...], refs[n + i][...], refs[2 * n + i][...], refs[3 * n + i][...])
            refs[4 * n + i][...] = d
            refs[5 * n + i][...] = m2
            refs[6 * n + i][...] = v2

    shapes = [jax.ShapeDtypeStruct(a.shape, F32) for a in ws]
    outs = pl.pallas_call(body, name="adamw_small", out_shape=shapes * 3,
                          compiler_params=pltpu.CompilerParams(vmem_limit_bytes=VMEM_LIMIT))(*ws, *gs, *ms, *vs)
    return outs[:n], outs[n:2 * n], outs[2 * n:]


def _adamw_call(name, w, g, m, v, host=None):
    rows, cols = w.shape
    by_rows = rows % 8 == 0

    def body(w_ref, g_ref, m_ref, v_ref, g_out, d_ref, m_out, v_out):
        gv = g_ref[...] if by_rows else g_ref[0:rows, :]
        g_out[...] = gv
        d_ref[...], m_out[...], v_out[...] = _adamw_update(w_ref[...], gv, m_ref[...], v_ref[...])

    if by_rows:
        tr = _tile(rows, 256, 8)
        spec = g_spec = pl.BlockSpec((tr, cols), lambda i: (i, 0))
        grid = (rows // tr,)
    else:
        tc = _tile(cols, 256, 128)
        spec = pl.BlockSpec((rows, tc), lambda j: (0, j))
        g_spec = pl.BlockSpec((g.shape[0], tc), lambda j: (0, j))
        grid = (cols // tc,)
    return _pcall(body, name, grid, [spec, g_spec, spec, spec], [spec] * 4, [jax.ShapeDtypeStruct((rows, cols), F32)] * 4,
                  [w, g, m, v], ("arbitrary",), host=host)


def _rope_tables(t):
    half = ROPE // 2
    inv_freq = np.float32(ROPE_THETA) ** (-np.arange(half, dtype=np.float32) / np.float32(half))
    ang = np.arange(t, dtype=np.float32)[:, None] * inv_freq[None, :].astype(np.float32)
    z = np.zeros((t, HEAD - ROPE), np.float32)
    cos = np.concatenate([np.cos(ang), np.cos(ang), z], axis=1).astype(np.float32)
    sin = np.concatenate([np.sin(ang), np.sin(ang), z], axis=1).astype(np.float32)
    k = np.arange(HEAD)[:, None]
    l = np.arange(HEAD)[None, :]
    perm = np.where((l < half) & (k == l + half), -1.0, 0.0) + np.where((l >= half) & (l < ROPE) & (k == l - half), 1.0, 0.0)
    return jnp.asarray(cos), jnp.asarray(sin), jnp.asarray(perm.astype(np.float32))


def _win_to_pad(w):
    z = lambda n: jnp.zeros((n, w.shape[1]), w.dtype)
    return jnp.concatenate([w[576:2112], w[2112:2624], w[0:256], w[256:512], w[512:576], z(64), w[2624:2632], z(120)],
                           axis=0)


def _win_from_pad(g):
    return jnp.concatenate([g[2048:2304], g[2304:2560], g[2560:2624], g[0:1536], g[1536:2048], g[2688:2696]], axis=0)


def _qk_to_pad(w):
    w4 = w.reshape(HEADS, QK_DIM, w.shape[-1])
    return jnp.concatenate([w4, jnp.zeros((HEADS, QK_PAD - QK_DIM, w.shape[-1]), w.dtype)], axis=1).reshape(
        HEADS * QK_PAD, w.shape[-1])


def _qk_from_pad(g):
    return g.reshape(HEADS, QK_PAD, g.shape[-1])[:, :QK_DIM].reshape(HEADS * QK_DIM, g.shape[-1])


def _ff_to_pad(a, axis):
    shape = list(a.shape)
    shape[axis:axis + 1] = [N_CHIPS, FF_SHARD]
    a4 = a.reshape(shape)
    shape[axis + 1] = FF_BLOCK - FF_SHARD
    out = jnp.concatenate([a4, jnp.zeros(shape, a.dtype)], axis=axis + 1)
    shape[axis:axis + 2] = [D_FF_P]
    return out.reshape(shape)


def _ff_from_pad(a, axis):
    shape = list(a.shape)
    shape[axis:axis + 1] = [N_CHIPS, FF_BLOCK]
    a4 = lax.slice_in_dim(a.reshape(shape), 0, FF_SHARD, axis=axis + 1)
    shape[axis:axis + 2] = [D_FF]
    return a4.reshape(shape)


class _LocalPlan:
    def __init__(self, wt):
        self.wt, self.grads = wt, {}

    def weight(self, name):
        return self.wt[name]

    def host(self, point):
        return None

    def grad(self, name, value):
        self.grads[name] = value


def _local_step(x, tgt, wt, plan=None):
    plan = _LocalPlan(wt) if plan is None else plan
    s = x.shape[0]
    n_valid = N_META + s
    t = -(-n_valid // HEAD) * HEAD
    zpad = jnp.zeros((t - n_valid, D_MODEL), F32)
    h0 = jnp.concatenate([wt["meta_tokens"], x, zpad], axis=0)
    tgt_p = jnp.concatenate([jnp.zeros((N_META, D_MODEL), F32), tgt, zpad], axis=0)
    cos, sin, perm = _rope_tables(t)
    qn_w = jnp.concatenate([wt["q_norm_w"], jnp.zeros((1, QK_PAD - QK_DIM), F32)], axis=1)
    kn_w = jnp.concatenate([wt["k_norm_w"], jnp.zeros((1, QK_PAD - QK_DIM), F32)], axis=1)
    head_id = jnp.arange(DN_WIDTH)[None, :] // HEAD
    lane = jnp.arange(HEAD)[:, None]
    sel_a = (lane == head_id).astype(F32)
    sel_b = (lane == head_id + HEADS).astype(F32)
    alog = jnp.repeat(wt["dn_A_log"], HEAD, axis=1)
    dtb = jnp.repeat(wt["dn_dt_bias"], HEAD, axis=1)
    conv_w, conv_b = wt["ffn_conv_w"], wt["ffn_conv_b"]

    u = _rms_fwd("attn_norm_fwd", h0, wt["attn_norm_w"], host=plan.host("attn_norm_fwd"))
    win, wq, wkv = plan.weight("w_in_t"), plan.weight("w_q_t"), plan.weight("w_kv_t")
    proj = _matmul("in_proj", u, win, "nt", F32)
    z = (proj, DN_WIDTH, 3)
    q_lat, kv_lat, k_pe, ab = (proj, LORA, 8), (proj, LORA, 9), (proj, HEAD, 20), (proj, HEAD, 21)
    mla_consts = (wt["q_a_norm_w"], wq, wt["kv_a_norm_w"], wkv, qn_w, kn_w, perm)
    q, k, v = _mla_prep_fwd(q_lat, kv_lat, k_pe, cos, sin, *mla_consts)
    o_mla = _attn_fwd(q, k, v, host=plan.host("attn_fwd"))
    conv = _dn_conv_fwd(proj, wt["dn_conv_w"])
    dn_consts = (sel_a, sel_b, alog, dtb)
    qn, kn, g, beta = _dn_prep_fwd(conv, ab, *dn_consts)
    n_mat, b_mat, q_eff, o_own, eg = _dn_chunk_fwd(qn, kn, conv, g, beta, host=plan.host("dn_chunk_fwd"))
    sall = _dn_rec_fwd(n_mat, b_mat, eg)
    o_dn = _dn_o_fwd(sall, q_eff, o_own)
    w_out = plan.weight("w_out")
    mixed, h1, n2 = _mix_out_proj(o_mla, o_dn, z, wt["mla_out_norm_w"], wt["dn_out_norm_w"], w_out, h0, wt["ffn_norm_w"])
    w_gate, w_up = plan.weight("w_gate_t"), plan.weight("w_up_t")
    gpre, up, act = _ffn_glu_fwd(n2, w_gate, w_up, conv_w, conv_b, host=plan.host("ffn_glu_fwd"))
    w_down = plan.weight("w_down")
    dy, dy16, sq = _down_proj_loss(act, w_down, h1, tgt_p, n_valid)

    grads = {}
    plan.grad("w_down", _matmul("down_dw", act, dy16, "tn", BF16))
    dgpre, dup, grads["ffn_conv_w"], grads["ffn_conv_b"] = _ffn_glu_bwd(gpre, up, dy16, w_down, conv_w, conv_b)
    plan.grad("w_gate_t", _matmul("gate_dw", dgpre, n2, "tn", BF16))
    plan.grad("w_up_t", _matmul("up_dw", dup, n2, "tn", BF16))
    dh1, dh1_16, grads["ffn_norm_w"] = _ffn_in_bwd(dgpre, dup, w_gate, w_up, h1, dy, wt["ffn_norm_w"],
                                                   host=plan.host("ffn_in_bwd"))
    plan.grad("w_out", _matmul("out_dw", mixed, dh1_16, "tn", BF16))
    do_mla, do_dn, dz, grads["mla_out_norm_w"], grads["dn_out_norm_w"] = _mix_out_bwd(
        o_mla, o_dn, z, dh1_16, w_out, wt["mla_out_norm_w"], wt["dn_out_norm_w"], host=plan.host("mix_out_bwd"))
    dq_eff, ds_out = _dn_o_bwd(sall, q_eff, do_dn)
    gall = _dn_rec_bwd(n_mat, eg, ds_out)
    dqn, dkn, dv_dn, dg, dbeta = _dn_chunk_bwd(qn, kn, conv, g, beta, sall, gall, dq_eff, do_dn,
                                               host=plan.host("dn_chunk_bwd"))
    dconv, dab, dalog, ddtb = _dn_prep_bwd(conv, ab, dqn, dkn, dv_dn, dg, dbeta, *dn_consts)
    grads["dn_A_log"] = jnp.sum(dalog.reshape(HEADS, HEAD), axis=1)[None, :]
    grads["dn_dt_bias"] = jnp.sum(ddtb.reshape(HEADS, HEAD), axis=1)[None, :]
    ddn_pre, grads["dn_conv_w"] = _dn_conv_bwd(proj, wt["dn_conv_w"], dconv)
    dq, dk, dv = _attn_bwd(q, k, v, do_mla, host=plan.host("attn_bwd"))
    dq_lat, dkv_lat, dk_pe, dqa, dwq, dkva, dwkv, dqnw, dknw = _mla_prep_bwd(
        q_lat, kv_lat, k_pe, cos, sin, dq, dk, dv, *mla_consts, host=plan.host("mla_prep_bwd"))
    grads["q_a_norm_w"], grads["kv_a_norm_w"] = dqa, dkva
    plan.grad("w_q_t", dwq)
    plan.grad("w_kv_t", dwkv)
    grads["q_norm_w"], grads["k_norm_w"] = dqnw[:, :QK_DIM], dknw[:, :QK_DIM]
    dproj = jnp.concatenate([ddn_pre, dz, dq_lat, dkv_lat, dk_pe, dab], axis=1)
    plan.grad("w_in_t", _matmul("in_dw", dproj, u, "tn", F32))
    du = _matmul("in_dx", dproj, win, "nn", BF16, host=plan.host("in_dx"))
    dh0, _, grads["attn_norm_w"] = _rms_bwd("attn_norm_bwd", h0, wt["attn_norm_w"], [du], dh1,
                                            host=plan.host("attn_norm_bwd"))
    grads["meta_tokens"] = dh0[0:N_META]
    if isinstance(plan, _LocalPlan):
        grads.update(plan.grads)
    return sq, dh0[N_META:n_valid], grads


def _mesh_pos():
    return lax.axis_index("x"), lax.axis_index("y"), lax.axis_index("c")


def _other_chips(x, y):
    return [(1 - x, y), (x, 1 - y), (1 - x, 1 - y)]


def _remote(src, dst, send_sems, recv_sems, k, to):
    return pltpu.make_async_remote_copy(src_ref=src, dst_ref=dst, send_sem=send_sems.at[k], recv_sem=recv_sems.at[k],
                                        device_id=to, device_id_type=MESH)


SIBLING_ID, CHIPS_ID, GATHER_ID, ALL_ID = 1, 2, 3, 4


def _sibling_peer():
    x, y, c = _mesh_pos()
    return [(x, y, 1 - c)]


def _chip_peers():
    x, y, c = _mesh_pos()
    return [(qx, qy, c) for qx, qy in _other_chips(x, y)]


def _copies_exchange(make, ins, out_shape, nsem, peers=None, cid=None):
    def prog(in_refs, out_refs, send_sems, recv_sems):
        copies = make(in_refs, out_refs, send_sems, recv_sems)

        def start():
            for cp in copies:
                cp.start()

        def finish():
            for cp in copies:
                cp.wait()

        return start, finish

    return _Exchange(prog, ins, out_shape, nsem, peers, cid)


def _all_gather(shards):
    def prog(srcs, dsts, send_sems, recv_sems):
        x, y, c = _mesh_pos()
        p = 2 * x + y
        sibling = (x, y, 1 - c)
        chips = _other_chips(x, y)
        bufs = tuple((s, d, s.shape[0] // 2) for s, d in zip(srcs, dsts))

        def half(ref, rows, which):
            return ref.at[pl.ds(which * rows, rows), :]

        def copy(i, k, src, dst, to):
            return _remote(src, dst, send_sems, recv_sems, 6 * i + k, to)

        sends = [copy(i, j, half(src, rows, c), half(dst.at[p], rows, c), (*chip, c))
                 for i, (src, dst, rows) in enumerate(bufs) for j, chip in enumerate(chips)]

        def start():
            for cp in sends:
                cp.start()

        def finish():
            passed = []
            for i, (src, dst, rows) in enumerate(bufs):
                for j, (qx, qy) in enumerate(chips):
                    block = half(dst.at[2 * qx + qy], rows, c)
                    copy(i, j, block, block, (x, y, c)).wait_recv()
                    fwd = copy(i, 3 + j, block, block, sibling)
                    fwd.start()
                    passed.append(fwd)
            for i, (src, dst, rows) in enumerate(bufs):
                for j, (qx, qy) in enumerate(chips):
                    block = half(dst.at[2 * qx + qy], rows, 1 - c)
                    copy(i, 3 + j, block, block, (x, y, c)).wait_recv()
            for cp in sends + passed:
                cp.wait_send()

        return start, finish

    return _Exchange(prog, shards, [jax.ShapeDtypeStruct((N_CHIPS, *s.shape), s.dtype) for s in shards], 6 * len(shards),
                     lambda: _sibling_peer() + _chip_peers(), GATHER_ID)


def _gathered(ex):
    p = 2 * lax.axis_index("x") + lax.axis_index("y")
    return [lax.dynamic_update_slice(g, s[None], (p, 0, 0)) for g, s in zip(ex.outs, ex.ins)]


def _rs_to_sibling(bufs):
    def make(srcs, dsts, send_sems, recv_sems):
        x, y, c = _mesh_pos()
        copies = []
        for i, (src, dst) in enumerate(zip(srcs, dsts)):
            half = src.shape[1] // 2
            copies.append(_remote(src.at[:, pl.ds((1 - c) * half, half), :], dst, send_sems, recv_sems, i, (x, y, 1 - c)))
        return copies

    return _copies_exchange(make, bufs,
                            [jax.ShapeDtypeStruct((N_CHIPS, b.shape[1] // 2, b.shape[2]), b.dtype) for b in bufs],
                            len(bufs), _sibling_peer, SIBLING_ID)


def _rs_pair_add(name, bufs, gots, c, out_dtype):
    n = len(bufs)

    def body(c_ref, *refs):
        for a_ref, b_ref, o_ref in zip(refs[:n], refs[n:2 * n], refs[2 * n:]):
            o_ref[...] = (a_ref[...].astype(F32) + b_ref[...].astype(F32)).astype(out_dtype)

    mine = [pl.BlockSpec((None, g.shape[1], g.shape[2]), lambda j, cr: (j, cr[0], 0)) for g in gots]
    whole = [pl.BlockSpec((None, g.shape[1], g.shape[2]), lambda j, cr: (j, 0, 0)) for g in gots]
    return pl.pallas_call(
        body, name=name,
        grid_spec=pltpu.PrefetchScalarGridSpec(num_scalar_prefetch=1, grid=(N_CHIPS,), in_specs=mine + whole, out_specs=whole),
        out_shape=[jax.ShapeDtypeStruct(g.shape, out_dtype) for g in gots],
        compiler_params=_cparams(("arbitrary",)))(c, *bufs, *gots)


def _rs_to_chips(accs):
    def make(srcs, dsts, send_sems, recv_sems):
        x, y, c = _mesh_pos()
        return [_remote(src.at[2 * qx + qy], dst.at[k], send_sems, recv_sems, 3 * i + k, (qx, qy, c))
                for i, (src, dst) in enumerate(zip(srcs, dsts)) for k, (qx, qy) in enumerate(_other_chips(x, y))]

    return _copies_exchange(make, accs, [jax.ShapeDtypeStruct((3, a.shape[1], a.shape[2]), a.dtype) for a in accs],
                            3 * len(accs), _chip_peers, CHIPS_ID)


def _rs_chip_add(name, accs, gots, p):
    n = len(accs)
    slot = (0, 1, 0, 2)

    def body(p_ref, *refs):
        me = p_ref[0]
        for own_ref, got_ref, o_ref in zip(refs[:n], refs[n:2 * n], refs[2 * n:]):
            total = None
            for chip in range(N_CHIPS):
                val = own_ref[...].astype(F32)
                for e in (1, 2, 3):
                    val = jnp.where((chip ^ me) == e, got_ref[slot[e]].astype(F32), val)
                total = val if total is None else total + val
            o_ref[...] = total

    own = [pl.BlockSpec((None, a.shape[1], a.shape[2]), lambda i, pr: (pr[0], 0, 0)) for a in accs]
    got = [pl.BlockSpec(g.shape, lambda i, pr: (0, 0, 0)) for g in gots]
    out = [pl.BlockSpec((a.shape[1], a.shape[2]), lambda i, pr: (0, 0)) for a in accs]
    return pl.pallas_call(
        body, name=name,
        grid_spec=pltpu.PrefetchScalarGridSpec(num_scalar_prefetch=1, grid=(1,), in_specs=own + got, out_specs=out),
        out_shape=[jax.ShapeDtypeStruct((a.shape[1], a.shape[2]), F32) for a in accs],
        compiler_params=_cparams(("arbitrary",)))(p, *accs, *gots)


def _rs_share(ress):
    def make(srcs, dsts, send_sems, recv_sems):
        x, y, c = _mesh_pos()
        return [_remote(src, dst, send_sems, recv_sems, i, (x, y, 1 - c)) for i, (src, dst) in enumerate(zip(srcs, dsts))]

    return _copies_exchange(make, ress, [jax.ShapeDtypeStruct(r.shape, F32) for r in ress], len(ress), _sibling_peer,
                            SIBLING_ID)


def _shared(ex):
    south = lax.axis_index("c") == 0
    return [jnp.concatenate([jnp.where(south, r, g), jnp.where(south, g, r)], axis=0) for r, g in zip(ex.ins, ex.outs)]


def _all_to_all_devices(vec):
    def others():
        x, y, c = _mesh_pos()
        return [((1 - x if r & 4 else x), (1 - y if r & 2 else y), (1 - c if r & 1 else c)) for r in range(1, 8)]

    def make(srcs, dsts, send_sems, recv_sems):
        x, y, c = _mesh_pos()
        me = 4 * x + 2 * y + c
        return [_remote(srcs[0], dsts[0].at[me], send_sems, recv_sems, r, peer) for r, peer in enumerate(others())]

    return _copies_exchange(make, [vec], [jax.ShapeDtypeStruct((8, *vec.shape), vec.dtype)], 7, others, ALL_ID)


def _sum_devices(stack):
    def body(s_ref, o_ref):
        total = s_ref[0]
        for d in range(1, 8):
            total = total + s_ref[d]
        o_ref[...] = total

    return pl.pallas_call(body, name="sum_devices", out_shape=jax.ShapeDtypeStruct(stack.shape[1:], F32),
                          compiler_params=pltpu.CompilerParams(vmem_limit_bytes=VMEM_LIMIT))(stack)


def _pad_rows(flat, rows):
    return jnp.concatenate([flat, jnp.zeros((rows * LANES - flat.shape[0],), flat.dtype)]).reshape(rows, LANES)


def _unshard(g4, shape, axis):
    a = g4.reshape(N_CHIPS, *shape)
    if axis == 0:
        return a.reshape(N_CHIPS * shape[0], shape[1])
    return jnp.transpose(a, (1, 0, 2)).reshape(shape[0], N_CHIPS * shape[1])


def _shard4(full, shape, axis):
    if axis == 0:
        return full.reshape(N_CHIPS, shape[0] * shape[1])
    a = full.reshape(shape[0], N_CHIPS, shape[1])
    return jnp.transpose(a, (1, 0, 2)).reshape(N_CHIPS, shape[0] * shape[1])


def _pad_axis0(a, rows):
    return jnp.concatenate([a, jnp.zeros((rows - a.shape[0], *a.shape[1:]), a.dtype)], axis=0)


def _pad_axis1(a, rows):
    return jnp.concatenate([a, jnp.zeros((a.shape[0], rows - a.shape[1], *a.shape[2:]), a.dtype)], axis=1)


def _shard_to_strip(name, w):
    _, (shape, axis, rows) = name, {n: (s, ax, r) for n, s, ax, r in BIG}[name]
    w2 = w.reshape(shape).astype(BF16)
    if name == "w_in":
        return w2
    return _pad_axis0(w2.T if axis == 1 else w2, rows)


LOCAL_NAME = dict(w_in="w_in_t", w_q_b="w_q_t", w_kv_b="w_kv_t", w_out="w_out", w_gate="w_gate_t", w_up="w_up_t",
                  w_down="w_down")


WIN_SEGMENTS = ((576, 2112, 0), (2112, 2624, 1536), (0, 256, 2048), (256, 512, 2304), (512, 576, 2560), (2624, 2632, 2688))


def _strips_to_weight(name, g4):
    if name == "w_in":
        return _win_to_pad(jnp.transpose(g4, (0, 2, 1)).reshape(IN_COLS, D_MODEL))
    if name == "w_q_b":
        return _qk_to_pad(g4.reshape(HEADS * QK_DIM, LORA))
    return g4.reshape(N_CHIPS * g4.shape[1], g4.shape[2])


def _grad_to_strips(name, g):
    if name == "w_in":
        strips = []
        for q in range(N_CHIPS):
            pieces = []
            for a, b, local in sorted(WIN_SEGMENTS):
                s, e = max(a, q * IN_SHARD), min(b, (q + 1) * IN_SHARD)
                if s < e:
                    pieces.append(g[local + s - a:local + e - a])
            pieces.append(jnp.zeros((IN_SHARD_P - IN_SHARD, D_MODEL), g.dtype))
            strips.append(jnp.concatenate(pieces, axis=0))
        return jnp.stack(strips)
    if name == "w_q_b":
        return _qk_from_pad(g).reshape(N_CHIPS, QK_DIM, LORA)
    return g.reshape(N_CHIPS, g.shape[0] // N_CHIPS, g.shape[1])


class _MeshPlan:
    LATE = dict(attn_norm_fwd=("w_in", "w_q_b", "w_kv_b"), attn_fwd=("w_up",), dn_chunk_fwd=("w_out", "w_gate"),
                ffn_glu_fwd=("w_down",))
    GROUP_A = ("w_down", "w_gate", "w_up", "w_out")
    GROUP_B = ("w_in", "w_q_b", "w_kv_b")

    def __init__(self, w):
        x, y, c = _mesh_pos()
        self.ci = jnp.reshape(c, (1,)).astype(jnp.int32)
        self.pi = jnp.reshape(2 * x + y, (1,)).astype(jnp.int32)
        self.strip = {n: _shard_to_strip(n, w[n]) for n, _, _, _ in BIG}
        self.gathers, self.weights, self.g, self.acc, self.reduced = {}, {}, {}, {}, {}
        self.sibs, self.sib, self.chip, self.share, self.halves = [], None, None, None, [None, None]

    def gather_small(self, small):
        ex = _all_gather([small])
        ex.run("all_gather_small")
        return _gathered(ex)[0]

    def weight(self, local_name):
        if local_name not in self.weights:
            for point, (names, ex) in list(self.gathers.items()):
                if ex.outs is not None:
                    for n, g4 in zip(names, _gathered(ex)):
                        if "/" in n:
                            n, half = n.split("/")
                            self.halves[int(half)] = g4
                            if None in self.halves:
                                continue
                            g4 = jnp.concatenate(self.halves, axis=1)
                        self.weights[LOCAL_NAME[n]] = _strips_to_weight(n, g4)
                    del self.gathers[point]
        return self.weights[local_name]

    def _shard(self, name):
        if "/" not in name:
            return self.strip[name]
        name, half = name.split("/")
        rows = self.strip[name].shape[0] // 2
        return self.strip[name][int(half) * rows:(int(half) + 1) * rows]

    def grad(self, local_name, value):
        name = {v: k for k, v in LOCAL_NAME.items()}[local_name]
        self.g[name] = _grad_to_strips(name, value)

    def _pair_add(self, names, gots):
        accs = _rs_pair_add("rs_pair_add_" + names[0], [self.g[n] for n in names], gots, self.ci, BF16)
        self.acc.update(zip(names, accs))

    def _chip_add(self, names, chip):
        return _rs_chip_add("rs_chip_add_" + names[0], [self.acc[n] for n in names], chip.outs, self.pi)

    def _take_shared(self, names, share):
        for n, strip in zip(names, _shared(share)):
            self.reduced[n] = strip

    def host(self, point):
        a, b = self.GROUP_A, self.GROUP_B
        if point in self.LATE:
            names = self.LATE[point]
            ex = _all_gather([self._shard(n) for n in names])
            self.gathers[point] = (names, ex)
            return ex
        if point in ("ffn_in_bwd", "mix_out_bwd"):
            names = dict(ffn_in_bwd=a[:3], mix_out_bwd=a[3:])[point]
            ex = _rs_to_sibling([self.g[n] for n in names])
            self.sibs.append(ex)
            return ex
        if point == "dn_chunk_bwd":
            self._pair_add(a, [o for ex in self.sibs for o in ex.outs])
            self.chip1 = _rs_to_chips([self.acc[n] for n in a[:2]])
            return self.chip1
        if point == "attn_bwd":
            self.chip2 = _rs_to_chips([self.acc[n] for n in a[2:]])
            return self.chip2
        if point == "mla_prep_bwd":
            ress = self._chip_add(a[:2], self.chip1) + self._chip_add(a[2:], self.chip2)
            self.share = _rs_share(ress)
            return self.share
        if point == "in_dx":
            self._take_shared(a, self.share)
            self.sib = _rs_to_sibling([self.g[n] for n in b])
            return self.sib
        if point == "attn_norm_bwd":
            self._pair_add(b, self.sib.outs)
            self.chip = _rs_to_chips([self.acc[n] for n in b])
            return self.chip
        return None

    def last_share(self):
        self.share = _rs_share(self._chip_add(self.GROUP_B, self.chip))
        return self.share

    def finish(self):
        self._take_shared(self.GROUP_B, self.share)
        return self.reduced


def _strip_to_shard(name, strip):
    shape, axis = {n: (s, ax) for n, s, ax, _ in BIG}[name]
    rows = shape[axis]
    return strip[:rows].T if axis == 1 else strip[:rows]


def kernel(x, meta_tokens, attn_norm_w, w_in, q_a_norm_w, w_q_b, kv_a_norm_w, w_kv_b, q_norm_w, k_norm_w, mla_out_norm_w, dn_conv_w, dn_A_log, dn_dt_bias, dn_out_norm_w, w_out, ffn_norm_w, w_gate, w_up, ffn_conv_w, ffn_conv_b, w_down, loss_target, m_meta_tokens, m_attn_norm_w, m_w_in, m_q_a_norm_w, m_w_q_b, m_kv_a_norm_w, m_w_kv_b, m_q_norm_w, m_k_norm_w, m_mla_out_norm_w, m_dn_conv_w, m_dn_A_log, m_dn_dt_bias, m_dn_out_norm_w, m_w_out, m_ffn_norm_w, m_w_gate, m_w_up, m_ffn_conv_w, m_ffn_conv_b, m_w_down, v_meta_tokens, v_attn_norm_w, v_w_in, v_q_a_norm_w, v_w_q_b, v_kv_a_norm_w, v_w_kv_b, v_q_norm_w, v_k_norm_w, v_mla_out_norm_w, v_dn_conv_w, v_dn_A_log, v_dn_dt_bias, v_dn_out_norm_w, v_w_out, v_ffn_norm_w, v_w_gate, v_w_up, v_ffn_conv_w, v_ffn_conv_b, v_w_down):
    local = dict(locals())
    w = {n: local[n] for n in WEIGHTS}
    m = {n: local["m_" + n] for n in WEIGHTS}
    v = {n: local["v_" + n] for n in WEIGHTS}
    p = 2 * lax.axis_index("x") + lax.axis_index("y")

    plan = _MeshPlan(w)
    wf = _pad_rows(jnp.concatenate([w[n].reshape(-1) for n, _, _ in SMALL_SHARDED]), SMALL_ROWS)
    gf = plan.gather_small(wf).reshape(N_CHIPS, -1)
    full = {}
    off = 0
    for n, s, ax in SMALL_SHARDED:
        full[n] = _unshard(gf[:, off:off + s[0] * s[1]], s, ax)
        off += s[0] * s[1]
    for n, _ in REPLICATED:
        full[n] = w[n]
    full["ffn_conv_w"] = _ff_to_pad(full["ffn_conv_w"], 1)
    full["ffn_conv_b"] = _ff_to_pad(full["ffn_conv_b"], 1)

    sq, grad_x, g = _local_step(x[0], loss_target[0], full, plan)
    g["ffn_conv_w"] = _ff_from_pad(g["ffn_conv_w"], 1)
    g["ffn_conv_b"] = _ff_from_pad(g["ffn_conv_b"], 1)

    small_all = [n for n, _, _ in SMALL_SHARDED] + [n for n, _ in REPLICATED]
    vec = jnp.concatenate([g[n].reshape(-1) for n in small_all] + [jnp.reshape(0.5 / D_MODEL * jnp.sum(sq), (1,))])
    vec = _pad_rows(vec, -(-vec.shape[0] // (8 * LANES)) * 8)
    a2a = _all_to_all_devices(vec)

    gs, delta, new_m, new_v = {}, {}, {}, {}
    big = {n: (s, ax) for n, s, ax, _ in BIG}

    def adamw_big(n, strips, host=None):
        s, ax = big[n]
        flip = ax == 1 and s[1] % 8 == 0
        there = (lambda a: a.reshape(s).T) if flip else (lambda a: a.reshape(s))
        back = (lambda a: a.T.reshape(w[n].shape)) if flip else (lambda a: a.reshape(w[n].shape))
        strip = strips[n] if flip or ax == 0 else strips[n][:s[1]].T
        g2, d2, m2, v2 = _adamw_call("adamw_" + n, there(w[n]), strip, there(m[n]), there(v[n]), host=host)
        gs[n], delta[n], new_m[n], new_v[n] = back(g2), back(d2), back(m2), back(v2)

    adamw_big("w_down", plan.reduced, host=a2a)
    adamw_big("w_gate", plan.reduced, host=plan.last_share())
    adamw_big("w_up", plan.reduced)
    adamw_big("w_out", plan.reduced)
    strips = plan.finish()
    for n in plan.GROUP_B:
        adamw_big(n, strips)
    me = 4 * lax.axis_index("x") + 2 * lax.axis_index("y") + lax.axis_index("c")
    red = _sum_devices(lax.dynamic_update_slice(a2a.outs[0], vec[None], (me, 0, 0))).reshape(-1)
    off = 0
    for n in small_all:
        tot = red[off:off + g[n].size].reshape(g[n].shape)
        off += g[n].size
        shard = {sn: (s, ax) for sn, s, ax in SMALL_SHARDED}.get(n)
        if shard is not None:
            tot = lax.dynamic_slice_in_dim(tot, p * shard[0][1], shard[0][1], axis=1)
        gs[n] = tot
    loss = red[off]
    two_d = lambda a: a.reshape(a.shape[-2], a.shape[-1])
    outs = _adamw_small([two_d(w[n]) for n in small_all], [two_d(gs[n]) for n in small_all],
                        [two_d(m[n]) for n in small_all], [two_d(v[n]) for n in small_all])
    for i, n in enumerate(small_all):
        for dst, src in ((delta, outs[0]), (new_m, outs[1]), (new_v, outs[2])):
            dst[n] = src[i].reshape(w[n].shape)

    grad_out = [gs[n].reshape(w[n].shape) for n in WEIGHTS]
    return (loss, grad_x[None], *grad_out, *[delta[n] for n in WEIGHTS], *[new_m[n] for n in WEIGHTS],
            *[new_v[n] for n in WEIGHTS])
```

```python
import functools
import math

import jax
import jax.numpy as jnp
import numpy as np
from jax import lax
from jax.experimental import pallas as pl
from jax.experimental.pallas import tpu as pltpu

F32 = jnp.float32
BF16 = jnp.bfloat16
HI = lax.Precision.HIGHEST
MESH = pl.DeviceIdType.MESH

N_META = 16
D_MODEL = 1024
HEADS = 4
HEAD = 128
ROPE = 64
QK_DIM = HEAD + ROPE
QK_PAD = 2 * HEAD
LORA = 256
DN_WIDTH = HEADS * HEAD
CHUNK = 64
D_FF = 2816
N_CHIPS = 4
FF_SHARD = D_FF // N_CHIPS
FF_BLOCK = 768
D_FF_P = N_CHIPS * FF_BLOCK
IN_COLS = 2632
IN_SHARD = IN_COLS // N_CHIPS
IN_SHARD_P = 672
IN_PAD = 2816
NORM_EPS = 1e-6
ROPE_THETA = 10000.0
LANES = 512

ADAM_LR, ADAM_B1, ADAM_B2, ADAM_EPS, ADAM_WD, ADAM_STEP = 0.001, 0.9, 0.999, 1e-08, 0.01, 10

VMEM_LIMIT = 56 * 1024 * 1024

BIG = (("w_in", (1024, 658), 1, IN_SHARD_P), ("w_q_b", (256, 192), 1, 192), ("w_kv_b", (256, 256), 1, 256),
       ("w_out", (256, 1024), 0, 256), ("w_gate", (1024, 704), 1, FF_BLOCK), ("w_up", (1024, 704), 1, FF_BLOCK),
       ("w_down", (704, 1024), 0, FF_BLOCK))
SMALL_SHARDED = (("meta_tokens", (16, 256), 1), ("dn_conv_w", (4, 384), 1), ("ffn_conv_w", (3, 704), 1))
REPLICATED = (("attn_norm_w", 1024), ("q_a_norm_w", 256), ("kv_a_norm_w", 256), ("q_norm_w", 192), ("k_norm_w", 192),
              ("mla_out_norm_w", 128), ("dn_A_log", 4), ("dn_dt_bias", 4), ("dn_out_norm_w", 128), ("ffn_norm_w", 1024),
              ("ffn_conv_b", 2816))
WEIGHTS = ("meta_tokens", "attn_norm_w", "w_in", "q_a_norm_w", "w_q_b", "kv_a_norm_w", "w_kv_b", "q_norm_w", "k_norm_w",
           "mla_out_norm_w", "dn_conv_w", "dn_A_log", "dn_dt_bias", "dn_out_norm_w", "w_out", "ffn_norm_w", "w_gate",
           "w_up", "ffn_conv_w", "ffn_conv_b", "w_down")

SMALL_ROWS = 16
REP_ROWS = 16


def _cparams(sem):
    return pltpu.CompilerParams(dimension_semantics=sem, vmem_limit_bytes=VMEM_LIMIT)


class _Exchange:
    def __init__(self, prog, ins, out_shape, nsem, peers=None, cid=None):
        self.prog, self.ins, self.out_shape, self.nsem = prog, list(ins), list(out_shape), nsem
        self.peers, self.cid = peers, cid
        self.outs = None

    def sems(self):
        return [pltpu.SemaphoreType.DMA((self.nsem,)), pltpu.SemaphoreType.DMA((self.nsem,))]

    def programs(self, in_refs, out_refs, send_sems, recv_sems):
        start, finish = self.prog(in_refs, out_refs, send_sems, recv_sems)
        if self.cid is None:
            return start, finish
        peers = self.peers()

        def shake_and_start():
            barrier = pltpu.get_barrier_semaphore()
            for peer in peers:
                pl.semaphore_signal(barrier, inc=1, device_id=peer, device_id_type=MESH)
            pl.semaphore_wait(barrier, len(peers))
            start()

        return shake_and_start, finish

    def cparams(self, **kw):
        return pltpu.CompilerParams(has_side_effects=True, collective_id=self.cid, **kw)

    def run(self, name):
        any_spec = pl.BlockSpec(memory_space=pl.ANY)
        n = len(self.ins)

        def body(*refs):
            start, finish = self.programs(refs[:n], refs[n:-2], refs[-2], refs[-1])
            start()
            finish()

        self.outs = pl.pallas_call(
            body, name=name, in_specs=[any_spec] * n, out_specs=[any_spec] * len(self.out_shape),
            out_shape=self.out_shape, scratch_shapes=self.sems(), compiler_params=self.cparams())(*self.ins)
        return self.outs


def _pcall(body, name, grid, in_specs, out_specs, out_shape, args, sem, scratch_shapes=(), host=None):
    single = not isinstance(out_shape, (list, tuple))
    out_specs, out_shape = ([out_specs], [out_shape]) if single else (list(out_specs), list(out_shape))
    if host is None:
        outs = pl.pallas_call(body, name=name, grid=grid, in_specs=list(in_specs), out_specs=out_specs, out_shape=out_shape,
                              scratch_shapes=list(scratch_shapes), compiler_params=_cparams(sem))(*args)
        return outs[0] if single else outs
    any_spec = pl.BlockSpec(memory_space=pl.ANY)
    n_in, n_out, n_scr, nx_in, nx_out = len(in_specs), len(out_specs), len(scratch_shapes), len(host.ins), len(host.out_shape)

    def hosted(*refs):
        c_in, x_in = refs[:n_in], refs[n_in:n_in + nx_in]
        o0 = n_in + nx_in
        c_out, x_out = refs[o0:o0 + n_out], refs[o0 + n_out:o0 + n_out + nx_out]
        s0 = o0 + n_out + nx_out
        start, finish = host.programs(x_in, x_out, refs[s0 + n_scr], refs[s0 + n_scr + 1])
        first = functools.reduce(jnp.logical_and, [pl.program_id(d) == 0 for d in range(len(grid))])
        last = functools.reduce(jnp.logical_and, [pl.program_id(d) == grid[d] - 1 for d in range(len(grid))])
        pl.when(first)(start)
        body(*c_in, *c_out, *refs[s0:s0 + n_scr])
        pl.when(last)(finish)

    outs = pl.pallas_call(
        hosted, name=name, grid=grid, in_specs=list(in_specs) + [any_spec] * nx_in,
        out_specs=out_specs + [any_spec] * nx_out, out_shape=out_shape + host.out_shape,
        scratch_shapes=list(scratch_shapes) + host.sems(),
        compiler_params=host.cparams(dimension_semantics=sem, vmem_limit_bytes=VMEM_LIMIT))(*args, *host.ins)
    host.outs = outs[n_out:]
    return outs[0] if single else outs[:n_out]


NN, NT, TN = ((1,), (0,)), ((1,), (1,)), ((0,), (0,))


def _shift_dims(dims, batch):
    if not batch:
        return (dims, ((), ()))
    return (((dims[0][0] + 1,), (dims[1][0] + 1,)), ((0,), (0,)))


def _make_mm(dims, exact, batch=False):
    def raw(a, b, d):
        dn = _shift_dims(d, batch)
        if exact == "split_lhs":
            ah, bh = a.astype(BF16), b.astype(BF16)
            al = (a - ah.astype(F32)).astype(BF16)
            return lax.dot_general(ah, bh, dn, preferred_element_type=F32) + lax.dot_general(al, bh, dn,
                                                                                              preferred_element_type=F32)
        if exact == "split":
            ah, bh = a.astype(BF16), b.astype(BF16)
            al, bl = (a - ah.astype(F32)).astype(BF16), (b - bh.astype(F32)).astype(BF16)
            dot = lambda p, q: lax.dot_general(p, q, dn, preferred_element_type=F32)
            return dot(ah, bh) + (dot(ah, bl) + dot(al, bh))
        if exact:
            return lax.dot_general(a.astype(F32), b.astype(F32), dn, precision=HI, preferred_element_type=F32)
        return lax.dot_general(a.astype(BF16), b.astype(BF16), dn, preferred_element_type=F32)

    @jax.custom_vjp
    def mm(a, b):
        return raw(a, b, dims)

    def fwd(a, b):
        return raw(a, b, dims), (a, b)

    def bwd(res, g):
        a, b = res
        if dims == NN:
            da, db = raw(g, b, NT), raw(a, g, TN)
        elif dims == NT:
            da, db = raw(g, b, NN), raw(g, a, TN)
        else:
            da, db = raw(b, g, NT), raw(a, g, NN)
        return da.astype(a.dtype), db.astype(b.dtype)

    mm.defvjp(fwd, bwd)
    return mm


_mm = _make_mm(NN, False)
_mm_nt = _make_mm(NT, False)
_mm_tn = _make_mm(TN, False)
_mmx = _make_mm(NN, "split_lhs")
_bmm = _make_mm(NN, False, batch=True)
_bmm_nt = _make_mm(NT, False, batch=True)
_bmm_tn = _make_mm(TN, False, batch=True)
_bmmx = _make_mm(NN, True, batch=True)
_bmms = _make_mm(NN, "split", batch=True)
_bmms_nt = _make_mm(NT, "split", batch=True)
_bmms_tn = _make_mm(TN, "split", batch=True)


@jax.custom_vjp
def _unit_lower_inv(a):
    n = a.shape[-1]
    eye = (lax.broadcasted_iota(jnp.int32, a.shape, 1) == lax.broadcasted_iota(jnp.int32, a.shape, 2)).astype(F32)
    x = -a
    t = eye + x
    for _ in range(max(n.bit_length() - 2, 0)):
        x = _bmms(x, x)
        t = t + _bmms(t, x)
    return t


def _unit_lower_inv_fwd(a):
    t = _unit_lower_inv(a)
    return t, t


def _unit_lower_inv_bwd(t, g):
    return (-_bmms_tn(t, _bmms_nt(g, t)),)


_unit_lower_inv.defvjp(_unit_lower_inv_fwd, _unit_lower_inv_bwd)


def _scan_chunk_rows(x, reverse):
    nb, c, w = x.shape
    y = x.reshape(nb * c, w)
    pos = lax.broadcasted_iota(jnp.int32, y.shape, 0) % c
    step = 1
    while step < c:
        if reverse:
            y = y + jnp.where(pos < c - step, pltpu.roll(y, nb * c - step, 0), 0.0)
        else:
            y = y + jnp.where(pos >= step, pltpu.roll(y, step, 0), 0.0)
        step *= 2
    return y.reshape(nb, c, w)


@jax.custom_vjp
def _chunk_cumsum(x):
    return _scan_chunk_rows(x, False)


_chunk_cumsum.defvjp(lambda x: (_scan_chunk_rows(x, False), None), lambda _, g: (_scan_chunk_rows(g, True),))


def _rms(x, w, n):
    ms = jnp.sum(x * x, axis=-1, keepdims=True) * (1.0 / n)
    return x * lax.rsqrt(ms + NORM_EPS) * w


def _silu(x):
    return x * jax.nn.sigmoid(x)


def _softplus(x):
    return jnp.maximum(x, 0.0) + jnp.log(1.0 + jnp.exp(-jnp.abs(x)))


def _rope(x, cos, sin, perm):
    return x * cos + _mmx(x, perm) * sin


def _mla_prep_fn(rows, consts):
    q_lat, kv_lat, k_pe, cos, sin = rows
    qn = _rms(q_lat, consts["qa_w"], LORA)
    kvn = _rms(kv_lat, consts["kva_w"], LORA)
    outs = []
    for h in range(HEADS):
        q_n = _mm_nt(qn, consts["wq_n"][h])
        q_r = _mm_nt(qn, consts["wq_r"][h])
        rs = lax.rsqrt((jnp.sum(q_n * q_n, -1, keepdims=True) + jnp.sum(q_r * q_r, -1, keepdims=True)) * (1.0 / QK_DIM)
                       + NORM_EPS)
        q_n = q_n * rs * consts["qn_n"]
        q_r = _rope(q_r * rs * consts["qn_r"], cos, sin, consts["perm"])
        k_n = _mm_nt(kvn, consts["wk_n"][h])
        v = _mm_nt(kvn, consts["wv"][h])
        rk = lax.rsqrt((jnp.sum(k_n * k_n, -1, keepdims=True) + jnp.sum(k_pe * k_pe, -1, keepdims=True)) * (1.0 / QK_DIM)
                       + NORM_EPS)
        k_n = k_n * rk * consts["kn_n"]
        k_r = _rope(k_pe * rk * consts["kn_r"], cos, sin, consts["perm"])
        outs += [q_n, q_r, k_n, k_r, v]
    return tuple(outs)


def _attn_fn(q, k, v, row0):
    s = _mm_nt(q, k) * (1.0 / math.sqrt(QK_DIM))
    qpos = row0 + lax.broadcasted_iota(jnp.int32, s.shape, 0)
    kpos = lax.broadcasted_iota(jnp.int32, s.shape, 1)
    s = jnp.where(kpos <= qpos, s, -1e30)
    m = lax.stop_gradient(jnp.max(s, axis=-1, keepdims=True))
    p = jnp.exp(s - m)
    p = p / jnp.sum(p, axis=-1, keepdims=True)
    return _mm(p, v)


def _dn_prep_fn(rows, consts):
    qc, kc, ab = rows
    a_b = _mmx(ab, consts["sel_a"])
    b_b = _mmx(ab, consts["sel_b"])
    beta = jax.nn.sigmoid(b_b)
    g = -jnp.exp(consts["alog"]) * _softplus(a_b + consts["dtb"])
    qs, ks = [], []
    for h in range(HEADS):
        q, k = qc[h], kc[h]
        qs.append(q * lax.rsqrt(jnp.sum(q * q, -1, keepdims=True) + NORM_EPS))
        ks.append(k * lax.rsqrt(jnp.sum(k * k, -1, keepdims=True) + NORM_EPS))
    return tuple(qs), tuple(ks), g, beta


def _dn_chunk_fn(q, k, v, gb, g64, bb):
    nb = q.shape[0]
    ri = lax.broadcasted_iota(jnp.int32, (nb, CHUNK, CHUNK), 1)
    ci = lax.broadcasted_iota(jnp.int32, (nb, CHUNK, CHUNK), 2)
    tri = ri >= ci
    strict = ri > ci
    tril = tri.astype(F32)
    eye = (ri == ci).astype(F32)
    ones = jnp.ones((nb, CHUNK, CHUNK), F32)
    gc = _chunk_cumsum(gb)
    gc64 = _chunk_cumsum(g64)
    grow = _bmmx(ones, eye * gc64)
    diff = gc64 - grow
    decay = jnp.where(tri, jnp.exp(jnp.where(tri, diff, 0.0)), 0.0)
    kb = k * bb
    vb = v * bb
    a = jnp.where(strict, _bmm_nt(kb, k) * decay, 0.0)
    tinv = _unit_lower_inv(a)
    u = _bmm(tinv, vb)
    w = _bmm(tinv, kb * jnp.exp(gc))
    qs = q * (1.0 / math.sqrt(HEAD))
    qk = _bmm_nt(qs, k) * decay
    qg = qs * jnp.exp(gc)
    glast = jnp.sum(gb, axis=1, keepdims=True)
    kdec = k * jnp.exp(glast - gc)
    n_mat = _bmm_tn(kdec, w)
    b_mat = _bmm_tn(kdec, u)
    q_eff = qg - _bmm(qk, w)
    o_own = _bmm(qk, u)
    return n_mat, b_mat, q_eff, o_own, jnp.exp(glast)


def _dn_rec_fn(s, n_mat, b_mat, eg):
    return s * eg - _mm(n_mat, s) + b_mat


def _dn_o_fn(s, q_eff, o_own):
    return _bmm(q_eff, s) + o_own


def _dn_out_fn(o, z, w):
    return _rms(o, w, HEAD) * _silu(z)


def _row_tile(t, parts=8):
    return t // parts if (t // parts) % 16 == 0 else t


def _tile(n, pref, unit):
    best = n
    for cand in range(unit, min(n, pref) + 1, unit):
        if n % cand == 0:
            best = cand
    return best if best <= pref else n


def _rows_call(name, body, rows, consts, outs, accs, r, host=None):
    rows = [a if isinstance(a, tuple) else (a, a.shape[1], 0) for a in rows]
    t = rows[0][0].shape[0]
    zero = lambda nd: (lambda i: (0,) * nd)
    in_specs = [pl.BlockSpec((r, w), functools.partial(lambda i, b: (i, b), b=blk)) for _, w, blk in rows]
    rows = [a for a, _, _ in rows]
    in_specs += [pl.BlockSpec(a.shape, zero(a.ndim)) for a in consts]
    out_shape = [jax.ShapeDtypeStruct((t, w), dt) for w, dt in outs] + [jax.ShapeDtypeStruct(s, F32) for s in accs]
    out_specs = [pl.BlockSpec((r, w), lambda i: (i, 0)) for w, _ in outs] + [pl.BlockSpec(s, zero(len(s))) for s in accs]
    return _pcall(body, name, (t // r,), in_specs, out_specs, out_shape, [*rows, *consts], ("arbitrary",), host=host)


def _accumulate(ref, val):
    @pl.when(pl.program_id(0) == 0)
    def _():
        ref[...] = jnp.zeros(ref.shape, ref.dtype)

    ref[...] += val


def _matmul(name, a, b, dims, out_dtype, res=None, host=None):
    if dims == "nn":
        (m, k), n = a.shape, b.shape[1]
    elif dims == "nt":
        (m, k), n = a.shape, b.shape[0]
    else:
        (k, m), n = a.shape, b.shape[1]
    tm = _tile(m, 1100, 16) if dims != "tn" else _tile(m, 640, 128)
    tn = _tile(n, 1408, 128)
    if dims == "nn":
        a_spec, b_spec, dn = pl.BlockSpec((tm, k), lambda i, j: (i, 0)), pl.BlockSpec((k, tn), lambda i, j: (0, j)), NN
    elif dims == "nt":
        a_spec, b_spec, dn = pl.BlockSpec((tm, k), lambda i, j: (i, 0)), pl.BlockSpec((tn, k), lambda i, j: (j, 0)), NT
    else:
        a_spec, b_spec, dn = pl.BlockSpec((k, tm), lambda i, j: (0, i)), pl.BlockSpec((k, tn), lambda i, j: (0, j)), TN
    o_spec = pl.BlockSpec((tm, tn), lambda i, j: (i, j))

    def body(*refs):
        a_ref, b_ref, o_ref = refs[0], refs[1], refs[-1]
        acc = lax.dot_general(a_ref[...].astype(BF16), b_ref[...].astype(BF16), (dn, ((), ())),
                              preferred_element_type=F32)
        if res is not None:
            acc = acc + refs[2][...]
        o_ref[...] = acc.astype(out_dtype)

    ins = [a, b] + ([res] if res is not None else [])
    specs = [a_spec, b_spec] + ([o_spec] if res is not None else [])
    return _pcall(body, name, (m // tm, n // tn), specs, o_spec, jax.ShapeDtypeStruct((m, n), out_dtype), ins,
                  ("arbitrary", "arbitrary"), host=host)


def _rms_fwd(name, h, w, host=None):
    n = h.shape[1]

    def body(h_ref, w_ref, o_ref):
        o_ref[...] = _rms(h_ref[...], w_ref[...], n).astype(BF16)

    return _rows_call(name, body, [h], [w], [(n, BF16)], [], _row_tile(h.shape[0]), host=host)[0]


def _rms_bwd(name, h, w, cts, resid, host=None):
    n = h.shape[1]
    nct = len(cts)

    def body(*refs):
        h_ref, ct_refs, r_ref, w_ref = refs[0], refs[1:1 + nct], refs[1 + nct], refs[2 + nct]
        dh_ref, dh16_ref, dw_ref = refs[-3], refs[-2], refs[-1]
        ct = ct_refs[0][...].astype(F32)
        for c in ct_refs[1:]:
            ct = ct + c[...].astype(F32)
        _, vjp = jax.vjp(lambda x, ww: _rms(x, ww, n), h_ref[...], w_ref[...])
        dh, dw = vjp(ct)
        dh = dh + r_ref[...]
        dh_ref[...] = dh
        dh16_ref[...] = dh.astype(BF16)
        _accumulate(dw_ref, dw)

    return _rows_call(name, body, [h, *cts, resid], [w], [(n, F32), (n, BF16)], [(1, n)], _row_tile(h.shape[0]), host=host)


def _mla_consts_from_refs(qa, wq, kva, wkv, qn, kn, perm):
    f = lambda r: r[...].astype(F32)
    return dict(
        qa_w=f(qa), kva_w=f(kva), perm=f(perm),
        wq_n=[wq[h * QK_PAD:h * QK_PAD + HEAD, :].astype(F32) for h in range(HEADS)],
        wq_r=[wq[h * QK_PAD + HEAD:(h + 1) * QK_PAD, :].astype(F32) for h in range(HEADS)],
        wk_n=[wkv[h * QK_PAD:h * QK_PAD + HEAD, :].astype(F32) for h in range(HEADS)],
        wv=[wkv[h * QK_PAD + HEAD:(h + 1) * QK_PAD, :].astype(F32) for h in range(HEADS)],
        qn_n=qn[:, 0:HEAD], qn_r=qn[:, HEAD:QK_PAD], kn_n=kn[:, 0:HEAD], kn_r=kn[:, HEAD:QK_PAD])


def _mla_prep_fwd(q_lat, kv_lat, k_pe, cos, sin, qa, wq, kva, wkv, qn, kn, perm):
    def body(ql, kvl, kp, c, s, qa_r, wq_r, kva_r, wkv_r, qn_r, kn_r, p_r, q_out, k_out, v_out):
        consts = _mla_consts_from_refs(qa_r, wq_r, kva_r, wkv_r, qn_r, kn_r, p_r)
        outs = _mla_prep_fn((ql[...], kvl[...], kp[...], c[...], s[...]), consts)
        for h in range(HEADS):
            q_n, q_r, k_n, k_r, v = outs[5 * h:5 * h + 5]
            q_out[:, h * QK_PAD:h * QK_PAD + HEAD] = q_n.astype(BF16)
            q_out[:, h * QK_PAD + HEAD:(h + 1) * QK_PAD] = q_r.astype(BF16)
            k_out[:, h * QK_PAD:h * QK_PAD + HEAD] = k_n.astype(BF16)
            k_out[:, h * QK_PAD + HEAD:(h + 1) * QK_PAD] = k_r.astype(BF16)
            v_out[:, h * HEAD:(h + 1) * HEAD] = v.astype(BF16)

    return _rows_call("mla_prep_fwd", body, [q_lat, kv_lat, k_pe, cos, sin], [qa, wq, kva, wkv, qn, kn, perm],
                      [(HEADS * QK_PAD, BF16), (HEADS * QK_PAD, BF16), (DN_WIDTH, BF16)], [], _row_tile(cos.shape[0], 4))


def _mla_prep_bwd(q_lat, kv_lat, k_pe, cos, sin, dq, dk, dv, qa, wq, kva, wkv, qn, kn, perm, host=None):
    def body(ql, kvl, kp, c, s, dq_r, dk_r, dv_r, qa_r, wq_r, kva_r, wkv_r, qn_r, kn_r, p_r,
             dql, dkvl, dkp, dqa, dwq, dkva, dwkv, dqn, dkn):
        consts = _mla_consts_from_refs(qa_r, wq_r, kva_r, wkv_r, qn_r, kn_r, p_r)
        cc, ss, pm = c[...], s[...], consts.pop("perm")
        _, vjp = jax.vjp(lambda rows, cs: _mla_prep_fn((*rows, cc, ss), dict(cs, perm=pm)), (ql[...], kvl[...], kp[...]),
                         consts)
        cts = []
        for h in range(HEADS):
            cts += [dq_r[:, h * QK_PAD:h * QK_PAD + HEAD], dq_r[:, h * QK_PAD + HEAD:(h + 1) * QK_PAD],
                    dk_r[:, h * QK_PAD:h * QK_PAD + HEAD], dk_r[:, h * QK_PAD + HEAD:(h + 1) * QK_PAD],
                    dv_r[:, h * HEAD:(h + 1) * HEAD]]
        (d_ql, d_kvl, d_kp), dc = vjp(tuple(cts))
        dql[...] = d_ql.astype(BF16)
        dkvl[...] = d_kvl.astype(BF16)
        dkp[...] = d_kp.astype(BF16)
        first = pl.program_id(0) == 0

        def acc(ref, sl, val):
            @pl.when(first)
            def _():
                ref[sl] = val

            @pl.when(jnp.logical_not(first))
            def _():
                ref[sl] += val

        full = (slice(None), slice(None))
        acc(dqa, full, dc["qa_w"])
        acc(dkva, full, dc["kva_w"])
        for h in range(HEADS):
            acc(dwq, (slice(h * QK_PAD, h * QK_PAD + HEAD), slice(None)), dc["wq_n"][h])
            acc(dwq, (slice(h * QK_PAD + HEAD, (h + 1) * QK_PAD), slice(None)), dc["wq_r"][h])
            acc(dwkv, (slice(h * QK_PAD, h * QK_PAD + HEAD), slice(None)), dc["wk_n"][h])
            acc(dwkv, (slice(h * QK_PAD + HEAD, (h + 1) * QK_PAD), slice(None)), dc["wv"][h])
        acc(dqn, (slice(None), slice(0, HEAD)), dc["qn_n"])
        acc(dqn, (slice(None), slice(HEAD, QK_PAD)), dc["qn_r"])
        acc(dkn, (slice(None), slice(0, HEAD)), dc["kn_n"])
        acc(dkn, (slice(None), slice(HEAD, QK_PAD)), dc["kn_r"])

    return _rows_call("mla_prep_bwd", body, [q_lat, kv_lat, k_pe, cos, sin, dq, dk, dv],
                      [qa, wq, kva, wkv, qn, kn, perm],
                      [(LORA, BF16), (LORA, BF16), (HEAD, BF16)],
                      [(1, LORA), wq.shape, (1, LORA), wkv.shape, (1, QK_PAD), (1, QK_PAD)], _row_tile(cos.shape[0], 4),
                      host=host)


ATTN_Q_ROWS = 256


def _attn_blocks(t):
    return [(r0, min(ATTN_Q_ROWS, t - r0)) for r0 in range(0, t, ATTN_Q_ROWS)]


def _attn_fwd(q, k, v, host=None):
    t = q.shape[0]

    def body(q_ref, k_ref, v_ref, o_ref):
        for r0, rows in _attn_blocks(t):
            ext = r0 + rows
            o_ref[r0:ext, :] = _attn_fn(q_ref[r0:ext, :], k_ref[0:ext, :], v_ref[0:ext, :], r0)

    qk_spec = pl.BlockSpec((t, QK_PAD), lambda h: (0, h))
    v_spec = pl.BlockSpec((t, HEAD), lambda h: (0, h))
    return _pcall(body, "attn_fwd", (HEADS,), [qk_spec, qk_spec, v_spec], v_spec,
                  jax.ShapeDtypeStruct((t, HEADS * HEAD), F32), [q, k, v], ("arbitrary",), host=host)


def _attn_bwd(q, k, v, do, host=None):
    t = q.shape[0]

    def body(q_ref, k_ref, v_ref, do_ref, dq_ref, dk_ref, dv_ref):
        dk_ref[...] = jnp.zeros(dk_ref.shape, F32)
        dv_ref[...] = jnp.zeros(dv_ref.shape, F32)
        for r0, rows in _attn_blocks(t):
            ext = r0 + rows
            _, vjp = jax.vjp(functools.partial(_attn_fn, row0=r0), q_ref[r0:ext, :].astype(F32),
                             k_ref[0:ext, :].astype(F32), v_ref[0:ext, :].astype(F32))
            dq, dk, dv = vjp(do_ref[r0:ext, :])
            dq_ref[r0:ext, :] = dq
            dk_ref[0:ext, :] += dk
            dv_ref[0:ext, :] += dv

    qk_spec = pl.BlockSpec((t, QK_PAD), lambda h: (0, h))
    v_spec = pl.BlockSpec((t, HEAD), lambda h: (0, h))
    return _pcall(body, "attn_bwd", (HEADS,), [qk_spec, qk_spec, v_spec, v_spec], [qk_spec, qk_spec, v_spec],
                  [jax.ShapeDtypeStruct((t, HEADS * QK_PAD), F32), jax.ShapeDtypeStruct((t, HEADS * QK_PAD), F32),
                   jax.ShapeDtypeStruct((t, HEADS * HEAD), F32)], [q, k, v, do], ("arbitrary",), host=host)


def _mix_out_proj(o_mla, o_dn, z, w_mla, w_dn, w_out, h0, w_ffn):
    def body(om_ref, od_ref, z_ref, h0_ref, wm_ref, wd_ref, wo_ref, wf_ref, mixed_ref, h1_ref, n2_ref):
        for h in range(HEADS):
            sl = slice(h * HEAD, (h + 1) * HEAD)
            mixed_ref[:, sl] = _rms(om_ref[:, sl], wm_ref[...], HEAD).astype(BF16)
            mixed_ref[:, DN_WIDTH + h * HEAD:DN_WIDTH + (h + 1) * HEAD] = _dn_out_fn(od_ref[:, sl], z_ref[:, sl],
                                                                                     wd_ref[...]).astype(BF16)
        h1 = _mm(mixed_ref[...], wo_ref[...]) + h0_ref[...]
        h1_ref[...] = h1
        n2_ref[...] = _rms(h1, wf_ref[...], D_MODEL).astype(BF16)

    return _rows_call("mix_out_proj", body, [o_mla, o_dn, z, h0], [w_mla, w_dn, w_out, w_ffn],
                      [(D_MODEL, BF16), (D_MODEL, F32), (D_MODEL, BF16)], [], _row_tile(o_mla.shape[0], 4))


def _down_proj_loss(act, w_down, h1, tgt, n_valid):
    t, n = h1.shape
    r = _row_tile(t, 4)

    def body(a_ref, h_ref, t_ref, w_ref, dy_ref, dy16_ref, acc_ref):
        h2 = _mm(a_ref[...], w_ref[...]) + h_ref[...]
        rows = pl.program_id(0) * r + lax.broadcasted_iota(jnp.int32, (r, n), 0)
        valid = jnp.logical_and(rows >= N_META, rows < n_valid)
        e = jnp.where(valid, h2 - t_ref[...], 0.0)
        dy = e * (1.0 / n)
        dy_ref[...] = dy
        dy16_ref[...] = dy.astype(BF16)
        _accumulate(acc_ref, jnp.sum(e * e, axis=0, keepdims=True))

    return _rows_call("down_proj_loss", body, [act, h1, tgt], [w_down], [(n, F32), (n, BF16)], [(1, n)], r)


def _ffn_in_bwd(dgpre, dup, w_gate_t, w_up_t, h1, dy, w_ffn, host=None):
    n = h1.shape[1]

    def body(dg_ref, du_ref, h_ref, dy_ref, wg_ref, wu_ref, w_ref, dh_ref, dh16_ref, dw_ref):
        ct = _mm(dg_ref[...], wg_ref[...]) + _mm(du_ref[...], wu_ref[...])
        _, vjp = jax.vjp(lambda x, ww: _rms(x, ww, n), h_ref[...], w_ref[...])
        dh, dw = vjp(ct)
        dh = dh + dy_ref[...]
        dh_ref[...] = dh
        dh16_ref[...] = dh.astype(BF16)
        _accumulate(dw_ref, dw)

    return _rows_call("ffn_in_bwd", body, [dgpre, dup, h1, dy], [w_gate_t, w_up_t, w_ffn], [(n, F32), (n, BF16)], [(1, n)],
                      _row_tile(h1.shape[0]), host=host)


def _mix_out_bwd(o_mla, o_dn, z, dh1, w_out, w_mla, w_dn, host=None):
    def body(om_ref, od_ref, z_ref, dh_ref, wo_ref, wm_ref, wd_ref, dom_ref, dod_ref, dz_ref, dwm_ref, dwd_ref):
        dwm = dwd = None
        for h in range(HEADS):
            sl = slice(h * HEAD, (h + 1) * HEAD)
            _, vjp = jax.vjp(lambda o, w: _rms(o, w, HEAD), om_ref[:, sl], wm_ref[...])
            do, dw = vjp(_mm_nt(dh_ref[...], wo_ref[sl, :]))
            dom_ref[:, sl] = do
            dwm = dw if dwm is None else dwm + dw
            _, vjp = jax.vjp(_dn_out_fn, od_ref[:, sl], z_ref[:, sl], wd_ref[...])
            do, dz, dw = vjp(_mm_nt(dh_ref[...], wo_ref[DN_WIDTH + h * HEAD:DN_WIDTH + (h + 1) * HEAD, :]))
            dod_ref[:, sl] = do
            dz_ref[:, sl] = dz.astype(BF16)
            dwd = dw if dwd is None else dwd + dw
        _accumulate(dwm_ref, dwm)
        _accumulate(dwd_ref, dwd)

    return _rows_call("mix_out_bwd", body, [o_mla, o_dn, z, dh1], [w_out, w_mla, w_dn],
                      [(DN_WIDTH, F32), (DN_WIDTH, F32), (DN_WIDTH, BF16)], [(1, HEAD), (1, HEAD)],
                      _row_tile(o_mla.shape[0], 4), host=host)


def _shift_down(x, s):
    if s == 0:
        return x
    rows = lax.broadcasted_iota(jnp.int32, x.shape, 0)
    return jnp.where(rows >= s, pltpu.roll(x, s, 0), 0.0)


def _shift_up(x, s):
    if s == 0:
        return x
    t = x.shape[0]
    rows = lax.broadcasted_iota(jnp.int32, x.shape, 0)
    return jnp.where(rows < t - s, pltpu.roll(x, t - s, 0), 0.0)


def _col_call(name, body, cols, taps, outs, tap_outs, cw, host=None):
    t, c = cols[0].shape[0], taps[0].shape[1]
    in_specs = [pl.BlockSpec((t, cw), lambda j: (0, j)) for _ in cols]
    in_specs += [pl.BlockSpec((a.shape[0], cw), lambda j: (0, j)) for a in taps]
    out_shape = [jax.ShapeDtypeStruct((t, c), dt) for dt in outs] + [jax.ShapeDtypeStruct((n, c), F32) for n in tap_outs]
    out_specs = [pl.BlockSpec((t, cw), lambda j: (0, j)) for _ in outs]
    out_specs += [pl.BlockSpec((n, cw), lambda j: (0, j)) for n in tap_outs]
    return _pcall(body, name, (c // cw,), in_specs, out_specs, out_shape, [*cols, *taps], ("arbitrary",), host=host)


def _causal_conv(x, w_ref, width, zero_tail=False):
    down = (lambda a, s: pltpu.roll(a, s, 0)) if zero_tail else _shift_down
    acc = w_ref[width - 1:width, :] * x
    for j in range(width - 1):
        acc = acc + w_ref[j:j + 1, :] * down(x, width - 1 - j)
    return acc


def _causal_conv_bwd(x, dpre, w_ref, dx_ref, dw_ref, width, zero_tail=False):
    t = x.shape[0]
    down = (lambda a, s: pltpu.roll(a, s, 0)) if zero_tail else _shift_down
    up = (lambda a, s: pltpu.roll(a, t - s, 0)) if zero_tail else _shift_up
    dx = w_ref[width - 1:width, :] * dpre
    dw_ref[width - 1:width, :] = jnp.sum(dpre * x, axis=0, keepdims=True)
    for j in range(width - 1):
        s = width - 1 - j
        dx = dx + w_ref[j:j + 1, :] * up(dpre, s)
        dw_ref[j:j + 1, :] = jnp.sum(dpre * down(x, s), axis=0, keepdims=True)
    dx_ref[...] = dx.astype(dx_ref.dtype)


def _dsilu(x):
    sg = jax.nn.sigmoid(x)
    return sg * (1.0 + x * (1.0 - sg))


def _dn_conv_fwd(x, w):
    def body(x_ref, w_ref, y_ref):
        y_ref[...] = _silu(_causal_conv(x_ref[...], w_ref, 4, zero_tail=True))

    return _col_call("dn_conv_fwd", body, [x], [w], [F32], [], 256)[0]


def _dn_conv_bwd(x, w, dy):
    def body(x_ref, dy_ref, w_ref, dx_ref, dw_ref):
        xv = x_ref[...]
        dpre = dy_ref[...] * _dsilu(_causal_conv(xv, w_ref, 4, zero_tail=True))
        _causal_conv_bwd(xv, dpre, w_ref, dx_ref, dw_ref, 4, zero_tail=True)

    return _col_call("dn_conv_bwd", body, [x, dy], [w], [BF16], [4], 256)


def _ffn_glu_fwd(n2, w_gate_t, w_up_t, w, b, host=None):
    t, k = n2.shape
    c, cw = w_gate_t.shape[0], 256

    def body(n_ref, wg_ref, wu_ref, w_ref, b_ref, g_ref, u_ref, a_ref):
        nv = n_ref[...]
        g16 = _mm_nt(nv, wg_ref[...]).astype(BF16)
        u16 = _mm_nt(nv, wu_ref[...]).astype(BF16)
        g_ref[...] = g16
        u_ref[...] = u16
        gate = _causal_conv(g16.astype(F32), w_ref, 3) + b_ref[...]
        a_ref[...] = (_silu(gate) * u16.astype(F32)).astype(BF16)

    wspec = pl.BlockSpec((cw, k), lambda j: (j, 0))
    col = pl.BlockSpec((t, cw), lambda j: (0, j))
    in_specs = [pl.BlockSpec((t, k), lambda j: (0, 0)), wspec, wspec, pl.BlockSpec((w.shape[0], cw), lambda j: (0, j)),
                pl.BlockSpec((1, cw), lambda j: (0, j))]
    return _pcall(body, "ffn_glu_fwd", (c // cw,), in_specs, [col] * 3, [jax.ShapeDtypeStruct((t, c), BF16)] * 3,
                  [n2, w_gate_t, w_up_t, w, b], ("arbitrary",), host=host)


def _ffn_glu_bwd(gpre, up, dy16, w_down, w, b):
    t, k = dy16.shape
    c, cw = w_down.shape[0], 256

    def body(g_ref, u_ref, dy_ref, wd_ref, w_ref, b_ref, dg_ref, du_ref, dw_ref, db_ref):
        gv = g_ref[...].astype(F32)
        gate = _causal_conv(gv, w_ref, 3) + b_ref[...]
        da = _mm_nt(dy_ref[...], wd_ref[...])
        sg = jax.nn.sigmoid(gate)
        du_ref[...] = (da * (gate * sg)).astype(BF16)
        dgate = da * u_ref[...].astype(F32) * (sg * (1.0 + gate * (1.0 - sg)))
        db_ref[...] = jnp.sum(dgate, axis=0, keepdims=True)
        _causal_conv_bwd(gv, dgate, w_ref, dg_ref, dw_ref, 3)

    col = pl.BlockSpec((t, cw), lambda j: (0, j))
    taps = lambda rows: pl.BlockSpec((rows, cw), lambda j: (0, j))
    in_specs = [col, col, pl.BlockSpec((t, k), lambda j: (0, 0)), pl.BlockSpec((cw, k), lambda j: (j, 0)),
                taps(w.shape[0]), taps(1)]
    return _pcall(body, "ffn_glu_bwd", (c // cw,), in_specs, [col, col, taps(w.shape[0]), taps(1)],
                  [jax.ShapeDtypeStruct((t, c), BF16)] * 2 + [jax.ShapeDtypeStruct((w.shape[0], c), F32),
                                                             jax.ShapeDtypeStruct((1, c), F32)],
                  [gpre, up, dy16, w_down, w, b], ("arbitrary",))


def _dn_prep_consts(sa, sb, al, dt):
    return dict(sel_a=sa[...], sel_b=sb[...], alog=al[...], dtb=dt[...])


def _dn_prep_fwd(conv, ab, sel_a, sel_b, alog, dtb):
    def body(c_ref, ab_ref, sa, sb, al, dt, q_out, k_out, g_out, b_out):
        qc = tuple(c_ref[:, h * HEAD:(h + 1) * HEAD] for h in range(HEADS))
        kc = tuple(c_ref[:, DN_WIDTH + h * HEAD:DN_WIDTH + (h + 1) * HEAD] for h in range(HEADS))
        qs, ks, g, beta = _dn_prep_fn((qc, kc, ab_ref[...]), _dn_prep_consts(sa, sb, al, dt))
        for h in range(HEADS):
            q_out[:, h * HEAD:(h + 1) * HEAD] = qs[h]
            k_out[:, h * HEAD:(h + 1) * HEAD] = ks[h]
        g_out[...] = g
        b_out[...] = beta

    return _rows_call("dn_prep_fwd", body, [conv, ab], [sel_a, sel_b, alog, dtb], [(DN_WIDTH, F32)] * 4, [],
                      _row_tile(conv.shape[0]))


def _dn_prep_bwd(conv, ab, dq, dk, dv, dg, db, sel_a, sel_b, alog, dtb):
    def body(c_ref, ab_ref, dq_r, dk_r, dv_r, dg_r, db_r, sa, sb, al, dt, dc_out, dab_out, dal_out, ddt_out):
        qc = tuple(c_ref[:, h * HEAD:(h + 1) * HEAD] for h in range(HEADS))
        kc = tuple(c_ref[:, DN_WIDTH + h * HEAD:DN_WIDTH + (h + 1) * HEAD] for h in range(HEADS))
        consts = _dn_prep_consts(sa, sb, al, dt)
        sel = dict(sel_a=consts["sel_a"], sel_b=consts["sel_b"])
        _, vjp = jax.vjp(lambda rows, ad: _dn_prep_fn(rows, {**sel, **ad}), (qc, kc, ab_ref[...]),
                         dict(alog=consts["alog"], dtb=consts["dtb"]))
        cq = tuple(dq_r[:, h * HEAD:(h + 1) * HEAD] for h in range(HEADS))
        ck = tuple(dk_r[:, h * HEAD:(h + 1) * HEAD] for h in range(HEADS))
        (dqc, dkc, dab), dad = vjp((cq, ck, dg_r[...], db_r[...]))
        for h in range(HEADS):
            dc_out[:, h * HEAD:(h + 1) * HEAD] = dqc[h]
            dc_out[:, DN_WIDTH + h * HEAD:DN_WIDTH + (h + 1) * HEAD] = dkc[h]
        dc_out[:, 2 * DN_WIDTH:3 * DN_WIDTH] = dv_r[...]
        dab_out[...] = dab.astype(BF16)
        _accumulate(dal_out, dad["alog"])
        _accumulate(ddt_out, dad["dtb"])

    return _rows_call("dn_prep_bwd", body, [conv, ab, dq, dk, dv, dg, db], [sel_a, sel_b, alog, dtb],
                      [(3 * DN_WIDTH, F32), (HEAD, BF16)], [(1, DN_WIDTH), (1, DN_WIDTH)], _row_tile(conv.shape[0]))


def _chunk_batch(t):
    nc = t // CHUNK
    return nc // 2 if nc % 2 == 0 else nc


def _dn_chunk_specs(t, nb):
    rows = nb * CHUNK
    blk = pl.BlockSpec((rows, HEAD), lambda h, b: (b, h))
    vblk = pl.BlockSpec((rows, HEAD), lambda h, b: (b, 2 * HEADS + h))
    mat = pl.BlockSpec((nb, HEAD, HEAD), lambda h, b: (b, h, 0))
    return rows, blk, vblk, mat


def _dn_chunk_fwd(qn, kn, conv, g, beta, host=None):
    t = qn.shape[0]
    nb = _chunk_batch(t)
    rows, blk, vblk, mat = _dn_chunk_specs(t, nb)

    def body(q_ref, k_ref, v_ref, g_ref, b_ref, n_o, b_o, qe_o, oo_o, eg_o):
        r3 = lambda x: x.reshape(nb, CHUNK, x.shape[-1])
        n_mat, b_mat, q_eff, o_own, eg = _dn_chunk_fn(r3(q_ref[...]), r3(k_ref[...]), r3(v_ref[...]), r3(g_ref[...]),
                                                      r3(g_ref[:, 0:CHUNK]), r3(b_ref[...]))
        n_o[...] = n_mat
        b_o[...] = b_mat
        qe_o[...] = q_eff.reshape(rows, HEAD)
        oo_o[...] = o_own.reshape(rows, HEAD)
        eg_o[...] = jnp.broadcast_to(eg, (nb, HEAD, HEAD))

    nc = t // CHUNK
    mats = jax.ShapeDtypeStruct((nc, DN_WIDTH, HEAD), F32)
    rowsd = jax.ShapeDtypeStruct((t, DN_WIDTH), F32)
    return _pcall(body, "dn_chunk_fwd", (HEADS, t // rows), [blk, blk, vblk, blk, blk], [mat, mat, blk, blk, mat],
                  [mats, mats, rowsd, rowsd, mats], [qn, kn, conv, g, beta], ("arbitrary", "arbitrary"), host=host)


def _dn_chunk_bwd(qn, kn, conv, g, beta, sall, gall, dq_eff, do, host=None):
    t = qn.shape[0]
    nb = _chunk_batch(t)
    rows, blk, vblk, mat = _dn_chunk_specs(t, nb)

    def body(q_ref, k_ref, v_ref, g_ref, b_ref, s_ref, ga_ref, dqe_ref, do_ref, dq_o, dk_o, dv_o, dg_o, db_o):
        r3 = lambda x: x.reshape(nb, CHUNK, x.shape[-1])
        _, vjp = jax.vjp(_dn_chunk_fn, r3(q_ref[...]), r3(k_ref[...]), r3(v_ref[...]), r3(g_ref[...]),
                         r3(g_ref[:, 0:CHUNK]), r3(b_ref[...]))
        s, ga = s_ref[...], ga_ref[...]
        d_n = -_bmm_nt(ga, s)
        d_eg = jnp.sum(ga * s, axis=1, keepdims=True)
        dq, dk, dv, dg, dg64, db = vjp((d_n, ga, r3(dqe_ref[...]), r3(do_ref[...]), d_eg))
        for o_ref, val in zip((dq_o, dk_o, dv_o, dg_o, db_o), (dq, dk, dv, dg, db)):
            o_ref[...] = val.reshape(rows, HEAD)
        dg_o[:, 0:CHUNK] += dg64.reshape(rows, CHUNK)

    return _pcall(body, "dn_chunk_bwd", (HEADS, t // rows), [blk, blk, vblk, blk, blk, mat, mat, blk, blk], [blk] * 5,
                  [jax.ShapeDtypeStruct((t, DN_WIDTH), F32)] * 5, [qn, kn, conv, g, beta, sall, gall, dq_eff, do],
                  ("arbitrary", "arbitrary"), host=host)


def _dn_rec_fwd(n_mat, b_mat, eg, host=None):
    nc = n_mat.shape[0]
    nb = _chunk_batch(nc * CHUNK)
    spec = pl.BlockSpec((nb, DN_WIDTH, HEAD), lambda i: (i, 0, 0))

    def body(n_ref, b_ref, eg_ref, sall_ref, s_scr):
        @pl.when(pl.program_id(0) == 0)
        def _():
            s_scr[...] = jnp.zeros(s_scr.shape, F32)

        for j in range(nb):
            sall_ref[j] = s_scr[...]
            for h in range(HEADS):
                sl = slice(h * HEAD, (h + 1) * HEAD)
                s_scr[sl, :] = _dn_rec_fn(s_scr[sl, :], n_ref[j, sl, :], b_ref[j, sl, :],
                                          eg_ref[j, h * HEAD:h * HEAD + 1, :])

    return _pcall(body, "dn_rec_fwd", (nc // nb,), [spec] * 3, spec, jax.ShapeDtypeStruct((nc, DN_WIDTH, HEAD), F32),
                  [n_mat, b_mat, eg], ("arbitrary",), scratch_shapes=[pltpu.VMEM((DN_WIDTH, HEAD), F32)], host=host)


def _dn_rec_bwd(n_mat, eg, ds_out, host=None):
    nc = n_mat.shape[0]
    nb = _chunk_batch(nc * CHUNK)
    steps = nc // nb
    spec = pl.BlockSpec((nb, DN_WIDTH, HEAD), lambda i: (steps - 1 - i, 0, 0))

    def body(n_ref, eg_ref, dso_ref, gall_ref, g_scr):
        @pl.when(pl.program_id(0) == 0)
        def _():
            g_scr[...] = jnp.zeros(g_scr.shape, F32)

        for j in reversed(range(nb)):
            gall_ref[j] = g_scr[...]
            for h in range(HEADS):
                sl = slice(h * HEAD, (h + 1) * HEAD)
                gv = g_scr[sl, :]
                g_scr[sl, :] = (gv * eg_ref[j, h * HEAD:h * HEAD + 1, :] - _mm_tn(n_ref[j, sl, :], gv)
                                + dso_ref[j, sl, :])

    return _pcall(body, "dn_rec_bwd", (steps,), [spec] * 3, spec, jax.ShapeDtypeStruct((nc, DN_WIDTH, HEAD), F32),
                  [n_mat, eg, ds_out], ("arbitrary",), scratch_shapes=[pltpu.VMEM((DN_WIDTH, HEAD), F32)], host=host)


def _dn_o_fwd(sall, q_eff, o_own):
    t = q_eff.shape[0]
    nb = _chunk_batch(t)
    rows, blk, _, mat = _dn_chunk_specs(t, nb)

    def body(s_ref, qe_ref, oo_ref, o_ref):
        r3 = lambda x: x.reshape(nb, CHUNK, HEAD)
        o_ref[...] = _dn_o_fn(s_ref[...], r3(qe_ref[...]), r3(oo_ref[...])).reshape(rows, HEAD)

    return _pcall(body, "dn_o_fwd", (HEADS, t // rows), [mat, blk, blk], blk, jax.ShapeDtypeStruct((t, DN_WIDTH), F32),
                  [sall, q_eff, o_own], ("arbitrary", "arbitrary"))


def _dn_o_bwd(sall, q_eff, do, host=None):
    t = q_eff.shape[0]
    nb = _chunk_batch(t)
    rows, blk, _, mat = _dn_chunk_specs(t, nb)

    def body(s_ref, qe_ref, do_ref, dqe_ref, ds_ref):
        r3 = lambda x: x.reshape(nb, CHUNK, HEAD)
        dov = r3(do_ref[...])
        dqe_ref[...] = _bmm_nt(dov, s_ref[...]).reshape(rows, HEAD)
        ds_ref[...] = _bmm_tn(r3(qe_ref[...]), dov)

    nc = t // CHUNK
    return _pcall(body, "dn_o_bwd", (HEADS, t // rows), [mat, blk, blk], [blk, mat],
                  [jax.ShapeDtypeStruct((t, DN_WIDTH), F32), jax.ShapeDtypeStruct((nc, DN_WIDTH, HEAD), F32)],
                  [sall, q_eff, do], ("arbitrary", "arbitrary"), host=host)


def _adamw_update(w, g, m, v):
    m2 = ADAM_B1 * m + (1.0 - ADAM_B1) * g
    v2 = ADAM_B2 * v + (1.0 - ADAM_B2) * (g * g)
    m_hat = m2 / (1.0 - ADAM_B1 ** ADAM_STEP)
    v_hat = v2 / (1.0 - ADAM_B2 ** ADAM_STEP)
    return -ADAM_LR * (m_hat / (jnp.sqrt(v_hat) + ADAM_EPS) + ADAM_WD * w), m2, v2


def _adamw_small(ws, gs, ms, vs):
    n = len(ws)

    def body(*refs):
        for i in range(n):
            d, m2, v2 = _adamw_update(refs[i][...], refs[n + i][...], refs[2 * n + i][...], refs[3 * n + i][...])
            refs[4 * n + i][...] = d
            refs[5 * n + i][...] = m2
            refs[6 * n + i][...] = v2

    shapes = [jax.ShapeDtypeStruct(a.shape, F32) for a in ws]
    outs = pl.pallas_call(body, name="adamw_small", out_shape=shapes * 3,
                          compiler_params=pltpu.CompilerParams(vmem_limit_bytes=VMEM_LIMIT))(*ws, *gs, *ms, *vs)
    return outs[:n], outs[n:2 * n], outs[2 * n:]


def _adamw_call(name, w, g, m, v, host=None):
    rows, cols = w.shape
    by_rows = rows % 8 == 0

    def body(w_ref, g_ref, m_ref, v_ref, g_out, d_ref, m_out, v_out):
        gv = g_ref[...] if by_rows else g_ref[0:rows, :]
        g_out[...] = gv
        d_ref[...], m_out[...], v_out[...] = _adamw_update(w_ref[...], gv, m_ref[...], v_ref[...])

    if by_rows:
        tr = _tile(rows, 256, 8)
        spec = g_spec = pl.BlockSpec((tr, cols), lambda i: (i, 0))
        grid = (rows // tr,)
    else:
        tc = _tile(cols, 256, 128)
        spec = pl.BlockSpec((rows, tc), lambda j: (0, j))
        g_spec = pl.BlockSpec((g.shape[0], tc), lambda j: (0, j))
        grid = (cols // tc,)
    return _pcall(body, name, grid, [spec, g_spec, spec, spec], [spec] * 4, [jax.ShapeDtypeStruct((rows, cols), F32)] * 4,
                  [w, g, m, v], ("arbitrary",), host=host)


def _rope_tables(t):
    half = ROPE // 2
    inv_freq = np.float32(ROPE_THETA) ** (-np.arange(half, dtype=np.float32) / np.float32(half))
    ang = np.arange(t, dtype=np.float32)[:, None] * inv_freq[None, :].astype(np.float32)
    z = np.zeros((t, HEAD - ROPE), np.float32)
    cos = np.concatenate([np.cos(ang), np.cos(ang), z], axis=1).astype(np.float32)
    sin = np.concatenate([np.sin(ang), np.sin(ang), z], axis=1).astype(np.float32)
    k = np.arange(HEAD)[:, None]
    l = np.arange(HEAD)[None, :]
    perm = np.where((l < half) & (k == l + half), -1.0, 0.0) + np.where((l >= half) & (l < ROPE) & (k == l - half), 1.0, 0.0)
    return jnp.asarray(cos), jnp.asarray(sin), jnp.asarray(perm.astype(np.float32))


def _win_to_pad(w):
    z = lambda n: jnp.zeros((n, w.shape[1]), w.dtype)
    return jnp.concatenate([w[576:2112], w[2112:2624], w[0:256], w[256:512], w[512:576], z(64), w[2624:2632], z(120)],
                           axis=0)


def _win_from_pad(g):
    return jnp.concatenate([g[2048:2304], g[2304:2560], g[2560:2624], g[0:1536], g[1536:2048], g[2688:2696]], axis=0)


def _qk_to_pad(w):
    w4 = w.reshape(HEADS, QK_DIM, w.shape[-1])
    return jnp.concatenate([w4, jnp.zeros((HEADS, QK_PAD - QK_DIM, w.shape[-1]), w.dtype)], axis=1).reshape(
        HEADS * QK_PAD, w.shape[-1])


def _qk_from_pad(g):
    return g.reshape(HEADS, QK_PAD, g.shape[-1])[:, :QK_DIM].reshape(HEADS * QK_DIM, g.shape[-1])


def _ff_to_pad(a, axis):
    shape = list(a.shape)
    shape[axis:axis + 1] = [N_CHIPS, FF_SHARD]
    a4 = a.reshape(shape)
    shape[axis + 1] = FF_BLOCK - FF_SHARD
    out = jnp.concatenate([a4, jnp.zeros(shape, a.dtype)], axis=axis + 1)
    shape[axis:axis + 2] = [D_FF_P]
    return out.reshape(shape)


def _ff_from_pad(a, axis):
    shape = list(a.shape)
    shape[axis:axis + 1] = [N_CHIPS, FF_BLOCK]
    a4 = lax.slice_in_dim(a.reshape(shape), 0, FF_SHARD, axis=axis + 1)
    shape[axis:axis + 2] = [D_FF]
    return a4.reshape(shape)


class _LocalPlan:
    def __init__(self, wt):
        self.wt, self.grads = wt, {}

    def weight(self, name):
        return self.wt[name]

    def host(self, point):
        return None

    def grad(self, name, value):
        self.grads[name] = value


def _local_step(x, tgt, wt, plan=None):
    plan = _LocalPlan(wt) if plan is None else plan
    s = x.shape[0]
    n_valid = N_META + s
    t = -(-n_valid // HEAD) * HEAD
    assert t - n_valid >= 3, "the DeltaNet conv kernels rely on at least three zero rows after the sequence"
    zpad = jnp.zeros((t - n_valid, D_MODEL), F32)
    h0 = jnp.concatenate([wt["meta_tokens"], x, zpad], axis=0)
    tgt_p = jnp.concatenate([jnp.zeros((N_META, D_MODEL), F32), tgt, zpad], axis=0)
    cos, sin, perm = _rope_tables(t)
    qn_w = jnp.concatenate([wt["q_norm_w"], jnp.zeros((1, QK_PAD - QK_DIM), F32)], axis=1)
    kn_w = jnp.concatenate([wt["k_norm_w"], jnp.zeros((1, QK_PAD - QK_DIM), F32)], axis=1)
    head_id = jnp.arange(DN_WIDTH)[None, :] // HEAD
    lane = jnp.arange(HEAD)[:, None]
    sel_a = (lane == head_id).astype(F32)
    sel_b = (lane == head_id + HEADS).astype(F32)
    alog = jnp.repeat(wt["dn_A_log"], HEAD, axis=1)
    dtb = jnp.repeat(wt["dn_dt_bias"], HEAD, axis=1)
    conv_w, conv_b = wt["ffn_conv_w"], wt["ffn_conv_b"]

    u = _rms_fwd("attn_norm_fwd", h0, wt["attn_norm_w"], host=plan.host("attn_norm_fwd"))
    win, wq, wkv = plan.weight("w_in_t"), plan.weight("w_q_t"), plan.weight("w_kv_t")
    proj = _matmul("in_proj", u, win, "nt", F32)
    z = (proj, DN_WIDTH, 3)
    q_lat, kv_lat, k_pe, ab = (proj, LORA, 8), (proj, LORA, 9), (proj, HEAD, 20), (proj, HEAD, 21)
    mla_consts = (wt["q_a_norm_w"], wq, wt["kv_a_norm_w"], wkv, qn_w, kn_w, perm)
    q, k, v = _mla_prep_fwd(q_lat, kv_lat, k_pe, cos, sin, *mla_consts)
    o_mla = _attn_fwd(q, k, v, host=plan.host("attn_fwd"))
    conv = _dn_conv_fwd(proj, wt["dn_conv_w"])
    dn_consts = (sel_a, sel_b, alog, dtb)
    qn, kn, g, beta = _dn_prep_fwd(conv, ab, *dn_consts)
    n_mat, b_mat, q_eff, o_own, eg = _dn_chunk_fwd(qn, kn, conv, g, beta, host=plan.host("dn_chunk_fwd"))
    sall = _dn_rec_fwd(n_mat, b_mat, eg)
    o_dn = _dn_o_fwd(sall, q_eff, o_own)
    w_out = plan.weight("w_out")
    mixed, h1, n2 = _mix_out_proj(o_mla, o_dn, z, wt["mla_out_norm_w"], wt["dn_out_norm_w"], w_out, h0, wt["ffn_norm_w"])
    w_gate, w_up = plan.weight("w_gate_t"), plan.weight("w_up_t")
    gpre, up, act = _ffn_glu_fwd(n2, w_gate, w_up, conv_w, conv_b, host=plan.host("ffn_glu_fwd"))
    w_down = plan.weight("w_down")
    dy, dy16, sq = _down_proj_loss(act, w_down, h1, tgt_p, n_valid)

    grads = {}
    plan.grad("w_down", _matmul("down_dw", act, dy16, "tn", BF16))
    dgpre, dup, grads["ffn_conv_w"], grads["ffn_conv_b"] = _ffn_glu_bwd(gpre, up, dy16, w_down, conv_w, conv_b)
    plan.grad("w_gate_t", _matmul("gate_dw", dgpre, n2, "tn", BF16))
    plan.grad("w_up_t", _matmul("up_dw", dup, n2, "tn", BF16))
    dh1, dh1_16, grads["ffn_norm_w"] = _ffn_in_bwd(dgpre, dup, w_gate, w_up, h1, dy, wt["ffn_norm_w"],
                                                   host=plan.host("ffn_in_bwd"))
    plan.grad("w_out", _matmul("out_dw", mixed, dh1_16, "tn", BF16))
    do_mla, do_dn, dz, grads["mla_out_norm_w"], grads["dn_out_norm_w"] = _mix_out_bwd(
        o_mla, o_dn, z, dh1_16, w_out, wt["mla_out_norm_w"], wt["dn_out_norm_w"], host=plan.host("mix_out_bwd"))
    dq_eff, ds_out = _dn_o_bwd(sall, q_eff, do_dn)
    gall = _dn_rec_bwd(n_mat, eg, ds_out)
    dqn, dkn, dv_dn, dg, dbeta = _dn_chunk_bwd(qn, kn, conv, g, beta, sall, gall, dq_eff, do_dn,
                                               host=plan.host("dn_chunk_bwd"))
    dconv, dab, dalog, ddtb = _dn_prep_bwd(conv, ab, dqn, dkn, dv_dn, dg, dbeta, *dn_consts)
    grads["dn_A_log"] = jnp.sum(dalog.reshape(HEADS, HEAD), axis=1)[None, :]
    grads["dn_dt_bias"] = jnp.sum(ddtb.reshape(HEADS, HEAD), axis=1)[None, :]
    ddn_pre, grads["dn_conv_w"] = _dn_conv_bwd(proj, wt["dn_conv_w"], dconv)
    dq, dk, dv = _attn_bwd(q, k, v, do_mla, host=plan.host("attn_bwd"))
    dq_lat, dkv_lat, dk_pe, dqa, dwq, dkva, dwkv, dqnw, dknw = _mla_prep_bwd(
        q_lat, kv_lat, k_pe, cos, sin, dq, dk, dv, *mla_consts, host=plan.host("mla_prep_bwd"))
    grads["q_a_norm_w"], grads["kv_a_norm_w"] = dqa, dkva
    plan.grad("w_q_t", dwq)
    plan.grad("w_kv_t", dwkv)
    grads["q_norm_w"], grads["k_norm_w"] = dqnw[:, :QK_DIM], dknw[:, :QK_DIM]
    dproj = jnp.concatenate([ddn_pre, dz, dq_lat, dkv_lat, dk_pe, dab], axis=1)
    plan.grad("w_in_t", _matmul("in_dw", dproj, u, "tn", F32))
    du = _matmul("in_dx", dproj, win, "nn", BF16, host=plan.host("in_dx"))
    dh0, _, grads["attn_norm_w"] = _rms_bwd("attn_norm_bwd", h0, wt["attn_norm_w"], [du], dh1,
                                            host=plan.host("attn_norm_bwd"))
    grads["meta_tokens"] = dh0[0:N_META]
    if isinstance(plan, _LocalPlan):
        grads.update(plan.grads)
    return sq, dh0[N_META:n_valid], grads


def _mesh_pos():
    return lax.axis_index("x"), lax.axis_index("y"), lax.axis_index("c")


def _other_chips(x, y):
    return [(1 - x, y), (x, 1 - y), (1 - x, 1 - y)]


def _remote(src, dst, send_sems, recv_sems, k, to):
    return pltpu.make_async_remote_copy(src_ref=src, dst_ref=dst, send_sem=send_sems.at[k], recv_sem=recv_sems.at[k],
                                        device_id=to, device_id_type=MESH)


SIBLING_ID, CHIPS_ID, GATHER_ID, ALL_ID = 1, 2, 3, 4


def _sibling_peer():
    x, y, c = _mesh_pos()
    return [(x, y, 1 - c)]


def _chip_peers():
    x, y, c = _mesh_pos()
    return [(qx, qy, c) for qx, qy in _other_chips(x, y)]


def _copies_exchange(make, ins, out_shape, nsem, peers=None, cid=None):
    def prog(in_refs, out_refs, send_sems, recv_sems):
        copies = make(in_refs, out_refs, send_sems, recv_sems)

        def start():
            for cp in copies:
                cp.start()

        def finish():
            for cp in copies:
                cp.wait()

        return start, finish

    return _Exchange(prog, ins, out_shape, nsem, peers, cid)


def _all_gather(shards):
    def prog(srcs, dsts, send_sems, recv_sems):
        x, y, c = _mesh_pos()
        p = 2 * x + y
        sibling = (x, y, 1 - c)
        chips = _other_chips(x, y)
        bufs = tuple((s, d, s.shape[0] // 2) for s, d in zip(srcs, dsts))

        def half(ref, rows, which):
            return ref.at[pl.ds(which * rows, rows), :]

        def copy(i, k, src, dst, to):
            return _remote(src, dst, send_sems, recv_sems, 6 * i + k, to)

        sends = [copy(i, j, half(src, rows, c), half(dst.at[p], rows, c), (*chip, c))
                 for i, (src, dst, rows) in enumerate(bufs) for j, chip in enumerate(chips)]

        def start():
            for cp in sends:
                cp.start()

        def finish():
            passed = []
            for i, (src, dst, rows) in enumerate(bufs):
                for j, (qx, qy) in enumerate(chips):
                    block = half(dst.at[2 * qx + qy], rows, c)
                    copy(i, j, block, block, (x, y, c)).wait_recv()
                    fwd = copy(i, 3 + j, block, block, sibling)
                    fwd.start()
                    passed.append(fwd)
            for i, (src, dst, rows) in enumerate(bufs):
                for j, (qx, qy) in enumerate(chips):
                    block = half(dst.at[2 * qx + qy], rows, 1 - c)
                    copy(i, 3 + j, block, block, (x, y, c)).wait_recv()
            for cp in sends + passed:
                cp.wait_send()

        return start, finish

    return _Exchange(prog, shards, [jax.ShapeDtypeStruct((N_CHIPS, *s.shape), s.dtype) for s in shards], 6 * len(shards),
                     lambda: _sibling_peer() + _chip_peers(), GATHER_ID)


def _all_gather_small(block):
    def make(srcs, dsts, send_sems, recv_sems):
        x, y, c = _mesh_pos()
        return [_remote(srcs[0], dsts[0].at[2 * x + y], send_sems, recv_sems, k, (qx, qy, c))
                for k, (qx, qy) in enumerate(_other_chips(x, y))]

    return _copies_exchange(make, [block], [jax.ShapeDtypeStruct((N_CHIPS, *block.shape), block.dtype)], 3, _chip_peers,
                            CHIPS_ID)


def _gathered(ex):
    p = 2 * lax.axis_index("x") + lax.axis_index("y")
    return [lax.dynamic_update_slice(g, s[None], (p, 0, 0)) for g, s in zip(ex.outs, ex.ins)]


def _rs_to_sibling(bufs):
    def make(srcs, dsts, send_sems, recv_sems):
        x, y, c = _mesh_pos()
        copies = []
        for i, (src, dst) in enumerate(zip(srcs, dsts)):
            half = src.shape[1] // 2
            copies.append(_remote(src.at[:, pl.ds((1 - c) * half, half), :], dst, send_sems, recv_sems, i, (x, y, 1 - c)))
        return copies

    return _copies_exchange(make, bufs,
                            [jax.ShapeDtypeStruct((N_CHIPS, b.shape[1] // 2, b.shape[2]), b.dtype) for b in bufs],
                            len(bufs), _sibling_peer, SIBLING_ID)


def _rs_pair_add(name, bufs, gots, c, out_dtype):
    n = len(bufs)

    def body(c_ref, *refs):
        for a_ref, b_ref, o_ref in zip(refs[:n], refs[n:2 * n], refs[2 * n:]):
            o_ref[...] = (a_ref[...].astype(F32) + b_ref[...].astype(F32)).astype(out_dtype)

    mine = [pl.BlockSpec((None, g.shape[1], g.shape[2]), lambda j, cr: (j, cr[0], 0)) for g in gots]
    whole = [pl.BlockSpec((None, g.shape[1], g.shape[2]), lambda j, cr: (j, 0, 0)) for g in gots]
    return pl.pallas_call(
        body, name=name,
        grid_spec=pltpu.PrefetchScalarGridSpec(num_scalar_prefetch=1, grid=(N_CHIPS,), in_specs=mine + whole, out_specs=whole),
        out_shape=[jax.ShapeDtypeStruct(g.shape, out_dtype) for g in gots],
        compiler_params=_cparams(("arbitrary",)))(c, *bufs, *gots)


def _rs_to_chips(accs):
    def make(srcs, dsts, send_sems, recv_sems):
        x, y, c = _mesh_pos()
        return [_remote(src.at[2 * qx + qy], dst.at[k], send_sems, recv_sems, 3 * i + k, (qx, qy, c))
                for i, (src, dst) in enumerate(zip(srcs, dsts)) for k, (qx, qy) in enumerate(_other_chips(x, y))]

    return _copies_exchange(make, accs, [jax.ShapeDtypeStruct((3, a.shape[1], a.shape[2]), a.dtype) for a in accs],
                            3 * len(accs), _chip_peers, CHIPS_ID)


def _rs_chip_add(name, accs, gots, p):
    n = len(accs)
    slot = (0, 1, 0, 2)

    def body(p_ref, *refs):
        me = p_ref[0]
        for own_ref, got_ref, o_ref in zip(refs[:n], refs[n:2 * n], refs[2 * n:]):
            total = None
            for chip in range(N_CHIPS):
                val = own_ref[...].astype(F32)
                for e in (1, 2, 3):
                    val = jnp.where((chip ^ me) == e, got_ref[slot[e]].astype(F32), val)
                total = val if total is None else total + val
            o_ref[...] = total

    own = [pl.BlockSpec((None, a.shape[1], a.shape[2]), lambda i, pr: (pr[0], 0, 0)) for a in accs]
    got = [pl.BlockSpec(g.shape, lambda i, pr: (0, 0, 0)) for g in gots]
    out = [pl.BlockSpec((a.shape[1], a.shape[2]), lambda i, pr: (0, 0)) for a in accs]
    return pl.pallas_call(
        body, name=name,
        grid_spec=pltpu.PrefetchScalarGridSpec(num_scalar_prefetch=1, grid=(1,), in_specs=own + got, out_specs=out),
        out_shape=[jax.ShapeDtypeStruct((a.shape[1], a.shape[2]), F32) for a in accs],
        compiler_params=_cparams(("arbitrary",)))(p, *accs, *gots)


def _rs_share(ress):
    def make(srcs, dsts, send_sems, recv_sems):
        x, y, c = _mesh_pos()
        return [_remote(src, dst, send_sems, recv_sems, i, (x, y, 1 - c)) for i, (src, dst) in enumerate(zip(srcs, dsts))]

    return _copies_exchange(make, ress, [jax.ShapeDtypeStruct(r.shape, F32) for r in ress], len(ress), _sibling_peer,
                            SIBLING_ID)


def _shared(ex):
    south = lax.axis_index("c") == 0
    return [jnp.concatenate([jnp.where(south, r, g), jnp.where(south, g, r)], axis=0) for r, g in zip(ex.ins, ex.outs)]


def _all_to_all_devices(vec):
    def others():
        x, y, c = _mesh_pos()
        return [((1 - x if r & 4 else x), (1 - y if r & 2 else y), (1 - c if r & 1 else c)) for r in range(1, 8)]

    def make(srcs, dsts, send_sems, recv_sems):
        x, y, c = _mesh_pos()
        me = 4 * x + 2 * y + c
        return [_remote(srcs[0], dsts[0].at[me], send_sems, recv_sems, r, peer) for r, peer in enumerate(others())]

    return _copies_exchange(make, [vec], [jax.ShapeDtypeStruct((8, *vec.shape), vec.dtype)], 7, others, ALL_ID)


def _sum_devices(stack):
    def body(s_ref, o_ref):
        total = s_ref[0]
        for d in range(1, 8):
            total = total + s_ref[d]
        o_ref[...] = total

    return pl.pallas_call(body, name="sum_devices", out_shape=jax.ShapeDtypeStruct(stack.shape[1:], F32),
                          compiler_params=pltpu.CompilerParams(vmem_limit_bytes=VMEM_LIMIT))(stack)


def _pad_rows(flat, rows):
    return jnp.concatenate([flat, jnp.zeros((rows * LANES - flat.shape[0],), flat.dtype)]).reshape(rows, LANES)


def _unshard(g4, shape, axis):
    a = g4.reshape(N_CHIPS, *shape)
    if axis == 0:
        return a.reshape(N_CHIPS * shape[0], shape[1])
    return jnp.transpose(a, (1, 0, 2)).reshape(shape[0], N_CHIPS * shape[1])


def _shard4(full, shape, axis):
    if axis == 0:
        return full.reshape(N_CHIPS, shape[0] * shape[1])
    a = full.reshape(shape[0], N_CHIPS, shape[1])
    return jnp.transpose(a, (1, 0, 2)).reshape(N_CHIPS, shape[0] * shape[1])


def _pad_axis0(a, rows):
    return jnp.concatenate([a, jnp.zeros((rows - a.shape[0], *a.shape[1:]), a.dtype)], axis=0)


def _pad_axis1(a, rows):
    return jnp.concatenate([a, jnp.zeros((a.shape[0], rows - a.shape[1], *a.shape[2:]), a.dtype)], axis=1)


def _shard_to_strip(name, w):
    _, (shape, axis, rows) = name, {n: (s, ax, r) for n, s, ax, r in BIG}[name]
    w2 = w.reshape(shape).astype(BF16)
    if name == "w_in":
        return w2
    return _pad_axis0(w2.T if axis == 1 else w2, rows)


LOCAL_NAME = dict(w_in="w_in_t", w_q_b="w_q_t", w_kv_b="w_kv_t", w_out="w_out", w_gate="w_gate_t", w_up="w_up_t",
                  w_down="w_down")


WIN_SEGMENTS = ((576, 2112, 0), (2112, 2624, 1536), (0, 256, 2048), (256, 512, 2304), (512, 576, 2560), (2624, 2632, 2688))


def _strips_to_weight(name, g4):
    if name == "w_in":
        return _win_to_pad(jnp.transpose(g4, (0, 2, 1)).reshape(IN_COLS, D_MODEL))
    if name == "w_q_b":
        return _qk_to_pad(g4.reshape(HEADS * QK_DIM, LORA))
    return g4.reshape(N_CHIPS * g4.shape[1], g4.shape[2])


def _grad_to_strips(name, g):
    if name == "w_in":
        strips = []
        for q in range(N_CHIPS):
            pieces = []
            for a, b, local in sorted(WIN_SEGMENTS):
                s, e = max(a, q * IN_SHARD), min(b, (q + 1) * IN_SHARD)
                if s < e:
                    pieces.append(g[local + s - a:local + e - a])
            pieces.append(jnp.zeros((IN_SHARD_P - IN_SHARD, D_MODEL), g.dtype))
            strips.append(jnp.concatenate(pieces, axis=0))
        return jnp.stack(strips)
    if name == "w_q_b":
        return _qk_from_pad(g).reshape(N_CHIPS, QK_DIM, LORA)
    return g.reshape(N_CHIPS, g.shape[0] // N_CHIPS, g.shape[1])


class _MeshPlan:
    LATE = dict(attn_norm_fwd=("w_in", "w_q_b", "w_kv_b"), attn_fwd=("w_up",), dn_chunk_fwd=("w_out", "w_gate"),
                ffn_glu_fwd=("w_down",))
    GROUP_A = ("w_down", "w_gate", "w_up", "w_out")
    GROUP_B = ("w_in", "w_q_b", "w_kv_b")

    def __init__(self, w):
        x, y, c = _mesh_pos()
        self.ci = jnp.reshape(c, (1,)).astype(jnp.int32)
        self.pi = jnp.reshape(2 * x + y, (1,)).astype(jnp.int32)
        self.strip = {n: _shard_to_strip(n, w[n]) for n, _, _, _ in BIG}
        self.gathers, self.weights, self.g, self.acc, self.reduced = {}, {}, {}, {}, {}
        self.sibs, self.sib, self.chip, self.share, self.halves = [], None, None, None, [None, None]

    def gather_small(self, small):
        ex = _all_gather_small(small)
        ex.run("all_gather_small")
        return _gathered(ex)[0]

    def weight(self, local_name):
        if local_name not in self.weights:
            for point, (names, ex) in list(self.gathers.items()):
                if ex.outs is not None:
                    for n, g4 in zip(names, _gathered(ex)):
                        if "/" in n:
                            n, half = n.split("/")
                            self.halves[int(half)] = g4
                            if None in self.halves:
                                continue
                            g4 = jnp.concatenate(self.halves, axis=1)
                        self.weights[LOCAL_NAME[n]] = _strips_to_weight(n, g4)
                    del self.gathers[point]
        return self.weights[local_name]

    def _shard(self, name):
        if "/" not in name:
            return self.strip[name]
        name, half = name.split("/")
        rows = self.strip[name].shape[0] // 2
        return self.strip[name][int(half) * rows:(int(half) + 1) * rows]

    def grad(self, local_name, value):
        name = {v: k for k, v in LOCAL_NAME.items()}[local_name]
        self.g[name] = _grad_to_strips(name, value)

    def _pair_add(self, names, gots):
        accs = _rs_pair_add("rs_pair_add_" + names[0], [self.g[n] for n in names], gots, self.ci, BF16)
        self.acc.update(zip(names, accs))

    def _chip_add(self, names, chip):
        return _rs_chip_add("rs_chip_add_" + names[0], [self.acc[n] for n in names], chip.outs, self.pi)

    def _take_shared(self, names, share):
        for n, strip in zip(names, _shared(share)):
            self.reduced[n] = strip

    def host(self, point):
        a, b = self.GROUP_A, self.GROUP_B
        if point in self.LATE:
            names = self.LATE[point]
            ex = _all_gather([self._shard(n) for n in names])
            self.gathers[point] = (names, ex)
            return ex
        if point in ("ffn_in_bwd", "mix_out_bwd"):
            names = dict(ffn_in_bwd=a[:3], mix_out_bwd=a[3:])[point]
            ex = _rs_to_sibling([self.g[n] for n in names])
            self.sibs.append(ex)
            return ex
        if point == "dn_chunk_bwd":
            self._pair_add(a, [o for ex in self.sibs for o in ex.outs])
            self.chip1 = _rs_to_chips([self.acc[n] for n in a[:2]])
            return self.chip1
        if point == "attn_bwd":
            self.chip2 = _rs_to_chips([self.acc[n] for n in a[2:]])
            return self.chip2
        if point == "mla_prep_bwd":
            ress = self._chip_add(a[:2], self.chip1) + self._chip_add(a[2:], self.chip2)
            self.share = _rs_share(ress)
            return self.share
        if point == "in_dx":
            self._take_shared(a, self.share)
            self.sib = _rs_to_sibling([self.g[n] for n in b])
            return self.sib
        if point == "attn_norm_bwd":
            self._pair_add(b, self.sib.outs)
            self.chip = _rs_to_chips([self.acc[n] for n in b])
            return self.chip
        return None

    def last_share(self):
        self.share = _rs_share(self._chip_add(self.GROUP_B, self.chip))
        return self.share

    def finish(self):
        self._take_shared(self.GROUP_B, self.share)
        return self.reduced


def _strip_to_shard(name, strip):
    shape, axis = {n: (s, ax) for n, s, ax, _ in BIG}[name]
    rows = shape[axis]
    return strip[:rows].T if axis == 1 else strip[:rows]


def kernel(x, meta_tokens, attn_norm_w, w_in, q_a_norm_w, w_q_b, kv_a_norm_w, w_kv_b, q_norm_w, k_norm_w, mla_out_norm_w, dn_conv_w, dn_A_log, dn_dt_bias, dn_out_norm_w, w_out, ffn_norm_w, w_gate, w_up, ffn_conv_w, ffn_conv_b, w_down, loss_target, m_meta_tokens, m_attn_norm_w, m_w_in, m_q_a_norm_w, m_w_q_b, m_kv_a_norm_w, m_w_kv_b, m_q_norm_w, m_k_norm_w, m_mla_out_norm_w, m_dn_conv_w, m_dn_A_log, m_dn_dt_bias, m_dn_out_norm_w, m_w_out, m_ffn_norm_w, m_w_gate, m_w_up, m_ffn_conv_w, m_ffn_conv_b, m_w_down, v_meta_tokens, v_attn_norm_w, v_w_in, v_q_a_norm_w, v_w_q_b, v_kv_a_norm_w, v_w_kv_b, v_q_norm_w, v_k_norm_w, v_mla_out_norm_w, v_dn_conv_w, v_dn_A_log, v_dn_dt_bias, v_dn_out_norm_w, v_w_out, v_ffn_norm_w, v_w_gate, v_w_up, v_ffn_conv_w, v_ffn_conv_b, v_w_down):
    local = dict(locals())
    w = {n: local[n] for n in WEIGHTS}
    m = {n: local["m_" + n] for n in WEIGHTS}
    v = {n: local["v_" + n] for n in WEIGHTS}
    p = 2 * lax.axis_index("x") + lax.axis_index("y")

    plan = _MeshPlan(w)
    wf = _pad_rows(jnp.concatenate([w[n].reshape(-1) for n, _, _ in SMALL_SHARDED]), SMALL_ROWS)
    gf = plan.gather_small(wf).reshape(N_CHIPS, -1)
    full = {}
    off = 0
    for n, s, ax in SMALL_SHARDED:
        full[n] = _unshard(gf[:, off:off + s[0] * s[1]], s, ax)
        off += s[0] * s[1]
    for n, _ in REPLICATED:
        full[n] = w[n]
    full["ffn_conv_w"] = _ff_to_pad(full["ffn_conv_w"], 1)
    full["ffn_conv_b"] = _ff_to_pad(full["ffn_conv_b"], 1)

    sq, grad_x, g = _local_step(x[0], loss_target[0], full, plan)
    g["ffn_conv_w"] = _ff_from_pad(g["ffn_conv_w"], 1)
    g["ffn_conv_b"] = _ff_from_pad(g["ffn_conv_b"], 1)

    small_all = [n for n, _, _ in SMALL_SHARDED] + [n for n, _ in REPLICATED]
    vec = jnp.concatenate([g[n].reshape(-1) for n in small_all] + [jnp.reshape(0.5 / D_MODEL * jnp.sum(sq), (1,))])
    vec = _pad_rows(vec, -(-vec.shape[0] // (8 * LANES)) * 8)
    a2a = _all_to_all_devices(vec)

    gs, delta, new_m, new_v = {}, {}, {}, {}
    big = {n: (s, ax) for n, s, ax, _ in BIG}

    def adamw_big(n, strips, host=None):
        s, ax = big[n]
        flip = ax == 1 and s[1] % 8 == 0
        there = (lambda a: a.reshape(s).T) if flip else (lambda a: a.reshape(s))
        back = (lambda a: a.T.reshape(w[n].shape)) if flip else (lambda a: a.reshape(w[n].shape))
        strip = strips[n] if flip or ax == 0 else strips[n][:s[1]].T
        g2, d2, m2, v2 = _adamw_call("adamw_" + n, there(w[n]), strip, there(m[n]), there(v[n]), host=host)
        gs[n], delta[n], new_m[n], new_v[n] = back(g2), back(d2), back(m2), back(v2)

    adamw_big("w_down", plan.reduced, host=a2a)
    adamw_big("w_gate", plan.reduced, host=plan.last_share())
    adamw_big("w_up", plan.reduced)
    adamw_big("w_out", plan.reduced)
    strips = plan.finish()
    for n in plan.GROUP_B:
        adamw_big(n, strips)
    me = 4 * lax.axis_index("x") + 2 * lax.axis_index("y") + lax.axis_index("c")
    red = _sum_devices(lax.dynamic_update_slice(a2a.outs[0], vec[None], (me, 0, 0))).reshape(-1)
    off = 0
    for n in small_all:
        tot = red[off:off + g[n].size].reshape(g[n].shape)
        off += g[n].size
        shard = {sn: (s, ax) for sn, s, ax in SMALL_SHARDED}.get(n)
        if shard is not None:
            tot = lax.dynamic_slice_in_dim(tot, p * shard[0][1], shard[0][1], axis=1)
        gs[n] = tot
    loss = red[off]
    two_d = lambda a: a.reshape(a.shape[-2], a.shape[-1])
    outs = _adamw_small([two_d(w[n]) for n in small_all], [two_d(gs[n]) for n in small_all],
                        [two_d(m[n]) for n in small_all], [two_d(v[n]) for n in small_all])
    for i, n in enumerate(small_all):
        for dst, src in ((delta, outs[0]), (new_m, outs[1]), (new_v, outs[2])):
            dst[n] = src[i].reshape(w[n].shape)

    grad_out = [gs[n].reshape(w[n].shape) for n in WEIGHTS]
    return (loss, grad_x[None], *grad_out, *[delta[n] for n in WEIGHTS], *[new_m[n] for n in WEIGHTS],
            *[new_v[n] for n in WEIGHTS])
```

```python
import functools
import math

import jax
import jax.numpy as jnp
import numpy as np
from jax import lax
from jax.experimental import pallas as pl
from jax.experimental.pallas import tpu as pltpu

F32 = jnp.float32
BF16 = jnp.bfloat16
HI = lax.Precision.HIGHEST
MESH = pl.DeviceIdType.MESH

N_META = 16
D_MODEL = 1024
HEADS = 4
HEAD = 128
ROPE = 64
QK_DIM = HEAD + ROPE
QK_PAD = 2 * HEAD
LORA = 256
DN_WIDTH = HEADS * HEAD
CHUNK = 64
D_FF = 2816
N_CHIPS = 4
FF_SHARD = D_FF // N_CHIPS
FF_BLOCK = 768
D_FF_P = N_CHIPS * FF_BLOCK
IN_COLS = 2632
IN_SHARD = IN_COLS // N_CHIPS
IN_SHARD_P = 672
IN_PAD = 2816
NORM_EPS = 1e-6
ROPE_THETA = 10000.0
LANES = 512

ADAM_LR, ADAM_B1, ADAM_B2, ADAM_EPS, ADAM_WD, ADAM_STEP = 0.001, 0.9, 0.999, 1e-08, 0.01, 10

VMEM_LIMIT = 56 * 1024 * 1024

BIG = (("w_in", (1024, 658), 1, IN_SHARD_P), ("w_q_b", (256, 192), 1, 192), ("w_kv_b", (256, 256), 1, 256),
       ("w_out", (256, 1024), 0, 256), ("w_gate", (1024, 704), 1, FF_BLOCK), ("w_up", (1024, 704), 1, FF_BLOCK),
       ("w_down", (704, 1024), 0, FF_BLOCK))
SMALL_SHARDED = (("meta_tokens", (16, 256), 1), ("dn_conv_w", (4, 384), 1), ("ffn_conv_w", (3, 704), 1))
REPLICATED = (("attn_norm_w", 1024), ("q_a_norm_w", 256), ("kv_a_norm_w", 256), ("q_norm_w", 192), ("k_norm_w", 192),
              ("mla_out_norm_w", 128), ("dn_A_log", 4), ("dn_dt_bias", 4), ("dn_out_norm_w", 128), ("ffn_norm_w", 1024),
              ("ffn_conv_b", 2816))
WEIGHTS = ("meta_tokens", "attn_norm_w", "w_in", "q_a_norm_w", "w_q_b", "kv_a_norm_w", "w_kv_b", "q_norm_w", "k_norm_w",
           "mla_out_norm_w", "dn_conv_w", "dn_A_log", "dn_dt_bias", "dn_out_norm_w", "w_out", "ffn_norm_w", "w_gate",
           "w_up", "ffn_conv_w", "ffn_conv_b", "w_down")

SMALL_ROWS = 16
REP_ROWS = 16


def _cparams(sem):
    return pltpu.CompilerParams(dimension_semantics=sem, vmem_limit_bytes=VMEM_LIMIT)


class _Exchange:
    def __init__(self, prog, ins, out_shape, nsem, peers=None, cid=None):
        self.prog, self.ins, self.out_shape, self.nsem = prog, list(ins), list(out_shape), nsem
        self.peers, self.cid = peers, cid
        self.outs = None

    def sems(self):
        return [pltpu.SemaphoreType.DMA((self.nsem,)), pltpu.SemaphoreType.DMA((self.nsem,))]

    def programs(self, in_refs, out_refs, send_sems, recv_sems):
        start, finish = self.prog(in_refs, out_refs, send_sems, recv_sems)
        if self.cid is None:
            return start, finish
        peers = self.peers()

        def shake_and_start():
            barrier = pltpu.get_barrier_semaphore()
            for peer in peers:
                pl.semaphore_signal(barrier, inc=1, device_id=peer, device_id_type=MESH)
            pl.semaphore_wait(barrier, len(peers))
            start()

        return shake_and_start, finish

    def cparams(self, **kw):
        return pltpu.CompilerParams(has_side_effects=True, collective_id=self.cid, **kw)

    def run(self, name):
        any_spec = pl.BlockSpec(memory_space=pl.ANY)
        n = len(self.ins)

        def body(*refs):
            start, finish = self.programs(refs[:n], refs[n:-2], refs[-2], refs[-1])
            start()
            finish()

        self.outs = pl.pallas_call(
            body, name=name, in_specs=[any_spec] * n, out_specs=[any_spec] * len(self.out_shape),
            out_shape=self.out_shape, scratch_shapes=self.sems(), compiler_params=self.cparams())(*self.ins)
        return self.outs


def _pcall(body, name, grid, in_specs, out_specs, out_shape, args, sem, scratch_shapes=(), host=None):
    single = not isinstance(out_shape, (list, tuple))
    out_specs, out_shape = ([out_specs], [out_shape]) if single else (list(out_specs), list(out_shape))
    if host is None:
        outs = pl.pallas_call(body, name=name, grid=grid, in_specs=list(in_specs), out_specs=out_specs, out_shape=out_shape,
                              scratch_shapes=list(scratch_shapes), compiler_params=_cparams(sem))(*args)
        return outs[0] if single else outs
    any_spec = pl.BlockSpec(memory_space=pl.ANY)
    n_in, n_out, n_scr, nx_in, nx_out = len(in_specs), len(out_specs), len(scratch_shapes), len(host.ins), len(host.out_shape)

    def hosted(*refs):
        c_in, x_in = refs[:n_in], refs[n_in:n_in + nx_in]
        o0 = n_in + nx_in
        c_out, x_out = refs[o0:o0 + n_out], refs[o0 + n_out:o0 + n_out + nx_out]
        s0 = o0 + n_out + nx_out
        start, finish = host.programs(x_in, x_out, refs[s0 + n_scr], refs[s0 + n_scr + 1])
        first = functools.reduce(jnp.logical_and, [pl.program_id(d) == 0 for d in range(len(grid))])
        last = functools.reduce(jnp.logical_and, [pl.program_id(d) == grid[d] - 1 for d in range(len(grid))])
        pl.when(first)(start)
        body(*c_in, *c_out, *refs[s0:s0 + n_scr])
        pl.when(last)(finish)

    outs = pl.pallas_call(
        hosted, name=name, grid=grid, in_specs=list(in_specs) + [any_spec] * nx_in,
        out_specs=out_specs + [any_spec] * nx_out, out_shape=out_shape + host.out_shape,
        scratch_shapes=list(scratch_shapes) + host.sems(),
        compiler_params=host.cparams(dimension_semantics=sem, vmem_limit_bytes=VMEM_LIMIT))(*args, *host.ins)
    host.outs = outs[n_out:]
    return outs[0] if single else outs[:n_out]


NN, NT, TN = ((1,), (0,)), ((1,), (1,)), ((0,), (0,))


def _shift_dims(dims, batch):
    if not batch:
        return (dims, ((), ()))
    return (((dims[0][0] + 1,), (dims[1][0] + 1,)), ((0,), (0,)))


def _make_mm(dims, exact, batch=False):
    def raw(a, b, d):
        dn = _shift_dims(d, batch)
        if exact == "split_lhs":
            ah, bh = a.astype(BF16), b.astype(BF16)
            al = (a - ah.astype(F32)).astype(BF16)
            return lax.dot_general(ah, bh, dn, preferred_element_type=F32) + lax.dot_general(al, bh, dn,
                                                                                              preferred_element_type=F32)
        if exact == "split":
            ah, bh = a.astype(BF16), b.astype(BF16)
            al, bl = (a - ah.astype(F32)).astype(BF16), (b - bh.astype(F32)).astype(BF16)
            dot = lambda p, q: lax.dot_general(p, q, dn, preferred_element_type=F32)
            return dot(ah, bh) + (dot(ah, bl) + dot(al, bh))
        if exact:
            return lax.dot_general(a.astype(F32), b.astype(F32), dn, precision=HI, preferred_element_type=F32)
        return lax.dot_general(a.astype(BF16), b.astype(BF16), dn, preferred_element_type=F32)

    @jax.custom_vjp
    def mm(a, b):
        return raw(a, b, dims)

    def fwd(a, b):
        return raw(a, b, dims), (a, b)

    def bwd(res, g):
        a, b = res
        if dims == NN:
            da, db = raw(g, b, NT), raw(a, g, TN)
        elif dims == NT:
            da, db = raw(g, b, NN), raw(g, a, TN)
        else:
            da, db = raw(b, g, NT), raw(a, g, NN)
        return da.astype(a.dtype), db.astype(b.dtype)

    mm.defvjp(fwd, bwd)
    return mm


_mm = _make_mm(NN, False)
_mm_nt = _make_mm(NT, False)
_mm_tn = _make_mm(TN, False)
_mmx = _make_mm(NN, "split_lhs")
_bmm = _make_mm(NN, False, batch=True)
_bmm_nt = _make_mm(NT, False, batch=True)
_bmm_tn = _make_mm(TN, False, batch=True)
_bmmx = _make_mm(NN, True, batch=True)
_bmms = _make_mm(NN, "split", batch=True)
_bmms_nt = _make_mm(NT, "split", batch=True)
_bmms_tn = _make_mm(TN, "split", batch=True)


@jax.custom_vjp
def _unit_lower_inv(a):
    n = a.shape[-1]
    eye = (lax.broadcasted_iota(jnp.int32, a.shape, 1) == lax.broadcasted_iota(jnp.int32, a.shape, 2)).astype(F32)
    x = -a
    t = eye + x
    for _ in range(max(n.bit_length() - 2, 0)):
        x = _bmms(x, x)
        t = t + _bmms(t, x)
    return t


def _unit_lower_inv_fwd(a):
    t = _unit_lower_inv(a)
    return t, t


def _unit_lower_inv_bwd(t, g):
    return (-_bmms_tn(t, _bmms_nt(g, t)),)


_unit_lower_inv.defvjp(_unit_lower_inv_fwd, _unit_lower_inv_bwd)


def _scan_chunk_rows(x, reverse):
    nb, c, w = x.shape
    y = x.reshape(nb * c, w)
    pos = lax.broadcasted_iota(jnp.int32, y.shape, 0) % c
    step = 1
    while step < c:
        if reverse:
            y = y + jnp.where(pos < c - step, pltpu.roll(y, nb * c - step, 0), 0.0)
        else:
            y = y + jnp.where(pos >= step, pltpu.roll(y, step, 0), 0.0)
        step *= 2
    return y.reshape(nb, c, w)


@jax.custom_vjp
def _chunk_cumsum(x):
    return _scan_chunk_rows(x, False)


_chunk_cumsum.defvjp(lambda x: (_scan_chunk_rows(x, False), None), lambda _, g: (_scan_chunk_rows(g, True),))


def _rms(x, w, n):
    ms = jnp.sum(x * x, axis=-1, keepdims=True) * (1.0 / n)
    return x * lax.rsqrt(ms + NORM_EPS) * w


def _silu(x):
    return x * jax.nn.sigmoid(x)


def _softplus(x):
    return jnp.maximum(x, 0.0) + jnp.log(1.0 + jnp.exp(-jnp.abs(x)))


def _rope(x, cos, sin, perm):
    return x * cos + _mmx(x, perm) * sin


def _mla_prep_fn(rows, consts):
    q_lat, kv_lat, k_pe, cos, sin = rows
    qn = _rms(q_lat, consts["qa_w"], LORA)
    kvn = _rms(kv_lat, consts["kva_w"], LORA)
    outs = []
    for h in range(HEADS):
        q_n = _mm_nt(qn, consts["wq_n"][h])
        q_r = _mm_nt(qn, consts["wq_r"][h])
        rs = lax.rsqrt((jnp.sum(q_n * q_n, -1, keepdims=True) + jnp.sum(q_r * q_r, -1, keepdims=True)) * (1.0 / QK_DIM)
                       + NORM_EPS)
        q_n = q_n * rs * consts["qn_n"]
        q_r = _rope(q_r * rs * consts["qn_r"], cos, sin, consts["perm"])
        k_n = _mm_nt(kvn, consts["wk_n"][h])
        v = _mm_nt(kvn, consts["wv"][h])
        rk = lax.rsqrt((jnp.sum(k_n * k_n, -1, keepdims=True) + jnp.sum(k_pe * k_pe, -1, keepdims=True)) * (1.0 / QK_DIM)
                       + NORM_EPS)
        k_n = k_n * rk * consts["kn_n"]
        k_r = _rope(k_pe * rk * consts["kn_r"], cos, sin, consts["perm"])
        outs += [q_n, q_r, k_n, k_r, v]
    return tuple(outs)


def _attn_fn(q, k, v, row0):
    s = _mm_nt(q, k) * (1.0 / math.sqrt(QK_DIM))
    qpos = row0 + lax.broadcasted_iota(jnp.int32, s.shape, 0)
    kpos = lax.broadcasted_iota(jnp.int32, s.shape, 1)
    s = jnp.where(kpos <= qpos, s, -1e30)
    m = lax.stop_gradient(jnp.max(s, axis=-1, keepdims=True))
    p = jnp.exp(s - m)
    p = p / jnp.sum(p, axis=-1, keepdims=True)
    return _mm(p, v)


def _dn_prep_fn(rows, consts):
    qc, kc, ab = rows
    a_b = _mmx(ab, consts["sel_a"])
    b_b = _mmx(ab, consts["sel_b"])
    beta = jax.nn.sigmoid(b_b)
    g = -jnp.exp(consts["alog"]) * _softplus(a_b + consts["dtb"])
    qs, ks = [], []
    for h in range(HEADS):
        q, k = qc[h], kc[h]
        qs.append(q * lax.rsqrt(jnp.sum(q * q, -1, keepdims=True) + NORM_EPS))
        ks.append(k * lax.rsqrt(jnp.sum(k * k, -1, keepdims=True) + NORM_EPS))
    return tuple(qs), tuple(ks), g, beta


def _dn_chunk_fn(q, k, v, gb, g64, bb):
    nb = q.shape[0]
    ri = lax.broadcasted_iota(jnp.int32, (nb, CHUNK, CHUNK), 1)
    ci = lax.broadcasted_iota(jnp.int32, (nb, CHUNK, CHUNK), 2)
    tri = ri >= ci
    strict = ri > ci
    tril = tri.astype(F32)
    eye = (ri == ci).astype(F32)
    ones = jnp.ones((nb, CHUNK, CHUNK), F32)
    gc = _chunk_cumsum(gb)
    gc64 = _chunk_cumsum(g64)
    grow = _bmmx(ones, eye * gc64)
    diff = gc64 - grow
    decay = jnp.where(tri, jnp.exp(jnp.where(tri, diff, 0.0)), 0.0)
    kb = k * bb
    vb = v * bb
    a = jnp.where(strict, _bmm_nt(kb, k) * decay, 0.0)
    tinv = _unit_lower_inv(a)
    u = _bmm(tinv, vb)
    w = _bmm(tinv, kb * jnp.exp(gc))
    qs = q * (1.0 / math.sqrt(HEAD))
    qk = _bmm_nt(qs, k) * decay
    qg = qs * jnp.exp(gc)
    glast = jnp.sum(gb, axis=1, keepdims=True)
    kdec = k * jnp.exp(glast - gc)
    n_mat = _bmm_tn(kdec, w)
    b_mat = _bmm_tn(kdec, u)
    q_eff = qg - _bmm(qk, w)
    o_own = _bmm(qk, u)
    return n_mat, b_mat, q_eff, o_own, jnp.exp(glast)


def _dn_rec_fn(s, n_mat, b_mat, eg):
    return s * eg - _mm(n_mat, s) + b_mat


def _dn_o_fn(s, q_eff, o_own):
    return _bmm(q_eff, s) + o_own


def _dn_out_fn(o, z, w):
    return _rms(o, w, HEAD) * _silu(z)


def _row_tile(t, parts=8):
    return t // parts if (t // parts) % 16 == 0 else t


def _tile(n, pref, unit):
    best = n
    for cand in range(unit, min(n, pref) + 1, unit):
        if n % cand == 0:
            best = cand
    return best if best <= pref else n


def _rows_call(name, body, rows, consts, outs, accs, r, host=None):
    rows = [a if isinstance(a, tuple) else (a, a.shape[1], 0) for a in rows]
    t = rows[0][0].shape[0]
    zero = lambda nd: (lambda i: (0,) * nd)
    in_specs = [pl.BlockSpec((r, w), functools.partial(lambda i, b: (i, b), b=blk)) for _, w, blk in rows]
    rows = [a for a, _, _ in rows]
    in_specs += [pl.BlockSpec(a.shape, zero(a.ndim)) for a in consts]
    out_shape = [jax.ShapeDtypeStruct((t, w), dt) for w, dt in outs] + [jax.ShapeDtypeStruct(s, F32) for s in accs]
    out_specs = [pl.BlockSpec((r, w), lambda i: (i, 0)) for w, _ in outs] + [pl.BlockSpec(s, zero(len(s))) for s in accs]
    return _pcall(body, name, (t // r,), in_specs, out_specs, out_shape, [*rows, *consts], ("arbitrary",), host=host)


def _accumulate(ref, val):
    @pl.when(pl.program_id(0) == 0)
    def _():
        ref[...] = jnp.zeros(ref.shape, ref.dtype)

    ref[...] += val


def _matmul(name, a, b, dims, out_dtype, res=None, host=None):
    if dims == "nn":
        (m, k), n = a.shape, b.shape[1]
    elif dims == "nt":
        (m, k), n = a.shape, b.shape[0]
    else:
        (k, m), n = a.shape, b.shape[1]
    tm = _tile(m, 1100, 16) if dims != "tn" else _tile(m, 640, 128)
    tn = _tile(n, 1408, 128)
    if dims == "nn":
        a_spec, b_spec, dn = pl.BlockSpec((tm, k), lambda i, j: (i, 0)), pl.BlockSpec((k, tn), lambda i, j: (0, j)), NN
    elif dims == "nt":
        a_spec, b_spec, dn = pl.BlockSpec((tm, k), lambda i, j: (i, 0)), pl.BlockSpec((tn, k), lambda i, j: (j, 0)), NT
    else:
        a_spec, b_spec, dn = pl.BlockSpec((k, tm), lambda i, j: (0, i)), pl.BlockSpec((k, tn), lambda i, j: (0, j)), TN
    o_spec = pl.BlockSpec((tm, tn), lambda i, j: (i, j))

    def body(*refs):
        a_ref, b_ref, o_ref = refs[0], refs[1], refs[-1]
        acc = lax.dot_general(a_ref[...].astype(BF16), b_ref[...].astype(BF16), (dn, ((), ())),
                              preferred_element_type=F32)
        if res is not None:
            acc = acc + refs[2][...]
        o_ref[...] = acc.astype(out_dtype)

    ins = [a, b] + ([res] if res is not None else [])
    specs = [a_spec, b_spec] + ([o_spec] if res is not None else [])
    return _pcall(body, name, (m // tm, n // tn), specs, o_spec, jax.ShapeDtypeStruct((m, n), out_dtype), ins,
                  ("arbitrary", "arbitrary"), host=host)


def _rms_fwd(name, h, w, host=None):
    n = h.shape[1]

    def body(h_ref, w_ref, o_ref):
        o_ref[...] = _rms(h_ref[...], w_ref[...], n).astype(BF16)

    return _rows_call(name, body, [h], [w], [(n, BF16)], [], _row_tile(h.shape[0]), host=host)[0]


def _rms_bwd(name, h, w, cts, resid, host=None):
    n = h.shape[1]
    nct = len(cts)

    def body(*refs):
        h_ref, ct_refs, r_ref, w_ref = refs[0], refs[1:1 + nct], refs[1 + nct], refs[2 + nct]
        dh_ref, dh16_ref, dw_ref = refs[-3], refs[-2], refs[-1]
        ct = ct_refs[0][...].astype(F32)
        for c in ct_refs[1:]:
            ct = ct + c[...].astype(F32)
        _, vjp = jax.vjp(lambda x, ww: _rms(x, ww, n), h_ref[...], w_ref[...])
        dh, dw = vjp(ct)
        dh = dh + r_ref[...]
        dh_ref[...] = dh
        dh16_ref[...] = dh.astype(BF16)
        _accumulate(dw_ref, dw)

    return _rows_call(name, body, [h, *cts, resid], [w], [(n, F32), (n, BF16)], [(1, n)], _row_tile(h.shape[0]), host=host)


def _mla_consts_from_refs(qa, wq, kva, wkv, qn, kn, perm):
    f = lambda r: r[...].astype(F32)
    return dict(
        qa_w=f(qa), kva_w=f(kva), perm=f(perm),
        wq_n=[wq[h * QK_PAD:h * QK_PAD + HEAD, :].astype(F32) for h in range(HEADS)],
        wq_r=[wq[h * QK_PAD + HEAD:(h + 1) * QK_PAD, :].astype(F32) for h in range(HEADS)],
        wk_n=[wkv[h * QK_PAD:h * QK_PAD + HEAD, :].astype(F32) for h in range(HEADS)],
        wv=[wkv[h * QK_PAD + HEAD:(h + 1) * QK_PAD, :].astype(F32) for h in range(HEADS)],
        qn_n=qn[:, 0:HEAD], qn_r=qn[:, HEAD:QK_PAD], kn_n=kn[:, 0:HEAD], kn_r=kn[:, HEAD:QK_PAD])


def _mla_prep_fwd(q_lat, kv_lat, k_pe, cos, sin, qa, wq, kva, wkv, qn, kn, perm):
    def body(ql, kvl, kp, c, s, qa_r, wq_r, kva_r, wkv_r, qn_r, kn_r, p_r, q_out, k_out, v_out):
        consts = _mla_consts_from_refs(qa_r, wq_r, kva_r, wkv_r, qn_r, kn_r, p_r)
        outs = _mla_prep_fn((ql[...], kvl[...], kp[...], c[...], s[...]), consts)
        for h in range(HEADS):
            q_n, q_r, k_n, k_r, v = outs[5 * h:5 * h + 5]
            q_out[:, h * QK_PAD:h * QK_PAD + HEAD] = q_n.astype(BF16)
            q_out[:, h * QK_PAD + HEAD:(h + 1) * QK_PAD] = q_r.astype(BF16)
            k_out[:, h * QK_PAD:h * QK_PAD + HEAD] = k_n.astype(BF16)
            k_out[:, h * QK_PAD + HEAD:(h + 1) * QK_PAD] = k_r.astype(BF16)
            v_out[:, h * HEAD:(h + 1) * HEAD] = v.astype(BF16)

    return _rows_call("mla_prep_fwd", body, [q_lat, kv_lat, k_pe, cos, sin], [qa, wq, kva, wkv, qn, kn, perm],
                      [(HEADS * QK_PAD, BF16), (HEADS * QK_PAD, BF16), (DN_WIDTH, BF16)], [], _row_tile(cos.shape[0], 4))


def _mla_prep_bwd(q_lat, kv_lat, k_pe, cos, sin, dq, dk, dv, qa, wq, kva, wkv, qn, kn, perm, host=None):
    def body(ql, kvl, kp, c, s, dq_r, dk_r, dv_r, qa_r, wq_r, kva_r, wkv_r, qn_r, kn_r, p_r,
             dql, dkvl, dkp, dqa, dwq, dkva, dwkv, dqn, dkn):
        consts = _mla_consts_from_refs(qa_r, wq_r, kva_r, wkv_r, qn_r, kn_r, p_r)
        cc, ss, pm = c[...], s[...], consts.pop("perm")
        _, vjp = jax.vjp(lambda rows, cs: _mla_prep_fn((*rows, cc, ss), dict(cs, perm=pm)), (ql[...], kvl[...], kp[...]),
                         consts)
        cts = []
        for h in range(HEADS):
            cts += [dq_r[:, h * QK_PAD:h * QK_PAD + HEAD], dq_r[:, h * QK_PAD + HEAD:(h + 1) * QK_PAD],
                    dk_r[:, h * QK_PAD:h * QK_PAD + HEAD], dk_r[:, h * QK_PAD + HEAD:(h + 1) * QK_PAD],
                    dv_r[:, h * HEAD:(h + 1) * HEAD]]
        (d_ql, d_kvl, d_kp), dc = vjp(tuple(cts))
        dql[...] = d_ql.astype(BF16)
        dkvl[...] = d_kvl.astype(BF16)
        dkp[...] = d_kp.astype(BF16)
        first = pl.program_id(0) == 0

        def acc(ref, sl, val):
            @pl.when(first)
            def _():
                ref[sl] = val

            @pl.when(jnp.logical_not(first))
            def _():
                ref[sl] += val

        full = (slice(None), slice(None))
        acc(dqa, full, dc["qa_w"])
        acc(dkva, full, dc["kva_w"])
        for h in range(HEADS):
            acc(dwq, (slice(h * QK_PAD, h * QK_PAD + HEAD), slice(None)), dc["wq_n"][h])
            acc(dwq, (slice(h * QK_PAD + HEAD, (h + 1) * QK_PAD), slice(None)), dc["wq_r"][h])
            acc(dwkv, (slice(h * QK_PAD, h * QK_PAD + HEAD), slice(None)), dc["wk_n"][h])
            acc(dwkv, (slice(h * QK_PAD + HEAD, (h + 1) * QK_PAD), slice(None)), dc["wv"][h])
        acc(dqn, (slice(None), slice(0, HEAD)), dc["qn_n"])
        acc(dqn, (slice(None), slice(HEAD, QK_PAD)), dc["qn_r"])
        acc(dkn, (slice(None), slice(0, HEAD)), dc["kn_n"])
        acc(dkn, (slice(None), slice(HEAD, QK_PAD)), dc["kn_r"])

    return _rows_call("mla_prep_bwd", body, [q_lat, kv_lat, k_pe, cos, sin, dq, dk, dv],
                      [qa, wq, kva, wkv, qn, kn, perm],
                      [(LORA, BF16), (LORA, BF16), (HEAD, BF16)],
                      [(1, LORA), wq.shape, (1, LORA), wkv.shape, (1, QK_PAD), (1, QK_PAD)], _row_tile(cos.shape[0], 4),
                      host=host)


ATTN_Q_ROWS = 256


def _attn_blocks(t):
    return [(r0, min(ATTN_Q_ROWS, t - r0)) for r0 in range(0, t, ATTN_Q_ROWS)]


def _attn_fwd(q, k, v, host=None):
    t = q.shape[0]

    def body(q_ref, k_ref, v_ref, o_ref):
        for r0, rows in _attn_blocks(t):
            ext = r0 + rows
            o_ref[r0:ext, :] = _attn_fn(q_ref[r0:ext, :], k_ref[0:ext, :], v_ref[0:ext, :], r0)

    qk_spec = pl.BlockSpec((t, QK_PAD), lambda h: (0, h))
    v_spec = pl.BlockSpec((t, HEAD), lambda h: (0, h))
    return _pcall(body, "attn_fwd", (HEADS,), [qk_spec, qk_spec, v_spec], v_spec,
                  jax.ShapeDtypeStruct((t, HEADS * HEAD), F32), [q, k, v], ("arbitrary",), host=host)


def _attn_bwd(q, k, v, do, host=None):
    t = q.shape[0]

    def body(q_ref, k_ref, v_ref, do_ref, dq_ref, dk_ref, dv_ref):
        dk_ref[...] = jnp.zeros(dk_ref.shape, F32)
        dv_ref[...] = jnp.zeros(dv_ref.shape, F32)
        for r0, rows in _attn_blocks(t):
            ext = r0 + rows
            _, vjp = jax.vjp(functools.partial(_attn_fn, row0=r0), q_ref[r0:ext, :].astype(F32),
                             k_ref[0:ext, :].astype(F32), v_ref[0:ext, :].astype(F32))
            dq, dk, dv = vjp(do_ref[r0:ext, :])
            dq_ref[r0:ext, :] = dq
            dk_ref[0:ext, :] += dk
            dv_ref[0:ext, :] += dv

    qk_spec = pl.BlockSpec((t, QK_PAD), lambda h: (0, h))
    v_spec = pl.BlockSpec((t, HEAD), lambda h: (0, h))
    return _pcall(body, "attn_bwd", (HEADS,), [qk_spec, qk_spec, v_spec, v_spec], [qk_spec, qk_spec, v_spec],
                  [jax.ShapeDtypeStruct((t, HEADS * QK_PAD), F32), jax.ShapeDtypeStruct((t, HEADS * QK_PAD), F32),
                   jax.ShapeDtypeStruct((t, HEADS * HEAD), F32)], [q, k, v, do], ("arbitrary",), host=host)


def _mix_out_proj(o_mla, o_dn, z, w_mla, w_dn, w_out, h0, w_ffn):
    def body(om_ref, od_ref, z_ref, h0_ref, wm_ref, wd_ref, wo_ref, wf_ref, mixed_ref, h1_ref, n2_ref):
        for h in range(HEADS):
            sl = slice(h * HEAD, (h + 1) * HEAD)
            mixed_ref[:, sl] = _rms(om_ref[:, sl], wm_ref[...], HEAD).astype(BF16)
            mixed_ref[:, DN_WIDTH + h * HEAD:DN_WIDTH + (h + 1) * HEAD] = _dn_out_fn(od_ref[:, sl], z_ref[:, sl],
                                                                                     wd_ref[...]).astype(BF16)
        h1 = _mm(mixed_ref[...], wo_ref[...]) + h0_ref[...]
        h1_ref[...] = h1
        n2_ref[...] = _rms(h1, wf_ref[...], D_MODEL).astype(BF16)

    return _rows_call("mix_out_proj", body, [o_mla, o_dn, z, h0], [w_mla, w_dn, w_out, w_ffn],
                      [(D_MODEL, BF16), (D_MODEL, F32), (D_MODEL, BF16)], [], _row_tile(o_mla.shape[0], 4))


def _down_proj_loss(act, w_down, h1, tgt, n_valid):
    t, n = h1.shape
    r = _row_tile(t, 4)

    def body(a_ref, h_ref, t_ref, w_ref, dy_ref, dy16_ref, acc_ref):
        h2 = _mm(a_ref[...], w_ref[...]) + h_ref[...]
        rows = pl.program_id(0) * r + lax.broadcasted_iota(jnp.int32, (r, n), 0)
        valid = jnp.logical_and(rows >= N_META, rows < n_valid)
        e = jnp.where(valid, h2 - t_ref[...], 0.0)
        dy = e * (1.0 / n)
        dy_ref[...] = dy
        dy16_ref[...] = dy.astype(BF16)
        _accumulate(acc_ref, jnp.sum(e * e, axis=0, keepdims=True))

    return _rows_call("down_proj_loss", body, [act, h1, tgt], [w_down], [(n, F32), (n, BF16)], [(1, n)], r)


def _ffn_in_bwd(dgpre, dup, w_gate_t, w_up_t, h1, dy, w_ffn, host=None):
    n = h1.shape[1]

    def body(dg_ref, du_ref, h_ref, dy_ref, wg_ref, wu_ref, w_ref, dh_ref, dh16_ref, dw_ref):
        ct = _mm(dg_ref[...], wg_ref[...]) + _mm(du_ref[...], wu_ref[...])
        _, vjp = jax.vjp(lambda x, ww: _rms(x, ww, n), h_ref[...], w_ref[...])
        dh, dw = vjp(ct)
        dh = dh + dy_ref[...]
        dh_ref[...] = dh
        dh16_ref[...] = dh.astype(BF16)
        _accumulate(dw_ref, dw)

    return _rows_call("ffn_in_bwd", body, [dgpre, dup, h1, dy], [w_gate_t, w_up_t, w_ffn], [(n, F32), (n, BF16)], [(1, n)],
                      _row_tile(h1.shape[0]), host=host)


def _mix_out_bwd(o_mla, o_dn, z, dh1, w_out, w_mla, w_dn, host=None):
    def body(om_ref, od_ref, z_ref, dh_ref, wo_ref, wm_ref, wd_ref, dom_ref, dod_ref, dz_ref, dwm_ref, dwd_ref):
        dwm = dwd = None
        for h in range(HEADS):
            sl = slice(h * HEAD, (h + 1) * HEAD)
            _, vjp = jax.vjp(lambda o, w: _rms(o, w, HEAD), om_ref[:, sl], wm_ref[...])
            do, dw = vjp(_mm_nt(dh_ref[...], wo_ref[sl, :]))
            dom_ref[:, sl] = do
            dwm = dw if dwm is None else dwm + dw
            _, vjp = jax.vjp(_dn_out_fn, od_ref[:, sl], z_ref[:, sl], wd_ref[...])
            do, dz, dw = vjp(_mm_nt(dh_ref[...], wo_ref[DN_WIDTH + h * HEAD:DN_WIDTH + (h + 1) * HEAD, :]))
            dod_ref[:, sl] = do
            dz_ref[:, sl] = dz.astype(BF16)
            dwd = dw if dwd is None else dwd + dw
        _accumulate(dwm_ref, dwm)
        _accumulate(dwd_ref, dwd)

    return _rows_call("mix_out_bwd", body, [o_mla, o_dn, z, dh1], [w_out, w_mla, w_dn],
                      [(DN_WIDTH, F32), (DN_WIDTH, F32), (DN_WIDTH, BF16)], [(1, HEAD), (1, HEAD)],
                      _row_tile(o_mla.shape[0], 4), host=host)


def _shift_down(x, s):
    if s == 0:
        return x
    rows = lax.broadcasted_iota(jnp.int32, x.shape, 0)
    return jnp.where(rows >= s, pltpu.roll(x, s, 0), 0.0)


def _shift_up(x, s):
    if s == 0:
        return x
    t = x.shape[0]
    rows = lax.broadcasted_iota(jnp.int32, x.shape, 0)
    return jnp.where(rows < t - s, pltpu.roll(x, t - s, 0), 0.0)


def _col_call(name, body, cols, taps, outs, tap_outs, cw, host=None):
    t, c = cols[0].shape[0], taps[0].shape[1]
    in_specs = [pl.BlockSpec((t, cw), lambda j: (0, j)) for _ in cols]
    in_specs += [pl.BlockSpec((a.shape[0], cw), lambda j: (0, j)) for a in taps]
    out_shape = [jax.ShapeDtypeStruct((t, c), dt) for dt in outs] + [jax.ShapeDtypeStruct((n, c), F32) for n in tap_outs]
    out_specs = [pl.BlockSpec((t, cw), lambda j: (0, j)) for _ in outs]
    out_specs += [pl.BlockSpec((n, cw), lambda j: (0, j)) for n in tap_outs]
    return _pcall(body, name, (c // cw,), in_specs, out_specs, out_shape, [*cols, *taps], ("arbitrary",), host=host)


def _causal_conv(x, w_ref, width, zero_tail=False):
    down = (lambda a, s: pltpu.roll(a, s, 0)) if zero_tail else _shift_down
    acc = w_ref[width - 1:width, :] * x
    for j in range(width - 1):
        acc = acc + w_ref[j:j + 1, :] * down(x, width - 1 - j)
    return acc


def _causal_conv_bwd(x, dpre, w_ref, dx_ref, dw_ref, width, zero_tail=False):
    t = x.shape[0]
    down = (lambda a, s: pltpu.roll(a, s, 0)) if zero_tail else _shift_down
    up = (lambda a, s: pltpu.roll(a, t - s, 0)) if zero_tail else _shift_up
    dx = w_ref[width - 1:width, :] * dpre
    dw_ref[width - 1:width, :] = jnp.sum(dpre * x, axis=0, keepdims=True)
    for j in range(width - 1):
        s = width - 1 - j
        dx = dx + w_ref[j:j + 1, :] * up(dpre, s)
        dw_ref[j:j + 1, :] = jnp.sum(dpre * down(x, s), axis=0, keepdims=True)
    dx_ref[...] = dx.astype(dx_ref.dtype)


def _dsilu(x):
    sg = jax.nn.sigmoid(x)
    return sg * (1.0 + x * (1.0 - sg))


def _dn_conv_fwd(x, w):
    def body(x_ref, w_ref, y_ref):
        y_ref[...] = _silu(_causal_conv(x_ref[...], w_ref, 4, zero_tail=True))

    return _col_call("dn_conv_fwd", body, [x], [w], [F32], [], 256)[0]


def _dn_conv_bwd(x, w, dy):
    def body(x_ref, dy_ref, w_ref, dx_ref, dw_ref):
        xv = x_ref[...]
        dpre = dy_ref[...] * _dsilu(_causal_conv(xv, w_ref, 4, zero_tail=True))
        _causal_conv_bwd(xv, dpre, w_ref, dx_ref, dw_ref, 4, zero_tail=True)

    return _col_call("dn_conv_bwd", body, [x, dy], [w], [BF16], [4], 256)


def _ffn_glu_fwd(n2, w_gate_t, w_up_t, w, b, host=None):
    t, k = n2.shape
    c, cw = w_gate_t.shape[0], 256

    def body(n_ref, wg_ref, wu_ref, w_ref, b_ref, g_ref, u_ref, a_ref):
        nv = n_ref[...]
        g16 = _mm_nt(nv, wg_ref[...]).astype(BF16)
        u16 = _mm_nt(nv, wu_ref[...]).astype(BF16)
        g_ref[...] = g16
        u_ref[...] = u16
        gate = _causal_conv(g16.astype(F32), w_ref, 3) + b_ref[...]
        a_ref[...] = (_silu(gate) * u16.astype(F32)).astype(BF16)

    wspec = pl.BlockSpec((cw, k), lambda j: (j, 0))
    col = pl.BlockSpec((t, cw), lambda j: (0, j))
    in_specs = [pl.BlockSpec((t, k), lambda j: (0, 0)), wspec, wspec, pl.BlockSpec((w.shape[0], cw), lambda j: (0, j)),
                pl.BlockSpec((1, cw), lambda j: (0, j))]
    return _pcall(body, "ffn_glu_fwd", (c // cw,), in_specs, [col] * 3, [jax.ShapeDtypeStruct((t, c), BF16)] * 3,
                  [n2, w_gate_t, w_up_t, w, b], ("arbitrary",), host=host)


def _ffn_glu_bwd(gpre, up, dy16, w_down, w, b):
    t, k = dy16.shape
    c, cw = w_down.shape[0], 256

    def body(g_ref, u_ref, dy_ref, wd_ref, w_ref, b_ref, dg_ref, du_ref, dw_ref, db_ref):
        gv = g_ref[...].astype(F32)
        gate = _causal_conv(gv, w_ref, 3) + b_ref[...]
        da = _mm_nt(dy_ref[...], wd_ref[...])
        sg = jax.nn.sigmoid(gate)
        du_ref[...] = (da * (gate * sg)).astype(BF16)
        dgate = da * u_ref[...].astype(F32) * (sg * (1.0 + gate * (1.0 - sg)))
        db_ref[...] = jnp.sum(dgate, axis=0, keepdims=True)
        _causal_conv_bwd(gv, dgate, w_ref, dg_ref, dw_ref, 3)

    col = pl.BlockSpec((t, cw), lambda j: (0, j))
    taps = lambda rows: pl.BlockSpec((rows, cw), lambda j: (0, j))
    in_specs = [col, col, pl.BlockSpec((t, k), lambda j: (0, 0)), pl.BlockSpec((cw, k), lambda j: (j, 0)),
                taps(w.shape[0]), taps(1)]
    return _pcall(body, "ffn_glu_bwd", (c // cw,), in_specs, [col, col, taps(w.shape[0]), taps(1)],
                  [jax.ShapeDtypeStruct((t, c), BF16)] * 2 + [jax.ShapeDtypeStruct((w.shape[0], c), F32),
                                                             jax.ShapeDtypeStruct((1, c), F32)],
                  [gpre, up, dy16, w_down, w, b], ("arbitrary",))


def _dn_prep_consts(sa, sb, al, dt):
    return dict(sel_a=sa[...], sel_b=sb[...], alog=al[...], dtb=dt[...])


def _dn_prep_fwd(conv, ab, sel_a, sel_b, alog, dtb):
    def body(c_ref, ab_ref, sa, sb, al, dt, q_out, k_out, g_out, b_out):
        qc = tuple(c_ref[:, h * HEAD:(h + 1) * HEAD] for h in range(HEADS))
        kc = tuple(c_ref[:, DN_WIDTH + h * HEAD:DN_WIDTH + (h + 1) * HEAD] for h in range(HEADS))
        qs, ks, g, beta = _dn_prep_fn((qc, kc, ab_ref[...]), _dn_prep_consts(sa, sb, al, dt))
        for h in range(HEADS):
            q_out[:, h * HEAD:(h + 1) * HEAD] = qs[h]
            k_out[:, h * HEAD:(h + 1) * HEAD] = ks[h]
        g_out[...] = g
        b_out[...] = beta

    return _rows_call("dn_prep_fwd", body, [conv, ab], [sel_a, sel_b, alog, dtb], [(DN_WIDTH, F32)] * 4, [],
                      _row_tile(conv.shape[0]))


def _dn_prep_bwd(conv, ab, dq, dk, dv, dg, db, sel_a, sel_b, alog, dtb):
    def body(c_ref, ab_ref, dq_r, dk_r, dv_r, dg_r, db_r, sa, sb, al, dt, dc_out, dab_out, dal_out, ddt_out):
        qc = tuple(c_ref[:, h * HEAD:(h + 1) * HEAD] for h in range(HEADS))
        kc = tuple(c_ref[:, DN_WIDTH + h * HEAD:DN_WIDTH + (h + 1) * HEAD] for h in range(HEADS))
        consts = _dn_prep_consts(sa, sb, al, dt)
        sel = dict(sel_a=consts["sel_a"], sel_b=consts["sel_b"])
        _, vjp = jax.vjp(lambda rows, ad: _dn_prep_fn(rows, {**sel, **ad}), (qc, kc, ab_ref[...]),
                         dict(alog=consts["alog"], dtb=consts["dtb"]))
        cq = tuple(dq_r[:, h * HEAD:(h + 1) * HEAD] for h in range(HEADS))
        ck = tuple(dk_r[:, h * HEAD:(h + 1) * HEAD] for h in range(HEADS))
        (dqc, dkc, dab), dad = vjp((cq, ck, dg_r[...], db_r[...]))
        for h in range(HEADS):
            dc_out[:, h * HEAD:(h + 1) * HEAD] = dqc[h]
            dc_out[:, DN_WIDTH + h * HEAD:DN_WIDTH + (h + 1) * HEAD] = dkc[h]
        dc_out[:, 2 * DN_WIDTH:3 * DN_WIDTH] = dv_r[...]
        dab_out[...] = dab.astype(BF16)
        _accumulate(dal_out, dad["alog"])
        _accumulate(ddt_out, dad["dtb"])

    return _rows_call("dn_prep_bwd", body, [conv, ab, dq, dk, dv, dg, db], [sel_a, sel_b, alog, dtb],
                      [(3 * DN_WIDTH, F32), (HEAD, BF16)], [(1, DN_WIDTH), (1, DN_WIDTH)], _row_tile(conv.shape[0]))


def _chunk_batch(t):
    nc = t // CHUNK
    return nc // 2 if nc % 2 == 0 else nc


def _dn_chunk_specs(t, nb):
    rows = nb * CHUNK
    blk = pl.BlockSpec((rows, HEAD), lambda h, b: (b, h))
    vblk = pl.BlockSpec((rows, HEAD), lambda h, b: (b, 2 * HEADS + h))
    mat = pl.BlockSpec((nb, HEAD, HEAD), lambda h, b: (b, h, 0))
    return rows, blk, vblk, mat


def _dn_chunk_fwd(qn, kn, conv, g, beta, host=None):
    t = qn.shape[0]
    nb = _chunk_batch(t)
    rows, blk, vblk, mat = _dn_chunk_specs(t, nb)

    def body(q_ref, k_ref, v_ref, g_ref, b_ref, n_o, b_o, qe_o, oo_o, eg_o):
        r3 = lambda x: x.reshape(nb, CHUNK, x.shape[-1])
        n_mat, b_mat, q_eff, o_own, eg = _dn_chunk_fn(r3(q_ref[...]), r3(k_ref[...]), r3(v_ref[...]), r3(g_ref[...]),
                                                      r3(g_ref[:, 0:CHUNK]), r3(b_ref[...]))
        n_o[...] = n_mat
        b_o[...] = b_mat
        qe_o[...] = q_eff.reshape(rows, HEAD)
        oo_o[...] = o_own.reshape(rows, HEAD)
        eg_o[...] = jnp.broadcast_to(eg, (nb, HEAD, HEAD))

    nc = t // CHUNK
    mats = jax.ShapeDtypeStruct((nc, DN_WIDTH, HEAD), F32)
    rowsd = jax.ShapeDtypeStruct((t, DN_WIDTH), F32)
    return _pcall(body, "dn_chunk_fwd", (HEADS, t // rows), [blk, blk, vblk, blk, blk], [mat, mat, blk, blk, mat],
                  [mats, mats, rowsd, rowsd, mats], [qn, kn, conv, g, beta], ("arbitrary", "arbitrary"), host=host)


def _dn_chunk_bwd(qn, kn, conv, g, beta, sall, gall, dq_eff, do, host=None):
    t = qn.shape[0]
    nb = _chunk_batch(t)
    rows, blk, vblk, mat = _dn_chunk_specs(t, nb)

    def body(q_ref, k_ref, v_ref, g_ref, b_ref, s_ref, ga_ref, dqe_ref, do_ref, dq_o, dk_o, dv_o, dg_o, db_o):
        r3 = lambda x: x.reshape(nb, CHUNK, x.shape[-1])
        _, vjp = jax.vjp(_dn_chunk_fn, r3(q_ref[...]), r3(k_ref[...]), r3(v_ref[...]), r3(g_ref[...]),
                         r3(g_ref[:, 0:CHUNK]), r3(b_ref[...]))
        s, ga = s_ref[...], ga_ref[...]
        d_n = -_bmm_nt(ga, s)
        d_eg = jnp.sum(ga * s, axis=1, keepdims=True)
        dq, dk, dv, dg, dg64, db = vjp((d_n, ga, r3(dqe_ref[...]), r3(do_ref[...]), d_eg))
        for o_ref, val in zip((dq_o, dk_o, dv_o, dg_o, db_o), (dq, dk, dv, dg, db)):
            o_ref[...] = val.reshape(rows, HEAD)
        dg_o[:, 0:CHUNK] += dg64.reshape(rows, CHUNK)

    return _pcall(body, "dn_chunk_bwd", (HEADS, t // rows), [blk, blk, vblk, blk, blk, mat, mat, blk, blk], [blk] * 5,
                  [jax.ShapeDtypeStruct((t, DN_WIDTH), F32)] * 5, [qn, kn, conv, g, beta, sall, gall, dq_eff, do],
                  ("arbitrary", "arbitrary"), host=host)


def _dn_rec_fwd(n_mat, b_mat, eg, host=None):
    nc = n_mat.shape[0]
    nb = _chunk_batch(nc * CHUNK)
    spec = pl.BlockSpec((nb, DN_WIDTH, HEAD), lambda i: (i, 0, 0))

    def body(n_ref, b_ref, eg_ref, sall_ref, s_scr):
        @pl.when(pl.program_id(0) == 0)
        def _():
            s_scr[...] = jnp.zeros(s_scr.shape, F32)

        for j in range(nb):
            sall_ref[j] = s_scr[...]
            for h in range(HEADS):
                sl = slice(h * HEAD, (h + 1) * HEAD)
                s_scr[sl, :] = _dn_rec_fn(s_scr[sl, :], n_ref[j, sl, :], b_ref[j, sl, :],
                                          eg_ref[j, h * HEAD:h * HEAD + 1, :])

    return _pcall(body, "dn_rec_fwd", (nc // nb,), [spec] * 3, spec, jax.ShapeDtypeStruct((nc, DN_WIDTH, HEAD), F32),
                  [n_mat, b_mat, eg], ("arbitrary",), scratch_shapes=[pltpu.VMEM((DN_WIDTH, HEAD), F32)], host=host)


def _dn_rec_bwd(n_mat, eg, ds_out, host=None):
    nc = n_mat.shape[0]
    nb = _chunk_batch(nc * CHUNK)
    steps = nc // nb
    spec = pl.BlockSpec((nb, DN_WIDTH, HEAD), lambda i: (steps - 1 - i, 0, 0))

    def body(n_ref, eg_ref, dso_ref, gall_ref, g_scr):
        @pl.when(pl.program_id(0) == 0)
        def _():
            g_scr[...] = jnp.zeros(g_scr.shape, F32)

        for j in reversed(range(nb)):
            gall_ref[j] = g_scr[...]
            for h in range(HEADS):
                sl = slice(h * HEAD, (h + 1) * HEAD)
                gv = g_scr[sl, :]
                g_scr[sl, :] = (gv * eg_ref[j, h * HEAD:h * HEAD + 1, :] - _mm_tn(n_ref[j, sl, :], gv)
                                + dso_ref[j, sl, :])

    return _pcall(body, "dn_rec_bwd", (steps,), [spec] * 3, spec, jax.ShapeDtypeStruct((nc, DN_WIDTH, HEAD), F32),
                  [n_mat, eg, ds_out], ("arbitrary",), scratch_shapes=[pltpu.VMEM((DN_WIDTH, HEAD), F32)], host=host)


def _dn_o_fwd(sall, q_eff, o_own):
    t = q_eff.shape[0]
    nb = _chunk_batch(t)
    rows, blk, _, mat = _dn_chunk_specs(t, nb)

    def body(s_ref, qe_ref, oo_ref, o_ref):
        r3 = lambda x: x.reshape(nb, CHUNK, HEAD)
        o_ref[...] = _dn_o_fn(s_ref[...], r3(qe_ref[...]), r3(oo_ref[...])).reshape(rows, HEAD)

    return _pcall(body, "dn_o_fwd", (HEADS, t // rows), [mat, blk, blk], blk, jax.ShapeDtypeStruct((t, DN_WIDTH), F32),
                  [sall, q_eff, o_own], ("arbitrary", "arbitrary"))


def _dn_o_bwd(sall, q_eff, do, host=None):
    t = q_eff.shape[0]
    nb = _chunk_batch(t)
    rows, blk, _, mat = _dn_chunk_specs(t, nb)

    def body(s_ref, qe_ref, do_ref, dqe_ref, ds_ref):
        r3 = lambda x: x.reshape(nb, CHUNK, HEAD)
        dov = r3(do_ref[...])
        dqe_ref[...] = _bmm_nt(dov, s_ref[...]).reshape(rows, HEAD)
        ds_ref[...] = _bmm_tn(r3(qe_ref[...]), dov)

    nc = t // CHUNK
    return _pcall(body, "dn_o_bwd", (HEADS, t // rows), [mat, blk, blk], [blk, mat],
                  [jax.ShapeDtypeStruct((t, DN_WIDTH), F32), jax.ShapeDtypeStruct((nc, DN_WIDTH, HEAD), F32)],
                  [sall, q_eff, do], ("arbitrary", "arbitrary"), host=host)


def _adamw_update(w, g, m, v):
    m2 = ADAM_B1 * m + (1.0 - ADAM_B1) * g
    v2 = ADAM_B2 * v + (1.0 - ADAM_B2) * (g * g)
    m_hat = m2 / (1.0 - ADAM_B1 ** ADAM_STEP)
    v_hat = v2 / (1.0 - ADAM_B2 ** ADAM_STEP)
    return -ADAM_LR * (m_hat / (jnp.sqrt(v_hat) + ADAM_EPS) + ADAM_WD * w), m2, v2


def _adamw_small(ws, gs, ms, vs):
    n = len(ws)

    def body(*refs):
        for i in range(n):
            d, m2, v2 = _adamw_update(refs[i][...], refs[n + i][...], refs[2 * n + i][...], refs[3 * n + i][...])
            refs[4 * n + i][...] = d
            refs[5 * n + i][...] = m2
            refs[6 * n + i][...] = v2

    shapes = [jax.ShapeDtypeStruct(a.shape, F32) for a in ws]
    outs = pl.pallas_call(body, name="adamw_small", out_shape=shapes * 3,
                          compiler_params=pltpu.CompilerParams(vmem_limit_bytes=VMEM_LIMIT))(*ws, *gs, *ms, *vs)
    return outs[:n], outs[n:2 * n], outs[2 * n:]


def _adamw_call(name, w, g, m, v, host=None):
    rows, cols = w.shape
    by_rows = rows % 8 == 0

    def body(w_ref, g_ref, m_ref, v_ref, g_out, d_ref, m_out, v_out):
        gv = g_ref[...] if by_rows else g_ref[0:rows, :]
        g_out[...] = gv
        d_ref[...], m_out[...], v_out[...] = _adamw_update(w_ref[...], gv, m_ref[...], v_ref[...])

    if by_rows:
        tr = _tile(rows, 256, 8)
        spec = g_spec = pl.BlockSpec((tr, cols), lambda i: (i, 0))
        grid = (rows // tr,)
    else:
        tc = _tile(cols, 256, 128)
        spec = pl.BlockSpec((rows, tc), lambda j: (0, j))
        g_spec = pl.BlockSpec((g.shape[0], tc), lambda j: (0, j))
        grid = (cols // tc,)
    return _pcall(body, name, grid, [spec, g_spec, spec, spec], [spec] * 4, [jax.ShapeDtypeStruct((rows, cols), F32)] * 4,
                  [w, g, m, v], ("arbitrary",), host=host)


def _adamw_rows3d(name, w, g, m, v):
    rows, _, cols = w.shape
    tr = max(d for d in range(1, 129) if rows % d == 0)

    def body(w_ref, g_ref, m_ref, v_ref, d_ref, m_out, v_out):
        d_ref[...], m_out[...], v_out[...] = _adamw_update(w_ref[...], g_ref[...], m_ref[...], v_ref[...])

    spec = pl.BlockSpec((tr, 1, cols), lambda i: (i, 0, 0))
    return pl.pallas_call(body, name=name, grid=(rows // tr,), in_specs=[spec] * 4, out_specs=[spec] * 3,
                          out_shape=[jax.ShapeDtypeStruct(w.shape, F32)] * 3, compiler_params=_cparams(("arbitrary",)))(
                              w, g, m, v)


def _rope_tables(t):
    half = ROPE // 2
    inv_freq = np.float32(ROPE_THETA) ** (-np.arange(half, dtype=np.float32) / np.float32(half))
    ang = np.arange(t, dtype=np.float32)[:, None] * inv_freq[None, :].astype(np.float32)
    z = np.zeros((t, HEAD - ROPE), np.float32)
    cos = np.concatenate([np.cos(ang), np.cos(ang), z], axis=1).astype(np.float32)
    sin = np.concatenate([np.sin(ang), np.sin(ang), z], axis=1).astype(np.float32)
    k = np.arange(HEAD)[:, None]
    l = np.arange(HEAD)[None, :]
    perm = np.where((l < half) & (k == l + half), -1.0, 0.0) + np.where((l >= half) & (l < ROPE) & (k == l - half), 1.0, 0.0)
    return jnp.asarray(cos), jnp.asarray(sin), jnp.asarray(perm.astype(np.float32))


def _win_to_pad(w):
    z = lambda n: jnp.zeros((n, w.shape[1]), w.dtype)
    return jnp.concatenate([w[576:2112], w[2112:2624], w[0:256], w[256:512], w[512:576], z(64), w[2624:2632], z(120)],
                           axis=0)


def _win_from_pad(g):
    return jnp.concatenate([g[2048:2304], g[2304:2560], g[2560:2624], g[0:1536], g[1536:2048], g[2688:2696]], axis=0)


def _qk_to_pad(w):
    w4 = w.reshape(HEADS, QK_DIM, w.shape[-1])
    return jnp.concatenate([w4, jnp.zeros((HEADS, QK_PAD - QK_DIM, w.shape[-1]), w.dtype)], axis=1).reshape(
        HEADS * QK_PAD, w.shape[-1])


def _qk_from_pad(g):
    return g.reshape(HEADS, QK_PAD, g.shape[-1])[:, :QK_DIM].reshape(HEADS * QK_DIM, g.shape[-1])


def _ff_to_pad(a, axis):
    shape = list(a.shape)
    shape[axis:axis + 1] = [N_CHIPS, FF_SHARD]
    a4 = a.reshape(shape)
    shape[axis + 1] = FF_BLOCK - FF_SHARD
    out = jnp.concatenate([a4, jnp.zeros(shape, a.dtype)], axis=axis + 1)
    shape[axis:axis + 2] = [D_FF_P]
    return out.reshape(shape)


def _ff_from_pad(a, axis):
    shape = list(a.shape)
    shape[axis:axis + 1] = [N_CHIPS, FF_BLOCK]
    a4 = lax.slice_in_dim(a.reshape(shape), 0, FF_SHARD, axis=axis + 1)
    shape[axis:axis + 2] = [D_FF]
    return a4.reshape(shape)


class _LocalPlan:
    def __init__(self, wt):
        self.wt, self.grads = wt, {}

    def weight(self, name):
        return self.wt[name]

    def host(self, point):
        return None

    def grad(self, name, value):
        self.grads[name] = value


def _local_step(x, tgt, wt, plan=None):
    plan = _LocalPlan(wt) if plan is None else plan
    s = x.shape[0]
    n_valid = N_META + s
    t = -(-n_valid // HEAD) * HEAD
    assert t - n_valid >= 3, "the DeltaNet conv kernels rely on at least three zero rows after the sequence"
    zpad = jnp.zeros((t - n_valid, D_MODEL), F32)
    h0 = jnp.concatenate([wt["meta_tokens"], x, zpad], axis=0)
    tgt_p = jnp.concatenate([jnp.zeros((N_META, D_MODEL), F32), tgt, zpad], axis=0)
    cos, sin, perm = _rope_tables(t)
    qn_w = jnp.concatenate([wt["q_norm_w"], jnp.zeros((1, QK_PAD - QK_DIM), F32)], axis=1)
    kn_w = jnp.concatenate([wt["k_norm_w"], jnp.zeros((1, QK_PAD - QK_DIM), F32)], axis=1)
    head_id = jnp.arange(DN_WIDTH)[None, :] // HEAD
    lane = jnp.arange(HEAD)[:, None]
    sel_a = (lane == head_id).astype(F32)
    sel_b = (lane == head_id + HEADS).astype(F32)
    alog = jnp.repeat(wt["dn_A_log"], HEAD, axis=1)
    dtb = jnp.repeat(wt["dn_dt_bias"], HEAD, axis=1)
    conv_w, conv_b = wt["ffn_conv_w"], wt["ffn_conv_b"]

    u = _rms_fwd("attn_norm_fwd", h0, wt["attn_norm_w"], host=plan.host("attn_norm_fwd"))
    win, wq, wkv = plan.weight("w_in_t"), plan.weight("w_q_t"), plan.weight("w_kv_t")
    proj = _matmul("in_proj", u, win, "nt", F32)
    z = (proj, DN_WIDTH, 3)
    q_lat, kv_lat, k_pe, ab = (proj, LORA, 8), (proj, LORA, 9), (proj, HEAD, 20), (proj, HEAD, 21)
    mla_consts = (wt["q_a_norm_w"], wq, wt["kv_a_norm_w"], wkv, qn_w, kn_w, perm)
    q, k, v = _mla_prep_fwd(q_lat, kv_lat, k_pe, cos, sin, *mla_consts)
    o_mla = _attn_fwd(q, k, v, host=plan.host("attn_fwd"))
    conv = _dn_conv_fwd(proj, wt["dn_conv_w"])
    dn_consts = (sel_a, sel_b, alog, dtb)
    qn, kn, g, beta = _dn_prep_fwd(conv, ab, *dn_consts)
    n_mat, b_mat, q_eff, o_own, eg = _dn_chunk_fwd(qn, kn, conv, g, beta, host=plan.host("dn_chunk_fwd"))
    sall = _dn_rec_fwd(n_mat, b_mat, eg)
    o_dn = _dn_o_fwd(sall, q_eff, o_own)
    w_out = plan.weight("w_out")
    mixed, h1, n2 = _mix_out_proj(o_mla, o_dn, z, wt["mla_out_norm_w"], wt["dn_out_norm_w"], w_out, h0, wt["ffn_norm_w"])
    w_gate, w_up = plan.weight("w_gate_t"), plan.weight("w_up_t")
    gpre, up, act = _ffn_glu_fwd(n2, w_gate, w_up, conv_w, conv_b, host=plan.host("ffn_glu_fwd"))
    w_down = plan.weight("w_down")
    dy, dy16, sq = _down_proj_loss(act, w_down, h1, tgt_p, n_valid)

    grads = {}
    plan.grad("w_down", _matmul("down_dw", act, dy16, "tn", BF16))
    dgpre, dup, grads["ffn_conv_w"], grads["ffn_conv_b"] = _ffn_glu_bwd(gpre, up, dy16, w_down, conv_w, conv_b)
    plan.grad("w_gate_t", _matmul("gate_dw", dgpre, n2, "tn", BF16))
    plan.grad("w_up_t", _matmul("up_dw", dup, n2, "tn", BF16))
    dh1, dh1_16, grads["ffn_norm_w"] = _ffn_in_bwd(dgpre, dup, w_gate, w_up, h1, dy, wt["ffn_norm_w"],
                                                   host=plan.host("ffn_in_bwd"))
    plan.grad("w_out", _matmul("out_dw", mixed, dh1_16, "tn", BF16))
    do_mla, do_dn, dz, grads["mla_out_norm_w"], grads["dn_out_norm_w"] = _mix_out_bwd(
        o_mla, o_dn, z, dh1_16, w_out, wt["mla_out_norm_w"], wt["dn_out_norm_w"], host=plan.host("mix_out_bwd"))
    dq_eff, ds_out = _dn_o_bwd(sall, q_eff, do_dn)
    gall = _dn_rec_bwd(n_mat, eg, ds_out)
    dqn, dkn, dv_dn, dg, dbeta = _dn_chunk_bwd(qn, kn, conv, g, beta, sall, gall, dq_eff, do_dn,
                                               host=plan.host("dn_chunk_bwd"))
    dconv, dab, dalog, ddtb = _dn_prep_bwd(conv, ab, dqn, dkn, dv_dn, dg, dbeta, *dn_consts)
    grads["dn_A_log"] = jnp.sum(dalog.reshape(HEADS, HEAD), axis=1)[None, :]
    grads["dn_dt_bias"] = jnp.sum(ddtb.reshape(HEADS, HEAD), axis=1)[None, :]
    ddn_pre, grads["dn_conv_w"] = _dn_conv_bwd(proj, wt["dn_conv_w"], dconv)
    dq, dk, dv = _attn_bwd(q, k, v, do_mla, host=plan.host("attn_bwd"))
    dq_lat, dkv_lat, dk_pe, dqa, dwq, dkva, dwkv, dqnw, dknw = _mla_prep_bwd(
        q_lat, kv_lat, k_pe, cos, sin, dq, dk, dv, *mla_consts, host=plan.host("mla_prep_bwd"))
    grads["q_a_norm_w"], grads["kv_a_norm_w"] = dqa, dkva
    plan.grad("w_q_t", dwq)
    plan.grad("w_kv_t", dwkv)
    grads["q_norm_w"], grads["k_norm_w"] = dqnw[:, :QK_DIM], dknw[:, :QK_DIM]
    dproj = jnp.concatenate([ddn_pre, dz, dq_lat, dkv_lat, dk_pe, dab], axis=1)
    plan.grad("w_in_t", _matmul("in_dw", dproj, u, "tn", F32))
    du = _matmul("in_dx", dproj, win, "nn", BF16, host=plan.host("in_dx"))
    dh0, _, grads["attn_norm_w"] = _rms_bwd("attn_norm_bwd", h0, wt["attn_norm_w"], [du], dh1,
                                            host=plan.host("attn_norm_bwd"))
    grads["meta_tokens"] = dh0[0:N_META]
    if isinstance(plan, _LocalPlan):
        grads.update(plan.grads)
    return sq, dh0[N_META:n_valid], grads


def _mesh_pos():
    return lax.axis_index("x"), lax.axis_index("y"), lax.axis_index("c")


def _other_chips(x, y):
    return [(1 - x, y), (x, 1 - y), (1 - x, 1 - y)]


def _remote(src, dst, send_sems, recv_sems, k, to):
    return pltpu.make_async_remote_copy(src_ref=src, dst_ref=dst, send_sem=send_sems.at[k], recv_sem=recv_sems.at[k],
                                        device_id=to, device_id_type=MESH)


SIBLING_ID, CHIPS_ID, GATHER_ID, ALL_ID = 1, 2, 3, 4


def _sibling_peer():
    x, y, c = _mesh_pos()
    return [(x, y, 1 - c)]


def _chip_peers():
    x, y, c = _mesh_pos()
    return [(qx, qy, c) for qx, qy in _other_chips(x, y)]


def _copies_exchange(make, ins, out_shape, nsem, peers=None, cid=None):
    def prog(in_refs, out_refs, send_sems, recv_sems):
        copies = make(in_refs, out_refs, send_sems, recv_sems)

        def start():
            for cp in copies:
                cp.start()

        def finish():
            for cp in copies:
                cp.wait()

        return start, finish

    return _Exchange(prog, ins, out_shape, nsem, peers, cid)


def _all_gather(shards):
    def prog(srcs, dsts, send_sems, recv_sems):
        x, y, c = _mesh_pos()
        p = 2 * x + y
        sibling = (x, y, 1 - c)
        chips = _other_chips(x, y)
        bufs = tuple((s, d, s.shape[0] // 2) for s, d in zip(srcs, dsts))

        def half(ref, rows, which):
            return ref.at[pl.ds(which * rows, rows), :]

        def copy(i, k, src, dst, to):
            return _remote(src, dst, send_sems, recv_sems, 6 * i + k, to)

        sends = [copy(i, j, half(src, rows, c), half(dst.at[p], rows, c), (*chip, c))
                 for i, (src, dst, rows) in enumerate(bufs) for j, chip in enumerate(chips)]

        def start():
            for cp in sends:
                cp.start()

        def finish():
            passed = []
            for i, (src, dst, rows) in enumerate(bufs):
                for j, (qx, qy) in enumerate(chips):
                    block = half(dst.at[2 * qx + qy], rows, c)
                    copy(i, j, block, block, (x, y, c)).wait_recv()
                    fwd = copy(i, 3 + j, block, block, sibling)
                    fwd.start()
                    passed.append(fwd)
            for i, (src, dst, rows) in enumerate(bufs):
                for j, (qx, qy) in enumerate(chips):
                    block = half(dst.at[2 * qx + qy], rows, 1 - c)
                    copy(i, 3 + j, block, block, (x, y, c)).wait_recv()
            for cp in sends + passed:
                cp.wait_send()

        return start, finish

    return _Exchange(prog, shards, [jax.ShapeDtypeStruct((N_CHIPS, *s.shape), s.dtype) for s in shards], 6 * len(shards),
                     lambda: _sibling_peer() + _chip_peers(), GATHER_ID)


def _all_gather_small(block):
    def make(srcs, dsts, send_sems, recv_sems):
        x, y, c = _mesh_pos()
        return [_remote(srcs[0], dsts[0].at[2 * x + y], send_sems, recv_sems, k, (qx, qy, c))
                for k, (qx, qy) in enumerate(_other_chips(x, y))]

    return _copies_exchange(make, [block], [jax.ShapeDtypeStruct((N_CHIPS, *block.shape), block.dtype)], 3, _chip_peers,
                            CHIPS_ID)


def _gathered(ex):
    p = 2 * lax.axis_index("x") + lax.axis_index("y")
    return [lax.dynamic_update_slice(g, s[None], (p, 0, 0)) for g, s in zip(ex.outs, ex.ins)]


def _rs_to_sibling(bufs):
    def make(srcs, dsts, send_sems, recv_sems):
        x, y, c = _mesh_pos()
        copies = []
        for i, (src, dst) in enumerate(zip(srcs, dsts)):
            half = src.shape[1] // 2
            copies.append(_remote(src.at[:, pl.ds((1 - c) * half, half), :], dst, send_sems, recv_sems, i, (x, y, 1 - c)))
        return copies

    return _copies_exchange(make, bufs,
                            [jax.ShapeDtypeStruct((N_CHIPS, b.shape[1] // 2, b.shape[2]), b.dtype) for b in bufs],
                            len(bufs), _sibling_peer, SIBLING_ID)


def _rs_pair_add(name, bufs, gots, c, out_dtype):
    n = len(bufs)

    def body(c_ref, *refs):
        for a_ref, b_ref, o_ref in zip(refs[:n], refs[n:2 * n], refs[2 * n:]):
            o_ref[...] = (a_ref[...].astype(F32) + b_ref[...].astype(F32)).astype(out_dtype)

    mine = [pl.BlockSpec((None, g.shape[1], g.shape[2]), lambda j, cr: (j, cr[0], 0)) for g in gots]
    whole = [pl.BlockSpec((None, g.shape[1], g.shape[2]), lambda j, cr: (j, 0, 0)) for g in gots]
    return pl.pallas_call(
        body, name=name,
        grid_spec=pltpu.PrefetchScalarGridSpec(num_scalar_prefetch=1, grid=(N_CHIPS,), in_specs=mine + whole, out_specs=whole),
        out_shape=[jax.ShapeDtypeStruct(g.shape, out_dtype) for g in gots],
        compiler_params=_cparams(("arbitrary",)))(c, *bufs, *gots)


def _rs_to_chips(accs):
    def make(srcs, dsts, send_sems, recv_sems):
        x, y, c = _mesh_pos()
        return [_remote(src.at[2 * qx + qy], dst.at[k], send_sems, recv_sems, 3 * i + k, (qx, qy, c))
                for i, (src, dst) in enumerate(zip(srcs, dsts)) for k, (qx, qy) in enumerate(_other_chips(x, y))]

    return _copies_exchange(make, accs, [jax.ShapeDtypeStruct((3, a.shape[1], a.shape[2]), a.dtype) for a in accs],
                            3 * len(accs), _chip_peers, CHIPS_ID)


def _rs_chip_add(name, accs, gots, p):
    n = len(accs)
    slot = (0, 1, 0, 2)

    def body(p_ref, *refs):
        me = p_ref[0]
        for own_ref, got_ref, o_ref in zip(refs[:n], refs[n:2 * n], refs[2 * n:]):
            total = None
            for chip in range(N_CHIPS):
                val = own_ref[...].astype(F32)
                for e in (1, 2, 3):
                    val = jnp.where((chip ^ me) == e, got_ref[slot[e]].astype(F32), val)
                total = val if total is None else total + val
            o_ref[...] = total

    own = [pl.BlockSpec((None, a.shape[1], a.shape[2]), lambda i, pr: (pr[0], 0, 0)) for a in accs]
    got = [pl.BlockSpec(g.shape, lambda i, pr: (0, 0, 0)) for g in gots]
    out = [pl.BlockSpec((a.shape[1], a.shape[2]), lambda i, pr: (0, 0)) for a in accs]
    return pl.pallas_call(
        body, name=name,
        grid_spec=pltpu.PrefetchScalarGridSpec(num_scalar_prefetch=1, grid=(1,), in_specs=own + got, out_specs=out),
        out_shape=[jax.ShapeDtypeStruct((a.shape[1], a.shape[2]), F32) for a in accs],
        compiler_params=_cparams(("arbitrary",)))(p, *accs, *gots)


def _rs_share(ress):
    def make(srcs, dsts, send_sems, recv_sems):
        x, y, c = _mesh_pos()
        return [_remote(src, dst, send_sems, recv_sems, i, (x, y, 1 - c)) for i, (src, dst) in enumerate(zip(srcs, dsts))]

    return _copies_exchange(make, ress, [jax.ShapeDtypeStruct(r.shape, F32) for r in ress], len(ress), _sibling_peer,
                            SIBLING_ID)


def _shared(ex):
    south = lax.axis_index("c") == 0
    return [jnp.concatenate([jnp.where(south, r, g), jnp.where(south, g, r)], axis=0) for r, g in zip(ex.ins, ex.outs)]


def _all_to_all_devices(vec):
    def others():
        x, y, c = _mesh_pos()
        return [((1 - x if r & 4 else x), (1 - y if r & 2 else y), (1 - c if r & 1 else c)) for r in range(1, 8)]

    def make(srcs, dsts, send_sems, recv_sems):
        x, y, c = _mesh_pos()
        me = 4 * x + 2 * y + c
        return [_remote(srcs[0], dsts[0].at[me], send_sems, recv_sems, r, peer) for r, peer in enumerate(others())]

    return _copies_exchange(make, [vec], [jax.ShapeDtypeStruct((8, *vec.shape), vec.dtype)], 7, others, ALL_ID)


def _sum_devices(stack):
    def body(s_ref, o_ref):
        total = s_ref[0]
        for d in range(1, 8):
            total = total + s_ref[d]
        o_ref[...] = total

    return pl.pallas_call(body, name="sum_devices", out_shape=jax.ShapeDtypeStruct(stack.shape[1:], F32),
                          compiler_params=pltpu.CompilerParams(vmem_limit_bytes=VMEM_LIMIT))(stack)


def _pad_rows(flat, rows):
    return jnp.concatenate([flat, jnp.zeros((rows * LANES - flat.shape[0],), flat.dtype)]).reshape(rows, LANES)


def _unshard(g4, shape, axis):
    a = g4.reshape(N_CHIPS, *shape)
    if axis == 0:
        return a.reshape(N_CHIPS * shape[0], shape[1])
    return jnp.transpose(a, (1, 0, 2)).reshape(shape[0], N_CHIPS * shape[1])


def _shard4(full, shape, axis):
    if axis == 0:
        return full.reshape(N_CHIPS, shape[0] * shape[1])
    a = full.reshape(shape[0], N_CHIPS, shape[1])
    return jnp.transpose(a, (1, 0, 2)).reshape(N_CHIPS, shape[0] * shape[1])


def _pad_axis0(a, rows):
    return jnp.concatenate([a, jnp.zeros((rows - a.shape[0], *a.shape[1:]), a.dtype)], axis=0)


def _pad_axis1(a, rows):
    return jnp.concatenate([a, jnp.zeros((a.shape[0], rows - a.shape[1], *a.shape[2:]), a.dtype)], axis=1)


def _shard_to_strip(name, w):
    _, (shape, axis, rows) = name, {n: (s, ax, r) for n, s, ax, r in BIG}[name]
    w2 = w.reshape(shape).astype(BF16)
    return _pad_axis0(w2.T if axis == 1 else w2, rows)


LOCAL_NAME = dict(w_in="w_in_t", w_q_b="w_q_t", w_kv_b="w_kv_t", w_out="w_out", w_gate="w_gate_t", w_up="w_up_t",
                  w_down="w_down")


WIN_SEGMENTS = ((576, 2112, 0), (2112, 2624, 1536), (0, 256, 2048), (256, 512, 2304), (512, 576, 2560), (2624, 2632, 2688))


def _strips_to_weight(name, g4):
    if name == "w_in":
        return _win_to_pad(g4[:, :IN_SHARD].reshape(IN_COLS, D_MODEL))
    if name == "w_q_b":
        return _qk_to_pad(g4.reshape(HEADS * QK_DIM, LORA))
    return g4.reshape(N_CHIPS * g4.shape[1], g4.shape[2])


def _grad_to_strips(name, g):
    if name == "w_in":
        strips = []
        for q in range(N_CHIPS):
            pieces = []
            for a, b, local in sorted(WIN_SEGMENTS):
                s, e = max(a, q * IN_SHARD), min(b, (q + 1) * IN_SHARD)
                if s < e:
                    pieces.append(g[local + s - a:local + e - a])
            pieces.append(jnp.zeros((IN_SHARD_P - IN_SHARD, D_MODEL), g.dtype))
            strips.append(jnp.concatenate(pieces, axis=0))
        return jnp.stack(strips)
    if name == "w_q_b":
        return _qk_from_pad(g).reshape(N_CHIPS, QK_DIM, LORA)
    return g.reshape(N_CHIPS, g.shape[0] // N_CHIPS, g.shape[1])


class _MeshPlan:
    LATE = dict(attn_norm_fwd=("w_in", "w_q_b", "w_kv_b"), attn_fwd=("w_up",), dn_chunk_fwd=("w_out", "w_gate"),
                ffn_glu_fwd=("w_down",))
    GROUP_A = ("w_down", "w_gate", "w_up", "w_out")
    GROUP_B = ("w_in", "w_q_b", "w_kv_b")

    def __init__(self, w):
        x, y, c = _mesh_pos()
        self.ci = jnp.reshape(c, (1,)).astype(jnp.int32)
        self.pi = jnp.reshape(2 * x + y, (1,)).astype(jnp.int32)
        self.strip = {n: _shard_to_strip(n, w[n]) for n, _, _, _ in BIG}
        self.gathers, self.weights, self.g, self.acc, self.reduced = {}, {}, {}, {}, {}
        self.sibs, self.sib, self.chip, self.share, self.halves = [], None, None, None, [None, None]

    def gather_small(self, small):
        ex = _all_gather_small(small)
        ex.run("all_gather_small")
        return _gathered(ex)[0]

    def weight(self, local_name):
        if local_name not in self.weights:
            for point, (names, ex) in list(self.gathers.items()):
                if ex.outs is not None:
                    for n, g4 in zip(names, _gathered(ex)):
                        if "/" in n:
                            n, half = n.split("/")
                            self.halves[int(half)] = g4
                            if None in self.halves:
                                continue
                            g4 = jnp.concatenate(self.halves, axis=1)
                        self.weights[LOCAL_NAME[n]] = _strips_to_weight(n, g4)
                    del self.gathers[point]
        return self.weights[local_name]

    def _shard(self, name):
        if "/" not in name:
            return self.strip[name]
        name, half = name.split("/")
        rows = self.strip[name].shape[0] // 2
        return self.strip[name][int(half) * rows:(int(half) + 1) * rows]

    def grad(self, local_name, value):
        name = {v: k for k, v in LOCAL_NAME.items()}[local_name]
        self.g[name] = _grad_to_strips(name, value)

    def _pair_add(self, names, gots):
        accs = _rs_pair_add("rs_pair_add_" + names[0], [self.g[n] for n in names], gots, self.ci, BF16)
        self.acc.update(zip(names, accs))

    def _chip_add(self, names, chip):
        return _rs_chip_add("rs_chip_add_" + names[0], [self.acc[n] for n in names], chip.outs, self.pi)

    def _take_shared(self, names, share):
        for n, strip in zip(names, _shared(share)):
            self.reduced[n] = strip

    def host(self, point):
        a, b = self.GROUP_A, self.GROUP_B
        if point in self.LATE:
            names = self.LATE[point]
            ex = _all_gather([self._shard(n) for n in names])
            self.gathers[point] = (names, ex)
            return ex
        if point in ("ffn_in_bwd", "mix_out_bwd"):
            names = dict(ffn_in_bwd=a[:3], mix_out_bwd=a[3:])[point]
            ex = _rs_to_sibling([self.g[n] for n in names])
            self.sibs.append(ex)
            return ex
        if point == "dn_chunk_bwd":
            self._pair_add(a, [o for ex in self.sibs for o in ex.outs])
            self.chip1 = _rs_to_chips([self.acc[n] for n in a[:2]])
            return self.chip1
        if point == "attn_bwd":
            self.chip2 = _rs_to_chips([self.acc[n] for n in a[2:]])
            return self.chip2
        if point == "mla_prep_bwd":
            ress = self._chip_add(a[:2], self.chip1) + self._chip_add(a[2:], self.chip2)
            self.share = _rs_share(ress)
            return self.share
        if point == "in_dx":
            self._take_shared(a, self.share)
            self.sib = _rs_to_sibling([self.g[n] for n in b])
            return self.sib
        if point == "attn_norm_bwd":
            self._pair_add(b, self.sib.outs)
            self.chip = _rs_to_chips([self.acc[n] for n in b])
            return self.chip
        return None

    def last_share(self):
        self.share = _rs_share(self._chip_add(self.GROUP_B, self.chip))
        return self.share

    def finish(self):
        self._take_shared(self.GROUP_B, self.share)
        return self.reduced


def _strip_to_shard(name, strip):
    shape, axis = {n: (s, ax) for n, s, ax, _ in BIG}[name]
    rows = shape[axis]
    return strip[:rows].T if axis == 1 else strip[:rows]


def kernel(x, meta_tokens, attn_norm_w, w_in, q_a_norm_w, w_q_b, kv_a_norm_w, w_kv_b, q_norm_w, k_norm_w, mla_out_norm_w, dn_conv_w, dn_A_log, dn_dt_bias, dn_out_norm_w, w_out, ffn_norm_w, w_gate, w_up, ffn_conv_w, ffn_conv_b, w_down, loss_target, m_meta_tokens, m_attn_norm_w, m_w_in, m_q_a_norm_w, m_w_q_b, m_kv_a_norm_w, m_w_kv_b, m_q_norm_w, m_k_norm_w, m_mla_out_norm_w, m_dn_conv_w, m_dn_A_log, m_dn_dt_bias, m_dn_out_norm_w, m_w_out, m_ffn_norm_w, m_w_gate, m_w_up, m_ffn_conv_w, m_ffn_conv_b, m_w_down, v_meta_tokens, v_attn_norm_w, v_w_in, v_q_a_norm_w, v_w_q_b, v_kv_a_norm_w, v_w_kv_b, v_q_norm_w, v_k_norm_w, v_mla_out_norm_w, v_dn_conv_w, v_dn_A_log, v_dn_dt_bias, v_dn_out_norm_w, v_w_out, v_ffn_norm_w, v_w_gate, v_w_up, v_ffn_conv_w, v_ffn_conv_b, v_w_down):
    local = dict(locals())
    w = {n: local[n] for n in WEIGHTS}
    m = {n: local["m_" + n] for n in WEIGHTS}
    v = {n: local["v_" + n] for n in WEIGHTS}
    p = 2 * lax.axis_index("x") + lax.axis_index("y")

    plan = _MeshPlan(w)
    wf = _pad_rows(jnp.concatenate([w[n].reshape(-1) for n, _, _ in SMALL_SHARDED]), SMALL_ROWS)
    gf = plan.gather_small(wf).reshape(N_CHIPS, -1)
    full = {}
    off = 0
    for n, s, ax in SMALL_SHARDED:
        full[n] = _unshard(gf[:, off:off + s[0] * s[1]], s, ax)
        off += s[0] * s[1]
    for n, _ in REPLICATED:
        full[n] = w[n]
    full["ffn_conv_w"] = _ff_to_pad(full["ffn_conv_w"], 1)
    full["ffn_conv_b"] = _ff_to_pad(full["ffn_conv_b"], 1)

    sq, grad_x, g = _local_step(x[0], loss_target[0], full, plan)
    g["ffn_conv_w"] = _ff_from_pad(g["ffn_conv_w"], 1)
    g["ffn_conv_b"] = _ff_from_pad(g["ffn_conv_b"], 1)

    small_all = [n for n, _, _ in SMALL_SHARDED] + [n for n, _ in REPLICATED]
    vec = jnp.concatenate([g[n].reshape(-1) for n in small_all] + [jnp.reshape(0.5 / D_MODEL * jnp.sum(sq), (1,))])
    vec = _pad_rows(vec, -(-vec.shape[0] // (8 * LANES)) * 8)
    a2a = _all_to_all_devices(vec)

    gs, delta, new_m, new_v = {}, {}, {}, {}
    big = {n: (s, ax) for n, s, ax, _ in BIG}

    def adamw_big(n, strips, host=None):
        s, ax = big[n]
        if ax == 1 and s[1] % 8:
            there = lambda a: jnp.transpose(a, (2, 0, 1))
            back = lambda a: jnp.transpose(a, (1, 2, 0))
            g3 = strips[n][:s[1]].reshape(s[1], 1, s[0])
            d2, m2, v2 = _adamw_rows3d("adamw_" + n, there(w[n]), g3, there(m[n]), there(v[n]))
            gs[n], delta[n], new_m[n], new_v[n] = back(g3), back(d2), back(m2), back(v2)
            return
        flip = ax == 1 and s[1] % 8 == 0
        there = (lambda a: a.reshape(s).T) if flip else (lambda a: a.reshape(s))
        back = (lambda a: a.T.reshape(w[n].shape)) if flip else (lambda a: a.reshape(w[n].shape))
        strip = strips[n] if flip or ax == 0 else strips[n][:s[1]].T
        g2, d2, m2, v2 = _adamw_call("adamw_" + n, there(w[n]), strip, there(m[n]), there(v[n]), host=host)
        gs[n], delta[n], new_m[n], new_v[n] = back(g2), back(d2), back(m2), back(v2)

    adamw_big("w_down", plan.reduced, host=a2a)
    adamw_big("w_gate", plan.reduced, host=plan.last_share())
    adamw_big("w_up", plan.reduced)
    adamw_big("w_out", plan.reduced)
    strips = plan.finish()
    for n in plan.GROUP_B:
        adamw_big(n, strips)
    me = 4 * lax.axis_index("x") + 2 * lax.axis_index("y") + lax.axis_index("c")
    red = _sum_devices(lax.dynamic_update_slice(a2a.outs[0], vec[None], (me, 0, 0))).reshape(-1)
    off = 0
    for n in small_all:
        tot = red[off:off + g[n].size].reshape(g[n].shape)
        off += g[n].size
        shard = {sn: (s, ax) for sn, s, ax in SMALL_SHARDED}.get(n)
        if shard is not None:
            tot = lax.dynamic_slice_in_dim(tot, p * shard[0][1], shard[0][1], axis=1)
        gs[n] = tot
    loss = red[off]
    two_d = lambda a: a.reshape(a.shape[-2], a.shape[-1])
    outs = _adamw_small([two_d(w[n]) for n in small_all], [two_d(gs[n]) for n in small_all],
                        [two_d(m[n]) for n in small_all], [two_d(v[n]) for n in small_all])
    for i, n in enumerate(small_all):
        for dst, src in ((delta, outs[0]), (new_m, outs[1]), (new_v, outs[2])):
            dst[n] = src[i].reshape(w[n].shape)

    grad_out = [gs[n].reshape(w[n].shape) for n in WEIGHTS]
    return (loss, grad_x[None], *grad_out, *[delta[n] for n in WEIGHTS], *[new_m[n] for n in WEIGHTS],
            *[new_v[n] for n in WEIGHTS])
```

```python
import functools
import math

import jax
import jax.numpy as jnp
import numpy as np
from jax import lax
from jax.experimental import pallas as pl
from jax.experimental.pallas import tpu as pltpu

F32 = jnp.float32
BF16 = jnp.bfloat16
HI = lax.Precision.HIGHEST
MESH = pl.DeviceIdType.MESH

N_META = 16
D_MODEL = 1024
HEADS = 4
HEAD = 128
ROPE = 64
QK_DIM = HEAD + ROPE
QK_PAD = 2 * HEAD
LORA = 256
DN_WIDTH = HEADS * HEAD
CHUNK = 64
D_FF = 2816
N_CHIPS = 4
FF_SHARD = D_FF // N_CHIPS
FF_BLOCK = 768
D_FF_P = N_CHIPS * FF_BLOCK
IN_COLS = 2632
IN_SHARD = IN_COLS // N_CHIPS
IN_SHARD_P = 672
IN_PAD = 2816
NORM_EPS = 1e-6
ROPE_THETA = 10000.0
LANES = 512

ADAM_LR, ADAM_B1, ADAM_B2, ADAM_EPS, ADAM_WD, ADAM_STEP = 0.001, 0.9, 0.999, 1e-08, 0.01, 10

VMEM_LIMIT = 56 * 1024 * 1024

BIG = (("w_in", (1024, 658), 1, IN_SHARD_P), ("w_q_b", (256, 192), 1, 192), ("w_kv_b", (256, 256), 1, 256),
       ("w_out", (256, 1024), 0, 256), ("w_gate", (1024, 704), 1, FF_BLOCK), ("w_up", (1024, 704), 1, FF_BLOCK),
       ("w_down", (704, 1024), 0, FF_BLOCK))
SMALL_SHARDED = (("meta_tokens", (16, 256), 1), ("dn_conv_w", (4, 384), 1), ("ffn_conv_w", (3, 704), 1))
REPLICATED = (("attn_norm_w", 1024), ("q_a_norm_w", 256), ("kv_a_norm_w", 256), ("q_norm_w", 192), ("k_norm_w", 192),
              ("mla_out_norm_w", 128), ("dn_A_log", 4), ("dn_dt_bias", 4), ("dn_out_norm_w", 128), ("ffn_norm_w", 1024),
              ("ffn_conv_b", 2816))
WEIGHTS = ("meta_tokens", "attn_norm_w", "w_in", "q_a_norm_w", "w_q_b", "kv_a_norm_w", "w_kv_b", "q_norm_w", "k_norm_w",
           "mla_out_norm_w", "dn_conv_w", "dn_A_log", "dn_dt_bias", "dn_out_norm_w", "w_out", "ffn_norm_w", "w_gate",
           "w_up", "ffn_conv_w", "ffn_conv_b", "w_down")

SMALL_ROWS = 16
REP_ROWS = 16


def _cparams(sem):
    return pltpu.CompilerParams(dimension_semantics=sem, vmem_limit_bytes=VMEM_LIMIT)


class _Exchange:
    def __init__(self, prog, ins, out_shape, nsem, peers=None, cid=None):
        self.prog, self.ins, self.out_shape, self.nsem = prog, list(ins), list(out_shape), nsem
        self.peers, self.cid = peers, cid
        self.outs = None

    def sems(self):
        return [pltpu.SemaphoreType.DMA((self.nsem,)), pltpu.SemaphoreType.DMA((self.nsem,))]

    def programs(self, in_refs, out_refs, send_sems, recv_sems):
        start, finish = self.prog(in_refs, out_refs, send_sems, recv_sems)
        if self.cid is None:
            return start, finish
        peers = self.peers()

        def shake_and_start():
            barrier = pltpu.get_barrier_semaphore()
            for peer in peers:
                pl.semaphore_signal(barrier, inc=1, device_id=peer, device_id_type=MESH)
            pl.semaphore_wait(barrier, len(peers))
            start()

        return shake_and_start, finish

    def cparams(self, **kw):
        return pltpu.CompilerParams(has_side_effects=True, collective_id=self.cid, **kw)

    def run(self, name):
        any_spec = pl.BlockSpec(memory_space=pl.ANY)
        n = len(self.ins)

        def body(*refs):
            start, finish = self.programs(refs[:n], refs[n:-2], refs[-2], refs[-1])
            start()
            finish()

        self.outs = pl.pallas_call(
            body, name=name, in_specs=[any_spec] * n, out_specs=[any_spec] * len(self.out_shape),
            out_shape=self.out_shape, scratch_shapes=self.sems(), compiler_params=self.cparams())(*self.ins)
        return self.outs


def _pcall(body, name, grid, in_specs, out_specs, out_shape, args, sem, scratch_shapes=(), host=None):
    single = not isinstance(out_shape, (list, tuple))
    out_specs, out_shape = ([out_specs], [out_shape]) if single else (list(out_specs), list(out_shape))
    if host is None:
        outs = pl.pallas_call(body, name=name, grid=grid, in_specs=list(in_specs), out_specs=out_specs, out_shape=out_shape,
                              scratch_shapes=list(scratch_shapes), compiler_params=_cparams(sem))(*args)
        return outs[0] if single else outs
    any_spec = pl.BlockSpec(memory_space=pl.ANY)
    n_in, n_out, n_scr, nx_in, nx_out = len(in_specs), len(out_specs), len(scratch_shapes), len(host.ins), len(host.out_shape)

    def hosted(*refs):
        c_in, x_in = refs[:n_in], refs[n_in:n_in + nx_in]
        o0 = n_in + nx_in
        c_out, x_out = refs[o0:o0 + n_out], refs[o0 + n_out:o0 + n_out + nx_out]
        s0 = o0 + n_out + nx_out
        start, finish = host.programs(x_in, x_out, refs[s0 + n_scr], refs[s0 + n_scr + 1])
        first = functools.reduce(jnp.logical_and, [pl.program_id(d) == 0 for d in range(len(grid))])
        last = functools.reduce(jnp.logical_and, [pl.program_id(d) == grid[d] - 1 for d in range(len(grid))])
        pl.when(first)(start)
        body(*c_in, *c_out, *refs[s0:s0 + n_scr])
        pl.when(last)(finish)

    outs = pl.pallas_call(
        hosted, name=name, grid=grid, in_specs=list(in_specs) + [any_spec] * nx_in,
        out_specs=out_specs + [any_spec] * nx_out, out_shape=out_shape + host.out_shape,
        scratch_shapes=list(scratch_shapes) + host.sems(),
        compiler_params=host.cparams(dimension_semantics=sem, vmem_limit_bytes=VMEM_LIMIT))(*args, *host.ins)
    host.outs = outs[n_out:]
    return outs[0] if single else outs[:n_out]


NN, NT, TN = ((1,), (0,)), ((1,), (1,)), ((0,), (0,))


def _shift_dims(dims, batch):
    if not batch:
        return (dims, ((), ()))
    return (((dims[0][0] + 1,), (dims[1][0] + 1,)), ((0,), (0,)))


def _make_mm(dims, exact, batch=False):
    def raw(a, b, d):
        dn = _shift_dims(d, batch)
        if exact == "split_lhs":
            ah, bh = a.astype(BF16), b.astype(BF16)
            al = (a - ah.astype(F32)).astype(BF16)
            return lax.dot_general(ah, bh, dn, preferred_element_type=F32) + lax.dot_general(al, bh, dn,
                                                                                              preferred_element_type=F32)
        if exact == "split":
            ah, bh = a.astype(BF16), b.astype(BF16)
            al, bl = (a - ah.astype(F32)).astype(BF16), (b - bh.astype(F32)).astype(BF16)
            dot = lambda p, q: lax.dot_general(p, q, dn, preferred_element_type=F32)
            return dot(ah, bh) + (dot(ah, bl) + dot(al, bh))
        if exact:
            return lax.dot_general(a.astype(F32), b.astype(F32), dn, precision=HI, preferred_element_type=F32)
        return lax.dot_general(a.astype(BF16), b.astype(BF16), dn, preferred_element_type=F32)

    @jax.custom_vjp
    def mm(a, b):
        return raw(a, b, dims)

    def fwd(a, b):
        return raw(a, b, dims), (a, b)

    def bwd(res, g):
        a, b = res
        if dims == NN:
            da, db = raw(g, b, NT), raw(a, g, TN)
        elif dims == NT:
            da, db = raw(g, b, NN), raw(g, a, TN)
        else:
            da, db = raw(b, g, NT), raw(a, g, NN)
        return da.astype(a.dtype), db.astype(b.dtype)

    mm.defvjp(fwd, bwd)
    return mm


_mm = _make_mm(NN, False)
_mm_nt = _make_mm(NT, False)
_mm_tn = _make_mm(TN, False)
_mmx = _make_mm(NN, "split_lhs")
_bmm = _make_mm(NN, False, batch=True)
_bmm_nt = _make_mm(NT, False, batch=True)
_bmm_tn = _make_mm(TN, False, batch=True)
_bmmx = _make_mm(NN, True, batch=True)
_bmms = _make_mm(NN, "split", batch=True)
_bmms_nt = _make_mm(NT, "split", batch=True)
_bmms_tn = _make_mm(TN, "split", batch=True)


@jax.custom_vjp
def _unit_lower_inv(a):
    n = a.shape[-1]
    eye = (lax.broadcasted_iota(jnp.int32, a.shape, 1) == lax.broadcasted_iota(jnp.int32, a.shape, 2)).astype(F32)
    x = -a
    t = eye + x
    for _ in range(max(n.bit_length() - 2, 0)):
        x = _bmms(x, x)
        t = t + _bmms(t, x)
    return t


def _unit_lower_inv_fwd(a):
    t = _unit_lower_inv(a)
    return t, t


def _unit_lower_inv_bwd(t, g):
    return (-_bmms_tn(t, _bmms_nt(g, t)),)


_unit_lower_inv.defvjp(_unit_lower_inv_fwd, _unit_lower_inv_bwd)


def _scan_chunk_rows(x, reverse):
    nb, c, w = x.shape
    y = x.reshape(nb * c, w)
    pos = lax.broadcasted_iota(jnp.int32, y.shape, 0) % c
    step = 1
    while step < c:
        if reverse:
            y = y + jnp.where(pos < c - step, pltpu.roll(y, nb * c - step, 0), 0.0)
        else:
            y = y + jnp.where(pos >= step, pltpu.roll(y, step, 0), 0.0)
        step *= 2
    return y.reshape(nb, c, w)


@jax.custom_vjp
def _chunk_cumsum(x):
    return _scan_chunk_rows(x, False)


_chunk_cumsum.defvjp(lambda x: (_scan_chunk_rows(x, False), None), lambda _, g: (_scan_chunk_rows(g, True),))


def _rms(x, w, n):
    ms = jnp.sum(x * x, axis=-1, keepdims=True) * (1.0 / n)
    return x * lax.rsqrt(ms + NORM_EPS) * w


def _silu(x):
    return x * jax.nn.sigmoid(x)


def _softplus(x):
    return jnp.maximum(x, 0.0) + jnp.log(1.0 + jnp.exp(-jnp.abs(x)))


def _rope(x, cos, sin, perm):
    return x * cos + _mmx(x, perm) * sin


def _mla_prep_fn(rows, consts):
    q_lat, kv_lat, k_pe, cos, sin = rows
    qn = _rms(q_lat, consts["qa_w"], LORA)
    kvn = _rms(kv_lat, consts["kva_w"], LORA)
    outs = []
    for h in range(HEADS):
        q_n = _mm_nt(qn, consts["wq_n"][h])
        q_r = _mm_nt(qn, consts["wq_r"][h])
        rs = lax.rsqrt((jnp.sum(q_n * q_n, -1, keepdims=True) + jnp.sum(q_r * q_r, -1, keepdims=True)) * (1.0 / QK_DIM)
                       + NORM_EPS)
        q_n = q_n * rs * consts["qn_n"]
        q_r = _rope(q_r * rs * consts["qn_r"], cos, sin, consts["perm"])
        k_n = _mm_nt(kvn, consts["wk_n"][h])
        v = _mm_nt(kvn, consts["wv"][h])
        rk = lax.rsqrt((jnp.sum(k_n * k_n, -1, keepdims=True) + jnp.sum(k_pe * k_pe, -1, keepdims=True)) * (1.0 / QK_DIM)
                       + NORM_EPS)
        k_n = k_n * rk * consts["kn_n"]
        k_r = _rope(k_pe * rk * consts["kn_r"], cos, sin, consts["perm"])
        outs += [q_n, q_r, k_n, k_r, v]
    return tuple(outs)


def _attn_fn(q, k, v, row0):
    s = _mm_nt(q, k) * (1.0 / math.sqrt(QK_DIM))
    qpos = row0 + lax.broadcasted_iota(jnp.int32, s.shape, 0)
    kpos = lax.broadcasted_iota(jnp.int32, s.shape, 1)
    s = jnp.where(kpos <= qpos, s, -1e30)
    m = lax.stop_gradient(jnp.max(s, axis=-1, keepdims=True))
    p = jnp.exp(s - m)
    p = p / jnp.sum(p, axis=-1, keepdims=True)
    return _mm(p, v)


def _dn_prep_fn(rows, consts):
    qc, kc, ab = rows
    a_b = _mmx(ab, consts["sel_a"])
    b_b = _mmx(ab, consts["sel_b"])
    beta = jax.nn.sigmoid(b_b)
    g = -jnp.exp(consts["alog"]) * _softplus(a_b + consts["dtb"])
    qs, ks = [], []
    for h in range(HEADS):
        q, k = qc[h], kc[h]
        qs.append(q * lax.rsqrt(jnp.sum(q * q, -1, keepdims=True) + NORM_EPS))
        ks.append(k * lax.rsqrt(jnp.sum(k * k, -1, keepdims=True) + NORM_EPS))
    return tuple(qs), tuple(ks), g, beta


def _dn_chunk_fn(q, k, v, gb, g64, bb):
    nb = q.shape[0]
    ri = lax.broadcasted_iota(jnp.int32, (nb, CHUNK, CHUNK), 1)
    ci = lax.broadcasted_iota(jnp.int32, (nb, CHUNK, CHUNK), 2)
    tri = ri >= ci
    strict = ri > ci
    tril = tri.astype(F32)
    eye = (ri == ci).astype(F32)
    ones = jnp.ones((nb, CHUNK, CHUNK), F32)
    gc = _chunk_cumsum(gb)
    gc64 = _chunk_cumsum(g64)
    grow = _bmmx(ones, eye * gc64)
    diff = gc64 - grow
    decay = jnp.where(tri, jnp.exp(jnp.where(tri, diff, 0.0)), 0.0)
    kb = k * bb
    vb = v * bb
    a = jnp.where(strict, _bmm_nt(kb, k) * decay, 0.0)
    tinv = _unit_lower_inv(a)
    u = _bmm(tinv, vb)
    w = _bmm(tinv, kb * jnp.exp(gc))
    qs = q * (1.0 / math.sqrt(HEAD))
    qk = _bmm_nt(qs, k) * decay
    qg = qs * jnp.exp(gc)
    glast = jnp.sum(gb, axis=1, keepdims=True)
    kdec = k * jnp.exp(glast - gc)
    n_mat = _bmm_tn(kdec, w)
    b_mat = _bmm_tn(kdec, u)
    q_eff = qg - _bmm(qk, w)
    o_own = _bmm(qk, u)
    return n_mat, b_mat, q_eff, o_own, jnp.exp(glast)


def _dn_rec_fn(s, n_mat, b_mat, eg):
    return s * eg - _mm(n_mat, s) + b_mat


def _dn_o_fn(s, q_eff, o_own):
    return _bmm(q_eff, s) + o_own


def _dn_out_fn(o, z, w):
    return _rms(o, w, HEAD) * _silu(z)


def _row_tile(t, parts=8):
    return t // parts if (t // parts) % 16 == 0 else t


def _tile(n, pref, unit):
    best = n
    for cand in range(unit, min(n, pref) + 1, unit):
        if n % cand == 0:
            best = cand
    return best if best <= pref else n


def _rows_call(name, body, rows, consts, outs, accs, r, host=None):
    rows = [a if isinstance(a, tuple) else (a, a.shape[1], 0) for a in rows]
    t = rows[0][0].shape[0]
    zero = lambda nd: (lambda i: (0,) * nd)
    in_specs = [pl.BlockSpec((r, w), functools.partial(lambda i, b: (i, b), b=blk)) for _, w, blk in rows]
    rows = [a for a, _, _ in rows]
    in_specs += [pl.BlockSpec(a.shape, zero(a.ndim)) for a in consts]
    out_shape = [jax.ShapeDtypeStruct((t, w), dt) for w, dt in outs] + [jax.ShapeDtypeStruct(s, F32) for s in accs]
    out_specs = [pl.BlockSpec((r, w), lambda i: (i, 0)) for w, _ in outs] + [pl.BlockSpec(s, zero(len(s))) for s in accs]
    return _pcall(body, name, (t // r,), in_specs, out_specs, out_shape, [*rows, *consts], ("arbitrary",), host=host)


def _accumulate(ref, val):
    @pl.when(pl.program_id(0) == 0)
    def _():
        ref[...] = jnp.zeros(ref.shape, ref.dtype)

    ref[...] += val


def _matmul(name, a, b, dims, out_dtype, res=None, host=None):
    if dims == "nn":
        (m, k), n = a.shape, b.shape[1]
    elif dims == "nt":
        (m, k), n = a.shape, b.shape[0]
    else:
        (k, m), n = a.shape, b.shape[1]
    tm = _tile(m, 1100, 16) if dims != "tn" else _tile(m, 640, 128)
    tn = _tile(n, 1408, 128)
    if dims == "nn":
        a_spec, b_spec, dn = pl.BlockSpec((tm, k), lambda i, j: (i, 0)), pl.BlockSpec((k, tn), lambda i, j: (0, j)), NN
    elif dims == "nt":
        a_spec, b_spec, dn = pl.BlockSpec((tm, k), lambda i, j: (i, 0)), pl.BlockSpec((tn, k), lambda i, j: (j, 0)), NT
    else:
        a_spec, b_spec, dn = pl.BlockSpec((k, tm), lambda i, j: (0, i)), pl.BlockSpec((k, tn), lambda i, j: (0, j)), TN
    o_spec = pl.BlockSpec((tm, tn), lambda i, j: (i, j))

    def body(*refs):
        a_ref, b_ref, o_ref = refs[0], refs[1], refs[-1]
        acc = lax.dot_general(a_ref[...].astype(BF16), b_ref[...].astype(BF16), (dn, ((), ())),
                              preferred_element_type=F32)
        if res is not None:
            acc = acc + refs[2][...]
        o_ref[...] = acc.astype(out_dtype)

    ins = [a, b] + ([res] if res is not None else [])
    specs = [a_spec, b_spec] + ([o_spec] if res is not None else [])
    return _pcall(body, name, (m // tm, n // tn), specs, o_spec, jax.ShapeDtypeStruct((m, n), out_dtype), ins,
                  ("arbitrary", "arbitrary"), host=host)


def _rms_fwd(name, h, w, host=None):
    n = h.shape[1]

    def body(h_ref, w_ref, o_ref):
        o_ref[...] = _rms(h_ref[...], w_ref[...], n).astype(BF16)

    return _rows_call(name, body, [h], [w], [(n, BF16)], [], _row_tile(h.shape[0]), host=host)[0]


def _rms_bwd(name, h, w, cts, resid, host=None):
    n = h.shape[1]
    nct = len(cts)

    def body(*refs):
        h_ref, ct_refs, r_ref, w_ref = refs[0], refs[1:1 + nct], refs[1 + nct], refs[2 + nct]
        dh_ref, dh16_ref, dw_ref = refs[-3], refs[-2], refs[-1]
        ct = ct_refs[0][...].astype(F32)
        for c in ct_refs[1:]:
            ct = ct + c[...].astype(F32)
        _, vjp = jax.vjp(lambda x, ww: _rms(x, ww, n), h_ref[...], w_ref[...])
        dh, dw = vjp(ct)
        dh = dh + r_ref[...]
        dh_ref[...] = dh
        dh16_ref[...] = dh.astype(BF16)
        _accumulate(dw_ref, dw)

    return _rows_call(name, body, [h, *cts, resid], [w], [(n, F32), (n, BF16)], [(1, n)], _row_tile(h.shape[0]), host=host)


def _mla_consts_from_refs(qa, wq, kva, wkv, qn, kn, perm):
    f = lambda r: r[...].astype(F32)
    return dict(
        qa_w=f(qa), kva_w=f(kva), perm=f(perm),
        wq_n=[wq[h * QK_PAD:h * QK_PAD + HEAD, :].astype(F32) for h in range(HEADS)],
        wq_r=[wq[h * QK_PAD + HEAD:(h + 1) * QK_PAD, :].astype(F32) for h in range(HEADS)],
        wk_n=[wkv[h * QK_PAD:h * QK_PAD + HEAD, :].astype(F32) for h in range(HEADS)],
        wv=[wkv[h * QK_PAD + HEAD:(h + 1) * QK_PAD, :].astype(F32) for h in range(HEADS)],
        qn_n=qn[:, 0:HEAD], qn_r=qn[:, HEAD:QK_PAD], kn_n=kn[:, 0:HEAD], kn_r=kn[:, HEAD:QK_PAD])


def _mla_prep_fwd(q_lat, kv_lat, k_pe, cos, sin, qa, wq, kva, wkv, qn, kn, perm):
    def body(ql, kvl, kp, c, s, qa_r, wq_r, kva_r, wkv_r, qn_r, kn_r, p_r, q_out, k_out, v_out):
        consts = _mla_consts_from_refs(qa_r, wq_r, kva_r, wkv_r, qn_r, kn_r, p_r)
        outs = _mla_prep_fn((ql[...], kvl[...], kp[...], c[...], s[...]), consts)
        for h in range(HEADS):
            q_n, q_r, k_n, k_r, v = outs[5 * h:5 * h + 5]
            q_out[:, h * QK_PAD:h * QK_PAD + HEAD] = q_n.astype(BF16)
            q_out[:, h * QK_PAD + HEAD:(h + 1) * QK_PAD] = q_r.astype(BF16)
            k_out[:, h * QK_PAD:h * QK_PAD + HEAD] = k_n.astype(BF16)
            k_out[:, h * QK_PAD + HEAD:(h + 1) * QK_PAD] = k_r.astype(BF16)
            v_out[:, h * HEAD:(h + 1) * HEAD] = v.astype(BF16)

    return _rows_call("mla_prep_fwd", body, [q_lat, kv_lat, k_pe, cos, sin], [qa, wq, kva, wkv, qn, kn, perm],
                      [(HEADS * QK_PAD, BF16), (HEADS * QK_PAD, BF16), (DN_WIDTH, BF16)], [], _row_tile(cos.shape[0], 4))


def _mla_prep_bwd(q_lat, kv_lat, k_pe, cos, sin, dq, dk, dv, qa, wq, kva, wkv, qn, kn, perm, host=None):
    def body(ql, kvl, kp, c, s, dq_r, dk_r, dv_r, qa_r, wq_r, kva_r, wkv_r, qn_r, kn_r, p_r,
             dql, dkvl, dkp, dqa, dwq, dkva, dwkv, dqn, dkn):
        consts = _mla_consts_from_refs(qa_r, wq_r, kva_r, wkv_r, qn_r, kn_r, p_r)
        cc, ss, pm = c[...], s[...], consts.pop("perm")
        _, vjp = jax.vjp(lambda rows, cs: _mla_prep_fn((*rows, cc, ss), dict(cs, perm=pm)), (ql[...], kvl[...], kp[...]),
                         consts)
        cts = []
        for h in range(HEADS):
            cts += [dq_r[:, h * QK_PAD:h * QK_PAD + HEAD], dq_r[:, h * QK_PAD + HEAD:(h + 1) * QK_PAD],
                    dk_r[:, h * QK_PAD:h * QK_PAD + HEAD], dk_r[:, h * QK_PAD + HEAD:(h + 1) * QK_PAD],
                    dv_r[:, h * HEAD:(h + 1) * HEAD]]
        (d_ql, d_kvl, d_kp), dc = vjp(tuple(cts))
        dql[...] = d_ql.astype(BF16)
        dkvl[...] = d_kvl.astype(BF16)
        dkp[...] = d_kp.astype(BF16)
        first = pl.program_id(0) == 0

        def acc(ref, sl, val):
            @pl.when(first)
            def _():
                ref[sl] = val

            @pl.when(jnp.logical_not(first))
            def _():
                ref[sl] += val

        full = (slice(None), slice(None))
        acc(dqa, full, dc["qa_w"])
        acc(dkva, full, dc["kva_w"])
        for h in range(HEADS):
            acc(dwq, (slice(h * QK_PAD, h * QK_PAD + HEAD), slice(None)), dc["wq_n"][h])
            acc(dwq, (slice(h * QK_PAD + HEAD, (h + 1) * QK_PAD), slice(None)), dc["wq_r"][h])
            acc(dwkv, (slice(h * QK_PAD, h * QK_PAD + HEAD), slice(None)), dc["wk_n"][h])
            acc(dwkv, (slice(h * QK_PAD + HEAD, (h + 1) * QK_PAD), slice(None)), dc["wv"][h])
        acc(dqn, (slice(None), slice(0, HEAD)), dc["qn_n"])
        acc(dqn, (slice(None), slice(HEAD, QK_PAD)), dc["qn_r"])
        acc(dkn, (slice(None), slice(0, HEAD)), dc["kn_n"])
        acc(dkn, (slice(None), slice(HEAD, QK_PAD)), dc["kn_r"])

    return _rows_call("mla_prep_bwd", body, [q_lat, kv_lat, k_pe, cos, sin, dq, dk, dv],
                      [qa, wq, kva, wkv, qn, kn, perm],
                      [(LORA, BF16), (LORA, BF16), (HEAD, BF16)],
                      [(1, LORA), wq.shape, (1, LORA), wkv.shape, (1, QK_PAD), (1, QK_PAD)], _row_tile(cos.shape[0], 4),
                      host=host)


ATTN_Q_ROWS = 256


def _attn_blocks(t):
    return [(r0, min(ATTN_Q_ROWS, t - r0)) for r0 in range(0, t, ATTN_Q_ROWS)]


def _attn_fwd(q, k, v, host=None):
    t = q.shape[0]

    def body(q_ref, k_ref, v_ref, o_ref):
        for r0, rows in _attn_blocks(t):
            ext = r0 + rows
            o_ref[r0:ext, :] = _attn_fn(q_ref[r0:ext, :], k_ref[0:ext, :], v_ref[0:ext, :], r0)

    qk_spec = pl.BlockSpec((t, QK_PAD), lambda h: (0, h))
    v_spec = pl.BlockSpec((t, HEAD), lambda h: (0, h))
    return _pcall(body, "attn_fwd", (HEADS,), [qk_spec, qk_spec, v_spec], v_spec,
                  jax.ShapeDtypeStruct((t, HEADS * HEAD), F32), [q, k, v], ("arbitrary",), host=host)


def _attn_bwd(q, k, v, do, host=None):
    t = q.shape[0]

    def body(q_ref, k_ref, v_ref, do_ref, dq_ref, dk_ref, dv_ref):
        dk_ref[...] = jnp.zeros(dk_ref.shape, F32)
        dv_ref[...] = jnp.zeros(dv_ref.shape, F32)
        for r0, rows in _attn_blocks(t):
            ext = r0 + rows
            _, vjp = jax.vjp(functools.partial(_attn_fn, row0=r0), q_ref[r0:ext, :].astype(F32),
                             k_ref[0:ext, :].astype(F32), v_ref[0:ext, :].astype(F32))
            dq, dk, dv = vjp(do_ref[r0:ext, :])
            dq_ref[r0:ext, :] = dq
            dk_ref[0:ext, :] += dk
            dv_ref[0:ext, :] += dv

    qk_spec = pl.BlockSpec((t, QK_PAD), lambda h: (0, h))
    v_spec = pl.BlockSpec((t, HEAD), lambda h: (0, h))
    return _pcall(body, "attn_bwd", (HEADS,), [qk_spec, qk_spec, v_spec, v_spec], [qk_spec, qk_spec, v_spec],
                  [jax.ShapeDtypeStruct((t, HEADS * QK_PAD), F32), jax.ShapeDtypeStruct((t, HEADS * QK_PAD), F32),
                   jax.ShapeDtypeStruct((t, HEADS * HEAD), F32)], [q, k, v, do], ("arbitrary",), host=host)


def _mix_out_proj(o_mla, o_dn, z, w_mla, w_dn, w_out, h0, w_ffn):
    def body(om_ref, od_ref, z_ref, h0_ref, wm_ref, wd_ref, wo_ref, wf_ref, mixed_ref, h1_ref, n2_ref):
        for h in range(HEADS):
            sl = slice(h * HEAD, (h + 1) * HEAD)
            mixed_ref[:, sl] = _rms(om_ref[:, sl], wm_ref[...], HEAD).astype(BF16)
            mixed_ref[:, DN_WIDTH + h * HEAD:DN_WIDTH + (h + 1) * HEAD] = _dn_out_fn(od_ref[:, sl], z_ref[:, sl],
                                                                                     wd_ref[...]).astype(BF16)
        h1 = _mm(mixed_ref[...], wo_ref[...]) + h0_ref[...]
        h1_ref[...] = h1
        n2_ref[...] = _rms(h1, wf_ref[...], D_MODEL).astype(BF16)

    return _rows_call("mix_out_proj", body, [o_mla, o_dn, z, h0], [w_mla, w_dn, w_out, w_ffn],
                      [(D_MODEL, BF16), (D_MODEL, F32), (D_MODEL, BF16)], [], _row_tile(o_mla.shape[0], 4))


def _down_proj_loss(act, w_down, h1, tgt, n_valid):
    t, n = h1.shape
    r = _row_tile(t, 4)

    def body(a_ref, h_ref, t_ref, w_ref, dy_ref, dy16_ref, acc_ref):
        h2 = _mm(a_ref[...], w_ref[...]) + h_ref[...]
        rows = pl.program_id(0) * r + lax.broadcasted_iota(jnp.int32, (r, n), 0)
        valid = jnp.logical_and(rows >= N_META, rows < n_valid)
        e = jnp.where(valid, h2 - t_ref[...], 0.0)
        dy = e * (1.0 / n)
        dy_ref[...] = dy
        dy16_ref[...] = dy.astype(BF16)
        _accumulate(acc_ref, jnp.sum(e * e, axis=0, keepdims=True))

    return _rows_call("down_proj_loss", body, [act, h1, tgt], [w_down], [(n, F32), (n, BF16)], [(1, n)], r)


def _in_proj_bwd_x(pieces, win, host=None):
    offs = np.cumsum([0] + [p.shape[1] for p in pieces])

    def body(*refs):
        p_refs, w_ref, o_ref = refs[:len(pieces)], refs[len(pieces)], refs[-1]
        acc = None
        for i, p_ref in enumerate(p_refs):
            part = _mm(p_ref[...], w_ref[int(offs[i]):int(offs[i + 1]), :])
            acc = part if acc is None else acc + part
        o_ref[...] = acc.astype(BF16)

    return _rows_call("in_dx", body, pieces, [win], [(win.shape[1], BF16)], [], _row_tile(pieces[0].shape[0], 4), host=host)[0]


def _in_proj_bwd_w(pieces, u):
    offs = np.cumsum([0] + [p.shape[1] for p in pieces])

    def body(*refs):
        p_refs, u_ref, o_ref = refs[:len(pieces)], refs[len(pieces)], refs[-1]
        first = pl.program_id(0) == 0
        uv = u_ref[...]
        for i, p_ref in enumerate(p_refs):
            rows = slice(int(offs[i]), int(offs[i + 1]))
            part = _mm_tn(p_ref[...], uv)

            @pl.when(first)
            def _():
                o_ref[rows, :] = part

            @pl.when(jnp.logical_not(first))
            def _():
                o_ref[rows, :] += part

    return _rows_call("in_dw", body, [*pieces, u], [], [], [(int(offs[-1]), u.shape[1])], _row_tile(u.shape[0], 4))[0]


def _ffn_in_bwd(dgpre, dup, w_gate_t, w_up_t, h1, dy, w_ffn, host=None):
    n = h1.shape[1]

    def body(dg_ref, du_ref, h_ref, dy_ref, wg_ref, wu_ref, w_ref, dh_ref, dh16_ref, dw_ref):
        ct = _mm(dg_ref[...], wg_ref[...]) + _mm(du_ref[...], wu_ref[...])
        _, vjp = jax.vjp(lambda x, ww: _rms(x, ww, n), h_ref[...], w_ref[...])
        dh, dw = vjp(ct)
        dh = dh + dy_ref[...]
        dh_ref[...] = dh
        dh16_ref[...] = dh.astype(BF16)
        _accumulate(dw_ref, dw)

    return _rows_call("ffn_in_bwd", body, [dgpre, dup, h1, dy], [w_gate_t, w_up_t, w_ffn], [(n, F32), (n, BF16)], [(1, n)],
                      _row_tile(h1.shape[0]), host=host)


def _mix_out_bwd(o_mla, o_dn, z, dh1, w_out, w_mla, w_dn, host=None):
    def body(om_ref, od_ref, z_ref, dh_ref, wo_ref, wm_ref, wd_ref, dom_ref, dod_ref, dz_ref, dwm_ref, dwd_ref):
        dwm = dwd = None
        for h in range(HEADS):
            sl = slice(h * HEAD, (h + 1) * HEAD)
            _, vjp = jax.vjp(lambda o, w: _rms(o, w, HEAD), om_ref[:, sl], wm_ref[...])
            do, dw = vjp(_mm_nt(dh_ref[...], wo_ref[sl, :]))
            dom_ref[:, sl] = do
            dwm = dw if dwm is None else dwm + dw
            _, vjp = jax.vjp(_dn_out_fn, od_ref[:, sl], z_ref[:, sl], wd_ref[...])
            do, dz, dw = vjp(_mm_nt(dh_ref[...], wo_ref[DN_WIDTH + h * HEAD:DN_WIDTH + (h + 1) * HEAD, :]))
            dod_ref[:, sl] = do
            dz_ref[:, sl] = dz.astype(BF16)
            dwd = dw if dwd is None else dwd + dw
        _accumulate(dwm_ref, dwm)
        _accumulate(dwd_ref, dwd)

    return _rows_call("mix_out_bwd", body, [o_mla, o_dn, z, dh1], [w_out, w_mla, w_dn],
                      [(DN_WIDTH, F32), (DN_WIDTH, F32), (DN_WIDTH, BF16)], [(1, HEAD), (1, HEAD)],
                      _row_tile(o_mla.shape[0], 4), host=host)


def _shift_down(x, s):
    if s == 0:
        return x
    rows = lax.broadcasted_iota(jnp.int32, x.shape, 0)
    return jnp.where(rows >= s, pltpu.roll(x, s, 0), 0.0)


def _shift_up(x, s):
    if s == 0:
        return x
    t = x.shape[0]
    rows = lax.broadcasted_iota(jnp.int32, x.shape, 0)
    return jnp.where(rows < t - s, pltpu.roll(x, t - s, 0), 0.0)


def _col_call(name, body, cols, taps, outs, tap_outs, cw, host=None):
    t, c = cols[0].shape[0], taps[0].shape[1]
    in_specs = [pl.BlockSpec((t, cw), lambda j: (0, j)) for _ in cols]
    in_specs += [pl.BlockSpec((a.shape[0], cw), lambda j: (0, j)) for a in taps]
    out_shape = [jax.ShapeDtypeStruct((t, c), dt) for dt in outs] + [jax.ShapeDtypeStruct((n, c), F32) for n in tap_outs]
    out_specs = [pl.BlockSpec((t, cw), lambda j: (0, j)) for _ in outs]
    out_specs += [pl.BlockSpec((n, cw), lambda j: (0, j)) for n in tap_outs]
    return _pcall(body, name, (c // cw,), in_specs, out_specs, out_shape, [*cols, *taps], ("arbitrary",), host=host)


def _causal_conv(x, w_ref, width, zero_tail=False):
    down = (lambda a, s: pltpu.roll(a, s, 0)) if zero_tail else _shift_down
    acc = w_ref[width - 1:width, :] * x
    for j in range(width - 1):
        acc = acc + w_ref[j:j + 1, :] * down(x, width - 1 - j)
    return acc


def _causal_conv_bwd(x, dpre, w_ref, dx_ref, dw_ref, width, zero_tail=False):
    t = x.shape[0]
    down = (lambda a, s: pltpu.roll(a, s, 0)) if zero_tail else _shift_down
    up = (lambda a, s: pltpu.roll(a, t - s, 0)) if zero_tail else _shift_up
    dx = w_ref[width - 1:width, :] * dpre
    dw_ref[width - 1:width, :] = jnp.sum(dpre * x, axis=0, keepdims=True)
    for j in range(width - 1):
        s = width - 1 - j
        dx = dx + w_ref[j:j + 1, :] * up(dpre, s)
        dw_ref[j:j + 1, :] = jnp.sum(dpre * down(x, s), axis=0, keepdims=True)
    dx_ref[...] = dx.astype(dx_ref.dtype)


def _dsilu(x):
    sg = jax.nn.sigmoid(x)
    return sg * (1.0 + x * (1.0 - sg))


def _dn_conv_fwd(x, w):
    def body(x_ref, w_ref, y_ref):
        y_ref[...] = _silu(_causal_conv(x_ref[...], w_ref, 4, zero_tail=True))

    return _col_call("dn_conv_fwd", body, [x], [w], [F32], [], 256)[0]


def _dn_conv_bwd(x, w, dy):
    def body(x_ref, dy_ref, w_ref, dx_ref, dw_ref):
        xv = x_ref[...]
        dpre = dy_ref[...] * _dsilu(_causal_conv(xv, w_ref, 4, zero_tail=True))
        _causal_conv_bwd(xv, dpre, w_ref, dx_ref, dw_ref, 4, zero_tail=True)

    return _col_call("dn_conv_bwd", body, [x, dy], [w], [BF16], [4], 256)


def _ffn_glu_fwd(n2, w_gate_t, w_up_t, w, b, host=None):
    t, k = n2.shape
    c, cw = w_gate_t.shape[0], 256

    def body(n_ref, wg_ref, wu_ref, w_ref, b_ref, g_ref, u_ref, a_ref):
        nv = n_ref[...]
        g16 = _mm_nt(nv, wg_ref[...]).astype(BF16)
        u16 = _mm_nt(nv, wu_ref[...]).astype(BF16)
        g_ref[...] = g16
        u_ref[...] = u16
        gate = _causal_conv(g16.astype(F32), w_ref, 3) + b_ref[...]
        a_ref[...] = (_silu(gate) * u16.astype(F32)).astype(BF16)

    wspec = pl.BlockSpec((cw, k), lambda j: (j, 0))
    col = pl.BlockSpec((t, cw), lambda j: (0, j))
    in_specs = [pl.BlockSpec((t, k), lambda j: (0, 0)), wspec, wspec, pl.BlockSpec((w.shape[0], cw), lambda j: (0, j)),
                pl.BlockSpec((1, cw), lambda j: (0, j))]
    return _pcall(body, "ffn_glu_fwd", (c // cw,), in_specs, [col] * 3, [jax.ShapeDtypeStruct((t, c), BF16)] * 3,
                  [n2, w_gate_t, w_up_t, w, b], ("arbitrary",), host=host)


def _ffn_glu_bwd(gpre, up, dy16, w_down, w, b):
    t, k = dy16.shape
    c, cw = w_down.shape[0], 256

    def body(g_ref, u_ref, dy_ref, wd_ref, w_ref, b_ref, dg_ref, du_ref, dw_ref, db_ref):
        gv = g_ref[...].astype(F32)
        gate = _causal_conv(gv, w_ref, 3) + b_ref[...]
        da = _mm_nt(dy_ref[...], wd_ref[...])
        sg = jax.nn.sigmoid(gate)
        du_ref[...] = (da * (gate * sg)).astype(BF16)
        dgate = da * u_ref[...].astype(F32) * (sg * (1.0 + gate * (1.0 - sg)))
        db_ref[...] = jnp.sum(dgate, axis=0, keepdims=True)
        _causal_conv_bwd(gv, dgate, w_ref, dg_ref, dw_ref, 3)

    col = pl.BlockSpec((t, cw), lambda j: (0, j))
    taps = lambda rows: pl.BlockSpec((rows, cw), lambda j: (0, j))
    in_specs = [col, col, pl.BlockSpec((t, k), lambda j: (0, 0)), pl.BlockSpec((cw, k), lambda j: (j, 0)),
                taps(w.shape[0]), taps(1)]
    return _pcall(body, "ffn_glu_bwd", (c // cw,), in_specs, [col, col, taps(w.shape[0]), taps(1)],
                  [jax.ShapeDtypeStruct((t, c), BF16)] * 2 + [jax.ShapeDtypeStruct((w.shape[0], c), F32),
                                                             jax.ShapeDtypeStruct((1, c), F32)],
                  [gpre, up, dy16, w_down, w, b], ("arbitrary",))


def _dn_prep_consts(sa, sb, al, dt):
    return dict(sel_a=sa[...], sel_b=sb[...], alog=al[...], dtb=dt[...])


def _dn_prep_fwd(conv, ab, sel_a, sel_b, alog, dtb):
    def body(c_ref, ab_ref, sa, sb, al, dt, q_out, k_out, g_out, b_out):
        qc = tuple(c_ref[:, h * HEAD:(h + 1) * HEAD] for h in range(HEADS))
        kc = tuple(c_ref[:, DN_WIDTH + h * HEAD:DN_WIDTH + (h + 1) * HEAD] for h in range(HEADS))
        qs, ks, g, beta = _dn_prep_fn((qc, kc, ab_ref[...]), _dn_prep_consts(sa, sb, al, dt))
        for h in range(HEADS):
            q_out[:, h * HEAD:(h + 1) * HEAD] = qs[h]
            k_out[:, h * HEAD:(h + 1) * HEAD] = ks[h]
        g_out[...] = g
        b_out[...] = beta

    return _rows_call("dn_prep_fwd", body, [conv, ab], [sel_a, sel_b, alog, dtb], [(DN_WIDTH, F32)] * 4, [],
                      _row_tile(conv.shape[0]))


def _dn_prep_bwd(conv, ab, dq, dk, dv, dg, db, sel_a, sel_b, alog, dtb):
    def body(c_ref, ab_ref, dq_r, dk_r, dv_r, dg_r, db_r, sa, sb, al, dt, dc_out, dab_out, dal_out, ddt_out):
        qc = tuple(c_ref[:, h * HEAD:(h + 1) * HEAD] for h in range(HEADS))
        kc = tuple(c_ref[:, DN_WIDTH + h * HEAD:DN_WIDTH + (h + 1) * HEAD] for h in range(HEADS))
        consts = _dn_prep_consts(sa, sb, al, dt)
        sel = dict(sel_a=consts["sel_a"], sel_b=consts["sel_b"])
        _, vjp = jax.vjp(lambda rows, ad: _dn_prep_fn(rows, {**sel, **ad}), (qc, kc, ab_ref[...]),
                         dict(alog=consts["alog"], dtb=consts["dtb"]))
        cq = tuple(dq_r[:, h * HEAD:(h + 1) * HEAD] for h in range(HEADS))
        ck = tuple(dk_r[:, h * HEAD:(h + 1) * HEAD] for h in range(HEADS))
        (dqc, dkc, dab), dad = vjp((cq, ck, dg_r[...], db_r[...]))
        for h in range(HEADS):
            dc_out[:, h * HEAD:(h + 1) * HEAD] = dqc[h]
            dc_out[:, DN_WIDTH + h * HEAD:DN_WIDTH + (h + 1) * HEAD] = dkc[h]
        dc_out[:, 2 * DN_WIDTH:3 * DN_WIDTH] = dv_r[...]
        dab_out[...] = dab.astype(BF16)
        _accumulate(dal_out, dad["alog"])
        _accumulate(ddt_out, dad["dtb"])

    return _rows_call("dn_prep_bwd", body, [conv, ab, dq, dk, dv, dg, db], [sel_a, sel_b, alog, dtb],
                      [(3 * DN_WIDTH, F32), (HEAD, BF16)], [(1, DN_WIDTH), (1, DN_WIDTH)], _row_tile(conv.shape[0]))


def _chunk_batch(t):
    nc = t // CHUNK
    return nc // 2 if nc % 2 == 0 else nc


def _dn_chunk_specs(t, nb):
    rows = nb * CHUNK
    blk = pl.BlockSpec((rows, HEAD), lambda h, b: (b, h))
    vblk = pl.BlockSpec((rows, HEAD), lambda h, b: (b, 2 * HEADS + h))
    mat = pl.BlockSpec((nb, HEAD, HEAD), lambda h, b: (b, h, 0))
    return rows, blk, vblk, mat


def _dn_chunk_fwd(qn, kn, conv, g, beta, host=None):
    t = qn.shape[0]
    nb = _chunk_batch(t)
    rows, blk, vblk, mat = _dn_chunk_specs(t, nb)

    def body(q_ref, k_ref, v_ref, g_ref, b_ref, n_o, b_o, qe_o, oo_o, eg_o):
        r3 = lambda x: x.reshape(nb, CHUNK, x.shape[-1])
        n_mat, b_mat, q_eff, o_own, eg = _dn_chunk_fn(r3(q_ref[...]), r3(k_ref[...]), r3(v_ref[...]), r3(g_ref[...]),
                                                      r3(g_ref[:, 0:CHUNK]), r3(b_ref[...]))
        n_o[...] = n_mat
        b_o[...] = b_mat
        qe_o[...] = q_eff.reshape(rows, HEAD)
        oo_o[...] = o_own.reshape(rows, HEAD)
        eg_o[...] = jnp.broadcast_to(eg, (nb, HEAD, HEAD))

    nc = t // CHUNK
    mats = jax.ShapeDtypeStruct((nc, DN_WIDTH, HEAD), F32)
    rowsd = jax.ShapeDtypeStruct((t, DN_WIDTH), F32)
    return _pcall(body, "dn_chunk_fwd", (HEADS, t // rows), [blk, blk, vblk, blk, blk], [mat, mat, blk, blk, mat],
                  [mats, mats, rowsd, rowsd, mats], [qn, kn, conv, g, beta], ("arbitrary", "arbitrary"), host=host)


def _dn_chunk_bwd(qn, kn, conv, g, beta, sall, gall, dq_eff, do, host=None):
    t = qn.shape[0]
    nb = _chunk_batch(t)
    rows, blk, vblk, mat = _dn_chunk_specs(t, nb)

    def body(q_ref, k_ref, v_ref, g_ref, b_ref, s_ref, ga_ref, dqe_ref, do_ref, dq_o, dk_o, dv_o, dg_o, db_o):
        r3 = lambda x: x.reshape(nb, CHUNK, x.shape[-1])
        _, vjp = jax.vjp(_dn_chunk_fn, r3(q_ref[...]), r3(k_ref[...]), r3(v_ref[...]), r3(g_ref[...]),
                         r3(g_ref[:, 0:CHUNK]), r3(b_ref[...]))
        s, ga = s_ref[...], ga_ref[...]
        d_n = -_bmm_nt(ga, s)
        d_eg = jnp.sum(ga * s, axis=1, keepdims=True)
        dq, dk, dv, dg, dg64, db = vjp((d_n, ga, r3(dqe_ref[...]), r3(do_ref[...]), d_eg))
        for o_ref, val in zip((dq_o, dk_o, dv_o, dg_o, db_o), (dq, dk, dv, dg, db)):
            o_ref[...] = val.reshape(rows, HEAD)
        dg_o[:, 0:CHUNK] += dg64.reshape(rows, CHUNK)

    return _pcall(body, "dn_chunk_bwd", (HEADS, t // rows), [blk, blk, vblk, blk, blk, mat, mat, blk, blk], [blk] * 5,
                  [jax.ShapeDtypeStruct((t, DN_WIDTH), F32)] * 5, [qn, kn, conv, g, beta, sall, gall, dq_eff, do],
                  ("arbitrary", "arbitrary"), host=host)


def _dn_rec_fwd(n_mat, b_mat, eg, host=None):
    nc = n_mat.shape[0]
    nb = _chunk_batch(nc * CHUNK)
    spec = pl.BlockSpec((nb, DN_WIDTH, HEAD), lambda i: (i, 0, 0))

    def body(n_ref, b_ref, eg_ref, sall_ref, s_scr):
        @pl.when(pl.program_id(0) == 0)
        def _():
            s_scr[...] = jnp.zeros(s_scr.shape, F32)

        for j in range(nb):
            sall_ref[j] = s_scr[...]
            for h in range(HEADS):
                sl = slice(h * HEAD, (h + 1) * HEAD)
                s_scr[sl, :] = _dn_rec_fn(s_scr[sl, :], n_ref[j, sl, :], b_ref[j, sl, :],
                                          eg_ref[j, h * HEAD:h * HEAD + 1, :])

    return _pcall(body, "dn_rec_fwd", (nc // nb,), [spec] * 3, spec, jax.ShapeDtypeStruct((nc, DN_WIDTH, HEAD), F32),
                  [n_mat, b_mat, eg], ("arbitrary",), scratch_shapes=[pltpu.VMEM((DN_WIDTH, HEAD), F32)], host=host)


def _dn_rec_bwd(n_mat, eg, ds_out, host=None):
    nc = n_mat.shape[0]
    nb = _chunk_batch(nc * CHUNK)
    steps = nc // nb
    spec = pl.BlockSpec((nb, DN_WIDTH, HEAD), lambda i: (steps - 1 - i, 0, 0))

    def body(n_ref, eg_ref, dso_ref, gall_ref, g_scr):
        @pl.when(pl.program_id(0) == 0)
        def _():
            g_scr[...] = jnp.zeros(g_scr.shape, F32)

        for j in reversed(range(nb)):
            gall_ref[j] = g_scr[...]
            for h in range(HEADS):
                sl = slice(h * HEAD, (h + 1) * HEAD)
                gv = g_scr[sl, :]
                g_scr[sl, :] = (gv * eg_ref[j, h * HEAD:h * HEAD + 1, :] - _mm_tn(n_ref[j, sl, :], gv)
                                + dso_ref[j, sl, :])

    return _pcall(body, "dn_rec_bwd", (steps,), [spec] * 3, spec, jax.ShapeDtypeStruct((nc, DN_WIDTH, HEAD), F32),
                  [n_mat, eg, ds_out], ("arbitrary",), scratch_shapes=[pltpu.VMEM((DN_WIDTH, HEAD), F32)], host=host)


def _dn_o_fwd(sall, q_eff, o_own):
    t = q_eff.shape[0]
    nb = _chunk_batch(t)
    rows, blk, _, mat = _dn_chunk_specs(t, nb)

    def body(s_ref, qe_ref, oo_ref, o_ref):
        r3 = lambda x: x.reshape(nb, CHUNK, HEAD)
        o_ref[...] = _dn_o_fn(s_ref[...], r3(qe_ref[...]), r3(oo_ref[...])).reshape(rows, HEAD)

    return _pcall(body, "dn_o_fwd", (HEADS, t // rows), [mat, blk, blk], blk, jax.ShapeDtypeStruct((t, DN_WIDTH), F32),
                  [sall, q_eff, o_own], ("arbitrary", "arbitrary"))


def _dn_o_bwd(sall, q_eff, do, host=None):
    t = q_eff.shape[0]
    nb = _chunk_batch(t)
    rows, blk, _, mat = _dn_chunk_specs(t, nb)

    def body(s_ref, qe_ref, do_ref, dqe_ref, ds_ref):
        r3 = lambda x: x.reshape(nb, CHUNK, HEAD)
        dov = r3(do_ref[...])
        dqe_ref[...] = _bmm_nt(dov, s_ref[...]).reshape(rows, HEAD)
        ds_ref[...] = _bmm_tn(r3(qe_ref[...]), dov)

    nc = t // CHUNK
    return _pcall(body, "dn_o_bwd", (HEADS, t // rows), [mat, blk, blk], [blk, mat],
                  [jax.ShapeDtypeStruct((t, DN_WIDTH), F32), jax.ShapeDtypeStruct((nc, DN_WIDTH, HEAD), F32)],
                  [sall, q_eff, do], ("arbitrary", "arbitrary"), host=host)


def _adamw_update(w, g, m, v):
    m2 = ADAM_B1 * m + (1.0 - ADAM_B1) * g
    v2 = ADAM_B2 * v + (1.0 - ADAM_B2) * (g * g)
    m_hat = m2 / (1.0 - ADAM_B1 ** ADAM_STEP)
    v_hat = v2 / (1.0 - ADAM_B2 ** ADAM_STEP)
    return -ADAM_LR * (m_hat / (jnp.sqrt(v_hat) + ADAM_EPS) + ADAM_WD * w), m2, v2


def _adamw_small(ws, gs, ms, vs):
    n = len(ws)

    def body(*refs):
        for i in range(n):
            d, m2, v2 = _adamw_update(refs[i][...], refs[n + i][...], refs[2 * n + i][...], refs[3 * n + i][...])
            refs[4 * n + i][...] = d
            refs[5 * n + i][...] = m2
            refs[6 * n + i][...] = v2

    shapes = [jax.ShapeDtypeStruct(a.shape, F32) for a in ws]
    outs = pl.pallas_call(body, name="adamw_small", out_shape=shapes * 3,
                          compiler_params=pltpu.CompilerParams(vmem_limit_bytes=VMEM_LIMIT))(*ws, *gs, *ms, *vs)
    return outs[:n], outs[n:2 * n], outs[2 * n:]


def _adamw_call(name, w, g, m, v, host=None):
    rows, cols = w.shape
    by_rows = rows % 8 == 0

    def body(w_ref, g_ref, m_ref, v_ref, g_out, d_ref, m_out, v_out):
        gv = g_ref[...] if by_rows else g_ref[0:rows, :]
        g_out[...] = gv
        d_ref[...], m_out[...], v_out[...] = _adamw_update(w_ref[...], gv, m_ref[...], v_ref[...])

    if by_rows:
        tr = _tile(rows, 256, 8)
        spec = g_spec = pl.BlockSpec((tr, cols), lambda i: (i, 0))
        grid = (rows // tr,)
    else:
        tc = _tile(cols, 256, 128)
        spec = pl.BlockSpec((rows, tc), lambda j: (0, j))
        g_spec = pl.BlockSpec((g.shape[0], tc), lambda j: (0, j))
        grid = (cols // tc,)
    return _pcall(body, name, grid, [spec, g_spec, spec, spec], [spec] * 4, [jax.ShapeDtypeStruct((rows, cols), F32)] * 4,
                  [w, g, m, v], ("arbitrary",), host=host)


def _adamw_rows3d(name, w, g, m, v):
    rows, _, cols = w.shape
    tr = max(d for d in range(1, 129) if rows % d == 0)

    def body(w_ref, g_ref, m_ref, v_ref, d_ref, m_out, v_out):
        d_ref[...], m_out[...], v_out[...] = _adamw_update(w_ref[...], g_ref[...], m_ref[...], v_ref[...])

    spec = pl.BlockSpec((tr, 1, cols), lambda i: (i, 0, 0))
    return pl.pallas_call(body, name=name, grid=(rows // tr,), in_specs=[spec] * 4, out_specs=[spec] * 3,
                          out_shape=[jax.ShapeDtypeStruct(w.shape, F32)] * 3, compiler_params=_cparams(("arbitrary",)))(
                              w, g, m, v)


def _rope_tables(t):
    half = ROPE // 2
    inv_freq = np.float32(ROPE_THETA) ** (-np.arange(half, dtype=np.float32) / np.float32(half))
    ang = np.arange(t, dtype=np.float32)[:, None] * inv_freq[None, :].astype(np.float32)
    z = np.zeros((t, HEAD - ROPE), np.float32)
    cos = np.concatenate([np.cos(ang), np.cos(ang), z], axis=1).astype(np.float32)
    sin = np.concatenate([np.sin(ang), np.sin(ang), z], axis=1).astype(np.float32)
    k = np.arange(HEAD)[:, None]
    l = np.arange(HEAD)[None, :]
    perm = np.where((l < half) & (k == l + half), -1.0, 0.0) + np.where((l >= half) & (l < ROPE) & (k == l - half), 1.0, 0.0)
    return jnp.asarray(cos), jnp.asarray(sin), jnp.asarray(perm.astype(np.float32))


def _win_to_pad(w):
    z = lambda n: jnp.zeros((n, w.shape[1]), w.dtype)
    return jnp.concatenate([w[576:2112], w[2112:2624], w[0:256], w[256:512], w[512:576], z(64), w[2624:2632], z(120)],
                           axis=0)


def _win_from_pad(g):
    return jnp.concatenate([g[2048:2304], g[2304:2560], g[2560:2624], g[0:1536], g[1536:2048], g[2688:2696]], axis=0)


def _qk_to_pad(w):
    w4 = w.reshape(HEADS, QK_DIM, w.shape[-1])
    return jnp.concatenate([w4, jnp.zeros((HEADS, QK_PAD - QK_DIM, w.shape[-1]), w.dtype)], axis=1).reshape(
        HEADS * QK_PAD, w.shape[-1])


def _qk_from_pad(g):
    return g.reshape(HEADS, QK_PAD, g.shape[-1])[:, :QK_DIM].reshape(HEADS * QK_DIM, g.shape[-1])


def _ff_to_pad(a, axis):
    shape = list(a.shape)
    shape[axis:axis + 1] = [N_CHIPS, FF_SHARD]
    a4 = a.reshape(shape)
    shape[axis + 1] = FF_BLOCK - FF_SHARD
    out = jnp.concatenate([a4, jnp.zeros(shape, a.dtype)], axis=axis + 1)
    shape[axis:axis + 2] = [D_FF_P]
    return out.reshape(shape)


def _ff_from_pad(a, axis):
    shape = list(a.shape)
    shape[axis:axis + 1] = [N_CHIPS, FF_BLOCK]
    a4 = lax.slice_in_dim(a.reshape(shape), 0, FF_SHARD, axis=axis + 1)
    shape[axis:axis + 2] = [D_FF]
    return a4.reshape(shape)


class _LocalPlan:
    def __init__(self, wt):
        self.wt, self.grads = wt, {}

    def weight(self, name):
        return self.wt[name]

    def host(self, point):
        return None

    def grad(self, name, value):
        self.grads[name] = value


def _local_step(x, tgt, wt, plan=None):
    plan = _LocalPlan(wt) if plan is None else plan
    s = x.shape[0]
    n_valid = N_META + s
    t = -(-n_valid // HEAD) * HEAD
    assert t - n_valid >= 3, "the DeltaNet conv kernels rely on at least three zero rows after the sequence"
    zpad = jnp.zeros((t - n_valid, D_MODEL), F32)
    h0 = jnp.concatenate([wt["meta_tokens"], x, zpad], axis=0)
    tgt_p = jnp.concatenate([jnp.zeros((N_META, D_MODEL), F32), tgt, zpad], axis=0)
    cos, sin, perm = _rope_tables(t)
    qn_w = jnp.concatenate([wt["q_norm_w"], jnp.zeros((1, QK_PAD - QK_DIM), F32)], axis=1)
    kn_w = jnp.concatenate([wt["k_norm_w"], jnp.zeros((1, QK_PAD - QK_DIM), F32)], axis=1)
    head_id = jnp.arange(DN_WIDTH)[None, :] // HEAD
    lane = jnp.arange(HEAD)[:, None]
    sel_a = (lane == head_id).astype(F32)
    sel_b = (lane == head_id + HEADS).astype(F32)
    alog = jnp.repeat(wt["dn_A_log"], HEAD, axis=1)
    dtb = jnp.repeat(wt["dn_dt_bias"], HEAD, axis=1)
    conv_w, conv_b = wt["ffn_conv_w"], wt["ffn_conv_b"]

    u = _rms_fwd("attn_norm_fwd", h0, wt["attn_norm_w"], host=plan.host("attn_norm_fwd"))
    win, wq, wkv = plan.weight("w_in_t"), plan.weight("w_q_t"), plan.weight("w_kv_t")
    proj = _matmul("in_proj", u, win, "nt", F32)
    z = (proj, DN_WIDTH, 3)
    q_lat, kv_lat, k_pe, ab = (proj, LORA, 8), (proj, LORA, 9), (proj, HEAD, 20), (proj, HEAD, 21)
    mla_consts = (wt["q_a_norm_w"], wq, wt["kv_a_norm_w"], wkv, qn_w, kn_w, perm)
    q, k, v = _mla_prep_fwd(q_lat, kv_lat, k_pe, cos, sin, *mla_consts)
    o_mla = _attn_fwd(q, k, v, host=plan.host("attn_fwd"))
    conv = _dn_conv_fwd(proj, wt["dn_conv_w"])
    dn_consts = (sel_a, sel_b, alog, dtb)
    qn, kn, g, beta = _dn_prep_fwd(conv, ab, *dn_consts)
    n_mat, b_mat, q_eff, o_own, eg = _dn_chunk_fwd(qn, kn, conv, g, beta, host=plan.host("dn_chunk_fwd"))
    sall = _dn_rec_fwd(n_mat, b_mat, eg)
    o_dn = _dn_o_fwd(sall, q_eff, o_own)
    w_out = plan.weight("w_out")
    mixed, h1, n2 = _mix_out_proj(o_mla, o_dn, z, wt["mla_out_norm_w"], wt["dn_out_norm_w"], w_out, h0, wt["ffn_norm_w"])
    w_gate, w_up = plan.weight("w_gate_t"), plan.weight("w_up_t")
    gpre, up, act = _ffn_glu_fwd(n2, w_gate, w_up, conv_w, conv_b, host=plan.host("ffn_glu_fwd"))
    w_down = plan.weight("w_down")
    dy, dy16, sq = _down_proj_loss(act, w_down, h1, tgt_p, n_valid)

    grads = {}
    plan.grad("w_down", _matmul("down_dw", act, dy16, "tn", BF16))
    dgpre, dup, grads["ffn_conv_w"], grads["ffn_conv_b"] = _ffn_glu_bwd(gpre, up, dy16, w_down, conv_w, conv_b)
    plan.grad("w_gate_t", _matmul("gate_dw", dgpre, n2, "tn", BF16))
    plan.grad("w_up_t", _matmul("up_dw", dup, n2, "tn", BF16))
    dh1, dh1_16, grads["ffn_norm_w"] = _ffn_in_bwd(dgpre, dup, w_gate, w_up, h1, dy, wt["ffn_norm_w"],
                                                   host=plan.host("ffn_in_bwd"))
    plan.grad("w_out", _matmul("out_dw", mixed, dh1_16, "tn", BF16))
    do_mla, do_dn, dz, grads["mla_out_norm_w"], grads["dn_out_norm_w"] = _mix_out_bwd(
        o_mla, o_dn, z, dh1_16, w_out, wt["mla_out_norm_w"], wt["dn_out_norm_w"], host=plan.host("mix_out_bwd"))
    dq_eff, ds_out = _dn_o_bwd(sall, q_eff, do_dn)
    gall = _dn_rec_bwd(n_mat, eg, ds_out)
    dqn, dkn, dv_dn, dg, dbeta = _dn_chunk_bwd(qn, kn, conv, g, beta, sall, gall, dq_eff, do_dn,
                                               host=plan.host("dn_chunk_bwd"))
    dconv, dab, dalog, ddtb = _dn_prep_bwd(conv, ab, dqn, dkn, dv_dn, dg, dbeta, *dn_consts)
    grads["dn_A_log"] = jnp.sum(dalog.reshape(HEADS, HEAD), axis=1)[None, :]
    grads["dn_dt_bias"] = jnp.sum(ddtb.reshape(HEADS, HEAD), axis=1)[None, :]
    ddn_pre, grads["dn_conv_w"] = _dn_conv_bwd(proj, wt["dn_conv_w"], dconv)
    dq, dk, dv = _attn_bwd(q, k, v, do_mla, host=plan.host("attn_bwd"))
    dq_lat, dkv_lat, dk_pe, dqa, dwq, dkva, dwkv, dqnw, dknw = _mla_prep_bwd(
        q_lat, kv_lat, k_pe, cos, sin, dq, dk, dv, *mla_consts, host=plan.host("mla_prep_bwd"))
    grads["q_a_norm_w"], grads["kv_a_norm_w"] = dqa, dkva
    plan.grad("w_q_t", dwq)
    plan.grad("w_kv_t", dwkv)
    grads["q_norm_w"], grads["k_norm_w"] = dqnw[:, :QK_DIM], dknw[:, :QK_DIM]
    dproj = [ddn_pre, dz, dq_lat, dkv_lat, dk_pe, dab]
    plan.grad("w_in_t", _in_proj_bwd_w(dproj, u))
    du = _in_proj_bwd_x(dproj, win, host=plan.host("in_dx"))
    dh0, _, grads["attn_norm_w"] = _rms_bwd("attn_norm_bwd", h0, wt["attn_norm_w"], [du], dh1,
                                            host=plan.host("attn_norm_bwd"))
    grads["meta_tokens"] = dh0[0:N_META]
    if isinstance(plan, _LocalPlan):
        grads.update(plan.grads)
    return sq, dh0[N_META:n_valid], grads


def _mesh_pos():
    return lax.axis_index("x"), lax.axis_index("y"), lax.axis_index("c")


def _other_chips(x, y):
    return [(1 - x, y), (x, 1 - y), (1 - x, 1 - y)]


def _remote(src, dst, send_sems, recv_sems, k, to):
    return pltpu.make_async_remote_copy(src_ref=src, dst_ref=dst, send_sem=send_sems.at[k], recv_sem=recv_sems.at[k],
                                        device_id=to, device_id_type=MESH)


SIBLING_ID, CHIPS_ID, GATHER_ID, ALL_ID = 1, 2, 3, 4


def _sibling_peer():
    x, y, c = _mesh_pos()
    return [(x, y, 1 - c)]


def _chip_peers():
    x, y, c = _mesh_pos()
    return [(qx, qy, c) for qx, qy in _other_chips(x, y)]


def _copies_exchange(make, ins, out_shape, nsem, peers=None, cid=None):
    def prog(in_refs, out_refs, send_sems, recv_sems):
        copies = make(in_refs, out_refs, send_sems, recv_sems)

        def start():
            for cp in copies:
                cp.start()

        def finish():
            for cp in copies:
                cp.wait()

        return start, finish

    return _Exchange(prog, ins, out_shape, nsem, peers, cid)


def _all_gather(shards):
    def prog(srcs, dsts, send_sems, recv_sems):
        x, y, c = _mesh_pos()
        p = 2 * x + y
        sibling = (x, y, 1 - c)
        chips = _other_chips(x, y)
        bufs = tuple((s, d, s.shape[0] // 2) for s, d in zip(srcs, dsts))

        def half(ref, rows, which):
            return ref.at[pl.ds(which * rows, rows), :]

        def copy(i, k, src, dst, to):
            return _remote(src, dst, send_sems, recv_sems, 6 * i + k, to)

        sends = [copy(i, j, half(src, rows, c), half(dst.at[p], rows, c), (*chip, c))
                 for i, (src, dst, rows) in enumerate(bufs) for j, chip in enumerate(chips)]

        def start():
            for cp in sends:
                cp.start()

        def finish():
            passed = []
            for i, (src, dst, rows) in enumerate(bufs):
                for j, (qx, qy) in enumerate(chips):
                    block = half(dst.at[2 * qx + qy], rows, c)
                    copy(i, j, block, block, (x, y, c)).wait_recv()
                    fwd = copy(i, 3 + j, block, block, sibling)
                    fwd.start()
                    passed.append(fwd)
            for i, (src, dst, rows) in enumerate(bufs):
                for j, (qx, qy) in enumerate(chips):
                    block = half(dst.at[2 * qx + qy], rows, 1 - c)
                    copy(i, 3 + j, block, block, (x, y, c)).wait_recv()
            for cp in sends + passed:
                cp.wait_send()

        return start, finish

    return _Exchange(prog, shards, [jax.ShapeDtypeStruct((N_CHIPS, *s.shape), s.dtype) for s in shards], 6 * len(shards),
                     lambda: _sibling_peer() + _chip_peers(), GATHER_ID)


def _all_gather_small(block):
    def make(srcs, dsts, send_sems, recv_sems):
        x, y, c = _mesh_pos()
        return [_remote(srcs[0], dsts[0].at[2 * x + y], send_sems, recv_sems, k, (qx, qy, c))
                for k, (qx, qy) in enumerate(_other_chips(x, y))]

    return _copies_exchange(make, [block], [jax.ShapeDtypeStruct((N_CHIPS, *block.shape), block.dtype)], 3, _chip_peers,
                            CHIPS_ID)


def _gathered(ex):
    p = 2 * lax.axis_index("x") + lax.axis_index("y")
    return [lax.dynamic_update_slice(g, s[None], (p, 0, 0)) for g, s in zip(ex.outs, ex.ins)]


def _rs_to_sibling(bufs):
    def make(srcs, dsts, send_sems, recv_sems):
        x, y, c = _mesh_pos()
        copies = []
        for i, (src, dst) in enumerate(zip(srcs, dsts)):
            half = src.shape[1] // 2
            copies.append(_remote(src.at[:, pl.ds((1 - c) * half, half), :], dst, send_sems, recv_sems, i, (x, y, 1 - c)))
        return copies

    return _copies_exchange(make, bufs,
                            [jax.ShapeDtypeStruct((N_CHIPS, b.shape[1] // 2, b.shape[2]), b.dtype) for b in bufs],
                            len(bufs), _sibling_peer, SIBLING_ID)


def _rs_pair_add(name, bufs, gots, c, out_dtype):
    n = len(bufs)

    def body(c_ref, *refs):
        for a_ref, b_ref, o_ref in zip(refs[:n], refs[n:2 * n], refs[2 * n:]):
            o_ref[...] = (a_ref[...].astype(F32) + b_ref[...].astype(F32)).astype(out_dtype)

    mine = [pl.BlockSpec((None, g.shape[1], g.shape[2]), lambda j, cr: (j, cr[0], 0)) for g in gots]
    whole = [pl.BlockSpec((None, g.shape[1], g.shape[2]), lambda j, cr: (j, 0, 0)) for g in gots]
    return pl.pallas_call(
        body, name=name,
        grid_spec=pltpu.PrefetchScalarGridSpec(num_scalar_prefetch=1, grid=(N_CHIPS,), in_specs=mine + whole, out_specs=whole),
        out_shape=[jax.ShapeDtypeStruct(g.shape, out_dtype) for g in gots],
        compiler_params=_cparams(("arbitrary",)))(c, *bufs, *gots)


def _rs_to_chips(accs):
    def make(srcs, dsts, send_sems, recv_sems):
        x, y, c = _mesh_pos()
        return [_remote(src.at[2 * qx + qy], dst.at[k], send_sems, recv_sems, 3 * i + k, (qx, qy, c))
                for i, (src, dst) in enumerate(zip(srcs, dsts)) for k, (qx, qy) in enumerate(_other_chips(x, y))]

    return _copies_exchange(make, accs, [jax.ShapeDtypeStruct((3, a.shape[1], a.shape[2]), a.dtype) for a in accs],
                            3 * len(accs), _chip_peers, CHIPS_ID)


def _rs_chip_add(name, accs, gots, p):
    n = len(accs)
    slot = (0, 1, 0, 2)

    def body(p_ref, *refs):
        me = p_ref[0]
        for own_ref, got_ref, o_ref in zip(refs[:n], refs[n:2 * n], refs[2 * n:]):
            total = None
            for chip in range(N_CHIPS):
                val = own_ref[...].astype(F32)
                for e in (1, 2, 3):
                    val = jnp.where((chip ^ me) == e, got_ref[slot[e]].astype(F32), val)
                total = val if total is None else total + val
            o_ref[...] = total

    own = [pl.BlockSpec((None, a.shape[1], a.shape[2]), lambda i, pr: (pr[0], 0, 0)) for a in accs]
    got = [pl.BlockSpec(g.shape, lambda i, pr: (0, 0, 0)) for g in gots]
    out = [pl.BlockSpec((a.shape[1], a.shape[2]), lambda i, pr: (0, 0)) for a in accs]
    return pl.pallas_call(
        body, name=name,
        grid_spec=pltpu.PrefetchScalarGridSpec(num_scalar_prefetch=1, grid=(1,), in_specs=own + got, out_specs=out),
        out_shape=[jax.ShapeDtypeStruct((a.shape[1], a.shape[2]), F32) for a in accs],
        compiler_params=_cparams(("arbitrary",)))(p, *accs, *gots)


def _rs_share(ress):
    def make(srcs, dsts, send_sems, recv_sems):
        x, y, c = _mesh_pos()
        return [_remote(src, dst, send_sems, recv_sems, i, (x, y, 1 - c)) for i, (src, dst) in enumerate(zip(srcs, dsts))]

    return _copies_exchange(make, ress, [jax.ShapeDtypeStruct(r.shape, F32) for r in ress], len(ress), _sibling_peer,
                            SIBLING_ID)


def _shared(ex):
    south = lax.axis_index("c") == 0
    return [jnp.concatenate([jnp.where(south, r, g), jnp.where(south, g, r)], axis=0) for r, g in zip(ex.ins, ex.outs)]


def _all_to_all_devices(vec):
    def others():
        x, y, c = _mesh_pos()
        return [((1 - x if r & 4 else x), (1 - y if r & 2 else y), (1 - c if r & 1 else c)) for r in range(1, 8)]

    def make(srcs, dsts, send_sems, recv_sems):
        x, y, c = _mesh_pos()
        me = 4 * x + 2 * y + c
        return [_remote(srcs[0], dsts[0].at[me], send_sems, recv_sems, r, peer) for r, peer in enumerate(others())]

    return _copies_exchange(make, [vec], [jax.ShapeDtypeStruct((8, *vec.shape), vec.dtype)], 7, others, ALL_ID)


def _sum_devices(stack):
    def body(s_ref, o_ref):
        total = s_ref[0]
        for d in range(1, 8):
            total = total + s_ref[d]
        o_ref[...] = total

    return pl.pallas_call(body, name="sum_devices", out_shape=jax.ShapeDtypeStruct(stack.shape[1:], F32),
                          compiler_params=pltpu.CompilerParams(vmem_limit_bytes=VMEM_LIMIT))(stack)


def _pad_rows(flat, rows):
    return jnp.concatenate([flat, jnp.zeros((rows * LANES - flat.shape[0],), flat.dtype)]).reshape(rows, LANES)


def _unshard(g4, shape, axis):
    a = g4.reshape(N_CHIPS, *shape)
    if axis == 0:
        return a.reshape(N_CHIPS * shape[0], shape[1])
    return jnp.transpose(a, (1, 0, 2)).reshape(shape[0], N_CHIPS * shape[1])


def _shard4(full, shape, axis):
    if axis == 0:
        return full.reshape(N_CHIPS, shape[0] * shape[1])
    a = full.reshape(shape[0], N_CHIPS, shape[1])
    return jnp.transpose(a, (1, 0, 2)).reshape(N_CHIPS, shape[0] * shape[1])


def _pad_axis0(a, rows):
    return jnp.concatenate([a, jnp.zeros((rows - a.shape[0], *a.shape[1:]), a.dtype)], axis=0)


def _pad_axis1(a, rows):
    return jnp.concatenate([a, jnp.zeros((a.shape[0], rows - a.shape[1], *a.shape[2:]), a.dtype)], axis=1)


def _shard_to_strip(name, w):
    _, (shape, axis, rows) = name, {n: (s, ax, r) for n, s, ax, r in BIG}[name]
    w2 = w.reshape(shape).astype(BF16)
    return _pad_axis0(w2.T if axis == 1 else w2, rows)


LOCAL_NAME = dict(w_in="w_in_t", w_q_b="w_q_t", w_kv_b="w_kv_t", w_out="w_out", w_gate="w_gate_t", w_up="w_up_t",
                  w_down="w_down")


WIN_SEGMENTS = ((576, 2112, 0), (2112, 2624, 1536), (0, 256, 2048), (256, 512, 2304), (512, 576, 2560), (2624, 2632, 2688))


def _strips_to_weight(name, g4):
    if name == "w_in":
        return _win_to_pad(g4[:, :IN_SHARD].reshape(IN_COLS, D_MODEL))
    if name == "w_q_b":
        return _qk_to_pad(g4.reshape(HEADS * QK_DIM, LORA))
    return g4.reshape(N_CHIPS * g4.shape[1], g4.shape[2])


def _grad_to_strips(name, g):
    if name == "w_in":
        strips = []
        for q in range(N_CHIPS):
            pieces = []
            for a, b, local in sorted(WIN_SEGMENTS):
                s, e = max(a, q * IN_SHARD), min(b, (q + 1) * IN_SHARD)
                if s < e:
                    pieces.append(g[local + s - a:local + e - a])
            pieces.append(jnp.zeros((IN_SHARD_P - IN_SHARD, D_MODEL), g.dtype))
            strips.append(jnp.concatenate(pieces, axis=0))
        return jnp.stack(strips)
    if name == "w_q_b":
        return _qk_from_pad(g).reshape(N_CHIPS, QK_DIM, LORA)
    return g.reshape(N_CHIPS, g.shape[0] // N_CHIPS, g.shape[1])


class _MeshPlan:
    LATE = dict(attn_norm_fwd=("w_in", "w_q_b", "w_kv_b"), attn_fwd=("w_up",), dn_chunk_fwd=("w_out", "w_gate"),
                ffn_glu_fwd=("w_down",))
    GROUP_A = ("w_down", "w_gate", "w_up", "w_out")
    GROUP_B = ("w_in", "w_q_b", "w_kv_b")

    def __init__(self, w):
        x, y, c = _mesh_pos()
        self.ci = jnp.reshape(c, (1,)).astype(jnp.int32)
        self.pi = jnp.reshape(2 * x + y, (1,)).astype(jnp.int32)
        self.strip = {n: _shard_to_strip(n, w[n]) for n, _, _, _ in BIG}
        self.gathers, self.weights, self.g, self.acc, self.reduced = {}, {}, {}, {}, {}
        self.sibs, self.sib, self.chip, self.share, self.halves = [], None, None, None, [None, None]

    def gather_small(self, small):
        ex = _all_gather_small(small)
        ex.run("all_gather_small")
        return _gathered(ex)[0]

    def weight(self, local_name):
        if local_name not in self.weights:
            for point, (names, ex) in list(self.gathers.items()):
                if ex.outs is not None:
                    for n, g4 in zip(names, _gathered(ex)):
                        if "/" in n:
                            n, half = n.split("/")
                            self.halves[int(half)] = g4
                            if None in self.halves:
                                continue
                            g4 = jnp.concatenate(self.halves, axis=1)
                        self.weights[LOCAL_NAME[n]] = _strips_to_weight(n, g4)
                    del self.gathers[point]
        return self.weights[local_name]

    def _shard(self, name):
        if "/" not in name:
            return self.strip[name]
        name, half = name.split("/")
        rows = self.strip[name].shape[0] // 2
        return self.strip[name][int(half) * rows:(int(half) + 1) * rows]

    def grad(self, local_name, value):
        name = {v: k for k, v in LOCAL_NAME.items()}[local_name]
        self.g[name] = _grad_to_strips(name, value)

    def _pair_add(self, names, gots):
        accs = _rs_pair_add("rs_pair_add_" + names[0], [self.g[n] for n in names], gots, self.ci, BF16)
        self.acc.update(zip(names, accs))

    def _chip_add(self, names, chip):
        return _rs_chip_add("rs_chip_add_" + names[0], [self.acc[n] for n in names], chip.outs, self.pi)

    def _take_shared(self, names, share):
        for n, strip in zip(names, _shared(share)):
            self.reduced[n] = strip

    def host(self, point):
        a, b = self.GROUP_A, self.GROUP_B
        if point in self.LATE:
            names = self.LATE[point]
            ex = _all_gather([self._shard(n) for n in names])
            self.gathers[point] = (names, ex)
            return ex
        if point in ("ffn_in_bwd", "mix_out_bwd"):
            names = dict(ffn_in_bwd=a[:3], mix_out_bwd=a[3:])[point]
            ex = _rs_to_sibling([self.g[n] for n in names])
            self.sibs.append(ex)
            return ex
        if point == "dn_chunk_bwd":
            self._pair_add(a, [o for ex in self.sibs for o in ex.outs])
            self.chip1 = _rs_to_chips([self.acc[n] for n in a[:2]])
            return self.chip1
        if point == "attn_bwd":
            self.chip2 = _rs_to_chips([self.acc[n] for n in a[2:]])
            return self.chip2
        if point == "mla_prep_bwd":
            ress = self._chip_add(a[:2], self.chip1) + self._chip_add(a[2:], self.chip2)
            self.share = _rs_share(ress)
            return self.share
        if point == "in_dx":
            self._take_shared(a, self.share)
            self.sib = _rs_to_sibling([self.g[n] for n in b])
            return self.sib
        if point == "attn_norm_bwd":
            self._pair_add(b, self.sib.outs)
            self.chip = _rs_to_chips([self.acc[n] for n in b])
            return self.chip
        return None

    def last_share(self):
        self.share = _rs_share(self._chip_add(self.GROUP_B, self.chip))
        return self.share

    def finish(self):
        self._take_shared(self.GROUP_B, self.share)
        return self.reduced


def _strip_to_shard(name, strip):
    shape, axis = {n: (s, ax) for n, s, ax, _ in BIG}[name]
    rows = shape[axis]
    return strip[:rows].T if axis == 1 else strip[:rows]


def kernel(x, meta_tokens, attn_norm_w, w_in, q_a_norm_w, w_q_b, kv_a_norm_w, w_kv_b, q_norm_w, k_norm_w, mla_out_norm_w, dn_conv_w, dn_A_log, dn_dt_bias, dn_out_norm_w, w_out, ffn_norm_w, w_gate, w_up, ffn_conv_w, ffn_conv_b, w_down, loss_target, m_meta_tokens, m_attn_norm_w, m_w_in, m_q_a_norm_w, m_w_q_b, m_kv_a_norm_w, m_w_kv_b, m_q_norm_w, m_k_norm_w, m_mla_out_norm_w, m_dn_conv_w, m_dn_A_log, m_dn_dt_bias, m_dn_out_norm_w, m_w_out, m_ffn_norm_w, m_w_gate, m_w_up, m_ffn_conv_w, m_ffn_conv_b, m_w_down, v_meta_tokens, v_attn_norm_w, v_w_in, v_q_a_norm_w, v_w_q_b, v_kv_a_norm_w, v_w_kv_b, v_q_norm_w, v_k_norm_w, v_mla_out_norm_w, v_dn_conv_w, v_dn_A_log, v_dn_dt_bias, v_dn_out_norm_w, v_w_out, v_ffn_norm_w, v_w_gate, v_w_up, v_ffn_conv_w, v_ffn_conv_b, v_w_down):
    local = dict(locals())
    w = {n: local[n] for n in WEIGHTS}
    m = {n: local["m_" + n] for n in WEIGHTS}
    v = {n: local["v_" + n] for n in WEIGHTS}
    p = 2 * lax.axis_index("x") + lax.axis_index("y")

    plan = _MeshPlan(w)
    wf = _pad_rows(jnp.concatenate([w[n].reshape(-1) for n, _, _ in SMALL_SHARDED]), SMALL_ROWS)
    gf = plan.gather_small(wf).reshape(N_CHIPS, -1)
    full = {}
    off = 0
    for n, s, ax in SMALL_SHARDED:
        full[n] = _unshard(gf[:, off:off + s[0] * s[1]], s, ax)
        off += s[0] * s[1]
    for n, _ in REPLICATED:
        full[n] = w[n]
    full["ffn_conv_w"] = _ff_to_pad(full["ffn_conv_w"], 1)
    full["ffn_conv_b"] = _ff_to_pad(full["ffn_conv_b"], 1)

    sq, grad_x, g = _local_step(x[0], loss_target[0], full, plan)
    g["ffn_conv_w"] = _ff_from_pad(g["ffn_conv_w"], 1)
    g["ffn_conv_b"] = _ff_from_pad(g["ffn_conv_b"], 1)

    small_all = [n for n, _, _ in SMALL_SHARDED] + [n for n, _ in REPLICATED]
    vec = jnp.concatenate([g[n].reshape(-1) for n in small_all] + [jnp.reshape(0.5 / D_MODEL * jnp.sum(sq), (1,))])
    vec = _pad_rows(vec, -(-vec.shape[0] // (8 * LANES)) * 8)
    a2a = _all_to_all_devices(vec)

    gs, delta, new_m, new_v = {}, {}, {}, {}
    big = {n: (s, ax) for n, s, ax, _ in BIG}

    def adamw_big(n, strips, host=None):
        s, ax = big[n]
        if ax == 1 and s[1] % 8:
            there = lambda a: jnp.transpose(a, (2, 0, 1))
            back = lambda a: jnp.transpose(a, (1, 2, 0))
            g3 = strips[n][:s[1]].reshape(s[1], 1, s[0])
            d2, m2, v2 = _adamw_rows3d("adamw_" + n, there(w[n]), g3, there(m[n]), there(v[n]))
            gs[n], delta[n], new_m[n], new_v[n] = back(g3), back(d2), back(m2), back(v2)
            return
        flip = ax == 1 and s[1] % 8 == 0
        there = (lambda a: a.reshape(s).T) if flip else (lambda a: a.reshape(s))
        back = (lambda a: a.T.reshape(w[n].shape)) if flip else (lambda a: a.reshape(w[n].shape))
        strip = strips[n] if flip or ax == 0 else strips[n][:s[1]].T
        g2, d2, m2, v2 = _adamw_call("adamw_" + n, there(w[n]), strip, there(m[n]), there(v[n]), host=host)
        gs[n], delta[n], new_m[n], new_v[n] = back(g2), back(d2), back(m2), back(v2)

    adamw_big("w_down", plan.reduced, host=a2a)
    adamw_big("w_gate", plan.reduced, host=plan.last_share())
    adamw_big("w_up", plan.reduced)
    adamw_big("w_out", plan.reduced)
    strips = plan.finish()
    for n in plan.GROUP_B:
        adamw_big(n, strips)
    me = 4 * lax.axis_index("x") + 2 * lax.axis_index("y") + lax.axis_index("c")
    red = _sum_devices(lax.dynamic_update_slice(a2a.outs[0], vec[None], (me, 0, 0))).reshape(-1)
    off = 0
    for n in small_all:
        tot = red[off:off + g[n].size].reshape(g[n].shape)
        off += g[n].size
        shard = {sn: (s, ax) for sn, s, ax in SMALL_SHARDED}.get(n)
        if shard is not None:
            tot = lax.dynamic_slice_in_dim(tot, p * shard[0][1], shard[0][1], axis=1)
        gs[n] = tot
    loss = red[off]
    two_d = lambda a: a.reshape(a.shape[-2], a.shape[-1])
    outs = _adamw_small([two_d(w[n]) for n in small_all], [two_d(gs[n]) for n in small_all],
                        [two_d(m[n]) for n in small_all], [two_d(v[n]) for n in small_all])
    for i, n in enumerate(small_all):
        for dst, src in ((delta, outs[0]), (new_m, outs[1]), (new_v, outs[2])):
            dst[n] = src[i].reshape(w[n].shape)

    grad_out = [gs[n].reshape(w[n].shape) for n in WEIGHTS]
    return (loss, grad_x[None], *grad_out, *[delta[n] for n in WEIGHTS], *[new_m[n] for n in WEIGHTS],
            *[new_v[n] for n in WEIGHTS])
```

```python
import functools
import math

import jax
import jax.numpy as jnp
import numpy as np
from jax import lax
from jax.experimental import pallas as pl
from jax.experimental.pallas import tpu as pltpu

F32 = jnp.float32
BF16 = jnp.bfloat16
HI = lax.Precision.HIGHEST
MESH = pl.DeviceIdType.MESH

N_META = 16
D_MODEL = 1024
HEADS = 4
HEAD = 128
ROPE = 64
QK_DIM = HEAD + ROPE
QK_PAD = 2 * HEAD
LORA = 256
DN_WIDTH = HEADS * HEAD
CHUNK = 64
D_FF = 2816
N_CHIPS = 4
FF_SHARD = D_FF // N_CHIPS
FF_BLOCK = 768
D_FF_P = N_CHIPS * FF_BLOCK
IN_COLS = 2632
IN_SHARD = IN_COLS // N_CHIPS
IN_SHARD_P = 672
IN_PAD = 2816
NORM_EPS = 1e-6
ROPE_THETA = 10000.0
LANES = 512

ADAM_LR, ADAM_B1, ADAM_B2, ADAM_EPS, ADAM_WD, ADAM_STEP = 0.001, 0.9, 0.999, 1e-08, 0.01, 10

VMEM_LIMIT = 56 * 1024 * 1024

BIG = (("w_in", (1024, 658), 1, IN_SHARD_P), ("w_q_b", (256, 192), 1, 192), ("w_kv_b", (256, 256), 1, 256),
       ("w_out", (256, 1024), 0, 256), ("w_gate", (1024, 704), 1, FF_BLOCK), ("w_up", (1024, 704), 1, FF_BLOCK),
       ("w_down", (704, 1024), 0, FF_BLOCK))
SMALL_SHARDED = (("meta_tokens", (16, 256), 1), ("dn_conv_w", (4, 384), 1), ("ffn_conv_w", (3, 704), 1))
REPLICATED = (("attn_norm_w", 1024), ("q_a_norm_w", 256), ("kv_a_norm_w", 256), ("q_norm_w", 192), ("k_norm_w", 192),
              ("mla_out_norm_w", 128), ("dn_A_log", 4), ("dn_dt_bias", 4), ("dn_out_norm_w", 128), ("ffn_norm_w", 1024),
              ("ffn_conv_b", 2816))
WEIGHTS = ("meta_tokens", "attn_norm_w", "w_in", "q_a_norm_w", "w_q_b", "kv_a_norm_w", "w_kv_b", "q_norm_w", "k_norm_w",
           "mla_out_norm_w", "dn_conv_w", "dn_A_log", "dn_dt_bias", "dn_out_norm_w", "w_out", "ffn_norm_w", "w_gate",
           "w_up", "ffn_conv_w", "ffn_conv_b", "w_down")

SMALL_ROWS = 16
REP_ROWS = 16


def _cparams(sem):
    return pltpu.CompilerParams(dimension_semantics=sem, vmem_limit_bytes=VMEM_LIMIT)


class _Exchange:
    def __init__(self, prog, ins, out_shape, nsem, peers=None, cid=None):
        self.prog, self.ins, self.out_shape, self.nsem = prog, list(ins), list(out_shape), nsem
        self.peers, self.cid = peers, cid
        self.outs = None

    def sems(self):
        return [pltpu.SemaphoreType.DMA((self.nsem,)), pltpu.SemaphoreType.DMA((self.nsem,))]

    def programs(self, in_refs, out_refs, send_sems, recv_sems):
        start, finish = self.prog(in_refs, out_refs, send_sems, recv_sems)
        if self.cid is None:
            return start, finish
        peers = self.peers()

        def shake_and_start():
            barrier = pltpu.get_barrier_semaphore()
            for peer in peers:
                pl.semaphore_signal(barrier, inc=1, device_id=peer, device_id_type=MESH)
            pl.semaphore_wait(barrier, len(peers))
            start()

        return shake_and_start, finish

    def cparams(self, **kw):
        return pltpu.CompilerParams(has_side_effects=True, collective_id=self.cid, **kw)

    def run(self, name):
        any_spec = pl.BlockSpec(memory_space=pl.ANY)
        n = len(self.ins)

        def body(*refs):
            start, finish = self.programs(refs[:n], refs[n:-2], refs[-2], refs[-1])
            start()
            finish()

        self.outs = pl.pallas_call(
            body, name=name, in_specs=[any_spec] * n, out_specs=[any_spec] * len(self.out_shape),
            out_shape=self.out_shape, scratch_shapes=self.sems(), compiler_params=self.cparams())(*self.ins)
        return self.outs


def _pcall(body, name, grid, in_specs, out_specs, out_shape, args, sem, scratch_shapes=(), host=None):
    single = not isinstance(out_shape, (list, tuple))
    out_specs, out_shape = ([out_specs], [out_shape]) if single else (list(out_specs), list(out_shape))
    if host is None:
        outs = pl.pallas_call(body, name=name, grid=grid, in_specs=list(in_specs), out_specs=out_specs, out_shape=out_shape,
                              scratch_shapes=list(scratch_shapes), compiler_params=_cparams(sem))(*args)
        return outs[0] if single else outs
    any_spec = pl.BlockSpec(memory_space=pl.ANY)
    n_in, n_out, n_scr, nx_in, nx_out = len(in_specs), len(out_specs), len(scratch_shapes), len(host.ins), len(host.out_shape)

    def hosted(*refs):
        c_in, x_in = refs[:n_in], refs[n_in:n_in + nx_in]
        o0 = n_in + nx_in
        c_out, x_out = refs[o0:o0 + n_out], refs[o0 + n_out:o0 + n_out + nx_out]
        s0 = o0 + n_out + nx_out
        start, finish = host.programs(x_in, x_out, refs[s0 + n_scr], refs[s0 + n_scr + 1])
        first = functools.reduce(jnp.logical_and, [pl.program_id(d) == 0 for d in range(len(grid))])
        last = functools.reduce(jnp.logical_and, [pl.program_id(d) == grid[d] - 1 for d in range(len(grid))])
        pl.when(first)(start)
        body(*c_in, *c_out, *refs[s0:s0 + n_scr])
        pl.when(last)(finish)

    outs = pl.pallas_call(
        hosted, name=name, grid=grid, in_specs=list(in_specs) + [any_spec] * nx_in,
        out_specs=out_specs + [any_spec] * nx_out, out_shape=out_shape + host.out_shape,
        scratch_shapes=list(scratch_shapes) + host.sems(),
        compiler_params=host.cparams(dimension_semantics=sem, vmem_limit_bytes=VMEM_LIMIT))(*args, *host.ins)
    host.outs = outs[n_out:]
    return outs[0] if single else outs[:n_out]


NN, NT, TN = ((1,), (0,)), ((1,), (1,)), ((0,), (0,))


def _shift_dims(dims, batch):
    if not batch:
        return (dims, ((), ()))
    return (((dims[0][0] + 1,), (dims[1][0] + 1,)), ((0,), (0,)))


def _make_mm(dims, exact, batch=False):
    def raw(a, b, d):
        dn = _shift_dims(d, batch)
        if exact == "split_lhs":
            ah, bh = a.astype(BF16), b.astype(BF16)
            al = (a - ah.astype(F32)).astype(BF16)
            return lax.dot_general(ah, bh, dn, preferred_element_type=F32) + lax.dot_general(al, bh, dn,
                                                                                              preferred_element_type=F32)
        if exact == "split":
            ah, bh = a.astype(BF16), b.astype(BF16)
            al, bl = (a - ah.astype(F32)).astype(BF16), (b - bh.astype(F32)).astype(BF16)
            dot = lambda p, q: lax.dot_general(p, q, dn, preferred_element_type=F32)
            return dot(ah, bh) + (dot(ah, bl) + dot(al, bh))
        if exact:
            return lax.dot_general(a.astype(F32), b.astype(F32), dn, precision=HI, preferred_element_type=F32)
        return lax.dot_general(a.astype(BF16), b.astype(BF16), dn, preferred_element_type=F32)

    @jax.custom_vjp
    def mm(a, b):
        return raw(a, b, dims)

    def fwd(a, b):
        return raw(a, b, dims), (a, b)

    def bwd(res, g):
        a, b = res
        if dims == NN:
            da, db = raw(g, b, NT), raw(a, g, TN)
        elif dims == NT:
            da, db = raw(g, b, NN), raw(g, a, TN)
        else:
            da, db = raw(b, g, NT), raw(a, g, NN)
        return da.astype(a.dtype), db.astype(b.dtype)

    mm.defvjp(fwd, bwd)
    return mm


_mm = _make_mm(NN, False)
_mm_nt = _make_mm(NT, False)
_mm_tn = _make_mm(TN, False)
_mmx = _make_mm(NN, "split_lhs")
_bmm = _make_mm(NN, False, batch=True)
_bmm_nt = _make_mm(NT, False, batch=True)
_bmm_tn = _make_mm(TN, False, batch=True)
_bmmx = _make_mm(NN, True, batch=True)
_bmms = _make_mm(NN, "split", batch=True)
_bmms_nt = _make_mm(NT, "split", batch=True)
_bmms_tn = _make_mm(TN, "split", batch=True)


@jax.custom_vjp
def _unit_lower_inv(a):
    n = a.shape[-1]
    eye = (lax.broadcasted_iota(jnp.int32, a.shape, 1) == lax.broadcasted_iota(jnp.int32, a.shape, 2)).astype(F32)
    x = -a
    t = eye + x
    for _ in range(max(n.bit_length() - 2, 0)):
        x = _bmms(x, x)
        t = t + _bmms(t, x)
    return t


def _unit_lower_inv_fwd(a):
    t = _unit_lower_inv(a)
    return t, t


def _unit_lower_inv_bwd(t, g):
    return (-_bmms_tn(t, _bmms_nt(g, t)),)


_unit_lower_inv.defvjp(_unit_lower_inv_fwd, _unit_lower_inv_bwd)


def _scan_chunk_rows(x, reverse):
    nb, c, w = x.shape
    y = x.reshape(nb * c, w)
    pos = lax.broadcasted_iota(jnp.int32, y.shape, 0) % c
    step = 1
    while step < c:
        if reverse:
            y = y + jnp.where(pos < c - step, pltpu.roll(y, nb * c - step, 0), 0.0)
        else:
            y = y + jnp.where(pos >= step, pltpu.roll(y, step, 0), 0.0)
        step *= 2
    return y.reshape(nb, c, w)


@jax.custom_vjp
def _chunk_cumsum(x):
    return _scan_chunk_rows(x, False)


_chunk_cumsum.defvjp(lambda x: (_scan_chunk_rows(x, False), None), lambda _, g: (_scan_chunk_rows(g, True),))


def _rms(x, w, n):
    ms = jnp.sum(x * x, axis=-1, keepdims=True) * (1.0 / n)
    return x * lax.rsqrt(ms + NORM_EPS) * w


def _silu(x):
    return x * jax.nn.sigmoid(x)


def _softplus(x):
    return jnp.maximum(x, 0.0) + jnp.log(1.0 + jnp.exp(-jnp.abs(x)))


def _rope(x, cos, sin, perm):
    return x * cos + _mmx(x, perm) * sin


def _mla_prep_fn(rows, consts):
    q_lat, kv_lat, k_pe, cos, sin = rows
    qn = _rms(q_lat, consts["qa_w"], LORA)
    kvn = _rms(kv_lat, consts["kva_w"], LORA)
    outs = []
    for h in range(HEADS):
        q_n = _mm_nt(qn, consts["wq_n"][h])
        q_r = _mm_nt(qn, consts["wq_r"][h])
        rs = lax.rsqrt((jnp.sum(q_n * q_n, -1, keepdims=True) + jnp.sum(q_r * q_r, -1, keepdims=True)) * (1.0 / QK_DIM)
                       + NORM_EPS)
        q_n = q_n * rs * consts["qn_n"]
        q_r = _rope(q_r * rs * consts["qn_r"], cos, sin, consts["perm"])
        k_n = _mm_nt(kvn, consts["wk_n"][h])
        v = _mm_nt(kvn, consts["wv"][h])
        rk = lax.rsqrt((jnp.sum(k_n * k_n, -1, keepdims=True) + jnp.sum(k_pe * k_pe, -1, keepdims=True)) * (1.0 / QK_DIM)
                       + NORM_EPS)
        k_n = k_n * rk * consts["kn_n"]
        k_r = _rope(k_pe * rk * consts["kn_r"], cos, sin, consts["perm"])
        outs += [q_n, q_r, k_n, k_r, v]
    return tuple(outs)


def _attn_fn(q, k, v, row0):
    s = _mm_nt(q, k) * (1.0 / math.sqrt(QK_DIM))
    qpos = row0 + lax.broadcasted_iota(jnp.int32, s.shape, 0)
    kpos = lax.broadcasted_iota(jnp.int32, s.shape, 1)
    s = jnp.where(kpos <= qpos, s, -1e30)
    m = lax.stop_gradient(jnp.max(s, axis=-1, keepdims=True))
    p = jnp.exp(s - m)
    p = p / jnp.sum(p, axis=-1, keepdims=True)
    return _mm(p, v)


def _dn_prep_fn(rows, consts):
    qc, kc, ab = rows
    a_b = _mmx(ab, consts["sel_a"])
    b_b = _mmx(ab, consts["sel_b"])
    beta = jax.nn.sigmoid(b_b)
    g = -jnp.exp(consts["alog"]) * _softplus(a_b + consts["dtb"])
    qs, ks = [], []
    for h in range(HEADS):
        q, k = qc[h], kc[h]
        qs.append(q * lax.rsqrt(jnp.sum(q * q, -1, keepdims=True) + NORM_EPS))
        ks.append(k * lax.rsqrt(jnp.sum(k * k, -1, keepdims=True) + NORM_EPS))
    return tuple(qs), tuple(ks), g, beta


def _dn_chunk_fn(q, k, v, gb, g64, bb):
    nb = q.shape[0]
    ri = lax.broadcasted_iota(jnp.int32, (nb, CHUNK, CHUNK), 1)
    ci = lax.broadcasted_iota(jnp.int32, (nb, CHUNK, CHUNK), 2)
    tri = ri >= ci
    strict = ri > ci
    tril = tri.astype(F32)
    eye = (ri == ci).astype(F32)
    ones = jnp.ones((nb, CHUNK, CHUNK), F32)
    gc = _chunk_cumsum(gb)
    gc64 = _chunk_cumsum(g64)
    grow = _bmmx(ones, eye * gc64)
    diff = gc64 - grow
    decay = jnp.where(tri, jnp.exp(jnp.where(tri, diff, 0.0)), 0.0)
    kb = k * bb
    vb = v * bb
    a = jnp.where(strict, _bmm_nt(kb, k) * decay, 0.0)
    tinv = _unit_lower_inv(a)
    u = _bmm(tinv, vb)
    w = _bmm(tinv, kb * jnp.exp(gc))
    qs = q * (1.0 / math.sqrt(HEAD))
    qk = _bmm_nt(qs, k) * decay
    qg = qs * jnp.exp(gc)
    glast = jnp.sum(gb, axis=1, keepdims=True)
    kdec = k * jnp.exp(glast - gc)
    n_mat = _bmm_tn(kdec, w)
    b_mat = _bmm_tn(kdec, u)
    q_eff = qg - _bmm(qk, w)
    o_own = _bmm(qk, u)
    return n_mat, b_mat, q_eff, o_own, jnp.exp(glast)


def _dn_rec_fn(s, n_mat, b_mat, eg):
    return s * eg - _mm(n_mat, s) + b_mat


def _dn_o_fn(s, q_eff, o_own):
    return _bmm(q_eff, s) + o_own


def _dn_out_fn(o, z, w):
    return _rms(o, w, HEAD) * _silu(z)


def _row_tile(t, parts=8):
    return t // parts if (t // parts) % 16 == 0 else t


def _tile(n, pref, unit):
    best = n
    for cand in range(unit, min(n, pref) + 1, unit):
        if n % cand == 0:
            best = cand
    return best if best <= pref else n


def _rows_call(name, body, rows, consts, outs, accs, r, host=None):
    rows = [a if isinstance(a, tuple) else (a, a.shape[1], 0) for a in rows]
    t = rows[0][0].shape[0]
    zero = lambda nd: (lambda i: (0,) * nd)
    in_specs = [pl.BlockSpec((r, w), functools.partial(lambda i, b: (i, b), b=blk)) for _, w, blk in rows]
    rows = [a for a, _, _ in rows]
    in_specs += [pl.BlockSpec(a.shape, zero(a.ndim)) for a in consts]
    out_shape = [jax.ShapeDtypeStruct((t, w), dt) for w, dt in outs] + [jax.ShapeDtypeStruct(s, F32) for s in accs]
    out_specs = [pl.BlockSpec((r, w), lambda i: (i, 0)) for w, _ in outs] + [pl.BlockSpec(s, zero(len(s))) for s in accs]
    return _pcall(body, name, (t // r,), in_specs, out_specs, out_shape, [*rows, *consts], ("arbitrary",), host=host)


def _accumulate(ref, val):
    @pl.when(pl.program_id(0) == 0)
    def _():
        ref[...] = jnp.zeros(ref.shape, ref.dtype)

    ref[...] += val


def _matmul(name, a, b, dims, out_dtype, res=None, host=None):
    if dims == "nn":
        (m, k), n = a.shape, b.shape[1]
    elif dims == "nt":
        (m, k), n = a.shape, b.shape[0]
    else:
        (k, m), n = a.shape, b.shape[1]
    tm = _tile(m, 1100, 16) if dims != "tn" else _tile(m, 640, 128)
    tn = _tile(n, 1408, 128)
    if dims == "nn":
        a_spec, b_spec, dn = pl.BlockSpec((tm, k), lambda i, j: (i, 0)), pl.BlockSpec((k, tn), lambda i, j: (0, j)), NN
    elif dims == "nt":
        a_spec, b_spec, dn = pl.BlockSpec((tm, k), lambda i, j: (i, 0)), pl.BlockSpec((tn, k), lambda i, j: (j, 0)), NT
    else:
        a_spec, b_spec, dn = pl.BlockSpec((k, tm), lambda i, j: (0, i)), pl.BlockSpec((k, tn), lambda i, j: (0, j)), TN
    o_spec = pl.BlockSpec((tm, tn), lambda i, j: (i, j))

    def body(*refs):
        a_ref, b_ref, o_ref = refs[0], refs[1], refs[-1]
        acc = lax.dot_general(a_ref[...].astype(BF16), b_ref[...].astype(BF16), (dn, ((), ())),
                              preferred_element_type=F32)
        if res is not None:
            acc = acc + refs[2][...]
        o_ref[...] = acc.astype(out_dtype)

    ins = [a, b] + ([res] if res is not None else [])
    specs = [a_spec, b_spec] + ([o_spec] if res is not None else [])
    return _pcall(body, name, (m // tm, n // tn), specs, o_spec, jax.ShapeDtypeStruct((m, n), out_dtype), ins,
                  ("arbitrary", "arbitrary"), host=host)


def _rms_fwd(name, h, w, host=None):
    n = h.shape[1]

    def body(h_ref, w_ref, o_ref):
        o_ref[...] = _rms(h_ref[...], w_ref[...], n).astype(BF16)

    return _rows_call(name, body, [h], [w], [(n, BF16)], [], _row_tile(h.shape[0]), host=host)[0]


def _rms_bwd(name, h, w, cts, resid, host=None):
    n = h.shape[1]
    nct = len(cts)

    def body(*refs):
        h_ref, ct_refs, r_ref, w_ref = refs[0], refs[1:1 + nct], refs[1 + nct], refs[2 + nct]
        dh_ref, dh16_ref, dw_ref = refs[-3], refs[-2], refs[-1]
        ct = ct_refs[0][...].astype(F32)
        for c in ct_refs[1:]:
            ct = ct + c[...].astype(F32)
        _, vjp = jax.vjp(lambda x, ww: _rms(x, ww, n), h_ref[...], w_ref[...])
        dh, dw = vjp(ct)
        dh = dh + r_ref[...]
        dh_ref[...] = dh
        dh16_ref[...] = dh.astype(BF16)
        _accumulate(dw_ref, dw)

    return _rows_call(name, body, [h, *cts, resid], [w], [(n, F32), (n, BF16)], [(1, n)], _row_tile(h.shape[0]), host=host)


def _mla_consts_from_refs(qa, wq, kva, wkv, qn, kn, perm):
    f = lambda r: r[...].astype(F32)
    return dict(
        qa_w=f(qa), kva_w=f(kva), perm=f(perm),
        wq_n=[wq[h * QK_PAD:h * QK_PAD + HEAD, :].astype(F32) for h in range(HEADS)],
        wq_r=[wq[h * QK_PAD + HEAD:(h + 1) * QK_PAD, :].astype(F32) for h in range(HEADS)],
        wk_n=[wkv[h * QK_PAD:h * QK_PAD + HEAD, :].astype(F32) for h in range(HEADS)],
        wv=[wkv[h * QK_PAD + HEAD:(h + 1) * QK_PAD, :].astype(F32) for h in range(HEADS)],
        qn_n=qn[:, 0:HEAD], qn_r=qn[:, HEAD:QK_PAD], kn_n=kn[:, 0:HEAD], kn_r=kn[:, HEAD:QK_PAD])


def _mla_prep_fwd(q_lat, kv_lat, k_pe, cos, sin, qa, wq, kva, wkv, qn, kn, perm):
    def body(ql, kvl, kp, c, s, qa_r, wq_r, kva_r, wkv_r, qn_r, kn_r, p_r, q_out, k_out, v_out):
        consts = _mla_consts_from_refs(qa_r, wq_r, kva_r, wkv_r, qn_r, kn_r, p_r)
        outs = _mla_prep_fn((ql[...], kvl[...], kp[...], c[...], s[...]), consts)
        for h in range(HEADS):
            q_n, q_r, k_n, k_r, v = outs[5 * h:5 * h + 5]
            q_out[:, h * QK_PAD:h * QK_PAD + HEAD] = q_n.astype(BF16)
            q_out[:, h * QK_PAD + HEAD:(h + 1) * QK_PAD] = q_r.astype(BF16)
            k_out[:, h * QK_PAD:h * QK_PAD + HEAD] = k_n.astype(BF16)
            k_out[:, h * QK_PAD + HEAD:(h + 1) * QK_PAD] = k_r.astype(BF16)
            v_out[:, h * HEAD:(h + 1) * HEAD] = v.astype(BF16)

    return _rows_call("mla_prep_fwd", body, [q_lat, kv_lat, k_pe, cos, sin], [qa, wq, kva, wkv, qn, kn, perm],
                      [(HEADS * QK_PAD, BF16), (HEADS * QK_PAD, BF16), (DN_WIDTH, BF16)], [], _row_tile(cos.shape[0], 4))


def _mla_prep_bwd(q_lat, kv_lat, k_pe, cos, sin, dq, dk, dv, qa, wq, kva, wkv, qn, kn, perm, host=None):
    def body(ql, kvl, kp, c, s, dq_r, dk_r, dv_r, qa_r, wq_r, kva_r, wkv_r, qn_r, kn_r, p_r,
             dql, dkvl, dkp, dqa, dwq, dkva, dwkv, dqn, dkn):
        consts = _mla_consts_from_refs(qa_r, wq_r, kva_r, wkv_r, qn_r, kn_r, p_r)
        cc, ss, pm = c[...], s[...], consts.pop("perm")
        _, vjp = jax.vjp(lambda rows, cs: _mla_prep_fn((*rows, cc, ss), dict(cs, perm=pm)), (ql[...], kvl[...], kp[...]),
                         consts)
        cts = []
        for h in range(HEADS):
            cts += [dq_r[:, h * QK_PAD:h * QK_PAD + HEAD], dq_r[:, h * QK_PAD + HEAD:(h + 1) * QK_PAD],
                    dk_r[:, h * QK_PAD:h * QK_PAD + HEAD], dk_r[:, h * QK_PAD + HEAD:(h + 1) * QK_PAD],
                    dv_r[:, h * HEAD:(h + 1) * HEAD]]
        (d_ql, d_kvl, d_kp), dc = vjp(tuple(cts))
        dql[...] = d_ql.astype(BF16)
        dkvl[...] = d_kvl.astype(BF16)
        dkp[...] = d_kp.astype(BF16)
        first = pl.program_id(0) == 0

        def acc(ref, sl, val):
            @pl.when(first)
            def _():
                ref[sl] = val

            @pl.when(jnp.logical_not(first))
            def _():
                ref[sl] += val

        full = (slice(None), slice(None))
        acc(dqa, full, dc["qa_w"])
        acc(dkva, full, dc["kva_w"])
        for h in range(HEADS):
            acc(dwq, (slice(h * QK_PAD, h * QK_PAD + HEAD), slice(None)), dc["wq_n"][h])
            acc(dwq, (slice(h * QK_PAD + HEAD, (h + 1) * QK_PAD), slice(None)), dc["wq_r"][h])
            acc(dwkv, (slice(h * QK_PAD, h * QK_PAD + HEAD), slice(None)), dc["wk_n"][h])
            acc(dwkv, (slice(h * QK_PAD + HEAD, (h + 1) * QK_PAD), slice(None)), dc["wv"][h])
        acc(dqn, (slice(None), slice(0, HEAD)), dc["qn_n"])
        acc(dqn, (slice(None), slice(HEAD, QK_PAD)), dc["qn_r"])
        acc(dkn, (slice(None), slice(0, HEAD)), dc["kn_n"])
        acc(dkn, (slice(None), slice(HEAD, QK_PAD)), dc["kn_r"])

    return _rows_call("mla_prep_bwd", body, [q_lat, kv_lat, k_pe, cos, sin, dq, dk, dv],
                      [qa, wq, kva, wkv, qn, kn, perm],
                      [(LORA, BF16), (LORA, BF16), (HEAD, BF16)],
                      [(1, LORA), wq.shape, (1, LORA), wkv.shape, (1, QK_PAD), (1, QK_PAD)], _row_tile(cos.shape[0], 4),
                      host=host)


ATTN_Q_ROWS = 256


def _attn_blocks(t):
    return [(r0, min(ATTN_Q_ROWS, t - r0)) for r0 in range(0, t, ATTN_Q_ROWS)]


def _attn_fwd(q, k, v, host=None):
    t = q.shape[0]

    def body(q_ref, k_ref, v_ref, o_ref):
        for r0, rows in _attn_blocks(t):
            ext = r0 + rows
            o_ref[r0:ext, :] = _attn_fn(q_ref[r0:ext, :], k_ref[0:ext, :], v_ref[0:ext, :], r0)

    qk_spec = pl.BlockSpec((t, QK_PAD), lambda h: (0, h))
    v_spec = pl.BlockSpec((t, HEAD), lambda h: (0, h))
    return _pcall(body, "attn_fwd", (HEADS,), [qk_spec, qk_spec, v_spec], v_spec,
                  jax.ShapeDtypeStruct((t, HEADS * HEAD), F32), [q, k, v], ("arbitrary",), host=host)


def _attn_bwd(q, k, v, do, host=None):
    t = q.shape[0]

    def body(q_ref, k_ref, v_ref, do_ref, dq_ref, dk_ref, dv_ref):
        dk_ref[...] = jnp.zeros(dk_ref.shape, F32)
        dv_ref[...] = jnp.zeros(dv_ref.shape, F32)
        for r0, rows in _attn_blocks(t):
            ext = r0 + rows
            _, vjp = jax.vjp(functools.partial(_attn_fn, row0=r0), q_ref[r0:ext, :].astype(F32),
                             k_ref[0:ext, :].astype(F32), v_ref[0:ext, :].astype(F32))
            dq, dk, dv = vjp(do_ref[r0:ext, :])
            dq_ref[r0:ext, :] = dq
            dk_ref[0:ext, :] += dk
            dv_ref[0:ext, :] += dv

    qk_spec = pl.BlockSpec((t, QK_PAD), lambda h: (0, h))
    v_spec = pl.BlockSpec((t, HEAD), lambda h: (0, h))
    return _pcall(body, "attn_bwd", (HEADS,), [qk_spec, qk_spec, v_spec, v_spec], [qk_spec, qk_spec, v_spec],
                  [jax.ShapeDtypeStruct((t, HEADS * QK_PAD), F32), jax.ShapeDtypeStruct((t, HEADS * QK_PAD), F32),
                   jax.ShapeDtypeStruct((t, HEADS * HEAD), F32)], [q, k, v, do], ("arbitrary",), host=host)


def _mix_out_proj(o_mla, o_dn, z, w_mla, w_dn, w_out, h0, w_ffn):
    def body(om_ref, od_ref, z_ref, h0_ref, wm_ref, wd_ref, wo_ref, wf_ref, mixed_ref, h1_ref, n2_ref):
        for h in range(HEADS):
            sl = slice(h * HEAD, (h + 1) * HEAD)
            mixed_ref[:, sl] = _rms(om_ref[:, sl], wm_ref[...], HEAD).astype(BF16)
            mixed_ref[:, DN_WIDTH + h * HEAD:DN_WIDTH + (h + 1) * HEAD] = _dn_out_fn(od_ref[:, sl], z_ref[:, sl],
                                                                                     wd_ref[...]).astype(BF16)
        h1 = _mm(mixed_ref[...], wo_ref[...]) + h0_ref[...]
        h1_ref[...] = h1
        n2_ref[...] = _rms(h1, wf_ref[...], D_MODEL).astype(BF16)

    return _rows_call("mix_out_proj", body, [o_mla, o_dn, z, h0], [w_mla, w_dn, w_out, w_ffn],
                      [(D_MODEL, BF16), (D_MODEL, F32), (D_MODEL, BF16)], [], _row_tile(o_mla.shape[0], 4))


def _down_proj_loss(act, w_down, h1, tgt, n_valid):
    t, n = h1.shape
    r = _row_tile(t, 4)

    def body(a_ref, h_ref, t_ref, w_ref, dy_ref, dy16_ref, acc_ref):
        h2 = _mm(a_ref[...], w_ref[...]) + h_ref[...]
        rows = pl.program_id(0) * r + lax.broadcasted_iota(jnp.int32, (r, n), 0)
        valid = jnp.logical_and(rows >= N_META, rows < n_valid)
        e = jnp.where(valid, h2 - t_ref[...], 0.0)
        dy = e * (1.0 / n)
        dy_ref[...] = dy
        dy16_ref[...] = dy.astype(BF16)
        _accumulate(acc_ref, jnp.sum(e * e, axis=0, keepdims=True))

    return _rows_call("down_proj_loss", body, [act, h1, tgt], [w_down], [(n, F32), (n, BF16)], [(1, n)], r)


def _in_proj_bwd_x(pieces, win, host=None):
    offs = np.cumsum([0] + [p.shape[1] for p in pieces])

    def body(*refs):
        p_refs, w_ref, o_ref = refs[:len(pieces)], refs[len(pieces)], refs[-1]
        acc = None
        for i, p_ref in enumerate(p_refs):
            part = _mm(p_ref[...], w_ref[int(offs[i]):int(offs[i + 1]), :])
            acc = part if acc is None else acc + part
        o_ref[...] = acc.astype(BF16)

    return _rows_call("in_dx", body, pieces, [win], [(win.shape[1], BF16)], [], _row_tile(pieces[0].shape[0], 4), host=host)[0]


def _in_proj_bwd_w(pieces, u):
    offs = np.cumsum([0] + [p.shape[1] for p in pieces])

    def body(*refs):
        p_refs, u_ref, o_ref = refs[:len(pieces)], refs[len(pieces)], refs[-1]
        first = pl.program_id(0) == 0
        uv = u_ref[...]
        for i, p_ref in enumerate(p_refs):
            rows = slice(int(offs[i]), int(offs[i + 1]))
            part = _mm_tn(p_ref[...], uv)

            @pl.when(first)
            def _():
                o_ref[rows, :] = part

            @pl.when(jnp.logical_not(first))
            def _():
                o_ref[rows, :] += part

    return _rows_call("in_dw", body, [*pieces, u], [], [], [(int(offs[-1]), u.shape[1])], _row_tile(u.shape[0], 2))[0]


def _ffn_in_bwd(dgpre, dup, w_gate_t, w_up_t, h1, dy, w_ffn, host=None):
    n = h1.shape[1]

    def body(dg_ref, du_ref, h_ref, dy_ref, wg_ref, wu_ref, w_ref, dh_ref, dh16_ref, dw_ref):
        ct = _mm(dg_ref[...], wg_ref[...]) + _mm(du_ref[...], wu_ref[...])
        _, vjp = jax.vjp(lambda x, ww: _rms(x, ww, n), h_ref[...], w_ref[...])
        dh, dw = vjp(ct)
        dh = dh + dy_ref[...]
        dh_ref[...] = dh
        dh16_ref[...] = dh.astype(BF16)
        _accumulate(dw_ref, dw)

    return _rows_call("ffn_in_bwd", body, [dgpre, dup, h1, dy], [w_gate_t, w_up_t, w_ffn], [(n, F32), (n, BF16)], [(1, n)],
                      _row_tile(h1.shape[0]), host=host)


def _mix_out_bwd(o_mla, o_dn, z, dh1, w_out, w_mla, w_dn, host=None):
    def body(om_ref, od_ref, z_ref, dh_ref, wo_ref, wm_ref, wd_ref, dom_ref, dod_ref, dz_ref, dwm_ref, dwd_ref):
        dwm = dwd = None
        for h in range(HEADS):
            sl = slice(h * HEAD, (h + 1) * HEAD)
            _, vjp = jax.vjp(lambda o, w: _rms(o, w, HEAD), om_ref[:, sl], wm_ref[...])
            do, dw = vjp(_mm_nt(dh_ref[...], wo_ref[sl, :]))
            dom_ref[:, sl] = do
            dwm = dw if dwm is None else dwm + dw
            _, vjp = jax.vjp(_dn_out_fn, od_ref[:, sl], z_ref[:, sl], wd_ref[...])
            do, dz, dw = vjp(_mm_nt(dh_ref[...], wo_ref[DN_WIDTH + h * HEAD:DN_WIDTH + (h + 1) * HEAD, :]))
            dod_ref[:, sl] = do
            dz_ref[:, sl] = dz.astype(BF16)
            dwd = dw if dwd is None else dwd + dw
        _accumulate(dwm_ref, dwm)
        _accumulate(dwd_ref, dwd)

    return _rows_call("mix_out_bwd", body, [o_mla, o_dn, z, dh1], [w_out, w_mla, w_dn],
                      [(DN_WIDTH, F32), (DN_WIDTH, F32), (DN_WIDTH, BF16)], [(1, HEAD), (1, HEAD)],
                      _row_tile(o_mla.shape[0], 4), host=host)


def _shift_down(x, s):
    if s == 0:
        return x
    rows = lax.broadcasted_iota(jnp.int32, x.shape, 0)
    return jnp.where(rows >= s, pltpu.roll(x, s, 0), 0.0)


def _shift_up(x, s):
    if s == 0:
        return x
    t = x.shape[0]
    rows = lax.broadcasted_iota(jnp.int32, x.shape, 0)
    return jnp.where(rows < t - s, pltpu.roll(x, t - s, 0), 0.0)


def _col_call(name, body, cols, taps, outs, tap_outs, cw, host=None):
    t, c = cols[0].shape[0], taps[0].shape[1]
    in_specs = [pl.BlockSpec((t, cw), lambda j: (0, j)) for _ in cols]
    in_specs += [pl.BlockSpec((a.shape[0], cw), lambda j: (0, j)) for a in taps]
    out_shape = [jax.ShapeDtypeStruct((t, c), dt) for dt in outs] + [jax.ShapeDtypeStruct((n, c), F32) for n in tap_outs]
    out_specs = [pl.BlockSpec((t, cw), lambda j: (0, j)) for _ in outs]
    out_specs += [pl.BlockSpec((n, cw), lambda j: (0, j)) for n in tap_outs]
    return _pcall(body, name, (c // cw,), in_specs, out_specs, out_shape, [*cols, *taps], ("arbitrary",), host=host)


def _causal_conv(x, w_ref, width, zero_tail=False):
    down = (lambda a, s: pltpu.roll(a, s, 0)) if zero_tail else _shift_down
    acc = w_ref[width - 1:width, :] * x
    for j in range(width - 1):
        acc = acc + w_ref[j:j + 1, :] * down(x, width - 1 - j)
    return acc


def _causal_conv_bwd(x, dpre, w_ref, dx_ref, dw_ref, width, zero_tail=False):
    t = x.shape[0]
    down = (lambda a, s: pltpu.roll(a, s, 0)) if zero_tail else _shift_down
    up = (lambda a, s: pltpu.roll(a, t - s, 0)) if zero_tail else _shift_up
    dx = w_ref[width - 1:width, :] * dpre
    dw_ref[width - 1:width, :] = jnp.sum(dpre * x, axis=0, keepdims=True)
    for j in range(width - 1):
        s = width - 1 - j
        dx = dx + w_ref[j:j + 1, :] * up(dpre, s)
        dw_ref[j:j + 1, :] = jnp.sum(dpre * down(x, s), axis=0, keepdims=True)
    dx_ref[...] = dx.astype(dx_ref.dtype)


def _dsilu(x):
    sg = jax.nn.sigmoid(x)
    return sg * (1.0 + x * (1.0 - sg))


def _dn_conv_fwd(x, w):
    def body(x_ref, w_ref, y_ref):
        y_ref[...] = _silu(_causal_conv(x_ref[...], w_ref, 4, zero_tail=True))

    return _col_call("dn_conv_fwd", body, [x], [w], [F32], [], 256)[0]


def _dn_conv_bwd(x, w, dy):
    def body(x_ref, dy_ref, w_ref, dx_ref, dw_ref):
        xv = x_ref[...]
        dpre = dy_ref[...] * _dsilu(_causal_conv(xv, w_ref, 4, zero_tail=True))
        _causal_conv_bwd(xv, dpre, w_ref, dx_ref, dw_ref, 4, zero_tail=True)

    return _col_call("dn_conv_bwd", body, [x, dy], [w], [BF16], [4], 256)


def _ffn_glu_fwd(n2, w_gate_t, w_up_t, w, b, host=None):
    t, k = n2.shape
    c, cw = w_gate_t.shape[0], 256

    def body(n_ref, wg_ref, wu_ref, w_ref, b_ref, g_ref, u_ref, a_ref):
        nv = n_ref[...]
        g16 = _mm_nt(nv, wg_ref[...]).astype(BF16)
        u16 = _mm_nt(nv, wu_ref[...]).astype(BF16)
        g_ref[...] = g16
        u_ref[...] = u16
        gate = _causal_conv(g16.astype(F32), w_ref, 3) + b_ref[...]
        a_ref[...] = (_silu(gate) * u16.astype(F32)).astype(BF16)

    wspec = pl.BlockSpec((cw, k), lambda j: (j, 0))
    col = pl.BlockSpec((t, cw), lambda j: (0, j))
    in_specs = [pl.BlockSpec((t, k), lambda j: (0, 0)), wspec, wspec, pl.BlockSpec((w.shape[0], cw), lambda j: (0, j)),
                pl.BlockSpec((1, cw), lambda j: (0, j))]
    return _pcall(body, "ffn_glu_fwd", (c // cw,), in_specs, [col] * 3, [jax.ShapeDtypeStruct((t, c), BF16)] * 3,
                  [n2, w_gate_t, w_up_t, w, b], ("arbitrary",), host=host)


def _ffn_glu_bwd(gpre, up, dy16, w_down, w, b):
    t, k = dy16.shape
    c, cw = w_down.shape[0], 256

    def body(g_ref, u_ref, dy_ref, wd_ref, w_ref, b_ref, dg_ref, du_ref, dw_ref, db_ref):
        gv = g_ref[...].astype(F32)
        gate = _causal_conv(gv, w_ref, 3) + b_ref[...]
        da = _mm_nt(dy_ref[...], wd_ref[...])
        sg = jax.nn.sigmoid(gate)
        du_ref[...] = (da * (gate * sg)).astype(BF16)
        dgate = da * u_ref[...].astype(F32) * (sg * (1.0 + gate * (1.0 - sg)))
        db_ref[...] = jnp.sum(dgate, axis=0, keepdims=True)
        _causal_conv_bwd(gv, dgate, w_ref, dg_ref, dw_ref, 3)

    col = pl.BlockSpec((t, cw), lambda j: (0, j))
    taps = lambda rows: pl.BlockSpec((rows, cw), lambda j: (0, j))
    in_specs = [col, col, pl.BlockSpec((t, k), lambda j: (0, 0)), pl.BlockSpec((cw, k), lambda j: (j, 0)),
                taps(w.shape[0]), taps(1)]
    return _pcall(body, "ffn_glu_bwd", (c // cw,), in_specs, [col, col, taps(w.shape[0]), taps(1)],
                  [jax.ShapeDtypeStruct((t, c), BF16)] * 2 + [jax.ShapeDtypeStruct((w.shape[0], c), F32),
                                                             jax.ShapeDtypeStruct((1, c), F32)],
                  [gpre, up, dy16, w_down, w, b], ("arbitrary",))


def _dn_prep_consts(sa, sb, al, dt):
    return dict(sel_a=sa[...], sel_b=sb[...], alog=al[...], dtb=dt[...])


def _dn_prep_fwd(conv, ab, sel_a, sel_b, alog, dtb):
    def body(c_ref, ab_ref, sa, sb, al, dt, q_out, k_out, g_out, b_out):
        qc = tuple(c_ref[:, h * HEAD:(h + 1) * HEAD] for h in range(HEADS))
        kc = tuple(c_ref[:, DN_WIDTH + h * HEAD:DN_WIDTH + (h + 1) * HEAD] for h in range(HEADS))
        qs, ks, g, beta = _dn_prep_fn((qc, kc, ab_ref[...]), _dn_prep_consts(sa, sb, al, dt))
        for h in range(HEADS):
            q_out[:, h * HEAD:(h + 1) * HEAD] = qs[h]
            k_out[:, h * HEAD:(h + 1) * HEAD] = ks[h]
        g_out[...] = g
        b_out[...] = beta

    return _rows_call("dn_prep_fwd", body, [conv, ab], [sel_a, sel_b, alog, dtb], [(DN_WIDTH, F32)] * 4, [],
                      _row_tile(conv.shape[0]))


def _dn_prep_bwd(conv, ab, dq, dk, dv, dg, db, sel_a, sel_b, alog, dtb):
    def body(c_ref, ab_ref, dq_r, dk_r, dv_r, dg_r, db_r, sa, sb, al, dt, dc_out, dab_out, dal_out, ddt_out):
        qc = tuple(c_ref[:, h * HEAD:(h + 1) * HEAD] for h in range(HEADS))
        kc = tuple(c_ref[:, DN_WIDTH + h * HEAD:DN_WIDTH + (h + 1) * HEAD] for h in range(HEADS))
        consts = _dn_prep_consts(sa, sb, al, dt)
        sel = dict(sel_a=consts["sel_a"], sel_b=consts["sel_b"])
        _, vjp = jax.vjp(lambda rows, ad: _dn_prep_fn(rows, {**sel, **ad}), (qc, kc, ab_ref[...]),
                         dict(alog=consts["alog"], dtb=consts["dtb"]))
        cq = tuple(dq_r[:, h * HEAD:(h + 1) * HEAD] for h in range(HEADS))
        ck = tuple(dk_r[:, h * HEAD:(h + 1) * HEAD] for h in range(HEADS))
        (dqc, dkc, dab), dad = vjp((cq, ck, dg_r[...], db_r[...]))
        for h in range(HEADS):
            dc_out[:, h * HEAD:(h + 1) * HEAD] = dqc[h]
            dc_out[:, DN_WIDTH + h * HEAD:DN_WIDTH + (h + 1) * HEAD] = dkc[h]
        dc_out[:, 2 * DN_WIDTH:3 * DN_WIDTH] = dv_r[...]
        dab_out[...] = dab.astype(BF16)
        _accumulate(dal_out, dad["alog"])
        _accumulate(ddt_out, dad["dtb"])

    return _rows_call("dn_prep_bwd", body, [conv, ab, dq, dk, dv, dg, db], [sel_a, sel_b, alog, dtb],
                      [(3 * DN_WIDTH, F32), (HEAD, BF16)], [(1, DN_WIDTH), (1, DN_WIDTH)], _row_tile(conv.shape[0]))


def _chunk_batch(t):
    nc = t // CHUNK
    return nc // 2 if nc % 2 == 0 else nc


def _dn_chunk_specs(t, nb):
    rows = nb * CHUNK
    blk = pl.BlockSpec((rows, HEAD), lambda h, b: (b, h))
    vblk = pl.BlockSpec((rows, HEAD), lambda h, b: (b, 2 * HEADS + h))
    mat = pl.BlockSpec((nb, HEAD, HEAD), lambda h, b: (b, h, 0))
    return rows, blk, vblk, mat


def _dn_chunk_fwd(qn, kn, conv, g, beta, host=None):
    t = qn.shape[0]
    nb = _chunk_batch(t)
    rows, blk, vblk, mat = _dn_chunk_specs(t, nb)

    def body(q_ref, k_ref, v_ref, g_ref, b_ref, n_o, b_o, qe_o, oo_o, eg_o):
        r3 = lambda x: x.reshape(nb, CHUNK, x.shape[-1])
        n_mat, b_mat, q_eff, o_own, eg = _dn_chunk_fn(r3(q_ref[...]), r3(k_ref[...]), r3(v_ref[...]), r3(g_ref[...]),
                                                      r3(g_ref[:, 0:CHUNK]), r3(b_ref[...]))
        n_o[...] = n_mat
        b_o[...] = b_mat
        qe_o[...] = q_eff.reshape(rows, HEAD)
        oo_o[...] = o_own.reshape(rows, HEAD)
        eg_o[...] = jnp.broadcast_to(eg, (nb, HEAD, HEAD))

    nc = t // CHUNK
    mats = jax.ShapeDtypeStruct((nc, DN_WIDTH, HEAD), F32)
    rowsd = jax.ShapeDtypeStruct((t, DN_WIDTH), F32)
    return _pcall(body, "dn_chunk_fwd", (HEADS, t // rows), [blk, blk, vblk, blk, blk], [mat, mat, blk, blk, mat],
                  [mats, mats, rowsd, rowsd, mats], [qn, kn, conv, g, beta], ("arbitrary", "arbitrary"), host=host)


def _dn_chunk_bwd(qn, kn, conv, g, beta, sall, gall, dq_eff, do, host=None):
    t = qn.shape[0]
    nb = _chunk_batch(t)
    rows, blk, vblk, mat = _dn_chunk_specs(t, nb)

    def body(q_ref, k_ref, v_ref, g_ref, b_ref, s_ref, ga_ref, dqe_ref, do_ref, dq_o, dk_o, dv_o, dg_o, db_o):
        r3 = lambda x: x.reshape(nb, CHUNK, x.shape[-1])
        _, vjp = jax.vjp(_dn_chunk_fn, r3(q_ref[...]), r3(k_ref[...]), r3(v_ref[...]), r3(g_ref[...]),
                         r3(g_ref[:, 0:CHUNK]), r3(b_ref[...]))
        s, ga = s_ref[...], ga_ref[...]
        d_n = -_bmm_nt(ga, s)
        d_eg = jnp.sum(ga * s, axis=1, keepdims=True)
        dq, dk, dv, dg, dg64, db = vjp((d_n, ga, r3(dqe_ref[...]), r3(do_ref[...]), d_eg))
        for o_ref, val in zip((dq_o, dk_o, dv_o, dg_o, db_o), (dq, dk, dv, dg, db)):
            o_ref[...] = val.reshape(rows, HEAD)
        dg_o[:, 0:CHUNK] += dg64.reshape(rows, CHUNK)

    return _pcall(body, "dn_chunk_bwd", (HEADS, t // rows), [blk, blk, vblk, blk, blk, mat, mat, blk, blk], [blk] * 5,
                  [jax.ShapeDtypeStruct((t, DN_WIDTH), F32)] * 5, [qn, kn, conv, g, beta, sall, gall, dq_eff, do],
                  ("arbitrary", "arbitrary"), host=host)


def _dn_rec_fwd(n_mat, b_mat, eg, host=None):
    nc = n_mat.shape[0]
    nb = _chunk_batch(nc * CHUNK)
    spec = pl.BlockSpec((nb, DN_WIDTH, HEAD), lambda i: (i, 0, 0))

    def body(n_ref, b_ref, eg_ref, sall_ref, s_scr):
        @pl.when(pl.program_id(0) == 0)
        def _():
            s_scr[...] = jnp.zeros(s_scr.shape, F32)

        for j in range(nb):
            sall_ref[j] = s_scr[...]
            for h in range(HEADS):
                sl = slice(h * HEAD, (h + 1) * HEAD)
                s_scr[sl, :] = _dn_rec_fn(s_scr[sl, :], n_ref[j, sl, :], b_ref[j, sl, :],
                                          eg_ref[j, h * HEAD:h * HEAD + 1, :])

    return _pcall(body, "dn_rec_fwd", (nc // nb,), [spec] * 3, spec, jax.ShapeDtypeStruct((nc, DN_WIDTH, HEAD), F32),
                  [n_mat, b_mat, eg], ("arbitrary",), scratch_shapes=[pltpu.VMEM((DN_WIDTH, HEAD), F32)], host=host)


def _dn_rec_bwd(n_mat, eg, ds_out, host=None):
    nc = n_mat.shape[0]
    nb = _chunk_batch(nc * CHUNK)
    steps = nc // nb
    spec = pl.BlockSpec((nb, DN_WIDTH, HEAD), lambda i: (steps - 1 - i, 0, 0))

    def body(n_ref, eg_ref, dso_ref, gall_ref, g_scr):
        @pl.when(pl.program_id(0) == 0)
        def _():
            g_scr[...] = jnp.zeros(g_scr.shape, F32)

        for j in reversed(range(nb)):
            gall_ref[j] = g_scr[...]
            for h in range(HEADS):
                sl = slice(h * HEAD, (h + 1) * HEAD)
                gv = g_scr[sl, :]
                g_scr[sl, :] = (gv * eg_ref[j, h * HEAD:h * HEAD + 1, :] - _mm_tn(n_ref[j, sl, :], gv)
                                + dso_ref[j, sl, :])

    return _pcall(body, "dn_rec_bwd", (steps,), [spec] * 3, spec, jax.ShapeDtypeStruct((nc, DN_WIDTH, HEAD), F32),
                  [n_mat, eg, ds_out], ("arbitrary",), scratch_shapes=[pltpu.VMEM((DN_WIDTH, HEAD), F32)], host=host)


def _dn_o_fwd(sall, q_eff, o_own):
    t = q_eff.shape[0]
    nb = _chunk_batch(t)
    rows, blk, _, mat = _dn_chunk_specs(t, nb)

    def body(s_ref, qe_ref, oo_ref, o_ref):
        r3 = lambda x: x.reshape(nb, CHUNK, HEAD)
        o_ref[...] = _dn_o_fn(s_ref[...], r3(qe_ref[...]), r3(oo_ref[...])).reshape(rows, HEAD)

    return _pcall(body, "dn_o_fwd", (HEADS, t // rows), [mat, blk, blk], blk, jax.ShapeDtypeStruct((t, DN_WIDTH), F32),
                  [sall, q_eff, o_own], ("arbitrary", "arbitrary"))


def _dn_o_bwd(sall, q_eff, do, host=None):
    t = q_eff.shape[0]
    nb = _chunk_batch(t)
    rows, blk, _, mat = _dn_chunk_specs(t, nb)

    def body(s_ref, qe_ref, do_ref, dqe_ref, ds_ref):
        r3 = lambda x: x.reshape(nb, CHUNK, HEAD)
        dov = r3(do_ref[...])
        dqe_ref[...] = _bmm_nt(dov, s_ref[...]).reshape(rows, HEAD)
        ds_ref[...] = _bmm_tn(r3(qe_ref[...]), dov)

    nc = t // CHUNK
    return _pcall(body, "dn_o_bwd", (HEADS, t // rows), [mat, blk, blk], [blk, mat],
                  [jax.ShapeDtypeStruct((t, DN_WIDTH), F32), jax.ShapeDtypeStruct((nc, DN_WIDTH, HEAD), F32)],
                  [sall, q_eff, do], ("arbitrary", "arbitrary"), host=host)


def _adamw_update(w, g, m, v):
    m2 = ADAM_B1 * m + (1.0 - ADAM_B1) * g
    v2 = ADAM_B2 * v + (1.0 - ADAM_B2) * (g * g)
    m_hat = m2 / (1.0 - ADAM_B1 ** ADAM_STEP)
    v_hat = v2 / (1.0 - ADAM_B2 ** ADAM_STEP)
    return -ADAM_LR * (m_hat / (jnp.sqrt(v_hat) + ADAM_EPS) + ADAM_WD * w), m2, v2


def _adamw_small(ws, gs, ms, vs):
    n = len(ws)

    def body(*refs):
        for i in range(n):
            d, m2, v2 = _adamw_update(refs[i][...], refs[n + i][...], refs[2 * n + i][...], refs[3 * n + i][...])
            refs[4 * n + i][...] = d
            refs[5 * n + i][...] = m2
            refs[6 * n + i][...] = v2

    shapes = [jax.ShapeDtypeStruct(a.shape, F32) for a in ws]
    outs = pl.pallas_call(body, name="adamw_small", out_shape=shapes * 3,
                          compiler_params=pltpu.CompilerParams(vmem_limit_bytes=VMEM_LIMIT))(*ws, *gs, *ms, *vs)
    return outs[:n], outs[n:2 * n], outs[2 * n:]


def _adamw_call(name, w, g, m, v, host=None):
    rows, cols = w.shape
    by_rows = rows % 8 == 0

    def body(w_ref, g_ref, m_ref, v_ref, g_out, d_ref, m_out, v_out):
        gv = g_ref[...] if by_rows else g_ref[0:rows, :]
        g_out[...] = gv
        d_ref[...], m_out[...], v_out[...] = _adamw_update(w_ref[...], gv, m_ref[...], v_ref[...])

    if by_rows:
        tr = _tile(rows, 256, 8)
        spec = g_spec = pl.BlockSpec((tr, cols), lambda i: (i, 0))
        grid = (rows // tr,)
    else:
        tc = _tile(cols, 256, 128)
        spec = pl.BlockSpec((rows, tc), lambda j: (0, j))
        g_spec = pl.BlockSpec((g.shape[0], tc), lambda j: (0, j))
        grid = (cols // tc,)
    return _pcall(body, name, grid, [spec, g_spec, spec, spec], [spec] * 4, [jax.ShapeDtypeStruct((rows, cols), F32)] * 4,
                  [w, g, m, v], ("arbitrary",), host=host)


def _adamw_rows3d(name, w, g, m, v):
    rows, _, cols = w.shape
    tr = max(d for d in range(1, 129) if rows % d == 0)

    def body(w_ref, g_ref, m_ref, v_ref, d_ref, m_out, v_out):
        d_ref[...], m_out[...], v_out[...] = _adamw_update(w_ref[...], g_ref[...], m_ref[...], v_ref[...])

    spec = pl.BlockSpec((tr, 1, cols), lambda i: (i, 0, 0))
    return pl.pallas_call(body, name=name, grid=(rows // tr,), in_specs=[spec] * 4, out_specs=[spec] * 3,
                          out_shape=[jax.ShapeDtypeStruct(w.shape, F32)] * 3, compiler_params=_cparams(("arbitrary",)))(
                              w, g, m, v)


def _rope_tables(t):
    half = ROPE // 2
    inv_freq = np.float32(ROPE_THETA) ** (-np.arange(half, dtype=np.float32) / np.float32(half))
    ang = np.arange(t, dtype=np.float32)[:, None] * inv_freq[None, :].astype(np.float32)
    z = np.zeros((t, HEAD - ROPE), np.float32)
    cos = np.concatenate([np.cos(ang), np.cos(ang), z], axis=1).astype(np.float32)
    sin = np.concatenate([np.sin(ang), np.sin(ang), z], axis=1).astype(np.float32)
    k = np.arange(HEAD)[:, None]
    l = np.arange(HEAD)[None, :]
    perm = np.where((l < half) & (k == l + half), -1.0, 0.0) + np.where((l >= half) & (l < ROPE) & (k == l - half), 1.0, 0.0)
    return jnp.asarray(cos), jnp.asarray(sin), jnp.asarray(perm.astype(np.float32))


def _win_to_pad(w):
    z = lambda n: jnp.zeros((n, w.shape[1]), w.dtype)
    return jnp.concatenate([w[576:2112], w[2112:2624], w[0:256], w[256:512], w[512:576], z(64), w[2624:2632], z(120)],
                           axis=0)


def _win_from_pad(g):
    return jnp.concatenate([g[2048:2304], g[2304:2560], g[2560:2624], g[0:1536], g[1536:2048], g[2688:2696]], axis=0)


def _qk_to_pad(w):
    w4 = w.reshape(HEADS, QK_DIM, w.shape[-1])
    return jnp.concatenate([w4, jnp.zeros((HEADS, QK_PAD - QK_DIM, w.shape[-1]), w.dtype)], axis=1).reshape(
        HEADS * QK_PAD, w.shape[-1])


def _qk_from_pad(g):
    return g.reshape(HEADS, QK_PAD, g.shape[-1])[:, :QK_DIM].reshape(HEADS * QK_DIM, g.shape[-1])


def _ff_to_pad(a, axis):
    shape = list(a.shape)
    shape[axis:axis + 1] = [N_CHIPS, FF_SHARD]
    a4 = a.reshape(shape)
    shape[axis + 1] = FF_BLOCK - FF_SHARD
    out = jnp.concatenate([a4, jnp.zeros(shape, a.dtype)], axis=axis + 1)
    shape[axis:axis + 2] = [D_FF_P]
    return out.reshape(shape)


def _ff_from_pad(a, axis):
    shape = list(a.shape)
    shape[axis:axis + 1] = [N_CHIPS, FF_BLOCK]
    a4 = lax.slice_in_dim(a.reshape(shape), 0, FF_SHARD, axis=axis + 1)
    shape[axis:axis + 2] = [D_FF]
    return a4.reshape(shape)


class _LocalPlan:
    def __init__(self, wt):
        self.wt, self.grads = wt, {}

    def weight(self, name):
        return self.wt[name]

    def host(self, point):
        return None

    def grad(self, name, value):
        self.grads[name] = value


def _local_step(x, tgt, wt, plan=None):
    plan = _LocalPlan(wt) if plan is None else plan
    s = x.shape[0]
    n_valid = N_META + s
    t = -(-n_valid // HEAD) * HEAD
    assert t - n_valid >= 3, "the DeltaNet conv kernels rely on at least three zero rows after the sequence"
    zpad = jnp.zeros((t - n_valid, D_MODEL), F32)
    h0 = jnp.concatenate([wt["meta_tokens"], x, zpad], axis=0)
    tgt_p = jnp.concatenate([jnp.zeros((N_META, D_MODEL), F32), tgt, zpad], axis=0)
    cos, sin, perm = _rope_tables(t)
    qn_w = jnp.concatenate([wt["q_norm_w"], jnp.zeros((1, QK_PAD - QK_DIM), F32)], axis=1)
    kn_w = jnp.concatenate([wt["k_norm_w"], jnp.zeros((1, QK_PAD - QK_DIM), F32)], axis=1)
    head_id = jnp.arange(DN_WIDTH)[None, :] // HEAD
    lane = jnp.arange(HEAD)[:, None]
    sel_a = (lane == head_id).astype(F32)
    sel_b = (lane == head_id + HEADS).astype(F32)
    alog = jnp.repeat(wt["dn_A_log"], HEAD, axis=1)
    dtb = jnp.repeat(wt["dn_dt_bias"], HEAD, axis=1)
    conv_w, conv_b = wt["ffn_conv_w"], wt["ffn_conv_b"]

    u = _rms_fwd("attn_norm_fwd", h0, wt["attn_norm_w"], host=plan.host("attn_norm_fwd"))
    win, wq, wkv = plan.weight("w_in_t"), plan.weight("w_q_t"), plan.weight("w_kv_t")
    proj = _matmul("in_proj", u, win, "nt", F32)
    z = (proj, DN_WIDTH, 3)
    q_lat, kv_lat, k_pe, ab = (proj, LORA, 8), (proj, LORA, 9), (proj, HEAD, 20), (proj, HEAD, 21)
    mla_consts = (wt["q_a_norm_w"], wq, wt["kv_a_norm_w"], wkv, qn_w, kn_w, perm)
    q, k, v = _mla_prep_fwd(q_lat, kv_lat, k_pe, cos, sin, *mla_consts)
    o_mla = _attn_fwd(q, k, v, host=plan.host("attn_fwd"))
    conv = _dn_conv_fwd(proj, wt["dn_conv_w"])
    dn_consts = (sel_a, sel_b, alog, dtb)
    qn, kn, g, beta = _dn_prep_fwd(conv, ab, *dn_consts)
    n_mat, b_mat, q_eff, o_own, eg = _dn_chunk_fwd(qn, kn, conv, g, beta, host=plan.host("dn_chunk_fwd"))
    sall = _dn_rec_fwd(n_mat, b_mat, eg)
    o_dn = _dn_o_fwd(sall, q_eff, o_own)
    w_out = plan.weight("w_out")
    mixed, h1, n2 = _mix_out_proj(o_mla, o_dn, z, wt["mla_out_norm_w"], wt["dn_out_norm_w"], w_out, h0, wt["ffn_norm_w"])
    w_gate, w_up = plan.weight("w_gate_t"), plan.weight("w_up_t")
    gpre, up, act = _ffn_glu_fwd(n2, w_gate, w_up, conv_w, conv_b, host=plan.host("ffn_glu_fwd"))
    w_down = plan.weight("w_down")
    dy, dy16, sq = _down_proj_loss(act, w_down, h1, tgt_p, n_valid)

    grads = {}
    plan.grad("w_down", _matmul("down_dw", act, dy16, "tn", BF16))
    dgpre, dup, grads["ffn_conv_w"], grads["ffn_conv_b"] = _ffn_glu_bwd(gpre, up, dy16, w_down, conv_w, conv_b)
    plan.grad("w_gate_t", _matmul("gate_dw", dgpre, n2, "tn", BF16))
    plan.grad("w_up_t", _matmul("up_dw", dup, n2, "tn", BF16))
    dh1, dh1_16, grads["ffn_norm_w"] = _ffn_in_bwd(dgpre, dup, w_gate, w_up, h1, dy, wt["ffn_norm_w"],
                                                   host=plan.host("ffn_in_bwd"))
    plan.grad("w_out", _matmul("out_dw", mixed, dh1_16, "tn", BF16))
    do_mla, do_dn, dz, grads["mla_out_norm_w"], grads["dn_out_norm_w"] = _mix_out_bwd(
        o_mla, o_dn, z, dh1_16, w_out, wt["mla_out_norm_w"], wt["dn_out_norm_w"], host=plan.host("mix_out_bwd"))
    dq_eff, ds_out = _dn_o_bwd(sall, q_eff, do_dn)
    gall = _dn_rec_bwd(n_mat, eg, ds_out)
    dqn, dkn, dv_dn, dg, dbeta = _dn_chunk_bwd(qn, kn, conv, g, beta, sall, gall, dq_eff, do_dn,
                                               host=plan.host("dn_chunk_bwd"))
    dconv, dab, dalog, ddtb = _dn_prep_bwd(conv, ab, dqn, dkn, dv_dn, dg, dbeta, *dn_consts)
    grads["dn_A_log"] = jnp.sum(dalog.reshape(HEADS, HEAD), axis=1)[None, :]
    grads["dn_dt_bias"] = jnp.sum(ddtb.reshape(HEADS, HEAD), axis=1)[None, :]
    ddn_pre, grads["dn_conv_w"] = _dn_conv_bwd(proj, wt["dn_conv_w"], dconv)
    dq, dk, dv = _attn_bwd(q, k, v, do_mla, host=plan.host("attn_bwd"))
    dq_lat, dkv_lat, dk_pe, dqa, dwq, dkva, dwkv, dqnw, dknw = _mla_prep_bwd(
        q_lat, kv_lat, k_pe, cos, sin, dq, dk, dv, *mla_consts, host=plan.host("mla_prep_bwd"))
    grads["q_a_norm_w"], grads["kv_a_norm_w"] = dqa, dkva
    plan.grad("w_q_t", dwq)
    plan.grad("w_kv_t", dwkv)
    grads["q_norm_w"], grads["k_norm_w"] = dqnw[:, :QK_DIM], dknw[:, :QK_DIM]
    dproj = [ddn_pre, dz, dq_lat, dkv_lat, dk_pe, dab]
    plan.grad("w_in_t", _in_proj_bwd_w(dproj, u))
    du = _in_proj_bwd_x(dproj, win, host=plan.host("in_dx"))
    dh0, _, grads["attn_norm_w"] = _rms_bwd("attn_norm_bwd", h0, wt["attn_norm_w"], [du], dh1,
                                            host=plan.host("attn_norm_bwd"))
    grads["meta_tokens"] = dh0[0:N_META]
    if isinstance(plan, _LocalPlan):
        grads.update(plan.grads)
    return sq, dh0[N_META:n_valid], grads


def _mesh_pos():
    return lax.axis_index("x"), lax.axis_index("y"), lax.axis_index("c")


def _other_chips(x, y):
    return [(1 - x, y), (x, 1 - y), (1 - x, 1 - y)]


def _remote(src, dst, send_sems, recv_sems, k, to):
    return pltpu.make_async_remote_copy(src_ref=src, dst_ref=dst, send_sem=send_sems.at[k], recv_sem=recv_sems.at[k],
                                        device_id=to, device_id_type=MESH)


SIBLING_ID, CHIPS_ID, GATHER_ID, ALL_ID = 1, 2, 3, 4


def _sibling_peer():
    x, y, c = _mesh_pos()
    return [(x, y, 1 - c)]


def _chip_peers():
    x, y, c = _mesh_pos()
    return [(qx, qy, c) for qx, qy in _other_chips(x, y)]


def _copies_exchange(make, ins, out_shape, nsem, peers=None, cid=None):
    def prog(in_refs, out_refs, send_sems, recv_sems):
        copies = make(in_refs, out_refs, send_sems, recv_sems)

        def start():
            for cp in copies:
                cp.start()

        def finish():
            for cp in copies:
                cp.wait()

        return start, finish

    return _Exchange(prog, ins, out_shape, nsem, peers, cid)


def _all_gather(shards):
    def prog(srcs, dsts, send_sems, recv_sems):
        x, y, c = _mesh_pos()
        p = 2 * x + y
        sibling = (x, y, 1 - c)
        chips = _other_chips(x, y)
        bufs = tuple((s, d, s.shape[0] // 2) for s, d in zip(srcs, dsts))

        def half(ref, rows, which):
            return ref.at[pl.ds(which * rows, rows), :]

        def copy(i, k, src, dst, to):
            return _remote(src, dst, send_sems, recv_sems, 6 * i + k, to)

        sends = [copy(i, j, half(src, rows, c), half(dst.at[p], rows, c), (*chip, c))
                 for i, (src, dst, rows) in enumerate(bufs) for j, chip in enumerate(chips)]

        def start():
            for cp in sends:
                cp.start()

        def finish():
            passed = []
            for i, (src, dst, rows) in enumerate(bufs):
                for j, (qx, qy) in enumerate(chips):
                    block = half(dst.at[2 * qx + qy], rows, c)
                    copy(i, j, block, block, (x, y, c)).wait_recv()
                    fwd = copy(i, 3 + j, block, block, sibling)
                    fwd.start()
                    passed.append(fwd)
            for i, (src, dst, rows) in enumerate(bufs):
                for j, (qx, qy) in enumerate(chips):
                    block = half(dst.at[2 * qx + qy], rows, 1 - c)
                    copy(i, 3 + j, block, block, (x, y, c)).wait_recv()
            for cp in sends + passed:
                cp.wait_send()

        return start, finish

    return _Exchange(prog, shards, [jax.ShapeDtypeStruct((N_CHIPS, *s.shape), s.dtype) for s in shards], 6 * len(shards),
                     lambda: _sibling_peer() + _chip_peers(), GATHER_ID)


def _all_gather_small(block):
    def make(srcs, dsts, send_sems, recv_sems):
        x, y, c = _mesh_pos()
        return [_remote(srcs[0], dsts[0].at[2 * x + y], send_sems, recv_sems, k, (qx, qy, c))
                for k, (qx, qy) in enumerate(_other_chips(x, y))]

    return _copies_exchange(make, [block], [jax.ShapeDtypeStruct((N_CHIPS, *block.shape), block.dtype)], 3, _chip_peers,
                            CHIPS_ID)


def _gathered(ex):
    p = 2 * lax.axis_index("x") + lax.axis_index("y")
    return [lax.dynamic_update_slice(g, s[None], (p, 0, 0)) for g, s in zip(ex.outs, ex.ins)]


def _rs_to_sibling(bufs):
    def make(srcs, dsts, send_sems, recv_sems):
        x, y, c = _mesh_pos()
        copies = []
        for i, (src, dst) in enumerate(zip(srcs, dsts)):
            half = src.shape[1] // 2
            copies.append(_remote(src.at[:, pl.ds((1 - c) * half, half), :], dst, send_sems, recv_sems, i, (x, y, 1 - c)))
        return copies

    return _copies_exchange(make, bufs,
                            [jax.ShapeDtypeStruct((N_CHIPS, b.shape[1] // 2, b.shape[2]), b.dtype) for b in bufs],
                            len(bufs), _sibling_peer, SIBLING_ID)


def _rs_pair_add(name, bufs, gots, c, out_dtype):
    n = len(bufs)

    def body(c_ref, *refs):
        for a_ref, b_ref, o_ref in zip(refs[:n], refs[n:2 * n], refs[2 * n:]):
            o_ref[...] = (a_ref[...].astype(F32) + b_ref[...].astype(F32)).astype(out_dtype)

    mine = [pl.BlockSpec((None, g.shape[1], g.shape[2]), lambda j, cr: (j, cr[0], 0)) for g in gots]
    whole = [pl.BlockSpec((None, g.shape[1], g.shape[2]), lambda j, cr: (j, 0, 0)) for g in gots]
    return pl.pallas_call(
        body, name=name,
        grid_spec=pltpu.PrefetchScalarGridSpec(num_scalar_prefetch=1, grid=(N_CHIPS,), in_specs=mine + whole, out_specs=whole),
        out_shape=[jax.ShapeDtypeStruct(g.shape, out_dtype) for g in gots],
        compiler_params=_cparams(("arbitrary",)))(c, *bufs, *gots)


def _rs_to_chips(accs):
    def make(srcs, dsts, send_sems, recv_sems):
        x, y, c = _mesh_pos()
        return [_remote(src.at[2 * qx + qy], dst.at[k], send_sems, recv_sems, 3 * i + k, (qx, qy, c))
                for i, (src, dst) in enumerate(zip(srcs, dsts)) for k, (qx, qy) in enumerate(_other_chips(x, y))]

    return _copies_exchange(make, accs, [jax.ShapeDtypeStruct((3, a.shape[1], a.shape[2]), a.dtype) for a in accs],
                            3 * len(accs), _chip_peers, CHIPS_ID)


def _rs_chip_add(name, accs, gots, p):
    n = len(accs)
    slot = (0, 1, 0, 2)

    def body(p_ref, *refs):
        me = p_ref[0]
        for own_ref, got_ref, o_ref in zip(refs[:n], refs[n:2 * n], refs[2 * n:]):
            total = None
            for chip in range(N_CHIPS):
                val = own_ref[...].astype(F32)
                for e in (1, 2, 3):
                    val = jnp.where((chip ^ me) == e, got_ref[slot[e]].astype(F32), val)
                total = val if total is None else total + val
            o_ref[...] = total

    own = [pl.BlockSpec((None, a.shape[1], a.shape[2]), lambda i, pr: (pr[0], 0, 0)) for a in accs]
    got = [pl.BlockSpec(g.shape, lambda i, pr: (0, 0, 0)) for g in gots]
    out = [pl.BlockSpec((a.shape[1], a.shape[2]), lambda i, pr: (0, 0)) for a in accs]
    return pl.pallas_call(
        body, name=name,
        grid_spec=pltpu.PrefetchScalarGridSpec(num_scalar_prefetch=1, grid=(1,), in_specs=own + got, out_specs=out),
        out_shape=[jax.ShapeDtypeStruct((a.shape[1], a.shape[2]), F32) for a in accs],
        compiler_params=_cparams(("arbitrary",)))(p, *accs, *gots)


def _rs_share(ress):
    def make(srcs, dsts, send_sems, recv_sems):
        x, y, c = _mesh_pos()
        return [_remote(src, dst, send_sems, recv_sems, i, (x, y, 1 - c)) for i, (src, dst) in enumerate(zip(srcs, dsts))]

    return _copies_exchange(make, ress, [jax.ShapeDtypeStruct(r.shape, F32) for r in ress], len(ress), _sibling_peer,
                            SIBLING_ID)


def _shared(ex):
    south = lax.axis_index("c") == 0
    return [jnp.concatenate([jnp.where(south, r, g), jnp.where(south, g, r)], axis=0) for r, g in zip(ex.ins, ex.outs)]


def _all_to_all_devices(vec):
    def others():
        x, y, c = _mesh_pos()
        return [((1 - x if r & 4 else x), (1 - y if r & 2 else y), (1 - c if r & 1 else c)) for r in range(1, 8)]

    def make(srcs, dsts, send_sems, recv_sems):
        x, y, c = _mesh_pos()
        me = 4 * x + 2 * y + c
        return [_remote(srcs[0], dsts[0].at[me], send_sems, recv_sems, r, peer) for r, peer in enumerate(others())]

    return _copies_exchange(make, [vec], [jax.ShapeDtypeStruct((8, *vec.shape), vec.dtype)], 7, others, ALL_ID)


def _sum_devices(stack):
    def body(s_ref, o_ref):
        total = s_ref[0]
        for d in range(1, 8):
            total = total + s_ref[d]
        o_ref[...] = total

    return pl.pallas_call(body, name="sum_devices", out_shape=jax.ShapeDtypeStruct(stack.shape[1:], F32),
                          compiler_params=pltpu.CompilerParams(vmem_limit_bytes=VMEM_LIMIT))(stack)


def _pad_rows(flat, rows):
    return jnp.concatenate([flat, jnp.zeros((rows * LANES - flat.shape[0],), flat.dtype)]).reshape(rows, LANES)


def _unshard(g4, shape, axis):
    a = g4.reshape(N_CHIPS, *shape)
    if axis == 0:
        return a.reshape(N_CHIPS * shape[0], shape[1])
    return jnp.transpose(a, (1, 0, 2)).reshape(shape[0], N_CHIPS * shape[1])


def _shard4(full, shape, axis):
    if axis == 0:
        return full.reshape(N_CHIPS, shape[0] * shape[1])
    a = full.reshape(shape[0], N_CHIPS, shape[1])
    return jnp.transpose(a, (1, 0, 2)).reshape(N_CHIPS, shape[0] * shape[1])


def _pad_axis0(a, rows):
    return jnp.concatenate([a, jnp.zeros((rows - a.shape[0], *a.shape[1:]), a.dtype)], axis=0)


def _pad_axis1(a, rows):
    return jnp.concatenate([a, jnp.zeros((a.shape[0], rows - a.shape[1], *a.shape[2:]), a.dtype)], axis=1)


def _shard_to_strip(name, w):
    _, (shape, axis, rows) = name, {n: (s, ax, r) for n, s, ax, r in BIG}[name]
    w2 = w.reshape(shape).astype(BF16)
    return _pad_axis0(w2.T if axis == 1 else w2, rows)


LOCAL_NAME = dict(w_in="w_in_t", w_q_b="w_q_t", w_kv_b="w_kv_t", w_out="w_out", w_gate="w_gate_t", w_up="w_up_t",
                  w_down="w_down")


WIN_SEGMENTS = ((576, 2112, 0), (2112, 2624, 1536), (0, 256, 2048), (256, 512, 2304), (512, 576, 2560), (2624, 2632, 2688))


def _strips_to_weight(name, g4):
    if name == "w_in":
        return _win_to_pad(g4[:, :IN_SHARD].reshape(IN_COLS, D_MODEL))
    if name == "w_q_b":
        return _qk_to_pad(g4.reshape(HEADS * QK_DIM, LORA))
    return g4.reshape(N_CHIPS * g4.shape[1], g4.shape[2])


def _grad_to_strips(name, g):
    if name == "w_in":
        strips = []
        for q in range(N_CHIPS):
            pieces = []
            for a, b, local in sorted(WIN_SEGMENTS):
                s, e = max(a, q * IN_SHARD), min(b, (q + 1) * IN_SHARD)
                if s < e:
                    pieces.append(g[local + s - a:local + e - a])
            pieces.append(jnp.zeros((IN_SHARD_P - IN_SHARD, D_MODEL), g.dtype))
            strips.append(jnp.concatenate(pieces, axis=0))
        return jnp.stack(strips)
    if name == "w_q_b":
        return _qk_from_pad(g).reshape(N_CHIPS, QK_DIM, LORA)
    return g.reshape(N_CHIPS, g.shape[0] // N_CHIPS, g.shape[1])


class _MeshPlan:
    LATE = dict(attn_norm_fwd=("w_in", "w_q_b", "w_kv_b"), attn_fwd=("w_up",), dn_chunk_fwd=("w_out", "w_gate"),
                ffn_glu_fwd=("w_down",))
    GROUP_A = ("w_down", "w_gate", "w_up", "w_out")
    GROUP_B = ("w_in", "w_q_b", "w_kv_b")

    def __init__(self, w):
        x, y, c = _mesh_pos()
        self.ci = jnp.reshape(c, (1,)).astype(jnp.int32)
        self.pi = jnp.reshape(2 * x + y, (1,)).astype(jnp.int32)
        self.strip = {n: _shard_to_strip(n, w[n]) for n, _, _, _ in BIG}
        self.gathers, self.weights, self.g, self.acc, self.reduced = {}, {}, {}, {}, {}
        self.sibs, self.sib, self.chip, self.share, self.halves = [], None, None, None, [None, None]

    def gather_small(self, small):
        ex = _all_gather_small(small)
        ex.run("all_gather_small")
        return _gathered(ex)[0]

    def weight(self, local_name):
        if local_name not in self.weights:
            for point, (names, ex) in list(self.gathers.items()):
                if ex.outs is not None:
                    for n, g4 in zip(names, _gathered(ex)):
                        if "/" in n:
                            n, half = n.split("/")
                            self.halves[int(half)] = g4
                            if None in self.halves:
                                continue
                            g4 = jnp.concatenate(self.halves, axis=1)
                        self.weights[LOCAL_NAME[n]] = _strips_to_weight(n, g4)
                    del self.gathers[point]
        return self.weights[local_name]

    def _shard(self, name):
        if "/" not in name:
            return self.strip[name]
        name, half = name.split("/")
        rows = self.strip[name].shape[0] // 2
        return self.strip[name][int(half) * rows:(int(half) + 1) * rows]

    def grad(self, local_name, value):
        name = {v: k for k, v in LOCAL_NAME.items()}[local_name]
        self.g[name] = _grad_to_strips(name, value)

    def _pair_add(self, names, gots):
        accs = _rs_pair_add("rs_pair_add_" + names[0], [self.g[n] for n in names], gots, self.ci, BF16)
        self.acc.update(zip(names, accs))

    def _chip_add(self, names, chip):
        return _rs_chip_add("rs_chip_add_" + names[0], [self.acc[n] for n in names], chip.outs, self.pi)

    def _take_shared(self, names, share):
        for n, strip in zip(names, _shared(share)):
            self.reduced[n] = strip

    def host(self, point):
        a, b = self.GROUP_A, self.GROUP_B
        if point in self.LATE:
            names = self.LATE[point]
            ex = _all_gather([self._shard(n) for n in names])
            self.gathers[point] = (names, ex)
            return ex
        if point in ("ffn_in_bwd", "mix_out_bwd"):
            names = dict(ffn_in_bwd=a[:3], mix_out_bwd=a[3:])[point]
            ex = _rs_to_sibling([self.g[n] for n in names])
            self.sibs.append(ex)
            return ex
        if point == "dn_chunk_bwd":
            self._pair_add(a, [o for ex in self.sibs for o in ex.outs])
            self.chip1 = _rs_to_chips([self.acc[n] for n in a[:2]])
            return self.chip1
        if point == "attn_bwd":
            self.chip2 = _rs_to_chips([self.acc[n] for n in a[2:]])
            return self.chip2
        if point == "mla_prep_bwd":
            ress = self._chip_add(a[:2], self.chip1) + self._chip_add(a[2:], self.chip2)
            self.share = _rs_share(ress)
            return self.share
        if point == "in_dx":
            self._take_shared(a, self.share)
            self.sib = _rs_to_sibling([self.g[n] for n in b])
            return self.sib
        if point == "attn_norm_bwd":
            self._pair_add(b, self.sib.outs)
            self.chip = _rs_to_chips([self.acc[n] for n in b])
            return self.chip
        return None

    def last_share(self):
        self.share = _rs_share(self._chip_add(self.GROUP_B, self.chip))
        return self.share

    def finish(self):
        self._take_shared(self.GROUP_B, self.share)
        return self.reduced


def _strip_to_shard(name, strip):
    shape, axis = {n: (s, ax) for n, s, ax, _ in BIG}[name]
    rows = shape[axis]
    return strip[:rows].T if axis == 1 else strip[:rows]


def kernel(x, meta_tokens, attn_norm_w, w_in, q_a_norm_w, w_q_b, kv_a_norm_w, w_kv_b, q_norm_w, k_norm_w, mla_out_norm_w, dn_conv_w, dn_A_log, dn_dt_bias, dn_out_norm_w, w_out, ffn_norm_w, w_gate, w_up, ffn_conv_w, ffn_conv_b, w_down, loss_target, m_meta_tokens, m_attn_norm_w, m_w_in, m_q_a_norm_w, m_w_q_b, m_kv_a_norm_w, m_w_kv_b, m_q_norm_w, m_k_norm_w, m_mla_out_norm_w, m_dn_conv_w, m_dn_A_log, m_dn_dt_bias, m_dn_out_norm_w, m_w_out, m_ffn_norm_w, m_w_gate, m_w_up, m_ffn_conv_w, m_ffn_conv_b, m_w_down, v_meta_tokens, v_attn_norm_w, v_w_in, v_q_a_norm_w, v_w_q_b, v_kv_a_norm_w, v_w_kv_b, v_q_norm_w, v_k_norm_w, v_mla_out_norm_w, v_dn_conv_w, v_dn_A_log, v_dn_dt_bias, v_dn_out_norm_w, v_w_out, v_ffn_norm_w, v_w_gate, v_w_up, v_ffn_conv_w, v_ffn_conv_b, v_w_down):
    local = dict(locals())
    w = {n: local[n] for n in WEIGHTS}
    m = {n: local["m_" + n] for n in WEIGHTS}
    v = {n: local["v_" + n] for n in WEIGHTS}
    p = 2 * lax.axis_index("x") + lax.axis_index("y")

    plan = _MeshPlan(w)
    wf = _pad_rows(jnp.concatenate([w[n].reshape(-1) for n, _, _ in SMALL_SHARDED]), SMALL_ROWS)
    gf = plan.gather_small(wf).reshape(N_CHIPS, -1)
    full = {}
    off = 0
    for n, s, ax in SMALL_SHARDED:
        full[n] = _unshard(gf[:, off:off + s[0] * s[1]], s, ax)
        off += s[0] * s[1]
    for n, _ in REPLICATED:
        full[n] = w[n]
    full["ffn_conv_w"] = _ff_to_pad(full["ffn_conv_w"], 1)
    full["ffn_conv_b"] = _ff_to_pad(full["ffn_conv_b"], 1)

    sq, grad_x, g = _local_step(x[0], loss_target[0], full, plan)
    g["ffn_conv_w"] = _ff_from_pad(g["ffn_conv_w"], 1)
    g["ffn_conv_b"] = _ff_from_pad(g["ffn_conv_b"], 1)

    small_all = [n for n, _, _ in SMALL_SHARDED] + [n for n, _ in REPLICATED]
    vec = jnp.concatenate([g[n].reshape(-1) for n in small_all] + [jnp.reshape(0.5 / D_MODEL * jnp.sum(sq), (1,))])
    vec = _pad_rows(vec, -(-vec.shape[0] // (8 * LANES)) * 8)
    a2a = _all_to_all_devices(vec)

    gs, delta, new_m, new_v = {}, {}, {}, {}
    big = {n: (s, ax) for n, s, ax, _ in BIG}

    def adamw_big(n, strips, host=None):
        s, ax = big[n]
        if ax == 1 and s[1] % 8:
            there = lambda a: jnp.transpose(a, (2, 0, 1))
            back = lambda a: jnp.transpose(a, (1, 2, 0))
            g3 = strips[n][:s[1]].reshape(s[1], 1, s[0])
            d2, m2, v2 = _adamw_rows3d("adamw_" + n, there(w[n]), g3, there(m[n]), there(v[n]))
            gs[n], delta[n], new_m[n], new_v[n] = back(g3), back(d2), back(m2), back(v2)
            return
        flip = ax == 1 and s[1] % 8 == 0
        there = (lambda a: a.reshape(s).T) if flip else (lambda a: a.reshape(s))
        back = (lambda a: a.T.reshape(w[n].shape)) if flip else (lambda a: a.reshape(w[n].shape))
        strip = strips[n] if flip or ax == 0 else strips[n][:s[1]].T
        g2, d2, m2, v2 = _adamw_call("adamw_" + n, there(w[n]), strip, there(m[n]), there(v[n]), host=host)
        gs[n], delta[n], new_m[n], new_v[n] = back(g2), back(d2), back(m2), back(v2)

    adamw_big("w_down", plan.reduced, host=a2a)
    adamw_big("w_gate", plan.reduced, host=plan.last_share())
    adamw_big("w_up", plan.reduced)
    adamw_big("w_out", plan.reduced)
    strips = plan.finish()
    for n in plan.GROUP_B:
        adamw_big(n, strips)
    me = 4 * lax.axis_index("x") + 2 * lax.axis_index("y") + lax.axis_index("c")
    red = _sum_devices(lax.dynamic_update_slice(a2a.outs[0], vec[None], (me, 0, 0))).reshape(-1)
    off = 0
    for n in small_all:
        tot = red[off:off + g[n].size].reshape(g[n].shape)
        off += g[n].size
        shard = {sn: (s, ax) for sn, s, ax in SMALL_SHARDED}.get(n)
        if shard is not None:
            tot = lax.dynamic_slice_in_dim(tot, p * shard[0][1], shard[0][1], axis=1)
        gs[n] = tot
    loss = red[off]
    two_d = lambda a: a.reshape(a.shape[-2], a.shape[-1])
    outs = _adamw_small([two_d(w[n]) for n in small_all], [two_d(gs[n]) for n in small_all],
                        [two_d(m[n]) for n in small_all], [two_d(v[n]) for n in small_all])
    for i, n in enumerate(small_all):
        for dst, src in ((delta, outs[0]), (new_m, outs[1]), (new_v, outs[2])):
            dst[n] = src[i].reshape(w[n].shape)

    grad_out = [gs[n].reshape(w[n].shape) for n in WEIGHTS]
    return (loss, grad_x[None], *grad_out, *[delta[n] for n in WEIGHTS], *[new_m[n] for n in WEIGHTS],
            *[new_v[n] for n in WEIGHTS])
```

```python
import functools
import math

import jax
import jax.numpy as jnp
import numpy as np
from jax import lax
from jax.experimental import pallas as pl
from jax.experimental.pallas import tpu as pltpu

F32 = jnp.float32
BF16 = jnp.bfloat16
HI = lax.Precision.HIGHEST
MESH = pl.DeviceIdType.MESH

N_META = 16
D_MODEL = 1024
HEADS = 4
HEAD = 128
ROPE = 64
QK_DIM = HEAD + ROPE
QK_PAD = 2 * HEAD
LORA = 256
DN_WIDTH = HEADS * HEAD
CHUNK = 64
D_FF = 2816
N_CHIPS = 4
FF_SHARD = D_FF // N_CHIPS
FF_BLOCK = 768
D_FF_P = N_CHIPS * FF_BLOCK
IN_COLS = 2632
IN_SHARD = IN_COLS // N_CHIPS
IN_SHARD_P = 672
IN_PAD = 2816
NORM_EPS = 1e-6
ROPE_THETA = 10000.0
LANES = 512

ADAM_LR, ADAM_B1, ADAM_B2, ADAM_EPS, ADAM_WD, ADAM_STEP = 0.001, 0.9, 0.999, 1e-08, 0.01, 10

VMEM_LIMIT = 56 * 1024 * 1024

BIG = (("w_in", (1024, 658), 1, IN_SHARD_P), ("w_q_b", (256, 192), 1, 192), ("w_kv_b", (256, 256), 1, 256),
       ("w_out", (256, 1024), 0, 256), ("w_gate", (1024, 704), 1, FF_BLOCK), ("w_up", (1024, 704), 1, FF_BLOCK),
       ("w_down", (704, 1024), 0, FF_BLOCK))
SMALL_SHARDED = (("meta_tokens", (16, 256), 1), ("dn_conv_w", (4, 384), 1), ("ffn_conv_w", (3, 704), 1))
REPLICATED = (("attn_norm_w", 1024), ("q_a_norm_w", 256), ("kv_a_norm_w", 256), ("q_norm_w", 192), ("k_norm_w", 192),
              ("mla_out_norm_w", 128), ("dn_A_log", 4), ("dn_dt_bias", 4), ("dn_out_norm_w", 128), ("ffn_norm_w", 1024),
              ("ffn_conv_b", 2816))
WEIGHTS = ("meta_tokens", "attn_norm_w", "w_in", "q_a_norm_w", "w_q_b", "kv_a_norm_w", "w_kv_b", "q_norm_w", "k_norm_w",
           "mla_out_norm_w", "dn_conv_w", "dn_A_log", "dn_dt_bias", "dn_out_norm_w", "w_out", "ffn_norm_w", "w_gate",
           "w_up", "ffn_conv_w", "ffn_conv_b", "w_down")

SMALL_ROWS = 16
REP_ROWS = 16


def _cparams(sem):
    return pltpu.CompilerParams(dimension_semantics=sem, vmem_limit_bytes=VMEM_LIMIT)


class _Exchange:
    def __init__(self, prog, ins, out_shape, nsem, peers=None, cid=None):
        self.prog, self.ins, self.out_shape, self.nsem = prog, list(ins), list(out_shape), nsem
        self.peers, self.cid = peers, cid
        self.outs = None

    def sems(self):
        return [pltpu.SemaphoreType.DMA((self.nsem,)), pltpu.SemaphoreType.DMA((self.nsem,))]

    def programs(self, in_refs, out_refs, send_sems, recv_sems):
        start, finish = self.prog(in_refs, out_refs, send_sems, recv_sems)
        if self.cid is None:
            return start, finish
        peers = self.peers()

        def shake_and_start():
            barrier = pltpu.get_barrier_semaphore()
            for peer in peers:
                pl.semaphore_signal(barrier, inc=1, device_id=peer, device_id_type=MESH)
            pl.semaphore_wait(barrier, len(peers))
            start()

        return shake_and_start, finish

    def cparams(self, **kw):
        return pltpu.CompilerParams(has_side_effects=True, collective_id=self.cid, **kw)

    def run(self, name):
        any_spec = pl.BlockSpec(memory_space=pl.ANY)
        n = len(self.ins)

        def body(*refs):
            start, finish = self.programs(refs[:n], refs[n:-2], refs[-2], refs[-1])
            start()
            finish()

        self.outs = pl.pallas_call(
            body, name=name, in_specs=[any_spec] * n, out_specs=[any_spec] * len(self.out_shape),
            out_shape=self.out_shape, scratch_shapes=self.sems(), compiler_params=self.cparams())(*self.ins)
        return self.outs


def _pcall(body, name, grid, in_specs, out_specs, out_shape, args, sem, scratch_shapes=(), host=None):
    single = not isinstance(out_shape, (list, tuple))
    out_specs, out_shape = ([out_specs], [out_shape]) if single else (list(out_specs), list(out_shape))
    if host is None:
        outs = pl.pallas_call(body, name=name, grid=grid, in_specs=list(in_specs), out_specs=out_specs, out_shape=out_shape,
                              scratch_shapes=list(scratch_shapes), compiler_params=_cparams(sem))(*args)
        return outs[0] if single else outs
    any_spec = pl.BlockSpec(memory_space=pl.ANY)
    n_in, n_out, n_scr, nx_in, nx_out = len(in_specs), len(out_specs), len(scratch_shapes), len(host.ins), len(host.out_shape)

    def hosted(*refs):
        c_in, x_in = refs[:n_in], refs[n_in:n_in + nx_in]
        o0 = n_in + nx_in
        c_out, x_out = refs[o0:o0 + n_out], refs[o0 + n_out:o0 + n_out + nx_out]
        s0 = o0 + n_out + nx_out
        start, finish = host.programs(x_in, x_out, refs[s0 + n_scr], refs[s0 + n_scr + 1])
        first = functools.reduce(jnp.logical_and, [pl.program_id(d) == 0 for d in range(len(grid))])
        last = functools.reduce(jnp.logical_and, [pl.program_id(d) == grid[d] - 1 for d in range(len(grid))])
        pl.when(first)(start)
        body(*c_in, *c_out, *refs[s0:s0 + n_scr])
        pl.when(last)(finish)

    outs = pl.pallas_call(
        hosted, name=name, grid=grid, in_specs=list(in_specs) + [any_spec] * nx_in,
        out_specs=out_specs + [any_spec] * nx_out, out_shape=out_shape + host.out_shape,
        scratch_shapes=list(scratch_shapes) + host.sems(),
        compiler_params=host.cparams(dimension_semantics=sem, vmem_limit_bytes=VMEM_LIMIT))(*args, *host.ins)
    host.outs = outs[n_out:]
    return outs[0] if single else outs[:n_out]


NN, NT, TN = ((1,), (0,)), ((1,), (1,)), ((0,), (0,))


def _shift_dims(dims, batch):
    if not batch:
        return (dims, ((), ()))
    return (((dims[0][0] + 1,), (dims[1][0] + 1,)), ((0,), (0,)))


def _make_mm(dims, exact, batch=False):
    def raw(a, b, d):
        dn = _shift_dims(d, batch)
        if exact == "split_lhs":
            ah, bh = a.astype(BF16), b.astype(BF16)
            al = (a - ah.astype(F32)).astype(BF16)
            return lax.dot_general(ah, bh, dn, preferred_element_type=F32) + lax.dot_general(al, bh, dn,
                                                                                              preferred_element_type=F32)
        if exact == "split":
            ah, bh = a.astype(BF16), b.astype(BF16)
            al, bl = (a - ah.astype(F32)).astype(BF16), (b - bh.astype(F32)).astype(BF16)
            dot = lambda p, q: lax.dot_general(p, q, dn, preferred_element_type=F32)
            return dot(ah, bh) + (dot(ah, bl) + dot(al, bh))
        if exact:
            return lax.dot_general(a.astype(F32), b.astype(F32), dn, precision=HI, preferred_element_type=F32)
        return lax.dot_general(a.astype(BF16), b.astype(BF16), dn, preferred_element_type=F32)

    @jax.custom_vjp
    def mm(a, b):
        return raw(a, b, dims)

    def fwd(a, b):
        return raw(a, b, dims), (a, b)

    def bwd(res, g):
        a, b = res
        if dims == NN:
            da, db = raw(g, b, NT), raw(a, g, TN)
        elif dims == NT:
            da, db = raw(g, b, NN), raw(g, a, TN)
        else:
            da, db = raw(b, g, NT), raw(a, g, NN)
        return da.astype(a.dtype), db.astype(b.dtype)

    mm.defvjp(fwd, bwd)
    return mm


_mm = _make_mm(NN, False)
_mm_nt = _make_mm(NT, False)
_mm_tn = _make_mm(TN, False)
_mmx = _make_mm(NN, "split_lhs")
_bmm = _make_mm(NN, False, batch=True)
_bmm_nt = _make_mm(NT, False, batch=True)
_bmm_tn = _make_mm(TN, False, batch=True)
_bmmx = _make_mm(NN, True, batch=True)
_bmms = _make_mm(NN, "split", batch=True)
_bmms_nt = _make_mm(NT, "split", batch=True)
_bmms_tn = _make_mm(TN, "split", batch=True)


@jax.custom_vjp
def _unit_lower_inv(a):
    n = a.shape[-1]
    eye = (lax.broadcasted_iota(jnp.int32, a.shape, 1) == lax.broadcasted_iota(jnp.int32, a.shape, 2)).astype(F32)
    x = -a
    t = eye + x
    for _ in range(max(n.bit_length() - 2, 0)):
        x = _bmms(x, x)
        t = t + _bmms(t, x)
    return t


def _unit_lower_inv_fwd(a):
    t = _unit_lower_inv(a)
    return t, t


def _unit_lower_inv_bwd(t, g):
    return (-_bmms_tn(t, _bmms_nt(g, t)),)


_unit_lower_inv.defvjp(_unit_lower_inv_fwd, _unit_lower_inv_bwd)


def _scan_chunk_rows(x, reverse):
    nb, c, w = x.shape
    y = x.reshape(nb * c, w)
    pos = lax.broadcasted_iota(jnp.int32, y.shape, 0) % c
    step = 1
    while step < c:
        if reverse:
            y = y + jnp.where(pos < c - step, pltpu.roll(y, nb * c - step, 0), 0.0)
        else:
            y = y + jnp.where(pos >= step, pltpu.roll(y, step, 0), 0.0)
        step *= 2
    return y.reshape(nb, c, w)


@jax.custom_vjp
def _chunk_cumsum(x):
    return _scan_chunk_rows(x, False)


_chunk_cumsum.defvjp(lambda x: (_scan_chunk_rows(x, False), None), lambda _, g: (_scan_chunk_rows(g, True),))


def _rms(x, w, n):
    ms = jnp.sum(x * x, axis=-1, keepdims=True) * (1.0 / n)
    return x * lax.rsqrt(ms + NORM_EPS) * w


def _silu(x):
    return x * jax.nn.sigmoid(x)


def _softplus(x):
    return jnp.maximum(x, 0.0) + jnp.log(1.0 + jnp.exp(-jnp.abs(x)))


def _rope(x, cos, sin, perm):
    return x * cos + _mmx(x, perm) * sin


def _mla_prep_fn(rows, consts):
    q_lat, kv_lat, k_pe, cos, sin = rows
    qn = _rms(q_lat, consts["qa_w"], LORA)
    kvn = _rms(kv_lat, consts["kva_w"], LORA)
    outs = []
    for h in range(HEADS):
        q_n = _mm_nt(qn, consts["wq_n"][h])
        q_r = _mm_nt(qn, consts["wq_r"][h])
        rs = lax.rsqrt((jnp.sum(q_n * q_n, -1, keepdims=True) + jnp.sum(q_r * q_r, -1, keepdims=True)) * (1.0 / QK_DIM)
                       + NORM_EPS)
        q_n = q_n * rs * consts["qn_n"]
        q_r = _rope(q_r * rs * consts["qn_r"], cos, sin, consts["perm"])
        k_n = _mm_nt(kvn, consts["wk_n"][h])
        v = _mm_nt(kvn, consts["wv"][h])
        rk = lax.rsqrt((jnp.sum(k_n * k_n, -1, keepdims=True) + jnp.sum(k_pe * k_pe, -1, keepdims=True)) * (1.0 / QK_DIM)
                       + NORM_EPS)
        k_n = k_n * rk * consts["kn_n"]
        k_r = _rope(k_pe * rk * consts["kn_r"], cos, sin, consts["perm"])
        outs += [q_n, q_r, k_n, k_r, v]
    return tuple(outs)


def _attn_fn(q, k, v, row0):
    s = _mm_nt(q, k) * (1.0 / math.sqrt(QK_DIM))
    qpos = row0 + lax.broadcasted_iota(jnp.int32, s.shape, 0)
    kpos = lax.broadcasted_iota(jnp.int32, s.shape, 1)
    s = jnp.where(kpos <= qpos, s, -1e30)
    m = lax.stop_gradient(jnp.max(s, axis=-1, keepdims=True))
    p = jnp.exp(s - m)
    p = p / jnp.sum(p, axis=-1, keepdims=True)
    return _mm(p, v)


def _dn_prep_fn(rows, consts):
    qc, kc, ab = rows
    a_b = _mmx(ab, consts["sel_a"])
    b_b = _mmx(ab, consts["sel_b"])
    beta = jax.nn.sigmoid(b_b)
    g = -jnp.exp(consts["alog"]) * _softplus(a_b + consts["dtb"])
    qs, ks = [], []
    for h in range(HEADS):
        q, k = qc[h], kc[h]
        qs.append(q * lax.rsqrt(jnp.sum(q * q, -1, keepdims=True) + NORM_EPS))
        ks.append(k * lax.rsqrt(jnp.sum(k * k, -1, keepdims=True) + NORM_EPS))
    return tuple(qs), tuple(ks), g, beta


def _dn_chunk_fn(q, k, v, gb, g64, bb):
    nb = q.shape[0]
    ri = lax.broadcasted_iota(jnp.int32, (nb, CHUNK, CHUNK), 1)
    ci = lax.broadcasted_iota(jnp.int32, (nb, CHUNK, CHUNK), 2)
    tri = ri >= ci
    strict = ri > ci
    tril = tri.astype(F32)
    eye = (ri == ci).astype(F32)
    ones = jnp.ones((nb, CHUNK, CHUNK), F32)
    gc = _chunk_cumsum(gb)
    gc64 = _chunk_cumsum(g64)
    grow = _bmmx(ones, eye * gc64)
    diff = gc64 - grow
    decay = jnp.where(tri, jnp.exp(jnp.where(tri, diff, 0.0)), 0.0)
    kb = k * bb
    vb = v * bb
    a = jnp.where(strict, _bmm_nt(kb, k) * decay, 0.0)
    tinv = _unit_lower_inv(a)
    u = _bmm(tinv, vb)
    w = _bmm(tinv, kb * jnp.exp(gc))
    qs = q * (1.0 / math.sqrt(HEAD))
    qk = _bmm_nt(qs, k) * decay
    qg = qs * jnp.exp(gc)
    glast = jnp.sum(gb, axis=1, keepdims=True)
    kdec = k * jnp.exp(glast - gc)
    n_mat = _bmm_tn(kdec, w)
    b_mat = _bmm_tn(kdec, u)
    q_eff = qg - _bmm(qk, w)
    o_own = _bmm(qk, u)
    return n_mat, b_mat, q_eff, o_own, jnp.exp(glast)


def _dn_rec_fn(s, n_mat, b_mat, eg):
    return s * eg - _mm(n_mat, s) + b_mat


def _dn_o_fn(s, q_eff, o_own):
    return _bmm(q_eff, s) + o_own


def _dn_out_fn(o, z, w):
    return _rms(o, w, HEAD) * _silu(z)


def _row_tile(t, parts=8):
    return t // parts if (t // parts) % 16 == 0 else t


def _tile(n, pref, unit):
    best = n
    for cand in range(unit, min(n, pref) + 1, unit):
        if n % cand == 0:
            best = cand
    return best if best <= pref else n


def _rows_call(name, body, rows, consts, outs, accs, r, host=None):
    rows = [a if isinstance(a, tuple) else (a, a.shape[1], 0) for a in rows]
    t = rows[0][0].shape[0]
    zero = lambda nd: (lambda i: (0,) * nd)
    in_specs = [pl.BlockSpec((r, w), functools.partial(lambda i, b: (i, b), b=blk)) for _, w, blk in rows]
    rows = [a for a, _, _ in rows]
    in_specs += [pl.BlockSpec(a.shape, zero(a.ndim)) for a in consts]
    out_shape = [jax.ShapeDtypeStruct((t, w), dt) for w, dt in outs] + [jax.ShapeDtypeStruct(s, F32) for s in accs]
    out_specs = [pl.BlockSpec((r, w), lambda i: (i, 0)) for w, _ in outs] + [pl.BlockSpec(s, zero(len(s))) for s in accs]
    return _pcall(body, name, (t // r,), in_specs, out_specs, out_shape, [*rows, *consts], ("arbitrary",), host=host)


def _accumulate(ref, val):
    @pl.when(pl.program_id(0) == 0)
    def _():
        ref[...] = jnp.zeros(ref.shape, ref.dtype)

    ref[...] += val


def _matmul(name, a, b, dims, out_dtype, res=None, host=None):
    if dims == "nn":
        (m, k), n = a.shape, b.shape[1]
    elif dims == "nt":
        (m, k), n = a.shape, b.shape[0]
    else:
        (k, m), n = a.shape, b.shape[1]
    tm = _tile(m, 1100, 16) if dims != "tn" else _tile(m, 640, 128)
    tn = _tile(n, 1408, 128)
    if dims == "nn":
        a_spec, b_spec, dn = pl.BlockSpec((tm, k), lambda i, j: (i, 0)), pl.BlockSpec((k, tn), lambda i, j: (0, j)), NN
    elif dims == "nt":
        a_spec, b_spec, dn = pl.BlockSpec((tm, k), lambda i, j: (i, 0)), pl.BlockSpec((tn, k), lambda i, j: (j, 0)), NT
    else:
        a_spec, b_spec, dn = pl.BlockSpec((k, tm), lambda i, j: (0, i)), pl.BlockSpec((k, tn), lambda i, j: (0, j)), TN
    o_spec = pl.BlockSpec((tm, tn), lambda i, j: (i, j))

    def body(*refs):
        a_ref, b_ref, o_ref = refs[0], refs[1], refs[-1]
        acc = lax.dot_general(a_ref[...].astype(BF16), b_ref[...].astype(BF16), (dn, ((), ())),
                              preferred_element_type=F32)
        if res is not None:
            acc = acc + refs[2][...]
        o_ref[...] = acc.astype(out_dtype)

    ins = [a, b] + ([res] if res is not None else [])
    specs = [a_spec, b_spec] + ([o_spec] if res is not None else [])
    return _pcall(body, name, (m // tm, n // tn), specs, o_spec, jax.ShapeDtypeStruct((m, n), out_dtype), ins,
                  ("arbitrary", "arbitrary"), host=host)


def _rms_fwd(name, h, w, host=None):
    n = h.shape[1]

    def body(h_ref, w_ref, o_ref):
        o_ref[...] = _rms(h_ref[...], w_ref[...], n).astype(BF16)

    return _rows_call(name, body, [h], [w], [(n, BF16)], [], _row_tile(h.shape[0]), host=host)[0]


def _rms_bwd(name, h, w, cts, resid, host=None):
    n = h.shape[1]
    nct = len(cts)

    def body(*refs):
        h_ref, ct_refs, r_ref, w_ref = refs[0], refs[1:1 + nct], refs[1 + nct], refs[2 + nct]
        dh_ref, dh16_ref, dw_ref = refs[-3], refs[-2], refs[-1]
        ct = ct_refs[0][...].astype(F32)
        for c in ct_refs[1:]:
            ct = ct + c[...].astype(F32)
        _, vjp = jax.vjp(lambda x, ww: _rms(x, ww, n), h_ref[...], w_ref[...])
        dh, dw = vjp(ct)
        dh = dh + r_ref[...]
        dh_ref[...] = dh
        dh16_ref[...] = dh.astype(BF16)
        _accumulate(dw_ref, dw)

    return _rows_call(name, body, [h, *cts, resid], [w], [(n, F32), (n, BF16)], [(1, n)], _row_tile(h.shape[0]), host=host)


def _mla_consts_from_refs(qa, wq, kva, wkv, qn, kn, perm):
    f = lambda r: r[...].astype(F32)
    return dict(
        qa_w=f(qa), kva_w=f(kva), perm=f(perm),
        wq_n=[wq[h * QK_PAD:h * QK_PAD + HEAD, :].astype(F32) for h in range(HEADS)],
        wq_r=[wq[h * QK_PAD + HEAD:(h + 1) * QK_PAD, :].astype(F32) for h in range(HEADS)],
        wk_n=[wkv[h * QK_PAD:h * QK_PAD + HEAD, :].astype(F32) for h in range(HEADS)],
        wv=[wkv[h * QK_PAD + HEAD:(h + 1) * QK_PAD, :].astype(F32) for h in range(HEADS)],
        qn_n=qn[:, 0:HEAD], qn_r=qn[:, HEAD:QK_PAD], kn_n=kn[:, 0:HEAD], kn_r=kn[:, HEAD:QK_PAD])


def _mla_prep_fwd(q_lat, kv_lat, k_pe, cos, sin, qa, wq, kva, wkv, qn, kn, perm):
    def body(ql, kvl, kp, c, s, qa_r, wq_r, kva_r, wkv_r, qn_r, kn_r, p_r, q_out, k_out, v_out):
        consts = _mla_consts_from_refs(qa_r, wq_r, kva_r, wkv_r, qn_r, kn_r, p_r)
        outs = _mla_prep_fn((ql[...], kvl[...], kp[...], c[...], s[...]), consts)
        for h in range(HEADS):
            q_n, q_r, k_n, k_r, v = outs[5 * h:5 * h + 5]
            q_out[:, h * QK_PAD:h * QK_PAD + HEAD] = q_n.astype(BF16)
            q_out[:, h * QK_PAD + HEAD:(h + 1) * QK_PAD] = q_r.astype(BF16)
            k_out[:, h * QK_PAD:h * QK_PAD + HEAD] = k_n.astype(BF16)
            k_out[:, h * QK_PAD + HEAD:(h + 1) * QK_PAD] = k_r.astype(BF16)
            v_out[:, h * HEAD:(h + 1) * HEAD] = v.astype(BF16)

    return _rows_call("mla_prep_fwd", body, [q_lat, kv_lat, k_pe, cos, sin], [qa, wq, kva, wkv, qn, kn, perm],
                      [(HEADS * QK_PAD, BF16), (HEADS * QK_PAD, BF16), (DN_WIDTH, BF16)], [], _row_tile(cos.shape[0], 4))


def _mla_prep_bwd(q_lat, kv_lat, k_pe, cos, sin, dq, dk, dv, qa, wq, kva, wkv, qn, kn, perm, host=None):
    def body(ql, kvl, kp, c, s, dq_r, dk_r, dv_r, qa_r, wq_r, kva_r, wkv_r, qn_r, kn_r, p_r,
             dql, dkvl, dkp, dqa, dwq, dkva, dwkv, dqn, dkn):
        consts = _mla_consts_from_refs(qa_r, wq_r, kva_r, wkv_r, qn_r, kn_r, p_r)
        cc, ss, pm = c[...], s[...], consts.pop("perm")
        _, vjp = jax.vjp(lambda rows, cs: _mla_prep_fn((*rows, cc, ss), dict(cs, perm=pm)), (ql[...], kvl[...], kp[...]),
                         consts)
        cts = []
        for h in range(HEADS):
            cts += [dq_r[:, h * QK_PAD:h * QK_PAD + HEAD], dq_r[:, h * QK_PAD + HEAD:(h + 1) * QK_PAD],
                    dk_r[:, h * QK_PAD:h * QK_PAD + HEAD], dk_r[:, h * QK_PAD + HEAD:(h + 1) * QK_PAD],
                    dv_r[:, h * HEAD:(h + 1) * HEAD]]
        (d_ql, d_kvl, d_kp), dc = vjp(tuple(cts))
        dql[...] = d_ql.astype(BF16)
        dkvl[...] = d_kvl.astype(BF16)
        dkp[...] = d_kp.astype(BF16)
        first = pl.program_id(0) == 0

        def acc(ref, sl, val):
            @pl.when(first)
            def _():
                ref[sl] = val

            @pl.when(jnp.logical_not(first))
            def _():
                ref[sl] += val

        full = (slice(None), slice(None))
        acc(dqa, full, dc["qa_w"])
        acc(dkva, full, dc["kva_w"])
        for h in range(HEADS):
            acc(dwq, (slice(h * QK_PAD, h * QK_PAD + HEAD), slice(None)), dc["wq_n"][h])
            acc(dwq, (slice(h * QK_PAD + HEAD, (h + 1) * QK_PAD), slice(None)), dc["wq_r"][h])
            acc(dwkv, (slice(h * QK_PAD, h * QK_PAD + HEAD), slice(None)), dc["wk_n"][h])
            acc(dwkv, (slice(h * QK_PAD + HEAD, (h + 1) * QK_PAD), slice(None)), dc["wv"][h])
        acc(dqn, (slice(None), slice(0, HEAD)), dc["qn_n"])
        acc(dqn, (slice(None), slice(HEAD, QK_PAD)), dc["qn_r"])
        acc(dkn, (slice(None), slice(0, HEAD)), dc["kn_n"])
        acc(dkn, (slice(None), slice(HEAD, QK_PAD)), dc["kn_r"])

    return _rows_call("mla_prep_bwd", body, [q_lat, kv_lat, k_pe, cos, sin, dq, dk, dv],
                      [qa, wq, kva, wkv, qn, kn, perm],
                      [(LORA, BF16), (LORA, BF16), (HEAD, BF16)],
                      [(1, LORA), wq.shape, (1, LORA), wkv.shape, (1, QK_PAD), (1, QK_PAD)], _row_tile(cos.shape[0], 4),
                      host=host)


ATTN_Q_ROWS = 512


def _attn_blocks(t):
    return [(r0, min(ATTN_Q_ROWS, t - r0)) for r0 in range(0, t, ATTN_Q_ROWS)]


def _attn_fwd(q, k, v, host=None):
    t = q.shape[0]

    def body(q_ref, k_ref, v_ref, o_ref):
        for r0, rows in _attn_blocks(t):
            ext = r0 + rows
            o_ref[r0:ext, :] = _attn_fn(q_ref[r0:ext, :], k_ref[0:ext, :], v_ref[0:ext, :], r0)

    qk_spec = pl.BlockSpec((t, QK_PAD), lambda h: (0, h))
    v_spec = pl.BlockSpec((t, HEAD), lambda h: (0, h))
    return _pcall(body, "attn_fwd", (HEADS,), [qk_spec, qk_spec, v_spec], v_spec,
                  jax.ShapeDtypeStruct((t, HEADS * HEAD), F32), [q, k, v], ("arbitrary",), host=host)


def _attn_bwd(q, k, v, do, host=None):
    t = q.shape[0]

    def body(q_ref, k_ref, v_ref, do_ref, dq_ref, dk_ref, dv_ref):
        dk_ref[...] = jnp.zeros(dk_ref.shape, F32)
        dv_ref[...] = jnp.zeros(dv_ref.shape, F32)
        for r0, rows in _attn_blocks(t):
            ext = r0 + rows
            _, vjp = jax.vjp(functools.partial(_attn_fn, row0=r0), q_ref[r0:ext, :].astype(F32),
                             k_ref[0:ext, :].astype(F32), v_ref[0:ext, :].astype(F32))
            dq, dk, dv = vjp(do_ref[r0:ext, :])
            dq_ref[r0:ext, :] = dq
            dk_ref[0:ext, :] += dk
            dv_ref[0:ext, :] += dv

    qk_spec = pl.BlockSpec((t, QK_PAD), lambda h: (0, h))
    v_spec = pl.BlockSpec((t, HEAD), lambda h: (0, h))
    return _pcall(body, "attn_bwd", (HEADS,), [qk_spec, qk_spec, v_spec, v_spec], [qk_spec, qk_spec, v_spec],
                  [jax.ShapeDtypeStruct((t, HEADS * QK_PAD), F32), jax.ShapeDtypeStruct((t, HEADS * QK_PAD), F32),
                   jax.ShapeDtypeStruct((t, HEADS * HEAD), F32)], [q, k, v, do], ("arbitrary",), host=host)


def _mix_out_proj(o_mla, o_dn, z, w_mla, w_dn, w_out, h0, w_ffn):
    def body(om_ref, od_ref, z_ref, h0_ref, wm_ref, wd_ref, wo_ref, wf_ref, mixed_ref, h1_ref, n2_ref):
        for h in range(HEADS):
            sl = slice(h * HEAD, (h + 1) * HEAD)
            mixed_ref[:, sl] = _rms(om_ref[:, sl], wm_ref[...], HEAD).astype(BF16)
            mixed_ref[:, DN_WIDTH + h * HEAD:DN_WIDTH + (h + 1) * HEAD] = _dn_out_fn(od_ref[:, sl], z_ref[:, sl],
                                                                                     wd_ref[...]).astype(BF16)
        h1 = _mm(mixed_ref[...], wo_ref[...]) + h0_ref[...]
        h1_ref[...] = h1
        n2_ref[...] = _rms(h1, wf_ref[...], D_MODEL).astype(BF16)

    return _rows_call("mix_out_proj", body, [o_mla, o_dn, z, h0], [w_mla, w_dn, w_out, w_ffn],
                      [(D_MODEL, BF16), (D_MODEL, F32), (D_MODEL, BF16)], [], _row_tile(o_mla.shape[0], 4))


def _down_proj_loss(act, w_down, h1, tgt, n_valid):
    t, n = h1.shape
    r = _row_tile(t, 4)

    def body(a_ref, h_ref, t_ref, w_ref, dy_ref, dy16_ref, acc_ref):
        h2 = _mm(a_ref[...], w_ref[...]) + h_ref[...]
        rows = pl.program_id(0) * r + lax.broadcasted_iota(jnp.int32, (r, n), 0)
        valid = jnp.logical_and(rows >= N_META, rows < n_valid)
        e = jnp.where(valid, h2 - t_ref[...], 0.0)
        dy = e * (1.0 / n)
        dy_ref[...] = dy
        dy16_ref[...] = dy.astype(BF16)
        _accumulate(acc_ref, jnp.sum(e * e, axis=0, keepdims=True))

    return _rows_call("down_proj_loss", body, [act, h1, tgt], [w_down], [(n, F32), (n, BF16)], [(1, n)], r)


def _in_proj_bwd_x(pieces, win, host=None):
    offs = np.cumsum([0] + [p.shape[1] for p in pieces])

    def body(*refs):
        p_refs, w_ref, o_ref = refs[:len(pieces)], refs[len(pieces)], refs[-1]
        acc = None
        for i, p_ref in enumerate(p_refs):
            part = _mm(p_ref[...], w_ref[int(offs[i]):int(offs[i + 1]), :])
            acc = part if acc is None else acc + part
        o_ref[...] = acc.astype(BF16)

    return _rows_call("in_dx", body, pieces, [win], [(win.shape[1], BF16)], [], _row_tile(pieces[0].shape[0], 4), host=host)[0]


def _in_proj_bwd_w(pieces, u):
    offs = np.cumsum([0] + [p.shape[1] for p in pieces])

    def body(*refs):
        p_refs, u_ref, o_ref = refs[:len(pieces)], refs[len(pieces)], refs[-1]
        first = pl.program_id(0) == 0
        uv = u_ref[...]
        for i, p_ref in enumerate(p_refs):
            rows = slice(int(offs[i]), int(offs[i + 1]))
            part = _mm_tn(p_ref[...], uv)

            @pl.when(first)
            def _():
                o_ref[rows, :] = part

            @pl.when(jnp.logical_not(first))
            def _():
                o_ref[rows, :] += part

    return _rows_call("in_dw", body, [*pieces, u], [], [], [(int(offs[-1]), u.shape[1])], _row_tile(u.shape[0], 2))[0]


def _ffn_in_bwd(dgpre, dup, w_gate_t, w_up_t, h1, dy, w_ffn, host=None):
    n = h1.shape[1]

    def body(dg_ref, du_ref, h_ref, dy_ref, wg_ref, wu_ref, w_ref, dh_ref, dh16_ref, dw_ref):
        ct = _mm(dg_ref[...], wg_ref[...]) + _mm(du_ref[...], wu_ref[...])
        _, vjp = jax.vjp(lambda x, ww: _rms(x, ww, n), h_ref[...], w_ref[...])
        dh, dw = vjp(ct)
        dh = dh + dy_ref[...]
        dh_ref[...] = dh
        dh16_ref[...] = dh.astype(BF16)
        _accumulate(dw_ref, dw)

    return _rows_call("ffn_in_bwd", body, [dgpre, dup, h1, dy], [w_gate_t, w_up_t, w_ffn], [(n, F32), (n, BF16)], [(1, n)],
                      _row_tile(h1.shape[0]), host=host)


def _mix_out_bwd(o_mla, o_dn, z, dh1, w_out, w_mla, w_dn, host=None):
    def body(om_ref, od_ref, z_ref, dh_ref, wo_ref, wm_ref, wd_ref, dom_ref, dod_ref, dz_ref, dwm_ref, dwd_ref):
        dwm = dwd = None
        for h in range(HEADS):
            sl = slice(h * HEAD, (h + 1) * HEAD)
            _, vjp = jax.vjp(lambda o, w: _rms(o, w, HEAD), om_ref[:, sl], wm_ref[...])
            do, dw = vjp(_mm_nt(dh_ref[...], wo_ref[sl, :]))
            dom_ref[:, sl] = do
            dwm = dw if dwm is None else dwm + dw
            _, vjp = jax.vjp(_dn_out_fn, od_ref[:, sl], z_ref[:, sl], wd_ref[...])
            do, dz, dw = vjp(_mm_nt(dh_ref[...], wo_ref[DN_WIDTH + h * HEAD:DN_WIDTH + (h + 1) * HEAD, :]))
            dod_ref[:, sl] = do
            dz_ref[:, sl] = dz.astype(BF16)
            dwd = dw if dwd is None else dwd + dw
        _accumulate(dwm_ref, dwm)
        _accumulate(dwd_ref, dwd)

    return _rows_call("mix_out_bwd", body, [o_mla, o_dn, z, dh1], [w_out, w_mla, w_dn],
                      [(DN_WIDTH, F32), (DN_WIDTH, F32), (DN_WIDTH, BF16)], [(1, HEAD), (1, HEAD)],
                      _row_tile(o_mla.shape[0], 4), host=host)


def _shift_down(x, s):
    if s == 0:
        return x
    rows = lax.broadcasted_iota(jnp.int32, x.shape, 0)
    return jnp.where(rows >= s, pltpu.roll(x, s, 0), 0.0)


def _shift_up(x, s):
    if s == 0:
        return x
    t = x.shape[0]
    rows = lax.broadcasted_iota(jnp.int32, x.shape, 0)
    return jnp.where(rows < t - s, pltpu.roll(x, t - s, 0), 0.0)


def _col_call(name, body, cols, taps, outs, tap_outs, cw, host=None):
    t, c = cols[0].shape[0], taps[0].shape[1]
    in_specs = [pl.BlockSpec((t, cw), lambda j: (0, j)) for _ in cols]
    in_specs += [pl.BlockSpec((a.shape[0], cw), lambda j: (0, j)) for a in taps]
    out_shape = [jax.ShapeDtypeStruct((t, c), dt) for dt in outs] + [jax.ShapeDtypeStruct((n, c), F32) for n in tap_outs]
    out_specs = [pl.BlockSpec((t, cw), lambda j: (0, j)) for _ in outs]
    out_specs += [pl.BlockSpec((n, cw), lambda j: (0, j)) for n in tap_outs]
    return _pcall(body, name, (c // cw,), in_specs, out_specs, out_shape, [*cols, *taps], ("arbitrary",), host=host)


def _causal_conv(x, w_ref, width, zero_tail=False):
    down = (lambda a, s: pltpu.roll(a, s, 0)) if zero_tail else _shift_down
    acc = w_ref[width - 1:width, :] * x
    for j in range(width - 1):
        acc = acc + w_ref[j:j + 1, :] * down(x, width - 1 - j)
    return acc


def _causal_conv_bwd(x, dpre, w_ref, dx_ref, dw_ref, width, zero_tail=False):
    t = x.shape[0]
    down = (lambda a, s: pltpu.roll(a, s, 0)) if zero_tail else _shift_down
    up = (lambda a, s: pltpu.roll(a, t - s, 0)) if zero_tail else _shift_up
    dx = w_ref[width - 1:width, :] * dpre
    dw_ref[width - 1:width, :] = jnp.sum(dpre * x, axis=0, keepdims=True)
    for j in range(width - 1):
        s = width - 1 - j
        dx = dx + w_ref[j:j + 1, :] * up(dpre, s)
        dw_ref[j:j + 1, :] = jnp.sum(dpre * down(x, s), axis=0, keepdims=True)
    dx_ref[...] = dx.astype(dx_ref.dtype)


def _dsilu(x):
    sg = jax.nn.sigmoid(x)
    return sg * (1.0 + x * (1.0 - sg))


def _dn_conv_fwd(x, w):
    def body(x_ref, w_ref, y_ref):
        y_ref[...] = _silu(_causal_conv(x_ref[...], w_ref, 4, zero_tail=True))

    return _col_call("dn_conv_fwd", body, [x], [w], [F32], [], 256)[0]


def _dn_conv_bwd(x, w, dy):
    def body(x_ref, dy_ref, w_ref, dx_ref, dw_ref):
        xv = x_ref[...]
        dpre = dy_ref[...] * _dsilu(_causal_conv(xv, w_ref, 4, zero_tail=True))
        _causal_conv_bwd(xv, dpre, w_ref, dx_ref, dw_ref, 4, zero_tail=True)

    return _col_call("dn_conv_bwd", body, [x, dy], [w], [BF16], [4], 256)


def _ffn_glu_fwd(n2, w_gate_t, w_up_t, w, b, host=None):
    t, k = n2.shape
    c, cw = w_gate_t.shape[0], 256

    def body(n_ref, wg_ref, wu_ref, w_ref, b_ref, g_ref, u_ref, a_ref):
        nv = n_ref[...]
        g16 = _mm_nt(nv, wg_ref[...]).astype(BF16)
        u16 = _mm_nt(nv, wu_ref[...]).astype(BF16)
        g_ref[...] = g16
        u_ref[...] = u16
        gate = _causal_conv(g16.astype(F32), w_ref, 3) + b_ref[...]
        a_ref[...] = (_silu(gate) * u16.astype(F32)).astype(BF16)

    wspec = pl.BlockSpec((cw, k), lambda j: (j, 0))
    col = pl.BlockSpec((t, cw), lambda j: (0, j))
    in_specs = [pl.BlockSpec((t, k), lambda j: (0, 0)), wspec, wspec, pl.BlockSpec((w.shape[0], cw), lambda j: (0, j)),
                pl.BlockSpec((1, cw), lambda j: (0, j))]
    return _pcall(body, "ffn_glu_fwd", (c // cw,), in_specs, [col] * 3, [jax.ShapeDtypeStruct((t, c), BF16)] * 3,
                  [n2, w_gate_t, w_up_t, w, b], ("arbitrary",), host=host)


def _ffn_glu_bwd(gpre, up, dy16, w_down, w, b):
    t, k = dy16.shape
    c, cw = w_down.shape[0], 256

    def body(g_ref, u_ref, dy_ref, wd_ref, w_ref, b_ref, dg_ref, du_ref, dw_ref, db_ref):
        gv = g_ref[...].astype(F32)
        gate = _causal_conv(gv, w_ref, 3) + b_ref[...]
        da = _mm_nt(dy_ref[...], wd_ref[...])
        sg = jax.nn.sigmoid(gate)
        du_ref[...] = (da * (gate * sg)).astype(BF16)
        dgate = da * u_ref[...].astype(F32) * (sg * (1.0 + gate * (1.0 - sg)))
        db_ref[...] = jnp.sum(dgate, axis=0, keepdims=True)
        _causal_conv_bwd(gv, dgate, w_ref, dg_ref, dw_ref, 3)

    col = pl.BlockSpec((t, cw), lambda j: (0, j))
    taps = lambda rows: pl.BlockSpec((rows, cw), lambda j: (0, j))
    in_specs = [col, col, pl.BlockSpec((t, k), lambda j: (0, 0)), pl.BlockSpec((cw, k), lambda j: (j, 0)),
                taps(w.shape[0]), taps(1)]
    return _pcall(body, "ffn_glu_bwd", (c // cw,), in_specs, [col, col, taps(w.shape[0]), taps(1)],
                  [jax.ShapeDtypeStruct((t, c), BF16)] * 2 + [jax.ShapeDtypeStruct((w.shape[0], c), F32),
                                                             jax.ShapeDtypeStruct((1, c), F32)],
                  [gpre, up, dy16, w_down, w, b], ("arbitrary",))


def _dn_prep_consts(sa, sb, al, dt):
    return dict(sel_a=sa[...], sel_b=sb[...], alog=al[...], dtb=dt[...])


def _dn_prep_fwd(conv, ab, sel_a, sel_b, alog, dtb):
    def body(c_ref, ab_ref, sa, sb, al, dt, q_out, k_out, g_out, b_out):
        qc = tuple(c_ref[:, h * HEAD:(h + 1) * HEAD] for h in range(HEADS))
        kc = tuple(c_ref[:, DN_WIDTH + h * HEAD:DN_WIDTH + (h + 1) * HEAD] for h in range(HEADS))
        qs, ks, g, beta = _dn_prep_fn((qc, kc, ab_ref[...]), _dn_prep_consts(sa, sb, al, dt))
        for h in range(HEADS):
            q_out[:, h * HEAD:(h + 1) * HEAD] = qs[h]
            k_out[:, h * HEAD:(h + 1) * HEAD] = ks[h]
        g_out[...] = g
        b_out[...] = beta

    return _rows_call("dn_prep_fwd", body, [conv, ab], [sel_a, sel_b, alog, dtb], [(DN_WIDTH, F32)] * 4, [],
                      _row_tile(conv.shape[0], 4))


def _dn_prep_bwd(conv, ab, dq, dk, dv, dg, db, sel_a, sel_b, alog, dtb):
    def body(c_ref, ab_ref, dq_r, dk_r, dv_r, dg_r, db_r, sa, sb, al, dt, dc_out, dab_out, dal_out, ddt_out):
        qc = tuple(c_ref[:, h * HEAD:(h + 1) * HEAD] for h in range(HEADS))
        kc = tuple(c_ref[:, DN_WIDTH + h * HEAD:DN_WIDTH + (h + 1) * HEAD] for h in range(HEADS))
        consts = _dn_prep_consts(sa, sb, al, dt)
        sel = dict(sel_a=consts["sel_a"], sel_b=consts["sel_b"])
        _, vjp = jax.vjp(lambda rows, ad: _dn_prep_fn(rows, {**sel, **ad}), (qc, kc, ab_ref[...]),
                         dict(alog=consts["alog"], dtb=consts["dtb"]))
        cq = tuple(dq_r[:, h * HEAD:(h + 1) * HEAD] for h in range(HEADS))
        ck = tuple(dk_r[:, h * HEAD:(h + 1) * HEAD] for h in range(HEADS))
        (dqc, dkc, dab), dad = vjp((cq, ck, dg_r[...], db_r[...]))
        for h in range(HEADS):
            dc_out[:, h * HEAD:(h + 1) * HEAD] = dqc[h]
            dc_out[:, DN_WIDTH + h * HEAD:DN_WIDTH + (h + 1) * HEAD] = dkc[h]
        dc_out[:, 2 * DN_WIDTH:3 * DN_WIDTH] = dv_r[...]
        dab_out[...] = dab.astype(BF16)
        _accumulate(dal_out, dad["alog"])
        _accumulate(ddt_out, dad["dtb"])

    return _rows_call("dn_prep_bwd", body, [conv, ab, dq, dk, dv, dg, db], [sel_a, sel_b, alog, dtb],
                      [(3 * DN_WIDTH, F32), (HEAD, BF16)], [(1, DN_WIDTH), (1, DN_WIDTH)], _row_tile(conv.shape[0], 4))


def _chunk_batch(t):
    nc = t // CHUNK
    return nc // 2 if nc % 2 == 0 else nc


def _dn_chunk_specs(t, nb):
    rows = nb * CHUNK
    blk = pl.BlockSpec((rows, HEAD), lambda h, b: (b, h))
    vblk = pl.BlockSpec((rows, HEAD), lambda h, b: (b, 2 * HEADS + h))
    mat = pl.BlockSpec((nb, HEAD, HEAD), lambda h, b: (b, h, 0))
    return rows, blk, vblk, mat


def _dn_chunk_fwd(qn, kn, conv, g, beta, host=None):
    t = qn.shape[0]
    nb = _chunk_batch(t)
    rows, blk, vblk, mat = _dn_chunk_specs(t, nb)

    def body(q_ref, k_ref, v_ref, g_ref, b_ref, n_o, b_o, qe_o, oo_o, eg_o):
        r3 = lambda x: x.reshape(nb, CHUNK, x.shape[-1])
        n_mat, b_mat, q_eff, o_own, eg = _dn_chunk_fn(r3(q_ref[...]), r3(k_ref[...]), r3(v_ref[...]), r3(g_ref[...]),
                                                      r3(g_ref[:, 0:CHUNK]), r3(b_ref[...]))
        n_o[...] = n_mat
        b_o[...] = b_mat
        qe_o[...] = q_eff.reshape(rows, HEAD)
        oo_o[...] = o_own.reshape(rows, HEAD)
        eg_o[...] = jnp.broadcast_to(eg, (nb, HEAD, HEAD))

    nc = t // CHUNK
    mats = jax.ShapeDtypeStruct((nc, DN_WIDTH, HEAD), F32)
    rowsd = jax.ShapeDtypeStruct((t, DN_WIDTH), F32)
    return _pcall(body, "dn_chunk_fwd", (HEADS, t // rows), [blk, blk, vblk, blk, blk], [mat, mat, blk, blk, mat],
                  [mats, mats, rowsd, rowsd, mats], [qn, kn, conv, g, beta], ("arbitrary", "arbitrary"), host=host)


def _dn_chunk_bwd(qn, kn, conv, g, beta, sall, gall, dq_eff, do, host=None):
    t = qn.shape[0]
    nb = _chunk_batch(t)
    rows, blk, vblk, mat = _dn_chunk_specs(t, nb)

    def body(q_ref, k_ref, v_ref, g_ref, b_ref, s_ref, ga_ref, dqe_ref, do_ref, dq_o, dk_o, dv_o, dg_o, db_o):
        r3 = lambda x: x.reshape(nb, CHUNK, x.shape[-1])
        _, vjp = jax.vjp(_dn_chunk_fn, r3(q_ref[...]), r3(k_ref[...]), r3(v_ref[...]), r3(g_ref[...]),
                         r3(g_ref[:, 0:CHUNK]), r3(b_ref[...]))
        s, ga = s_ref[...], ga_ref[...]
        d_n = -_bmm_nt(ga, s)
        d_eg = jnp.sum(ga * s, axis=1, keepdims=True)
        dq, dk, dv, dg, dg64, db = vjp((d_n, ga, r3(dqe_ref[...]), r3(do_ref[...]), d_eg))
        for o_ref, val in zip((dq_o, dk_o, dv_o, dg_o, db_o), (dq, dk, dv, dg, db)):
            o_ref[...] = val.reshape(rows, HEAD)
        dg_o[:, 0:CHUNK] += dg64.reshape(rows, CHUNK)

    return _pcall(body, "dn_chunk_bwd", (HEADS, t // rows), [blk, blk, vblk, blk, blk, mat, mat, blk, blk], [blk] * 5,
                  [jax.ShapeDtypeStruct((t, DN_WIDTH), F32)] * 5, [qn, kn, conv, g, beta, sall, gall, dq_eff, do],
                  ("arbitrary", "arbitrary"), host=host)


def _dn_rec_fwd(n_mat, b_mat, eg, host=None):
    nc = n_mat.shape[0]
    nb = _chunk_batch(nc * CHUNK)
    spec = pl.BlockSpec((nb, DN_WIDTH, HEAD), lambda i: (i, 0, 0))

    def body(n_ref, b_ref, eg_ref, sall_ref, s_scr):
        @pl.when(pl.program_id(0) == 0)
        def _():
            s_scr[...] = jnp.zeros(s_scr.shape, F32)

        for j in range(nb):
            sall_ref[j] = s_scr[...]
            for h in range(HEADS):
                sl = slice(h * HEAD, (h + 1) * HEAD)
                s_scr[sl, :] = _dn_rec_fn(s_scr[sl, :], n_ref[j, sl, :], b_ref[j, sl, :],
                                          eg_ref[j, h * HEAD:h * HEAD + 1, :])

    return _pcall(body, "dn_rec_fwd", (nc // nb,), [spec] * 3, spec, jax.ShapeDtypeStruct((nc, DN_WIDTH, HEAD), F32),
                  [n_mat, b_mat, eg], ("arbitrary",), scratch_shapes=[pltpu.VMEM((DN_WIDTH, HEAD), F32)], host=host)


def _dn_rec_bwd(n_mat, eg, ds_out, host=None):
    nc = n_mat.shape[0]
    nb = _chunk_batch(nc * CHUNK)
    steps = nc // nb
    spec = pl.BlockSpec((nb, DN_WIDTH, HEAD), lambda i: (steps - 1 - i, 0, 0))

    def body(n_ref, eg_ref, dso_ref, gall_ref, g_scr):
        @pl.when(pl.program_id(0) == 0)
        def _():
            g_scr[...] = jnp.zeros(g_scr.shape, F32)

        for j in reversed(range(nb)):
            gall_ref[j] = g_scr[...]
            for h in range(HEADS):
                sl = slice(h * HEAD, (h + 1) * HEAD)
                gv = g_scr[sl, :]
                g_scr[sl, :] = (gv * eg_ref[j, h * HEAD:h * HEAD + 1, :] - _mm_tn(n_ref[j, sl, :], gv)
                                + dso_ref[j, sl, :])

    return _pcall(body, "dn_rec_bwd", (steps,), [spec] * 3, spec, jax.ShapeDtypeStruct((nc, DN_WIDTH, HEAD), F32),
                  [n_mat, eg, ds_out], ("arbitrary",), scratch_shapes=[pltpu.VMEM((DN_WIDTH, HEAD), F32)], host=host)


def _dn_o_fwd(sall, q_eff, o_own):
    t = q_eff.shape[0]
    nb = _chunk_batch(t)
    rows, blk, _, mat = _dn_chunk_specs(t, nb)

    def body(s_ref, qe_ref, oo_ref, o_ref):
        r3 = lambda x: x.reshape(nb, CHUNK, HEAD)
        o_ref[...] = _dn_o_fn(s_ref[...], r3(qe_ref[...]), r3(oo_ref[...])).reshape(rows, HEAD)

    return _pcall(body, "dn_o_fwd", (HEADS, t // rows), [mat, blk, blk], blk, jax.ShapeDtypeStruct((t, DN_WIDTH), F32),
                  [sall, q_eff, o_own], ("arbitrary", "arbitrary"))


def _dn_o_bwd(sall, q_eff, do, host=None):
    t = q_eff.shape[0]
    nb = _chunk_batch(t)
    rows, blk, _, mat = _dn_chunk_specs(t, nb)

    def body(s_ref, qe_ref, do_ref, dqe_ref, ds_ref):
        r3 = lambda x: x.reshape(nb, CHUNK, HEAD)
        dov = r3(do_ref[...])
        dqe_ref[...] = _bmm_nt(dov, s_ref[...]).reshape(rows, HEAD)
        ds_ref[...] = _bmm_tn(r3(qe_ref[...]), dov)

    nc = t // CHUNK
    return _pcall(body, "dn_o_bwd", (HEADS, t // rows), [mat, blk, blk], [blk, mat],
                  [jax.ShapeDtypeStruct((t, DN_WIDTH), F32), jax.ShapeDtypeStruct((nc, DN_WIDTH, HEAD), F32)],
                  [sall, q_eff, do], ("arbitrary", "arbitrary"), host=host)


def _adamw_update(w, g, m, v):
    m2 = ADAM_B1 * m + (1.0 - ADAM_B1) * g
    v2 = ADAM_B2 * v + (1.0 - ADAM_B2) * (g * g)
    m_hat = m2 / (1.0 - ADAM_B1 ** ADAM_STEP)
    v_hat = v2 / (1.0 - ADAM_B2 ** ADAM_STEP)
    return -ADAM_LR * (m_hat / (jnp.sqrt(v_hat) + ADAM_EPS) + ADAM_WD * w), m2, v2


def _adamw_small(ws, gs, ms, vs):
    n = len(ws)

    def body(*refs):
        for i in range(n):
            d, m2, v2 = _adamw_update(refs[i][...], refs[n + i][...], refs[2 * n + i][...], refs[3 * n + i][...])
            refs[4 * n + i][...] = d
            refs[5 * n + i][...] = m2
            refs[6 * n + i][...] = v2

    shapes = [jax.ShapeDtypeStruct(a.shape, F32) for a in ws]
    outs = pl.pallas_call(body, name="adamw_small", out_shape=shapes * 3,
                          compiler_params=pltpu.CompilerParams(vmem_limit_bytes=VMEM_LIMIT))(*ws, *gs, *ms, *vs)
    return outs[:n], outs[n:2 * n], outs[2 * n:]


def _adamw_call(name, w, g, m, v, host=None):
    rows, cols = w.shape
    by_rows = rows % 8 == 0

    def body(w_ref, g_ref, m_ref, v_ref, g_out, d_ref, m_out, v_out):
        gv = g_ref[...] if by_rows else g_ref[0:rows, :]
        g_out[...] = gv
        d_ref[...], m_out[...], v_out[...] = _adamw_update(w_ref[...], gv, m_ref[...], v_ref[...])

    if by_rows:
        tr = _tile(rows, 256, 8)
        spec = g_spec = pl.BlockSpec((tr, cols), lambda i: (i, 0))
        grid = (rows // tr,)
    else:
        tc = _tile(cols, 256, 128)
        spec = pl.BlockSpec((rows, tc), lambda j: (0, j))
        g_spec = pl.BlockSpec((g.shape[0], tc), lambda j: (0, j))
        grid = (cols // tc,)
    return _pcall(body, name, grid, [spec, g_spec, spec, spec], [spec] * 4, [jax.ShapeDtypeStruct((rows, cols), F32)] * 4,
                  [w, g, m, v], ("arbitrary",), host=host)


def _adamw_rows3d(name, w, g, m, v):
    rows, _, cols = w.shape
    tr = max(d for d in range(1, 129) if rows % d == 0)

    def body(w_ref, g_ref, m_ref, v_ref, d_ref, m_out, v_out):
        d_ref[...], m_out[...], v_out[...] = _adamw_update(w_ref[...], g_ref[...], m_ref[...], v_ref[...])

    spec = pl.BlockSpec((tr, 1, cols), lambda i: (i, 0, 0))
    return pl.pallas_call(body, name=name, grid=(rows // tr,), in_specs=[spec] * 4, out_specs=[spec] * 3,
                          out_shape=[jax.ShapeDtypeStruct(w.shape, F32)] * 3, compiler_params=_cparams(("arbitrary",)))(
                              w, g, m, v)


def _rope_tables(t):
    half = ROPE // 2
    inv_freq = np.float32(ROPE_THETA) ** (-np.arange(half, dtype=np.float32) / np.float32(half))
    ang = np.arange(t, dtype=np.float32)[:, None] * inv_freq[None, :].astype(np.float32)
    z = np.zeros((t, HEAD - ROPE), np.float32)
    cos = np.concatenate([np.cos(ang), np.cos(ang), z], axis=1).astype(np.float32)
    sin = np.concatenate([np.sin(ang), np.sin(ang), z], axis=1).astype(np.float32)
    k = np.arange(HEAD)[:, None]
    l = np.arange(HEAD)[None, :]
    perm = np.where((l < half) & (k == l + half), -1.0, 0.0) + np.where((l >= half) & (l < ROPE) & (k == l - half), 1.0, 0.0)
    return jnp.asarray(cos), jnp.asarray(sin), jnp.asarray(perm.astype(np.float32))


def _win_to_pad(w):
    z = lambda n: jnp.zeros((n, w.shape[1]), w.dtype)
    return jnp.concatenate([w[576:2112], w[2112:2624], w[0:256], w[256:512], w[512:576], z(64), w[2624:2632], z(120)],
                           axis=0)


def _win_from_pad(g):
    return jnp.concatenate([g[2048:2304], g[2304:2560], g[2560:2624], g[0:1536], g[1536:2048], g[2688:2696]], axis=0)


def _qk_to_pad(w):
    w4 = w.reshape(HEADS, QK_DIM, w.shape[-1])
    return jnp.concatenate([w4, jnp.zeros((HEADS, QK_PAD - QK_DIM, w.shape[-1]), w.dtype)], axis=1).reshape(
        HEADS * QK_PAD, w.shape[-1])


def _qk_from_pad(g):
    return g.reshape(HEADS, QK_PAD, g.shape[-1])[:, :QK_DIM].reshape(HEADS * QK_DIM, g.shape[-1])


def _ff_to_pad(a, axis):
    shape = list(a.shape)
    shape[axis:axis + 1] = [N_CHIPS, FF_SHARD]
    a4 = a.reshape(shape)
    shape[axis + 1] = FF_BLOCK - FF_SHARD
    out = jnp.concatenate([a4, jnp.zeros(shape, a.dtype)], axis=axis + 1)
    shape[axis:axis + 2] = [D_FF_P]
    return out.reshape(shape)


def _ff_from_pad(a, axis):
    shape = list(a.shape)
    shape[axis:axis + 1] = [N_CHIPS, FF_BLOCK]
    a4 = lax.slice_in_dim(a.reshape(shape), 0, FF_SHARD, axis=axis + 1)
    shape[axis:axis + 2] = [D_FF]
    return a4.reshape(shape)


class _LocalPlan:
    def __init__(self, wt):
        self.wt, self.grads = wt, {}

    def weight(self, name):
        return self.wt[name]

    def host(self, point):
        return None

    def grad(self, name, value):
        self.grads[name] = value


def _local_step(x, tgt, wt, plan=None):
    plan = _LocalPlan(wt) if plan is None else plan
    s = x.shape[0]
    n_valid = N_META + s
    t = -(-n_valid // HEAD) * HEAD
    assert t - n_valid >= 3, "the DeltaNet conv kernels rely on at least three zero rows after the sequence"
    zpad = jnp.zeros((t - n_valid, D_MODEL), F32)
    h0 = jnp.concatenate([wt["meta_tokens"], x, zpad], axis=0)
    tgt_p = jnp.concatenate([jnp.zeros((N_META, D_MODEL), F32), tgt, zpad], axis=0)
    cos, sin, perm = _rope_tables(t)
    qn_w = jnp.concatenate([wt["q_norm_w"], jnp.zeros((1, QK_PAD - QK_DIM), F32)], axis=1)
    kn_w = jnp.concatenate([wt["k_norm_w"], jnp.zeros((1, QK_PAD - QK_DIM), F32)], axis=1)
    head_id = jnp.arange(DN_WIDTH)[None, :] // HEAD
    lane = jnp.arange(HEAD)[:, None]
    sel_a = (lane == head_id).astype(F32)
    sel_b = (lane == head_id + HEADS).astype(F32)
    alog = jnp.repeat(wt["dn_A_log"], HEAD, axis=1)
    dtb = jnp.repeat(wt["dn_dt_bias"], HEAD, axis=1)
    conv_w, conv_b = wt["ffn_conv_w"], wt["ffn_conv_b"]

    u = _rms_fwd("attn_norm_fwd", h0, wt["attn_norm_w"], host=plan.host("attn_norm_fwd"))
    win, wq, wkv = plan.weight("w_in_t"), plan.weight("w_q_t"), plan.weight("w_kv_t")
    proj = _matmul("in_proj", u, win, "nt", F32)
    z = (proj, DN_WIDTH, 3)
    q_lat, kv_lat, k_pe, ab = (proj, LORA, 8), (proj, LORA, 9), (proj, HEAD, 20), (proj, HEAD, 21)
    mla_consts = (wt["q_a_norm_w"], wq, wt["kv_a_norm_w"], wkv, qn_w, kn_w, perm)
    q, k, v = _mla_prep_fwd(q_lat, kv_lat, k_pe, cos, sin, *mla_consts)
    o_mla = _attn_fwd(q, k, v, host=plan.host("attn_fwd"))
    conv = _dn_conv_fwd(proj, wt["dn_conv_w"])
    dn_consts = (sel_a, sel_b, alog, dtb)
    qn, kn, g, beta = _dn_prep_fwd(conv, ab, *dn_consts)
    n_mat, b_mat, q_eff, o_own, eg = _dn_chunk_fwd(qn, kn, conv, g, beta, host=plan.host("dn_chunk_fwd"))
    sall = _dn_rec_fwd(n_mat, b_mat, eg)
    o_dn = _dn_o_fwd(sall, q_eff, o_own)
    w_out = plan.weight("w_out")
    mixed, h1, n2 = _mix_out_proj(o_mla, o_dn, z, wt["mla_out_norm_w"], wt["dn_out_norm_w"], w_out, h0, wt["ffn_norm_w"])
    w_gate, w_up = plan.weight("w_gate_t"), plan.weight("w_up_t")
    gpre, up, act = _ffn_glu_fwd(n2, w_gate, w_up, conv_w, conv_b, host=plan.host("ffn_glu_fwd"))
    w_down = plan.weight("w_down")
    dy, dy16, sq = _down_proj_loss(act, w_down, h1, tgt_p, n_valid)

    grads = {}
    plan.grad("w_down", _matmul("down_dw", act, dy16, "tn", BF16))
    dgpre, dup, grads["ffn_conv_w"], grads["ffn_conv_b"] = _ffn_glu_bwd(gpre, up, dy16, w_down, conv_w, conv_b)
    plan.grad("w_gate_t", _matmul("gate_dw", dgpre, n2, "tn", BF16))
    plan.grad("w_up_t", _matmul("up_dw", dup, n2, "tn", BF16))
    dh1, dh1_16, grads["ffn_norm_w"] = _ffn_in_bwd(dgpre, dup, w_gate, w_up, h1, dy, wt["ffn_norm_w"],
                                                   host=plan.host("ffn_in_bwd"))
    plan.grad("w_out", _matmul("out_dw", mixed, dh1_16, "tn", BF16))
    do_mla, do_dn, dz, grads["mla_out_norm_w"], grads["dn_out_norm_w"] = _mix_out_bwd(
        o_mla, o_dn, z, dh1_16, w_out, wt["mla_out_norm_w"], wt["dn_out_norm_w"], host=plan.host("mix_out_bwd"))
    dq_eff, ds_out = _dn_o_bwd(sall, q_eff, do_dn)
    gall = _dn_rec_bwd(n_mat, eg, ds_out)
    dqn, dkn, dv_dn, dg, dbeta = _dn_chunk_bwd(qn, kn, conv, g, beta, sall, gall, dq_eff, do_dn,
                                               host=plan.host("dn_chunk_bwd"))
    dconv, dab, dalog, ddtb = _dn_prep_bwd(conv, ab, dqn, dkn, dv_dn, dg, dbeta, *dn_consts)
    grads["dn_A_log"] = jnp.sum(dalog.reshape(HEADS, HEAD), axis=1)[None, :]
    grads["dn_dt_bias"] = jnp.sum(ddtb.reshape(HEADS, HEAD), axis=1)[None, :]
    ddn_pre, grads["dn_conv_w"] = _dn_conv_bwd(proj, wt["dn_conv_w"], dconv)
    dq, dk, dv = _attn_bwd(q, k, v, do_mla, host=plan.host("attn_bwd"))
    dq_lat, dkv_lat, dk_pe, dqa, dwq, dkva, dwkv, dqnw, dknw = _mla_prep_bwd(
        q_lat, kv_lat, k_pe, cos, sin, dq, dk, dv, *mla_consts, host=plan.host("mla_prep_bwd"))
    grads["q_a_norm_w"], grads["kv_a_norm_w"] = dqa, dkva
    plan.grad("w_q_t", dwq)
    plan.grad("w_kv_t", dwkv)
    grads["q_norm_w"], grads["k_norm_w"] = dqnw[:, :QK_DIM], dknw[:, :QK_DIM]
    dproj = [ddn_pre, dz, dq_lat, dkv_lat, dk_pe, dab]
    plan.grad("w_in_t", _in_proj_bwd_w(dproj, u))
    du = _in_proj_bwd_x(dproj, win, host=plan.host("in_dx"))
    dh0, _, grads["attn_norm_w"] = _rms_bwd("attn_norm_bwd", h0, wt["attn_norm_w"], [du], dh1,
                                            host=plan.host("attn_norm_bwd"))
    grads["meta_tokens"] = dh0[0:N_META]
    if isinstance(plan, _LocalPlan):
        grads.update(plan.grads)
    return sq, dh0[N_META:n_valid], grads


def _mesh_pos():
    return lax.axis_index("x"), lax.axis_index("y"), lax.axis_index("c")


def _other_chips(x, y):
    return [(1 - x, y), (x, 1 - y), (1 - x, 1 - y)]


def _remote(src, dst, send_sems, recv_sems, k, to):
    return pltpu.make_async_remote_copy(src_ref=src, dst_ref=dst, send_sem=send_sems.at[k], recv_sem=recv_sems.at[k],
                                        device_id=to, device_id_type=MESH)


SIBLING_ID, CHIPS_ID, GATHER_ID, ALL_ID = 1, 2, 3, 4


def _sibling_peer():
    x, y, c = _mesh_pos()
    return [(x, y, 1 - c)]


def _chip_peers():
    x, y, c = _mesh_pos()
    return [(qx, qy, c) for qx, qy in _other_chips(x, y)]


def _copies_exchange(make, ins, out_shape, nsem, peers=None, cid=None):
    def prog(in_refs, out_refs, send_sems, recv_sems):
        copies = make(in_refs, out_refs, send_sems, recv_sems)

        def start():
            for cp in copies:
                cp.start()

        def finish():
            for cp in copies:
                cp.wait()

        return start, finish

    return _Exchange(prog, ins, out_shape, nsem, peers, cid)


def _all_gather(shards):
    def prog(srcs, dsts, send_sems, recv_sems):
        x, y, c = _mesh_pos()
        p = 2 * x + y
        sibling = (x, y, 1 - c)
        chips = _other_chips(x, y)
        bufs = tuple((s, d, s.shape[0] // 2) for s, d in zip(srcs, dsts))

        def half(ref, rows, which):
            return ref.at[pl.ds(which * rows, rows), :]

        def copy(i, k, src, dst, to):
            return _remote(src, dst, send_sems, recv_sems, 6 * i + k, to)

        sends = [copy(i, j, half(src, rows, c), half(dst.at[p], rows, c), (*chip, c))
                 for i, (src, dst, rows) in enumerate(bufs) for j, chip in enumerate(chips)]

        def start():
            for cp in sends:
                cp.start()

        def finish():
            passed = []
            for i, (src, dst, rows) in enumerate(bufs):
                for j, (qx, qy) in enumerate(chips):
                    block = half(dst.at[2 * qx + qy], rows, c)
                    copy(i, j, block, block, (x, y, c)).wait_recv()
                    fwd = copy(i, 3 + j, block, block, sibling)
                    fwd.start()
                    passed.append(fwd)
            for i, (src, dst, rows) in enumerate(bufs):
                for j, (qx, qy) in enumerate(chips):
                    block = half(dst.at[2 * qx + qy], rows, 1 - c)
                    copy(i, 3 + j, block, block, (x, y, c)).wait_recv()
            for cp in sends + passed:
                cp.wait_send()

        return start, finish

    return _Exchange(prog, shards, [jax.ShapeDtypeStruct((N_CHIPS, *s.shape), s.dtype) for s in shards], 6 * len(shards),
                     lambda: _sibling_peer() + _chip_peers(), GATHER_ID)


def _all_gather_small(block):
    def make(srcs, dsts, send_sems, recv_sems):
        x, y, c = _mesh_pos()
        return [_remote(srcs[0], dsts[0].at[2 * x + y], send_sems, recv_sems, k, (qx, qy, c))
                for k, (qx, qy) in enumerate(_other_chips(x, y))]

    return _copies_exchange(make, [block], [jax.ShapeDtypeStruct((N_CHIPS, *block.shape), block.dtype)], 3, _chip_peers,
                            CHIPS_ID)


def _gathered(ex):
    p = 2 * lax.axis_index("x") + lax.axis_index("y")
    return [lax.dynamic_update_slice(g, s[None], (p, 0, 0)) for g, s in zip(ex.outs, ex.ins)]


def _rs_to_sibling(bufs):
    def make(srcs, dsts, send_sems, recv_sems):
        x, y, c = _mesh_pos()
        copies = []
        for i, (src, dst) in enumerate(zip(srcs, dsts)):
            half = src.shape[1] // 2
            copies.append(_remote(src.at[:, pl.ds((1 - c) * half, half), :], dst, send_sems, recv_sems, i, (x, y, 1 - c)))
        return copies

    return _copies_exchange(make, bufs,
                            [jax.ShapeDtypeStruct((N_CHIPS, b.shape[1] // 2, b.shape[2]), b.dtype) for b in bufs],
                            len(bufs), _sibling_peer, SIBLING_ID)


def _rs_pair_add(name, bufs, gots, c, out_dtype):
    n = len(bufs)

    def body(c_ref, *refs):
        for a_ref, b_ref, o_ref in zip(refs[:n], refs[n:2 * n], refs[2 * n:]):
            o_ref[...] = (a_ref[...].astype(F32) + b_ref[...].astype(F32)).astype(out_dtype)

    mine = [pl.BlockSpec((None, g.shape[1], g.shape[2]), lambda j, cr: (j, cr[0], 0)) for g in gots]
    whole = [pl.BlockSpec((None, g.shape[1], g.shape[2]), lambda j, cr: (j, 0, 0)) for g in gots]
    return pl.pallas_call(
        body, name=name,
        grid_spec=pltpu.PrefetchScalarGridSpec(num_scalar_prefetch=1, grid=(N_CHIPS,), in_specs=mine + whole, out_specs=whole),
        out_shape=[jax.ShapeDtypeStruct(g.shape, out_dtype) for g in gots],
        compiler_params=_cparams(("arbitrary",)))(c, *bufs, *gots)


def _rs_to_chips(accs):
    def make(srcs, dsts, send_sems, recv_sems):
        x, y, c = _mesh_pos()
        return [_remote(src.at[2 * qx + qy], dst.at[k], send_sems, recv_sems, 3 * i + k, (qx, qy, c))
                for i, (src, dst) in enumerate(zip(srcs, dsts)) for k, (qx, qy) in enumerate(_other_chips(x, y))]

    return _copies_exchange(make, accs, [jax.ShapeDtypeStruct((3, a.shape[1], a.shape[2]), a.dtype) for a in accs],
                            3 * len(accs), _chip_peers, CHIPS_ID)


def _rs_chip_add(name, accs, gots, p):
    n = len(accs)
    slot = (0, 1, 0, 2)

    def body(p_ref, *refs):
        me = p_ref[0]
        for own_ref, got_ref, o_ref in zip(refs[:n], refs[n:2 * n], refs[2 * n:]):
            total = None
            for chip in range(N_CHIPS):
                val = own_ref[...].astype(F32)
                for e in (1, 2, 3):
                    val = jnp.where((chip ^ me) == e, got_ref[slot[e]].astype(F32), val)
                total = val if total is None else total + val
            o_ref[...] = total

    own = [pl.BlockSpec((None, a.shape[1], a.shape[2]), lambda i, pr: (pr[0], 0, 0)) for a in accs]
    got = [pl.BlockSpec(g.shape, lambda i, pr: (0, 0, 0)) for g in gots]
    out = [pl.BlockSpec((a.shape[1], a.shape[2]), lambda i, pr: (0, 0)) for a in accs]
    return pl.pallas_call(
        body, name=name,
        grid_spec=pltpu.PrefetchScalarGridSpec(num_scalar_prefetch=1, grid=(1,), in_specs=own + got, out_specs=out),
        out_shape=[jax.ShapeDtypeStruct((a.shape[1], a.shape[2]), F32) for a in accs],
        compiler_params=_cparams(("arbitrary",)))(p, *accs, *gots)


def _rs_share(ress):
    def make(srcs, dsts, send_sems, recv_sems):
        x, y, c = _mesh_pos()
        return [_remote(src, dst, send_sems, recv_sems, i, (x, y, 1 - c)) for i, (src, dst) in enumerate(zip(srcs, dsts))]

    return _copies_exchange(make, ress, [jax.ShapeDtypeStruct(r.shape, F32) for r in ress], len(ress), _sibling_peer,
                            SIBLING_ID)


def _shared(ex):
    south = lax.axis_index("c") == 0
    return [jnp.concatenate([jnp.where(south, r, g), jnp.where(south, g, r)], axis=0) for r, g in zip(ex.ins, ex.outs)]


def _all_to_all_devices(vec):
    def others():
        x, y, c = _mesh_pos()
        return [((1 - x if r & 4 else x), (1 - y if r & 2 else y), (1 - c if r & 1 else c)) for r in range(1, 8)]

    def make(srcs, dsts, send_sems, recv_sems):
        x, y, c = _mesh_pos()
        me = 4 * x + 2 * y + c
        return [_remote(srcs[0], dsts[0].at[me], send_sems, recv_sems, r, peer) for r, peer in enumerate(others())]

    return _copies_exchange(make, [vec], [jax.ShapeDtypeStruct((8, *vec.shape), vec.dtype)], 7, others, ALL_ID)


def _sum_devices(stack):
    def body(s_ref, o_ref):
        total = s_ref[0]
        for d in range(1, 8):
            total = total + s_ref[d]
        o_ref[...] = total

    return pl.pallas_call(body, name="sum_devices", out_shape=jax.ShapeDtypeStruct(stack.shape[1:], F32),
                          compiler_params=pltpu.CompilerParams(vmem_limit_bytes=VMEM_LIMIT))(stack)


def _pad_rows(flat, rows):
    return jnp.concatenate([flat, jnp.zeros((rows * LANES - flat.shape[0],), flat.dtype)]).reshape(rows, LANES)


def _unshard(g4, shape, axis):
    a = g4.reshape(N_CHIPS, *shape)
    if axis == 0:
        return a.reshape(N_CHIPS * shape[0], shape[1])
    return jnp.transpose(a, (1, 0, 2)).reshape(shape[0], N_CHIPS * shape[1])


def _shard4(full, shape, axis):
    if axis == 0:
        return full.reshape(N_CHIPS, shape[0] * shape[1])
    a = full.reshape(shape[0], N_CHIPS, shape[1])
    return jnp.transpose(a, (1, 0, 2)).reshape(N_CHIPS, shape[0] * shape[1])


def _pad_axis0(a, rows):
    return jnp.concatenate([a, jnp.zeros((rows - a.shape[0], *a.shape[1:]), a.dtype)], axis=0)


def _pad_axis1(a, rows):
    return jnp.concatenate([a, jnp.zeros((a.shape[0], rows - a.shape[1], *a.shape[2:]), a.dtype)], axis=1)


def _shard_to_strip(name, w):
    _, (shape, axis, rows) = name, {n: (s, ax, r) for n, s, ax, r in BIG}[name]
    w2 = w.reshape(shape).astype(BF16)
    return _pad_axis0(w2.T if axis == 1 else w2, rows)


LOCAL_NAME = dict(w_in="w_in_t", w_q_b="w_q_t", w_kv_b="w_kv_t", w_out="w_out", w_gate="w_gate_t", w_up="w_up_t",
                  w_down="w_down")


WIN_SEGMENTS = ((576, 2112, 0), (2112, 2624, 1536), (0, 256, 2048), (256, 512, 2304), (512, 576, 2560), (2624, 2632, 2688))


def _strips_to_weight(name, g4):
    if name == "w_in":
        return _win_to_pad(g4[:, :IN_SHARD].reshape(IN_COLS, D_MODEL))
    if name == "w_q_b":
        return _qk_to_pad(g4.reshape(HEADS * QK_DIM, LORA))
    return g4.reshape(N_CHIPS * g4.shape[1], g4.shape[2])


def _grad_to_strips(name, g):
    if name == "w_in":
        strips = []
        for q in range(N_CHIPS):
            pieces = []
            for a, b, local in sorted(WIN_SEGMENTS):
                s, e = max(a, q * IN_SHARD), min(b, (q + 1) * IN_SHARD)
                if s < e:
                    pieces.append(g[local + s - a:local + e - a])
            pieces.append(jnp.zeros((IN_SHARD_P - IN_SHARD, D_MODEL), g.dtype))
            strips.append(jnp.concatenate(pieces, axis=0))
        return jnp.stack(strips)
    if name == "w_q_b":
        return _qk_from_pad(g).reshape(N_CHIPS, QK_DIM, LORA)
    return g.reshape(N_CHIPS, g.shape[0] // N_CHIPS, g.shape[1])


class _MeshPlan:
    LATE = dict(attn_norm_fwd=("w_in", "w_q_b", "w_kv_b"), attn_fwd=("w_up",), dn_chunk_fwd=("w_out", "w_gate"),
                ffn_glu_fwd=("w_down",))
    GROUP_A = ("w_down", "w_gate", "w_up", "w_out")
    GROUP_B = ("w_in", "w_q_b", "w_kv_b")

    def __init__(self, w):
        x, y, c = _mesh_pos()
        self.ci = jnp.reshape(c, (1,)).astype(jnp.int32)
        self.pi = jnp.reshape(2 * x + y, (1,)).astype(jnp.int32)
        self.strip = {n: _shard_to_strip(n, w[n]) for n, _, _, _ in BIG}
        self.gathers, self.weights, self.g, self.acc, self.reduced = {}, {}, {}, {}, {}
        self.sibs, self.sib, self.chip, self.share, self.halves = [], None, None, None, [None, None]

    def gather_small(self, small):
        ex = _all_gather_small(small)
        ex.run("all_gather_small")
        return _gathered(ex)[0]

    def weight(self, local_name):
        if local_name not in self.weights:
            for point, (names, ex) in list(self.gathers.items()):
                if ex.outs is not None:
                    for n, g4 in zip(names, _gathered(ex)):
                        if "/" in n:
                            n, half = n.split("/")
                            self.halves[int(half)] = g4
                            if None in self.halves:
                                continue
                            g4 = jnp.concatenate(self.halves, axis=1)
                        self.weights[LOCAL_NAME[n]] = _strips_to_weight(n, g4)
                    del self.gathers[point]
        return self.weights[local_name]

    def _shard(self, name):
        if "/" not in name:
            return self.strip[name]
        name, half = name.split("/")
        rows = self.strip[name].shape[0] // 2
        return self.strip[name][int(half) * rows:(int(half) + 1) * rows]

    def grad(self, local_name, value):
        name = {v: k for k, v in LOCAL_NAME.items()}[local_name]
        self.g[name] = _grad_to_strips(name, value)

    def _pair_add(self, names, gots):
        accs = _rs_pair_add("rs_pair_add_" + names[0], [self.g[n] for n in names], gots, self.ci, BF16)
        self.acc.update(zip(names, accs))

    def _chip_add(self, names, chip):
        return _rs_chip_add("rs_chip_add_" + names[0], [self.acc[n] for n in names], chip.outs, self.pi)

    def _take_shared(self, names, share):
        for n, strip in zip(names, _shared(share)):
            self.reduced[n] = strip

    def host(self, point):
        a, b = self.GROUP_A, self.GROUP_B
        if point in self.LATE:
            names = self.LATE[point]
            ex = _all_gather([self._shard(n) for n in names])
            self.gathers[point] = (names, ex)
            return ex
        if point in ("ffn_in_bwd", "mix_out_bwd"):
            names = dict(ffn_in_bwd=a[:3], mix_out_bwd=a[3:])[point]
            ex = _rs_to_sibling([self.g[n] for n in names])
            self.sibs.append(ex)
            return ex
        if point == "dn_chunk_bwd":
            self._pair_add(a, [o for ex in self.sibs for o in ex.outs])
            self.chip1 = _rs_to_chips([self.acc[n] for n in a[:2]])
            return self.chip1
        if point == "attn_bwd":
            self.chip2 = _rs_to_chips([self.acc[n] for n in a[2:]])
            return self.chip2
        if point == "mla_prep_bwd":
            ress = self._chip_add(a[:2], self.chip1) + self._chip_add(a[2:], self.chip2)
            self.share = _rs_share(ress)
            return self.share
        if point == "in_dx":
            self._take_shared(a, self.share)
            self.sib = _rs_to_sibling([self.g[n] for n in b])
            return self.sib
        if point == "attn_norm_bwd":
            self._pair_add(b, self.sib.outs)
            self.chip = _rs_to_chips([self.acc[n] for n in b])
            return self.chip
        return None

    def last_share(self):
        self.share = _rs_share(self._chip_add(self.GROUP_B, self.chip))
        return self.share

    def finish(self):
        self._take_shared(self.GROUP_B, self.share)
        return self.reduced


def _strip_to_shard(name, strip):
    shape, axis = {n: (s, ax) for n, s, ax, _ in BIG}[name]
    rows = shape[axis]
    return strip[:rows].T if axis == 1 else strip[:rows]


def kernel(x, meta_tokens, attn_norm_w, w_in, q_a_norm_w, w_q_b, kv_a_norm_w, w_kv_b, q_norm_w, k_norm_w, mla_out_norm_w, dn_conv_w, dn_A_log, dn_dt_bias, dn_out_norm_w, w_out, ffn_norm_w, w_gate, w_up, ffn_conv_w, ffn_conv_b, w_down, loss_target, m_meta_tokens, m_attn_norm_w, m_w_in, m_q_a_norm_w, m_w_q_b, m_kv_a_norm_w, m_w_kv_b, m_q_norm_w, m_k_norm_w, m_mla_out_norm_w, m_dn_conv_w, m_dn_A_log, m_dn_dt_bias, m_dn_out_norm_w, m_w_out, m_ffn_norm_w, m_w_gate, m_w_up, m_ffn_conv_w, m_ffn_conv_b, m_w_down, v_meta_tokens, v_attn_norm_w, v_w_in, v_q_a_norm_w, v_w_q_b, v_kv_a_norm_w, v_w_kv_b, v_q_norm_w, v_k_norm_w, v_mla_out_norm_w, v_dn_conv_w, v_dn_A_log, v_dn_dt_bias, v_dn_out_norm_w, v_w_out, v_ffn_norm_w, v_w_gate, v_w_up, v_ffn_conv_w, v_ffn_conv_b, v_w_down):
    local = dict(locals())
    w = {n: local[n] for n in WEIGHTS}
    m = {n: local["m_" + n] for n in WEIGHTS}
    v = {n: local["v_" + n] for n in WEIGHTS}
    p = 2 * lax.axis_index("x") + lax.axis_index("y")

    plan = _MeshPlan(w)
    wf = _pad_rows(jnp.concatenate([w[n].reshape(-1) for n, _, _ in SMALL_SHARDED]), SMALL_ROWS)
    gf = plan.gather_small(wf).reshape(N_CHIPS, -1)
    full = {}
    off = 0
    for n, s, ax in SMALL_SHARDED:
        full[n] = _unshard(gf[:, off:off + s[0] * s[1]], s, ax)
        off += s[0] * s[1]
    for n, _ in REPLICATED:
        full[n] = w[n]
    full["ffn_conv_w"] = _ff_to_pad(full["ffn_conv_w"], 1)
    full["ffn_conv_b"] = _ff_to_pad(full["ffn_conv_b"], 1)

    sq, grad_x, g = _local_step(x[0], loss_target[0], full, plan)
    g["ffn_conv_w"] = _ff_from_pad(g["ffn_conv_w"], 1)
    g["ffn_conv_b"] = _ff_from_pad(g["ffn_conv_b"], 1)

    small_all = [n for n, _, _ in SMALL_SHARDED] + [n for n, _ in REPLICATED]
    vec = jnp.concatenate([g[n].reshape(-1) for n in small_all] + [jnp.reshape(0.5 / D_MODEL * jnp.sum(sq), (1,))])
    vec = _pad_rows(vec, -(-vec.shape[0] // (8 * LANES)) * 8)
    a2a = _all_to_all_devices(vec)

    gs, delta, new_m, new_v = {}, {}, {}, {}
    big = {n: (s, ax) for n, s, ax, _ in BIG}

    def adamw_big(n, strips, host=None):
        s, ax = big[n]
        if ax == 1 and s[1] % 8:
            there = lambda a: jnp.transpose(a, (2, 0, 1))
            back = lambda a: jnp.transpose(a, (1, 2, 0))
            g3 = strips[n][:s[1]].reshape(s[1], 1, s[0])
            d2, m2, v2 = _adamw_rows3d("adamw_" + n, there(w[n]), g3, there(m[n]), there(v[n]))
            gs[n], delta[n], new_m[n], new_v[n] = back(g3), back(d2), back(m2), back(v2)
            return
        flip = ax == 1 and s[1] % 8 == 0
        there = (lambda a: a.reshape(s).T) if flip else (lambda a: a.reshape(s))
        back = (lambda a: a.T.reshape(w[n].shape)) if flip else (lambda a: a.reshape(w[n].shape))
        strip = strips[n] if flip or ax == 0 else strips[n][:s[1]].T
        g2, d2, m2, v2 = _adamw_call("adamw_" + n, there(w[n]), strip, there(m[n]), there(v[n]), host=host)
        gs[n], delta[n], new_m[n], new_v[n] = back(g2), back(d2), back(m2), back(v2)

    adamw_big("w_down", plan.reduced, host=a2a)
    adamw_big("w_gate", plan.reduced, host=plan.last_share())
    adamw_big("w_up", plan.reduced)
    adamw_big("w_out", plan.reduced)
    strips = plan.finish()
    for n in plan.GROUP_B:
        adamw_big(n, strips)
    me = 4 * lax.axis_index("x") + 2 * lax.axis_index("y") + lax.axis_index("c")
    red = _sum_devices(lax.dynamic_update_slice(a2a.outs[0], vec[None], (me, 0, 0))).reshape(-1)
    off = 0
    for n in small_all:
        tot = red[off:off + g[n].size].reshape(g[n].shape)
        off += g[n].size
        shard = {sn: (s, ax) for sn, s, ax in SMALL_SHARDED}.get(n)
        if shard is not None:
            tot = lax.dynamic_slice_in_dim(tot, p * shard[0][1], shard[0][1], axis=1)
        gs[n] = tot
    loss = red[off]
    two_d = lambda a: a.reshape(a.shape[-2], a.shape[-1])
    outs = _adamw_small([two_d(w[n]) for n in small_all], [two_d(gs[n]) for n in small_all],
                        [two_d(m[n]) for n in small_all], [two_d(v[n]) for n in small_all])
    for i, n in enumerate(small_all):
        for dst, src in ((delta, outs[0]), (new_m, outs[1]), (new_v, outs[2])):
            dst[n] = src[i].reshape(w[n].shape)

    grad_out = [gs[n].reshape(w[n].shape) for n in WEIGHTS]
    return (loss, grad_x[None], *grad_out, *[delta[n] for n in WEIGHTS], *[new_m[n] for n in WEIGHTS],
            *[new_v[n] for n in WEIGHTS])
```

```python
import functools
import math

import jax
import jax.numpy as jnp
import numpy as np
from jax import lax
from jax.experimental import pallas as pl
from jax.experimental.pallas import tpu as pltpu

F32 = jnp.float32
BF16 = jnp.bfloat16
HI = lax.Precision.HIGHEST
MESH = pl.DeviceIdType.MESH

N_META = 16
D_MODEL = 1024
HEADS = 4
HEAD = 128
ROPE = 64
QK_DIM = HEAD + ROPE
QK_PAD = 2 * HEAD
LORA = 256
DN_WIDTH = HEADS * HEAD
CHUNK = 64
D_FF = 2816
N_CHIPS = 4
FF_SHARD = D_FF // N_CHIPS
FF_BLOCK = 768
D_FF_P = N_CHIPS * FF_BLOCK
IN_COLS = 2632
IN_SHARD = IN_COLS // N_CHIPS
IN_SHARD_P = 672
NORM_EPS = 1e-6
ROPE_THETA = 10000.0
LANES = 512

ADAM_LR, ADAM_B1, ADAM_B2, ADAM_EPS, ADAM_WD, ADAM_STEP = 0.001, 0.9, 0.999, 1e-08, 0.01, 10

VMEM_LIMIT = 56 * 1024 * 1024

BIG = (("w_in", (1024, 658), 1, IN_SHARD_P), ("w_q_b", (256, 192), 1, 192), ("w_kv_b", (256, 256), 1, 256),
       ("w_out", (256, 1024), 0, 256), ("w_gate", (1024, 704), 1, FF_BLOCK), ("w_up", (1024, 704), 1, FF_BLOCK),
       ("w_down", (704, 1024), 0, FF_BLOCK))
SMALL_SHARDED = (("meta_tokens", (16, 256), 1), ("dn_conv_w", (4, 384), 1), ("ffn_conv_w", (3, 704), 1))
REPLICATED = (("attn_norm_w", 1024), ("q_a_norm_w", 256), ("kv_a_norm_w", 256), ("q_norm_w", 192), ("k_norm_w", 192),
              ("mla_out_norm_w", 128), ("dn_A_log", 4), ("dn_dt_bias", 4), ("dn_out_norm_w", 128), ("ffn_norm_w", 1024),
              ("ffn_conv_b", 2816))
WEIGHTS = ("meta_tokens", "attn_norm_w", "w_in", "q_a_norm_w", "w_q_b", "kv_a_norm_w", "w_kv_b", "q_norm_w", "k_norm_w",
           "mla_out_norm_w", "dn_conv_w", "dn_A_log", "dn_dt_bias", "dn_out_norm_w", "w_out", "ffn_norm_w", "w_gate",
           "w_up", "ffn_conv_w", "ffn_conv_b", "w_down")

SMALL_ROWS = 16


def _cparams(sem):
    return pltpu.CompilerParams(dimension_semantics=sem, vmem_limit_bytes=VMEM_LIMIT)


class _Exchange:
    def __init__(self, prog, ins, out_shape, nsem, peers=None, cid=None):
        self.prog, self.ins, self.out_shape, self.nsem = prog, list(ins), list(out_shape), nsem
        self.peers, self.cid = peers, cid
        self.outs = None

    def sems(self):
        return [pltpu.SemaphoreType.DMA((self.nsem,)), pltpu.SemaphoreType.DMA((self.nsem,))]

    def programs(self, in_refs, out_refs, send_sems, recv_sems):
        start, finish = self.prog(in_refs, out_refs, send_sems, recv_sems)
        if self.cid is None:
            return start, finish
        peers = self.peers()

        def shake_and_start():
            barrier = pltpu.get_barrier_semaphore()
            for peer in peers:
                pl.semaphore_signal(barrier, inc=1, device_id=peer, device_id_type=MESH)
            pl.semaphore_wait(barrier, len(peers))
            start()

        return shake_and_start, finish

    def cparams(self, **kw):
        return pltpu.CompilerParams(has_side_effects=True, collective_id=self.cid, **kw)

    def run(self, name):
        any_spec = pl.BlockSpec(memory_space=pl.ANY)
        n = len(self.ins)

        def body(*refs):
            start, finish = self.programs(refs[:n], refs[n:-2], refs[-2], refs[-1])
            start()
            finish()

        self.outs = pl.pallas_call(
            body, name=name, in_specs=[any_spec] * n, out_specs=[any_spec] * len(self.out_shape),
            out_shape=self.out_shape, scratch_shapes=self.sems(), compiler_params=self.cparams())(*self.ins)
        return self.outs


def _pcall(body, name, grid, in_specs, out_specs, out_shape, args, sem, scratch_shapes=(), host=None):
    single = not isinstance(out_shape, (list, tuple))
    out_specs, out_shape = ([out_specs], [out_shape]) if single else (list(out_specs), list(out_shape))
    if host is None:
        outs = pl.pallas_call(body, name=name, grid=grid, in_specs=list(in_specs), out_specs=out_specs, out_shape=out_shape,
                              scratch_shapes=list(scratch_shapes), compiler_params=_cparams(sem))(*args)
        return outs[0] if single else outs
    any_spec = pl.BlockSpec(memory_space=pl.ANY)
    n_in, n_out, n_scr, nx_in, nx_out = len(in_specs), len(out_specs), len(scratch_shapes), len(host.ins), len(host.out_shape)

    def hosted(*refs):
        c_in, x_in = refs[:n_in], refs[n_in:n_in + nx_in]
        o0 = n_in + nx_in
        c_out, x_out = refs[o0:o0 + n_out], refs[o0 + n_out:o0 + n_out + nx_out]
        s0 = o0 + n_out + nx_out
        start, finish = host.programs(x_in, x_out, refs[s0 + n_scr], refs[s0 + n_scr + 1])
        first = functools.reduce(jnp.logical_and, [pl.program_id(d) == 0 for d in range(len(grid))])
        last = functools.reduce(jnp.logical_and, [pl.program_id(d) == grid[d] - 1 for d in range(len(grid))])
        pl.when(first)(start)
        body(*c_in, *c_out, *refs[s0:s0 + n_scr])
        pl.when(last)(finish)

    outs = pl.pallas_call(
        hosted, name=name, grid=grid, in_specs=list(in_specs) + [any_spec] * nx_in,
        out_specs=out_specs + [any_spec] * nx_out, out_shape=out_shape + host.out_shape,
        scratch_shapes=list(scratch_shapes) + host.sems(),
        compiler_params=host.cparams(dimension_semantics=sem, vmem_limit_bytes=VMEM_LIMIT))(*args, *host.ins)
    host.outs = outs[n_out:]
    return outs[0] if single else outs[:n_out]


NN, NT, TN = ((1,), (0,)), ((1,), (1,)), ((0,), (0,))


def _shift_dims(dims, batch):
    if not batch:
        return (dims, ((), ()))
    return (((dims[0][0] + 1,), (dims[1][0] + 1,)), ((0,), (0,)))


def _make_mm(dims, exact, batch=False):
    def raw(a, b, d):
        dn = _shift_dims(d, batch)
        if exact == "split_lhs":
            ah, bh = a.astype(BF16), b.astype(BF16)
            al = (a - ah.astype(F32)).astype(BF16)
            return lax.dot_general(ah, bh, dn, preferred_element_type=F32) + lax.dot_general(al, bh, dn,
                                                                                              preferred_element_type=F32)
        if exact == "split":
            ah, bh = a.astype(BF16), b.astype(BF16)
            al, bl = (a - ah.astype(F32)).astype(BF16), (b - bh.astype(F32)).astype(BF16)
            dot = lambda p, q: lax.dot_general(p, q, dn, preferred_element_type=F32)
            return dot(ah, bh) + (dot(ah, bl) + dot(al, bh))
        if exact:
            return lax.dot_general(a.astype(F32), b.astype(F32), dn, precision=HI, preferred_element_type=F32)
        return lax.dot_general(a.astype(BF16), b.astype(BF16), dn, preferred_element_type=F32)

    @jax.custom_vjp
    def mm(a, b):
        return raw(a, b, dims)

    def fwd(a, b):
        return raw(a, b, dims), (a, b)

    def bwd(res, g):
        a, b = res
        if dims == NN:
            da, db = raw(g, b, NT), raw(a, g, TN)
        elif dims == NT:
            da, db = raw(g, b, NN), raw(g, a, TN)
        else:
            da, db = raw(b, g, NT), raw(a, g, NN)
        return da.astype(a.dtype), db.astype(b.dtype)

    mm.defvjp(fwd, bwd)
    return mm


_mm = _make_mm(NN, False)
_mm_nt = _make_mm(NT, False)
_mm_tn = _make_mm(TN, False)
_mmx = _make_mm(NN, "split_lhs")
_bmm = _make_mm(NN, False, batch=True)
_bmm_nt = _make_mm(NT, False, batch=True)
_bmm_tn = _make_mm(TN, False, batch=True)
_bmmx = _make_mm(NN, True, batch=True)
_bmms = _make_mm(NN, "split", batch=True)
_bmms_nt = _make_mm(NT, "split", batch=True)
_bmms_tn = _make_mm(TN, "split", batch=True)


@jax.custom_vjp
def _unit_lower_inv(a):
    n = a.shape[-1]
    eye = (lax.broadcasted_iota(jnp.int32, a.shape, 1) == lax.broadcasted_iota(jnp.int32, a.shape, 2)).astype(F32)
    x = -a
    t = eye + x
    for _ in range(max(n.bit_length() - 2, 0)):
        x = _bmms(x, x)
        t = t + _bmms(t, x)
    return t


def _unit_lower_inv_fwd(a):
    t = _unit_lower_inv(a)
    return t, t


def _unit_lower_inv_bwd(t, g):
    return (-_bmms_tn(t, _bmms_nt(g, t)),)


_unit_lower_inv.defvjp(_unit_lower_inv_fwd, _unit_lower_inv_bwd)


def _scan_chunk_rows(x, reverse):
    nb, c, w = x.shape
    y = x.reshape(nb * c, w)
    pos = lax.broadcasted_iota(jnp.int32, y.shape, 0) % c
    step = 1
    while step < c:
        if reverse:
            y = y + jnp.where(pos < c - step, pltpu.roll(y, nb * c - step, 0), 0.0)
        else:
            y = y + jnp.where(pos >= step, pltpu.roll(y, step, 0), 0.0)
        step *= 2
    return y.reshape(nb, c, w)


@jax.custom_vjp
def _chunk_cumsum(x):
    return _scan_chunk_rows(x, False)


_chunk_cumsum.defvjp(lambda x: (_scan_chunk_rows(x, False), None), lambda _, g: (_scan_chunk_rows(g, True),))


def _rms(x, w, n):
    ms = jnp.sum(x * x, axis=-1, keepdims=True) * (1.0 / n)
    return x * lax.rsqrt(ms + NORM_EPS) * w


def _silu(x):
    return x * jax.nn.sigmoid(x)


def _softplus(x):
    return jnp.maximum(x, 0.0) + jnp.log(1.0 + jnp.exp(-jnp.abs(x)))


def _rope(x, cos, sin, perm):
    return x * cos + _mmx(x, perm) * sin


def _mla_prep_fn(rows, consts):
    q_lat, kv_lat, k_pe, cos, sin = rows
    qn = _rms(q_lat, consts["qa_w"], LORA)
    kvn = _rms(kv_lat, consts["kva_w"], LORA)
    outs = []
    for h in range(HEADS):
        q_n = _mm_nt(qn, consts["wq_n"][h])
        q_r = _mm_nt(qn, consts["wq_r"][h])
        rs = lax.rsqrt((jnp.sum(q_n * q_n, -1, keepdims=True) + jnp.sum(q_r * q_r, -1, keepdims=True)) * (1.0 / QK_DIM)
                       + NORM_EPS)
        q_n = q_n * rs * consts["qn_n"]
        q_r = _rope(q_r * rs * consts["qn_r"], cos, sin, consts["perm"])
        k_n = _mm_nt(kvn, consts["wk_n"][h])
        v = _mm_nt(kvn, consts["wv"][h])
        rk = lax.rsqrt((jnp.sum(k_n * k_n, -1, keepdims=True) + jnp.sum(k_pe * k_pe, -1, keepdims=True)) * (1.0 / QK_DIM)
                       + NORM_EPS)
        k_n = k_n * rk * consts["kn_n"]
        k_r = _rope(k_pe * rk * consts["kn_r"], cos, sin, consts["perm"])
        outs += [q_n, q_r, k_n, k_r, v]
    return tuple(outs)


def _attn_fn(q, k, v, row0):
    s = _mm_nt(q, k) * (1.0 / math.sqrt(QK_DIM))
    qpos = row0 + lax.broadcasted_iota(jnp.int32, s.shape, 0)
    kpos = lax.broadcasted_iota(jnp.int32, s.shape, 1)
    s = jnp.where(kpos <= qpos, s, -1e30)
    m = lax.stop_gradient(jnp.max(s, axis=-1, keepdims=True))
    p = jnp.exp(s - m)
    p = p / jnp.sum(p, axis=-1, keepdims=True)
    return _mm(p, v)


def _dn_prep_fn(rows, consts):
    qc, kc, ab = rows
    a_b = _mmx(ab, consts["sel_a"])
    b_b = _mmx(ab, consts["sel_b"])
    beta = jax.nn.sigmoid(b_b)
    g = -jnp.exp(consts["alog"]) * _softplus(a_b + consts["dtb"])
    qs, ks = [], []
    for h in range(HEADS):
        q, k = qc[h], kc[h]
        qs.append(q * lax.rsqrt(jnp.sum(q * q, -1, keepdims=True) + NORM_EPS))
        ks.append(k * lax.rsqrt(jnp.sum(k * k, -1, keepdims=True) + NORM_EPS))
    return tuple(qs), tuple(ks), g, beta


def _dn_chunk_fn(q, k, v, gb, g64, bb):
    nb = q.shape[0]
    ri = lax.broadcasted_iota(jnp.int32, (nb, CHUNK, CHUNK), 1)
    ci = lax.broadcasted_iota(jnp.int32, (nb, CHUNK, CHUNK), 2)
    tri = ri >= ci
    strict = ri > ci
    tril = tri.astype(F32)
    eye = (ri == ci).astype(F32)
    ones = jnp.ones((nb, CHUNK, CHUNK), F32)
    gc = _chunk_cumsum(gb)
    gc64 = _chunk_cumsum(g64)
    grow = _bmmx(ones, eye * gc64)
    diff = gc64 - grow
    decay = jnp.where(tri, jnp.exp(jnp.where(tri, diff, 0.0)), 0.0)
    kb = k * bb
    vb = v * bb
    a = jnp.where(strict, _bmm_nt(kb, k) * decay, 0.0)
    tinv = _unit_lower_inv(a)
    u = _bmm(tinv, vb)
    w = _bmm(tinv, kb * jnp.exp(gc))
    qs = q * (1.0 / math.sqrt(HEAD))
    qk = _bmm_nt(qs, k) * decay
    qg = qs * jnp.exp(gc)
    glast = jnp.sum(gb, axis=1, keepdims=True)
    kdec = k * jnp.exp(glast - gc)
    n_mat = _bmm_tn(kdec, w)
    b_mat = _bmm_tn(kdec, u)
    q_eff = qg - _bmm(qk, w)
    o_own = _bmm(qk, u)
    return n_mat, b_mat, q_eff, o_own, jnp.exp(glast)


def _dn_rec_fn(s, n_mat, b_mat, eg):
    return s * eg - _mm(n_mat, s) + b_mat


def _dn_o_fn(s, q_eff, o_own):
    return _bmm(q_eff, s) + o_own


def _dn_out_fn(o, z, w):
    return _rms(o, w, HEAD) * _silu(z)


def _row_tile(t, parts=8):
    return t // parts if (t // parts) % 16 == 0 else t


def _tile(n, pref, unit):
    best = n
    for cand in range(unit, min(n, pref) + 1, unit):
        if n % cand == 0:
            best = cand
    return best if best <= pref else n


def _rows_call(name, body, rows, consts, outs, accs, r, host=None):
    rows = [a if isinstance(a, tuple) else (a, a.shape[1], 0) for a in rows]
    t = rows[0][0].shape[0]
    zero = lambda nd: (lambda i: (0,) * nd)
    in_specs = [pl.BlockSpec((r, w), functools.partial(lambda i, b: (i, b), b=blk)) for _, w, blk in rows]
    rows = [a for a, _, _ in rows]
    in_specs += [pl.BlockSpec(a.shape, zero(a.ndim)) for a in consts]
    out_shape = [jax.ShapeDtypeStruct((t, w), dt) for w, dt in outs] + [jax.ShapeDtypeStruct(s, F32) for s in accs]
    out_specs = [pl.BlockSpec((r, w), lambda i: (i, 0)) for w, _ in outs] + [pl.BlockSpec(s, zero(len(s))) for s in accs]
    return _pcall(body, name, (t // r,), in_specs, out_specs, out_shape, [*rows, *consts], ("arbitrary",), host=host)


def _accumulate(ref, val):
    @pl.when(pl.program_id(0) == 0)
    def _():
        ref[...] = jnp.zeros(ref.shape, ref.dtype)

    ref[...] += val


def _matmul(name, a, b, dims, out_dtype, res=None, host=None):
    if dims == "nn":
        (m, k), n = a.shape, b.shape[1]
    elif dims == "nt":
        (m, k), n = a.shape, b.shape[0]
    else:
        (k, m), n = a.shape, b.shape[1]
    tm = _tile(m, 1100, 16) if dims != "tn" else _tile(m, 640, 128)
    tn = _tile(n, 1408, 128)
    if dims == "nn":
        a_spec, b_spec, dn = pl.BlockSpec((tm, k), lambda i, j: (i, 0)), pl.BlockSpec((k, tn), lambda i, j: (0, j)), NN
    elif dims == "nt":
        a_spec, b_spec, dn = pl.BlockSpec((tm, k), lambda i, j: (i, 0)), pl.BlockSpec((tn, k), lambda i, j: (j, 0)), NT
    else:
        a_spec, b_spec, dn = pl.BlockSpec((k, tm), lambda i, j: (0, i)), pl.BlockSpec((k, tn), lambda i, j: (0, j)), TN
    o_spec = pl.BlockSpec((tm, tn), lambda i, j: (i, j))

    def body(*refs):
        a_ref, b_ref, o_ref = refs[0], refs[1], refs[-1]
        acc = lax.dot_general(a_ref[...].astype(BF16), b_ref[...].astype(BF16), (dn, ((), ())),
                              preferred_element_type=F32)
        if res is not None:
            acc = acc + refs[2][...]
        o_ref[...] = acc.astype(out_dtype)

    ins = [a, b] + ([res] if res is not None else [])
    specs = [a_spec, b_spec] + ([o_spec] if res is not None else [])
    return _pcall(body, name, (m // tm, n // tn), specs, o_spec, jax.ShapeDtypeStruct((m, n), out_dtype), ins,
                  ("arbitrary", "arbitrary"), host=host)


def _rms_fwd(name, h, w, host=None):
    n = h.shape[1]

    def body(h_ref, w_ref, o_ref):
        o_ref[...] = _rms(h_ref[...], w_ref[...], n).astype(BF16)

    return _rows_call(name, body, [h], [w], [(n, BF16)], [], _row_tile(h.shape[0]), host=host)[0]


def _rms_bwd(name, h, w, cts, resid, host=None):
    n = h.shape[1]
    nct = len(cts)

    def body(*refs):
        h_ref, ct_refs, r_ref, w_ref = refs[0], refs[1:1 + nct], refs[1 + nct], refs[2 + nct]
        dh_ref, dh16_ref, dw_ref = refs[-3], refs[-2], refs[-1]
        ct = ct_refs[0][...].astype(F32)
        for c in ct_refs[1:]:
            ct = ct + c[...].astype(F32)
        _, vjp = jax.vjp(lambda x, ww: _rms(x, ww, n), h_ref[...], w_ref[...])
        dh, dw = vjp(ct)
        dh = dh + r_ref[...]
        dh_ref[...] = dh
        dh16_ref[...] = dh.astype(BF16)
        _accumulate(dw_ref, dw)

    return _rows_call(name, body, [h, *cts, resid], [w], [(n, F32), (n, BF16)], [(1, n)], _row_tile(h.shape[0]), host=host)


def _mla_consts_from_refs(qa, wq, kva, wkv, qn, kn, perm):
    f = lambda r: r[...].astype(F32)
    return dict(
        qa_w=f(qa), kva_w=f(kva), perm=f(perm),
        wq_n=[wq[h * QK_PAD:h * QK_PAD + HEAD, :].astype(F32) for h in range(HEADS)],
        wq_r=[wq[h * QK_PAD + HEAD:(h + 1) * QK_PAD, :].astype(F32) for h in range(HEADS)],
        wk_n=[wkv[h * QK_PAD:h * QK_PAD + HEAD, :].astype(F32) for h in range(HEADS)],
        wv=[wkv[h * QK_PAD + HEAD:(h + 1) * QK_PAD, :].astype(F32) for h in range(HEADS)],
        qn_n=qn[:, 0:HEAD], qn_r=qn[:, HEAD:QK_PAD], kn_n=kn[:, 0:HEAD], kn_r=kn[:, HEAD:QK_PAD])


def _mla_prep_fwd(q_lat, kv_lat, k_pe, cos, sin, qa, wq, kva, wkv, qn, kn, perm):
    def body(ql, kvl, kp, c, s, qa_r, wq_r, kva_r, wkv_r, qn_r, kn_r, p_r, q_out, k_out, v_out):
        consts = _mla_consts_from_refs(qa_r, wq_r, kva_r, wkv_r, qn_r, kn_r, p_r)
        outs = _mla_prep_fn((ql[...], kvl[...], kp[...], c[...], s[...]), consts)
        for h in range(HEADS):
            q_n, q_r, k_n, k_r, v = outs[5 * h:5 * h + 5]
            q_out[:, h * QK_PAD:h * QK_PAD + HEAD] = q_n.astype(BF16)
            q_out[:, h * QK_PAD + HEAD:(h + 1) * QK_PAD] = q_r.astype(BF16)
            k_out[:, h * QK_PAD:h * QK_PAD + HEAD] = k_n.astype(BF16)
            k_out[:, h * QK_PAD + HEAD:(h + 1) * QK_PAD] = k_r.astype(BF16)
            v_out[:, h * HEAD:(h + 1) * HEAD] = v.astype(BF16)

    return _rows_call("mla_prep_fwd", body, [q_lat, kv_lat, k_pe, cos, sin], [qa, wq, kva, wkv, qn, kn, perm],
                      [(HEADS * QK_PAD, BF16), (HEADS * QK_PAD, BF16), (DN_WIDTH, BF16)], [], _row_tile(cos.shape[0], 4))


def _mla_prep_bwd(q_lat, kv_lat, k_pe, cos, sin, dq, dk, dv, qa, wq, kva, wkv, qn, kn, perm, host=None):
    def body(ql, kvl, kp, c, s, dq_r, dk_r, dv_r, qa_r, wq_r, kva_r, wkv_r, qn_r, kn_r, p_r,
             dql, dkvl, dkp, dqa, dwq, dkva, dwkv, dqn, dkn):
        consts = _mla_consts_from_refs(qa_r, wq_r, kva_r, wkv_r, qn_r, kn_r, p_r)
        cc, ss, pm = c[...], s[...], consts.pop("perm")
        _, vjp = jax.vjp(lambda rows, cs: _mla_prep_fn((*rows, cc, ss), dict(cs, perm=pm)), (ql[...], kvl[...], kp[...]),
                         consts)
        cts = []
        for h in range(HEADS):
            cts += [dq_r[:, h * QK_PAD:h * QK_PAD + HEAD], dq_r[:, h * QK_PAD + HEAD:(h + 1) * QK_PAD],
                    dk_r[:, h * QK_PAD:h * QK_PAD + HEAD], dk_r[:, h * QK_PAD + HEAD:(h + 1) * QK_PAD],
                    dv_r[:, h * HEAD:(h + 1) * HEAD]]
        (d_ql, d_kvl, d_kp), dc = vjp(tuple(cts))
        dql[...] = d_ql.astype(BF16)
        dkvl[...] = d_kvl.astype(BF16)
        dkp[...] = d_kp.astype(BF16)
        first = pl.program_id(0) == 0

        def acc(ref, sl, val):
            @pl.when(first)
            def _():
                ref[sl] = val

            @pl.when(jnp.logical_not(first))
            def _():
                ref[sl] += val

        full = (slice(None), slice(None))
        acc(dqa, full, dc["qa_w"])
        acc(dkva, full, dc["kva_w"])
        for h in range(HEADS):
            acc(dwq, (slice(h * QK_PAD, h * QK_PAD + HEAD), slice(None)), dc["wq_n"][h])
            acc(dwq, (slice(h * QK_PAD + HEAD, (h + 1) * QK_PAD), slice(None)), dc["wq_r"][h])
            acc(dwkv, (slice(h * QK_PAD, h * QK_PAD + HEAD), slice(None)), dc["wk_n"][h])
            acc(dwkv, (slice(h * QK_PAD + HEAD, (h + 1) * QK_PAD), slice(None)), dc["wv"][h])
        acc(dqn, (slice(None), slice(0, HEAD)), dc["qn_n"])
        acc(dqn, (slice(None), slice(HEAD, QK_PAD)), dc["qn_r"])
        acc(dkn, (slice(None), slice(0, HEAD)), dc["kn_n"])
        acc(dkn, (slice(None), slice(HEAD, QK_PAD)), dc["kn_r"])

    return _rows_call("mla_prep_bwd", body, [q_lat, kv_lat, k_pe, cos, sin, dq, dk, dv],
                      [qa, wq, kva, wkv, qn, kn, perm],
                      [(LORA, BF16), (LORA, BF16), (HEAD, BF16)],
                      [(1, LORA), wq.shape, (1, LORA), wkv.shape, (1, QK_PAD), (1, QK_PAD)], _row_tile(cos.shape[0], 4),
                      host=host)


ATTN_Q_ROWS = 512


def _attn_blocks(t):
    return [(r0, min(ATTN_Q_ROWS, t - r0)) for r0 in range(0, t, ATTN_Q_ROWS)]


def _attn_fwd(q, k, v, host=None):
    t = q.shape[0]

    def body(q_ref, k_ref, v_ref, o_ref):
        for r0, rows in _attn_blocks(t):
            ext = r0 + rows
            o_ref[r0:ext, :] = _attn_fn(q_ref[r0:ext, :], k_ref[0:ext, :], v_ref[0:ext, :], r0)

    qk_spec = pl.BlockSpec((t, QK_PAD), lambda h: (0, h))
    v_spec = pl.BlockSpec((t, HEAD), lambda h: (0, h))
    return _pcall(body, "attn_fwd", (HEADS,), [qk_spec, qk_spec, v_spec], v_spec,
                  jax.ShapeDtypeStruct((t, HEADS * HEAD), F32), [q, k, v], ("arbitrary",), host=host)


def _attn_bwd(q, k, v, do, host=None):
    t = q.shape[0]

    def body(q_ref, k_ref, v_ref, do_ref, dq_ref, dk_ref, dv_ref):
        dk_ref[...] = jnp.zeros(dk_ref.shape, F32)
        dv_ref[...] = jnp.zeros(dv_ref.shape, F32)
        for r0, rows in _attn_blocks(t):
            ext = r0 + rows
            _, vjp = jax.vjp(functools.partial(_attn_fn, row0=r0), q_ref[r0:ext, :].astype(F32),
                             k_ref[0:ext, :].astype(F32), v_ref[0:ext, :].astype(F32))
            dq, dk, dv = vjp(do_ref[r0:ext, :])
            dq_ref[r0:ext, :] = dq
            dk_ref[0:ext, :] += dk
            dv_ref[0:ext, :] += dv

    qk_spec = pl.BlockSpec((t, QK_PAD), lambda h: (0, h))
    v_spec = pl.BlockSpec((t, HEAD), lambda h: (0, h))
    return _pcall(body, "attn_bwd", (HEADS,), [qk_spec, qk_spec, v_spec, v_spec], [qk_spec, qk_spec, v_spec],
                  [jax.ShapeDtypeStruct((t, HEADS * QK_PAD), F32), jax.ShapeDtypeStruct((t, HEADS * QK_PAD), F32),
                   jax.ShapeDtypeStruct((t, HEADS * HEAD), F32)], [q, k, v, do], ("arbitrary",), host=host)


def _mix_out_proj(o_mla, o_dn, z, w_mla, w_dn, w_out, h0, w_ffn):
    def body(om_ref, od_ref, z_ref, h0_ref, wm_ref, wd_ref, wo_ref, wf_ref, mixed_ref, h1_ref, n2_ref):
        for h in range(HEADS):
            sl = slice(h * HEAD, (h + 1) * HEAD)
            mixed_ref[:, sl] = _rms(om_ref[:, sl], wm_ref[...], HEAD).astype(BF16)
            mixed_ref[:, DN_WIDTH + h * HEAD:DN_WIDTH + (h + 1) * HEAD] = _dn_out_fn(od_ref[:, sl], z_ref[:, sl],
                                                                                     wd_ref[...]).astype(BF16)
        h1 = _mm(mixed_ref[...], wo_ref[...]) + h0_ref[...]
        h1_ref[...] = h1
        n2_ref[...] = _rms(h1, wf_ref[...], D_MODEL).astype(BF16)

    return _rows_call("mix_out_proj", body, [o_mla, o_dn, z, h0], [w_mla, w_dn, w_out, w_ffn],
                      [(D_MODEL, BF16), (D_MODEL, F32), (D_MODEL, BF16)], [], _row_tile(o_mla.shape[0], 4))


def _down_proj_loss(act, w_down, h1, tgt, n_valid):
    t, n = h1.shape
    r = _row_tile(t, 4)

    def body(a_ref, h_ref, t_ref, w_ref, dy_ref, dy16_ref, acc_ref):
        h2 = _mm(a_ref[...], w_ref[...]) + h_ref[...]
        rows = pl.program_id(0) * r + lax.broadcasted_iota(jnp.int32, (r, n), 0)
        valid = jnp.logical_and(rows >= N_META, rows < n_valid)
        e = jnp.where(valid, h2 - t_ref[...], 0.0)
        dy = e * (1.0 / n)
        dy_ref[...] = dy
        dy16_ref[...] = dy.astype(BF16)
        _accumulate(acc_ref, jnp.sum(e * e, axis=0, keepdims=True))

    return _rows_call("down_proj_loss", body, [act, h1, tgt], [w_down], [(n, F32), (n, BF16)], [(1, n)], r)


def _in_proj_bwd_x(pieces, win, host=None):
    offs = np.cumsum([0] + [p.shape[1] for p in pieces])

    def body(*refs):
        p_refs, w_ref, o_ref = refs[:len(pieces)], refs[len(pieces)], refs[-1]
        acc = None
        for i, p_ref in enumerate(p_refs):
            part = _mm(p_ref[...], w_ref[int(offs[i]):int(offs[i + 1]), :])
            acc = part if acc is None else acc + part
        o_ref[...] = acc.astype(BF16)

    return _rows_call("in_dx", body, pieces, [win], [(win.shape[1], BF16)], [], _row_tile(pieces[0].shape[0], 4), host=host)[0]


def _in_proj_bwd_w(pieces, u):
    offs = np.cumsum([0] + [p.shape[1] for p in pieces])

    def body(*refs):
        p_refs, u_ref, o_ref = refs[:len(pieces)], refs[len(pieces)], refs[-1]
        first = pl.program_id(0) == 0
        uv = u_ref[...]
        for i, p_ref in enumerate(p_refs):
            rows = slice(int(offs[i]), int(offs[i + 1]))
            part = _mm_tn(p_ref[...], uv)

            @pl.when(first)
            def _():
                o_ref[rows, :] = part

            @pl.when(jnp.logical_not(first))
            def _():
                o_ref[rows, :] += part

    return _rows_call("in_dw", body, [*pieces, u], [], [], [(int(offs[-1]), u.shape[1])], _row_tile(u.shape[0], 2))[0]


def _ffn_in_bwd(dgpre, dup, w_gate_t, w_up_t, h1, dy, w_ffn, host=None):
    n = h1.shape[1]

    def body(dg_ref, du_ref, h_ref, dy_ref, wg_ref, wu_ref, w_ref, dh_ref, dh16_ref, dw_ref):
        ct = _mm(dg_ref[...], wg_ref[...]) + _mm(du_ref[...], wu_ref[...])
        _, vjp = jax.vjp(lambda x, ww: _rms(x, ww, n), h_ref[...], w_ref[...])
        dh, dw = vjp(ct)
        dh = dh + dy_ref[...]
        dh_ref[...] = dh
        dh16_ref[...] = dh.astype(BF16)
        _accumulate(dw_ref, dw)

    return _rows_call("ffn_in_bwd", body, [dgpre, dup, h1, dy], [w_gate_t, w_up_t, w_ffn], [(n, F32), (n, BF16)], [(1, n)],
                      _row_tile(h1.shape[0]), host=host)


def _mix_out_bwd(o_mla, o_dn, z, dh1, w_out, w_mla, w_dn, host=None):
    def body(om_ref, od_ref, z_ref, dh_ref, wo_ref, wm_ref, wd_ref, dom_ref, dod_ref, dz_ref, dwm_ref, dwd_ref):
        dwm = dwd = None
        for h in range(HEADS):
            sl = slice(h * HEAD, (h + 1) * HEAD)
            _, vjp = jax.vjp(lambda o, w: _rms(o, w, HEAD), om_ref[:, sl], wm_ref[...])
            do, dw = vjp(_mm_nt(dh_ref[...], wo_ref[sl, :]))
            dom_ref[:, sl] = do
            dwm = dw if dwm is None else dwm + dw
            _, vjp = jax.vjp(_dn_out_fn, od_ref[:, sl], z_ref[:, sl], wd_ref[...])
            do, dz, dw = vjp(_mm_nt(dh_ref[...], wo_ref[DN_WIDTH + h * HEAD:DN_WIDTH + (h + 1) * HEAD, :]))
            dod_ref[:, sl] = do
            dz_ref[:, sl] = dz.astype(BF16)
            dwd = dw if dwd is None else dwd + dw
        _accumulate(dwm_ref, dwm)
        _accumulate(dwd_ref, dwd)

    return _rows_call("mix_out_bwd", body, [o_mla, o_dn, z, dh1], [w_out, w_mla, w_dn],
                      [(DN_WIDTH, F32), (DN_WIDTH, F32), (DN_WIDTH, BF16)], [(1, HEAD), (1, HEAD)],
                      _row_tile(o_mla.shape[0], 4), host=host)


def _shift_down(x, s):
    if s == 0:
        return x
    rows = lax.broadcasted_iota(jnp.int32, x.shape, 0)
    return jnp.where(rows >= s, pltpu.roll(x, s, 0), 0.0)


def _shift_up(x, s):
    if s == 0:
        return x
    t = x.shape[0]
    rows = lax.broadcasted_iota(jnp.int32, x.shape, 0)
    return jnp.where(rows < t - s, pltpu.roll(x, t - s, 0), 0.0)


def _col_call(name, body, cols, taps, outs, tap_outs, cw, host=None):
    t, c = cols[0].shape[0], taps[0].shape[1]
    in_specs = [pl.BlockSpec((t, cw), lambda j: (0, j)) for _ in cols]
    in_specs += [pl.BlockSpec((a.shape[0], cw), lambda j: (0, j)) for a in taps]
    out_shape = [jax.ShapeDtypeStruct((t, c), dt) for dt in outs] + [jax.ShapeDtypeStruct((n, c), F32) for n in tap_outs]
    out_specs = [pl.BlockSpec((t, cw), lambda j: (0, j)) for _ in outs]
    out_specs += [pl.BlockSpec((n, cw), lambda j: (0, j)) for n in tap_outs]
    return _pcall(body, name, (c // cw,), in_specs, out_specs, out_shape, [*cols, *taps], ("arbitrary",), host=host)


def _causal_conv(x, w_ref, width, zero_tail=False):
    down = (lambda a, s: pltpu.roll(a, s, 0)) if zero_tail else _shift_down
    acc = w_ref[width - 1:width, :] * x
    for j in range(width - 1):
        acc = acc + w_ref[j:j + 1, :] * down(x, width - 1 - j)
    return acc


def _causal_conv_bwd(x, dpre, w_ref, dx_ref, dw_ref, width, zero_tail=False):
    t = x.shape[0]
    down = (lambda a, s: pltpu.roll(a, s, 0)) if zero_tail else _shift_down
    up = (lambda a, s: pltpu.roll(a, t - s, 0)) if zero_tail else _shift_up
    dx = w_ref[width - 1:width, :] * dpre
    dw_ref[width - 1:width, :] = jnp.sum(dpre * x, axis=0, keepdims=True)
    for j in range(width - 1):
        s = width - 1 - j
        dx = dx + w_ref[j:j + 1, :] * up(dpre, s)
        dw_ref[j:j + 1, :] = jnp.sum(dpre * down(x, s), axis=0, keepdims=True)
    dx_ref[...] = dx.astype(dx_ref.dtype)


def _dsilu(x):
    sg = jax.nn.sigmoid(x)
    return sg * (1.0 + x * (1.0 - sg))


def _dn_conv_fwd(x, w):
    def body(x_ref, w_ref, y_ref):
        y_ref[...] = _silu(_causal_conv(x_ref[...], w_ref, 4, zero_tail=True))

    return _col_call("dn_conv_fwd", body, [x], [w], [F32], [], 256)[0]


def _dn_conv_bwd(x, w, dy):
    def body(x_ref, dy_ref, w_ref, dx_ref, dw_ref):
        xv = x_ref[...]
        dpre = dy_ref[...] * _dsilu(_causal_conv(xv, w_ref, 4, zero_tail=True))
        _causal_conv_bwd(xv, dpre, w_ref, dx_ref, dw_ref, 4, zero_tail=True)

    return _col_call("dn_conv_bwd", body, [x, dy], [w], [BF16], [4], 256)


def _ffn_glu_fwd(n2, w_gate_t, w_up_t, w, b, host=None):
    t, k = n2.shape
    c, cw = w_gate_t.shape[0], 256

    def body(n_ref, wg_ref, wu_ref, w_ref, b_ref, g_ref, u_ref, a_ref):
        nv = n_ref[...]
        g16 = _mm_nt(nv, wg_ref[...]).astype(BF16)
        u16 = _mm_nt(nv, wu_ref[...]).astype(BF16)
        g_ref[...] = g16
        u_ref[...] = u16
        gate = _causal_conv(g16.astype(F32), w_ref, 3) + b_ref[...]
        a_ref[...] = (_silu(gate) * u16.astype(F32)).astype(BF16)

    wspec = pl.BlockSpec((cw, k), lambda j: (j, 0))
    col = pl.BlockSpec((t, cw), lambda j: (0, j))
    in_specs = [pl.BlockSpec((t, k), lambda j: (0, 0)), wspec, wspec, pl.BlockSpec((w.shape[0], cw), lambda j: (0, j)),
                pl.BlockSpec((1, cw), lambda j: (0, j))]
    return _pcall(body, "ffn_glu_fwd", (c // cw,), in_specs, [col] * 3, [jax.ShapeDtypeStruct((t, c), BF16)] * 3,
                  [n2, w_gate_t, w_up_t, w, b], ("arbitrary",), host=host)


def _ffn_glu_bwd(gpre, up, dy16, w_down, w, b):
    t, k = dy16.shape
    c, cw = w_down.shape[0], 256

    def body(g_ref, u_ref, dy_ref, wd_ref, w_ref, b_ref, dg_ref, du_ref, dw_ref, db_ref):
        gv = g_ref[...].astype(F32)
        gate = _causal_conv(gv, w_ref, 3) + b_ref[...]
        da = _mm_nt(dy_ref[...], wd_ref[...])
        sg = jax.nn.sigmoid(gate)
        du_ref[...] = (da * (gate * sg)).astype(BF16)
        dgate = da * u_ref[...].astype(F32) * (sg * (1.0 + gate * (1.0 - sg)))
        db_ref[...] = jnp.sum(dgate, axis=0, keepdims=True)
        _causal_conv_bwd(gv, dgate, w_ref, dg_ref, dw_ref, 3)

    col = pl.BlockSpec((t, cw), lambda j: (0, j))
    taps = lambda rows: pl.BlockSpec((rows, cw), lambda j: (0, j))
    in_specs = [col, col, pl.BlockSpec((t, k), lambda j: (0, 0)), pl.BlockSpec((cw, k), lambda j: (j, 0)),
                taps(w.shape[0]), taps(1)]
    return _pcall(body, "ffn_glu_bwd", (c // cw,), in_specs, [col, col, taps(w.shape[0]), taps(1)],
                  [jax.ShapeDtypeStruct((t, c), BF16)] * 2 + [jax.ShapeDtypeStruct((w.shape[0], c), F32),
                                                             jax.ShapeDtypeStruct((1, c), F32)],
                  [gpre, up, dy16, w_down, w, b], ("arbitrary",))


def _dn_prep_consts(sa, sb, al, dt):
    return dict(sel_a=sa[...], sel_b=sb[...], alog=al[...], dtb=dt[...])


def _dn_prep_fwd(conv, ab, sel_a, sel_b, alog, dtb):
    def body(c_ref, ab_ref, sa, sb, al, dt, q_out, k_out, g_out, b_out):
        qc = tuple(c_ref[:, h * HEAD:(h + 1) * HEAD] for h in range(HEADS))
        kc = tuple(c_ref[:, DN_WIDTH + h * HEAD:DN_WIDTH + (h + 1) * HEAD] for h in range(HEADS))
        qs, ks, g, beta = _dn_prep_fn((qc, kc, ab_ref[...]), _dn_prep_consts(sa, sb, al, dt))
        for h in range(HEADS):
            q_out[:, h * HEAD:(h + 1) * HEAD] = qs[h]
            k_out[:, h * HEAD:(h + 1) * HEAD] = ks[h]
        g_out[...] = g
        b_out[...] = beta

    return _rows_call("dn_prep_fwd", body, [conv, ab], [sel_a, sel_b, alog, dtb], [(DN_WIDTH, F32)] * 4, [],
                      _row_tile(conv.shape[0], 4))


def _dn_prep_bwd(conv, ab, dq, dk, dv, dg, db, sel_a, sel_b, alog, dtb):
    def body(c_ref, ab_ref, dq_r, dk_r, dv_r, dg_r, db_r, sa, sb, al, dt, dc_out, dab_out, dal_out, ddt_out):
        qc = tuple(c_ref[:, h * HEAD:(h + 1) * HEAD] for h in range(HEADS))
        kc = tuple(c_ref[:, DN_WIDTH + h * HEAD:DN_WIDTH + (h + 1) * HEAD] for h in range(HEADS))
        consts = _dn_prep_consts(sa, sb, al, dt)
        sel = dict(sel_a=consts["sel_a"], sel_b=consts["sel_b"])
        _, vjp = jax.vjp(lambda rows, ad: _dn_prep_fn(rows, {**sel, **ad}), (qc, kc, ab_ref[...]),
                         dict(alog=consts["alog"], dtb=consts["dtb"]))
        cq = tuple(dq_r[:, h * HEAD:(h + 1) * HEAD] for h in range(HEADS))
        ck = tuple(dk_r[:, h * HEAD:(h + 1) * HEAD] for h in range(HEADS))
        (dqc, dkc, dab), dad = vjp((cq, ck, dg_r[...], db_r[...]))
        for h in range(HEADS):
            dc_out[:, h * HEAD:(h + 1) * HEAD] = dqc[h]
            dc_out[:, DN_WIDTH + h * HEAD:DN_WIDTH + (h + 1) * HEAD] = dkc[h]
        dc_out[:, 2 * DN_WIDTH:3 * DN_WIDTH] = dv_r[...]
        dab_out[...] = dab.astype(BF16)
        _accumulate(dal_out, dad["alog"])
        _accumulate(ddt_out, dad["dtb"])

    return _rows_call("dn_prep_bwd", body, [conv, ab, dq, dk, dv, dg, db], [sel_a, sel_b, alog, dtb],
                      [(3 * DN_WIDTH, F32), (HEAD, BF16)], [(1, DN_WIDTH), (1, DN_WIDTH)], _row_tile(conv.shape[0], 4))


def _chunk_batch(t):
    nc = t // CHUNK
    return nc // 2 if nc % 2 == 0 else nc


def _dn_chunk_specs(t, nb):
    rows = nb * CHUNK
    blk = pl.BlockSpec((rows, HEAD), lambda h, b: (b, h))
    vblk = pl.BlockSpec((rows, HEAD), lambda h, b: (b, 2 * HEADS + h))
    mat = pl.BlockSpec((nb, HEAD, HEAD), lambda h, b: (b, h, 0))
    return rows, blk, vblk, mat


def _dn_chunk_fwd(qn, kn, conv, g, beta, host=None):
    t = qn.shape[0]
    nb = _chunk_batch(t)
    rows, blk, vblk, mat = _dn_chunk_specs(t, nb)

    def body(q_ref, k_ref, v_ref, g_ref, b_ref, n_o, b_o, qe_o, oo_o, eg_o):
        r3 = lambda x: x.reshape(nb, CHUNK, x.shape[-1])
        n_mat, b_mat, q_eff, o_own, eg = _dn_chunk_fn(r3(q_ref[...]), r3(k_ref[...]), r3(v_ref[...]), r3(g_ref[...]),
                                                      r3(g_ref[:, 0:CHUNK]), r3(b_ref[...]))
        n_o[...] = n_mat
        b_o[...] = b_mat
        qe_o[...] = q_eff.reshape(rows, HEAD)
        oo_o[...] = o_own.reshape(rows, HEAD)
        eg_o[...] = jnp.broadcast_to(eg, (nb, HEAD, HEAD))

    nc = t // CHUNK
    mats = jax.ShapeDtypeStruct((nc, DN_WIDTH, HEAD), F32)
    rowsd = jax.ShapeDtypeStruct((t, DN_WIDTH), F32)
    return _pcall(body, "dn_chunk_fwd", (HEADS, t // rows), [blk, blk, vblk, blk, blk], [mat, mat, blk, blk, mat],
                  [mats, mats, rowsd, rowsd, mats], [qn, kn, conv, g, beta], ("arbitrary", "arbitrary"), host=host)


def _dn_chunk_bwd(qn, kn, conv, g, beta, sall, gall, dq_eff, do, host=None):
    t = qn.shape[0]
    nb = _chunk_batch(t)
    rows, blk, vblk, mat = _dn_chunk_specs(t, nb)

    def body(q_ref, k_ref, v_ref, g_ref, b_ref, s_ref, ga_ref, dqe_ref, do_ref, dq_o, dk_o, dv_o, dg_o, db_o):
        r3 = lambda x: x.reshape(nb, CHUNK, x.shape[-1])
        _, vjp = jax.vjp(_dn_chunk_fn, r3(q_ref[...]), r3(k_ref[...]), r3(v_ref[...]), r3(g_ref[...]),
                         r3(g_ref[:, 0:CHUNK]), r3(b_ref[...]))
        s, ga = s_ref[...], ga_ref[...]
        d_n = -_bmm_nt(ga, s)
        d_eg = jnp.sum(ga * s, axis=1, keepdims=True)
        dq, dk, dv, dg, dg64, db = vjp((d_n, ga, r3(dqe_ref[...]), r3(do_ref[...]), d_eg))
        for o_ref, val in zip((dq_o, dk_o, dv_o, dg_o, db_o), (dq, dk, dv, dg, db)):
            o_ref[...] = val.reshape(rows, HEAD)
        dg_o[:, 0:CHUNK] += dg64.reshape(rows, CHUNK)

    return _pcall(body, "dn_chunk_bwd", (HEADS, t // rows), [blk, blk, vblk, blk, blk, mat, mat, blk, blk], [blk] * 5,
                  [jax.ShapeDtypeStruct((t, DN_WIDTH), F32)] * 5, [qn, kn, conv, g, beta, sall, gall, dq_eff, do],
                  ("arbitrary", "arbitrary"), host=host)


def _dn_rec_fwd(n_mat, b_mat, eg, host=None):
    nc = n_mat.shape[0]
    nb = _chunk_batch(nc * CHUNK)
    spec = pl.BlockSpec((nb, DN_WIDTH, HEAD), lambda i: (i, 0, 0))

    def body(n_ref, b_ref, eg_ref, sall_ref, s_scr):
        @pl.when(pl.program_id(0) == 0)
        def _():
            s_scr[...] = jnp.zeros(s_scr.shape, F32)

        for j in range(nb):
            sall_ref[j] = s_scr[...]
            for h in range(HEADS):
                sl = slice(h * HEAD, (h + 1) * HEAD)
                s_scr[sl, :] = _dn_rec_fn(s_scr[sl, :], n_ref[j, sl, :], b_ref[j, sl, :],
                                          eg_ref[j, h * HEAD:h * HEAD + 1, :])

    return _pcall(body, "dn_rec_fwd", (nc // nb,), [spec] * 3, spec, jax.ShapeDtypeStruct((nc, DN_WIDTH, HEAD), F32),
                  [n_mat, b_mat, eg], ("arbitrary",), scratch_shapes=[pltpu.VMEM((DN_WIDTH, HEAD), F32)], host=host)


def _dn_rec_bwd(n_mat, eg, ds_out, host=None):
    nc = n_mat.shape[0]
    nb = _chunk_batch(nc * CHUNK)
    steps = nc // nb
    spec = pl.BlockSpec((nb, DN_WIDTH, HEAD), lambda i: (steps - 1 - i, 0, 0))

    def body(n_ref, eg_ref, dso_ref, gall_ref, g_scr):
        @pl.when(pl.program_id(0) == 0)
        def _():
            g_scr[...] = jnp.zeros(g_scr.shape, F32)

        for j in reversed(range(nb)):
            gall_ref[j] = g_scr[...]
            for h in range(HEADS):
                sl = slice(h * HEAD, (h + 1) * HEAD)
                gv = g_scr[sl, :]
                g_scr[sl, :] = (gv * eg_ref[j, h * HEAD:h * HEAD + 1, :] - _mm_tn(n_ref[j, sl, :], gv)
                                + dso_ref[j, sl, :])

    return _pcall(body, "dn_rec_bwd", (steps,), [spec] * 3, spec, jax.ShapeDtypeStruct((nc, DN_WIDTH, HEAD), F32),
                  [n_mat, eg, ds_out], ("arbitrary",), scratch_shapes=[pltpu.VMEM((DN_WIDTH, HEAD), F32)], host=host)


def _dn_o_fwd(sall, q_eff, o_own):
    t = q_eff.shape[0]
    nb = _chunk_batch(t)
    rows, blk, _, mat = _dn_chunk_specs(t, nb)

    def body(s_ref, qe_ref, oo_ref, o_ref):
        r3 = lambda x: x.reshape(nb, CHUNK, HEAD)
        o_ref[...] = _dn_o_fn(s_ref[...], r3(qe_ref[...]), r3(oo_ref[...])).reshape(rows, HEAD)

    return _pcall(body, "dn_o_fwd", (HEADS, t // rows), [mat, blk, blk], blk, jax.ShapeDtypeStruct((t, DN_WIDTH), F32),
                  [sall, q_eff, o_own], ("arbitrary", "arbitrary"))


def _dn_o_bwd(sall, q_eff, do, host=None):
    t = q_eff.shape[0]
    nb = _chunk_batch(t)
    rows, blk, _, mat = _dn_chunk_specs(t, nb)

    def body(s_ref, qe_ref, do_ref, dqe_ref, ds_ref):
        r3 = lambda x: x.reshape(nb, CHUNK, HEAD)
        dov = r3(do_ref[...])
        dqe_ref[...] = _bmm_nt(dov, s_ref[...]).reshape(rows, HEAD)
        ds_ref[...] = _bmm_tn(r3(qe_ref[...]), dov)

    nc = t // CHUNK
    return _pcall(body, "dn_o_bwd", (HEADS, t // rows), [mat, blk, blk], [blk, mat],
                  [jax.ShapeDtypeStruct((t, DN_WIDTH), F32), jax.ShapeDtypeStruct((nc, DN_WIDTH, HEAD), F32)],
                  [sall, q_eff, do], ("arbitrary", "arbitrary"), host=host)


def _adamw_update(w, g, m, v):
    m2 = ADAM_B1 * m + (1.0 - ADAM_B1) * g
    v2 = ADAM_B2 * v + (1.0 - ADAM_B2) * (g * g)
    m_hat = m2 / (1.0 - ADAM_B1 ** ADAM_STEP)
    v_hat = v2 / (1.0 - ADAM_B2 ** ADAM_STEP)
    return -ADAM_LR * (m_hat / (jnp.sqrt(v_hat) + ADAM_EPS) + ADAM_WD * w), m2, v2


def _adamw_small(ws, gs, ms, vs):
    n = len(ws)

    def body(*refs):
        for i in range(n):
            d, m2, v2 = _adamw_update(refs[i][...], refs[n + i][...], refs[2 * n + i][...], refs[3 * n + i][...])
            refs[4 * n + i][...] = d
            refs[5 * n + i][...] = m2
            refs[6 * n + i][...] = v2

    shapes = [jax.ShapeDtypeStruct(a.shape, F32) for a in ws]
    outs = pl.pallas_call(body, name="adamw_small", out_shape=shapes * 3,
                          compiler_params=pltpu.CompilerParams(vmem_limit_bytes=VMEM_LIMIT))(*ws, *gs, *ms, *vs)
    return outs[:n], outs[n:2 * n], outs[2 * n:]


def _adamw_call(name, w, g, m, v, host=None):
    rows, cols = w.shape
    by_rows = rows % 8 == 0

    def body(w_ref, g_ref, m_ref, v_ref, g_out, d_ref, m_out, v_out):
        gv = g_ref[...] if by_rows else g_ref[0:rows, :]
        g_out[...] = gv
        d_ref[...], m_out[...], v_out[...] = _adamw_update(w_ref[...], gv, m_ref[...], v_ref[...])

    if by_rows:
        tr = _tile(rows, 256, 8)
        spec = g_spec = pl.BlockSpec((tr, cols), lambda i: (i, 0))
        grid = (rows // tr,)
    else:
        tc = _tile(cols, 256, 128)
        spec = pl.BlockSpec((rows, tc), lambda j: (0, j))
        g_spec = pl.BlockSpec((g.shape[0], tc), lambda j: (0, j))
        grid = (cols // tc,)
    return _pcall(body, name, grid, [spec, g_spec, spec, spec], [spec] * 4, [jax.ShapeDtypeStruct((rows, cols), F32)] * 4,
                  [w, g, m, v], ("arbitrary",), host=host)


def _adamw_rows3d(name, w, g, m, v):
    rows, _, cols = w.shape
    tr = max(d for d in range(1, 129) if rows % d == 0)

    def body(w_ref, g_ref, m_ref, v_ref, d_ref, m_out, v_out):
        d_ref[...], m_out[...], v_out[...] = _adamw_update(w_ref[...], g_ref[...], m_ref[...], v_ref[...])

    spec = pl.BlockSpec((tr, 1, cols), lambda i: (i, 0, 0))
    return pl.pallas_call(body, name=name, grid=(rows // tr,), in_specs=[spec] * 4, out_specs=[spec] * 3,
                          out_shape=[jax.ShapeDtypeStruct(w.shape, F32)] * 3, compiler_params=_cparams(("arbitrary",)))(
                              w, g, m, v)


def _rope_tables(t):
    half = ROPE // 2
    inv_freq = np.float32(ROPE_THETA) ** (-np.arange(half, dtype=np.float32) / np.float32(half))
    ang = np.arange(t, dtype=np.float32)[:, None] * inv_freq[None, :].astype(np.float32)
    z = np.zeros((t, HEAD - ROPE), np.float32)
    cos = np.concatenate([np.cos(ang), np.cos(ang), z], axis=1).astype(np.float32)
    sin = np.concatenate([np.sin(ang), np.sin(ang), z], axis=1).astype(np.float32)
    k = np.arange(HEAD)[:, None]
    l = np.arange(HEAD)[None, :]
    perm = np.where((l < half) & (k == l + half), -1.0, 0.0) + np.where((l >= half) & (l < ROPE) & (k == l - half), 1.0, 0.0)
    return jnp.asarray(cos), jnp.asarray(sin), jnp.asarray(perm.astype(np.float32))


def _win_to_pad(w):
    z = lambda n: jnp.zeros((n, w.shape[1]), w.dtype)
    return jnp.concatenate([w[576:2112], w[2112:2624], w[0:256], w[256:512], w[512:576], z(64), w[2624:2632], z(120)],
                           axis=0)


def _qk_to_pad(w):
    w4 = w.reshape(HEADS, QK_DIM, w.shape[-1])
    return jnp.concatenate([w4, jnp.zeros((HEADS, QK_PAD - QK_DIM, w.shape[-1]), w.dtype)], axis=1).reshape(
        HEADS * QK_PAD, w.shape[-1])


def _qk_from_pad(g):
    return g.reshape(HEADS, QK_PAD, g.shape[-1])[:, :QK_DIM].reshape(HEADS * QK_DIM, g.shape[-1])


def _ff_to_pad(a, axis):
    shape = list(a.shape)
    shape[axis:axis + 1] = [N_CHIPS, FF_SHARD]
    a4 = a.reshape(shape)
    shape[axis + 1] = FF_BLOCK - FF_SHARD
    out = jnp.concatenate([a4, jnp.zeros(shape, a.dtype)], axis=axis + 1)
    shape[axis:axis + 2] = [D_FF_P]
    return out.reshape(shape)


def _ff_from_pad(a, axis):
    shape = list(a.shape)
    shape[axis:axis + 1] = [N_CHIPS, FF_BLOCK]
    a4 = lax.slice_in_dim(a.reshape(shape), 0, FF_SHARD, axis=axis + 1)
    shape[axis:axis + 2] = [D_FF]
    return a4.reshape(shape)


class _LocalPlan:
    def __init__(self, wt):
        self.wt, self.grads = wt, {}

    def weight(self, name):
        return self.wt[name]

    def host(self, point):
        return None

    def grad(self, name, value):
        self.grads[name] = value


def _local_step(x, tgt, wt, plan=None):
    plan = _LocalPlan(wt) if plan is None else plan
    s = x.shape[0]
    n_valid = N_META + s
    t = -(-n_valid // HEAD) * HEAD
    assert t - n_valid >= 3, "the DeltaNet conv kernels rely on at least three zero rows after the sequence"
    zpad = jnp.zeros((t - n_valid, D_MODEL), F32)
    h0 = jnp.concatenate([wt["meta_tokens"], x, zpad], axis=0)
    tgt_p = jnp.concatenate([jnp.zeros((N_META, D_MODEL), F32), tgt, zpad], axis=0)
    cos, sin, perm = _rope_tables(t)
    qn_w = jnp.concatenate([wt["q_norm_w"], jnp.zeros((1, QK_PAD - QK_DIM), F32)], axis=1)
    kn_w = jnp.concatenate([wt["k_norm_w"], jnp.zeros((1, QK_PAD - QK_DIM), F32)], axis=1)
    head_id = jnp.arange(DN_WIDTH)[None, :] // HEAD
    lane = jnp.arange(HEAD)[:, None]
    sel_a = (lane == head_id).astype(F32)
    sel_b = (lane == head_id + HEADS).astype(F32)
    alog = jnp.repeat(wt["dn_A_log"], HEAD, axis=1)
    dtb = jnp.repeat(wt["dn_dt_bias"], HEAD, axis=1)
    conv_w, conv_b = wt["ffn_conv_w"], wt["ffn_conv_b"]

    u = _rms_fwd("attn_norm_fwd", h0, wt["attn_norm_w"], host=plan.host("attn_norm_fwd"))
    win, wq, wkv = plan.weight("w_in_t"), plan.weight("w_q_t"), plan.weight("w_kv_t")
    proj = _matmul("in_proj", u, win, "nt", F32)
    z = (proj, DN_WIDTH, 3)
    q_lat, kv_lat, k_pe, ab = (proj, LORA, 8), (proj, LORA, 9), (proj, HEAD, 20), (proj, HEAD, 21)
    mla_consts = (wt["q_a_norm_w"], wq, wt["kv_a_norm_w"], wkv, qn_w, kn_w, perm)
    q, k, v = _mla_prep_fwd(q_lat, kv_lat, k_pe, cos, sin, *mla_consts)
    o_mla = _attn_fwd(q, k, v, host=plan.host("attn_fwd"))
    conv = _dn_conv_fwd(proj, wt["dn_conv_w"])
    dn_consts = (sel_a, sel_b, alog, dtb)
    qn, kn, g, beta = _dn_prep_fwd(conv, ab, *dn_consts)
    n_mat, b_mat, q_eff, o_own, eg = _dn_chunk_fwd(qn, kn, conv, g, beta, host=plan.host("dn_chunk_fwd"))
    sall = _dn_rec_fwd(n_mat, b_mat, eg)
    o_dn = _dn_o_fwd(sall, q_eff, o_own)
    w_out = plan.weight("w_out")
    mixed, h1, n2 = _mix_out_proj(o_mla, o_dn, z, wt["mla_out_norm_w"], wt["dn_out_norm_w"], w_out, h0, wt["ffn_norm_w"])
    w_gate, w_up = plan.weight("w_gate_t"), plan.weight("w_up_t")
    gpre, up, act = _ffn_glu_fwd(n2, w_gate, w_up, conv_w, conv_b, host=plan.host("ffn_glu_fwd"))
    w_down = plan.weight("w_down")
    dy, dy16, sq = _down_proj_loss(act, w_down, h1, tgt_p, n_valid)

    grads = {}
    plan.grad("w_down", _matmul("down_dw", act, dy16, "tn", BF16))
    dgpre, dup, grads["ffn_conv_w"], grads["ffn_conv_b"] = _ffn_glu_bwd(gpre, up, dy16, w_down, conv_w, conv_b)
    plan.grad("w_gate_t", _matmul("gate_dw", dgpre, n2, "tn", BF16))
    plan.grad("w_up_t", _matmul("up_dw", dup, n2, "tn", BF16))
    dh1, dh1_16, grads["ffn_norm_w"] = _ffn_in_bwd(dgpre, dup, w_gate, w_up, h1, dy, wt["ffn_norm_w"],
                                                   host=plan.host("ffn_in_bwd"))
    plan.grad("w_out", _matmul("out_dw", mixed, dh1_16, "tn", BF16))
    do_mla, do_dn, dz, grads["mla_out_norm_w"], grads["dn_out_norm_w"] = _mix_out_bwd(
        o_mla, o_dn, z, dh1_16, w_out, wt["mla_out_norm_w"], wt["dn_out_norm_w"], host=plan.host("mix_out_bwd"))
    dq_eff, ds_out = _dn_o_bwd(sall, q_eff, do_dn)
    gall = _dn_rec_bwd(n_mat, eg, ds_out)
    dqn, dkn, dv_dn, dg, dbeta = _dn_chunk_bwd(qn, kn, conv, g, beta, sall, gall, dq_eff, do_dn,
                                               host=plan.host("dn_chunk_bwd"))
    dconv, dab, dalog, ddtb = _dn_prep_bwd(conv, ab, dqn, dkn, dv_dn, dg, dbeta, *dn_consts)
    grads["dn_A_log"] = jnp.sum(dalog.reshape(HEADS, HEAD), axis=1)[None, :]
    grads["dn_dt_bias"] = jnp.sum(ddtb.reshape(HEADS, HEAD), axis=1)[None, :]
    ddn_pre, grads["dn_conv_w"] = _dn_conv_bwd(proj, wt["dn_conv_w"], dconv)
    dq, dk, dv = _attn_bwd(q, k, v, do_mla, host=plan.host("attn_bwd"))
    dq_lat, dkv_lat, dk_pe, dqa, dwq, dkva, dwkv, dqnw, dknw = _mla_prep_bwd(
        q_lat, kv_lat, k_pe, cos, sin, dq, dk, dv, *mla_consts, host=plan.host("mla_prep_bwd"))
    grads["q_a_norm_w"], grads["kv_a_norm_w"] = dqa, dkva
    plan.grad("w_q_t", dwq)
    plan.grad("w_kv_t", dwkv)
    grads["q_norm_w"], grads["k_norm_w"] = dqnw[:, :QK_DIM], dknw[:, :QK_DIM]
    dproj = [ddn_pre, dz, dq_lat, dkv_lat, dk_pe, dab]
    plan.grad("w_in_t", _in_proj_bwd_w(dproj, u))
    du = _in_proj_bwd_x(dproj, win, host=plan.host("in_dx"))
    dh0, _, grads["attn_norm_w"] = _rms_bwd("attn_norm_bwd", h0, wt["attn_norm_w"], [du], dh1,
                                            host=plan.host("attn_norm_bwd"))
    grads["meta_tokens"] = dh0[0:N_META]
    if isinstance(plan, _LocalPlan):
        grads.update(plan.grads)
    return sq, dh0[N_META:n_valid], grads


def _mesh_pos():
    return lax.axis_index("x"), lax.axis_index("y"), lax.axis_index("c")


def _other_chips(x, y):
    return [(1 - x, y), (x, 1 - y), (1 - x, 1 - y)]


def _remote(src, dst, send_sems, recv_sems, k, to):
    return pltpu.make_async_remote_copy(src_ref=src, dst_ref=dst, send_sem=send_sems.at[k], recv_sem=recv_sems.at[k],
                                        device_id=to, device_id_type=MESH)


SIBLING_ID, CHIPS_ID, GATHER_ID, ALL_ID = 1, 2, 3, 4


def _sibling_peer():
    x, y, c = _mesh_pos()
    return [(x, y, 1 - c)]


def _chip_peers():
    x, y, c = _mesh_pos()
    return [(qx, qy, c) for qx, qy in _other_chips(x, y)]


def _copies_exchange(make, ins, out_shape, nsem, peers=None, cid=None):
    def prog(in_refs, out_refs, send_sems, recv_sems):
        copies = make(in_refs, out_refs, send_sems, recv_sems)

        def start():
            for cp in copies:
                cp.start()

        def finish():
            for cp in copies:
                cp.wait()

        return start, finish

    return _Exchange(prog, ins, out_shape, nsem, peers, cid)


def _all_gather(shards):
    def prog(srcs, dsts, send_sems, recv_sems):
        x, y, c = _mesh_pos()
        p = 2 * x + y
        sibling = (x, y, 1 - c)
        chips = _other_chips(x, y)
        bufs = tuple((s, d, s.shape[0] // 2) for s, d in zip(srcs, dsts))

        def half(ref, rows, which):
            return ref.at[pl.ds(which * rows, rows), :]

        def copy(i, k, src, dst, to):
            return _remote(src, dst, send_sems, recv_sems, 6 * i + k, to)

        sends = [copy(i, j, half(src, rows, c), half(dst.at[p], rows, c), (*chip, c))
                 for i, (src, dst, rows) in enumerate(bufs) for j, chip in enumerate(chips)]

        def start():
            for cp in sends:
                cp.start()

        def finish():
            passed = []
            for i, (src, dst, rows) in enumerate(bufs):
                for j, (qx, qy) in enumerate(chips):
                    block = half(dst.at[2 * qx + qy], rows, c)
                    copy(i, j, block, block, (x, y, c)).wait_recv()
                    fwd = copy(i, 3 + j, block, block, sibling)
                    fwd.start()
                    passed.append(fwd)
            for i, (src, dst, rows) in enumerate(bufs):
                for j, (qx, qy) in enumerate(chips):
                    block = half(dst.at[2 * qx + qy], rows, 1 - c)
                    copy(i, 3 + j, block, block, (x, y, c)).wait_recv()
            for cp in sends + passed:
                cp.wait_send()

        return start, finish

    return _Exchange(prog, shards, [jax.ShapeDtypeStruct((N_CHIPS, *s.shape), s.dtype) for s in shards], 6 * len(shards),
                     lambda: _sibling_peer() + _chip_peers(), GATHER_ID)


def _all_gather_small(block):
    def make(srcs, dsts, send_sems, recv_sems):
        x, y, c = _mesh_pos()
        return [_remote(srcs[0], dsts[0].at[2 * x + y], send_sems, recv_sems, k, (qx, qy, c))
                for k, (qx, qy) in enumerate(_other_chips(x, y))]

    return _copies_exchange(make, [block], [jax.ShapeDtypeStruct((N_CHIPS, *block.shape), block.dtype)], 3, _chip_peers,
                            CHIPS_ID)


def _gathered(ex):
    p = 2 * lax.axis_index("x") + lax.axis_index("y")
    return [lax.dynamic_update_slice(g, s[None], (p, 0, 0)) for g, s in zip(ex.outs, ex.ins)]


def _rs_to_sibling(bufs):
    def make(srcs, dsts, send_sems, recv_sems):
        x, y, c = _mesh_pos()
        copies = []
        for i, (src, dst) in enumerate(zip(srcs, dsts)):
            half = src.shape[1] // 2
            copies.append(_remote(src.at[:, pl.ds((1 - c) * half, half), :], dst, send_sems, recv_sems, i, (x, y, 1 - c)))
        return copies

    return _copies_exchange(make, bufs,
                            [jax.ShapeDtypeStruct((N_CHIPS, b.shape[1] // 2, b.shape[2]), b.dtype) for b in bufs],
                            len(bufs), _sibling_peer, SIBLING_ID)


def _rs_pair_add(name, bufs, gots, c, out_dtype):
    n = len(bufs)

    def body(c_ref, *refs):
        for a_ref, b_ref, o_ref in zip(refs[:n], refs[n:2 * n], refs[2 * n:]):
            o_ref[...] = (a_ref[...].astype(F32) + b_ref[...].astype(F32)).astype(out_dtype)

    mine = [pl.BlockSpec((None, g.shape[1], g.shape[2]), lambda j, cr: (j, cr[0], 0)) for g in gots]
    whole = [pl.BlockSpec((None, g.shape[1], g.shape[2]), lambda j, cr: (j, 0, 0)) for g in gots]
    return pl.pallas_call(
        body, name=name,
        grid_spec=pltpu.PrefetchScalarGridSpec(num_scalar_prefetch=1, grid=(N_CHIPS,), in_specs=mine + whole, out_specs=whole),
        out_shape=[jax.ShapeDtypeStruct(g.shape, out_dtype) for g in gots],
        compiler_params=_cparams(("arbitrary",)))(c, *bufs, *gots)


def _rs_to_chips(accs):
    def make(srcs, dsts, send_sems, recv_sems):
        x, y, c = _mesh_pos()
        return [_remote(src.at[2 * qx + qy], dst.at[k], send_sems, recv_sems, 3 * i + k, (qx, qy, c))
                for i, (src, dst) in enumerate(zip(srcs, dsts)) for k, (qx, qy) in enumerate(_other_chips(x, y))]

    return _copies_exchange(make, accs, [jax.ShapeDtypeStruct((3, a.shape[1], a.shape[2]), a.dtype) for a in accs],
                            3 * len(accs), _chip_peers, CHIPS_ID)


def _rs_chip_add(name, accs, gots, p):
    n = len(accs)
    slot = (0, 1, 0, 2)

    def body(p_ref, *refs):
        me = p_ref[0]
        for own_ref, got_ref, o_ref in zip(refs[:n], refs[n:2 * n], refs[2 * n:]):
            total = None
            for chip in range(N_CHIPS):
                val = own_ref[...].astype(F32)
                for e in (1, 2, 3):
                    val = jnp.where((chip ^ me) == e, got_ref[slot[e]].astype(F32), val)
                total = val if total is None else total + val
            o_ref[...] = total

    own = [pl.BlockSpec((None, a.shape[1], a.shape[2]), lambda i, pr: (pr[0], 0, 0)) for a in accs]
    got = [pl.BlockSpec(g.shape, lambda i, pr: (0, 0, 0)) for g in gots]
    out = [pl.BlockSpec((a.shape[1], a.shape[2]), lambda i, pr: (0, 0)) for a in accs]
    return pl.pallas_call(
        body, name=name,
        grid_spec=pltpu.PrefetchScalarGridSpec(num_scalar_prefetch=1, grid=(1,), in_specs=own + got, out_specs=out),
        out_shape=[jax.ShapeDtypeStruct((a.shape[1], a.shape[2]), F32) for a in accs],
        compiler_params=_cparams(("arbitrary",)))(p, *accs, *gots)


def _rs_share(ress):
    def make(srcs, dsts, send_sems, recv_sems):
        x, y, c = _mesh_pos()
        rows = lambda src: pl.ds(pl.multiple_of(c * src.shape[0], 8), src.shape[0])
        return [_remote(src, dst.at[rows(src)], send_sems, recv_sems, i, (x, y, 1 - c))
                for i, (src, dst) in enumerate(zip(srcs, dsts))]

    return _copies_exchange(make, ress, [jax.ShapeDtypeStruct((2 * r.shape[0], r.shape[1]), F32) for r in ress], len(ress),
                            _sibling_peer, SIBLING_ID)


def _shared(ex):
    c = lax.axis_index("c")
    return [lax.dynamic_update_slice(g, r, (c * r.shape[0], 0)) for r, g in zip(ex.ins, ex.outs)]


def _all_to_all_devices(vec):
    def others():
        x, y, c = _mesh_pos()
        return [((1 - x if r & 4 else x), (1 - y if r & 2 else y), (1 - c if r & 1 else c)) for r in range(1, 8)]

    def make(srcs, dsts, send_sems, recv_sems):
        x, y, c = _mesh_pos()
        me = 4 * x + 2 * y + c
        return [_remote(srcs[0], dsts[0].at[me], send_sems, recv_sems, r, peer) for r, peer in enumerate(others())]

    return _copies_exchange(make, [vec], [jax.ShapeDtypeStruct((8, *vec.shape), vec.dtype)], 7, others, ALL_ID)


def _sum_devices(stack):
    def body(s_ref, o_ref):
        total = s_ref[0]
        for d in range(1, 8):
            total = total + s_ref[d]
        o_ref[...] = total

    return pl.pallas_call(body, name="sum_devices", out_shape=jax.ShapeDtypeStruct(stack.shape[1:], F32),
                          compiler_params=pltpu.CompilerParams(vmem_limit_bytes=VMEM_LIMIT))(stack)


def _pad_rows(flat, rows):
    return jnp.concatenate([flat, jnp.zeros((rows * LANES - flat.shape[0],), flat.dtype)]).reshape(rows, LANES)


def _unshard(g4, shape, axis):
    a = g4.reshape(N_CHIPS, *shape)
    if axis == 0:
        return a.reshape(N_CHIPS * shape[0], shape[1])
    return jnp.transpose(a, (1, 0, 2)).reshape(shape[0], N_CHIPS * shape[1])


def _pad_axis0(a, rows):
    return jnp.concatenate([a, jnp.zeros((rows - a.shape[0], *a.shape[1:]), a.dtype)], axis=0)


def _shard_to_strip(name, w):
    _, (shape, axis, rows) = name, {n: (s, ax, r) for n, s, ax, r in BIG}[name]
    w2 = w.reshape(shape).astype(BF16)
    return _pad_axis0(w2.T if axis == 1 else w2, rows)


LOCAL_NAME = dict(w_in="w_in_t", w_q_b="w_q_t", w_kv_b="w_kv_t", w_out="w_out", w_gate="w_gate_t", w_up="w_up_t",
                  w_down="w_down")


WIN_SEGMENTS = ((576, 2112, 0), (2112, 2624, 1536), (0, 256, 2048), (256, 512, 2304), (512, 576, 2560), (2624, 2632, 2688))


def _strips_to_weight(name, g4):
    if name == "w_in":
        return _win_to_pad(g4[:, :IN_SHARD].reshape(IN_COLS, D_MODEL))
    if name == "w_q_b":
        return _qk_to_pad(g4.reshape(HEADS * QK_DIM, LORA))
    return g4.reshape(N_CHIPS * g4.shape[1], g4.shape[2])


def _grad_to_strips(name, g):
    if name == "w_in":
        strips = []
        for q in range(N_CHIPS):
            pieces = []
            for a, b, local in sorted(WIN_SEGMENTS):
                s, e = max(a, q * IN_SHARD), min(b, (q + 1) * IN_SHARD)
                if s < e:
                    pieces.append(g[local + s - a:local + e - a])
            pieces.append(jnp.zeros((IN_SHARD_P - IN_SHARD, D_MODEL), g.dtype))
            strips.append(jnp.concatenate(pieces, axis=0))
        return jnp.stack(strips)
    if name == "w_q_b":
        return _qk_from_pad(g).reshape(N_CHIPS, QK_DIM, LORA)
    return g.reshape(N_CHIPS, g.shape[0] // N_CHIPS, g.shape[1])


class _MeshPlan:
    LATE = dict(attn_norm_fwd=("w_in", "w_q_b", "w_kv_b"), attn_fwd=("w_up",), dn_chunk_fwd=("w_out", "w_gate"),
                ffn_glu_fwd=("w_down",))
    GROUP_A = ("w_down", "w_gate", "w_up", "w_out")
    GROUP_B = ("w_in", "w_q_b", "w_kv_b")

    def __init__(self, w):
        x, y, c = _mesh_pos()
        self.ci = jnp.reshape(c, (1,)).astype(jnp.int32)
        self.pi = jnp.reshape(2 * x + y, (1,)).astype(jnp.int32)
        self.strip = {n: _shard_to_strip(n, w[n]) for n, _, _, _ in BIG}
        self.gathers, self.weights, self.g, self.acc, self.reduced = {}, {}, {}, {}, {}
        self.sibs, self.sib, self.chip, self.share = [], None, None, None

    def gather_small(self, small):
        ex = _all_gather_small(small)
        ex.run("all_gather_small")
        return _gathered(ex)[0]

    def weight(self, local_name):
        if local_name not in self.weights:
            for point, (names, ex) in list(self.gathers.items()):
                if ex.outs is not None:
                    for n, g4 in zip(names, _gathered(ex)):
                        self.weights[LOCAL_NAME[n]] = _strips_to_weight(n, g4)
                    del self.gathers[point]
        return self.weights[local_name]

    def grad(self, local_name, value):
        name = {v: k for k, v in LOCAL_NAME.items()}[local_name]
        self.g[name] = _grad_to_strips(name, value)

    def _pair_add(self, names, gots):
        accs = _rs_pair_add("rs_pair_add_" + names[0], [self.g[n] for n in names], gots, self.ci, BF16)
        self.acc.update(zip(names, accs))

    def _chip_add(self, names, chip):
        return _rs_chip_add("rs_chip_add_" + names[0], [self.acc[n] for n in names], chip.outs, self.pi)

    def _take_shared(self, names, share):
        for n, strip in zip(names, _shared(share)):
            self.reduced[n] = strip

    def host(self, point):
        a, b = self.GROUP_A, self.GROUP_B
        if point in self.LATE:
            names = self.LATE[point]
            ex = _all_gather([self.strip[n] for n in names])
            self.gathers[point] = (names, ex)
            return ex
        if point in ("ffn_in_bwd", "mix_out_bwd"):
            names = dict(ffn_in_bwd=a[:3], mix_out_bwd=a[3:])[point]
            ex = _rs_to_sibling([self.g[n] for n in names])
            self.sibs.append(ex)
            return ex
        if point == "dn_chunk_bwd":
            self._pair_add(a, [o for ex in self.sibs for o in ex.outs])
            self.chip1 = _rs_to_chips([self.acc[n] for n in a[:2]])
            return self.chip1
        if point == "attn_bwd":
            self.chip2 = _rs_to_chips([self.acc[n] for n in a[2:]])
            return self.chip2
        if point == "mla_prep_bwd":
            ress = self._chip_add(a[:2], self.chip1) + self._chip_add(a[2:], self.chip2)
            self.share = _rs_share(ress)
            return self.share
        if point == "in_dx":
            self._take_shared(a, self.share)
            self.sib = _rs_to_sibling([self.g[n] for n in b])
            return self.sib
        if point == "attn_norm_bwd":
            self._pair_add(b, self.sib.outs)
            self.chip = _rs_to_chips([self.acc[n] for n in b])
            return self.chip
        return None

    def last_share(self):
        self.share = _rs_share(self._chip_add(self.GROUP_B, self.chip))
        return self.share

    def finish(self):
        self._take_shared(self.GROUP_B, self.share)
        return self.reduced


def kernel(x, meta_tokens, attn_norm_w, w_in, q_a_norm_w, w_q_b, kv_a_norm_w, w_kv_b, q_norm_w, k_norm_w, mla_out_norm_w, dn_conv_w, dn_A_log, dn_dt_bias, dn_out_norm_w, w_out, ffn_norm_w, w_gate, w_up, ffn_conv_w, ffn_conv_b, w_down, loss_target, m_meta_tokens, m_attn_norm_w, m_w_in, m_q_a_norm_w, m_w_q_b, m_kv_a_norm_w, m_w_kv_b, m_q_norm_w, m_k_norm_w, m_mla_out_norm_w, m_dn_conv_w, m_dn_A_log, m_dn_dt_bias, m_dn_out_norm_w, m_w_out, m_ffn_norm_w, m_w_gate, m_w_up, m_ffn_conv_w, m_ffn_conv_b, m_w_down, v_meta_tokens, v_attn_norm_w, v_w_in, v_q_a_norm_w, v_w_q_b, v_kv_a_norm_w, v_w_kv_b, v_q_norm_w, v_k_norm_w, v_mla_out_norm_w, v_dn_conv_w, v_dn_A_log, v_dn_dt_bias, v_dn_out_norm_w, v_w_out, v_ffn_norm_w, v_w_gate, v_w_up, v_ffn_conv_w, v_ffn_conv_b, v_w_down):
    local = dict(locals())
    w = {n: local[n] for n in WEIGHTS}
    m = {n: local["m_" + n] for n in WEIGHTS}
    v = {n: local["v_" + n] for n in WEIGHTS}
    p = 2 * lax.axis_index("x") + lax.axis_index("y")

    plan = _MeshPlan(w)
    wf = _pad_rows(jnp.concatenate([w[n].reshape(-1) for n, _, _ in SMALL_SHARDED]), SMALL_ROWS)
    gf = plan.gather_small(wf).reshape(N_CHIPS, -1)
    full = {}
    off = 0
    for n, s, ax in SMALL_SHARDED:
        full[n] = _unshard(gf[:, off:off + s[0] * s[1]], s, ax)
        off += s[0] * s[1]
    for n, _ in REPLICATED:
        full[n] = w[n]
    full["ffn_conv_w"] = _ff_to_pad(full["ffn_conv_w"], 1)
    full["ffn_conv_b"] = _ff_to_pad(full["ffn_conv_b"], 1)

    sq, grad_x, g = _local_step(x[0], loss_target[0], full, plan)
    g["ffn_conv_w"] = _ff_from_pad(g["ffn_conv_w"], 1)
    g["ffn_conv_b"] = _ff_from_pad(g["ffn_conv_b"], 1)

    small_all = [n for n, _, _ in SMALL_SHARDED] + [n for n, _ in REPLICATED]
    vec = jnp.concatenate([g[n].reshape(-1) for n in small_all] + [jnp.reshape(0.5 / D_MODEL * jnp.sum(sq), (1,))])
    vec = _pad_rows(vec, -(-vec.shape[0] // (8 * LANES)) * 8)
    a2a = _all_to_all_devices(vec)

    gs, delta, new_m, new_v = {}, {}, {}, {}
    big = {n: (s, ax) for n, s, ax, _ in BIG}

    def adamw_big(n, strips, host=None):
        s, ax = big[n]
        if ax == 1 and s[1] % 8:
            there = lambda a: jnp.transpose(a, (2, 0, 1))
            back = lambda a: jnp.transpose(a, (1, 2, 0))
            g3 = strips[n][:s[1]].reshape(s[1], 1, s[0])
            d2, m2, v2 = _adamw_rows3d("adamw_" + n, there(w[n]), g3, there(m[n]), there(v[n]))
            gs[n], delta[n], new_m[n], new_v[n] = back(g3), back(d2), back(m2), back(v2)
            return
        flip = ax == 1 and s[1] % 8 == 0
        there = (lambda a: a.reshape(s).T) if flip else (lambda a: a.reshape(s))
        back = (lambda a: a.T.reshape(w[n].shape)) if flip else (lambda a: a.reshape(w[n].shape))
        strip = strips[n] if flip or ax == 0 else strips[n][:s[1]].T
        g2, d2, m2, v2 = _adamw_call("adamw_" + n, there(w[n]), strip, there(m[n]), there(v[n]), host=host)
        gs[n], delta[n], new_m[n], new_v[n] = back(g2), back(d2), back(m2), back(v2)

    adamw_big("w_down", plan.reduced, host=a2a)
    adamw_big("w_gate", plan.reduced, host=plan.last_share())
    adamw_big("w_up", plan.reduced)
    adamw_big("w_out", plan.reduced)
    strips = plan.finish()
    for n in plan.GROUP_B:
        adamw_big(n, strips)
    me = 4 * lax.axis_index("x") + 2 * lax.axis_index("y") + lax.axis_index("c")
    red = _sum_devices(lax.dynamic_update_slice(a2a.outs[0], vec[None], (me, 0, 0))).reshape(-1)
    off = 0
    for n in small_all:
        tot = red[off:off + g[n].size].reshape(g[n].shape)
        off += g[n].size
        shard = {sn: (s, ax) for sn, s, ax in SMALL_SHARDED}.get(n)
        if shard is not None:
            tot = lax.dynamic_slice_in_dim(tot, p * shard[0][1], shard[0][1], axis=1)
        gs[n] = tot
    loss = red[off]
    two_d = lambda a: a.reshape(a.shape[-2], a.shape[-1])
    outs = _adamw_small([two_d(w[n]) for n in small_all], [two_d(gs[n]) for n in small_all],
                        [two_d(m[n]) for n in small_all], [two_d(v[n]) for n in small_all])
    for i, n in enumerate(small_all):
        for dst, src in ((delta, outs[0]), (new_m, outs[1]), (new_v, outs[2])):
            dst[n] = src[i].reshape(w[n].shape)

    grad_out = [gs[n].reshape(w[n].shape) for n in WEIGHTS]
    return (loss, grad_x[None], *grad_out, *[delta[n] for n in WEIGHTS], *[new_m[n] for n in WEIGHTS],
            *[new_v[n] for n in WEIGHTS])
```

```python
import functools
import math

import jax
import jax.numpy as jnp
import numpy as np
from jax import lax
from jax.experimental import pallas as pl
from jax.experimental.pallas import tpu as pltpu

F32 = jnp.float32
BF16 = jnp.bfloat16
HI = lax.Precision.HIGHEST
MESH = pl.DeviceIdType.MESH

N_META = 16
D_MODEL = 1024
HEADS = 4
HEAD = 128
ROPE = 64
QK_DIM = HEAD + ROPE
QK_PAD = 2 * HEAD
LORA = 256
DN_WIDTH = HEADS * HEAD
CHUNK = 64
D_FF = 2816
N_CHIPS = 4
FF_SHARD = D_FF // N_CHIPS
FF_BLOCK = 768
D_FF_P = N_CHIPS * FF_BLOCK
IN_COLS = 2632
IN_SHARD = IN_COLS // N_CHIPS
IN_SHARD_P = 672
NORM_EPS = 1e-6
ROPE_THETA = 10000.0
LANES = 512

ADAM_LR, ADAM_B1, ADAM_B2, ADAM_EPS, ADAM_WD, ADAM_STEP = 0.001, 0.9, 0.999, 1e-08, 0.01, 10

VMEM_LIMIT = 56 * 1024 * 1024

BIG = (("w_in", (1024, 658), 1, IN_SHARD_P), ("w_q_b", (256, 192), 1, 192), ("w_kv_b", (256, 256), 1, 256),
       ("w_out", (256, 1024), 0, 256), ("w_gate", (1024, 704), 1, FF_BLOCK), ("w_up", (1024, 704), 1, FF_BLOCK),
       ("w_down", (704, 1024), 0, FF_BLOCK))
SMALL_SHARDED = (("meta_tokens", (16, 256), 1), ("dn_conv_w", (4, 384), 1), ("ffn_conv_w", (3, 704), 1))
REPLICATED = (("attn_norm_w", 1024), ("q_a_norm_w", 256), ("kv_a_norm_w", 256), ("q_norm_w", 192), ("k_norm_w", 192),
              ("mla_out_norm_w", 128), ("dn_A_log", 4), ("dn_dt_bias", 4), ("dn_out_norm_w", 128), ("ffn_norm_w", 1024),
              ("ffn_conv_b", 2816))
WEIGHTS = ("meta_tokens", "attn_norm_w", "w_in", "q_a_norm_w", "w_q_b", "kv_a_norm_w", "w_kv_b", "q_norm_w", "k_norm_w",
           "mla_out_norm_w", "dn_conv_w", "dn_A_log", "dn_dt_bias", "dn_out_norm_w", "w_out", "ffn_norm_w", "w_gate",
           "w_up", "ffn_conv_w", "ffn_conv_b", "w_down")

SMALL_ROWS = 16


def _cparams(sem):
    return pltpu.CompilerParams(dimension_semantics=sem, vmem_limit_bytes=VMEM_LIMIT)


class _Exchange:
    def __init__(self, prog, ins, out_shape, nsem, peers=None, cid=None):
        self.prog, self.ins, self.out_shape, self.nsem = prog, list(ins), list(out_shape), nsem
        self.peers, self.cid = peers, cid
        self.outs = None

    def sems(self):
        return [pltpu.SemaphoreType.DMA((self.nsem,)), pltpu.SemaphoreType.DMA((self.nsem,))]

    def programs(self, in_refs, out_refs, send_sems, recv_sems):
        start, finish = self.prog(in_refs, out_refs, send_sems, recv_sems)
        if self.cid is None:
            return start, finish
        peers = self.peers()

        def shake_and_start():
            barrier = pltpu.get_barrier_semaphore()
            for peer in peers:
                pl.semaphore_signal(barrier, inc=1, device_id=peer, device_id_type=MESH)
            pl.semaphore_wait(barrier, len(peers))
            start()

        return shake_and_start, finish

    def cparams(self, **kw):
        return pltpu.CompilerParams(has_side_effects=True, collective_id=self.cid, **kw)

    def run(self, name):
        any_spec = pl.BlockSpec(memory_space=pl.ANY)
        n = len(self.ins)

        def body(*refs):
            start, finish = self.programs(refs[:n], refs[n:-2], refs[-2], refs[-1])
            start()
            finish()

        self.outs = pl.pallas_call(
            body, name=name, in_specs=[any_spec] * n, out_specs=[any_spec] * len(self.out_shape),
            out_shape=self.out_shape, scratch_shapes=self.sems(), compiler_params=self.cparams())(*self.ins)
        return self.outs


def _pcall(body, name, grid, in_specs, out_specs, out_shape, args, sem, scratch_shapes=(), host=None):
    single = not isinstance(out_shape, (list, tuple))
    out_specs, out_shape = ([out_specs], [out_shape]) if single else (list(out_specs), list(out_shape))
    if host is None:
        outs = pl.pallas_call(body, name=name, grid=grid, in_specs=list(in_specs), out_specs=out_specs, out_shape=out_shape,
                              scratch_shapes=list(scratch_shapes), compiler_params=_cparams(sem))(*args)
        return outs[0] if single else outs
    any_spec = pl.BlockSpec(memory_space=pl.ANY)
    n_in, n_out, n_scr, nx_in, nx_out = len(in_specs), len(out_specs), len(scratch_shapes), len(host.ins), len(host.out_shape)

    def hosted(*refs):
        c_in, x_in = refs[:n_in], refs[n_in:n_in + nx_in]
        o0 = n_in + nx_in
        c_out, x_out = refs[o0:o0 + n_out], refs[o0 + n_out:o0 + n_out + nx_out]
        s0 = o0 + n_out + nx_out
        start, finish = host.programs(x_in, x_out, refs[s0 + n_scr], refs[s0 + n_scr + 1])
        first = functools.reduce(jnp.logical_and, [pl.program_id(d) == 0 for d in range(len(grid))])
        last = functools.reduce(jnp.logical_and, [pl.program_id(d) == grid[d] - 1 for d in range(len(grid))])
        pl.when(first)(start)
        body(*c_in, *c_out, *refs[s0:s0 + n_scr])
        pl.when(last)(finish)

    outs = pl.pallas_call(
        hosted, name=name, grid=grid, in_specs=list(in_specs) + [any_spec] * nx_in,
        out_specs=out_specs + [any_spec] * nx_out, out_shape=out_shape + host.out_shape,
        scratch_shapes=list(scratch_shapes) + host.sems(),
        compiler_params=host.cparams(dimension_semantics=sem, vmem_limit_bytes=VMEM_LIMIT))(*args, *host.ins)
    host.outs = outs[n_out:]
    return outs[0] if single else outs[:n_out]


NN, NT, TN = ((1,), (0,)), ((1,), (1,)), ((0,), (0,))


def _shift_dims(dims, batch):
    if not batch:
        return (dims, ((), ()))
    return (((dims[0][0] + 1,), (dims[1][0] + 1,)), ((0,), (0,)))


def _make_mm(dims, exact, batch=False):
    def raw(a, b, d):
        dn = _shift_dims(d, batch)
        if exact == "split_lhs":
            ah, bh = a.astype(BF16), b.astype(BF16)
            al = (a - ah.astype(F32)).astype(BF16)
            return lax.dot_general(ah, bh, dn, preferred_element_type=F32) + lax.dot_general(al, bh, dn,
                                                                                              preferred_element_type=F32)
        if exact == "split":
            ah, bh = a.astype(BF16), b.astype(BF16)
            al, bl = (a - ah.astype(F32)).astype(BF16), (b - bh.astype(F32)).astype(BF16)
            dot = lambda p, q: lax.dot_general(p, q, dn, preferred_element_type=F32)
            return dot(ah, bh) + (dot(ah, bl) + dot(al, bh))
        if exact:
            return lax.dot_general(a.astype(F32), b.astype(F32), dn, precision=HI, preferred_element_type=F32)
        return lax.dot_general(a.astype(BF16), b.astype(BF16), dn, preferred_element_type=F32)

    @jax.custom_vjp
    def mm(a, b):
        return raw(a, b, dims)

    def fwd(a, b):
        return raw(a, b, dims), (a, b)

    def bwd(res, g):
        a, b = res
        if dims == NN:
            da, db = raw(g, b, NT), raw(a, g, TN)
        elif dims == NT:
            da, db = raw(g, b, NN), raw(g, a, TN)
        else:
            da, db = raw(b, g, NT), raw(a, g, NN)
        return da.astype(a.dtype), db.astype(b.dtype)

    mm.defvjp(fwd, bwd)
    return mm


_mm = _make_mm(NN, False)
_mm_nt = _make_mm(NT, False)
_mm_tn = _make_mm(TN, False)
_mmx = _make_mm(NN, "split_lhs")
_bmm = _make_mm(NN, False, batch=True)
_bmm_nt = _make_mm(NT, False, batch=True)
_bmm_tn = _make_mm(TN, False, batch=True)
_bmmx = _make_mm(NN, True, batch=True)
_bmms = _make_mm(NN, "split", batch=True)
_bmms_nt = _make_mm(NT, "split", batch=True)
_bmms_tn = _make_mm(TN, "split", batch=True)


@jax.custom_vjp
def _unit_lower_inv(a):
    n = a.shape[-1]
    eye = (lax.broadcasted_iota(jnp.int32, a.shape, 1) == lax.broadcasted_iota(jnp.int32, a.shape, 2)).astype(F32)
    x = -a
    t = eye + x
    for _ in range(max(n.bit_length() - 2, 0)):
        x = _bmms(x, x)
        t = t + _bmms(t, x)
    return t


def _unit_lower_inv_fwd(a):
    t = _unit_lower_inv(a)
    return t, t


def _unit_lower_inv_bwd(t, g):
    return (-_bmms_tn(t, _bmms_nt(g, t)),)


_unit_lower_inv.defvjp(_unit_lower_inv_fwd, _unit_lower_inv_bwd)


def _scan_chunk_rows(x, reverse):
    nb, c, w = x.shape
    y = x.reshape(nb * c, w)
    pos = lax.broadcasted_iota(jnp.int32, y.shape, 0) % c
    step = 1
    while step < c:
        if reverse:
            y = y + jnp.where(pos < c - step, pltpu.roll(y, nb * c - step, 0), 0.0)
        else:
            y = y + jnp.where(pos >= step, pltpu.roll(y, step, 0), 0.0)
        step *= 2
    return y.reshape(nb, c, w)


@jax.custom_vjp
def _chunk_cumsum(x):
    return _scan_chunk_rows(x, False)


_chunk_cumsum.defvjp(lambda x: (_scan_chunk_rows(x, False), None), lambda _, g: (_scan_chunk_rows(g, True),))


def _rms(x, w, n):
    ms = jnp.sum(x * x, axis=-1, keepdims=True) * (1.0 / n)
    return x * lax.rsqrt(ms + NORM_EPS) * w


def _silu(x):
    return x * jax.nn.sigmoid(x)


def _softplus(x):
    return jnp.maximum(x, 0.0) + jnp.log(1.0 + jnp.exp(-jnp.abs(x)))


def _rope(x, cos, sin, perm):
    return x * cos + _mmx(x, perm) * sin


def _mla_prep_fn(rows, consts):
    q_lat, kv_lat, k_pe, cos, sin = rows
    qn = _rms(q_lat, consts["qa_w"], LORA)
    kvn = _rms(kv_lat, consts["kva_w"], LORA)
    outs = []
    for h in range(HEADS):
        q_n = _mm_nt(qn, consts["wq_n"][h])
        q_r = _mm_nt(qn, consts["wq_r"][h])
        rs = lax.rsqrt((jnp.sum(q_n * q_n, -1, keepdims=True) + jnp.sum(q_r * q_r, -1, keepdims=True)) * (1.0 / QK_DIM)
                       + NORM_EPS)
        q_n = q_n * rs * consts["qn_n"]
        q_r = _rope(q_r * rs * consts["qn_r"], cos, sin, consts["perm"])
        k_n = _mm_nt(kvn, consts["wk_n"][h])
        v = _mm_nt(kvn, consts["wv"][h])
        rk = lax.rsqrt((jnp.sum(k_n * k_n, -1, keepdims=True) + jnp.sum(k_pe * k_pe, -1, keepdims=True)) * (1.0 / QK_DIM)
                       + NORM_EPS)
        k_n = k_n * rk * consts["kn_n"]
        k_r = _rope(k_pe * rk * consts["kn_r"], cos, sin, consts["perm"])
        outs += [q_n, q_r, k_n, k_r, v]
    return tuple(outs)


def _attn_fn(q, k, v, row0):
    s = _mm_nt(q, k) * (1.0 / math.sqrt(QK_DIM))
    qpos = row0 + lax.broadcasted_iota(jnp.int32, s.shape, 0)
    kpos = lax.broadcasted_iota(jnp.int32, s.shape, 1)
    s = jnp.where(kpos <= qpos, s, -1e30)
    m = lax.stop_gradient(jnp.max(s, axis=-1, keepdims=True))
    p = jnp.exp(s - m)
    p = p / jnp.sum(p, axis=-1, keepdims=True)
    return _mm(p, v)


def _dn_prep_fn(rows, consts):
    qc, kc, ab = rows
    a_b = _mmx(ab, consts["sel_a"])
    b_b = _mmx(ab, consts["sel_b"])
    beta = jax.nn.sigmoid(b_b)
    g = -jnp.exp(consts["alog"]) * _softplus(a_b + consts["dtb"])
    qs, ks = [], []
    for h in range(HEADS):
        q, k = qc[h], kc[h]
        qs.append(q * lax.rsqrt(jnp.sum(q * q, -1, keepdims=True) + NORM_EPS))
        ks.append(k * lax.rsqrt(jnp.sum(k * k, -1, keepdims=True) + NORM_EPS))
    return tuple(qs), tuple(ks), g, beta


def _dn_chunk_fn(q, k, v, gb, g64, bb):
    nb = q.shape[0]
    ri = lax.broadcasted_iota(jnp.int32, (nb, CHUNK, CHUNK), 1)
    ci = lax.broadcasted_iota(jnp.int32, (nb, CHUNK, CHUNK), 2)
    tri = ri >= ci
    strict = ri > ci
    tril = tri.astype(F32)
    eye = (ri == ci).astype(F32)
    ones = jnp.ones((nb, CHUNK, CHUNK), F32)
    gc = _chunk_cumsum(gb)
    gc64 = _chunk_cumsum(g64)
    grow = _bmmx(ones, eye * gc64)
    diff = gc64 - grow
    decay = jnp.where(tri, jnp.exp(jnp.where(tri, diff, 0.0)), 0.0)
    kb = k * bb
    vb = v * bb
    a = jnp.where(strict, _bmm_nt(kb, k) * decay, 0.0)
    tinv = _unit_lower_inv(a)
    u = _bmm(tinv, vb)
    w = _bmm(tinv, kb * jnp.exp(gc))
    qs = q * (1.0 / math.sqrt(HEAD))
    qk = _bmm_nt(qs, k) * decay
    qg = qs * jnp.exp(gc)
    glast = jnp.sum(gb, axis=1, keepdims=True)
    kdec = k * jnp.exp(glast - gc)
    n_mat = _bmm_tn(kdec, w)
    b_mat = _bmm_tn(kdec, u)
    q_eff = qg - _bmm(qk, w)
    o_own = _bmm(qk, u)
    return n_mat, b_mat, q_eff, o_own, jnp.exp(glast)


def _dn_rec_fn(s, n_mat, b_mat, eg):
    return s * eg - _mm(n_mat, s) + b_mat


def _dn_o_fn(s, q_eff, o_own):
    return _bmm(q_eff, s) + o_own


def _dn_out_fn(o, z, w):
    return _rms(o, w, HEAD) * _silu(z)


def _row_tile(t, parts=8):
    return t // parts if (t // parts) % 16 == 0 else t


def _tile(n, pref, unit):
    best = n
    for cand in range(unit, min(n, pref) + 1, unit):
        if n % cand == 0:
            best = cand
    return best if best <= pref else n


def _rows_call(name, body, rows, consts, outs, accs, r, host=None):
    rows = [a if isinstance(a, tuple) else (a, a.shape[1], 0) for a in rows]
    t = rows[0][0].shape[0]
    zero = lambda nd: (lambda i: (0,) * nd)
    in_specs = [pl.BlockSpec((r, w), functools.partial(lambda i, b: (i, b), b=blk)) for _, w, blk in rows]
    rows = [a for a, _, _ in rows]
    in_specs += [pl.BlockSpec(a.shape, zero(a.ndim)) for a in consts]
    out_shape = [jax.ShapeDtypeStruct((t, w), dt) for w, dt in outs] + [jax.ShapeDtypeStruct(s, F32) for s in accs]
    out_specs = [pl.BlockSpec((r, w), lambda i: (i, 0)) for w, _ in outs] + [pl.BlockSpec(s, zero(len(s))) for s in accs]
    return _pcall(body, name, (t // r,), in_specs, out_specs, out_shape, [*rows, *consts], ("arbitrary",), host=host)


def _accumulate(ref, val):
    @pl.when(pl.program_id(0) == 0)
    def _():
        ref[...] = jnp.zeros(ref.shape, ref.dtype)

    ref[...] += val


def _matmul(name, a, b, dims, out_dtype, res=None, host=None):
    if dims == "nn":
        (m, k), n = a.shape, b.shape[1]
    elif dims == "nt":
        (m, k), n = a.shape, b.shape[0]
    else:
        (k, m), n = a.shape, b.shape[1]
    tm = _tile(m, 1100, 16) if dims != "tn" else _tile(m, 640, 128)
    tn = _tile(n, 1408, 128)
    if dims == "nn":
        a_spec, b_spec, dn = pl.BlockSpec((tm, k), lambda i, j: (i, 0)), pl.BlockSpec((k, tn), lambda i, j: (0, j)), NN
    elif dims == "nt":
        a_spec, b_spec, dn = pl.BlockSpec((tm, k), lambda i, j: (i, 0)), pl.BlockSpec((tn, k), lambda i, j: (j, 0)), NT
    else:
        a_spec, b_spec, dn = pl.BlockSpec((k, tm), lambda i, j: (0, i)), pl.BlockSpec((k, tn), lambda i, j: (0, j)), TN
    o_spec = pl.BlockSpec((tm, tn), lambda i, j: (i, j))

    def body(*refs):
        a_ref, b_ref, o_ref = refs[0], refs[1], refs[-1]
        acc = lax.dot_general(a_ref[...].astype(BF16), b_ref[...].astype(BF16), (dn, ((), ())),
                              preferred_element_type=F32)
        if res is not None:
            acc = acc + refs[2][...]
        o_ref[...] = acc.astype(out_dtype)

    ins = [a, b] + ([res] if res is not None else [])
    specs = [a_spec, b_spec] + ([o_spec] if res is not None else [])
    return _pcall(body, name, (m // tm, n // tn), specs, o_spec, jax.ShapeDtypeStruct((m, n), out_dtype), ins,
                  ("arbitrary", "arbitrary"), host=host)


def _rms_fwd(name, h, w, host=None):
    n = h.shape[1]

    def body(h_ref, w_ref, o_ref):
        o_ref[...] = _rms(h_ref[...], w_ref[...], n).astype(BF16)

    return _rows_call(name, body, [h], [w], [(n, BF16)], [], _row_tile(h.shape[0]), host=host)[0]


def _rms_bwd(name, h, w, cts, resid, host=None):
    n = h.shape[1]
    nct = len(cts)

    def body(*refs):
        h_ref, ct_refs, r_ref, w_ref = refs[0], refs[1:1 + nct], refs[1 + nct], refs[2 + nct]
        dh_ref, dh16_ref, dw_ref = refs[-3], refs[-2], refs[-1]
        ct = ct_refs[0][...].astype(F32)
        for c in ct_refs[1:]:
            ct = ct + c[...].astype(F32)
        _, vjp = jax.vjp(lambda x, ww: _rms(x, ww, n), h_ref[...], w_ref[...])
        dh, dw = vjp(ct)
        dh = dh + r_ref[...]
        dh_ref[...] = dh
        dh16_ref[...] = dh.astype(BF16)
        _accumulate(dw_ref, dw)

    return _rows_call(name, body, [h, *cts, resid], [w], [(n, F32), (n, BF16)], [(1, n)], _row_tile(h.shape[0]), host=host)


def _mla_consts_from_refs(qa, wq, kva, wkv, qn, kn, perm):
    f = lambda r: r[...].astype(F32)
    return dict(
        qa_w=f(qa), kva_w=f(kva), perm=f(perm),
        wq_n=[wq[h * QK_PAD:h * QK_PAD + HEAD, :].astype(F32) for h in range(HEADS)],
        wq_r=[wq[h * QK_PAD + HEAD:(h + 1) * QK_PAD, :].astype(F32) for h in range(HEADS)],
        wk_n=[wkv[h * QK_PAD:h * QK_PAD + HEAD, :].astype(F32) for h in range(HEADS)],
        wv=[wkv[h * QK_PAD + HEAD:(h + 1) * QK_PAD, :].astype(F32) for h in range(HEADS)],
        qn_n=qn[:, 0:HEAD], qn_r=qn[:, HEAD:QK_PAD], kn_n=kn[:, 0:HEAD], kn_r=kn[:, HEAD:QK_PAD])


def _mla_prep_fwd(q_lat, kv_lat, k_pe, cos, sin, qa, wq, kva, wkv, qn, kn, perm):
    def body(ql, kvl, kp, c, s, qa_r, wq_r, kva_r, wkv_r, qn_r, kn_r, p_r, q_out, k_out, v_out):
        consts = _mla_consts_from_refs(qa_r, wq_r, kva_r, wkv_r, qn_r, kn_r, p_r)
        outs = _mla_prep_fn((ql[...], kvl[...], kp[...], c[...], s[...]), consts)
        for h in range(HEADS):
            q_n, q_r, k_n, k_r, v = outs[5 * h:5 * h + 5]
            q_out[:, h * QK_PAD:h * QK_PAD + HEAD] = q_n.astype(BF16)
            q_out[:, h * QK_PAD + HEAD:(h + 1) * QK_PAD] = q_r.astype(BF16)
            k_out[:, h * QK_PAD:h * QK_PAD + HEAD] = k_n.astype(BF16)
            k_out[:, h * QK_PAD + HEAD:(h + 1) * QK_PAD] = k_r.astype(BF16)
            v_out[:, h * HEAD:(h + 1) * HEAD] = v.astype(BF16)

    return _rows_call("mla_prep_fwd", body, [q_lat, kv_lat, k_pe, cos, sin], [qa, wq, kva, wkv, qn, kn, perm],
                      [(HEADS * QK_PAD, BF16), (HEADS * QK_PAD, BF16), (DN_WIDTH, BF16)], [], _row_tile(cos.shape[0], 4))


def _mla_prep_bwd(q_lat, kv_lat, k_pe, cos, sin, dq, dk, dv, qa, wq, kva, wkv, qn, kn, perm, host=None):
    def body(ql, kvl, kp, c, s, dq_r, dk_r, dv_r, qa_r, wq_r, kva_r, wkv_r, qn_r, kn_r, p_r,
             dql, dkvl, dkp, dqa, dwq, dkva, dwkv, dqn, dkn):
        consts = _mla_consts_from_refs(qa_r, wq_r, kva_r, wkv_r, qn_r, kn_r, p_r)
        cc, ss, pm = c[...], s[...], consts.pop("perm")
        _, vjp = jax.vjp(lambda rows, cs: _mla_prep_fn((*rows, cc, ss), dict(cs, perm=pm)), (ql[...], kvl[...], kp[...]),
                         consts)
        cts = []
        for h in range(HEADS):
            cts += [dq_r[:, h * QK_PAD:h * QK_PAD + HEAD], dq_r[:, h * QK_PAD + HEAD:(h + 1) * QK_PAD],
                    dk_r[:, h * QK_PAD:h * QK_PAD + HEAD], dk_r[:, h * QK_PAD + HEAD:(h + 1) * QK_PAD],
                    dv_r[:, h * HEAD:(h + 1) * HEAD]]
        (d_ql, d_kvl, d_kp), dc = vjp(tuple(cts))
        dql[...] = d_ql.astype(BF16)
        dkvl[...] = d_kvl.astype(BF16)
        dkp[...] = d_kp.astype(BF16)
        first = pl.program_id(0) == 0

        def acc(ref, sl, val):
            @pl.when(first)
            def _():
                ref[sl] = val

            @pl.when(jnp.logical_not(first))
            def _():
                ref[sl] += val

        full = (slice(None), slice(None))
        acc(dqa, full, dc["qa_w"])
        acc(dkva, full, dc["kva_w"])
        for h in range(HEADS):
            acc(dwq, (slice(h * QK_PAD, h * QK_PAD + HEAD), slice(None)), dc["wq_n"][h])
            acc(dwq, (slice(h * QK_PAD + HEAD, (h + 1) * QK_PAD), slice(None)), dc["wq_r"][h])
            acc(dwkv, (slice(h * QK_PAD, h * QK_PAD + HEAD), slice(None)), dc["wk_n"][h])
            acc(dwkv, (slice(h * QK_PAD + HEAD, (h + 1) * QK_PAD), slice(None)), dc["wv"][h])
        acc(dqn, (slice(None), slice(0, HEAD)), dc["qn_n"])
        acc(dqn, (slice(None), slice(HEAD, QK_PAD)), dc["qn_r"])
        acc(dkn, (slice(None), slice(0, HEAD)), dc["kn_n"])
        acc(dkn, (slice(None), slice(HEAD, QK_PAD)), dc["kn_r"])

    return _rows_call("mla_prep_bwd", body, [q_lat, kv_lat, k_pe, cos, sin, dq, dk, dv],
                      [qa, wq, kva, wkv, qn, kn, perm],
                      [(LORA, BF16), (LORA, BF16), (HEAD, BF16)],
                      [(1, LORA), wq.shape, (1, LORA), wkv.shape, (1, QK_PAD), (1, QK_PAD)], _row_tile(cos.shape[0], 4),
                      host=host)


ATTN_Q_ROWS = 512


def _attn_blocks(t):
    return [(r0, min(ATTN_Q_ROWS, t - r0)) for r0 in range(0, t, ATTN_Q_ROWS)]


def _attn_fwd(q, k, v, host=None):
    t = q.shape[0]

    def body(q_ref, k_ref, v_ref, o_ref):
        for r0, rows in _attn_blocks(t):
            ext = r0 + rows
            o_ref[r0:ext, :] = _attn_fn(q_ref[r0:ext, :], k_ref[0:ext, :], v_ref[0:ext, :], r0)

    qk_spec = pl.BlockSpec((t, QK_PAD), lambda h: (0, h))
    v_spec = pl.BlockSpec((t, HEAD), lambda h: (0, h))
    return _pcall(body, "attn_fwd", (HEADS,), [qk_spec, qk_spec, v_spec], v_spec,
                  jax.ShapeDtypeStruct((t, HEADS * HEAD), F32), [q, k, v], ("arbitrary",), host=host)


def _attn_bwd(q, k, v, do, host=None):
    t = q.shape[0]

    def body(q_ref, k_ref, v_ref, do_ref, dq_ref, dk_ref, dv_ref):
        dk_ref[...] = jnp.zeros(dk_ref.shape, F32)
        dv_ref[...] = jnp.zeros(dv_ref.shape, F32)
        for r0, rows in _attn_blocks(t):
            ext = r0 + rows
            _, vjp = jax.vjp(functools.partial(_attn_fn, row0=r0), q_ref[r0:ext, :].astype(F32),
                             k_ref[0:ext, :].astype(F32), v_ref[0:ext, :].astype(F32))
            dq, dk, dv = vjp(do_ref[r0:ext, :])
            dq_ref[r0:ext, :] = dq
            dk_ref[0:ext, :] += dk
            dv_ref[0:ext, :] += dv

    qk_spec = pl.BlockSpec((t, QK_PAD), lambda h: (0, h))
    v_spec = pl.BlockSpec((t, HEAD), lambda h: (0, h))
    return _pcall(body, "attn_bwd", (HEADS,), [qk_spec, qk_spec, v_spec, v_spec], [qk_spec, qk_spec, v_spec],
                  [jax.ShapeDtypeStruct((t, HEADS * QK_PAD), F32), jax.ShapeDtypeStruct((t, HEADS * QK_PAD), F32),
                   jax.ShapeDtypeStruct((t, HEADS * HEAD), F32)], [q, k, v, do], ("arbitrary",), host=host)


def _mix_out_proj(o_mla, o_dn, z, w_mla, w_dn, w_out, h0, w_ffn):
    def body(om_ref, od_ref, z_ref, h0_ref, wm_ref, wd_ref, wo_ref, wf_ref, mixed_ref, h1_ref, n2_ref):
        for h in range(HEADS):
            sl = slice(h * HEAD, (h + 1) * HEAD)
            mixed_ref[:, sl] = _rms(om_ref[:, sl], wm_ref[...], HEAD).astype(BF16)
            mixed_ref[:, DN_WIDTH + h * HEAD:DN_WIDTH + (h + 1) * HEAD] = _dn_out_fn(od_ref[:, sl], z_ref[:, sl],
                                                                                     wd_ref[...]).astype(BF16)
        h1 = _mm(mixed_ref[...], wo_ref[...]) + h0_ref[...]
        h1_ref[...] = h1
        n2_ref[...] = _rms(h1, wf_ref[...], D_MODEL).astype(BF16)

    return _rows_call("mix_out_proj", body, [o_mla, o_dn, z, h0], [w_mla, w_dn, w_out, w_ffn],
                      [(D_MODEL, BF16), (D_MODEL, F32), (D_MODEL, BF16)], [], _row_tile(o_mla.shape[0], 4))


def _down_proj_loss(act, w_down, h1, tgt, n_valid):
    t, n = h1.shape
    r = _row_tile(t, 4)

    def body(a_ref, h_ref, t_ref, w_ref, dy_ref, dy16_ref, acc_ref):
        h2 = _mm(a_ref[...], w_ref[...]) + h_ref[...]
        rows = pl.program_id(0) * r + lax.broadcasted_iota(jnp.int32, (r, n), 0)
        valid = jnp.logical_and(rows >= N_META, rows < n_valid)
        e = jnp.where(valid, h2 - t_ref[...], 0.0)
        dy = e * (1.0 / n)
        dy_ref[...] = dy
        dy16_ref[...] = dy.astype(BF16)
        _accumulate(acc_ref, jnp.sum(e * e, axis=0, keepdims=True))

    return _rows_call("down_proj_loss", body, [act, h1, tgt], [w_down], [(n, F32), (n, BF16)], [(1, n)], r)


def _in_proj_bwd_x(pieces, win, host=None):
    offs = np.cumsum([0] + [p.shape[1] for p in pieces])

    def body(*refs):
        p_refs, w_ref, o_ref = refs[:len(pieces)], refs[len(pieces)], refs[-1]
        acc = None
        for i, p_ref in enumerate(p_refs):
            part = _mm(p_ref[...], w_ref[int(offs[i]):int(offs[i + 1]), :])
            acc = part if acc is None else acc + part
        o_ref[...] = acc.astype(BF16)

    return _rows_call("in_dx", body, pieces, [win], [(win.shape[1], BF16)], [], _row_tile(pieces[0].shape[0], 4), host=host)[0]


def _in_proj_bwd_w(pieces, u):
    offs = np.cumsum([0] + [p.shape[1] for p in pieces])

    def body(*refs):
        p_refs, u_ref, o_ref = refs[:len(pieces)], refs[len(pieces)], refs[-1]
        first = pl.program_id(0) == 0
        uv = u_ref[...]
        for i, p_ref in enumerate(p_refs):
            rows = slice(int(offs[i]), int(offs[i + 1]))
            part = _mm_tn(p_ref[...], uv)

            @pl.when(first)
            def _():
                o_ref[rows, :] = part

            @pl.when(jnp.logical_not(first))
            def _():
                o_ref[rows, :] += part

    return _rows_call("in_dw", body, [*pieces, u], [], [], [(int(offs[-1]), u.shape[1])], _row_tile(u.shape[0], 2))[0]


def _ffn_in_bwd(dgpre, dup, w_gate_t, w_up_t, h1, dy, w_ffn, host=None):
    n = h1.shape[1]

    def body(dg_ref, du_ref, h_ref, dy_ref, wg_ref, wu_ref, w_ref, dh_ref, dh16_ref, dw_ref):
        ct = _mm(dg_ref[...], wg_ref[...]) + _mm(du_ref[...], wu_ref[...])
        _, vjp = jax.vjp(lambda x, ww: _rms(x, ww, n), h_ref[...], w_ref[...])
        dh, dw = vjp(ct)
        dh = dh + dy_ref[...]
        dh_ref[...] = dh
        dh16_ref[...] = dh.astype(BF16)
        _accumulate(dw_ref, dw)

    return _rows_call("ffn_in_bwd", body, [dgpre, dup, h1, dy], [w_gate_t, w_up_t, w_ffn], [(n, F32), (n, BF16)], [(1, n)],
                      _row_tile(h1.shape[0]), host=host)


def _mix_out_bwd(o_mla, o_dn, z, dh1, w_out, w_mla, w_dn, host=None):
    def body(om_ref, od_ref, z_ref, dh_ref, wo_ref, wm_ref, wd_ref, dom_ref, dod_ref, dz_ref, dwm_ref, dwd_ref):
        dwm = dwd = None
        for h in range(HEADS):
            sl = slice(h * HEAD, (h + 1) * HEAD)
            _, vjp = jax.vjp(lambda o, w: _rms(o, w, HEAD), om_ref[:, sl], wm_ref[...])
            do, dw = vjp(_mm_nt(dh_ref[...], wo_ref[sl, :]))
            dom_ref[:, sl] = do
            dwm = dw if dwm is None else dwm + dw
            _, vjp = jax.vjp(_dn_out_fn, od_ref[:, sl], z_ref[:, sl], wd_ref[...])
            do, dz, dw = vjp(_mm_nt(dh_ref[...], wo_ref[DN_WIDTH + h * HEAD:DN_WIDTH + (h + 1) * HEAD, :]))
            dod_ref[:, sl] = do
            dz_ref[:, sl] = dz.astype(BF16)
            dwd = dw if dwd is None else dwd + dw
        _accumulate(dwm_ref, dwm)
        _accumulate(dwd_ref, dwd)

    return _rows_call("mix_out_bwd", body, [o_mla, o_dn, z, dh1], [w_out, w_mla, w_dn],
                      [(DN_WIDTH, F32), (DN_WIDTH, F32), (DN_WIDTH, BF16)], [(1, HEAD), (1, HEAD)],
                      _row_tile(o_mla.shape[0], 4), host=host)


def _shift_down(x, s):
    if s == 0:
        return x
    rows = lax.broadcasted_iota(jnp.int32, x.shape, 0)
    return jnp.where(rows >= s, pltpu.roll(x, s, 0), 0.0)


def _shift_up(x, s):
    if s == 0:
        return x
    t = x.shape[0]
    rows = lax.broadcasted_iota(jnp.int32, x.shape, 0)
    return jnp.where(rows < t - s, pltpu.roll(x, t - s, 0), 0.0)


def _col_call(name, body, cols, taps, outs, tap_outs, cw, host=None):
    t, c = cols[0].shape[0], taps[0].shape[1]
    in_specs = [pl.BlockSpec((t, cw), lambda j: (0, j)) for _ in cols]
    in_specs += [pl.BlockSpec((a.shape[0], cw), lambda j: (0, j)) for a in taps]
    out_shape = [jax.ShapeDtypeStruct((t, c), dt) for dt in outs] + [jax.ShapeDtypeStruct((n, c), F32) for n in tap_outs]
    out_specs = [pl.BlockSpec((t, cw), lambda j: (0, j)) for _ in outs]
    out_specs += [pl.BlockSpec((n, cw), lambda j: (0, j)) for n in tap_outs]
    return _pcall(body, name, (c // cw,), in_specs, out_specs, out_shape, [*cols, *taps], ("arbitrary",), host=host)


def _causal_conv(x, w_ref, width, zero_tail=False):
    down = (lambda a, s: pltpu.roll(a, s, 0)) if zero_tail else _shift_down
    acc = w_ref[width - 1:width, :] * x
    for j in range(width - 1):
        acc = acc + w_ref[j:j + 1, :] * down(x, width - 1 - j)
    return acc


def _causal_conv_bwd(x, dpre, w_ref, dx_ref, dw_ref, width, zero_tail=False):
    t = x.shape[0]
    down = (lambda a, s: pltpu.roll(a, s, 0)) if zero_tail else _shift_down
    up = (lambda a, s: pltpu.roll(a, t - s, 0)) if zero_tail else _shift_up
    dx = w_ref[width - 1:width, :] * dpre
    dw_ref[width - 1:width, :] = jnp.sum(dpre * x, axis=0, keepdims=True)
    for j in range(width - 1):
        s = width - 1 - j
        dx = dx + w_ref[j:j + 1, :] * up(dpre, s)
        dw_ref[j:j + 1, :] = jnp.sum(dpre * down(x, s), axis=0, keepdims=True)
    dx_ref[...] = dx.astype(dx_ref.dtype)


def _dsilu(x):
    sg = jax.nn.sigmoid(x)
    return sg * (1.0 + x * (1.0 - sg))


def _dn_conv_fwd(x, w):
    def body(x_ref, w_ref, y_ref):
        y_ref[...] = _silu(_causal_conv(x_ref[...], w_ref, 4, zero_tail=True))

    return _col_call("dn_conv_fwd", body, [x], [w], [F32], [], 256)[0]


def _dn_conv_bwd(x, w, dy):
    def body(x_ref, dy_ref, w_ref, dx_ref, dw_ref):
        xv = x_ref[...]
        dpre = dy_ref[...] * _dsilu(_causal_conv(xv, w_ref, 4, zero_tail=True))
        _causal_conv_bwd(xv, dpre, w_ref, dx_ref, dw_ref, 4, zero_tail=True)

    return _col_call("dn_conv_bwd", body, [x, dy], [w], [BF16], [4], 256)


def _ffn_glu_fwd(n2, w_gate_t, w_up_t, w, b, host=None):
    t, k = n2.shape
    c, cw = w_gate_t.shape[0], 256

    def body(n_ref, wg_ref, wu_ref, w_ref, b_ref, g_ref, u_ref, a_ref):
        nv = n_ref[...]
        g16 = _mm_nt(nv, wg_ref[...]).astype(BF16)
        u16 = _mm_nt(nv, wu_ref[...]).astype(BF16)
        g_ref[...] = g16
        u_ref[...] = u16
        gate = _causal_conv(g16.astype(F32), w_ref, 3) + b_ref[...]
        a_ref[...] = (_silu(gate) * u16.astype(F32)).astype(BF16)

    wspec = pl.BlockSpec((cw, k), lambda j: (j, 0))
    col = pl.BlockSpec((t, cw), lambda j: (0, j))
    in_specs = [pl.BlockSpec((t, k), lambda j: (0, 0)), wspec, wspec, pl.BlockSpec((w.shape[0], cw), lambda j: (0, j)),
                pl.BlockSpec((1, cw), lambda j: (0, j))]
    return _pcall(body, "ffn_glu_fwd", (c // cw,), in_specs, [col] * 3, [jax.ShapeDtypeStruct((t, c), BF16)] * 3,
                  [n2, w_gate_t, w_up_t, w, b], ("arbitrary",), host=host)


def _ffn_glu_bwd(gpre, up, dy16, w_down, w, b):
    t, k = dy16.shape
    c, cw = w_down.shape[0], 256

    def body(g_ref, u_ref, dy_ref, wd_ref, w_ref, b_ref, dg_ref, du_ref, dw_ref, db_ref):
        gv = g_ref[...].astype(F32)
        gate = _causal_conv(gv, w_ref, 3) + b_ref[...]
        da = _mm_nt(dy_ref[...], wd_ref[...])
        sg = jax.nn.sigmoid(gate)
        du_ref[...] = (da * (gate * sg)).astype(BF16)
        dgate = da * u_ref[...].astype(F32) * (sg * (1.0 + gate * (1.0 - sg)))
        db_ref[...] = jnp.sum(dgate, axis=0, keepdims=True)
        _causal_conv_bwd(gv, dgate, w_ref, dg_ref, dw_ref, 3)

    col = pl.BlockSpec((t, cw), lambda j: (0, j))
    taps = lambda rows: pl.BlockSpec((rows, cw), lambda j: (0, j))
    in_specs = [col, col, pl.BlockSpec((t, k), lambda j: (0, 0)), pl.BlockSpec((cw, k), lambda j: (j, 0)),
                taps(w.shape[0]), taps(1)]
    return _pcall(body, "ffn_glu_bwd", (c // cw,), in_specs, [col, col, taps(w.shape[0]), taps(1)],
                  [jax.ShapeDtypeStruct((t, c), BF16)] * 2 + [jax.ShapeDtypeStruct((w.shape[0], c), F32),
                                                             jax.ShapeDtypeStruct((1, c), F32)],
                  [gpre, up, dy16, w_down, w, b], ("arbitrary",))


def _dn_prep_consts(sa, sb, al, dt):
    return dict(sel_a=sa[...], sel_b=sb[...], alog=al[...], dtb=dt[...])


def _dn_prep_fwd(conv, ab, sel_a, sel_b, alog, dtb):
    def body(c_ref, ab_ref, sa, sb, al, dt, q_out, k_out, g_out, b_out):
        qc = tuple(c_ref[:, h * HEAD:(h + 1) * HEAD] for h in range(HEADS))
        kc = tuple(c_ref[:, DN_WIDTH + h * HEAD:DN_WIDTH + (h + 1) * HEAD] for h in range(HEADS))
        qs, ks, g, beta = _dn_prep_fn((qc, kc, ab_ref[...]), _dn_prep_consts(sa, sb, al, dt))
        for h in range(HEADS):
            q_out[:, h * HEAD:(h + 1) * HEAD] = qs[h]
            k_out[:, h * HEAD:(h + 1) * HEAD] = ks[h]
        g_out[...] = g
        b_out[...] = beta

    return _rows_call("dn_prep_fwd", body, [conv, ab], [sel_a, sel_b, alog, dtb], [(DN_WIDTH, F32)] * 4, [],
                      _row_tile(conv.shape[0], 4))


def _dn_prep_bwd(conv, ab, dq, dk, dv, dg, db, sel_a, sel_b, alog, dtb):
    def body(c_ref, ab_ref, dq_r, dk_r, dv_r, dg_r, db_r, sa, sb, al, dt, dc_out, dab_out, dal_out, ddt_out):
        qc = tuple(c_ref[:, h * HEAD:(h + 1) * HEAD] for h in range(HEADS))
        kc = tuple(c_ref[:, DN_WIDTH + h * HEAD:DN_WIDTH + (h + 1) * HEAD] for h in range(HEADS))
        consts = _dn_prep_consts(sa, sb, al, dt)
        sel = dict(sel_a=consts["sel_a"], sel_b=consts["sel_b"])
        _, vjp = jax.vjp(lambda rows, ad: _dn_prep_fn(rows, {**sel, **ad}), (qc, kc, ab_ref[...]),
                         dict(alog=consts["alog"], dtb=consts["dtb"]))
        cq = tuple(dq_r[:, h * HEAD:(h + 1) * HEAD] for h in range(HEADS))
        ck = tuple(dk_r[:, h * HEAD:(h + 1) * HEAD] for h in range(HEADS))
        (dqc, dkc, dab), dad = vjp((cq, ck, dg_r[...], db_r[...]))
        for h in range(HEADS):
            dc_out[:, h * HEAD:(h + 1) * HEAD] = dqc[h]
            dc_out[:, DN_WIDTH + h * HEAD:DN_WIDTH + (h + 1) * HEAD] = dkc[h]
        dc_out[:, 2 * DN_WIDTH:3 * DN_WIDTH] = dv_r[...]
        dab_out[...] = dab.astype(BF16)
        _accumulate(dal_out, dad["alog"])
        _accumulate(ddt_out, dad["dtb"])

    return _rows_call("dn_prep_bwd", body, [conv, ab, dq, dk, dv, dg, db], [sel_a, sel_b, alog, dtb],
                      [(3 * DN_WIDTH, F32), (HEAD, BF16)], [(1, DN_WIDTH), (1, DN_WIDTH)], _row_tile(conv.shape[0], 4))


def _chunk_batch(t):
    nc = t // CHUNK
    return nc // 2 if nc % 2 == 0 else nc


def _dn_chunk_specs(t, nb):
    rows = nb * CHUNK
    blk = pl.BlockSpec((rows, HEAD), lambda h, b: (b, h))
    vblk = pl.BlockSpec((rows, HEAD), lambda h, b: (b, 2 * HEADS + h))
    mat = pl.BlockSpec((nb, HEAD, HEAD), lambda h, b: (b, h, 0))
    return rows, blk, vblk, mat


def _dn_chunk_fwd(qn, kn, conv, g, beta, host=None):
    t = qn.shape[0]
    nb = _chunk_batch(t)
    rows, blk, vblk, mat = _dn_chunk_specs(t, nb)

    def body(q_ref, k_ref, v_ref, g_ref, b_ref, n_o, b_o, qe_o, oo_o, eg_o):
        r3 = lambda x: x.reshape(nb, CHUNK, x.shape[-1])
        n_mat, b_mat, q_eff, o_own, eg = _dn_chunk_fn(r3(q_ref[...]), r3(k_ref[...]), r3(v_ref[...]), r3(g_ref[...]),
                                                      r3(g_ref[:, 0:CHUNK]), r3(b_ref[...]))
        n_o[...] = n_mat
        b_o[...] = b_mat
        qe_o[...] = q_eff.reshape(rows, HEAD)
        oo_o[...] = o_own.reshape(rows, HEAD)
        eg_o[...] = jnp.broadcast_to(eg, (nb, HEAD, HEAD))

    nc = t // CHUNK
    mats = jax.ShapeDtypeStruct((nc, DN_WIDTH, HEAD), F32)
    rowsd = jax.ShapeDtypeStruct((t, DN_WIDTH), F32)
    return _pcall(body, "dn_chunk_fwd", (HEADS, t // rows), [blk, blk, vblk, blk, blk], [mat, mat, blk, blk, mat],
                  [mats, mats, rowsd, rowsd, mats], [qn, kn, conv, g, beta], ("arbitrary", "arbitrary"), host=host)


def _dn_chunk_bwd(qn, kn, conv, g, beta, sall, gall, dq_eff, do, host=None):
    t = qn.shape[0]
    nb = _chunk_batch(t)
    rows, blk, vblk, mat = _dn_chunk_specs(t, nb)

    def body(q_ref, k_ref, v_ref, g_ref, b_ref, s_ref, ga_ref, dqe_ref, do_ref, dq_o, dk_o, dv_o, dg_o, db_o):
        r3 = lambda x: x.reshape(nb, CHUNK, x.shape[-1])
        _, vjp = jax.vjp(_dn_chunk_fn, r3(q_ref[...]), r3(k_ref[...]), r3(v_ref[...]), r3(g_ref[...]),
                         r3(g_ref[:, 0:CHUNK]), r3(b_ref[...]))
        s, ga = s_ref[...], ga_ref[...]
        d_n = -_bmm_nt(ga, s)
        d_eg = jnp.sum(ga * s, axis=1, keepdims=True)
        dq, dk, dv, dg, dg64, db = vjp((d_n, ga, r3(dqe_ref[...]), r3(do_ref[...]), d_eg))
        for o_ref, val in zip((dq_o, dk_o, dv_o, dg_o, db_o), (dq, dk, dv, dg, db)):
            o_ref[...] = val.reshape(rows, HEAD)
        dg_o[:, 0:CHUNK] += dg64.reshape(rows, CHUNK)

    return _pcall(body, "dn_chunk_bwd", (HEADS, t // rows), [blk, blk, vblk, blk, blk, mat, mat, blk, blk], [blk] * 5,
                  [jax.ShapeDtypeStruct((t, DN_WIDTH), F32)] * 5, [qn, kn, conv, g, beta, sall, gall, dq_eff, do],
                  ("arbitrary", "arbitrary"), host=host)


def _dn_rec_fwd(n_mat, b_mat, eg, host=None):
    nc = n_mat.shape[0]
    nb = _chunk_batch(nc * CHUNK)
    spec = pl.BlockSpec((nb, DN_WIDTH, HEAD), lambda i: (i, 0, 0))

    def body(n_ref, b_ref, eg_ref, sall_ref, s_scr):
        @pl.when(pl.program_id(0) == 0)
        def _():
            s_scr[...] = jnp.zeros(s_scr.shape, F32)

        for j in range(nb):
            sall_ref[j] = s_scr[...]
            for h in range(HEADS):
                sl = slice(h * HEAD, (h + 1) * HEAD)
                s_scr[sl, :] = _dn_rec_fn(s_scr[sl, :], n_ref[j, sl, :], b_ref[j, sl, :],
                                          eg_ref[j, h * HEAD:h * HEAD + 1, :])

    return _pcall(body, "dn_rec_fwd", (nc // nb,), [spec] * 3, spec, jax.ShapeDtypeStruct((nc, DN_WIDTH, HEAD), F32),
                  [n_mat, b_mat, eg], ("arbitrary",), scratch_shapes=[pltpu.VMEM((DN_WIDTH, HEAD), F32)], host=host)


def _dn_rec_bwd(n_mat, eg, ds_out, host=None):
    nc = n_mat.shape[0]
    nb = _chunk_batch(nc * CHUNK)
    steps = nc // nb
    spec = pl.BlockSpec((nb, DN_WIDTH, HEAD), lambda i: (steps - 1 - i, 0, 0))

    def body(n_ref, eg_ref, dso_ref, gall_ref, g_scr):
        @pl.when(pl.program_id(0) == 0)
        def _():
            g_scr[...] = jnp.zeros(g_scr.shape, F32)

        for j in reversed(range(nb)):
            gall_ref[j] = g_scr[...]
            for h in range(HEADS):
                sl = slice(h * HEAD, (h + 1) * HEAD)
                gv = g_scr[sl, :]
                g_scr[sl, :] = (gv * eg_ref[j, h * HEAD:h * HEAD + 1, :] - _mm_tn(n_ref[j, sl, :], gv)
                                + dso_ref[j, sl, :])

    return _pcall(body, "dn_rec_bwd", (steps,), [spec] * 3, spec, jax.ShapeDtypeStruct((nc, DN_WIDTH, HEAD), F32),
                  [n_mat, eg, ds_out], ("arbitrary",), scratch_shapes=[pltpu.VMEM((DN_WIDTH, HEAD), F32)], host=host)


def _dn_o_fwd(sall, q_eff, o_own):
    t = q_eff.shape[0]
    nb = _chunk_batch(t)
    rows, blk, _, mat = _dn_chunk_specs(t, nb)

    def body(s_ref, qe_ref, oo_ref, o_ref):
        r3 = lambda x: x.reshape(nb, CHUNK, HEAD)
        o_ref[...] = _dn_o_fn(s_ref[...], r3(qe_ref[...]), r3(oo_ref[...])).reshape(rows, HEAD)

    return _pcall(body, "dn_o_fwd", (HEADS, t // rows), [mat, blk, blk], blk, jax.ShapeDtypeStruct((t, DN_WIDTH), F32),
                  [sall, q_eff, o_own], ("arbitrary", "arbitrary"))


def _dn_o_bwd(sall, q_eff, do, host=None):
    t = q_eff.shape[0]
    nb = _chunk_batch(t)
    rows, blk, _, mat = _dn_chunk_specs(t, nb)

    def body(s_ref, qe_ref, do_ref, dqe_ref, ds_ref):
        r3 = lambda x: x.reshape(nb, CHUNK, HEAD)
        dov = r3(do_ref[...])
        dqe_ref[...] = _bmm_nt(dov, s_ref[...]).reshape(rows, HEAD)
        ds_ref[...] = _bmm_tn(r3(qe_ref[...]), dov)

    nc = t // CHUNK
    return _pcall(body, "dn_o_bwd", (HEADS, t // rows), [mat, blk, blk], [blk, mat],
                  [jax.ShapeDtypeStruct((t, DN_WIDTH), F32), jax.ShapeDtypeStruct((nc, DN_WIDTH, HEAD), F32)],
                  [sall, q_eff, do], ("arbitrary", "arbitrary"), host=host)


def _adamw_update(w, g, m, v):
    m2 = ADAM_B1 * m + (1.0 - ADAM_B1) * g
    v2 = ADAM_B2 * v + (1.0 - ADAM_B2) * (g * g)
    m_hat = m2 / (1.0 - ADAM_B1 ** ADAM_STEP)
    v_hat = v2 / (1.0 - ADAM_B2 ** ADAM_STEP)
    return -ADAM_LR * (m_hat / (jnp.sqrt(v_hat) + ADAM_EPS) + ADAM_WD * w), m2, v2


def _adamw_small(ws, gs, ms, vs):
    n = len(ws)

    def body(*refs):
        for i in range(n):
            d, m2, v2 = _adamw_update(refs[i][...], refs[n + i][...], refs[2 * n + i][...], refs[3 * n + i][...])
            refs[4 * n + i][...] = d
            refs[5 * n + i][...] = m2
            refs[6 * n + i][...] = v2

    shapes = [jax.ShapeDtypeStruct(a.shape, F32) for a in ws]
    outs = pl.pallas_call(body, name="adamw_small", out_shape=shapes * 3,
                          compiler_params=pltpu.CompilerParams(vmem_limit_bytes=VMEM_LIMIT))(*ws, *gs, *ms, *vs)
    return outs[:n], outs[n:2 * n], outs[2 * n:]


def _adamw_call(name, ws, gs, ms, vs, host=None):
    k = len(ws)
    rows, cols = ws[0].shape
    assert all(w.shape == (rows, cols) for w in ws) and all(g.shape == gs[0].shape for g in gs)
    by_rows = rows % 8 == 0

    def body(*refs):
        for i in range(k):
            w_ref, g_ref, m_ref, v_ref, g_out, d_ref, m_out, v_out = refs[i::k]
            gv = g_ref[...] if by_rows else g_ref[0:rows, :]
            g_out[...] = gv
            d_ref[...], m_out[...], v_out[...] = _adamw_update(w_ref[...], gv, m_ref[...], v_ref[...])

    if by_rows:
        tr = _tile(rows, 256 // k, 8)
        spec = g_spec = pl.BlockSpec((tr, cols), lambda i: (i, 0))
        grid = (rows // tr,)
    else:
        tc = _tile(cols, 256, 128)
        spec = pl.BlockSpec((rows, tc), lambda j: (0, j))
        g_spec = pl.BlockSpec((gs[0].shape[0], tc), lambda j: (0, j))
        grid = (cols // tc,)
    outs = _pcall(body, name, grid, [spec] * k + [g_spec] * k + [spec] * 2 * k, [spec] * 4 * k,
                  [jax.ShapeDtypeStruct((rows, cols), F32)] * 4 * k, [*ws, *gs, *ms, *vs], ("arbitrary",), host=host)
    return [outs[i::k] for i in range(k)]


def _adamw_rows3d(name, w, g, m, v):
    rows, _, cols = w.shape
    tr = max(d for d in range(1, 129) if rows % d == 0)

    def body(w_ref, g_ref, m_ref, v_ref, d_ref, m_out, v_out):
        d_ref[...], m_out[...], v_out[...] = _adamw_update(w_ref[...], g_ref[...], m_ref[...], v_ref[...])

    spec = pl.BlockSpec((tr, 1, cols), lambda i: (i, 0, 0))
    return pl.pallas_call(body, name=name, grid=(rows // tr,), in_specs=[spec] * 4, out_specs=[spec] * 3,
                          out_shape=[jax.ShapeDtypeStruct(w.shape, F32)] * 3, compiler_params=_cparams(("arbitrary",)))(
                              w, g, m, v)


def _rope_tables(t):
    half = ROPE // 2
    inv_freq = np.float32(ROPE_THETA) ** (-np.arange(half, dtype=np.float32) / np.float32(half))
    ang = np.arange(t, dtype=np.float32)[:, None] * inv_freq[None, :].astype(np.float32)
    z = np.zeros((t, HEAD - ROPE), np.float32)
    cos = np.concatenate([np.cos(ang), np.cos(ang), z], axis=1).astype(np.float32)
    sin = np.concatenate([np.sin(ang), np.sin(ang), z], axis=1).astype(np.float32)
    k = np.arange(HEAD)[:, None]
    l = np.arange(HEAD)[None, :]
    perm = np.where((l < half) & (k == l + half), -1.0, 0.0) + np.where((l >= half) & (l < ROPE) & (k == l - half), 1.0, 0.0)
    return jnp.asarray(cos), jnp.asarray(sin), jnp.asarray(perm.astype(np.float32))


def _win_to_pad(w):
    z = lambda n: jnp.zeros((n, w.shape[1]), w.dtype)
    return jnp.concatenate([w[576:2112], w[2112:2624], w[0:256], w[256:512], w[512:576], z(64), w[2624:2632], z(120)],
                           axis=0)


def _qk_to_pad(w):
    w4 = w.reshape(HEADS, QK_DIM, w.shape[-1])
    return jnp.concatenate([w4, jnp.zeros((HEADS, QK_PAD - QK_DIM, w.shape[-1]), w.dtype)], axis=1).reshape(
        HEADS * QK_PAD, w.shape[-1])


def _qk_from_pad(g):
    return g.reshape(HEADS, QK_PAD, g.shape[-1])[:, :QK_DIM].reshape(HEADS * QK_DIM, g.shape[-1])


def _ff_to_pad(a, axis):
    shape = list(a.shape)
    shape[axis:axis + 1] = [N_CHIPS, FF_SHARD]
    a4 = a.reshape(shape)
    shape[axis + 1] = FF_BLOCK - FF_SHARD
    out = jnp.concatenate([a4, jnp.zeros(shape, a.dtype)], axis=axis + 1)
    shape[axis:axis + 2] = [D_FF_P]
    return out.reshape(shape)


def _ff_from_pad(a, axis):
    shape = list(a.shape)
    shape[axis:axis + 1] = [N_CHIPS, FF_BLOCK]
    a4 = lax.slice_in_dim(a.reshape(shape), 0, FF_SHARD, axis=axis + 1)
    shape[axis:axis + 2] = [D_FF]
    return a4.reshape(shape)


class _LocalPlan:
    def __init__(self, wt):
        self.wt, self.grads = wt, {}

    def weight(self, name):
        return self.wt[name]

    def host(self, point):
        return None

    def grad(self, name, value):
        self.grads[name] = value


def _local_step(x, tgt, wt, plan=None):
    plan = _LocalPlan(wt) if plan is None else plan
    s = x.shape[0]
    n_valid = N_META + s
    t = -(-n_valid // HEAD) * HEAD
    assert t - n_valid >= 3, "the DeltaNet conv kernels rely on at least three zero rows after the sequence"
    zpad = jnp.zeros((t - n_valid, D_MODEL), F32)
    h0 = jnp.concatenate([wt["meta_tokens"], x, zpad], axis=0)
    tgt_p = jnp.concatenate([jnp.zeros((N_META, D_MODEL), F32), tgt, zpad], axis=0)
    cos, sin, perm = _rope_tables(t)
    qn_w = jnp.concatenate([wt["q_norm_w"], jnp.zeros((1, QK_PAD - QK_DIM), F32)], axis=1)
    kn_w = jnp.concatenate([wt["k_norm_w"], jnp.zeros((1, QK_PAD - QK_DIM), F32)], axis=1)
    head_id = jnp.arange(DN_WIDTH)[None, :] // HEAD
    lane = jnp.arange(HEAD)[:, None]
    sel_a = (lane == head_id).astype(F32)
    sel_b = (lane == head_id + HEADS).astype(F32)
    alog = jnp.repeat(wt["dn_A_log"], HEAD, axis=1)
    dtb = jnp.repeat(wt["dn_dt_bias"], HEAD, axis=1)
    conv_w, conv_b = wt["ffn_conv_w"], wt["ffn_conv_b"]

    u = _rms_fwd("attn_norm_fwd", h0, wt["attn_norm_w"], host=plan.host("attn_norm_fwd"))
    win, wq, wkv = plan.weight("w_in_t"), plan.weight("w_q_t"), plan.weight("w_kv_t")
    proj = _matmul("in_proj", u, win, "nt", F32)
    z = (proj, DN_WIDTH, 3)
    q_lat, kv_lat, k_pe, ab = (proj, LORA, 8), (proj, LORA, 9), (proj, HEAD, 20), (proj, HEAD, 21)
    mla_consts = (wt["q_a_norm_w"], wq, wt["kv_a_norm_w"], wkv, qn_w, kn_w, perm)
    q, k, v = _mla_prep_fwd(q_lat, kv_lat, k_pe, cos, sin, *mla_consts)
    o_mla = _attn_fwd(q, k, v, host=plan.host("attn_fwd"))
    conv = _dn_conv_fwd(proj, wt["dn_conv_w"])
    dn_consts = (sel_a, sel_b, alog, dtb)
    qn, kn, g, beta = _dn_prep_fwd(conv, ab, *dn_consts)
    n_mat, b_mat, q_eff, o_own, eg = _dn_chunk_fwd(qn, kn, conv, g, beta, host=plan.host("dn_chunk_fwd"))
    sall = _dn_rec_fwd(n_mat, b_mat, eg)
    o_dn = _dn_o_fwd(sall, q_eff, o_own)
    w_out = plan.weight("w_out")
    mixed, h1, n2 = _mix_out_proj(o_mla, o_dn, z, wt["mla_out_norm_w"], wt["dn_out_norm_w"], w_out, h0, wt["ffn_norm_w"])
    w_gate, w_up = plan.weight("w_gate_t"), plan.weight("w_up_t")
    gpre, up, act = _ffn_glu_fwd(n2, w_gate, w_up, conv_w, conv_b, host=plan.host("ffn_glu_fwd"))
    w_down = plan.weight("w_down")
    dy, dy16, sq = _down_proj_loss(act, w_down, h1, tgt_p, n_valid)

    grads = {}
    plan.grad("w_down", _matmul("down_dw", act, dy16, "tn", BF16))
    dgpre, dup, grads["ffn_conv_w"], grads["ffn_conv_b"] = _ffn_glu_bwd(gpre, up, dy16, w_down, conv_w, conv_b)
    plan.grad("w_gate_t", _matmul("gate_dw", dgpre, n2, "tn", BF16))
    plan.grad("w_up_t", _matmul("up_dw", dup, n2, "tn", BF16))
    dh1, dh1_16, grads["ffn_norm_w"] = _ffn_in_bwd(dgpre, dup, w_gate, w_up, h1, dy, wt["ffn_norm_w"],
                                                   host=plan.host("ffn_in_bwd"))
    plan.grad("w_out", _matmul("out_dw", mixed, dh1_16, "tn", BF16))
    do_mla, do_dn, dz, grads["mla_out_norm_w"], grads["dn_out_norm_w"] = _mix_out_bwd(
        o_mla, o_dn, z, dh1_16, w_out, wt["mla_out_norm_w"], wt["dn_out_norm_w"], host=plan.host("mix_out_bwd"))
    dq_eff, ds_out = _dn_o_bwd(sall, q_eff, do_dn)
    gall = _dn_rec_bwd(n_mat, eg, ds_out)
    dqn, dkn, dv_dn, dg, dbeta = _dn_chunk_bwd(qn, kn, conv, g, beta, sall, gall, dq_eff, do_dn,
                                               host=plan.host("dn_chunk_bwd"))
    dconv, dab, dalog, ddtb = _dn_prep_bwd(conv, ab, dqn, dkn, dv_dn, dg, dbeta, *dn_consts)
    grads["dn_A_log"] = jnp.sum(dalog.reshape(HEADS, HEAD), axis=1)[None, :]
    grads["dn_dt_bias"] = jnp.sum(ddtb.reshape(HEADS, HEAD), axis=1)[None, :]
    ddn_pre, grads["dn_conv_w"] = _dn_conv_bwd(proj, wt["dn_conv_w"], dconv)
    dq, dk, dv = _attn_bwd(q, k, v, do_mla, host=plan.host("attn_bwd"))
    dq_lat, dkv_lat, dk_pe, dqa, dwq, dkva, dwkv, dqnw, dknw = _mla_prep_bwd(
        q_lat, kv_lat, k_pe, cos, sin, dq, dk, dv, *mla_consts, host=plan.host("mla_prep_bwd"))
    grads["q_a_norm_w"], grads["kv_a_norm_w"] = dqa, dkva
    plan.grad("w_q_t", dwq)
    plan.grad("w_kv_t", dwkv)
    grads["q_norm_w"], grads["k_norm_w"] = dqnw[:, :QK_DIM], dknw[:, :QK_DIM]
    dproj = [ddn_pre, dz, dq_lat, dkv_lat, dk_pe, dab]
    plan.grad("w_in_t", _in_proj_bwd_w(dproj, u))
    du = _in_proj_bwd_x(dproj, win, host=plan.host("in_dx"))
    dh0, _, grads["attn_norm_w"] = _rms_bwd("attn_norm_bwd", h0, wt["attn_norm_w"], [du], dh1,
                                            host=plan.host("attn_norm_bwd"))
    grads["meta_tokens"] = dh0[0:N_META]
    if isinstance(plan, _LocalPlan):
        grads.update(plan.grads)
    return sq, dh0[N_META:n_valid], grads


def _mesh_pos():
    return lax.axis_index("x"), lax.axis_index("y"), lax.axis_index("c")


def _other_chips(x, y):
    return [(1 - x, y), (x, 1 - y), (1 - x, 1 - y)]


def _remote(src, dst, send_sems, recv_sems, k, to):
    return pltpu.make_async_remote_copy(src_ref=src, dst_ref=dst, send_sem=send_sems.at[k], recv_sem=recv_sems.at[k],
                                        device_id=to, device_id_type=MESH)


SIBLING_ID, CHIPS_ID, GATHER_ID, ALL_ID = 1, 2, 3, 4


def _sibling_peer():
    x, y, c = _mesh_pos()
    return [(x, y, 1 - c)]


def _chip_peers():
    x, y, c = _mesh_pos()
    return [(qx, qy, c) for qx, qy in _other_chips(x, y)]


def _copies_exchange(make, ins, out_shape, nsem, peers=None, cid=None):
    def prog(in_refs, out_refs, send_sems, recv_sems):
        copies = make(in_refs, out_refs, send_sems, recv_sems)

        def start():
            for cp in copies:
                cp.start()

        def finish():
            for cp in copies:
                cp.wait()

        return start, finish

    return _Exchange(prog, ins, out_shape, nsem, peers, cid)


def _all_gather(shards):
    def prog(srcs, dsts, send_sems, recv_sems):
        x, y, c = _mesh_pos()
        p = 2 * x + y
        sibling = (x, y, 1 - c)
        chips = _other_chips(x, y)
        bufs = tuple((s, d, s.shape[0] // 2) for s, d in zip(srcs, dsts))

        def half(ref, rows, which):
            return ref.at[pl.ds(which * rows, rows), :]

        def copy(i, k, src, dst, to):
            return _remote(src, dst, send_sems, recv_sems, 6 * i + k, to)

        sends = [copy(i, j, half(src, rows, c), half(dst.at[p], rows, c), (*chip, c))
                 for i, (src, dst, rows) in enumerate(bufs) for j, chip in enumerate(chips)]

        def start():
            for cp in sends:
                cp.start()

        def finish():
            passed = []
            for i, (src, dst, rows) in enumerate(bufs):
                for j, (qx, qy) in enumerate(chips):
                    block = half(dst.at[2 * qx + qy], rows, c)
                    copy(i, j, block, block, (x, y, c)).wait_recv()
                    fwd = copy(i, 3 + j, block, block, sibling)
                    fwd.start()
                    passed.append(fwd)
            for i, (src, dst, rows) in enumerate(bufs):
                for j, (qx, qy) in enumerate(chips):
                    block = half(dst.at[2 * qx + qy], rows, 1 - c)
                    copy(i, 3 + j, block, block, (x, y, c)).wait_recv()
            for cp in sends + passed:
                cp.wait_send()

        return start, finish

    return _Exchange(prog, shards, [jax.ShapeDtypeStruct((N_CHIPS, *s.shape), s.dtype) for s in shards], 6 * len(shards),
                     lambda: _sibling_peer() + _chip_peers(), GATHER_ID)


def _all_gather_small(block):
    def make(srcs, dsts, send_sems, recv_sems):
        x, y, c = _mesh_pos()
        return [_remote(srcs[0], dsts[0].at[2 * x + y], send_sems, recv_sems, k, (qx, qy, c))
                for k, (qx, qy) in enumerate(_other_chips(x, y))]

    return _copies_exchange(make, [block], [jax.ShapeDtypeStruct((N_CHIPS, *block.shape), block.dtype)], 3, _chip_peers,
                            CHIPS_ID)


def _gathered(ex):
    p = 2 * lax.axis_index("x") + lax.axis_index("y")
    return [lax.dynamic_update_slice(g, s[None], (p, 0, 0)) for g, s in zip(ex.outs, ex.ins)]


def _rs_to_sibling(bufs):
    def make(srcs, dsts, send_sems, recv_sems):
        x, y, c = _mesh_pos()
        copies = []
        for i, (src, dst) in enumerate(zip(srcs, dsts)):
            half = src.shape[1] // 2
            copies.append(_remote(src.at[:, pl.ds((1 - c) * half, half), :], dst, send_sems, recv_sems, i, (x, y, 1 - c)))
        return copies

    return _copies_exchange(make, bufs,
                            [jax.ShapeDtypeStruct((N_CHIPS, b.shape[1] // 2, b.shape[2]), b.dtype) for b in bufs],
                            len(bufs), _sibling_peer, SIBLING_ID)


def _rs_pair_add(name, bufs, gots, c, out_dtype):
    n = len(bufs)

    def body(c_ref, *refs):
        for a_ref, b_ref, o_ref in zip(refs[:n], refs[n:2 * n], refs[2 * n:]):
            o_ref[...] = (a_ref[...].astype(F32) + b_ref[...].astype(F32)).astype(out_dtype)

    mine = [pl.BlockSpec((None, g.shape[1], g.shape[2]), lambda j, cr: (j, cr[0], 0)) for g in gots]
    whole = [pl.BlockSpec((None, g.shape[1], g.shape[2]), lambda j, cr: (j, 0, 0)) for g in gots]
    return pl.pallas_call(
        body, name=name,
        grid_spec=pltpu.PrefetchScalarGridSpec(num_scalar_prefetch=1, grid=(N_CHIPS,), in_specs=mine + whole, out_specs=whole),
        out_shape=[jax.ShapeDtypeStruct(g.shape, out_dtype) for g in gots],
        compiler_params=_cparams(("arbitrary",)))(c, *bufs, *gots)


def _rs_to_chips(accs):
    def make(srcs, dsts, send_sems, recv_sems):
        x, y, c = _mesh_pos()
        return [_remote(src.at[2 * qx + qy], dst.at[k], send_sems, recv_sems, 3 * i + k, (qx, qy, c))
                for i, (src, dst) in enumerate(zip(srcs, dsts)) for k, (qx, qy) in enumerate(_other_chips(x, y))]

    return _copies_exchange(make, accs, [jax.ShapeDtypeStruct((3, a.shape[1], a.shape[2]), a.dtype) for a in accs],
                            3 * len(accs), _chip_peers, CHIPS_ID)


def _rs_chip_add(name, accs, gots, p):
    n = len(accs)
    slot = (0, 1, 0, 2)

    def body(p_ref, *refs):
        me = p_ref[0]
        for own_ref, got_ref, o_ref in zip(refs[:n], refs[n:2 * n], refs[2 * n:]):
            total = None
            for chip in range(N_CHIPS):
                val = own_ref[...].astype(F32)
                for e in (1, 2, 3):
                    val = jnp.where((chip ^ me) == e, got_ref[slot[e]].astype(F32), val)
                total = val if total is None else total + val
            o_ref[...] = total

    own = [pl.BlockSpec((None, a.shape[1], a.shape[2]), lambda i, pr: (pr[0], 0, 0)) for a in accs]
    got = [pl.BlockSpec(g.shape, lambda i, pr: (0, 0, 0)) for g in gots]
    out = [pl.BlockSpec((a.shape[1], a.shape[2]), lambda i, pr: (0, 0)) for a in accs]
    return pl.pallas_call(
        body, name=name,
        grid_spec=pltpu.PrefetchScalarGridSpec(num_scalar_prefetch=1, grid=(1,), in_specs=own + got, out_specs=out),
        out_shape=[jax.ShapeDtypeStruct((a.shape[1], a.shape[2]), F32) for a in accs],
        compiler_params=_cparams(("arbitrary",)))(p, *accs, *gots)


def _rs_share(ress):
    def make(srcs, dsts, send_sems, recv_sems):
        x, y, c = _mesh_pos()
        return [_remote(src, dst, send_sems, recv_sems, i, (x, y, 1 - c)) for i, (src, dst) in enumerate(zip(srcs, dsts))]

    return _copies_exchange(make, ress, [jax.ShapeDtypeStruct(r.shape, F32) for r in ress], len(ress), _sibling_peer,
                            SIBLING_ID)


def _shared(ex):
    south = lax.axis_index("c") == 0
    return [jnp.concatenate([jnp.where(south, r, g), jnp.where(south, g, r)], axis=0) for r, g in zip(ex.ins, ex.outs)]


def _all_to_all_devices(vec):
    def others():
        x, y, c = _mesh_pos()
        return [((1 - x if r & 4 else x), (1 - y if r & 2 else y), (1 - c if r & 1 else c)) for r in range(1, 8)]

    def make(srcs, dsts, send_sems, recv_sems):
        x, y, c = _mesh_pos()
        me = 4 * x + 2 * y + c
        return [_remote(srcs[0], dsts[0].at[me], send_sems, recv_sems, r, peer) for r, peer in enumerate(others())]

    return _copies_exchange(make, [vec], [jax.ShapeDtypeStruct((8, *vec.shape), vec.dtype)], 7, others, ALL_ID)


def _sum_devices(stack):
    def body(s_ref, o_ref):
        total = s_ref[0]
        for d in range(1, 8):
            total = total + s_ref[d]
        o_ref[...] = total

    return pl.pallas_call(body, name="sum_devices", out_shape=jax.ShapeDtypeStruct(stack.shape[1:], F32),
                          compiler_params=pltpu.CompilerParams(vmem_limit_bytes=VMEM_LIMIT))(stack)


def _pad_rows(flat, rows):
    return jnp.concatenate([flat, jnp.zeros((rows * LANES - flat.shape[0],), flat.dtype)]).reshape(rows, LANES)


def _unshard(g4, shape, axis):
    a = g4.reshape(N_CHIPS, *shape)
    if axis == 0:
        return a.reshape(N_CHIPS * shape[0], shape[1])
    return jnp.transpose(a, (1, 0, 2)).reshape(shape[0], N_CHIPS * shape[1])


def _pad_axis0(a, rows):
    return jnp.concatenate([a, jnp.zeros((rows - a.shape[0], *a.shape[1:]), a.dtype)], axis=0)


def _shard_to_strip(name, w):
    _, (shape, axis, rows) = name, {n: (s, ax, r) for n, s, ax, r in BIG}[name]
    w2 = w.reshape(shape).astype(BF16)
    return _pad_axis0(w2.T if axis == 1 else w2, rows)


LOCAL_NAME = dict(w_in="w_in_t", w_q_b="w_q_t", w_kv_b="w_kv_t", w_out="w_out", w_gate="w_gate_t", w_up="w_up_t",
                  w_down="w_down")


WIN_SEGMENTS = ((576, 2112, 0), (2112, 2624, 1536), (0, 256, 2048), (256, 512, 2304), (512, 576, 2560), (2624, 2632, 2688))


def _strips_to_weight(name, g4):
    if name == "w_in":
        return _win_to_pad(g4[:, :IN_SHARD].reshape(IN_COLS, D_MODEL))
    if name == "w_q_b":
        return _qk_to_pad(g4.reshape(HEADS * QK_DIM, LORA))
    return g4.reshape(N_CHIPS * g4.shape[1], g4.shape[2])


def _grad_to_strips(name, g):
    if name == "w_in":
        strips = []
        for q in range(N_CHIPS):
            pieces = []
            for a, b, local in sorted(WIN_SEGMENTS):
                s, e = max(a, q * IN_SHARD), min(b, (q + 1) * IN_SHARD)
                if s < e:
                    pieces.append(g[local + s - a:local + e - a])
            pieces.append(jnp.zeros((IN_SHARD_P - IN_SHARD, D_MODEL), g.dtype))
            strips.append(jnp.concatenate(pieces, axis=0))
        return jnp.stack(strips)
    if name == "w_q_b":
        return _qk_from_pad(g).reshape(N_CHIPS, QK_DIM, LORA)
    return g.reshape(N_CHIPS, g.shape[0] // N_CHIPS, g.shape[1])


class _MeshPlan:
    LATE = dict(attn_norm_fwd=("w_in", "w_q_b", "w_kv_b"), attn_fwd=("w_up",), dn_chunk_fwd=("w_out", "w_gate"),
                ffn_glu_fwd=("w_down",))
    GROUP_A = ("w_down", "w_gate", "w_up", "w_out")
    GROUP_B = ("w_in", "w_q_b", "w_kv_b")

    def __init__(self, w):
        x, y, c = _mesh_pos()
        self.ci = jnp.reshape(c, (1,)).astype(jnp.int32)
        self.pi = jnp.reshape(2 * x + y, (1,)).astype(jnp.int32)
        self.strip = {n: _shard_to_strip(n, w[n]) for n, _, _, _ in BIG}
        self.gathers, self.weights, self.g, self.acc, self.reduced = {}, {}, {}, {}, {}
        self.sibs, self.sib, self.chip, self.share = [], None, None, None

    def gather_small(self, small):
        ex = _all_gather_small(small)
        ex.run("all_gather_small")
        return _gathered(ex)[0]

    def weight(self, local_name):
        if local_name not in self.weights:
            for point, (names, ex) in list(self.gathers.items()):
                if ex.outs is not None:
                    for n, g4 in zip(names, _gathered(ex)):
                        self.weights[LOCAL_NAME[n]] = _strips_to_weight(n, g4)
                    del self.gathers[point]
        return self.weights[local_name]

    def grad(self, local_name, value):
        name = {v: k for k, v in LOCAL_NAME.items()}[local_name]
        self.g[name] = _grad_to_strips(name, value)

    def _pair_add(self, names, gots):
        accs = _rs_pair_add("rs_pair_add_" + names[0], [self.g[n] for n in names], gots, self.ci, BF16)
        self.acc.update(zip(names, accs))

    def _chip_add(self, names, chip):
        return _rs_chip_add("rs_chip_add_" + names[0], [self.acc[n] for n in names], chip.outs, self.pi)

    def _take_shared(self, names, share):
        for n, strip in zip(names, _shared(share)):
            self.reduced[n] = strip

    def host(self, point):
        a, b = self.GROUP_A, self.GROUP_B
        if point in self.LATE:
            names = self.LATE[point]
            ex = _all_gather([self.strip[n] for n in names])
            self.gathers[point] = (names, ex)
            return ex
        if point in ("ffn_in_bwd", "mix_out_bwd"):
            names = dict(ffn_in_bwd=a[:3], mix_out_bwd=a[3:])[point]
            ex = _rs_to_sibling([self.g[n] for n in names])
            self.sibs.append(ex)
            return ex
        if point == "dn_chunk_bwd":
            self._pair_add(a, [o for ex in self.sibs for o in ex.outs])
            self.chip1 = _rs_to_chips([self.acc[n] for n in a[:2]])
            return self.chip1
        if point == "attn_bwd":
            self.chip2 = _rs_to_chips([self.acc[n] for n in a[2:]])
            return self.chip2
        if point == "mla_prep_bwd":
            ress = self._chip_add(a[:2], self.chip1) + self._chip_add(a[2:], self.chip2)
            self.share = _rs_share(ress)
            return self.share
        if point == "in_dx":
            self._take_shared(a, self.share)
            self.sib = _rs_to_sibling([self.g[n] for n in b])
            return self.sib
        if point == "attn_norm_bwd":
            self._pair_add(b, self.sib.outs)
            self.chip = _rs_to_chips([self.acc[n] for n in b])
            return self.chip
        return None

    def last_share(self):
        self.share = _rs_share(self._chip_add(self.GROUP_B, self.chip))
        return self.share

    def finish(self):
        self._take_shared(self.GROUP_B, self.share)
        return self.reduced


def kernel(x, meta_tokens, attn_norm_w, w_in, q_a_norm_w, w_q_b, kv_a_norm_w, w_kv_b, q_norm_w, k_norm_w, mla_out_norm_w, dn_conv_w, dn_A_log, dn_dt_bias, dn_out_norm_w, w_out, ffn_norm_w, w_gate, w_up, ffn_conv_w, ffn_conv_b, w_down, loss_target, m_meta_tokens, m_attn_norm_w, m_w_in, m_q_a_norm_w, m_w_q_b, m_kv_a_norm_w, m_w_kv_b, m_q_norm_w, m_k_norm_w, m_mla_out_norm_w, m_dn_conv_w, m_dn_A_log, m_dn_dt_bias, m_dn_out_norm_w, m_w_out, m_ffn_norm_w, m_w_gate, m_w_up, m_ffn_conv_w, m_ffn_conv_b, m_w_down, v_meta_tokens, v_attn_norm_w, v_w_in, v_q_a_norm_w, v_w_q_b, v_kv_a_norm_w, v_w_kv_b, v_q_norm_w, v_k_norm_w, v_mla_out_norm_w, v_dn_conv_w, v_dn_A_log, v_dn_dt_bias, v_dn_out_norm_w, v_w_out, v_ffn_norm_w, v_w_gate, v_w_up, v_ffn_conv_w, v_ffn_conv_b, v_w_down):
    local = dict(locals())
    w = {n: local[n] for n in WEIGHTS}
    m = {n: local["m_" + n] for n in WEIGHTS}
    v = {n: local["v_" + n] for n in WEIGHTS}
    p = 2 * lax.axis_index("x") + lax.axis_index("y")

    plan = _MeshPlan(w)
    wf = _pad_rows(jnp.concatenate([w[n].reshape(-1) for n, _, _ in SMALL_SHARDED]), SMALL_ROWS)
    gf = plan.gather_small(wf).reshape(N_CHIPS, -1)
    full = {}
    off = 0
    for n, s, ax in SMALL_SHARDED:
        full[n] = _unshard(gf[:, off:off + s[0] * s[1]], s, ax)
        off += s[0] * s[1]
    for n, _ in REPLICATED:
        full[n] = w[n]
    full["ffn_conv_w"] = _ff_to_pad(full["ffn_conv_w"], 1)
    full["ffn_conv_b"] = _ff_to_pad(full["ffn_conv_b"], 1)

    sq, grad_x, g = _local_step(x[0], loss_target[0], full, plan)
    g["ffn_conv_w"] = _ff_from_pad(g["ffn_conv_w"], 1)
    g["ffn_conv_b"] = _ff_from_pad(g["ffn_conv_b"], 1)

    small_all = [n for n, _, _ in SMALL_SHARDED] + [n for n, _ in REPLICATED]
    vec = jnp.concatenate([g[n].reshape(-1) for n in small_all] + [jnp.reshape(0.5 / D_MODEL * jnp.sum(sq), (1,))])
    vec = _pad_rows(vec, -(-vec.shape[0] // (8 * LANES)) * 8)
    a2a = _all_to_all_devices(vec)

    gs, delta, new_m, new_v = {}, {}, {}, {}
    big = {n: (s, ax) for n, s, ax, _ in BIG}

    def adamw_big(names, strips, host=None):
        n = names[0]
        s, ax = big[n]
        if ax == 1 and s[1] % 8:
            there = lambda a: jnp.transpose(a, (2, 0, 1))
            back = lambda a: jnp.transpose(a, (1, 2, 0))
            g3 = strips[n][:s[1]].reshape(s[1], 1, s[0])
            d2, m2, v2 = _adamw_rows3d("adamw_" + n, there(w[n]), g3, there(m[n]), there(v[n]))
            gs[n], delta[n], new_m[n], new_v[n] = back(g3), back(d2), back(m2), back(v2)
            return

        def there(n, a):
            s, ax = big[n]
            return a.reshape(s).T if ax == 1 else a.reshape(s)

        def back(n, a):
            return (a.T if big[n][1] == 1 else a).reshape(w[n].shape)

        outs = _adamw_call("adamw_" + n, [there(n, w[n]) for n in names], [strips[n] for n in names],
                           [there(n, m[n]) for n in names], [there(n, v[n]) for n in names], host=host)
        for n, (g2, d2, m2, v2) in zip(names, outs):
            gs[n], delta[n], new_m[n], new_v[n] = back(n, g2), back(n, d2), back(n, m2), back(n, v2)

    adamw_big(("w_down", "w_gate"), plan.reduced, host=a2a)
    adamw_big(("w_up",), plan.reduced, host=plan.last_share())
    adamw_big(("w_out",), plan.reduced)
    strips = plan.finish()
    for n in plan.GROUP_B:
        adamw_big((n,), strips)
    me = 4 * lax.axis_index("x") + 2 * lax.axis_index("y") + lax.axis_index("c")
    red = _sum_devices(lax.dynamic_update_slice(a2a.outs[0], vec[None], (me, 0, 0))).reshape(-1)
    off = 0
    for n in small_all:
        tot = red[off:off + g[n].size].reshape(g[n].shape)
        off += g[n].size
        shard = {sn: (s, ax) for sn, s, ax in SMALL_SHARDED}.get(n)
        if shard is not None:
            tot = lax.dynamic_slice_in_dim(tot, p * shard[0][1], shard[0][1], axis=1)
        gs[n] = tot
    loss = red[off]
    two_d = lambda a: a.reshape(a.shape[-2], a.shape[-1])
    outs = _adamw_small([two_d(w[n]) for n in small_all], [two_d(gs[n]) for n in small_all],
                        [two_d(m[n]) for n in small_all], [two_d(v[n]) for n in small_all])
    for i, n in enumerate(small_all):
        for dst, src in ((delta, outs[0]), (new_m, outs[1]), (new_v, outs[2])):
            dst[n] = src[i].reshape(w[n].shape)

    grad_out = [gs[n].reshape(w[n].shape) for n in WEIGHTS]
    return (loss, grad_x[None], *grad_out, *[delta[n] for n in WEIGHTS], *[new_m[n] for n in WEIGHTS],
            *[new_v[n] for n in WEIGHTS])
```

```python
import functools
import math

import jax
import jax.numpy as jnp
import numpy as np
from jax import lax
from jax.experimental import pallas as pl
from jax.experimental.pallas import tpu as pltpu

F32 = jnp.float32
BF16 = jnp.bfloat16
HI = lax.Precision.HIGHEST
MESH = pl.DeviceIdType.MESH

N_META = 16
D_MODEL = 1024
HEADS = 4
HEAD = 128
ROPE = 64
QK_DIM = HEAD + ROPE
QK_PAD = 2 * HEAD
LORA = 256
DN_WIDTH = HEADS * HEAD
CHUNK = 64
D_FF = 2816
N_CHIPS = 4
FF_SHARD = D_FF // N_CHIPS
FF_BLOCK = 768
D_FF_P = N_CHIPS * FF_BLOCK
IN_COLS = 2632
IN_SHARD = IN_COLS // N_CHIPS
IN_SHARD_P = 672
NORM_EPS = 1e-6
ROPE_THETA = 10000.0
LANES = 512

ADAM_LR, ADAM_B1, ADAM_B2, ADAM_EPS, ADAM_WD, ADAM_STEP = 0.001, 0.9, 0.999, 1e-08, 0.01, 10

VMEM_LIMIT = 56 * 1024 * 1024

BIG = (("w_in", (1024, 658), 1, IN_SHARD_P), ("w_q_b", (256, 192), 1, 192), ("w_kv_b", (256, 256), 1, 256),
       ("w_out", (256, 1024), 0, 256), ("w_gate", (1024, 704), 1, FF_BLOCK), ("w_up", (1024, 704), 1, FF_BLOCK),
       ("w_down", (704, 1024), 0, FF_BLOCK))
SMALL_SHARDED = (("meta_tokens", (16, 256), 1), ("dn_conv_w", (4, 384), 1), ("ffn_conv_w", (3, 704), 1))
REPLICATED = (("attn_norm_w", 1024), ("q_a_norm_w", 256), ("kv_a_norm_w", 256), ("q_norm_w", 192), ("k_norm_w", 192),
              ("mla_out_norm_w", 128), ("dn_A_log", 4), ("dn_dt_bias", 4), ("dn_out_norm_w", 128), ("ffn_norm_w", 1024),
              ("ffn_conv_b", 2816))
WEIGHTS = ("meta_tokens", "attn_norm_w", "w_in", "q_a_norm_w", "w_q_b", "kv_a_norm_w", "w_kv_b", "q_norm_w", "k_norm_w",
           "mla_out_norm_w", "dn_conv_w", "dn_A_log", "dn_dt_bias", "dn_out_norm_w", "w_out", "ffn_norm_w", "w_gate",
           "w_up", "ffn_conv_w", "ffn_conv_b", "w_down")

SMALL_ROWS = 16


def _cparams(sem):
    return pltpu.CompilerParams(dimension_semantics=sem, vmem_limit_bytes=VMEM_LIMIT)


class _Exchange:
    def __init__(self, prog, ins, out_shape, nsem, peers=None, cid=None):
        self.prog, self.ins, self.out_shape, self.nsem = prog, list(ins), list(out_shape), nsem
        self.peers, self.cid = peers, cid
        self.outs = None

    def sems(self):
        return [pltpu.SemaphoreType.DMA((self.nsem,)), pltpu.SemaphoreType.DMA((self.nsem,))]

    def programs(self, in_refs, out_refs, send_sems, recv_sems):
        start, finish = self.prog(in_refs, out_refs, send_sems, recv_sems)
        if self.cid is None:
            return start, finish
        peers = self.peers()

        def shake_and_start():
            barrier = pltpu.get_barrier_semaphore()
            for peer in peers:
                pl.semaphore_signal(barrier, inc=1, device_id=peer, device_id_type=MESH)
            pl.semaphore_wait(barrier, len(peers))
            start()

        return shake_and_start, finish

    def cparams(self, **kw):
        return pltpu.CompilerParams(has_side_effects=True, collective_id=self.cid, **kw)

    def run(self, name):
        any_spec = pl.BlockSpec(memory_space=pl.ANY)
        n = len(self.ins)

        def body(*refs):
            start, finish = self.programs(refs[:n], refs[n:-2], refs[-2], refs[-1])
            start()
            finish()

        self.outs = pl.pallas_call(
            body, name=name, in_specs=[any_spec] * n, out_specs=[any_spec] * len(self.out_shape),
            out_shape=self.out_shape, scratch_shapes=self.sems(), compiler_params=self.cparams())(*self.ins)
        return self.outs


def _pcall(body, name, grid, in_specs, out_specs, out_shape, args, sem, scratch_shapes=(), host=None):
    single = not isinstance(out_shape, (list, tuple))
    out_specs, out_shape = ([out_specs], [out_shape]) if single else (list(out_specs), list(out_shape))
    if host is None:
        outs = pl.pallas_call(body, name=name, grid=grid, in_specs=list(in_specs), out_specs=out_specs, out_shape=out_shape,
                              scratch_shapes=list(scratch_shapes), compiler_params=_cparams(sem))(*args)
        return outs[0] if single else outs
    any_spec = pl.BlockSpec(memory_space=pl.ANY)
    n_in, n_out, n_scr, nx_in, nx_out = len(in_specs), len(out_specs), len(scratch_shapes), len(host.ins), len(host.out_shape)

    def hosted(*refs):
        c_in, x_in = refs[:n_in], refs[n_in:n_in + nx_in]
        o0 = n_in + nx_in
        c_out, x_out = refs[o0:o0 + n_out], refs[o0 + n_out:o0 + n_out + nx_out]
        s0 = o0 + n_out + nx_out
        start, finish = host.programs(x_in, x_out, refs[s0 + n_scr], refs[s0 + n_scr + 1])
        first = functools.reduce(jnp.logical_and, [pl.program_id(d) == 0 for d in range(len(grid))])
        last = functools.reduce(jnp.logical_and, [pl.program_id(d) == grid[d] - 1 for d in range(len(grid))])
        pl.when(first)(start)
        body(*c_in, *c_out, *refs[s0:s0 + n_scr])
        pl.when(last)(finish)

    outs = pl.pallas_call(
        hosted, name=name, grid=grid, in_specs=list(in_specs) + [any_spec] * nx_in,
        out_specs=out_specs + [any_spec] * nx_out, out_shape=out_shape + host.out_shape,
        scratch_shapes=list(scratch_shapes) + host.sems(),
        compiler_params=host.cparams(dimension_semantics=sem, vmem_limit_bytes=VMEM_LIMIT))(*args, *host.ins)
    host.outs = outs[n_out:]
    return outs[0] if single else outs[:n_out]


NN, NT, TN = ((1,), (0,)), ((1,), (1,)), ((0,), (0,))


def _shift_dims(dims, batch):
    if not batch:
        return (dims, ((), ()))
    return (((dims[0][0] + 1,), (dims[1][0] + 1,)), ((0,), (0,)))


def _make_mm(dims, exact, batch=False):
    def raw(a, b, d):
        dn = _shift_dims(d, batch)
        if exact == "split_lhs":
            ah, bh = a.astype(BF16), b.astype(BF16)
            al = (a - ah.astype(F32)).astype(BF16)
            return lax.dot_general(ah, bh, dn, preferred_element_type=F32) + lax.dot_general(al, bh, dn,
                                                                                              preferred_element_type=F32)
        if exact == "split":
            ah, bh = a.astype(BF16), b.astype(BF16)
            al, bl = (a - ah.astype(F32)).astype(BF16), (b - bh.astype(F32)).astype(BF16)
            dot = lambda p, q: lax.dot_general(p, q, dn, preferred_element_type=F32)
            return dot(ah, bh) + (dot(ah, bl) + dot(al, bh))
        if exact:
            return lax.dot_general(a.astype(F32), b.astype(F32), dn, precision=HI, preferred_element_type=F32)
        return lax.dot_general(a.astype(BF16), b.astype(BF16), dn, preferred_element_type=F32)

    @jax.custom_vjp
    def mm(a, b):
        return raw(a, b, dims)

    def fwd(a, b):
        return raw(a, b, dims), (a, b)

    def bwd(res, g):
        a, b = res
        if dims == NN:
            da, db = raw(g, b, NT), raw(a, g, TN)
        elif dims == NT:
            da, db = raw(g, b, NN), raw(g, a, TN)
        else:
            da, db = raw(b, g, NT), raw(a, g, NN)
        return da.astype(a.dtype), db.astype(b.dtype)

    mm.defvjp(fwd, bwd)
    return mm


_mm = _make_mm(NN, False)
_mm_nt = _make_mm(NT, False)
_mm_tn = _make_mm(TN, False)
_mmx = _make_mm(NN, "split_lhs")
_bmm = _make_mm(NN, False, batch=True)
_bmm_nt = _make_mm(NT, False, batch=True)
_bmm_tn = _make_mm(TN, False, batch=True)
_bmmx = _make_mm(NN, True, batch=True)
_bmms = _make_mm(NN, "split", batch=True)
_bmms_nt = _make_mm(NT, "split", batch=True)
_bmms_tn = _make_mm(TN, "split", batch=True)


@jax.custom_vjp
def _unit_lower_inv(a):
    n = a.shape[-1]
    eye = (lax.broadcasted_iota(jnp.int32, a.shape, 1) == lax.broadcasted_iota(jnp.int32, a.shape, 2)).astype(F32)
    x = -a
    t = eye + x
    for _ in range(max(n.bit_length() - 2, 0)):
        x = _bmms(x, x)
        t = t + _bmms(t, x)
    return t


def _unit_lower_inv_fwd(a):
    t = _unit_lower_inv(a)
    return t, t


def _unit_lower_inv_bwd(t, g):
    return (-_bmms_tn(t, _bmms_nt(g, t)),)


_unit_lower_inv.defvjp(_unit_lower_inv_fwd, _unit_lower_inv_bwd)


def _scan_chunk_rows(x, reverse):
    nb, c, w = x.shape
    y = x.reshape(nb * c, w)
    pos = lax.broadcasted_iota(jnp.int32, y.shape, 0) % c
    step = 1
    while step < c:
        if reverse:
            y = y + jnp.where(pos < c - step, pltpu.roll(y, nb * c - step, 0), 0.0)
        else:
            y = y + jnp.where(pos >= step, pltpu.roll(y, step, 0), 0.0)
        step *= 2
    return y.reshape(nb, c, w)


@jax.custom_vjp
def _chunk_cumsum(x):
    return _scan_chunk_rows(x, False)


_chunk_cumsum.defvjp(lambda x: (_scan_chunk_rows(x, False), None), lambda _, g: (_scan_chunk_rows(g, True),))


def _rms(x, w, n):
    ms = jnp.sum(x * x, axis=-1, keepdims=True) * (1.0 / n)
    return x * lax.rsqrt(ms + NORM_EPS) * w


def _silu(x):
    return x * jax.nn.sigmoid(x)


def _softplus(x):
    return jnp.maximum(x, 0.0) + jnp.log(1.0 + jnp.exp(-jnp.abs(x)))


def _rope(x, cos, sin, perm):
    return x * cos + _mmx(x, perm) * sin


def _mla_prep_fn(rows, consts):
    q_lat, kv_lat, k_pe, cos, sin = rows
    qn = _rms(q_lat, consts["qa_w"], LORA)
    kvn = _rms(kv_lat, consts["kva_w"], LORA)
    outs = []
    for h in range(HEADS):
        q_n = _mm_nt(qn, consts["wq_n"][h])
        q_r = _mm_nt(qn, consts["wq_r"][h])
        rs = lax.rsqrt((jnp.sum(q_n * q_n, -1, keepdims=True) + jnp.sum(q_r * q_r, -1, keepdims=True)) * (1.0 / QK_DIM)
                       + NORM_EPS)
        q_n = q_n * rs * consts["qn_n"]
        q_r = _rope(q_r * rs * consts["qn_r"], cos, sin, consts["perm"])
        k_n = _mm_nt(kvn, consts["wk_n"][h])
        v = _mm_nt(kvn, consts["wv"][h])
        rk = lax.rsqrt((jnp.sum(k_n * k_n, -1, keepdims=True) + jnp.sum(k_pe * k_pe, -1, keepdims=True)) * (1.0 / QK_DIM)
                       + NORM_EPS)
        k_n = k_n * rk * consts["kn_n"]
        k_r = _rope(k_pe * rk * consts["kn_r"], cos, sin, consts["perm"])
        outs += [q_n, q_r, k_n, k_r, v]
    return tuple(outs)


def _attn_fn(q, k, v, row0):
    s = _mm_nt(q, k) * (1.0 / math.sqrt(QK_DIM))
    qpos = row0 + lax.broadcasted_iota(jnp.int32, s.shape, 0)
    kpos = lax.broadcasted_iota(jnp.int32, s.shape, 1)
    s = jnp.where(kpos <= qpos, s, -1e30)
    m = lax.stop_gradient(jnp.max(s, axis=-1, keepdims=True))
    p = jnp.exp(s - m)
    p = p / jnp.sum(p, axis=-1, keepdims=True)
    return _mm(p, v)


def _dn_prep_fn(rows, consts):
    qc, kc, ab = rows
    a_b = _mmx(ab, consts["sel_a"])
    b_b = _mmx(ab, consts["sel_b"])
    beta = jax.nn.sigmoid(b_b)
    g = -jnp.exp(consts["alog"]) * _softplus(a_b + consts["dtb"])
    qs, ks = [], []
    for h in range(HEADS):
        q, k = qc[h], kc[h]
        qs.append(q * lax.rsqrt(jnp.sum(q * q, -1, keepdims=True) + NORM_EPS))
        ks.append(k * lax.rsqrt(jnp.sum(k * k, -1, keepdims=True) + NORM_EPS))
    return tuple(qs), tuple(ks), g, beta


def _dn_chunk_fn(q, k, v, gb, g64, bb):
    nb = q.shape[0]
    ri = lax.broadcasted_iota(jnp.int32, (nb, CHUNK, CHUNK), 1)
    ci = lax.broadcasted_iota(jnp.int32, (nb, CHUNK, CHUNK), 2)
    tri = ri >= ci
    strict = ri > ci
    tril = tri.astype(F32)
    eye = (ri == ci).astype(F32)
    ones = jnp.ones((nb, CHUNK, CHUNK), F32)
    gc = _chunk_cumsum(gb)
    gc64 = _chunk_cumsum(g64)
    grow = _bmmx(ones, eye * gc64)
    diff = gc64 - grow
    decay = jnp.where(tri, jnp.exp(jnp.where(tri, diff, 0.0)), 0.0)
    kb = k * bb
    vb = v * bb
    a = jnp.where(strict, _bmm_nt(kb, k) * decay, 0.0)
    tinv = _unit_lower_inv(a)
    u = _bmm(tinv, vb)
    w = _bmm(tinv, kb * jnp.exp(gc))
    qs = q * (1.0 / math.sqrt(HEAD))
    qk = _bmm_nt(qs, k) * decay
    qg = qs * jnp.exp(gc)
    glast = jnp.sum(gb, axis=1, keepdims=True)
    kdec = k * jnp.exp(glast - gc)
    n_mat = _bmm_tn(kdec, w)
    b_mat = _bmm_tn(kdec, u)
    q_eff = qg - _bmm(qk, w)
    o_own = _bmm(qk, u)
    return n_mat, b_mat, q_eff, o_own, jnp.exp(glast)


def _dn_rec_fn(s, n_mat, b_mat, eg):
    return s * eg - _mm(n_mat, s) + b_mat


def _dn_o_fn(s, q_eff, o_own):
    return _bmm(q_eff, s) + o_own


def _dn_out_fn(o, z, w):
    return _rms(o, w, HEAD) * _silu(z)


def _row_tile(t, parts=8):
    return t // parts if (t // parts) % 16 == 0 else t


def _tile(n, pref, unit):
    best = n
    for cand in range(unit, min(n, pref) + 1, unit):
        if n % cand == 0:
            best = cand
    return best if best <= pref else n


def _rows_call(name, body, rows, consts, outs, accs, r, host=None):
    rows = [a if isinstance(a, tuple) else (a, a.shape[1], 0) for a in rows]
    t = rows[0][0].shape[0]
    zero = lambda nd: (lambda i: (0,) * nd)
    in_specs = [pl.BlockSpec((r, w), functools.partial(lambda i, b: (i, b), b=blk)) for _, w, blk in rows]
    rows = [a for a, _, _ in rows]
    in_specs += [pl.BlockSpec(a.shape, zero(a.ndim)) for a in consts]
    out_shape = [jax.ShapeDtypeStruct((t, w), dt) for w, dt in outs] + [jax.ShapeDtypeStruct(s, F32) for s in accs]
    out_specs = [pl.BlockSpec((r, w), lambda i: (i, 0)) for w, _ in outs] + [pl.BlockSpec(s, zero(len(s))) for s in accs]
    return _pcall(body, name, (t // r,), in_specs, out_specs, out_shape, [*rows, *consts], ("arbitrary",), host=host)


def _accumulate(ref, val):
    @pl.when(pl.program_id(0) == 0)
    def _():
        ref[...] = jnp.zeros(ref.shape, ref.dtype)

    ref[...] += val


def _matmul(name, a, b, dims, out_dtype, res=None, host=None):
    if dims == "nn":
        (m, k), n = a.shape, b.shape[1]
    elif dims == "nt":
        (m, k), n = a.shape, b.shape[0]
    else:
        (k, m), n = a.shape, b.shape[1]
    tm = _tile(m, 1100, 16) if dims != "tn" else _tile(m, 640, 128)
    tn = _tile(n, 1408, 128)
    if dims == "nn":
        a_spec, b_spec, dn = pl.BlockSpec((tm, k), lambda i, j: (i, 0)), pl.BlockSpec((k, tn), lambda i, j: (0, j)), NN
    elif dims == "nt":
        a_spec, b_spec, dn = pl.BlockSpec((tm, k), lambda i, j: (i, 0)), pl.BlockSpec((tn, k), lambda i, j: (j, 0)), NT
    else:
        a_spec, b_spec, dn = pl.BlockSpec((k, tm), lambda i, j: (0, i)), pl.BlockSpec((k, tn), lambda i, j: (0, j)), TN
    o_spec = pl.BlockSpec((tm, tn), lambda i, j: (i, j))

    def body(*refs):
        a_ref, b_ref, o_ref = refs[0], refs[1], refs[-1]
        acc = lax.dot_general(a_ref[...].astype(BF16), b_ref[...].astype(BF16), (dn, ((), ())),
                              preferred_element_type=F32)
        if res is not None:
            acc = acc + refs[2][...]
        o_ref[...] = acc.astype(out_dtype)

    ins = [a, b] + ([res] if res is not None else [])
    specs = [a_spec, b_spec] + ([o_spec] if res is not None else [])
    return _pcall(body, name, (m // tm, n // tn), specs, o_spec, jax.ShapeDtypeStruct((m, n), out_dtype), ins,
                  ("arbitrary", "arbitrary"), host=host)


def _rms_fwd(name, h, w, host=None):
    n = h.shape[1]

    def body(h_ref, w_ref, o_ref):
        o_ref[...] = _rms(h_ref[...], w_ref[...], n).astype(BF16)

    return _rows_call(name, body, [h], [w], [(n, BF16)], [], _row_tile(h.shape[0]), host=host)[0]


def _rms_bwd(name, h, w, cts, resid, host=None):
    n = h.shape[1]
    nct = len(cts)

    def body(*refs):
        h_ref, ct_refs, r_ref, w_ref = refs[0], refs[1:1 + nct], refs[1 + nct], refs[2 + nct]
        dh_ref, dh16_ref, dw_ref = refs[-3], refs[-2], refs[-1]
        ct = ct_refs[0][...].astype(F32)
        for c in ct_refs[1:]:
            ct = ct + c[...].astype(F32)
        _, vjp = jax.vjp(lambda x, ww: _rms(x, ww, n), h_ref[...], w_ref[...])
        dh, dw = vjp(ct)
        dh = dh + r_ref[...]
        dh_ref[...] = dh
        dh16_ref[...] = dh.astype(BF16)
        _accumulate(dw_ref, dw)

    return _rows_call(name, body, [h, *cts, resid], [w], [(n, F32), (n, BF16)], [(1, n)], _row_tile(h.shape[0]), host=host)


def _mla_consts_from_refs(qa, wq, kva, wkv, qn, kn, perm):
    f = lambda r: r[...].astype(F32)
    return dict(
        qa_w=f(qa), kva_w=f(kva), perm=f(perm),
        wq_n=[wq[h * QK_PAD:h * QK_PAD + HEAD, :].astype(F32) for h in range(HEADS)],
        wq_r=[wq[h * QK_PAD + HEAD:(h + 1) * QK_PAD, :].astype(F32) for h in range(HEADS)],
        wk_n=[wkv[h * QK_PAD:h * QK_PAD + HEAD, :].astype(F32) for h in range(HEADS)],
        wv=[wkv[h * QK_PAD + HEAD:(h + 1) * QK_PAD, :].astype(F32) for h in range(HEADS)],
        qn_n=qn[:, 0:HEAD], qn_r=qn[:, HEAD:QK_PAD], kn_n=kn[:, 0:HEAD], kn_r=kn[:, HEAD:QK_PAD])


def _mla_prep_fwd(q_lat, kv_lat, k_pe, cos, sin, qa, wq, kva, wkv, qn, kn, perm):
    def body(ql, kvl, kp, c, s, qa_r, wq_r, kva_r, wkv_r, qn_r, kn_r, p_r, q_out, k_out, v_out):
        consts = _mla_consts_from_refs(qa_r, wq_r, kva_r, wkv_r, qn_r, kn_r, p_r)
        outs = _mla_prep_fn((ql[...], kvl[...], kp[...], c[...], s[...]), consts)
        for h in range(HEADS):
            q_n, q_r, k_n, k_r, v = outs[5 * h:5 * h + 5]
            q_out[:, h * QK_PAD:h * QK_PAD + HEAD] = q_n.astype(BF16)
            q_out[:, h * QK_PAD + HEAD:(h + 1) * QK_PAD] = q_r.astype(BF16)
            k_out[:, h * QK_PAD:h * QK_PAD + HEAD] = k_n.astype(BF16)
            k_out[:, h * QK_PAD + HEAD:(h + 1) * QK_PAD] = k_r.astype(BF16)
            v_out[:, h * HEAD:(h + 1) * HEAD] = v.astype(BF16)

    return _rows_call("mla_prep_fwd", body, [q_lat, kv_lat, k_pe, cos, sin], [qa, wq, kva, wkv, qn, kn, perm],
                      [(HEADS * QK_PAD, BF16), (HEADS * QK_PAD, BF16), (DN_WIDTH, BF16)], [], _row_tile(cos.shape[0], 4))


def _mla_prep_bwd(q_lat, kv_lat, k_pe, cos, sin, dq, dk, dv, qa, wq, kva, wkv, qn, kn, perm, host=None):
    def body(ql, kvl, kp, c, s, dq_r, dk_r, dv_r, qa_r, wq_r, kva_r, wkv_r, qn_r, kn_r, p_r,
             dql, dkvl, dkp, dqa, dwq, dkva, dwkv, dqn, dkn):
        consts = _mla_consts_from_refs(qa_r, wq_r, kva_r, wkv_r, qn_r, kn_r, p_r)
        cc, ss, pm = c[...], s[...], consts.pop("perm")
        _, vjp = jax.vjp(lambda rows, cs: _mla_prep_fn((*rows, cc, ss), dict(cs, perm=pm)), (ql[...], kvl[...], kp[...]),
                         consts)
        cts = []
        for h in range(HEADS):
            cts += [dq_r[:, h * QK_PAD:h * QK_PAD + HEAD], dq_r[:, h * QK_PAD + HEAD:(h + 1) * QK_PAD],
                    dk_r[:, h * QK_PAD:h * QK_PAD + HEAD], dk_r[:, h * QK_PAD + HEAD:(h + 1) * QK_PAD],
                    dv_r[:, h * HEAD:(h + 1) * HEAD]]
        (d_ql, d_kvl, d_kp), dc = vjp(tuple(cts))
        dql[...] = d_ql.astype(BF16)
        dkvl[...] = d_kvl.astype(BF16)
        dkp[...] = d_kp.astype(BF16)
        first = pl.program_id(0) == 0

        def acc(ref, sl, val):
            @pl.when(first)
            def _():
                ref[sl] = val

            @pl.when(jnp.logical_not(first))
            def _():
                ref[sl] += val

        full = (slice(None), slice(None))
        acc(dqa, full, dc["qa_w"])
        acc(dkva, full, dc["kva_w"])
        for h in range(HEADS):
            acc(dwq, (slice(h * QK_PAD, h * QK_PAD + HEAD), slice(None)), dc["wq_n"][h])
            acc(dwq, (slice(h * QK_PAD + HEAD, (h + 1) * QK_PAD), slice(None)), dc["wq_r"][h])
            acc(dwkv, (slice(h * QK_PAD, h * QK_PAD + HEAD), slice(None)), dc["wk_n"][h])
            acc(dwkv, (slice(h * QK_PAD + HEAD, (h + 1) * QK_PAD), slice(None)), dc["wv"][h])
        acc(dqn, (slice(None), slice(0, HEAD)), dc["qn_n"])
        acc(dqn, (slice(None), slice(HEAD, QK_PAD)), dc["qn_r"])
        acc(dkn, (slice(None), slice(0, HEAD)), dc["kn_n"])
        acc(dkn, (slice(None), slice(HEAD, QK_PAD)), dc["kn_r"])

    return _rows_call("mla_prep_bwd", body, [q_lat, kv_lat, k_pe, cos, sin, dq, dk, dv],
                      [qa, wq, kva, wkv, qn, kn, perm],
                      [(LORA, BF16), (LORA, BF16), (HEAD, BF16)],
                      [(1, LORA), wq.shape, (1, LORA), wkv.shape, (1, QK_PAD), (1, QK_PAD)], _row_tile(cos.shape[0], 4),
                      host=host)


ATTN_Q_ROWS = 512


def _attn_blocks(t):
    return [(r0, min(ATTN_Q_ROWS, t - r0)) for r0 in range(0, t, ATTN_Q_ROWS)]


def _attn_fwd(q, k, v, host=None):
    t = q.shape[0]

    def body(q_ref, k_ref, v_ref, o_ref):
        for r0, rows in _attn_blocks(t):
            ext = r0 + rows
            o_ref[r0:ext, :] = _attn_fn(q_ref[r0:ext, :], k_ref[0:ext, :], v_ref[0:ext, :], r0)

    qk_spec = pl.BlockSpec((t, QK_PAD), lambda h: (0, h))
    v_spec = pl.BlockSpec((t, HEAD), lambda h: (0, h))
    return _pcall(body, "attn_fwd", (HEADS,), [qk_spec, qk_spec, v_spec], v_spec,
                  jax.ShapeDtypeStruct((t, HEADS * HEAD), F32), [q, k, v], ("arbitrary",), host=host)


def _attn_bwd(q, k, v, do, host=None):
    t = q.shape[0]

    def body(q_ref, k_ref, v_ref, do_ref, dq_ref, dk_ref, dv_ref):
        dk_ref[...] = jnp.zeros(dk_ref.shape, F32)
        dv_ref[...] = jnp.zeros(dv_ref.shape, F32)
        for r0, rows in _attn_blocks(t):
            ext = r0 + rows
            _, vjp = jax.vjp(functools.partial(_attn_fn, row0=r0), q_ref[r0:ext, :].astype(F32),
                             k_ref[0:ext, :].astype(F32), v_ref[0:ext, :].astype(F32))
            dq, dk, dv = vjp(do_ref[r0:ext, :])
            dq_ref[r0:ext, :] = dq
            dk_ref[0:ext, :] += dk
            dv_ref[0:ext, :] += dv

    qk_spec = pl.BlockSpec((t, QK_PAD), lambda h: (0, h))
    v_spec = pl.BlockSpec((t, HEAD), lambda h: (0, h))
    return _pcall(body, "attn_bwd", (HEADS,), [qk_spec, qk_spec, v_spec, v_spec], [qk_spec, qk_spec, v_spec],
                  [jax.ShapeDtypeStruct((t, HEADS * QK_PAD), F32), jax.ShapeDtypeStruct((t, HEADS * QK_PAD), F32),
                   jax.ShapeDtypeStruct((t, HEADS * HEAD), F32)], [q, k, v, do], ("arbitrary",), host=host)


def _mix_out_proj(o_mla, o_dn, z, w_mla, w_dn, w_out, h0, w_ffn):
    def body(om_ref, od_ref, z_ref, h0_ref, wm_ref, wd_ref, wo_ref, wf_ref, mixed_ref, h1_ref, n2_ref):
        for h in range(HEADS):
            sl = slice(h * HEAD, (h + 1) * HEAD)
            mixed_ref[:, sl] = _rms(om_ref[:, sl], wm_ref[...], HEAD).astype(BF16)
            mixed_ref[:, DN_WIDTH + h * HEAD:DN_WIDTH + (h + 1) * HEAD] = _dn_out_fn(od_ref[:, sl], z_ref[:, sl],
                                                                                     wd_ref[...]).astype(BF16)
        h1 = _mm(mixed_ref[...], wo_ref[...]) + h0_ref[...]
        h1_ref[...] = h1
        n2_ref[...] = _rms(h1, wf_ref[...], D_MODEL).astype(BF16)

    return _rows_call("mix_out_proj", body, [o_mla, o_dn, z, h0], [w_mla, w_dn, w_out, w_ffn],
                      [(D_MODEL, BF16), (D_MODEL, F32), (D_MODEL, BF16)], [], _row_tile(o_mla.shape[0], 4))


def _down_proj_loss(act, w_down, h1, tgt, n_valid):
    t, n = h1.shape
    r = _row_tile(t, 4)

    def body(a_ref, h_ref, t_ref, w_ref, dy_ref, dy16_ref, acc_ref):
        h2 = _mm(a_ref[...], w_ref[...]) + h_ref[...]
        rows = pl.program_id(0) * r + lax.broadcasted_iota(jnp.int32, (r, n), 0)
        valid = jnp.logical_and(rows >= N_META, rows < n_valid)
        e = jnp.where(valid, h2 - t_ref[...], 0.0)
        dy = e * (1.0 / n)
        dy_ref[...] = dy
        dy16_ref[...] = dy.astype(BF16)
        _accumulate(acc_ref, jnp.sum(e * e, axis=0, keepdims=True))

    return _rows_call("down_proj_loss", body, [act, h1, tgt], [w_down], [(n, F32), (n, BF16)], [(1, n)], r)


def _in_proj_bwd_x(pieces, win, host=None):
    offs = np.cumsum([0] + [p.shape[1] for p in pieces])

    def body(*refs):
        p_refs, w_ref, o_ref = refs[:len(pieces)], refs[len(pieces)], refs[-1]
        acc = None
        for i, p_ref in enumerate(p_refs):
            part = _mm(p_ref[...], w_ref[int(offs[i]):int(offs[i + 1]), :])
            acc = part if acc is None else acc + part
        o_ref[...] = acc.astype(BF16)

    return _rows_call("in_dx", body, pieces, [win], [(win.shape[1], BF16)], [], _row_tile(pieces[0].shape[0], 4), host=host)[0]


def _in_proj_bwd_w(pieces, u):
    offs = np.cumsum([0] + [p.shape[1] for p in pieces])

    def body(*refs):
        p_refs, u_ref, o_ref = refs[:len(pieces)], refs[len(pieces)], refs[-1]
        first = pl.program_id(0) == 0
        uv = u_ref[...]
        for i, p_ref in enumerate(p_refs):
            rows = slice(int(offs[i]), int(offs[i + 1]))
            part = _mm_tn(p_ref[...], uv)

            @pl.when(first)
            def _():
                o_ref[rows, :] = part

            @pl.when(jnp.logical_not(first))
            def _():
                o_ref[rows, :] += part

    return _rows_call("in_dw", body, [*pieces, u], [], [], [(int(offs[-1]), u.shape[1])], _row_tile(u.shape[0], 2))[0]


def _ffn_in_bwd(dgpre, dup, w_gate_t, w_up_t, h1, dy, w_ffn, host=None):
    n = h1.shape[1]

    def body(dg_ref, du_ref, h_ref, dy_ref, wg_ref, wu_ref, w_ref, dh_ref, dh16_ref, dw_ref):
        ct = _mm(dg_ref[...], wg_ref[...]) + _mm(du_ref[...], wu_ref[...])
        _, vjp = jax.vjp(lambda x, ww: _rms(x, ww, n), h_ref[...], w_ref[...])
        dh, dw = vjp(ct)
        dh = dh + dy_ref[...]
        dh_ref[...] = dh
        dh16_ref[...] = dh.astype(BF16)
        _accumulate(dw_ref, dw)

    return _rows_call("ffn_in_bwd", body, [dgpre, dup, h1, dy], [w_gate_t, w_up_t, w_ffn], [(n, F32), (n, BF16)], [(1, n)],
                      _row_tile(h1.shape[0]), host=host)


def _mix_out_bwd(o_mla, o_dn, z, dh1, w_out, w_mla, w_dn, host=None):
    def body(om_ref, od_ref, z_ref, dh_ref, wo_ref, wm_ref, wd_ref, dom_ref, dod_ref, dz_ref, dwm_ref, dwd_ref):
        dwm = dwd = None
        for h in range(HEADS):
            sl = slice(h * HEAD, (h + 1) * HEAD)
            _, vjp = jax.vjp(lambda o, w: _rms(o, w, HEAD), om_ref[:, sl], wm_ref[...])
            do, dw = vjp(_mm_nt(dh_ref[...], wo_ref[sl, :]))
            dom_ref[:, sl] = do
            dwm = dw if dwm is None else dwm + dw
            _, vjp = jax.vjp(_dn_out_fn, od_ref[:, sl], z_ref[:, sl], wd_ref[...])
            do, dz, dw = vjp(_mm_nt(dh_ref[...], wo_ref[DN_WIDTH + h * HEAD:DN_WIDTH + (h + 1) * HEAD, :]))
            dod_ref[:, sl] = do
            dz_ref[:, sl] = dz.astype(BF16)
            dwd = dw if dwd is None else dwd + dw
        _accumulate(dwm_ref, dwm)
        _accumulate(dwd_ref, dwd)

    return _rows_call("mix_out_bwd", body, [o_mla, o_dn, z, dh1], [w_out, w_mla, w_dn],
                      [(DN_WIDTH, F32), (DN_WIDTH, F32), (DN_WIDTH, BF16)], [(1, HEAD), (1, HEAD)],
                      _row_tile(o_mla.shape[0], 4), host=host)


def _shift_down(x, s):
    if s == 0:
        return x
    rows = lax.broadcasted_iota(jnp.int32, x.shape, 0)
    return jnp.where(rows >= s, pltpu.roll(x, s, 0), 0.0)


def _shift_up(x, s):
    if s == 0:
        return x
    t = x.shape[0]
    rows = lax.broadcasted_iota(jnp.int32, x.shape, 0)
    return jnp.where(rows < t - s, pltpu.roll(x, t - s, 0), 0.0)


def _col_call(name, body, cols, taps, outs, tap_outs, cw, host=None):
    t, c = cols[0].shape[0], taps[0].shape[1]
    in_specs = [pl.BlockSpec((t, cw), lambda j: (0, j)) for _ in cols]
    in_specs += [pl.BlockSpec((a.shape[0], cw), lambda j: (0, j)) for a in taps]
    out_shape = [jax.ShapeDtypeStruct((t, c), dt) for dt in outs] + [jax.ShapeDtypeStruct((n, c), F32) for n in tap_outs]
    out_specs = [pl.BlockSpec((t, cw), lambda j: (0, j)) for _ in outs]
    out_specs += [pl.BlockSpec((n, cw), lambda j: (0, j)) for n in tap_outs]
    return _pcall(body, name, (c // cw,), in_specs, out_specs, out_shape, [*cols, *taps], ("arbitrary",), host=host)


def _causal_conv(x, w_ref, width, zero_tail=False):
    down = (lambda a, s: pltpu.roll(a, s, 0)) if zero_tail else _shift_down
    acc = w_ref[width - 1:width, :] * x
    for j in range(width - 1):
        acc = acc + w_ref[j:j + 1, :] * down(x, width - 1 - j)
    return acc


def _causal_conv_bwd(x, dpre, w_ref, dx_ref, dw_ref, width, zero_tail=False):
    t = x.shape[0]
    down = (lambda a, s: pltpu.roll(a, s, 0)) if zero_tail else _shift_down
    up = (lambda a, s: pltpu.roll(a, t - s, 0)) if zero_tail else _shift_up
    dx = w_ref[width - 1:width, :] * dpre
    dw_ref[width - 1:width, :] = jnp.sum(dpre * x, axis=0, keepdims=True)
    for j in range(width - 1):
        s = width - 1 - j
        dx = dx + w_ref[j:j + 1, :] * up(dpre, s)
        dw_ref[j:j + 1, :] = jnp.sum(dpre * down(x, s), axis=0, keepdims=True)
    dx_ref[...] = dx.astype(dx_ref.dtype)


def _dsilu(x):
    sg = jax.nn.sigmoid(x)
    return sg * (1.0 + x * (1.0 - sg))


def _dn_conv_fwd(x, w):
    def body(x_ref, w_ref, y_ref):
        y_ref[...] = _silu(_causal_conv(x_ref[...], w_ref, 4, zero_tail=True))

    return _col_call("dn_conv_fwd", body, [x], [w], [F32], [], 256)[0]


def _dn_conv_bwd(x, w, dy):
    def body(x_ref, dy_ref, w_ref, dx_ref, dw_ref):
        xv = x_ref[...]
        dpre = dy_ref[...] * _dsilu(_causal_conv(xv, w_ref, 4, zero_tail=True))
        _causal_conv_bwd(xv, dpre, w_ref, dx_ref, dw_ref, 4, zero_tail=True)

    return _col_call("dn_conv_bwd", body, [x, dy], [w], [BF16], [4], 256)


def _ffn_glu_fwd(n2, w_gate_t, w_up_t, w, b, host=None):
    t, k = n2.shape
    c, cw = w_gate_t.shape[0], 256

    def body(n_ref, wg_ref, wu_ref, w_ref, b_ref, g_ref, u_ref, a_ref):
        nv = n_ref[...]
        g16 = _mm_nt(nv, wg_ref[...]).astype(BF16)
        u16 = _mm_nt(nv, wu_ref[...]).astype(BF16)
        g_ref[...] = g16
        u_ref[...] = u16
        gate = _causal_conv(g16.astype(F32), w_ref, 3) + b_ref[...]
        a_ref[...] = (_silu(gate) * u16.astype(F32)).astype(BF16)

    wspec = pl.BlockSpec((cw, k), lambda j: (j, 0))
    col = pl.BlockSpec((t, cw), lambda j: (0, j))
    in_specs = [pl.BlockSpec((t, k), lambda j: (0, 0)), wspec, wspec, pl.BlockSpec((w.shape[0], cw), lambda j: (0, j)),
                pl.BlockSpec((1, cw), lambda j: (0, j))]
    return _pcall(body, "ffn_glu_fwd", (c // cw,), in_specs, [col] * 3, [jax.ShapeDtypeStruct((t, c), BF16)] * 3,
                  [n2, w_gate_t, w_up_t, w, b], ("arbitrary",), host=host)


def _ffn_glu_bwd(gpre, up, dy16, w_down, w, b):
    t, k = dy16.shape
    c, cw = w_down.shape[0], 256

    def body(g_ref, u_ref, dy_ref, wd_ref, w_ref, b_ref, dg_ref, du_ref, dw_ref, db_ref):
        gv = g_ref[...].astype(F32)
        gate = _causal_conv(gv, w_ref, 3) + b_ref[...]
        da = _mm_nt(dy_ref[...], wd_ref[...])
        sg = jax.nn.sigmoid(gate)
        du_ref[...] = (da * (gate * sg)).astype(BF16)
        dgate = da * u_ref[...].astype(F32) * (sg * (1.0 + gate * (1.0 - sg)))
        db_ref[...] = jnp.sum(dgate, axis=0, keepdims=True)
        _causal_conv_bwd(gv, dgate, w_ref, dg_ref, dw_ref, 3)

    col = pl.BlockSpec((t, cw), lambda j: (0, j))
    taps = lambda rows: pl.BlockSpec((rows, cw), lambda j: (0, j))
    in_specs = [col, col, pl.BlockSpec((t, k), lambda j: (0, 0)), pl.BlockSpec((cw, k), lambda j: (j, 0)),
                taps(w.shape[0]), taps(1)]
    return _pcall(body, "ffn_glu_bwd", (c // cw,), in_specs, [col, col, taps(w.shape[0]), taps(1)],
                  [jax.ShapeDtypeStruct((t, c), BF16)] * 2 + [jax.ShapeDtypeStruct((w.shape[0], c), F32),
                                                             jax.ShapeDtypeStruct((1, c), F32)],
                  [gpre, up, dy16, w_down, w, b], ("arbitrary",))


def _dn_prep_consts(sa, sb, al, dt):
    return dict(sel_a=sa[...], sel_b=sb[...], alog=al[...], dtb=dt[...])


def _dn_prep_fwd(conv, ab, sel_a, sel_b, alog, dtb):
    def body(c_ref, ab_ref, sa, sb, al, dt, q_out, k_out, g_out, b_out):
        qc = tuple(c_ref[:, h * HEAD:(h + 1) * HEAD] for h in range(HEADS))
        kc = tuple(c_ref[:, DN_WIDTH + h * HEAD:DN_WIDTH + (h + 1) * HEAD] for h in range(HEADS))
        qs, ks, g, beta = _dn_prep_fn((qc, kc, ab_ref[...]), _dn_prep_consts(sa, sb, al, dt))
        for h in range(HEADS):
            q_out[:, h * HEAD:(h + 1) * HEAD] = qs[h]
            k_out[:, h * HEAD:(h + 1) * HEAD] = ks[h]
        g_out[...] = g
        b_out[...] = beta

    return _rows_call("dn_prep_fwd", body, [conv, ab], [sel_a, sel_b, alog, dtb], [(DN_WIDTH, F32)] * 4, [],
                      _row_tile(conv.shape[0], 4))


def _dn_prep_bwd(conv, ab, dq, dk, dv, dg, db, sel_a, sel_b, alog, dtb):
    def body(c_ref, ab_ref, dq_r, dk_r, dv_r, dg_r, db_r, sa, sb, al, dt, dc_out, dab_out, dal_out, ddt_out):
        qc = tuple(c_ref[:, h * HEAD:(h + 1) * HEAD] for h in range(HEADS))
        kc = tuple(c_ref[:, DN_WIDTH + h * HEAD:DN_WIDTH + (h + 1) * HEAD] for h in range(HEADS))
        consts = _dn_prep_consts(sa, sb, al, dt)
        sel = dict(sel_a=consts["sel_a"], sel_b=consts["sel_b"])
        _, vjp = jax.vjp(lambda rows, ad: _dn_prep_fn(rows, {**sel, **ad}), (qc, kc, ab_ref[...]),
                         dict(alog=consts["alog"], dtb=consts["dtb"]))
        cq = tuple(dq_r[:, h * HEAD:(h + 1) * HEAD] for h in range(HEADS))
        ck = tuple(dk_r[:, h * HEAD:(h + 1) * HEAD] for h in range(HEADS))
        (dqc, dkc, dab), dad = vjp((cq, ck, dg_r[...], db_r[...]))
        for h in range(HEADS):
            dc_out[:, h * HEAD:(h + 1) * HEAD] = dqc[h]
            dc_out[:, DN_WIDTH + h * HEAD:DN_WIDTH + (h + 1) * HEAD] = dkc[h]
        dc_out[:, 2 * DN_WIDTH:3 * DN_WIDTH] = dv_r[...]
        dab_out[...] = dab.astype(BF16)
        _accumulate(dal_out, dad["alog"])
        _accumulate(ddt_out, dad["dtb"])

    return _rows_call("dn_prep_bwd", body, [conv, ab, dq, dk, dv, dg, db], [sel_a, sel_b, alog, dtb],
                      [(3 * DN_WIDTH, F32), (HEAD, BF16)], [(1, DN_WIDTH), (1, DN_WIDTH)], _row_tile(conv.shape[0], 4))


def _chunk_batch(t):
    nc = t // CHUNK
    return nc // 2 if nc % 2 == 0 else nc


def _dn_chunk_specs(t, nb):
    rows = nb * CHUNK
    blk = pl.BlockSpec((rows, HEAD), lambda h, b: (b, h))
    vblk = pl.BlockSpec((rows, HEAD), lambda h, b: (b, 2 * HEADS + h))
    mat = pl.BlockSpec((nb, HEAD, HEAD), lambda h, b: (b, h, 0))
    return rows, blk, vblk, mat


def _dn_chunk_fwd(qn, kn, conv, g, beta, host=None):
    t = qn.shape[0]
    nb = _chunk_batch(t)
    rows, blk, vblk, mat = _dn_chunk_specs(t, nb)

    def body(q_ref, k_ref, v_ref, g_ref, b_ref, n_o, b_o, qe_o, oo_o, eg_o):
        r3 = lambda x: x.reshape(nb, CHUNK, x.shape[-1])
        n_mat, b_mat, q_eff, o_own, eg = _dn_chunk_fn(r3(q_ref[...]), r3(k_ref[...]), r3(v_ref[...]), r3(g_ref[...]),
                                                      r3(g_ref[:, 0:CHUNK]), r3(b_ref[...]))
        n_o[...] = n_mat
        b_o[...] = b_mat
        qe_o[...] = q_eff.reshape(rows, HEAD)
        oo_o[...] = o_own.reshape(rows, HEAD)
        eg_o[...] = jnp.broadcast_to(eg, (nb, HEAD, HEAD))

    nc = t // CHUNK
    mats = jax.ShapeDtypeStruct((nc, DN_WIDTH, HEAD), F32)
    rowsd = jax.ShapeDtypeStruct((t, DN_WIDTH), F32)
    return _pcall(body, "dn_chunk_fwd", (HEADS, t // rows), [blk, blk, vblk, blk, blk], [mat, mat, blk, blk, mat],
                  [mats, mats, rowsd, rowsd, mats], [qn, kn, conv, g, beta], ("arbitrary", "arbitrary"), host=host)


def _dn_chunk_bwd(qn, kn, conv, g, beta, sall, gall, dq_eff, do, host=None):
    t = qn.shape[0]
    nb = _chunk_batch(t)
    rows, blk, vblk, mat = _dn_chunk_specs(t, nb)

    def body(q_ref, k_ref, v_ref, g_ref, b_ref, s_ref, ga_ref, dqe_ref, do_ref, dq_o, dk_o, dv_o, dg_o, db_o):
        r3 = lambda x: x.reshape(nb, CHUNK, x.shape[-1])
        _, vjp = jax.vjp(_dn_chunk_fn, r3(q_ref[...]), r3(k_ref[...]), r3(v_ref[...]), r3(g_ref[...]),
                         r3(g_ref[:, 0:CHUNK]), r3(b_ref[...]))
        s, ga = s_ref[...], ga_ref[...]
        d_n = -_bmm_nt(ga, s)
        d_eg = jnp.sum(ga * s, axis=1, keepdims=True)
        dq, dk, dv, dg, dg64, db = vjp((d_n, ga, r3(dqe_ref[...]), r3(do_ref[...]), d_eg))
        for o_ref, val in zip((dq_o, dk_o, dv_o, dg_o, db_o), (dq, dk, dv, dg, db)):
            o_ref[...] = val.reshape(rows, HEAD)
        dg_o[:, 0:CHUNK] += dg64.reshape(rows, CHUNK)

    return _pcall(body, "dn_chunk_bwd", (HEADS, t // rows), [blk, blk, vblk, blk, blk, mat, mat, blk, blk], [blk] * 5,
                  [jax.ShapeDtypeStruct((t, DN_WIDTH), F32)] * 5, [qn, kn, conv, g, beta, sall, gall, dq_eff, do],
                  ("arbitrary", "arbitrary"), host=host)


def _dn_rec_fwd(n_mat, b_mat, eg, host=None):
    nc = n_mat.shape[0]
    nb = _chunk_batch(nc * CHUNK)
    spec = pl.BlockSpec((nb, DN_WIDTH, HEAD), lambda i: (i, 0, 0))

    def body(n_ref, b_ref, eg_ref, sall_ref, s_scr):
        @pl.when(pl.program_id(0) == 0)
        def _():
            s_scr[...] = jnp.zeros(s_scr.shape, F32)

        for j in range(nb):
            sall_ref[j] = s_scr[...]
            for h in range(HEADS):
                sl = slice(h * HEAD, (h + 1) * HEAD)
                s_scr[sl, :] = _dn_rec_fn(s_scr[sl, :], n_ref[j, sl, :], b_ref[j, sl, :],
                                          eg_ref[j, h * HEAD:h * HEAD + 1, :])

    return _pcall(body, "dn_rec_fwd", (nc // nb,), [spec] * 3, spec, jax.ShapeDtypeStruct((nc, DN_WIDTH, HEAD), F32),
                  [n_mat, b_mat, eg], ("arbitrary",), scratch_shapes=[pltpu.VMEM((DN_WIDTH, HEAD), F32)], host=host)


def _dn_rec_bwd(n_mat, eg, ds_out, host=None):
    nc = n_mat.shape[0]
    nb = _chunk_batch(nc * CHUNK)
    steps = nc // nb
    spec = pl.BlockSpec((nb, DN_WIDTH, HEAD), lambda i: (steps - 1 - i, 0, 0))

    def body(n_ref, eg_ref, dso_ref, gall_ref, g_scr):
        @pl.when(pl.program_id(0) == 0)
        def _():
            g_scr[...] = jnp.zeros(g_scr.shape, F32)

        for j in reversed(range(nb)):
            gall_ref[j] = g_scr[...]
            for h in range(HEADS):
                sl = slice(h * HEAD, (h + 1) * HEAD)
                gv = g_scr[sl, :]
                g_scr[sl, :] = (gv * eg_ref[j, h * HEAD:h * HEAD + 1, :] - _mm_tn(n_ref[j, sl, :], gv)
                                + dso_ref[j, sl, :])

    return _pcall(body, "dn_rec_bwd", (steps,), [spec] * 3, spec, jax.ShapeDtypeStruct((nc, DN_WIDTH, HEAD), F32),
                  [n_mat, eg, ds_out], ("arbitrary",), scratch_shapes=[pltpu.VMEM((DN_WIDTH, HEAD), F32)], host=host)


def _dn_o_fwd(sall, q_eff, o_own):
    t = q_eff.shape[0]
    nb = _chunk_batch(t)
    rows, blk, _, mat = _dn_chunk_specs(t, nb)

    def body(s_ref, qe_ref, oo_ref, o_ref):
        r3 = lambda x: x.reshape(nb, CHUNK, HEAD)
        o_ref[...] = _dn_o_fn(s_ref[...], r3(qe_ref[...]), r3(oo_ref[...])).reshape(rows, HEAD)

    return _pcall(body, "dn_o_fwd", (HEADS, t // rows), [mat, blk, blk], blk, jax.ShapeDtypeStruct((t, DN_WIDTH), F32),
                  [sall, q_eff, o_own], ("arbitrary", "arbitrary"))


def _dn_o_bwd(sall, q_eff, do, host=None):
    t = q_eff.shape[0]
    nb = _chunk_batch(t)
    rows, blk, _, mat = _dn_chunk_specs(t, nb)

    def body(s_ref, qe_ref, do_ref, dqe_ref, ds_ref):
        r3 = lambda x: x.reshape(nb, CHUNK, HEAD)
        dov = r3(do_ref[...])
        dqe_ref[...] = _bmm_nt(dov, s_ref[...]).reshape(rows, HEAD)
        ds_ref[...] = _bmm_tn(r3(qe_ref[...]), dov)

    nc = t // CHUNK
    return _pcall(body, "dn_o_bwd", (HEADS, t // rows), [mat, blk, blk], [blk, mat],
                  [jax.ShapeDtypeStruct((t, DN_WIDTH), F32), jax.ShapeDtypeStruct((nc, DN_WIDTH, HEAD), F32)],
                  [sall, q_eff, do], ("arbitrary", "arbitrary"), host=host)


def _adamw_update(w, g, m, v):
    m2 = ADAM_B1 * m + (1.0 - ADAM_B1) * g
    v2 = ADAM_B2 * v + (1.0 - ADAM_B2) * (g * g)
    m_hat = m2 / (1.0 - ADAM_B1 ** ADAM_STEP)
    v_hat = v2 / (1.0 - ADAM_B2 ** ADAM_STEP)
    return -ADAM_LR * (m_hat / (jnp.sqrt(v_hat) + ADAM_EPS) + ADAM_WD * w), m2, v2


def _adamw_small(ws, gs, ms, vs):
    n = len(ws)

    def body(*refs):
        for i in range(n):
            d, m2, v2 = _adamw_update(refs[i][...], refs[n + i][...], refs[2 * n + i][...], refs[3 * n + i][...])
            refs[4 * n + i][...] = d
            refs[5 * n + i][...] = m2
            refs[6 * n + i][...] = v2

    shapes = [jax.ShapeDtypeStruct(a.shape, F32) for a in ws]
    outs = pl.pallas_call(body, name="adamw_small", out_shape=shapes * 3,
                          compiler_params=pltpu.CompilerParams(vmem_limit_bytes=VMEM_LIMIT))(*ws, *gs, *ms, *vs)
    return outs[:n], outs[n:2 * n], outs[2 * n:]


def _adamw_call(name, ws, gs, ms, vs, host=None):
    k = len(ws)
    rows, cols = ws[0].shape
    assert all(w.shape == (rows, cols) for w in ws) and all(g.shape == gs[0].shape for g in gs)
    by_rows = rows % 8 == 0

    def body(*refs):
        for i in range(k):
            w_ref, g_ref, m_ref, v_ref, g_out, d_ref, m_out, v_out = refs[i::k]
            gv = g_ref[...] if by_rows else g_ref[0:rows, :]
            g_out[...] = gv
            d_ref[...], m_out[...], v_out[...] = _adamw_update(w_ref[...], gv, m_ref[...], v_ref[...])

    if by_rows:
        tr = _tile(rows, 256 // k, 8)
        spec = g_spec = pl.BlockSpec((tr, cols), lambda i: (i, 0))
        grid = (rows // tr,)
    else:
        tc = _tile(cols, 256, 128)
        spec = pl.BlockSpec((rows, tc), lambda j: (0, j))
        g_spec = pl.BlockSpec((gs[0].shape[0], tc), lambda j: (0, j))
        grid = (cols // tc,)
    outs = _pcall(body, name, grid, [spec] * k + [g_spec] * k + [spec] * 2 * k, [spec] * 4 * k,
                  [jax.ShapeDtypeStruct((rows, cols), F32)] * 4 * k, [*ws, *gs, *ms, *vs], ("arbitrary",), host=host)
    return [outs[i::k] for i in range(k)]


def _adamw_rows3d(name, w, g, m, v):
    rows, _, cols = w.shape
    tr = max(d for d in range(1, 129) if rows % d == 0)

    def body(w_ref, g_ref, m_ref, v_ref, d_ref, m_out, v_out):
        d_ref[...], m_out[...], v_out[...] = _adamw_update(w_ref[...], g_ref[...], m_ref[...], v_ref[...])

    spec = pl.BlockSpec((tr, 1, cols), lambda i: (i, 0, 0))
    return pl.pallas_call(body, name=name, grid=(rows // tr,), in_specs=[spec] * 4, out_specs=[spec] * 3,
                          out_shape=[jax.ShapeDtypeStruct(w.shape, F32)] * 3, compiler_params=_cparams(("arbitrary",)))(
                              w, g, m, v)


def _rope_tables(t):
    half = ROPE // 2
    inv_freq = np.float32(ROPE_THETA) ** (-np.arange(half, dtype=np.float32) / np.float32(half))
    ang = np.arange(t, dtype=np.float32)[:, None] * inv_freq[None, :].astype(np.float32)
    z = np.zeros((t, HEAD - ROPE), np.float32)
    cos = np.concatenate([np.cos(ang), np.cos(ang), z], axis=1).astype(np.float32)
    sin = np.concatenate([np.sin(ang), np.sin(ang), z], axis=1).astype(np.float32)
    k = np.arange(HEAD)[:, None]
    l = np.arange(HEAD)[None, :]
    perm = np.where((l < half) & (k == l + half), -1.0, 0.0) + np.where((l >= half) & (l < ROPE) & (k == l - half), 1.0, 0.0)
    return jnp.asarray(cos), jnp.asarray(sin), jnp.asarray(perm.astype(np.float32))


def _win_to_pad(w):
    z = lambda n: jnp.zeros((n, w.shape[1]), w.dtype)
    return jnp.concatenate([w[576:2112], w[2112:2624], w[0:256], w[256:512], w[512:576], z(64), w[2624:2632], z(120)],
                           axis=0)


def _qk_to_pad(w):
    w4 = w.reshape(HEADS, QK_DIM, w.shape[-1])
    return jnp.concatenate([w4, jnp.zeros((HEADS, QK_PAD - QK_DIM, w.shape[-1]), w.dtype)], axis=1).reshape(
        HEADS * QK_PAD, w.shape[-1])


def _qk_from_pad(g):
    return g.reshape(HEADS, QK_PAD, g.shape[-1])[:, :QK_DIM].reshape(HEADS * QK_DIM, g.shape[-1])


def _ff_to_pad(a, axis):
    shape = list(a.shape)
    shape[axis:axis + 1] = [N_CHIPS, FF_SHARD]
    a4 = a.reshape(shape)
    shape[axis + 1] = FF_BLOCK - FF_SHARD
    out = jnp.concatenate([a4, jnp.zeros(shape, a.dtype)], axis=axis + 1)
    shape[axis:axis + 2] = [D_FF_P]
    return out.reshape(shape)


def _ff_from_pad(a, axis):
    shape = list(a.shape)
    shape[axis:axis + 1] = [N_CHIPS, FF_BLOCK]
    a4 = lax.slice_in_dim(a.reshape(shape), 0, FF_SHARD, axis=axis + 1)
    shape[axis:axis + 2] = [D_FF]
    return a4.reshape(shape)


class _LocalPlan:
    def __init__(self, wt):
        self.wt, self.grads = wt, {}

    def weight(self, name):
        return self.wt[name]

    def host(self, point):
        return None

    def grad(self, name, value):
        self.grads[name] = value


def _local_step(x, tgt, wt, plan=None):
    plan = _LocalPlan(wt) if plan is None else plan
    s = x.shape[0]
    n_valid = N_META + s
    t = -(-n_valid // HEAD) * HEAD
    assert t - n_valid >= 3, "the DeltaNet conv kernels rely on at least three zero rows after the sequence"
    zpad = jnp.zeros((t - n_valid, D_MODEL), F32)
    h0 = jnp.concatenate([wt["meta_tokens"], x, zpad], axis=0)
    tgt_p = jnp.concatenate([jnp.zeros((N_META, D_MODEL), F32), tgt, zpad], axis=0)
    cos, sin, perm = _rope_tables(t)
    qn_w = jnp.concatenate([wt["q_norm_w"], jnp.zeros((1, QK_PAD - QK_DIM), F32)], axis=1)
    kn_w = jnp.concatenate([wt["k_norm_w"], jnp.zeros((1, QK_PAD - QK_DIM), F32)], axis=1)
    head_id = jnp.arange(DN_WIDTH)[None, :] // HEAD
    lane = jnp.arange(HEAD)[:, None]
    sel_a = (lane == head_id).astype(F32)
    sel_b = (lane == head_id + HEADS).astype(F32)
    alog = jnp.repeat(wt["dn_A_log"], HEAD, axis=1)
    dtb = jnp.repeat(wt["dn_dt_bias"], HEAD, axis=1)
    conv_w, conv_b = wt["ffn_conv_w"], wt["ffn_conv_b"]

    u = _rms_fwd("attn_norm_fwd", h0, wt["attn_norm_w"], host=plan.host("attn_norm_fwd"))
    win, wq, wkv = plan.weight("w_in_t"), plan.weight("w_q_t"), plan.weight("w_kv_t")
    proj = _matmul("in_proj", u, win, "nt", F32)
    z = (proj, DN_WIDTH, 3)
    q_lat, kv_lat, k_pe, ab = (proj, LORA, 8), (proj, LORA, 9), (proj, HEAD, 20), (proj, HEAD, 21)
    mla_consts = (wt["q_a_norm_w"], wq, wt["kv_a_norm_w"], wkv, qn_w, kn_w, perm)
    q, k, v = _mla_prep_fwd(q_lat, kv_lat, k_pe, cos, sin, *mla_consts)
    o_mla = _attn_fwd(q, k, v, host=plan.host("attn_fwd"))
    conv = _dn_conv_fwd(proj, wt["dn_conv_w"])
    dn_consts = (sel_a, sel_b, alog, dtb)
    qn, kn, g, beta = _dn_prep_fwd(conv, ab, *dn_consts)
    n_mat, b_mat, q_eff, o_own, eg = _dn_chunk_fwd(qn, kn, conv, g, beta, host=plan.host("dn_chunk_fwd"))
    sall = _dn_rec_fwd(n_mat, b_mat, eg)
    o_dn = _dn_o_fwd(sall, q_eff, o_own)
    w_out = plan.weight("w_out")
    mixed, h1, n2 = _mix_out_proj(o_mla, o_dn, z, wt["mla_out_norm_w"], wt["dn_out_norm_w"], w_out, h0, wt["ffn_norm_w"])
    w_gate, w_up = plan.weight("w_gate_t"), plan.weight("w_up_t")
    gpre, up, act = _ffn_glu_fwd(n2, w_gate, w_up, conv_w, conv_b, host=plan.host("ffn_glu_fwd"))
    w_down = plan.weight("w_down")
    dy, dy16, sq = _down_proj_loss(act, w_down, h1, tgt_p, n_valid)

    grads = {}
    plan.grad("w_down", _matmul("down_dw", act, dy16, "tn", BF16))
    dgpre, dup, grads["ffn_conv_w"], grads["ffn_conv_b"] = _ffn_glu_bwd(gpre, up, dy16, w_down, conv_w, conv_b)
    plan.grad("w_gate_t", _matmul("gate_dw", dgpre, n2, "tn", BF16))
    plan.grad("w_up_t", _matmul("up_dw", dup, n2, "tn", BF16))
    dh1, dh1_16, grads["ffn_norm_w"] = _ffn_in_bwd(dgpre, dup, w_gate, w_up, h1, dy, wt["ffn_norm_w"],
                                                   host=plan.host("ffn_in_bwd"))
    plan.grad("w_out", _matmul("out_dw", mixed, dh1_16, "tn", BF16))
    do_mla, do_dn, dz, grads["mla_out_norm_w"], grads["dn_out_norm_w"] = _mix_out_bwd(
        o_mla, o_dn, z, dh1_16, w_out, wt["mla_out_norm_w"], wt["dn_out_norm_w"], host=plan.host("mix_out_bwd"))
    dq_eff, ds_out = _dn_o_bwd(sall, q_eff, do_dn)
    gall = _dn_rec_bwd(n_mat, eg, ds_out)
    dqn, dkn, dv_dn, dg, dbeta = _dn_chunk_bwd(qn, kn, conv, g, beta, sall, gall, dq_eff, do_dn,
                                               host=plan.host("dn_chunk_bwd"))
    dconv, dab, dalog, ddtb = _dn_prep_bwd(conv, ab, dqn, dkn, dv_dn, dg, dbeta, *dn_consts)
    grads["dn_A_log"] = jnp.sum(dalog.reshape(HEADS, HEAD), axis=1)[None, :]
    grads["dn_dt_bias"] = jnp.sum(ddtb.reshape(HEADS, HEAD), axis=1)[None, :]
    ddn_pre, grads["dn_conv_w"] = _dn_conv_bwd(proj, wt["dn_conv_w"], dconv)
    dq, dk, dv = _attn_bwd(q, k, v, do_mla, host=plan.host("attn_bwd"))
    dq_lat, dkv_lat, dk_pe, dqa, dwq, dkva, dwkv, dqnw, dknw = _mla_prep_bwd(
        q_lat, kv_lat, k_pe, cos, sin, dq, dk, dv, *mla_consts, host=plan.host("mla_prep_bwd"))
    grads["q_a_norm_w"], grads["kv_a_norm_w"] = dqa, dkva
    plan.grad("w_q_t", dwq)
    plan.grad("w_kv_t", dwkv)
    grads["q_norm_w"], grads["k_norm_w"] = dqnw[:, :QK_DIM], dknw[:, :QK_DIM]
    dproj = [ddn_pre, dz, dq_lat, dkv_lat, dk_pe, dab]
    plan.grad("w_in_t", _in_proj_bwd_w(dproj, u))
    du = _in_proj_bwd_x(dproj, win, host=plan.host("in_dx"))
    dh0, _, grads["attn_norm_w"] = _rms_bwd("attn_norm_bwd", h0, wt["attn_norm_w"], [du], dh1,
                                            host=plan.host("attn_norm_bwd"))
    grads["meta_tokens"] = dh0[0:N_META]
    if isinstance(plan, _LocalPlan):
        grads.update(plan.grads)
    return sq, dh0[N_META:n_valid], grads


def _mesh_pos():
    return lax.axis_index("x"), lax.axis_index("y"), lax.axis_index("c")


def _other_chips(x, y):
    return [(1 - x, y), (x, 1 - y), (1 - x, 1 - y)]


def _remote(src, dst, send_sems, recv_sems, k, to):
    return pltpu.make_async_remote_copy(src_ref=src, dst_ref=dst, send_sem=send_sems.at[k], recv_sem=recv_sems.at[k],
                                        device_id=to, device_id_type=MESH)


SIBLING_ID, CHIPS_ID, GATHER_ID, ALL_ID = 1, 2, 3, 4


def _sibling_peer():
    x, y, c = _mesh_pos()
    return [(x, y, 1 - c)]


def _chip_peers():
    x, y, c = _mesh_pos()
    return [(qx, qy, c) for qx, qy in _other_chips(x, y)]


def _copies_exchange(make, ins, out_shape, nsem, peers=None, cid=None):
    def prog(in_refs, out_refs, send_sems, recv_sems):
        copies = make(in_refs, out_refs, send_sems, recv_sems)

        def start():
            for cp in copies:
                cp.start()

        def finish():
            for cp in copies:
                cp.wait()

        return start, finish

    return _Exchange(prog, ins, out_shape, nsem, peers, cid)


def _all_gather(shards):
    def prog(srcs, dsts, send_sems, recv_sems):
        x, y, c = _mesh_pos()
        p = 2 * x + y
        sibling = (x, y, 1 - c)
        chips = _other_chips(x, y)
        bufs = tuple((s, d, s.shape[0] // 2) for s, d in zip(srcs, dsts))

        def half(ref, rows, which):
            return ref.at[pl.ds(which * rows, rows), :]

        def copy(i, k, src, dst, to):
            return _remote(src, dst, send_sems, recv_sems, 6 * i + k, to)

        sends = [copy(i, j, half(src, rows, c), half(dst.at[p], rows, c), (*chip, c))
                 for i, (src, dst, rows) in enumerate(bufs) for j, chip in enumerate(chips)]

        def start():
            for cp in sends:
                cp.start()

        def finish():
            passed = []
            for i, (src, dst, rows) in enumerate(bufs):
                for j, (qx, qy) in enumerate(chips):
                    block = half(dst.at[2 * qx + qy], rows, c)
                    copy(i, j, block, block, (x, y, c)).wait_recv()
                    fwd = copy(i, 3 + j, block, block, sibling)
                    fwd.start()
                    passed.append(fwd)
            for i, (src, dst, rows) in enumerate(bufs):
                for j, (qx, qy) in enumerate(chips):
                    block = half(dst.at[2 * qx + qy], rows, 1 - c)
                    copy(i, 3 + j, block, block, (x, y, c)).wait_recv()
            for cp in sends + passed:
                cp.wait_send()

        return start, finish

    return _Exchange(prog, shards, [jax.ShapeDtypeStruct((N_CHIPS, *s.shape), s.dtype) for s in shards], 6 * len(shards),
                     lambda: _sibling_peer() + _chip_peers(), GATHER_ID)


def _all_gather_small(block):
    def make(srcs, dsts, send_sems, recv_sems):
        x, y, c = _mesh_pos()
        return [_remote(srcs[0], dsts[0].at[2 * x + y], send_sems, recv_sems, k, (qx, qy, c))
                for k, (qx, qy) in enumerate(_other_chips(x, y))]

    return _copies_exchange(make, [block], [jax.ShapeDtypeStruct((N_CHIPS, *block.shape), block.dtype)], 3, _chip_peers,
                            CHIPS_ID)


def _gathered(ex):
    p = 2 * lax.axis_index("x") + lax.axis_index("y")
    return [lax.dynamic_update_slice(g, s[None], (p, 0, 0)) for g, s in zip(ex.outs, ex.ins)]


def _rs_to_sibling(bufs):
    def make(srcs, dsts, send_sems, recv_sems):
        x, y, c = _mesh_pos()
        copies = []
        for i, (src, dst) in enumerate(zip(srcs, dsts)):
            half = src.shape[1] // 2
            copies.append(_remote(src.at[:, pl.ds((1 - c) * half, half), :], dst, send_sems, recv_sems, i, (x, y, 1 - c)))
        return copies

    return _copies_exchange(make, bufs,
                            [jax.ShapeDtypeStruct((N_CHIPS, b.shape[1] // 2, b.shape[2]), b.dtype) for b in bufs],
                            len(bufs), _sibling_peer, SIBLING_ID)


def _rs_pair_add(name, bufs, gots, c, out_dtype):
    n = len(bufs)

    def body(c_ref, *refs):
        for a_ref, b_ref, o_ref in zip(refs[:n], refs[n:2 * n], refs[2 * n:]):
            o_ref[...] = (a_ref[...].astype(F32) + b_ref[...].astype(F32)).astype(out_dtype)

    mine = [pl.BlockSpec((None, g.shape[1], g.shape[2]), lambda j, cr: (j, cr[0], 0)) for g in gots]
    whole = [pl.BlockSpec((None, g.shape[1], g.shape[2]), lambda j, cr: (j, 0, 0)) for g in gots]
    return pl.pallas_call(
        body, name=name,
        grid_spec=pltpu.PrefetchScalarGridSpec(num_scalar_prefetch=1, grid=(N_CHIPS,), in_specs=mine + whole, out_specs=whole),
        out_shape=[jax.ShapeDtypeStruct(g.shape, out_dtype) for g in gots],
        compiler_params=_cparams(("arbitrary",)))(c, *bufs, *gots)


def _rs_to_chips(accs):
    def make(srcs, dsts, send_sems, recv_sems):
        x, y, c = _mesh_pos()
        return [_remote(src.at[2 * qx + qy], dst.at[k], send_sems, recv_sems, 3 * i + k, (qx, qy, c))
                for i, (src, dst) in enumerate(zip(srcs, dsts)) for k, (qx, qy) in enumerate(_other_chips(x, y))]

    return _copies_exchange(make, accs, [jax.ShapeDtypeStruct((3, a.shape[1], a.shape[2]), a.dtype) for a in accs],
                            3 * len(accs), _chip_peers, CHIPS_ID)


def _rs_chip_add(name, accs, gots, p):
    n = len(accs)
    slot = (0, 1, 0, 2)

    def body(p_ref, *refs):
        me = p_ref[0]
        for own_ref, got_ref, o_ref in zip(refs[:n], refs[n:2 * n], refs[2 * n:]):
            total = None
            for chip in range(N_CHIPS):
                val = own_ref[...].astype(F32)
                for e in (1, 2, 3):
                    val = jnp.where((chip ^ me) == e, got_ref[slot[e]].astype(F32), val)
                total = val if total is None else total + val
            o_ref[...] = total

    own = [pl.BlockSpec((None, a.shape[1], a.shape[2]), lambda i, pr: (pr[0], 0, 0)) for a in accs]
    got = [pl.BlockSpec(g.shape, lambda i, pr: (0, 0, 0)) for g in gots]
    out = [pl.BlockSpec((a.shape[1], a.shape[2]), lambda i, pr: (0, 0)) for a in accs]
    return pl.pallas_call(
        body, name=name,
        grid_spec=pltpu.PrefetchScalarGridSpec(num_scalar_prefetch=1, grid=(1,), in_specs=own + got, out_specs=out),
        out_shape=[jax.ShapeDtypeStruct((a.shape[1], a.shape[2]), F32) for a in accs],
        compiler_params=_cparams(("arbitrary",)))(p, *accs, *gots)


def _rs_share(ress):
    def make(srcs, dsts, send_sems, recv_sems):
        x, y, c = _mesh_pos()
        return [_remote(src, dst, send_sems, recv_sems, i, (x, y, 1 - c)) for i, (src, dst) in enumerate(zip(srcs, dsts))]

    return _copies_exchange(make, ress, [jax.ShapeDtypeStruct(r.shape, F32) for r in ress], len(ress), _sibling_peer,
                            SIBLING_ID)


def _shared(ex):
    south = lax.axis_index("c") == 0
    return [jnp.concatenate([jnp.where(south, r, g), jnp.where(south, g, r)], axis=0) for r, g in zip(ex.ins, ex.outs)]


def _all_to_all_devices(vec):
    def others():
        x, y, c = _mesh_pos()
        return [((1 - x if r & 4 else x), (1 - y if r & 2 else y), (1 - c if r & 1 else c)) for r in range(1, 8)]

    def make(srcs, dsts, send_sems, recv_sems):
        x, y, c = _mesh_pos()
        me = 4 * x + 2 * y + c
        return [_remote(srcs[0], dsts[0].at[me], send_sems, recv_sems, r, peer) for r, peer in enumerate(others())]

    return _copies_exchange(make, [vec], [jax.ShapeDtypeStruct((8, *vec.shape), vec.dtype)], 7, others, ALL_ID)


def _sum_devices(stack):
    def body(s_ref, o_ref):
        total = s_ref[0]
        for d in range(1, 8):
            total = total + s_ref[d]
        o_ref[...] = total

    return pl.pallas_call(body, name="sum_devices", out_shape=jax.ShapeDtypeStruct(stack.shape[1:], F32),
                          compiler_params=pltpu.CompilerParams(vmem_limit_bytes=VMEM_LIMIT))(stack)


def _pad_rows(flat, rows):
    return jnp.concatenate([flat, jnp.zeros((rows * LANES - flat.shape[0],), flat.dtype)]).reshape(rows, LANES)


def _unshard(g4, shape, axis):
    a = g4.reshape(N_CHIPS, *shape)
    if axis == 0:
        return a.reshape(N_CHIPS * shape[0], shape[1])
    return jnp.transpose(a, (1, 0, 2)).reshape(shape[0], N_CHIPS * shape[1])


def _pad_axis0(a, rows):
    return jnp.concatenate([a, jnp.zeros((rows - a.shape[0], *a.shape[1:]), a.dtype)], axis=0)


def _shard_to_strip(name, w):
    _, (shape, axis, rows) = name, {n: (s, ax, r) for n, s, ax, r in BIG}[name]
    w2 = w.reshape(shape).astype(BF16)
    return _pad_axis0(w2.T if axis == 1 else w2, rows)


LOCAL_NAME = dict(w_in="w_in_t", w_q_b="w_q_t", w_kv_b="w_kv_t", w_out="w_out", w_gate="w_gate_t", w_up="w_up_t",
                  w_down="w_down")


WIN_SEGMENTS = ((576, 2112, 0), (2112, 2624, 1536), (0, 256, 2048), (256, 512, 2304), (512, 576, 2560), (2624, 2632, 2688))


def _strips_to_weight(name, g4):
    if name == "w_in":
        return _win_to_pad(g4[:, :IN_SHARD].reshape(IN_COLS, D_MODEL))
    if name == "w_q_b":
        return _qk_to_pad(g4.reshape(HEADS * QK_DIM, LORA))
    return g4.reshape(N_CHIPS * g4.shape[1], g4.shape[2])


def _grad_to_strips(name, g):
    if name == "w_in":
        strips = []
        for q in range(N_CHIPS):
            pieces = []
            for a, b, local in sorted(WIN_SEGMENTS):
                s, e = max(a, q * IN_SHARD), min(b, (q + 1) * IN_SHARD)
                if s < e:
                    pieces.append(g[local + s - a:local + e - a])
            pieces.append(jnp.zeros((IN_SHARD_P - IN_SHARD, D_MODEL), g.dtype))
            strips.append(jnp.concatenate(pieces, axis=0))
        return jnp.stack(strips)
    if name == "w_q_b":
        return _qk_from_pad(g).reshape(N_CHIPS, QK_DIM, LORA)
    return g.reshape(N_CHIPS, g.shape[0] // N_CHIPS, g.shape[1])


class _MeshPlan:
    LATE = dict(attn_norm_fwd=("w_in", "w_q_b", "w_kv_b"), attn_fwd=("w_up",), dn_chunk_fwd=("w_out", "w_gate"),
                ffn_glu_fwd=("w_down",))
    GROUP_A = ("w_down", "w_gate", "w_up", "w_out")
    GROUP_B = ("w_in", "w_q_b", "w_kv_b")

    def __init__(self, w):
        x, y, c = _mesh_pos()
        self.ci = jnp.reshape(c, (1,)).astype(jnp.int32)
        self.pi = jnp.reshape(2 * x + y, (1,)).astype(jnp.int32)
        self.strip = {n: _shard_to_strip(n, w[n]) for n, _, _, _ in BIG}
        self.gathers, self.weights, self.g, self.acc, self.reduced = {}, {}, {}, {}, {}
        self.sibs, self.sib, self.chip, self.share = [], None, None, None

    def gather_small(self, small):
        ex = _all_gather_small(small)
        ex.run("all_gather_small")
        return _gathered(ex)[0]

    def weight(self, local_name):
        if local_name not in self.weights:
            for point, (names, ex) in list(self.gathers.items()):
                if ex.outs is not None:
                    for n, g4 in zip(names, _gathered(ex)):
                        self.weights[LOCAL_NAME[n]] = _strips_to_weight(n, g4)
                    del self.gathers[point]
        return self.weights[local_name]

    def grad(self, local_name, value):
        name = {v: k for k, v in LOCAL_NAME.items()}[local_name]
        self.g[name] = _grad_to_strips(name, value)

    def _pair_add(self, names, gots):
        accs = _rs_pair_add("rs_pair_add_" + names[0], [self.g[n] for n in names], gots, self.ci, BF16)
        self.acc.update(zip(names, accs))

    def _chip_add(self, names, chip):
        return _rs_chip_add("rs_chip_add_" + names[0], [self.acc[n] for n in names], chip.outs, self.pi)

    def _take_shared(self, names, share):
        for n, strip in zip(names, _shared(share)):
            self.reduced[n] = strip

    def host(self, point):
        a, b = self.GROUP_A, self.GROUP_B
        if point in self.LATE:
            names = self.LATE[point]
            ex = _all_gather([self.strip[n] for n in names])
            self.gathers[point] = (names, ex)
            return ex
        if point in ("ffn_in_bwd", "mix_out_bwd"):
            names = dict(ffn_in_bwd=a[:3], mix_out_bwd=a[3:])[point]
            ex = _rs_to_sibling([self.g[n] for n in names])
            self.sibs.append(ex)
            return ex
        if point == "dn_chunk_bwd":
            self._pair_add(a, [o for ex in self.sibs for o in ex.outs])
            self.chip1 = _rs_to_chips([self.acc[n] for n in a[:2]])
            return self.chip1
        if point == "attn_bwd":
            self.chip2 = _rs_to_chips([self.acc[n] for n in a[2:]])
            return self.chip2
        if point == "mla_prep_bwd":
            ress = self._chip_add(a[:2], self.chip1) + self._chip_add(a[2:], self.chip2)
            self.share = _rs_share(ress)
            return self.share
        if point == "in_dx":
            self._take_shared(a, self.share)
            self.sib = _rs_to_sibling([self.g[n] for n in b])
            return self.sib
        if point == "attn_norm_bwd":
            self._pair_add(b, self.sib.outs)
            self.chip = _rs_to_chips([self.acc[n] for n in b])
            return self.chip
        return None

    def last_share(self):
        self.share = _rs_share(self._chip_add(self.GROUP_B, self.chip))
        return self.share

    def finish(self):
        self._take_shared(self.GROUP_B, self.share)
        return self.reduced


def kernel(x, meta_tokens, attn_norm_w, w_in, q_a_norm_w, w_q_b, kv_a_norm_w, w_kv_b, q_norm_w, k_norm_w, mla_out_norm_w, dn_conv_w, dn_A_log, dn_dt_bias, dn_out_norm_w, w_out, ffn_norm_w, w_gate, w_up, ffn_conv_w, ffn_conv_b, w_down, loss_target, m_meta_tokens, m_attn_norm_w, m_w_in, m_q_a_norm_w, m_w_q_b, m_kv_a_norm_w, m_w_kv_b, m_q_norm_w, m_k_norm_w, m_mla_out_norm_w, m_dn_conv_w, m_dn_A_log, m_dn_dt_bias, m_dn_out_norm_w, m_w_out, m_ffn_norm_w, m_w_gate, m_w_up, m_ffn_conv_w, m_ffn_conv_b, m_w_down, v_meta_tokens, v_attn_norm_w, v_w_in, v_q_a_norm_w, v_w_q_b, v_kv_a_norm_w, v_w_kv_b, v_q_norm_w, v_k_norm_w, v_mla_out_norm_w, v_dn_conv_w, v_dn_A_log, v_dn_dt_bias, v_dn_out_norm_w, v_w_out, v_ffn_norm_w, v_w_gate, v_w_up, v_ffn_conv_w, v_ffn_conv_b, v_w_down):
    local = dict(locals())
    w = {n: local[n] for n in WEIGHTS}
    m = {n: local["m_" + n] for n in WEIGHTS}
    v = {n: local["v_" + n] for n in WEIGHTS}
    p = 2 * lax.axis_index("x") + lax.axis_index("y")

    plan = _MeshPlan(w)
    wf = _pad_rows(jnp.concatenate([w[n].reshape(-1) for n, _, _ in SMALL_SHARDED]), SMALL_ROWS)
    gf = plan.gather_small(wf).reshape(N_CHIPS, -1)
    full = {}
    off = 0
    for n, s, ax in SMALL_SHARDED:
        full[n] = _unshard(gf[:, off:off + s[0] * s[1]], s, ax)
        off += s[0] * s[1]
    for n, _ in REPLICATED:
        full[n] = w[n]
    full["ffn_conv_w"] = _ff_to_pad(full["ffn_conv_w"], 1)
    full["ffn_conv_b"] = _ff_to_pad(full["ffn_conv_b"], 1)

    sq, grad_x, g = _local_step(x[0], loss_target[0], full, plan)
    g["ffn_conv_w"] = _ff_from_pad(g["ffn_conv_w"], 1)
    g["ffn_conv_b"] = _ff_from_pad(g["ffn_conv_b"], 1)

    small_all = [n for n, _, _ in SMALL_SHARDED] + [n for n, _ in REPLICATED]
    vec = jnp.concatenate([g[n].reshape(-1) for n in small_all] + [jnp.reshape(0.5 / D_MODEL * jnp.sum(sq), (1,))])
    vec = _pad_rows(vec, -(-vec.shape[0] // (8 * LANES)) * 8)
    a2a = _all_to_all_devices(vec)

    gs, delta, new_m, new_v = {}, {}, {}, {}
    big = {n: (s, ax) for n, s, ax, _ in BIG}

    def adamw_big(names, strips, host=None):
        n = names[0]
        s, ax = big[n]
        if ax == 1 and s[1] % 8:
            there = lambda a: jnp.transpose(a, (2, 0, 1))
            back = lambda a: jnp.transpose(a, (1, 2, 0))
            g3 = strips[n][:s[1]].reshape(s[1], 1, s[0])
            d2, m2, v2 = _adamw_rows3d("adamw_" + n, there(w[n]), g3, there(m[n]), there(v[n]))
            gs[n], delta[n], new_m[n], new_v[n] = back(g3), back(d2), back(m2), back(v2)
            return

        def there(n, a):
            s, ax = big[n]
            return a.reshape(s).T if ax == 1 else a.reshape(s)

        def back(n, a):
            return (a.T if big[n][1] == 1 else a).reshape(w[n].shape)

        outs = _adamw_call("adamw_" + n, [there(n, w[n]) for n in names], [strips[n] for n in names],
                           [there(n, m[n]) for n in names], [there(n, v[n]) for n in names], host=host)
        for n, (g2, d2, m2, v2) in zip(names, outs):
            gs[n], delta[n], new_m[n], new_v[n] = back(n, g2), back(n, d2), back(n, m2), back(n, v2)

    adamw_big(("w_down", "w_gate"), plan.reduced, host=a2a)
    adamw_big(("w_out",), plan.reduced, host=plan.last_share())
    adamw_big(("w_up",), plan.reduced)
    strips = plan.finish()
    for n in plan.GROUP_B:
        adamw_big((n,), strips)
    me = 4 * lax.axis_index("x") + 2 * lax.axis_index("y") + lax.axis_index("c")
    red = _sum_devices(lax.dynamic_update_slice(a2a.outs[0], vec[None], (me, 0, 0))).reshape(-1)
    off = 0
    for n in small_all:
        tot = red[off:off + g[n].size].reshape(g[n].shape)
        off += g[n].size
        shard = {sn: (s, ax) for sn, s, ax in SMALL_SHARDED}.get(n)
        if shard is not None:
            tot = lax.dynamic_slice_in_dim(tot, p * shard[0][1], shard[0][1], axis=1)
        gs[n] = tot
    loss = red[off]
    two_d = lambda a: a.reshape(a.shape[-2], a.shape[-1])
    outs = _adamw_small([two_d(w[n]) for n in small_all], [two_d(gs[n]) for n in small_all],
                        [two_d(m[n]) for n in small_all], [two_d(v[n]) for n in small_all])
    for i, n in enumerate(small_all):
        for dst, src in ((delta, outs[0]), (new_m, outs[1]), (new_v, outs[2])):
            dst[n] = src[i].reshape(w[n].shape)

    grad_out = [gs[n].reshape(w[n].shape) for n in WEIGHTS]
    return (loss, grad_x[None], *grad_out, *[delta[n] for n in WEIGHTS], *[new_m[n] for n in WEIGHTS],
            *[new_v[n] for n in WEIGHTS])
```

```python
import functools
import math

import jax
import jax.numpy as jnp
import numpy as np
from jax import lax
from jax.experimental import pallas as pl
from jax.experimental.pallas import tpu as pltpu

F32 = jnp.float32
BF16 = jnp.bfloat16
HI = lax.Precision.HIGHEST
MESH = pl.DeviceIdType.MESH

N_META = 16
D_MODEL = 1024
HEADS = 4
HEAD = 128
ROPE = 64
QK_DIM = HEAD + ROPE
QK_PAD = 2 * HEAD
LORA = 256
DN_WIDTH = HEADS * HEAD
CHUNK = 64
D_FF = 2816
N_CHIPS = 4
FF_SHARD = D_FF // N_CHIPS
FF_BLOCK = 768
D_FF_P = N_CHIPS * FF_BLOCK
IN_COLS = 2632
IN_SHARD = IN_COLS // N_CHIPS
IN_SHARD_P = 672
NORM_EPS = 1e-6
ROPE_THETA = 10000.0
LANES = 512

ADAM_LR, ADAM_B1, ADAM_B2, ADAM_EPS, ADAM_WD, ADAM_STEP = 0.001, 0.9, 0.999, 1e-08, 0.01, 10

VMEM_LIMIT = 56 * 1024 * 1024

BIG = (("w_in", (1024, 658), 1, IN_SHARD_P), ("w_q_b", (256, 192), 1, 192), ("w_kv_b", (256, 256), 1, 256),
       ("w_out", (256, 1024), 0, 256), ("w_gate", (1024, 704), 1, FF_BLOCK), ("w_up", (1024, 704), 1, FF_BLOCK),
       ("w_down", (704, 1024), 0, FF_BLOCK))
SMALL_SHARDED = (("meta_tokens", (16, 256), 1), ("dn_conv_w", (4, 384), 1), ("ffn_conv_w", (3, 704), 1))
REPLICATED = (("attn_norm_w", 1024), ("q_a_norm_w", 256), ("kv_a_norm_w", 256), ("q_norm_w", 192), ("k_norm_w", 192),
              ("mla_out_norm_w", 128), ("dn_A_log", 4), ("dn_dt_bias", 4), ("dn_out_norm_w", 128), ("ffn_norm_w", 1024),
              ("ffn_conv_b", 2816))
WEIGHTS = ("meta_tokens", "attn_norm_w", "w_in", "q_a_norm_w", "w_q_b", "kv_a_norm_w", "w_kv_b", "q_norm_w", "k_norm_w",
           "mla_out_norm_w", "dn_conv_w", "dn_A_log", "dn_dt_bias", "dn_out_norm_w", "w_out", "ffn_norm_w", "w_gate",
           "w_up", "ffn_conv_w", "ffn_conv_b", "w_down")

SMALL_ROWS = 16


def _cparams(sem):
    return pltpu.CompilerParams(dimension_semantics=sem, vmem_limit_bytes=VMEM_LIMIT)


class _Exchange:
    def __init__(self, prog, ins, out_shape, nsem, peers=None, cid=None):
        self.prog, self.ins, self.out_shape, self.nsem = prog, list(ins), list(out_shape), nsem
        self.peers, self.cid = peers, cid
        self.outs = None

    def sems(self):
        return [pltpu.SemaphoreType.DMA((self.nsem,)), pltpu.SemaphoreType.DMA((self.nsem,))]

    def programs(self, in_refs, out_refs, send_sems, recv_sems):
        start, finish = self.prog(in_refs, out_refs, send_sems, recv_sems)
        if self.cid is None:
            return start, finish
        peers = self.peers()

        def shake_and_start():
            barrier = pltpu.get_barrier_semaphore()
            for peer in peers:
                pl.semaphore_signal(barrier, inc=1, device_id=peer, device_id_type=MESH)
            pl.semaphore_wait(barrier, len(peers))
            start()

        return shake_and_start, finish

    def cparams(self, **kw):
        return pltpu.CompilerParams(has_side_effects=True, collective_id=self.cid, **kw)

    def run(self, name):
        any_spec = pl.BlockSpec(memory_space=pl.ANY)
        n = len(self.ins)

        def body(*refs):
            start, finish = self.programs(refs[:n], refs[n:-2], refs[-2], refs[-1])
            start()
            finish()

        self.outs = pl.pallas_call(
            body, name=name, in_specs=[any_spec] * n, out_specs=[any_spec] * len(self.out_shape),
            out_shape=self.out_shape, scratch_shapes=self.sems(), compiler_params=self.cparams())(*self.ins)
        return self.outs


def _pcall(body, name, grid, in_specs, out_specs, out_shape, args, sem, scratch_shapes=(), host=None):
    single = not isinstance(out_shape, (list, tuple))
    out_specs, out_shape = ([out_specs], [out_shape]) if single else (list(out_specs), list(out_shape))
    if host is None:
        outs = pl.pallas_call(body, name=name, grid=grid, in_specs=list(in_specs), out_specs=out_specs, out_shape=out_shape,
                              scratch_shapes=list(scratch_shapes), compiler_params=_cparams(sem))(*args)
        return outs[0] if single else outs
    any_spec = pl.BlockSpec(memory_space=pl.ANY)
    n_in, n_out, n_scr, nx_in, nx_out = len(in_specs), len(out_specs), len(scratch_shapes), len(host.ins), len(host.out_shape)

    def hosted(*refs):
        c_in, x_in = refs[:n_in], refs[n_in:n_in + nx_in]
        o0 = n_in + nx_in
        c_out, x_out = refs[o0:o0 + n_out], refs[o0 + n_out:o0 + n_out + nx_out]
        s0 = o0 + n_out + nx_out
        start, finish = host.programs(x_in, x_out, refs[s0 + n_scr], refs[s0 + n_scr + 1])
        first = functools.reduce(jnp.logical_and, [pl.program_id(d) == 0 for d in range(len(grid))])
        last = functools.reduce(jnp.logical_and, [pl.program_id(d) == grid[d] - 1 for d in range(len(grid))])
        pl.when(first)(start)
        body(*c_in, *c_out, *refs[s0:s0 + n_scr])
        pl.when(last)(finish)

    outs = pl.pallas_call(
        hosted, name=name, grid=grid, in_specs=list(in_specs) + [any_spec] * nx_in,
        out_specs=out_specs + [any_spec] * nx_out, out_shape=out_shape + host.out_shape,
        scratch_shapes=list(scratch_shapes) + host.sems(),
        compiler_params=host.cparams(dimension_semantics=sem, vmem_limit_bytes=VMEM_LIMIT))(*args, *host.ins)
    host.outs = outs[n_out:]
    return outs[0] if single else outs[:n_out]


NN, NT, TN = ((1,), (0,)), ((1,), (1,)), ((0,), (0,))


def _shift_dims(dims, batch):
    if not batch:
        return (dims, ((), ()))
    return (((dims[0][0] + 1,), (dims[1][0] + 1,)), ((0,), (0,)))


def _make_mm(dims, exact, batch=False):
    def raw(a, b, d):
        dn = _shift_dims(d, batch)
        if exact == "split_lhs":
            ah, bh = a.astype(BF16), b.astype(BF16)
            al = (a - ah.astype(F32)).astype(BF16)
            return lax.dot_general(ah, bh, dn, preferred_element_type=F32) + lax.dot_general(al, bh, dn,
                                                                                              preferred_element_type=F32)
        if exact == "split":
            ah, bh = a.astype(BF16), b.astype(BF16)
            al, bl = (a - ah.astype(F32)).astype(BF16), (b - bh.astype(F32)).astype(BF16)
            dot = lambda p, q: lax.dot_general(p, q, dn, preferred_element_type=F32)
            return dot(ah, bh) + (dot(ah, bl) + dot(al, bh))
        if exact:
            return lax.dot_general(a.astype(F32), b.astype(F32), dn, precision=HI, preferred_element_type=F32)
        return lax.dot_general(a.astype(BF16), b.astype(BF16), dn, preferred_element_type=F32)

    @jax.custom_vjp
    def mm(a, b):
        return raw(a, b, dims)

    def fwd(a, b):
        return raw(a, b, dims), (a, b)

    def bwd(res, g):
        a, b = res
        if dims == NN:
            da, db = raw(g, b, NT), raw(a, g, TN)
        elif dims == NT:
            da, db = raw(g, b, NN), raw(g, a, TN)
        else:
            da, db = raw(b, g, NT), raw(a, g, NN)
        return da.astype(a.dtype), db.astype(b.dtype)

    mm.defvjp(fwd, bwd)
    return mm


_mm = _make_mm(NN, False)
_mm_nt = _make_mm(NT, False)
_mm_tn = _make_mm(TN, False)
_mmx = _make_mm(NN, "split_lhs")
_bmm = _make_mm(NN, False, batch=True)
_bmm_nt = _make_mm(NT, False, batch=True)
_bmm_tn = _make_mm(TN, False, batch=True)
_bmmx = _make_mm(NN, True, batch=True)
_bmms = _make_mm(NN, "split", batch=True)
_bmms_nt = _make_mm(NT, "split", batch=True)
_bmms_tn = _make_mm(TN, "split", batch=True)


@jax.custom_vjp
def _unit_lower_inv(a):
    n = a.shape[-1]
    eye = (lax.broadcasted_iota(jnp.int32, a.shape, 1) == lax.broadcasted_iota(jnp.int32, a.shape, 2)).astype(F32)
    x = -a
    t = eye + x
    for _ in range(max(n.bit_length() - 2, 0)):
        x = _bmms(x, x)
        t = t + _bmms(t, x)
    return t


def _unit_lower_inv_fwd(a):
    t = _unit_lower_inv(a)
    return t, t


def _unit_lower_inv_bwd(t, g):
    return (-_bmms_tn(t, _bmms_nt(g, t)),)


_unit_lower_inv.defvjp(_unit_lower_inv_fwd, _unit_lower_inv_bwd)


def _scan_chunk_rows(x, reverse):
    nb, c, w = x.shape
    y = x.reshape(nb * c, w)
    pos = lax.broadcasted_iota(jnp.int32, y.shape, 0) % c
    step = 1
    while step < c:
        if reverse:
            y = y + jnp.where(pos < c - step, pltpu.roll(y, nb * c - step, 0), 0.0)
        else:
            y = y + jnp.where(pos >= step, pltpu.roll(y, step, 0), 0.0)
        step *= 2
    return y.reshape(nb, c, w)


@jax.custom_vjp
def _chunk_cumsum(x):
    return _scan_chunk_rows(x, False)


_chunk_cumsum.defvjp(lambda x: (_scan_chunk_rows(x, False), None), lambda _, g: (_scan_chunk_rows(g, True),))


def _rms(x, w, n):
    ms = jnp.sum(x * x, axis=-1, keepdims=True) * (1.0 / n)
    return x * lax.rsqrt(ms + NORM_EPS) * w


def _silu(x):
    return x * jax.nn.sigmoid(x)


def _softplus(x):
    return jnp.maximum(x, 0.0) + jnp.log(1.0 + jnp.exp(-jnp.abs(x)))


def _rope(x, cos, sin, perm):
    return x * cos + _mmx(x, perm) * sin


def _mla_prep_fn(rows, consts):
    q_lat, kv_lat, k_pe, cos, sin = rows
    qn = _rms(q_lat, consts["qa_w"], LORA)
    kvn = _rms(kv_lat, consts["kva_w"], LORA)
    outs = []
    for h in range(HEADS):
        q_n = _mm_nt(qn, consts["wq_n"][h])
        q_r = _mm_nt(qn, consts["wq_r"][h])
        rs = lax.rsqrt((jnp.sum(q_n * q_n, -1, keepdims=True) + jnp.sum(q_r * q_r, -1, keepdims=True)) * (1.0 / QK_DIM)
                       + NORM_EPS)
        q_n = q_n * rs * consts["qn_n"]
        q_r = _rope(q_r * rs * consts["qn_r"], cos, sin, consts["perm"])
        k_n = _mm_nt(kvn, consts["wk_n"][h])
        v = _mm_nt(kvn, consts["wv"][h])
        rk = lax.rsqrt((jnp.sum(k_n * k_n, -1, keepdims=True) + jnp.sum(k_pe * k_pe, -1, keepdims=True)) * (1.0 / QK_DIM)
                       + NORM_EPS)
        k_n = k_n * rk * consts["kn_n"]
        k_r = _rope(k_pe * rk * consts["kn_r"], cos, sin, consts["perm"])
        outs += [q_n, q_r, k_n, k_r, v]
    return tuple(outs)


def _attn_fn(q, k, v, row0):
    s = _mm_nt(q, k) * (1.0 / math.sqrt(QK_DIM))
    qpos = row0 + lax.broadcasted_iota(jnp.int32, s.shape, 0)
    kpos = lax.broadcasted_iota(jnp.int32, s.shape, 1)
    s = jnp.where(kpos <= qpos, s, -1e30)
    m = lax.stop_gradient(jnp.max(s, axis=-1, keepdims=True))
    p = jnp.exp(s - m)
    p = p / jnp.sum(p, axis=-1, keepdims=True)
    return _mm(p, v)


def _dn_prep_fn(rows, consts):
    qc, kc, ab = rows
    a_b = _mmx(ab, consts["sel_a"])
    b_b = _mmx(ab, consts["sel_b"])
    beta = jax.nn.sigmoid(b_b)
    g = -jnp.exp(consts["alog"]) * _softplus(a_b + consts["dtb"])
    qs, ks = [], []
    for h in range(HEADS):
        q, k = qc[h], kc[h]
        qs.append(q * lax.rsqrt(jnp.sum(q * q, -1, keepdims=True) + NORM_EPS))
        ks.append(k * lax.rsqrt(jnp.sum(k * k, -1, keepdims=True) + NORM_EPS))
    return tuple(qs), tuple(ks), g, beta


def _dn_chunk_fn(q, k, v, gb, g64, bb):
    nb = q.shape[0]
    ri = lax.broadcasted_iota(jnp.int32, (nb, CHUNK, CHUNK), 1)
    ci = lax.broadcasted_iota(jnp.int32, (nb, CHUNK, CHUNK), 2)
    tri = ri >= ci
    strict = ri > ci
    tril = tri.astype(F32)
    eye = (ri == ci).astype(F32)
    ones = jnp.ones((nb, CHUNK, CHUNK), F32)
    gc = _chunk_cumsum(gb)
    gc64 = _chunk_cumsum(g64)
    grow = _bmmx(ones, eye * gc64)
    diff = gc64 - grow
    decay = jnp.where(tri, jnp.exp(jnp.where(tri, diff, 0.0)), 0.0)
    kb = k * bb
    vb = v * bb
    a = jnp.where(strict, _bmm_nt(kb, k) * decay, 0.0)
    tinv = _unit_lower_inv(a)
    u = _bmm(tinv, vb)
    w = _bmm(tinv, kb * jnp.exp(gc))
    qs = q * (1.0 / math.sqrt(HEAD))
    qk = _bmm_nt(qs, k) * decay
    qg = qs * jnp.exp(gc)
    glast = jnp.sum(gb, axis=1, keepdims=True)
    kdec = k * jnp.exp(glast - gc)
    n_mat = _bmm_tn(kdec, w)
    b_mat = _bmm_tn(kdec, u)
    q_eff = qg - _bmm(qk, w)
    o_own = _bmm(qk, u)
    return n_mat, b_mat, q_eff, o_own, jnp.exp(glast)


def _dn_rec_fn(s, n_mat, b_mat, eg):
    return s * eg - _mm(n_mat, s) + b_mat


def _dn_o_fn(s, q_eff, o_own):
    return _bmm(q_eff, s) + o_own


def _dn_out_fn(o, z, w):
    return _rms(o, w, HEAD) * _silu(z)


def _row_tile(t, parts=8):
    return t // parts if (t // parts) % 16 == 0 else t


def _tile(n, pref, unit):
    best = n
    for cand in range(unit, min(n, pref) + 1, unit):
        if n % cand == 0:
            best = cand
    return best if best <= pref else n


def _rows_call(name, body, rows, consts, outs, accs, r, host=None):
    rows = [a if isinstance(a, tuple) else (a, a.shape[1], 0) for a in rows]
    t = rows[0][0].shape[0]
    zero = lambda nd: (lambda i: (0,) * nd)
    in_specs = [pl.BlockSpec((r, w), functools.partial(lambda i, b: (i, b), b=blk)) for _, w, blk in rows]
    rows = [a for a, _, _ in rows]
    in_specs += [pl.BlockSpec(a.shape, zero(a.ndim), pipeline_mode=pl.Buffered(1)) for a in consts]
    out_shape = [jax.ShapeDtypeStruct((t, w), dt) for w, dt in outs] + [jax.ShapeDtypeStruct(s, F32) for s in accs]
    out_specs = [pl.BlockSpec((r, w), lambda i: (i, 0)) for w, _ in outs] + [pl.BlockSpec(s, zero(len(s))) for s in accs]
    return _pcall(body, name, (t // r,), in_specs, out_specs, out_shape, [*rows, *consts], ("arbitrary",), host=host)


def _accumulate(ref, val):
    @pl.when(pl.program_id(0) == 0)
    def _():
        ref[...] = jnp.zeros(ref.shape, ref.dtype)

    ref[...] += val


def _matmul(name, a, b, dims, out_dtype, res=None, host=None):
    if dims == "nn":
        (m, k), n = a.shape, b.shape[1]
    elif dims == "nt":
        (m, k), n = a.shape, b.shape[0]
    else:
        (k, m), n = a.shape, b.shape[1]
    tm = _tile(m, 1100, 16) if dims != "tn" else _tile(m, 640, 128)
    tn = _tile(n, 1408, 128)
    if dims == "nn":
        a_spec, b_spec, dn = pl.BlockSpec((tm, k), lambda i, j: (i, 0)), pl.BlockSpec((k, tn), lambda i, j: (0, j)), NN
    elif dims == "nt":
        a_spec, b_spec, dn = pl.BlockSpec((tm, k), lambda i, j: (i, 0)), pl.BlockSpec((tn, k), lambda i, j: (j, 0)), NT
    else:
        a_spec, b_spec, dn = pl.BlockSpec((k, tm), lambda i, j: (0, i)), pl.BlockSpec((k, tn), lambda i, j: (0, j)), TN
    o_spec = pl.BlockSpec((tm, tn), lambda i, j: (i, j))

    def body(*refs):
        a_ref, b_ref, o_ref = refs[0], refs[1], refs[-1]
        acc = lax.dot_general(a_ref[...].astype(BF16), b_ref[...].astype(BF16), (dn, ((), ())),
                              preferred_element_type=F32)
        if res is not None:
            acc = acc + refs[2][...]
        o_ref[...] = acc.astype(out_dtype)

    ins = [a, b] + ([res] if res is not None else [])
    specs = [a_spec, b_spec] + ([o_spec] if res is not None else [])
    return _pcall(body, name, (m // tm, n // tn), specs, o_spec, jax.ShapeDtypeStruct((m, n), out_dtype), ins,
                  ("arbitrary", "arbitrary"), host=host)


def _rms_fwd(name, h, w, host=None):
    n = h.shape[1]

    def body(h_ref, w_ref, o_ref):
        o_ref[...] = _rms(h_ref[...], w_ref[...], n).astype(BF16)

    return _rows_call(name, body, [h], [w], [(n, BF16)], [], _row_tile(h.shape[0]), host=host)[0]


def _rms_bwd(name, h, w, cts, resid, host=None):
    n = h.shape[1]
    nct = len(cts)

    def body(*refs):
        h_ref, ct_refs, r_ref, w_ref = refs[0], refs[1:1 + nct], refs[1 + nct], refs[2 + nct]
        dh_ref, dh16_ref, dw_ref = refs[-3], refs[-2], refs[-1]
        ct = ct_refs[0][...].astype(F32)
        for c in ct_refs[1:]:
            ct = ct + c[...].astype(F32)
        _, vjp = jax.vjp(lambda x, ww: _rms(x, ww, n), h_ref[...], w_ref[...])
        dh, dw = vjp(ct)
        dh = dh + r_ref[...]
        dh_ref[...] = dh
        dh16_ref[...] = dh.astype(BF16)
        _accumulate(dw_ref, dw)

    return _rows_call(name, body, [h, *cts, resid], [w], [(n, F32), (n, BF16)], [(1, n)], _row_tile(h.shape[0]), host=host)


def _mla_consts_from_refs(qa, wq, kva, wkv, qn, kn, perm):
    f = lambda r: r[...].astype(F32)
    return dict(
        qa_w=f(qa), kva_w=f(kva), perm=f(perm),
        wq_n=[wq[h * QK_PAD:h * QK_PAD + HEAD, :].astype(F32) for h in range(HEADS)],
        wq_r=[wq[h * QK_PAD + HEAD:(h + 1) * QK_PAD, :].astype(F32) for h in range(HEADS)],
        wk_n=[wkv[h * QK_PAD:h * QK_PAD + HEAD, :].astype(F32) for h in range(HEADS)],
        wv=[wkv[h * QK_PAD + HEAD:(h + 1) * QK_PAD, :].astype(F32) for h in range(HEADS)],
        qn_n=qn[:, 0:HEAD], qn_r=qn[:, HEAD:QK_PAD], kn_n=kn[:, 0:HEAD], kn_r=kn[:, HEAD:QK_PAD])


def _mla_prep_fwd(q_lat, kv_lat, k_pe, cos, sin, qa, wq, kva, wkv, qn, kn, perm):
    def body(ql, kvl, kp, c, s, qa_r, wq_r, kva_r, wkv_r, qn_r, kn_r, p_r, q_out, k_out, v_out):
        consts = _mla_consts_from_refs(qa_r, wq_r, kva_r, wkv_r, qn_r, kn_r, p_r)
        outs = _mla_prep_fn((ql[...], kvl[...], kp[...], c[...], s[...]), consts)
        for h in range(HEADS):
            q_n, q_r, k_n, k_r, v = outs[5 * h:5 * h + 5]
            q_out[:, h * QK_PAD:h * QK_PAD + HEAD] = q_n.astype(BF16)
            q_out[:, h * QK_PAD + HEAD:(h + 1) * QK_PAD] = q_r.astype(BF16)
            k_out[:, h * QK_PAD:h * QK_PAD + HEAD] = k_n.astype(BF16)
            k_out[:, h * QK_PAD + HEAD:(h + 1) * QK_PAD] = k_r.astype(BF16)
            v_out[:, h * HEAD:(h + 1) * HEAD] = v.astype(BF16)

    return _rows_call("mla_prep_fwd", body, [q_lat, kv_lat, k_pe, cos, sin], [qa, wq, kva, wkv, qn, kn, perm],
                      [(HEADS * QK_PAD, BF16), (HEADS * QK_PAD, BF16), (DN_WIDTH, BF16)], [], _row_tile(cos.shape[0], 4))


def _mla_prep_bwd(q_lat, kv_lat, k_pe, cos, sin, dq, dk, dv, qa, wq, kva, wkv, qn, kn, perm, host=None):
    def body(ql, kvl, kp, c, s, dq_r, dk_r, dv_r, qa_r, wq_r, kva_r, wkv_r, qn_r, kn_r, p_r,
             dql, dkvl, dkp, dqa, dwq, dkva, dwkv, dqn, dkn):
        consts = _mla_consts_from_refs(qa_r, wq_r, kva_r, wkv_r, qn_r, kn_r, p_r)
        cc, ss, pm = c[...], s[...], consts.pop("perm")
        _, vjp = jax.vjp(lambda rows, cs: _mla_prep_fn((*rows, cc, ss), dict(cs, perm=pm)), (ql[...], kvl[...], kp[...]),
                         consts)
        cts = []
        for h in range(HEADS):
            cts += [dq_r[:, h * QK_PAD:h * QK_PAD + HEAD], dq_r[:, h * QK_PAD + HEAD:(h + 1) * QK_PAD],
                    dk_r[:, h * QK_PAD:h * QK_PAD + HEAD], dk_r[:, h * QK_PAD + HEAD:(h + 1) * QK_PAD],
                    dv_r[:, h * HEAD:(h + 1) * HEAD]]
        (d_ql, d_kvl, d_kp), dc = vjp(tuple(cts))
        dql[...] = d_ql.astype(BF16)
        dkvl[...] = d_kvl.astype(BF16)
        dkp[...] = d_kp.astype(BF16)
        first = pl.program_id(0) == 0

        def acc(ref, sl, val):
            @pl.when(first)
            def _():
                ref[sl] = val

            @pl.when(jnp.logical_not(first))
            def _():
                ref[sl] += val

        full = (slice(None), slice(None))
        acc(dqa, full, dc["qa_w"])
        acc(dkva, full, dc["kva_w"])
        for h in range(HEADS):
            acc(dwq, (slice(h * QK_PAD, h * QK_PAD + HEAD), slice(None)), dc["wq_n"][h])
            acc(dwq, (slice(h * QK_PAD + HEAD, (h + 1) * QK_PAD), slice(None)), dc["wq_r"][h])
            acc(dwkv, (slice(h * QK_PAD, h * QK_PAD + HEAD), slice(None)), dc["wk_n"][h])
            acc(dwkv, (slice(h * QK_PAD + HEAD, (h + 1) * QK_PAD), slice(None)), dc["wv"][h])
        acc(dqn, (slice(None), slice(0, HEAD)), dc["qn_n"])
        acc(dqn, (slice(None), slice(HEAD, QK_PAD)), dc["qn_r"])
        acc(dkn, (slice(None), slice(0, HEAD)), dc["kn_n"])
        acc(dkn, (slice(None), slice(HEAD, QK_PAD)), dc["kn_r"])

    return _rows_call("mla_prep_bwd", body, [q_lat, kv_lat, k_pe, cos, sin, dq, dk, dv],
                      [qa, wq, kva, wkv, qn, kn, perm],
                      [(LORA, BF16), (LORA, BF16), (HEAD, BF16)],
                      [(1, LORA), wq.shape, (1, LORA), wkv.shape, (1, QK_PAD), (1, QK_PAD)], _row_tile(cos.shape[0], 4),
                      host=host)


ATTN_Q_ROWS = 512


def _attn_blocks(t):
    return [(r0, min(ATTN_Q_ROWS, t - r0)) for r0 in range(0, t, ATTN_Q_ROWS)]


def _attn_fwd(q, k, v, host=None):
    t = q.shape[0]

    def body(q_ref, k_ref, v_ref, o_ref):
        for r0, rows in _attn_blocks(t):
            ext = r0 + rows
            o_ref[r0:ext, :] = _attn_fn(q_ref[r0:ext, :], k_ref[0:ext, :], v_ref[0:ext, :], r0)

    qk_spec = pl.BlockSpec((t, QK_PAD), lambda h: (0, h))
    v_spec = pl.BlockSpec((t, HEAD), lambda h: (0, h))
    return _pcall(body, "attn_fwd", (HEADS,), [qk_spec, qk_spec, v_spec], v_spec,
                  jax.ShapeDtypeStruct((t, HEADS * HEAD), F32), [q, k, v], ("arbitrary",), host=host)


def _attn_bwd(q, k, v, do, host=None):
    t = q.shape[0]

    def body(q_ref, k_ref, v_ref, do_ref, dq_ref, dk_ref, dv_ref):
        dk_ref[...] = jnp.zeros(dk_ref.shape, F32)
        dv_ref[...] = jnp.zeros(dv_ref.shape, F32)
        for r0, rows in _attn_blocks(t):
            ext = r0 + rows
            _, vjp = jax.vjp(functools.partial(_attn_fn, row0=r0), q_ref[r0:ext, :].astype(F32),
                             k_ref[0:ext, :].astype(F32), v_ref[0:ext, :].astype(F32))
            dq, dk, dv = vjp(do_ref[r0:ext, :])
            dq_ref[r0:ext, :] = dq
            dk_ref[0:ext, :] += dk
            dv_ref[0:ext, :] += dv

    qk_spec = pl.BlockSpec((t, QK_PAD), lambda h: (0, h))
    v_spec = pl.BlockSpec((t, HEAD), lambda h: (0, h))
    return _pcall(body, "attn_bwd", (HEADS,), [qk_spec, qk_spec, v_spec, v_spec], [qk_spec, qk_spec, v_spec],
                  [jax.ShapeDtypeStruct((t, HEADS * QK_PAD), F32), jax.ShapeDtypeStruct((t, HEADS * QK_PAD), F32),
                   jax.ShapeDtypeStruct((t, HEADS * HEAD), F32)], [q, k, v, do], ("arbitrary",), host=host)


def _mix_out_proj(o_mla, o_dn, z, w_mla, w_dn, w_out, h0, w_ffn):
    def body(om_ref, od_ref, z_ref, h0_ref, wm_ref, wd_ref, wo_ref, wf_ref, mixed_ref, h1_ref, n2_ref):
        for h in range(HEADS):
            sl = slice(h * HEAD, (h + 1) * HEAD)
            mixed_ref[:, sl] = _rms(om_ref[:, sl], wm_ref[...], HEAD).astype(BF16)
            mixed_ref[:, DN_WIDTH + h * HEAD:DN_WIDTH + (h + 1) * HEAD] = _dn_out_fn(od_ref[:, sl], z_ref[:, sl],
                                                                                     wd_ref[...]).astype(BF16)
        h1 = _mm(mixed_ref[...], wo_ref[...]) + h0_ref[...]
        h1_ref[...] = h1
        n2_ref[...] = _rms(h1, wf_ref[...], D_MODEL).astype(BF16)

    return _rows_call("mix_out_proj", body, [o_mla, o_dn, z, h0], [w_mla, w_dn, w_out, w_ffn],
                      [(D_MODEL, BF16), (D_MODEL, F32), (D_MODEL, BF16)], [], _row_tile(o_mla.shape[0], 4))


def _down_proj_loss(act, w_down, h1, tgt, n_valid):
    t, n = h1.shape
    r = _row_tile(t, 4)

    def body(a_ref, h_ref, t_ref, w_ref, dy_ref, dy16_ref, acc_ref):
        h2 = _mm(a_ref[...], w_ref[...]) + h_ref[...]
        rows = pl.program_id(0) * r + lax.broadcasted_iota(jnp.int32, (r, n), 0)
        valid = jnp.logical_and(rows >= N_META, rows < n_valid)
        e = jnp.where(valid, h2 - t_ref[...], 0.0)
        dy = e * (1.0 / n)
        dy_ref[...] = dy
        dy16_ref[...] = dy.astype(BF16)
        _accumulate(acc_ref, jnp.sum(e * e, axis=0, keepdims=True))

    return _rows_call("down_proj_loss", body, [act, h1, tgt], [w_down], [(n, F32), (n, BF16)], [(1, n)], r)


def _in_proj_bwd_x(pieces, win, host=None):
    offs = np.cumsum([0] + [p.shape[1] for p in pieces])

    def body(*refs):
        p_refs, w_ref, o_ref = refs[:len(pieces)], refs[len(pieces)], refs[-1]
        acc = None
        for i, p_ref in enumerate(p_refs):
            part = _mm(p_ref[...], w_ref[int(offs[i]):int(offs[i + 1]), :])
            acc = part if acc is None else acc + part
        o_ref[...] = acc.astype(BF16)

    return _rows_call("in_dx", body, pieces, [win], [(win.shape[1], BF16)], [], _row_tile(pieces[0].shape[0], 4), host=host)[0]


def _in_proj_bwd_w(pieces, u):
    offs = np.cumsum([0] + [p.shape[1] for p in pieces])

    def body(*refs):
        p_refs, u_ref, o_ref = refs[:len(pieces)], refs[len(pieces)], refs[-1]
        first = pl.program_id(0) == 0
        uv = u_ref[...]
        for i, p_ref in enumerate(p_refs):
            rows = slice(int(offs[i]), int(offs[i + 1]))
            part = _mm_tn(p_ref[...], uv)

            @pl.when(first)
            def _():
                o_ref[rows, :] = part

            @pl.when(jnp.logical_not(first))
            def _():
                o_ref[rows, :] += part

    return _rows_call("in_dw", body, [*pieces, u], [], [], [(int(offs[-1]), u.shape[1])], _row_tile(u.shape[0], 2))[0]


def _ffn_in_bwd(dgpre, dup, w_gate_t, w_up_t, h1, dy, w_ffn, host=None):
    n = h1.shape[1]

    def body(dg_ref, du_ref, h_ref, dy_ref, wg_ref, wu_ref, w_ref, dh_ref, dh16_ref, dw_ref):
        ct = _mm(dg_ref[...], wg_ref[...]) + _mm(du_ref[...], wu_ref[...])
        _, vjp = jax.vjp(lambda x, ww: _rms(x, ww, n), h_ref[...], w_ref[...])
        dh, dw = vjp(ct)
        dh = dh + dy_ref[...]
        dh_ref[...] = dh
        dh16_ref[...] = dh.astype(BF16)
        _accumulate(dw_ref, dw)

    return _rows_call("ffn_in_bwd", body, [dgpre, dup, h1, dy], [w_gate_t, w_up_t, w_ffn], [(n, F32), (n, BF16)], [(1, n)],
                      _row_tile(h1.shape[0], 4), host=host)


def _mix_out_bwd(o_mla, o_dn, z, dh1, w_out, w_mla, w_dn, host=None):
    def body(om_ref, od_ref, z_ref, dh_ref, wo_ref, wm_ref, wd_ref, dom_ref, dod_ref, dz_ref, dwm_ref, dwd_ref):
        dwm = dwd = None
        for h in range(HEADS):
            sl = slice(h * HEAD, (h + 1) * HEAD)
            _, vjp = jax.vjp(lambda o, w: _rms(o, w, HEAD), om_ref[:, sl], wm_ref[...])
            do, dw = vjp(_mm_nt(dh_ref[...], wo_ref[sl, :]))
            dom_ref[:, sl] = do
            dwm = dw if dwm is None else dwm + dw
            _, vjp = jax.vjp(_dn_out_fn, od_ref[:, sl], z_ref[:, sl], wd_ref[...])
            do, dz, dw = vjp(_mm_nt(dh_ref[...], wo_ref[DN_WIDTH + h * HEAD:DN_WIDTH + (h + 1) * HEAD, :]))
            dod_ref[:, sl] = do
            dz_ref[:, sl] = dz.astype(BF16)
            dwd = dw if dwd is None else dwd + dw
        _accumulate(dwm_ref, dwm)
        _accumulate(dwd_ref, dwd)

    return _rows_call("mix_out_bwd", body, [o_mla, o_dn, z, dh1], [w_out, w_mla, w_dn],
                      [(DN_WIDTH, F32), (DN_WIDTH, F32), (DN_WIDTH, BF16)], [(1, HEAD), (1, HEAD)],
                      _row_tile(o_mla.shape[0], 4), host=host)


def _shift_down(x, s):
    if s == 0:
        return x
    rows = lax.broadcasted_iota(jnp.int32, x.shape, 0)
    return jnp.where(rows >= s, pltpu.roll(x, s, 0), 0.0)


def _shift_up(x, s):
    if s == 0:
        return x
    t = x.shape[0]
    rows = lax.broadcasted_iota(jnp.int32, x.shape, 0)
    return jnp.where(rows < t - s, pltpu.roll(x, t - s, 0), 0.0)


def _col_call(name, body, cols, taps, outs, tap_outs, cw, host=None):
    t, c = cols[0].shape[0], taps[0].shape[1]
    in_specs = [pl.BlockSpec((t, cw), lambda j: (0, j)) for _ in cols]
    in_specs += [pl.BlockSpec((a.shape[0], cw), lambda j: (0, j)) for a in taps]
    out_shape = [jax.ShapeDtypeStruct((t, c), dt) for dt in outs] + [jax.ShapeDtypeStruct((n, c), F32) for n in tap_outs]
    out_specs = [pl.BlockSpec((t, cw), lambda j: (0, j)) for _ in outs]
    out_specs += [pl.BlockSpec((n, cw), lambda j: (0, j)) for n in tap_outs]
    return _pcall(body, name, (c // cw,), in_specs, out_specs, out_shape, [*cols, *taps], ("arbitrary",), host=host)


def _causal_conv(x, w_ref, width, zero_tail=False):
    down = (lambda a, s: pltpu.roll(a, s, 0)) if zero_tail else _shift_down
    acc = w_ref[width - 1:width, :] * x
    for j in range(width - 1):
        acc = acc + w_ref[j:j + 1, :] * down(x, width - 1 - j)
    return acc


def _causal_conv_bwd(x, dpre, w_ref, dx_ref, dw_ref, width, zero_tail=False):
    t = x.shape[0]
    down = (lambda a, s: pltpu.roll(a, s, 0)) if zero_tail else _shift_down
    up = (lambda a, s: pltpu.roll(a, t - s, 0)) if zero_tail else _shift_up
    dx = w_ref[width - 1:width, :] * dpre
    dw_ref[width - 1:width, :] = jnp.sum(dpre * x, axis=0, keepdims=True)
    for j in range(width - 1):
        s = width - 1 - j
        dx = dx + w_ref[j:j + 1, :] * up(dpre, s)
        dw_ref[j:j + 1, :] = jnp.sum(dpre * down(x, s), axis=0, keepdims=True)
    dx_ref[...] = dx.astype(dx_ref.dtype)


def _dsilu(x):
    sg = jax.nn.sigmoid(x)
    return sg * (1.0 + x * (1.0 - sg))


def _dn_conv_fwd(x, w):
    def body(x_ref, w_ref, y_ref):
        y_ref[...] = _silu(_causal_conv(x_ref[...], w_ref, 4, zero_tail=True))

    return _col_call("dn_conv_fwd", body, [x], [w], [F32], [], 256)[0]


def _dn_conv_bwd(x, w, dy):
    def body(x_ref, dy_ref, w_ref, dx_ref, dw_ref):
        xv = x_ref[...]
        dpre = dy_ref[...] * _dsilu(_causal_conv(xv, w_ref, 4, zero_tail=True))
        _causal_conv_bwd(xv, dpre, w_ref, dx_ref, dw_ref, 4, zero_tail=True)

    return _col_call("dn_conv_bwd", body, [x, dy], [w], [BF16], [4], 256)


def _ffn_glu_fwd(n2, w_gate_t, w_up_t, w, b, host=None):
    t, k = n2.shape
    c, cw = w_gate_t.shape[0], 256

    def body(n_ref, wg_ref, wu_ref, w_ref, b_ref, g_ref, u_ref, a_ref):
        nv = n_ref[...]
        g16 = _mm_nt(nv, wg_ref[...]).astype(BF16)
        u16 = _mm_nt(nv, wu_ref[...]).astype(BF16)
        g_ref[...] = g16
        u_ref[...] = u16
        gate = _causal_conv(g16.astype(F32), w_ref, 3) + b_ref[...]
        a_ref[...] = (_silu(gate) * u16.astype(F32)).astype(BF16)

    wspec = pl.BlockSpec((cw, k), lambda j: (j, 0))
    col = pl.BlockSpec((t, cw), lambda j: (0, j))
    in_specs = [pl.BlockSpec((t, k), lambda j: (0, 0)), wspec, wspec, pl.BlockSpec((w.shape[0], cw), lambda j: (0, j)),
                pl.BlockSpec((1, cw), lambda j: (0, j))]
    return _pcall(body, "ffn_glu_fwd", (c // cw,), in_specs, [col] * 3, [jax.ShapeDtypeStruct((t, c), BF16)] * 3,
                  [n2, w_gate_t, w_up_t, w, b], ("arbitrary",), host=host)


def _ffn_glu_bwd(gpre, up, dy16, w_down, w, b):
    t, k = dy16.shape
    c, cw = w_down.shape[0], 256

    def body(g_ref, u_ref, dy_ref, wd_ref, w_ref, b_ref, dg_ref, du_ref, dw_ref, db_ref):
        gv = g_ref[...].astype(F32)
        gate = _causal_conv(gv, w_ref, 3) + b_ref[...]
        da = _mm_nt(dy_ref[...], wd_ref[...])
        sg = jax.nn.sigmoid(gate)
        du_ref[...] = (da * (gate * sg)).astype(BF16)
        dgate = da * u_ref[...].astype(F32) * (sg * (1.0 + gate * (1.0 - sg)))
        db_ref[...] = jnp.sum(dgate, axis=0, keepdims=True)
        _causal_conv_bwd(gv, dgate, w_ref, dg_ref, dw_ref, 3)

    col = pl.BlockSpec((t, cw), lambda j: (0, j))
    taps = lambda rows: pl.BlockSpec((rows, cw), lambda j: (0, j))
    in_specs = [col, col, pl.BlockSpec((t, k), lambda j: (0, 0)), pl.BlockSpec((cw, k), lambda j: (j, 0)),
                taps(w.shape[0]), taps(1)]
    return _pcall(body, "ffn_glu_bwd", (c // cw,), in_specs, [col, col, taps(w.shape[0]), taps(1)],
                  [jax.ShapeDtypeStruct((t, c), BF16)] * 2 + [jax.ShapeDtypeStruct((w.shape[0], c), F32),
                                                             jax.ShapeDtypeStruct((1, c), F32)],
                  [gpre, up, dy16, w_down, w, b], ("arbitrary",))


def _dn_prep_consts(sa, sb, al, dt):
    return dict(sel_a=sa[...], sel_b=sb[...], alog=al[...], dtb=dt[...])


def _dn_prep_fwd(conv, ab, sel_a, sel_b, alog, dtb):
    def body(c_ref, ab_ref, sa, sb, al, dt, q_out, k_out, g_out, b_out):
        qc = tuple(c_ref[:, h * HEAD:(h + 1) * HEAD] for h in range(HEADS))
        kc = tuple(c_ref[:, DN_WIDTH + h * HEAD:DN_WIDTH + (h + 1) * HEAD] for h in range(HEADS))
        qs, ks, g, beta = _dn_prep_fn((qc, kc, ab_ref[...]), _dn_prep_consts(sa, sb, al, dt))
        for h in range(HEADS):
            q_out[:, h * HEAD:(h + 1) * HEAD] = qs[h]
            k_out[:, h * HEAD:(h + 1) * HEAD] = ks[h]
        g_out[...] = g
        b_out[...] = beta

    return _rows_call("dn_prep_fwd", body, [conv, ab], [sel_a, sel_b, alog, dtb], [(DN_WIDTH, F32)] * 4, [],
                      _row_tile(conv.shape[0], 4))


def _dn_prep_bwd(conv, ab, dq, dk, dv, dg, db, sel_a, sel_b, alog, dtb):
    def body(c_ref, ab_ref, dq_r, dk_r, dv_r, dg_r, db_r, sa, sb, al, dt, dc_out, dab_out, dal_out, ddt_out):
        qc = tuple(c_ref[:, h * HEAD:(h + 1) * HEAD] for h in range(HEADS))
        kc = tuple(c_ref[:, DN_WIDTH + h * HEAD:DN_WIDTH + (h + 1) * HEAD] for h in range(HEADS))
        consts = _dn_prep_consts(sa, sb, al, dt)
        sel = dict(sel_a=consts["sel_a"], sel_b=consts["sel_b"])
        _, vjp = jax.vjp(lambda rows, ad: _dn_prep_fn(rows, {**sel, **ad}), (qc, kc, ab_ref[...]),
                         dict(alog=consts["alog"], dtb=consts["dtb"]))
        cq = tuple(dq_r[:, h * HEAD:(h + 1) * HEAD] for h in range(HEADS))
        ck = tuple(dk_r[:, h * HEAD:(h + 1) * HEAD] for h in range(HEADS))
        (dqc, dkc, dab), dad = vjp((cq, ck, dg_r[...], db_r[...]))
        for h in range(HEADS):
            dc_out[:, h * HEAD:(h + 1) * HEAD] = dqc[h]
            dc_out[:, DN_WIDTH + h * HEAD:DN_WIDTH + (h + 1) * HEAD] = dkc[h]
        dc_out[:, 2 * DN_WIDTH:3 * DN_WIDTH] = dv_r[...]
        dab_out[...] = dab.astype(BF16)
        _accumulate(dal_out, dad["alog"])
        _accumulate(ddt_out, dad["dtb"])

    return _rows_call("dn_prep_bwd", body, [conv, ab, dq, dk, dv, dg, db], [sel_a, sel_b, alog, dtb],
                      [(3 * DN_WIDTH, F32), (HEAD, BF16)], [(1, DN_WIDTH), (1, DN_WIDTH)], _row_tile(conv.shape[0], 4))


def _chunk_batch(t):
    nc = t // CHUNK
    return nc // 2 if nc % 2 == 0 else nc


def _dn_chunk_specs(t, nb):
    rows = nb * CHUNK
    blk = pl.BlockSpec((rows, HEAD), lambda h, b: (b, h))
    vblk = pl.BlockSpec((rows, HEAD), lambda h, b: (b, 2 * HEADS + h))
    mat = pl.BlockSpec((nb, HEAD, HEAD), lambda h, b: (b, h, 0))
    return rows, blk, vblk, mat


def _dn_chunk_fwd(qn, kn, conv, g, beta, host=None):
    t = qn.shape[0]
    nb = _chunk_batch(t)
    rows, blk, vblk, mat = _dn_chunk_specs(t, nb)

    def body(q_ref, k_ref, v_ref, g_ref, b_ref, n_o, b_o, qe_o, oo_o, eg_o):
        r3 = lambda x: x.reshape(nb, CHUNK, x.shape[-1])
        n_mat, b_mat, q_eff, o_own, eg = _dn_chunk_fn(r3(q_ref[...]), r3(k_ref[...]), r3(v_ref[...]), r3(g_ref[...]),
                                                      r3(g_ref[:, 0:CHUNK]), r3(b_ref[...]))
        n_o[...] = n_mat
        b_o[...] = b_mat
        qe_o[...] = q_eff.reshape(rows, HEAD)
        oo_o[...] = o_own.reshape(rows, HEAD)
        eg_o[...] = jnp.broadcast_to(eg, (nb, HEAD, HEAD))

    nc = t // CHUNK
    mats = jax.ShapeDtypeStruct((nc, DN_WIDTH, HEAD), F32)
    rowsd = jax.ShapeDtypeStruct((t, DN_WIDTH), F32)
    return _pcall(body, "dn_chunk_fwd", (HEADS, t // rows), [blk, blk, vblk, blk, blk], [mat, mat, blk, blk, mat],
                  [mats, mats, rowsd, rowsd, mats], [qn, kn, conv, g, beta], ("arbitrary", "arbitrary"), host=host)


def _dn_chunk_bwd(qn, kn, conv, g, beta, sall, gall, dq_eff, do, host=None):
    t = qn.shape[0]
    nb = _chunk_batch(t)
    rows, blk, vblk, mat = _dn_chunk_specs(t, nb)

    def body(q_ref, k_ref, v_ref, g_ref, b_ref, s_ref, ga_ref, dqe_ref, do_ref, dq_o, dk_o, dv_o, dg_o, db_o):
        r3 = lambda x: x.reshape(nb, CHUNK, x.shape[-1])
        _, vjp = jax.vjp(_dn_chunk_fn, r3(q_ref[...]), r3(k_ref[...]), r3(v_ref[...]), r3(g_ref[...]),
                         r3(g_ref[:, 0:CHUNK]), r3(b_ref[...]))
        s, ga = s_ref[...], ga_ref[...]
        d_n = -_bmm_nt(ga, s)
        d_eg = jnp.sum(ga * s, axis=1, keepdims=True)
        dq, dk, dv, dg, dg64, db = vjp((d_n, ga, r3(dqe_ref[...]), r3(do_ref[...]), d_eg))
        for o_ref, val in zip((dq_o, dk_o, dv_o, dg_o, db_o), (dq, dk, dv, dg, db)):
            o_ref[...] = val.reshape(rows, HEAD)
        dg_o[:, 0:CHUNK] += dg64.reshape(rows, CHUNK)

    return _pcall(body, "dn_chunk_bwd", (HEADS, t // rows), [blk, blk, vblk, blk, blk, mat, mat, blk, blk], [blk] * 5,
                  [jax.ShapeDtypeStruct((t, DN_WIDTH), F32)] * 5, [qn, kn, conv, g, beta, sall, gall, dq_eff, do],
                  ("arbitrary", "arbitrary"), host=host)


def _dn_rec_fwd(n_mat, b_mat, eg, host=None):
    nc = n_mat.shape[0]
    nb = _chunk_batch(nc * CHUNK)
    spec = pl.BlockSpec((nb, DN_WIDTH, HEAD), lambda i: (i, 0, 0))

    def body(n_ref, b_ref, eg_ref, sall_ref, s_scr):
        @pl.when(pl.program_id(0) == 0)
        def _():
            s_scr[...] = jnp.zeros(s_scr.shape, F32)

        for j in range(nb):
            sall_ref[j] = s_scr[...]
            for h in range(HEADS):
                sl = slice(h * HEAD, (h + 1) * HEAD)
                s_scr[sl, :] = _dn_rec_fn(s_scr[sl, :], n_ref[j, sl, :], b_ref[j, sl, :],
                                          eg_ref[j, h * HEAD:h * HEAD + 1, :])

    return _pcall(body, "dn_rec_fwd", (nc // nb,), [spec] * 3, spec, jax.ShapeDtypeStruct((nc, DN_WIDTH, HEAD), F32),
                  [n_mat, b_mat, eg], ("arbitrary",), scratch_shapes=[pltpu.VMEM((DN_WIDTH, HEAD), F32)], host=host)


def _dn_rec_bwd(n_mat, eg, ds_out, host=None):
    nc = n_mat.shape[0]
    nb = _chunk_batch(nc * CHUNK)
    steps = nc // nb
    spec = pl.BlockSpec((nb, DN_WIDTH, HEAD), lambda i: (steps - 1 - i, 0, 0))

    def body(n_ref, eg_ref, dso_ref, gall_ref, g_scr):
        @pl.when(pl.program_id(0) == 0)
        def _():
            g_scr[...] = jnp.zeros(g_scr.shape, F32)

        for j in reversed(range(nb)):
            gall_ref[j] = g_scr[...]
            for h in range(HEADS):
                sl = slice(h * HEAD, (h + 1) * HEAD)
                gv = g_scr[sl, :]
                g_scr[sl, :] = (gv * eg_ref[j, h * HEAD:h * HEAD + 1, :] - _mm_tn(n_ref[j, sl, :], gv)
                                + dso_ref[j, sl, :])

    return _pcall(body, "dn_rec_bwd", (steps,), [spec] * 3, spec, jax.ShapeDtypeStruct((nc, DN_WIDTH, HEAD), F32),
                  [n_mat, eg, ds_out], ("arbitrary",), scratch_shapes=[pltpu.VMEM((DN_WIDTH, HEAD), F32)], host=host)


def _dn_o_fwd(sall, q_eff, o_own):
    t = q_eff.shape[0]
    nb = _chunk_batch(t)
    rows, blk, _, mat = _dn_chunk_specs(t, nb)

    def body(s_ref, qe_ref, oo_ref, o_ref):
        r3 = lambda x: x.reshape(nb, CHUNK, HEAD)
        o_ref[...] = _dn_o_fn(s_ref[...], r3(qe_ref[...]), r3(oo_ref[...])).reshape(rows, HEAD)

    return _pcall(body, "dn_o_fwd", (HEADS, t // rows), [mat, blk, blk], blk, jax.ShapeDtypeStruct((t, DN_WIDTH), F32),
                  [sall, q_eff, o_own], ("arbitrary", "arbitrary"))


def _dn_o_bwd(sall, q_eff, do, host=None):
    t = q_eff.shape[0]
    nb = _chunk_batch(t)
    rows, blk, _, mat = _dn_chunk_specs(t, nb)

    def body(s_ref, qe_ref, do_ref, dqe_ref, ds_ref):
        r3 = lambda x: x.reshape(nb, CHUNK, HEAD)
        dov = r3(do_ref[...])
        dqe_ref[...] = _bmm_nt(dov, s_ref[...]).reshape(rows, HEAD)
        ds_ref[...] = _bmm_tn(r3(qe_ref[...]), dov)

    nc = t // CHUNK
    return _pcall(body, "dn_o_bwd", (HEADS, t // rows), [mat, blk, blk], [blk, mat],
                  [jax.ShapeDtypeStruct((t, DN_WIDTH), F32), jax.ShapeDtypeStruct((nc, DN_WIDTH, HEAD), F32)],
                  [sall, q_eff, do], ("arbitrary", "arbitrary"), host=host)


def _adamw_update(w, g, m, v):
    m2 = ADAM_B1 * m + (1.0 - ADAM_B1) * g
    v2 = ADAM_B2 * v + (1.0 - ADAM_B2) * (g * g)
    m_hat = m2 / (1.0 - ADAM_B1 ** ADAM_STEP)
    v_hat = v2 / (1.0 - ADAM_B2 ** ADAM_STEP)
    return -ADAM_LR * (m_hat / (jnp.sqrt(v_hat) + ADAM_EPS) + ADAM_WD * w), m2, v2


def _adamw_small(ws, gs, ms, vs):
    n = len(ws)

    def body(*refs):
        for i in range(n):
            d, m2, v2 = _adamw_update(refs[i][...], refs[n + i][...], refs[2 * n + i][...], refs[3 * n + i][...])
            refs[4 * n + i][...] = d
            refs[5 * n + i][...] = m2
            refs[6 * n + i][...] = v2

    shapes = [jax.ShapeDtypeStruct(a.shape, F32) for a in ws]
    outs = pl.pallas_call(body, name="adamw_small", out_shape=shapes * 3,
                          compiler_params=pltpu.CompilerParams(vmem_limit_bytes=VMEM_LIMIT))(*ws, *gs, *ms, *vs)
    return outs[:n], outs[n:2 * n], outs[2 * n:]


def _adamw_call(name, ws, gs, ms, vs, host=None):
    k = len(ws)
    rows, cols = ws[0].shape
    assert all(w.shape == (rows, cols) for w in ws) and all(g.shape == gs[0].shape for g in gs)
    by_rows = rows % 8 == 0

    def body(*refs):
        for i in range(k):
            w_ref, g_ref, m_ref, v_ref, g_out, d_ref, m_out, v_out = refs[i::k]
            gv = g_ref[...] if by_rows else g_ref[0:rows, :]
            g_out[...] = gv
            d_ref[...], m_out[...], v_out[...] = _adamw_update(w_ref[...], gv, m_ref[...], v_ref[...])

    if by_rows:
        tr = _tile(rows, 256 // k, 8)
        spec = g_spec = pl.BlockSpec((tr, cols), lambda i: (i, 0))
        grid = (rows // tr,)
    else:
        tc = _tile(cols, 256, 128)
        spec = pl.BlockSpec((rows, tc), lambda j: (0, j))
        g_spec = pl.BlockSpec((gs[0].shape[0], tc), lambda j: (0, j))
        grid = (cols // tc,)
    outs = _pcall(body, name, grid, [spec] * k + [g_spec] * k + [spec] * 2 * k, [spec] * 4 * k,
                  [jax.ShapeDtypeStruct((rows, cols), F32)] * 4 * k, [*ws, *gs, *ms, *vs], ("arbitrary",), host=host)
    return [outs[i::k] for i in range(k)]


def _adamw_rows3d(name, w, g, m, v):
    rows, _, cols = w.shape
    tr = max(d for d in range(1, 129) if rows % d == 0)

    def body(w_ref, g_ref, m_ref, v_ref, d_ref, m_out, v_out):
        d_ref[...], m_out[...], v_out[...] = _adamw_update(w_ref[...], g_ref[...], m_ref[...], v_ref[...])

    spec = pl.BlockSpec((tr, 1, cols), lambda i: (i, 0, 0))
    return pl.pallas_call(body, name=name, grid=(rows // tr,), in_specs=[spec] * 4, out_specs=[spec] * 3,
                          out_shape=[jax.ShapeDtypeStruct(w.shape, F32)] * 3, compiler_params=_cparams(("arbitrary",)))(
                              w, g, m, v)


def _rope_tables(t):
    half = ROPE // 2
    inv_freq = np.float32(ROPE_THETA) ** (-np.arange(half, dtype=np.float32) / np.float32(half))
    ang = np.arange(t, dtype=np.float32)[:, None] * inv_freq[None, :].astype(np.float32)
    z = np.zeros((t, HEAD - ROPE), np.float32)
    cos = np.concatenate([np.cos(ang), np.cos(ang), z], axis=1).astype(np.float32)
    sin = np.concatenate([np.sin(ang), np.sin(ang), z], axis=1).astype(np.float32)
    k = np.arange(HEAD)[:, None]
    l = np.arange(HEAD)[None, :]
    perm = np.where((l < half) & (k == l + half), -1.0, 0.0) + np.where((l >= half) & (l < ROPE) & (k == l - half), 1.0, 0.0)
    return jnp.asarray(cos), jnp.asarray(sin), jnp.asarray(perm.astype(np.float32))


def _win_to_pad(w):
    z = lambda n: jnp.zeros((n, w.shape[1]), w.dtype)
    return jnp.concatenate([w[576:2112], w[2112:2624], w[0:256], w[256:512], w[512:576], z(64), w[2624:2632], z(120)],
                           axis=0)


def _qk_to_pad(w):
    w4 = w.reshape(HEADS, QK_DIM, w.shape[-1])
    return jnp.concatenate([w4, jnp.zeros((HEADS, QK_PAD - QK_DIM, w.shape[-1]), w.dtype)], axis=1).reshape(
        HEADS * QK_PAD, w.shape[-1])


def _qk_from_pad(g):
    return g.reshape(HEADS, QK_PAD, g.shape[-1])[:, :QK_DIM].reshape(HEADS * QK_DIM, g.shape[-1])


def _ff_to_pad(a, axis):
    shape = list(a.shape)
    shape[axis:axis + 1] = [N_CHIPS, FF_SHARD]
    a4 = a.reshape(shape)
    shape[axis + 1] = FF_BLOCK - FF_SHARD
    out = jnp.concatenate([a4, jnp.zeros(shape, a.dtype)], axis=axis + 1)
    shape[axis:axis + 2] = [D_FF_P]
    return out.reshape(shape)


def _ff_from_pad(a, axis):
    shape = list(a.shape)
    shape[axis:axis + 1] = [N_CHIPS, FF_BLOCK]
    a4 = lax.slice_in_dim(a.reshape(shape), 0, FF_SHARD, axis=axis + 1)
    shape[axis:axis + 2] = [D_FF]
    return a4.reshape(shape)


class _LocalPlan:
    def __init__(self, wt):
        self.wt, self.grads = wt, {}

    def weight(self, name):
        return self.wt[name]

    def host(self, point):
        return None

    def grad(self, name, value):
        self.grads[name] = value


def _local_step(x, tgt, wt, plan=None):
    plan = _LocalPlan(wt) if plan is None else plan
    s = x.shape[0]
    n_valid = N_META + s
    t = -(-n_valid // HEAD) * HEAD
    assert t - n_valid >= 3, "the DeltaNet conv kernels rely on at least three zero rows after the sequence"
    zpad = jnp.zeros((t - n_valid, D_MODEL), F32)
    h0 = jnp.concatenate([wt["meta_tokens"], x, zpad], axis=0)
    tgt_p = jnp.concatenate([jnp.zeros((N_META, D_MODEL), F32), tgt, zpad], axis=0)
    cos, sin, perm = _rope_tables(t)
    qn_w = jnp.concatenate([wt["q_norm_w"], jnp.zeros((1, QK_PAD - QK_DIM), F32)], axis=1)
    kn_w = jnp.concatenate([wt["k_norm_w"], jnp.zeros((1, QK_PAD - QK_DIM), F32)], axis=1)
    head_id = jnp.arange(DN_WIDTH)[None, :] // HEAD
    lane = jnp.arange(HEAD)[:, None]
    sel_a = (lane == head_id).astype(F32)
    sel_b = (lane == head_id + HEADS).astype(F32)
    alog = jnp.repeat(wt["dn_A_log"], HEAD, axis=1)
    dtb = jnp.repeat(wt["dn_dt_bias"], HEAD, axis=1)
    conv_w, conv_b = wt["ffn_conv_w"], wt["ffn_conv_b"]

    u = _rms_fwd("attn_norm_fwd", h0, wt["attn_norm_w"], host=plan.host("attn_norm_fwd"))
    win, wq, wkv = plan.weight("w_in_t"), plan.weight("w_q_t"), plan.weight("w_kv_t")
    proj = _matmul("in_proj", u, win, "nt", F32)
    z = (proj, DN_WIDTH, 3)
    q_lat, kv_lat, k_pe, ab = (proj, LORA, 8), (proj, LORA, 9), (proj, HEAD, 20), (proj, HEAD, 21)
    mla_consts = (wt["q_a_norm_w"], wq, wt["kv_a_norm_w"], wkv, qn_w, kn_w, perm)
    q, k, v = _mla_prep_fwd(q_lat, kv_lat, k_pe, cos, sin, *mla_consts)
    o_mla = _attn_fwd(q, k, v, host=plan.host("attn_fwd"))
    conv = _dn_conv_fwd(proj, wt["dn_conv_w"])
    dn_consts = (sel_a, sel_b, alog, dtb)
    qn, kn, g, beta = _dn_prep_fwd(conv, ab, *dn_consts)
    n_mat, b_mat, q_eff, o_own, eg = _dn_chunk_fwd(qn, kn, conv, g, beta, host=plan.host("dn_chunk_fwd"))
    sall = _dn_rec_fwd(n_mat, b_mat, eg)
    o_dn = _dn_o_fwd(sall, q_eff, o_own)
    w_out = plan.weight("w_out")
    mixed, h1, n2 = _mix_out_proj(o_mla, o_dn, z, wt["mla_out_norm_w"], wt["dn_out_norm_w"], w_out, h0, wt["ffn_norm_w"])
    w_gate, w_up = plan.weight("w_gate_t"), plan.weight("w_up_t")
    gpre, up, act = _ffn_glu_fwd(n2, w_gate, w_up, conv_w, conv_b, host=plan.host("ffn_glu_fwd"))
    w_down = plan.weight("w_down")
    dy, dy16, sq = _down_proj_loss(act, w_down, h1, tgt_p, n_valid)

    grads = {}
    plan.grad("w_down", _matmul("down_dw", act, dy16, "tn", BF16))
    dgpre, dup, grads["ffn_conv_w"], grads["ffn_conv_b"] = _ffn_glu_bwd(gpre, up, dy16, w_down, conv_w, conv_b)
    plan.grad("w_gate_t", _matmul("gate_dw", dgpre, n2, "tn", BF16))
    plan.grad("w_up_t", _matmul("up_dw", dup, n2, "tn", BF16))
    dh1, dh1_16, grads["ffn_norm_w"] = _ffn_in_bwd(dgpre, dup, w_gate, w_up, h1, dy, wt["ffn_norm_w"],
                                                   host=plan.host("ffn_in_bwd"))
    plan.grad("w_out", _matmul("out_dw", mixed, dh1_16, "tn", BF16))
    do_mla, do_dn, dz, grads["mla_out_norm_w"], grads["dn_out_norm_w"] = _mix_out_bwd(
        o_mla, o_dn, z, dh1_16, w_out, wt["mla_out_norm_w"], wt["dn_out_norm_w"], host=plan.host("mix_out_bwd"))
    dq_eff, ds_out = _dn_o_bwd(sall, q_eff, do_dn)
    gall = _dn_rec_bwd(n_mat, eg, ds_out)
    dqn, dkn, dv_dn, dg, dbeta = _dn_chunk_bwd(qn, kn, conv, g, beta, sall, gall, dq_eff, do_dn,
                                               host=plan.host("dn_chunk_bwd"))
    dconv, dab, dalog, ddtb = _dn_prep_bwd(conv, ab, dqn, dkn, dv_dn, dg, dbeta, *dn_consts)
    grads["dn_A_log"] = jnp.sum(dalog.reshape(HEADS, HEAD), axis=1)[None, :]
    grads["dn_dt_bias"] = jnp.sum(ddtb.reshape(HEADS, HEAD), axis=1)[None, :]
    ddn_pre, grads["dn_conv_w"] = _dn_conv_bwd(proj, wt["dn_conv_w"], dconv)
    dq, dk, dv = _attn_bwd(q, k, v, do_mla, host=plan.host("attn_bwd"))
    dq_lat, dkv_lat, dk_pe, dqa, dwq, dkva, dwkv, dqnw, dknw = _mla_prep_bwd(
        q_lat, kv_lat, k_pe, cos, sin, dq, dk, dv, *mla_consts, host=plan.host("mla_prep_bwd"))
    grads["q_a_norm_w"], grads["kv_a_norm_w"] = dqa, dkva
    plan.grad("w_q_t", dwq)
    plan.grad("w_kv_t", dwkv)
    grads["q_norm_w"], grads["k_norm_w"] = dqnw[:, :QK_DIM], dknw[:, :QK_DIM]
    dproj = [ddn_pre, dz, dq_lat, dkv_lat, dk_pe, dab]
    plan.grad("w_in_t", _in_proj_bwd_w(dproj, u))
    du = _in_proj_bwd_x(dproj, win, host=plan.host("in_dx"))
    dh0, _, grads["attn_norm_w"] = _rms_bwd("attn_norm_bwd", h0, wt["attn_norm_w"], [du], dh1,
                                            host=plan.host("attn_norm_bwd"))
    grads["meta_tokens"] = dh0[0:N_META]
    if isinstance(plan, _LocalPlan):
        grads.update(plan.grads)
    return sq, dh0[N_META:n_valid], grads


def _mesh_pos():
    return lax.axis_index("x"), lax.axis_index("y"), lax.axis_index("c")


def _other_chips(x, y):
    return [(1 - x, y), (x, 1 - y), (1 - x, 1 - y)]


def _remote(src, dst, send_sems, recv_sems, k, to):
    return pltpu.make_async_remote_copy(src_ref=src, dst_ref=dst, send_sem=send_sems.at[k], recv_sem=recv_sems.at[k],
                                        device_id=to, device_id_type=MESH)


SIBLING_ID, CHIPS_ID, GATHER_ID, ALL_ID = 1, 2, 3, 4


def _sibling_peer():
    x, y, c = _mesh_pos()
    return [(x, y, 1 - c)]


def _chip_peers():
    x, y, c = _mesh_pos()
    return [(qx, qy, c) for qx, qy in _other_chips(x, y)]


def _copies_exchange(make, ins, out_shape, nsem, peers=None, cid=None):
    def prog(in_refs, out_refs, send_sems, recv_sems):
        copies = make(in_refs, out_refs, send_sems, recv_sems)

        def start():
            for cp in copies:
                cp.start()

        def finish():
            for cp in copies:
                cp.wait()

        return start, finish

    return _Exchange(prog, ins, out_shape, nsem, peers, cid)


def _all_gather(shards):
    def prog(srcs, dsts, send_sems, recv_sems):
        x, y, c = _mesh_pos()
        p = 2 * x + y
        sibling = (x, y, 1 - c)
        chips = _other_chips(x, y)
        bufs = tuple((s, d, s.shape[0] // 2) for s, d in zip(srcs, dsts))

        def half(ref, rows, which):
            return ref.at[pl.ds(which * rows, rows), :]

        def copy(i, k, src, dst, to):
            return _remote(src, dst, send_sems, recv_sems, 6 * i + k, to)

        sends = [copy(i, j, half(src, rows, c), half(dst.at[p], rows, c), (*chip, c))
                 for i, (src, dst, rows) in enumerate(bufs) for j, chip in enumerate(chips)]

        def start():
            for cp in sends:
                cp.start()

        def finish():
            passed = []
            for i, (src, dst, rows) in enumerate(bufs):
                for j, (qx, qy) in enumerate(chips):
                    block = half(dst.at[2 * qx + qy], rows, c)
                    copy(i, j, block, block, (x, y, c)).wait_recv()
                    fwd = copy(i, 3 + j, block, block, sibling)
                    fwd.start()
                    passed.append(fwd)
            for i, (src, dst, rows) in enumerate(bufs):
                for j, (qx, qy) in enumerate(chips):
                    block = half(dst.at[2 * qx + qy], rows, 1 - c)
                    copy(i, 3 + j, block, block, (x, y, c)).wait_recv()
            for cp in sends + passed:
                cp.wait_send()

        return start, finish

    return _Exchange(prog, shards, [jax.ShapeDtypeStruct((N_CHIPS, *s.shape), s.dtype) for s in shards], 6 * len(shards),
                     lambda: _sibling_peer() + _chip_peers(), GATHER_ID)


def _all_gather_small(block):
    def make(srcs, dsts, send_sems, recv_sems):
        x, y, c = _mesh_pos()
        return [_remote(srcs[0], dsts[0].at[2 * x + y], send_sems, recv_sems, k, (qx, qy, c))
                for k, (qx, qy) in enumerate(_other_chips(x, y))]

    return _copies_exchange(make, [block], [jax.ShapeDtypeStruct((N_CHIPS, *block.shape), block.dtype)], 3, _chip_peers,
                            CHIPS_ID)


def _gathered(ex):
    p = 2 * lax.axis_index("x") + lax.axis_index("y")
    return [lax.dynamic_update_slice(g, s[None], (p, 0, 0)) for g, s in zip(ex.outs, ex.ins)]


def _rs_to_sibling(bufs):
    def make(srcs, dsts, send_sems, recv_sems):
        x, y, c = _mesh_pos()
        copies = []
        for i, (src, dst) in enumerate(zip(srcs, dsts)):
            half = src.shape[1] // 2
            copies.append(_remote(src.at[:, pl.ds((1 - c) * half, half), :], dst, send_sems, recv_sems, i, (x, y, 1 - c)))
        return copies

    return _copies_exchange(make, bufs,
                            [jax.ShapeDtypeStruct((N_CHIPS, b.shape[1] // 2, b.shape[2]), b.dtype) for b in bufs],
                            len(bufs), _sibling_peer, SIBLING_ID)


def _rs_pair_add(name, bufs, gots, c, out_dtype):
    n = len(bufs)

    def body(c_ref, *refs):
        for a_ref, b_ref, o_ref in zip(refs[:n], refs[n:2 * n], refs[2 * n:]):
            o_ref[...] = (a_ref[...].astype(F32) + b_ref[...].astype(F32)).astype(out_dtype)

    mine = [pl.BlockSpec((None, g.shape[1], g.shape[2]), lambda j, cr: (j, cr[0], 0)) for g in gots]
    whole = [pl.BlockSpec((None, g.shape[1], g.shape[2]), lambda j, cr: (j, 0, 0)) for g in gots]
    return pl.pallas_call(
        body, name=name,
        grid_spec=pltpu.PrefetchScalarGridSpec(num_scalar_prefetch=1, grid=(N_CHIPS,), in_specs=mine + whole, out_specs=whole),
        out_shape=[jax.ShapeDtypeStruct(g.shape, out_dtype) for g in gots],
        compiler_params=_cparams(("arbitrary",)))(c, *bufs, *gots)


def _rs_to_chips(accs):
    def make(srcs, dsts, send_sems, recv_sems):
        x, y, c = _mesh_pos()
        return [_remote(src.at[2 * qx + qy], dst.at[k], send_sems, recv_sems, 3 * i + k, (qx, qy, c))
                for i, (src, dst) in enumerate(zip(srcs, dsts)) for k, (qx, qy) in enumerate(_other_chips(x, y))]

    return _copies_exchange(make, accs, [jax.ShapeDtypeStruct((3, a.shape[1], a.shape[2]), a.dtype) for a in accs],
                            3 * len(accs), _chip_peers, CHIPS_ID)


def _rs_chip_add(name, accs, gots, p):
    n = len(accs)
    slot = (0, 1, 0, 2)

    def body(p_ref, *refs):
        me = p_ref[0]
        for own_ref, got_ref, o_ref in zip(refs[:n], refs[n:2 * n], refs[2 * n:]):
            total = None
            for chip in range(N_CHIPS):
                val = own_ref[...].astype(F32)
                for e in (1, 2, 3):
                    val = jnp.where((chip ^ me) == e, got_ref[slot[e]].astype(F32), val)
                total = val if total is None else total + val
            o_ref[...] = total

    own = [pl.BlockSpec((None, a.shape[1], a.shape[2]), lambda i, pr: (pr[0], 0, 0)) for a in accs]
    got = [pl.BlockSpec(g.shape, lambda i, pr: (0, 0, 0)) for g in gots]
    out = [pl.BlockSpec((a.shape[1], a.shape[2]), lambda i, pr: (0, 0)) for a in accs]
    return pl.pallas_call(
        body, name=name,
        grid_spec=pltpu.PrefetchScalarGridSpec(num_scalar_prefetch=1, grid=(1,), in_specs=own + got, out_specs=out),
        out_shape=[jax.ShapeDtypeStruct((a.shape[1], a.shape[2]), F32) for a in accs],
        compiler_params=_cparams(("arbitrary",)))(p, *accs, *gots)


def _rs_share(ress):
    def make(srcs, dsts, send_sems, recv_sems):
        x, y, c = _mesh_pos()
        return [_remote(src, dst, send_sems, recv_sems, i, (x, y, 1 - c)) for i, (src, dst) in enumerate(zip(srcs, dsts))]

    return _copies_exchange(make, ress, [jax.ShapeDtypeStruct(r.shape, F32) for r in ress], len(ress), _sibling_peer,
                            SIBLING_ID)


def _shared(ex):
    south = lax.axis_index("c") == 0
    return [jnp.concatenate([jnp.where(south, r, g), jnp.where(south, g, r)], axis=0) for r, g in zip(ex.ins, ex.outs)]


def _all_to_all_devices(vec):
    def others():
        x, y, c = _mesh_pos()
        return [((1 - x if r & 4 else x), (1 - y if r & 2 else y), (1 - c if r & 1 else c)) for r in range(1, 8)]

    def make(srcs, dsts, send_sems, recv_sems):
        x, y, c = _mesh_pos()
        me = 4 * x + 2 * y + c
        return [_remote(srcs[0], dsts[0].at[me], send_sems, recv_sems, r, peer) for r, peer in enumerate(others())]

    return _copies_exchange(make, [vec], [jax.ShapeDtypeStruct((8, *vec.shape), vec.dtype)], 7, others, ALL_ID)


def _sum_devices(stack):
    def body(s_ref, o_ref):
        total = s_ref[0]
        for d in range(1, 8):
            total = total + s_ref[d]
        o_ref[...] = total

    return pl.pallas_call(body, name="sum_devices", out_shape=jax.ShapeDtypeStruct(stack.shape[1:], F32),
                          compiler_params=pltpu.CompilerParams(vmem_limit_bytes=VMEM_LIMIT))(stack)


def _pad_rows(flat, rows):
    return jnp.concatenate([flat, jnp.zeros((rows * LANES - flat.shape[0],), flat.dtype)]).reshape(rows, LANES)


def _unshard(g4, shape, axis):
    a = g4.reshape(N_CHIPS, *shape)
    if axis == 0:
        return a.reshape(N_CHIPS * shape[0], shape[1])
    return jnp.transpose(a, (1, 0, 2)).reshape(shape[0], N_CHIPS * shape[1])


def _pad_axis0(a, rows):
    return jnp.concatenate([a, jnp.zeros((rows - a.shape[0], *a.shape[1:]), a.dtype)], axis=0)


def _shard_to_strip(name, w):
    _, (shape, axis, rows) = name, {n: (s, ax, r) for n, s, ax, r in BIG}[name]
    w2 = w.reshape(shape).astype(BF16)
    return _pad_axis0(w2.T if axis == 1 else w2, rows)


LOCAL_NAME = dict(w_in="w_in_t", w_q_b="w_q_t", w_kv_b="w_kv_t", w_out="w_out", w_gate="w_gate_t", w_up="w_up_t",
                  w_down="w_down")


WIN_SEGMENTS = ((576, 2112, 0), (2112, 2624, 1536), (0, 256, 2048), (256, 512, 2304), (512, 576, 2560), (2624, 2632, 2688))


def _strips_to_weight(name, g4):
    if name == "w_in":
        return _win_to_pad(g4[:, :IN_SHARD].reshape(IN_COLS, D_MODEL))
    if name == "w_q_b":
        return _qk_to_pad(g4.reshape(HEADS * QK_DIM, LORA))
    return g4.reshape(N_CHIPS * g4.shape[1], g4.shape[2])


def _grad_to_strips(name, g):
    if name == "w_in":
        strips = []
        for q in range(N_CHIPS):
            pieces = []
            for a, b, local in sorted(WIN_SEGMENTS):
                s, e = max(a, q * IN_SHARD), min(b, (q + 1) * IN_SHARD)
                if s < e:
                    pieces.append(g[local + s - a:local + e - a])
            pieces.append(jnp.zeros((IN_SHARD_P - IN_SHARD, D_MODEL), g.dtype))
            strips.append(jnp.concatenate(pieces, axis=0))
        return jnp.stack(strips)
    if name == "w_q_b":
        return _qk_from_pad(g).reshape(N_CHIPS, QK_DIM, LORA)
    return g.reshape(N_CHIPS, g.shape[0] // N_CHIPS, g.shape[1])


class _MeshPlan:
    LATE = dict(attn_norm_fwd=("w_in", "w_q_b", "w_kv_b"), attn_fwd=("w_up",), dn_chunk_fwd=("w_out", "w_gate"),
                ffn_glu_fwd=("w_down",))
    GROUP_A = ("w_down", "w_gate", "w_up", "w_out")
    GROUP_B = ("w_in", "w_q_b", "w_kv_b")

    def __init__(self, w):
        x, y, c = _mesh_pos()
        self.ci = jnp.reshape(c, (1,)).astype(jnp.int32)
        self.pi = jnp.reshape(2 * x + y, (1,)).astype(jnp.int32)
        self.strip = {n: _shard_to_strip(n, w[n]) for n, _, _, _ in BIG}
        self.gathers, self.weights, self.g, self.acc, self.reduced = {}, {}, {}, {}, {}
        self.sibs, self.sib, self.chip, self.share = [], None, None, None

    def gather_small(self, small):
        ex = _all_gather_small(small)
        ex.run("all_gather_small")
        return _gathered(ex)[0]

    def weight(self, local_name):
        if local_name not in self.weights:
            for point, (names, ex) in list(self.gathers.items()):
                if ex.outs is not None:
                    for n, g4 in zip(names, _gathered(ex)):
                        self.weights[LOCAL_NAME[n]] = _strips_to_weight(n, g4)
                    del self.gathers[point]
        return self.weights[local_name]

    def grad(self, local_name, value):
        name = {v: k for k, v in LOCAL_NAME.items()}[local_name]
        self.g[name] = _grad_to_strips(name, value)

    def _pair_add(self, names, gots):
        accs = _rs_pair_add("rs_pair_add_" + names[0], [self.g[n] for n in names], gots, self.ci, BF16)
        self.acc.update(zip(names, accs))

    def _chip_add(self, names, chip):
        return _rs_chip_add("rs_chip_add_" + names[0], [self.acc[n] for n in names], chip.outs, self.pi)

    def _take_shared(self, names, share):
        for n, strip in zip(names, _shared(share)):
            self.reduced[n] = strip

    def host(self, point):
        a, b = self.GROUP_A, self.GROUP_B
        if point in self.LATE:
            names = self.LATE[point]
            ex = _all_gather([self.strip[n] for n in names])
            self.gathers[point] = (names, ex)
            return ex
        if point in ("ffn_in_bwd", "mix_out_bwd"):
            names = dict(ffn_in_bwd=a[:3], mix_out_bwd=a[3:])[point]
            ex = _rs_to_sibling([self.g[n] for n in names])
            self.sibs.append(ex)
            return ex
        if point == "dn_chunk_bwd":
            self._pair_add(a, [o for ex in self.sibs for o in ex.outs])
            self.chip1 = _rs_to_chips([self.acc[n] for n in a[:2]])
            return self.chip1
        if point == "attn_bwd":
            self.chip2 = _rs_to_chips([self.acc[n] for n in a[2:]])
            return self.chip2
        if point == "mla_prep_bwd":
            ress = self._chip_add(a[:2], self.chip1) + self._chip_add(a[2:], self.chip2)
            self.share = _rs_share(ress)
            return self.share
        if point == "in_dx":
            self._take_shared(a, self.share)
            self.sib = _rs_to_sibling([self.g[n] for n in b])
            return self.sib
        if point == "attn_norm_bwd":
            self._pair_add(b, self.sib.outs)
            self.chip = _rs_to_chips([self.acc[n] for n in b])
            return self.chip
        return None

    def last_share(self):
        self.share = _rs_share(self._chip_add(self.GROUP_B, self.chip))
        return self.share

    def finish(self):
        self._take_shared(self.GROUP_B, self.share)
        return self.reduced


def kernel(x, meta_tokens, attn_norm_w, w_in, q_a_norm_w, w_q_b, kv_a_norm_w, w_kv_b, q_norm_w, k_norm_w, mla_out_norm_w, dn_conv_w, dn_A_log, dn_dt_bias, dn_out_norm_w, w_out, ffn_norm_w, w_gate, w_up, ffn_conv_w, ffn_conv_b, w_down, loss_target, m_meta_tokens, m_attn_norm_w, m_w_in, m_q_a_norm_w, m_w_q_b, m_kv_a_norm_w, m_w_kv_b, m_q_norm_w, m_k_norm_w, m_mla_out_norm_w, m_dn_conv_w, m_dn_A_log, m_dn_dt_bias, m_dn_out_norm_w, m_w_out, m_ffn_norm_w, m_w_gate, m_w_up, m_ffn_conv_w, m_ffn_conv_b, m_w_down, v_meta_tokens, v_attn_norm_w, v_w_in, v_q_a_norm_w, v_w_q_b, v_kv_a_norm_w, v_w_kv_b, v_q_norm_w, v_k_norm_w, v_mla_out_norm_w, v_dn_conv_w, v_dn_A_log, v_dn_dt_bias, v_dn_out_norm_w, v_w_out, v_ffn_norm_w, v_w_gate, v_w_up, v_ffn_conv_w, v_ffn_conv_b, v_w_down):
    local = dict(locals())
    w = {n: local[n] for n in WEIGHTS}
    m = {n: local["m_" + n] for n in WEIGHTS}
    v = {n: local["v_" + n] for n in WEIGHTS}
    p = 2 * lax.axis_index("x") + lax.axis_index("y")

    plan = _MeshPlan(w)
    wf = _pad_rows(jnp.concatenate([w[n].reshape(-1) for n, _, _ in SMALL_SHARDED]), SMALL_ROWS)
    gf = plan.gather_small(wf).reshape(N_CHIPS, -1)
    full = {}
    off = 0
    for n, s, ax in SMALL_SHARDED:
        full[n] = _unshard(gf[:, off:off + s[0] * s[1]], s, ax)
        off += s[0] * s[1]
    for n, _ in REPLICATED:
        full[n] = w[n]
    full["ffn_conv_w"] = _ff_to_pad(full["ffn_conv_w"], 1)
    full["ffn_conv_b"] = _ff_to_pad(full["ffn_conv_b"], 1)

    sq, grad_x, g = _local_step(x[0], loss_target[0], full, plan)
    g["ffn_conv_w"] = _ff_from_pad(g["ffn_conv_w"], 1)
    g["ffn_conv_b"] = _ff_from_pad(g["ffn_conv_b"], 1)

    small_all = [n for n, _, _ in SMALL_SHARDED] + [n for n, _ in REPLICATED]
    vec = jnp.concatenate([g[n].reshape(-1) for n in small_all] + [jnp.reshape(0.5 / D_MODEL * jnp.sum(sq), (1,))])
    vec = _pad_rows(vec, -(-vec.shape[0] // (8 * LANES)) * 8)
    a2a = _all_to_all_devices(vec)

    gs, delta, new_m, new_v = {}, {}, {}, {}
    big = {n: (s, ax) for n, s, ax, _ in BIG}

    def adamw_big(names, strips, host=None):
        n = names[0]
        s, ax = big[n]
        if ax == 1 and s[1] % 8:
            there = lambda a: jnp.transpose(a, (2, 0, 1))
            back = lambda a: jnp.transpose(a, (1, 2, 0))
            g3 = strips[n][:s[1]].reshape(s[1], 1, s[0])
            d2, m2, v2 = _adamw_rows3d("adamw_" + n, there(w[n]), g3, there(m[n]), there(v[n]))
            gs[n], delta[n], new_m[n], new_v[n] = back(g3), back(d2), back(m2), back(v2)
            return

        def there(n, a):
            s, ax = big[n]
            return a.reshape(s).T if ax == 1 else a.reshape(s)

        def back(n, a):
            return (a.T if big[n][1] == 1 else a).reshape(w[n].shape)

        outs = _adamw_call("adamw_" + n, [there(n, w[n]) for n in names], [strips[n] for n in names],
                           [there(n, m[n]) for n in names], [there(n, v[n]) for n in names], host=host)
        for n, (g2, d2, m2, v2) in zip(names, outs):
            gs[n], delta[n], new_m[n], new_v[n] = back(n, g2), back(n, d2), back(n, m2), back(n, v2)

    adamw_big(("w_down", "w_gate"), plan.reduced, host=a2a)
    adamw_big(("w_out",), plan.reduced, host=plan.last_share())
    adamw_big(("w_up",), plan.reduced)
    strips = plan.finish()
    for n in plan.GROUP_B:
        adamw_big((n,), strips)
    me = 4 * lax.axis_index("x") + 2 * lax.axis_index("y") + lax.axis_index("c")
    red = _sum_devices(lax.dynamic_update_slice(a2a.outs[0], vec[None], (me, 0, 0))).reshape(-1)
    off = 0
    for n in small_all:
        tot = red[off:off + g[n].size].reshape(g[n].shape)
        off += g[n].size
        shard = {sn: (s, ax) for sn, s, ax in SMALL_SHARDED}.get(n)
        if shard is not None:
            tot = lax.dynamic_slice_in_dim(tot, p * shard[0][1], shard[0][1], axis=1)
        gs[n] = tot
    loss = red[off]
    two_d = lambda a: a.reshape(a.shape[-2], a.shape[-1])
    outs = _adamw_small([two_d(w[n]) for n in small_all], [two_d(gs[n]) for n in small_all],
                        [two_d(m[n]) for n in small_all], [two_d(v[n]) for n in small_all])
    for i, n in enumerate(small_all):
        for dst, src in ((delta, outs[0]), (new_m, outs[1]), (new_v, outs[2])):
            dst[n] = src[i].reshape(w[n].shape)

    grad_out = [gs[n].reshape(w[n].shape) for n in WEIGHTS]
    return (loss, grad_x[None], *grad_out, *[delta[n] for n in WEIGHTS], *[new_m[n] for n in WEIGHTS],
            *[new_v[n] for n in WEIGHTS])
```

```python
import functools
import math

import jax
import jax.numpy as jnp
import numpy as np
from jax import lax
from jax.experimental import pallas as pl
from jax.experimental.pallas import tpu as pltpu

F32 = jnp.float32
BF16 = jnp.bfloat16
HI = lax.Precision.HIGHEST
MESH = pl.DeviceIdType.MESH

N_META = 16
D_MODEL = 1024
HEADS = 4
HEAD = 128
ROPE = 64
QK_DIM = HEAD + ROPE
QK_PAD = 2 * HEAD
LORA = 256
DN_WIDTH = HEADS * HEAD
CHUNK = 64
D_FF = 2816
N_CHIPS = 4
FF_SHARD = D_FF // N_CHIPS
FF_BLOCK = 768
D_FF_P = N_CHIPS * FF_BLOCK
IN_COLS = 2632
IN_SHARD = IN_COLS // N_CHIPS
IN_SHARD_P = 672
NORM_EPS = 1e-6
ROPE_THETA = 10000.0
LANES = 512

ADAM_LR, ADAM_B1, ADAM_B2, ADAM_EPS, ADAM_WD, ADAM_STEP = 0.001, 0.9, 0.999, 1e-08, 0.01, 10

VMEM_LIMIT = 56 * 1024 * 1024

BIG = (("w_in", (1024, 658), 1, IN_SHARD_P), ("w_q_b", (256, 192), 1, 192), ("w_kv_b", (256, 256), 1, 256),
       ("w_out", (256, 1024), 0, 256), ("w_gate", (1024, 704), 1, FF_BLOCK), ("w_up", (1024, 704), 1, FF_BLOCK),
       ("w_down", (704, 1024), 0, FF_BLOCK))
SMALL_SHARDED = (("meta_tokens", (16, 256), 1), ("dn_conv_w", (4, 384), 1), ("ffn_conv_w", (3, 704), 1))
REPLICATED = (("attn_norm_w", 1024), ("q_a_norm_w", 256), ("kv_a_norm_w", 256), ("q_norm_w", 192), ("k_norm_w", 192),
              ("mla_out_norm_w", 128), ("dn_A_log", 4), ("dn_dt_bias", 4), ("dn_out_norm_w", 128), ("ffn_norm_w", 1024),
              ("ffn_conv_b", 2816))
WEIGHTS = ("meta_tokens", "attn_norm_w", "w_in", "q_a_norm_w", "w_q_b", "kv_a_norm_w", "w_kv_b", "q_norm_w", "k_norm_w",
           "mla_out_norm_w", "dn_conv_w", "dn_A_log", "dn_dt_bias", "dn_out_norm_w", "w_out", "ffn_norm_w", "w_gate",
           "w_up", "ffn_conv_w", "ffn_conv_b", "w_down")

SMALL_ROWS = 16


def _cparams(sem):
    return pltpu.CompilerParams(dimension_semantics=sem, vmem_limit_bytes=VMEM_LIMIT)


class _Exchange:
    def __init__(self, prog, ins, out_shape, nsem, peers=None, cid=None):
        self.prog, self.ins, self.out_shape, self.nsem = prog, list(ins), list(out_shape), nsem
        self.peers, self.cid = peers, cid
        self.outs = None

    def sems(self):
        return [pltpu.SemaphoreType.DMA((self.nsem,)), pltpu.SemaphoreType.DMA((self.nsem,))]

    def programs(self, in_refs, out_refs, send_sems, recv_sems):
        start, finish = self.prog(in_refs, out_refs, send_sems, recv_sems)
        if self.cid is None:
            return (lambda: None), start, finish
        peers = self.peers()

        def announce():
            barrier = pltpu.get_barrier_semaphore()
            for peer in peers:
                pl.semaphore_signal(barrier, inc=1, device_id=peer, device_id_type=MESH)

        def wait_and_start():
            pl.semaphore_wait(pltpu.get_barrier_semaphore(), len(peers))
            start()

        return announce, wait_and_start, finish

    def cparams(self, **kw):
        return pltpu.CompilerParams(has_side_effects=True, collective_id=self.cid, **kw)

    def run(self, name):
        any_spec = pl.BlockSpec(memory_space=pl.ANY)
        n = len(self.ins)

        def body(*refs):
            announce, start, finish = self.programs(refs[:n], refs[n:-2], refs[-2], refs[-1])
            announce()
            start()
            finish()

        self.outs = pl.pallas_call(
            body, name=name, in_specs=[any_spec] * n, out_specs=[any_spec] * len(self.out_shape),
            out_shape=self.out_shape, scratch_shapes=self.sems(), compiler_params=self.cparams())(*self.ins)
        return self.outs


def _pcall(body, name, grid, in_specs, out_specs, out_shape, args, sem, scratch_shapes=(), host=None):
    single = not isinstance(out_shape, (list, tuple))
    out_specs, out_shape = ([out_specs], [out_shape]) if single else (list(out_specs), list(out_shape))
    if host is None:
        outs = pl.pallas_call(body, name=name, grid=grid, in_specs=list(in_specs), out_specs=out_specs, out_shape=out_shape,
                              scratch_shapes=list(scratch_shapes), compiler_params=_cparams(sem))(*args)
        return outs[0] if single else outs
    any_spec = pl.BlockSpec(memory_space=pl.ANY)
    n_in, n_out, n_scr, nx_in, nx_out = len(in_specs), len(out_specs), len(scratch_shapes), len(host.ins), len(host.out_shape)

    def hosted(*refs):
        c_in, x_in = refs[:n_in], refs[n_in:n_in + nx_in]
        o0 = n_in + nx_in
        c_out, x_out = refs[o0:o0 + n_out], refs[o0 + n_out:o0 + n_out + nx_out]
        s0 = o0 + n_out + nx_out
        announce, start, finish = host.programs(x_in, x_out, refs[s0 + n_scr], refs[s0 + n_scr + 1])
        first = functools.reduce(jnp.logical_and, [pl.program_id(d) == 0 for d in range(len(grid))])
        last = functools.reduce(jnp.logical_and, [pl.program_id(d) == grid[d] - 1 for d in range(len(grid))])
        pl.when(first)(announce)
        if math.prod(grid) == 1:
            start()
        body(*c_in, *c_out, *refs[s0:s0 + n_scr])
        if math.prod(grid) > 1:
            pl.when(first)(start)
        pl.when(last)(finish)

    outs = pl.pallas_call(
        hosted, name=name, grid=grid, in_specs=list(in_specs) + [any_spec] * nx_in,
        out_specs=out_specs + [any_spec] * nx_out, out_shape=out_shape + host.out_shape,
        scratch_shapes=list(scratch_shapes) + host.sems(),
        compiler_params=host.cparams(dimension_semantics=sem, vmem_limit_bytes=VMEM_LIMIT))(*args, *host.ins)
    host.outs = outs[n_out:]
    return outs[0] if single else outs[:n_out]


NN, NT, TN = ((1,), (0,)), ((1,), (1,)), ((0,), (0,))


def _shift_dims(dims, batch):
    if not batch:
        return (dims, ((), ()))
    return (((dims[0][0] + 1,), (dims[1][0] + 1,)), ((0,), (0,)))


def _make_mm(dims, exact, batch=False):
    def raw(a, b, d):
        dn = _shift_dims(d, batch)
        if exact == "split_lhs":
            ah, bh = a.astype(BF16), b.astype(BF16)
            al = (a - ah.astype(F32)).astype(BF16)
            return lax.dot_general(ah, bh, dn, preferred_element_type=F32) + lax.dot_general(al, bh, dn,
                                                                                              preferred_element_type=F32)
        if exact == "split":
            ah, bh = a.astype(BF16), b.astype(BF16)
            al, bl = (a - ah.astype(F32)).astype(BF16), (b - bh.astype(F32)).astype(BF16)
            dot = lambda p, q: lax.dot_general(p, q, dn, preferred_element_type=F32)
            return dot(ah, bh) + (dot(ah, bl) + dot(al, bh))
        if exact:
            return lax.dot_general(a.astype(F32), b.astype(F32), dn, precision=HI, preferred_element_type=F32)
        return lax.dot_general(a.astype(BF16), b.astype(BF16), dn, preferred_element_type=F32)

    @jax.custom_vjp
    def mm(a, b):
        return raw(a, b, dims)

    def fwd(a, b):
        return raw(a, b, dims), (a, b)

    def bwd(res, g):
        a, b = res
        if dims == NN:
            da, db = raw(g, b, NT), raw(a, g, TN)
        elif dims == NT:
            da, db = raw(g, b, NN), raw(g, a, TN)
        else:
            da, db = raw(b, g, NT), raw(a, g, NN)
        return da.astype(a.dtype), db.astype(b.dtype)

    mm.defvjp(fwd, bwd)
    return mm


_mm = _make_mm(NN, False)
_mm_nt = _make_mm(NT, False)
_mm_tn = _make_mm(TN, False)
_mmx = _make_mm(NN, "split_lhs")
_bmm = _make_mm(NN, False, batch=True)
_bmm_nt = _make_mm(NT, False, batch=True)
_bmm_tn = _make_mm(TN, False, batch=True)
_bmmx = _make_mm(NN, True, batch=True)
_bmms = _make_mm(NN, "split", batch=True)
_bmms_nt = _make_mm(NT, "split", batch=True)
_bmms_tn = _make_mm(TN, "split", batch=True)


@jax.custom_vjp
def _unit_lower_inv(a):
    n = a.shape[-1]
    eye = (lax.broadcasted_iota(jnp.int32, a.shape, 1) == lax.broadcasted_iota(jnp.int32, a.shape, 2)).astype(F32)
    x = -a
    t = eye + x
    for _ in range(max(n.bit_length() - 2, 0)):
        x = _bmms(x, x)
        t = t + _bmms(t, x)
    return t


def _unit_lower_inv_fwd(a):
    t = _unit_lower_inv(a)
    return t, t


def _unit_lower_inv_bwd(t, g):
    return (-_bmms_tn(t, _bmms_nt(g, t)),)


_unit_lower_inv.defvjp(_unit_lower_inv_fwd, _unit_lower_inv_bwd)


def _scan_chunk_rows(x, reverse):
    nb, c, w = x.shape
    y = x.reshape(nb * c, w)
    pos = lax.broadcasted_iota(jnp.int32, y.shape, 0) % c
    step = 1
    while step < c:
        if reverse:
            y = y + jnp.where(pos < c - step, pltpu.roll(y, nb * c - step, 0), 0.0)
        else:
            y = y + jnp.where(pos >= step, pltpu.roll(y, step, 0), 0.0)
        step *= 2
    return y.reshape(nb, c, w)


@jax.custom_vjp
def _chunk_cumsum(x):
    return _scan_chunk_rows(x, False)


_chunk_cumsum.defvjp(lambda x: (_scan_chunk_rows(x, False), None), lambda _, g: (_scan_chunk_rows(g, True),))


def _rms(x, w, n):
    ms = jnp.sum(x * x, axis=-1, keepdims=True) * (1.0 / n)
    return x * lax.rsqrt(ms + NORM_EPS) * w


def _silu(x):
    return x * jax.nn.sigmoid(x)


def _softplus(x):
    return jnp.maximum(x, 0.0) + jnp.log(1.0 + jnp.exp(-jnp.abs(x)))


def _rope(x, cos, sin, perm):
    return x * cos + _mmx(x, perm) * sin


def _mla_prep_fn(rows, consts):
    q_lat, kv_lat, k_pe, cos, sin = rows
    qn = _rms(q_lat, consts["qa_w"], LORA)
    kvn = _rms(kv_lat, consts["kva_w"], LORA)
    outs = []
    for h in range(HEADS):
        q_n = _mm_nt(qn, consts["wq_n"][h])
        q_r = _mm_nt(qn, consts["wq_r"][h])
        rs = lax.rsqrt((jnp.sum(q_n * q_n, -1, keepdims=True) + jnp.sum(q_r * q_r, -1, keepdims=True)) * (1.0 / QK_DIM)
                       + NORM_EPS)
        q_n = q_n * rs * consts["qn_n"]
        q_r = _rope(q_r * rs * consts["qn_r"], cos, sin, consts["perm"])
        k_n = _mm_nt(kvn, consts["wk_n"][h])
        v = _mm_nt(kvn, consts["wv"][h])
        rk = lax.rsqrt((jnp.sum(k_n * k_n, -1, keepdims=True) + jnp.sum(k_pe * k_pe, -1, keepdims=True)) * (1.0 / QK_DIM)
                       + NORM_EPS)
        k_n = k_n * rk * consts["kn_n"]
        k_r = _rope(k_pe * rk * consts["kn_r"], cos, sin, consts["perm"])
        outs += [q_n, q_r, k_n, k_r, v]
    return tuple(outs)


def _attn_fn(q, k, v, row0):
    s = _mm_nt(q, k) * (1.0 / math.sqrt(QK_DIM))
    qpos = row0 + lax.broadcasted_iota(jnp.int32, s.shape, 0)
    kpos = lax.broadcasted_iota(jnp.int32, s.shape, 1)
    s = jnp.where(kpos <= qpos, s, -1e30)
    m = lax.stop_gradient(jnp.max(s, axis=-1, keepdims=True))
    p = jnp.exp(s - m)
    p = p / jnp.sum(p, axis=-1, keepdims=True)
    return _mm(p, v)


def _dn_prep_fn(rows, consts):
    qc, kc, ab = rows
    a_b = _mmx(ab, consts["sel_a"])
    b_b = _mmx(ab, consts["sel_b"])
    beta = jax.nn.sigmoid(b_b)
    g = -jnp.exp(consts["alog"]) * _softplus(a_b + consts["dtb"])
    qs, ks = [], []
    for h in range(HEADS):
        q, k = qc[h], kc[h]
        qs.append(q * lax.rsqrt(jnp.sum(q * q, -1, keepdims=True) + NORM_EPS))
        ks.append(k * lax.rsqrt(jnp.sum(k * k, -1, keepdims=True) + NORM_EPS))
    return tuple(qs), tuple(ks), g, beta


def _dn_chunk_fn(q, k, v, gb, g64, bb):
    nb = q.shape[0]
    ri = lax.broadcasted_iota(jnp.int32, (nb, CHUNK, CHUNK), 1)
    ci = lax.broadcasted_iota(jnp.int32, (nb, CHUNK, CHUNK), 2)
    tri = ri >= ci
    strict = ri > ci
    tril = tri.astype(F32)
    eye = (ri == ci).astype(F32)
    ones = jnp.ones((nb, CHUNK, CHUNK), F32)
    gc = _chunk_cumsum(gb)
    gc64 = _chunk_cumsum(g64)
    grow = _bmmx(ones, eye * gc64)
    diff = gc64 - grow
    decay = jnp.where(tri, jnp.exp(jnp.where(tri, diff, 0.0)), 0.0)
    kb = k * bb
    vb = v * bb
    a = jnp.where(strict, _bmm_nt(kb, k) * decay, 0.0)
    tinv = _unit_lower_inv(a)
    u = _bmm(tinv, vb)
    w = _bmm(tinv, kb * jnp.exp(gc))
    qs = q * (1.0 / math.sqrt(HEAD))
    qk = _bmm_nt(qs, k) * decay
    qg = qs * jnp.exp(gc)
    glast = jnp.sum(gb, axis=1, keepdims=True)
    kdec = k * jnp.exp(glast - gc)
    n_mat = _bmm_tn(kdec, w)
    b_mat = _bmm_tn(kdec, u)
    q_eff = qg - _bmm(qk, w)
    o_own = _bmm(qk, u)
    return n_mat, b_mat, q_eff, o_own, jnp.exp(glast)


def _dn_rec_fn(s, n_mat, b_mat, eg):
    return s * eg - _mm(n_mat, s) + b_mat


def _dn_o_fn(s, q_eff, o_own):
    return _bmm(q_eff, s) + o_own


def _dn_out_fn(o, z, w):
    return _rms(o, w, HEAD) * _silu(z)


def _row_tile(t, parts=8):
    return t // parts if (t // parts) % 16 == 0 else t


def _tile(n, pref, unit):
    best = n
    for cand in range(unit, min(n, pref) + 1, unit):
        if n % cand == 0:
            best = cand
    return best if best <= pref else n


def _rows_call(name, body, rows, consts, outs, accs, r, host=None):
    rows = [a if isinstance(a, tuple) else (a, a.shape[1], 0) for a in rows]
    t = rows[0][0].shape[0]
    zero = lambda nd: (lambda i: (0,) * nd)
    in_specs = [pl.BlockSpec((r, w), functools.partial(lambda i, b: (i, b), b=blk)) for _, w, blk in rows]
    rows = [a for a, _, _ in rows]
    in_specs += [pl.BlockSpec(a.shape, zero(a.ndim)) for a in consts]
    out_shape = [jax.ShapeDtypeStruct((t, w), dt) for w, dt in outs] + [jax.ShapeDtypeStruct(s, F32) for s in accs]
    out_specs = [pl.BlockSpec((r, w), lambda i: (i, 0)) for w, _ in outs] + [pl.BlockSpec(s, zero(len(s))) for s in accs]
    return _pcall(body, name, (t // r,), in_specs, out_specs, out_shape, [*rows, *consts], ("arbitrary",), host=host)


def _accumulate(ref, val):
    @pl.when(pl.program_id(0) == 0)
    def _():
        ref[...] = jnp.zeros(ref.shape, ref.dtype)

    ref[...] += val


def _matmul(name, a, b, dims, out_dtype, res=None, host=None):
    if dims == "nn":
        (m, k), n = a.shape, b.shape[1]
    elif dims == "nt":
        (m, k), n = a.shape, b.shape[0]
    else:
        (k, m), n = a.shape, b.shape[1]
    tm = _tile(m, 1100, 16) if dims != "tn" else _tile(m, 640, 128)
    tn = _tile(n, 1408, 128)
    if dims == "nn":
        a_spec, b_spec, dn = pl.BlockSpec((tm, k), lambda i, j: (i, 0)), pl.BlockSpec((k, tn), lambda i, j: (0, j)), NN
    elif dims == "nt":
        a_spec, b_spec, dn = pl.BlockSpec((tm, k), lambda i, j: (i, 0)), pl.BlockSpec((tn, k), lambda i, j: (j, 0)), NT
    else:
        a_spec, b_spec, dn = pl.BlockSpec((k, tm), lambda i, j: (0, i)), pl.BlockSpec((k, tn), lambda i, j: (0, j)), TN
    o_spec = pl.BlockSpec((tm, tn), lambda i, j: (i, j))

    def body(*refs):
        a_ref, b_ref, o_ref = refs[0], refs[1], refs[-1]
        acc = lax.dot_general(a_ref[...].astype(BF16), b_ref[...].astype(BF16), (dn, ((), ())),
                              preferred_element_type=F32)
        if res is not None:
            acc = acc + refs[2][...]
        o_ref[...] = acc.astype(out_dtype)

    ins = [a, b] + ([res] if res is not None else [])
    specs = [a_spec, b_spec] + ([o_spec] if res is not None else [])
    return _pcall(body, name, (m // tm, n // tn), specs, o_spec, jax.ShapeDtypeStruct((m, n), out_dtype), ins,
                  ("arbitrary", "arbitrary"), host=host)


def _rms_fwd(name, h, w, host=None):
    n = h.shape[1]

    def body(h_ref, w_ref, o_ref):
        o_ref[...] = _rms(h_ref[...], w_ref[...], n).astype(BF16)

    return _rows_call(name, body, [h], [w], [(n, BF16)], [], _row_tile(h.shape[0]), host=host)[0]


def _rms_bwd(name, h, w, cts, resid, host=None):
    n = h.shape[1]
    nct = len(cts)

    def body(*refs):
        h_ref, ct_refs, r_ref, w_ref = refs[0], refs[1:1 + nct], refs[1 + nct], refs[2 + nct]
        dh_ref, dh16_ref, dw_ref = refs[-3], refs[-2], refs[-1]
        ct = ct_refs[0][...].astype(F32)
        for c in ct_refs[1:]:
            ct = ct + c[...].astype(F32)
        _, vjp = jax.vjp(lambda x, ww: _rms(x, ww, n), h_ref[...], w_ref[...])
        dh, dw = vjp(ct)
        dh = dh + r_ref[...]
        dh_ref[...] = dh
        dh16_ref[...] = dh.astype(BF16)
        _accumulate(dw_ref, dw)

    return _rows_call(name, body, [h, *cts, resid], [w], [(n, F32), (n, BF16)], [(1, n)], _row_tile(h.shape[0]), host=host)


def _mla_consts_from_refs(qa, wq, kva, wkv, qn, kn, perm):
    f = lambda r: r[...].astype(F32)
    return dict(
        qa_w=f(qa), kva_w=f(kva), perm=f(perm),
        wq_n=[wq[h * QK_PAD:h * QK_PAD + HEAD, :].astype(F32) for h in range(HEADS)],
        wq_r=[wq[h * QK_PAD + HEAD:(h + 1) * QK_PAD, :].astype(F32) for h in range(HEADS)],
        wk_n=[wkv[h * QK_PAD:h * QK_PAD + HEAD, :].astype(F32) for h in range(HEADS)],
        wv=[wkv[h * QK_PAD + HEAD:(h + 1) * QK_PAD, :].astype(F32) for h in range(HEADS)],
        qn_n=qn[:, 0:HEAD], qn_r=qn[:, HEAD:QK_PAD], kn_n=kn[:, 0:HEAD], kn_r=kn[:, HEAD:QK_PAD])


def _mla_prep_fwd(q_lat, kv_lat, k_pe, cos, sin, qa, wq, kva, wkv, qn, kn, perm):
    def body(ql, kvl, kp, c, s, qa_r, wq_r, kva_r, wkv_r, qn_r, kn_r, p_r, q_out, k_out, v_out):
        consts = _mla_consts_from_refs(qa_r, wq_r, kva_r, wkv_r, qn_r, kn_r, p_r)
        outs = _mla_prep_fn((ql[...], kvl[...], kp[...], c[...], s[...]), consts)
        for h in range(HEADS):
            q_n, q_r, k_n, k_r, v = outs[5 * h:5 * h + 5]
            q_out[:, h * QK_PAD:h * QK_PAD + HEAD] = q_n.astype(BF16)
            q_out[:, h * QK_PAD + HEAD:(h + 1) * QK_PAD] = q_r.astype(BF16)
            k_out[:, h * QK_PAD:h * QK_PAD + HEAD] = k_n.astype(BF16)
            k_out[:, h * QK_PAD + HEAD:(h + 1) * QK_PAD] = k_r.astype(BF16)
            v_out[:, h * HEAD:(h + 1) * HEAD] = v.astype(BF16)

    return _rows_call("mla_prep_fwd", body, [q_lat, kv_lat, k_pe, cos, sin], [qa, wq, kva, wkv, qn, kn, perm],
                      [(HEADS * QK_PAD, BF16), (HEADS * QK_PAD, BF16), (DN_WIDTH, BF16)], [], _row_tile(cos.shape[0], 4))


def _mla_prep_bwd(q_lat, kv_lat, k_pe, cos, sin, dq, dk, dv, qa, wq, kva, wkv, qn, kn, perm, host=None):
    def body(ql, kvl, kp, c, s, dq_r, dk_r, dv_r, qa_r, wq_r, kva_r, wkv_r, qn_r, kn_r, p_r,
             dql, dkvl, dkp, dqa, dwq, dkva, dwkv, dqn, dkn):
        consts = _mla_consts_from_refs(qa_r, wq_r, kva_r, wkv_r, qn_r, kn_r, p_r)
        cc, ss, pm = c[...], s[...], consts.pop("perm")
        _, vjp = jax.vjp(lambda rows, cs: _mla_prep_fn((*rows, cc, ss), dict(cs, perm=pm)), (ql[...], kvl[...], kp[...]),
                         consts)
        cts = []
        for h in range(HEADS):
            cts += [dq_r[:, h * QK_PAD:h * QK_PAD + HEAD], dq_r[:, h * QK_PAD + HEAD:(h + 1) * QK_PAD],
                    dk_r[:, h * QK_PAD:h * QK_PAD + HEAD], dk_r[:, h * QK_PAD + HEAD:(h + 1) * QK_PAD],
                    dv_r[:, h * HEAD:(h + 1) * HEAD]]
        (d_ql, d_kvl, d_kp), dc = vjp(tuple(cts))
        dql[...] = d_ql.astype(BF16)
        dkvl[...] = d_kvl.astype(BF16)
        dkp[...] = d_kp.astype(BF16)
        first = pl.program_id(0) == 0

        def acc(ref, sl, val):
            @pl.when(first)
            def _():
                ref[sl] = val

            @pl.when(jnp.logical_not(first))
            def _():
                ref[sl] += val

        full = (slice(None), slice(None))
        acc(dqa, full, dc["qa_w"])
        acc(dkva, full, dc["kva_w"])
        for h in range(HEADS):
            acc(dwq, (slice(h * QK_PAD, h * QK_PAD + HEAD), slice(None)), dc["wq_n"][h])
            acc(dwq, (slice(h * QK_PAD + HEAD, (h + 1) * QK_PAD), slice(None)), dc["wq_r"][h])
            acc(dwkv, (slice(h * QK_PAD, h * QK_PAD + HEAD), slice(None)), dc["wk_n"][h])
            acc(dwkv, (slice(h * QK_PAD + HEAD, (h + 1) * QK_PAD), slice(None)), dc["wv"][h])
        acc(dqn, (slice(None), slice(0, HEAD)), dc["qn_n"])
        acc(dqn, (slice(None), slice(HEAD, QK_PAD)), dc["qn_r"])
        acc(dkn, (slice(None), slice(0, HEAD)), dc["kn_n"])
        acc(dkn, (slice(None), slice(HEAD, QK_PAD)), dc["kn_r"])

    return _rows_call("mla_prep_bwd", body, [q_lat, kv_lat, k_pe, cos, sin, dq, dk, dv],
                      [qa, wq, kva, wkv, qn, kn, perm],
                      [(LORA, BF16), (LORA, BF16), (HEAD, BF16)],
                      [(1, LORA), wq.shape, (1, LORA), wkv.shape, (1, QK_PAD), (1, QK_PAD)], _row_tile(cos.shape[0], 4),
                      host=host)


ATTN_Q_ROWS = 512


def _attn_blocks(t):
    return [(r0, min(ATTN_Q_ROWS, t - r0)) for r0 in range(0, t, ATTN_Q_ROWS)]


def _attn_fwd(q, k, v, host=None):
    t = q.shape[0]

    def body(q_ref, k_ref, v_ref, o_ref):
        for r0, rows in _attn_blocks(t):
            ext = r0 + rows
            o_ref[r0:ext, :] = _attn_fn(q_ref[r0:ext, :], k_ref[0:ext, :], v_ref[0:ext, :], r0)

    qk_spec = pl.BlockSpec((t, QK_PAD), lambda h: (0, h))
    v_spec = pl.BlockSpec((t, HEAD), lambda h: (0, h))
    return _pcall(body, "attn_fwd", (HEADS,), [qk_spec, qk_spec, v_spec], v_spec,
                  jax.ShapeDtypeStruct((t, HEADS * HEAD), F32), [q, k, v], ("arbitrary",), host=host)


def _attn_bwd(q, k, v, do, host=None):
    t = q.shape[0]

    def body(q_ref, k_ref, v_ref, do_ref, dq_ref, dk_ref, dv_ref):
        dk_ref[...] = jnp.zeros(dk_ref.shape, F32)
        dv_ref[...] = jnp.zeros(dv_ref.shape, F32)
        for r0, rows in _attn_blocks(t):
            ext = r0 + rows
            _, vjp = jax.vjp(functools.partial(_attn_fn, row0=r0), q_ref[r0:ext, :].astype(F32),
                             k_ref[0:ext, :].astype(F32), v_ref[0:ext, :].astype(F32))
            dq, dk, dv = vjp(do_ref[r0:ext, :])
            dq_ref[r0:ext, :] = dq
            dk_ref[0:ext, :] += dk
            dv_ref[0:ext, :] += dv

    qk_spec = pl.BlockSpec((t, QK_PAD), lambda h: (0, h))
    v_spec = pl.BlockSpec((t, HEAD), lambda h: (0, h))
    return _pcall(body, "attn_bwd", (HEADS,), [qk_spec, qk_spec, v_spec, v_spec], [qk_spec, qk_spec, v_spec],
                  [jax.ShapeDtypeStruct((t, HEADS * QK_PAD), F32), jax.ShapeDtypeStruct((t, HEADS * QK_PAD), F32),
                   jax.ShapeDtypeStruct((t, HEADS * HEAD), F32)], [q, k, v, do], ("arbitrary",), host=host)


def _mix_out_proj(o_mla, o_dn, z, w_mla, w_dn, w_out, h0, w_ffn):
    def body(om_ref, od_ref, z_ref, h0_ref, wm_ref, wd_ref, wo_ref, wf_ref, mixed_ref, h1_ref, n2_ref):
        for h in range(HEADS):
            sl = slice(h * HEAD, (h + 1) * HEAD)
            mixed_ref[:, sl] = _rms(om_ref[:, sl], wm_ref[...], HEAD).astype(BF16)
            mixed_ref[:, DN_WIDTH + h * HEAD:DN_WIDTH + (h + 1) * HEAD] = _dn_out_fn(od_ref[:, sl], z_ref[:, sl],
                                                                                     wd_ref[...]).astype(BF16)
        h1 = _mm(mixed_ref[...], wo_ref[...]) + h0_ref[...]
        h1_ref[...] = h1
        n2_ref[...] = _rms(h1, wf_ref[...], D_MODEL).astype(BF16)

    return _rows_call("mix_out_proj", body, [o_mla, o_dn, z, h0], [w_mla, w_dn, w_out, w_ffn],
                      [(D_MODEL, BF16), (D_MODEL, F32), (D_MODEL, BF16)], [], _row_tile(o_mla.shape[0], 4))


def _down_proj_loss(act, w_down, h1, tgt, n_valid):
    t, n = h1.shape
    r = _row_tile(t, 4)

    def body(a_ref, h_ref, t_ref, w_ref, dy_ref, dy16_ref, acc_ref):
        h2 = _mm(a_ref[...], w_ref[...]) + h_ref[...]
        rows = pl.program_id(0) * r + lax.broadcasted_iota(jnp.int32, (r, n), 0)
        valid = jnp.logical_and(rows >= N_META, rows < n_valid)
        e = jnp.where(valid, h2 - t_ref[...], 0.0)
        dy = e * (1.0 / n)
        dy_ref[...] = dy
        dy16_ref[...] = dy.astype(BF16)
        _accumulate(acc_ref, jnp.sum(e * e, axis=0, keepdims=True))

    return _rows_call("down_proj_loss", body, [act, h1, tgt], [w_down], [(n, F32), (n, BF16)], [(1, n)], r)


def _in_proj_bwd_x(pieces, win, host=None):
    offs = np.cumsum([0] + [p.shape[1] for p in pieces])

    def body(*refs):
        p_refs, w_ref, o_ref = refs[:len(pieces)], refs[len(pieces)], refs[-1]
        acc = None
        for i, p_ref in enumerate(p_refs):
            part = _mm(p_ref[...], w_ref[int(offs[i]):int(offs[i + 1]), :])
            acc = part if acc is None else acc + part
        o_ref[...] = acc.astype(BF16)

    return _rows_call("in_dx", body, pieces, [win], [(win.shape[1], BF16)], [], _row_tile(pieces[0].shape[0], 4), host=host)[0]


def _in_proj_bwd_w(pieces, u):
    offs = np.cumsum([0] + [p.shape[1] for p in pieces])

    def body(*refs):
        p_refs, u_ref, o_ref = refs[:len(pieces)], refs[len(pieces)], refs[-1]
        first = pl.program_id(0) == 0
        uv = u_ref[...]
        for i, p_ref in enumerate(p_refs):
            rows = slice(int(offs[i]), int(offs[i + 1]))
            part = _mm_tn(p_ref[...], uv)

            @pl.when(first)
            def _():
                o_ref[rows, :] = part

            @pl.when(jnp.logical_not(first))
            def _():
                o_ref[rows, :] += part

    return _rows_call("in_dw", body, [*pieces, u], [], [], [(int(offs[-1]), u.shape[1])], _row_tile(u.shape[0], 2))[0]


def _ffn_in_bwd(dgpre, dup, w_gate_t, w_up_t, h1, dy, w_ffn, host=None):
    n = h1.shape[1]

    def body(dg_ref, du_ref, h_ref, dy_ref, wg_ref, wu_ref, w_ref, dh_ref, dh16_ref, dw_ref):
        ct = _mm(dg_ref[...], wg_ref[...]) + _mm(du_ref[...], wu_ref[...])
        _, vjp = jax.vjp(lambda x, ww: _rms(x, ww, n), h_ref[...], w_ref[...])
        dh, dw = vjp(ct)
        dh = dh + dy_ref[...]
        dh_ref[...] = dh
        dh16_ref[...] = dh.astype(BF16)
        _accumulate(dw_ref, dw)

    return _rows_call("ffn_in_bwd", body, [dgpre, dup, h1, dy], [w_gate_t, w_up_t, w_ffn], [(n, F32), (n, BF16)], [(1, n)],
                      _row_tile(h1.shape[0]), host=host)


def _mix_out_bwd(o_mla, o_dn, z, dh1, w_out, w_mla, w_dn, host=None):
    def body(om_ref, od_ref, z_ref, dh_ref, wo_ref, wm_ref, wd_ref, dom_ref, dod_ref, dz_ref, dwm_ref, dwd_ref):
        dwm = dwd = None
        for h in range(HEADS):
            sl = slice(h * HEAD, (h + 1) * HEAD)
            _, vjp = jax.vjp(lambda o, w: _rms(o, w, HEAD), om_ref[:, sl], wm_ref[...])
            do, dw = vjp(_mm_nt(dh_ref[...], wo_ref[sl, :]))
            dom_ref[:, sl] = do
            dwm = dw if dwm is None else dwm + dw
            _, vjp = jax.vjp(_dn_out_fn, od_ref[:, sl], z_ref[:, sl], wd_ref[...])
            do, dz, dw = vjp(_mm_nt(dh_ref[...], wo_ref[DN_WIDTH + h * HEAD:DN_WIDTH + (h + 1) * HEAD, :]))
            dod_ref[:, sl] = do
            dz_ref[:, sl] = dz.astype(BF16)
            dwd = dw if dwd is None else dwd + dw
        _accumulate(dwm_ref, dwm)
        _accumulate(dwd_ref, dwd)

    return _rows_call("mix_out_bwd", body, [o_mla, o_dn, z, dh1], [w_out, w_mla, w_dn],
                      [(DN_WIDTH, F32), (DN_WIDTH, F32), (DN_WIDTH, BF16)], [(1, HEAD), (1, HEAD)],
                      _row_tile(o_mla.shape[0], 4), host=host)


def _shift_down(x, s):
    if s == 0:
        return x
    rows = lax.broadcasted_iota(jnp.int32, x.shape, 0)
    return jnp.where(rows >= s, pltpu.roll(x, s, 0), 0.0)


def _shift_up(x, s):
    if s == 0:
        return x
    t = x.shape[0]
    rows = lax.broadcasted_iota(jnp.int32, x.shape, 0)
    return jnp.where(rows < t - s, pltpu.roll(x, t - s, 0), 0.0)


def _col_call(name, body, cols, taps, outs, tap_outs, cw, host=None):
    t, c = cols[0].shape[0], taps[0].shape[1]
    in_specs = [pl.BlockSpec((t, cw), lambda j: (0, j)) for _ in cols]
    in_specs += [pl.BlockSpec((a.shape[0], cw), lambda j: (0, j)) for a in taps]
    out_shape = [jax.ShapeDtypeStruct((t, c), dt) for dt in outs] + [jax.ShapeDtypeStruct((n, c), F32) for n in tap_outs]
    out_specs = [pl.BlockSpec((t, cw), lambda j: (0, j)) for _ in outs]
    out_specs += [pl.BlockSpec((n, cw), lambda j: (0, j)) for n in tap_outs]
    return _pcall(body, name, (c // cw,), in_specs, out_specs, out_shape, [*cols, *taps], ("arbitrary",), host=host)


def _causal_conv(x, w_ref, width, zero_tail=False):
    down = (lambda a, s: pltpu.roll(a, s, 0)) if zero_tail else _shift_down
    acc = w_ref[width - 1:width, :] * x
    for j in range(width - 1):
        acc = acc + w_ref[j:j + 1, :] * down(x, width - 1 - j)
    return acc


def _causal_conv_bwd(x, dpre, w_ref, dx_ref, dw_ref, width, zero_tail=False):
    t = x.shape[0]
    down = (lambda a, s: pltpu.roll(a, s, 0)) if zero_tail else _shift_down
    up = (lambda a, s: pltpu.roll(a, t - s, 0)) if zero_tail else _shift_up
    dx = w_ref[width - 1:width, :] * dpre
    dw_ref[width - 1:width, :] = jnp.sum(dpre * x, axis=0, keepdims=True)
    for j in range(width - 1):
        s = width - 1 - j
        dx = dx + w_ref[j:j + 1, :] * up(dpre, s)
        dw_ref[j:j + 1, :] = jnp.sum(dpre * down(x, s), axis=0, keepdims=True)
    dx_ref[...] = dx.astype(dx_ref.dtype)


def _dsilu(x):
    sg = jax.nn.sigmoid(x)
    return sg * (1.0 + x * (1.0 - sg))


def _dn_conv_fwd(x, w):
    def body(x_ref, w_ref, y_ref):
        y_ref[...] = _silu(_causal_conv(x_ref[...], w_ref, 4, zero_tail=True))

    return _col_call("dn_conv_fwd", body, [x], [w], [F32], [], 256)[0]


def _dn_conv_bwd(x, w, dy):
    def body(x_ref, dy_ref, w_ref, dx_ref, dw_ref):
        xv = x_ref[...]
        dpre = dy_ref[...] * _dsilu(_causal_conv(xv, w_ref, 4, zero_tail=True))
        _causal_conv_bwd(xv, dpre, w_ref, dx_ref, dw_ref, 4, zero_tail=True)

    return _col_call("dn_conv_bwd", body, [x, dy], [w], [BF16], [4], 256)


def _ffn_glu_fwd(n2, w_gate_t, w_up_t, w, b, host=None):
    t, k = n2.shape
    c, cw = w_gate_t.shape[0], 256

    def body(n_ref, wg_ref, wu_ref, w_ref, b_ref, g_ref, u_ref, a_ref):
        nv = n_ref[...]
        g16 = _mm_nt(nv, wg_ref[...]).astype(BF16)
        u16 = _mm_nt(nv, wu_ref[...]).astype(BF16)
        g_ref[...] = g16
        u_ref[...] = u16
        gate = _causal_conv(g16.astype(F32), w_ref, 3) + b_ref[...]
        a_ref[...] = (_silu(gate) * u16.astype(F32)).astype(BF16)

    wspec = pl.BlockSpec((cw, k), lambda j: (j, 0))
    col = pl.BlockSpec((t, cw), lambda j: (0, j))
    in_specs = [pl.BlockSpec((t, k), lambda j: (0, 0)), wspec, wspec, pl.BlockSpec((w.shape[0], cw), lambda j: (0, j)),
                pl.BlockSpec((1, cw), lambda j: (0, j))]
    return _pcall(body, "ffn_glu_fwd", (c // cw,), in_specs, [col] * 3, [jax.ShapeDtypeStruct((t, c), BF16)] * 3,
                  [n2, w_gate_t, w_up_t, w, b], ("arbitrary",), host=host)


def _ffn_glu_bwd(gpre, up, dy16, w_down, w, b):
    t, k = dy16.shape
    c, cw = w_down.shape[0], 256

    def body(g_ref, u_ref, dy_ref, wd_ref, w_ref, b_ref, dg_ref, du_ref, dw_ref, db_ref):
        gv = g_ref[...].astype(F32)
        gate = _causal_conv(gv, w_ref, 3) + b_ref[...]
        da = _mm_nt(dy_ref[...], wd_ref[...])
        sg = jax.nn.sigmoid(gate)
        du_ref[...] = (da * (gate * sg)).astype(BF16)
        dgate = da * u_ref[...].astype(F32) * (sg * (1.0 + gate * (1.0 - sg)))
        db_ref[...] = jnp.sum(dgate, axis=0, keepdims=True)
        _causal_conv_bwd(gv, dgate, w_ref, dg_ref, dw_ref, 3)

    col = pl.BlockSpec((t, cw), lambda j: (0, j))
    taps = lambda rows: pl.BlockSpec((rows, cw), lambda j: (0, j))
    in_specs = [col, col, pl.BlockSpec((t, k), lambda j: (0, 0)), pl.BlockSpec((cw, k), lambda j: (j, 0)),
                taps(w.shape[0]), taps(1)]
    return _pcall(body, "ffn_glu_bwd", (c // cw,), in_specs, [col, col, taps(w.shape[0]), taps(1)],
                  [jax.ShapeDtypeStruct((t, c), BF16)] * 2 + [jax.ShapeDtypeStruct((w.shape[0], c), F32),
                                                             jax.ShapeDtypeStruct((1, c), F32)],
                  [gpre, up, dy16, w_down, w, b], ("arbitrary",))


def _dn_prep_consts(sa, sb, al, dt):
    return dict(sel_a=sa[...], sel_b=sb[...], alog=al[...], dtb=dt[...])


def _dn_prep_fwd(conv, ab, sel_a, sel_b, alog, dtb):
    def body(c_ref, ab_ref, sa, sb, al, dt, q_out, k_out, g_out, b_out):
        qc = tuple(c_ref[:, h * HEAD:(h + 1) * HEAD] for h in range(HEADS))
        kc = tuple(c_ref[:, DN_WIDTH + h * HEAD:DN_WIDTH + (h + 1) * HEAD] for h in range(HEADS))
        qs, ks, g, beta = _dn_prep_fn((qc, kc, ab_ref[...]), _dn_prep_consts(sa, sb, al, dt))
        for h in range(HEADS):
            q_out[:, h * HEAD:(h + 1) * HEAD] = qs[h]
            k_out[:, h * HEAD:(h + 1) * HEAD] = ks[h]
        g_out[...] = g
        b_out[...] = beta

    return _rows_call("dn_prep_fwd", body, [conv, ab], [sel_a, sel_b, alog, dtb], [(DN_WIDTH, F32)] * 4, [],
                      _row_tile(conv.shape[0], 4))


def _dn_prep_bwd(conv, ab, dq, dk, dv, dg, db, sel_a, sel_b, alog, dtb):
    def body(c_ref, ab_ref, dq_r, dk_r, dv_r, dg_r, db_r, sa, sb, al, dt, dc_out, dab_out, dal_out, ddt_out):
        qc = tuple(c_ref[:, h * HEAD:(h + 1) * HEAD] for h in range(HEADS))
        kc = tuple(c_ref[:, DN_WIDTH + h * HEAD:DN_WIDTH + (h + 1) * HEAD] for h in range(HEADS))
        consts = _dn_prep_consts(sa, sb, al, dt)
        sel = dict(sel_a=consts["sel_a"], sel_b=consts["sel_b"])
        _, vjp = jax.vjp(lambda rows, ad: _dn_prep_fn(rows, {**sel, **ad}), (qc, kc, ab_ref[...]),
                         dict(alog=consts["alog"], dtb=consts["dtb"]))
        cq = tuple(dq_r[:, h * HEAD:(h + 1) * HEAD] for h in range(HEADS))
        ck = tuple(dk_r[:, h * HEAD:(h + 1) * HEAD] for h in range(HEADS))
        (dqc, dkc, dab), dad = vjp((cq, ck, dg_r[...], db_r[...]))
        for h in range(HEADS):
            dc_out[:, h * HEAD:(h + 1) * HEAD] = dqc[h]
            dc_out[:, DN_WIDTH + h * HEAD:DN_WIDTH + (h + 1) * HEAD] = dkc[h]
        dc_out[:, 2 * DN_WIDTH:3 * DN_WIDTH] = dv_r[...]
        dab_out[...] = dab.astype(BF16)
        _accumulate(dal_out, dad["alog"])
        _accumulate(ddt_out, dad["dtb"])

    return _rows_call("dn_prep_bwd", body, [conv, ab, dq, dk, dv, dg, db], [sel_a, sel_b, alog, dtb],
                      [(3 * DN_WIDTH, F32), (HEAD, BF16)], [(1, DN_WIDTH), (1, DN_WIDTH)], _row_tile(conv.shape[0], 4))


def _chunk_batch(t):
    nc = t // CHUNK
    return nc // 2 if nc % 2 == 0 else nc


def _dn_chunk_specs(t, nb):
    rows = nb * CHUNK
    blk = pl.BlockSpec((rows, HEAD), lambda h, b: (b, h))
    vblk = pl.BlockSpec((rows, HEAD), lambda h, b: (b, 2 * HEADS + h))
    mat = pl.BlockSpec((nb, HEAD, HEAD), lambda h, b: (b, h, 0))
    return rows, blk, vblk, mat


def _dn_chunk_fwd(qn, kn, conv, g, beta, host=None):
    t = qn.shape[0]
    nb = _chunk_batch(t)
    rows, blk, vblk, mat = _dn_chunk_specs(t, nb)

    def body(q_ref, k_ref, v_ref, g_ref, b_ref, n_o, b_o, qe_o, oo_o, eg_o):
        r3 = lambda x: x.reshape(nb, CHUNK, x.shape[-1])
        n_mat, b_mat, q_eff, o_own, eg = _dn_chunk_fn(r3(q_ref[...]), r3(k_ref[...]), r3(v_ref[...]), r3(g_ref[...]),
                                                      r3(g_ref[:, 0:CHUNK]), r3(b_ref[...]))
        n_o[...] = n_mat
        b_o[...] = b_mat
        qe_o[...] = q_eff.reshape(rows, HEAD)
        oo_o[...] = o_own.reshape(rows, HEAD)
        eg_o[...] = jnp.broadcast_to(eg, (nb, HEAD, HEAD))

    nc = t // CHUNK
    mats = jax.ShapeDtypeStruct((nc, DN_WIDTH, HEAD), F32)
    rowsd = jax.ShapeDtypeStruct((t, DN_WIDTH), F32)
    return _pcall(body, "dn_chunk_fwd", (HEADS, t // rows), [blk, blk, vblk, blk, blk], [mat, mat, blk, blk, mat],
                  [mats, mats, rowsd, rowsd, mats], [qn, kn, conv, g, beta], ("arbitrary", "arbitrary"), host=host)


def _dn_chunk_bwd(qn, kn, conv, g, beta, sall, gall, dq_eff, do, host=None):
    t = qn.shape[0]
    nb = _chunk_batch(t)
    rows, blk, vblk, mat = _dn_chunk_specs(t, nb)

    def body(q_ref, k_ref, v_ref, g_ref, b_ref, s_ref, ga_ref, dqe_ref, do_ref, dq_o, dk_o, dv_o, dg_o, db_o):
        r3 = lambda x: x.reshape(nb, CHUNK, x.shape[-1])
        _, vjp = jax.vjp(_dn_chunk_fn, r3(q_ref[...]), r3(k_ref[...]), r3(v_ref[...]), r3(g_ref[...]),
                         r3(g_ref[:, 0:CHUNK]), r3(b_ref[...]))
        s, ga = s_ref[...], ga_ref[...]
        d_n = -_bmm_nt(ga, s)
        d_eg = jnp.sum(ga * s, axis=1, keepdims=True)
        dq, dk, dv, dg, dg64, db = vjp((d_n, ga, r3(dqe_ref[...]), r3(do_ref[...]), d_eg))
        for o_ref, val in zip((dq_o, dk_o, dv_o, dg_o, db_o), (dq, dk, dv, dg, db)):
            o_ref[...] = val.reshape(rows, HEAD)
        dg_o[:, 0:CHUNK] += dg64.reshape(rows, CHUNK)

    return _pcall(body, "dn_chunk_bwd", (HEADS, t // rows), [blk, blk, vblk, blk, blk, mat, mat, blk, blk], [blk] * 5,
                  [jax.ShapeDtypeStruct((t, DN_WIDTH), F32)] * 5, [qn, kn, conv, g, beta, sall, gall, dq_eff, do],
                  ("arbitrary", "arbitrary"), host=host)


def _dn_rec_fwd(n_mat, b_mat, eg, host=None):
    nc = n_mat.shape[0]
    nb = _chunk_batch(nc * CHUNK)
    spec = pl.BlockSpec((nb, DN_WIDTH, HEAD), lambda i: (i, 0, 0))

    def body(n_ref, b_ref, eg_ref, sall_ref, s_scr):
        @pl.when(pl.program_id(0) == 0)
        def _():
            s_scr[...] = jnp.zeros(s_scr.shape, F32)

        for j in range(nb):
            sall_ref[j] = s_scr[...]
            for h in range(HEADS):
                sl = slice(h * HEAD, (h + 1) * HEAD)
                s_scr[sl, :] = _dn_rec_fn(s_scr[sl, :], n_ref[j, sl, :], b_ref[j, sl, :],
                                          eg_ref[j, h * HEAD:h * HEAD + 1, :])

    return _pcall(body, "dn_rec_fwd", (nc // nb,), [spec] * 3, spec, jax.ShapeDtypeStruct((nc, DN_WIDTH, HEAD), F32),
                  [n_mat, b_mat, eg], ("arbitrary",), scratch_shapes=[pltpu.VMEM((DN_WIDTH, HEAD), F32)], host=host)


def _dn_rec_bwd(n_mat, eg, ds_out, host=None):
    nc = n_mat.shape[0]
    nb = _chunk_batch(nc * CHUNK)
    steps = nc // nb
    spec = pl.BlockSpec((nb, DN_WIDTH, HEAD), lambda i: (steps - 1 - i, 0, 0))

    def body(n_ref, eg_ref, dso_ref, gall_ref, g_scr):
        @pl.when(pl.program_id(0) == 0)
        def _():
            g_scr[...] = jnp.zeros(g_scr.shape, F32)

        for j in reversed(range(nb)):
            gall_ref[j] = g_scr[...]
            for h in range(HEADS):
                sl = slice(h * HEAD, (h + 1) * HEAD)
                gv = g_scr[sl, :]
                g_scr[sl, :] = (gv * eg_ref[j, h * HEAD:h * HEAD + 1, :] - _mm_tn(n_ref[j, sl, :], gv)
                                + dso_ref[j, sl, :])

    return _pcall(body, "dn_rec_bwd", (steps,), [spec] * 3, spec, jax.ShapeDtypeStruct((nc, DN_WIDTH, HEAD), F32),
                  [n_mat, eg, ds_out], ("arbitrary",), scratch_shapes=[pltpu.VMEM((DN_WIDTH, HEAD), F32)], host=host)


def _dn_o_fwd(sall, q_eff, o_own):
    t = q_eff.shape[0]
    nb = _chunk_batch(t)
    rows, blk, _, mat = _dn_chunk_specs(t, nb)

    def body(s_ref, qe_ref, oo_ref, o_ref):
        r3 = lambda x: x.reshape(nb, CHUNK, HEAD)
        o_ref[...] = _dn_o_fn(s_ref[...], r3(qe_ref[...]), r3(oo_ref[...])).reshape(rows, HEAD)

    return _pcall(body, "dn_o_fwd", (HEADS, t // rows), [mat, blk, blk], blk, jax.ShapeDtypeStruct((t, DN_WIDTH), F32),
                  [sall, q_eff, o_own], ("arbitrary", "arbitrary"))


def _dn_o_bwd(sall, q_eff, do, host=None):
    t = q_eff.shape[0]
    nb = _chunk_batch(t)
    rows, blk, _, mat = _dn_chunk_specs(t, nb)

    def body(s_ref, qe_ref, do_ref, dqe_ref, ds_ref):
        r3 = lambda x: x.reshape(nb, CHUNK, HEAD)
        dov = r3(do_ref[...])
        dqe_ref[...] = _bmm_nt(dov, s_ref[...]).reshape(rows, HEAD)
        ds_ref[...] = _bmm_tn(r3(qe_ref[...]), dov)

    nc = t // CHUNK
    return _pcall(body, "dn_o_bwd", (HEADS, t // rows), [mat, blk, blk], [blk, mat],
                  [jax.ShapeDtypeStruct((t, DN_WIDTH), F32), jax.ShapeDtypeStruct((nc, DN_WIDTH, HEAD), F32)],
                  [sall, q_eff, do], ("arbitrary", "arbitrary"), host=host)


def _adamw_update(w, g, m, v):
    m2 = ADAM_B1 * m + (1.0 - ADAM_B1) * g
    v2 = ADAM_B2 * v + (1.0 - ADAM_B2) * (g * g)
    m_hat = m2 / (1.0 - ADAM_B1 ** ADAM_STEP)
    v_hat = v2 / (1.0 - ADAM_B2 ** ADAM_STEP)
    return -ADAM_LR * (m_hat / (jnp.sqrt(v_hat) + ADAM_EPS) + ADAM_WD * w), m2, v2


def _adamw_small(ws, gs, ms, vs):
    n = len(ws)

    def body(*refs):
        for i in range(n):
            d, m2, v2 = _adamw_update(refs[i][...], refs[n + i][...], refs[2 * n + i][...], refs[3 * n + i][...])
            refs[4 * n + i][...] = d
            refs[5 * n + i][...] = m2
            refs[6 * n + i][...] = v2

    shapes = [jax.ShapeDtypeStruct(a.shape, F32) for a in ws]
    outs = pl.pallas_call(body, name="adamw_small", out_shape=shapes * 3,
                          compiler_params=pltpu.CompilerParams(vmem_limit_bytes=VMEM_LIMIT))(*ws, *gs, *ms, *vs)
    return outs[:n], outs[n:2 * n], outs[2 * n:]


def _adamw_call(name, ws, gs, ms, vs, host=None):
    k = len(ws)
    rows, cols = ws[0].shape
    assert all(w.shape == (rows, cols) for w in ws) and all(g.shape == gs[0].shape for g in gs)
    by_rows = rows % 8 == 0

    def body(*refs):
        for i in range(k):
            w_ref, g_ref, m_ref, v_ref, g_out, d_ref, m_out, v_out = refs[i::k]
            gv = g_ref[...] if by_rows else g_ref[0:rows, :]
            g_out[...] = gv
            d_ref[...], m_out[...], v_out[...] = _adamw_update(w_ref[...], gv, m_ref[...], v_ref[...])

    if by_rows:
        tr = _tile(rows, 256 // k, 8)
        spec = g_spec = pl.BlockSpec((tr, cols), lambda i: (i, 0))
        grid = (rows // tr,)
    else:
        tc = _tile(cols, 256, 128)
        spec = pl.BlockSpec((rows, tc), lambda j: (0, j))
        g_spec = pl.BlockSpec((gs[0].shape[0], tc), lambda j: (0, j))
        grid = (cols // tc,)
    outs = _pcall(body, name, grid, [spec] * k + [g_spec] * k + [spec] * 2 * k, [spec] * 4 * k,
                  [jax.ShapeDtypeStruct((rows, cols), F32)] * 4 * k, [*ws, *gs, *ms, *vs], ("arbitrary",), host=host)
    return [outs[i::k] for i in range(k)]


def _adamw_rows3d(name, w, g, m, v):
    rows, _, cols = w.shape
    tr = max(d for d in range(1, 129) if rows % d == 0)

    def body(w_ref, g_ref, m_ref, v_ref, d_ref, m_out, v_out):
        d_ref[...], m_out[...], v_out[...] = _adamw_update(w_ref[...], g_ref[...], m_ref[...], v_ref[...])

    spec = pl.BlockSpec((tr, 1, cols), lambda i: (i, 0, 0))
    return pl.pallas_call(body, name=name, grid=(rows // tr,), in_specs=[spec] * 4, out_specs=[spec] * 3,
                          out_shape=[jax.ShapeDtypeStruct(w.shape, F32)] * 3, compiler_params=_cparams(("arbitrary",)))(
                              w, g, m, v)


def _rope_tables(t):
    half = ROPE // 2
    inv_freq = np.float32(ROPE_THETA) ** (-np.arange(half, dtype=np.float32) / np.float32(half))
    ang = np.arange(t, dtype=np.float32)[:, None] * inv_freq[None, :].astype(np.float32)
    z = np.zeros((t, HEAD - ROPE), np.float32)
    cos = np.concatenate([np.cos(ang), np.cos(ang), z], axis=1).astype(np.float32)
    sin = np.concatenate([np.sin(ang), np.sin(ang), z], axis=1).astype(np.float32)
    k = np.arange(HEAD)[:, None]
    l = np.arange(HEAD)[None, :]
    perm = np.where((l < half) & (k == l + half), -1.0, 0.0) + np.where((l >= half) & (l < ROPE) & (k == l - half), 1.0, 0.0)
    return jnp.asarray(cos), jnp.asarray(sin), jnp.asarray(perm.astype(np.float32))


def _win_to_pad(w):
    z = lambda n: jnp.zeros((n, w.shape[1]), w.dtype)
    return jnp.concatenate([w[576:2112], w[2112:2624], w[0:256], w[256:512], w[512:576], z(64), w[2624:2632], z(120)],
                           axis=0)


def _qk_to_pad(w):
    w4 = w.reshape(HEADS, QK_DIM, w.shape[-1])
    return jnp.concatenate([w4, jnp.zeros((HEADS, QK_PAD - QK_DIM, w.shape[-1]), w.dtype)], axis=1).reshape(
        HEADS * QK_PAD, w.shape[-1])


def _qk_from_pad(g):
    return g.reshape(HEADS, QK_PAD, g.shape[-1])[:, :QK_DIM].reshape(HEADS * QK_DIM, g.shape[-1])


def _ff_to_pad(a, axis):
    shape = list(a.shape)
    shape[axis:axis + 1] = [N_CHIPS, FF_SHARD]
    a4 = a.reshape(shape)
    shape[axis + 1] = FF_BLOCK - FF_SHARD
    out = jnp.concatenate([a4, jnp.zeros(shape, a.dtype)], axis=axis + 1)
    shape[axis:axis + 2] = [D_FF_P]
    return out.reshape(shape)


def _ff_from_pad(a, axis):
    shape = list(a.shape)
    shape[axis:axis + 1] = [N_CHIPS, FF_BLOCK]
    a4 = lax.slice_in_dim(a.reshape(shape), 0, FF_SHARD, axis=axis + 1)
    shape[axis:axis + 2] = [D_FF]
    return a4.reshape(shape)


class _LocalPlan:
    def __init__(self, wt):
        self.wt, self.grads = wt, {}

    def weight(self, name):
        return self.wt[name]

    def host(self, point):
        return None

    def grad(self, name, value):
        self.grads[name] = value


def _local_step(x, tgt, wt, plan=None):
    plan = _LocalPlan(wt) if plan is None else plan
    s = x.shape[0]
    n_valid = N_META + s
    t = -(-n_valid // HEAD) * HEAD
    assert t - n_valid >= 3, "the DeltaNet conv kernels rely on at least three zero rows after the sequence"
    zpad = jnp.zeros((t - n_valid, D_MODEL), F32)
    h0 = jnp.concatenate([wt["meta_tokens"], x, zpad], axis=0)
    tgt_p = jnp.concatenate([jnp.zeros((N_META, D_MODEL), F32), tgt, zpad], axis=0)
    cos, sin, perm = _rope_tables(t)
    qn_w = jnp.concatenate([wt["q_norm_w"], jnp.zeros((1, QK_PAD - QK_DIM), F32)], axis=1)
    kn_w = jnp.concatenate([wt["k_norm_w"], jnp.zeros((1, QK_PAD - QK_DIM), F32)], axis=1)
    head_id = jnp.arange(DN_WIDTH)[None, :] // HEAD
    lane = jnp.arange(HEAD)[:, None]
    sel_a = (lane == head_id).astype(F32)
    sel_b = (lane == head_id + HEADS).astype(F32)
    alog = jnp.repeat(wt["dn_A_log"], HEAD, axis=1)
    dtb = jnp.repeat(wt["dn_dt_bias"], HEAD, axis=1)
    conv_w, conv_b = wt["ffn_conv_w"], wt["ffn_conv_b"]

    u = _rms_fwd("attn_norm_fwd", h0, wt["attn_norm_w"], host=plan.host("attn_norm_fwd"))
    win, wq, wkv = plan.weight("w_in_t"), plan.weight("w_q_t"), plan.weight("w_kv_t")
    proj = _matmul("in_proj", u, win, "nt", F32)
    z = (proj, DN_WIDTH, 3)
    q_lat, kv_lat, k_pe, ab = (proj, LORA, 8), (proj, LORA, 9), (proj, HEAD, 20), (proj, HEAD, 21)
    mla_consts = (wt["q_a_norm_w"], wq, wt["kv_a_norm_w"], wkv, qn_w, kn_w, perm)
    q, k, v = _mla_prep_fwd(q_lat, kv_lat, k_pe, cos, sin, *mla_consts)
    o_mla = _attn_fwd(q, k, v, host=plan.host("attn_fwd"))
    conv = _dn_conv_fwd(proj, wt["dn_conv_w"])
    dn_consts = (sel_a, sel_b, alog, dtb)
    qn, kn, g, beta = _dn_prep_fwd(conv, ab, *dn_consts)
    n_mat, b_mat, q_eff, o_own, eg = _dn_chunk_fwd(qn, kn, conv, g, beta, host=plan.host("dn_chunk_fwd"))
    sall = _dn_rec_fwd(n_mat, b_mat, eg)
    o_dn = _dn_o_fwd(sall, q_eff, o_own)
    w_out = plan.weight("w_out")
    mixed, h1, n2 = _mix_out_proj(o_mla, o_dn, z, wt["mla_out_norm_w"], wt["dn_out_norm_w"], w_out, h0, wt["ffn_norm_w"])
    w_gate, w_up = plan.weight("w_gate_t"), plan.weight("w_up_t")
    gpre, up, act = _ffn_glu_fwd(n2, w_gate, w_up, conv_w, conv_b, host=plan.host("ffn_glu_fwd"))
    w_down = plan.weight("w_down")
    dy, dy16, sq = _down_proj_loss(act, w_down, h1, tgt_p, n_valid)

    grads = {}
    plan.grad("w_down", _matmul("down_dw", act, dy16, "tn", BF16))
    dgpre, dup, grads["ffn_conv_w"], grads["ffn_conv_b"] = _ffn_glu_bwd(gpre, up, dy16, w_down, conv_w, conv_b)
    plan.grad("w_gate_t", _matmul("gate_dw", dgpre, n2, "tn", BF16))
    plan.grad("w_up_t", _matmul("up_dw", dup, n2, "tn", BF16))
    dh1, dh1_16, grads["ffn_norm_w"] = _ffn_in_bwd(dgpre, dup, w_gate, w_up, h1, dy, wt["ffn_norm_w"],
                                                   host=plan.host("ffn_in_bwd"))
    plan.grad("w_out", _matmul("out_dw", mixed, dh1_16, "tn", BF16))
    do_mla, do_dn, dz, grads["mla_out_norm_w"], grads["dn_out_norm_w"] = _mix_out_bwd(
        o_mla, o_dn, z, dh1_16, w_out, wt["mla_out_norm_w"], wt["dn_out_norm_w"], host=plan.host("mix_out_bwd"))
    dq_eff, ds_out = _dn_o_bwd(sall, q_eff, do_dn)
    gall = _dn_rec_bwd(n_mat, eg, ds_out)
    dqn, dkn, dv_dn, dg, dbeta = _dn_chunk_bwd(qn, kn, conv, g, beta, sall, gall, dq_eff, do_dn,
                                               host=plan.host("dn_chunk_bwd"))
    dconv, dab, dalog, ddtb = _dn_prep_bwd(conv, ab, dqn, dkn, dv_dn, dg, dbeta, *dn_consts)
    grads["dn_A_log"] = jnp.sum(dalog.reshape(HEADS, HEAD), axis=1)[None, :]
    grads["dn_dt_bias"] = jnp.sum(ddtb.reshape(HEADS, HEAD), axis=1)[None, :]
    ddn_pre, grads["dn_conv_w"] = _dn_conv_bwd(proj, wt["dn_conv_w"], dconv)
    dq, dk, dv = _attn_bwd(q, k, v, do_mla, host=plan.host("attn_bwd"))
    dq_lat, dkv_lat, dk_pe, dqa, dwq, dkva, dwkv, dqnw, dknw = _mla_prep_bwd(
        q_lat, kv_lat, k_pe, cos, sin, dq, dk, dv, *mla_consts, host=plan.host("mla_prep_bwd"))
    grads["q_a_norm_w"], grads["kv_a_norm_w"] = dqa, dkva
    plan.grad("w_q_t", dwq)
    plan.grad("w_kv_t", dwkv)
    grads["q_norm_w"], grads["k_norm_w"] = dqnw[:, :QK_DIM], dknw[:, :QK_DIM]
    dproj = [ddn_pre, dz, dq_lat, dkv_lat, dk_pe, dab]
    plan.grad("w_in_t", _in_proj_bwd_w(dproj, u))
    du = _in_proj_bwd_x(dproj, win, host=plan.host("in_dx"))
    dh0, _, grads["attn_norm_w"] = _rms_bwd("attn_norm_bwd", h0, wt["attn_norm_w"], [du], dh1,
                                            host=plan.host("attn_norm_bwd"))
    grads["meta_tokens"] = dh0[0:N_META]
    if isinstance(plan, _LocalPlan):
        grads.update(plan.grads)
    return sq, dh0[N_META:n_valid], grads


def _mesh_pos():
    return lax.axis_index("x"), lax.axis_index("y"), lax.axis_index("c")


def _other_chips(x, y):
    return [(1 - x, y), (x, 1 - y), (1 - x, 1 - y)]


def _remote(src, dst, send_sems, recv_sems, k, to):
    return pltpu.make_async_remote_copy(src_ref=src, dst_ref=dst, send_sem=send_sems.at[k], recv_sem=recv_sems.at[k],
                                        device_id=to, device_id_type=MESH)


SIBLING_ID, CHIPS_ID, GATHER_ID, ALL_ID = 1, 2, 3, 4


def _sibling_peer():
    x, y, c = _mesh_pos()
    return [(x, y, 1 - c)]


def _chip_peers():
    x, y, c = _mesh_pos()
    return [(qx, qy, c) for qx, qy in _other_chips(x, y)]


def _copies_exchange(make, ins, out_shape, nsem, peers=None, cid=None):
    def prog(in_refs, out_refs, send_sems, recv_sems):
        copies = make(in_refs, out_refs, send_sems, recv_sems)

        def start():
            for cp in copies:
                cp.start()

        def finish():
            for cp in copies:
                cp.wait()

        return start, finish

    return _Exchange(prog, ins, out_shape, nsem, peers, cid)


def _all_gather(shards):
    def prog(srcs, dsts, send_sems, recv_sems):
        x, y, c = _mesh_pos()
        p = 2 * x + y
        sibling = (x, y, 1 - c)
        chips = _other_chips(x, y)
        bufs = tuple((s, d, s.shape[0] // 2) for s, d in zip(srcs, dsts))

        def half(ref, rows, which):
            return ref.at[pl.ds(which * rows, rows), :]

        def copy(i, k, src, dst, to):
            return _remote(src, dst, send_sems, recv_sems, 6 * i + k, to)

        sends = [copy(i, j, half(src, rows, c), half(dst.at[p], rows, c), (*chip, c))
                 for i, (src, dst, rows) in enumerate(bufs) for j, chip in enumerate(chips)]

        def start():
            for cp in sends:
                cp.start()

        def finish():
            passed = []
            for i, (src, dst, rows) in enumerate(bufs):
                for j, (qx, qy) in enumerate(chips):
                    block = half(dst.at[2 * qx + qy], rows, c)
                    copy(i, j, block, block, (x, y, c)).wait_recv()
                    fwd = copy(i, 3 + j, block, block, sibling)
                    fwd.start()
                    passed.append(fwd)
            for i, (src, dst, rows) in enumerate(bufs):
                for j, (qx, qy) in enumerate(chips):
                    block = half(dst.at[2 * qx + qy], rows, 1 - c)
                    copy(i, 3 + j, block, block, (x, y, c)).wait_recv()
            for cp in sends + passed:
                cp.wait_send()

        return start, finish

    return _Exchange(prog, shards, [jax.ShapeDtypeStruct((N_CHIPS, *s.shape), s.dtype) for s in shards], 6 * len(shards),
                     lambda: _sibling_peer() + _chip_peers(), GATHER_ID)


def _all_gather_small(block):
    def make(srcs, dsts, send_sems, recv_sems):
        x, y, c = _mesh_pos()
        return [_remote(srcs[0], dsts[0].at[2 * x + y], send_sems, recv_sems, k, (qx, qy, c))
                for k, (qx, qy) in enumerate(_other_chips(x, y))]

    return _copies_exchange(make, [block], [jax.ShapeDtypeStruct((N_CHIPS, *block.shape), block.dtype)], 3, _chip_peers,
                            CHIPS_ID)


def _gathered(ex):
    p = 2 * lax.axis_index("x") + lax.axis_index("y")
    return [lax.dynamic_update_slice(g, s[None], (p, 0, 0)) for g, s in zip(ex.outs, ex.ins)]


def _rs_to_sibling(bufs):
    def make(srcs, dsts, send_sems, recv_sems):
        x, y, c = _mesh_pos()
        copies = []
        for i, (src, dst) in enumerate(zip(srcs, dsts)):
            half = src.shape[1] // 2
            copies.append(_remote(src.at[:, pl.ds((1 - c) * half, half), :], dst, send_sems, recv_sems, i, (x, y, 1 - c)))
        return copies

    return _copies_exchange(make, bufs,
                            [jax.ShapeDtypeStruct((N_CHIPS, b.shape[1] // 2, b.shape[2]), b.dtype) for b in bufs],
                            len(bufs), _sibling_peer, SIBLING_ID)


def _rs_pair_add(name, bufs, gots, c, out_dtype):
    n = len(bufs)

    def body(c_ref, *refs):
        for a_ref, b_ref, o_ref in zip(refs[:n], refs[n:2 * n], refs[2 * n:]):
            o_ref[...] = (a_ref[...].astype(F32) + b_ref[...].astype(F32)).astype(out_dtype)

    mine = [pl.BlockSpec((None, g.shape[1], g.shape[2]), lambda j, cr: (j, cr[0], 0)) for g in gots]
    whole = [pl.BlockSpec((None, g.shape[1], g.shape[2]), lambda j, cr: (j, 0, 0)) for g in gots]
    return pl.pallas_call(
        body, name=name,
        grid_spec=pltpu.PrefetchScalarGridSpec(num_scalar_prefetch=1, grid=(N_CHIPS,), in_specs=mine + whole, out_specs=whole),
        out_shape=[jax.ShapeDtypeStruct(g.shape, out_dtype) for g in gots],
        compiler_params=_cparams(("arbitrary",)))(c, *bufs, *gots)


def _rs_to_chips(accs):
    def make(srcs, dsts, send_sems, recv_sems):
        x, y, c = _mesh_pos()
        return [_remote(src.at[2 * qx + qy], dst.at[k], send_sems, recv_sems, 3 * i + k, (qx, qy, c))
                for i, (src, dst) in enumerate(zip(srcs, dsts)) for k, (qx, qy) in enumerate(_other_chips(x, y))]

    return _copies_exchange(make, accs, [jax.ShapeDtypeStruct((3, a.shape[1], a.shape[2]), a.dtype) for a in accs],
                            3 * len(accs), _chip_peers, CHIPS_ID)


def _rs_chip_add(name, accs, gots, p):
    n = len(accs)
    slot = (0, 1, 0, 2)

    def body(p_ref, *refs):
        me = p_ref[0]
        for own_ref, got_ref, o_ref in zip(refs[:n], refs[n:2 * n], refs[2 * n:]):
            total = None
            for chip in range(N_CHIPS):
                val = own_ref[...].astype(F32)
                for e in (1, 2, 3):
                    val = jnp.where((chip ^ me) == e, got_ref[slot[e]].astype(F32), val)
                total = val if total is None else total + val
            o_ref[...] = total

    own = [pl.BlockSpec((None, a.shape[1], a.shape[2]), lambda i, pr: (pr[0], 0, 0)) for a in accs]
    got = [pl.BlockSpec(g.shape, lambda i, pr: (0, 0, 0)) for g in gots]
    out = [pl.BlockSpec((a.shape[1], a.shape[2]), lambda i, pr: (0, 0)) for a in accs]
    return pl.pallas_call(
        body, name=name,
        grid_spec=pltpu.PrefetchScalarGridSpec(num_scalar_prefetch=1, grid=(1,), in_specs=own + got, out_specs=out),
        out_shape=[jax.ShapeDtypeStruct((a.shape[1], a.shape[2]), F32) for a in accs],
        compiler_params=_cparams(("arbitrary",)))(p, *accs, *gots)


def _rs_share(ress):
    def make(srcs, dsts, send_sems, recv_sems):
        x, y, c = _mesh_pos()
        return [_remote(src, dst, send_sems, recv_sems, i, (x, y, 1 - c)) for i, (src, dst) in enumerate(zip(srcs, dsts))]

    return _copies_exchange(make, ress, [jax.ShapeDtypeStruct(r.shape, F32) for r in ress], len(ress), _sibling_peer,
                            SIBLING_ID)


def _shared(ex):
    south = lax.axis_index("c") == 0
    return [jnp.concatenate([jnp.where(south, r, g), jnp.where(south, g, r)], axis=0) for r, g in zip(ex.ins, ex.outs)]


def _all_to_all_devices(vec):
    def others():
        x, y, c = _mesh_pos()
        return [((1 - x if r & 4 else x), (1 - y if r & 2 else y), (1 - c if r & 1 else c)) for r in range(1, 8)]

    def make(srcs, dsts, send_sems, recv_sems):
        x, y, c = _mesh_pos()
        me = 4 * x + 2 * y + c
        return [_remote(srcs[0], dsts[0].at[me], send_sems, recv_sems, r, peer) for r, peer in enumerate(others())]

    return _copies_exchange(make, [vec], [jax.ShapeDtypeStruct((8, *vec.shape), vec.dtype)], 7, others, ALL_ID)


def _sum_devices(stack):
    def body(s_ref, o_ref):
        total = s_ref[0]
        for d in range(1, 8):
            total = total + s_ref[d]
        o_ref[...] = total

    return pl.pallas_call(body, name="sum_devices", out_shape=jax.ShapeDtypeStruct(stack.shape[1:], F32),
                          compiler_params=pltpu.CompilerParams(vmem_limit_bytes=VMEM_LIMIT))(stack)


def _pad_rows(flat, rows):
    return jnp.concatenate([flat, jnp.zeros((rows * LANES - flat.shape[0],), flat.dtype)]).reshape(rows, LANES)


def _unshard(g4, shape, axis):
    a = g4.reshape(N_CHIPS, *shape)
    if axis == 0:
        return a.reshape(N_CHIPS * shape[0], shape[1])
    return jnp.transpose(a, (1, 0, 2)).reshape(shape[0], N_CHIPS * shape[1])


def _pad_axis0(a, rows):
    return jnp.concatenate([a, jnp.zeros((rows - a.shape[0], *a.shape[1:]), a.dtype)], axis=0)


def _shard_to_strip(name, w):
    _, (shape, axis, rows) = name, {n: (s, ax, r) for n, s, ax, r in BIG}[name]
    w2 = w.reshape(shape).astype(BF16)
    return _pad_axis0(w2.T if axis == 1 else w2, rows)


LOCAL_NAME = dict(w_in="w_in_t", w_q_b="w_q_t", w_kv_b="w_kv_t", w_out="w_out", w_gate="w_gate_t", w_up="w_up_t",
                  w_down="w_down")


WIN_SEGMENTS = ((576, 2112, 0), (2112, 2624, 1536), (0, 256, 2048), (256, 512, 2304), (512, 576, 2560), (2624, 2632, 2688))


def _strips_to_weight(name, g4):
    if name == "w_in":
        return _win_to_pad(g4[:, :IN_SHARD].reshape(IN_COLS, D_MODEL))
    if name == "w_q_b":
        return _qk_to_pad(g4.reshape(HEADS * QK_DIM, LORA))
    return g4.reshape(N_CHIPS * g4.shape[1], g4.shape[2])


def _grad_to_strips(name, g):
    if name == "w_in":
        strips = []
        for q in range(N_CHIPS):
            pieces = []
            for a, b, local in sorted(WIN_SEGMENTS):
                s, e = max(a, q * IN_SHARD), min(b, (q + 1) * IN_SHARD)
                if s < e:
                    pieces.append(g[local + s - a:local + e - a])
            pieces.append(jnp.zeros((IN_SHARD_P - IN_SHARD, D_MODEL), g.dtype))
            strips.append(jnp.concatenate(pieces, axis=0))
        return jnp.stack(strips)
    if name == "w_q_b":
        return _qk_from_pad(g).reshape(N_CHIPS, QK_DIM, LORA)
    return g.reshape(N_CHIPS, g.shape[0] // N_CHIPS, g.shape[1])


class _MeshPlan:
    LATE = dict(attn_norm_fwd=("w_in", "w_q_b", "w_kv_b"), attn_fwd=("w_up",), dn_chunk_fwd=("w_out", "w_gate"),
                ffn_glu_fwd=("w_down",))
    GROUP_A = ("w_down", "w_gate", "w_up", "w_out")
    GROUP_B = ("w_in", "w_q_b", "w_kv_b")

    def __init__(self, w):
        x, y, c = _mesh_pos()
        self.ci = jnp.reshape(c, (1,)).astype(jnp.int32)
        self.pi = jnp.reshape(2 * x + y, (1,)).astype(jnp.int32)
        self.strip = {n: _shard_to_strip(n, w[n]) for n, _, _, _ in BIG}
        self.gathers, self.weights, self.g, self.acc, self.reduced = {}, {}, {}, {}, {}
        self.sibs, self.sib, self.chip, self.share = [], None, None, None

    def gather_small(self, small):
        ex = _all_gather_small(small)
        ex.run("all_gather_small")
        return _gathered(ex)[0]

    def weight(self, local_name):
        if local_name not in self.weights:
            for point, (names, ex) in list(self.gathers.items()):
                if ex.outs is not None:
                    for n, g4 in zip(names, _gathered(ex)):
                        self.weights[LOCAL_NAME[n]] = _strips_to_weight(n, g4)
                    del self.gathers[point]
        return self.weights[local_name]

    def grad(self, local_name, value):
        name = {v: k for k, v in LOCAL_NAME.items()}[local_name]
        self.g[name] = _grad_to_strips(name, value)

    def _pair_add(self, names, gots):
        accs = _rs_pair_add("rs_pair_add_" + names[0], [self.g[n] for n in names], gots, self.ci, BF16)
        self.acc.update(zip(names, accs))

    def _chip_add(self, names, chip):
        return _rs_chip_add("rs_chip_add_" + names[0], [self.acc[n] for n in names], chip.outs, self.pi)

    def _take_shared(self, names, share):
        for n, strip in zip(names, _shared(share)):
            self.reduced[n] = strip

    def host(self, point):
        a, b = self.GROUP_A, self.GROUP_B
        if point in self.LATE:
            names = self.LATE[point]
            ex = _all_gather([self.strip[n] for n in names])
            self.gathers[point] = (names, ex)
            return ex
        if point in ("ffn_in_bwd", "mix_out_bwd"):
            names = dict(ffn_in_bwd=a[:3], mix_out_bwd=a[3:])[point]
            ex = _rs_to_sibling([self.g[n] for n in names])
            self.sibs.append(ex)
            return ex
        if point == "dn_chunk_bwd":
            self._pair_add(a, [o for ex in self.sibs for o in ex.outs])
            self.chip1 = _rs_to_chips([self.acc[n] for n in a[:2]])
            return self.chip1
        if point == "attn_bwd":
            self.chip2 = _rs_to_chips([self.acc[n] for n in a[2:]])
            return self.chip2
        if point == "mla_prep_bwd":
            ress = self._chip_add(a[:2], self.chip1) + self._chip_add(a[2:], self.chip2)
            self.share = _rs_share(ress)
            return self.share
        if point == "in_dx":
            self._take_shared(a, self.share)
            self.sib = _rs_to_sibling([self.g[n] for n in b])
            return self.sib
        if point == "attn_norm_bwd":
            self._pair_add(b, self.sib.outs)
            self.chip = _rs_to_chips([self.acc[n] for n in b])
            return self.chip
        return None

    def last_share(self):
        self.share = _rs_share(self._chip_add(self.GROUP_B, self.chip))
        return self.share

    def finish(self):
        self._take_shared(self.GROUP_B, self.share)
        return self.reduced


def kernel(x, meta_tokens, attn_norm_w, w_in, q_a_norm_w, w_q_b, kv_a_norm_w, w_kv_b, q_norm_w, k_norm_w, mla_out_norm_w, dn_conv_w, dn_A_log, dn_dt_bias, dn_out_norm_w, w_out, ffn_norm_w, w_gate, w_up, ffn_conv_w, ffn_conv_b, w_down, loss_target, m_meta_tokens, m_attn_norm_w, m_w_in, m_q_a_norm_w, m_w_q_b, m_kv_a_norm_w, m_w_kv_b, m_q_norm_w, m_k_norm_w, m_mla_out_norm_w, m_dn_conv_w, m_dn_A_log, m_dn_dt_bias, m_dn_out_norm_w, m_w_out, m_ffn_norm_w, m_w_gate, m_w_up, m_ffn_conv_w, m_ffn_conv_b, m_w_down, v_meta_tokens, v_attn_norm_w, v_w_in, v_q_a_norm_w, v_w_q_b, v_kv_a_norm_w, v_w_kv_b, v_q_norm_w, v_k_norm_w, v_mla_out_norm_w, v_dn_conv_w, v_dn_A_log, v_dn_dt_bias, v_dn_out_norm_w, v_w_out, v_ffn_norm_w, v_w_gate, v_w_up, v_ffn_conv_w, v_ffn_conv_b, v_w_down):
    local = dict(locals())
    w = {n: local[n] for n in WEIGHTS}
    m = {n: local["m_" + n] for n in WEIGHTS}
    v = {n: local["v_" + n] for n in WEIGHTS}
    p = 2 * lax.axis_index("x") + lax.axis_index("y")

    plan = _MeshPlan(w)
    wf = _pad_rows(jnp.concatenate([w[n].reshape(-1) for n, _, _ in SMALL_SHARDED]), SMALL_ROWS)
    gf = plan.gather_small(wf).reshape(N_CHIPS, -1)
    full = {}
    off = 0
    for n, s, ax in SMALL_SHARDED:
        full[n] = _unshard(gf[:, off:off + s[0] * s[1]], s, ax)
        off += s[0] * s[1]
    for n, _ in REPLICATED:
        full[n] = w[n]
    full["ffn_conv_w"] = _ff_to_pad(full["ffn_conv_w"], 1)
    full["ffn_conv_b"] = _ff_to_pad(full["ffn_conv_b"], 1)

    sq, grad_x, g = _local_step(x[0], loss_target[0], full, plan)
    g["ffn_conv_w"] = _ff_from_pad(g["ffn_conv_w"], 1)
    g["ffn_conv_b"] = _ff_from_pad(g["ffn_conv_b"], 1)

    small_all = [n for n, _, _ in SMALL_SHARDED] + [n for n, _ in REPLICATED]
    vec = jnp.concatenate([g[n].reshape(-1) for n in small_all] + [jnp.reshape(0.5 / D_MODEL * jnp.sum(sq), (1,))])
    vec = _pad_rows(vec, -(-vec.shape[0] // (8 * LANES)) * 8)
    a2a = _all_to_all_devices(vec)

    gs, delta, new_m, new_v = {}, {}, {}, {}
    big = {n: (s, ax) for n, s, ax, _ in BIG}

    def adamw_big(names, strips, host=None):
        n = names[0]
        s, ax = big[n]
        if ax == 1 and s[1] % 8:
            there = lambda a: jnp.transpose(a, (2, 0, 1))
            back = lambda a: jnp.transpose(a, (1, 2, 0))
            g3 = strips[n][:s[1]].reshape(s[1], 1, s[0])
            d2, m2, v2 = _adamw_rows3d("adamw_" + n, there(w[n]), g3, there(m[n]), there(v[n]))
            gs[n], delta[n], new_m[n], new_v[n] = back(g3), back(d2), back(m2), back(v2)
            return

        def there(n, a):
            s, ax = big[n]
            return a.reshape(s).T if ax == 1 else a.reshape(s)

        def back(n, a):
            return (a.T if big[n][1] == 1 else a).reshape(w[n].shape)

        outs = _adamw_call("adamw_" + n, [there(n, w[n]) for n in names], [strips[n] for n in names],
                           [there(n, m[n]) for n in names], [there(n, v[n]) for n in names], host=host)
        for n, (g2, d2, m2, v2) in zip(names, outs):
            gs[n], delta[n], new_m[n], new_v[n] = back(n, g2), back(n, d2), back(n, m2), back(n, v2)

    adamw_big(("w_down", "w_gate"), plan.reduced, host=a2a)
    adamw_big(("w_out",), plan.reduced, host=plan.last_share())
    adamw_big(("w_up",), plan.reduced)
    strips = plan.finish()
    for n in plan.GROUP_B:
        adamw_big((n,), strips)
    me = 4 * lax.axis_index("x") + 2 * lax.axis_index("y") + lax.axis_index("c")
    red = _sum_devices(lax.dynamic_update_slice(a2a.outs[0], vec[None], (me, 0, 0))).reshape(-1)
    off = 0
    for n in small_all:
        tot = red[off:off + g[n].size].reshape(g[n].shape)
        off += g[n].size
        shard = {sn: (s, ax) for sn, s, ax in SMALL_SHARDED}.get(n)
        if shard is not None:
            tot = lax.dynamic_slice_in_dim(tot, p * shard[0][1], shard[0][1], axis=1)
        gs[n] = tot
    loss = red[off]
    two_d = lambda a: a.reshape(a.shape[-2], a.shape[-1])
    outs = _adamw_small([two_d(w[n]) for n in small_all], [two_d(gs[n]) for n in small_all],
                        [two_d(m[n]) for n in small_all], [two_d(v[n]) for n in small_all])
    for i, n in enumerate(small_all):
        for dst, src in ((delta, outs[0]), (new_m, outs[1]), (new_v, outs[2])):
            dst[n] = src[i].reshape(w[n].shape)

    grad_out = [gs[n].reshape(w[n].shape) for n in WEIGHTS]
    return (loss, grad_x[None], *grad_out, *[delta[n] for n in WEIGHTS], *[new_m[n] for n in WEIGHTS],
            *[new_v[n] for n in WEIGHTS])
```
